```python
import jax, jax.numpy as jnp
from jax import lax
import numpy as np

D_MODEL = 1024
BATCH = 16
SEQ = 2048
DEPTH = 1

CHUNK = 128
A_GROUPS = 4
A_WIDTH = 512
A_GROUP_DIM = A_WIDTH // A_GROUPS
N_HEADS = 8
N_KV_HEADS = 2
HEAD_DIM = 64
Q_DIM = N_HEADS * HEAD_DIM
KV_DIM = N_KV_HEADS * HEAD_DIM
WINDOW = 128
BLOCK = 128
N_BUCKETS = 32
MAX_DISTANCE = 128
D_FF = 2816
CONV_WIDTH = 3
EPS = 1e-6
NEG_INF = -1e30
IN_SIZES = (A_WIDTH, A_WIDTH, Q_DIM, KV_DIM, KV_DIM, D_MODEL, D_MODEL)
IN_DIM = sum(IN_SIZES)

kernel_name = "hybrid_gated_gmlp_swa_convffn"


def rmsnorm(x, g):
    xf = x.astype(jnp.float32)
    r = lax.rsqrt(jnp.mean(xf * xf, axis=-1, keepdims=True) + EPS)
    return (xf * r * g.astype(jnp.float32)).astype(x.dtype)


def band_buckets():
    i = np.arange(BLOCK)[:, None]
    j = np.arange(2 * BLOCK)[None, :]
    dist = i + BLOCK - j
    valid = (dist >= 0) & (dist < WINDOW)
    d = np.clip(dist, 0, None)
    max_exact = N_BUCKETS // 2
    large = max_exact + (np.log(np.maximum(d, 1) / max_exact) / np.log(MAX_DISTANCE / max_exact)
                         * (N_BUCKETS - max_exact)).astype(np.int32)
    large = np.minimum(large, N_BUCKETS - 1)
    buckets = np.where(d < max_exact, d, large).astype(np.int32)
    return buckets, valid


def spatial_gating(u, v, g_sgu, w_s, b_s):
    B, S = v.shape[0], v.shape[1]
    nc = S // CHUNK
    v = rmsnorm(v, g_sgu).reshape(B, nc, CHUNK, A_GROUPS, A_GROUP_DIM)
    causal = jnp.tril(jnp.ones((CHUNK, CHUNK), dtype=w_s.dtype))
    w_masked = w_s * causal[None]
    s = jnp.einsum('gts,bcsgd->bctgd', w_masked, v) + jnp.transpose(b_s)[None, None, :, :, None]
    return u * s.reshape(B, S, A_WIDTH)


def swa_sink_attention(q, k, v, sinks, rel_bias):
    B, S = q.shape[0], q.shape[1]
    nb = S // BLOCK
    G = N_HEADS // N_KV_HEADS
    qb = q.reshape(B, nb, BLOCK, N_KV_HEADS, G, HEAD_DIM)
    kb = k.reshape(B, nb, BLOCK, N_KV_HEADS, HEAD_DIM)
    vb = v.reshape(B, nb, BLOCK, N_KV_HEADS, HEAD_DIM)
    pad = ((0, 0), (1, 0), (0, 0), (0, 0), (0, 0))
    kw = jnp.concatenate([jnp.pad(kb, pad)[:, :-1], kb], axis=2)
    vw = jnp.concatenate([jnp.pad(vb, pad)[:, :-1], vb], axis=2)
    scale = HEAD_DIM ** -0.5
    scores = jnp.einsum('bnqhgd,bnkhd->bnhgqk', qb, kw).astype(jnp.float32) * scale
    buckets, valid = band_buckets()
    bias = rel_bias[buckets].astype(jnp.float32)
    bias = jnp.transpose(bias, (2, 0, 1)).reshape(N_KV_HEADS, G, BLOCK, 2 * BLOCK)
    first_ok = (np.arange(nb)[:, None] > 0) | (np.arange(2 * BLOCK)[None, :] >= BLOCK)
    mask = valid[None, :, :] & first_ok[:, None, :]
    scores = jnp.where(mask[None, :, None, None], scores + bias[None, None], NEG_INF)
    sink = sinks.astype(jnp.float32).reshape(N_KV_HEADS, G)[None, None, :, :, None, None]
    m = jnp.maximum(jnp.max(scores, axis=-1, keepdims=True), sink)
    p = jnp.exp(scores - m)
    probs = p / (jnp.sum(p, axis=-1, keepdims=True) + jnp.exp(sink - m))
    out = jnp.einsum('bnhgqk,bnkhd->bnqhgd', probs.astype(vw.dtype), vw)
    return out.reshape(B, S, Q_DIM)


def causal_depthwise_conv(x, w, b):
    S = x.shape[1]
    xp = jnp.pad(x, ((0, 0), (CONV_WIDTH - 1, 0), (0, 0)))
    y = w[0] * xp[:, 0:S]
    for j in range(1, CONV_WIDTH):
        y = y + w[j] * xp[:, j:j + S]
    return y + b


def _fwd_setup_inputs(seed: int = 0) -> dict:
    key = jax.random.key(seed)
    ks = jax.random.split(key, 20)
    f32 = jnp.float32
    nrm = lambda k, shape, s: (jax.random.normal(k, shape, f32) * s)
    L = DEPTH
    return {
        "x": nrm(ks[0], (BATCH, SEQ, D_MODEL), 1.0),
        "g_mix": 1.0 + nrm(ks[1], (L, D_MODEL), 0.05),
        "w_in": nrm(ks[2], (L, D_MODEL, IN_DIM), D_MODEL ** -0.5),
        "g_sgu": 1.0 + nrm(ks[3], (L, A_WIDTH), 0.05),
        "w_s": nrm(ks[4], (L, A_GROUPS, CHUNK, CHUNK), CHUNK ** -0.5),
        "b_s": 1.0 + nrm(ks[5], (L, A_GROUPS, CHUNK), 0.1),
        "sinks": nrm(ks[6], (L, N_HEADS), 0.5),
        "rel_bias": nrm(ks[7], (N_BUCKETS, N_HEADS), 0.5),
        "w_pa": nrm(ks[8], (L, A_WIDTH, D_MODEL), A_WIDTH ** -0.5),
        "w_pb": nrm(ks[9], (L, Q_DIM, D_MODEL), Q_DIM ** -0.5),
        "w_out": nrm(ks[10], (L, D_MODEL, D_MODEL), D_MODEL ** -0.5),
        "g_ffn": 1.0 + nrm(ks[11], (L, D_MODEL), 0.05),
        "w_up": nrm(ks[12], (L, D_MODEL, 2 * D_FF), D_MODEL ** -0.5),
        "w_conv": nrm(ks[13], (L, CONV_WIDTH, 2 * D_FF), CONV_WIDTH ** -0.5),
        "b_conv": nrm(ks[14], (L, 2 * D_FF), 0.01),
        "w_down": nrm(ks[15], (L, D_FF, D_MODEL), D_FF ** -0.5),
        "g_final": 1.0 + nrm(ks[16], (D_MODEL,), 0.05),
    }


def _fwd_reference(x, g_mix, w_in, g_sgu, w_s, b_s, sinks, rel_bias, w_pa, w_pb, w_out,
              g_ffn, w_up, w_conv, b_conv, w_down, g_final):
    splits = [int(c) for c in np.cumsum(IN_SIZES)[:-1]]
    B, S = x.shape[0], x.shape[1]
    for l in range(DEPTH):
        h = rmsnorm(x, g_mix[l])
        proj = jnp.einsum('bsd,de->bse', h, w_in[l])
        pu, pv, q, k, v, gate_a, gate_b = jnp.split(proj, splits, axis=-1)
        y_a = spatial_gating(jax.nn.gelu(pu), jax.nn.gelu(pv), g_sgu[l], w_s[l], b_s[l])
        y_b = swa_sink_attention(q.reshape(B, S, N_HEADS, HEAD_DIM),
                                 k.reshape(B, S, N_KV_HEADS, HEAD_DIM),
                                 v.reshape(B, S, N_KV_HEADS, HEAD_DIM),
                                 sinks[l], rel_bias)
        merged = (jax.nn.sigmoid(gate_a) * jnp.einsum('bse,ed->bsd', y_a, w_pa[l])
                  + jax.nn.sigmoid(gate_b) * jnp.einsum('bse,ed->bsd', y_b, w_pb[l]))
        x = x + jnp.einsum('bsd,de->bse', merged, w_out[l])
        h2 = rmsnorm(x, g_ffn[l])
        up = causal_depthwise_conv(jnp.einsum('bsd,df->bsf', h2, w_up[l]), w_conv[l], b_conv[l])
        gate, val = jnp.split(up, 2, axis=-1)
        x = x + jnp.einsum('bsf,fd->bsd', jax.nn.silu(gate) * val, w_down[l])
    return rmsnorm(x, g_final)


import jax as _jax
import jax.numpy as _jnp

TWIN_FORMAT = 'train_step'
FWD_PARAMS = ['x', 'g_mix', 'w_in', 'g_sgu', 'w_s', 'b_s', 'sinks', 'rel_bias', 'w_pa', 'w_pb', 'w_out', 'g_ffn', 'w_up', 'w_conv', 'b_conv', 'w_down', 'g_final']
TWIN_WEIGHTS = ['g_mix', 'w_in', 'g_sgu', 'w_s', 'b_s', 'sinks', 'rel_bias', 'w_pa', 'w_pb', 'w_out', 'g_ffn', 'w_up', 'w_conv', 'b_conv', 'w_down', 'g_final']
TWIN_DIFF_INPUT = 'x'
TWIN_INPUTS = ['x', 'g_mix', 'w_in', 'g_sgu', 'w_s', 'b_s', 'sinks', 'rel_bias', 'w_pa', 'w_pb', 'w_out', 'g_ffn', 'w_up', 'w_conv', 'b_conv', 'w_down', 'g_final', 'loss_target', 'm_g_mix', 'm_w_in', 'm_g_sgu', 'm_w_s', 'm_b_s', 'm_sinks', 'm_rel_bias', 'm_w_pa', 'm_w_pb', 'm_w_out', 'm_g_ffn', 'm_w_up', 'm_w_conv', 'm_b_conv', 'm_w_down', 'm_g_final', 'v_g_mix', 'v_w_in', 'v_g_sgu', 'v_w_s', 'v_b_s', 'v_sinks', 'v_rel_bias', 'v_w_pa', 'v_w_pb', 'v_w_out', 'v_g_ffn', 'v_w_up', 'v_w_conv', 'v_b_conv', 'v_w_down', 'v_g_final']
TWIN_OUTPUTS = ['loss', 'grad_x', 'grad_g_mix', 'grad_w_in', 'grad_g_sgu', 'grad_w_s', 'grad_b_s', 'grad_sinks', 'grad_rel_bias', 'grad_w_pa', 'grad_w_pb', 'grad_w_out', 'grad_g_ffn', 'grad_w_up', 'grad_w_conv', 'grad_b_conv', 'grad_w_down', 'grad_g_final', 'delta_g_mix', 'delta_w_in', 'delta_g_sgu', 'delta_w_s', 'delta_b_s', 'delta_sinks', 'delta_rel_bias', 'delta_w_pa', 'delta_w_pb', 'delta_w_out', 'delta_g_ffn', 'delta_w_up', 'delta_w_conv', 'delta_b_conv', 'delta_w_down', 'delta_g_final', 'new_m_g_mix', 'new_m_w_in', 'new_m_g_sgu', 'new_m_w_s', 'new_m_b_s', 'new_m_sinks', 'new_m_rel_bias', 'new_m_w_pa', 'new_m_w_pb', 'new_m_w_out', 'new_m_g_ffn', 'new_m_w_up', 'new_m_w_conv', 'new_m_b_conv', 'new_m_w_down', 'new_m_g_final', 'new_v_g_mix', 'new_v_w_in', 'new_v_g_sgu', 'new_v_w_s', 'new_v_b_s', 'new_v_sinks', 'new_v_rel_bias', 'new_v_w_pa', 'new_v_w_pb', 'new_v_w_out', 'new_v_g_ffn', 'new_v_w_up', 'new_v_w_conv', 'new_v_b_conv', 'new_v_w_down', 'new_v_g_final']
TWIN_LEAF_KINDS = {'loss': 'loss', 'grad_x': 'grad_x', 'grad_g_mix': 'grad_w', 'grad_w_in': 'grad_w', 'grad_g_sgu': 'grad_w', 'grad_w_s': 'grad_w', 'grad_b_s': 'grad_w', 'grad_sinks': 'grad_w', 'grad_rel_bias': 'grad_w', 'grad_w_pa': 'grad_w', 'grad_w_pb': 'grad_w', 'grad_w_out': 'grad_w', 'grad_g_ffn': 'grad_w', 'grad_w_up': 'grad_w', 'grad_w_conv': 'grad_w', 'grad_b_conv': 'grad_w', 'grad_w_down': 'grad_w', 'grad_g_final': 'grad_w', 'delta_g_mix': 'delta_w', 'delta_w_in': 'delta_w', 'delta_g_sgu': 'delta_w', 'delta_w_s': 'delta_w', 'delta_b_s': 'delta_w', 'delta_sinks': 'delta_w', 'delta_rel_bias': 'delta_w', 'delta_w_pa': 'delta_w', 'delta_w_pb': 'delta_w', 'delta_w_out': 'delta_w', 'delta_g_ffn': 'delta_w', 'delta_w_up': 'delta_w', 'delta_w_conv': 'delta_w', 'delta_b_conv': 'delta_w', 'delta_w_down': 'delta_w', 'delta_g_final': 'delta_w', 'new_m_g_mix': 'new_m', 'new_m_w_in': 'new_m', 'new_m_g_sgu': 'new_m', 'new_m_w_s': 'new_m', 'new_m_b_s': 'new_m', 'new_m_sinks': 'new_m', 'new_m_rel_bias': 'new_m', 'new_m_w_pa': 'new_m', 'new_m_w_pb': 'new_m', 'new_m_w_out': 'new_m', 'new_m_g_ffn': 'new_m', 'new_m_w_up': 'new_m', 'new_m_w_conv': 'new_m', 'new_m_b_conv': 'new_m', 'new_m_w_down': 'new_m', 'new_m_g_final': 'new_m', 'new_v_g_mix': 'new_v', 'new_v_w_in': 'new_v', 'new_v_g_sgu': 'new_v', 'new_v_w_s': 'new_v', 'new_v_b_s': 'new_v', 'new_v_sinks': 'new_v', 'new_v_rel_bias': 'new_v', 'new_v_w_pa': 'new_v', 'new_v_w_pb': 'new_v', 'new_v_w_out': 'new_v', 'new_v_g_ffn': 'new_v', 'new_v_w_up': 'new_v', 'new_v_w_conv': 'new_v', 'new_v_b_conv': 'new_v', 'new_v_w_down': 'new_v', 'new_v_g_final': 'new_v'}


def _forward(args):
    return _fwd_reference(*[args[k] for k in FWD_PARAMS])


def _output_shape():
    out = _jax.eval_shape(lambda: _forward(_fwd_setup_inputs(0)))
    return out.shape, out.dtype

N_MICROBATCH = 1
ADAM_LR = 0.001
ADAM_B1 = 0.9
ADAM_B2 = 0.999
ADAM_EPS = 1e-08
ADAM_WD = 0.01
ADAM_STEP = 10
PER_EXAMPLE_BATCH_AXIS = {'x': 0, 'loss_target': 0}
SHARED_INPUTS = []
_WEIGHT_DTYPES = {'g_mix': _jnp.float32, 'w_in': _jnp.float32, 'g_sgu': _jnp.float32, 'w_s': _jnp.float32, 'b_s': _jnp.float32, 'sinks': _jnp.float32, 'rel_bias': _jnp.float32, 'w_pa': _jnp.float32, 'w_pb': _jnp.float32, 'w_out': _jnp.float32, 'g_ffn': _jnp.float32, 'w_up': _jnp.float32, 'w_conv': _jnp.float32, 'b_conv': _jnp.float32, 'w_down': _jnp.float32, 'g_final': _jnp.float32}
MOMENT_SCALE = {'g_mix': 1.060340e-01, 'w_in': 5.493394e-02, 'g_sgu': 6.673066e-02, 'w_s': 6.428748e-02, 'b_s': 9.264563e-02, 'sinks': 2.293666e-02, 'rel_bias': 3.260388e-02, 'w_pa': 8.562884e-02, 'w_pb': 2.201525e-02, 'w_out': 8.473550e-02, 'g_ffn': 1.307892e-01, 'w_up': 5.478225e-02, 'w_conv': 5.523876e-02, 'b_conv': 5.523169e-02, 'w_down': 9.077671e-02, 'g_final': 3.208889e+01}


def _to_microbatches(a, axis):
    t = _jnp.moveaxis(a, axis, 0)
    t = t.reshape((N_MICROBATCH, t.shape[0] // N_MICROBATCH) + t.shape[1:])
    return _jnp.moveaxis(t, 1, axis + 1)


def setup_inputs(seed: int = 0) -> dict:
    inp = _fwd_setup_inputs(seed)
    key = _jax.random.fold_in(_jax.random.key(seed), 7919)
    shape, _ = _output_shape()
    out = dict(inp)
    out["loss_target"] = _jax.random.normal(_jax.random.fold_in(key, 0), shape, _jnp.float32)
    for i, name in enumerate(TWIN_WEIGHTS):
        w = inp[name].astype(_jnp.float32)
        if MOMENT_SCALE is None:
            s = _jnp.sqrt(_jnp.mean(_jnp.square(w)) + 1e-30)
        else:
            s = MOMENT_SCALE[name]
        km, kv = _jax.random.split(_jax.random.fold_in(key, i + 1))
        out[name] = w
        out["m_" + name] = s * _jax.random.normal(km, w.shape, _jnp.float32)
        out["v_" + name] = (s * s) * _jax.random.uniform(kv, w.shape, _jnp.float32, 0.5, 1.5)
    if N_MICROBATCH > 1:
        for name, axis in PER_EXAMPLE_BATCH_AXIS.items():
            out[name] = _to_microbatches(out[name], axis)
    return {'x': out['x'], 'g_mix': out['g_mix'], 'w_in': out['w_in'], 'g_sgu': out['g_sgu'], 'w_s': out['w_s'], 'b_s': out['b_s'], 'sinks': out['sinks'], 'rel_bias': out['rel_bias'], 'w_pa': out['w_pa'], 'w_pb': out['w_pb'], 'w_out': out['w_out'], 'g_ffn': out['g_ffn'], 'w_up': out['w_up'], 'w_conv': out['w_conv'], 'b_conv': out['b_conv'], 'w_down': out['w_down'], 'g_final': out['g_final'], 'loss_target': out['loss_target'], 'm_g_mix': out['m_g_mix'], 'm_w_in': out['m_w_in'], 'm_g_sgu': out['m_g_sgu'], 'm_w_s': out['m_w_s'], 'm_b_s': out['m_b_s'], 'm_sinks': out['m_sinks'], 'm_rel_bias': out['m_rel_bias'], 'm_w_pa': out['m_w_pa'], 'm_w_pb': out['m_w_pb'], 'm_w_out': out['m_w_out'], 'm_g_ffn': out['m_g_ffn'], 'm_w_up': out['m_w_up'], 'm_w_conv': out['m_w_conv'], 'm_b_conv': out['m_b_conv'], 'm_w_down': out['m_w_down'], 'm_g_final': out['m_g_final'], 'v_g_mix': out['v_g_mix'], 'v_w_in': out['v_w_in'], 'v_g_sgu': out['v_g_sgu'], 'v_w_s': out['v_w_s'], 'v_b_s': out['v_b_s'], 'v_sinks': out['v_sinks'], 'v_rel_bias': out['v_rel_bias'], 'v_w_pa': out['v_w_pa'], 'v_w_pb': out['v_w_pb'], 'v_w_out': out['v_w_out'], 'v_g_ffn': out['v_g_ffn'], 'v_w_up': out['v_w_up'], 'v_w_conv': out['v_w_conv'], 'v_b_conv': out['v_b_conv'], 'v_w_down': out['v_w_down'], 'v_g_final': out['v_g_final']}


def _loss(weights, diff, rest, loss_target):
    with _jax.named_scope("forward"):
        args = {**rest, TWIN_DIFF_INPUT: diff, **{k: w.astype(_WEIGHT_DTYPES[k]) for k, w in weights.items()}}
        y = _forward(args)
    with _jax.named_scope("loss_head"):
        err = _jnp.square(y.astype(_jnp.float32) - loss_target)
        return 0.5 * _jnp.sum(_jnp.mean(err, axis=-1)) if err.ndim else 0.5 * err


def _adamw(w, g, m, v):
    m = ADAM_B1 * m + (1.0 - ADAM_B1) * g
    v = ADAM_B2 * v + (1.0 - ADAM_B2) * _jnp.square(g)
    m_hat = m / (1.0 - ADAM_B1 ** ADAM_STEP)
    v_hat = v / (1.0 - ADAM_B2 ** ADAM_STEP)
    delta = -ADAM_LR * (m_hat / (_jnp.sqrt(v_hat) + ADAM_EPS) + ADAM_WD * w)
    return delta, m, v


def reference(x, g_mix, w_in, g_sgu, w_s, b_s, sinks, rel_bias, w_pa, w_pb, w_out, g_ffn, w_up, w_conv, b_conv, w_down, g_final, loss_target, m_g_mix, m_w_in, m_g_sgu, m_w_s, m_b_s, m_sinks, m_rel_bias, m_w_pa, m_w_pb, m_w_out, m_g_ffn, m_w_up, m_w_conv, m_b_conv, m_w_down, m_g_final, v_g_mix, v_w_in, v_g_sgu, v_w_s, v_b_s, v_sinks, v_rel_bias, v_w_pa, v_w_pb, v_w_out, v_g_ffn, v_w_up, v_w_conv, v_b_conv, v_w_down, v_g_final):
    given = dict(x=x, g_mix=g_mix, w_in=w_in, g_sgu=g_sgu, w_s=w_s, b_s=b_s, sinks=sinks, rel_bias=rel_bias, w_pa=w_pa, w_pb=w_pb, w_out=w_out, g_ffn=g_ffn, w_up=w_up, w_conv=w_conv, b_conv=b_conv, w_down=w_down, g_final=g_final, loss_target=loss_target, m_g_mix=m_g_mix, m_w_in=m_w_in, m_g_sgu=m_g_sgu, m_w_s=m_w_s, m_b_s=m_b_s, m_sinks=m_sinks, m_rel_bias=m_rel_bias, m_w_pa=m_w_pa, m_w_pb=m_w_pb, m_w_out=m_w_out, m_g_ffn=m_g_ffn, m_w_up=m_w_up, m_w_conv=m_w_conv, m_b_conv=m_b_conv, m_w_down=m_w_down, m_g_final=m_g_final, v_g_mix=v_g_mix, v_w_in=v_w_in, v_g_sgu=v_g_sgu, v_w_s=v_w_s, v_b_s=v_b_s, v_sinks=v_sinks, v_rel_bias=v_rel_bias, v_w_pa=v_w_pa, v_w_pb=v_w_pb, v_w_out=v_w_out, v_g_ffn=v_g_ffn, v_w_up=v_w_up, v_w_conv=v_w_conv, v_b_conv=v_b_conv, v_w_down=v_w_down, v_g_final=v_g_final)
    weights = {n: given[n] for n in TWIN_WEIGHTS}
    shared = {n: given[n] for n in SHARED_INPUTS}
    per_example = {n: given[n] for n in ['x']}
    grad_fn = _jax.value_and_grad(_loss, argnums=(0, 1))

    def one_microbatch(ex, loss_target):
        ex = dict(ex)
        diff = ex.pop(TWIN_DIFF_INPUT)
        return grad_fn(weights, diff, {**shared, **ex}, loss_target)

    if N_MICROBATCH == 1:
        loss, (grad_w, grad_x) = one_microbatch(per_example, given["loss_target"])
    else:
        def body(carry, xs):
            loss_sum, grad_sum = carry
            l_k, (gw_k, gx_k) = one_microbatch(xs[0], xs[1])
            with _jax.named_scope("update"):
                return (loss_sum + l_k, _jax.tree.map(_jnp.add, grad_sum, gw_k)), gx_k

        init = (_jnp.zeros((), _jnp.float32), _jax.tree.map(_jnp.zeros_like, weights))
        (loss, grad_w), grad_x = _jax.lax.scan(body, init, (per_example, given["loss_target"]))
    with _jax.named_scope("update"):
        delta_w, new_m, new_v = {}, {}, {}
        for n in TWIN_WEIGHTS:
            delta_w[n], new_m[n], new_v[n] = _adamw(weights[n], grad_w[n], given["m_" + n], given["v_" + n])
    return (loss, grad_x, *[grad_w[n] for n in TWIN_WEIGHTS], *[delta_w[n] for n in TWIN_WEIGHTS],
            *[new_m[n] for n in TWIN_WEIGHTS], *[new_v[n] for n in TWIN_WEIGHTS])
```

```python
import functools

import numpy as np
import jax
import jax.numpy as jnp
from jax import lax
from jax.experimental import pallas as pl
from jax.experimental.pallas import tpu as pltpu

F32 = jnp.float32
BF16 = jnp.bfloat16

D_MODEL = 1024
CHUNK = 128
A_GROUPS = 4
A_WIDTH = 512
N_HEADS = 8
HEAD_DIM = 64
Q_DIM = 512
KV_DIM = 128
N_BUCKETS = 32
MAX_DISTANCE = 128
D_FF = 2816
EPS = 1e-6
NEG_INF = -1e30
G_DIM = 2 * D_MODEL
A_DIM = 2 * A_WIDTH
B_DIM = Q_DIM + 2 * KV_DIM
LANES = 128
BF16_ROWS = 16
N_CHIPS = 4
N_DEV = 8

ADAM_LR = 0.001
ADAM_B1 = 0.9
ADAM_B2 = 0.999
ADAM_EPS = 1e-08
ADAM_WD = 0.01
ADAM_STEP = 10

MESH = pl.DeviceIdType.MESH
_GELU_C = 0.7978845608028654
_GELU_A = 0.044715


def _cp(sem=None, vmem_mb=None):
    kw = {}
    if sem is not None:
        kw["dimension_semantics"] = sem
    if vmem_mb is not None:
        kw["vmem_limit_bytes"] = vmem_mb << 20
    return pltpu.CompilerParams(**kw)


def _dot(a, b):
    return jnp.dot(a, b, preferred_element_type=F32)


def _dot_nt(a, b):
    return lax.dot_general(a, b, (((1,), (1,)), ((), ())), preferred_element_type=F32)


def _dot_tn(a, b):
    return lax.dot_general(a, b, (((0,), (0,)), ((), ())), preferred_element_type=F32)


def _rms_r(x):
    return lax.rsqrt(jnp.mean(x * x, axis=-1, keepdims=True) + EPS)


def _rms_bwd(dh, n, r, g):
    dn = dh * g
    return r * (dn - n * jnp.mean(dn * n, axis=-1, keepdims=True))


def _gelu(x):
    t = jnp.tanh(_GELU_C * (x + _GELU_A * (x * x * x)))
    return 0.5 * x * (1.0 + t), t


def _gelu_grad(x, t):
    return 0.5 * (1.0 + t) + 0.5 * x * (1.0 - t * t) * (_GELU_C * (1.0 + 3.0 * _GELU_A * x * x))


def _sigmoid(x):
    return 1.0 / (1.0 + jnp.exp(-x))


def _row(tm, w):
    return pl.BlockSpec((tm, w), lambda i: (i, 0))


def _full(shape):
    nd = len(shape)
    return pl.BlockSpec(tuple(shape), lambda *_: (0,) * nd)


def _sds(shape, dtype):
    return jax.ShapeDtypeStruct(tuple(shape), dtype)


def _band_buckets():
    i = np.arange(CHUNK)[:, None]
    j = np.arange(2 * CHUNK)[None, :]
    dist = i + CHUNK - j
    valid = (dist >= 0) & (dist < CHUNK)
    d = np.clip(dist, 0, None)
    max_exact = N_BUCKETS // 2
    large = max_exact + (np.log(np.maximum(d, 1) / max_exact) / np.log(MAX_DISTANCE / max_exact)
                         * (N_BUCKETS - max_exact)).astype(np.int32)
    large = np.minimum(large, N_BUCKETS - 1)
    buckets = np.where(d < max_exact, d, large).astype(np.int32)
    return np.where(valid, buckets, -1).astype(np.int32)


def _inproj(x2, g_mix, w_g, w_a, w_b, tm):
    T = x2.shape[0]

    def body(x_ref, g_ref, wg_ref, wa_ref, wb_ref, pg_ref, pa_ref, pb_ref, h_ref):
        x = x_ref[...]
        h = (x * _rms_r(x) * g_ref[...]).astype(BF16)
        h_ref[...] = h
        pg_ref[...] = _dot(h, wg_ref[...]).astype(BF16)
        pa_ref[...] = _dot(h, wa_ref[...]).astype(BF16)
        pb_ref[...] = _dot(h, wb_ref[...]).astype(BF16)

    return pl.pallas_call(
        body, name="inproj", grid=(T // tm,),
        in_specs=[_row(tm, D_MODEL), _full(g_mix.shape), _full(w_g.shape), _full(w_a.shape), _full(w_b.shape)],
        out_specs=[_row(tm, G_DIM), _row(tm, A_DIM), _row(tm, B_DIM), _row(tm, D_MODEL)],
        out_shape=[_sds((T, G_DIM), BF16), _sds((T, A_DIM), BF16), _sds((T, B_DIM), BF16), _sds((T, D_MODEL), BF16)],
        compiler_params=_cp(("arbitrary",), 48),
    )(x2, g_mix, w_g, w_a, w_b)


def _sgu_parts(p, g):
    pu = p[:, :A_WIDTH]
    pv = p[:, A_WIDTH:]
    u, tu = _gelu(pu)
    vv, tv = _gelu(pv)
    rv = _rms_r(vv)
    vn = (vv * rv * g).astype(BF16)
    return pu, pv, u, tu, vv, tv, rv, vn


def _tril():
    r = lax.broadcasted_iota(jnp.int32, (CHUNK, CHUNK), 0)
    c = lax.broadcasted_iota(jnp.int32, (CHUNK, CHUNK), 1)
    return r >= c


def _sgu_fwd(proj_a, g_sgu, w_s, b_st, tm):
    T = proj_a.shape[0]

    def body(p_ref, g_ref, ws_ref, bs_ref, y_ref):
        tril = _tril()
        _, _, u, _, _, _, _, vn = _sgu_parts(p_ref[...].astype(F32), g_ref[...])
        for gi in range(A_GROUPS):
            wm = jnp.where(tril, ws_ref[gi], 0.0).astype(BF16)
            bcol = bs_ref[:, gi:gi + 1]
            cs = slice(gi * CHUNK, (gi + 1) * CHUNK)
            for c in range(tm // CHUNK):
                rs = slice(c * CHUNK, (c + 1) * CHUNK)
                s = _dot(wm, vn[rs, cs]) + bcol
                y_ref[rs, cs] = (u[rs, cs] * s).astype(BF16)

    return pl.pallas_call(
        body, name="sgu_fwd", grid=(T // tm,),
        in_specs=[_row(tm, A_DIM), _full(g_sgu.shape), _full(w_s.shape), _full(b_st.shape)],
        out_specs=_row(tm, A_WIDTH), out_shape=_sds((T, A_WIDTH), BF16),
        compiler_params=_cp(("arbitrary",)),
    )(proj_a, g_sgu, w_s, b_st)


def _kv_variants(a):
    a = a.astype(F32)
    lane = lax.broadcasted_iota(jnp.int32, a.shape, 1)
    lo = jnp.where(lane < HEAD_DIM, a, 0.0)
    hi = jnp.where(lane >= HEAD_DIM, a, 0.0)
    lo_r = pltpu.roll(lo, HEAD_DIM, 1)
    hi_r = pltpu.roll(hi, HEAD_DIM, 1)
    return ((lo.astype(BF16), lo_r.astype(BF16)), (hi_r.astype(BF16), hi.astype(BF16)))


def _build_bias(bias_scr, bk_ref, rel_ref):
    bk = bk_ref[...]
    for h in range(N_HEADS):
        acc = jnp.zeros((CHUNK, 2 * CHUNK), F32)
        for b in range(N_BUCKETS):
            acc = jnp.where(bk == b, rel_ref[b, h], acc)
        bias_scr[h] = acc


def _attn_probs(qp, k_h, bias_h, sink_h, ok):
    s = _dot_nt(qp, k_h) * (HEAD_DIM ** -0.5) + bias_h
    s = jnp.where(ok, s, NEG_INF)
    m = jnp.maximum(jnp.max(s, axis=-1, keepdims=True), sink_h)
    p = jnp.exp(s - m)
    es = jnp.exp(sink_h - m)
    den = jnp.sum(p, axis=-1, keepdims=True) + es
    return p / den, es / den


def _attn_block_inputs(qkv_ref, n):
    r0 = pl.multiple_of(n * CHUNK, CHUNK)
    rp = pl.multiple_of(jnp.maximum(n - 1, 0) * CHUNK, CHUNK)
    kw = jnp.concatenate([qkv_ref[pl.ds(rp, CHUNK), Q_DIM:Q_DIM + KV_DIM],
                          qkv_ref[pl.ds(r0, CHUNK), Q_DIM:Q_DIM + KV_DIM]], axis=0)
    vw = jnp.concatenate([qkv_ref[pl.ds(rp, CHUNK), Q_DIM + KV_DIM:B_DIM],
                          qkv_ref[pl.ds(r0, CHUNK), Q_DIM + KV_DIM:B_DIM]], axis=0)
    return r0, _kv_variants(kw), _kv_variants(vw)


def _attn_fwd(proj_b, sinks, rel_bias, n_seq, seq):
    nb = seq // CHUNK
    bk = jnp.asarray(_band_buckets())

    def body(qkv_ref, bk_ref, rel_ref, sink_ref, o_ref, bias_scr):
        _build_bias(bias_scr, bk_ref, rel_ref)
        col = lax.broadcasted_iota(jnp.int32, (CHUNK, 2 * CHUNK), 1)
        valid = bk_ref[...] >= 0

        def blk(n, carry):
            r0, kv, vv = _attn_block_inputs(qkv_ref, n)
            ok = valid & ((col >= CHUNK) | (n > 0))
            for pr in range(N_HEADS // 2):
                qp = qkv_ref[pl.ds(r0, CHUNK), pr * LANES:(pr + 1) * LANES]
                kvh = pr // 2
                acc = jnp.zeros((CHUNK, LANES), F32)
                for hh in range(2):
                    h = 2 * pr + hh
                    prob, _ = _attn_probs(qp, kv[kvh][hh], bias_scr[h], sink_ref[0, h], ok)
                    acc = acc + _dot(prob.astype(BF16), vv[kvh][hh])
                o_ref[pl.ds(r0, CHUNK), pr * LANES:(pr + 1) * LANES] = acc.astype(BF16)
            return carry

        lax.fori_loop(0, nb, blk, 0)

    smem = pl.BlockSpec(memory_space=pltpu.SMEM)
    return pl.pallas_call(
        body, name="attn_fwd", grid=(n_seq,),
        in_specs=[_row(seq, B_DIM), _full(bk.shape), smem, smem],
        out_specs=_row(seq, Q_DIM), out_shape=_sds((n_seq * seq, Q_DIM), BF16),
        scratch_shapes=[pltpu.VMEM((N_HEADS, CHUNK, 2 * CHUNK), F32)],
        compiler_params=_cp(("arbitrary",)),
    )(proj_b, bk, rel_bias, sinks)


def _merge_fwd(x2, y_a, y_b, proj_g, w_pa, w_pb, w_out, tm):
    T = x2.shape[0]

    def body(x_ref, ya_ref, yb_ref, g_ref, wpa_ref, wpb_ref, wo_ref, x1_ref, mg_ref):
        g = g_ref[...].astype(F32)
        pa = _dot(ya_ref[...], wpa_ref[...])
        pb = _dot(yb_ref[...], wpb_ref[...])
        merged = (_sigmoid(g[:, :D_MODEL]) * pa + _sigmoid(g[:, D_MODEL:]) * pb).astype(BF16)
        mg_ref[...] = merged
        x1_ref[...] = x_ref[...] + _dot(merged, wo_ref[...])

    return pl.pallas_call(
        body, name="merge_fwd", grid=(T // tm,),
        in_specs=[_row(tm, D_MODEL), _row(tm, A_WIDTH), _row(tm, Q_DIM), _row(tm, G_DIM),
                  _full(w_pa.shape), _full(w_pb.shape), _full(w_out.shape)],
        out_specs=[_row(tm, D_MODEL), _row(tm, D_MODEL)],
        out_shape=[_sds((T, D_MODEL), F32), _sds((T, D_MODEL), BF16)],
        compiler_params=_cp(("arbitrary",), 40),
    )(x2, y_a, y_b, proj_g, w_pa, w_pb, w_out)


def _upproj(x1, g_ffn, w_up, tm):
    T = x1.shape[0]

    def body(x_ref, g_ref, w_ref, u_ref, h_ref):
        x = x_ref[...]
        h = (x * _rms_r(x) * g_ref[...]).astype(BF16)
        h_ref[...] = h
        u_ref[...] = _dot(h, w_ref[...]).astype(BF16)

    return pl.pallas_call(
        body, name="upproj", grid=(T // tm,),
        in_specs=[_row(tm, D_MODEL), _full(g_ffn.shape), _full(w_up.shape)],
        out_specs=[_row(tm, 2 * D_FF), _row(tm, D_MODEL)],
        out_shape=[_sds((T, 2 * D_FF), BF16), _sds((T, D_MODEL), BF16)],
        compiler_params=_cp(("arbitrary",), 56),
    )(x1, g_ffn, w_up)


def _shift_down(u, halo, k):
    rolled = pltpu.roll(u, k, 0)
    row = lax.broadcasted_iota(jnp.int32, u.shape, 0)
    if k == 1:
        return jnp.where(row == 0, halo[1:2], rolled)
    return jnp.where(row == 0, halo[0:1], jnp.where(row == 1, halo[1:2], rolled))


def _shift_up(d, halo, k):
    tm = d.shape[0]
    rolled = pltpu.roll(d, tm - k, 0)
    row = lax.broadcasted_iota(jnp.int32, d.shape, 0)
    if k == 1:
        return jnp.where(row == tm - 1, halo[0:1], rolled)
    return jnp.where(row == tm - 2, halo[0:1], jnp.where(row == tm - 1, halo[1:2], rolled))


def _conv_taps(u_ref, halo_ref, cols, at_start):
    u = u_ref[:, cols].astype(F32)
    hl = halo_ref[:, cols].astype(F32)[BF16_ROWS - 2:BF16_ROWS]
    hl = jnp.where(at_start, 0.0, hl)
    return u, _shift_down(u, hl, 1), _shift_down(u, hl, 2)


def _conv_out(taps, wc, bc):
    u, u1, u2 = taps
    return wc[0:1] * u2 + wc[1:2] * u1 + wc[2:3] * u + bc


def _prev_halo_spec(tm, width, col_block=None):
    k = tm // BF16_ROWS
    if col_block is None:
        return pl.BlockSpec((BF16_ROWS, width), lambda i: (jnp.maximum(i * k - 1, 0), 0))
    return pl.BlockSpec((BF16_ROWS, width), lambda j, i: (jnp.maximum(i * k - 1, 0), col_block(j)))


def _ffn_down_loss(upre, x1, target, w_conv, b_conv, w_down, g_final, tm, seq):
    T = x1.shape[0]
    tiles_per_seq = seq // tm
    half = D_FF // 2

    def body(u_ref, hl_ref, x1_ref, t_ref, wc_ref, bc_ref, wd_ref, g_ref, dx2_ref, loss_ref, gg_ref):
        i = pl.program_id(0)
        at_start = (i % tiles_per_seq) == 0
        acc = jnp.zeros((tm, D_MODEL), F32)
        for j in range(2):
            gc = slice(j * half, (j + 1) * half)
            vc = slice(D_FF + j * half, D_FF + (j + 1) * half)
            gate = _conv_out(_conv_taps(u_ref, hl_ref, gc, at_start), wc_ref[:, gc], bc_ref[:, gc])
            val = _conv_out(_conv_taps(u_ref, hl_ref, vc, at_start), wc_ref[:, vc], bc_ref[:, vc])
            act = (gate * _sigmoid(gate) * val).astype(BF16)
            acc = acc + _dot(act, wd_ref[gc, :])
        x2 = x1_ref[...] + acc
        r = _rms_r(x2)
        n = x2 * r
        g = g_ref[...]
        diff = n * g - t_ref[...]
        dy = diff * (1.0 / D_MODEL)
        dx2_ref[...] = _rms_bwd(dy, n, r, g)

        @pl.when(i == 0)
        def _():
            loss_ref[...] = jnp.zeros_like(loss_ref)
            gg_ref[...] = jnp.zeros_like(gg_ref)

        loss_ref[...] += 0.5 * jnp.sum(jnp.mean(diff * diff, axis=-1, keepdims=True), axis=0, keepdims=True)
        gg_ref[...] += jnp.sum(dy * n, axis=0, keepdims=True)

    return pl.pallas_call(
        body, name="ffn_down_loss", grid=(T // tm,),
        in_specs=[_row(tm, 2 * D_FF), _prev_halo_spec(tm, 2 * D_FF), _row(tm, D_MODEL), _row(tm, D_MODEL),
                  _full(w_conv.shape), _full(b_conv.shape), _full(w_down.shape), _full(g_final.shape)],
        out_specs=[_row(tm, D_MODEL), _full((1, 1)), _full((1, D_MODEL))],
        out_shape=[_sds((T, D_MODEL), F32), _sds((1, 1), F32), _sds((1, D_MODEL), F32)],
        compiler_params=_cp(("arbitrary",), 56),
    )(upre, upre, x1, target, w_conv, b_conv, w_down, g_final)


def _ffn_bwd_act(upre, dx2, w_conv, b_conv, w_down, tm, seq):
    T = dx2.shape[0]
    tiles_per_seq = seq // tm
    half = D_FF // 2
    nt = T // tm

    def body(ug_ref, uv_ref, hg_ref, hv_ref, dx_ref, wcg_ref, wcv_ref, bcg_ref, bcv_ref, wd_ref,
             dg_ref, dv_ref, gwd_ref, gbg_ref, gbv_ref, gwg_ref, gwv_ref):
        i = pl.program_id(1)
        at_start = (i % tiles_per_seq) == 0
        allc = slice(0, half)
        tg = _conv_taps(ug_ref, hg_ref, allc, at_start)
        tv = _conv_taps(uv_ref, hv_ref, allc, at_start)
        gate = _conv_out(tg, wcg_ref[...], bcg_ref[...])
        val = _conv_out(tv, wcv_ref[...], bcv_ref[...])
        sg = _sigmoid(gate)
        silu = gate * sg
        dx = dx_ref[...].astype(BF16)
        d_act = _dot_nt(dx, wd_ref[...])
        d_val = d_act * silu
        d_gate = d_act * val * (sg * (1.0 + gate * (1.0 - sg)))
        dg_ref[...] = d_gate.astype(BF16)
        dv_ref[...] = d_val.astype(BF16)

        @pl.when(i == 0)
        def _():
            for r in (gwd_ref, gbg_ref, gbv_ref, gwg_ref, gwv_ref):
                r[...] = jnp.zeros_like(r)

        gwd_ref[...] += _dot_tn((silu * val).astype(BF16), dx)
        gbg_ref[...] += jnp.sum(d_gate, axis=0, keepdims=True)
        gbv_ref[...] += jnp.sum(d_val, axis=0, keepdims=True)
        for k in range(3):
            gwg_ref[k:k + 1, :] += jnp.sum(d_gate * tg[2 - k], axis=0, keepdims=True)
            gwv_ref[k:k + 1, :] += jnp.sum(d_val * tv[2 - k], axis=0, keepdims=True)

    gcol = lambda j: j
    vcol = lambda j: 2 + j
    tile = lambda cb: pl.BlockSpec((tm, half), lambda j, i: (i, cb(j)))
    vec = lambda rows, cb: pl.BlockSpec((rows, half), lambda j, i: (0, cb(j)))
    return pl.pallas_call(
        body, name="ffn_bwd_act", grid=(2, nt),
        in_specs=[tile(gcol), tile(vcol), _prev_halo_spec(tm, half, gcol), _prev_halo_spec(tm, half, vcol),
                  pl.BlockSpec((tm, D_MODEL), lambda j, i: (i, 0)),
                  vec(3, gcol), vec(3, vcol), vec(1, gcol), vec(1, vcol),
                  pl.BlockSpec((half, D_MODEL), lambda j, i: (j, 0))],
        out_specs=[tile(gcol), tile(gcol), pl.BlockSpec((half, D_MODEL), lambda j, i: (j, 0)),
                   vec(1, gcol), vec(1, gcol), vec(3, gcol), vec(3, gcol)],
        out_shape=[_sds((T, D_FF), BF16), _sds((T, D_FF), BF16), _sds((D_FF, D_MODEL), F32),
                   _sds((1, D_FF), F32), _sds((1, D_FF), F32), _sds((3, D_FF), F32), _sds((3, D_FF), F32)],
        compiler_params=_cp(("arbitrary", "arbitrary"), 56),
    )(upre, upre, upre, upre, dx2, w_conv, w_conv, b_conv, b_conv, w_down)


def _ffn_bwd_up(d_gate, d_val, dx2, x1, g_ffn, w_conv, w_up, tm, seq):
    T = dx2.shape[0]
    tiles_per_seq = seq // tm
    k16 = tm // BF16_ROWS
    n16 = T // BF16_ROWS
    cw = D_FF // 2

    def body(dg_ref, dv_ref, hg_ref, hv_ref, dx2_ref, x1_ref, g_ref, wc_ref, wu_ref, du_ref, dx1_ref, gg_ref):
        i = pl.program_id(0)
        at_end = (i % tiles_per_seq) == tiles_per_seq - 1
        dh = jnp.zeros((tm, D_MODEL), F32)
        for j in range(4):
            src, hsrc = (dg_ref, hg_ref) if j < 2 else (dv_ref, hv_ref)
            ls = slice((j % 2) * cw, (j % 2 + 1) * cw)
            cs = slice(j * cw, (j + 1) * cw)
            d = src[:, ls].astype(F32)
            hl = hsrc[:, ls].astype(F32)[0:2]
            hl = jnp.where(at_end, 0.0, hl)
            wc = wc_ref[:, cs]
            du = (wc[2:3] * d + wc[1:2] * _shift_up(d, hl, 1) + wc[0:1] * _shift_up(d, hl, 2)).astype(BF16)
            du_ref[:, cs] = du
            dh = dh + _dot_nt(du, wu_ref[:, cs])
        x = x1_ref[...]
        r = _rms_r(x)
        n = x * r
        dx1_ref[...] = dx2_ref[...] + _rms_bwd(dh, n, r, g_ref[...])

        @pl.when(i == 0)
        def _():
            gg_ref[...] = jnp.zeros_like(gg_ref)

        gg_ref[...] += jnp.sum(dh * n, axis=0, keepdims=True)

    nxt = pl.BlockSpec((BF16_ROWS, D_FF), lambda i: (jnp.minimum((i + 1) * k16, n16 - 1), 0))
    return pl.pallas_call(
        body, name="ffn_bwd_up", grid=(T // tm,),
        in_specs=[_row(tm, D_FF), _row(tm, D_FF), nxt, nxt, _row(tm, D_MODEL), _row(tm, D_MODEL),
                  _full(g_ffn.shape), _full(w_conv.shape), _full(w_up.shape)],
        out_specs=[_row(tm, 2 * D_FF), _row(tm, D_MODEL), _full((1, D_MODEL))],
        out_shape=[_sds((T, 2 * D_FF), BF16), _sds((T, D_MODEL), F32), _sds((1, D_MODEL), F32)],
        compiler_params=_cp(("arbitrary",), 60),
    )(d_gate, d_val, d_gate, d_val, dx2, x1, g_ffn, w_conv, w_up)


def _matmul_tn(a, b, tn, tk, name):
    T, M = a.shape
    N = b.shape[1]

    def body(a_ref, b_ref, o_ref):
        @pl.when(pl.program_id(1) == 0)
        def _():
            o_ref[...] = jnp.zeros_like(o_ref)

        o_ref[...] += _dot_tn(a_ref[...], b_ref[...])

    return pl.pallas_call(
        body, name=name, grid=(N // tn, T // tk),
        in_specs=[pl.BlockSpec((tk, M), lambda j, k: (k, 0)), pl.BlockSpec((tk, tn), lambda j, k: (k, j))],
        out_specs=pl.BlockSpec((M, tn), lambda j, k: (0, j)), out_shape=_sds((M, N), F32),
        compiler_params=_cp(("arbitrary", "arbitrary"), 48),
    )(a, b)


def _merge_bwd(dx1, merged, y_a, y_b, proj_g, w_pa, w_pb, w_out, tm):
    T = dx1.shape[0]

    def body(dx_ref, mg_ref, ya_ref, yb_ref, g_ref, wpa_ref, wpb_ref, wo_ref,
             dg_ref, dya_ref, dyb_ref, gwo_ref, gwpa_ref, gwpb_ref):
        dx = dx_ref[...].astype(BF16)
        dm = _dot_nt(dx, wo_ref[...])
        g = g_ref[...].astype(F32)
        ya = ya_ref[...]
        yb = yb_ref[...]
        pa = _dot(ya, wpa_ref[...])
        pb = _dot(yb, wpb_ref[...])
        sa = _sigmoid(g[:, :D_MODEL])
        sb = _sigmoid(g[:, D_MODEL:])
        dpa = (dm * sa).astype(BF16)
        dpb = (dm * sb).astype(BF16)
        dg_ref[:, :D_MODEL] = (dm * pa * (sa * (1.0 - sa))).astype(BF16)
        dg_ref[:, D_MODEL:] = (dm * pb * (sb * (1.0 - sb))).astype(BF16)
        dya_ref[...] = _dot_nt(dpa, wpa_ref[...]).astype(BF16)
        dyb_ref[...] = _dot_nt(dpb, wpb_ref[...]).astype(BF16)

        @pl.when(pl.program_id(0) == 0)
        def _():
            for r in (gwo_ref, gwpa_ref, gwpb_ref):
                r[...] = jnp.zeros_like(r)

        gwo_ref[...] += _dot_tn(mg_ref[...], dx)
        gwpa_ref[...] += _dot_tn(ya, dpa)
        gwpb_ref[...] += _dot_tn(yb, dpb)

    return pl.pallas_call(
        body, name="merge_bwd", grid=(T // tm,),
        in_specs=[_row(tm, D_MODEL), _row(tm, D_MODEL), _row(tm, A_WIDTH), _row(tm, Q_DIM), _row(tm, G_DIM),
                  _full(w_pa.shape), _full(w_pb.shape), _full(w_out.shape)],
        out_specs=[_row(tm, G_DIM), _row(tm, A_WIDTH), _row(tm, Q_DIM),
                   _full(w_out.shape), _full(w_pa.shape), _full(w_pb.shape)],
        out_shape=[_sds((T, G_DIM), BF16), _sds((T, A_WIDTH), BF16), _sds((T, Q_DIM), BF16),
                   _sds(w_out.shape, F32), _sds(w_pa.shape, F32), _sds(w_pb.shape, F32)],
        compiler_params=_cp(("arbitrary",), 56),
    )(dx1, merged, y_a, y_b, proj_g, w_pa, w_pb, w_out)


def _sgu_bwd(proj_a, d_ya, g_sgu, w_s, b_st, tm):
    T = proj_a.shape[0]

    def body(p_ref, dy_ref, g_ref, ws_ref, bs_ref, dp_ref, gws_ref, gbs_ref, gg_ref):
        tril = _tril()
        g = g_ref[...]
        pu, pv, u, tu, vv, tv, rv, vn = _sgu_parts(p_ref[...].astype(F32), g)
        dy = dy_ref[...].astype(F32)

        @pl.when(pl.program_id(0) == 0)
        def _():
            for r in (gws_ref, gbs_ref, gg_ref):
                r[...] = jnp.zeros_like(r)

        du_cols = []
        dvn_cols = []
        for gi in range(A_GROUPS):
            wm = jnp.where(tril, ws_ref[gi], 0.0).astype(BF16)
            wmt = wm.astype(F32).T.astype(BF16)
            bcol = bs_ref[:, gi:gi + 1]
            cs = slice(gi * CHUNK, (gi + 1) * CHUNK)
            du_rows = []
            dvn_rows = []
            gw = jnp.zeros((CHUNK, CHUNK), F32)
            gb = jnp.zeros((CHUNK, 1), F32)
            for c in range(tm // CHUNK):
                rs = slice(c * CHUNK, (c + 1) * CHUNK)
                vn_c = vn[rs, cs]
                s = _dot(wm, vn_c) + bcol
                dy_c = dy[rs, cs]
                ds = dy_c * u[rs, cs]
                du_rows.append(dy_c * s)
                dsb = ds.astype(BF16)
                gw = gw + _dot_nt(dsb, vn_c)
                gb = gb + jnp.sum(ds, axis=-1, keepdims=True)
                dvn_rows.append(_dot(wmt, dsb))
            gws_ref[gi] += jnp.where(tril, gw, 0.0)
            gbs_ref[:, gi:gi + 1] += gb
            du_cols.append(jnp.concatenate(du_rows, axis=0))
            dvn_cols.append(jnp.concatenate(dvn_rows, axis=0))
        du = jnp.concatenate(du_cols, axis=1)
        dvn = jnp.concatenate(dvn_cols, axis=1)
        vhat = vv * rv
        gg_ref[...] += jnp.sum(dvn * vhat, axis=0, keepdims=True)
        dvv = _rms_bwd(dvn, vhat, rv, g)
        dp_ref[:, :A_WIDTH] = (du * _gelu_grad(pu, tu)).astype(BF16)
        dp_ref[:, A_WIDTH:] = (dvv * _gelu_grad(pv, tv)).astype(BF16)

    return pl.pallas_call(
        body, name="sgu_bwd", grid=(T // tm,),
        in_specs=[_row(tm, A_DIM), _row(tm, A_WIDTH), _full(g_sgu.shape), _full(w_s.shape), _full(b_st.shape)],
        out_specs=[_row(tm, A_DIM), _full(w_s.shape), _full(b_st.shape), _full(g_sgu.shape)],
        out_shape=[_sds((T, A_DIM), BF16), _sds(w_s.shape, F32), _sds(b_st.shape, F32), _sds(g_sgu.shape, F32)],
        compiler_params=_cp(("arbitrary",)),
    )(proj_a, d_ya, g_sgu, w_s, b_st)


def _attn_bwd(proj_b, d_yb, sinks, rel_bias, n_seq, seq):
    nb = seq // CHUNK
    bk = jnp.asarray(_band_buckets())

    def body(qkv_ref, do_ref, bk_ref, rel_ref, sink_ref, d_ref, gs_ref, gr_ref, bias_scr, dbias_scr, dk_scr, dv_scr, ds_scr):
        b = pl.program_id(0)
        _build_bias(bias_scr, bk_ref, rel_ref)
        col = lax.broadcasted_iota(jnp.int32, (CHUNK, 2 * CHUNK), 1)
        lane = lax.broadcasted_iota(jnp.int32, (2 * CHUNK, LANES), 1)
        valid = bk_ref[...] >= 0

        @pl.when(b == 0)
        def _():
            dbias_scr[...] = jnp.zeros_like(dbias_scr)
            ds_scr[...] = jnp.zeros_like(ds_scr)

        dk_scr[...] = jnp.zeros_like(dk_scr)
        dv_scr[...] = jnp.zeros_like(dv_scr)

        def to_kv_lanes(a, hh, kvh):
            a = jnp.where((lane >= HEAD_DIM) if hh == 1 else (lane < HEAD_DIM), a, 0.0)
            return a if hh == kvh else pltpu.roll(a, HEAD_DIM, 1)

        def blk(n, carry):
            r0, kv, vv = _attn_block_inputs(qkv_ref, n)
            ok = valid & ((col >= CHUNK) | (n > 0))
            dkw = jnp.zeros((2 * CHUNK, KV_DIM), F32)
            dvw = jnp.zeros((2 * CHUNK, KV_DIM), F32)
            for pr in range(N_HEADS // 2):
                ps = slice(pr * LANES, (pr + 1) * LANES)
                qp = qkv_ref[pl.ds(r0, CHUNK), ps]
                dop = do_ref[pl.ds(r0, CHUNK), ps]
                kvh = pr // 2
                dq = jnp.zeros((CHUNK, LANES), F32)
                for hh in range(2):
                    h = 2 * pr + hh
                    prob, psink = _attn_probs(qp, kv[kvh][hh], bias_scr[h], sink_ref[0, h], ok)
                    dp = _dot_nt(dop, vv[kvh][hh])
                    delta = jnp.sum(prob * dp, axis=-1, keepdims=True)
                    dsc = prob * (dp - delta)
                    ds_scr[h] += psink * delta
                    dbias_scr[h] += dsc
                    dsb = (dsc * (HEAD_DIM ** -0.5)).astype(BF16)
                    dq = dq + _dot(dsb, kv[kvh][hh])
                    dkw = dkw + to_kv_lanes(_dot_tn(dsb, qp), hh, kvh)
                    dvw = dvw + to_kv_lanes(_dot_tn(prob.astype(BF16), dop), hh, kvh)
                d_ref[pl.ds(r0, CHUNK), ps] = dq.astype(BF16)
            dk_scr[pl.ds(r0, 2 * CHUNK), :] += dkw
            dv_scr[pl.ds(r0, 2 * CHUNK), :] += dvw
            return carry

        lax.fori_loop(0, nb, blk, 0)
        d_ref[:, Q_DIM:Q_DIM + KV_DIM] = dk_scr[CHUNK:, :].astype(BF16)
        d_ref[:, Q_DIM + KV_DIM:] = dv_scr[CHUNK:, :].astype(BF16)

        @pl.when(b == n_seq - 1)
        def _():
            bkv = bk_ref[...]
            for h in range(N_HEADS):
                gs_ref[0:1, h:h + 1] = -jnp.sum(ds_scr[h], axis=0, keepdims=True)
                db = dbias_scr[h]
                for bb in range(N_BUCKETS):
                    part = jnp.sum(jnp.where(bkv == bb, db, 0.0), axis=-1, keepdims=True)
                    gr_ref[bb:bb + 1, h:h + 1] = jnp.sum(part, axis=0, keepdims=True)

    smem = pl.BlockSpec(memory_space=pltpu.SMEM)
    return pl.pallas_call(
        body, name="attn_bwd", grid=(n_seq,),
        in_specs=[_row(seq, B_DIM), _row(seq, Q_DIM), _full(bk.shape), smem, smem],
        out_specs=[_row(seq, B_DIM), _full((1, N_HEADS)), _full((N_BUCKETS, N_HEADS))],
        out_shape=[_sds((n_seq * seq, B_DIM), BF16), _sds((1, N_HEADS), F32), _sds((N_BUCKETS, N_HEADS), F32)],
        scratch_shapes=[pltpu.VMEM((N_HEADS, CHUNK, 2 * CHUNK), F32), pltpu.VMEM((N_HEADS, CHUNK, 2 * CHUNK), F32),
                        pltpu.VMEM((seq + CHUNK, KV_DIM), F32), pltpu.VMEM((seq + CHUNK, KV_DIM), F32),
                        pltpu.VMEM((N_HEADS, CHUNK, 1), F32)],
        compiler_params=_cp(("arbitrary",), 40),
    )(proj_b, d_yb, bk, rel_bias, sinks)


def _inproj_bwd(d_g, d_a, d_b, x2, dx1, g_mix, w_g, w_a, w_b, tm):
    T = x2.shape[0]

    def body(dg_ref, da_ref, db_ref, x_ref, dx1_ref, g_ref, wg_ref, wa_ref, wb_ref, gx_ref, gg_ref):
        dh = _dot_nt(dg_ref[...], wg_ref[...]) + _dot_nt(da_ref[...], wa_ref[...]) + _dot_nt(db_ref[...], wb_ref[...])
        x = x_ref[...]
        r = _rms_r(x)
        n = x * r
        gx_ref[...] = dx1_ref[...] + _rms_bwd(dh, n, r, g_ref[...])

        @pl.when(pl.program_id(0) == 0)
        def _():
            gg_ref[...] = jnp.zeros_like(gg_ref)

        gg_ref[...] += jnp.sum(dh * n, axis=0, keepdims=True)

    return pl.pallas_call(
        body, name="inproj_bwd", grid=(T // tm,),
        in_specs=[_row(tm, G_DIM), _row(tm, A_DIM), _row(tm, B_DIM), _row(tm, D_MODEL), _row(tm, D_MODEL),
                  _full(g_mix.shape), _full(w_g.shape), _full(w_a.shape), _full(w_b.shape)],
        out_specs=[_row(tm, D_MODEL), _full((1, D_MODEL))],
        out_shape=[_sds((T, D_MODEL), F32), _sds((1, D_MODEL), F32)],
        compiler_params=_cp(("arbitrary",), 48),
    )(d_g, d_a, d_b, x2, dx1, g_mix, w_g, w_a, w_b)


def _local_step(x, target, g_mix, g_sgu, w_s, b_s, sinks, rel_bias, g_ffn, b_conv, g_final,
                w_g, w_a, w_b, w_pa, w_pb, w_out, w_up, w_conv, w_down):
    n_seq, seq, _ = x.shape
    T = n_seq * seq
    tm = min(256, seq)
    x2 = x.reshape(T, D_MODEL)
    tgt = target.reshape(T, D_MODEL)
    b_st = b_s.T
    g_fin = g_final.reshape(1, D_MODEL)

    proj_g, proj_a, proj_b, h = _inproj(x2, g_mix, w_g, w_a, w_b, tm)
    y_a = _sgu_fwd(proj_a, g_sgu, w_s, b_st, tm)
    y_b = _attn_fwd(proj_b, sinks, rel_bias, n_seq, seq)
    x1, merged = _merge_fwd(x2, y_a, y_b, proj_g, w_pa, w_pb, w_out, tm)
    upre, h2 = _upproj(x1, g_ffn, w_up, tm)
    dx2, loss, gg_final = _ffn_down_loss(upre, x1, tgt, w_conv, b_conv, w_down, g_fin, tm, seq)

    d_gate, d_val, gw_down, gb_g, gb_v, gwc_g, gwc_v = _ffn_bwd_act(upre, dx2, w_conv, b_conv, w_down, tm, seq)
    gb_conv = jnp.concatenate([gb_g, gb_v], axis=1)
    gw_conv = jnp.concatenate([gwc_g, gwc_v], axis=1)
    d_upre, dx1, gg_ffn = _ffn_bwd_up(d_gate, d_val, dx2, x1, g_ffn, w_conv, w_up, tm, seq)
    gw_up = _matmul_tn(h2, d_upre, 2 * D_FF // 4, min(512, T), "grad_w_up")
    d_g, d_ya, d_yb, gw_out, gw_pa, gw_pb = _merge_bwd(dx1, merged, y_a, y_b, proj_g, w_pa, w_pb, w_out, tm)
    d_a, gw_s, gb_st, gg_sgu = _sgu_bwd(proj_a, d_ya, g_sgu, w_s, b_st, tm)
    d_b, g_sinks, g_rel = _attn_bwd(proj_b, d_yb, sinks, rel_bias, n_seq, seq)
    grad_x, gg_mix = _inproj_bwd(d_g, d_a, d_b, x2, dx1, g_mix, w_g, w_a, w_b, tm)
    gw_g = _matmul_tn(h, d_g, D_MODEL, min(512, T), "grad_w_in_gate")
    gw_a = _matmul_tn(h, d_a, A_DIM, min(512, T), "grad_w_in_a")
    gw_b = _matmul_tn(h, d_b, B_DIM, min(512, T), "grad_w_in_b")
    gw_in = jnp.concatenate([gw_a, gw_b, gw_g], axis=1)

    small = dict(g_mix=gg_mix, g_sgu=gg_sgu, w_s=gw_s, b_s=gb_st.T, sinks=g_sinks, rel_bias=g_rel,
                 g_ffn=gg_ffn, b_conv=gb_conv, g_final=gg_final, w_conv=gw_conv)
    big = dict(w_in=gw_in, w_pa=gw_pa, w_pb=gw_pb, w_out=gw_out, w_up=gw_up, w_down=gw_down)
    return loss, grad_x.reshape(x.shape), small, big


_PACK_ROWS = (("w_in", 960), ("w_pa", 128), ("w_pb", 128), ("w_out", 256), ("w_up", 1408), ("w_down", 704))
PACK_ROWS = sum(r for _, r in _PACK_ROWS)
HALF_ROWS = PACK_ROWS // 2
_COL_SHARDED = ("w_in", "w_pa", "w_pb", "w_up")

_SMALL = (("loss", (1, 1)), ("g_final", (1, D_MODEL)), ("g_mix", (1, D_MODEL)), ("g_ffn", (1, D_MODEL)),
          ("g_sgu", (1, A_WIDTH)), ("b_s", (A_GROUPS, CHUNK)), ("sinks", (1, N_HEADS)), ("rel_bias", (N_BUCKETS, N_HEADS)),
          ("b_conv", (1, 2 * D_FF)), ("w_conv", (3, 2 * D_FF)), ("w_s", (A_GROUPS, CHUNK, CHUNK)))
SMALL_ROWS = 96


def _pack_shard(parts, dtype):
    return jnp.concatenate([parts[n].astype(dtype).reshape(r, D_MODEL) for n, r in _PACK_ROWS], axis=0)


def _unpack_shard(buf, shapes):
    out = {}
    off = 0
    for n, r in _PACK_ROWS:
        out[n] = buf[off:off + r].reshape(shapes[n])
        off += r
    return out


def _shard_of(full, name, i):
    if name in _COL_SHARDED:
        w = full.shape[1] // N_CHIPS
        return full[:, i * w:(i + 1) * w]
    h = full.shape[0] // N_CHIPS
    return full[i * h:(i + 1) * h]


def _unpack_full(gathered, shard_shapes):
    per_chip = [_unpack_shard(gathered[i], shard_shapes) for i in range(N_CHIPS)]
    return {n: jnp.concatenate([per_chip[i][n] for i in range(N_CHIPS)], axis=1 if n in _COL_SHARDED else 0)
            for n, _ in _PACK_ROWS}


def _pack_small(vals):
    flat = jnp.concatenate([vals[n].astype(F32).reshape(-1) for n, _ in _SMALL])
    flat = jnp.pad(flat, (0, SMALL_ROWS * D_MODEL - flat.shape[0]))
    return flat.reshape(SMALL_ROWS, D_MODEL)


def _unpack_small(buf):
    flat = buf.reshape(-1)
    out = {}
    off = 0
    for n, shp in _SMALL:
        k = int(np.prod(shp))
        out[n] = flat[off:off + k].reshape(shp)
        off += k
    return out


HBM = pl.BlockSpec(memory_space=pltpu.HBM)


def _mesh_pos():
    return lax.axis_index("x"), lax.axis_index("y"), lax.axis_index("c")


def _other_chips(x, y):
    return [(1 - x, y), (x, 1 - y), (1 - x, 1 - y)]


def _remote(src, dst, send_sem, recv_sem, to):
    return pltpu.make_async_remote_copy(src_ref=src, dst_ref=dst, send_sem=send_sem, recv_sem=recv_sem,
                                        device_id=to, device_id_type=MESH)


def _allgather_weights(packed, w_conv_pad):
    def body(w_ref, c_ref, out_ref, oc_ref, send_sems, recv_sems, csend_sems, crecv_sems, local_sems):
        x, y, c = _mesh_pos()
        me = 2 * x + y
        sibling = (x, y, 1 - c)
        chips = _other_chips(x, y)

        def half(chip, hc):
            return out_ref.at[chip, pl.ds(hc * HALF_ROWS, HALF_ROWS), :]

        mine = pltpu.make_async_copy(w_ref, out_ref.at[me], local_sems.at[0])
        mine_c = pltpu.make_async_copy(c_ref, oc_ref.at[me], local_sems.at[1])
        mine.start()
        mine_c.start()
        first = [_remote(w_ref.at[pl.ds(c * HALF_ROWS, HALF_ROWS), :], half(me, c), send_sems.at[j], recv_sems.at[j], (cx, cy, c))
                 for j, (cx, cy) in enumerate(chips)]
        first_c = [_remote(c_ref, oc_ref.at[me], csend_sems.at[j], crecv_sems.at[j], (cx, cy, c))
                   for j, (cx, cy) in enumerate(chips)]
        for cp in first + first_c:
            cp.start()
        passed = [_remote(half(2 * cx + cy, c), half(2 * cx + cy, c), send_sems.at[3 + j], recv_sems.at[3 + j], sibling)
                  for j, (cx, cy) in enumerate(chips)]
        for j, (cx, cy) in enumerate(chips):
            _remote(half(2 * cx + cy, c), half(2 * cx + cy, c), send_sems.at[j], recv_sems.at[j], (x, y, c)).wait_recv()
            passed[j].start()
        for j, (cx, cy) in enumerate(chips):
            _remote(half(2 * cx + cy, 1 - c), half(2 * cx + cy, 1 - c), send_sems.at[3 + j], recv_sems.at[3 + j], (x, y, c)).wait_recv()
            _remote(c_ref, oc_ref.at[2 * cx + cy], csend_sems.at[j], crecv_sems.at[j], (x, y, c)).wait_recv()
        for cp in first + first_c + passed:
            cp.wait_send()
        mine.wait()
        mine_c.wait()

    return pl.pallas_call(
        body, name="allgather_weights",
        in_specs=[HBM, HBM], out_specs=[HBM, HBM],
        out_shape=[_sds((N_CHIPS,) + packed.shape, packed.dtype), _sds((N_CHIPS,) + w_conv_pad.shape, w_conv_pad.dtype)],
        scratch_shapes=[pltpu.SemaphoreType.DMA((6,)), pltpu.SemaphoreType.DMA((6,)),
                        pltpu.SemaphoreType.DMA((3,)), pltpu.SemaphoreType.DMA((3,)), pltpu.SemaphoreType.DMA((2,))],
    )(packed, w_conv_pad)


def _pair_exchange(parts):
    def body(p_ref, q_ref, send_sems, recv_sems):
        x, y, c = _mesh_pos()
        copies = [_remote(p_ref.at[i, 1 - c], q_ref.at[i], send_sems.at[i], recv_sems.at[i], (x, y, 1 - c))
                  for i in range(N_CHIPS)]
        for cp in copies:
            cp.start()
        for cp in copies:
            cp.wait()

    return pl.pallas_call(
        body, name="grad_pair_exchange", in_specs=[HBM], out_specs=HBM,
        out_shape=_sds((N_CHIPS, HALF_ROWS, D_MODEL), parts.dtype),
        scratch_shapes=[pltpu.SemaphoreType.DMA((N_CHIPS,)), pltpu.SemaphoreType.DMA((N_CHIPS,))],
    )(parts)


def _pair_add(parts, from_sibling, c):
    tr = 448

    def body(c_ref, p_ref, q_ref, o_ref):
        o_ref[...] = (p_ref[...].astype(F32)[:, 0] + q_ref[...].astype(F32)).astype(BF16)

    return pl.pallas_call(
        body, name="grad_pair_add",
        grid_spec=pltpu.PrefetchScalarGridSpec(
            num_scalar_prefetch=1, grid=(N_CHIPS, HALF_ROWS // tr),
            in_specs=[pl.BlockSpec((1, 1, tr, D_MODEL), lambda i, r, cr: (i, cr[0], r, 0)),
                      pl.BlockSpec((1, tr, D_MODEL), lambda i, r, cr: (i, r, 0))],
            out_specs=pl.BlockSpec((1, tr, D_MODEL), lambda i, r, cr: (i, r, 0))),
        out_shape=_sds((N_CHIPS, HALF_ROWS, D_MODEL), BF16),
        compiler_params=_cp(("arbitrary", "arbitrary")),
    )(c.reshape(1), parts, from_sibling)


def _chip_exchange(sums):
    def body(s_ref, r_ref, send_sems, recv_sems):
        x, y, c = _mesh_pos()
        me = 2 * x + y
        copies = [_remote(s_ref.at[2 * cx + cy], r_ref.at[j], send_sems.at[j], recv_sems.at[j], (cx, cy, c))
                  for j, (cx, cy) in enumerate(_other_chips(x, y))]
        for cp in copies:
            cp.start()
        for cp in copies:
            cp.wait()

    return pl.pallas_call(
        body, name="grad_chip_exchange", in_specs=[HBM], out_specs=HBM,
        out_shape=_sds((3, HALF_ROWS, D_MODEL), sums.dtype),
        scratch_shapes=[pltpu.SemaphoreType.DMA((3,)), pltpu.SemaphoreType.DMA((3,))],
    )(sums)


def _owner_sum(parts, from_sibling, from_chips, me, c):
    tr = 448

    def body(s_ref, p_ref, q_ref, r_ref, o_ref):
        acc = p_ref[0, 0].astype(F32) + q_ref[0].astype(F32)
        for j in range(3):
            acc = acc + r_ref[j].astype(F32)
        o_ref[...] = acc

    return pl.pallas_call(
        body, name="grad_owner_sum",
        grid_spec=pltpu.PrefetchScalarGridSpec(
            num_scalar_prefetch=1, grid=(HALF_ROWS // tr,),
            in_specs=[pl.BlockSpec((1, 1, tr, D_MODEL), lambda r, s: (s[0], s[1], r, 0)),
                      pl.BlockSpec((1, tr, D_MODEL), lambda r, s: (s[0], r, 0)),
                      pl.BlockSpec((3, tr, D_MODEL), lambda r, s: (0, r, 0))],
            out_specs=pl.BlockSpec((tr, D_MODEL), lambda r, s: (r, 0))),
        out_shape=_sds((HALF_ROWS, D_MODEL), F32),
        compiler_params=_cp(("arbitrary",)),
    )(jnp.stack([me, c]), parts, from_sibling, from_chips)


def _pair_share(half):
    def body(h_ref, o_ref, send_sem, recv_sem, local_sem):
        x, y, c = _mesh_pos()
        mine = pltpu.make_async_copy(h_ref, o_ref.at[c], local_sem)
        mine.start()
        cp = _remote(h_ref, o_ref.at[c], send_sem, recv_sem, (x, y, 1 - c))
        cp.start()
        _remote(h_ref, o_ref.at[1 - c], send_sem, recv_sem, (x, y, c)).wait_recv()
        cp.wait_send()
        mine.wait()

    return pl.pallas_call(
        body, name="grad_pair_share", in_specs=[HBM], out_specs=HBM,
        out_shape=_sds((2,) + half.shape, half.dtype),
        scratch_shapes=[pltpu.SemaphoreType.DMA, pltpu.SemaphoreType.DMA, pltpu.SemaphoreType.DMA],
    )(half)


def _allgather_small(block):
    m_per = block.shape[0]

    def body(x_ref, out_ref, send_sems, recv_sems, local_sem):
        x, y, c = _mesh_pos()
        me, sibling = (x, y, c), (x, y, 1 - c)
        chips = _other_chips(x, y)

        def rows(px, py, pc):
            return out_ref.at[4 * px + 2 * py + pc]

        def copy(k, block_of, to, src=None):
            return _remote(rows(*block_of) if src is None else src, rows(*block_of), send_sems.at[k], recv_sems.at[k], to)

        mine = pltpu.make_async_copy(x_ref, rows(*me), local_sem)
        mine.start()
        first = [copy(0, me, sibling, src=x_ref)]
        first += [copy(1 + j, me, (*chip, c), src=x_ref) for j, chip in enumerate(chips)]
        for cp in first:
            cp.start()
        passed = [copy(4 + j, (*chip, c), sibling) for j, chip in enumerate(chips)]
        for j, chip in enumerate(chips):
            copy(1 + j, (*chip, c), me).wait_recv()
            passed[j].start()
        copy(0, sibling, me).wait_recv()
        for j, chip in enumerate(chips):
            copy(4 + j, (*chip, 1 - c), me).wait_recv()
        for cp in first + passed:
            cp.wait_send()
        mine.wait()

    return pl.pallas_call(
        body, name="allgather_small",
        in_specs=[pl.BlockSpec(memory_space=pltpu.VMEM)], out_specs=pl.BlockSpec(memory_space=pltpu.VMEM),
        out_shape=_sds((N_DEV, m_per, D_MODEL), block.dtype),
        scratch_shapes=[pltpu.SemaphoreType.DMA((7,)), pltpu.SemaphoreType.DMA((7,)), pltpu.SemaphoreType.DMA],
    )(block)


def _adam_math(w, g, m, v):
    m = ADAM_B1 * m + (1.0 - ADAM_B1) * g
    v = ADAM_B2 * v + (1.0 - ADAM_B2) * (g * g)
    m_hat = m / (1.0 - ADAM_B1 ** ADAM_STEP)
    v_hat = v / (1.0 - ADAM_B2 ** ADAM_STEP)
    delta = -ADAM_LR * (m_hat / (jnp.sqrt(v_hat) + ADAM_EPS) + ADAM_WD * w)
    return delta, m, v


def _adamw(w, g, m, v, name):
    rows, cols = w.shape
    tr = rows
    for cand in (256, 128, 64, 32, 16, 8):
        if rows % cand == 0 and rows > cand:
            tr = cand
            break

    def body(w_ref, g_ref, m_ref, v_ref, d_ref, nm_ref, nv_ref):
        d, nm, nv = _adam_math(w_ref[...], g_ref[...], m_ref[...], v_ref[...])
        d_ref[...] = d
        nm_ref[...] = nm
        nv_ref[...] = nv

    spec = pl.BlockSpec((tr, cols), lambda i: (i, 0))
    return pl.pallas_call(
        body, name=name, grid=(rows // tr,), in_specs=[spec] * 4, out_specs=[spec] * 3,
        out_shape=[_sds(w.shape, F32)] * 3, compiler_params=_cp(("arbitrary",)),
    )(w, g, m, v)


def _small_sum_adamw(gathered, w, m, v):
    def body(a_ref, w_ref, m_ref, v_ref, g_ref, d_ref, nm_ref, nv_ref):
        g = a_ref[0]
        for k in range(1, N_DEV):
            g = g + a_ref[k]
        g_ref[...] = g
        d, nm, nv = _adam_math(w_ref[...], g, m_ref[...], v_ref[...])
        d_ref[...] = d
        nm_ref[...] = nm
        nv_ref[...] = nv

    return pl.pallas_call(
        body, name="small_sum_adamw", out_shape=[_sds(w.shape, F32)] * 4,
    )(gathered, w, m, v)


_NAMES = ("g_mix", "w_in", "g_sgu", "w_s", "b_s", "sinks", "rel_bias", "w_pa", "w_pb", "w_out",
          "g_ffn", "w_up", "w_conv", "b_conv", "w_down", "g_final")
_BIG = tuple(n for n, _ in _PACK_ROWS)


def kernel(x, g_mix, w_in, g_sgu, w_s, b_s, sinks, rel_bias, w_pa, w_pb, w_out, g_ffn, w_up, w_conv, b_conv, w_down, g_final, loss_target, m_g_mix, m_w_in, m_g_sgu, m_w_s, m_b_s, m_sinks, m_rel_bias, m_w_pa, m_w_pb, m_w_out, m_g_ffn, m_w_up, m_w_conv, m_b_conv, m_w_down, m_g_final, v_g_mix, v_w_in, v_g_sgu, v_w_s, v_b_s, v_sinks, v_rel_bias, v_w_pa, v_w_pb, v_w_out, v_g_ffn, v_w_up, v_w_conv, v_b_conv, v_w_down, v_g_final):
    w = dict(g_mix=g_mix, w_in=w_in, g_sgu=g_sgu, w_s=w_s, b_s=b_s, sinks=sinks, rel_bias=rel_bias, w_pa=w_pa, w_pb=w_pb,
             w_out=w_out, g_ffn=g_ffn, w_up=w_up, w_conv=w_conv, b_conv=b_conv, w_down=w_down, g_final=g_final)
    m = dict(g_mix=m_g_mix, w_in=m_w_in, g_sgu=m_g_sgu, w_s=m_w_s, b_s=m_b_s, sinks=m_sinks, rel_bias=m_rel_bias, w_pa=m_w_pa,
             w_pb=m_w_pb, w_out=m_w_out, g_ffn=m_g_ffn, w_up=m_w_up, w_conv=m_w_conv, b_conv=m_b_conv, w_down=m_w_down,
             g_final=m_g_final)
    v = dict(g_mix=v_g_mix, w_in=v_w_in, g_sgu=v_g_sgu, w_s=v_w_s, b_s=v_b_s, sinks=v_sinks, rel_bias=v_rel_bias, w_pa=v_w_pa,
             w_pb=v_w_pb, w_out=v_w_out, g_ffn=v_g_ffn, w_up=v_w_up, w_conv=v_w_conv, b_conv=v_b_conv, w_down=v_w_down,
             g_final=v_g_final)
    xi, yi, ci = _mesh_pos()
    me = 2 * xi + yi

    shard = {n: w[n][0] for n in _BIG}
    shard_shapes = {n: shard[n].shape for n in _BIG}
    wc_shard = w["w_conv"][0]
    wc_pad = jnp.pad(wc_shard, ((0, 5), (0, 0)))
    gathered, wc_all = _allgather_weights(_pack_shard(shard, BF16), wc_pad)
    full = _unpack_full(gathered, shard_shapes)
    w_conv_full = jnp.concatenate([wc_all[i, :3] for i in range(N_CHIPS)], axis=1)
    w_in_full = full["w_in"]
    w_a = w_in_full[:, :A_DIM]
    w_b = w_in_full[:, A_DIM:A_DIM + B_DIM]
    w_g = w_in_full[:, A_DIM + B_DIM:]

    loss, grad_x, small, big = _local_step(
        x, loss_target, w["g_mix"], w["g_sgu"], w["w_s"][0], w["b_s"][0], w["sinks"], w["rel_bias"], w["g_ffn"],
        w["b_conv"], w["g_final"], w_g, w_a, w_b, full["w_pa"], full["w_pb"], full["w_out"], full["w_up"],
        w_conv_full, full["w_down"])

    small["loss"] = loss
    all_small = _allgather_small(_pack_small(small))
    sw = {n: (jnp.zeros((1, 1), F32) if n in ("loss", "w_conv") else w[n]) for n, _ in _SMALL}
    sm = {n: (jnp.zeros((1, 1), F32) if n in ("loss", "w_conv") else m[n]) for n, _ in _SMALL}
    sv = {n: (jnp.zeros((1, 1), F32) if n in ("loss", "w_conv") else v[n]) for n, _ in _SMALL}
    for d in (sw, sm, sv):
        d["w_conv"] = jnp.zeros((3, 2 * D_FF), F32)
    s_g, s_d, s_m, s_v = [_unpack_small(a) for a in _small_sum_adamw(all_small, _pack_small(sw), _pack_small(sm), _pack_small(sv))]

    parts = jnp.stack([_pack_shard({n: _shard_of(big[n], n, i) for n in _BIG}, BF16) for i in range(N_CHIPS)])
    parts = parts.reshape(N_CHIPS, 2, HALF_ROWS, D_MODEL)
    from_sibling = _pair_exchange(parts)
    pair_sums = _pair_add(parts, from_sibling, ci)
    from_chips = _chip_exchange(pair_sums)
    mine_half = _owner_sum(parts, from_sibling, from_chips, me, ci)
    g_big = _unpack_shard(_pair_share(mine_half).reshape(PACK_ROWS, D_MODEL), shard_shapes)

    grads, deltas, new_m, new_v = {}, {}, {}, {}
    for n in _BIG:
        d, nm, nv = _adamw(shard[n], g_big[n], m[n][0], v[n][0], "adamw_" + n)
        grads[n], deltas[n], new_m[n], new_v[n] = g_big[n][None], d[None], nm[None], nv[None]
    wcols = wc_shard.shape[1]
    g_wc = lax.dynamic_slice(s_g["w_conv"], (0, me * wcols), (3, wcols))
    d, nm, nv = _adamw(wc_shard, g_wc, m["w_conv"][0], v["w_conv"][0], "adamw_w_conv")
    grads["w_conv"], deltas["w_conv"], new_m["w_conv"], new_v["w_conv"] = g_wc[None], d[None], nm[None], nv[None]
    for n, _ in _SMALL:
        if n in ("loss", "w_conv"):
            continue
        shp = w[n].shape
        grads[n], deltas[n], new_m[n], new_v[n] = (s_g[n].reshape(shp), s_d[n].reshape(shp), s_m[n].reshape(shp),
                                                    s_v[n].reshape(shp))

    return (s_g["loss"].reshape(()), grad_x, *[grads[n] for n in _NAMES], *[deltas[n] for n in _NAMES],
            *[new_m[n] for n in _NAMES], *[new_v[n] for n in _NAMES])
```

```python
import functools

import numpy as np
import jax
import jax.numpy as jnp
from jax import lax
from jax.experimental import pallas as pl
from jax.experimental.pallas import tpu as pltpu

F32 = jnp.float32
BF16 = jnp.bfloat16

D_MODEL = 1024
CHUNK = 128
A_GROUPS = 4
A_WIDTH = 512
N_HEADS = 8
HEAD_DIM = 64
Q_DIM = 512
KV_DIM = 128
N_BUCKETS = 32
MAX_DISTANCE = 128
D_FF = 2816
EPS = 1e-6
NEG_INF = -1e30
G_DIM = 2 * D_MODEL
A_DIM = 2 * A_WIDTH
B_DIM = Q_DIM + 2 * KV_DIM
LANES = 128
BF16_ROWS = 16
N_CHIPS = 4
N_DEV = 8

ADAM_LR = 0.001
ADAM_B1 = 0.9
ADAM_B2 = 0.999
ADAM_EPS = 1e-08
ADAM_WD = 0.01
ADAM_STEP = 10

MESH = pl.DeviceIdType.MESH
_GELU_C = 0.7978845608028654
_GELU_A = 0.044715


def _cp(sem=None, vmem_mb=None):
    kw = {}
    if sem is not None:
        kw["dimension_semantics"] = sem
    if vmem_mb is not None:
        kw["vmem_limit_bytes"] = vmem_mb << 20
    return pltpu.CompilerParams(**kw)


def _dot(a, b):
    return jnp.dot(a, b, preferred_element_type=F32)


def _dot_nt(a, b):
    return lax.dot_general(a, b, (((1,), (1,)), ((), ())), preferred_element_type=F32)


def _dot_tn(a, b):
    return lax.dot_general(a, b, (((0,), (0,)), ((), ())), preferred_element_type=F32)


def _rms_r(x):
    return lax.rsqrt(jnp.mean(x * x, axis=-1, keepdims=True) + EPS)


def _rms_bwd(dh, n, r, g):
    dn = dh * g
    return r * (dn - n * jnp.mean(dn * n, axis=-1, keepdims=True))


def _gelu(x):
    t = jnp.tanh(_GELU_C * (x + _GELU_A * (x * x * x)))
    return 0.5 * x * (1.0 + t), t


def _gelu_grad(x, t):
    return 0.5 * (1.0 + t) + 0.5 * x * (1.0 - t * t) * (_GELU_C * (1.0 + 3.0 * _GELU_A * x * x))


def _sigmoid(x):
    return 1.0 / (1.0 + jnp.exp(-x))


def _row(tm, w):
    return pl.BlockSpec((tm, w), lambda i: (i, 0))


def _full(shape):
    nd = len(shape)
    return pl.BlockSpec(tuple(shape), lambda *_: (0,) * nd)


def _sds(shape, dtype):
    return jax.ShapeDtypeStruct(tuple(shape), dtype)


def _band_buckets():
    i = np.arange(CHUNK)[:, None]
    j = np.arange(2 * CHUNK)[None, :]
    dist = i + CHUNK - j
    valid = (dist >= 0) & (dist < CHUNK)
    d = np.clip(dist, 0, None)
    max_exact = N_BUCKETS // 2
    large = max_exact + (np.log(np.maximum(d, 1) / max_exact) / np.log(MAX_DISTANCE / max_exact)
                         * (N_BUCKETS - max_exact)).astype(np.int32)
    large = np.minimum(large, N_BUCKETS - 1)
    buckets = np.where(d < max_exact, d, large).astype(np.int32)
    return np.where(valid, buckets, -1).astype(np.int32)


def _inproj(x2, g_mix, w_g, w_a, w_b, tm):
    T = x2.shape[0]

    def body(x_ref, g_ref, wg_ref, wa_ref, wb_ref, pg_ref, pa_ref, pb_ref, h_ref):
        x = x_ref[...]
        h = (x * _rms_r(x) * g_ref[...]).astype(BF16)
        h_ref[...] = h
        pg_ref[...] = _dot(h, wg_ref[...]).astype(BF16)
        pa_ref[...] = _dot(h, wa_ref[...]).astype(BF16)
        pb_ref[...] = _dot(h, wb_ref[...]).astype(BF16)

    return pl.pallas_call(
        body, name="inproj", grid=(T // tm,),
        in_specs=[_row(tm, D_MODEL), _full(g_mix.shape), _full(w_g.shape), _full(w_a.shape), _full(w_b.shape)],
        out_specs=[_row(tm, G_DIM), _row(tm, A_DIM), _row(tm, B_DIM), _row(tm, D_MODEL)],
        out_shape=[_sds((T, G_DIM), BF16), _sds((T, A_DIM), BF16), _sds((T, B_DIM), BF16), _sds((T, D_MODEL), BF16)],
        compiler_params=_cp(("arbitrary",), 48),
    )(x2, g_mix, w_g, w_a, w_b)


def _sgu_parts(p, g):
    pu = p[:, :A_WIDTH]
    pv = p[:, A_WIDTH:]
    u, tu = _gelu(pu)
    vv, tv = _gelu(pv)
    rv = _rms_r(vv)
    vn = (vv * rv * g).astype(BF16)
    return pu, pv, u, tu, vv, tv, rv, vn


def _tril():
    r = lax.broadcasted_iota(jnp.int32, (CHUNK, CHUNK), 0)
    c = lax.broadcasted_iota(jnp.int32, (CHUNK, CHUNK), 1)
    return r >= c


def _sgu_fwd(proj_a, g_sgu, w_s, b_st, tm):
    T = proj_a.shape[0]

    def body(p_ref, g_ref, ws_ref, bs_ref, y_ref):
        tril = _tril()
        _, _, u, _, _, _, _, vn = _sgu_parts(p_ref[...].astype(F32), g_ref[...])
        for gi in range(A_GROUPS):
            wm = jnp.where(tril, ws_ref[gi], 0.0).astype(BF16)
            bcol = bs_ref[:, gi:gi + 1]
            cs = slice(gi * CHUNK, (gi + 1) * CHUNK)
            for c in range(tm // CHUNK):
                rs = slice(c * CHUNK, (c + 1) * CHUNK)
                s = _dot(wm, vn[rs, cs]) + bcol
                y_ref[rs, cs] = (u[rs, cs] * s).astype(BF16)

    return pl.pallas_call(
        body, name="sgu_fwd", grid=(T // tm,),
        in_specs=[_row(tm, A_DIM), _full(g_sgu.shape), _full(w_s.shape), _full(b_st.shape)],
        out_specs=_row(tm, A_WIDTH), out_shape=_sds((T, A_WIDTH), BF16),
        compiler_params=_cp(("arbitrary",)),
    )(proj_a, g_sgu, w_s, b_st)


def _kv_variants(a):
    a = a.astype(F32)
    lane = lax.broadcasted_iota(jnp.int32, a.shape, 1)
    lo = jnp.where(lane < HEAD_DIM, a, 0.0)
    hi = jnp.where(lane >= HEAD_DIM, a, 0.0)
    lo_r = pltpu.roll(lo, HEAD_DIM, 1)
    hi_r = pltpu.roll(hi, HEAD_DIM, 1)
    return ((lo.astype(BF16), lo_r.astype(BF16)), (hi_r.astype(BF16), hi.astype(BF16)))


def _build_bias(bias_scr, bk_ref, rel_ref):
    bk = bk_ref[...]
    for h in range(N_HEADS):
        acc = jnp.zeros((CHUNK, 2 * CHUNK), F32)
        for b in range(N_BUCKETS):
            acc = jnp.where(bk == b, rel_ref[b, h], acc)
        bias_scr[h] = acc


def _attn_probs(qp, k_h, bias_h, sink_h, ok):
    s = _dot_nt(qp, k_h) * (HEAD_DIM ** -0.5) + bias_h
    s = jnp.where(ok, s, NEG_INF)
    m = jnp.maximum(jnp.max(s, axis=-1, keepdims=True), sink_h)
    p = jnp.exp(s - m)
    es = jnp.exp(sink_h - m)
    den = jnp.sum(p, axis=-1, keepdims=True) + es
    return p / den, es / den


def _attn_block_inputs(qkv_ref, n):
    r0 = pl.multiple_of(n * CHUNK, CHUNK)
    rp = pl.multiple_of(jnp.maximum(n - 1, 0) * CHUNK, CHUNK)
    kw = jnp.concatenate([qkv_ref[pl.ds(rp, CHUNK), Q_DIM:Q_DIM + KV_DIM],
                          qkv_ref[pl.ds(r0, CHUNK), Q_DIM:Q_DIM + KV_DIM]], axis=0)
    vw = jnp.concatenate([qkv_ref[pl.ds(rp, CHUNK), Q_DIM + KV_DIM:B_DIM],
                          qkv_ref[pl.ds(r0, CHUNK), Q_DIM + KV_DIM:B_DIM]], axis=0)
    return r0, _kv_variants(kw), _kv_variants(vw)


def _attn_fwd(proj_b, sinks, rel_bias, n_seq, seq):
    nb = seq // CHUNK
    bk = jnp.asarray(_band_buckets())

    def body(qkv_ref, bk_ref, rel_ref, sink_ref, o_ref, bias_scr):
        _build_bias(bias_scr, bk_ref, rel_ref)
        col = lax.broadcasted_iota(jnp.int32, (CHUNK, 2 * CHUNK), 1)
        valid = bk_ref[...] >= 0

        def blk(n, carry):
            r0, kv, vv = _attn_block_inputs(qkv_ref, n)
            ok = valid & ((col >= CHUNK) | (n > 0))
            for pr in range(N_HEADS // 2):
                qp = qkv_ref[pl.ds(r0, CHUNK), pr * LANES:(pr + 1) * LANES]
                kvh = pr // 2
                acc = jnp.zeros((CHUNK, LANES), F32)
                for hh in range(2):
                    h = 2 * pr + hh
                    prob, _ = _attn_probs(qp, kv[kvh][hh], bias_scr[h], sink_ref[0, h], ok)
                    acc = acc + _dot(prob.astype(BF16), vv[kvh][hh])
                o_ref[pl.ds(r0, CHUNK), pr * LANES:(pr + 1) * LANES] = acc.astype(BF16)
            return carry

        lax.fori_loop(0, nb, blk, 0)

    smem = pl.BlockSpec(memory_space=pltpu.SMEM)
    return pl.pallas_call(
        body, name="attn_fwd", grid=(n_seq,),
        in_specs=[_row(seq, B_DIM), _full(bk.shape), smem, smem],
        out_specs=_row(seq, Q_DIM), out_shape=_sds((n_seq * seq, Q_DIM), BF16),
        scratch_shapes=[pltpu.VMEM((N_HEADS, CHUNK, 2 * CHUNK), F32)],
        compiler_params=_cp(("arbitrary",)),
    )(proj_b, bk, rel_bias, sinks)


def _merge_fwd(x2, y_a, y_b, proj_g, w_pa, w_pb, w_out, tm):
    T = x2.shape[0]

    def body(x_ref, ya_ref, yb_ref, g_ref, wpa_ref, wpb_ref, wo_ref, x1_ref, mg_ref):
        g = g_ref[...].astype(F32)
        pa = _dot(ya_ref[...], wpa_ref[...])
        pb = _dot(yb_ref[...], wpb_ref[...])
        merged = (_sigmoid(g[:, :D_MODEL]) * pa + _sigmoid(g[:, D_MODEL:]) * pb).astype(BF16)
        mg_ref[...] = merged
        x1_ref[...] = x_ref[...] + _dot(merged, wo_ref[...])

    return pl.pallas_call(
        body, name="merge_fwd", grid=(T // tm,),
        in_specs=[_row(tm, D_MODEL), _row(tm, A_WIDTH), _row(tm, Q_DIM), _row(tm, G_DIM),
                  _full(w_pa.shape), _full(w_pb.shape), _full(w_out.shape)],
        out_specs=[_row(tm, D_MODEL), _row(tm, D_MODEL)],
        out_shape=[_sds((T, D_MODEL), F32), _sds((T, D_MODEL), BF16)],
        compiler_params=_cp(("arbitrary",), 40),
    )(x2, y_a, y_b, proj_g, w_pa, w_pb, w_out)


def _upproj(x1, g_ffn, w_up, tm):
    T = x1.shape[0]

    def body(x_ref, g_ref, w_ref, u_ref, h_ref):
        x = x_ref[...]
        h = (x * _rms_r(x) * g_ref[...]).astype(BF16)
        h_ref[...] = h
        u_ref[...] = _dot(h, w_ref[...]).astype(BF16)

    return pl.pallas_call(
        body, name="upproj", grid=(T // tm,),
        in_specs=[_row(tm, D_MODEL), _full(g_ffn.shape), _full(w_up.shape)],
        out_specs=[_row(tm, 2 * D_FF), _row(tm, D_MODEL)],
        out_shape=[_sds((T, 2 * D_FF), BF16), _sds((T, D_MODEL), BF16)],
        compiler_params=_cp(("arbitrary",), 56),
    )(x1, g_ffn, w_up)


def _shift_down(u, halo, k):
    rolled = pltpu.roll(u, k, 0)
    row = lax.broadcasted_iota(jnp.int32, u.shape, 0)
    if k == 1:
        return jnp.where(row == 0, halo[1:2], rolled)
    return jnp.where(row == 0, halo[0:1], jnp.where(row == 1, halo[1:2], rolled))


def _shift_up(d, halo, k):
    tm = d.shape[0]
    rolled = pltpu.roll(d, tm - k, 0)
    row = lax.broadcasted_iota(jnp.int32, d.shape, 0)
    if k == 1:
        return jnp.where(row == tm - 1, halo[0:1], rolled)
    return jnp.where(row == tm - 2, halo[0:1], jnp.where(row == tm - 1, halo[1:2], rolled))


def _conv_taps(u_ref, halo_ref, cols, at_start):
    u = u_ref[:, cols].astype(F32)
    hl = halo_ref[:, cols].astype(F32)[BF16_ROWS - 2:BF16_ROWS]
    hl = jnp.where(at_start, 0.0, hl)
    return u, _shift_down(u, hl, 1), _shift_down(u, hl, 2)


def _conv_out(taps, wc, bc):
    u, u1, u2 = taps
    return wc[0:1] * u2 + wc[1:2] * u1 + wc[2:3] * u + bc


def _prev_halo_spec(tm, width, col_block=None):
    k = tm // BF16_ROWS
    if col_block is None:
        return pl.BlockSpec((BF16_ROWS, width), lambda i: (jnp.maximum(i * k - 1, 0), 0))
    return pl.BlockSpec((BF16_ROWS, width), lambda j, i: (jnp.maximum(i * k - 1, 0), col_block(j)))


def _ffn_down_loss(upre, x1, target, w_conv, b_conv, w_down, g_final, tm, seq):
    T = x1.shape[0]
    tiles_per_seq = seq // tm
    half = D_FF // 2

    def body(u_ref, hl_ref, x1_ref, t_ref, wc_ref, bc_ref, wd_ref, g_ref, dx2_ref, loss_ref, gg_ref):
        i = pl.program_id(0)
        at_start = (i % tiles_per_seq) == 0
        acc = jnp.zeros((tm, D_MODEL), F32)
        for j in range(2):
            gc = slice(j * half, (j + 1) * half)
            vc = slice(D_FF + j * half, D_FF + (j + 1) * half)
            gate = _conv_out(_conv_taps(u_ref, hl_ref, gc, at_start), wc_ref[:, gc], bc_ref[:, gc])
            val = _conv_out(_conv_taps(u_ref, hl_ref, vc, at_start), wc_ref[:, vc], bc_ref[:, vc])
            act = (gate * _sigmoid(gate) * val).astype(BF16)
            acc = acc + _dot(act, wd_ref[gc, :])
        x2 = x1_ref[...] + acc
        r = _rms_r(x2)
        n = x2 * r
        g = g_ref[...]
        diff = n * g - t_ref[...]
        dy = diff * (1.0 / D_MODEL)
        dx2_ref[...] = _rms_bwd(dy, n, r, g)

        @pl.when(i == 0)
        def _():
            loss_ref[...] = jnp.zeros_like(loss_ref)
            gg_ref[...] = jnp.zeros_like(gg_ref)

        loss_ref[...] += 0.5 * jnp.sum(jnp.mean(diff * diff, axis=-1, keepdims=True), axis=0, keepdims=True)
        gg_ref[...] += jnp.sum(dy * n, axis=0, keepdims=True)

    return pl.pallas_call(
        body, name="ffn_down_loss", grid=(T // tm,),
        in_specs=[_row(tm, 2 * D_FF), _prev_halo_spec(tm, 2 * D_FF), _row(tm, D_MODEL), _row(tm, D_MODEL),
                  _full(w_conv.shape), _full(b_conv.shape), _full(w_down.shape), _full(g_final.shape)],
        out_specs=[_row(tm, D_MODEL), _full((1, 1)), _full((1, D_MODEL))],
        out_shape=[_sds((T, D_MODEL), F32), _sds((1, 1), F32), _sds((1, D_MODEL), F32)],
        compiler_params=_cp(("arbitrary",), 56),
    )(upre, upre, x1, target, w_conv, b_conv, w_down, g_final)


def _ffn_bwd_act(upre, dx2, w_conv, b_conv, w_down, tm, seq):
    T = dx2.shape[0]
    tiles_per_seq = seq // tm
    half = D_FF // 2
    nt = T // tm

    def body(ug_ref, uv_ref, hg_ref, hv_ref, dx_ref, wcg_ref, wcv_ref, bcg_ref, bcv_ref, wd_ref,
             dg_ref, dv_ref, gwd_ref, gbg_ref, gbv_ref, gwg_ref, gwv_ref):
        i = pl.program_id(1)
        at_start = (i % tiles_per_seq) == 0
        allc = slice(0, half)
        tg = _conv_taps(ug_ref, hg_ref, allc, at_start)
        tv = _conv_taps(uv_ref, hv_ref, allc, at_start)
        gate = _conv_out(tg, wcg_ref[...], bcg_ref[...])
        val = _conv_out(tv, wcv_ref[...], bcv_ref[...])
        sg = _sigmoid(gate)
        silu = gate * sg
        dx = dx_ref[...].astype(BF16)
        d_act = _dot_nt(dx, wd_ref[...])
        d_val = d_act * silu
        d_gate = d_act * val * (sg * (1.0 + gate * (1.0 - sg)))
        dg_ref[...] = d_gate.astype(BF16)
        dv_ref[...] = d_val.astype(BF16)

        @pl.when(i == 0)
        def _():
            for r in (gwd_ref, gbg_ref, gbv_ref, gwg_ref, gwv_ref):
                r[...] = jnp.zeros_like(r)

        gwd_ref[...] += _dot_tn((silu * val).astype(BF16), dx)
        gbg_ref[...] += jnp.sum(d_gate, axis=0, keepdims=True)
        gbv_ref[...] += jnp.sum(d_val, axis=0, keepdims=True)
        for k in range(3):
            gwg_ref[k:k + 1, :] += jnp.sum(d_gate * tg[2 - k], axis=0, keepdims=True)
            gwv_ref[k:k + 1, :] += jnp.sum(d_val * tv[2 - k], axis=0, keepdims=True)

    gcol = lambda j: j
    vcol = lambda j: 2 + j
    tile = lambda cb: pl.BlockSpec((tm, half), lambda j, i: (i, cb(j)))
    vec = lambda rows, cb: pl.BlockSpec((rows, half), lambda j, i: (0, cb(j)))
    return pl.pallas_call(
        body, name="ffn_bwd_act", grid=(2, nt),
        in_specs=[tile(gcol), tile(vcol), _prev_halo_spec(tm, half, gcol), _prev_halo_spec(tm, half, vcol),
                  pl.BlockSpec((tm, D_MODEL), lambda j, i: (i, 0)),
                  vec(3, gcol), vec(3, vcol), vec(1, gcol), vec(1, vcol),
                  pl.BlockSpec((half, D_MODEL), lambda j, i: (j, 0))],
        out_specs=[tile(gcol), tile(gcol), pl.BlockSpec((half, D_MODEL), lambda j, i: (j, 0)),
                   vec(1, gcol), vec(1, gcol), vec(3, gcol), vec(3, gcol)],
        out_shape=[_sds((T, D_FF), BF16), _sds((T, D_FF), BF16), _sds((D_FF, D_MODEL), F32),
                   _sds((1, D_FF), F32), _sds((1, D_FF), F32), _sds((3, D_FF), F32), _sds((3, D_FF), F32)],
        compiler_params=_cp(("arbitrary", "arbitrary"), 56),
    )(upre, upre, upre, upre, dx2, w_conv, w_conv, b_conv, b_conv, w_down)


def _ffn_bwd_up(d_gate, d_val, dx2, x1, g_ffn, w_conv, w_up, tm, seq):
    T = dx2.shape[0]
    tiles_per_seq = seq // tm
    k16 = tm // BF16_ROWS
    n16 = T // BF16_ROWS
    cw = D_FF // 2

    def body(dg_ref, dv_ref, hg_ref, hv_ref, dx2_ref, x1_ref, g_ref, wc_ref, wu_ref, du_ref, dx1_ref, gg_ref):
        i = pl.program_id(0)
        at_end = (i % tiles_per_seq) == tiles_per_seq - 1
        dh = jnp.zeros((tm, D_MODEL), F32)
        for j in range(4):
            src, hsrc = (dg_ref, hg_ref) if j < 2 else (dv_ref, hv_ref)
            ls = slice((j % 2) * cw, (j % 2 + 1) * cw)
            cs = slice(j * cw, (j + 1) * cw)
            d = src[:, ls].astype(F32)
            hl = hsrc[:, ls].astype(F32)[0:2]
            hl = jnp.where(at_end, 0.0, hl)
            wc = wc_ref[:, cs]
            du = (wc[2:3] * d + wc[1:2] * _shift_up(d, hl, 1) + wc[0:1] * _shift_up(d, hl, 2)).astype(BF16)
            du_ref[:, cs] = du
            dh = dh + _dot_nt(du, wu_ref[:, cs])
        x = x1_ref[...]
        r = _rms_r(x)
        n = x * r
        dx1_ref[...] = dx2_ref[...] + _rms_bwd(dh, n, r, g_ref[...])

        @pl.when(i == 0)
        def _():
            gg_ref[...] = jnp.zeros_like(gg_ref)

        gg_ref[...] += jnp.sum(dh * n, axis=0, keepdims=True)

    nxt = pl.BlockSpec((BF16_ROWS, D_FF), lambda i: (jnp.minimum((i + 1) * k16, n16 - 1), 0))
    return pl.pallas_call(
        body, name="ffn_bwd_up", grid=(T // tm,),
        in_specs=[_row(tm, D_FF), _row(tm, D_FF), nxt, nxt, _row(tm, D_MODEL), _row(tm, D_MODEL),
                  _full(g_ffn.shape), _full(w_conv.shape), _full(w_up.shape)],
        out_specs=[_row(tm, 2 * D_FF), _row(tm, D_MODEL), _full((1, D_MODEL))],
        out_shape=[_sds((T, 2 * D_FF), BF16), _sds((T, D_MODEL), F32), _sds((1, D_MODEL), F32)],
        compiler_params=_cp(("arbitrary",), 60),
    )(d_gate, d_val, d_gate, d_val, dx2, x1, g_ffn, w_conv, w_up)


def _matmul_tn(a, b, tn, tk, name):
    T, M = a.shape
    N = b.shape[1]

    def body(a_ref, b_ref, o_ref):
        @pl.when(pl.program_id(1) == 0)
        def _():
            o_ref[...] = jnp.zeros_like(o_ref)

        o_ref[...] += _dot_tn(a_ref[...], b_ref[...])

    return pl.pallas_call(
        body, name=name, grid=(N // tn, T // tk),
        in_specs=[pl.BlockSpec((tk, M), lambda j, k: (k, 0)), pl.BlockSpec((tk, tn), lambda j, k: (k, j))],
        out_specs=pl.BlockSpec((M, tn), lambda j, k: (0, j)), out_shape=_sds((M, N), F32),
        compiler_params=_cp(("arbitrary", "arbitrary"), 48),
    )(a, b)


def _merge_bwd(dx1, merged, y_a, y_b, proj_g, w_pa, w_pb, w_out, tm):
    T = dx1.shape[0]

    def body(dx_ref, mg_ref, ya_ref, yb_ref, g_ref, wpa_ref, wpb_ref, wo_ref,
             dg_ref, dya_ref, dyb_ref, gwo_ref, gwpa_ref, gwpb_ref):
        dx = dx_ref[...].astype(BF16)
        dm = _dot_nt(dx, wo_ref[...])
        g = g_ref[...].astype(F32)
        ya = ya_ref[...]
        yb = yb_ref[...]
        pa = _dot(ya, wpa_ref[...])
        pb = _dot(yb, wpb_ref[...])
        sa = _sigmoid(g[:, :D_MODEL])
        sb = _sigmoid(g[:, D_MODEL:])
        dpa = (dm * sa).astype(BF16)
        dpb = (dm * sb).astype(BF16)
        dg_ref[:, :D_MODEL] = (dm * pa * (sa * (1.0 - sa))).astype(BF16)
        dg_ref[:, D_MODEL:] = (dm * pb * (sb * (1.0 - sb))).astype(BF16)
        dya_ref[...] = _dot_nt(dpa, wpa_ref[...]).astype(BF16)
        dyb_ref[...] = _dot_nt(dpb, wpb_ref[...]).astype(BF16)

        @pl.when(pl.program_id(0) == 0)
        def _():
            for r in (gwo_ref, gwpa_ref, gwpb_ref):
                r[...] = jnp.zeros_like(r)

        gwo_ref[...] += _dot_tn(mg_ref[...], dx)
        gwpa_ref[...] += _dot_tn(ya, dpa)
        gwpb_ref[...] += _dot_tn(yb, dpb)

    return pl.pallas_call(
        body, name="merge_bwd", grid=(T // tm,),
        in_specs=[_row(tm, D_MODEL), _row(tm, D_MODEL), _row(tm, A_WIDTH), _row(tm, Q_DIM), _row(tm, G_DIM),
                  _full(w_pa.shape), _full(w_pb.shape), _full(w_out.shape)],
        out_specs=[_row(tm, G_DIM), _row(tm, A_WIDTH), _row(tm, Q_DIM),
                   _full(w_out.shape), _full(w_pa.shape), _full(w_pb.shape)],
        out_shape=[_sds((T, G_DIM), BF16), _sds((T, A_WIDTH), BF16), _sds((T, Q_DIM), BF16),
                   _sds(w_out.shape, F32), _sds(w_pa.shape, F32), _sds(w_pb.shape, F32)],
        compiler_params=_cp(("arbitrary",), 56),
    )(dx1, merged, y_a, y_b, proj_g, w_pa, w_pb, w_out)


def _sgu_bwd(proj_a, d_ya, g_sgu, w_s, b_st, tm):
    T = proj_a.shape[0]

    def body(p_ref, dy_ref, g_ref, ws_ref, bs_ref, dp_ref, gws_ref, gbs_ref, gg_ref):
        tril = _tril()
        g = g_ref[...]
        pu, pv, u, tu, vv, tv, rv, vn = _sgu_parts(p_ref[...].astype(F32), g)
        dy = dy_ref[...].astype(F32)

        @pl.when(pl.program_id(0) == 0)
        def _():
            for r in (gws_ref, gbs_ref, gg_ref):
                r[...] = jnp.zeros_like(r)

        du_cols = []
        dvn_cols = []
        for gi in range(A_GROUPS):
            wm = jnp.where(tril, ws_ref[gi], 0.0).astype(BF16)
            wmt = wm.astype(F32).T.astype(BF16)
            bcol = bs_ref[:, gi:gi + 1]
            cs = slice(gi * CHUNK, (gi + 1) * CHUNK)
            du_rows = []
            dvn_rows = []
            gw = jnp.zeros((CHUNK, CHUNK), F32)
            gb = jnp.zeros((CHUNK, 1), F32)
            for c in range(tm // CHUNK):
                rs = slice(c * CHUNK, (c + 1) * CHUNK)
                vn_c = vn[rs, cs]
                s = _dot(wm, vn_c) + bcol
                dy_c = dy[rs, cs]
                ds = dy_c * u[rs, cs]
                du_rows.append(dy_c * s)
                dsb = ds.astype(BF16)
                gw = gw + _dot_nt(dsb, vn_c)
                gb = gb + jnp.sum(ds, axis=-1, keepdims=True)
                dvn_rows.append(_dot(wmt, dsb))
            gws_ref[gi] += jnp.where(tril, gw, 0.0)
            gbs_ref[:, gi:gi + 1] += gb
            du_cols.append(jnp.concatenate(du_rows, axis=0))
            dvn_cols.append(jnp.concatenate(dvn_rows, axis=0))
        du = jnp.concatenate(du_cols, axis=1)
        dvn = jnp.concatenate(dvn_cols, axis=1)
        vhat = vv * rv
        gg_ref[...] += jnp.sum(dvn * vhat, axis=0, keepdims=True)
        dvv = _rms_bwd(dvn, vhat, rv, g)
        dp_ref[:, :A_WIDTH] = (du * _gelu_grad(pu, tu)).astype(BF16)
        dp_ref[:, A_WIDTH:] = (dvv * _gelu_grad(pv, tv)).astype(BF16)

    return pl.pallas_call(
        body, name="sgu_bwd", grid=(T // tm,),
        in_specs=[_row(tm, A_DIM), _row(tm, A_WIDTH), _full(g_sgu.shape), _full(w_s.shape), _full(b_st.shape)],
        out_specs=[_row(tm, A_DIM), _full(w_s.shape), _full(b_st.shape), _full(g_sgu.shape)],
        out_shape=[_sds((T, A_DIM), BF16), _sds(w_s.shape, F32), _sds(b_st.shape, F32), _sds(g_sgu.shape, F32)],
        compiler_params=_cp(("arbitrary",)),
    )(proj_a, d_ya, g_sgu, w_s, b_st)


def _attn_bwd(proj_b, d_yb, sinks, rel_bias, n_seq, seq):
    nb = seq // CHUNK
    bk = jnp.asarray(_band_buckets())

    def body(qkv_ref, do_ref, bk_ref, rel_ref, sink_ref, d_ref, gs_ref, gr_ref, bias_scr, dbias_scr, dk_scr, dv_scr, ds_scr):
        b = pl.program_id(0)
        _build_bias(bias_scr, bk_ref, rel_ref)
        col = lax.broadcasted_iota(jnp.int32, (CHUNK, 2 * CHUNK), 1)
        lane = lax.broadcasted_iota(jnp.int32, (2 * CHUNK, LANES), 1)
        valid = bk_ref[...] >= 0

        @pl.when(b == 0)
        def _():
            dbias_scr[...] = jnp.zeros_like(dbias_scr)
            ds_scr[...] = jnp.zeros_like(ds_scr)

        dk_scr[...] = jnp.zeros_like(dk_scr)
        dv_scr[...] = jnp.zeros_like(dv_scr)

        def to_kv_lanes(a, hh, kvh):
            a = jnp.where((lane >= HEAD_DIM) if hh == 1 else (lane < HEAD_DIM), a, 0.0)
            return a if hh == kvh else pltpu.roll(a, HEAD_DIM, 1)

        def blk(n, carry):
            r0, kv, vv = _attn_block_inputs(qkv_ref, n)
            ok = valid & ((col >= CHUNK) | (n > 0))
            dkw = jnp.zeros((2 * CHUNK, KV_DIM), F32)
            dvw = jnp.zeros((2 * CHUNK, KV_DIM), F32)
            for pr in range(N_HEADS // 2):
                ps = slice(pr * LANES, (pr + 1) * LANES)
                qp = qkv_ref[pl.ds(r0, CHUNK), ps]
                dop = do_ref[pl.ds(r0, CHUNK), ps]
                kvh = pr // 2
                dq = jnp.zeros((CHUNK, LANES), F32)
                for hh in range(2):
                    h = 2 * pr + hh
                    prob, psink = _attn_probs(qp, kv[kvh][hh], bias_scr[h], sink_ref[0, h], ok)
                    dp = _dot_nt(dop, vv[kvh][hh])
                    delta = jnp.sum(prob * dp, axis=-1, keepdims=True)
                    dsc = prob * (dp - delta)
                    ds_scr[h] += psink * delta
                    dbias_scr[h] += dsc
                    dsb = (dsc * (HEAD_DIM ** -0.5)).astype(BF16)
                    dq = dq + _dot(dsb, kv[kvh][hh])
                    dkw = dkw + to_kv_lanes(_dot_tn(dsb, qp), hh, kvh)
                    dvw = dvw + to_kv_lanes(_dot_tn(prob.astype(BF16), dop), hh, kvh)
                d_ref[pl.ds(r0, CHUNK), ps] = dq.astype(BF16)
            dk_scr[pl.ds(r0, 2 * CHUNK), :] += dkw
            dv_scr[pl.ds(r0, 2 * CHUNK), :] += dvw
            return carry

        lax.fori_loop(0, nb, blk, 0)
        d_ref[:, Q_DIM:Q_DIM + KV_DIM] = dk_scr[CHUNK:, :].astype(BF16)
        d_ref[:, Q_DIM + KV_DIM:] = dv_scr[CHUNK:, :].astype(BF16)

        @pl.when(b == n_seq - 1)
        def _():
            bkv = bk_ref[...]
            for h in range(N_HEADS):
                gs_ref[0:1, h:h + 1] = -jnp.sum(ds_scr[h], axis=0, keepdims=True)
                db = dbias_scr[h]
                for bb in range(N_BUCKETS):
                    part = jnp.sum(jnp.where(bkv == bb, db, 0.0), axis=-1, keepdims=True)
                    gr_ref[bb:bb + 1, h:h + 1] = jnp.sum(part, axis=0, keepdims=True)

    smem = pl.BlockSpec(memory_space=pltpu.SMEM)
    return pl.pallas_call(
        body, name="attn_bwd", grid=(n_seq,),
        in_specs=[_row(seq, B_DIM), _row(seq, Q_DIM), _full(bk.shape), smem, smem],
        out_specs=[_row(seq, B_DIM), _full((1, N_HEADS)), _full((N_BUCKETS, N_HEADS))],
        out_shape=[_sds((n_seq * seq, B_DIM), BF16), _sds((1, N_HEADS), F32), _sds((N_BUCKETS, N_HEADS), F32)],
        scratch_shapes=[pltpu.VMEM((N_HEADS, CHUNK, 2 * CHUNK), F32), pltpu.VMEM((N_HEADS, CHUNK, 2 * CHUNK), F32),
                        pltpu.VMEM((seq + CHUNK, KV_DIM), F32), pltpu.VMEM((seq + CHUNK, KV_DIM), F32),
                        pltpu.VMEM((N_HEADS, CHUNK, 1), F32)],
        compiler_params=_cp(("arbitrary",), 40),
    )(proj_b, d_yb, bk, rel_bias, sinks)


def _inproj_bwd(d_g, d_a, d_b, x2, dx1, g_mix, w_g, w_a, w_b, tm):
    T = x2.shape[0]

    def body(dg_ref, da_ref, db_ref, x_ref, dx1_ref, g_ref, wg_ref, wa_ref, wb_ref, gx_ref, gg_ref):
        dh = _dot_nt(dg_ref[...], wg_ref[...]) + _dot_nt(da_ref[...], wa_ref[...]) + _dot_nt(db_ref[...], wb_ref[...])
        x = x_ref[...]
        r = _rms_r(x)
        n = x * r
        gx_ref[...] = dx1_ref[...] + _rms_bwd(dh, n, r, g_ref[...])

        @pl.when(pl.program_id(0) == 0)
        def _():
            gg_ref[...] = jnp.zeros_like(gg_ref)

        gg_ref[...] += jnp.sum(dh * n, axis=0, keepdims=True)

    return pl.pallas_call(
        body, name="inproj_bwd", grid=(T // tm,),
        in_specs=[_row(tm, G_DIM), _row(tm, A_DIM), _row(tm, B_DIM), _row(tm, D_MODEL), _row(tm, D_MODEL),
                  _full(g_mix.shape), _full(w_g.shape), _full(w_a.shape), _full(w_b.shape)],
        out_specs=[_row(tm, D_MODEL), _full((1, D_MODEL))],
        out_shape=[_sds((T, D_MODEL), F32), _sds((1, D_MODEL), F32)],
        compiler_params=_cp(("arbitrary",), 48),
    )(d_g, d_a, d_b, x2, dx1, g_mix, w_g, w_a, w_b)


def _local_step(x, target, g_mix, g_sgu, w_s, b_s, sinks, rel_bias, g_ffn, b_conv, g_final,
                w_g, w_a, w_b, w_pa, w_pb, w_out, w_up, w_conv, w_down):
    n_seq, seq, _ = x.shape
    T = n_seq * seq
    tm = min(256, seq)
    x2 = x.reshape(T, D_MODEL)
    tgt = target.reshape(T, D_MODEL)
    b_st = b_s.T
    g_fin = g_final.reshape(1, D_MODEL)

    proj_g, proj_a, proj_b, h = _inproj(x2, g_mix, w_g, w_a, w_b, tm)
    y_a = _sgu_fwd(proj_a, g_sgu, w_s, b_st, tm)
    y_b = _attn_fwd(proj_b, sinks, rel_bias, n_seq, seq)
    x1, merged = _merge_fwd(x2, y_a, y_b, proj_g, w_pa, w_pb, w_out, tm)
    upre, h2 = _upproj(x1, g_ffn, w_up, tm)
    dx2, loss, gg_final = _ffn_down_loss(upre, x1, tgt, w_conv, b_conv, w_down, g_fin, tm, seq)

    d_gate, d_val, gw_down, gb_g, gb_v, gwc_g, gwc_v = _ffn_bwd_act(upre, dx2, w_conv, b_conv, w_down, tm, seq)
    gb_conv = jnp.concatenate([gb_g, gb_v], axis=1)
    gw_conv = jnp.concatenate([gwc_g, gwc_v], axis=1)
    d_upre, dx1, gg_ffn = _ffn_bwd_up(d_gate, d_val, dx2, x1, g_ffn, w_conv, w_up, tm, seq)
    gw_up = _matmul_tn(h2, d_upre, 2 * D_FF // 4, min(512, T), "grad_w_up")
    d_g, d_ya, d_yb, gw_out, gw_pa, gw_pb = _merge_bwd(dx1, merged, y_a, y_b, proj_g, w_pa, w_pb, w_out, tm)
    d_a, gw_s, gb_st, gg_sgu = _sgu_bwd(proj_a, d_ya, g_sgu, w_s, b_st, tm)
    d_b, g_sinks, g_rel = _attn_bwd(proj_b, d_yb, sinks, rel_bias, n_seq, seq)
    grad_x, gg_mix = _inproj_bwd(d_g, d_a, d_b, x2, dx1, g_mix, w_g, w_a, w_b, tm)
    gw_g = _matmul_tn(h, d_g, D_MODEL, min(512, T), "grad_w_in_gate")
    gw_a = _matmul_tn(h, d_a, A_DIM, min(512, T), "grad_w_in_a")
    gw_b = _matmul_tn(h, d_b, B_DIM, min(512, T), "grad_w_in_b")
    gw_in = jnp.concatenate([gw_a, gw_b, gw_g], axis=1)

    small = dict(g_mix=gg_mix, g_sgu=gg_sgu, w_s=gw_s, b_s=gb_st.T, sinks=g_sinks, rel_bias=g_rel,
                 g_ffn=gg_ffn, b_conv=gb_conv, g_final=gg_final, w_conv=gw_conv)
    big = dict(w_in=gw_in, w_pa=gw_pa, w_pb=gw_pb, w_out=gw_out, w_up=gw_up, w_down=gw_down)
    return loss, grad_x.reshape(x.shape), small, big


_PACK_ROWS = (("w_in", 960), ("w_pa", 128), ("w_pb", 128), ("w_out", 256), ("w_up", 1408), ("w_down", 704))
PACK_ROWS = sum(r for _, r in _PACK_ROWS)
HALF_ROWS = PACK_ROWS // 2
_COL_SHARDED = ("w_in", "w_pa", "w_pb", "w_up")

_SMALL = (("loss", (1, 1)), ("g_final", (1, D_MODEL)), ("g_mix", (1, D_MODEL)), ("g_ffn", (1, D_MODEL)),
          ("g_sgu", (1, A_WIDTH)), ("b_s", (A_GROUPS, CHUNK)), ("sinks", (1, N_HEADS)), ("rel_bias", (N_BUCKETS, N_HEADS)),
          ("b_conv", (1, 2 * D_FF)), ("w_conv", (3, 2 * D_FF)), ("w_s", (A_GROUPS, CHUNK, CHUNK)))
SMALL_ROWS = 96


def _pack_shard(parts, dtype):
    return jnp.concatenate([parts[n].astype(dtype).reshape(r, D_MODEL) for n, r in _PACK_ROWS], axis=0)


def _unpack_shard(buf, shapes):
    out = {}
    off = 0
    for n, r in _PACK_ROWS:
        out[n] = buf[off:off + r].reshape(shapes[n])
        off += r
    return out


def _shard_of(full, name, i):
    if name in _COL_SHARDED:
        w = full.shape[1] // N_CHIPS
        return full[:, i * w:(i + 1) * w]
    h = full.shape[0] // N_CHIPS
    return full[i * h:(i + 1) * h]


def _unpack_full(gathered, shard_shapes):
    per_chip = [_unpack_shard(gathered[i], shard_shapes) for i in range(N_CHIPS)]
    return {n: jnp.concatenate([per_chip[i][n] for i in range(N_CHIPS)], axis=1 if n in _COL_SHARDED else 0)
            for n, _ in _PACK_ROWS}


def _pack_small(vals):
    flat = jnp.concatenate([vals[n].astype(F32).reshape(-1) for n, _ in _SMALL])
    flat = jnp.pad(flat, (0, SMALL_ROWS * D_MODEL - flat.shape[0]))
    return flat.reshape(SMALL_ROWS, D_MODEL)


def _unpack_small(buf):
    flat = buf.reshape(-1)
    out = {}
    off = 0
    for n, shp in _SMALL:
        k = int(np.prod(shp))
        out[n] = flat[off:off + k].reshape(shp)
        off += k
    return out


HBM = pl.BlockSpec(memory_space=pltpu.HBM)


def _mesh_pos():
    return lax.axis_index("x"), lax.axis_index("y"), lax.axis_index("c")


def _other_chips(x, y):
    return [(1 - x, y), (x, 1 - y), (1 - x, 1 - y)]


def _remote(src, dst, send_sem, recv_sem, to):
    return pltpu.make_async_remote_copy(src_ref=src, dst_ref=dst, send_sem=send_sem, recv_sem=recv_sem,
                                        device_id=to, device_id_type=MESH)


def _own_slot(own, n, at):
    return lax.dynamic_update_slice(lax.empty((n,) + own.shape, own.dtype), own[None], (at,) + (0,) * own.ndim)


def _allgather_weights(packed, w_conv_pad, me):
    def body(w_ref, c_ref, out_ref, oc_ref, send_sems, recv_sems, csend_sems, crecv_sems):
        x, y, c = _mesh_pos()
        me = 2 * x + y
        sibling = (x, y, 1 - c)
        chips = _other_chips(x, y)

        def half(chip, hc):
            return out_ref.at[chip, pl.ds(hc * HALF_ROWS, HALF_ROWS), :]

        first = [_remote(w_ref.at[me, pl.ds(c * HALF_ROWS, HALF_ROWS), :], half(me, c), send_sems.at[j], recv_sems.at[j], (cx, cy, c))
                 for j, (cx, cy) in enumerate(chips)]
        first_c = [_remote(c_ref.at[me], oc_ref.at[me], csend_sems.at[j], crecv_sems.at[j], (cx, cy, c))
                   for j, (cx, cy) in enumerate(chips)]
        for cp in first + first_c:
            cp.start()
        passed = [_remote(half(2 * cx + cy, c), half(2 * cx + cy, c), send_sems.at[3 + j], recv_sems.at[3 + j], sibling)
                  for j, (cx, cy) in enumerate(chips)]
        for j, (cx, cy) in enumerate(chips):
            _remote(half(2 * cx + cy, c), half(2 * cx + cy, c), send_sems.at[j], recv_sems.at[j], (x, y, c)).wait_recv()
            passed[j].start()
        for j, (cx, cy) in enumerate(chips):
            _remote(half(2 * cx + cy, 1 - c), half(2 * cx + cy, 1 - c), send_sems.at[3 + j], recv_sems.at[3 + j], (x, y, c)).wait_recv()
            _remote(c_ref.at[me], oc_ref.at[2 * cx + cy], csend_sems.at[j], crecv_sems.at[j], (x, y, c)).wait_recv()
        for cp in first + first_c + passed:
            cp.wait_send()

    return pl.pallas_call(
        body, name="allgather_weights",
        in_specs=[HBM, HBM], out_specs=[HBM, HBM], input_output_aliases={0: 0, 1: 1},
        out_shape=[_sds((N_CHIPS,) + packed.shape, packed.dtype), _sds((N_CHIPS,) + w_conv_pad.shape, w_conv_pad.dtype)],
        scratch_shapes=[pltpu.SemaphoreType.DMA((6,)), pltpu.SemaphoreType.DMA((6,)),
                        pltpu.SemaphoreType.DMA((3,)), pltpu.SemaphoreType.DMA((3,))],
    )(_own_slot(packed, N_CHIPS, me), _own_slot(w_conv_pad, N_CHIPS, me))


def _pair_exchange(parts):
    def body(p_ref, q_ref, send_sems, recv_sems):
        x, y, c = _mesh_pos()
        copies = [_remote(p_ref.at[i, 1 - c], q_ref.at[i], send_sems.at[i], recv_sems.at[i], (x, y, 1 - c))
                  for i in range(N_CHIPS)]
        for cp in copies:
            cp.start()
        for cp in copies:
            cp.wait()

    return pl.pallas_call(
        body, name="grad_pair_exchange", in_specs=[HBM], out_specs=HBM,
        out_shape=_sds((N_CHIPS, HALF_ROWS, D_MODEL), parts.dtype),
        scratch_shapes=[pltpu.SemaphoreType.DMA((N_CHIPS,)), pltpu.SemaphoreType.DMA((N_CHIPS,))],
    )(parts)


def _pair_add(parts, from_sibling, c):
    tr = 448

    def body(c_ref, p_ref, q_ref, o_ref):
        o_ref[...] = (p_ref[...].astype(F32)[:, 0] + q_ref[...].astype(F32)).astype(BF16)

    return pl.pallas_call(
        body, name="grad_pair_add",
        grid_spec=pltpu.PrefetchScalarGridSpec(
            num_scalar_prefetch=1, grid=(N_CHIPS, HALF_ROWS // tr),
            in_specs=[pl.BlockSpec((1, 1, tr, D_MODEL), lambda i, r, cr: (i, cr[0], r, 0)),
                      pl.BlockSpec((1, tr, D_MODEL), lambda i, r, cr: (i, r, 0))],
            out_specs=pl.BlockSpec((1, tr, D_MODEL), lambda i, r, cr: (i, r, 0))),
        out_shape=_sds((N_CHIPS, HALF_ROWS, D_MODEL), BF16),
        compiler_params=_cp(("arbitrary", "arbitrary")),
    )(c.reshape(1), parts, from_sibling)


def _chip_exchange(sums):
    def body(s_ref, r_ref, send_sems, recv_sems):
        x, y, c = _mesh_pos()
        me = 2 * x + y
        copies = [_remote(s_ref.at[2 * cx + cy], r_ref.at[j], send_sems.at[j], recv_sems.at[j], (cx, cy, c))
                  for j, (cx, cy) in enumerate(_other_chips(x, y))]
        for cp in copies:
            cp.start()
        for cp in copies:
            cp.wait()

    return pl.pallas_call(
        body, name="grad_chip_exchange", in_specs=[HBM], out_specs=HBM,
        out_shape=_sds((3, HALF_ROWS, D_MODEL), sums.dtype),
        scratch_shapes=[pltpu.SemaphoreType.DMA((3,)), pltpu.SemaphoreType.DMA((3,))],
    )(sums)


def _owner_sum(parts, from_sibling, from_chips, me, c):
    tr = 448

    def body(s_ref, p_ref, q_ref, r_ref, o_ref):
        acc = p_ref[0, 0].astype(F32) + q_ref[0].astype(F32)
        for j in range(3):
            acc = acc + r_ref[j].astype(F32)
        o_ref[0] = acc

    return pl.pallas_call(
        body, name="grad_owner_sum",
        grid_spec=pltpu.PrefetchScalarGridSpec(
            num_scalar_prefetch=1, grid=(HALF_ROWS // tr,),
            in_specs=[pl.BlockSpec((1, 1, tr, D_MODEL), lambda r, s: (s[0], s[1], r, 0)),
                      pl.BlockSpec((1, tr, D_MODEL), lambda r, s: (s[0], r, 0)),
                      pl.BlockSpec((3, tr, D_MODEL), lambda r, s: (0, r, 0))],
            out_specs=pl.BlockSpec((1, tr, D_MODEL), lambda r, s: (s[1], r, 0))),
        out_shape=_sds((2, HALF_ROWS, D_MODEL), F32),
        compiler_params=_cp(("arbitrary",)),
    )(jnp.stack([me, c]), parts, from_sibling, from_chips)


def _pair_share(halves):
    def body(h_ref, o_ref, send_sem, recv_sem):
        x, y, c = _mesh_pos()
        cp = _remote(h_ref.at[c], o_ref.at[c], send_sem, recv_sem, (x, y, 1 - c))
        cp.start()
        _remote(h_ref.at[c], o_ref.at[1 - c], send_sem, recv_sem, (x, y, c)).wait_recv()
        cp.wait_send()

    return pl.pallas_call(
        body, name="grad_pair_share", in_specs=[HBM], out_specs=HBM, input_output_aliases={0: 0},
        out_shape=_sds(halves.shape, halves.dtype),
        scratch_shapes=[pltpu.SemaphoreType.DMA, pltpu.SemaphoreType.DMA],
    )(halves)


def _allgather_small(block):
    m_per = block.shape[0]

    def body(x_ref, out_ref, send_sems, recv_sems, local_sem):
        x, y, c = _mesh_pos()
        me, sibling = (x, y, c), (x, y, 1 - c)
        chips = _other_chips(x, y)

        def rows(px, py, pc):
            return out_ref.at[4 * px + 2 * py + pc]

        def copy(k, block_of, to, src=None):
            return _remote(rows(*block_of) if src is None else src, rows(*block_of), send_sems.at[k], recv_sems.at[k], to)

        mine = pltpu.make_async_copy(x_ref, rows(*me), local_sem)
        mine.start()
        first = [copy(0, me, sibling, src=x_ref)]
        first += [copy(1 + j, me, (*chip, c), src=x_ref) for j, chip in enumerate(chips)]
        for cp in first:
            cp.start()
        passed = [copy(4 + j, (*chip, c), sibling) for j, chip in enumerate(chips)]
        for j, chip in enumerate(chips):
            copy(1 + j, (*chip, c), me).wait_recv()
            passed[j].start()
        copy(0, sibling, me).wait_recv()
        for j, chip in enumerate(chips):
            copy(4 + j, (*chip, 1 - c), me).wait_recv()
        for cp in first + passed:
            cp.wait_send()
        mine.wait()

    return pl.pallas_call(
        body, name="allgather_small",
        in_specs=[pl.BlockSpec(memory_space=pltpu.VMEM)], out_specs=pl.BlockSpec(memory_space=pltpu.VMEM),
        out_shape=_sds((N_DEV, m_per, D_MODEL), block.dtype),
        scratch_shapes=[pltpu.SemaphoreType.DMA((7,)), pltpu.SemaphoreType.DMA((7,)), pltpu.SemaphoreType.DMA],
    )(block)


def _adam_math(w, g, m, v):
    m = ADAM_B1 * m + (1.0 - ADAM_B1) * g
    v = ADAM_B2 * v + (1.0 - ADAM_B2) * (g * g)
    m_hat = m / (1.0 - ADAM_B1 ** ADAM_STEP)
    v_hat = v / (1.0 - ADAM_B2 ** ADAM_STEP)
    delta = -ADAM_LR * (m_hat / (jnp.sqrt(v_hat) + ADAM_EPS) + ADAM_WD * w)
    return delta, m, v


def _adamw(w, g, m, v, name):
    rows, cols = w.shape
    tr = rows
    for cand in (256, 128, 64, 32, 16, 8):
        if rows % cand == 0 and rows > cand:
            tr = cand
            break

    def body(w_ref, g_ref, m_ref, v_ref, d_ref, nm_ref, nv_ref):
        d, nm, nv = _adam_math(w_ref[...], g_ref[...], m_ref[...], v_ref[...])
        d_ref[...] = d
        nm_ref[...] = nm
        nv_ref[...] = nv

    spec = pl.BlockSpec((tr, cols), lambda i: (i, 0))
    return pl.pallas_call(
        body, name=name, grid=(rows // tr,), in_specs=[spec] * 4, out_specs=[spec] * 3,
        out_shape=[_sds(w.shape, F32)] * 3, compiler_params=_cp(("arbitrary",)),
    )(w, g, m, v)


def _small_sum_adamw(gathered, w, m, v):
    def body(a_ref, w_ref, m_ref, v_ref, g_ref, d_ref, nm_ref, nv_ref):
        g = a_ref[0]
        for k in range(1, N_DEV):
            g = g + a_ref[k]
        g_ref[...] = g
        d, nm, nv = _adam_math(w_ref[...], g, m_ref[...], v_ref[...])
        d_ref[...] = d
        nm_ref[...] = nm
        nv_ref[...] = nv

    return pl.pallas_call(
        body, name="small_sum_adamw", out_shape=[_sds(w.shape, F32)] * 4,
    )(gathered, w, m, v)


_NAMES = ("g_mix", "w_in", "g_sgu", "w_s", "b_s", "sinks", "rel_bias", "w_pa", "w_pb", "w_out",
          "g_ffn", "w_up", "w_conv", "b_conv", "w_down", "g_final")
_BIG = tuple(n for n, _ in _PACK_ROWS)


def kernel(x, g_mix, w_in, g_sgu, w_s, b_s, sinks, rel_bias, w_pa, w_pb, w_out, g_ffn, w_up, w_conv, b_conv, w_down, g_final, loss_target, m_g_mix, m_w_in, m_g_sgu, m_w_s, m_b_s, m_sinks, m_rel_bias, m_w_pa, m_w_pb, m_w_out, m_g_ffn, m_w_up, m_w_conv, m_b_conv, m_w_down, m_g_final, v_g_mix, v_w_in, v_g_sgu, v_w_s, v_b_s, v_sinks, v_rel_bias, v_w_pa, v_w_pb, v_w_out, v_g_ffn, v_w_up, v_w_conv, v_b_conv, v_w_down, v_g_final):
    w = dict(g_mix=g_mix, w_in=w_in, g_sgu=g_sgu, w_s=w_s, b_s=b_s, sinks=sinks, rel_bias=rel_bias, w_pa=w_pa, w_pb=w_pb,
             w_out=w_out, g_ffn=g_ffn, w_up=w_up, w_conv=w_conv, b_conv=b_conv, w_down=w_down, g_final=g_final)
    m = dict(g_mix=m_g_mix, w_in=m_w_in, g_sgu=m_g_sgu, w_s=m_w_s, b_s=m_b_s, sinks=m_sinks, rel_bias=m_rel_bias, w_pa=m_w_pa,
             w_pb=m_w_pb, w_out=m_w_out, g_ffn=m_g_ffn, w_up=m_w_up, w_conv=m_w_conv, b_conv=m_b_conv, w_down=m_w_down,
             g_final=m_g_final)
    v = dict(g_mix=v_g_mix, w_in=v_w_in, g_sgu=v_g_sgu, w_s=v_w_s, b_s=v_b_s, sinks=v_sinks, rel_bias=v_rel_bias, w_pa=v_w_pa,
             w_pb=v_w_pb, w_out=v_w_out, g_ffn=v_g_ffn, w_up=v_w_up, w_conv=v_w_conv, b_conv=v_b_conv, w_down=v_w_down,
             g_final=v_g_final)
    xi, yi, ci = _mesh_pos()
    me = 2 * xi + yi

    shard = {n: w[n][0] for n in _BIG}
    shard_shapes = {n: shard[n].shape for n in _BIG}
    wc_shard = w["w_conv"][0]
    wc_pad = jnp.pad(wc_shard, ((0, 5), (0, 0)))
    gathered, wc_all = _allgather_weights(_pack_shard(shard, BF16), wc_pad, me)
    full = _unpack_full(gathered, shard_shapes)
    w_conv_full = jnp.concatenate([wc_all[i, :3] for i in range(N_CHIPS)], axis=1)
    w_in_full = full["w_in"]
    w_a = w_in_full[:, :A_DIM]
    w_b = w_in_full[:, A_DIM:A_DIM + B_DIM]
    w_g = w_in_full[:, A_DIM + B_DIM:]

    loss, grad_x, small, big = _local_step(
        x, loss_target, w["g_mix"], w["g_sgu"], w["w_s"][0], w["b_s"][0], w["sinks"], w["rel_bias"], w["g_ffn"],
        w["b_conv"], w["g_final"], w_g, w_a, w_b, full["w_pa"], full["w_pb"], full["w_out"], full["w_up"],
        w_conv_full, full["w_down"])

    small["loss"] = loss
    all_small = _allgather_small(_pack_small(small))
    sw = {n: (jnp.zeros((1, 1), F32) if n in ("loss", "w_conv") else w[n]) for n, _ in _SMALL}
    sm = {n: (jnp.zeros((1, 1), F32) if n in ("loss", "w_conv") else m[n]) for n, _ in _SMALL}
    sv = {n: (jnp.zeros((1, 1), F32) if n in ("loss", "w_conv") else v[n]) for n, _ in _SMALL}
    for d in (sw, sm, sv):
        d["w_conv"] = jnp.zeros((3, 2 * D_FF), F32)
    s_g, s_d, s_m, s_v = [_unpack_small(a) for a in _small_sum_adamw(all_small, _pack_small(sw), _pack_small(sm), _pack_small(sv))]

    parts = jnp.stack([_pack_shard({n: _shard_of(big[n], n, i) for n in _BIG}, BF16) for i in range(N_CHIPS)])
    parts = parts.reshape(N_CHIPS, 2, HALF_ROWS, D_MODEL)
    from_sibling = _pair_exchange(parts)
    pair_sums = _pair_add(parts, from_sibling, ci)
    from_chips = _chip_exchange(pair_sums)
    mine_half = _owner_sum(parts, from_sibling, from_chips, me, ci)
    g_big = _unpack_shard(_pair_share(mine_half).reshape(PACK_ROWS, D_MODEL), shard_shapes)

    grads, deltas, new_m, new_v = {}, {}, {}, {}
    for n in _BIG:
        d, nm, nv = _adamw(shard[n], g_big[n], m[n][0], v[n][0], "adamw_" + n)
        grads[n], deltas[n], new_m[n], new_v[n] = g_big[n][None], d[None], nm[None], nv[None]
    wcols = wc_shard.shape[1]
    g_wc = lax.dynamic_slice(s_g["w_conv"], (0, me * wcols), (3, wcols))
    d, nm, nv = _adamw(wc_shard, g_wc, m["w_conv"][0], v["w_conv"][0], "adamw_w_conv")
    grads["w_conv"], deltas["w_conv"], new_m["w_conv"], new_v["w_conv"] = g_wc[None], d[None], nm[None], nv[None]
    for n, _ in _SMALL:
        if n in ("loss", "w_conv"):
            continue
        shp = w[n].shape
        grads[n], deltas[n], new_m[n], new_v[n] = (s_g[n].reshape(shp), s_d[n].reshape(shp), s_m[n].reshape(shp),
                                                    s_v[n].reshape(shp))

    return (s_g["loss"].reshape(()), grad_x, *[grads[n] for n in _NAMES], *[deltas[n] for n in _NAMES],
            *[new_m[n] for n in _NAMES], *[new_v[n] for n in _NAMES])
```

```python
import functools

import numpy as np
import jax
import jax.numpy as jnp
from jax import lax
from jax.experimental import pallas as pl
from jax.experimental.pallas import tpu as pltpu

F32 = jnp.float32
BF16 = jnp.bfloat16

D_MODEL = 1024
CHUNK = 128
A_GROUPS = 4
A_WIDTH = 512
N_HEADS = 8
HEAD_DIM = 64
Q_DIM = 512
KV_DIM = 128
N_BUCKETS = 32
MAX_DISTANCE = 128
D_FF = 2816
EPS = 1e-6
NEG_INF = -1e30
G_DIM = 2 * D_MODEL
A_DIM = 2 * A_WIDTH
B_DIM = Q_DIM + 2 * KV_DIM
LANES = 128
BF16_ROWS = 16
N_CHIPS = 4
N_DEV = 8

ADAM_LR = 0.001
ADAM_B1 = 0.9
ADAM_B2 = 0.999
ADAM_EPS = 1e-08
ADAM_WD = 0.01
ADAM_STEP = 10

MESH = pl.DeviceIdType.MESH
_GELU_C = 0.7978845608028654
_GELU_A = 0.044715


def _cp(sem=None, vmem_mb=None):
    kw = {}
    if sem is not None:
        kw["dimension_semantics"] = sem
    if vmem_mb is not None:
        kw["vmem_limit_bytes"] = vmem_mb << 20
    return pltpu.CompilerParams(**kw)


def _dot(a, b):
    return jnp.dot(a, b, preferred_element_type=F32)


def _dot_nt(a, b):
    return lax.dot_general(a, b, (((1,), (1,)), ((), ())), preferred_element_type=F32)


def _dot_tn(a, b):
    return lax.dot_general(a, b, (((0,), (0,)), ((), ())), preferred_element_type=F32)


def _rms_r(x):
    return lax.rsqrt(jnp.mean(x * x, axis=-1, keepdims=True) + EPS)


def _rms_bwd(dh, n, r, g):
    dn = dh * g
    return r * (dn - n * jnp.mean(dn * n, axis=-1, keepdims=True))


def _gelu(x):
    t = jnp.tanh(_GELU_C * (x + _GELU_A * (x * x * x)))
    return 0.5 * x * (1.0 + t), t


def _gelu_grad(x, t):
    return 0.5 * (1.0 + t) + 0.5 * x * (1.0 - t * t) * (_GELU_C * (1.0 + 3.0 * _GELU_A * x * x))


def _sigmoid(x):
    return 1.0 / (1.0 + jnp.exp(-x))


def _row(tm, w):
    return pl.BlockSpec((tm, w), lambda i: (i, 0))


def _full(shape):
    nd = len(shape)
    return pl.BlockSpec(tuple(shape), lambda *_: (0,) * nd)


def _sds(shape, dtype):
    return jax.ShapeDtypeStruct(tuple(shape), dtype)


def _band_buckets():
    i = np.arange(CHUNK)[:, None]
    j = np.arange(2 * CHUNK)[None, :]
    dist = i + CHUNK - j
    valid = (dist >= 0) & (dist < CHUNK)
    d = np.clip(dist, 0, None)
    max_exact = N_BUCKETS // 2
    large = max_exact + (np.log(np.maximum(d, 1) / max_exact) / np.log(MAX_DISTANCE / max_exact)
                         * (N_BUCKETS - max_exact)).astype(np.int32)
    large = np.minimum(large, N_BUCKETS - 1)
    buckets = np.where(d < max_exact, d, large).astype(np.int32)
    return np.where(valid, buckets, -1).astype(np.int32)


def _inproj(x2, g_mix, w_g, w_a, w_b, tm):
    T = x2.shape[0]

    def body(x_ref, g_ref, wg_ref, wa_ref, wb_ref, pg_ref, pa_ref, pb_ref, h_ref):
        x = x_ref[...]
        h = (x * _rms_r(x) * g_ref[...]).astype(BF16)
        h_ref[...] = h
        pg_ref[...] = _dot(h, wg_ref[...]).astype(BF16)
        pa_ref[...] = _dot(h, wa_ref[...]).astype(BF16)
        pb_ref[...] = _dot(h, wb_ref[...]).astype(BF16)

    return pl.pallas_call(
        body, name="inproj", grid=(T // tm,),
        in_specs=[_row(tm, D_MODEL), _full(g_mix.shape), _full(w_g.shape), _full(w_a.shape), _full(w_b.shape)],
        out_specs=[_row(tm, G_DIM), _row(tm, A_DIM), _row(tm, B_DIM), _row(tm, D_MODEL)],
        out_shape=[_sds((T, G_DIM), BF16), _sds((T, A_DIM), BF16), _sds((T, B_DIM), BF16), _sds((T, D_MODEL), BF16)],
        compiler_params=_cp(("arbitrary",), 48),
    )(x2, g_mix, w_g, w_a, w_b)


def _sgu_parts(p, g):
    pu = p[:, :A_WIDTH]
    pv = p[:, A_WIDTH:]
    u, tu = _gelu(pu)
    vv, tv = _gelu(pv)
    rv = _rms_r(vv)
    vn = (vv * rv * g).astype(BF16)
    return pu, pv, u, tu, vv, tv, rv, vn


def _tril():
    r = lax.broadcasted_iota(jnp.int32, (CHUNK, CHUNK), 0)
    c = lax.broadcasted_iota(jnp.int32, (CHUNK, CHUNK), 1)
    return r >= c


def _sgu_fwd(proj_a, g_sgu, w_s, b_st, tm):
    T = proj_a.shape[0]

    def body(p_ref, g_ref, ws_ref, bs_ref, y_ref):
        tril = _tril()
        _, _, u, _, _, _, _, vn = _sgu_parts(p_ref[...].astype(F32), g_ref[...])
        for gi in range(A_GROUPS):
            wm = jnp.where(tril, ws_ref[gi], 0.0).astype(BF16)
            bcol = bs_ref[:, gi:gi + 1]
            cs = slice(gi * CHUNK, (gi + 1) * CHUNK)
            for c in range(tm // CHUNK):
                rs = slice(c * CHUNK, (c + 1) * CHUNK)
                s = _dot(wm, vn[rs, cs]) + bcol
                y_ref[rs, cs] = (u[rs, cs] * s).astype(BF16)

    return pl.pallas_call(
        body, name="sgu_fwd", grid=(T // tm,),
        in_specs=[_row(tm, A_DIM), _full(g_sgu.shape), _full(w_s.shape), _full(b_st.shape)],
        out_specs=_row(tm, A_WIDTH), out_shape=_sds((T, A_WIDTH), BF16),
        compiler_params=_cp(("arbitrary",)),
    )(proj_a, g_sgu, w_s, b_st)


def _kv_variants(a):
    a = a.astype(F32)
    lane = lax.broadcasted_iota(jnp.int32, a.shape, 1)
    lo = jnp.where(lane < HEAD_DIM, a, 0.0)
    hi = jnp.where(lane >= HEAD_DIM, a, 0.0)
    lo_r = pltpu.roll(lo, HEAD_DIM, 1)
    hi_r = pltpu.roll(hi, HEAD_DIM, 1)
    return ((lo.astype(BF16), lo_r.astype(BF16)), (hi_r.astype(BF16), hi.astype(BF16)))


def _build_bias(bias_scr, bk_ref, rel_ref):
    bk = bk_ref[...]
    for h in range(N_HEADS):
        acc = jnp.zeros((CHUNK, 2 * CHUNK), F32)
        for b in range(N_BUCKETS):
            acc = jnp.where(bk == b, rel_ref[b, h], acc)
        bias_scr[h] = acc


def _attn_probs(qp, k_h, bias_h, sink_h, ok):
    s = _dot_nt(qp, k_h) * (HEAD_DIM ** -0.5) + bias_h
    s = jnp.where(ok, s, NEG_INF)
    m = jnp.maximum(jnp.max(s, axis=-1, keepdims=True), sink_h)
    p = jnp.exp(s - m)
    es = jnp.exp(sink_h - m)
    den = jnp.sum(p, axis=-1, keepdims=True) + es
    return p / den, es / den


def _attn_block_inputs(qkv_ref, n):
    r0 = pl.multiple_of(n * CHUNK, CHUNK)
    rp = pl.multiple_of(jnp.maximum(n - 1, 0) * CHUNK, CHUNK)
    kw = jnp.concatenate([qkv_ref[pl.ds(rp, CHUNK), Q_DIM:Q_DIM + KV_DIM],
                          qkv_ref[pl.ds(r0, CHUNK), Q_DIM:Q_DIM + KV_DIM]], axis=0)
    vw = jnp.concatenate([qkv_ref[pl.ds(rp, CHUNK), Q_DIM + KV_DIM:B_DIM],
                          qkv_ref[pl.ds(r0, CHUNK), Q_DIM + KV_DIM:B_DIM]], axis=0)
    return r0, _kv_variants(kw), _kv_variants(vw)


def _attn_fwd(proj_b, sinks, rel_bias, n_seq, seq):
    nb = seq // CHUNK
    bk = jnp.asarray(_band_buckets())

    def body(qkv_ref, bk_ref, rel_ref, sink_ref, o_ref, bias_scr):
        _build_bias(bias_scr, bk_ref, rel_ref)
        col = lax.broadcasted_iota(jnp.int32, (CHUNK, 2 * CHUNK), 1)
        valid = bk_ref[...] >= 0

        def blk(n, carry):
            r0, kv, vv = _attn_block_inputs(qkv_ref, n)
            ok = valid & ((col >= CHUNK) | (n > 0))
            for pr in range(N_HEADS // 2):
                qp = qkv_ref[pl.ds(r0, CHUNK), pr * LANES:(pr + 1) * LANES]
                kvh = pr // 2
                acc = jnp.zeros((CHUNK, LANES), F32)
                for hh in range(2):
                    h = 2 * pr + hh
                    prob, _ = _attn_probs(qp, kv[kvh][hh], bias_scr[h], sink_ref[0, h], ok)
                    acc = acc + _dot(prob.astype(BF16), vv[kvh][hh])
                o_ref[pl.ds(r0, CHUNK), pr * LANES:(pr + 1) * LANES] = acc.astype(BF16)
            return carry

        lax.fori_loop(0, nb, blk, 0)

    smem = pl.BlockSpec(memory_space=pltpu.SMEM)
    return pl.pallas_call(
        body, name="attn_fwd", grid=(n_seq,),
        in_specs=[_row(seq, B_DIM), _full(bk.shape), smem, smem],
        out_specs=_row(seq, Q_DIM), out_shape=_sds((n_seq * seq, Q_DIM), BF16),
        scratch_shapes=[pltpu.VMEM((N_HEADS, CHUNK, 2 * CHUNK), F32)],
        compiler_params=_cp(("arbitrary",)),
    )(proj_b, bk, rel_bias, sinks)


def _dot_stacked(a, w_ref):
    return jnp.concatenate([_dot(a, w_ref[i]) for i in range(N_CHIPS)], axis=1)


def _dot_nt_stacked(a, w_ref):
    w = w_ref.shape[2]
    acc = _dot_nt(a[:, :w], w_ref[0])
    for i in range(1, N_CHIPS):
        acc = acc + _dot_nt(a[:, i * w:(i + 1) * w], w_ref[i])
    return acc


def _merge_fwd(x2, y_a, y_b, proj_g, w_pa, w_pb, w_out, tm):
    T = x2.shape[0]

    def body(x_ref, ya_ref, yb_ref, g_ref, wpa_ref, wpb_ref, wo_ref, x1_ref, mg_ref):
        g = g_ref[...].astype(F32)
        pa = _dot_stacked(ya_ref[...], wpa_ref)
        pb = _dot_stacked(yb_ref[...], wpb_ref)
        merged = (_sigmoid(g[:, :D_MODEL]) * pa + _sigmoid(g[:, D_MODEL:]) * pb).astype(BF16)
        mg_ref[...] = merged
        x1_ref[...] = x_ref[...] + _dot(merged, wo_ref[...])

    return pl.pallas_call(
        body, name="merge_fwd", grid=(T // tm,),
        in_specs=[_row(tm, D_MODEL), _row(tm, A_WIDTH), _row(tm, Q_DIM), _row(tm, G_DIM),
                  _full(w_pa.shape), _full(w_pb.shape), _full(w_out.shape)],
        out_specs=[_row(tm, D_MODEL), _row(tm, D_MODEL)],
        out_shape=[_sds((T, D_MODEL), F32), _sds((T, D_MODEL), BF16)],
        compiler_params=_cp(("arbitrary",), 40),
    )(x2, y_a, y_b, proj_g, w_pa, w_pb, w_out)


def _upproj(x1, g_ffn, w_up, tm):
    T = x1.shape[0]
    cw = w_up.shape[2]

    def body(x_ref, g_ref, w_ref, u_ref, h_ref):
        x = x_ref[...]
        h = (x * _rms_r(x) * g_ref[...]).astype(BF16)
        h_ref[...] = h
        for i in range(N_CHIPS):
            u_ref[:, i * cw:(i + 1) * cw] = _dot(h, w_ref[i]).astype(BF16)

    return pl.pallas_call(
        body, name="upproj", grid=(T // tm,),
        in_specs=[_row(tm, D_MODEL), _full(g_ffn.shape), _full(w_up.shape)],
        out_specs=[_row(tm, 2 * D_FF), _row(tm, D_MODEL)],
        out_shape=[_sds((T, 2 * D_FF), BF16), _sds((T, D_MODEL), BF16)],
        compiler_params=_cp(("arbitrary",), 56),
    )(x1, g_ffn, w_up)


def _shift_down(u, halo, k):
    rolled = pltpu.roll(u, k, 0)
    row = lax.broadcasted_iota(jnp.int32, u.shape, 0)
    if k == 1:
        return jnp.where(row == 0, halo[1:2], rolled)
    return jnp.where(row == 0, halo[0:1], jnp.where(row == 1, halo[1:2], rolled))


def _shift_up(d, halo, k):
    tm = d.shape[0]
    rolled = pltpu.roll(d, tm - k, 0)
    row = lax.broadcasted_iota(jnp.int32, d.shape, 0)
    if k == 1:
        return jnp.where(row == tm - 1, halo[0:1], rolled)
    return jnp.where(row == tm - 2, halo[0:1], jnp.where(row == tm - 1, halo[1:2], rolled))


def _conv_taps(u_ref, halo_ref, cols, at_start):
    u = u_ref[:, cols].astype(F32)
    hl = halo_ref[:, cols].astype(F32)[BF16_ROWS - 2:BF16_ROWS]
    hl = jnp.where(at_start, 0.0, hl)
    return u, _shift_down(u, hl, 1), _shift_down(u, hl, 2)


def _conv_out(taps, wc, bc):
    u, u1, u2 = taps
    return wc[0:1] * u2 + wc[1:2] * u1 + wc[2:3] * u + bc


def _prev_halo_spec(tm, width, col_block=None):
    k = tm // BF16_ROWS
    if col_block is None:
        return pl.BlockSpec((BF16_ROWS, width), lambda i: (jnp.maximum(i * k - 1, 0), 0))
    return pl.BlockSpec((BF16_ROWS, width), lambda j, i: (jnp.maximum(i * k - 1, 0), col_block(j)))


def _ffn_down_loss(upre, x1, target, w_conv, b_conv, w_down, g_final, tm, seq):
    T = x1.shape[0]
    tiles_per_seq = seq // tm
    half = D_FF // 2

    def body(u_ref, hl_ref, x1_ref, t_ref, wc_ref, bc_ref, wd_ref, g_ref, dx2_ref, loss_ref, gg_ref):
        i = pl.program_id(0)
        at_start = (i % tiles_per_seq) == 0
        acc = jnp.zeros((tm, D_MODEL), F32)
        for j in range(2):
            gc = slice(j * half, (j + 1) * half)
            vc = slice(D_FF + j * half, D_FF + (j + 1) * half)
            gate = _conv_out(_conv_taps(u_ref, hl_ref, gc, at_start), wc_ref[:, gc], bc_ref[:, gc])
            val = _conv_out(_conv_taps(u_ref, hl_ref, vc, at_start), wc_ref[:, vc], bc_ref[:, vc])
            act = (gate * _sigmoid(gate) * val).astype(BF16)
            acc = acc + _dot(act, wd_ref[gc, :])
        x2 = x1_ref[...] + acc
        r = _rms_r(x2)
        n = x2 * r
        g = g_ref[...]
        diff = n * g - t_ref[...]
        dy = diff * (1.0 / D_MODEL)
        dx2_ref[...] = _rms_bwd(dy, n, r, g)

        @pl.when(i == 0)
        def _():
            loss_ref[...] = jnp.zeros_like(loss_ref)
            gg_ref[...] = jnp.zeros_like(gg_ref)

        loss_ref[...] += 0.5 * jnp.sum(jnp.mean(diff * diff, axis=-1, keepdims=True), axis=0, keepdims=True)
        gg_ref[...] += jnp.sum(dy * n, axis=0, keepdims=True)

    return pl.pallas_call(
        body, name="ffn_down_loss", grid=(T // tm,),
        in_specs=[_row(tm, 2 * D_FF), _prev_halo_spec(tm, 2 * D_FF), _row(tm, D_MODEL), _row(tm, D_MODEL),
                  _full(w_conv.shape), _full(b_conv.shape), _full(w_down.shape), _full(g_final.shape)],
        out_specs=[_row(tm, D_MODEL), _full((1, 1)), _full((1, D_MODEL))],
        out_shape=[_sds((T, D_MODEL), F32), _sds((1, 1), F32), _sds((1, D_MODEL), F32)],
        compiler_params=_cp(("arbitrary",), 56),
    )(upre, upre, x1, target, w_conv, b_conv, w_down, g_final)


def _ffn_bwd_act(upre, dx2, w_conv, b_conv, w_down, tm, seq):
    T = dx2.shape[0]
    tiles_per_seq = seq // tm
    half = D_FF // 2
    nt = T // tm

    def body(ug_ref, uv_ref, hg_ref, hv_ref, dx_ref, wcg_ref, wcv_ref, bcg_ref, bcv_ref, wd_ref,
             dg_ref, dv_ref, gwd_out, gbg_ref, gbv_ref, gwg_ref, gwv_ref, gwd_ref):
        i = pl.program_id(1)
        at_start = (i % tiles_per_seq) == 0
        allc = slice(0, half)
        tg = _conv_taps(ug_ref, hg_ref, allc, at_start)
        tv = _conv_taps(uv_ref, hv_ref, allc, at_start)
        gate = _conv_out(tg, wcg_ref[...], bcg_ref[...])
        val = _conv_out(tv, wcv_ref[...], bcv_ref[...])
        sg = _sigmoid(gate)
        silu = gate * sg
        dx = dx_ref[...].astype(BF16)
        d_act = _dot_nt(dx, wd_ref[...])
        d_val = d_act * silu
        d_gate = d_act * val * (sg * (1.0 + gate * (1.0 - sg)))
        dg_ref[...] = d_gate.astype(BF16)
        dv_ref[...] = d_val.astype(BF16)

        @pl.when(i == 0)
        def _():
            for r in (gwd_ref, gbg_ref, gbv_ref, gwg_ref, gwv_ref):
                r[...] = jnp.zeros_like(r)

        gwd_ref[...] += _dot_tn((silu * val).astype(BF16), dx)
        gbg_ref[...] += jnp.sum(d_gate, axis=0, keepdims=True)
        gbv_ref[...] += jnp.sum(d_val, axis=0, keepdims=True)
        for k in range(3):
            gwg_ref[k:k + 1, :] += jnp.sum(d_gate * tg[2 - k], axis=0, keepdims=True)
            gwv_ref[k:k + 1, :] += jnp.sum(d_val * tv[2 - k], axis=0, keepdims=True)

        @pl.when(i == nt - 1)
        def _():
            gwd_out[...] = gwd_ref[...].astype(BF16)

    gcol = lambda j: j
    vcol = lambda j: 2 + j
    tile = lambda cb: pl.BlockSpec((tm, half), lambda j, i: (i, cb(j)))
    vec = lambda rows, cb: pl.BlockSpec((rows, half), lambda j, i: (0, cb(j)))
    return pl.pallas_call(
        body, name="ffn_bwd_act", grid=(2, nt),
        in_specs=[tile(gcol), tile(vcol), _prev_halo_spec(tm, half, gcol), _prev_halo_spec(tm, half, vcol),
                  pl.BlockSpec((tm, D_MODEL), lambda j, i: (i, 0)),
                  vec(3, gcol), vec(3, vcol), vec(1, gcol), vec(1, vcol),
                  pl.BlockSpec((half, D_MODEL), lambda j, i: (j, 0))],
        out_specs=[tile(gcol), tile(gcol), pl.BlockSpec((half, D_MODEL), lambda j, i: (j, 0)),
                   vec(1, gcol), vec(1, gcol), vec(3, gcol), vec(3, gcol)],
        out_shape=[_sds((T, D_FF), BF16), _sds((T, D_FF), BF16), _sds((D_FF, D_MODEL), BF16),
                   _sds((1, D_FF), F32), _sds((1, D_FF), F32), _sds((3, D_FF), F32), _sds((3, D_FF), F32)],
        scratch_shapes=[pltpu.VMEM((half, D_MODEL), F32)],
        compiler_params=_cp(("arbitrary", "arbitrary"), 56),
    )(upre, upre, upre, upre, dx2, w_conv, w_conv, b_conv, b_conv, w_down)


def _ffn_bwd_up(d_gate, d_val, dx2, x1, g_ffn, w_conv, w_up, tm, seq):
    T = dx2.shape[0]
    tiles_per_seq = seq // tm
    k16 = tm // BF16_ROWS
    n16 = T // BF16_ROWS
    cw = D_FF // 2

    def body(dg_ref, dv_ref, hg_ref, hv_ref, dx2_ref, x1_ref, g_ref, wc_ref, wu_ref, du_ref, dx1_ref, gg_ref):
        i = pl.program_id(0)
        at_end = (i % tiles_per_seq) == tiles_per_seq - 1
        dh = jnp.zeros((tm, D_MODEL), F32)
        for j in range(4):
            src, hsrc = (dg_ref, hg_ref) if j < 2 else (dv_ref, hv_ref)
            ls = slice((j % 2) * cw, (j % 2 + 1) * cw)
            cs = slice(j * cw, (j + 1) * cw)
            d = src[:, ls].astype(F32)
            hl = hsrc[:, ls].astype(F32)[0:2]
            hl = jnp.where(at_end, 0.0, hl)
            wc = wc_ref[:, cs]
            du = (wc[2:3] * d + wc[1:2] * _shift_up(d, hl, 1) + wc[0:1] * _shift_up(d, hl, 2)).astype(BF16)
            du_ref[:, cs] = du
            dh = dh + _dot_nt(du, wu_ref[j])
        x = x1_ref[...]
        r = _rms_r(x)
        n = x * r
        dx1_ref[...] = dx2_ref[...] + _rms_bwd(dh, n, r, g_ref[...])

        @pl.when(i == 0)
        def _():
            gg_ref[...] = jnp.zeros_like(gg_ref)

        gg_ref[...] += jnp.sum(dh * n, axis=0, keepdims=True)

    nxt = pl.BlockSpec((BF16_ROWS, D_FF), lambda i: (jnp.minimum((i + 1) * k16, n16 - 1), 0))
    return pl.pallas_call(
        body, name="ffn_bwd_up", grid=(T // tm,),
        in_specs=[_row(tm, D_FF), _row(tm, D_FF), nxt, nxt, _row(tm, D_MODEL), _row(tm, D_MODEL),
                  _full(g_ffn.shape), _full(w_conv.shape), _full(w_up.shape)],
        out_specs=[_row(tm, 2 * D_FF), _row(tm, D_MODEL), _full((1, D_MODEL))],
        out_shape=[_sds((T, 2 * D_FF), BF16), _sds((T, D_MODEL), F32), _sds((1, D_MODEL), F32)],
        compiler_params=_cp(("arbitrary",), 60),
    )(d_gate, d_val, d_gate, d_val, dx2, x1, g_ffn, w_conv, w_up)


def _matmul_tn(a, b, tn, tk, name):
    T, M = a.shape
    N = b.shape[1]
    nk = T // tk

    def body(a_ref, b_ref, o_ref, acc_ref):
        k = pl.program_id(1)

        @pl.when(k == 0)
        def _():
            acc_ref[...] = jnp.zeros_like(acc_ref)

        acc_ref[...] += _dot_tn(a_ref[...], b_ref[...])

        @pl.when(k == nk - 1)
        def _():
            o_ref[...] = acc_ref[...].astype(BF16)

    return pl.pallas_call(
        body, name=name, grid=(N // tn, nk),
        in_specs=[pl.BlockSpec((tk, M), lambda j, k: (k, 0)), pl.BlockSpec((tk, tn), lambda j, k: (k, j))],
        out_specs=pl.BlockSpec((M, tn), lambda j, k: (0, j)), out_shape=_sds((M, N), BF16),
        scratch_shapes=[pltpu.VMEM((M, tn), F32)],
        compiler_params=_cp(("arbitrary", "arbitrary"), 48),
    )(a, b)


def _merge_bwd(dx1, merged, y_a, y_b, proj_g, w_pa, w_pb, w_out, tm):
    T = dx1.shape[0]

    nt = T // tm
    pshape = (A_WIDTH, D_MODEL)

    def body(dx_ref, mg_ref, ya_ref, yb_ref, g_ref, wpa_ref, wpb_ref, wo_ref,
             dg_ref, dya_ref, dyb_ref, gwo_out, gwpa_out, gwpb_out, gwo_ref, gwpa_ref, gwpb_ref):
        i = pl.program_id(0)
        dx = dx_ref[...].astype(BF16)
        dm = _dot_nt(dx, wo_ref[...])
        g = g_ref[...].astype(F32)
        ya = ya_ref[...]
        yb = yb_ref[...]
        pa = _dot_stacked(ya, wpa_ref)
        pb = _dot_stacked(yb, wpb_ref)
        sa = _sigmoid(g[:, :D_MODEL])
        sb = _sigmoid(g[:, D_MODEL:])
        dpa = (dm * sa).astype(BF16)
        dpb = (dm * sb).astype(BF16)
        dg_ref[:, :D_MODEL] = (dm * pa * (sa * (1.0 - sa))).astype(BF16)
        dg_ref[:, D_MODEL:] = (dm * pb * (sb * (1.0 - sb))).astype(BF16)
        dya_ref[...] = _dot_nt_stacked(dpa, wpa_ref).astype(BF16)
        dyb_ref[...] = _dot_nt_stacked(dpb, wpb_ref).astype(BF16)

        @pl.when(i == 0)
        def _():
            for r in (gwo_ref, gwpa_ref, gwpb_ref):
                r[...] = jnp.zeros_like(r)

        gwo_ref[...] += _dot_tn(mg_ref[...], dx)
        gwpa_ref[...] += _dot_tn(ya, dpa)
        gwpb_ref[...] += _dot_tn(yb, dpb)

        @pl.when(i == nt - 1)
        def _():
            gwo_out[...] = gwo_ref[...].astype(BF16)
            gwpa_out[...] = gwpa_ref[...].astype(BF16)
            gwpb_out[...] = gwpb_ref[...].astype(BF16)

    return pl.pallas_call(
        body, name="merge_bwd", grid=(nt,),
        in_specs=[_row(tm, D_MODEL), _row(tm, D_MODEL), _row(tm, A_WIDTH), _row(tm, Q_DIM), _row(tm, G_DIM),
                  _full(w_pa.shape), _full(w_pb.shape), _full(w_out.shape)],
        out_specs=[_row(tm, G_DIM), _row(tm, A_WIDTH), _row(tm, Q_DIM),
                   _full(w_out.shape), _full(pshape), _full(pshape)],
        out_shape=[_sds((T, G_DIM), BF16), _sds((T, A_WIDTH), BF16), _sds((T, Q_DIM), BF16),
                   _sds(w_out.shape, BF16), _sds(pshape, BF16), _sds(pshape, BF16)],
        scratch_shapes=[pltpu.VMEM(w_out.shape, F32), pltpu.VMEM(pshape, F32), pltpu.VMEM(pshape, F32)],
        compiler_params=_cp(("arbitrary",), 56),
    )(dx1, merged, y_a, y_b, proj_g, w_pa, w_pb, w_out)


def _sgu_bwd(proj_a, d_ya, g_sgu, w_s, b_st, tm):
    T = proj_a.shape[0]

    def body(p_ref, dy_ref, g_ref, ws_ref, bs_ref, dp_ref, gws_ref, gbs_ref, gg_ref):
        tril = _tril()
        g = g_ref[...]
        pu, pv, u, tu, vv, tv, rv, vn = _sgu_parts(p_ref[...].astype(F32), g)
        dy = dy_ref[...].astype(F32)

        @pl.when(pl.program_id(0) == 0)
        def _():
            for r in (gws_ref, gbs_ref, gg_ref):
                r[...] = jnp.zeros_like(r)

        du_cols = []
        dvn_cols = []
        for gi in range(A_GROUPS):
            wm = jnp.where(tril, ws_ref[gi], 0.0).astype(BF16)
            wmt = wm.astype(F32).T.astype(BF16)
            bcol = bs_ref[:, gi:gi + 1]
            cs = slice(gi * CHUNK, (gi + 1) * CHUNK)
            du_rows = []
            dvn_rows = []
            gw = jnp.zeros((CHUNK, CHUNK), F32)
            gb = jnp.zeros((CHUNK, 1), F32)
            for c in range(tm // CHUNK):
                rs = slice(c * CHUNK, (c + 1) * CHUNK)
                vn_c = vn[rs, cs]
                s = _dot(wm, vn_c) + bcol
                dy_c = dy[rs, cs]
                ds = dy_c * u[rs, cs]
                du_rows.append(dy_c * s)
                dsb = ds.astype(BF16)
                gw = gw + _dot_nt(dsb, vn_c)
                gb = gb + jnp.sum(ds, axis=-1, keepdims=True)
                dvn_rows.append(_dot(wmt, dsb))
            gws_ref[gi] += jnp.where(tril, gw, 0.0)
            gbs_ref[:, gi:gi + 1] += gb
            du_cols.append(jnp.concatenate(du_rows, axis=0))
            dvn_cols.append(jnp.concatenate(dvn_rows, axis=0))
        du = jnp.concatenate(du_cols, axis=1)
        dvn = jnp.concatenate(dvn_cols, axis=1)
        vhat = vv * rv
        gg_ref[...] += jnp.sum(dvn * vhat, axis=0, keepdims=True)
        dvv = _rms_bwd(dvn, vhat, rv, g)
        dp_ref[:, :A_WIDTH] = (du * _gelu_grad(pu, tu)).astype(BF16)
        dp_ref[:, A_WIDTH:] = (dvv * _gelu_grad(pv, tv)).astype(BF16)

    return pl.pallas_call(
        body, name="sgu_bwd", grid=(T // tm,),
        in_specs=[_row(tm, A_DIM), _row(tm, A_WIDTH), _full(g_sgu.shape), _full(w_s.shape), _full(b_st.shape)],
        out_specs=[_row(tm, A_DIM), _full(w_s.shape), _full(b_st.shape), _full(g_sgu.shape)],
        out_shape=[_sds((T, A_DIM), BF16), _sds(w_s.shape, F32), _sds(b_st.shape, F32), _sds(g_sgu.shape, F32)],
        compiler_params=_cp(("arbitrary",)),
    )(proj_a, d_ya, g_sgu, w_s, b_st)


def _attn_bwd(proj_b, d_yb, sinks, rel_bias, n_seq, seq):
    nb = seq // CHUNK
    bk = jnp.asarray(_band_buckets())

    def body(qkv_ref, do_ref, bk_ref, rel_ref, sink_ref, d_ref, gs_ref, gr_ref, bias_scr, dbias_scr, dk_scr, dv_scr, ds_scr):
        b = pl.program_id(0)
        _build_bias(bias_scr, bk_ref, rel_ref)
        col = lax.broadcasted_iota(jnp.int32, (CHUNK, 2 * CHUNK), 1)
        lane = lax.broadcasted_iota(jnp.int32, (2 * CHUNK, LANES), 1)
        valid = bk_ref[...] >= 0

        @pl.when(b == 0)
        def _():
            dbias_scr[...] = jnp.zeros_like(dbias_scr)
            ds_scr[...] = jnp.zeros_like(ds_scr)

        dk_scr[...] = jnp.zeros_like(dk_scr)
        dv_scr[...] = jnp.zeros_like(dv_scr)

        def to_kv_lanes(a, hh, kvh):
            a = jnp.where((lane >= HEAD_DIM) if hh == 1 else (lane < HEAD_DIM), a, 0.0)
            return a if hh == kvh else pltpu.roll(a, HEAD_DIM, 1)

        def blk(n, carry):
            r0, kv, vv = _attn_block_inputs(qkv_ref, n)
            ok = valid & ((col >= CHUNK) | (n > 0))
            dkw = jnp.zeros((2 * CHUNK, KV_DIM), F32)
            dvw = jnp.zeros((2 * CHUNK, KV_DIM), F32)
            for pr in range(N_HEADS // 2):
                ps = slice(pr * LANES, (pr + 1) * LANES)
                qp = qkv_ref[pl.ds(r0, CHUNK), ps]
                dop = do_ref[pl.ds(r0, CHUNK), ps]
                kvh = pr // 2
                dq = jnp.zeros((CHUNK, LANES), F32)
                for hh in range(2):
                    h = 2 * pr + hh
                    prob, psink = _attn_probs(qp, kv[kvh][hh], bias_scr[h], sink_ref[0, h], ok)
                    dp = _dot_nt(dop, vv[kvh][hh])
                    delta = jnp.sum(prob * dp, axis=-1, keepdims=True)
                    dsc = prob * (dp - delta)
                    ds_scr[h] += psink * delta
                    dbias_scr[h] += dsc
                    dsb = (dsc * (HEAD_DIM ** -0.5)).astype(BF16)
                    dq = dq + _dot(dsb, kv[kvh][hh])
                    dkw = dkw + to_kv_lanes(_dot_tn(dsb, qp), hh, kvh)
                    dvw = dvw + to_kv_lanes(_dot_tn(prob.astype(BF16), dop), hh, kvh)
                d_ref[pl.ds(r0, CHUNK), ps] = dq.astype(BF16)
            dk_scr[pl.ds(r0, 2 * CHUNK), :] += dkw
            dv_scr[pl.ds(r0, 2 * CHUNK), :] += dvw
            return carry

        lax.fori_loop(0, nb, blk, 0)
        d_ref[:, Q_DIM:Q_DIM + KV_DIM] = dk_scr[CHUNK:, :].astype(BF16)
        d_ref[:, Q_DIM + KV_DIM:] = dv_scr[CHUNK:, :].astype(BF16)

        @pl.when(b == n_seq - 1)
        def _():
            bkv = bk_ref[...]
            for h in range(N_HEADS):
                gs_ref[0:1, h:h + 1] = -jnp.sum(ds_scr[h], axis=0, keepdims=True)
                db = dbias_scr[h]
                for bb in range(N_BUCKETS):
                    part = jnp.sum(jnp.where(bkv == bb, db, 0.0), axis=-1, keepdims=True)
                    gr_ref[bb:bb + 1, h:h + 1] = jnp.sum(part, axis=0, keepdims=True)

    smem = pl.BlockSpec(memory_space=pltpu.SMEM)
    return pl.pallas_call(
        body, name="attn_bwd", grid=(n_seq,),
        in_specs=[_row(seq, B_DIM), _row(seq, Q_DIM), _full(bk.shape), smem, smem],
        out_specs=[_row(seq, B_DIM), _full((1, N_HEADS)), _full((N_BUCKETS, N_HEADS))],
        out_shape=[_sds((n_seq * seq, B_DIM), BF16), _sds((1, N_HEADS), F32), _sds((N_BUCKETS, N_HEADS), F32)],
        scratch_shapes=[pltpu.VMEM((N_HEADS, CHUNK, 2 * CHUNK), F32), pltpu.VMEM((N_HEADS, CHUNK, 2 * CHUNK), F32),
                        pltpu.VMEM((seq + CHUNK, KV_DIM), F32), pltpu.VMEM((seq + CHUNK, KV_DIM), F32),
                        pltpu.VMEM((N_HEADS, CHUNK, 1), F32)],
        compiler_params=_cp(("arbitrary",), 40),
    )(proj_b, d_yb, bk, rel_bias, sinks)


def _inproj_bwd(d_g, d_a, d_b, x2, dx1, g_mix, w_g, w_a, w_b, tm):
    T = x2.shape[0]

    def body(dg_ref, da_ref, db_ref, x_ref, dx1_ref, g_ref, wg_ref, wa_ref, wb_ref, gx_ref, gg_ref):
        dh = _dot_nt(dg_ref[...], wg_ref[...]) + _dot_nt(da_ref[...], wa_ref[...]) + _dot_nt(db_ref[...], wb_ref[...])
        x = x_ref[...]
        r = _rms_r(x)
        n = x * r
        gx_ref[...] = dx1_ref[...] + _rms_bwd(dh, n, r, g_ref[...])

        @pl.when(pl.program_id(0) == 0)
        def _():
            gg_ref[...] = jnp.zeros_like(gg_ref)

        gg_ref[...] += jnp.sum(dh * n, axis=0, keepdims=True)

    return pl.pallas_call(
        body, name="inproj_bwd", grid=(T // tm,),
        in_specs=[_row(tm, G_DIM), _row(tm, A_DIM), _row(tm, B_DIM), _row(tm, D_MODEL), _row(tm, D_MODEL),
                  _full(g_mix.shape), _full(w_g.shape), _full(w_a.shape), _full(w_b.shape)],
        out_specs=[_row(tm, D_MODEL), _full((1, D_MODEL))],
        out_shape=[_sds((T, D_MODEL), F32), _sds((1, D_MODEL), F32)],
        compiler_params=_cp(("arbitrary",), 48),
    )(d_g, d_a, d_b, x2, dx1, g_mix, w_g, w_a, w_b)


def _local_step(x, target, g_mix, g_sgu, w_s, b_s, sinks, rel_bias, g_ffn, b_conv, g_final,
                w_g, w_a, w_b, w_pa, w_pb, w_out, w_up, w_conv, w_down):
    n_seq, seq, _ = x.shape
    T = n_seq * seq
    tm = min(256, seq)
    x2 = x.reshape(T, D_MODEL)
    tgt = target.reshape(T, D_MODEL)
    b_st = b_s.T
    g_fin = g_final.reshape(1, D_MODEL)

    proj_g, proj_a, proj_b, h = _inproj(x2, g_mix, w_g, w_a, w_b, tm)
    y_a = _sgu_fwd(proj_a, g_sgu, w_s, b_st, tm)
    y_b = _attn_fwd(proj_b, sinks, rel_bias, n_seq, seq)
    x1, merged = _merge_fwd(x2, y_a, y_b, proj_g, w_pa, w_pb, w_out, tm)
    upre, h2 = _upproj(x1, g_ffn, w_up, tm)
    dx2, loss, gg_final = _ffn_down_loss(upre, x1, tgt, w_conv, b_conv, w_down, g_fin, tm, seq)

    d_gate, d_val, gw_down, gb_g, gb_v, gwc_g, gwc_v = _ffn_bwd_act(upre, dx2, w_conv, b_conv, w_down, tm, seq)
    gb_conv = jnp.concatenate([gb_g, gb_v], axis=1)
    gw_conv = jnp.concatenate([gwc_g, gwc_v], axis=1)
    d_upre, dx1, gg_ffn = _ffn_bwd_up(d_gate, d_val, dx2, x1, g_ffn, w_conv, w_up, tm, seq)
    gw_up = _matmul_tn(h2, d_upre, 2 * D_FF // 4, min(512, T), "grad_w_up")
    d_g, d_ya, d_yb, gw_out, gw_pa, gw_pb = _merge_bwd(dx1, merged, y_a, y_b, proj_g, w_pa, w_pb, w_out, tm)
    d_a, gw_s, gb_st, gg_sgu = _sgu_bwd(proj_a, d_ya, g_sgu, w_s, b_st, tm)
    d_b, g_sinks, g_rel = _attn_bwd(proj_b, d_yb, sinks, rel_bias, n_seq, seq)
    grad_x, gg_mix = _inproj_bwd(d_g, d_a, d_b, x2, dx1, g_mix, w_g, w_a, w_b, tm)
    gw_g = _matmul_tn(h, d_g, D_MODEL, min(512, T), "grad_w_in_gate")
    gw_a = _matmul_tn(h, d_a, A_DIM, min(512, T), "grad_w_in_a")
    gw_b = _matmul_tn(h, d_b, B_DIM, min(512, T), "grad_w_in_b")
    gw_in = jnp.concatenate([gw_a, gw_b, gw_g], axis=1).reshape(D_MODEL, N_CHIPS, -1).transpose(1, 0, 2)

    small = dict(g_mix=gg_mix, g_sgu=gg_sgu, w_s=gw_s, b_s=gb_st.T, sinks=g_sinks, rel_bias=g_rel,
                 g_ffn=gg_ffn, b_conv=gb_conv, g_final=gg_final, w_conv=gw_conv)
    big = dict(w_in=gw_in, w_pa=gw_pa, w_pb=gw_pb, w_out=gw_out, w_up=gw_up, w_down=gw_down)
    return loss, grad_x.reshape(x.shape), small, big


_BIG = ("w_in", "w_pa", "w_pb", "w_out", "w_up", "w_down")

_SMALL = (("loss", (1, 1)), ("g_final", (1, D_MODEL)), ("g_mix", (1, D_MODEL)), ("g_ffn", (1, D_MODEL)),
          ("g_sgu", (1, A_WIDTH)), ("b_s", (A_GROUPS, CHUNK)), ("sinks", (1, N_HEADS)), ("rel_bias", (N_BUCKETS, N_HEADS)),
          ("b_conv", (1, 2 * D_FF)), ("w_conv", (3, 2 * D_FF)), ("w_s", (A_GROUPS, CHUNK, CHUNK)))
SMALL_ROWS = 96


def _pack_small(vals):
    flat = jnp.concatenate([vals[n].astype(F32).reshape(-1) for n, _ in _SMALL])
    flat = jnp.pad(flat, (0, SMALL_ROWS * D_MODEL - flat.shape[0]))
    return flat.reshape(SMALL_ROWS, D_MODEL)


def _unpack_small(buf):
    flat = buf.reshape(-1)
    out = {}
    off = 0
    for n, shp in _SMALL:
        k = int(np.prod(shp))
        out[n] = flat[off:off + k].reshape(shp)
        off += k
    return out


HBM = pl.BlockSpec(memory_space=pltpu.HBM)


def _mesh_pos():
    return lax.axis_index("x"), lax.axis_index("y"), lax.axis_index("c")


def _other_chips(x, y):
    return [(1 - x, y), (x, 1 - y), (1 - x, 1 - y)]


def _remote(src, dst, send_sem, recv_sem, to):
    return pltpu.make_async_remote_copy(src_ref=src, dst_ref=dst, send_sem=send_sem, recv_sem=recv_sem,
                                        device_id=to, device_id_type=MESH)


def _own_slot(own, n, at):
    return lax.dynamic_update_slice(lax.empty((n,) + own.shape, own.dtype), own[None], (at,) + (0,) * own.ndim)


def _allgather_weights(stacks, wc_stack):
    names = list(stacks)
    n = len(names)

    def body(*refs):
        ins, outs = refs[:n + 1], refs[n + 1:2 * n + 2]
        send_sems, recv_sems = refs[2 * n + 2:]
        x, y, c = _mesh_pos()
        me = 2 * x + y
        sibling = (x, y, 1 - c)
        chips = _other_chips(x, y)

        def half(ref, chip, hc):
            hr = ref.shape[1] // 2
            return ref.at[chip, pl.ds(hc * hr, hr), :]

        first = []
        for k in range(n):
            first += [_remote(half(ins[k], me, c), half(outs[k], me, c), send_sems.at[6 * k + j], recv_sems.at[6 * k + j], (cx, cy, c))
                      for j, (cx, cy) in enumerate(chips)]
        first += [_remote(ins[n].at[me], outs[n].at[me], send_sems.at[6 * n + j], recv_sems.at[6 * n + j], (cx, cy, c))
                  for j, (cx, cy) in enumerate(chips)]
        for cp in first:
            cp.start()
        passed = []
        for k in range(n):
            for j, (cx, cy) in enumerate(chips):
                landed = half(outs[k], 2 * cx + cy, c)
                _remote(landed, landed, send_sems.at[6 * k + j], recv_sems.at[6 * k + j], (x, y, c)).wait_recv()
                passed.append(_remote(landed, landed, send_sems.at[6 * k + 3 + j], recv_sems.at[6 * k + 3 + j], sibling))
                passed[-1].start()
        for k in range(n):
            for j, (cx, cy) in enumerate(chips):
                theirs = half(outs[k], 2 * cx + cy, 1 - c)
                _remote(theirs, theirs, send_sems.at[6 * k + 3 + j], recv_sems.at[6 * k + 3 + j], (x, y, c)).wait_recv()
        for j, (cx, cy) in enumerate(chips):
            slot = outs[n].at[2 * cx + cy]
            _remote(slot, slot, send_sems.at[6 * n + j], recv_sems.at[6 * n + j], (x, y, c)).wait_recv()
        for cp in first + passed:
            cp.wait_send()

    arrays = [stacks[k] for k in names] + [wc_stack]
    outs = pl.pallas_call(
        body, name="allgather_weights",
        in_specs=[HBM] * (n + 1), out_specs=[HBM] * (n + 1), input_output_aliases={k: k for k in range(n + 1)},
        out_shape=[_sds(a.shape, a.dtype) for a in arrays],
        scratch_shapes=[pltpu.SemaphoreType.DMA((6 * n + 3,)), pltpu.SemaphoreType.DMA((6 * n + 3,))],
    )(*arrays)
    return dict(zip(names, outs[:n])), outs[n]


_KIND = {"w_in": "stack", "w_pa": "col", "w_pb": "col", "w_up": "col", "w_out": "row", "w_down": "row"}


def _half_view(ref, kind, h):
    if kind == "stack":
        k = ref.shape[1] // 2
        return ref.at[:, pl.ds(h * k, k), :]
    if kind == "col":
        k = ref.shape[0] // 2
        return ref.at[pl.ds(h * k, k), :]
    k = ref.shape[1] // 2
    return ref.at[:, pl.ds(h * k, k)]


def _shard_view(ref, kind, i):
    if kind == "stack":
        return ref.at[i]
    if kind == "col":
        k = ref.shape[1] // N_CHIPS
        return ref.at[:, pl.ds(i * k, k)]
    k = ref.shape[0] // N_CHIPS
    return ref.at[pl.ds(i * k, k), :]


def _region_view(ref, kind, h):
    if kind == "row":
        k = ref.shape[1] // 2
        return ref.at[:, pl.ds(h * k, k)]
    k = ref.shape[0] // 2
    return ref.at[pl.ds(h * k, k), :]


def _half_shape(shape, kind):
    if kind == "stack":
        return (shape[0], shape[1] // 2, shape[2])
    return (shape[0] // 2, shape[1]) if kind == "col" else (shape[0], shape[1] // 2)


def _part_shape(half_shape, kind):
    if kind == "stack":
        return tuple(half_shape[1:])
    k, w = half_shape
    return (k, w // N_CHIPS) if kind == "col" else (k // N_CHIPS, w)


def _pair_exchange(parts):
    names = list(parts)
    n = len(names)

    def body(*refs):
        p, q = refs[:n], refs[n:2 * n]
        send_sems, recv_sems = refs[2 * n:]
        x, y, c = _mesh_pos()

        def copy(k, h):
            return _remote(_half_view(p[k], _KIND[names[k]], h), q[k], send_sems.at[k], recv_sems.at[k], (x, y, 1 - c))

        for hc in range(2):
            @pl.when(c == hc)
            def _():
                for k in range(n):
                    copy(k, 1 - hc).start()

        for k in range(n):
            copy(k, 0).wait()

    outs = pl.pallas_call(
        body, name="grad_pair_exchange", in_specs=[HBM] * n, out_specs=[HBM] * n,
        out_shape=[_sds(_half_shape(parts[k].shape, _KIND[k]), parts[k].dtype) for k in names],
        scratch_shapes=[pltpu.SemaphoreType.DMA((n,)), pltpu.SemaphoreType.DMA((n,))],
    )(*[parts[k] for k in names])
    return dict(zip(names, outs))


def _half_blocks(shape, kind):
    if kind == "stack":
        _, k, w = shape
        tr = 256
        nb = k // 2 // tr
        return (N_CHIPS, nb), (1, tr, w), (lambda i, r, s: (i, r, 0)), (lambda i, r, s: (i, s[1] * nb + r, 0))
    k, w = shape
    if kind == "col":
        tr = 256 if w <= 2 * D_MODEL else 128
        nb = k // 2 // tr
        return (nb,), (tr, w), (lambda r, s: (r, 0)), (lambda r, s: (s[1] * nb + r, 0))
    tr = k // N_CHIPS
    return (N_CHIPS,), (tr, w // 2), (lambda r, s: (r, 0)), (lambda r, s: (r, s[1]))


def _pair_add(part, from_sibling, name, pos):
    kind = _KIND[name]
    grid, block, half_map, full_map = _half_blocks(part.shape, kind)

    def body(s_ref, p_ref, q_ref, o_ref):
        o_ref[...] = (p_ref[...].astype(F32) + q_ref[...].astype(F32)).astype(BF16)

    return pl.pallas_call(
        body, name="grad_pair_add_" + name,
        grid_spec=pltpu.PrefetchScalarGridSpec(
            num_scalar_prefetch=1, grid=grid,
            in_specs=[pl.BlockSpec(block, full_map), pl.BlockSpec(block, half_map)],
            out_specs=pl.BlockSpec(block, half_map)),
        out_shape=_sds(from_sibling.shape, BF16),
        compiler_params=_cp(("arbitrary",) * len(grid)),
    )(pos, part, from_sibling)


def _chip_exchange(sums):
    names = list(sums)
    n = len(names)

    def body(*refs):
        s, r = refs[:n], refs[n:2 * n]
        send_sems, recv_sems = refs[2 * n:]
        x, y, c = _mesh_pos()
        me = 2 * x + y

        def copy(k, i, j, to):
            return _remote(_shard_view(s[k], _KIND[names[k]], i), r[k].at[j], send_sems.at[3 * k + j], recv_sems.at[3 * k + j], to)

        for i in range(N_CHIPS):
            xi, yi = i // 2, i % 2
            j = jnp.where(xi != x, jnp.where(yi != y, 2, 0), 1)

            @pl.when(i != me)
            def _():
                for k in range(n):
                    copy(k, i, j, (xi, yi, c)).start()

        for k in range(n):
            for j in range(3):
                copy(k, 0, j, (x, y, c)).wait()

    outs = pl.pallas_call(
        body, name="grad_chip_exchange", in_specs=[HBM] * n, out_specs=[HBM] * n,
        out_shape=[_sds((3,) + _part_shape(sums[k].shape, _KIND[k]), sums[k].dtype) for k in names],
        scratch_shapes=[pltpu.SemaphoreType.DMA((3 * n,)), pltpu.SemaphoreType.DMA((3 * n,))],
    )(*[sums[k] for k in names])
    return dict(zip(names, outs))


def _owner_sum(part, from_sibling, from_chips, name, pos, shard_shape):
    kind = _KIND[name]
    _, pk, pw = from_chips.shape
    if kind == "row":
        tr, nb = pk, 1
        p_spec = pl.BlockSpec((tr, pw), lambda r, s: (s[0], s[1]))
        q_spec = pl.BlockSpec((tr, pw), lambda r, s: (s[0], 0))
        o_spec = pl.BlockSpec((tr, pw), lambda r, s: (0, s[1]))
    else:
        tr = 256
        nb = pk // tr
        if kind == "stack":
            p_spec = pl.BlockSpec((None, tr, pw), lambda r, s: (s[0], s[1] * nb + r, 0))
            q_spec = pl.BlockSpec((None, tr, pw), lambda r, s: (s[0], r, 0))
        else:
            p_spec = pl.BlockSpec((tr, pw), lambda r, s: (s[1] * nb + r, s[0]))
            q_spec = pl.BlockSpec((tr, pw), lambda r, s: (r, s[0]))
        o_spec = pl.BlockSpec((tr, pw), lambda r, s: (s[1] * nb + r, 0))

    def body(s_ref, p_ref, q_ref, r_ref, o_ref):
        acc = p_ref[...].astype(F32) + q_ref[...].astype(F32)
        for j in range(3):
            acc = acc + r_ref[j].astype(F32)
        o_ref[...] = acc

    return pl.pallas_call(
        body, name="grad_owner_sum_" + name,
        grid_spec=pltpu.PrefetchScalarGridSpec(
            num_scalar_prefetch=1, grid=(nb,),
            in_specs=[p_spec, q_spec, pl.BlockSpec((3, tr, pw), lambda r, s: (0, r, 0))],
            out_specs=o_spec),
        out_shape=_sds(shard_shape, F32),
        compiler_params=_cp(("arbitrary",), 32),
    )(pos, part, from_sibling, from_chips)


def _pair_share(shards):
    names = list(shards)
    n = len(names)

    def body(*refs):
        g_in, g_out = refs[:n], refs[n:2 * n]
        send_sems, recv_sems = refs[2 * n:]
        x, y, c = _mesh_pos()

        def copy(k, h):
            kind = _KIND[names[k]]
            return _remote(_region_view(g_in[k], kind, h), _region_view(g_out[k], kind, h), send_sems.at[k], recv_sems.at[k], (x, y, 1 - c))

        for hc in range(2):
            @pl.when(c == hc)
            def _():
                for k in range(n):
                    copy(k, hc).start()

        for k in range(n):
            copy(k, 0).wait()

    outs = pl.pallas_call(
        body, name="grad_pair_share", in_specs=[HBM] * n, out_specs=[HBM] * n, input_output_aliases={k: k for k in range(n)},
        out_shape=[_sds(shards[k].shape, shards[k].dtype) for k in names],
        scratch_shapes=[pltpu.SemaphoreType.DMA((n,)), pltpu.SemaphoreType.DMA((n,))],
    )(*[shards[k] for k in names])
    return dict(zip(names, outs))


def _allgather_small(block):
    m_per = block.shape[0]

    def body(x_ref, out_ref, send_sems, recv_sems, local_sem):
        x, y, c = _mesh_pos()
        me, sibling = (x, y, c), (x, y, 1 - c)
        chips = _other_chips(x, y)

        def rows(px, py, pc):
            return out_ref.at[4 * px + 2 * py + pc]

        def copy(k, block_of, to, src=None):
            return _remote(rows(*block_of) if src is None else src, rows(*block_of), send_sems.at[k], recv_sems.at[k], to)

        mine = pltpu.make_async_copy(x_ref, rows(*me), local_sem)
        mine.start()
        first = [copy(0, me, sibling, src=x_ref)]
        first += [copy(1 + j, me, (*chip, c), src=x_ref) for j, chip in enumerate(chips)]
        for cp in first:
            cp.start()
        passed = [copy(4 + j, (*chip, c), sibling) for j, chip in enumerate(chips)]
        for j, chip in enumerate(chips):
            copy(1 + j, (*chip, c), me).wait_recv()
            passed[j].start()
        copy(0, sibling, me).wait_recv()
        for j, chip in enumerate(chips):
            copy(4 + j, (*chip, 1 - c), me).wait_recv()
        for cp in first + passed:
            cp.wait_send()
        mine.wait()

    return pl.pallas_call(
        body, name="allgather_small",
        in_specs=[pl.BlockSpec(memory_space=pltpu.VMEM)], out_specs=pl.BlockSpec(memory_space=pltpu.VMEM),
        out_shape=_sds((N_DEV, m_per, D_MODEL), block.dtype),
        scratch_shapes=[pltpu.SemaphoreType.DMA((7,)), pltpu.SemaphoreType.DMA((7,)), pltpu.SemaphoreType.DMA],
    )(block)


def _adam_math(w, g, m, v):
    m = ADAM_B1 * m + (1.0 - ADAM_B1) * g
    v = ADAM_B2 * v + (1.0 - ADAM_B2) * (g * g)
    m_hat = m / (1.0 - ADAM_B1 ** ADAM_STEP)
    v_hat = v / (1.0 - ADAM_B2 ** ADAM_STEP)
    delta = -ADAM_LR * (m_hat / (jnp.sqrt(v_hat) + ADAM_EPS) + ADAM_WD * w)
    return delta, m, v


def _adamw(w, g, m, v, name):
    rows, cols = w.shape
    tr = rows
    for cand in (256, 128, 64, 32, 16, 8):
        if rows % cand == 0 and rows > cand:
            tr = cand
            break

    def body(w_ref, g_ref, m_ref, v_ref, d_ref, nm_ref, nv_ref):
        d, nm, nv = _adam_math(w_ref[...], g_ref[...], m_ref[...], v_ref[...])
        d_ref[...] = d
        nm_ref[...] = nm
        nv_ref[...] = nv

    spec = pl.BlockSpec((tr, cols), lambda i: (i, 0))
    return pl.pallas_call(
        body, name=name, grid=(rows // tr,), in_specs=[spec] * 4, out_specs=[spec] * 3,
        out_shape=[_sds(w.shape, F32)] * 3, compiler_params=_cp(("arbitrary",)),
    )(w, g, m, v)


def _small_sum_adamw(gathered, w, m, v):
    def body(a_ref, w_ref, m_ref, v_ref, g_ref, d_ref, nm_ref, nv_ref):
        g = a_ref[0]
        for k in range(1, N_DEV):
            g = g + a_ref[k]
        g_ref[...] = g
        d, nm, nv = _adam_math(w_ref[...], g, m_ref[...], v_ref[...])
        d_ref[...] = d
        nm_ref[...] = nm
        nv_ref[...] = nv

    return pl.pallas_call(
        body, name="small_sum_adamw", out_shape=[_sds(w.shape, F32)] * 4,
    )(gathered, w, m, v)


_NAMES = ("g_mix", "w_in", "g_sgu", "w_s", "b_s", "sinks", "rel_bias", "w_pa", "w_pb", "w_out",
          "g_ffn", "w_up", "w_conv", "b_conv", "w_down", "g_final")

def kernel(x, g_mix, w_in, g_sgu, w_s, b_s, sinks, rel_bias, w_pa, w_pb, w_out, g_ffn, w_up, w_conv, b_conv, w_down, g_final, loss_target, m_g_mix, m_w_in, m_g_sgu, m_w_s, m_b_s, m_sinks, m_rel_bias, m_w_pa, m_w_pb, m_w_out, m_g_ffn, m_w_up, m_w_conv, m_b_conv, m_w_down, m_g_final, v_g_mix, v_w_in, v_g_sgu, v_w_s, v_b_s, v_sinks, v_rel_bias, v_w_pa, v_w_pb, v_w_out, v_g_ffn, v_w_up, v_w_conv, v_b_conv, v_w_down, v_g_final):
    w = dict(g_mix=g_mix, w_in=w_in, g_sgu=g_sgu, w_s=w_s, b_s=b_s, sinks=sinks, rel_bias=rel_bias, w_pa=w_pa, w_pb=w_pb,
             w_out=w_out, g_ffn=g_ffn, w_up=w_up, w_conv=w_conv, b_conv=b_conv, w_down=w_down, g_final=g_final)
    m = dict(g_mix=m_g_mix, w_in=m_w_in, g_sgu=m_g_sgu, w_s=m_w_s, b_s=m_b_s, sinks=m_sinks, rel_bias=m_rel_bias, w_pa=m_w_pa,
             w_pb=m_w_pb, w_out=m_w_out, g_ffn=m_g_ffn, w_up=m_w_up, w_conv=m_w_conv, b_conv=m_b_conv, w_down=m_w_down,
             g_final=m_g_final)
    v = dict(g_mix=v_g_mix, w_in=v_w_in, g_sgu=v_g_sgu, w_s=v_w_s, b_s=v_b_s, sinks=v_sinks, rel_bias=v_rel_bias, w_pa=v_w_pa,
             w_pb=v_w_pb, w_out=v_w_out, g_ffn=v_g_ffn, w_up=v_w_up, w_conv=v_w_conv, b_conv=v_b_conv, w_down=v_w_down,
             g_final=v_g_final)
    xi, yi, ci = _mesh_pos()
    me = 2 * xi + yi

    shard = {n: w[n][0] for n in _BIG}
    shard_shapes = {n: shard[n].shape for n in _BIG}
    wc_shard = w["w_conv"][0]
    wc_pad = jnp.pad(wc_shard, ((0, 5), (0, 0)))
    stacks, wc_all = _allgather_weights({n: _own_slot(shard[n].astype(BF16), N_CHIPS, me) for n in _BIG},
                                        _own_slot(wc_pad, N_CHIPS, me))
    w_conv_full = jnp.concatenate([wc_all[i, :3] for i in range(N_CHIPS)], axis=1)
    w_in_full = stacks["w_in"].transpose(1, 0, 2).reshape(D_MODEL, -1)
    w_a = w_in_full[:, :A_DIM]
    w_b = w_in_full[:, A_DIM:A_DIM + B_DIM]
    w_g = w_in_full[:, A_DIM + B_DIM:]

    loss, grad_x, small, big = _local_step(
        x, loss_target, w["g_mix"], w["g_sgu"], w["w_s"][0], w["b_s"][0], w["sinks"], w["rel_bias"], w["g_ffn"],
        w["b_conv"], w["g_final"], w_g, w_a, w_b, stacks["w_pa"], stacks["w_pb"], stacks["w_out"].reshape(D_MODEL, D_MODEL),
        stacks["w_up"], w_conv_full, stacks["w_down"].reshape(D_FF, D_MODEL))

    small["loss"] = loss
    all_small = _allgather_small(_pack_small(small))
    sw = {n: (jnp.zeros((1, 1), F32) if n in ("loss", "w_conv") else w[n]) for n, _ in _SMALL}
    sm = {n: (jnp.zeros((1, 1), F32) if n in ("loss", "w_conv") else m[n]) for n, _ in _SMALL}
    sv = {n: (jnp.zeros((1, 1), F32) if n in ("loss", "w_conv") else v[n]) for n, _ in _SMALL}
    for d in (sw, sm, sv):
        d["w_conv"] = jnp.zeros((3, 2 * D_FF), F32)
    s_g, s_d, s_m, s_v = [_unpack_small(a) for a in _small_sum_adamw(all_small, _pack_small(sw), _pack_small(sm), _pack_small(sv))]

    pos = jnp.stack([me, ci])
    from_sibling = _pair_exchange(big)
    pair_sums = {n: _pair_add(big[n], from_sibling[n], n, pos) for n in _BIG}
    from_chips = _chip_exchange(pair_sums)
    g_big = _pair_share({n: _owner_sum(big[n], from_sibling[n], from_chips[n], n, pos, shard_shapes[n]) for n in _BIG})

    grads, deltas, new_m, new_v = {}, {}, {}, {}
    for n in _BIG:
        d, nm, nv = _adamw(shard[n], g_big[n], m[n][0], v[n][0], "adamw_" + n)
        grads[n], deltas[n], new_m[n], new_v[n] = g_big[n][None], d[None], nm[None], nv[None]
    wcols = wc_shard.shape[1]
    g_wc = lax.dynamic_slice(s_g["w_conv"], (0, me * wcols), (3, wcols))
    d, nm, nv = _adamw(wc_shard, g_wc, m["w_conv"][0], v["w_conv"][0], "adamw_w_conv")
    grads["w_conv"], deltas["w_conv"], new_m["w_conv"], new_v["w_conv"] = g_wc[None], d[None], nm[None], nv[None]
    for n, _ in _SMALL:
        if n in ("loss", "w_conv"):
            continue
        shp = w[n].shape
        grads[n], deltas[n], new_m[n], new_v[n] = (s_g[n].reshape(shp), s_d[n].reshape(shp), s_m[n].reshape(shp),
                                                    s_v[n].reshape(shp))

    return (s_g["loss"].reshape(()), grad_x, *[grads[n] for n in _NAMES], *[deltas[n] for n in _NAMES],
            *[new_m[n] for n in _NAMES], *[new_v[n] for n in _NAMES])
```

```python
import functools

import numpy as np
import jax
import jax.numpy as jnp
from jax import lax
from jax.experimental import pallas as pl
from jax.experimental.pallas import tpu as pltpu

F32 = jnp.float32
BF16 = jnp.bfloat16

D_MODEL = 1024
CHUNK = 128
A_GROUPS = 4
A_WIDTH = 512
N_HEADS = 8
HEAD_DIM = 64
Q_DIM = 512
KV_DIM = 128
N_BUCKETS = 32
MAX_DISTANCE = 128
D_FF = 2816
EPS = 1e-6
NEG_INF = -1e30
G_DIM = 2 * D_MODEL
A_DIM = 2 * A_WIDTH
B_DIM = Q_DIM + 2 * KV_DIM
LANES = 128
SUBLANES = 8
BF16_ROWS = 16
N_CHIPS = 4
N_DEV = 8

ADAM_LR = 0.001
ADAM_B1 = 0.9
ADAM_B2 = 0.999
ADAM_EPS = 1e-08
ADAM_WD = 0.01
ADAM_STEP = 10

MESH = pl.DeviceIdType.MESH
_GELU_C = 0.7978845608028654
_GELU_A = 0.044715


def _cp(sem=None, vmem_mb=None):
    kw = {}
    if sem is not None:
        kw["dimension_semantics"] = sem
    if vmem_mb is not None:
        kw["vmem_limit_bytes"] = vmem_mb << 20
    return pltpu.CompilerParams(**kw)


def _dot(a, b):
    return jnp.dot(a, b, preferred_element_type=F32)


def _dot_nt(a, b):
    return lax.dot_general(a, b, (((1,), (1,)), ((), ())), preferred_element_type=F32)


def _dot_tn(a, b):
    return lax.dot_general(a, b, (((0,), (0,)), ((), ())), preferred_element_type=F32)


def _rms_r(x):
    return lax.rsqrt(jnp.mean(x * x, axis=-1, keepdims=True) + EPS)


def _rms_bwd(dh, n, r, g):
    dn = dh * g
    return r * (dn - n * jnp.mean(dn * n, axis=-1, keepdims=True))


def _gelu(x):
    t = jnp.tanh(_GELU_C * (x + _GELU_A * (x * x * x)))
    return 0.5 * x * (1.0 + t), t


def _gelu_grad(x, t):
    return 0.5 * (1.0 + t) + 0.5 * x * (1.0 - t * t) * (_GELU_C * (1.0 + 3.0 * _GELU_A * x * x))


def _sigmoid(x):
    return 1.0 / (1.0 + jnp.exp(-x))


def _row(tm, w):
    return pl.BlockSpec((tm, w), lambda i: (i, 0))


def _full(shape):
    nd = len(shape)
    return pl.BlockSpec(tuple(shape), lambda *_: (0,) * nd)


def _resident(shape):
    nd = len(shape)
    return pl.BlockSpec(tuple(shape), lambda *_: (0,) * nd, pipeline_mode=pl.Buffered(1))


def _sds(shape, dtype):
    return jax.ShapeDtypeStruct(tuple(shape), dtype)


def _band_buckets():
    i = np.arange(CHUNK)[:, None]
    j = np.arange(2 * CHUNK)[None, :]
    dist = i + CHUNK - j
    valid = (dist >= 0) & (dist < CHUNK)
    d = np.clip(dist, 0, None)
    max_exact = N_BUCKETS // 2
    large = max_exact + (np.log(np.maximum(d, 1) / max_exact) / np.log(MAX_DISTANCE / max_exact)
                         * (N_BUCKETS - max_exact)).astype(np.int32)
    large = np.minimum(large, N_BUCKETS - 1)
    buckets = np.where(d < max_exact, d, large).astype(np.int32)
    return np.where(valid, buckets, -1).astype(np.int32)


def _inproj(x2, g_mix, w_g, w_a, w_b, tm):
    T = x2.shape[0]

    def body(x_ref, g_ref, wg_ref, wa_ref, wb_ref, pg_ref, pa_ref, pb_ref, h_ref):
        x = x_ref[...]
        h = (x * _rms_r(x) * g_ref[...]).astype(BF16)
        h_ref[...] = h
        pg_ref[...] = _dot(h, wg_ref[...]).astype(BF16)
        pa_ref[...] = _dot(h, wa_ref[...]).astype(BF16)
        pb_ref[...] = _dot(h, wb_ref[...]).astype(BF16)

    return pl.pallas_call(
        body, name="inproj", grid=(T // tm,),
        in_specs=[_row(tm, D_MODEL), _full(g_mix.shape), _resident(w_g.shape), _resident(w_a.shape), _resident(w_b.shape)],
        out_specs=[_row(tm, G_DIM), _row(tm, A_DIM), _row(tm, B_DIM), _row(tm, D_MODEL)],
        out_shape=[_sds((T, G_DIM), BF16), _sds((T, A_DIM), BF16), _sds((T, B_DIM), BF16), _sds((T, D_MODEL), BF16)],
        compiler_params=_cp(("arbitrary",), 48),
    )(x2, g_mix, w_g, w_a, w_b)


def _sgu_parts(p, g):
    pu = p[:, :A_WIDTH]
    pv = p[:, A_WIDTH:]
    u, tu = _gelu(pu)
    vv, tv = _gelu(pv)
    rv = _rms_r(vv)
    vn = (vv * rv * g).astype(BF16)
    return pu, pv, u, tu, vv, tv, rv, vn


def _tril():
    r = lax.broadcasted_iota(jnp.int32, (CHUNK, CHUNK), 0)
    c = lax.broadcasted_iota(jnp.int32, (CHUNK, CHUNK), 1)
    return r >= c


def _sgu_fwd(proj_a, g_sgu, w_s, b_st, tm):
    T = proj_a.shape[0]

    def body(p_ref, g_ref, ws_ref, bs_ref, y_ref):
        tril = _tril()
        _, _, u, _, _, _, _, vn = _sgu_parts(p_ref[...].astype(F32), g_ref[...])
        for gi in range(A_GROUPS):
            wm = jnp.where(tril, ws_ref[gi], 0.0).astype(BF16)
            bcol = bs_ref[:, gi:gi + 1]
            cs = slice(gi * CHUNK, (gi + 1) * CHUNK)
            for c in range(tm // CHUNK):
                rs = slice(c * CHUNK, (c + 1) * CHUNK)
                s = _dot(wm, vn[rs, cs]) + bcol
                y_ref[rs, cs] = (u[rs, cs] * s).astype(BF16)

    return pl.pallas_call(
        body, name="sgu_fwd", grid=(T // tm,),
        in_specs=[_row(tm, A_DIM), _full(g_sgu.shape), _full(w_s.shape), _full(b_st.shape)],
        out_specs=_row(tm, A_WIDTH), out_shape=_sds((T, A_WIDTH), BF16),
        compiler_params=_cp(("arbitrary",)),
    )(proj_a, g_sgu, w_s, b_st)


HEAD_ROWS = N_HEADS * CHUNK


def _head_rows(h):
    return slice(h * CHUNK, (h + 1) * CHUNK)


def _attn_setup(bias_scr, sink_scr, kvar_scr, qkv_ref, bk_ref, rel_ref, sink_ref):
    bk = bk_ref[...]
    for h in range(N_HEADS):
        acc = jnp.full((CHUNK, 2 * CHUNK), NEG_INF, F32)
        for b in range(N_BUCKETS):
            acc = jnp.where(bk == b, rel_ref[b, h], acc)
        bias_scr[_head_rows(h), :] = acc
        sink_scr[_head_rows(h), :] = jnp.full((CHUNK, LANES), sink_ref[0, h], F32)
    seq = qkv_ref.shape[0]
    rows_per = 2 * CHUNK
    for is_v in range(2):
        c0 = Q_DIM + is_v * KV_DIM
        for r in range(seq // rows_per):
            rs = slice(r * rows_per, (r + 1) * rows_per)
            a = qkv_ref[rs, c0:c0 + KV_DIM].astype(F32)
            lane = lax.broadcasted_iota(jnp.int32, a.shape, 1)
            lo = jnp.where(lane < HEAD_DIM, a, 0.0)
            hi = jnp.where(lane >= HEAD_DIM, a, 0.0)
            kvar_scr[4 * is_v + 0, rs, :] = lo.astype(BF16)
            kvar_scr[4 * is_v + 1, rs, :] = pltpu.roll(lo, HEAD_DIM, 1).astype(BF16)
            kvar_scr[4 * is_v + 2, rs, :] = pltpu.roll(hi, HEAD_DIM, 1).astype(BF16)
            kvar_scr[4 * is_v + 3, rs, :] = hi.astype(BF16)


def _rowsum(a, ones):
    hi = a.astype(BF16)
    lo = (a - hi.astype(F32)).astype(BF16)
    return _dot(hi, ones) + _dot(lo, ones)


def _both(a):
    return jnp.concatenate([a, a], axis=1)


def _attn_probs(qkv_ref, r0, n, kv, bias_scr, sink_scr, ones):
    s = jnp.concatenate([_dot_nt(qkv_ref[pl.ds(r0, CHUNK), (h // 2) * LANES:(h // 2 + 1) * LANES], kv[h // 4][h % 2])
                         for h in range(N_HEADS)], axis=0)
    s = s * (HEAD_DIM ** -0.5) + bias_scr[...]
    col = lax.broadcasted_iota(jnp.int32, s.shape, 1)
    s = jnp.where((col < CHUNK) & (n == 0), NEG_INF, s)
    sink = sink_scr[...]
    m = jnp.maximum(jnp.max(s, axis=-1, keepdims=True), sink)
    p = jnp.exp(s - _both(m))
    es = jnp.exp(sink - m)
    inv = 1.0 / (_rowsum(p, ones) + es)
    return p * _both(inv), es * inv


def _attn_block_inputs(kvar_scr, n):
    r0 = pl.multiple_of(n * CHUNK, CHUNK)
    rp = pl.multiple_of(jnp.maximum(n - 1, 0) * CHUNK, CHUNK)

    def both(idx):
        return jnp.concatenate([kvar_scr[idx, pl.ds(rp, CHUNK), :], kvar_scr[idx, pl.ds(r0, CHUNK), :]], axis=0)

    kv = ((both(0), both(1)), (both(2), both(3)))
    vv = ((both(4), both(5)), (both(6), both(7)))
    return r0, kv, vv


def _attn_fwd(proj_b, sinks, rel_bias, n_seq, seq):
    nb = seq // CHUNK
    bk = jnp.asarray(_band_buckets())

    def body(qkv_ref, bk_ref, rel_ref, sink_ref, o_ref, bias_scr, sink_scr, kvar_scr):
        _attn_setup(bias_scr, sink_scr, kvar_scr, qkv_ref, bk_ref, rel_ref, sink_ref)
        ones = jnp.ones((2 * CHUNK, LANES), BF16)

        def blk(n, carry):
            r0, kv, vv = _attn_block_inputs(kvar_scr, n)
            prob, _ = _attn_probs(qkv_ref, r0, n, kv, bias_scr, sink_scr, ones)
            pb = prob.astype(BF16)
            for pr in range(N_HEADS // 2):
                acc = _dot(pb[_head_rows(2 * pr)], vv[pr // 2][0]) + _dot(pb[_head_rows(2 * pr + 1)], vv[pr // 2][1])
                o_ref[pl.ds(r0, CHUNK), pr * LANES:(pr + 1) * LANES] = acc.astype(BF16)
            return carry

        lax.fori_loop(0, nb, blk, 0)

    smem = pl.BlockSpec(memory_space=pltpu.SMEM)
    return pl.pallas_call(
        body, name="attn_fwd", grid=(n_seq,),
        in_specs=[_row(seq, B_DIM), _full(bk.shape), smem, smem],
        out_specs=_row(seq, Q_DIM), out_shape=_sds((n_seq * seq, Q_DIM), BF16),
        scratch_shapes=[pltpu.VMEM((HEAD_ROWS, 2 * CHUNK), F32), pltpu.VMEM((HEAD_ROWS, LANES), F32),
                        pltpu.VMEM((8, seq, KV_DIM), BF16)],
        compiler_params=_cp(("arbitrary",), 40),
    )(proj_b, bk, rel_bias, sinks)


def _dot_stacked(a, w_ref):
    return jnp.concatenate([_dot(a, w_ref[i]) for i in range(N_CHIPS)], axis=1)


def _dot_nt_stacked(a, w_ref):
    w = w_ref.shape[2]
    acc = _dot_nt(a[:, :w], w_ref[0])
    for i in range(1, N_CHIPS):
        acc = acc + _dot_nt(a[:, i * w:(i + 1) * w], w_ref[i])
    return acc


def _merge_fwd(x2, y_a, y_b, proj_g, w_pa, w_pb, w_out, tm):
    T = x2.shape[0]

    def body(x_ref, ya_ref, yb_ref, g_ref, wpa_ref, wpb_ref, wo_ref, x1_ref, mg_ref):
        g = g_ref[...].astype(F32)
        pa = _dot_stacked(ya_ref[...], wpa_ref)
        pb = _dot_stacked(yb_ref[...], wpb_ref)
        merged = (_sigmoid(g[:, :D_MODEL]) * pa + _sigmoid(g[:, D_MODEL:]) * pb).astype(BF16)
        mg_ref[...] = merged
        x1_ref[...] = x_ref[...] + _dot(merged, wo_ref[...])

    return pl.pallas_call(
        body, name="merge_fwd", grid=(T // tm,),
        in_specs=[_row(tm, D_MODEL), _row(tm, A_WIDTH), _row(tm, Q_DIM), _row(tm, G_DIM),
                  _resident(w_pa.shape), _resident(w_pb.shape), _resident(w_out.shape)],
        out_specs=[_row(tm, D_MODEL), _row(tm, D_MODEL)],
        out_shape=[_sds((T, D_MODEL), F32), _sds((T, D_MODEL), BF16)],
        compiler_params=_cp(("arbitrary",), 40),
    )(x2, y_a, y_b, proj_g, w_pa, w_pb, w_out)


def _upproj(x1, g_ffn, w_up, tm):
    T = x1.shape[0]
    cw = w_up.shape[2]

    def body(x_ref, g_ref, w_ref, u_ref, h_ref):
        x = x_ref[...]
        h = (x * _rms_r(x) * g_ref[...]).astype(BF16)
        h_ref[...] = h
        for i in range(N_CHIPS):
            u_ref[:, i * cw:(i + 1) * cw] = _dot(h, w_ref[i]).astype(BF16)

    return pl.pallas_call(
        body, name="upproj", grid=(T // tm,),
        in_specs=[_row(tm, D_MODEL), _full(g_ffn.shape), _resident(w_up.shape)],
        out_specs=[_row(tm, 2 * D_FF), _row(tm, D_MODEL)],
        out_shape=[_sds((T, 2 * D_FF), BF16), _sds((T, D_MODEL), BF16)],
        compiler_params=_cp(("arbitrary",), 56),
    )(x1, g_ffn, w_up)


def _shift_down(u, halo, k):
    rolled = pltpu.roll(u, k, 0)
    head = rolled[:SUBLANES]
    row = lax.broadcasted_iota(jnp.int32, head.shape, 0)
    if k == 1:
        head = jnp.where(row == 0, halo[1:2], head)
    else:
        head = jnp.where(row == 0, halo[0:1], jnp.where(row == 1, halo[1:2], head))
    return jnp.concatenate([head, rolled[SUBLANES:]], axis=0)


def _shift_up(d, halo, k):
    tm = d.shape[0]
    rolled = pltpu.roll(d, tm - k, 0)
    tail = rolled[tm - SUBLANES:]
    row = lax.broadcasted_iota(jnp.int32, tail.shape, 0)
    if k == 1:
        tail = jnp.where(row == SUBLANES - 1, halo[0:1], tail)
    else:
        tail = jnp.where(row == SUBLANES - 2, halo[0:1], jnp.where(row == SUBLANES - 1, halo[1:2], tail))
    return jnp.concatenate([rolled[:tm - SUBLANES], tail], axis=0)


def _conv_taps(u_ref, halo_ref, cols, at_start):
    u = u_ref[:, cols].astype(F32)
    hl = halo_ref[:, cols].astype(F32)[BF16_ROWS - 2:BF16_ROWS]
    hl = jnp.where(at_start, 0.0, hl)
    return u, _shift_down(u, hl, 1), _shift_down(u, hl, 2)


def _conv_out(taps, wc, bc):
    u, u1, u2 = taps
    return wc[0:1] * u2 + wc[1:2] * u1 + wc[2:3] * u + bc


def _prev_halo_spec(tm, width, col_block=None):
    k = tm // BF16_ROWS
    if col_block is None:
        return pl.BlockSpec((BF16_ROWS, width), lambda i: (jnp.maximum(i * k - 1, 0), 0))
    return pl.BlockSpec((BF16_ROWS, width), lambda j, i: (jnp.maximum(i * k - 1, 0), col_block(j)))


def _ffn_down_loss(upre, x1, target, w_conv, b_conv, w_down, g_final, tm, seq):
    T = x1.shape[0]
    tiles_per_seq = seq // tm
    half = D_FF // 2

    def body(u_ref, hl_ref, x1_ref, t_ref, wc_ref, bc_ref, wd_ref, g_ref, dx2_ref, loss_ref, gg_ref, gate_ref, val_ref):
        i = pl.program_id(0)
        at_start = (i % tiles_per_seq) == 0
        acc = jnp.zeros((tm, D_MODEL), F32)
        for j in range(2):
            gc = slice(j * half, (j + 1) * half)
            vc = slice(D_FF + j * half, D_FF + (j + 1) * half)
            gate = _conv_out(_conv_taps(u_ref, hl_ref, gc, at_start), wc_ref[:, gc], bc_ref[:, gc])
            val = _conv_out(_conv_taps(u_ref, hl_ref, vc, at_start), wc_ref[:, vc], bc_ref[:, vc])
            gate_ref[:, gc] = gate.astype(BF16)
            val_ref[:, gc] = val.astype(BF16)
            act = (gate * _sigmoid(gate) * val).astype(BF16)
            acc = acc + _dot(act, wd_ref[gc, :])
        x2 = x1_ref[...] + acc
        r = _rms_r(x2)
        n = x2 * r
        g = g_ref[...]
        diff = n * g - t_ref[...]
        dy = diff * (1.0 / D_MODEL)
        dx2_ref[...] = _rms_bwd(dy, n, r, g)

        @pl.when(i == 0)
        def _():
            loss_ref[...] = jnp.zeros_like(loss_ref)
            gg_ref[...] = jnp.zeros_like(gg_ref)

        loss_ref[...] += 0.5 * jnp.sum(jnp.mean(diff * diff, axis=-1, keepdims=True), axis=0, keepdims=True)
        gg_ref[...] += jnp.sum(dy * n, axis=0, keepdims=True)

    return pl.pallas_call(
        body, name="ffn_down_loss", grid=(T // tm,),
        in_specs=[_row(tm, 2 * D_FF), _prev_halo_spec(tm, 2 * D_FF), _row(tm, D_MODEL), _row(tm, D_MODEL),
                  _full(w_conv.shape), _full(b_conv.shape), _resident(w_down.shape), _full(g_final.shape)],
        out_specs=[_row(tm, D_MODEL), _full((1, 1)), _full((1, D_MODEL)), _row(tm, D_FF), _row(tm, D_FF)],
        out_shape=[_sds((T, D_MODEL), F32), _sds((1, 1), F32), _sds((1, D_MODEL), F32),
                   _sds((T, D_FF), BF16), _sds((T, D_FF), BF16)],
        compiler_params=_cp(("arbitrary",), 56),
    )(upre, upre, x1, target, w_conv, b_conv, w_down, g_final)


def _ffn_bwd_act(gate, val, dx2, w_down, tm):
    T = dx2.shape[0]
    half = D_FF // 2
    nt = T // tm

    def body(g_ref, v_ref, dx_ref, wd_ref, dg_ref, dv_ref, gwd_out, gbg_ref, gbv_ref, gwd_ref):
        i = pl.program_id(1)
        gate = g_ref[...].astype(F32)
        val = v_ref[...].astype(F32)
        sg = _sigmoid(gate)
        silu = gate * sg
        dx = dx_ref[...].astype(BF16)
        d_act = _dot_nt(dx, wd_ref[...])
        d_val = d_act * silu
        d_gate = d_act * val * (sg * (1.0 + gate * (1.0 - sg)))
        dg_ref[...] = d_gate.astype(BF16)
        dv_ref[...] = d_val.astype(BF16)

        @pl.when(i == 0)
        def _():
            for r in (gwd_ref, gbg_ref, gbv_ref):
                r[...] = jnp.zeros_like(r)

        gwd_ref[...] += _dot_tn((silu * val).astype(BF16), dx)
        gbg_ref[...] += jnp.sum(d_gate, axis=0, keepdims=True)
        gbv_ref[...] += jnp.sum(d_val, axis=0, keepdims=True)

        @pl.when(i == nt - 1)
        def _():
            gwd_out[...] = gwd_ref[...].astype(BF16)

    tile = pl.BlockSpec((tm, half), lambda j, i: (i, j))
    vec = pl.BlockSpec((1, half), lambda j, i: (0, j))
    wrows = pl.BlockSpec((half, D_MODEL), lambda j, i: (j, 0))
    return pl.pallas_call(
        body, name="ffn_bwd_act", grid=(2, nt),
        in_specs=[tile, tile, pl.BlockSpec((tm, D_MODEL), lambda j, i: (i, 0)), wrows],
        out_specs=[tile, tile, wrows, vec, vec],
        out_shape=[_sds((T, D_FF), BF16), _sds((T, D_FF), BF16), _sds((D_FF, D_MODEL), BF16),
                   _sds((1, D_FF), F32), _sds((1, D_FF), F32)],
        scratch_shapes=[pltpu.VMEM((half, D_MODEL), F32)],
        compiler_params=_cp(("arbitrary", "arbitrary"), 56),
    )(gate, val, dx2, w_down)


def _ffn_bwd_up(d_gate, d_val, upre, dx2, x1, g_ffn, w_conv, w_up, tm, seq):
    T = dx2.shape[0]
    tiles_per_seq = seq // tm
    k16 = tm // BF16_ROWS
    n16 = T // BF16_ROWS
    cw = D_FF // 2

    def body(dg_ref, dv_ref, hg_ref, hv_ref, u_ref, dx2_ref, x1_ref, g_ref, wc_ref, wu_ref, du_ref, dx1_ref, gg_ref, gwc_ref):
        i = pl.program_id(0)
        at_end = (i % tiles_per_seq) == tiles_per_seq - 1

        @pl.when(i == 0)
        def _():
            gg_ref[...] = jnp.zeros_like(gg_ref)
            gwc_ref[...] = jnp.zeros_like(gwc_ref)

        dh = jnp.zeros((tm, D_MODEL), F32)
        for j in range(4):
            src, hsrc = (dg_ref, hg_ref) if j < 2 else (dv_ref, hv_ref)
            ls = slice((j % 2) * cw, (j % 2 + 1) * cw)
            cs = slice(j * cw, (j + 1) * cw)
            d = src[:, ls].astype(F32)
            hl = hsrc[:, ls].astype(F32)[0:2]
            hl = jnp.where(at_end, 0.0, hl)
            wc = wc_ref[:, cs]
            d1 = _shift_up(d, hl, 1)
            d2 = _shift_up(d, hl, 2)
            du = (wc[2:3] * d + wc[1:2] * d1 + wc[0:1] * d2).astype(BF16)
            du_ref[:, cs] = du
            dh = dh + _dot_nt(du, wu_ref[j])
            u = u_ref[:, cs].astype(F32)
            gwc_ref[0:1, cs] += jnp.sum(d2 * u, axis=0, keepdims=True)
            gwc_ref[1:2, cs] += jnp.sum(d1 * u, axis=0, keepdims=True)
            gwc_ref[2:3, cs] += jnp.sum(d * u, axis=0, keepdims=True)
        x = x1_ref[...]
        r = _rms_r(x)
        n = x * r
        dx1_ref[...] = dx2_ref[...] + _rms_bwd(dh, n, r, g_ref[...])
        gg_ref[...] += jnp.sum(dh * n, axis=0, keepdims=True)

    nxt = pl.BlockSpec((BF16_ROWS, D_FF), lambda i: (jnp.minimum((i + 1) * k16, n16 - 1), 0))
    return pl.pallas_call(
        body, name="ffn_bwd_up", grid=(T // tm,),
        in_specs=[_row(tm, D_FF), _row(tm, D_FF), nxt, nxt, _row(tm, 2 * D_FF), _row(tm, D_MODEL), _row(tm, D_MODEL),
                  _full(g_ffn.shape), _full(w_conv.shape), _resident(w_up.shape)],
        out_specs=[_row(tm, 2 * D_FF), _row(tm, D_MODEL), _full((1, D_MODEL)), _full((3, 2 * D_FF))],
        out_shape=[_sds((T, 2 * D_FF), BF16), _sds((T, D_MODEL), F32), _sds((1, D_MODEL), F32), _sds((3, 2 * D_FF), F32)],
        compiler_params=_cp(("arbitrary",), 56),
    )(d_gate, d_val, d_gate, d_val, upre, dx2, x1, g_ffn, w_conv, w_up)


def _matmul_tn(a, b, tn, tk, name):
    T, M = a.shape
    N = b.shape[1]
    nk = T // tk

    def body(a_ref, b_ref, o_ref, acc_ref):
        k = pl.program_id(1)

        @pl.when(k == 0)
        def _():
            acc_ref[...] = jnp.zeros_like(acc_ref)

        acc_ref[...] += _dot_tn(a_ref[...], b_ref[...])

        @pl.when(k == nk - 1)
        def _():
            o_ref[...] = acc_ref[...].astype(BF16)

    return pl.pallas_call(
        body, name=name, grid=(N // tn, nk),
        in_specs=[pl.BlockSpec((tk, M), lambda j, k: (k, 0)), pl.BlockSpec((tk, tn), lambda j, k: (k, j))],
        out_specs=pl.BlockSpec((M, tn), lambda j, k: (0, j)), out_shape=_sds((M, N), BF16),
        scratch_shapes=[pltpu.VMEM((M, tn), F32)],
        compiler_params=_cp(("arbitrary", "arbitrary"), 48),
    )(a, b)


def _merge_bwd(dx1, merged, y_a, y_b, proj_g, w_pa, w_pb, w_out, tm):
    T = dx1.shape[0]

    nt = T // tm
    pshape = (A_WIDTH, D_MODEL)

    def body(dx_ref, mg_ref, ya_ref, yb_ref, g_ref, wpa_ref, wpb_ref, wo_ref,
             dg_ref, dya_ref, dyb_ref, gwo_out, gwpa_out, gwpb_out, gwo_ref, gwpa_ref, gwpb_ref):
        i = pl.program_id(0)
        dx = dx_ref[...].astype(BF16)
        dm = _dot_nt(dx, wo_ref[...])
        g = g_ref[...].astype(F32)
        ya = ya_ref[...]
        yb = yb_ref[...]
        pa = _dot_stacked(ya, wpa_ref)
        pb = _dot_stacked(yb, wpb_ref)
        sa = _sigmoid(g[:, :D_MODEL])
        sb = _sigmoid(g[:, D_MODEL:])
        dpa = (dm * sa).astype(BF16)
        dpb = (dm * sb).astype(BF16)
        dg_ref[:, :D_MODEL] = (dm * pa * (sa * (1.0 - sa))).astype(BF16)
        dg_ref[:, D_MODEL:] = (dm * pb * (sb * (1.0 - sb))).astype(BF16)
        dya_ref[...] = _dot_nt_stacked(dpa, wpa_ref).astype(BF16)
        dyb_ref[...] = _dot_nt_stacked(dpb, wpb_ref).astype(BF16)

        @pl.when(i == 0)
        def _():
            for r in (gwo_ref, gwpa_ref, gwpb_ref):
                r[...] = jnp.zeros_like(r)

        gwo_ref[...] += _dot_tn(mg_ref[...], dx)
        gwpa_ref[...] += _dot_tn(ya, dpa)
        gwpb_ref[...] += _dot_tn(yb, dpb)

        @pl.when(i == nt - 1)
        def _():
            gwo_out[...] = gwo_ref[...].astype(BF16)
            gwpa_out[...] = gwpa_ref[...].astype(BF16)
            gwpb_out[...] = gwpb_ref[...].astype(BF16)

    return pl.pallas_call(
        body, name="merge_bwd", grid=(nt,),
        in_specs=[_row(tm, D_MODEL), _row(tm, D_MODEL), _row(tm, A_WIDTH), _row(tm, Q_DIM), _row(tm, G_DIM),
                  _resident(w_pa.shape), _resident(w_pb.shape), _resident(w_out.shape)],
        out_specs=[_row(tm, G_DIM), _row(tm, A_WIDTH), _row(tm, Q_DIM),
                   _full(w_out.shape), _full(pshape), _full(pshape)],
        out_shape=[_sds((T, G_DIM), BF16), _sds((T, A_WIDTH), BF16), _sds((T, Q_DIM), BF16),
                   _sds(w_out.shape, BF16), _sds(pshape, BF16), _sds(pshape, BF16)],
        scratch_shapes=[pltpu.VMEM(w_out.shape, F32), pltpu.VMEM(pshape, F32), pltpu.VMEM(pshape, F32)],
        compiler_params=_cp(("arbitrary",), 56),
    )(dx1, merged, y_a, y_b, proj_g, w_pa, w_pb, w_out)


def _sgu_bwd(proj_a, d_ya, g_sgu, w_s, b_st, tm):
    T = proj_a.shape[0]

    def body(p_ref, dy_ref, g_ref, ws_ref, bs_ref, dp_ref, gws_ref, gbs_ref, gg_ref):
        tril = _tril()
        g = g_ref[...]
        pu, pv, u, tu, vv, tv, rv, vn = _sgu_parts(p_ref[...].astype(F32), g)
        dy = dy_ref[...].astype(F32)

        @pl.when(pl.program_id(0) == 0)
        def _():
            for r in (gws_ref, gbs_ref, gg_ref):
                r[...] = jnp.zeros_like(r)

        du_cols = []
        dvn_cols = []
        for gi in range(A_GROUPS):
            wm = jnp.where(tril, ws_ref[gi], 0.0).astype(BF16)
            wmt = wm.astype(F32).T.astype(BF16)
            bcol = bs_ref[:, gi:gi + 1]
            cs = slice(gi * CHUNK, (gi + 1) * CHUNK)
            du_rows = []
            dvn_rows = []
            gw = jnp.zeros((CHUNK, CHUNK), F32)
            gb = jnp.zeros((CHUNK, 1), F32)
            for c in range(tm // CHUNK):
                rs = slice(c * CHUNK, (c + 1) * CHUNK)
                vn_c = vn[rs, cs]
                s = _dot(wm, vn_c) + bcol
                dy_c = dy[rs, cs]
                ds = dy_c * u[rs, cs]
                du_rows.append(dy_c * s)
                dsb = ds.astype(BF16)
                gw = gw + _dot_nt(dsb, vn_c)
                gb = gb + jnp.sum(ds, axis=-1, keepdims=True)
                dvn_rows.append(_dot(wmt, dsb))
            gws_ref[gi] += jnp.where(tril, gw, 0.0)
            gbs_ref[:, gi:gi + 1] += gb
            du_cols.append(jnp.concatenate(du_rows, axis=0))
            dvn_cols.append(jnp.concatenate(dvn_rows, axis=0))
        du = jnp.concatenate(du_cols, axis=1)
        dvn = jnp.concatenate(dvn_cols, axis=1)
        vhat = vv * rv
        gg_ref[...] += jnp.sum(dvn * vhat, axis=0, keepdims=True)
        dvv = _rms_bwd(dvn, vhat, rv, g)
        dp_ref[:, :A_WIDTH] = (du * _gelu_grad(pu, tu)).astype(BF16)
        dp_ref[:, A_WIDTH:] = (dvv * _gelu_grad(pv, tv)).astype(BF16)

    return pl.pallas_call(
        body, name="sgu_bwd", grid=(T // tm,),
        in_specs=[_row(tm, A_DIM), _row(tm, A_WIDTH), _full(g_sgu.shape), _full(w_s.shape), _full(b_st.shape)],
        out_specs=[_row(tm, A_DIM), _full(w_s.shape), _full(b_st.shape), _full(g_sgu.shape)],
        out_shape=[_sds((T, A_DIM), BF16), _sds(w_s.shape, F32), _sds(b_st.shape, F32), _sds(g_sgu.shape, F32)],
        compiler_params=_cp(("arbitrary",)),
    )(proj_a, d_ya, g_sgu, w_s, b_st)


def _attn_bwd(proj_b, d_yb, sinks, rel_bias, n_seq, seq):
    nb = seq // CHUNK
    bk = jnp.asarray(_band_buckets())

    def body(qkv_ref, do_ref, bk_ref, rel_ref, sink_ref, d_ref, gs_ref, gr_ref,
             bias_scr, sink_scr, kvar_scr, dbias_scr, dk_scr, dv_scr, ds_scr):
        b = pl.program_id(0)
        _attn_setup(bias_scr, sink_scr, kvar_scr, qkv_ref, bk_ref, rel_ref, sink_ref)
        ones = jnp.ones((2 * CHUNK, LANES), BF16)

        @pl.when(b == 0)
        def _():
            dbias_scr[...] = jnp.zeros_like(dbias_scr)
            ds_scr[...] = jnp.zeros_like(ds_scr)

        dk_scr[...] = jnp.zeros_like(dk_scr)
        dv_scr[...] = jnp.zeros_like(dv_scr)

        def transposed(a):
            return a.astype(F32).T.astype(BF16)

        def blk(n, carry):
            r0, kv, vv = _attn_block_inputs(kvar_scr, n)
            prob, psink = _attn_probs(qkv_ref, r0, n, kv, bias_scr, sink_scr, ones)
            dp = jnp.concatenate([_dot_nt(do_ref[pl.ds(r0, CHUNK), (h // 2) * LANES:(h // 2 + 1) * LANES], vv[h // 4][h % 2])
                                  for h in range(N_HEADS)], axis=0)
            delta = _rowsum(prob * dp, ones)
            dsc = prob * (dp - _both(delta))
            ds_scr[...] += psink * delta
            dbias_scr[...] += dsc
            dsb = (dsc * (HEAD_DIM ** -0.5)).astype(BF16)
            pb = prob.astype(BF16)
            dkt = [jnp.zeros((HEAD_DIM, 2 * CHUNK), F32) for _ in range(2)]
            dvt = [jnp.zeros((HEAD_DIM, 2 * CHUNK), F32) for _ in range(2)]
            for pr in range(N_HEADS // 2):
                ps = slice(pr * LANES, (pr + 1) * LANES)
                qpt = transposed(qkv_ref[pl.ds(r0, CHUNK), ps])
                dopt = transposed(do_ref[pl.ds(r0, CHUNK), ps])
                kvh = pr // 2
                dq = jnp.zeros((CHUNK, LANES), F32)
                for hh in range(2):
                    hr = _head_rows(2 * pr + hh)
                    rows = slice(hh * HEAD_DIM, (hh + 1) * HEAD_DIM)
                    dq = dq + _dot(dsb[hr], kv[kvh][hh])
                    dkt[kvh] = dkt[kvh] + _dot(qpt, dsb[hr])[rows]
                    dvt[kvh] = dvt[kvh] + _dot(dopt, pb[hr])[rows]
                d_ref[pl.ds(r0, CHUNK), ps] = dq.astype(BF16)
            dk_scr[:, pl.ds(r0, 2 * CHUNK)] += jnp.concatenate(dkt, axis=0)
            dv_scr[:, pl.ds(r0, 2 * CHUNK)] += jnp.concatenate(dvt, axis=0)
            return carry

        lax.fori_loop(0, nb, blk, 0)
        for n in range(nb):
            rows = slice(n * CHUNK, (n + 1) * CHUNK)
            cols = slice((n + 1) * CHUNK, (n + 2) * CHUNK)
            d_ref[rows, Q_DIM:Q_DIM + KV_DIM] = dk_scr[:, cols].T.astype(BF16)
            d_ref[rows, Q_DIM + KV_DIM:] = dv_scr[:, cols].T.astype(BF16)

        @pl.when(b == n_seq - 1)
        def _():
            bkv = bk_ref[...]
            for h in range(N_HEADS):
                gs_ref[0:1, h:h + 1] = -jnp.sum(ds_scr[_head_rows(h), 0:1], axis=0, keepdims=True)
                db = dbias_scr[_head_rows(h), :]
                for bb in range(N_BUCKETS):
                    part = jnp.sum(jnp.where(bkv == bb, db, 0.0), axis=-1, keepdims=True)
                    gr_ref[bb:bb + 1, h:h + 1] = jnp.sum(part, axis=0, keepdims=True)

    smem = pl.BlockSpec(memory_space=pltpu.SMEM)
    return pl.pallas_call(
        body, name="attn_bwd", grid=(n_seq,),
        in_specs=[_row(seq, B_DIM), _row(seq, Q_DIM), _full(bk.shape), smem, smem],
        out_specs=[_row(seq, B_DIM), _full((1, N_HEADS)), _full((N_BUCKETS, N_HEADS))],
        out_shape=[_sds((n_seq * seq, B_DIM), BF16), _sds((1, N_HEADS), F32), _sds((N_BUCKETS, N_HEADS), F32)],
        scratch_shapes=[pltpu.VMEM((HEAD_ROWS, 2 * CHUNK), F32), pltpu.VMEM((HEAD_ROWS, LANES), F32),
                        pltpu.VMEM((8, seq, KV_DIM), BF16), pltpu.VMEM((HEAD_ROWS, 2 * CHUNK), F32),
                        pltpu.VMEM((KV_DIM, seq + CHUNK), F32), pltpu.VMEM((KV_DIM, seq + CHUNK), F32),
                        pltpu.VMEM((HEAD_ROWS, LANES), F32)],
        compiler_params=_cp(("arbitrary",), 40),
    )(proj_b, d_yb, bk, rel_bias, sinks)


def _inproj_bwd(d_g, d_a, d_b, x2, dx1, g_mix, w_g, w_a, w_b, tm):
    T = x2.shape[0]

    def body(dg_ref, da_ref, db_ref, x_ref, dx1_ref, g_ref, wg_ref, wa_ref, wb_ref, gx_ref, gg_ref):
        dh = _dot_nt(dg_ref[...], wg_ref[...]) + _dot_nt(da_ref[...], wa_ref[...]) + _dot_nt(db_ref[...], wb_ref[...])
        x = x_ref[...]
        r = _rms_r(x)
        n = x * r
        gx_ref[...] = dx1_ref[...] + _rms_bwd(dh, n, r, g_ref[...])

        @pl.when(pl.program_id(0) == 0)
        def _():
            gg_ref[...] = jnp.zeros_like(gg_ref)

        gg_ref[...] += jnp.sum(dh * n, axis=0, keepdims=True)

    return pl.pallas_call(
        body, name="inproj_bwd", grid=(T // tm,),
        in_specs=[_row(tm, G_DIM), _row(tm, A_DIM), _row(tm, B_DIM), _row(tm, D_MODEL), _row(tm, D_MODEL),
                  _full(g_mix.shape), _resident(w_g.shape), _resident(w_a.shape), _resident(w_b.shape)],
        out_specs=[_row(tm, D_MODEL), _full((1, D_MODEL))],
        out_shape=[_sds((T, D_MODEL), F32), _sds((1, D_MODEL), F32)],
        compiler_params=_cp(("arbitrary",), 48),
    )(d_g, d_a, d_b, x2, dx1, g_mix, w_g, w_a, w_b)


def _local_step(x, target, g_mix, g_sgu, w_s, b_s, sinks, rel_bias, g_ffn, b_conv, g_final,
                w_g, w_a, w_b, w_pa, w_pb, w_out, w_up, w_conv, w_down):
    n_seq, seq, _ = x.shape
    T = n_seq * seq
    tm = min(256, seq)
    x2 = x.reshape(T, D_MODEL)
    tgt = target.reshape(T, D_MODEL)
    b_st = b_s.T
    g_fin = g_final.reshape(1, D_MODEL)

    proj_g, proj_a, proj_b, h = _inproj(x2, g_mix, w_g, w_a, w_b, tm)
    y_a = _sgu_fwd(proj_a, g_sgu, w_s, b_st, tm)
    y_b = _attn_fwd(proj_b, sinks, rel_bias, n_seq, seq)
    x1, merged = _merge_fwd(x2, y_a, y_b, proj_g, w_pa, w_pb, w_out, tm)
    upre, h2 = _upproj(x1, g_ffn, w_up, tm)
    dx2, loss, gg_final, gate, val = _ffn_down_loss(upre, x1, tgt, w_conv, b_conv, w_down, g_fin, tm, seq)

    d_gate, d_val, gw_down, gb_g, gb_v = _ffn_bwd_act(gate, val, dx2, w_down, tm)
    gb_conv = jnp.concatenate([gb_g, gb_v], axis=1)
    d_upre, dx1, gg_ffn, gw_conv = _ffn_bwd_up(d_gate, d_val, upre, dx2, x1, g_ffn, w_conv, w_up, tm, seq)
    gw_up = _matmul_tn(h2, d_upre, 2 * D_FF // 4, min(512, T), "grad_w_up")
    d_g, d_ya, d_yb, gw_out, gw_pa, gw_pb = _merge_bwd(dx1, merged, y_a, y_b, proj_g, w_pa, w_pb, w_out, tm)
    d_a, gw_s, gb_st, gg_sgu = _sgu_bwd(proj_a, d_ya, g_sgu, w_s, b_st, tm)
    d_b, g_sinks, g_rel = _attn_bwd(proj_b, d_yb, sinks, rel_bias, n_seq, seq)
    grad_x, gg_mix = _inproj_bwd(d_g, d_a, d_b, x2, dx1, g_mix, w_g, w_a, w_b, tm)
    gw_g = _matmul_tn(h, d_g, D_MODEL, min(512, T), "grad_w_in_gate")
    gw_a = _matmul_tn(h, d_a, A_DIM, min(512, T), "grad_w_in_a")
    gw_b = _matmul_tn(h, d_b, B_DIM, min(512, T), "grad_w_in_b")
    gw_in = jnp.concatenate([gw_a, gw_b, gw_g], axis=1).reshape(D_MODEL, N_CHIPS, -1).transpose(1, 0, 2)

    small = dict(g_mix=gg_mix, g_sgu=gg_sgu, w_s=gw_s, b_s=gb_st.T, sinks=g_sinks, rel_bias=g_rel,
                 g_ffn=gg_ffn, b_conv=gb_conv, g_final=gg_final, w_conv=gw_conv)
    big = dict(w_in=gw_in, w_pa=gw_pa, w_pb=gw_pb, w_out=gw_out, w_up=gw_up, w_down=gw_down)
    return loss, grad_x.reshape(x.shape), small, big


_BIG = ("w_in", "w_pa", "w_pb", "w_out", "w_up", "w_down")

_SMALL = (("loss", (1, 1)), ("g_final", (1, D_MODEL)), ("g_mix", (1, D_MODEL)), ("g_ffn", (1, D_MODEL)),
          ("g_sgu", (1, A_WIDTH)), ("b_s", (A_GROUPS, CHUNK)), ("sinks", (1, N_HEADS)), ("rel_bias", (N_BUCKETS, N_HEADS)),
          ("b_conv", (1, 2 * D_FF)), ("w_conv", (3, 2 * D_FF)), ("w_s", (A_GROUPS, CHUNK, CHUNK)))
SMALL_ROWS = 96


def _pack_small(vals):
    flat = jnp.concatenate([vals[n].astype(F32).reshape(-1) for n, _ in _SMALL])
    flat = jnp.pad(flat, (0, SMALL_ROWS * D_MODEL - flat.shape[0]))
    return flat.reshape(SMALL_ROWS, D_MODEL)


def _unpack_small(buf):
    flat = buf.reshape(-1)
    out = {}
    off = 0
    for n, shp in _SMALL:
        k = int(np.prod(shp))
        out[n] = flat[off:off + k].reshape(shp)
        off += k
    return out


HBM = pl.BlockSpec(memory_space=pltpu.HBM)


def _mesh_pos():
    return lax.axis_index("x"), lax.axis_index("y"), lax.axis_index("c")


def _other_chips(x, y):
    return [(1 - x, y), (x, 1 - y), (1 - x, 1 - y)]


def _remote(src, dst, send_sem, recv_sem, to):
    return pltpu.make_async_remote_copy(src_ref=src, dst_ref=dst, send_sem=send_sem, recv_sem=recv_sem,
                                        device_id=to, device_id_type=MESH)


def _own_slot(own, n, at):
    return lax.dynamic_update_slice(lax.empty((n,) + own.shape, own.dtype), own[None], (at,) + (0,) * own.ndim)


def _allgather_weights(stacks, wc_stack):
    names = list(stacks)
    n = len(names)

    def body(*refs):
        ins, outs = refs[:n + 1], refs[n + 1:2 * n + 2]
        send_sems, recv_sems = refs[2 * n + 2:]
        x, y, c = _mesh_pos()
        me = 2 * x + y
        sibling = (x, y, 1 - c)
        chips = _other_chips(x, y)

        def half(ref, chip, hc):
            hr = ref.shape[1] // 2
            return ref.at[chip, pl.ds(hc * hr, hr), :]

        first = []
        for k in range(n):
            first += [_remote(half(ins[k], me, c), half(outs[k], me, c), send_sems.at[6 * k + j], recv_sems.at[6 * k + j], (cx, cy, c))
                      for j, (cx, cy) in enumerate(chips)]
        first += [_remote(ins[n].at[me], outs[n].at[me], send_sems.at[6 * n + j], recv_sems.at[6 * n + j], (cx, cy, c))
                  for j, (cx, cy) in enumerate(chips)]
        for cp in first:
            cp.start()
        passed = []
        for k in range(n):
            for j, (cx, cy) in enumerate(chips):
                landed = half(outs[k], 2 * cx + cy, c)
                _remote(landed, landed, send_sems.at[6 * k + j], recv_sems.at[6 * k + j], (x, y, c)).wait_recv()
                passed.append(_remote(landed, landed, send_sems.at[6 * k + 3 + j], recv_sems.at[6 * k + 3 + j], sibling))
                passed[-1].start()
        for k in range(n):
            for j, (cx, cy) in enumerate(chips):
                theirs = half(outs[k], 2 * cx + cy, 1 - c)
                _remote(theirs, theirs, send_sems.at[6 * k + 3 + j], recv_sems.at[6 * k + 3 + j], (x, y, c)).wait_recv()
        for j, (cx, cy) in enumerate(chips):
            slot = outs[n].at[2 * cx + cy]
            _remote(slot, slot, send_sems.at[6 * n + j], recv_sems.at[6 * n + j], (x, y, c)).wait_recv()
        for cp in first + passed:
            cp.wait_send()

    arrays = [stacks[k] for k in names] + [wc_stack]
    outs = pl.pallas_call(
        body, name="allgather_weights",
        in_specs=[HBM] * (n + 1), out_specs=[HBM] * (n + 1), input_output_aliases={k: k for k in range(n + 1)},
        out_shape=[_sds(a.shape, a.dtype) for a in arrays],
        scratch_shapes=[pltpu.SemaphoreType.DMA((6 * n + 3,)), pltpu.SemaphoreType.DMA((6 * n + 3,))],
    )(*arrays)
    return dict(zip(names, outs[:n])), outs[n]


_KIND = {"w_in": "stack", "w_pa": "col", "w_pb": "col", "w_up": "col", "w_out": "row", "w_down": "row"}


def _half_view(ref, kind, h):
    if kind == "stack":
        k = ref.shape[1] // 2
        return ref.at[:, pl.ds(h * k, k), :]
    if kind == "col":
        k = ref.shape[0] // 2
        return ref.at[pl.ds(h * k, k), :]
    k = ref.shape[1] // 2
    return ref.at[:, pl.ds(h * k, k)]


def _shard_view(ref, kind, i):
    if kind == "stack":
        return ref.at[i]
    if kind == "col":
        k = ref.shape[1] // N_CHIPS
        return ref.at[:, pl.ds(i * k, k)]
    k = ref.shape[0] // N_CHIPS
    return ref.at[pl.ds(i * k, k), :]


def _region_view(ref, kind, h):
    if kind == "row":
        k = ref.shape[1] // 2
        return ref.at[:, pl.ds(h * k, k)]
    k = ref.shape[0] // 2
    return ref.at[pl.ds(h * k, k), :]


def _half_shape(shape, kind):
    if kind == "stack":
        return (shape[0], shape[1] // 2, shape[2])
    return (shape[0] // 2, shape[1]) if kind == "col" else (shape[0], shape[1] // 2)


def _part_shape(half_shape, kind):
    if kind == "stack":
        return tuple(half_shape[1:])
    k, w = half_shape
    return (k, w // N_CHIPS) if kind == "col" else (k // N_CHIPS, w)


def _pair_exchange(parts):
    names = list(parts)
    n = len(names)

    def body(*refs):
        p, q = refs[:n], refs[n:2 * n]
        send_sems, recv_sems = refs[2 * n:]
        x, y, c = _mesh_pos()

        def copy(k, h):
            return _remote(_half_view(p[k], _KIND[names[k]], h), q[k], send_sems.at[k], recv_sems.at[k], (x, y, 1 - c))

        for hc in range(2):
            @pl.when(c == hc)
            def _():
                for k in range(n):
                    copy(k, 1 - hc).start()

        for k in range(n):
            copy(k, 0).wait()

    outs = pl.pallas_call(
        body, name="grad_pair_exchange", in_specs=[HBM] * n, out_specs=[HBM] * n,
        out_shape=[_sds(_half_shape(parts[k].shape, _KIND[k]), parts[k].dtype) for k in names],
        scratch_shapes=[pltpu.SemaphoreType.DMA((n,)), pltpu.SemaphoreType.DMA((n,))],
    )(*[parts[k] for k in names])
    return dict(zip(names, outs))


def _half_blocks(shape, kind):
    if kind == "stack":
        _, k, w = shape
        tr = 256
        nb = k // 2 // tr
        return (N_CHIPS, nb), (1, tr, w), (lambda i, r, s: (i, r, 0)), (lambda i, r, s: (i, s[1] * nb + r, 0))
    k, w = shape
    if kind == "col":
        tr = 256 if w <= 2 * D_MODEL else 128
        nb = k // 2 // tr
        return (nb,), (tr, w), (lambda r, s: (r, 0)), (lambda r, s: (s[1] * nb + r, 0))
    tr = k // N_CHIPS
    return (N_CHIPS,), (tr, w // 2), (lambda r, s: (r, 0)), (lambda r, s: (r, s[1]))


def _pair_add(part, from_sibling, name, pos):
    kind = _KIND[name]
    grid, block, half_map, full_map = _half_blocks(part.shape, kind)

    def body(s_ref, p_ref, q_ref, o_ref):
        o_ref[...] = (p_ref[...].astype(F32) + q_ref[...].astype(F32)).astype(BF16)

    return pl.pallas_call(
        body, name="grad_pair_add_" + name,
        grid_spec=pltpu.PrefetchScalarGridSpec(
            num_scalar_prefetch=1, grid=grid,
            in_specs=[pl.BlockSpec(block, full_map), pl.BlockSpec(block, half_map)],
            out_specs=pl.BlockSpec(block, half_map)),
        out_shape=_sds(from_sibling.shape, BF16),
        compiler_params=_cp(("arbitrary",) * len(grid)),
    )(pos, part, from_sibling)


def _chip_exchange(sums):
    names = list(sums)
    n = len(names)

    def body(*refs):
        s, r = refs[:n], refs[n:2 * n]
        send_sems, recv_sems = refs[2 * n:]
        x, y, c = _mesh_pos()
        me = 2 * x + y

        def copy(k, i, j, to):
            return _remote(_shard_view(s[k], _KIND[names[k]], i), r[k].at[j], send_sems.at[3 * k + j], recv_sems.at[3 * k + j], to)

        for i in range(N_CHIPS):
            xi, yi = i // 2, i % 2
            j = jnp.where(xi != x, jnp.where(yi != y, 2, 0), 1)

            @pl.when(i != me)
            def _():
                for k in range(n):
                    copy(k, i, j, (xi, yi, c)).start()

        for k in range(n):
            for j in range(3):
                copy(k, 0, j, (x, y, c)).wait()

    outs = pl.pallas_call(
        body, name="grad_chip_exchange", in_specs=[HBM] * n, out_specs=[HBM] * n,
        out_shape=[_sds((3,) + _part_shape(sums[k].shape, _KIND[k]), sums[k].dtype) for k in names],
        scratch_shapes=[pltpu.SemaphoreType.DMA((3 * n,)), pltpu.SemaphoreType.DMA((3 * n,))],
    )(*[sums[k] for k in names])
    return dict(zip(names, outs))


def _owner_sum(part, from_sibling, from_chips, name, pos, shard_shape):
    kind = _KIND[name]
    _, pk, pw = from_chips.shape
    if kind == "row":
        tr, nb = pk, 1
        p_spec = pl.BlockSpec((tr, pw), lambda r, s: (s[0], s[1]))
        q_spec = pl.BlockSpec((tr, pw), lambda r, s: (s[0], 0))
        o_spec = pl.BlockSpec((tr, pw), lambda r, s: (0, s[1]))
    else:
        tr = 256
        nb = pk // tr
        if kind == "stack":
            p_spec = pl.BlockSpec((None, tr, pw), lambda r, s: (s[0], s[1] * nb + r, 0))
            q_spec = pl.BlockSpec((None, tr, pw), lambda r, s: (s[0], r, 0))
        else:
            p_spec = pl.BlockSpec((tr, pw), lambda r, s: (s[1] * nb + r, s[0]))
            q_spec = pl.BlockSpec((tr, pw), lambda r, s: (r, s[0]))
        o_spec = pl.BlockSpec((tr, pw), lambda r, s: (s[1] * nb + r, 0))

    def body(s_ref, p_ref, q_ref, r_ref, o_ref):
        acc = p_ref[...].astype(F32) + q_ref[...].astype(F32)
        for j in range(3):
            acc = acc + r_ref[j].astype(F32)
        o_ref[...] = acc

    return pl.pallas_call(
        body, name="grad_owner_sum_" + name,
        grid_spec=pltpu.PrefetchScalarGridSpec(
            num_scalar_prefetch=1, grid=(nb,),
            in_specs=[p_spec, q_spec, pl.BlockSpec((3, tr, pw), lambda r, s: (0, r, 0))],
            out_specs=o_spec),
        out_shape=_sds(shard_shape, F32),
        compiler_params=_cp(("arbitrary",), 32),
    )(pos, part, from_sibling, from_chips)


def _pair_share(shards):
    names = list(shards)
    n = len(names)

    def body(*refs):
        g_in, g_out = refs[:n], refs[n:2 * n]
        send_sems, recv_sems = refs[2 * n:]
        x, y, c = _mesh_pos()

        def copy(k, h):
            kind = _KIND[names[k]]
            return _remote(_region_view(g_in[k], kind, h), _region_view(g_out[k], kind, h), send_sems.at[k], recv_sems.at[k], (x, y, 1 - c))

        for hc in range(2):
            @pl.when(c == hc)
            def _():
                for k in range(n):
                    copy(k, hc).start()

        for k in range(n):
            copy(k, 0).wait()

    outs = pl.pallas_call(
        body, name="grad_pair_share", in_specs=[HBM] * n, out_specs=[HBM] * n, input_output_aliases={k: k for k in range(n)},
        out_shape=[_sds(shards[k].shape, shards[k].dtype) for k in names],
        scratch_shapes=[pltpu.SemaphoreType.DMA((n,)), pltpu.SemaphoreType.DMA((n,))],
    )(*[shards[k] for k in names])
    return dict(zip(names, outs))


def _allgather_small(block):
    m_per = block.shape[0]

    def body(x_ref, out_ref, send_sems, recv_sems, local_sem):
        x, y, c = _mesh_pos()
        me, sibling = (x, y, c), (x, y, 1 - c)
        chips = _other_chips(x, y)

        def rows(px, py, pc):
            return out_ref.at[4 * px + 2 * py + pc]

        def copy(k, block_of, to, src=None):
            return _remote(rows(*block_of) if src is None else src, rows(*block_of), send_sems.at[k], recv_sems.at[k], to)

        mine = pltpu.make_async_copy(x_ref, rows(*me), local_sem)
        mine.start()
        first = [copy(0, me, sibling, src=x_ref)]
        first += [copy(1 + j, me, (*chip, c), src=x_ref) for j, chip in enumerate(chips)]
        for cp in first:
            cp.start()
        passed = [copy(4 + j, (*chip, c), sibling) for j, chip in enumerate(chips)]
        for j, chip in enumerate(chips):
            copy(1 + j, (*chip, c), me).wait_recv()
            passed[j].start()
        copy(0, sibling, me).wait_recv()
        for j, chip in enumerate(chips):
            copy(4 + j, (*chip, 1 - c), me).wait_recv()
        for cp in first + passed:
            cp.wait_send()
        mine.wait()

    return pl.pallas_call(
        body, name="allgather_small",
        in_specs=[pl.BlockSpec(memory_space=pltpu.VMEM)], out_specs=pl.BlockSpec(memory_space=pltpu.VMEM),
        out_shape=_sds((N_DEV, m_per, D_MODEL), block.dtype),
        scratch_shapes=[pltpu.SemaphoreType.DMA((7,)), pltpu.SemaphoreType.DMA((7,)), pltpu.SemaphoreType.DMA],
    )(block)


def _adam_math(w, g, m, v):
    m = ADAM_B1 * m + (1.0 - ADAM_B1) * g
    v = ADAM_B2 * v + (1.0 - ADAM_B2) * (g * g)
    m_hat = m / (1.0 - ADAM_B1 ** ADAM_STEP)
    v_hat = v / (1.0 - ADAM_B2 ** ADAM_STEP)
    delta = -ADAM_LR * (m_hat / (jnp.sqrt(v_hat) + ADAM_EPS) + ADAM_WD * w)
    return delta, m, v


def _adamw(w, g, m, v, name):
    rows, cols = w.shape
    tr = rows
    for cand in (256, 128, 64, 32, 16, 8):
        if rows % cand == 0 and rows > cand:
            tr = cand
            break

    def body(w_ref, g_ref, m_ref, v_ref, d_ref, nm_ref, nv_ref):
        d, nm, nv = _adam_math(w_ref[...], g_ref[...], m_ref[...], v_ref[...])
        d_ref[...] = d
        nm_ref[...] = nm
        nv_ref[...] = nv

    spec = pl.BlockSpec((tr, cols), lambda i: (i, 0))
    return pl.pallas_call(
        body, name=name, grid=(rows // tr,), in_specs=[spec] * 4, out_specs=[spec] * 3,
        out_shape=[_sds(w.shape, F32)] * 3, compiler_params=_cp(("arbitrary",)),
    )(w, g, m, v)


def _small_sum_adamw(gathered, w, m, v):
    def body(a_ref, w_ref, m_ref, v_ref, g_ref, d_ref, nm_ref, nv_ref):
        g = a_ref[0]
        for k in range(1, N_DEV):
            g = g + a_ref[k]
        g_ref[...] = g
        d, nm, nv = _adam_math(w_ref[...], g, m_ref[...], v_ref[...])
        d_ref[...] = d
        nm_ref[...] = nm
        nv_ref[...] = nv

    return pl.pallas_call(
        body, name="small_sum_adamw", out_shape=[_sds(w.shape, F32)] * 4,
    )(gathered, w, m, v)


_NAMES = ("g_mix", "w_in", "g_sgu", "w_s", "b_s", "sinks", "rel_bias", "w_pa", "w_pb", "w_out",
          "g_ffn", "w_up", "w_conv", "b_conv", "w_down", "g_final")

def kernel(x, g_mix, w_in, g_sgu, w_s, b_s, sinks, rel_bias, w_pa, w_pb, w_out, g_ffn, w_up, w_conv, b_conv, w_down, g_final, loss_target, m_g_mix, m_w_in, m_g_sgu, m_w_s, m_b_s, m_sinks, m_rel_bias, m_w_pa, m_w_pb, m_w_out, m_g_ffn, m_w_up, m_w_conv, m_b_conv, m_w_down, m_g_final, v_g_mix, v_w_in, v_g_sgu, v_w_s, v_b_s, v_sinks, v_rel_bias, v_w_pa, v_w_pb, v_w_out, v_g_ffn, v_w_up, v_w_conv, v_b_conv, v_w_down, v_g_final):
    w = dict(g_mix=g_mix, w_in=w_in, g_sgu=g_sgu, w_s=w_s, b_s=b_s, sinks=sinks, rel_bias=rel_bias, w_pa=w_pa, w_pb=w_pb,
             w_out=w_out, g_ffn=g_ffn, w_up=w_up, w_conv=w_conv, b_conv=b_conv, w_down=w_down, g_final=g_final)
    m = dict(g_mix=m_g_mix, w_in=m_w_in, g_sgu=m_g_sgu, w_s=m_w_s, b_s=m_b_s, sinks=m_sinks, rel_bias=m_rel_bias, w_pa=m_w_pa,
             w_pb=m_w_pb, w_out=m_w_out, g_ffn=m_g_ffn, w_up=m_w_up, w_conv=m_w_conv, b_conv=m_b_conv, w_down=m_w_down,
             g_final=m_g_final)
    v = dict(g_mix=v_g_mix, w_in=v_w_in, g_sgu=v_g_sgu, w_s=v_w_s, b_s=v_b_s, sinks=v_sinks, rel_bias=v_rel_bias, w_pa=v_w_pa,
             w_pb=v_w_pb, w_out=v_w_out, g_ffn=v_g_ffn, w_up=v_w_up, w_conv=v_w_conv, b_conv=v_b_conv, w_down=v_w_down,
             g_final=v_g_final)
    xi, yi, ci = _mesh_pos()
    me = 2 * xi + yi

    shard = {n: w[n][0] for n in _BIG}
    shard_shapes = {n: shard[n].shape for n in _BIG}
    wc_shard = w["w_conv"][0]
    wc_pad = jnp.pad(wc_shard, ((0, 5), (0, 0)))
    stacks, wc_all = _allgather_weights({n: _own_slot(shard[n].astype(BF16), N_CHIPS, me) for n in _BIG},
                                        _own_slot(wc_pad, N_CHIPS, me))
    w_conv_full = jnp.concatenate([wc_all[i, :3] for i in range(N_CHIPS)], axis=1)
    w_in_full = stacks["w_in"].transpose(1, 0, 2).reshape(D_MODEL, -1)
    w_a = w_in_full[:, :A_DIM]
    w_b = w_in_full[:, A_DIM:A_DIM + B_DIM]
    w_g = w_in_full[:, A_DIM + B_DIM:]

    loss, grad_x, small, big = _local_step(
        x, loss_target, w["g_mix"], w["g_sgu"], w["w_s"][0], w["b_s"][0], w["sinks"], w["rel_bias"], w["g_ffn"],
        w["b_conv"], w["g_final"], w_g, w_a, w_b, stacks["w_pa"], stacks["w_pb"], stacks["w_out"].reshape(D_MODEL, D_MODEL),
        stacks["w_up"], w_conv_full, stacks["w_down"].reshape(D_FF, D_MODEL))

    small["loss"] = loss
    all_small = _allgather_small(_pack_small(small))
    sw = {n: (jnp.zeros((1, 1), F32) if n in ("loss", "w_conv") else w[n]) for n, _ in _SMALL}
    sm = {n: (jnp.zeros((1, 1), F32) if n in ("loss", "w_conv") else m[n]) for n, _ in _SMALL}
    sv = {n: (jnp.zeros((1, 1), F32) if n in ("loss", "w_conv") else v[n]) for n, _ in _SMALL}
    for d in (sw, sm, sv):
        d["w_conv"] = jnp.zeros((3, 2 * D_FF), F32)
    s_g, s_d, s_m, s_v = [_unpack_small(a) for a in _small_sum_adamw(all_small, _pack_small(sw), _pack_small(sm), _pack_small(sv))]

    pos = jnp.stack([me, ci])
    from_sibling = _pair_exchange(big)
    pair_sums = {n: _pair_add(big[n], from_sibling[n], n, pos) for n in _BIG}
    from_chips = _chip_exchange(pair_sums)
    g_big = _pair_share({n: _owner_sum(big[n], from_sibling[n], from_chips[n], n, pos, shard_shapes[n]) for n in _BIG})

    grads, deltas, new_m, new_v = {}, {}, {}, {}
    for n in _BIG:
        d, nm, nv = _adamw(shard[n], g_big[n], m[n][0], v[n][0], "adamw_" + n)
        grads[n], deltas[n], new_m[n], new_v[n] = g_big[n][None], d[None], nm[None], nv[None]
    wcols = wc_shard.shape[1]
    g_wc = lax.dynamic_slice(s_g["w_conv"], (0, me * wcols), (3, wcols))
    d, nm, nv = _adamw(wc_shard, g_wc, m["w_conv"][0], v["w_conv"][0], "adamw_w_conv")
    grads["w_conv"], deltas["w_conv"], new_m["w_conv"], new_v["w_conv"] = g_wc[None], d[None], nm[None], nv[None]
    for n, _ in _SMALL:
        if n in ("loss", "w_conv"):
            continue
        shp = w[n].shape
        grads[n], deltas[n], new_m[n], new_v[n] = (s_g[n].reshape(shp), s_d[n].reshape(shp), s_m[n].reshape(shp),
                                                    s_v[n].reshape(shp))

    return (s_g["loss"].reshape(()), grad_x, *[grads[n] for n in _NAMES], *[deltas[n] for n in _NAMES],
            *[new_m[n] for n in _NAMES], *[new_v[n] for n in _NAMES])
```

```python
import functools

import numpy as np
import jax
import jax.numpy as jnp
from jax import lax
from jax.experimental import pallas as pl
from jax.experimental.pallas import tpu as pltpu

F32 = jnp.float32
BF16 = jnp.bfloat16

D_MODEL = 1024
CHUNK = 128
A_GROUPS = 4
A_WIDTH = 512
N_HEADS = 8
HEAD_DIM = 64
Q_DIM = 512
KV_DIM = 128
N_BUCKETS = 32
MAX_DISTANCE = 128
D_FF = 2816
EPS = 1e-6
NEG_INF = -1e30
G_DIM = 2 * D_MODEL
A_DIM = 2 * A_WIDTH
B_DIM = Q_DIM + 2 * KV_DIM
LANES = 128
SUBLANES = 8
BF16_ROWS = 16
N_CHIPS = 4
N_DEV = 8

ADAM_LR = 0.001
ADAM_B1 = 0.9
ADAM_B2 = 0.999
ADAM_EPS = 1e-08
ADAM_WD = 0.01
ADAM_STEP = 10

MESH = pl.DeviceIdType.MESH
_GELU_C = 0.7978845608028654
_GELU_A = 0.044715


def _cp(sem=None, vmem_mb=None):
    kw = {}
    if sem is not None:
        kw["dimension_semantics"] = sem
    if vmem_mb is not None:
        kw["vmem_limit_bytes"] = vmem_mb << 20
    return pltpu.CompilerParams(**kw)


def _dot(a, b):
    return jnp.dot(a, b, preferred_element_type=F32)


def _dot_nt(a, b):
    return lax.dot_general(a, b, (((1,), (1,)), ((), ())), preferred_element_type=F32)


def _dot_tn(a, b):
    return lax.dot_general(a, b, (((0,), (0,)), ((), ())), preferred_element_type=F32)


def _rms_r(x):
    return lax.rsqrt(jnp.mean(x * x, axis=-1, keepdims=True) + EPS)


def _rms_bwd(dh, n, r, g):
    dn = dh * g
    return r * (dn - n * jnp.mean(dn * n, axis=-1, keepdims=True))


def _gelu(x):
    t = jnp.tanh(_GELU_C * (x + _GELU_A * (x * x * x)))
    return 0.5 * x * (1.0 + t), t


def _gelu_grad(x, t):
    return 0.5 * (1.0 + t) + 0.5 * x * (1.0 - t * t) * (_GELU_C * (1.0 + 3.0 * _GELU_A * x * x))


def _sigmoid(x):
    return 1.0 / (1.0 + jnp.exp(-x))


def _row(tm, w):
    return pl.BlockSpec((tm, w), lambda i: (i, 0))


def _full(shape):
    nd = len(shape)
    return pl.BlockSpec(tuple(shape), lambda *_: (0,) * nd)


def _resident(shape):
    nd = len(shape)
    return pl.BlockSpec(tuple(shape), lambda *_: (0,) * nd, pipeline_mode=pl.Buffered(1))


def _sds(shape, dtype):
    return jax.ShapeDtypeStruct(tuple(shape), dtype)


HBM = pl.BlockSpec(memory_space=pltpu.HBM)
ANY = pl.BlockSpec(memory_space=pl.ANY)
SEM = pl.BlockSpec(memory_space=pltpu.SEMAPHORE)


def _band_buckets():
    i = np.arange(CHUNK)[:, None]
    j = np.arange(2 * CHUNK)[None, :]
    dist = i + CHUNK - j
    valid = (dist >= 0) & (dist < CHUNK)
    d = np.clip(dist, 0, None)
    max_exact = N_BUCKETS // 2
    large = max_exact + (np.log(np.maximum(d, 1) / max_exact) / np.log(MAX_DISTANCE / max_exact)
                         * (N_BUCKETS - max_exact)).astype(np.int32)
    large = np.minimum(large, N_BUCKETS - 1)
    buckets = np.where(d < max_exact, d, large).astype(np.int32)
    return np.where(valid, buckets, -1).astype(np.int32)


def _inproj(x2, g_mix, w_g, w_a, w_b, tm, after=None):
    T = x2.shape[0]
    order = [] if after is None else [after]

    def body(*refs):
        x_ref, g_ref, wg_ref, wa_ref, wb_ref = refs[:5]
        pg_ref, pa_ref, pb_ref, h_ref = refs[5 + len(order):]
        x = x_ref[...]
        h = (x * _rms_r(x) * g_ref[...]).astype(BF16)
        h_ref[...] = h
        pg_ref[...] = _dot(h, wg_ref[...]).astype(BF16)
        pa_ref[...] = _dot(h, wa_ref[...]).astype(BF16)
        pb_ref[...] = _dot(h, wb_ref[...]).astype(BF16)

    return pl.pallas_call(
        body, name="inproj", grid=(T // tm,),
        in_specs=[_row(tm, D_MODEL), _full(g_mix.shape), _resident(w_g.shape), _resident(w_a.shape), _resident(w_b.shape)]
        + [ANY] * len(order),
        out_specs=[_row(tm, G_DIM), _row(tm, A_DIM), _row(tm, B_DIM), _row(tm, D_MODEL)],
        out_shape=[_sds((T, G_DIM), BF16), _sds((T, A_DIM), BF16), _sds((T, B_DIM), BF16), _sds((T, D_MODEL), BF16)],
        compiler_params=_cp(("arbitrary",), 48),
    )(x2, g_mix, w_g, w_a, w_b, *order)


def _sgu_parts(p, g):
    pu = p[:, :A_WIDTH]
    pv = p[:, A_WIDTH:]
    u, tu = _gelu(pu)
    vv, tv = _gelu(pv)
    rv = _rms_r(vv)
    vn = (vv * rv * g).astype(BF16)
    return pu, pv, u, tu, vv, tv, rv, vn


def _tril():
    r = lax.broadcasted_iota(jnp.int32, (CHUNK, CHUNK), 0)
    c = lax.broadcasted_iota(jnp.int32, (CHUNK, CHUNK), 1)
    return r >= c


def _sgu_fwd(proj_a, g_sgu, w_s, b_st, tm):
    T = proj_a.shape[0]

    def body(p_ref, g_ref, ws_ref, bs_ref, y_ref):
        tril = _tril()
        _, _, u, _, _, _, _, vn = _sgu_parts(p_ref[...].astype(F32), g_ref[...])
        for gi in range(A_GROUPS):
            wm = jnp.where(tril, ws_ref[gi], 0.0).astype(BF16)
            bcol = bs_ref[:, gi:gi + 1]
            cs = slice(gi * CHUNK, (gi + 1) * CHUNK)
            for c in range(tm // CHUNK):
                rs = slice(c * CHUNK, (c + 1) * CHUNK)
                s = _dot(wm, vn[rs, cs]) + bcol
                y_ref[rs, cs] = (u[rs, cs] * s).astype(BF16)

    return pl.pallas_call(
        body, name="sgu_fwd", grid=(T // tm,),
        in_specs=[_row(tm, A_DIM), _full(g_sgu.shape), _full(w_s.shape), _full(b_st.shape)],
        out_specs=_row(tm, A_WIDTH), out_shape=_sds((T, A_WIDTH), BF16),
        compiler_params=_cp(("arbitrary",)),
    )(proj_a, g_sgu, w_s, b_st)


HEAD_ROWS = N_HEADS * CHUNK


def _head_rows(h):
    return slice(h * CHUNK, (h + 1) * CHUNK)


def _attn_setup(bias_scr, sink_scr, kvar_scr, qkv_ref, bk_ref, rel_ref, sink_ref):
    bk = bk_ref[...]
    for h in range(N_HEADS):
        acc = jnp.full((CHUNK, 2 * CHUNK), NEG_INF, F32)
        for b in range(N_BUCKETS):
            acc = jnp.where(bk == b, rel_ref[b, h], acc)
        bias_scr[_head_rows(h), :] = acc
        sink_scr[_head_rows(h), :] = jnp.full((CHUNK, LANES), sink_ref[0, h], F32)
    seq = qkv_ref.shape[0]
    rows_per = 2 * CHUNK
    for is_v in range(2):
        c0 = Q_DIM + is_v * KV_DIM
        for r in range(seq // rows_per):
            rs = slice(r * rows_per, (r + 1) * rows_per)
            a = qkv_ref[rs, c0:c0 + KV_DIM].astype(F32)
            lane = lax.broadcasted_iota(jnp.int32, a.shape, 1)
            lo = jnp.where(lane < HEAD_DIM, a, 0.0)
            hi = jnp.where(lane >= HEAD_DIM, a, 0.0)
            kvar_scr[4 * is_v + 0, rs, :] = lo.astype(BF16)
            kvar_scr[4 * is_v + 1, rs, :] = pltpu.roll(lo, HEAD_DIM, 1).astype(BF16)
            kvar_scr[4 * is_v + 2, rs, :] = pltpu.roll(hi, HEAD_DIM, 1).astype(BF16)
            kvar_scr[4 * is_v + 3, rs, :] = hi.astype(BF16)


def _rowsum(a, ones):
    hi = a.astype(BF16)
    lo = (a - hi.astype(F32)).astype(BF16)
    return _dot(hi, ones) + _dot(lo, ones)


def _both(a):
    return jnp.concatenate([a, a], axis=1)


def _attn_probs(qkv_ref, r0, n, kv, bias_scr, sink_scr, ones):
    s = jnp.concatenate([_dot_nt(qkv_ref[pl.ds(r0, CHUNK), (h // 2) * LANES:(h // 2 + 1) * LANES], kv[h // 4][h % 2])
                         for h in range(N_HEADS)], axis=0)
    s = s * (HEAD_DIM ** -0.5) + bias_scr[...]
    col = lax.broadcasted_iota(jnp.int32, s.shape, 1)
    s = jnp.where((col < CHUNK) & (n == 0), NEG_INF, s)
    sink = sink_scr[...]
    m = jnp.maximum(jnp.max(s, axis=-1, keepdims=True), sink)
    p = jnp.exp(s - _both(m))
    es = jnp.exp(sink - m)
    inv = 1.0 / (_rowsum(p, ones) + es)
    return p * _both(inv), es * inv


def _attn_block_inputs(kvar_scr, n):
    r0 = pl.multiple_of(n * CHUNK, CHUNK)
    rp = pl.multiple_of(jnp.maximum(n - 1, 0) * CHUNK, CHUNK)

    def both(idx):
        return jnp.concatenate([kvar_scr[idx, pl.ds(rp, CHUNK), :], kvar_scr[idx, pl.ds(r0, CHUNK), :]], axis=0)

    kv = ((both(0), both(1)), (both(2), both(3)))
    vv = ((both(4), both(5)), (both(6), both(7)))
    return r0, kv, vv


def _attn_fwd(proj_b, sinks, rel_bias, n_seq, seq):
    nb = seq // CHUNK
    bk = jnp.asarray(_band_buckets())

    def body(qkv_ref, bk_ref, rel_ref, sink_ref, o_ref, bias_scr, sink_scr, kvar_scr):
        _attn_setup(bias_scr, sink_scr, kvar_scr, qkv_ref, bk_ref, rel_ref, sink_ref)
        ones = jnp.ones((2 * CHUNK, LANES), BF16)

        def blk(n, carry):
            r0, kv, vv = _attn_block_inputs(kvar_scr, n)
            prob, _ = _attn_probs(qkv_ref, r0, n, kv, bias_scr, sink_scr, ones)
            pb = prob.astype(BF16)
            for pr in range(N_HEADS // 2):
                acc = _dot(pb[_head_rows(2 * pr)], vv[pr // 2][0]) + _dot(pb[_head_rows(2 * pr + 1)], vv[pr // 2][1])
                o_ref[pl.ds(r0, CHUNK), pr * LANES:(pr + 1) * LANES] = acc.astype(BF16)
            return carry

        lax.fori_loop(0, nb, blk, 0)

    smem = pl.BlockSpec(memory_space=pltpu.SMEM)
    return pl.pallas_call(
        body, name="attn_fwd", grid=(n_seq,),
        in_specs=[_row(seq, B_DIM), _full(bk.shape), smem, smem],
        out_specs=_row(seq, Q_DIM), out_shape=_sds((n_seq * seq, Q_DIM), BF16),
        scratch_shapes=[pltpu.VMEM((HEAD_ROWS, 2 * CHUNK), F32), pltpu.VMEM((HEAD_ROWS, LANES), F32),
                        pltpu.VMEM((8, seq, KV_DIM), BF16)],
        compiler_params=_cp(("arbitrary",), 40),
    )(proj_b, bk, rel_bias, sinks)


def _dot_stacked(a, w_ref):
    return jnp.concatenate([_dot(a, w_ref[i]) for i in range(N_CHIPS)], axis=1)


def _dot_nt_stacked(a, w_ref):
    w = w_ref.shape[2]
    acc = _dot_nt(a[:, :w], w_ref[0])
    for i in range(1, N_CHIPS):
        acc = acc + _dot_nt(a[:, i * w:(i + 1) * w], w_ref[i])
    return acc


def _merge_fwd(x2, y_a, y_b, proj_g, w_pa, w_pb, w_out, tm):
    T = x2.shape[0]

    def body(x_ref, ya_ref, yb_ref, g_ref, wpa_ref, wpb_ref, wo_ref, x1_ref, mg_ref):
        g = g_ref[...].astype(F32)
        pa = _dot_stacked(ya_ref[...], wpa_ref)
        pb = _dot_stacked(yb_ref[...], wpb_ref)
        merged = (_sigmoid(g[:, :D_MODEL]) * pa + _sigmoid(g[:, D_MODEL:]) * pb).astype(BF16)
        mg_ref[...] = merged
        x1_ref[...] = x_ref[...] + _dot(merged, wo_ref[...])

    return pl.pallas_call(
        body, name="merge_fwd", grid=(T // tm,),
        in_specs=[_row(tm, D_MODEL), _row(tm, A_WIDTH), _row(tm, Q_DIM), _row(tm, G_DIM),
                  _resident(w_pa.shape), _resident(w_pb.shape), _resident(w_out.shape)],
        out_specs=[_row(tm, D_MODEL), _row(tm, D_MODEL)],
        out_shape=[_sds((T, D_MODEL), F32), _sds((T, D_MODEL), BF16)],
        compiler_params=_cp(("arbitrary",), 40),
    )(x2, y_a, y_b, proj_g, w_pa, w_pb, w_out)


def _upproj(x1, g_ffn, w_up, tm):
    T = x1.shape[0]
    cw = w_up.shape[2]

    def body(x_ref, g_ref, w_ref, u_ref, h_ref):
        x = x_ref[...]
        h = (x * _rms_r(x) * g_ref[...]).astype(BF16)
        h_ref[...] = h
        for i in range(N_CHIPS):
            u_ref[:, i * cw:(i + 1) * cw] = _dot(h, w_ref[i]).astype(BF16)

    return pl.pallas_call(
        body, name="upproj", grid=(T // tm,),
        in_specs=[_row(tm, D_MODEL), _full(g_ffn.shape), _resident(w_up.shape)],
        out_specs=[_row(tm, 2 * D_FF), _row(tm, D_MODEL)],
        out_shape=[_sds((T, 2 * D_FF), BF16), _sds((T, D_MODEL), BF16)],
        compiler_params=_cp(("arbitrary",), 56),
    )(x1, g_ffn, w_up)


def _shift_down(u, halo, k):
    rolled = pltpu.roll(u, k, 0)
    head = rolled[:SUBLANES]
    row = lax.broadcasted_iota(jnp.int32, head.shape, 0)
    if k == 1:
        head = jnp.where(row == 0, halo[1:2], head)
    else:
        head = jnp.where(row == 0, halo[0:1], jnp.where(row == 1, halo[1:2], head))
    return jnp.concatenate([head, rolled[SUBLANES:]], axis=0)


def _shift_up(d, halo, k):
    tm = d.shape[0]
    rolled = pltpu.roll(d, tm - k, 0)
    tail = rolled[tm - SUBLANES:]
    row = lax.broadcasted_iota(jnp.int32, tail.shape, 0)
    if k == 1:
        tail = jnp.where(row == SUBLANES - 1, halo[0:1], tail)
    else:
        tail = jnp.where(row == SUBLANES - 2, halo[0:1], jnp.where(row == SUBLANES - 1, halo[1:2], tail))
    return jnp.concatenate([rolled[:tm - SUBLANES], tail], axis=0)


def _conv_taps(u_ref, halo_ref, cols, at_start):
    u = u_ref[:, cols].astype(F32)
    hl = halo_ref[:, cols].astype(F32)[BF16_ROWS - 2:BF16_ROWS]
    hl = jnp.where(at_start, 0.0, hl)
    return u, _shift_down(u, hl, 1), _shift_down(u, hl, 2)


def _conv_out(taps, wc, bc):
    u, u1, u2 = taps
    return wc[0:1] * u2 + wc[1:2] * u1 + wc[2:3] * u + bc


def _prev_halo_spec(tm, width, col_block=None):
    k = tm // BF16_ROWS
    if col_block is None:
        return pl.BlockSpec((BF16_ROWS, width), lambda i: (jnp.maximum(i * k - 1, 0), 0))
    return pl.BlockSpec((BF16_ROWS, width), lambda j, i: (jnp.maximum(i * k - 1, 0), col_block(j)))


def _ffn_down_loss(upre, x1, target, w_conv, b_conv, w_down, g_final, tm, seq):
    T = x1.shape[0]
    tiles_per_seq = seq // tm
    half = D_FF // 2

    def body(u_ref, hl_ref, x1_ref, t_ref, wc_ref, bc_ref, wd_ref, g_ref, dx2_ref, loss_ref, gg_ref, gate_ref, val_ref):
        i = pl.program_id(0)
        at_start = (i % tiles_per_seq) == 0
        acc = jnp.zeros((tm, D_MODEL), F32)
        for j in range(2):
            gc = slice(j * half, (j + 1) * half)
            vc = slice(D_FF + j * half, D_FF + (j + 1) * half)
            gate = _conv_out(_conv_taps(u_ref, hl_ref, gc, at_start), wc_ref[:, gc], bc_ref[:, gc])
            val = _conv_out(_conv_taps(u_ref, hl_ref, vc, at_start), wc_ref[:, vc], bc_ref[:, vc])
            gate_ref[:, gc] = gate.astype(BF16)
            val_ref[:, gc] = val.astype(BF16)
            act = (gate * _sigmoid(gate) * val).astype(BF16)
            acc = acc + _dot(act, wd_ref[gc, :])
        x2 = x1_ref[...] + acc
        r = _rms_r(x2)
        n = x2 * r
        g = g_ref[...]
        diff = n * g - t_ref[...]
        dy = diff * (1.0 / D_MODEL)
        dx2_ref[...] = _rms_bwd(dy, n, r, g)

        @pl.when(i == 0)
        def _():
            loss_ref[...] = jnp.zeros_like(loss_ref)
            gg_ref[...] = jnp.zeros_like(gg_ref)

        loss_ref[...] += 0.5 * jnp.sum(jnp.mean(diff * diff, axis=-1, keepdims=True), axis=0, keepdims=True)
        gg_ref[...] += jnp.sum(dy * n, axis=0, keepdims=True)

    return pl.pallas_call(
        body, name="ffn_down_loss", grid=(T // tm,),
        in_specs=[_row(tm, 2 * D_FF), _prev_halo_spec(tm, 2 * D_FF), _row(tm, D_MODEL), _row(tm, D_MODEL),
                  _full(w_conv.shape), _full(b_conv.shape), _resident(w_down.shape), _full(g_final.shape)],
        out_specs=[_row(tm, D_MODEL), _full((1, 1)), _full((1, D_MODEL)), _row(tm, D_FF), _row(tm, D_FF)],
        out_shape=[_sds((T, D_MODEL), F32), _sds((1, 1), F32), _sds((1, D_MODEL), F32),
                   _sds((T, D_FF), BF16), _sds((T, D_FF), BF16)],
        compiler_params=_cp(("arbitrary",), 56),
    )(upre, upre, x1, target, w_conv, b_conv, w_down, g_final)


def _ffn_bwd_act(gate, val, dx2, w_down, tm):
    T = dx2.shape[0]
    half = D_FF // 2
    nt = T // tm

    def body(g_ref, v_ref, dx_ref, wd_ref, dg_ref, dv_ref, gwd_out, gbg_ref, gbv_ref, gwd_ref):
        i = pl.program_id(1)
        gate = g_ref[...].astype(F32)
        val = v_ref[...].astype(F32)
        sg = _sigmoid(gate)
        silu = gate * sg
        dx = dx_ref[...].astype(BF16)
        d_act = _dot_nt(dx, wd_ref[...])
        d_val = d_act * silu
        d_gate = d_act * val * (sg * (1.0 + gate * (1.0 - sg)))
        dg_ref[...] = d_gate.astype(BF16)
        dv_ref[...] = d_val.astype(BF16)

        @pl.when(i == 0)
        def _():
            for r in (gwd_ref, gbg_ref, gbv_ref):
                r[...] = jnp.zeros_like(r)

        gwd_ref[...] += _dot_tn((silu * val).astype(BF16), dx)
        gbg_ref[...] += jnp.sum(d_gate, axis=0, keepdims=True)
        gbv_ref[...] += jnp.sum(d_val, axis=0, keepdims=True)

        @pl.when(i == nt - 1)
        def _():
            gwd_out[...] = gwd_ref[...].astype(BF16)

    tile = pl.BlockSpec((tm, half), lambda j, i: (i, j))
    vec = pl.BlockSpec((1, half), lambda j, i: (0, j))
    wrows = pl.BlockSpec((half, D_MODEL), lambda j, i: (j, 0))
    return pl.pallas_call(
        body, name="ffn_bwd_act", grid=(2, nt),
        in_specs=[tile, tile, pl.BlockSpec((tm, D_MODEL), lambda j, i: (i, 0)), wrows],
        out_specs=[tile, tile, wrows, vec, vec],
        out_shape=[_sds((T, D_FF), BF16), _sds((T, D_FF), BF16), _sds((D_FF, D_MODEL), BF16),
                   _sds((1, D_FF), F32), _sds((1, D_FF), F32)],
        scratch_shapes=[pltpu.VMEM((half, D_MODEL), F32)],
        compiler_params=_cp(("arbitrary", "arbitrary"), 56),
    )(gate, val, dx2, w_down)


def _ffn_bwd_up(d_gate, d_val, upre, dx2, x1, g_ffn, w_conv, w_up, tm, seq):
    T = dx2.shape[0]
    tiles_per_seq = seq // tm
    k16 = tm // BF16_ROWS
    n16 = T // BF16_ROWS
    cw = D_FF // 2

    def body(dg_ref, dv_ref, hg_ref, hv_ref, u_ref, dx2_ref, x1_ref, g_ref, wc_ref, wu_ref, du_ref, dx1_ref, gg_ref, gwc_ref):
        i = pl.program_id(0)
        at_end = (i % tiles_per_seq) == tiles_per_seq - 1

        @pl.when(i == 0)
        def _():
            gg_ref[...] = jnp.zeros_like(gg_ref)
            gwc_ref[...] = jnp.zeros_like(gwc_ref)

        dh = jnp.zeros((tm, D_MODEL), F32)
        for j in range(4):
            src, hsrc = (dg_ref, hg_ref) if j < 2 else (dv_ref, hv_ref)
            ls = slice((j % 2) * cw, (j % 2 + 1) * cw)
            cs = slice(j * cw, (j + 1) * cw)
            d = src[:, ls].astype(F32)
            hl = hsrc[:, ls].astype(F32)[0:2]
            hl = jnp.where(at_end, 0.0, hl)
            wc = wc_ref[:, cs]
            d1 = _shift_up(d, hl, 1)
            d2 = _shift_up(d, hl, 2)
            du = (wc[2:3] * d + wc[1:2] * d1 + wc[0:1] * d2).astype(BF16)
            du_ref[:, cs] = du
            dh = dh + _dot_nt(du, wu_ref[j])
            u = u_ref[:, cs].astype(F32)
            gwc_ref[0:1, cs] += jnp.sum(d2 * u, axis=0, keepdims=True)
            gwc_ref[1:2, cs] += jnp.sum(d1 * u, axis=0, keepdims=True)
            gwc_ref[2:3, cs] += jnp.sum(d * u, axis=0, keepdims=True)
        x = x1_ref[...]
        r = _rms_r(x)
        n = x * r
        dx1_ref[...] = dx2_ref[...] + _rms_bwd(dh, n, r, g_ref[...])
        gg_ref[...] += jnp.sum(dh * n, axis=0, keepdims=True)

    nxt = pl.BlockSpec((BF16_ROWS, D_FF), lambda i: (jnp.minimum((i + 1) * k16, n16 - 1), 0))
    return pl.pallas_call(
        body, name="ffn_bwd_up", grid=(T // tm,),
        in_specs=[_row(tm, D_FF), _row(tm, D_FF), nxt, nxt, _row(tm, 2 * D_FF), _row(tm, D_MODEL), _row(tm, D_MODEL),
                  _full(g_ffn.shape), _full(w_conv.shape), _resident(w_up.shape)],
        out_specs=[_row(tm, 2 * D_FF), _row(tm, D_MODEL), _full((1, D_MODEL)), _full((3, 2 * D_FF))],
        out_shape=[_sds((T, 2 * D_FF), BF16), _sds((T, D_MODEL), F32), _sds((1, D_MODEL), F32), _sds((3, 2 * D_FF), F32)],
        compiler_params=_cp(("arbitrary",), 56),
    )(d_gate, d_val, d_gate, d_val, upre, dx2, x1, g_ffn, w_conv, w_up)


def _matmul_tn(a, b, tn, tk, name):
    T, M = a.shape
    N = b.shape[1]
    nk = T // tk

    def body(a_ref, b_ref, o_ref, acc_ref):
        k = pl.program_id(1)

        @pl.when(k == 0)
        def _():
            acc_ref[...] = jnp.zeros_like(acc_ref)

        acc_ref[...] += _dot_tn(a_ref[...], b_ref[...])

        @pl.when(k == nk - 1)
        def _():
            o_ref[...] = acc_ref[...].astype(BF16)

    return pl.pallas_call(
        body, name=name, grid=(N // tn, nk),
        in_specs=[pl.BlockSpec((tk, M), lambda j, k: (k, 0)), pl.BlockSpec((tk, tn), lambda j, k: (k, j))],
        out_specs=pl.BlockSpec((M, tn), lambda j, k: (0, j)), out_shape=_sds((M, N), BF16),
        scratch_shapes=[pltpu.VMEM((M, tn), F32)],
        compiler_params=_cp(("arbitrary", "arbitrary"), 48),
    )(a, b)


def _merge_bwd(dx1, merged, y_a, y_b, proj_g, w_pa, w_pb, w_out, tm, after=None):
    T = dx1.shape[0]

    nt = T // tm
    pshape = (A_WIDTH, D_MODEL)
    order = [] if after is None else [after]

    def body(*refs):
        dx_ref, mg_ref, ya_ref, yb_ref, g_ref, wpa_ref, wpb_ref, wo_ref = refs[:8]
        dg_ref, dya_ref, dyb_ref, gwo_out, gwpa_out, gwpb_out, gwo_ref, gwpa_ref, gwpb_ref = refs[8 + len(order):]
        i = pl.program_id(0)
        dx = dx_ref[...].astype(BF16)
        dm = _dot_nt(dx, wo_ref[...])
        g = g_ref[...].astype(F32)
        ya = ya_ref[...]
        yb = yb_ref[...]
        pa = _dot_stacked(ya, wpa_ref)
        pb = _dot_stacked(yb, wpb_ref)
        sa = _sigmoid(g[:, :D_MODEL])
        sb = _sigmoid(g[:, D_MODEL:])
        dpa = (dm * sa).astype(BF16)
        dpb = (dm * sb).astype(BF16)
        dg_ref[:, :D_MODEL] = (dm * pa * (sa * (1.0 - sa))).astype(BF16)
        dg_ref[:, D_MODEL:] = (dm * pb * (sb * (1.0 - sb))).astype(BF16)
        dya_ref[...] = _dot_nt_stacked(dpa, wpa_ref).astype(BF16)
        dyb_ref[...] = _dot_nt_stacked(dpb, wpb_ref).astype(BF16)

        @pl.when(i == 0)
        def _():
            for r in (gwo_ref, gwpa_ref, gwpb_ref):
                r[...] = jnp.zeros_like(r)

        gwo_ref[...] += _dot_tn(mg_ref[...], dx)
        gwpa_ref[...] += _dot_tn(ya, dpa)
        gwpb_ref[...] += _dot_tn(yb, dpb)

        @pl.when(i == nt - 1)
        def _():
            gwo_out[...] = gwo_ref[...].astype(BF16)
            gwpa_out[...] = gwpa_ref[...].astype(BF16)
            gwpb_out[...] = gwpb_ref[...].astype(BF16)

    return pl.pallas_call(
        body, name="merge_bwd", grid=(nt,),
        in_specs=[_row(tm, D_MODEL), _row(tm, D_MODEL), _row(tm, A_WIDTH), _row(tm, Q_DIM), _row(tm, G_DIM),
                  _resident(w_pa.shape), _resident(w_pb.shape), _resident(w_out.shape)] + [ANY] * len(order),
        out_specs=[_row(tm, G_DIM), _row(tm, A_WIDTH), _row(tm, Q_DIM),
                   _full(w_out.shape), _full(pshape), _full(pshape)],
        out_shape=[_sds((T, G_DIM), BF16), _sds((T, A_WIDTH), BF16), _sds((T, Q_DIM), BF16),
                   _sds(w_out.shape, BF16), _sds(pshape, BF16), _sds(pshape, BF16)],
        scratch_shapes=[pltpu.VMEM(w_out.shape, F32), pltpu.VMEM(pshape, F32), pltpu.VMEM(pshape, F32)],
        compiler_params=_cp(("arbitrary",), 56),
    )(dx1, merged, y_a, y_b, proj_g, w_pa, w_pb, w_out, *order)


def _sgu_bwd(proj_a, d_ya, g_sgu, w_s, b_st, tm):
    T = proj_a.shape[0]

    def body(p_ref, dy_ref, g_ref, ws_ref, bs_ref, dp_ref, gws_ref, gbs_ref, gg_ref):
        tril = _tril()
        g = g_ref[...]
        pu, pv, u, tu, vv, tv, rv, vn = _sgu_parts(p_ref[...].astype(F32), g)
        dy = dy_ref[...].astype(F32)

        @pl.when(pl.program_id(0) == 0)
        def _():
            for r in (gws_ref, gbs_ref, gg_ref):
                r[...] = jnp.zeros_like(r)

        du_cols = []
        dvn_cols = []
        for gi in range(A_GROUPS):
            wm = jnp.where(tril, ws_ref[gi], 0.0).astype(BF16)
            wmt = wm.astype(F32).T.astype(BF16)
            bcol = bs_ref[:, gi:gi + 1]
            cs = slice(gi * CHUNK, (gi + 1) * CHUNK)
            du_rows = []
            dvn_rows = []
            gw = jnp.zeros((CHUNK, CHUNK), F32)
            gb = jnp.zeros((CHUNK, 1), F32)
            for c in range(tm // CHUNK):
                rs = slice(c * CHUNK, (c + 1) * CHUNK)
                vn_c = vn[rs, cs]
                s = _dot(wm, vn_c) + bcol
                dy_c = dy[rs, cs]
                ds = dy_c * u[rs, cs]
                du_rows.append(dy_c * s)
                dsb = ds.astype(BF16)
                gw = gw + _dot_nt(dsb, vn_c)
                gb = gb + jnp.sum(ds, axis=-1, keepdims=True)
                dvn_rows.append(_dot(wmt, dsb))
            gws_ref[gi] += jnp.where(tril, gw, 0.0)
            gbs_ref[:, gi:gi + 1] += gb
            du_cols.append(jnp.concatenate(du_rows, axis=0))
            dvn_cols.append(jnp.concatenate(dvn_rows, axis=0))
        du = jnp.concatenate(du_cols, axis=1)
        dvn = jnp.concatenate(dvn_cols, axis=1)
        vhat = vv * rv
        gg_ref[...] += jnp.sum(dvn * vhat, axis=0, keepdims=True)
        dvv = _rms_bwd(dvn, vhat, rv, g)
        dp_ref[:, :A_WIDTH] = (du * _gelu_grad(pu, tu)).astype(BF16)
        dp_ref[:, A_WIDTH:] = (dvv * _gelu_grad(pv, tv)).astype(BF16)

    return pl.pallas_call(
        body, name="sgu_bwd", grid=(T // tm,),
        in_specs=[_row(tm, A_DIM), _row(tm, A_WIDTH), _full(g_sgu.shape), _full(w_s.shape), _full(b_st.shape)],
        out_specs=[_row(tm, A_DIM), _full(w_s.shape), _full(b_st.shape), _full(g_sgu.shape)],
        out_shape=[_sds((T, A_DIM), BF16), _sds(w_s.shape, F32), _sds(b_st.shape, F32), _sds(g_sgu.shape, F32)],
        compiler_params=_cp(("arbitrary",)),
    )(proj_a, d_ya, g_sgu, w_s, b_st)


def _attn_bwd(proj_b, d_yb, sinks, rel_bias, n_seq, seq):
    nb = seq // CHUNK
    bk = jnp.asarray(_band_buckets())

    def body(qkv_ref, do_ref, bk_ref, rel_ref, sink_ref, d_ref, gs_ref, gr_ref,
             bias_scr, sink_scr, kvar_scr, dbias_scr, dk_scr, dv_scr, ds_scr):
        b = pl.program_id(0)
        _attn_setup(bias_scr, sink_scr, kvar_scr, qkv_ref, bk_ref, rel_ref, sink_ref)
        ones = jnp.ones((2 * CHUNK, LANES), BF16)

        @pl.when(b == 0)
        def _():
            dbias_scr[...] = jnp.zeros_like(dbias_scr)
            ds_scr[...] = jnp.zeros_like(ds_scr)

        dk_scr[...] = jnp.zeros_like(dk_scr)
        dv_scr[...] = jnp.zeros_like(dv_scr)

        def transposed(a):
            return a.astype(F32).T.astype(BF16)

        def blk(n, carry):
            r0, kv, vv = _attn_block_inputs(kvar_scr, n)
            prob, psink = _attn_probs(qkv_ref, r0, n, kv, bias_scr, sink_scr, ones)
            dp = jnp.concatenate([_dot_nt(do_ref[pl.ds(r0, CHUNK), (h // 2) * LANES:(h // 2 + 1) * LANES], vv[h // 4][h % 2])
                                  for h in range(N_HEADS)], axis=0)
            delta = _rowsum(prob * dp, ones)
            dsc = prob * (dp - _both(delta))
            ds_scr[...] += psink * delta
            dbias_scr[...] += dsc
            dsb = (dsc * (HEAD_DIM ** -0.5)).astype(BF16)
            pb = prob.astype(BF16)
            dkt = [jnp.zeros((HEAD_DIM, 2 * CHUNK), F32) for _ in range(2)]
            dvt = [jnp.zeros((HEAD_DIM, 2 * CHUNK), F32) for _ in range(2)]
            for pr in range(N_HEADS // 2):
                ps = slice(pr * LANES, (pr + 1) * LANES)
                qpt = transposed(qkv_ref[pl.ds(r0, CHUNK), ps])
                dopt = transposed(do_ref[pl.ds(r0, CHUNK), ps])
                kvh = pr // 2
                dq = jnp.zeros((CHUNK, LANES), F32)
                for hh in range(2):
                    hr = _head_rows(2 * pr + hh)
                    rows = slice(hh * HEAD_DIM, (hh + 1) * HEAD_DIM)
                    dq = dq + _dot(dsb[hr], kv[kvh][hh])
                    dkt[kvh] = dkt[kvh] + _dot(qpt, dsb[hr])[rows]
                    dvt[kvh] = dvt[kvh] + _dot(dopt, pb[hr])[rows]
                d_ref[pl.ds(r0, CHUNK), ps] = dq.astype(BF16)
            dk_scr[:, pl.ds(r0, 2 * CHUNK)] += jnp.concatenate(dkt, axis=0)
            dv_scr[:, pl.ds(r0, 2 * CHUNK)] += jnp.concatenate(dvt, axis=0)
            return carry

        lax.fori_loop(0, nb, blk, 0)
        for n in range(nb):
            rows = slice(n * CHUNK, (n + 1) * CHUNK)
            cols = slice((n + 1) * CHUNK, (n + 2) * CHUNK)
            d_ref[rows, Q_DIM:Q_DIM + KV_DIM] = dk_scr[:, cols].T.astype(BF16)
            d_ref[rows, Q_DIM + KV_DIM:] = dv_scr[:, cols].T.astype(BF16)

        @pl.when(b == n_seq - 1)
        def _():
            bkv = bk_ref[...]
            for h in range(N_HEADS):
                gs_ref[0:1, h:h + 1] = -jnp.sum(ds_scr[_head_rows(h), 0:1], axis=0, keepdims=True)
                db = dbias_scr[_head_rows(h), :]
                for bb in range(N_BUCKETS):
                    part = jnp.sum(jnp.where(bkv == bb, db, 0.0), axis=-1, keepdims=True)
                    gr_ref[bb:bb + 1, h:h + 1] = jnp.sum(part, axis=0, keepdims=True)

    smem = pl.BlockSpec(memory_space=pltpu.SMEM)
    return pl.pallas_call(
        body, name="attn_bwd", grid=(n_seq,),
        in_specs=[_row(seq, B_DIM), _row(seq, Q_DIM), _full(bk.shape), smem, smem],
        out_specs=[_row(seq, B_DIM), _full((1, N_HEADS)), _full((N_BUCKETS, N_HEADS))],
        out_shape=[_sds((n_seq * seq, B_DIM), BF16), _sds((1, N_HEADS), F32), _sds((N_BUCKETS, N_HEADS), F32)],
        scratch_shapes=[pltpu.VMEM((HEAD_ROWS, 2 * CHUNK), F32), pltpu.VMEM((HEAD_ROWS, LANES), F32),
                        pltpu.VMEM((8, seq, KV_DIM), BF16), pltpu.VMEM((HEAD_ROWS, 2 * CHUNK), F32),
                        pltpu.VMEM((KV_DIM, seq + CHUNK), F32), pltpu.VMEM((KV_DIM, seq + CHUNK), F32),
                        pltpu.VMEM((HEAD_ROWS, LANES), F32)],
        compiler_params=_cp(("arbitrary",), 40),
    )(proj_b, d_yb, bk, rel_bias, sinks)


def _inproj_bwd(d_g, d_a, d_b, x2, dx1, g_mix, w_g, w_a, w_b, tm):
    T = x2.shape[0]

    def body(dg_ref, da_ref, db_ref, x_ref, dx1_ref, g_ref, wg_ref, wa_ref, wb_ref, gx_ref, gg_ref):
        dh = _dot_nt(dg_ref[...], wg_ref[...]) + _dot_nt(da_ref[...], wa_ref[...]) + _dot_nt(db_ref[...], wb_ref[...])
        x = x_ref[...]
        r = _rms_r(x)
        n = x * r
        gx_ref[...] = dx1_ref[...] + _rms_bwd(dh, n, r, g_ref[...])

        @pl.when(pl.program_id(0) == 0)
        def _():
            gg_ref[...] = jnp.zeros_like(gg_ref)

        gg_ref[...] += jnp.sum(dh * n, axis=0, keepdims=True)

    return pl.pallas_call(
        body, name="inproj_bwd", grid=(T // tm,),
        in_specs=[_row(tm, G_DIM), _row(tm, A_DIM), _row(tm, B_DIM), _row(tm, D_MODEL), _row(tm, D_MODEL),
                  _full(g_mix.shape), _resident(w_g.shape), _resident(w_a.shape), _resident(w_b.shape)],
        out_specs=[_row(tm, D_MODEL), _full((1, D_MODEL))],
        out_shape=[_sds((T, D_MODEL), F32), _sds((1, D_MODEL), F32)],
        compiler_params=_cp(("arbitrary",), 48),
    )(d_g, d_a, d_b, x2, dx1, g_mix, w_g, w_a, w_b)


def _local_step(x, target, g_mix, g_sgu, w_s, b_s, sinks, rel_bias, g_ffn, b_conv, g_final,
                w_g, w_a, w_b, w_pa, w_pb, w_out, w_conv, ffn_weights, on_ffn_grads, after=None):
    n_seq, seq, _ = x.shape
    T = n_seq * seq
    tm = min(256, seq)
    x2 = x.reshape(T, D_MODEL)
    tgt = target.reshape(T, D_MODEL)
    b_st = b_s.T
    g_fin = g_final.reshape(1, D_MODEL)

    proj_g, proj_a, proj_b, h = _inproj(x2, g_mix, w_g, w_a, w_b, tm, after)
    y_a = _sgu_fwd(proj_a, g_sgu, w_s, b_st, tm)
    y_b = _attn_fwd(proj_b, sinks, rel_bias, n_seq, seq)
    w_up, w_down = ffn_weights(y_b)
    x1, merged = _merge_fwd(x2, y_a, y_b, proj_g, w_pa, w_pb, w_out, tm)
    upre, h2 = _upproj(x1, g_ffn, w_up, tm)
    dx2, loss, gg_final, gate, val = _ffn_down_loss(upre, x1, tgt, w_conv, b_conv, w_down, g_fin, tm, seq)

    d_gate, d_val, gw_down, gb_g, gb_v = _ffn_bwd_act(gate, val, dx2, w_down, tm)
    gb_conv = jnp.concatenate([gb_g, gb_v], axis=1)
    d_upre, dx1, gg_ffn, gw_conv = _ffn_bwd_up(d_gate, d_val, upre, dx2, x1, g_ffn, w_conv, w_up, tm, seq)
    gw_up = _matmul_tn(h2, d_upre, 2 * D_FF // 4, min(512, T), "grad_w_up")
    sent = on_ffn_grads(gw_up, gw_down)
    d_g, d_ya, d_yb, gw_out, gw_pa, gw_pb = _merge_bwd(dx1, merged, y_a, y_b, proj_g, w_pa, w_pb, w_out, tm, sent)
    d_a, gw_s, gb_st, gg_sgu = _sgu_bwd(proj_a, d_ya, g_sgu, w_s, b_st, tm)
    d_b, g_sinks, g_rel = _attn_bwd(proj_b, d_yb, sinks, rel_bias, n_seq, seq)
    grad_x, gg_mix = _inproj_bwd(d_g, d_a, d_b, x2, dx1, g_mix, w_g, w_a, w_b, tm)
    gw_g = _matmul_tn(h, d_g, D_MODEL, min(512, T), "grad_w_in_gate")
    gw_a = _matmul_tn(h, d_a, A_DIM, min(512, T), "grad_w_in_a")
    gw_b = _matmul_tn(h, d_b, B_DIM, min(512, T), "grad_w_in_b")
    gw_in = jnp.concatenate([gw_a, gw_b, gw_g], axis=1).reshape(D_MODEL, N_CHIPS, -1).transpose(1, 0, 2)

    small = dict(g_mix=gg_mix, g_sgu=gg_sgu, w_s=gw_s, b_s=gb_st.T, sinks=g_sinks, rel_bias=g_rel,
                 g_ffn=gg_ffn, b_conv=gb_conv, g_final=gg_final, w_conv=gw_conv)
    big = dict(w_in=gw_in, w_pa=gw_pa, w_pb=gw_pb, w_out=gw_out, w_up=gw_up, w_down=gw_down)
    return loss, grad_x.reshape(x.shape), small, big


_MIXER = ("w_in", "w_pa", "w_pb", "w_out")
_FFN = ("w_up", "w_down")
_BIG = _MIXER + _FFN

_SMALL = (("loss", (1, 1)), ("g_final", (1, D_MODEL)), ("g_mix", (1, D_MODEL)), ("g_ffn", (1, D_MODEL)),
          ("g_sgu", (1, A_WIDTH)), ("b_s", (A_GROUPS, CHUNK)), ("sinks", (1, N_HEADS)), ("rel_bias", (N_BUCKETS, N_HEADS)),
          ("b_conv", (1, 2 * D_FF)), ("w_conv", (3, 2 * D_FF)), ("w_s", (A_GROUPS, CHUNK, CHUNK)))
SMALL_ROWS = 96


def _pack_small(vals):
    flat = jnp.concatenate([vals[n].astype(F32).reshape(-1) for n, _ in _SMALL])
    flat = jnp.pad(flat, (0, SMALL_ROWS * D_MODEL - flat.shape[0]))
    return flat.reshape(SMALL_ROWS, D_MODEL)


def _unpack_small(buf):
    flat = buf.reshape(-1)
    out = {}
    off = 0
    for n, shp in _SMALL:
        k = int(np.prod(shp))
        out[n] = flat[off:off + k].reshape(shp)
        off += k
    return out


def _mesh_pos():
    return lax.axis_index("x"), lax.axis_index("y"), lax.axis_index("c")


def _other_chips(x, y):
    return [(1 - x, y), (x, 1 - y), (1 - x, 1 - y)]


def _remote(src, dst, send_sem, recv_sem, to):
    return pltpu.make_async_remote_copy(src_ref=src, dst_ref=dst, send_sem=send_sem, recv_sem=recv_sem,
                                        device_id=to, device_id_type=MESH)


def _own_slot(own, n, at):
    return lax.dynamic_update_slice(lax.empty((n,) + own.shape, own.dtype), own[None], (at,) + (0,) * own.ndim)


def _allgather_weights(stacks, wc_stack):
    names = list(stacks)
    n = len(names)

    def body(*refs):
        ins, outs = refs[:n + 1], refs[n + 1:2 * n + 2]
        send_sems, recv_sems = refs[2 * n + 2:]
        x, y, c = _mesh_pos()
        me = 2 * x + y
        sibling = (x, y, 1 - c)
        chips = _other_chips(x, y)

        def half(ref, chip, hc):
            hr = ref.shape[1] // 2
            return ref.at[chip, pl.ds(hc * hr, hr), :]

        first = []
        for k in range(n):
            first += [_remote(half(ins[k], me, c), half(outs[k], me, c), send_sems.at[6 * k + j], recv_sems.at[6 * k + j], (cx, cy, c))
                      for j, (cx, cy) in enumerate(chips)]
        first += [_remote(ins[n].at[me], outs[n].at[me], send_sems.at[6 * n + j], recv_sems.at[6 * n + j], (cx, cy, c))
                  for j, (cx, cy) in enumerate(chips)]
        for cp in first:
            cp.start()
        passed = []
        for k in range(n):
            for j, (cx, cy) in enumerate(chips):
                landed = half(outs[k], 2 * cx + cy, c)
                _remote(landed, landed, send_sems.at[6 * k + j], recv_sems.at[6 * k + j], (x, y, c)).wait_recv()
                passed.append(_remote(landed, landed, send_sems.at[6 * k + 3 + j], recv_sems.at[6 * k + 3 + j], sibling))
                passed[-1].start()
        for k in range(n):
            for j, (cx, cy) in enumerate(chips):
                theirs = half(outs[k], 2 * cx + cy, 1 - c)
                _remote(theirs, theirs, send_sems.at[6 * k + 3 + j], recv_sems.at[6 * k + 3 + j], (x, y, c)).wait_recv()
        for j, (cx, cy) in enumerate(chips):
            slot = outs[n].at[2 * cx + cy]
            _remote(slot, slot, send_sems.at[6 * n + j], recv_sems.at[6 * n + j], (x, y, c)).wait_recv()
        for cp in first + passed:
            cp.wait_send()

    arrays = [stacks[k] for k in names] + [wc_stack]
    outs = pl.pallas_call(
        body, name="allgather_weights",
        in_specs=[HBM] * (n + 1), out_specs=[HBM] * (n + 1), input_output_aliases={k: k for k in range(n + 1)},
        out_shape=[_sds(a.shape, a.dtype) for a in arrays],
        scratch_shapes=[pltpu.SemaphoreType.DMA((6 * n + 3,)), pltpu.SemaphoreType.DMA((6 * n + 3,))],
    )(*arrays)
    return dict(zip(names, outs[:n])), outs[n]


_KIND = {"w_in": "stack", "w_pa": "col", "w_pb": "col", "w_up": "col", "w_out": "row", "w_down": "row"}


def _half_view(ref, kind, h):
    if kind == "stack":
        k = ref.shape[1] // 2
        return ref.at[:, pl.ds(h * k, k), :]
    if kind == "col":
        k = ref.shape[0] // 2
        return ref.at[pl.ds(h * k, k), :]
    k = ref.shape[1] // 2
    return ref.at[:, pl.ds(h * k, k)]


def _shard_view(ref, kind, i):
    if kind == "stack":
        return ref.at[i]
    if kind == "col":
        k = ref.shape[1] // N_CHIPS
        return ref.at[:, pl.ds(i * k, k)]
    k = ref.shape[0] // N_CHIPS
    return ref.at[pl.ds(i * k, k), :]


def _region_view(ref, kind, h):
    if kind == "row":
        k = ref.shape[1] // 2
        return ref.at[:, pl.ds(h * k, k)]
    k = ref.shape[0] // 2
    return ref.at[pl.ds(h * k, k), :]


def _half_shape(shape, kind):
    if kind == "stack":
        return (shape[0], shape[1] // 2, shape[2])
    return (shape[0] // 2, shape[1]) if kind == "col" else (shape[0], shape[1] // 2)


def _part_shape(half_shape, kind):
    if kind == "stack":
        return tuple(half_shape[1:])
    k, w = half_shape
    return (k, w // N_CHIPS) if kind == "col" else (k // N_CHIPS, w)


def _pair_exchange(parts, tag):
    names = list(parts)
    n = len(names)

    def body(*refs):
        p, q = refs[:n], refs[n:2 * n]
        send_sems, recv_sems = refs[2 * n:]
        x, y, c = _mesh_pos()

        def copy(k, h):
            return _remote(_half_view(p[k], _KIND[names[k]], h), q[k], send_sems.at[k], recv_sems.at[k], (x, y, 1 - c))

        for hc in range(2):
            @pl.when(c == hc)
            def _():
                for k in range(n):
                    copy(k, 1 - hc).start()

        for k in range(n):
            copy(k, 0).wait()

    outs = pl.pallas_call(
        body, name="grad_pair_exchange_" + tag, in_specs=[HBM] * n, out_specs=[HBM] * n,
        out_shape=[_sds(_half_shape(parts[k].shape, _KIND[k]), parts[k].dtype) for k in names],
        scratch_shapes=[pltpu.SemaphoreType.DMA((n,)), pltpu.SemaphoreType.DMA((n,))],
    )(*[parts[k] for k in names])
    return dict(zip(names, outs))


def _half_blocks(shape, kind):
    if kind == "stack":
        _, k, w = shape
        tr = 256
        nb = k // 2 // tr
        return (N_CHIPS, nb), (1, tr, w), (lambda i, r, s: (i, r, 0)), (lambda i, r, s: (i, s[1] * nb + r, 0))
    k, w = shape
    if kind == "col":
        tr = 256 if w <= 2 * D_MODEL else 128
        nb = k // 2 // tr
        return (nb,), (tr, w), (lambda r, s: (r, 0)), (lambda r, s: (s[1] * nb + r, 0))
    tr = k // N_CHIPS
    return (N_CHIPS,), (tr, w // 2), (lambda r, s: (r, 0)), (lambda r, s: (r, s[1]))


def _pair_add(part, from_sibling, name, pos):
    kind = _KIND[name]
    grid, block, half_map, full_map = _half_blocks(part.shape, kind)

    def body(s_ref, p_ref, q_ref, o_ref):
        o_ref[...] = (p_ref[...].astype(F32) + q_ref[...].astype(F32)).astype(BF16)

    return pl.pallas_call(
        body, name="grad_pair_add_" + name,
        grid_spec=pltpu.PrefetchScalarGridSpec(
            num_scalar_prefetch=1, grid=grid,
            in_specs=[pl.BlockSpec(block, full_map), pl.BlockSpec(block, half_map)],
            out_specs=pl.BlockSpec(block, half_map)),
        out_shape=_sds(from_sibling.shape, BF16),
        compiler_params=_cp(("arbitrary",) * len(grid)),
    )(pos, part, from_sibling)


def _chip_exchange(sums):
    names = list(sums)
    n = len(names)

    def body(*refs):
        s, r = refs[:n], refs[n:2 * n]
        send_sems, recv_sems = refs[2 * n:]
        x, y, c = _mesh_pos()
        me = 2 * x + y

        def copy(k, i, j, to):
            return _remote(_shard_view(s[k], _KIND[names[k]], i), r[k].at[j], send_sems.at[3 * k + j], recv_sems.at[3 * k + j], to)

        for i in range(N_CHIPS):
            xi, yi = i // 2, i % 2
            j = jnp.where(xi != x, jnp.where(yi != y, 2, 0), 1)

            @pl.when(i != me)
            def _():
                for k in range(n):
                    copy(k, i, j, (xi, yi, c)).start()

        for k in range(n):
            for j in range(3):
                copy(k, 0, j, (x, y, c)).wait()

    outs = pl.pallas_call(
        body, name="grad_chip_exchange", in_specs=[HBM] * n, out_specs=[HBM] * n,
        out_shape=[_sds((3,) + _part_shape(sums[k].shape, _KIND[k]), sums[k].dtype) for k in names],
        scratch_shapes=[pltpu.SemaphoreType.DMA((3 * n,)), pltpu.SemaphoreType.DMA((3 * n,))],
    )(*[sums[k] for k in names])
    return dict(zip(names, outs))


_DATAFLOW = pltpu.SideEffectType.DATAFLOW_SIDE_EFFECTING
_TOKEN = (SUBLANES, LANES)


def _chip_exchange_start(sums):
    names = list(sums)
    n = len(names)
    lands = [lax.empty((3,) + _part_shape(sums[k].shape, _KIND[k]), sums[k].dtype) for k in names]

    def body(*refs):
        s, r = refs[:n], refs[n:2 * n]
        send_sems, recv_sems = refs[2 * n:2 * n + 2]
        token = refs[-1]
        x, y, c = _mesh_pos()
        me = 2 * x + y
        for i in range(N_CHIPS):
            xi, yi = i // 2, i % 2
            j = jnp.where(xi != x, jnp.where(yi != y, 2, 0), 1)

            @pl.when(i != me)
            def _():
                for k in range(n):
                    _remote(_shard_view(s[k], _KIND[names[k]], i), r[k].at[j], send_sems.at[3 * k + j], recv_sems.at[3 * k + j],
                            (xi, yi, c)).start()

        token[...] = jnp.zeros_like(token)

    arrays = [sums[k] for k in names] + lands
    outs = pl.pallas_call(
        body, name="grad_chip_exchange_start",
        in_specs=[HBM] * (2 * n), out_specs=[SEM, SEM] + [HBM] * (2 * n) + [pl.BlockSpec(memory_space=pltpu.VMEM)],
        out_shape=[pltpu.SemaphoreType.DMA((3 * n,)), pltpu.SemaphoreType.DMA((3 * n,))]
        + [pltpu.HBM(a.shape, a.dtype) for a in arrays] + [_sds(_TOKEN, F32)],
        input_output_aliases={k: 2 + k for k in range(2 * n)},
        compiler_params=pltpu.CompilerParams(has_side_effects=_DATAFLOW),
    )(*[pltpu.with_memory_space_constraint(a, pltpu.HBM) for a in arrays])
    return names, outs[0], outs[1], list(outs[2:2 + n]), list(outs[2 + n:2 + 2 * n]), outs[-1]


def _chip_exchange_wait(started, after):
    names, send_sems, recv_sems, sums, lands, _ = started
    n = len(names)

    def body(*refs):
        s, r = refs[:n], refs[n:2 * n]
        send, recv = refs[2 * n], refs[2 * n + 1]
        x, y, c = _mesh_pos()
        for k in range(n):
            for j in range(3):
                cp = _remote(_shard_view(s[k], _KIND[names[k]], 0), r[k].at[j], send.at[3 * k + j], recv.at[3 * k + j], (x, y, c))
                cp.wait_send()
                cp.wait_recv()

    outs = pl.pallas_call(
        body, name="grad_chip_exchange_wait",
        in_specs=[HBM] * (2 * n) + [SEM, SEM, ANY], out_specs=[HBM] * (2 * n),
        out_shape=[pltpu.HBM(a.shape, a.dtype) for a in sums + lands],
        input_output_aliases={k: k for k in range(2 * n)},
        compiler_params=pltpu.CompilerParams(has_side_effects=_DATAFLOW),
    )(*sums, *lands, send_sems, recv_sems, after)
    return dict(zip(names, outs[n:]))


def _allgather_start(stacks, after):
    names = list(stacks)
    n = len(names)

    def body(*refs):
        st = refs[:n]
        send_sems, recv_sems = refs[n + 1:n + 3]
        token = refs[-1]
        x, y, c = _mesh_pos()
        me = 2 * x + y
        for k in range(n):
            hr = st[k].shape[1] // 2
            mine = st[k].at[me, pl.ds(c * hr, hr), :]
            for j, (cx, cy) in enumerate(_other_chips(x, y)):
                _remote(mine, mine, send_sems.at[3 * k + j], recv_sems.at[3 * k + j], (cx, cy, c)).start()
        token[...] = jnp.zeros_like(token)

    arrays = [stacks[k] for k in names]
    outs = pl.pallas_call(
        body, name="allgather_start",
        in_specs=[HBM] * n + [ANY], out_specs=[SEM, SEM] + [HBM] * n + [pl.BlockSpec(memory_space=pltpu.VMEM)],
        out_shape=[pltpu.SemaphoreType.DMA((3 * n,)), pltpu.SemaphoreType.DMA((3 * n,))]
        + [pltpu.HBM(a.shape, a.dtype) for a in arrays] + [_sds(_TOKEN, F32)],
        input_output_aliases={k: 2 + k for k in range(n)},
        compiler_params=pltpu.CompilerParams(has_side_effects=_DATAFLOW),
    )(*[pltpu.with_memory_space_constraint(a, pltpu.HBM) for a in arrays], after)
    return names, outs[0], outs[1], list(outs[2:2 + n]), outs[-1]


def _allgather_wait(started, after):
    names, send_sems, recv_sems, stacks, _ = started
    n = len(names)

    def body(*refs):
        st = refs[:n]
        send, recv = refs[n], refs[n + 1]
        x, y, c = _mesh_pos()
        for k in range(n):
            hr = st[k].shape[1] // 2
            for j in range(3):
                slot = st[k].at[0, pl.ds(0, hr), :]
                cp = _remote(slot, slot, send.at[3 * k + j], recv.at[3 * k + j], (x, y, c))
                cp.wait_send()
                cp.wait_recv()

    outs = pl.pallas_call(
        body, name="allgather_wait",
        in_specs=[HBM] * n + [SEM, SEM, ANY], out_specs=[HBM] * n,
        out_shape=[pltpu.HBM(a.shape, a.dtype) for a in stacks],
        input_output_aliases={k: k for k in range(n)},
        compiler_params=pltpu.CompilerParams(has_side_effects=_DATAFLOW),
    )(*stacks, send_sems, recv_sems, after)
    return dict(zip(names, outs))


def _allgather_forward(stacks):
    names = list(stacks)
    n = len(names)

    def body(*refs):
        ins, outs = refs[:n], refs[n:2 * n]
        send_sems, recv_sems = refs[2 * n:]
        x, y, c = _mesh_pos()
        copies = []
        for k in range(n):
            hr = ins[k].shape[1] // 2
            for j, (cx, cy) in enumerate(_other_chips(x, y)):
                chip = 2 * cx + cy
                copies.append(_remote(ins[k].at[chip, pl.ds(c * hr, hr), :], outs[k].at[chip, pl.ds(c * hr, hr), :],
                                      send_sems.at[3 * k + j], recv_sems.at[3 * k + j], (x, y, 1 - c)))
        for cp in copies:
            cp.start()
        for cp in copies:
            cp.wait()

    arrays = [stacks[k] for k in names]
    outs = pl.pallas_call(
        body, name="allgather_forward", in_specs=[HBM] * n, out_specs=[HBM] * n,
        input_output_aliases={k: k for k in range(n)},
        out_shape=[_sds(a.shape, a.dtype) for a in arrays],
        scratch_shapes=[pltpu.SemaphoreType.DMA((3 * n,)), pltpu.SemaphoreType.DMA((3 * n,))],
    )(*arrays)
    return dict(zip(names, outs))


def _owner_sum(part, from_sibling, from_chips, name, pos, shard_shape):
    kind = _KIND[name]
    _, pk, pw = from_chips.shape
    if kind == "row":
        tr, nb = pk, 1
        p_spec = pl.BlockSpec((tr, pw), lambda r, s: (s[0], s[1]))
        q_spec = pl.BlockSpec((tr, pw), lambda r, s: (s[0], 0))
        o_spec = pl.BlockSpec((tr, pw), lambda r, s: (0, s[1]))
    else:
        tr = 256
        nb = pk // tr
        if kind == "stack":
            p_spec = pl.BlockSpec((None, tr, pw), lambda r, s: (s[0], s[1] * nb + r, 0))
            q_spec = pl.BlockSpec((None, tr, pw), lambda r, s: (s[0], r, 0))
        else:
            p_spec = pl.BlockSpec((tr, pw), lambda r, s: (s[1] * nb + r, s[0]))
            q_spec = pl.BlockSpec((tr, pw), lambda r, s: (r, s[0]))
        o_spec = pl.BlockSpec((tr, pw), lambda r, s: (s[1] * nb + r, 0))

    def body(s_ref, p_ref, q_ref, r_ref, o_ref):
        acc = p_ref[...].astype(F32) + q_ref[...].astype(F32)
        for j in range(3):
            acc = acc + r_ref[j].astype(F32)
        o_ref[...] = acc

    return pl.pallas_call(
        body, name="grad_owner_sum_" + name,
        grid_spec=pltpu.PrefetchScalarGridSpec(
            num_scalar_prefetch=1, grid=(nb,),
            in_specs=[p_spec, q_spec, pl.BlockSpec((3, tr, pw), lambda r, s: (0, r, 0))],
            out_specs=o_spec),
        out_shape=_sds(shard_shape, F32),
        compiler_params=_cp(("arbitrary",), 32),
    )(pos, part, from_sibling, from_chips)


def _pair_share(shards):
    names = list(shards)
    n = len(names)

    def body(*refs):
        g_in, g_out = refs[:n], refs[n:2 * n]
        send_sems, recv_sems = refs[2 * n:]
        x, y, c = _mesh_pos()

        def copy(k, h):
            kind = _KIND[names[k]]
            return _remote(_region_view(g_in[k], kind, h), _region_view(g_out[k], kind, h), send_sems.at[k], recv_sems.at[k], (x, y, 1 - c))

        for hc in range(2):
            @pl.when(c == hc)
            def _():
                for k in range(n):
                    copy(k, hc).start()

        for k in range(n):
            copy(k, 0).wait()

    outs = pl.pallas_call(
        body, name="grad_pair_share", in_specs=[HBM] * n, out_specs=[HBM] * n, input_output_aliases={k: k for k in range(n)},
        out_shape=[_sds(shards[k].shape, shards[k].dtype) for k in names],
        scratch_shapes=[pltpu.SemaphoreType.DMA((n,)), pltpu.SemaphoreType.DMA((n,))],
    )(*[shards[k] for k in names])
    return dict(zip(names, outs))


def _allgather_small(block):
    m_per = block.shape[0]

    def body(x_ref, out_ref, send_sems, recv_sems, local_sem):
        x, y, c = _mesh_pos()
        me, sibling = (x, y, c), (x, y, 1 - c)
        chips = _other_chips(x, y)

        def rows(px, py, pc):
            return out_ref.at[4 * px + 2 * py + pc]

        def copy(k, block_of, to, src=None):
            return _remote(rows(*block_of) if src is None else src, rows(*block_of), send_sems.at[k], recv_sems.at[k], to)

        mine = pltpu.make_async_copy(x_ref, rows(*me), local_sem)
        mine.start()
        first = [copy(0, me, sibling, src=x_ref)]
        first += [copy(1 + j, me, (*chip, c), src=x_ref) for j, chip in enumerate(chips)]
        for cp in first:
            cp.start()
        passed = [copy(4 + j, (*chip, c), sibling) for j, chip in enumerate(chips)]
        for j, chip in enumerate(chips):
            copy(1 + j, (*chip, c), me).wait_recv()
            passed[j].start()
        copy(0, sibling, me).wait_recv()
        for j, chip in enumerate(chips):
            copy(4 + j, (*chip, 1 - c), me).wait_recv()
        for cp in first + passed:
            cp.wait_send()
        mine.wait()

    return pl.pallas_call(
        body, name="allgather_small",
        in_specs=[pl.BlockSpec(memory_space=pltpu.VMEM)], out_specs=pl.BlockSpec(memory_space=pltpu.VMEM),
        out_shape=_sds((N_DEV, m_per, D_MODEL), block.dtype),
        scratch_shapes=[pltpu.SemaphoreType.DMA((7,)), pltpu.SemaphoreType.DMA((7,)), pltpu.SemaphoreType.DMA],
    )(block)


def _adam_math(w, g, m, v):
    m = ADAM_B1 * m + (1.0 - ADAM_B1) * g
    v = ADAM_B2 * v + (1.0 - ADAM_B2) * (g * g)
    m_hat = m / (1.0 - ADAM_B1 ** ADAM_STEP)
    v_hat = v / (1.0 - ADAM_B2 ** ADAM_STEP)
    delta = -ADAM_LR * (m_hat / (jnp.sqrt(v_hat) + ADAM_EPS) + ADAM_WD * w)
    return delta, m, v


def _adamw(w, g, m, v, name):
    rows, cols = w.shape
    tr = rows
    for cand in (256, 128, 64, 32, 16, 8):
        if rows % cand == 0 and rows > cand:
            tr = cand
            break

    def body(w_ref, g_ref, m_ref, v_ref, d_ref, nm_ref, nv_ref):
        d, nm, nv = _adam_math(w_ref[...], g_ref[...], m_ref[...], v_ref[...])
        d_ref[...] = d
        nm_ref[...] = nm
        nv_ref[...] = nv

    spec = pl.BlockSpec((tr, cols), lambda i: (i, 0))
    return pl.pallas_call(
        body, name=name, grid=(rows // tr,), in_specs=[spec] * 4, out_specs=[spec] * 3,
        out_shape=[_sds(w.shape, F32)] * 3, compiler_params=_cp(("arbitrary",)),
    )(w, g, m, v)


def _small_sum_adamw(gathered, w, m, v):
    def body(a_ref, w_ref, m_ref, v_ref, g_ref, d_ref, nm_ref, nv_ref):
        g = a_ref[0]
        for k in range(1, N_DEV):
            g = g + a_ref[k]
        g_ref[...] = g
        d, nm, nv = _adam_math(w_ref[...], g, m_ref[...], v_ref[...])
        d_ref[...] = d
        nm_ref[...] = nm
        nv_ref[...] = nv

    return pl.pallas_call(
        body, name="small_sum_adamw", out_shape=[_sds(w.shape, F32)] * 4,
    )(gathered, w, m, v)


_NAMES = ("g_mix", "w_in", "g_sgu", "w_s", "b_s", "sinks", "rel_bias", "w_pa", "w_pb", "w_out",
          "g_ffn", "w_up", "w_conv", "b_conv", "w_down", "g_final")

def kernel(x, g_mix, w_in, g_sgu, w_s, b_s, sinks, rel_bias, w_pa, w_pb, w_out, g_ffn, w_up, w_conv, b_conv, w_down, g_final, loss_target, m_g_mix, m_w_in, m_g_sgu, m_w_s, m_b_s, m_sinks, m_rel_bias, m_w_pa, m_w_pb, m_w_out, m_g_ffn, m_w_up, m_w_conv, m_b_conv, m_w_down, m_g_final, v_g_mix, v_w_in, v_g_sgu, v_w_s, v_b_s, v_sinks, v_rel_bias, v_w_pa, v_w_pb, v_w_out, v_g_ffn, v_w_up, v_w_conv, v_b_conv, v_w_down, v_g_final):
    w = dict(g_mix=g_mix, w_in=w_in, g_sgu=g_sgu, w_s=w_s, b_s=b_s, sinks=sinks, rel_bias=rel_bias, w_pa=w_pa, w_pb=w_pb,
             w_out=w_out, g_ffn=g_ffn, w_up=w_up, w_conv=w_conv, b_conv=b_conv, w_down=w_down, g_final=g_final)
    m = dict(g_mix=m_g_mix, w_in=m_w_in, g_sgu=m_g_sgu, w_s=m_w_s, b_s=m_b_s, sinks=m_sinks, rel_bias=m_rel_bias, w_pa=m_w_pa,
             w_pb=m_w_pb, w_out=m_w_out, g_ffn=m_g_ffn, w_up=m_w_up, w_conv=m_w_conv, b_conv=m_b_conv, w_down=m_w_down,
             g_final=m_g_final)
    v = dict(g_mix=v_g_mix, w_in=v_w_in, g_sgu=v_g_sgu, w_s=v_w_s, b_s=v_b_s, sinks=v_sinks, rel_bias=v_rel_bias, w_pa=v_w_pa,
             w_pb=v_w_pb, w_out=v_w_out, g_ffn=v_g_ffn, w_up=v_w_up, w_conv=v_w_conv, b_conv=v_b_conv, w_down=v_w_down,
             g_final=v_g_final)
    xi, yi, ci = _mesh_pos()
    me = 2 * xi + yi

    shard = {n: w[n][0] for n in _BIG}
    shard_shapes = {n: shard[n].shape for n in _BIG}
    wc_shard = w["w_conv"][0]
    wc_pad = jnp.pad(wc_shard, ((0, 5), (0, 0)))
    own = {n: _own_slot(shard[n].astype(BF16), N_CHIPS, me) for n in _BIG}
    stacks, wc_all = _allgather_weights({n: own[n] for n in _MIXER}, _own_slot(wc_pad, N_CHIPS, me))
    ffn_gather = _allgather_start({n: own[n] for n in _FFN}, stacks["w_in"])
    w_conv_full = jnp.concatenate([wc_all[i, :3] for i in range(N_CHIPS)], axis=1)
    w_in_full = stacks["w_in"].transpose(1, 0, 2).reshape(D_MODEL, -1)
    w_a = w_in_full[:, :A_DIM]
    w_b = w_in_full[:, A_DIM:A_DIM + B_DIM]
    w_g = w_in_full[:, A_DIM + B_DIM:]
    pos = jnp.stack([me, ci])

    def ffn_weights(done):
        st = _allgather_forward(_allgather_wait(ffn_gather, done))
        return st["w_up"], st["w_down"].reshape(D_FF, D_MODEL)

    ffn_exchange = []

    def on_ffn_grads(gw_up, gw_down):
        parts = dict(w_up=gw_up, w_down=gw_down)
        sib = _pair_exchange(parts, "ffn")
        started = _chip_exchange_start({n: _pair_add(parts[n], sib[n], n, pos) for n in _FFN})
        ffn_exchange.append((sib, started))
        return started[-1]

    loss, grad_x, small, big = _local_step(
        x, loss_target, w["g_mix"], w["g_sgu"], w["w_s"][0], w["b_s"][0], w["sinks"], w["rel_bias"], w["g_ffn"],
        w["b_conv"], w["g_final"], w_g, w_a, w_b, stacks["w_pa"], stacks["w_pb"], stacks["w_out"].reshape(D_MODEL, D_MODEL),
        w_conv_full, ffn_weights, on_ffn_grads, ffn_gather[-1])

    small["loss"] = loss
    all_small = _allgather_small(_pack_small(small))
    sw = {n: (jnp.zeros((1, 1), F32) if n in ("loss", "w_conv") else w[n]) for n, _ in _SMALL}
    sm = {n: (jnp.zeros((1, 1), F32) if n in ("loss", "w_conv") else m[n]) for n, _ in _SMALL}
    sv = {n: (jnp.zeros((1, 1), F32) if n in ("loss", "w_conv") else v[n]) for n, _ in _SMALL}
    for d in (sw, sm, sv):
        d["w_conv"] = jnp.zeros((3, 2 * D_FF), F32)
    s_g, s_d, s_m, s_v = [_unpack_small(a) for a in _small_sum_adamw(all_small, _pack_small(sw), _pack_small(sm), _pack_small(sv))]

    from_sibling, started = ffn_exchange[0]
    from_chips = _chip_exchange_wait(started, grad_x)
    mixer = {n: big[n] for n in _MIXER}
    sib = _pair_exchange(mixer, "mixer")
    from_sibling.update(sib)
    from_chips.update(_chip_exchange({n: _pair_add(mixer[n], sib[n], n, pos) for n in _MIXER}))
    g_big = _pair_share({n: _owner_sum(big[n], from_sibling[n], from_chips[n], n, pos, shard_shapes[n]) for n in _BIG})

    grads, deltas, new_m, new_v = {}, {}, {}, {}
    for n in _BIG:
        d, nm, nv = _adamw(shard[n], g_big[n], m[n][0], v[n][0], "adamw_" + n)
        grads[n], deltas[n], new_m[n], new_v[n] = g_big[n][None], d[None], nm[None], nv[None]
    wcols = wc_shard.shape[1]
    g_wc = lax.dynamic_slice(s_g["w_conv"], (0, me * wcols), (3, wcols))
    d, nm, nv = _adamw(wc_shard, g_wc, m["w_conv"][0], v["w_conv"][0], "adamw_w_conv")
    grads["w_conv"], deltas["w_conv"], new_m["w_conv"], new_v["w_conv"] = g_wc[None], d[None], nm[None], nv[None]
    for n, _ in _SMALL:
        if n in ("loss", "w_conv"):
            continue
        shp = w[n].shape
        grads[n], deltas[n], new_m[n], new_v[n] = (s_g[n].reshape(shp), s_d[n].reshape(shp), s_m[n].reshape(shp),
                                                    s_v[n].reshape(shp))

    return (s_g["loss"].reshape(()), grad_x, *[grads[n] for n in _NAMES], *[deltas[n] for n in _NAMES],
            *[new_m[n] for n in _NAMES], *[new_v[n] for n in _NAMES])
```

```python
import functools

import numpy as np
import jax
import jax.numpy as jnp
from jax import lax
from jax.experimental import pallas as pl
from jax.experimental.pallas import tpu as pltpu

F32 = jnp.float32
BF16 = jnp.bfloat16

D_MODEL = 1024
CHUNK = 128
A_GROUPS = 4
A_WIDTH = 512
N_HEADS = 8
HEAD_DIM = 64
Q_DIM = 512
KV_DIM = 128
N_BUCKETS = 32
MAX_DISTANCE = 128
D_FF = 2816
EPS = 1e-6
NEG_INF = -1e30
G_DIM = 2 * D_MODEL
A_DIM = 2 * A_WIDTH
B_DIM = Q_DIM + 2 * KV_DIM
LANES = 128
SUBLANES = 8
BF16_ROWS = 16
N_CHIPS = 4
N_DEV = 8

ADAM_LR = 0.001
ADAM_B1 = 0.9
ADAM_B2 = 0.999
ADAM_EPS = 1e-08
ADAM_WD = 0.01
ADAM_STEP = 10

MESH = pl.DeviceIdType.MESH
_GELU_C = 0.7978845608028654
_GELU_A = 0.044715


def _cp(sem=None, vmem_mb=None):
    kw = {}
    if sem is not None:
        kw["dimension_semantics"] = sem
    if vmem_mb is not None:
        kw["vmem_limit_bytes"] = vmem_mb << 20
    return pltpu.CompilerParams(**kw)


def _dot(a, b):
    return jnp.dot(a, b, preferred_element_type=F32)


def _dot_nt(a, b):
    return lax.dot_general(a, b, (((1,), (1,)), ((), ())), preferred_element_type=F32)


def _dot_tn(a, b):
    return lax.dot_general(a, b, (((0,), (0,)), ((), ())), preferred_element_type=F32)


def _rms_r(x):
    return lax.rsqrt(jnp.mean(x * x, axis=-1, keepdims=True) + EPS)


def _rms_bwd(dh, n, r, g):
    dn = dh * g
    return r * (dn - n * jnp.mean(dn * n, axis=-1, keepdims=True))


def _gelu(x):
    t = jnp.tanh(_GELU_C * (x + _GELU_A * (x * x * x)))
    return 0.5 * x * (1.0 + t), t


def _gelu_grad(x, t):
    return 0.5 * (1.0 + t) + 0.5 * x * (1.0 - t * t) * (_GELU_C * (1.0 + 3.0 * _GELU_A * x * x))


def _sigmoid(x):
    return 1.0 / (1.0 + jnp.exp(-x))


def _row(tm, w):
    return pl.BlockSpec((tm, w), lambda i: (i, 0))


def _full(shape):
    nd = len(shape)
    return pl.BlockSpec(tuple(shape), lambda *_: (0,) * nd)


def _resident(shape):
    nd = len(shape)
    return pl.BlockSpec(tuple(shape), lambda *_: (0,) * nd, pipeline_mode=pl.Buffered(1))


def _sds(shape, dtype):
    return jax.ShapeDtypeStruct(tuple(shape), dtype)


HBM = pl.BlockSpec(memory_space=pltpu.HBM)
ANY = pl.BlockSpec(memory_space=pl.ANY)
SEM = pl.BlockSpec(memory_space=pltpu.SEMAPHORE)


def _band_buckets():
    i = np.arange(CHUNK)[:, None]
    j = np.arange(2 * CHUNK)[None, :]
    dist = i + CHUNK - j
    valid = (dist >= 0) & (dist < CHUNK)
    d = np.clip(dist, 0, None)
    max_exact = N_BUCKETS // 2
    large = max_exact + (np.log(np.maximum(d, 1) / max_exact) / np.log(MAX_DISTANCE / max_exact)
                         * (N_BUCKETS - max_exact)).astype(np.int32)
    large = np.minimum(large, N_BUCKETS - 1)
    buckets = np.where(d < max_exact, d, large).astype(np.int32)
    return np.where(valid, buckets, -1).astype(np.int32)


def _inproj(x2, g_mix, w_g, w_a, w_b, tm, after=None):
    T = x2.shape[0]
    order = [] if after is None else [after]

    def body(*refs):
        x_ref, g_ref, wg_ref, wa_ref, wb_ref = refs[:5]
        pg_ref, pa_ref, pb_ref, h_ref = refs[5 + len(order):]
        x = x_ref[...]
        h = (x * _rms_r(x) * g_ref[...]).astype(BF16)
        h_ref[...] = h
        pg_ref[...] = _dot(h, wg_ref[...]).astype(BF16)
        pa_ref[...] = _dot(h, wa_ref[...]).astype(BF16)
        pb_ref[...] = _dot(h, wb_ref[...]).astype(BF16)

    return pl.pallas_call(
        body, name="inproj", grid=(T // tm,),
        in_specs=[_row(tm, D_MODEL), _full(g_mix.shape), _resident(w_g.shape), _resident(w_a.shape), _resident(w_b.shape)]
        + [ANY] * len(order),
        out_specs=[_row(tm, G_DIM), _row(tm, A_DIM), _row(tm, B_DIM), _row(tm, D_MODEL)],
        out_shape=[_sds((T, G_DIM), BF16), _sds((T, A_DIM), BF16), _sds((T, B_DIM), BF16), _sds((T, D_MODEL), BF16)],
        compiler_params=_cp(("arbitrary",), 48),
    )(x2, g_mix, w_g, w_a, w_b, *order)


def _sgu_parts(p, g):
    pu = p[:, :A_WIDTH]
    pv = p[:, A_WIDTH:]
    u, tu = _gelu(pu)
    vv, tv = _gelu(pv)
    rv = _rms_r(vv)
    vn = (vv * rv * g).astype(BF16)
    return pu, pv, u, tu, vv, tv, rv, vn


def _tril():
    r = lax.broadcasted_iota(jnp.int32, (CHUNK, CHUNK), 0)
    c = lax.broadcasted_iota(jnp.int32, (CHUNK, CHUNK), 1)
    return r >= c


def _sgu_fwd(proj_a, g_sgu, w_s, b_st, tm):
    T = proj_a.shape[0]

    def body(p_ref, g_ref, ws_ref, bs_ref, y_ref):
        tril = _tril()
        _, _, u, _, _, _, _, vn = _sgu_parts(p_ref[...].astype(F32), g_ref[...])
        for gi in range(A_GROUPS):
            wm = jnp.where(tril, ws_ref[gi], 0.0).astype(BF16)
            bcol = bs_ref[:, gi:gi + 1]
            cs = slice(gi * CHUNK, (gi + 1) * CHUNK)
            for c in range(tm // CHUNK):
                rs = slice(c * CHUNK, (c + 1) * CHUNK)
                s = _dot(wm, vn[rs, cs]) + bcol
                y_ref[rs, cs] = (u[rs, cs] * s).astype(BF16)

    return pl.pallas_call(
        body, name="sgu_fwd", grid=(T // tm,),
        in_specs=[_row(tm, A_DIM), _full(g_sgu.shape), _full(w_s.shape), _full(b_st.shape)],
        out_specs=_row(tm, A_WIDTH), out_shape=_sds((T, A_WIDTH), BF16),
        compiler_params=_cp(("arbitrary",)),
    )(proj_a, g_sgu, w_s, b_st)


HEAD_ROWS = N_HEADS * CHUNK


def _head_rows(h):
    return slice(h * CHUNK, (h + 1) * CHUNK)


def _attn_setup(bias_scr, sink_scr, kvar_scr, qkv_ref, bk_ref, rel_ref, sink_ref):
    bk = bk_ref[...]
    for h in range(N_HEADS):
        acc = jnp.full((CHUNK, 2 * CHUNK), NEG_INF, F32)
        for b in range(N_BUCKETS):
            acc = jnp.where(bk == b, rel_ref[b, h], acc)
        bias_scr[_head_rows(h), :] = acc
        sink_scr[_head_rows(h), :] = jnp.full((CHUNK, LANES), sink_ref[0, h], F32)
    seq = qkv_ref.shape[0]
    rows_per = 2 * CHUNK
    for is_v in range(2):
        c0 = Q_DIM + is_v * KV_DIM
        for r in range(seq // rows_per):
            rs = slice(r * rows_per, (r + 1) * rows_per)
            a = qkv_ref[rs, c0:c0 + KV_DIM].astype(F32)
            lane = lax.broadcasted_iota(jnp.int32, a.shape, 1)
            lo = jnp.where(lane < HEAD_DIM, a, 0.0)
            hi = jnp.where(lane >= HEAD_DIM, a, 0.0)
            kvar_scr[4 * is_v + 0, rs, :] = lo.astype(BF16)
            kvar_scr[4 * is_v + 1, rs, :] = pltpu.roll(lo, HEAD_DIM, 1).astype(BF16)
            kvar_scr[4 * is_v + 2, rs, :] = pltpu.roll(hi, HEAD_DIM, 1).astype(BF16)
            kvar_scr[4 * is_v + 3, rs, :] = hi.astype(BF16)


def _rowsum(a, ones):
    hi = a.astype(BF16)
    lo = (a - hi.astype(F32)).astype(BF16)
    return _dot(hi, ones) + _dot(lo, ones)


def _both(a):
    return jnp.concatenate([a, a], axis=1)


def _attn_probs(qkv_ref, r0, n, kv, bias_scr, sink_scr, ones):
    s = jnp.concatenate([_dot_nt(qkv_ref[pl.ds(r0, CHUNK), (h // 2) * LANES:(h // 2 + 1) * LANES], kv[h // 4][h % 2])
                         for h in range(N_HEADS)], axis=0)
    s = s * (HEAD_DIM ** -0.5) + bias_scr[...]
    col = lax.broadcasted_iota(jnp.int32, s.shape, 1)
    s = jnp.where((col < CHUNK) & (n == 0), NEG_INF, s)
    sink = sink_scr[...]
    m = jnp.maximum(jnp.max(s, axis=-1, keepdims=True), sink)
    p = jnp.exp(s - _both(m))
    es = jnp.exp(sink - m)
    inv = 1.0 / (_rowsum(p, ones) + es)
    return p * _both(inv), es * inv


def _attn_block_inputs(kvar_scr, n):
    r0 = pl.multiple_of(n * CHUNK, CHUNK)
    rp = pl.multiple_of(jnp.maximum(n - 1, 0) * CHUNK, CHUNK)

    def both(idx):
        return jnp.concatenate([kvar_scr[idx, pl.ds(rp, CHUNK), :], kvar_scr[idx, pl.ds(r0, CHUNK), :]], axis=0)

    kv = ((both(0), both(1)), (both(2), both(3)))
    vv = ((both(4), both(5)), (both(6), both(7)))
    return r0, kv, vv


def _attn_fwd(proj_b, sinks, rel_bias, n_seq, seq):
    nb = seq // CHUNK
    bk = jnp.asarray(_band_buckets())

    def body(qkv_ref, bk_ref, rel_ref, sink_ref, o_ref, bias_scr, sink_scr, kvar_scr):
        _attn_setup(bias_scr, sink_scr, kvar_scr, qkv_ref, bk_ref, rel_ref, sink_ref)
        ones = jnp.ones((2 * CHUNK, LANES), BF16)

        def blk(n, carry):
            r0, kv, vv = _attn_block_inputs(kvar_scr, n)
            prob, _ = _attn_probs(qkv_ref, r0, n, kv, bias_scr, sink_scr, ones)
            pb = prob.astype(BF16)
            for pr in range(N_HEADS // 2):
                acc = _dot(pb[_head_rows(2 * pr)], vv[pr // 2][0]) + _dot(pb[_head_rows(2 * pr + 1)], vv[pr // 2][1])
                o_ref[pl.ds(r0, CHUNK), pr * LANES:(pr + 1) * LANES] = acc.astype(BF16)
            return carry

        lax.fori_loop(0, nb, blk, 0)

    smem = pl.BlockSpec(memory_space=pltpu.SMEM)
    return pl.pallas_call(
        body, name="attn_fwd", grid=(n_seq,),
        in_specs=[_row(seq, B_DIM), _full(bk.shape), smem, smem],
        out_specs=_row(seq, Q_DIM), out_shape=_sds((n_seq * seq, Q_DIM), BF16),
        scratch_shapes=[pltpu.VMEM((HEAD_ROWS, 2 * CHUNK), F32), pltpu.VMEM((HEAD_ROWS, LANES), F32),
                        pltpu.VMEM((8, seq, KV_DIM), BF16)],
        compiler_params=_cp(("arbitrary",), 40),
    )(proj_b, bk, rel_bias, sinks)


def _dot_stacked(a, w_ref):
    return jnp.concatenate([_dot(a, w_ref[i]) for i in range(N_CHIPS)], axis=1)


def _dot_nt_stacked(a, w_ref):
    w = w_ref.shape[2]
    acc = _dot_nt(a[:, :w], w_ref[0])
    for i in range(1, N_CHIPS):
        acc = acc + _dot_nt(a[:, i * w:(i + 1) * w], w_ref[i])
    return acc


def _merge_fwd(x2, y_a, y_b, proj_g, w_pa, w_pb, w_out, tm):
    T = x2.shape[0]

    def body(x_ref, ya_ref, yb_ref, g_ref, wpa_ref, wpb_ref, wo_ref, x1_ref, mg_ref):
        g = g_ref[...].astype(F32)
        pa = _dot_stacked(ya_ref[...], wpa_ref)
        pb = _dot_stacked(yb_ref[...], wpb_ref)
        merged = (_sigmoid(g[:, :D_MODEL]) * pa + _sigmoid(g[:, D_MODEL:]) * pb).astype(BF16)
        mg_ref[...] = merged
        x1_ref[...] = x_ref[...] + _dot(merged, wo_ref[...])

    return pl.pallas_call(
        body, name="merge_fwd", grid=(T // tm,),
        in_specs=[_row(tm, D_MODEL), _row(tm, A_WIDTH), _row(tm, Q_DIM), _row(tm, G_DIM),
                  _resident(w_pa.shape), _resident(w_pb.shape), _resident(w_out.shape)],
        out_specs=[_row(tm, D_MODEL), _row(tm, D_MODEL)],
        out_shape=[_sds((T, D_MODEL), F32), _sds((T, D_MODEL), BF16)],
        compiler_params=_cp(("arbitrary",), 40),
    )(x2, y_a, y_b, proj_g, w_pa, w_pb, w_out)


def _upproj(x1, g_ffn, w_up, tm):
    T = x1.shape[0]
    cw = w_up.shape[2]

    def body(x_ref, g_ref, w_ref, u_ref, h_ref):
        x = x_ref[...]
        h = (x * _rms_r(x) * g_ref[...]).astype(BF16)
        h_ref[...] = h
        for i in range(N_CHIPS):
            u_ref[:, i * cw:(i + 1) * cw] = _dot(h, w_ref[i]).astype(BF16)

    return pl.pallas_call(
        body, name="upproj", grid=(T // tm,),
        in_specs=[_row(tm, D_MODEL), _full(g_ffn.shape), _resident(w_up.shape)],
        out_specs=[_row(tm, 2 * D_FF), _row(tm, D_MODEL)],
        out_shape=[_sds((T, 2 * D_FF), BF16), _sds((T, D_MODEL), BF16)],
        compiler_params=_cp(("arbitrary",), 56),
    )(x1, g_ffn, w_up)


def _shift_down(u, halo, k):
    rolled = pltpu.roll(u, k, 0)
    head = rolled[:SUBLANES]
    row = lax.broadcasted_iota(jnp.int32, head.shape, 0)
    if k == 1:
        head = jnp.where(row == 0, halo[1:2], head)
    else:
        head = jnp.where(row == 0, halo[0:1], jnp.where(row == 1, halo[1:2], head))
    return jnp.concatenate([head, rolled[SUBLANES:]], axis=0)


def _shift_up(d, halo, k):
    tm = d.shape[0]
    rolled = pltpu.roll(d, tm - k, 0)
    tail = rolled[tm - SUBLANES:]
    row = lax.broadcasted_iota(jnp.int32, tail.shape, 0)
    if k == 1:
        tail = jnp.where(row == SUBLANES - 1, halo[0:1], tail)
    else:
        tail = jnp.where(row == SUBLANES - 2, halo[0:1], jnp.where(row == SUBLANES - 1, halo[1:2], tail))
    return jnp.concatenate([rolled[:tm - SUBLANES], tail], axis=0)


def _conv_taps(u_ref, halo_ref, cols, at_start):
    u = u_ref[:, cols].astype(F32)
    hl = halo_ref[:, cols].astype(F32)[BF16_ROWS - 2:BF16_ROWS]
    hl = jnp.where(at_start, 0.0, hl)
    return u, _shift_down(u, hl, 1), _shift_down(u, hl, 2)


def _conv_out(taps, wc, bc):
    u, u1, u2 = taps
    return wc[0:1] * u2 + wc[1:2] * u1 + wc[2:3] * u + bc


def _prev_halo_spec(tm, width, col_block=None):
    k = tm // BF16_ROWS
    if col_block is None:
        return pl.BlockSpec((BF16_ROWS, width), lambda i: (jnp.maximum(i * k - 1, 0), 0))
    return pl.BlockSpec((BF16_ROWS, width), lambda j, i: (jnp.maximum(i * k - 1, 0), col_block(j)))


def _ffn_down_loss(upre, x1, target, w_conv, b_conv, w_down, g_final, tm, seq):
    T = x1.shape[0]
    tiles_per_seq = seq // tm
    half = D_FF // 2

    def body(u_ref, hl_ref, x1_ref, t_ref, wc_ref, bc_ref, wd_ref, g_ref, dx2_ref, loss_ref, gg_ref, gate_ref, val_ref):
        i = pl.program_id(0)
        at_start = (i % tiles_per_seq) == 0
        acc = jnp.zeros((tm, D_MODEL), F32)
        for j in range(2):
            gc = slice(j * half, (j + 1) * half)
            vc = slice(D_FF + j * half, D_FF + (j + 1) * half)
            gate = _conv_out(_conv_taps(u_ref, hl_ref, gc, at_start), wc_ref[:, gc], bc_ref[:, gc])
            val = _conv_out(_conv_taps(u_ref, hl_ref, vc, at_start), wc_ref[:, vc], bc_ref[:, vc])
            gate_ref[:, gc] = gate.astype(BF16)
            val_ref[:, gc] = val.astype(BF16)
            act = (gate * _sigmoid(gate) * val).astype(BF16)
            acc = acc + _dot(act, wd_ref[gc, :])
        x2 = x1_ref[...] + acc
        r = _rms_r(x2)
        n = x2 * r
        g = g_ref[...]
        diff = n * g - t_ref[...]
        dy = diff * (1.0 / D_MODEL)
        dx2_ref[...] = _rms_bwd(dy, n, r, g)

        @pl.when(i == 0)
        def _():
            loss_ref[...] = jnp.zeros_like(loss_ref)
            gg_ref[...] = jnp.zeros_like(gg_ref)

        loss_ref[...] += 0.5 * jnp.sum(jnp.mean(diff * diff, axis=-1, keepdims=True), axis=0, keepdims=True)
        gg_ref[...] += jnp.sum(dy * n, axis=0, keepdims=True)

    return pl.pallas_call(
        body, name="ffn_down_loss", grid=(T // tm,),
        in_specs=[_row(tm, 2 * D_FF), _prev_halo_spec(tm, 2 * D_FF), _row(tm, D_MODEL), _row(tm, D_MODEL),
                  _full(w_conv.shape), _full(b_conv.shape), _resident(w_down.shape), _full(g_final.shape)],
        out_specs=[_row(tm, D_MODEL), _full((1, 1)), _full((1, D_MODEL)), _row(tm, D_FF), _row(tm, D_FF)],
        out_shape=[_sds((T, D_MODEL), F32), _sds((1, 1), F32), _sds((1, D_MODEL), F32),
                   _sds((T, D_FF), BF16), _sds((T, D_FF), BF16)],
        compiler_params=_cp(("arbitrary",), 56),
    )(upre, upre, x1, target, w_conv, b_conv, w_down, g_final)


def _ffn_bwd_act(gate, val, dx2, w_down, tm):
    T = dx2.shape[0]
    half = D_FF // 2
    nt = T // tm

    def body(g_ref, v_ref, dx_ref, wd_ref, dg_ref, dv_ref, gwd_out, gbg_ref, gbv_ref, gwd_ref):
        i = pl.program_id(1)
        gate = g_ref[...].astype(F32)
        val = v_ref[...].astype(F32)
        sg = _sigmoid(gate)
        silu = gate * sg
        dx = dx_ref[...].astype(BF16)
        d_act = _dot_nt(dx, wd_ref[...])
        d_val = d_act * silu
        d_gate = d_act * val * (sg * (1.0 + gate * (1.0 - sg)))
        dg_ref[...] = d_gate.astype(BF16)
        dv_ref[...] = d_val.astype(BF16)

        @pl.when(i == 0)
        def _():
            for r in (gwd_ref, gbg_ref, gbv_ref):
                r[...] = jnp.zeros_like(r)

        gwd_ref[...] += _dot_tn((silu * val).astype(BF16), dx)
        gbg_ref[...] += jnp.sum(d_gate, axis=0, keepdims=True)
        gbv_ref[...] += jnp.sum(d_val, axis=0, keepdims=True)

        @pl.when(i == nt - 1)
        def _():
            gwd_out[...] = gwd_ref[...].astype(BF16)

    tile = pl.BlockSpec((tm, half), lambda j, i: (i, j))
    vec = pl.BlockSpec((1, half), lambda j, i: (0, j))
    wrows = pl.BlockSpec((half, D_MODEL), lambda j, i: (j, 0))
    return pl.pallas_call(
        body, name="ffn_bwd_act", grid=(2, nt),
        in_specs=[tile, tile, pl.BlockSpec((tm, D_MODEL), lambda j, i: (i, 0)), wrows],
        out_specs=[tile, tile, wrows, vec, vec],
        out_shape=[_sds((T, D_FF), BF16), _sds((T, D_FF), BF16), _sds((D_FF, D_MODEL), BF16),
                   _sds((1, D_FF), F32), _sds((1, D_FF), F32)],
        scratch_shapes=[pltpu.VMEM((half, D_MODEL), F32)],
        compiler_params=_cp(("arbitrary", "arbitrary"), 56),
    )(gate, val, dx2, w_down)


def _ffn_bwd_up(d_gate, d_val, upre, dx2, x1, g_ffn, w_conv, w_up, tm, seq):
    T = dx2.shape[0]
    tiles_per_seq = seq // tm
    k16 = tm // BF16_ROWS
    n16 = T // BF16_ROWS
    cw = D_FF // 2

    def body(dg_ref, dv_ref, hg_ref, hv_ref, u_ref, dx2_ref, x1_ref, g_ref, wc_ref, wu_ref, du_ref, dx1_ref, gg_ref, gwc_ref):
        i = pl.program_id(0)
        at_end = (i % tiles_per_seq) == tiles_per_seq - 1

        @pl.when(i == 0)
        def _():
            gg_ref[...] = jnp.zeros_like(gg_ref)
            gwc_ref[...] = jnp.zeros_like(gwc_ref)

        dh = jnp.zeros((tm, D_MODEL), F32)
        for j in range(4):
            src, hsrc = (dg_ref, hg_ref) if j < 2 else (dv_ref, hv_ref)
            ls = slice((j % 2) * cw, (j % 2 + 1) * cw)
            cs = slice(j * cw, (j + 1) * cw)
            d = src[:, ls].astype(F32)
            hl = hsrc[:, ls].astype(F32)[0:2]
            hl = jnp.where(at_end, 0.0, hl)
            wc = wc_ref[:, cs]
            d1 = _shift_up(d, hl, 1)
            d2 = _shift_up(d, hl, 2)
            du = (wc[2:3] * d + wc[1:2] * d1 + wc[0:1] * d2).astype(BF16)
            du_ref[:, cs] = du
            dh = dh + _dot_nt(du, wu_ref[j])
            u = u_ref[:, cs].astype(F32)
            gwc_ref[0:1, cs] += jnp.sum(d2 * u, axis=0, keepdims=True)
            gwc_ref[1:2, cs] += jnp.sum(d1 * u, axis=0, keepdims=True)
            gwc_ref[2:3, cs] += jnp.sum(d * u, axis=0, keepdims=True)
        x = x1_ref[...]
        r = _rms_r(x)
        n = x * r
        dx1_ref[...] = dx2_ref[...] + _rms_bwd(dh, n, r, g_ref[...])
        gg_ref[...] += jnp.sum(dh * n, axis=0, keepdims=True)

    nxt = pl.BlockSpec((BF16_ROWS, D_FF), lambda i: (jnp.minimum((i + 1) * k16, n16 - 1), 0))
    return pl.pallas_call(
        body, name="ffn_bwd_up", grid=(T // tm,),
        in_specs=[_row(tm, D_FF), _row(tm, D_FF), nxt, nxt, _row(tm, 2 * D_FF), _row(tm, D_MODEL), _row(tm, D_MODEL),
                  _full(g_ffn.shape), _full(w_conv.shape), _resident(w_up.shape)],
        out_specs=[_row(tm, 2 * D_FF), _row(tm, D_MODEL), _full((1, D_MODEL)), _full((3, 2 * D_FF))],
        out_shape=[_sds((T, 2 * D_FF), BF16), _sds((T, D_MODEL), F32), _sds((1, D_MODEL), F32), _sds((3, 2 * D_FF), F32)],
        compiler_params=_cp(("arbitrary",), 56),
    )(d_gate, d_val, d_gate, d_val, upre, dx2, x1, g_ffn, w_conv, w_up)


def _matmul_tn(a, b, tn, tk, name):
    T, M = a.shape
    N = b.shape[1]
    nk = T // tk

    def body(a_ref, b_ref, o_ref, acc_ref):
        k = pl.program_id(1)

        @pl.when(k == 0)
        def _():
            acc_ref[...] = jnp.zeros_like(acc_ref)

        acc_ref[...] += _dot_tn(a_ref[...], b_ref[...])

        @pl.when(k == nk - 1)
        def _():
            o_ref[...] = acc_ref[...].astype(BF16)

    return pl.pallas_call(
        body, name=name, grid=(N // tn, nk),
        in_specs=[pl.BlockSpec((tk, M), lambda j, k: (k, 0)), pl.BlockSpec((tk, tn), lambda j, k: (k, j))],
        out_specs=pl.BlockSpec((M, tn), lambda j, k: (0, j)), out_shape=_sds((M, N), BF16),
        scratch_shapes=[pltpu.VMEM((M, tn), F32)],
        compiler_params=_cp(("arbitrary", "arbitrary"), 48),
    )(a, b)


def _merge_bwd(dx1, merged, y_a, y_b, proj_g, w_pa, w_pb, w_out, tm, after=None):
    T = dx1.shape[0]

    nt = T // tm
    pshape = (A_WIDTH, D_MODEL)
    order = [] if after is None else [after]

    def body(*refs):
        dx_ref, mg_ref, ya_ref, yb_ref, g_ref, wpa_ref, wpb_ref, wo_ref = refs[:8]
        dg_ref, dya_ref, dyb_ref, gwo_out, gwpa_out, gwpb_out, gwo_ref, gwpa_ref, gwpb_ref = refs[8 + len(order):]
        i = pl.program_id(0)
        dx = dx_ref[...].astype(BF16)
        dm = _dot_nt(dx, wo_ref[...])
        g = g_ref[...].astype(F32)
        ya = ya_ref[...]
        yb = yb_ref[...]
        pa = _dot_stacked(ya, wpa_ref)
        pb = _dot_stacked(yb, wpb_ref)
        sa = _sigmoid(g[:, :D_MODEL])
        sb = _sigmoid(g[:, D_MODEL:])
        dpa = (dm * sa).astype(BF16)
        dpb = (dm * sb).astype(BF16)
        dg_ref[:, :D_MODEL] = (dm * pa * (sa * (1.0 - sa))).astype(BF16)
        dg_ref[:, D_MODEL:] = (dm * pb * (sb * (1.0 - sb))).astype(BF16)
        dya_ref[...] = _dot_nt_stacked(dpa, wpa_ref).astype(BF16)
        dyb_ref[...] = _dot_nt_stacked(dpb, wpb_ref).astype(BF16)

        @pl.when(i == 0)
        def _():
            for r in (gwo_ref, gwpa_ref, gwpb_ref):
                r[...] = jnp.zeros_like(r)

        gwo_ref[...] += _dot_tn(mg_ref[...], dx)
        gwpa_ref[...] += _dot_tn(ya, dpa)
        gwpb_ref[...] += _dot_tn(yb, dpb)

        @pl.when(i == nt - 1)
        def _():
            gwo_out[...] = gwo_ref[...].astype(BF16)
            gwpa_out[...] = gwpa_ref[...].astype(BF16)
            gwpb_out[...] = gwpb_ref[...].astype(BF16)

    return pl.pallas_call(
        body, name="merge_bwd", grid=(nt,),
        in_specs=[_row(tm, D_MODEL), _row(tm, D_MODEL), _row(tm, A_WIDTH), _row(tm, Q_DIM), _row(tm, G_DIM),
                  _resident(w_pa.shape), _resident(w_pb.shape), _resident(w_out.shape)] + [ANY] * len(order),
        out_specs=[_row(tm, G_DIM), _row(tm, A_WIDTH), _row(tm, Q_DIM),
                   _full(w_out.shape), _full(pshape), _full(pshape)],
        out_shape=[_sds((T, G_DIM), BF16), _sds((T, A_WIDTH), BF16), _sds((T, Q_DIM), BF16),
                   _sds(w_out.shape, BF16), _sds(pshape, BF16), _sds(pshape, BF16)],
        scratch_shapes=[pltpu.VMEM(w_out.shape, F32), pltpu.VMEM(pshape, F32), pltpu.VMEM(pshape, F32)],
        compiler_params=_cp(("arbitrary",), 56),
    )(dx1, merged, y_a, y_b, proj_g, w_pa, w_pb, w_out, *order)


def _sgu_bwd(proj_a, d_ya, g_sgu, w_s, b_st, tm, after=None):
    T = proj_a.shape[0]
    order = [] if after is None else [after]

    def body(*refs):
        p_ref, dy_ref, g_ref, ws_ref, bs_ref = refs[:5]
        dp_ref, gws_ref, gbs_ref, gg_ref = refs[5 + len(order):]
        tril = _tril()
        g = g_ref[...]
        pu, pv, u, tu, vv, tv, rv, vn = _sgu_parts(p_ref[...].astype(F32), g)
        dy = dy_ref[...].astype(F32)

        @pl.when(pl.program_id(0) == 0)
        def _():
            for r in (gws_ref, gbs_ref, gg_ref):
                r[...] = jnp.zeros_like(r)

        du_cols = []
        dvn_cols = []
        for gi in range(A_GROUPS):
            wm = jnp.where(tril, ws_ref[gi], 0.0).astype(BF16)
            wmt = wm.astype(F32).T.astype(BF16)
            bcol = bs_ref[:, gi:gi + 1]
            cs = slice(gi * CHUNK, (gi + 1) * CHUNK)
            du_rows = []
            dvn_rows = []
            gw = jnp.zeros((CHUNK, CHUNK), F32)
            gb = jnp.zeros((CHUNK, 1), F32)
            for c in range(tm // CHUNK):
                rs = slice(c * CHUNK, (c + 1) * CHUNK)
                vn_c = vn[rs, cs]
                s = _dot(wm, vn_c) + bcol
                dy_c = dy[rs, cs]
                ds = dy_c * u[rs, cs]
                du_rows.append(dy_c * s)
                dsb = ds.astype(BF16)
                gw = gw + _dot_nt(dsb, vn_c)
                gb = gb + jnp.sum(ds, axis=-1, keepdims=True)
                dvn_rows.append(_dot(wmt, dsb))
            gws_ref[gi] += jnp.where(tril, gw, 0.0)
            gbs_ref[:, gi:gi + 1] += gb
            du_cols.append(jnp.concatenate(du_rows, axis=0))
            dvn_cols.append(jnp.concatenate(dvn_rows, axis=0))
        du = jnp.concatenate(du_cols, axis=1)
        dvn = jnp.concatenate(dvn_cols, axis=1)
        vhat = vv * rv
        gg_ref[...] += jnp.sum(dvn * vhat, axis=0, keepdims=True)
        dvv = _rms_bwd(dvn, vhat, rv, g)
        dp_ref[:, :A_WIDTH] = (du * _gelu_grad(pu, tu)).astype(BF16)
        dp_ref[:, A_WIDTH:] = (dvv * _gelu_grad(pv, tv)).astype(BF16)

    return pl.pallas_call(
        body, name="sgu_bwd", grid=(T // tm,),
        in_specs=[_row(tm, A_DIM), _row(tm, A_WIDTH), _full(g_sgu.shape), _full(w_s.shape), _full(b_st.shape)] + [ANY] * len(order),
        out_specs=[_row(tm, A_DIM), _full(w_s.shape), _full(b_st.shape), _full(g_sgu.shape)],
        out_shape=[_sds((T, A_DIM), BF16), _sds(w_s.shape, F32), _sds(b_st.shape, F32), _sds(g_sgu.shape, F32)],
        compiler_params=_cp(("arbitrary",)),
    )(proj_a, d_ya, g_sgu, w_s, b_st, *order)


def _attn_bwd(proj_b, d_yb, sinks, rel_bias, n_seq, seq):
    nb = seq // CHUNK
    bk = jnp.asarray(_band_buckets())

    def body(qkv_ref, do_ref, bk_ref, rel_ref, sink_ref, d_ref, gs_ref, gr_ref,
             bias_scr, sink_scr, kvar_scr, dbias_scr, dk_scr, dv_scr, ds_scr):
        b = pl.program_id(0)
        _attn_setup(bias_scr, sink_scr, kvar_scr, qkv_ref, bk_ref, rel_ref, sink_ref)
        ones = jnp.ones((2 * CHUNK, LANES), BF16)

        @pl.when(b == 0)
        def _():
            dbias_scr[...] = jnp.zeros_like(dbias_scr)
            ds_scr[...] = jnp.zeros_like(ds_scr)

        dk_scr[...] = jnp.zeros_like(dk_scr)
        dv_scr[...] = jnp.zeros_like(dv_scr)

        def transposed(a):
            return a.astype(F32).T.astype(BF16)

        def blk(n, carry):
            r0, kv, vv = _attn_block_inputs(kvar_scr, n)
            prob, psink = _attn_probs(qkv_ref, r0, n, kv, bias_scr, sink_scr, ones)
            dp = jnp.concatenate([_dot_nt(do_ref[pl.ds(r0, CHUNK), (h // 2) * LANES:(h // 2 + 1) * LANES], vv[h // 4][h % 2])
                                  for h in range(N_HEADS)], axis=0)
            delta = _rowsum(prob * dp, ones)
            dsc = prob * (dp - _both(delta))
            ds_scr[...] += psink * delta
            dbias_scr[...] += dsc
            dsb = (dsc * (HEAD_DIM ** -0.5)).astype(BF16)
            pb = prob.astype(BF16)
            dkt = [jnp.zeros((HEAD_DIM, 2 * CHUNK), F32) for _ in range(2)]
            dvt = [jnp.zeros((HEAD_DIM, 2 * CHUNK), F32) for _ in range(2)]
            for pr in range(N_HEADS // 2):
                ps = slice(pr * LANES, (pr + 1) * LANES)
                qpt = transposed(qkv_ref[pl.ds(r0, CHUNK), ps])
                dopt = transposed(do_ref[pl.ds(r0, CHUNK), ps])
                kvh = pr // 2
                dq = jnp.zeros((CHUNK, LANES), F32)
                for hh in range(2):
                    hr = _head_rows(2 * pr + hh)
                    rows = slice(hh * HEAD_DIM, (hh + 1) * HEAD_DIM)
                    dq = dq + _dot(dsb[hr], kv[kvh][hh])
                    dkt[kvh] = dkt[kvh] + _dot(qpt, dsb[hr])[rows]
                    dvt[kvh] = dvt[kvh] + _dot(dopt, pb[hr])[rows]
                d_ref[pl.ds(r0, CHUNK), ps] = dq.astype(BF16)
            dk_scr[:, pl.ds(r0, 2 * CHUNK)] += jnp.concatenate(dkt, axis=0)
            dv_scr[:, pl.ds(r0, 2 * CHUNK)] += jnp.concatenate(dvt, axis=0)
            return carry

        lax.fori_loop(0, nb, blk, 0)
        for n in range(nb):
            rows = slice(n * CHUNK, (n + 1) * CHUNK)
            cols = slice((n + 1) * CHUNK, (n + 2) * CHUNK)
            d_ref[rows, Q_DIM:Q_DIM + KV_DIM] = dk_scr[:, cols].T.astype(BF16)
            d_ref[rows, Q_DIM + KV_DIM:] = dv_scr[:, cols].T.astype(BF16)

        @pl.when(b == n_seq - 1)
        def _():
            bkv = bk_ref[...]
            for h in range(N_HEADS):
                gs_ref[0:1, h:h + 1] = -jnp.sum(ds_scr[_head_rows(h), 0:1], axis=0, keepdims=True)
                db = dbias_scr[_head_rows(h), :]
                for bb in range(N_BUCKETS):
                    part = jnp.sum(jnp.where(bkv == bb, db, 0.0), axis=-1, keepdims=True)
                    gr_ref[bb:bb + 1, h:h + 1] = jnp.sum(part, axis=0, keepdims=True)

    smem = pl.BlockSpec(memory_space=pltpu.SMEM)
    return pl.pallas_call(
        body, name="attn_bwd", grid=(n_seq,),
        in_specs=[_row(seq, B_DIM), _row(seq, Q_DIM), _full(bk.shape), smem, smem],
        out_specs=[_row(seq, B_DIM), _full((1, N_HEADS)), _full((N_BUCKETS, N_HEADS))],
        out_shape=[_sds((n_seq * seq, B_DIM), BF16), _sds((1, N_HEADS), F32), _sds((N_BUCKETS, N_HEADS), F32)],
        scratch_shapes=[pltpu.VMEM((HEAD_ROWS, 2 * CHUNK), F32), pltpu.VMEM((HEAD_ROWS, LANES), F32),
                        pltpu.VMEM((8, seq, KV_DIM), BF16), pltpu.VMEM((HEAD_ROWS, 2 * CHUNK), F32),
                        pltpu.VMEM((KV_DIM, seq + CHUNK), F32), pltpu.VMEM((KV_DIM, seq + CHUNK), F32),
                        pltpu.VMEM((HEAD_ROWS, LANES), F32)],
        compiler_params=_cp(("arbitrary",), 40),
    )(proj_b, d_yb, bk, rel_bias, sinks)


def _inproj_bwd(d_g, d_a, d_b, x2, dx1, g_mix, w_g, w_a, w_b, tm, after=None):
    T = x2.shape[0]
    order = [] if after is None else [after]

    def body(*refs):
        dg_ref, da_ref, db_ref, x_ref, dx1_ref, g_ref, wg_ref, wa_ref, wb_ref = refs[:9]
        gx_ref, gg_ref = refs[9 + len(order):]
        dh = _dot_nt(dg_ref[...], wg_ref[...]) + _dot_nt(da_ref[...], wa_ref[...]) + _dot_nt(db_ref[...], wb_ref[...])
        x = x_ref[...]
        r = _rms_r(x)
        n = x * r
        gx_ref[...] = dx1_ref[...] + _rms_bwd(dh, n, r, g_ref[...])

        @pl.when(pl.program_id(0) == 0)
        def _():
            gg_ref[...] = jnp.zeros_like(gg_ref)

        gg_ref[...] += jnp.sum(dh * n, axis=0, keepdims=True)

    return pl.pallas_call(
        body, name="inproj_bwd", grid=(T // tm,),
        in_specs=[_row(tm, G_DIM), _row(tm, A_DIM), _row(tm, B_DIM), _row(tm, D_MODEL), _row(tm, D_MODEL),
                  _full(g_mix.shape), _resident(w_g.shape), _resident(w_a.shape), _resident(w_b.shape)] + [ANY] * len(order),
        out_specs=[_row(tm, D_MODEL), _full((1, D_MODEL))],
        out_shape=[_sds((T, D_MODEL), F32), _sds((1, D_MODEL), F32)],
        compiler_params=_cp(("arbitrary",), 48),
    )(d_g, d_a, d_b, x2, dx1, g_mix, w_g, w_a, w_b, *order)


def _local_step(x, target, g_mix, g_sgu, w_s, b_s, sinks, rel_bias, g_ffn, b_conv, g_final,
                w_g, w_a, w_b, w_conv, late_weights, on_grads, after=None):
    n_seq, seq, _ = x.shape
    T = n_seq * seq
    tm = min(256, seq)
    x2 = x.reshape(T, D_MODEL)
    tgt = target.reshape(T, D_MODEL)
    b_st = b_s.T
    g_fin = g_final.reshape(1, D_MODEL)

    proj_g, proj_a, proj_b, h = _inproj(x2, g_mix, w_g, w_a, w_b, tm, after)
    y_a = _sgu_fwd(proj_a, g_sgu, w_s, b_st, tm)
    y_b = _attn_fwd(proj_b, sinks, rel_bias, n_seq, seq)
    w_pa, w_pb, w_out, w_up, w_down = late_weights(y_b)
    x1, merged = _merge_fwd(x2, y_a, y_b, proj_g, w_pa, w_pb, w_out, tm)
    upre, h2 = _upproj(x1, g_ffn, w_up, tm)
    dx2, loss, gg_final, gate, val = _ffn_down_loss(upre, x1, tgt, w_conv, b_conv, w_down, g_fin, tm, seq)

    d_gate, d_val, gw_down, gb_g, gb_v = _ffn_bwd_act(gate, val, dx2, w_down, tm)
    gb_conv = jnp.concatenate([gb_g, gb_v], axis=1)
    d_upre, dx1, gg_ffn, gw_conv = _ffn_bwd_up(d_gate, d_val, upre, dx2, x1, g_ffn, w_conv, w_up, tm, seq)
    gw_up = _matmul_tn(h2, d_upre, 2 * D_FF // 4, min(512, T), "grad_w_up")
    sent = on_grads("ffn", dict(w_up=gw_up, w_down=gw_down))
    d_g, d_ya, d_yb, gw_out, gw_pa, gw_pb = _merge_bwd(dx1, merged, y_a, y_b, proj_g, w_pa, w_pb, w_out, tm, sent)
    sent = on_grads("proj", dict(w_pa=gw_pa, w_pb=gw_pb, w_out=gw_out))
    d_a, gw_s, gb_st, gg_sgu = _sgu_bwd(proj_a, d_ya, g_sgu, w_s, b_st, tm, sent)
    d_b, g_sinks, g_rel = _attn_bwd(proj_b, d_yb, sinks, rel_bias, n_seq, seq)
    gw_g = _matmul_tn(h, d_g, D_MODEL, min(512, T), "grad_w_in_gate")
    gw_a = _matmul_tn(h, d_a, A_DIM, min(512, T), "grad_w_in_a")
    gw_b = _matmul_tn(h, d_b, B_DIM, min(512, T), "grad_w_in_b")
    gw_in = jnp.concatenate([gw_a, gw_b, gw_g], axis=1).reshape(D_MODEL, N_CHIPS, -1).transpose(1, 0, 2)
    sent = on_grads("in", dict(w_in=gw_in))
    grad_x, gg_mix = _inproj_bwd(d_g, d_a, d_b, x2, dx1, g_mix, w_g, w_a, w_b, tm, sent)

    small = dict(g_mix=gg_mix, g_sgu=gg_sgu, w_s=gw_s, b_s=gb_st.T, sinks=g_sinks, rel_bias=g_rel,
                 g_ffn=gg_ffn, b_conv=gb_conv, g_final=gg_final, w_conv=gw_conv)
    big = dict(w_in=gw_in, w_pa=gw_pa, w_pb=gw_pb, w_out=gw_out, w_up=gw_up, w_down=gw_down)
    return loss, grad_x.reshape(x.shape), small, big


_MIXER = ("w_in", "w_pa", "w_pb", "w_out")
_FFN = ("w_up", "w_down")
_BIG = _MIXER + _FFN

_SMALL = (("loss", (1, 1)), ("g_final", (1, D_MODEL)), ("g_mix", (1, D_MODEL)), ("g_ffn", (1, D_MODEL)),
          ("g_sgu", (1, A_WIDTH)), ("b_s", (A_GROUPS, CHUNK)), ("sinks", (1, N_HEADS)), ("rel_bias", (N_BUCKETS, N_HEADS)),
          ("b_conv", (1, 2 * D_FF)), ("w_conv", (3, 2 * D_FF)), ("w_s", (A_GROUPS, CHUNK, CHUNK)))
SMALL_ROWS = 96


def _pack_small(vals):
    flat = jnp.concatenate([vals[n].astype(F32).reshape(-1) for n, _ in _SMALL])
    flat = jnp.pad(flat, (0, SMALL_ROWS * D_MODEL - flat.shape[0]))
    return flat.reshape(SMALL_ROWS, D_MODEL)


def _unpack_small(buf):
    flat = buf.reshape(-1)
    out = {}
    off = 0
    for n, shp in _SMALL:
        k = int(np.prod(shp))
        out[n] = flat[off:off + k].reshape(shp)
        off += k
    return out


def _mesh_pos():
    return lax.axis_index("x"), lax.axis_index("y"), lax.axis_index("c")


def _other_chips(x, y):
    return [(1 - x, y), (x, 1 - y), (1 - x, 1 - y)]


def _remote(src, dst, send_sem, recv_sem, to):
    return pltpu.make_async_remote_copy(src_ref=src, dst_ref=dst, send_sem=send_sem, recv_sem=recv_sem,
                                        device_id=to, device_id_type=MESH)


def _own_slot(own, n, at):
    return lax.dynamic_update_slice(lax.empty((n,) + own.shape, own.dtype), own[None], (at,) + (0,) * own.ndim)


def _allgather_weights(stacks, wc_stack):
    names = list(stacks)
    n = len(names)

    def body(*refs):
        ins, outs = refs[:n + 1], refs[n + 1:2 * n + 2]
        send_sems, recv_sems = refs[2 * n + 2:]
        x, y, c = _mesh_pos()
        me = 2 * x + y
        sibling = (x, y, 1 - c)
        chips = _other_chips(x, y)

        def half(ref, chip, hc):
            hr = ref.shape[1] // 2
            return ref.at[chip, pl.ds(hc * hr, hr), :]

        first = []
        for k in range(n):
            first += [_remote(half(ins[k], me, c), half(outs[k], me, c), send_sems.at[6 * k + j], recv_sems.at[6 * k + j], (cx, cy, c))
                      for j, (cx, cy) in enumerate(chips)]
        first += [_remote(ins[n].at[me], outs[n].at[me], send_sems.at[6 * n + j], recv_sems.at[6 * n + j], (cx, cy, c))
                  for j, (cx, cy) in enumerate(chips)]
        for cp in first:
            cp.start()
        passed = []
        for k in range(n):
            for j, (cx, cy) in enumerate(chips):
                landed = half(outs[k], 2 * cx + cy, c)
                _remote(landed, landed, send_sems.at[6 * k + j], recv_sems.at[6 * k + j], (x, y, c)).wait_recv()
                passed.append(_remote(landed, landed, send_sems.at[6 * k + 3 + j], recv_sems.at[6 * k + 3 + j], sibling))
                passed[-1].start()
        for k in range(n):
            for j, (cx, cy) in enumerate(chips):
                theirs = half(outs[k], 2 * cx + cy, 1 - c)
                _remote(theirs, theirs, send_sems.at[6 * k + 3 + j], recv_sems.at[6 * k + 3 + j], (x, y, c)).wait_recv()
        for j, (cx, cy) in enumerate(chips):
            slot = outs[n].at[2 * cx + cy]
            _remote(slot, slot, send_sems.at[6 * n + j], recv_sems.at[6 * n + j], (x, y, c)).wait_recv()
        for cp in first + passed:
            cp.wait_send()

    arrays = [stacks[k] for k in names] + [wc_stack]
    outs = pl.pallas_call(
        body, name="allgather_weights",
        in_specs=[HBM] * (n + 1), out_specs=[HBM] * (n + 1), input_output_aliases={k: k for k in range(n + 1)},
        out_shape=[_sds(a.shape, a.dtype) for a in arrays],
        scratch_shapes=[pltpu.SemaphoreType.DMA((6 * n + 3,)), pltpu.SemaphoreType.DMA((6 * n + 3,))],
    )(*arrays)
    return dict(zip(names, outs[:n])), outs[n]


_KIND = {"w_in": "stack", "w_pa": "col", "w_pb": "col", "w_up": "col", "w_out": "row", "w_down": "row"}


def _half_view(ref, kind, h):
    if kind == "stack":
        k = ref.shape[1] // 2
        return ref.at[:, pl.ds(h * k, k), :]
    if kind == "col":
        k = ref.shape[0] // 2
        return ref.at[pl.ds(h * k, k), :]
    k = ref.shape[1] // 2
    return ref.at[:, pl.ds(h * k, k)]


def _shard_view(ref, kind, i):
    if kind == "stack":
        return ref.at[i]
    if kind == "col":
        k = ref.shape[1] // N_CHIPS
        return ref.at[:, pl.ds(i * k, k)]
    k = ref.shape[0] // N_CHIPS
    return ref.at[pl.ds(i * k, k), :]


def _region_view(ref, kind, h):
    if kind == "row":
        k = ref.shape[1] // 2
        return ref.at[:, pl.ds(h * k, k)]
    k = ref.shape[0] // 2
    return ref.at[pl.ds(h * k, k), :]


def _half_shape(shape, kind):
    if kind == "stack":
        return (shape[0], shape[1] // 2, shape[2])
    return (shape[0] // 2, shape[1]) if kind == "col" else (shape[0], shape[1] // 2)


def _part_shape(half_shape, kind):
    if kind == "stack":
        return tuple(half_shape[1:])
    k, w = half_shape
    return (k, w // N_CHIPS) if kind == "col" else (k // N_CHIPS, w)


def _pair_exchange(parts, tag):
    names = list(parts)
    n = len(names)

    def body(*refs):
        p, q = refs[:n], refs[n:2 * n]
        send_sems, recv_sems = refs[2 * n:]
        x, y, c = _mesh_pos()

        def copy(k, h):
            return _remote(_half_view(p[k], _KIND[names[k]], h), q[k], send_sems.at[k], recv_sems.at[k], (x, y, 1 - c))

        for hc in range(2):
            @pl.when(c == hc)
            def _():
                for k in range(n):
                    copy(k, 1 - hc).start()

        for k in range(n):
            copy(k, 0).wait()

    outs = pl.pallas_call(
        body, name="grad_pair_exchange_" + tag, in_specs=[HBM] * n, out_specs=[HBM] * n,
        out_shape=[_sds(_half_shape(parts[k].shape, _KIND[k]), parts[k].dtype) for k in names],
        scratch_shapes=[pltpu.SemaphoreType.DMA((n,)), pltpu.SemaphoreType.DMA((n,))],
    )(*[parts[k] for k in names])
    return dict(zip(names, outs))


def _half_blocks(shape, kind):
    if kind == "stack":
        _, k, w = shape
        tr = 256
        nb = k // 2 // tr
        return (N_CHIPS, nb), (1, tr, w), (lambda i, r, s: (i, r, 0)), (lambda i, r, s: (i, s[1] * nb + r, 0))
    k, w = shape
    if kind == "col":
        tr = 256 if w <= 2 * D_MODEL else 128
        nb = k // 2 // tr
        return (nb,), (tr, w), (lambda r, s: (r, 0)), (lambda r, s: (s[1] * nb + r, 0))
    tr = k // N_CHIPS
    return (N_CHIPS,), (tr, w // 2), (lambda r, s: (r, 0)), (lambda r, s: (r, s[1]))


def _pair_add(part, from_sibling, name, pos):
    kind = _KIND[name]
    grid, block, half_map, full_map = _half_blocks(part.shape, kind)

    def body(s_ref, p_ref, q_ref, o_ref):
        o_ref[...] = (p_ref[...].astype(F32) + q_ref[...].astype(F32)).astype(BF16)

    return pl.pallas_call(
        body, name="grad_pair_add_" + name,
        grid_spec=pltpu.PrefetchScalarGridSpec(
            num_scalar_prefetch=1, grid=grid,
            in_specs=[pl.BlockSpec(block, full_map), pl.BlockSpec(block, half_map)],
            out_specs=pl.BlockSpec(block, half_map)),
        out_shape=_sds(from_sibling.shape, BF16),
        compiler_params=_cp(("arbitrary",) * len(grid)),
    )(pos, part, from_sibling)


_DATAFLOW = pltpu.SideEffectType.DATAFLOW_SIDE_EFFECTING
_TOKEN = (SUBLANES, LANES)


def _chip_exchange_start(sums, tag):
    names = list(sums)
    n = len(names)
    lands = [lax.empty((3,) + _part_shape(sums[k].shape, _KIND[k]), sums[k].dtype) for k in names]

    def body(*refs):
        s, r = refs[:n], refs[n:2 * n]
        send_sems, recv_sems = refs[2 * n:2 * n + 2]
        token = refs[-1]
        x, y, c = _mesh_pos()
        me = 2 * x + y
        for i in range(N_CHIPS):
            xi, yi = i // 2, i % 2
            j = jnp.where(xi != x, jnp.where(yi != y, 2, 0), 1)

            @pl.when(i != me)
            def _():
                for k in range(n):
                    _remote(_shard_view(s[k], _KIND[names[k]], i), r[k].at[j], send_sems.at[3 * k + j], recv_sems.at[3 * k + j],
                            (xi, yi, c)).start()

        token[...] = jnp.zeros_like(token)

    arrays = [sums[k] for k in names] + lands
    outs = pl.pallas_call(
        body, name="grad_chip_exchange_start_" + tag,
        in_specs=[HBM] * (2 * n), out_specs=[SEM, SEM] + [HBM] * (2 * n) + [pl.BlockSpec(memory_space=pltpu.VMEM)],
        out_shape=[pltpu.SemaphoreType.DMA((3 * n,)), pltpu.SemaphoreType.DMA((3 * n,))]
        + [pltpu.HBM(a.shape, a.dtype) for a in arrays] + [_sds(_TOKEN, F32)],
        input_output_aliases={k: 2 + k for k in range(2 * n)},
        compiler_params=pltpu.CompilerParams(has_side_effects=_DATAFLOW),
    )(*[pltpu.with_memory_space_constraint(a, pltpu.HBM) for a in arrays])
    return names, outs[0], outs[1], list(outs[2:2 + n]), list(outs[2 + n:2 + 2 * n]), outs[-1]


def _chip_exchange_wait(started, after, tag):
    names, send_sems, recv_sems, sums, lands, _ = started
    n = len(names)

    def body(*refs):
        s, r = refs[:n], refs[n:2 * n]
        send, recv = refs[2 * n], refs[2 * n + 1]
        x, y, c = _mesh_pos()
        for k in range(n):
            for j in range(3):
                cp = _remote(_shard_view(s[k], _KIND[names[k]], 0), r[k].at[j], send.at[3 * k + j], recv.at[3 * k + j], (x, y, c))
                cp.wait_send()
                cp.wait_recv()

    outs = pl.pallas_call(
        body, name="grad_chip_exchange_wait_" + tag,
        in_specs=[HBM] * (2 * n) + [SEM, SEM, ANY], out_specs=[HBM] * (2 * n),
        out_shape=[pltpu.HBM(a.shape, a.dtype) for a in sums + lands],
        input_output_aliases={k: k for k in range(2 * n)},
        compiler_params=pltpu.CompilerParams(has_side_effects=_DATAFLOW),
    )(*sums, *lands, send_sems, recv_sems, after)
    return dict(zip(names, outs[n:]))


def _allgather_start(stacks, after):
    names = list(stacks)
    n = len(names)

    def body(*refs):
        st = refs[:n]
        send_sems, recv_sems = refs[n + 1:n + 3]
        token = refs[-1]
        x, y, c = _mesh_pos()
        me = 2 * x + y
        for k in range(n):
            hr = st[k].shape[1] // 2
            mine = st[k].at[me, pl.ds(c * hr, hr), :]
            for j, (cx, cy) in enumerate(_other_chips(x, y)):
                _remote(mine, mine, send_sems.at[3 * k + j], recv_sems.at[3 * k + j], (cx, cy, c)).start()
        token[...] = jnp.zeros_like(token)

    arrays = [stacks[k] for k in names]
    outs = pl.pallas_call(
        body, name="allgather_start",
        in_specs=[HBM] * n + [ANY], out_specs=[SEM, SEM] + [HBM] * n + [pl.BlockSpec(memory_space=pltpu.VMEM)],
        out_shape=[pltpu.SemaphoreType.DMA((3 * n,)), pltpu.SemaphoreType.DMA((3 * n,))]
        + [pltpu.HBM(a.shape, a.dtype) for a in arrays] + [_sds(_TOKEN, F32)],
        input_output_aliases={k: 2 + k for k in range(n)},
        compiler_params=pltpu.CompilerParams(has_side_effects=_DATAFLOW),
    )(*[pltpu.with_memory_space_constraint(a, pltpu.HBM) for a in arrays], after)
    return names, outs[0], outs[1], list(outs[2:2 + n]), outs[-1]


def _allgather_wait(started, after):
    names, send_sems, recv_sems, stacks, _ = started
    n = len(names)

    def body(*refs):
        st = refs[:n]
        send, recv = refs[n], refs[n + 1]
        x, y, c = _mesh_pos()
        for k in range(n):
            hr = st[k].shape[1] // 2
            for j in range(3):
                slot = st[k].at[0, pl.ds(0, hr), :]
                cp = _remote(slot, slot, send.at[3 * k + j], recv.at[3 * k + j], (x, y, c))
                cp.wait_send()
                cp.wait_recv()

    outs = pl.pallas_call(
        body, name="allgather_wait",
        in_specs=[HBM] * n + [SEM, SEM, ANY], out_specs=[HBM] * n,
        out_shape=[pltpu.HBM(a.shape, a.dtype) for a in stacks],
        input_output_aliases={k: k for k in range(n)},
        compiler_params=pltpu.CompilerParams(has_side_effects=_DATAFLOW),
    )(*stacks, send_sems, recv_sems, after)
    return dict(zip(names, outs))


def _allgather_forward(stacks):
    names = list(stacks)
    n = len(names)

    def body(*refs):
        ins, outs = refs[:n], refs[n:2 * n]
        send_sems, recv_sems = refs[2 * n:]
        x, y, c = _mesh_pos()
        copies = []
        for k in range(n):
            hr = ins[k].shape[1] // 2
            for j, (cx, cy) in enumerate(_other_chips(x, y)):
                chip = 2 * cx + cy
                copies.append(_remote(ins[k].at[chip, pl.ds(c * hr, hr), :], outs[k].at[chip, pl.ds(c * hr, hr), :],
                                      send_sems.at[3 * k + j], recv_sems.at[3 * k + j], (x, y, 1 - c)))
        for cp in copies:
            cp.start()
        for cp in copies:
            cp.wait()

    arrays = [stacks[k] for k in names]
    outs = pl.pallas_call(
        body, name="allgather_forward", in_specs=[HBM] * n, out_specs=[HBM] * n,
        input_output_aliases={k: k for k in range(n)},
        out_shape=[_sds(a.shape, a.dtype) for a in arrays],
        scratch_shapes=[pltpu.SemaphoreType.DMA((3 * n,)), pltpu.SemaphoreType.DMA((3 * n,))],
    )(*arrays)
    return dict(zip(names, outs))


def _owner_sum(part, from_sibling, from_chips, name, pos, shard_shape):
    kind = _KIND[name]
    _, pk, pw = from_chips.shape
    if kind == "row":
        tr, nb = pk, 1
        p_spec = pl.BlockSpec((tr, pw), lambda r, s: (s[0], s[1]))
        q_spec = pl.BlockSpec((tr, pw), lambda r, s: (s[0], 0))
        o_spec = pl.BlockSpec((tr, pw), lambda r, s: (0, s[1]))
    else:
        tr = 256
        nb = pk // tr
        if kind == "stack":
            p_spec = pl.BlockSpec((None, tr, pw), lambda r, s: (s[0], s[1] * nb + r, 0))
            q_spec = pl.BlockSpec((None, tr, pw), lambda r, s: (s[0], r, 0))
        else:
            p_spec = pl.BlockSpec((tr, pw), lambda r, s: (s[1] * nb + r, s[0]))
            q_spec = pl.BlockSpec((tr, pw), lambda r, s: (r, s[0]))
        o_spec = pl.BlockSpec((tr, pw), lambda r, s: (s[1] * nb + r, 0))

    def body(s_ref, p_ref, q_ref, r_ref, o_ref):
        acc = p_ref[...].astype(F32) + q_ref[...].astype(F32)
        for j in range(3):
            acc = acc + r_ref[j].astype(F32)
        o_ref[...] = acc

    return pl.pallas_call(
        body, name="grad_owner_sum_" + name,
        grid_spec=pltpu.PrefetchScalarGridSpec(
            num_scalar_prefetch=1, grid=(nb,),
            in_specs=[p_spec, q_spec, pl.BlockSpec((3, tr, pw), lambda r, s: (0, r, 0))],
            out_specs=o_spec),
        out_shape=_sds(shard_shape, F32),
        compiler_params=_cp(("arbitrary",), 32),
    )(pos, part, from_sibling, from_chips)


def _pair_share(shards):
    names = list(shards)
    n = len(names)

    def body(*refs):
        g_in, g_out = refs[:n], refs[n:2 * n]
        send_sems, recv_sems = refs[2 * n:]
        x, y, c = _mesh_pos()

        def copy(k, h):
            kind = _KIND[names[k]]
            return _remote(_region_view(g_in[k], kind, h), _region_view(g_out[k], kind, h), send_sems.at[k], recv_sems.at[k], (x, y, 1 - c))

        for hc in range(2):
            @pl.when(c == hc)
            def _():
                for k in range(n):
                    copy(k, hc).start()

        for k in range(n):
            copy(k, 0).wait()

    outs = pl.pallas_call(
        body, name="grad_pair_share", in_specs=[HBM] * n, out_specs=[HBM] * n, input_output_aliases={k: k for k in range(n)},
        out_shape=[_sds(shards[k].shape, shards[k].dtype) for k in names],
        scratch_shapes=[pltpu.SemaphoreType.DMA((n,)), pltpu.SemaphoreType.DMA((n,))],
    )(*[shards[k] for k in names])
    return dict(zip(names, outs))


def _allgather_small(block):
    m_per = block.shape[0]

    def body(x_ref, out_ref, send_sems, recv_sems, local_sem):
        x, y, c = _mesh_pos()
        me, sibling = (x, y, c), (x, y, 1 - c)
        chips = _other_chips(x, y)

        def rows(px, py, pc):
            return out_ref.at[4 * px + 2 * py + pc]

        def copy(k, block_of, to, src=None):
            return _remote(rows(*block_of) if src is None else src, rows(*block_of), send_sems.at[k], recv_sems.at[k], to)

        mine = pltpu.make_async_copy(x_ref, rows(*me), local_sem)
        mine.start()
        first = [copy(0, me, sibling, src=x_ref)]
        first += [copy(1 + j, me, (*chip, c), src=x_ref) for j, chip in enumerate(chips)]
        for cp in first:
            cp.start()
        passed = [copy(4 + j, (*chip, c), sibling) for j, chip in enumerate(chips)]
        for j, chip in enumerate(chips):
            copy(1 + j, (*chip, c), me).wait_recv()
            passed[j].start()
        copy(0, sibling, me).wait_recv()
        for j, chip in enumerate(chips):
            copy(4 + j, (*chip, 1 - c), me).wait_recv()
        for cp in first + passed:
            cp.wait_send()
        mine.wait()

    return pl.pallas_call(
        body, name="allgather_small",
        in_specs=[pl.BlockSpec(memory_space=pltpu.VMEM)], out_specs=pl.BlockSpec(memory_space=pltpu.VMEM),
        out_shape=_sds((N_DEV, m_per, D_MODEL), block.dtype),
        scratch_shapes=[pltpu.SemaphoreType.DMA((7,)), pltpu.SemaphoreType.DMA((7,)), pltpu.SemaphoreType.DMA],
    )(block)


def _adam_math(w, g, m, v):
    m = ADAM_B1 * m + (1.0 - ADAM_B1) * g
    v = ADAM_B2 * v + (1.0 - ADAM_B2) * (g * g)
    m_hat = m / (1.0 - ADAM_B1 ** ADAM_STEP)
    v_hat = v / (1.0 - ADAM_B2 ** ADAM_STEP)
    delta = -ADAM_LR * (m_hat / (jnp.sqrt(v_hat) + ADAM_EPS) + ADAM_WD * w)
    return delta, m, v


def _adamw(w, g, m, v, name):
    rows, cols = w.shape
    tr = rows
    for cand in (256, 128, 64, 32, 16, 8):
        if rows % cand == 0 and rows > cand:
            tr = cand
            break

    def body(w_ref, g_ref, m_ref, v_ref, d_ref, nm_ref, nv_ref):
        d, nm, nv = _adam_math(w_ref[...], g_ref[...], m_ref[...], v_ref[...])
        d_ref[...] = d
        nm_ref[...] = nm
        nv_ref[...] = nv

    spec = pl.BlockSpec((tr, cols), lambda i: (i, 0))
    return pl.pallas_call(
        body, name=name, grid=(rows // tr,), in_specs=[spec] * 4, out_specs=[spec] * 3,
        out_shape=[_sds(w.shape, F32)] * 3, compiler_params=_cp(("arbitrary",)),
    )(w, g, m, v)


def _small_sum_adamw(gathered, w, m, v):
    def body(a_ref, w_ref, m_ref, v_ref, g_ref, d_ref, nm_ref, nv_ref):
        g = a_ref[0]
        for k in range(1, N_DEV):
            g = g + a_ref[k]
        g_ref[...] = g
        d, nm, nv = _adam_math(w_ref[...], g, m_ref[...], v_ref[...])
        d_ref[...] = d
        nm_ref[...] = nm
        nv_ref[...] = nv

    return pl.pallas_call(
        body, name="small_sum_adamw", out_shape=[_sds(w.shape, F32)] * 4,
    )(gathered, w, m, v)


_NAMES = ("g_mix", "w_in", "g_sgu", "w_s", "b_s", "sinks", "rel_bias", "w_pa", "w_pb", "w_out",
          "g_ffn", "w_up", "w_conv", "b_conv", "w_down", "g_final")

def kernel(x, g_mix, w_in, g_sgu, w_s, b_s, sinks, rel_bias, w_pa, w_pb, w_out, g_ffn, w_up, w_conv, b_conv, w_down, g_final, loss_target, m_g_mix, m_w_in, m_g_sgu, m_w_s, m_b_s, m_sinks, m_rel_bias, m_w_pa, m_w_pb, m_w_out, m_g_ffn, m_w_up, m_w_conv, m_b_conv, m_w_down, m_g_final, v_g_mix, v_w_in, v_g_sgu, v_w_s, v_b_s, v_sinks, v_rel_bias, v_w_pa, v_w_pb, v_w_out, v_g_ffn, v_w_up, v_w_conv, v_b_conv, v_w_down, v_g_final):
    w = dict(g_mix=g_mix, w_in=w_in, g_sgu=g_sgu, w_s=w_s, b_s=b_s, sinks=sinks, rel_bias=rel_bias, w_pa=w_pa, w_pb=w_pb,
             w_out=w_out, g_ffn=g_ffn, w_up=w_up, w_conv=w_conv, b_conv=b_conv, w_down=w_down, g_final=g_final)
    m = dict(g_mix=m_g_mix, w_in=m_w_in, g_sgu=m_g_sgu, w_s=m_w_s, b_s=m_b_s, sinks=m_sinks, rel_bias=m_rel_bias, w_pa=m_w_pa,
             w_pb=m_w_pb, w_out=m_w_out, g_ffn=m_g_ffn, w_up=m_w_up, w_conv=m_w_conv, b_conv=m_b_conv, w_down=m_w_down,
             g_final=m_g_final)
    v = dict(g_mix=v_g_mix, w_in=v_w_in, g_sgu=v_g_sgu, w_s=v_w_s, b_s=v_b_s, sinks=v_sinks, rel_bias=v_rel_bias, w_pa=v_w_pa,
             w_pb=v_w_pb, w_out=v_w_out, g_ffn=v_g_ffn, w_up=v_w_up, w_conv=v_w_conv, b_conv=v_b_conv, w_down=v_w_down,
             g_final=v_g_final)
    xi, yi, ci = _mesh_pos()
    me = 2 * xi + yi

    shard = {n: w[n][0] for n in _BIG}
    shard_shapes = {n: shard[n].shape for n in _BIG}
    wc_shard = w["w_conv"][0]
    wc_pad = jnp.pad(wc_shard, ((0, 5), (0, 0)))
    own = {n: _own_slot(shard[n].astype(BF16), N_CHIPS, me) for n in _BIG}
    stacks, wc_all = _allgather_weights({"w_in": own["w_in"]}, _own_slot(wc_pad, N_CHIPS, me))
    late_gather = _allgather_start({n: own[n] for n in _BIG[1:]}, stacks["w_in"])
    w_conv_full = jnp.concatenate([wc_all[i, :3] for i in range(N_CHIPS)], axis=1)
    w_in_full = stacks["w_in"].transpose(1, 0, 2).reshape(D_MODEL, -1)
    w_a = w_in_full[:, :A_DIM]
    w_b = w_in_full[:, A_DIM:A_DIM + B_DIM]
    w_g = w_in_full[:, A_DIM + B_DIM:]
    pos = jnp.stack([me, ci])

    def late_weights(done):
        st = _allgather_forward(_allgather_wait(late_gather, done))
        return st["w_pa"], st["w_pb"], st["w_out"].reshape(D_MODEL, D_MODEL), st["w_up"], st["w_down"].reshape(D_FF, D_MODEL)

    from_sibling, exchanges = {}, []

    def on_grads(group, parts):
        sib = _pair_exchange(parts, group)
        from_sibling.update(sib)
        started = _chip_exchange_start({n: _pair_add(parts[n], sib[n], n, pos) for n in parts}, group)
        exchanges.append((group, started))
        return started[-1]

    loss, grad_x, small, big = _local_step(
        x, loss_target, w["g_mix"], w["g_sgu"], w["w_s"][0], w["b_s"][0], w["sinks"], w["rel_bias"], w["g_ffn"],
        w["b_conv"], w["g_final"], w_g, w_a, w_b, w_conv_full, late_weights, on_grads, late_gather[-1])

    small["loss"] = loss
    all_small = _allgather_small(_pack_small(small))
    sw = {n: (jnp.zeros((1, 1), F32) if n in ("loss", "w_conv") else w[n]) for n, _ in _SMALL}
    sm = {n: (jnp.zeros((1, 1), F32) if n in ("loss", "w_conv") else m[n]) for n, _ in _SMALL}
    sv = {n: (jnp.zeros((1, 1), F32) if n in ("loss", "w_conv") else v[n]) for n, _ in _SMALL}
    for d in (sw, sm, sv):
        d["w_conv"] = jnp.zeros((3, 2 * D_FF), F32)
    s_g, s_d, s_m, s_v = [_unpack_small(a) for a in _small_sum_adamw(all_small, _pack_small(sw), _pack_small(sm), _pack_small(sv))]

    from_chips = {}
    for group, started in exchanges:
        from_chips.update(_chip_exchange_wait(started, grad_x, group))
    g_big = _pair_share({n: _owner_sum(big[n], from_sibling[n], from_chips[n], n, pos, shard_shapes[n]) for n in _BIG})

    grads, deltas, new_m, new_v = {}, {}, {}, {}
    for n in _BIG:
        d, nm, nv = _adamw(shard[n], g_big[n], m[n][0], v[n][0], "adamw_" + n)
        grads[n], deltas[n], new_m[n], new_v[n] = g_big[n][None], d[None], nm[None], nv[None]
    wcols = wc_shard.shape[1]
    g_wc = lax.dynamic_slice(s_g["w_conv"], (0, me * wcols), (3, wcols))
    d, nm, nv = _adamw(wc_shard, g_wc, m["w_conv"][0], v["w_conv"][0], "adamw_w_conv")
    grads["w_conv"], deltas["w_conv"], new_m["w_conv"], new_v["w_conv"] = g_wc[None], d[None], nm[None], nv[None]
    for n, _ in _SMALL:
        if n in ("loss", "w_conv"):
            continue
        shp = w[n].shape
        grads[n], deltas[n], new_m[n], new_v[n] = (s_g[n].reshape(shp), s_d[n].reshape(shp), s_m[n].reshape(shp),
                                                    s_v[n].reshape(shp))

    return (s_g["loss"].reshape(()), grad_x, *[grads[n] for n in _NAMES], *[deltas[n] for n in _NAMES],
            *[new_m[n] for n in _NAMES], *[new_v[n] for n in _NAMES])
```

```python
import functools

import numpy as np
import jax
import jax.numpy as jnp
from jax import lax
from jax.experimental import pallas as pl
from jax.experimental.pallas import tpu as pltpu

F32 = jnp.float32
BF16 = jnp.bfloat16

D_MODEL = 1024
CHUNK = 128
A_GROUPS = 4
A_WIDTH = 512
N_HEADS = 8
HEAD_DIM = 64
Q_DIM = 512
KV_DIM = 128
N_BUCKETS = 32
MAX_DISTANCE = 128
D_FF = 2816
EPS = 1e-6
NEG_INF = -1e30
G_DIM = 2 * D_MODEL
A_DIM = 2 * A_WIDTH
B_DIM = Q_DIM + 2 * KV_DIM
LANES = 128
SUBLANES = 8
ROW_TILE = 256
GRAD_ROW_TILE = 512
BF16_ROWS = 16
N_CHIPS = 4
N_DEV = 8

ADAM_LR = 0.001
ADAM_B1 = 0.9
ADAM_B2 = 0.999
ADAM_EPS = 1e-08
ADAM_WD = 0.01
ADAM_STEP = 10

MESH = pl.DeviceIdType.MESH
_GELU_C = 0.7978845608028654
_GELU_A = 0.044715


def _cp(sem=None, vmem_mb=None):
    kw = {}
    if sem is not None:
        kw["dimension_semantics"] = sem
    if vmem_mb is not None:
        kw["vmem_limit_bytes"] = vmem_mb << 20
    return pltpu.CompilerParams(**kw)


def _dot(a, b):
    return jnp.dot(a, b, preferred_element_type=F32)


def _dot_nt(a, b):
    return lax.dot_general(a, b, (((1,), (1,)), ((), ())), preferred_element_type=F32)


def _dot_tn(a, b):
    return lax.dot_general(a, b, (((0,), (0,)), ((), ())), preferred_element_type=F32)


def _rms_r(x):
    return lax.rsqrt(jnp.mean(x * x, axis=-1, keepdims=True) + EPS)


def _rms_bwd(dh, n, r, g):
    dn = dh * g
    return r * (dn - n * jnp.mean(dn * n, axis=-1, keepdims=True))


def _gelu(x):
    t = jnp.tanh(_GELU_C * (x + _GELU_A * (x * x * x)))
    return 0.5 * x * (1.0 + t), t


def _gelu_grad(x, t):
    return 0.5 * (1.0 + t) + 0.5 * x * (1.0 - t * t) * (_GELU_C * (1.0 + 3.0 * _GELU_A * x * x))


def _sigmoid(x):
    return 1.0 / (1.0 + jnp.exp(-x))


def _row(tm, w):
    return pl.BlockSpec((tm, w), lambda i: (i, 0))


def _full(shape):
    nd = len(shape)
    return pl.BlockSpec(tuple(shape), lambda *_: (0,) * nd)


def _resident(shape):
    nd = len(shape)
    return pl.BlockSpec(tuple(shape), lambda *_: (0,) * nd, pipeline_mode=pl.Buffered(1))


def _sds(shape, dtype):
    return jax.ShapeDtypeStruct(tuple(shape), dtype)


HBM = pl.BlockSpec(memory_space=pltpu.HBM)
ANY = pl.BlockSpec(memory_space=pl.ANY)
SEM = pl.BlockSpec(memory_space=pltpu.SEMAPHORE)


def _band_buckets():
    i = np.arange(CHUNK)[:, None]
    j = np.arange(2 * CHUNK)[None, :]
    dist = i + CHUNK - j
    valid = (dist >= 0) & (dist < CHUNK)
    d = np.clip(dist, 0, None)
    max_exact = N_BUCKETS // 2
    large = max_exact + (np.log(np.maximum(d, 1) / max_exact) / np.log(MAX_DISTANCE / max_exact)
                         * (N_BUCKETS - max_exact)).astype(np.int32)
    large = np.minimum(large, N_BUCKETS - 1)
    buckets = np.where(d < max_exact, d, large).astype(np.int32)
    return np.where(valid, buckets, -1).astype(np.int32)


def _inproj(x2, g_mix, w_g, w_a, w_b, tm, after=None):
    T = x2.shape[0]
    order = [] if after is None else [after]

    def body(*refs):
        x_ref, g_ref, wg_ref, wa_ref, wb_ref = refs[:5]
        pg_ref, pa_ref, pb_ref, h_ref = refs[5 + len(order):]
        x = x_ref[...]
        h = (x * _rms_r(x) * g_ref[...]).astype(BF16)
        h_ref[...] = h
        pg_ref[...] = _dot(h, wg_ref[...]).astype(BF16)
        pa_ref[...] = _dot(h, wa_ref[...]).astype(BF16)
        pb_ref[...] = _dot(h, wb_ref[...]).astype(BF16)

    return pl.pallas_call(
        body, name="inproj", grid=(T // tm,),
        in_specs=[_row(tm, D_MODEL), _full(g_mix.shape), _resident(w_g.shape), _resident(w_a.shape), _resident(w_b.shape)]
        + [ANY] * len(order),
        out_specs=[_row(tm, G_DIM), _row(tm, A_DIM), _row(tm, B_DIM), _row(tm, D_MODEL)],
        out_shape=[_sds((T, G_DIM), BF16), _sds((T, A_DIM), BF16), _sds((T, B_DIM), BF16), _sds((T, D_MODEL), BF16)],
        compiler_params=_cp(("arbitrary",), 48),
    )(x2, g_mix, w_g, w_a, w_b, *order)


def _sgu_parts(p, g):
    pu = p[:, :A_WIDTH]
    pv = p[:, A_WIDTH:]
    u, tu = _gelu(pu)
    vv, tv = _gelu(pv)
    rv = _rms_r(vv)
    vn = (vv * rv * g).astype(BF16)
    return pu, pv, u, tu, vv, tv, rv, vn


def _tril():
    r = lax.broadcasted_iota(jnp.int32, (CHUNK, CHUNK), 0)
    c = lax.broadcasted_iota(jnp.int32, (CHUNK, CHUNK), 1)
    return r >= c


def _sgu_fwd(proj_a, g_sgu, w_s, b_st, tm):
    T = proj_a.shape[0]

    def body(p_ref, g_ref, ws_ref, bs_ref, y_ref):
        tril = _tril()
        _, _, u, _, _, _, _, vn = _sgu_parts(p_ref[...].astype(F32), g_ref[...])
        for gi in range(A_GROUPS):
            wm = jnp.where(tril, ws_ref[gi], 0.0).astype(BF16)
            bcol = bs_ref[:, gi:gi + 1]
            cs = slice(gi * CHUNK, (gi + 1) * CHUNK)
            for c in range(tm // CHUNK):
                rs = slice(c * CHUNK, (c + 1) * CHUNK)
                s = _dot(wm, vn[rs, cs]) + bcol
                y_ref[rs, cs] = (u[rs, cs] * s).astype(BF16)

    return pl.pallas_call(
        body, name="sgu_fwd", grid=(T // tm,),
        in_specs=[_row(tm, A_DIM), _full(g_sgu.shape), _full(w_s.shape), _full(b_st.shape)],
        out_specs=_row(tm, A_WIDTH), out_shape=_sds((T, A_WIDTH), BF16),
        compiler_params=_cp(("arbitrary",)),
    )(proj_a, g_sgu, w_s, b_st)


HEAD_ROWS = N_HEADS * CHUNK


def _head_rows(h):
    return slice(h * CHUNK, (h + 1) * CHUNK)


def _attn_setup(bias_scr, sink_scr, kvar_scr, qkv_ref, bk_ref, rel_ref, sink_ref):
    bk = bk_ref[...]
    for h in range(N_HEADS):
        acc = jnp.full((CHUNK, 2 * CHUNK), NEG_INF, F32)
        for b in range(N_BUCKETS):
            acc = jnp.where(bk == b, rel_ref[b, h], acc)
        bias_scr[_head_rows(h), :] = acc
        sink_scr[_head_rows(h), :] = jnp.full((CHUNK, LANES), sink_ref[0, h], F32)
    seq = qkv_ref.shape[0]
    rows_per = 2 * CHUNK
    for is_v in range(2):
        c0 = Q_DIM + is_v * KV_DIM
        for r in range(seq // rows_per):
            rs = slice(r * rows_per, (r + 1) * rows_per)
            a = qkv_ref[rs, c0:c0 + KV_DIM].astype(F32)
            lane = lax.broadcasted_iota(jnp.int32, a.shape, 1)
            lo = jnp.where(lane < HEAD_DIM, a, 0.0)
            hi = jnp.where(lane >= HEAD_DIM, a, 0.0)
            kvar_scr[4 * is_v + 0, rs, :] = lo.astype(BF16)
            kvar_scr[4 * is_v + 1, rs, :] = pltpu.roll(lo, HEAD_DIM, 1).astype(BF16)
            kvar_scr[4 * is_v + 2, rs, :] = pltpu.roll(hi, HEAD_DIM, 1).astype(BF16)
            kvar_scr[4 * is_v + 3, rs, :] = hi.astype(BF16)


def _rowsum(a, ones):
    hi = a.astype(BF16)
    lo = (a - hi.astype(F32)).astype(BF16)
    return _dot(hi, ones) + _dot(lo, ones)


def _both(a):
    return jnp.concatenate([a, a], axis=1)


def _attn_probs(qkv_ref, r0, n, kv, bias_scr, sink_scr, ones):
    s = jnp.concatenate([_dot_nt(qkv_ref[pl.ds(r0, CHUNK), (h // 2) * LANES:(h // 2 + 1) * LANES], kv[h // 4][h % 2])
                         for h in range(N_HEADS)], axis=0)
    s = s * (HEAD_DIM ** -0.5) + bias_scr[...]
    col = lax.broadcasted_iota(jnp.int32, s.shape, 1)
    s = jnp.where((col < CHUNK) & (n == 0), NEG_INF, s)
    sink = sink_scr[...]
    m = jnp.maximum(jnp.max(s, axis=-1, keepdims=True), sink)
    p = jnp.exp(s - _both(m))
    es = jnp.exp(sink - m)
    inv = 1.0 / (_rowsum(p, ones) + es)
    return p * _both(inv), es * inv


def _attn_block_inputs(kvar_scr, n):
    r0 = pl.multiple_of(n * CHUNK, CHUNK)
    rp = pl.multiple_of(jnp.maximum(n - 1, 0) * CHUNK, CHUNK)

    def both(idx):
        return jnp.concatenate([kvar_scr[idx, pl.ds(rp, CHUNK), :], kvar_scr[idx, pl.ds(r0, CHUNK), :]], axis=0)

    kv = ((both(0), both(1)), (both(2), both(3)))
    vv = ((both(4), both(5)), (both(6), both(7)))
    return r0, kv, vv


def _attn_fwd(proj_b, sinks, rel_bias, n_seq, seq):
    nb = seq // CHUNK
    bk = jnp.asarray(_band_buckets())

    def body(qkv_ref, bk_ref, rel_ref, sink_ref, o_ref, bias_scr, sink_scr, kvar_scr):
        _attn_setup(bias_scr, sink_scr, kvar_scr, qkv_ref, bk_ref, rel_ref, sink_ref)
        ones = jnp.ones((2 * CHUNK, LANES), BF16)

        def blk(n, carry):
            r0, kv, vv = _attn_block_inputs(kvar_scr, n)
            prob, _ = _attn_probs(qkv_ref, r0, n, kv, bias_scr, sink_scr, ones)
            pb = prob.astype(BF16)
            for pr in range(N_HEADS // 2):
                acc = _dot(pb[_head_rows(2 * pr)], vv[pr // 2][0]) + _dot(pb[_head_rows(2 * pr + 1)], vv[pr // 2][1])
                o_ref[pl.ds(r0, CHUNK), pr * LANES:(pr + 1) * LANES] = acc.astype(BF16)
            return carry

        lax.fori_loop(0, nb, blk, 0)

    smem = pl.BlockSpec(memory_space=pltpu.SMEM)
    return pl.pallas_call(
        body, name="attn_fwd", grid=(n_seq,),
        in_specs=[_row(seq, B_DIM), _full(bk.shape), smem, smem],
        out_specs=_row(seq, Q_DIM), out_shape=_sds((n_seq * seq, Q_DIM), BF16),
        scratch_shapes=[pltpu.VMEM((HEAD_ROWS, 2 * CHUNK), F32), pltpu.VMEM((HEAD_ROWS, LANES), F32),
                        pltpu.VMEM((8, seq, KV_DIM), BF16)],
        compiler_params=_cp(("arbitrary",), 40),
    )(proj_b, bk, rel_bias, sinks)


def _dot_stacked(a, w_ref):
    return jnp.concatenate([_dot(a, w_ref[i]) for i in range(N_CHIPS)], axis=1)


def _dot_nt_stacked(a, w_ref):
    w = w_ref.shape[2]
    acc = _dot_nt(a[:, :w], w_ref[0])
    for i in range(1, N_CHIPS):
        acc = acc + _dot_nt(a[:, i * w:(i + 1) * w], w_ref[i])
    return acc


def _merge_fwd(x2, y_a, y_b, proj_g, w_pa, w_pb, w_out, tm):
    T = x2.shape[0]

    def body(x_ref, ya_ref, yb_ref, g_ref, wpa_ref, wpb_ref, wo_ref, x1_ref, mg_ref):
        g = g_ref[...].astype(F32)
        pa = _dot_stacked(ya_ref[...], wpa_ref)
        pb = _dot_stacked(yb_ref[...], wpb_ref)
        merged = (_sigmoid(g[:, :D_MODEL]) * pa + _sigmoid(g[:, D_MODEL:]) * pb).astype(BF16)
        mg_ref[...] = merged
        x1_ref[...] = x_ref[...] + _dot(merged, wo_ref[...])

    return pl.pallas_call(
        body, name="merge_fwd", grid=(T // tm,),
        in_specs=[_row(tm, D_MODEL), _row(tm, A_WIDTH), _row(tm, Q_DIM), _row(tm, G_DIM),
                  _resident(w_pa.shape), _resident(w_pb.shape), _resident(w_out.shape)],
        out_specs=[_row(tm, D_MODEL), _row(tm, D_MODEL)],
        out_shape=[_sds((T, D_MODEL), F32), _sds((T, D_MODEL), BF16)],
        compiler_params=_cp(("arbitrary",), 40),
    )(x2, y_a, y_b, proj_g, w_pa, w_pb, w_out)


def _upproj(x1, g_ffn, w_up, w_conv, b_conv, tm, seq):
    T = x1.shape[0]
    cw = w_up.shape[2]
    tiles_per_seq = seq // tm

    def body(x_ref, g_ref, w_ref, wc_ref, bc_ref, u_ref, h_ref, gate_ref, val_ref, tail_scr):
        at_start = (pl.program_id(0) % tiles_per_seq) == 0
        x = x_ref[...]
        h = (x * _rms_r(x) * g_ref[...]).astype(BF16)
        h_ref[...] = h
        for i in range(N_CHIPS):
            cs = slice(i * cw, (i + 1) * cw)
            u = _dot(h, w_ref[i])
            u_ref[:, cs] = u.astype(BF16)
            hl = jnp.where(at_start, 0.0, tail_scr[SUBLANES - 2:SUBLANES, cs])
            tail_scr[:, cs] = u[tm - SUBLANES:]
            up = _conv_out((u, _shift_down(u, hl, 1), _shift_down(u, hl, 2)), wc_ref[:, cs], bc_ref[:, cs])
            out_ref = gate_ref if i < N_CHIPS // 2 else val_ref
            out_ref[:, (i % 2) * cw:(i % 2 + 1) * cw] = up.astype(BF16)

    return pl.pallas_call(
        body, name="upproj", grid=(T // tm,),
        in_specs=[_row(tm, D_MODEL), _full(g_ffn.shape), _resident(w_up.shape), _full(w_conv.shape), _full(b_conv.shape)],
        out_specs=[_row(tm, 2 * D_FF), _row(tm, D_MODEL), _row(tm, D_FF), _row(tm, D_FF)],
        out_shape=[_sds((T, 2 * D_FF), BF16), _sds((T, D_MODEL), BF16), _sds((T, D_FF), BF16), _sds((T, D_FF), BF16)],
        scratch_shapes=[pltpu.VMEM((SUBLANES, 2 * D_FF), F32)],
        compiler_params=_cp(("arbitrary",), 56),
    )(x1, g_ffn, w_up, w_conv, b_conv)


def _shift_down(u, halo, k):
    rolled = pltpu.roll(u, k, 0)
    head = rolled[:SUBLANES]
    row = lax.broadcasted_iota(jnp.int32, head.shape, 0)
    if k == 1:
        head = jnp.where(row == 0, halo[1:2], head)
    else:
        head = jnp.where(row == 0, halo[0:1], jnp.where(row == 1, halo[1:2], head))
    return jnp.concatenate([head, rolled[SUBLANES:]], axis=0)


def _shift_up(d, halo, k):
    tm = d.shape[0]
    rolled = pltpu.roll(d, tm - k, 0)
    tail = rolled[tm - SUBLANES:]
    row = lax.broadcasted_iota(jnp.int32, tail.shape, 0)
    if k == 1:
        tail = jnp.where(row == SUBLANES - 1, halo[0:1], tail)
    else:
        tail = jnp.where(row == SUBLANES - 2, halo[0:1], jnp.where(row == SUBLANES - 1, halo[1:2], tail))
    return jnp.concatenate([rolled[:tm - SUBLANES], tail], axis=0)


def _conv_out(taps, wc, bc):
    u, u1, u2 = taps
    return wc[0:1] * u2 + wc[1:2] * u1 + wc[2:3] * u + bc


def _ffn_down_loss(gate, val, x1, target, w_down, g_final, tm):
    T = x1.shape[0]
    half = D_FF // 2

    def body(gt_ref, vl_ref, x1_ref, t_ref, wd_ref, g_ref, dx2_ref, loss_ref, gg_ref):
        i = pl.program_id(0)
        acc = jnp.zeros((tm, D_MODEL), F32)
        for j in range(2):
            gc = slice(j * half, (j + 1) * half)
            gate = gt_ref[:, gc].astype(F32)
            act = (gate * _sigmoid(gate) * vl_ref[:, gc].astype(F32)).astype(BF16)
            acc = acc + _dot(act, wd_ref[gc, :])
        x2 = x1_ref[...] + acc
        r = _rms_r(x2)
        n = x2 * r
        g = g_ref[...]
        diff = n * g - t_ref[...]
        dy = diff * (1.0 / D_MODEL)
        dx2_ref[...] = _rms_bwd(dy, n, r, g)

        @pl.when(i == 0)
        def _():
            loss_ref[...] = jnp.zeros_like(loss_ref)
            gg_ref[...] = jnp.zeros_like(gg_ref)

        loss_ref[...] += 0.5 * jnp.sum(jnp.mean(diff * diff, axis=-1, keepdims=True), axis=0, keepdims=True)
        gg_ref[...] += jnp.sum(dy * n, axis=0, keepdims=True)

    return pl.pallas_call(
        body, name="ffn_down_loss", grid=(T // tm,),
        in_specs=[_row(tm, D_FF), _row(tm, D_FF), _row(tm, D_MODEL), _row(tm, D_MODEL),
                  _resident(w_down.shape), _full(g_final.shape)],
        out_specs=[_row(tm, D_MODEL), _full((1, 1)), _full((1, D_MODEL))],
        out_shape=[_sds((T, D_MODEL), F32), _sds((1, 1), F32), _sds((1, D_MODEL), F32)],
        compiler_params=_cp(("arbitrary",), 48),
    )(gate, val, x1, target, w_down, g_final)


def _ffn_bwd_act(gate, val, dx2, w_down, tm):
    T = dx2.shape[0]
    half = D_FF // 2
    nt = T // tm

    def body(g_ref, v_ref, dx_ref, wd_ref, dg_ref, dv_ref, gwd_out, gbg_ref, gbv_ref, gwd_ref):
        i = pl.program_id(1)
        gate = g_ref[...].astype(F32)
        val = v_ref[...].astype(F32)
        sg = _sigmoid(gate)
        silu = gate * sg
        dx = dx_ref[...].astype(BF16)
        d_act = _dot_nt(dx, wd_ref[...])
        d_val = d_act * silu
        d_gate = d_act * val * (sg * (1.0 + gate * (1.0 - sg)))
        dg_ref[...] = d_gate.astype(BF16)
        dv_ref[...] = d_val.astype(BF16)

        @pl.when(i == 0)
        def _():
            for r in (gwd_ref, gbg_ref, gbv_ref):
                r[...] = jnp.zeros_like(r)

        gwd_ref[...] += _dot_tn((silu * val).astype(BF16), dx)
        gbg_ref[...] += jnp.sum(d_gate, axis=0, keepdims=True)
        gbv_ref[...] += jnp.sum(d_val, axis=0, keepdims=True)

        @pl.when(i == nt - 1)
        def _():
            gwd_out[...] = gwd_ref[...].astype(BF16)

    tile = pl.BlockSpec((tm, half), lambda j, i: (i, j))
    vec = pl.BlockSpec((1, half), lambda j, i: (0, j))
    wrows = pl.BlockSpec((half, D_MODEL), lambda j, i: (j, 0))
    return pl.pallas_call(
        body, name="ffn_bwd_act", grid=(2, nt),
        in_specs=[tile, tile, pl.BlockSpec((tm, D_MODEL), lambda j, i: (i, 0)), wrows],
        out_specs=[tile, tile, wrows, vec, vec],
        out_shape=[_sds((T, D_FF), BF16), _sds((T, D_FF), BF16), _sds((D_FF, D_MODEL), BF16),
                   _sds((1, D_FF), F32), _sds((1, D_FF), F32)],
        scratch_shapes=[pltpu.VMEM((half, D_MODEL), F32)],
        compiler_params=_cp(("arbitrary", "arbitrary"), 56),
    )(gate, val, dx2, w_down)


def _ffn_bwd_up(d_gate, d_val, upre, dx2, x1, g_ffn, w_conv, w_up, tm, seq):
    T = dx2.shape[0]
    tiles_per_seq = seq // tm
    k16 = tm // BF16_ROWS
    n16 = T // BF16_ROWS
    cw = D_FF // 2

    def body(dg_ref, dv_ref, hg_ref, hv_ref, u_ref, dx2_ref, x1_ref, g_ref, wc_ref, wu_ref, du_ref, dx1_ref, gg_ref, gwc_ref):
        i = pl.program_id(0)
        at_end = (i % tiles_per_seq) == tiles_per_seq - 1

        @pl.when(i == 0)
        def _():
            gg_ref[...] = jnp.zeros_like(gg_ref)
            gwc_ref[...] = jnp.zeros_like(gwc_ref)

        dh = jnp.zeros((tm, D_MODEL), F32)
        for j in range(4):
            src, hsrc = (dg_ref, hg_ref) if j < 2 else (dv_ref, hv_ref)
            ls = slice((j % 2) * cw, (j % 2 + 1) * cw)
            cs = slice(j * cw, (j + 1) * cw)
            d = src[:, ls].astype(F32)
            hl = hsrc[:, ls].astype(F32)[0:2]
            hl = jnp.where(at_end, 0.0, hl)
            wc = wc_ref[:, cs]
            d1 = _shift_up(d, hl, 1)
            d2 = _shift_up(d, hl, 2)
            du = (wc[2:3] * d + wc[1:2] * d1 + wc[0:1] * d2).astype(BF16)
            du_ref[:, cs] = du
            dh = dh + _dot_nt(du, wu_ref[j])
            u = u_ref[:, cs].astype(F32)
            gwc_ref[0:1, cs] += jnp.sum(d2 * u, axis=0, keepdims=True)
            gwc_ref[1:2, cs] += jnp.sum(d1 * u, axis=0, keepdims=True)
            gwc_ref[2:3, cs] += jnp.sum(d * u, axis=0, keepdims=True)
        x = x1_ref[...]
        r = _rms_r(x)
        n = x * r
        dx1_ref[...] = dx2_ref[...] + _rms_bwd(dh, n, r, g_ref[...])
        gg_ref[...] += jnp.sum(dh * n, axis=0, keepdims=True)

    nxt = pl.BlockSpec((BF16_ROWS, D_FF), lambda i: (jnp.minimum((i + 1) * k16, n16 - 1), 0))
    return pl.pallas_call(
        body, name="ffn_bwd_up", grid=(T // tm,),
        in_specs=[_row(tm, D_FF), _row(tm, D_FF), nxt, nxt, _row(tm, 2 * D_FF), _row(tm, D_MODEL), _row(tm, D_MODEL),
                  _full(g_ffn.shape), _full(w_conv.shape), _resident(w_up.shape)],
        out_specs=[_row(tm, 2 * D_FF), _row(tm, D_MODEL), _full((1, D_MODEL)), _full((3, 2 * D_FF))],
        out_shape=[_sds((T, 2 * D_FF), BF16), _sds((T, D_MODEL), F32), _sds((1, D_MODEL), F32), _sds((3, 2 * D_FF), F32)],
        compiler_params=_cp(("arbitrary",), 56),
    )(d_gate, d_val, d_gate, d_val, upre, dx2, x1, g_ffn, w_conv, w_up)


def _matmul_tn(a, b, tn, tk, name):
    T, M = a.shape
    N = b.shape[1]
    nk = T // tk

    def body(a_ref, b_ref, o_ref, acc_ref):
        k = pl.program_id(1)

        @pl.when(k == 0)
        def _():
            acc_ref[...] = jnp.zeros_like(acc_ref)

        acc_ref[...] += _dot_tn(a_ref[...], b_ref[...])

        @pl.when(k == nk - 1)
        def _():
            o_ref[...] = acc_ref[...].astype(BF16)

    return pl.pallas_call(
        body, name=name, grid=(N // tn, nk),
        in_specs=[pl.BlockSpec((tk, M), lambda j, k: (k, 0)), pl.BlockSpec((tk, tn), lambda j, k: (k, j))],
        out_specs=pl.BlockSpec((M, tn), lambda j, k: (0, j)), out_shape=_sds((M, N), BF16),
        scratch_shapes=[pltpu.VMEM((M, tn), F32)],
        compiler_params=_cp(("arbitrary", "arbitrary"), 48),
    )(a, b)


def _merge_bwd(dx1, merged, y_a, y_b, proj_g, w_pa, w_pb, w_out, tm, after=None):
    T = dx1.shape[0]

    nt = T // tm
    pshape = (A_WIDTH, D_MODEL)
    order = [] if after is None else [after]

    def body(*refs):
        dx_ref, mg_ref, ya_ref, yb_ref, g_ref, wpa_ref, wpb_ref, wo_ref = refs[:8]
        dg_ref, dya_ref, dyb_ref, gwo_out, gwpa_out, gwpb_out, gwo_ref, gwpa_ref, gwpb_ref = refs[8 + len(order):]
        i = pl.program_id(0)
        dx = dx_ref[...].astype(BF16)
        dm = _dot_nt(dx, wo_ref[...])
        g = g_ref[...].astype(F32)
        ya = ya_ref[...]
        yb = yb_ref[...]
        pa = _dot_stacked(ya, wpa_ref)
        pb = _dot_stacked(yb, wpb_ref)
        sa = _sigmoid(g[:, :D_MODEL])
        sb = _sigmoid(g[:, D_MODEL:])
        dpa = (dm * sa).astype(BF16)
        dpb = (dm * sb).astype(BF16)
        dg_ref[:, :D_MODEL] = (dm * pa * (sa * (1.0 - sa))).astype(BF16)
        dg_ref[:, D_MODEL:] = (dm * pb * (sb * (1.0 - sb))).astype(BF16)
        dya_ref[...] = _dot_nt_stacked(dpa, wpa_ref).astype(BF16)
        dyb_ref[...] = _dot_nt_stacked(dpb, wpb_ref).astype(BF16)

        @pl.when(i == 0)
        def _():
            for r in (gwo_ref, gwpa_ref, gwpb_ref):
                r[...] = jnp.zeros_like(r)

        gwo_ref[...] += _dot_tn(mg_ref[...], dx)
        gwpa_ref[...] += _dot_tn(ya, dpa)
        gwpb_ref[...] += _dot_tn(yb, dpb)

        @pl.when(i == nt - 1)
        def _():
            gwo_out[...] = gwo_ref[...].astype(BF16)
            gwpa_out[...] = gwpa_ref[...].astype(BF16)
            gwpb_out[...] = gwpb_ref[...].astype(BF16)

    return pl.pallas_call(
        body, name="merge_bwd", grid=(nt,),
        in_specs=[_row(tm, D_MODEL), _row(tm, D_MODEL), _row(tm, A_WIDTH), _row(tm, Q_DIM), _row(tm, G_DIM),
                  _resident(w_pa.shape), _resident(w_pb.shape), _resident(w_out.shape)] + [ANY] * len(order),
        out_specs=[_row(tm, G_DIM), _row(tm, A_WIDTH), _row(tm, Q_DIM),
                   _full(w_out.shape), _full(pshape), _full(pshape)],
        out_shape=[_sds((T, G_DIM), BF16), _sds((T, A_WIDTH), BF16), _sds((T, Q_DIM), BF16),
                   _sds(w_out.shape, BF16), _sds(pshape, BF16), _sds(pshape, BF16)],
        scratch_shapes=[pltpu.VMEM(w_out.shape, F32), pltpu.VMEM(pshape, F32), pltpu.VMEM(pshape, F32)],
        compiler_params=_cp(("arbitrary",), 56),
    )(dx1, merged, y_a, y_b, proj_g, w_pa, w_pb, w_out, *order)


def _sgu_bwd(proj_a, d_ya, g_sgu, w_s, b_st, tm, after=None):
    T = proj_a.shape[0]
    order = [] if after is None else [after]

    def body(*refs):
        p_ref, dy_ref, g_ref, ws_ref, bs_ref = refs[:5]
        dp_ref, gws_ref, gbs_ref, gg_ref = refs[5 + len(order):]
        tril = _tril()
        g = g_ref[...]
        pu, pv, u, tu, vv, tv, rv, vn = _sgu_parts(p_ref[...].astype(F32), g)
        dy = dy_ref[...].astype(F32)

        @pl.when(pl.program_id(0) == 0)
        def _():
            for r in (gws_ref, gbs_ref, gg_ref):
                r[...] = jnp.zeros_like(r)

        du_cols = []
        dvn_cols = []
        for gi in range(A_GROUPS):
            wm = jnp.where(tril, ws_ref[gi], 0.0).astype(BF16)
            wmt = wm.astype(F32).T.astype(BF16)
            bcol = bs_ref[:, gi:gi + 1]
            cs = slice(gi * CHUNK, (gi + 1) * CHUNK)
            du_rows = []
            dvn_rows = []
            gw = jnp.zeros((CHUNK, CHUNK), F32)
            gb = jnp.zeros((CHUNK, 1), F32)
            for c in range(tm // CHUNK):
                rs = slice(c * CHUNK, (c + 1) * CHUNK)
                vn_c = vn[rs, cs]
                s = _dot(wm, vn_c) + bcol
                dy_c = dy[rs, cs]
                ds = dy_c * u[rs, cs]
                du_rows.append(dy_c * s)
                dsb = ds.astype(BF16)
                gw = gw + _dot_nt(dsb, vn_c)
                gb = gb + jnp.sum(ds, axis=-1, keepdims=True)
                dvn_rows.append(_dot(wmt, dsb))
            gws_ref[gi] += jnp.where(tril, gw, 0.0)
            gbs_ref[:, gi:gi + 1] += gb
            du_cols.append(jnp.concatenate(du_rows, axis=0))
            dvn_cols.append(jnp.concatenate(dvn_rows, axis=0))
        du = jnp.concatenate(du_cols, axis=1)
        dvn = jnp.concatenate(dvn_cols, axis=1)
        vhat = vv * rv
        gg_ref[...] += jnp.sum(dvn * vhat, axis=0, keepdims=True)
        dvv = _rms_bwd(dvn, vhat, rv, g)
        dp_ref[:, :A_WIDTH] = (du * _gelu_grad(pu, tu)).astype(BF16)
        dp_ref[:, A_WIDTH:] = (dvv * _gelu_grad(pv, tv)).astype(BF16)

    return pl.pallas_call(
        body, name="sgu_bwd", grid=(T // tm,),
        in_specs=[_row(tm, A_DIM), _row(tm, A_WIDTH), _full(g_sgu.shape), _full(w_s.shape), _full(b_st.shape)] + [ANY] * len(order),
        out_specs=[_row(tm, A_DIM), _full(w_s.shape), _full(b_st.shape), _full(g_sgu.shape)],
        out_shape=[_sds((T, A_DIM), BF16), _sds(w_s.shape, F32), _sds(b_st.shape, F32), _sds(g_sgu.shape, F32)],
        compiler_params=_cp(("arbitrary",)),
    )(proj_a, d_ya, g_sgu, w_s, b_st, *order)


def _attn_bwd(proj_b, d_yb, sinks, rel_bias, n_seq, seq):
    nb = seq // CHUNK
    bk = jnp.asarray(_band_buckets())

    def body(qkv_ref, do_ref, bk_ref, rel_ref, sink_ref, d_ref, gs_ref, gr_ref,
             bias_scr, sink_scr, kvar_scr, dbias_scr, dk_scr, dv_scr, ds_scr):
        b = pl.program_id(0)
        _attn_setup(bias_scr, sink_scr, kvar_scr, qkv_ref, bk_ref, rel_ref, sink_ref)
        ones = jnp.ones((2 * CHUNK, LANES), BF16)

        @pl.when(b == 0)
        def _():
            dbias_scr[...] = jnp.zeros_like(dbias_scr)
            ds_scr[...] = jnp.zeros_like(ds_scr)

        dk_scr[...] = jnp.zeros_like(dk_scr)
        dv_scr[...] = jnp.zeros_like(dv_scr)

        def transposed(a):
            return a.astype(F32).T.astype(BF16)

        def blk(n, carry):
            r0, kv, vv = _attn_block_inputs(kvar_scr, n)
            prob, psink = _attn_probs(qkv_ref, r0, n, kv, bias_scr, sink_scr, ones)
            dp = jnp.concatenate([_dot_nt(do_ref[pl.ds(r0, CHUNK), (h // 2) * LANES:(h // 2 + 1) * LANES], vv[h // 4][h % 2])
                                  for h in range(N_HEADS)], axis=0)
            delta = _rowsum(prob * dp, ones)
            dsc = prob * (dp - _both(delta))
            ds_scr[...] += psink * delta
            dbias_scr[...] += dsc
            dsb = (dsc * (HEAD_DIM ** -0.5)).astype(BF16)
            pb = prob.astype(BF16)
            dkt = [jnp.zeros((HEAD_DIM, 2 * CHUNK), F32) for _ in range(2)]
            dvt = [jnp.zeros((HEAD_DIM, 2 * CHUNK), F32) for _ in range(2)]
            for pr in range(N_HEADS // 2):
                ps = slice(pr * LANES, (pr + 1) * LANES)
                qpt = transposed(qkv_ref[pl.ds(r0, CHUNK), ps])
                dopt = transposed(do_ref[pl.ds(r0, CHUNK), ps])
                kvh = pr // 2
                dq = jnp.zeros((CHUNK, LANES), F32)
                for hh in range(2):
                    hr = _head_rows(2 * pr + hh)
                    rows = slice(hh * HEAD_DIM, (hh + 1) * HEAD_DIM)
                    dq = dq + _dot(dsb[hr], kv[kvh][hh])
                    dkt[kvh] = dkt[kvh] + _dot(qpt, dsb[hr])[rows]
                    dvt[kvh] = dvt[kvh] + _dot(dopt, pb[hr])[rows]
                d_ref[pl.ds(r0, CHUNK), ps] = dq.astype(BF16)
            dk_scr[:, pl.ds(r0, 2 * CHUNK)] += jnp.concatenate(dkt, axis=0)
            dv_scr[:, pl.ds(r0, 2 * CHUNK)] += jnp.concatenate(dvt, axis=0)
            return carry

        lax.fori_loop(0, nb, blk, 0)
        for n in range(nb):
            rows = slice(n * CHUNK, (n + 1) * CHUNK)
            cols = slice((n + 1) * CHUNK, (n + 2) * CHUNK)
            d_ref[rows, Q_DIM:Q_DIM + KV_DIM] = dk_scr[:, cols].T.astype(BF16)
            d_ref[rows, Q_DIM + KV_DIM:] = dv_scr[:, cols].T.astype(BF16)

        @pl.when(b == n_seq - 1)
        def _():
            bkv = bk_ref[...]
            for h in range(N_HEADS):
                gs_ref[0:1, h:h + 1] = -jnp.sum(ds_scr[_head_rows(h), 0:1], axis=0, keepdims=True)
                db = dbias_scr[_head_rows(h), :]
                for bb in range(N_BUCKETS):
                    part = jnp.sum(jnp.where(bkv == bb, db, 0.0), axis=-1, keepdims=True)
                    gr_ref[bb:bb + 1, h:h + 1] = jnp.sum(part, axis=0, keepdims=True)

    smem = pl.BlockSpec(memory_space=pltpu.SMEM)
    return pl.pallas_call(
        body, name="attn_bwd", grid=(n_seq,),
        in_specs=[_row(seq, B_DIM), _row(seq, Q_DIM), _full(bk.shape), smem, smem],
        out_specs=[_row(seq, B_DIM), _full((1, N_HEADS)), _full((N_BUCKETS, N_HEADS))],
        out_shape=[_sds((n_seq * seq, B_DIM), BF16), _sds((1, N_HEADS), F32), _sds((N_BUCKETS, N_HEADS), F32)],
        scratch_shapes=[pltpu.VMEM((HEAD_ROWS, 2 * CHUNK), F32), pltpu.VMEM((HEAD_ROWS, LANES), F32),
                        pltpu.VMEM((8, seq, KV_DIM), BF16), pltpu.VMEM((HEAD_ROWS, 2 * CHUNK), F32),
                        pltpu.VMEM((KV_DIM, seq + CHUNK), F32), pltpu.VMEM((KV_DIM, seq + CHUNK), F32),
                        pltpu.VMEM((HEAD_ROWS, LANES), F32)],
        compiler_params=_cp(("arbitrary",), 40),
    )(proj_b, d_yb, bk, rel_bias, sinks)


def _inproj_bwd(d_g, d_a, d_b, x2, dx1, g_mix, w_g, w_a, w_b, tm, after=None):
    T = x2.shape[0]
    order = [] if after is None else [after]

    def body(*refs):
        dg_ref, da_ref, db_ref, x_ref, dx1_ref, g_ref, wg_ref, wa_ref, wb_ref = refs[:9]
        gx_ref, gg_ref = refs[9 + len(order):]
        dh = _dot_nt(dg_ref[...], wg_ref[...]) + _dot_nt(da_ref[...], wa_ref[...]) + _dot_nt(db_ref[...], wb_ref[...])
        x = x_ref[...]
        r = _rms_r(x)
        n = x * r
        gx_ref[...] = dx1_ref[...] + _rms_bwd(dh, n, r, g_ref[...])

        @pl.when(pl.program_id(0) == 0)
        def _():
            gg_ref[...] = jnp.zeros_like(gg_ref)

        gg_ref[...] += jnp.sum(dh * n, axis=0, keepdims=True)

    return pl.pallas_call(
        body, name="inproj_bwd", grid=(T // tm,),
        in_specs=[_row(tm, G_DIM), _row(tm, A_DIM), _row(tm, B_DIM), _row(tm, D_MODEL), _row(tm, D_MODEL),
                  _full(g_mix.shape), _resident(w_g.shape), _resident(w_a.shape), _resident(w_b.shape)] + [ANY] * len(order),
        out_specs=[_row(tm, D_MODEL), _full((1, D_MODEL))],
        out_shape=[_sds((T, D_MODEL), F32), _sds((1, D_MODEL), F32)],
        compiler_params=_cp(("arbitrary",), 48),
    )(d_g, d_a, d_b, x2, dx1, g_mix, w_g, w_a, w_b, *order)


def _local_step(x, target, g_mix, g_sgu, w_s, b_s, sinks, rel_bias, g_ffn, b_conv, g_final,
                w_g, w_a, w_b, w_conv, late_weights, on_grads, after=None):
    n_seq, seq, _ = x.shape
    T = n_seq * seq
    tm = min(ROW_TILE, seq)
    tw = min(GRAD_ROW_TILE, T)
    x2 = x.reshape(T, D_MODEL)
    tgt = target.reshape(T, D_MODEL)
    b_st = b_s.T
    g_fin = g_final.reshape(1, D_MODEL)

    proj_g, proj_a, proj_b, h = _inproj(x2, g_mix, w_g, w_a, w_b, tm, after)
    y_a = _sgu_fwd(proj_a, g_sgu, w_s, b_st, tm)
    y_b = _attn_fwd(proj_b, sinks, rel_bias, n_seq, seq)
    w_pa, w_pb, w_out, w_up, w_down = late_weights(y_b)
    x1, merged = _merge_fwd(x2, y_a, y_b, proj_g, w_pa, w_pb, w_out, tm)
    upre, h2, gate, val = _upproj(x1, g_ffn, w_up, w_conv, b_conv, tm, seq)
    dx2, loss, gg_final = _ffn_down_loss(gate, val, x1, tgt, w_down, g_fin, tm)

    d_gate, d_val, gw_down, gb_g, gb_v = _ffn_bwd_act(gate, val, dx2, w_down, tw)
    gb_conv = jnp.concatenate([gb_g, gb_v], axis=1)
    d_upre, dx1, gg_ffn, gw_conv = _ffn_bwd_up(d_gate, d_val, upre, dx2, x1, g_ffn, w_conv, w_up, tm, seq)
    gw_up = _matmul_tn(h2, d_upre, 2 * D_FF // 4, min(2 * GRAD_ROW_TILE, T), "grad_w_up")
    sent = on_grads("ffn", dict(w_up=gw_up, w_down=gw_down))
    d_g, d_ya, d_yb, gw_out, gw_pa, gw_pb = _merge_bwd(dx1, merged, y_a, y_b, proj_g, w_pa, w_pb, w_out, tm, sent)
    sent = on_grads("proj", dict(w_pa=gw_pa, w_pb=gw_pb, w_out=gw_out))
    d_a, gw_s, gb_st, gg_sgu = _sgu_bwd(proj_a, d_ya, g_sgu, w_s, b_st, tm, sent)
    d_b, g_sinks, g_rel = _attn_bwd(proj_b, d_yb, sinks, rel_bias, n_seq, seq)
    gw_g = _matmul_tn(h, d_g, D_MODEL, min(2 * GRAD_ROW_TILE, T), "grad_w_in_gate")
    gw_a = _matmul_tn(h, d_a, A_DIM, min(2 * GRAD_ROW_TILE, T), "grad_w_in_a")
    gw_b = _matmul_tn(h, d_b, B_DIM, min(2 * GRAD_ROW_TILE, T), "grad_w_in_b")
    gw_in = jnp.concatenate([gw_a, gw_b, gw_g], axis=1).reshape(D_MODEL, N_CHIPS, -1).transpose(1, 0, 2)
    sent = on_grads("in", dict(w_in=gw_in))
    grad_x, gg_mix = _inproj_bwd(d_g, d_a, d_b, x2, dx1, g_mix, w_g, w_a, w_b, tm, sent)

    small = dict(g_mix=gg_mix, g_sgu=gg_sgu, w_s=gw_s, b_s=gb_st.T, sinks=g_sinks, rel_bias=g_rel,
                 g_ffn=gg_ffn, b_conv=gb_conv, g_final=gg_final, w_conv=gw_conv)
    big = dict(w_in=gw_in, w_pa=gw_pa, w_pb=gw_pb, w_out=gw_out, w_up=gw_up, w_down=gw_down)
    return loss, grad_x.reshape(x.shape), small, big


_MIXER = ("w_in", "w_pa", "w_pb", "w_out")
_FFN = ("w_up", "w_down")
_BIG = _MIXER + _FFN

_SMALL = (("loss", (1, 1)), ("g_final", (1, D_MODEL)), ("g_mix", (1, D_MODEL)), ("g_ffn", (1, D_MODEL)),
          ("g_sgu", (1, A_WIDTH)), ("b_s", (A_GROUPS, CHUNK)), ("sinks", (1, N_HEADS)), ("rel_bias", (N_BUCKETS, N_HEADS)),
          ("b_conv", (1, 2 * D_FF)), ("w_conv", (3, 2 * D_FF)), ("w_s", (A_GROUPS, CHUNK, CHUNK)))
SMALL_ROWS = 96


def _pack_small(vals):
    flat = jnp.concatenate([vals[n].astype(F32).reshape(-1) for n, _ in _SMALL])
    flat = jnp.pad(flat, (0, SMALL_ROWS * D_MODEL - flat.shape[0]))
    return flat.reshape(SMALL_ROWS, D_MODEL)


def _unpack_small(buf):
    flat = buf.reshape(-1)
    out = {}
    off = 0
    for n, shp in _SMALL:
        k = int(np.prod(shp))
        out[n] = flat[off:off + k].reshape(shp)
        off += k
    return out


def _mesh_pos():
    return lax.axis_index("x"), lax.axis_index("y"), lax.axis_index("c")


def _other_chips(x, y):
    return [(1 - x, y), (x, 1 - y), (1 - x, 1 - y)]


def _remote(src, dst, send_sem, recv_sem, to):
    return pltpu.make_async_remote_copy(src_ref=src, dst_ref=dst, send_sem=send_sem, recv_sem=recv_sem,
                                        device_id=to, device_id_type=MESH)


def _own_slot(own, n, at):
    return lax.dynamic_update_slice(lax.empty((n,) + own.shape, own.dtype), own[None], (at,) + (0,) * own.ndim)


def _allgather_weights(stacks, wc_stack):
    names = list(stacks)
    n = len(names)

    def body(*refs):
        ins, outs = refs[:n + 1], refs[n + 1:2 * n + 2]
        send_sems, recv_sems = refs[2 * n + 2:]
        x, y, c = _mesh_pos()
        me = 2 * x + y
        sibling = (x, y, 1 - c)
        chips = _other_chips(x, y)

        def half(ref, chip, hc):
            hr = ref.shape[1] // 2
            return ref.at[chip, pl.ds(hc * hr, hr), :]

        first = []
        for k in range(n):
            first += [_remote(half(ins[k], me, c), half(outs[k], me, c), send_sems.at[6 * k + j], recv_sems.at[6 * k + j], (cx, cy, c))
                      for j, (cx, cy) in enumerate(chips)]
        first += [_remote(ins[n].at[me], outs[n].at[me], send_sems.at[6 * n + j], recv_sems.at[6 * n + j], (cx, cy, c))
                  for j, (cx, cy) in enumerate(chips)]
        for cp in first:
            cp.start()
        passed = []
        for k in range(n):
            for j, (cx, cy) in enumerate(chips):
                landed = half(outs[k], 2 * cx + cy, c)
                _remote(landed, landed, send_sems.at[6 * k + j], recv_sems.at[6 * k + j], (x, y, c)).wait_recv()
                passed.append(_remote(landed, landed, send_sems.at[6 * k + 3 + j], recv_sems.at[6 * k + 3 + j], sibling))
                passed[-1].start()
        for k in range(n):
            for j, (cx, cy) in enumerate(chips):
                theirs = half(outs[k], 2 * cx + cy, 1 - c)
                _remote(theirs, theirs, send_sems.at[6 * k + 3 + j], recv_sems.at[6 * k + 3 + j], (x, y, c)).wait_recv()
        for j, (cx, cy) in enumerate(chips):
            slot = outs[n].at[2 * cx + cy]
            _remote(slot, slot, send_sems.at[6 * n + j], recv_sems.at[6 * n + j], (x, y, c)).wait_recv()
        for cp in first + passed:
            cp.wait_send()

    arrays = [stacks[k] for k in names] + [wc_stack]
    outs = pl.pallas_call(
        body, name="allgather_weights",
        in_specs=[HBM] * (n + 1), out_specs=[HBM] * (n + 1), input_output_aliases={k: k for k in range(n + 1)},
        out_shape=[_sds(a.shape, a.dtype) for a in arrays],
        scratch_shapes=[pltpu.SemaphoreType.DMA((6 * n + 3,)), pltpu.SemaphoreType.DMA((6 * n + 3,))],
    )(*arrays)
    return dict(zip(names, outs[:n])), outs[n]


_KIND = {"w_in": "stack", "w_pa": "col", "w_pb": "col", "w_up": "col", "w_out": "row", "w_down": "row"}


def _half_view(ref, kind, h):
    if kind == "stack":
        k = ref.shape[1] // 2
        return ref.at[:, pl.ds(h * k, k), :]
    if kind == "col":
        k = ref.shape[0] // 2
        return ref.at[pl.ds(h * k, k), :]
    k = ref.shape[1] // 2
    return ref.at[:, pl.ds(h * k, k)]


def _shard_view(ref, kind, i):
    if kind == "stack":
        return ref.at[i]
    if kind == "col":
        k = ref.shape[1] // N_CHIPS
        return ref.at[:, pl.ds(i * k, k)]
    k = ref.shape[0] // N_CHIPS
    return ref.at[pl.ds(i * k, k), :]


def _region_view(ref, kind, h):
    if kind == "row":
        k = ref.shape[1] // 2
        return ref.at[:, pl.ds(h * k, k)]
    k = ref.shape[0] // 2
    return ref.at[pl.ds(h * k, k), :]


def _half_shape(shape, kind):
    if kind == "stack":
        return (shape[0], shape[1] // 2, shape[2])
    return (shape[0] // 2, shape[1]) if kind == "col" else (shape[0], shape[1] // 2)


def _part_shape(half_shape, kind):
    if kind == "stack":
        return tuple(half_shape[1:])
    k, w = half_shape
    return (k, w // N_CHIPS) if kind == "col" else (k // N_CHIPS, w)


def _pair_exchange(parts, tag):
    names = list(parts)
    n = len(names)

    def body(*refs):
        p, q = refs[:n], refs[n:2 * n]
        send_sems, recv_sems = refs[2 * n:]
        x, y, c = _mesh_pos()

        def copy(k, h):
            return _remote(_half_view(p[k], _KIND[names[k]], h), q[k], send_sems.at[k], recv_sems.at[k], (x, y, 1 - c))

        for hc in range(2):
            @pl.when(c == hc)
            def _():
                for k in range(n):
                    copy(k, 1 - hc).start()

        for k in range(n):
            copy(k, 0).wait()

    outs = pl.pallas_call(
        body, name="grad_pair_exchange_" + tag, in_specs=[HBM] * n, out_specs=[HBM] * n,
        out_shape=[_sds(_half_shape(parts[k].shape, _KIND[k]), parts[k].dtype) for k in names],
        scratch_shapes=[pltpu.SemaphoreType.DMA((n,)), pltpu.SemaphoreType.DMA((n,))],
    )(*[parts[k] for k in names])
    return dict(zip(names, outs))


def _half_blocks(shape, kind):
    if kind == "stack":
        _, k, w = shape
        tr = 256
        nb = k // 2 // tr
        return (N_CHIPS, nb), (1, tr, w), (lambda i, r, s: (i, r, 0)), (lambda i, r, s: (i, s[1] * nb + r, 0))
    k, w = shape
    if kind == "col":
        tr = 256 if w <= 2 * D_MODEL else 128
        nb = k // 2 // tr
        return (nb,), (tr, w), (lambda r, s: (r, 0)), (lambda r, s: (s[1] * nb + r, 0))
    tr = k // N_CHIPS
    return (N_CHIPS,), (tr, w // 2), (lambda r, s: (r, 0)), (lambda r, s: (r, s[1]))


def _pair_add(part, from_sibling, name, pos):
    kind = _KIND[name]
    grid, block, half_map, full_map = _half_blocks(part.shape, kind)

    def body(s_ref, p_ref, q_ref, o_ref):
        o_ref[...] = (p_ref[...].astype(F32) + q_ref[...].astype(F32)).astype(BF16)

    return pl.pallas_call(
        body, name="grad_pair_add_" + name,
        grid_spec=pltpu.PrefetchScalarGridSpec(
            num_scalar_prefetch=1, grid=grid,
            in_specs=[pl.BlockSpec(block, full_map), pl.BlockSpec(block, half_map)],
            out_specs=pl.BlockSpec(block, half_map)),
        out_shape=_sds(from_sibling.shape, BF16),
        compiler_params=_cp(("arbitrary",) * len(grid)),
    )(pos, part, from_sibling)


_DATAFLOW = pltpu.SideEffectType.DATAFLOW_SIDE_EFFECTING
_TOKEN = (SUBLANES, LANES)


def _chip_exchange_start(sums, tag):
    names = list(sums)
    n = len(names)
    lands = [lax.empty((3,) + _part_shape(sums[k].shape, _KIND[k]), sums[k].dtype) for k in names]

    def body(*refs):
        s, r = refs[:n], refs[n:2 * n]
        send_sems, recv_sems = refs[2 * n:2 * n + 2]
        token = refs[-1]
        x, y, c = _mesh_pos()
        me = 2 * x + y
        for i in range(N_CHIPS):
            xi, yi = i // 2, i % 2
            j = jnp.where(xi != x, jnp.where(yi != y, 2, 0), 1)

            @pl.when(i != me)
            def _():
                for k in range(n):
                    _remote(_shard_view(s[k], _KIND[names[k]], i), r[k].at[j], send_sems.at[3 * k + j], recv_sems.at[3 * k + j],
                            (xi, yi, c)).start()

        token[...] = jnp.zeros_like(token)

    arrays = [sums[k] for k in names] + lands
    outs = pl.pallas_call(
        body, name="grad_chip_exchange_start_" + tag,
        in_specs=[HBM] * (2 * n), out_specs=[SEM, SEM] + [HBM] * (2 * n) + [pl.BlockSpec(memory_space=pltpu.VMEM)],
        out_shape=[pltpu.SemaphoreType.DMA((3 * n,)), pltpu.SemaphoreType.DMA((3 * n,))]
        + [pltpu.HBM(a.shape, a.dtype) for a in arrays] + [_sds(_TOKEN, F32)],
        input_output_aliases={k: 2 + k for k in range(2 * n)},
        compiler_params=pltpu.CompilerParams(has_side_effects=_DATAFLOW),
    )(*[pltpu.with_memory_space_constraint(a, pltpu.HBM) for a in arrays])
    return names, outs[0], outs[1], list(outs[2:2 + n]), list(outs[2 + n:2 + 2 * n]), outs[-1]


def _chip_exchange_wait(started, after, tag):
    names, send_sems, recv_sems, sums, lands, _ = started
    n = len(names)

    def body(*refs):
        s, r = refs[:n], refs[n:2 * n]
        send, recv = refs[2 * n], refs[2 * n + 1]
        x, y, c = _mesh_pos()
        for k in range(n):
            for j in range(3):
                cp = _remote(_shard_view(s[k], _KIND[names[k]], 0), r[k].at[j], send.at[3 * k + j], recv.at[3 * k + j], (x, y, c))
                cp.wait_send()
                cp.wait_recv()

    outs = pl.pallas_call(
        body, name="grad_chip_exchange_wait_" + tag,
        in_specs=[HBM] * (2 * n) + [SEM, SEM, ANY], out_specs=[HBM] * (2 * n),
        out_shape=[pltpu.HBM(a.shape, a.dtype) for a in sums + lands],
        input_output_aliases={k: k for k in range(2 * n)},
        compiler_params=pltpu.CompilerParams(has_side_effects=_DATAFLOW),
    )(*sums, *lands, send_sems, recv_sems, after)
    return dict(zip(names, outs[n:]))


def _allgather_start(stacks, after):
    names = list(stacks)
    n = len(names)

    def body(*refs):
        st = refs[:n]
        send_sems, recv_sems = refs[n + 1:n + 3]
        token = refs[-1]
        x, y, c = _mesh_pos()
        me = 2 * x + y
        for k in range(n):
            hr = st[k].shape[1] // 2
            mine = st[k].at[me, pl.ds(c * hr, hr), :]
            for j, (cx, cy) in enumerate(_other_chips(x, y)):
                _remote(mine, mine, send_sems.at[3 * k + j], recv_sems.at[3 * k + j], (cx, cy, c)).start()
        token[...] = jnp.zeros_like(token)

    arrays = [stacks[k] for k in names]
    outs = pl.pallas_call(
        body, name="allgather_start",
        in_specs=[HBM] * n + [ANY], out_specs=[SEM, SEM] + [HBM] * n + [pl.BlockSpec(memory_space=pltpu.VMEM)],
        out_shape=[pltpu.SemaphoreType.DMA((3 * n,)), pltpu.SemaphoreType.DMA((3 * n,))]
        + [pltpu.HBM(a.shape, a.dtype) for a in arrays] + [_sds(_TOKEN, F32)],
        input_output_aliases={k: 2 + k for k in range(n)},
        compiler_params=pltpu.CompilerParams(has_side_effects=_DATAFLOW),
    )(*[pltpu.with_memory_space_constraint(a, pltpu.HBM) for a in arrays], after)
    return names, outs[0], outs[1], list(outs[2:2 + n]), outs[-1]


def _allgather_wait(started, after):
    names, send_sems, recv_sems, stacks, _ = started
    n = len(names)

    def body(*refs):
        st = refs[:n]
        send, recv = refs[n], refs[n + 1]
        x, y, c = _mesh_pos()
        for k in range(n):
            hr = st[k].shape[1] // 2
            for j in range(3):
                slot = st[k].at[0, pl.ds(0, hr), :]
                cp = _remote(slot, slot, send.at[3 * k + j], recv.at[3 * k + j], (x, y, c))
                cp.wait_send()
                cp.wait_recv()

    outs = pl.pallas_call(
        body, name="allgather_wait",
        in_specs=[HBM] * n + [SEM, SEM, ANY], out_specs=[HBM] * n,
        out_shape=[pltpu.HBM(a.shape, a.dtype) for a in stacks],
        input_output_aliases={k: k for k in range(n)},
        compiler_params=pltpu.CompilerParams(has_side_effects=_DATAFLOW),
    )(*stacks, send_sems, recv_sems, after)
    return dict(zip(names, outs))


def _allgather_forward(stacks):
    names = list(stacks)
    n = len(names)

    def body(*refs):
        ins, outs = refs[:n], refs[n:2 * n]
        send_sems, recv_sems = refs[2 * n:]
        x, y, c = _mesh_pos()
        copies = []
        for k in range(n):
            hr = ins[k].shape[1] // 2
            for j, (cx, cy) in enumerate(_other_chips(x, y)):
                chip = 2 * cx + cy
                copies.append(_remote(ins[k].at[chip, pl.ds(c * hr, hr), :], outs[k].at[chip, pl.ds(c * hr, hr), :],
                                      send_sems.at[3 * k + j], recv_sems.at[3 * k + j], (x, y, 1 - c)))
        for cp in copies:
            cp.start()
        for cp in copies:
            cp.wait()

    arrays = [stacks[k] for k in names]
    outs = pl.pallas_call(
        body, name="allgather_forward", in_specs=[HBM] * n, out_specs=[HBM] * n,
        input_output_aliases={k: k for k in range(n)},
        out_shape=[_sds(a.shape, a.dtype) for a in arrays],
        scratch_shapes=[pltpu.SemaphoreType.DMA((3 * n,)), pltpu.SemaphoreType.DMA((3 * n,))],
    )(*arrays)
    return dict(zip(names, outs))


def _owner_sum(part, from_sibling, from_chips, name, pos, shard_shape):
    kind = _KIND[name]
    _, pk, pw = from_chips.shape
    if kind == "row":
        tr, nb = pk, 1
        p_spec = pl.BlockSpec((tr, pw), lambda r, s: (s[0], s[1]))
        q_spec = pl.BlockSpec((tr, pw), lambda r, s: (s[0], 0))
        o_spec = pl.BlockSpec((tr, pw), lambda r, s: (0, s[1]))
    else:
        tr = 256
        nb = pk // tr
        if kind == "stack":
            p_spec = pl.BlockSpec((None, tr, pw), lambda r, s: (s[0], s[1] * nb + r, 0))
            q_spec = pl.BlockSpec((None, tr, pw), lambda r, s: (s[0], r, 0))
        else:
            p_spec = pl.BlockSpec((tr, pw), lambda r, s: (s[1] * nb + r, s[0]))
            q_spec = pl.BlockSpec((tr, pw), lambda r, s: (r, s[0]))
        o_spec = pl.BlockSpec((tr, pw), lambda r, s: (s[1] * nb + r, 0))

    def body(s_ref, p_ref, q_ref, r_ref, o_ref):
        acc = p_ref[...].astype(F32) + q_ref[...].astype(F32)
        for j in range(3):
            acc = acc + r_ref[j].astype(F32)
        o_ref[...] = acc

    return pl.pallas_call(
        body, name="grad_owner_sum_" + name,
        grid_spec=pltpu.PrefetchScalarGridSpec(
            num_scalar_prefetch=1, grid=(nb,),
            in_specs=[p_spec, q_spec, pl.BlockSpec((3, tr, pw), lambda r, s: (0, r, 0))],
            out_specs=o_spec),
        out_shape=_sds(shard_shape, F32),
        compiler_params=_cp(("arbitrary",), 32),
    )(pos, part, from_sibling, from_chips)


def _pair_share(shards):
    names = list(shards)
    n = len(names)

    def body(*refs):
        g_in, g_out = refs[:n], refs[n:2 * n]
        send_sems, recv_sems = refs[2 * n:]
        x, y, c = _mesh_pos()

        def copy(k, h):
            kind = _KIND[names[k]]
            return _remote(_region_view(g_in[k], kind, h), _region_view(g_out[k], kind, h), send_sems.at[k], recv_sems.at[k], (x, y, 1 - c))

        for hc in range(2):
            @pl.when(c == hc)
            def _():
                for k in range(n):
                    copy(k, hc).start()

        for k in range(n):
            copy(k, 0).wait()

    outs = pl.pallas_call(
        body, name="grad_pair_share", in_specs=[HBM] * n, out_specs=[HBM] * n, input_output_aliases={k: k for k in range(n)},
        out_shape=[_sds(shards[k].shape, shards[k].dtype) for k in names],
        scratch_shapes=[pltpu.SemaphoreType.DMA((n,)), pltpu.SemaphoreType.DMA((n,))],
    )(*[shards[k] for k in names])
    return dict(zip(names, outs))


def _allgather_small(block):
    m_per = block.shape[0]

    def body(x_ref, out_ref, send_sems, recv_sems, local_sem):
        x, y, c = _mesh_pos()
        me, sibling = (x, y, c), (x, y, 1 - c)
        chips = _other_chips(x, y)

        def rows(px, py, pc):
            return out_ref.at[4 * px + 2 * py + pc]

        def copy(k, block_of, to, src=None):
            return _remote(rows(*block_of) if src is None else src, rows(*block_of), send_sems.at[k], recv_sems.at[k], to)

        mine = pltpu.make_async_copy(x_ref, rows(*me), local_sem)
        mine.start()
        first = [copy(0, me, sibling, src=x_ref)]
        first += [copy(1 + j, me, (*chip, c), src=x_ref) for j, chip in enumerate(chips)]
        for cp in first:
            cp.start()
        passed = [copy(4 + j, (*chip, c), sibling) for j, chip in enumerate(chips)]
        for j, chip in enumerate(chips):
            copy(1 + j, (*chip, c), me).wait_recv()
            passed[j].start()
        copy(0, sibling, me).wait_recv()
        for j, chip in enumerate(chips):
            copy(4 + j, (*chip, 1 - c), me).wait_recv()
        for cp in first + passed:
            cp.wait_send()
        mine.wait()

    return pl.pallas_call(
        body, name="allgather_small",
        in_specs=[pl.BlockSpec(memory_space=pltpu.VMEM)], out_specs=pl.BlockSpec(memory_space=pltpu.VMEM),
        out_shape=_sds((N_DEV, m_per, D_MODEL), block.dtype),
        scratch_shapes=[pltpu.SemaphoreType.DMA((7,)), pltpu.SemaphoreType.DMA((7,)), pltpu.SemaphoreType.DMA],
    )(block)


def _adam_math(w, g, m, v):
    m = ADAM_B1 * m + (1.0 - ADAM_B1) * g
    v = ADAM_B2 * v + (1.0 - ADAM_B2) * (g * g)
    m_hat = m / (1.0 - ADAM_B1 ** ADAM_STEP)
    v_hat = v / (1.0 - ADAM_B2 ** ADAM_STEP)
    delta = -ADAM_LR * (m_hat / (jnp.sqrt(v_hat) + ADAM_EPS) + ADAM_WD * w)
    return delta, m, v


def _adamw(w, g, m, v, name):
    rows, cols = w.shape
    tr = rows
    for cand in (256, 128, 64, 32, 16, 8):
        if rows % cand == 0 and rows > cand:
            tr = cand
            break

    def body(w_ref, g_ref, m_ref, v_ref, d_ref, nm_ref, nv_ref):
        d, nm, nv = _adam_math(w_ref[...], g_ref[...], m_ref[...], v_ref[...])
        d_ref[...] = d
        nm_ref[...] = nm
        nv_ref[...] = nv

    spec = pl.BlockSpec((tr, cols), lambda i: (i, 0))
    return pl.pallas_call(
        body, name=name, grid=(rows // tr,), in_specs=[spec] * 4, out_specs=[spec] * 3,
        out_shape=[_sds(w.shape, F32)] * 3, compiler_params=_cp(("arbitrary",)),
    )(w, g, m, v)


def _small_sum_adamw(gathered, w, m, v):
    def body(a_ref, w_ref, m_ref, v_ref, g_ref, d_ref, nm_ref, nv_ref):
        g = a_ref[0]
        for k in range(1, N_DEV):
            g = g + a_ref[k]
        g_ref[...] = g
        d, nm, nv = _adam_math(w_ref[...], g, m_ref[...], v_ref[...])
        d_ref[...] = d
        nm_ref[...] = nm
        nv_ref[...] = nv

    return pl.pallas_call(
        body, name="small_sum_adamw", out_shape=[_sds(w.shape, F32)] * 4,
    )(gathered, w, m, v)


_NAMES = ("g_mix", "w_in", "g_sgu", "w_s", "b_s", "sinks", "rel_bias", "w_pa", "w_pb", "w_out",
          "g_ffn", "w_up", "w_conv", "b_conv", "w_down", "g_final")

def kernel(x, g_mix, w_in, g_sgu, w_s, b_s, sinks, rel_bias, w_pa, w_pb, w_out, g_ffn, w_up, w_conv, b_conv, w_down, g_final, loss_target, m_g_mix, m_w_in, m_g_sgu, m_w_s, m_b_s, m_sinks, m_rel_bias, m_w_pa, m_w_pb, m_w_out, m_g_ffn, m_w_up, m_w_conv, m_b_conv, m_w_down, m_g_final, v_g_mix, v_w_in, v_g_sgu, v_w_s, v_b_s, v_sinks, v_rel_bias, v_w_pa, v_w_pb, v_w_out, v_g_ffn, v_w_up, v_w_conv, v_b_conv, v_w_down, v_g_final):
    w = dict(g_mix=g_mix, w_in=w_in, g_sgu=g_sgu, w_s=w_s, b_s=b_s, sinks=sinks, rel_bias=rel_bias, w_pa=w_pa, w_pb=w_pb,
             w_out=w_out, g_ffn=g_ffn, w_up=w_up, w_conv=w_conv, b_conv=b_conv, w_down=w_down, g_final=g_final)
    m = dict(g_mix=m_g_mix, w_in=m_w_in, g_sgu=m_g_sgu, w_s=m_w_s, b_s=m_b_s, sinks=m_sinks, rel_bias=m_rel_bias, w_pa=m_w_pa,
             w_pb=m_w_pb, w_out=m_w_out, g_ffn=m_g_ffn, w_up=m_w_up, w_conv=m_w_conv, b_conv=m_b_conv, w_down=m_w_down,
             g_final=m_g_final)
    v = dict(g_mix=v_g_mix, w_in=v_w_in, g_sgu=v_g_sgu, w_s=v_w_s, b_s=v_b_s, sinks=v_sinks, rel_bias=v_rel_bias, w_pa=v_w_pa,
             w_pb=v_w_pb, w_out=v_w_out, g_ffn=v_g_ffn, w_up=v_w_up, w_conv=v_w_conv, b_conv=v_b_conv, w_down=v_w_down,
             g_final=v_g_final)
    xi, yi, ci = _mesh_pos()
    me = 2 * xi + yi

    shard = {n: w[n][0] for n in _BIG}
    shard_shapes = {n: shard[n].shape for n in _BIG}
    wc_shard = w["w_conv"][0]
    wc_pad = jnp.pad(wc_shard, ((0, 5), (0, 0)))
    own = {n: _own_slot(shard[n].astype(BF16), N_CHIPS, me) for n in _BIG}
    stacks, wc_all = _allgather_weights({"w_in": own["w_in"]}, _own_slot(wc_pad, N_CHIPS, me))
    late_gather = _allgather_start({n: own[n] for n in _BIG[1:]}, stacks["w_in"])
    w_conv_full = jnp.concatenate([wc_all[i, :3] for i in range(N_CHIPS)], axis=1)
    w_in_full = stacks["w_in"].transpose(1, 0, 2).reshape(D_MODEL, -1)
    w_a = w_in_full[:, :A_DIM]
    w_b = w_in_full[:, A_DIM:A_DIM + B_DIM]
    w_g = w_in_full[:, A_DIM + B_DIM:]
    pos = jnp.stack([me, ci])

    def late_weights(done):
        st = _allgather_forward(_allgather_wait(late_gather, done))
        return st["w_pa"], st["w_pb"], st["w_out"].reshape(D_MODEL, D_MODEL), st["w_up"], st["w_down"].reshape(D_FF, D_MODEL)

    from_sibling, exchanges = {}, []

    def on_grads(group, parts):
        sib = _pair_exchange(parts, group)
        from_sibling.update(sib)
        started = _chip_exchange_start({n: _pair_add(parts[n], sib[n], n, pos) for n in parts}, group)
        exchanges.append((group, started))
        return started[-1]

    loss, grad_x, small, big = _local_step(
        x, loss_target, w["g_mix"], w["g_sgu"], w["w_s"][0], w["b_s"][0], w["sinks"], w["rel_bias"], w["g_ffn"],
        w["b_conv"], w["g_final"], w_g, w_a, w_b, w_conv_full, late_weights, on_grads, late_gather[-1])

    small["loss"] = loss
    all_small = _allgather_small(_pack_small(small))
    sw = {n: (jnp.zeros((1, 1), F32) if n in ("loss", "w_conv") else w[n]) for n, _ in _SMALL}
    sm = {n: (jnp.zeros((1, 1), F32) if n in ("loss", "w_conv") else m[n]) for n, _ in _SMALL}
    sv = {n: (jnp.zeros((1, 1), F32) if n in ("loss", "w_conv") else v[n]) for n, _ in _SMALL}
    for d in (sw, sm, sv):
        d["w_conv"] = jnp.zeros((3, 2 * D_FF), F32)
    s_g, s_d, s_m, s_v = [_unpack_small(a) for a in _small_sum_adamw(all_small, _pack_small(sw), _pack_small(sm), _pack_small(sv))]

    from_chips = {}
    for group, started in exchanges:
        from_chips.update(_chip_exchange_wait(started, grad_x, group))
    g_big = _pair_share({n: _owner_sum(big[n], from_sibling[n], from_chips[n], n, pos, shard_shapes[n]) for n in _BIG})

    grads, deltas, new_m, new_v = {}, {}, {}, {}
    for n in _BIG:
        if n == "w_in":
            gt = g_big[n].T
            d, nm, nv = _adamw(shard[n].T, gt, m[n][0].T, v[n][0].T, "adamw_" + n)
            grads[n], deltas[n], new_m[n], new_v[n] = gt.T[None], d.T[None], nm.T[None], nv.T[None]
            continue
        d, nm, nv = _adamw(shard[n], g_big[n], m[n][0], v[n][0], "adamw_" + n)
        grads[n], deltas[n], new_m[n], new_v[n] = g_big[n][None], d[None], nm[None], nv[None]
    wcols = wc_shard.shape[1]
    g_wc = lax.dynamic_slice(s_g["w_conv"], (0, me * wcols), (3, wcols))
    d, nm, nv = _adamw(wc_shard, g_wc, m["w_conv"][0], v["w_conv"][0], "adamw_w_conv")
    grads["w_conv"], deltas["w_conv"], new_m["w_conv"], new_v["w_conv"] = g_wc[None], d[None], nm[None], nv[None]
    for n, _ in _SMALL:
        if n in ("loss", "w_conv"):
            continue
        shp = w[n].shape
        grads[n], deltas[n], new_m[n], new_v[n] = (s_g[n].reshape(shp), s_d[n].reshape(shp), s_m[n].reshape(shp),
                                                    s_v[n].reshape(shp))

    return (s_g["loss"].reshape(()), grad_x, *[grads[n] for n in _NAMES], *[deltas[n] for n in _NAMES],
            *[new_m[n] for n in _NAMES], *[new_v[n] for n in _NAMES])
```

```python
import functools

import numpy as np
import jax
import jax.numpy as jnp
from jax import lax
from jax.experimental import pallas as pl
from jax.experimental.pallas import tpu as pltpu

F32 = jnp.float32
BF16 = jnp.bfloat16

D_MODEL = 1024
CHUNK = 128
A_GROUPS = 4
A_WIDTH = 512
N_HEADS = 8
HEAD_DIM = 64
Q_DIM = 512
KV_DIM = 128
N_BUCKETS = 32
MAX_DISTANCE = 128
D_FF = 2816
EPS = 1e-6
NEG_INF = -1e30
G_DIM = 2 * D_MODEL
A_DIM = 2 * A_WIDTH
B_DIM = Q_DIM + 2 * KV_DIM
LANES = 128
SUBLANES = 8
ROW_TILE = 512
WIDE_ROW_TILE = 256
GRAD_ROW_TILE = 512
BF16_ROWS = 16
N_CHIPS = 4
N_DEV = 8

ADAM_LR = 0.001
ADAM_B1 = 0.9
ADAM_B2 = 0.999
ADAM_EPS = 1e-08
ADAM_WD = 0.01
ADAM_STEP = 10

MESH = pl.DeviceIdType.MESH
_GELU_C = 0.7978845608028654
_GELU_A = 0.044715


def _cp(sem=None, vmem_mb=None):
    kw = {}
    if sem is not None:
        kw["dimension_semantics"] = sem
    if vmem_mb is not None:
        kw["vmem_limit_bytes"] = vmem_mb << 20
    return pltpu.CompilerParams(**kw)


def _dot(a, b):
    return jnp.dot(a, b, preferred_element_type=F32)


def _dot_nt(a, b):
    return lax.dot_general(a, b, (((1,), (1,)), ((), ())), preferred_element_type=F32)


def _dot_tn(a, b):
    return lax.dot_general(a, b, (((0,), (0,)), ((), ())), preferred_element_type=F32)


def _rms_r(x):
    return lax.rsqrt(jnp.mean(x * x, axis=-1, keepdims=True) + EPS)


def _rms_bwd(dh, n, r, g):
    dn = dh * g
    return r * (dn - n * jnp.mean(dn * n, axis=-1, keepdims=True))


def _gelu(x):
    t = jnp.tanh(_GELU_C * (x + _GELU_A * (x * x * x)))
    return 0.5 * x * (1.0 + t), t


def _gelu_grad(x, t):
    return 0.5 * (1.0 + t) + 0.5 * x * (1.0 - t * t) * (_GELU_C * (1.0 + 3.0 * _GELU_A * x * x))


def _sigmoid(x):
    return 1.0 / (1.0 + jnp.exp(-x))


def _row(tm, w):
    return pl.BlockSpec((tm, w), lambda i: (i, 0))


def _full(shape):
    nd = len(shape)
    return pl.BlockSpec(tuple(shape), lambda *_: (0,) * nd)


def _resident(shape):
    nd = len(shape)
    return pl.BlockSpec(tuple(shape), lambda *_: (0,) * nd, pipeline_mode=pl.Buffered(1))


def _sds(shape, dtype):
    return jax.ShapeDtypeStruct(tuple(shape), dtype)


HBM = pl.BlockSpec(memory_space=pltpu.HBM)
ANY = pl.BlockSpec(memory_space=pl.ANY)
SEM = pl.BlockSpec(memory_space=pltpu.SEMAPHORE)


def _band_buckets():
    i = np.arange(CHUNK)[:, None]
    j = np.arange(2 * CHUNK)[None, :]
    dist = i + CHUNK - j
    valid = (dist >= 0) & (dist < CHUNK)
    d = np.clip(dist, 0, None)
    max_exact = N_BUCKETS // 2
    large = max_exact + (np.log(np.maximum(d, 1) / max_exact) / np.log(MAX_DISTANCE / max_exact)
                         * (N_BUCKETS - max_exact)).astype(np.int32)
    large = np.minimum(large, N_BUCKETS - 1)
    buckets = np.where(d < max_exact, d, large).astype(np.int32)
    return np.where(valid, buckets, -1).astype(np.int32)


def _inproj(x2, g_mix, w_g, w_a, w_b, tm, after=None):
    T = x2.shape[0]
    order = [] if after is None else [after]

    def body(*refs):
        x_ref, g_ref, wg_ref, wa_ref, wb_ref = refs[:5]
        pg_ref, pa_ref, pb_ref, h_ref = refs[5 + len(order):]
        x = x_ref[...]
        h = (x * _rms_r(x) * g_ref[...]).astype(BF16)
        h_ref[...] = h
        pg_ref[...] = _dot(h, wg_ref[...]).astype(BF16)
        pa_ref[...] = _dot(h, wa_ref[...]).astype(BF16)
        pb_ref[...] = _dot(h, wb_ref[...]).astype(BF16)

    return pl.pallas_call(
        body, name="inproj", grid=(T // tm,),
        in_specs=[_row(tm, D_MODEL), _full(g_mix.shape), _resident(w_g.shape), _resident(w_a.shape), _resident(w_b.shape)]
        + [ANY] * len(order),
        out_specs=[_row(tm, G_DIM), _row(tm, A_DIM), _row(tm, B_DIM), _row(tm, D_MODEL)],
        out_shape=[_sds((T, G_DIM), BF16), _sds((T, A_DIM), BF16), _sds((T, B_DIM), BF16), _sds((T, D_MODEL), BF16)],
        compiler_params=_cp(("arbitrary",), 48),
    )(x2, g_mix, w_g, w_a, w_b, *order)


def _sgu_parts(p, g):
    pu = p[:, :A_WIDTH]
    pv = p[:, A_WIDTH:]
    u, tu = _gelu(pu)
    vv, tv = _gelu(pv)
    rv = _rms_r(vv)
    vn = (vv * rv * g).astype(BF16)
    return pu, pv, u, tu, vv, tv, rv, vn


def _tril():
    r = lax.broadcasted_iota(jnp.int32, (CHUNK, CHUNK), 0)
    c = lax.broadcasted_iota(jnp.int32, (CHUNK, CHUNK), 1)
    return r >= c


def _sgu_fwd(proj_a, g_sgu, w_s, b_st, tm):
    T = proj_a.shape[0]

    def body(p_ref, g_ref, ws_ref, bs_ref, y_ref):
        tril = _tril()
        _, _, u, _, _, _, _, vn = _sgu_parts(p_ref[...].astype(F32), g_ref[...])
        for gi in range(A_GROUPS):
            wm = jnp.where(tril, ws_ref[gi], 0.0).astype(BF16)
            bcol = bs_ref[:, gi:gi + 1]
            cs = slice(gi * CHUNK, (gi + 1) * CHUNK)
            for c in range(tm // CHUNK):
                rs = slice(c * CHUNK, (c + 1) * CHUNK)
                s = _dot(wm, vn[rs, cs]) + bcol
                y_ref[rs, cs] = (u[rs, cs] * s).astype(BF16)

    return pl.pallas_call(
        body, name="sgu_fwd", grid=(T // tm,),
        in_specs=[_row(tm, A_DIM), _full(g_sgu.shape), _full(w_s.shape), _full(b_st.shape)],
        out_specs=_row(tm, A_WIDTH), out_shape=_sds((T, A_WIDTH), BF16),
        compiler_params=_cp(("arbitrary",)),
    )(proj_a, g_sgu, w_s, b_st)


HEAD_ROWS = N_HEADS * CHUNK


def _head_rows(h):
    return slice(h * CHUNK, (h + 1) * CHUNK)


def _attn_setup(bias_scr, sink_scr, kvar_scr, qkv_ref, bk_ref, rel_ref, sink_ref):
    bk = bk_ref[...]
    for h in range(N_HEADS):
        acc = jnp.full((CHUNK, 2 * CHUNK), NEG_INF, F32)
        for b in range(N_BUCKETS):
            acc = jnp.where(bk == b, rel_ref[b, h], acc)
        bias_scr[_head_rows(h), :] = acc
        sink_scr[_head_rows(h), :] = jnp.full((CHUNK, LANES), sink_ref[0, h], F32)
    seq = qkv_ref.shape[0]
    rows_per = 2 * CHUNK
    for is_v in range(2):
        c0 = Q_DIM + is_v * KV_DIM
        for r in range(seq // rows_per):
            rs = slice(r * rows_per, (r + 1) * rows_per)
            a = qkv_ref[rs, c0:c0 + KV_DIM].astype(F32)
            lane = lax.broadcasted_iota(jnp.int32, a.shape, 1)
            lo = jnp.where(lane < HEAD_DIM, a, 0.0)
            hi = jnp.where(lane >= HEAD_DIM, a, 0.0)
            kvar_scr[4 * is_v + 0, rs, :] = lo.astype(BF16)
            kvar_scr[4 * is_v + 1, rs, :] = pltpu.roll(lo, HEAD_DIM, 1).astype(BF16)
            kvar_scr[4 * is_v + 2, rs, :] = pltpu.roll(hi, HEAD_DIM, 1).astype(BF16)
            kvar_scr[4 * is_v + 3, rs, :] = hi.astype(BF16)


def _rowsum(a, ones):
    hi = a.astype(BF16)
    lo = (a - hi.astype(F32)).astype(BF16)
    return _dot(hi, ones) + _dot(lo, ones)


def _both(a):
    return jnp.concatenate([a, a], axis=1)


def _attn_probs(qkv_ref, r0, n, kv, bias_scr, sink_scr, ones):
    s = jnp.concatenate([_dot_nt(qkv_ref[pl.ds(r0, CHUNK), (h // 2) * LANES:(h // 2 + 1) * LANES], kv[h // 4][h % 2])
                         for h in range(N_HEADS)], axis=0)
    s = s * (HEAD_DIM ** -0.5) + bias_scr[...]
    col = lax.broadcasted_iota(jnp.int32, s.shape, 1)
    s = jnp.where((col < CHUNK) & (n == 0), NEG_INF, s)
    sink = sink_scr[...]
    m = jnp.maximum(jnp.max(s, axis=-1, keepdims=True), sink)
    p = jnp.exp(s - _both(m))
    es = jnp.exp(sink - m)
    inv = 1.0 / (_rowsum(p, ones) + es)
    return p * _both(inv), es * inv


def _attn_block_inputs(kvar_scr, n):
    r0 = pl.multiple_of(n * CHUNK, CHUNK)
    rp = pl.multiple_of(jnp.maximum(n - 1, 0) * CHUNK, CHUNK)

    def both(idx):
        return jnp.concatenate([kvar_scr[idx, pl.ds(rp, CHUNK), :], kvar_scr[idx, pl.ds(r0, CHUNK), :]], axis=0)

    kv = ((both(0), both(1)), (both(2), both(3)))
    vv = ((both(4), both(5)), (both(6), both(7)))
    return r0, kv, vv


def _attn_fwd(proj_b, sinks, rel_bias, n_seq, seq):
    nb = seq // CHUNK
    bk = jnp.asarray(_band_buckets())

    def body(qkv_ref, bk_ref, rel_ref, sink_ref, o_ref, bias_scr, sink_scr, kvar_scr):
        _attn_setup(bias_scr, sink_scr, kvar_scr, qkv_ref, bk_ref, rel_ref, sink_ref)
        ones = jnp.ones((2 * CHUNK, LANES), BF16)

        def blk(n, carry):
            r0, kv, vv = _attn_block_inputs(kvar_scr, n)
            prob, _ = _attn_probs(qkv_ref, r0, n, kv, bias_scr, sink_scr, ones)
            pb = prob.astype(BF16)
            for pr in range(N_HEADS // 2):
                acc = _dot(pb[_head_rows(2 * pr)], vv[pr // 2][0]) + _dot(pb[_head_rows(2 * pr + 1)], vv[pr // 2][1])
                o_ref[pl.ds(r0, CHUNK), pr * LANES:(pr + 1) * LANES] = acc.astype(BF16)
            return carry

        lax.fori_loop(0, nb, blk, 0)

    smem = pl.BlockSpec(memory_space=pltpu.SMEM)
    return pl.pallas_call(
        body, name="attn_fwd", grid=(n_seq,),
        in_specs=[_row(seq, B_DIM), _full(bk.shape), smem, smem],
        out_specs=_row(seq, Q_DIM), out_shape=_sds((n_seq * seq, Q_DIM), BF16),
        scratch_shapes=[pltpu.VMEM((HEAD_ROWS, 2 * CHUNK), F32), pltpu.VMEM((HEAD_ROWS, LANES), F32),
                        pltpu.VMEM((8, seq, KV_DIM), BF16)],
        compiler_params=_cp(("arbitrary",), 40),
    )(proj_b, bk, rel_bias, sinks)


def _dot_stacked(a, w_ref):
    return jnp.concatenate([_dot(a, w_ref[i]) for i in range(N_CHIPS)], axis=1)


def _dot_nt_stacked(a, w_ref):
    w = w_ref.shape[2]
    acc = _dot_nt(a[:, :w], w_ref[0])
    for i in range(1, N_CHIPS):
        acc = acc + _dot_nt(a[:, i * w:(i + 1) * w], w_ref[i])
    return acc


def _merge_fwd(x2, y_a, y_b, proj_g, w_pa, w_pb, w_out, tm):
    T = x2.shape[0]

    def body(x_ref, ya_ref, yb_ref, g_ref, wpa_ref, wpb_ref, wo_ref, x1_ref, mg_ref):
        g = g_ref[...].astype(F32)
        pa = _dot_stacked(ya_ref[...], wpa_ref)
        pb = _dot_stacked(yb_ref[...], wpb_ref)
        merged = (_sigmoid(g[:, :D_MODEL]) * pa + _sigmoid(g[:, D_MODEL:]) * pb).astype(BF16)
        mg_ref[...] = merged
        x1_ref[...] = x_ref[...] + _dot(merged, wo_ref[...])

    return pl.pallas_call(
        body, name="merge_fwd", grid=(T // tm,),
        in_specs=[_row(tm, D_MODEL), _row(tm, A_WIDTH), _row(tm, Q_DIM), _row(tm, G_DIM),
                  _resident(w_pa.shape), _resident(w_pb.shape), _resident(w_out.shape)],
        out_specs=[_row(tm, D_MODEL), _row(tm, D_MODEL)],
        out_shape=[_sds((T, D_MODEL), F32), _sds((T, D_MODEL), BF16)],
        compiler_params=_cp(("arbitrary",), 40),
    )(x2, y_a, y_b, proj_g, w_pa, w_pb, w_out)


def _upproj(x1, g_ffn, w_up, w_conv, b_conv, tm, seq):
    T = x1.shape[0]
    cw = w_up.shape[2]
    tiles_per_seq = seq // tm

    def body(x_ref, g_ref, w_ref, wc_ref, bc_ref, u_ref, h_ref, gate_ref, val_ref, tail_scr):
        at_start = (pl.program_id(0) % tiles_per_seq) == 0
        x = x_ref[...]
        h = (x * _rms_r(x) * g_ref[...]).astype(BF16)
        h_ref[...] = h
        for i in range(N_CHIPS):
            cs = slice(i * cw, (i + 1) * cw)
            u = _dot(h, w_ref[i])
            u_ref[:, cs] = u.astype(BF16)
            hl = jnp.where(at_start, 0.0, tail_scr[SUBLANES - 2:SUBLANES, cs])
            tail_scr[:, cs] = u[tm - SUBLANES:]
            up = _conv_out((u, _shift_down(u, hl, 1), _shift_down(u, hl, 2)), wc_ref[:, cs], bc_ref[:, cs])
            out_ref = gate_ref if i < N_CHIPS // 2 else val_ref
            out_ref[:, (i % 2) * cw:(i % 2 + 1) * cw] = up.astype(BF16)

    return pl.pallas_call(
        body, name="upproj", grid=(T // tm,),
        in_specs=[_row(tm, D_MODEL), _full(g_ffn.shape), _resident(w_up.shape), _full(w_conv.shape), _full(b_conv.shape)],
        out_specs=[_row(tm, 2 * D_FF), _row(tm, D_MODEL), _row(tm, D_FF), _row(tm, D_FF)],
        out_shape=[_sds((T, 2 * D_FF), BF16), _sds((T, D_MODEL), BF16), _sds((T, D_FF), BF16), _sds((T, D_FF), BF16)],
        scratch_shapes=[pltpu.VMEM((SUBLANES, 2 * D_FF), F32)],
        compiler_params=_cp(("arbitrary",), 56),
    )(x1, g_ffn, w_up, w_conv, b_conv)


def _shift_down(u, halo, k):
    rolled = pltpu.roll(u, k, 0)
    head = rolled[:SUBLANES]
    row = lax.broadcasted_iota(jnp.int32, head.shape, 0)
    if k == 1:
        head = jnp.where(row == 0, halo[1:2], head)
    else:
        head = jnp.where(row == 0, halo[0:1], jnp.where(row == 1, halo[1:2], head))
    return jnp.concatenate([head, rolled[SUBLANES:]], axis=0)


def _shift_up(d, halo, k):
    tm = d.shape[0]
    rolled = pltpu.roll(d, tm - k, 0)
    tail = rolled[tm - SUBLANES:]
    row = lax.broadcasted_iota(jnp.int32, tail.shape, 0)
    if k == 1:
        tail = jnp.where(row == SUBLANES - 1, halo[0:1], tail)
    else:
        tail = jnp.where(row == SUBLANES - 2, halo[0:1], jnp.where(row == SUBLANES - 1, halo[1:2], tail))
    return jnp.concatenate([rolled[:tm - SUBLANES], tail], axis=0)


def _conv_out(taps, wc, bc):
    u, u1, u2 = taps
    return wc[0:1] * u2 + wc[1:2] * u1 + wc[2:3] * u + bc


def _ffn_down_loss(gate, val, x1, target, w_down, g_final, tm):
    T = x1.shape[0]
    half = D_FF // 2

    def body(gt_ref, vl_ref, x1_ref, t_ref, wd_ref, g_ref, dx2_ref, loss_ref, gg_ref):
        i = pl.program_id(0)
        acc = jnp.zeros((tm, D_MODEL), F32)
        for j in range(2):
            gc = slice(j * half, (j + 1) * half)
            gate = gt_ref[:, gc].astype(F32)
            act = (gate * _sigmoid(gate) * vl_ref[:, gc].astype(F32)).astype(BF16)
            acc = acc + _dot(act, wd_ref[gc, :])
        x2 = x1_ref[...] + acc
        r = _rms_r(x2)
        n = x2 * r
        g = g_ref[...]
        diff = n * g - t_ref[...]
        dy = diff * (1.0 / D_MODEL)
        dx2_ref[...] = _rms_bwd(dy, n, r, g)

        @pl.when(i == 0)
        def _():
            loss_ref[...] = jnp.zeros_like(loss_ref)
            gg_ref[...] = jnp.zeros_like(gg_ref)

        loss_ref[...] += 0.5 * jnp.sum(jnp.mean(diff * diff, axis=-1, keepdims=True), axis=0, keepdims=True)
        gg_ref[...] += jnp.sum(dy * n, axis=0, keepdims=True)

    return pl.pallas_call(
        body, name="ffn_down_loss", grid=(T // tm,),
        in_specs=[_row(tm, D_FF), _row(tm, D_FF), _row(tm, D_MODEL), _row(tm, D_MODEL),
                  _resident(w_down.shape), _full(g_final.shape)],
        out_specs=[_row(tm, D_MODEL), _full((1, 1)), _full((1, D_MODEL))],
        out_shape=[_sds((T, D_MODEL), F32), _sds((1, 1), F32), _sds((1, D_MODEL), F32)],
        compiler_params=_cp(("arbitrary",), 48),
    )(gate, val, x1, target, w_down, g_final)


def _ffn_bwd_act(gate, val, dx2, w_down, tm):
    T = dx2.shape[0]
    half = D_FF // 2
    nt = T // tm

    def body(g_ref, v_ref, dx_ref, wd_ref, dg_ref, dv_ref, gwd_out, gbg_ref, gbv_ref, gwd_ref):
        i = pl.program_id(1)
        gate = g_ref[...].astype(F32)
        val = v_ref[...].astype(F32)
        sg = _sigmoid(gate)
        silu = gate * sg
        dx = dx_ref[...].astype(BF16)
        d_act = _dot_nt(dx, wd_ref[...])
        d_val = d_act * silu
        d_gate = d_act * val * (sg * (1.0 + gate * (1.0 - sg)))
        dg_ref[...] = d_gate.astype(BF16)
        dv_ref[...] = d_val.astype(BF16)

        @pl.when(i == 0)
        def _():
            for r in (gwd_ref, gbg_ref, gbv_ref):
                r[...] = jnp.zeros_like(r)

        gwd_ref[...] += _dot_tn((silu * val).astype(BF16), dx)
        gbg_ref[...] += jnp.sum(d_gate, axis=0, keepdims=True)
        gbv_ref[...] += jnp.sum(d_val, axis=0, keepdims=True)

        @pl.when(i == nt - 1)
        def _():
            gwd_out[...] = gwd_ref[...].astype(BF16)

    tile = pl.BlockSpec((tm, half), lambda j, i: (i, j))
    vec = pl.BlockSpec((1, half), lambda j, i: (0, j))
    wrows = pl.BlockSpec((half, D_MODEL), lambda j, i: (j, 0))
    return pl.pallas_call(
        body, name="ffn_bwd_act", grid=(2, nt),
        in_specs=[tile, tile, pl.BlockSpec((tm, D_MODEL), lambda j, i: (i, 0)), wrows],
        out_specs=[tile, tile, wrows, vec, vec],
        out_shape=[_sds((T, D_FF), BF16), _sds((T, D_FF), BF16), _sds((D_FF, D_MODEL), BF16),
                   _sds((1, D_FF), F32), _sds((1, D_FF), F32)],
        scratch_shapes=[pltpu.VMEM((half, D_MODEL), F32)],
        compiler_params=_cp(("arbitrary", "arbitrary"), 56),
    )(gate, val, dx2, w_down)


def _ffn_bwd_up(d_gate, d_val, upre, dx2, x1, g_ffn, w_conv, w_up, tm, seq):
    T = dx2.shape[0]
    tiles_per_seq = seq // tm
    k16 = tm // BF16_ROWS
    n16 = T // BF16_ROWS
    cw = D_FF // 2

    def body(dg_ref, dv_ref, hg_ref, hv_ref, u_ref, dx2_ref, x1_ref, g_ref, wc_ref, wu_ref, du_ref, dx1_ref, gg_ref, gwc_ref):
        i = pl.program_id(0)
        at_end = (i % tiles_per_seq) == tiles_per_seq - 1

        @pl.when(i == 0)
        def _():
            gg_ref[...] = jnp.zeros_like(gg_ref)
            gwc_ref[...] = jnp.zeros_like(gwc_ref)

        dh = jnp.zeros((tm, D_MODEL), F32)
        for j in range(4):
            src, hsrc = (dg_ref, hg_ref) if j < 2 else (dv_ref, hv_ref)
            ls = slice((j % 2) * cw, (j % 2 + 1) * cw)
            cs = slice(j * cw, (j + 1) * cw)
            d = src[:, ls].astype(F32)
            hl = hsrc[:, ls].astype(F32)[0:2]
            hl = jnp.where(at_end, 0.0, hl)
            wc = wc_ref[:, cs]
            d1 = _shift_up(d, hl, 1)
            d2 = _shift_up(d, hl, 2)
            du = (wc[2:3] * d + wc[1:2] * d1 + wc[0:1] * d2).astype(BF16)
            du_ref[:, cs] = du
            dh = dh + _dot_nt(du, wu_ref[j])
            u = u_ref[:, cs].astype(F32)
            gwc_ref[0:1, cs] += jnp.sum(d2 * u, axis=0, keepdims=True)
            gwc_ref[1:2, cs] += jnp.sum(d1 * u, axis=0, keepdims=True)
            gwc_ref[2:3, cs] += jnp.sum(d * u, axis=0, keepdims=True)
        x = x1_ref[...]
        r = _rms_r(x)
        n = x * r
        dx1_ref[...] = dx2_ref[...] + _rms_bwd(dh, n, r, g_ref[...])
        gg_ref[...] += jnp.sum(dh * n, axis=0, keepdims=True)

    nxt = pl.BlockSpec((BF16_ROWS, D_FF), lambda i: (jnp.minimum((i + 1) * k16, n16 - 1), 0))
    return pl.pallas_call(
        body, name="ffn_bwd_up", grid=(T // tm,),
        in_specs=[_row(tm, D_FF), _row(tm, D_FF), nxt, nxt, _row(tm, 2 * D_FF), _row(tm, D_MODEL), _row(tm, D_MODEL),
                  _full(g_ffn.shape), _full(w_conv.shape), _resident(w_up.shape)],
        out_specs=[_row(tm, 2 * D_FF), _row(tm, D_MODEL), _full((1, D_MODEL)), _full((3, 2 * D_FF))],
        out_shape=[_sds((T, 2 * D_FF), BF16), _sds((T, D_MODEL), F32), _sds((1, D_MODEL), F32), _sds((3, 2 * D_FF), F32)],
        compiler_params=_cp(("arbitrary",), 56),
    )(d_gate, d_val, d_gate, d_val, upre, dx2, x1, g_ffn, w_conv, w_up)


def _matmul_tn(a, b, tn, tk, name):
    T, M = a.shape
    N = b.shape[1]
    nk = T // tk

    def body(a_ref, b_ref, o_ref, acc_ref):
        k = pl.program_id(1)

        @pl.when(k == 0)
        def _():
            acc_ref[...] = jnp.zeros_like(acc_ref)

        acc_ref[...] += _dot_tn(a_ref[...], b_ref[...])

        @pl.when(k == nk - 1)
        def _():
            o_ref[...] = acc_ref[...].astype(BF16)

    return pl.pallas_call(
        body, name=name, grid=(N // tn, nk),
        in_specs=[pl.BlockSpec((tk, M), lambda j, k: (k, 0)), pl.BlockSpec((tk, tn), lambda j, k: (k, j))],
        out_specs=pl.BlockSpec((M, tn), lambda j, k: (0, j)), out_shape=_sds((M, N), BF16),
        scratch_shapes=[pltpu.VMEM((M, tn), F32)],
        compiler_params=_cp(("arbitrary", "arbitrary"), 48),
    )(a, b)


def _merge_bwd(dx1, merged, y_a, y_b, proj_g, w_pa, w_pb, w_out, tm, after=None):
    T = dx1.shape[0]

    nt = T // tm
    pshape = (A_WIDTH, D_MODEL)
    order = [] if after is None else [after]

    def body(*refs):
        dx_ref, mg_ref, ya_ref, yb_ref, g_ref, wpa_ref, wpb_ref, wo_ref = refs[:8]
        dg_ref, dya_ref, dyb_ref, gwo_out, gwpa_out, gwpb_out, gwo_ref, gwpa_ref, gwpb_ref = refs[8 + len(order):]
        i = pl.program_id(0)
        dx = dx_ref[...].astype(BF16)
        dm = _dot_nt(dx, wo_ref[...])
        g = g_ref[...].astype(F32)
        ya = ya_ref[...]
        yb = yb_ref[...]
        pa = _dot_stacked(ya, wpa_ref)
        pb = _dot_stacked(yb, wpb_ref)
        sa = _sigmoid(g[:, :D_MODEL])
        sb = _sigmoid(g[:, D_MODEL:])
        dpa = (dm * sa).astype(BF16)
        dpb = (dm * sb).astype(BF16)
        dg_ref[:, :D_MODEL] = (dm * pa * (sa * (1.0 - sa))).astype(BF16)
        dg_ref[:, D_MODEL:] = (dm * pb * (sb * (1.0 - sb))).astype(BF16)
        dya_ref[...] = _dot_nt_stacked(dpa, wpa_ref).astype(BF16)
        dyb_ref[...] = _dot_nt_stacked(dpb, wpb_ref).astype(BF16)

        @pl.when(i == 0)
        def _():
            for r in (gwo_ref, gwpa_ref, gwpb_ref):
                r[...] = jnp.zeros_like(r)

        gwo_ref[...] += _dot_tn(mg_ref[...], dx)
        gwpa_ref[...] += _dot_tn(ya, dpa)
        gwpb_ref[...] += _dot_tn(yb, dpb)

        @pl.when(i == nt - 1)
        def _():
            gwo_out[...] = gwo_ref[...].astype(BF16)
            gwpa_out[...] = gwpa_ref[...].astype(BF16)
            gwpb_out[...] = gwpb_ref[...].astype(BF16)

    return pl.pallas_call(
        body, name="merge_bwd", grid=(nt,),
        in_specs=[_row(tm, D_MODEL), _row(tm, D_MODEL), _row(tm, A_WIDTH), _row(tm, Q_DIM), _row(tm, G_DIM),
                  _resident(w_pa.shape), _resident(w_pb.shape), _resident(w_out.shape)] + [ANY] * len(order),
        out_specs=[_row(tm, G_DIM), _row(tm, A_WIDTH), _row(tm, Q_DIM),
                   _full(w_out.shape), _full(pshape), _full(pshape)],
        out_shape=[_sds((T, G_DIM), BF16), _sds((T, A_WIDTH), BF16), _sds((T, Q_DIM), BF16),
                   _sds(w_out.shape, BF16), _sds(pshape, BF16), _sds(pshape, BF16)],
        scratch_shapes=[pltpu.VMEM(w_out.shape, F32), pltpu.VMEM(pshape, F32), pltpu.VMEM(pshape, F32)],
        compiler_params=_cp(("arbitrary",), 56),
    )(dx1, merged, y_a, y_b, proj_g, w_pa, w_pb, w_out, *order)


def _sgu_bwd(proj_a, d_ya, g_sgu, w_s, b_st, tm, after=None):
    T = proj_a.shape[0]
    order = [] if after is None else [after]

    def body(*refs):
        p_ref, dy_ref, g_ref, ws_ref, bs_ref = refs[:5]
        dp_ref, gws_ref, gbs_ref, gg_ref = refs[5 + len(order):]
        tril = _tril()
        g = g_ref[...]
        pu, pv, u, tu, vv, tv, rv, vn = _sgu_parts(p_ref[...].astype(F32), g)
        dy = dy_ref[...].astype(F32)

        @pl.when(pl.program_id(0) == 0)
        def _():
            for r in (gws_ref, gbs_ref, gg_ref):
                r[...] = jnp.zeros_like(r)

        du_cols = []
        dvn_cols = []
        for gi in range(A_GROUPS):
            wm = jnp.where(tril, ws_ref[gi], 0.0).astype(BF16)
            wmt = wm.astype(F32).T.astype(BF16)
            bcol = bs_ref[:, gi:gi + 1]
            cs = slice(gi * CHUNK, (gi + 1) * CHUNK)
            du_rows = []
            dvn_rows = []
            gw = jnp.zeros((CHUNK, CHUNK), F32)
            gb = jnp.zeros((CHUNK, 1), F32)
            for c in range(tm // CHUNK):
                rs = slice(c * CHUNK, (c + 1) * CHUNK)
                vn_c = vn[rs, cs]
                s = _dot(wm, vn_c) + bcol
                dy_c = dy[rs, cs]
                ds = dy_c * u[rs, cs]
                du_rows.append(dy_c * s)
                dsb = ds.astype(BF16)
                gw = gw + _dot_nt(dsb, vn_c)
                gb = gb + jnp.sum(ds, axis=-1, keepdims=True)
                dvn_rows.append(_dot(wmt, dsb))
            gws_ref[gi] += jnp.where(tril, gw, 0.0)
            gbs_ref[:, gi:gi + 1] += gb
            du_cols.append(jnp.concatenate(du_rows, axis=0))
            dvn_cols.append(jnp.concatenate(dvn_rows, axis=0))
        du = jnp.concatenate(du_cols, axis=1)
        dvn = jnp.concatenate(dvn_cols, axis=1)
        vhat = vv * rv
        gg_ref[...] += jnp.sum(dvn * vhat, axis=0, keepdims=True)
        dvv = _rms_bwd(dvn, vhat, rv, g)
        dp_ref[:, :A_WIDTH] = (du * _gelu_grad(pu, tu)).astype(BF16)
        dp_ref[:, A_WIDTH:] = (dvv * _gelu_grad(pv, tv)).astype(BF16)

    return pl.pallas_call(
        body, name="sgu_bwd", grid=(T // tm,),
        in_specs=[_row(tm, A_DIM), _row(tm, A_WIDTH), _full(g_sgu.shape), _full(w_s.shape), _full(b_st.shape)] + [ANY] * len(order),
        out_specs=[_row(tm, A_DIM), _full(w_s.shape), _full(b_st.shape), _full(g_sgu.shape)],
        out_shape=[_sds((T, A_DIM), BF16), _sds(w_s.shape, F32), _sds(b_st.shape, F32), _sds(g_sgu.shape, F32)],
        compiler_params=_cp(("arbitrary",)),
    )(proj_a, d_ya, g_sgu, w_s, b_st, *order)


def _attn_bwd(proj_b, d_yb, sinks, rel_bias, n_seq, seq):
    nb = seq // CHUNK
    bk = jnp.asarray(_band_buckets())

    def body(qkv_ref, do_ref, bk_ref, rel_ref, sink_ref, d_ref, gs_ref, gr_ref,
             bias_scr, sink_scr, kvar_scr, dbias_scr, dk_scr, dv_scr, ds_scr):
        b = pl.program_id(0)
        _attn_setup(bias_scr, sink_scr, kvar_scr, qkv_ref, bk_ref, rel_ref, sink_ref)
        ones = jnp.ones((2 * CHUNK, LANES), BF16)

        @pl.when(b == 0)
        def _():
            dbias_scr[...] = jnp.zeros_like(dbias_scr)
            ds_scr[...] = jnp.zeros_like(ds_scr)

        dk_scr[...] = jnp.zeros_like(dk_scr)
        dv_scr[...] = jnp.zeros_like(dv_scr)

        def transposed(a):
            return a.astype(F32).T.astype(BF16)

        def blk(n, carry):
            r0, kv, vv = _attn_block_inputs(kvar_scr, n)
            prob, psink = _attn_probs(qkv_ref, r0, n, kv, bias_scr, sink_scr, ones)
            dp = jnp.concatenate([_dot_nt(do_ref[pl.ds(r0, CHUNK), (h // 2) * LANES:(h // 2 + 1) * LANES], vv[h // 4][h % 2])
                                  for h in range(N_HEADS)], axis=0)
            delta = _rowsum(prob * dp, ones)
            dsc = prob * (dp - _both(delta))
            ds_scr[...] += psink * delta
            dbias_scr[...] += dsc
            dsb = (dsc * (HEAD_DIM ** -0.5)).astype(BF16)
            pb = prob.astype(BF16)
            dkt = [jnp.zeros((HEAD_DIM, 2 * CHUNK), F32) for _ in range(2)]
            dvt = [jnp.zeros((HEAD_DIM, 2 * CHUNK), F32) for _ in range(2)]
            for pr in range(N_HEADS // 2):
                ps = slice(pr * LANES, (pr + 1) * LANES)
                qpt = transposed(qkv_ref[pl.ds(r0, CHUNK), ps])
                dopt = transposed(do_ref[pl.ds(r0, CHUNK), ps])
                kvh = pr // 2
                dq = jnp.zeros((CHUNK, LANES), F32)
                for hh in range(2):
                    hr = _head_rows(2 * pr + hh)
                    rows = slice(hh * HEAD_DIM, (hh + 1) * HEAD_DIM)
                    dq = dq + _dot(dsb[hr], kv[kvh][hh])
                    dkt[kvh] = dkt[kvh] + _dot(qpt, dsb[hr])[rows]
                    dvt[kvh] = dvt[kvh] + _dot(dopt, pb[hr])[rows]
                d_ref[pl.ds(r0, CHUNK), ps] = dq.astype(BF16)
            dk_scr[:, pl.ds(r0, 2 * CHUNK)] += jnp.concatenate(dkt, axis=0)
            dv_scr[:, pl.ds(r0, 2 * CHUNK)] += jnp.concatenate(dvt, axis=0)
            return carry

        lax.fori_loop(0, nb, blk, 0)
        for n in range(nb):
            rows = slice(n * CHUNK, (n + 1) * CHUNK)
            cols = slice((n + 1) * CHUNK, (n + 2) * CHUNK)
            d_ref[rows, Q_DIM:Q_DIM + KV_DIM] = dk_scr[:, cols].T.astype(BF16)
            d_ref[rows, Q_DIM + KV_DIM:] = dv_scr[:, cols].T.astype(BF16)

        @pl.when(b == n_seq - 1)
        def _():
            bkv = bk_ref[...]
            for h in range(N_HEADS):
                gs_ref[0:1, h:h + 1] = -jnp.sum(ds_scr[_head_rows(h), 0:1], axis=0, keepdims=True)
                db = dbias_scr[_head_rows(h), :]
                for bb in range(N_BUCKETS):
                    part = jnp.sum(jnp.where(bkv == bb, db, 0.0), axis=-1, keepdims=True)
                    gr_ref[bb:bb + 1, h:h + 1] = jnp.sum(part, axis=0, keepdims=True)

    smem = pl.BlockSpec(memory_space=pltpu.SMEM)
    return pl.pallas_call(
        body, name="attn_bwd", grid=(n_seq,),
        in_specs=[_row(seq, B_DIM), _row(seq, Q_DIM), _full(bk.shape), smem, smem],
        out_specs=[_row(seq, B_DIM), _full((1, N_HEADS)), _full((N_BUCKETS, N_HEADS))],
        out_shape=[_sds((n_seq * seq, B_DIM), BF16), _sds((1, N_HEADS), F32), _sds((N_BUCKETS, N_HEADS), F32)],
        scratch_shapes=[pltpu.VMEM((HEAD_ROWS, 2 * CHUNK), F32), pltpu.VMEM((HEAD_ROWS, LANES), F32),
                        pltpu.VMEM((8, seq, KV_DIM), BF16), pltpu.VMEM((HEAD_ROWS, 2 * CHUNK), F32),
                        pltpu.VMEM((KV_DIM, seq + CHUNK), F32), pltpu.VMEM((KV_DIM, seq + CHUNK), F32),
                        pltpu.VMEM((HEAD_ROWS, LANES), F32)],
        compiler_params=_cp(("arbitrary",), 40),
    )(proj_b, d_yb, bk, rel_bias, sinks)


def _inproj_bwd(d_g, d_a, d_b, x2, dx1, g_mix, w_g, w_a, w_b, tm, after=None):
    T = x2.shape[0]
    order = [] if after is None else [after]

    def body(*refs):
        dg_ref, da_ref, db_ref, x_ref, dx1_ref, g_ref, wg_ref, wa_ref, wb_ref = refs[:9]
        gx_ref, gg_ref = refs[9 + len(order):]
        dh = _dot_nt(dg_ref[...], wg_ref[...]) + _dot_nt(da_ref[...], wa_ref[...]) + _dot_nt(db_ref[...], wb_ref[...])
        x = x_ref[...]
        r = _rms_r(x)
        n = x * r
        gx_ref[...] = dx1_ref[...] + _rms_bwd(dh, n, r, g_ref[...])

        @pl.when(pl.program_id(0) == 0)
        def _():
            gg_ref[...] = jnp.zeros_like(gg_ref)

        gg_ref[...] += jnp.sum(dh * n, axis=0, keepdims=True)

    return pl.pallas_call(
        body, name="inproj_bwd", grid=(T // tm,),
        in_specs=[_row(tm, G_DIM), _row(tm, A_DIM), _row(tm, B_DIM), _row(tm, D_MODEL), _row(tm, D_MODEL),
                  _full(g_mix.shape), _resident(w_g.shape), _resident(w_a.shape), _resident(w_b.shape)] + [ANY] * len(order),
        out_specs=[_row(tm, D_MODEL), _full((1, D_MODEL))],
        out_shape=[_sds((T, D_MODEL), F32), _sds((1, D_MODEL), F32)],
        compiler_params=_cp(("arbitrary",), 48),
    )(d_g, d_a, d_b, x2, dx1, g_mix, w_g, w_a, w_b, *order)


def _local_step(x, target, g_mix, g_sgu, w_s, b_s, sinks, rel_bias, g_ffn, b_conv, g_final,
                w_g, w_a, w_b, w_conv, late_weights, on_grads, after=None):
    n_seq, seq, _ = x.shape
    T = n_seq * seq
    tm = min(ROW_TILE, seq)
    tw = min(GRAD_ROW_TILE, T)
    tf = min(WIDE_ROW_TILE, seq)
    x2 = x.reshape(T, D_MODEL)
    tgt = target.reshape(T, D_MODEL)
    b_st = b_s.T
    g_fin = g_final.reshape(1, D_MODEL)

    proj_g, proj_a, proj_b, h = _inproj(x2, g_mix, w_g, w_a, w_b, tm, after)
    y_a = _sgu_fwd(proj_a, g_sgu, w_s, b_st, tm)
    y_b = _attn_fwd(proj_b, sinks, rel_bias, n_seq, seq)
    w_pa, w_pb, w_out, w_up, w_down = late_weights(y_b)
    x1, merged = _merge_fwd(x2, y_a, y_b, proj_g, w_pa, w_pb, w_out, tm)
    upre, h2, gate, val = _upproj(x1, g_ffn, w_up, w_conv, b_conv, tf, seq)
    dx2, loss, gg_final = _ffn_down_loss(gate, val, x1, tgt, w_down, g_fin, tm)

    d_gate, d_val, gw_down, gb_g, gb_v = _ffn_bwd_act(gate, val, dx2, w_down, tw)
    gb_conv = jnp.concatenate([gb_g, gb_v], axis=1)
    d_upre, dx1, gg_ffn, gw_conv = _ffn_bwd_up(d_gate, d_val, upre, dx2, x1, g_ffn, w_conv, w_up, tf, seq)
    gw_up = _matmul_tn(h2, d_upre, 2 * D_FF // 4, min(2 * GRAD_ROW_TILE, T), "grad_w_up")
    sent = on_grads("ffn", dict(w_up=gw_up, w_down=gw_down))
    d_g, d_ya, d_yb, gw_out, gw_pa, gw_pb = _merge_bwd(dx1, merged, y_a, y_b, proj_g, w_pa, w_pb, w_out, tf, sent)
    sent = on_grads("proj", dict(w_pa=gw_pa, w_pb=gw_pb, w_out=gw_out))
    d_a, gw_s, gb_st, gg_sgu = _sgu_bwd(proj_a, d_ya, g_sgu, w_s, b_st, tm, sent)
    d_b, g_sinks, g_rel = _attn_bwd(proj_b, d_yb, sinks, rel_bias, n_seq, seq)
    gw_g = _matmul_tn(h, d_g, D_MODEL, min(2 * GRAD_ROW_TILE, T), "grad_w_in_gate")
    gw_a = _matmul_tn(h, d_a, A_DIM, min(2 * GRAD_ROW_TILE, T), "grad_w_in_a")
    gw_b = _matmul_tn(h, d_b, B_DIM, min(2 * GRAD_ROW_TILE, T), "grad_w_in_b")
    gw_in = jnp.concatenate([gw_a, gw_b, gw_g], axis=1).reshape(D_MODEL, N_CHIPS, -1).transpose(1, 0, 2)
    sent = on_grads("in", dict(w_in=gw_in))
    grad_x, gg_mix = _inproj_bwd(d_g, d_a, d_b, x2, dx1, g_mix, w_g, w_a, w_b, tm, sent)

    small = dict(g_mix=gg_mix, g_sgu=gg_sgu, w_s=gw_s, b_s=gb_st.T, sinks=g_sinks, rel_bias=g_rel,
                 g_ffn=gg_ffn, b_conv=gb_conv, g_final=gg_final, w_conv=gw_conv)
    big = dict(w_in=gw_in, w_pa=gw_pa, w_pb=gw_pb, w_out=gw_out, w_up=gw_up, w_down=gw_down)
    return loss, grad_x.reshape(x.shape), small, big


_MIXER = ("w_in", "w_pa", "w_pb", "w_out")
_FFN = ("w_up", "w_down")
_BIG = _MIXER + _FFN

_SMALL = (("loss", (1, 1)), ("g_final", (1, D_MODEL)), ("g_mix", (1, D_MODEL)), ("g_ffn", (1, D_MODEL)),
          ("g_sgu", (1, A_WIDTH)), ("b_s", (A_GROUPS, CHUNK)), ("sinks", (1, N_HEADS)), ("rel_bias", (N_BUCKETS, N_HEADS)),
          ("b_conv", (1, 2 * D_FF)), ("w_conv", (3, 2 * D_FF)), ("w_s", (A_GROUPS, CHUNK, CHUNK)))
SMALL_ROWS = 96


def _pack_small(vals):
    flat = jnp.concatenate([vals[n].astype(F32).reshape(-1) for n, _ in _SMALL])
    flat = jnp.pad(flat, (0, SMALL_ROWS * D_MODEL - flat.shape[0]))
    return flat.reshape(SMALL_ROWS, D_MODEL)


def _unpack_small(buf):
    flat = buf.reshape(-1)
    out = {}
    off = 0
    for n, shp in _SMALL:
        k = int(np.prod(shp))
        out[n] = flat[off:off + k].reshape(shp)
        off += k
    return out


def _mesh_pos():
    return lax.axis_index("x"), lax.axis_index("y"), lax.axis_index("c")


def _other_chips(x, y):
    return [(1 - x, y), (x, 1 - y), (1 - x, 1 - y)]


def _remote(src, dst, send_sem, recv_sem, to):
    return pltpu.make_async_remote_copy(src_ref=src, dst_ref=dst, send_sem=send_sem, recv_sem=recv_sem,
                                        device_id=to, device_id_type=MESH)


def _own_slot(own, n, at):
    return lax.dynamic_update_slice(lax.empty((n,) + own.shape, own.dtype), own[None], (at,) + (0,) * own.ndim)


def _allgather_weights(stacks, wc_stack):
    names = list(stacks)
    n = len(names)

    def body(*refs):
        ins, outs = refs[:n + 1], refs[n + 1:2 * n + 2]
        send_sems, recv_sems = refs[2 * n + 2:]
        x, y, c = _mesh_pos()
        me = 2 * x + y
        sibling = (x, y, 1 - c)
        chips = _other_chips(x, y)

        def half(ref, chip, hc):
            hr = ref.shape[1] // 2
            return ref.at[chip, pl.ds(hc * hr, hr), :]

        first = []
        for k in range(n):
            first += [_remote(half(ins[k], me, c), half(outs[k], me, c), send_sems.at[6 * k + j], recv_sems.at[6 * k + j], (cx, cy, c))
                      for j, (cx, cy) in enumerate(chips)]
        first += [_remote(ins[n].at[me], outs[n].at[me], send_sems.at[6 * n + j], recv_sems.at[6 * n + j], (cx, cy, c))
                  for j, (cx, cy) in enumerate(chips)]
        for cp in first:
            cp.start()
        passed = []
        for k in range(n):
            for j, (cx, cy) in enumerate(chips):
                landed = half(outs[k], 2 * cx + cy, c)
                _remote(landed, landed, send_sems.at[6 * k + j], recv_sems.at[6 * k + j], (x, y, c)).wait_recv()
                passed.append(_remote(landed, landed, send_sems.at[6 * k + 3 + j], recv_sems.at[6 * k + 3 + j], sibling))
                passed[-1].start()
        for k in range(n):
            for j, (cx, cy) in enumerate(chips):
                theirs = half(outs[k], 2 * cx + cy, 1 - c)
                _remote(theirs, theirs, send_sems.at[6 * k + 3 + j], recv_sems.at[6 * k + 3 + j], (x, y, c)).wait_recv()
        for j, (cx, cy) in enumerate(chips):
            slot = outs[n].at[2 * cx + cy]
            _remote(slot, slot, send_sems.at[6 * n + j], recv_sems.at[6 * n + j], (x, y, c)).wait_recv()
        for cp in first + passed:
            cp.wait_send()

    arrays = [stacks[k] for k in names] + [wc_stack]
    outs = pl.pallas_call(
        body, name="allgather_weights",
        in_specs=[HBM] * (n + 1), out_specs=[HBM] * (n + 1), input_output_aliases={k: k for k in range(n + 1)},
        out_shape=[_sds(a.shape, a.dtype) for a in arrays],
        scratch_shapes=[pltpu.SemaphoreType.DMA((6 * n + 3,)), pltpu.SemaphoreType.DMA((6 * n + 3,))],
    )(*arrays)
    return dict(zip(names, outs[:n])), outs[n]


_KIND = {"w_in": "stack", "w_pa": "col", "w_pb": "col", "w_up": "col", "w_out": "row", "w_down": "row"}


def _half_view(ref, kind, h):
    if kind == "stack":
        k = ref.shape[1] // 2
        return ref.at[:, pl.ds(h * k, k), :]
    if kind == "col":
        k = ref.shape[0] // 2
        return ref.at[pl.ds(h * k, k), :]
    k = ref.shape[1] // 2
    return ref.at[:, pl.ds(h * k, k)]


def _shard_view(ref, kind, i):
    if kind == "stack":
        return ref.at[i]
    if kind == "col":
        k = ref.shape[1] // N_CHIPS
        return ref.at[:, pl.ds(i * k, k)]
    k = ref.shape[0] // N_CHIPS
    return ref.at[pl.ds(i * k, k), :]


def _region_view(ref, kind, h):
    if kind == "row":
        k = ref.shape[1] // 2
        return ref.at[:, pl.ds(h * k, k)]
    k = ref.shape[0] // 2
    return ref.at[pl.ds(h * k, k), :]


def _half_shape(shape, kind):
    if kind == "stack":
        return (shape[0], shape[1] // 2, shape[2])
    return (shape[0] // 2, shape[1]) if kind == "col" else (shape[0], shape[1] // 2)


def _part_shape(half_shape, kind):
    if kind == "stack":
        return tuple(half_shape[1:])
    k, w = half_shape
    return (k, w // N_CHIPS) if kind == "col" else (k // N_CHIPS, w)


_DATAFLOW = pltpu.SideEffectType.DATAFLOW_SIDE_EFFECTING
_TOKEN = (SUBLANES, LANES)


def _split_start(name, arrays, n_sems, issue, after=None):
    n = len(arrays)
    order = [] if after is None else [after]

    def body(*refs):
        base = n + len(order)
        issue(refs[:n], refs[base], refs[base + 1])
        refs[-1][...] = jnp.zeros(_TOKEN, F32)

    outs = pl.pallas_call(
        body, name=name,
        in_specs=[HBM] * n + [ANY] * len(order), out_specs=[SEM, SEM] + [HBM] * n + [pl.BlockSpec(memory_space=pltpu.VMEM)],
        out_shape=[pltpu.SemaphoreType.DMA((n_sems,)), pltpu.SemaphoreType.DMA((n_sems,))]
        + [pltpu.HBM(a.shape, a.dtype) for a in arrays] + [_sds(_TOKEN, F32)],
        input_output_aliases={k: 2 + k for k in range(n)},
        compiler_params=pltpu.CompilerParams(has_side_effects=_DATAFLOW),
    )(*[pltpu.with_memory_space_constraint(a, pltpu.HBM) for a in arrays], *order)
    return outs[0], outs[1], list(outs[2:2 + n]), outs[-1]


def _split_wait(name, started, waits, after):
    send_sems, recv_sems, arrays, _ = started
    n = len(arrays)

    def body(*refs):
        waits(refs[:n], refs[n], refs[n + 1])

    return pl.pallas_call(
        body, name=name,
        in_specs=[HBM] * n + [SEM, SEM, ANY], out_specs=[HBM] * n,
        out_shape=[pltpu.HBM(a.shape, a.dtype) for a in arrays],
        input_output_aliases={k: k for k in range(n)},
        compiler_params=pltpu.CompilerParams(has_side_effects=_DATAFLOW),
    )(*arrays, send_sems, recv_sems, after)


def _wait_both(src, dst, send_sem, recv_sem):
    x, y, c = _mesh_pos()
    cp = _remote(src, dst, send_sem, recv_sem, (x, y, c))
    cp.wait_send()
    cp.wait_recv()


def _pair_exchange_start(parts, tag, after):
    names = list(parts)
    n = len(names)
    lands = [lax.empty(_half_shape(parts[k].shape, _KIND[k]), parts[k].dtype) for k in names]

    def issue(refs, send_sems, recv_sems):
        x, y, c = _mesh_pos()
        for hc in range(2):
            @pl.when(c == hc)
            def _():
                for k in range(n):
                    _remote(_half_view(refs[k], _KIND[names[k]], 1 - hc), refs[n + k], send_sems.at[k], recv_sems.at[k],
                            (x, y, 1 - c)).start()

    return names, _split_start("grad_pair_exchange_start_" + tag, [parts[k] for k in names] + lands, n, issue, after)


def _pair_exchange_wait(pending, tag, after):
    names, started = pending
    n = len(names)

    def waits(refs, send_sems, recv_sems):
        for k in range(n):
            _wait_both(_half_view(refs[k], _KIND[names[k]], 0), refs[n + k], send_sems.at[k], recv_sems.at[k])

    return dict(zip(names, _split_wait("grad_pair_exchange_wait_" + tag, started, waits, after)[n:]))


def _half_blocks(shape, kind):
    if kind == "stack":
        _, k, w = shape
        tr = 256
        nb = k // 2 // tr
        return (N_CHIPS, nb), (1, tr, w), (lambda i, r, s: (i, r, 0)), (lambda i, r, s: (i, s[1] * nb + r, 0))
    k, w = shape
    if kind == "col":
        tr = 256 if w <= 2 * D_MODEL else 128
        nb = k // 2 // tr
        return (nb,), (tr, w), (lambda r, s: (r, 0)), (lambda r, s: (s[1] * nb + r, 0))
    tr = k // N_CHIPS
    return (N_CHIPS,), (tr, w // 2), (lambda r, s: (r, 0)), (lambda r, s: (r, s[1]))


def _pair_add(part, from_sibling, name, pos):
    kind = _KIND[name]
    grid, block, half_map, full_map = _half_blocks(part.shape, kind)

    def body(s_ref, p_ref, q_ref, o_ref):
        o_ref[...] = (p_ref[...].astype(F32) + q_ref[...].astype(F32)).astype(BF16)

    return pl.pallas_call(
        body, name="grad_pair_add_" + name,
        grid_spec=pltpu.PrefetchScalarGridSpec(
            num_scalar_prefetch=1, grid=grid,
            in_specs=[pl.BlockSpec(block, full_map), pl.BlockSpec(block, half_map)],
            out_specs=pl.BlockSpec(block, half_map)),
        out_shape=_sds(from_sibling.shape, BF16),
        compiler_params=_cp(("arbitrary",) * len(grid)),
    )(pos, part, from_sibling)


def _chip_exchange_start(sums, tag, after):
    names = list(sums)
    n = len(names)
    lands = [lax.empty((3,) + _part_shape(sums[k].shape, _KIND[k]), sums[k].dtype) for k in names]

    def issue(refs, send_sems, recv_sems):
        x, y, c = _mesh_pos()
        me = 2 * x + y
        for i in range(N_CHIPS):
            xi, yi = i // 2, i % 2
            j = jnp.where(xi != x, jnp.where(yi != y, 2, 0), 1)

            @pl.when(i != me)
            def _():
                for k in range(n):
                    _remote(_shard_view(refs[k], _KIND[names[k]], i), refs[n + k].at[j], send_sems.at[3 * k + j],
                            recv_sems.at[3 * k + j], (xi, yi, c)).start()

    return names, _split_start("grad_chip_exchange_start_" + tag, [sums[k] for k in names] + lands, 3 * n, issue, after)


def _chip_exchange_wait(pending, tag, after):
    names, started = pending
    n = len(names)

    def waits(refs, send_sems, recv_sems):
        for k in range(n):
            for j in range(3):
                _wait_both(_shard_view(refs[k], _KIND[names[k]], 0), refs[n + k].at[j], send_sems.at[3 * k + j], recv_sems.at[3 * k + j])

    return dict(zip(names, _split_wait("grad_chip_exchange_wait_" + tag, started, waits, after)[n:]))


def _allgather_start(stacks, after):
    names = list(stacks)

    def issue(refs, send_sems, recv_sems):
        x, y, c = _mesh_pos()
        me = 2 * x + y
        for k, st in enumerate(refs):
            hr = st.shape[1] // 2
            mine = st.at[me, pl.ds(c * hr, hr), :]
            for j, (cx, cy) in enumerate(_other_chips(x, y)):
                _remote(mine, mine, send_sems.at[3 * k + j], recv_sems.at[3 * k + j], (cx, cy, c)).start()

    return names, _split_start("allgather_start", [stacks[k] for k in names], 3 * len(names), issue, after)


def _allgather_wait(pending, after):
    names, started = pending

    def waits(refs, send_sems, recv_sems):
        for k, st in enumerate(refs):
            slot = st.at[0, pl.ds(0, st.shape[1] // 2), :]
            for j in range(3):
                _wait_both(slot, slot, send_sems.at[3 * k + j], recv_sems.at[3 * k + j])

    return dict(zip(names, _split_wait("allgather_wait", started, waits, after)))


def _allgather_forward(stacks):
    names = list(stacks)
    n = len(names)

    def body(*refs):
        ins, outs = refs[:n], refs[n:2 * n]
        send_sems, recv_sems = refs[2 * n:]
        x, y, c = _mesh_pos()
        copies = []
        for k in range(n):
            hr = ins[k].shape[1] // 2
            for j, (cx, cy) in enumerate(_other_chips(x, y)):
                chip = 2 * cx + cy
                copies.append(_remote(ins[k].at[chip, pl.ds(c * hr, hr), :], outs[k].at[chip, pl.ds(c * hr, hr), :],
                                      send_sems.at[3 * k + j], recv_sems.at[3 * k + j], (x, y, 1 - c)))
        for cp in copies:
            cp.start()
        for cp in copies:
            cp.wait()

    arrays = [stacks[k] for k in names]
    outs = pl.pallas_call(
        body, name="allgather_forward", in_specs=[HBM] * n, out_specs=[HBM] * n,
        input_output_aliases={k: k for k in range(n)},
        out_shape=[_sds(a.shape, a.dtype) for a in arrays],
        scratch_shapes=[pltpu.SemaphoreType.DMA((3 * n,)), pltpu.SemaphoreType.DMA((3 * n,))],
    )(*arrays)
    return dict(zip(names, outs))


def _owner_sum(part, from_sibling, from_chips, name, pos, shard_shape):
    kind = _KIND[name]
    _, pk, pw = from_chips.shape
    if kind == "row":
        tr, nb = pk, 1
        p_spec = pl.BlockSpec((tr, pw), lambda r, s: (s[0], s[1]))
        q_spec = pl.BlockSpec((tr, pw), lambda r, s: (s[0], 0))
        o_spec = pl.BlockSpec((tr, pw), lambda r, s: (0, s[1]))
    else:
        tr = 256
        nb = pk // tr
        if kind == "stack":
            p_spec = pl.BlockSpec((None, tr, pw), lambda r, s: (s[0], s[1] * nb + r, 0))
            q_spec = pl.BlockSpec((None, tr, pw), lambda r, s: (s[0], r, 0))
        else:
            p_spec = pl.BlockSpec((tr, pw), lambda r, s: (s[1] * nb + r, s[0]))
            q_spec = pl.BlockSpec((tr, pw), lambda r, s: (r, s[0]))
        o_spec = pl.BlockSpec((tr, pw), lambda r, s: (s[1] * nb + r, 0))

    def body(s_ref, p_ref, q_ref, r_ref, o_ref):
        acc = p_ref[...].astype(F32) + q_ref[...].astype(F32)
        for j in range(3):
            acc = acc + r_ref[j].astype(F32)
        o_ref[...] = acc

    return pl.pallas_call(
        body, name="grad_owner_sum_" + name,
        grid_spec=pltpu.PrefetchScalarGridSpec(
            num_scalar_prefetch=1, grid=(nb,),
            in_specs=[p_spec, q_spec, pl.BlockSpec((3, tr, pw), lambda r, s: (0, r, 0))],
            out_specs=o_spec),
        out_shape=_sds(shard_shape, F32),
        compiler_params=_cp(("arbitrary",), 32),
    )(pos, part, from_sibling, from_chips)


def _pair_share_start(shards, tag, after):
    names = list(shards)

    def issue(refs, send_sems, recv_sems):
        x, y, c = _mesh_pos()
        for hc in range(2):
            @pl.when(c == hc)
            def _():
                for k, g in enumerate(refs):
                    mine = _region_view(g, _KIND[names[k]], hc)
                    _remote(mine, mine, send_sems.at[k], recv_sems.at[k], (x, y, 1 - c)).start()

    return names, _split_start("grad_pair_share_start_" + tag, [shards[k] for k in names], len(names), issue, after)


def _pair_share_wait(pending, tag, after):
    names, started = pending

    def waits(refs, send_sems, recv_sems):
        for k, g in enumerate(refs):
            region = _region_view(g, _KIND[names[k]], 0)
            _wait_both(region, region, send_sems.at[k], recv_sems.at[k])

    return dict(zip(names, _split_wait("grad_pair_share_wait_" + tag, started, waits, after)))


def _small_exchange_start(slots, after):
    def issue(refs, send_sems, recv_sems):
        x, y, c = _mesh_pos()
        mine = refs[0].at[4 * x + 2 * y + c]
        k = 0
        for px in range(2):
            for py in range(2):
                for pc in range(2):
                    if px + py + pc:
                        peer = (1 - x if px else x, 1 - y if py else y, 1 - c if pc else c)
                        _remote(mine, mine, send_sems.at[k], recv_sems.at[k], peer).start()
                        k += 1

    return _split_start("small_exchange_start", [slots], N_DEV - 1, issue, after)


def _small_exchange_wait(started, after):
    def waits(refs, send_sems, recv_sems):
        slot = refs[0].at[0]
        for k in range(N_DEV - 1):
            _wait_both(slot, slot, send_sems.at[k], recv_sems.at[k])

    return _split_wait("small_exchange_wait", started, waits, after)[0]


def _adam_math(w, g, m, v):
    m = ADAM_B1 * m + (1.0 - ADAM_B1) * g
    v = ADAM_B2 * v + (1.0 - ADAM_B2) * (g * g)
    m_hat = m / (1.0 - ADAM_B1 ** ADAM_STEP)
    v_hat = v / (1.0 - ADAM_B2 ** ADAM_STEP)
    delta = -ADAM_LR * (m_hat / (jnp.sqrt(v_hat) + ADAM_EPS) + ADAM_WD * w)
    return delta, m, v


def _adamw(w, g, m, v, name):
    rows, cols = w.shape
    tr = rows
    for cand in (256, 128, 64, 32, 16, 8):
        if rows % cand == 0 and rows > cand:
            tr = cand
            break

    def body(w_ref, g_ref, m_ref, v_ref, d_ref, nm_ref, nv_ref):
        d, nm, nv = _adam_math(w_ref[...], g_ref[...], m_ref[...], v_ref[...])
        d_ref[...] = d
        nm_ref[...] = nm
        nv_ref[...] = nv

    spec = pl.BlockSpec((tr, cols), lambda i: (i, 0))
    return pl.pallas_call(
        body, name=name, grid=(rows // tr,), in_specs=[spec] * 4, out_specs=[spec] * 3,
        out_shape=[_sds(w.shape, F32)] * 3, compiler_params=_cp(("arbitrary",)),
    )(w, g, m, v)


def _small_sum_adamw(gathered, w, m, v):
    def body(a_ref, w_ref, m_ref, v_ref, g_ref, d_ref, nm_ref, nv_ref):
        g = a_ref[0]
        for k in range(1, N_DEV):
            g = g + a_ref[k]
        g_ref[...] = g
        d, nm, nv = _adam_math(w_ref[...], g, m_ref[...], v_ref[...])
        d_ref[...] = d
        nm_ref[...] = nm
        nv_ref[...] = nv

    return pl.pallas_call(
        body, name="small_sum_adamw", out_shape=[_sds(w.shape, F32)] * 4,
    )(gathered, w, m, v)


_NAMES = ("g_mix", "w_in", "g_sgu", "w_s", "b_s", "sinks", "rel_bias", "w_pa", "w_pb", "w_out",
          "g_ffn", "w_up", "w_conv", "b_conv", "w_down", "g_final")

def kernel(x, g_mix, w_in, g_sgu, w_s, b_s, sinks, rel_bias, w_pa, w_pb, w_out, g_ffn, w_up, w_conv, b_conv, w_down, g_final, loss_target, m_g_mix, m_w_in, m_g_sgu, m_w_s, m_b_s, m_sinks, m_rel_bias, m_w_pa, m_w_pb, m_w_out, m_g_ffn, m_w_up, m_w_conv, m_b_conv, m_w_down, m_g_final, v_g_mix, v_w_in, v_g_sgu, v_w_s, v_b_s, v_sinks, v_rel_bias, v_w_pa, v_w_pb, v_w_out, v_g_ffn, v_w_up, v_w_conv, v_b_conv, v_w_down, v_g_final):
    w = dict(g_mix=g_mix, w_in=w_in, g_sgu=g_sgu, w_s=w_s, b_s=b_s, sinks=sinks, rel_bias=rel_bias, w_pa=w_pa, w_pb=w_pb,
             w_out=w_out, g_ffn=g_ffn, w_up=w_up, w_conv=w_conv, b_conv=b_conv, w_down=w_down, g_final=g_final)
    m = dict(g_mix=m_g_mix, w_in=m_w_in, g_sgu=m_g_sgu, w_s=m_w_s, b_s=m_b_s, sinks=m_sinks, rel_bias=m_rel_bias, w_pa=m_w_pa,
             w_pb=m_w_pb, w_out=m_w_out, g_ffn=m_g_ffn, w_up=m_w_up, w_conv=m_w_conv, b_conv=m_b_conv, w_down=m_w_down,
             g_final=m_g_final)
    v = dict(g_mix=v_g_mix, w_in=v_w_in, g_sgu=v_g_sgu, w_s=v_w_s, b_s=v_b_s, sinks=v_sinks, rel_bias=v_rel_bias, w_pa=v_w_pa,
             w_pb=v_w_pb, w_out=v_w_out, g_ffn=v_g_ffn, w_up=v_w_up, w_conv=v_w_conv, b_conv=v_b_conv, w_down=v_w_down,
             g_final=v_g_final)
    xi, yi, ci = _mesh_pos()
    me = 2 * xi + yi

    shard = {n: w[n][0] for n in _BIG}
    shard_shapes = {n: shard[n].shape for n in _BIG}
    wc_shard = w["w_conv"][0]
    wc_pad = jnp.pad(wc_shard, ((0, 5), (0, 0)))
    own = {n: _own_slot(shard[n].astype(BF16), N_CHIPS, me) for n in _BIG}
    stacks, wc_all = _allgather_weights({"w_in": own["w_in"]}, _own_slot(wc_pad, N_CHIPS, me))
    late_gather = _allgather_start({n: own[n] for n in _BIG[1:]}, stacks["w_in"])
    w_conv_full = jnp.concatenate([wc_all[i, :3] for i in range(N_CHIPS)], axis=1)
    w_in_full = stacks["w_in"].transpose(1, 0, 2).reshape(D_MODEL, -1)
    w_a = w_in_full[:, :A_DIM]
    w_b = w_in_full[:, A_DIM:A_DIM + B_DIM]
    w_g = w_in_full[:, A_DIM + B_DIM:]
    pos = jnp.stack([me, ci])

    def late_weights(done):
        st = _allgather_forward(_allgather_wait(late_gather, done))
        return st["w_pa"], st["w_pb"], st["w_out"].reshape(D_MODEL, D_MODEL), st["w_up"], st["w_down"].reshape(D_FF, D_MODEL)

    groups = {}

    def stage1(group, parts):
        groups[group] = dict(parts=parts, pair=_pair_exchange_start(parts, group, None))
        return groups[group]["pair"][1][-1]

    def stage2(group, after, order_after):
        g = groups[group]
        g["sib"] = _pair_exchange_wait(g["pair"], group, after)
        g["chip"] = _chip_exchange_start({n: _pair_add(g["parts"][n], g["sib"][n], n, pos) for n in g["parts"]}, group, order_after)
        return g["chip"][1][-1]

    def stage3(group, after, order_after):
        g = groups[group]
        got = _chip_exchange_wait(g["chip"], group, after)
        g["share"] = _pair_share_start(
            {n: _owner_sum(g["parts"][n], g["sib"][n], got[n], n, pos, shard_shapes[n]) for n in g["parts"]}, group, order_after)
        return g["share"][1][-1]

    grads, deltas, new_m, new_v = {}, {}, {}, {}

    def stage4(group, after):
        g_shard = _pair_share_wait(groups[group]["share"], group, after)
        last = None
        for n in g_shard:
            if n == "w_in":
                gt = g_shard[n].T
                d, nm, nv = _adamw(shard[n].T, gt, m[n][0].T, v[n][0].T, "adamw_" + n)
                grads[n], deltas[n], new_m[n], new_v[n] = gt.T[None], d.T[None], nm.T[None], nv.T[None]
            else:
                d, nm, nv = _adamw(shard[n], g_shard[n], m[n][0], v[n][0], "adamw_" + n)
                grads[n], deltas[n], new_m[n], new_v[n] = g_shard[n][None], d[None], nm[None], nv[None]
            last = nv
        return last

    def on_grads(group, parts):
        token = stage1(group, parts)
        some = next(iter(parts.values()))
        if group == "proj":
            token = stage2("ffn", some, token)
        if group == "in":
            token = stage2("proj", some, token)
            token = stage3("ffn", some, token)
        return token

    loss, grad_x, small, big = _local_step(
        x, loss_target, w["g_mix"], w["g_sgu"], w["w_s"][0], w["b_s"][0], w["sinks"], w["rel_bias"], w["g_ffn"],
        w["b_conv"], w["g_final"], w_g, w_a, w_b, w_conv_full, late_weights, on_grads, late_gather[1][-1])

    token = stage2("in", grad_x, None)
    small["loss"] = loss
    small_gather = _small_exchange_start(_own_slot(_pack_small(small), N_DEV, 2 * me + ci), token)
    token = stage3("proj", grad_x, small_gather[-1])
    done = stage4("ffn", token)
    done = stage4("proj", done)
    token = stage3("in", done, None)
    sw = {n: (jnp.zeros((1, 1), F32) if n in ("loss", "w_conv") else w[n]) for n, _ in _SMALL}
    sm = {n: (jnp.zeros((1, 1), F32) if n in ("loss", "w_conv") else m[n]) for n, _ in _SMALL}
    sv = {n: (jnp.zeros((1, 1), F32) if n in ("loss", "w_conv") else v[n]) for n, _ in _SMALL}
    for d in (sw, sm, sv):
        d["w_conv"] = jnp.zeros((3, 2 * D_FF), F32)
    all_small = _small_exchange_wait(small_gather, token)
    s_g, s_d, s_m, s_v = [_unpack_small(a) for a in _small_sum_adamw(all_small, _pack_small(sw), _pack_small(sm), _pack_small(sv))]
    stage4("in", all_small)
    wcols = wc_shard.shape[1]
    g_wc = lax.dynamic_slice(s_g["w_conv"], (0, me * wcols), (3, wcols))
    d, nm, nv = _adamw(wc_shard, g_wc, m["w_conv"][0], v["w_conv"][0], "adamw_w_conv")
    grads["w_conv"], deltas["w_conv"], new_m["w_conv"], new_v["w_conv"] = g_wc[None], d[None], nm[None], nv[None]
    for n, _ in _SMALL:
        if n in ("loss", "w_conv"):
            continue
        shp = w[n].shape
        grads[n], deltas[n], new_m[n], new_v[n] = (s_g[n].reshape(shp), s_d[n].reshape(shp), s_m[n].reshape(shp),
                                                    s_v[n].reshape(shp))

    return (s_g["loss"].reshape(()), grad_x, *[grads[n] for n in _NAMES], *[deltas[n] for n in _NAMES],
            *[new_m[n] for n in _NAMES], *[new_v[n] for n in _NAMES])
```

```python
import functools

import numpy as np
import jax
import jax.numpy as jnp
from jax import lax
from jax.experimental import pallas as pl
from jax.experimental.pallas import tpu as pltpu

F32 = jnp.float32
BF16 = jnp.bfloat16

D_MODEL = 1024
CHUNK = 128
A_GROUPS = 4
A_WIDTH = 512
N_HEADS = 8
HEAD_DIM = 64
Q_DIM = 512
KV_DIM = 128
N_BUCKETS = 32
MAX_DISTANCE = 128
D_FF = 2816
EPS = 1e-6
NEG_INF = -1e30
G_DIM = 2 * D_MODEL
A_DIM = 2 * A_WIDTH
B_DIM = Q_DIM + 2 * KV_DIM
LANES = 128
SUBLANES = 8
ROW_TILE = 512
WIDE_ROW_TILE = 256
GRAD_ROW_TILE = 512
BF16_ROWS = 16
N_CHIPS = 4
N_DEV = 8

ADAM_LR = 0.001
ADAM_B1 = 0.9
ADAM_B2 = 0.999
ADAM_EPS = 1e-08
ADAM_WD = 0.01
ADAM_STEP = 10

MESH = pl.DeviceIdType.MESH
_GELU_C = 0.7978845608028654
_GELU_A = 0.044715


def _cp(sem=None, vmem_mb=None):
    kw = {}
    if sem is not None:
        kw["dimension_semantics"] = sem
    if vmem_mb is not None:
        kw["vmem_limit_bytes"] = vmem_mb << 20
    return pltpu.CompilerParams(**kw)


def _dot(a, b):
    return jnp.dot(a, b, preferred_element_type=F32)


def _dot_nt(a, b):
    return lax.dot_general(a, b, (((1,), (1,)), ((), ())), preferred_element_type=F32)


def _dot_tn(a, b):
    return lax.dot_general(a, b, (((0,), (0,)), ((), ())), preferred_element_type=F32)


def _rms_r(x):
    return lax.rsqrt(jnp.mean(x * x, axis=-1, keepdims=True) + EPS)


def _rms_bwd(dh, n, r, g):
    dn = dh * g
    return r * (dn - n * jnp.mean(dn * n, axis=-1, keepdims=True))


def _gelu(x):
    t = jnp.tanh(_GELU_C * (x + _GELU_A * (x * x * x)))
    return 0.5 * x * (1.0 + t), t


def _gelu_grad(x, t):
    return 0.5 * (1.0 + t) + 0.5 * x * (1.0 - t * t) * (_GELU_C * (1.0 + 3.0 * _GELU_A * x * x))


def _sigmoid(x):
    return 1.0 / (1.0 + jnp.exp(-x))


def _tie(x, dep):
    return x if dep is None else lax.optimization_barrier((x, dep))[0]


def _row(tm, w):
    return pl.BlockSpec((tm, w), lambda i: (i, 0))


def _full(shape):
    nd = len(shape)
    return pl.BlockSpec(tuple(shape), lambda *_: (0,) * nd)


def _resident(shape):
    nd = len(shape)
    return pl.BlockSpec(tuple(shape), lambda *_: (0,) * nd, pipeline_mode=pl.Buffered(1))


def _sds(shape, dtype):
    return jax.ShapeDtypeStruct(tuple(shape), dtype)


HBM = pl.BlockSpec(memory_space=pltpu.HBM)
ANY = pl.BlockSpec(memory_space=pl.ANY)
SEM = pl.BlockSpec(memory_space=pltpu.SEMAPHORE)


def _band_buckets():
    i = np.arange(CHUNK)[:, None]
    j = np.arange(2 * CHUNK)[None, :]
    dist = i + CHUNK - j
    valid = (dist >= 0) & (dist < CHUNK)
    d = np.clip(dist, 0, None)
    max_exact = N_BUCKETS // 2
    large = max_exact + (np.log(np.maximum(d, 1) / max_exact) / np.log(MAX_DISTANCE / max_exact)
                         * (N_BUCKETS - max_exact)).astype(np.int32)
    large = np.minimum(large, N_BUCKETS - 1)
    buckets = np.where(d < max_exact, d, large).astype(np.int32)
    return np.where(valid, buckets, -1).astype(np.int32)


def _inproj(x2, g_mix, w_g, w_a, w_b, tm, after=None):
    T = x2.shape[0]
    order = [] if after is None else [after]

    def body(*refs):
        x_ref, g_ref, wg_ref, wa_ref, wb_ref = refs[:5]
        pg_ref, pa_ref, pb_ref, h_ref = refs[5 + len(order):]
        x = x_ref[...]
        h = (x * _rms_r(x) * g_ref[...]).astype(BF16)
        h_ref[...] = h
        pg_ref[...] = _dot(h, wg_ref[...]).astype(BF16)
        pa_ref[...] = _dot(h, wa_ref[...]).astype(BF16)
        pb_ref[...] = _dot(h, wb_ref[...]).astype(BF16)

    return pl.pallas_call(
        body, name="inproj", grid=(T // tm,),
        in_specs=[_row(tm, D_MODEL), _full(g_mix.shape), _resident(w_g.shape), _resident(w_a.shape), _resident(w_b.shape)]
        + [ANY] * len(order),
        out_specs=[_row(tm, G_DIM), _row(tm, A_DIM), _row(tm, B_DIM), _row(tm, D_MODEL)],
        out_shape=[_sds((T, G_DIM), BF16), _sds((T, A_DIM), BF16), _sds((T, B_DIM), BF16), _sds((T, D_MODEL), BF16)],
        compiler_params=_cp(("arbitrary",), 48),
    )(x2, g_mix, w_g, w_a, w_b, *order)


def _sgu_parts(p, g):
    pu = p[:, :A_WIDTH]
    pv = p[:, A_WIDTH:]
    u, tu = _gelu(pu)
    vv, tv = _gelu(pv)
    rv = _rms_r(vv)
    vn = (vv * rv * g).astype(BF16)
    return pu, pv, u, tu, vv, tv, rv, vn


def _tril():
    r = lax.broadcasted_iota(jnp.int32, (CHUNK, CHUNK), 0)
    c = lax.broadcasted_iota(jnp.int32, (CHUNK, CHUNK), 1)
    return r >= c


def _sgu_fwd(proj_a, g_sgu, w_s, b_st, tm):
    T = proj_a.shape[0]

    def body(p_ref, g_ref, ws_ref, bs_ref, y_ref):
        tril = _tril()
        _, _, u, _, _, _, _, vn = _sgu_parts(p_ref[...].astype(F32), g_ref[...])
        for gi in range(A_GROUPS):
            wm = jnp.where(tril, ws_ref[gi], 0.0).astype(BF16)
            bcol = bs_ref[:, gi:gi + 1]
            cs = slice(gi * CHUNK, (gi + 1) * CHUNK)
            for c in range(tm // CHUNK):
                rs = slice(c * CHUNK, (c + 1) * CHUNK)
                s = _dot(wm, vn[rs, cs]) + bcol
                y_ref[rs, cs] = (u[rs, cs] * s).astype(BF16)

    return pl.pallas_call(
        body, name="sgu_fwd", grid=(T // tm,),
        in_specs=[_row(tm, A_DIM), _full(g_sgu.shape), _full(w_s.shape), _full(b_st.shape)],
        out_specs=_row(tm, A_WIDTH), out_shape=_sds((T, A_WIDTH), BF16),
        compiler_params=_cp(("arbitrary",)),
    )(proj_a, g_sgu, w_s, b_st)


HEAD_ROWS = N_HEADS * CHUNK


def _head_rows(h):
    return slice(h * CHUNK, (h + 1) * CHUNK)


def _attn_setup(bias_scr, sink_scr, kvar_scr, qkv_ref, bk_ref, rel_ref, sink_ref):
    bk = bk_ref[...]
    for h in range(N_HEADS):
        acc = jnp.full((CHUNK, 2 * CHUNK), NEG_INF, F32)
        for b in range(N_BUCKETS):
            acc = jnp.where(bk == b, rel_ref[b, h], acc)
        bias_scr[_head_rows(h), :] = acc
        sink_scr[_head_rows(h), :] = jnp.full((CHUNK, LANES), sink_ref[0, h], F32)
    seq = qkv_ref.shape[0]
    rows_per = 2 * CHUNK
    for is_v in range(2):
        c0 = Q_DIM + is_v * KV_DIM
        for r in range(seq // rows_per):
            rs = slice(r * rows_per, (r + 1) * rows_per)
            a = qkv_ref[rs, c0:c0 + KV_DIM].astype(F32)
            lane = lax.broadcasted_iota(jnp.int32, a.shape, 1)
            lo = jnp.where(lane < HEAD_DIM, a, 0.0)
            hi = jnp.where(lane >= HEAD_DIM, a, 0.0)
            kvar_scr[4 * is_v + 0, rs, :] = lo.astype(BF16)
            kvar_scr[4 * is_v + 1, rs, :] = pltpu.roll(lo, HEAD_DIM, 1).astype(BF16)
            kvar_scr[4 * is_v + 2, rs, :] = pltpu.roll(hi, HEAD_DIM, 1).astype(BF16)
            kvar_scr[4 * is_v + 3, rs, :] = hi.astype(BF16)


def _rowsum(a, ones):
    hi = a.astype(BF16)
    lo = (a - hi.astype(F32)).astype(BF16)
    return _dot(hi, ones) + _dot(lo, ones)


def _both(a):
    return jnp.concatenate([a, a], axis=1)


def _attn_probs(qkv_ref, r0, n, kv, bias_scr, sink_scr, ones):
    s = jnp.concatenate([_dot_nt(qkv_ref[pl.ds(r0, CHUNK), (h // 2) * LANES:(h // 2 + 1) * LANES], kv[h // 4][h % 2])
                         for h in range(N_HEADS)], axis=0)
    s = s * (HEAD_DIM ** -0.5) + bias_scr[...]
    col = lax.broadcasted_iota(jnp.int32, s.shape, 1)
    s = jnp.where((col < CHUNK) & (n == 0), NEG_INF, s)
    sink = sink_scr[...]
    m = jnp.maximum(jnp.max(s, axis=-1, keepdims=True), sink)
    p = jnp.exp(s - _both(m))
    es = jnp.exp(sink - m)
    inv = 1.0 / (_rowsum(p, ones) + es)
    return p * _both(inv), es * inv


def _attn_block_inputs(kvar_scr, n):
    r0 = pl.multiple_of(n * CHUNK, CHUNK)
    rp = pl.multiple_of(jnp.maximum(n - 1, 0) * CHUNK, CHUNK)

    def both(idx):
        return jnp.concatenate([kvar_scr[idx, pl.ds(rp, CHUNK), :], kvar_scr[idx, pl.ds(r0, CHUNK), :]], axis=0)

    kv = ((both(0), both(1)), (both(2), both(3)))
    vv = ((both(4), both(5)), (both(6), both(7)))
    return r0, kv, vv


def _attn_fwd(proj_b, sinks, rel_bias, n_seq, seq):
    nb = seq // CHUNK
    bk = jnp.asarray(_band_buckets())

    def body(qkv_ref, bk_ref, rel_ref, sink_ref, o_ref, bias_scr, sink_scr, kvar_scr):
        _attn_setup(bias_scr, sink_scr, kvar_scr, qkv_ref, bk_ref, rel_ref, sink_ref)
        ones = jnp.ones((2 * CHUNK, LANES), BF16)

        def blk(n, carry):
            r0, kv, vv = _attn_block_inputs(kvar_scr, n)
            prob, _ = _attn_probs(qkv_ref, r0, n, kv, bias_scr, sink_scr, ones)
            pb = prob.astype(BF16)
            for pr in range(N_HEADS // 2):
                acc = _dot(pb[_head_rows(2 * pr)], vv[pr // 2][0]) + _dot(pb[_head_rows(2 * pr + 1)], vv[pr // 2][1])
                o_ref[pl.ds(r0, CHUNK), pr * LANES:(pr + 1) * LANES] = acc.astype(BF16)
            return carry

        lax.fori_loop(0, nb, blk, 0)

    smem = pl.BlockSpec(memory_space=pltpu.SMEM)
    return pl.pallas_call(
        body, name="attn_fwd", grid=(n_seq,),
        in_specs=[_row(seq, B_DIM), _full(bk.shape), smem, smem],
        out_specs=_row(seq, Q_DIM), out_shape=_sds((n_seq * seq, Q_DIM), BF16),
        scratch_shapes=[pltpu.VMEM((HEAD_ROWS, 2 * CHUNK), F32), pltpu.VMEM((HEAD_ROWS, LANES), F32),
                        pltpu.VMEM((8, seq, KV_DIM), BF16)],
        compiler_params=_cp(("arbitrary",), 40),
    )(proj_b, bk, rel_bias, sinks)


def _dot_stacked(a, w_ref):
    return jnp.concatenate([_dot(a, w_ref[i]) for i in range(N_CHIPS)], axis=1)


def _dot_nt_stacked(a, w_ref):
    w = w_ref.shape[2]
    acc = _dot_nt(a[:, :w], w_ref[0])
    for i in range(1, N_CHIPS):
        acc = acc + _dot_nt(a[:, i * w:(i + 1) * w], w_ref[i])
    return acc


def _merge_fwd(x2, y_a, y_b, proj_g, w_pa, w_pb, w_out, tm):
    T = x2.shape[0]

    def body(x_ref, ya_ref, yb_ref, g_ref, wpa_ref, wpb_ref, wo_ref, x1_ref, mg_ref):
        g = g_ref[...].astype(F32)
        pa = _dot_stacked(ya_ref[...], wpa_ref)
        pb = _dot_stacked(yb_ref[...], wpb_ref)
        merged = (_sigmoid(g[:, :D_MODEL]) * pa + _sigmoid(g[:, D_MODEL:]) * pb).astype(BF16)
        mg_ref[...] = merged
        x1_ref[...] = x_ref[...] + _dot(merged, wo_ref[...])

    return pl.pallas_call(
        body, name="merge_fwd", grid=(T // tm,),
        in_specs=[_row(tm, D_MODEL), _row(tm, A_WIDTH), _row(tm, Q_DIM), _row(tm, G_DIM),
                  _resident(w_pa.shape), _resident(w_pb.shape), _resident(w_out.shape)],
        out_specs=[_row(tm, D_MODEL), _row(tm, D_MODEL)],
        out_shape=[_sds((T, D_MODEL), F32), _sds((T, D_MODEL), BF16)],
        compiler_params=_cp(("arbitrary",), 40),
    )(x2, y_a, y_b, proj_g, w_pa, w_pb, w_out)


def _upproj(x1, g_ffn, w_up, w_conv, b_conv, tm, seq):
    T = x1.shape[0]
    cw = w_up.shape[2]
    tiles_per_seq = seq // tm

    def body(x_ref, g_ref, w_ref, wc_ref, bc_ref, u_ref, h_ref, gate_ref, val_ref, tail_scr):
        at_start = (pl.program_id(0) % tiles_per_seq) == 0
        x = x_ref[...]
        h = (x * _rms_r(x) * g_ref[...]).astype(BF16)
        h_ref[...] = h
        for i in range(N_CHIPS):
            cs = slice(i * cw, (i + 1) * cw)
            u = _dot(h, w_ref[i])
            u_ref[:, cs] = u.astype(BF16)
            hl = jnp.where(at_start, 0.0, tail_scr[SUBLANES - 2:SUBLANES, cs])
            tail_scr[:, cs] = u[tm - SUBLANES:]
            up = _conv_out((u, _shift_down(u, hl, 1), _shift_down(u, hl, 2)), wc_ref[:, cs], bc_ref[:, cs])
            out_ref = gate_ref if i < N_CHIPS // 2 else val_ref
            out_ref[:, (i % 2) * cw:(i % 2 + 1) * cw] = up.astype(BF16)

    return pl.pallas_call(
        body, name="upproj", grid=(T // tm,),
        in_specs=[_row(tm, D_MODEL), _full(g_ffn.shape), _resident(w_up.shape), _full(w_conv.shape), _full(b_conv.shape)],
        out_specs=[_row(tm, 2 * D_FF), _row(tm, D_MODEL), _row(tm, D_FF), _row(tm, D_FF)],
        out_shape=[_sds((T, 2 * D_FF), BF16), _sds((T, D_MODEL), BF16), _sds((T, D_FF), BF16), _sds((T, D_FF), BF16)],
        scratch_shapes=[pltpu.VMEM((SUBLANES, 2 * D_FF), F32)],
        compiler_params=_cp(("arbitrary",), 56),
    )(x1, g_ffn, w_up, w_conv, b_conv)


def _shift_down(u, halo, k):
    rolled = pltpu.roll(u, k, 0)
    head = rolled[:SUBLANES]
    row = lax.broadcasted_iota(jnp.int32, head.shape, 0)
    if k == 1:
        head = jnp.where(row == 0, halo[1:2], head)
    else:
        head = jnp.where(row == 0, halo[0:1], jnp.where(row == 1, halo[1:2], head))
    return jnp.concatenate([head, rolled[SUBLANES:]], axis=0)


def _shift_up(d, halo, k):
    tm = d.shape[0]
    rolled = pltpu.roll(d, tm - k, 0)
    tail = rolled[tm - SUBLANES:]
    row = lax.broadcasted_iota(jnp.int32, tail.shape, 0)
    if k == 1:
        tail = jnp.where(row == SUBLANES - 1, halo[0:1], tail)
    else:
        tail = jnp.where(row == SUBLANES - 2, halo[0:1], jnp.where(row == SUBLANES - 1, halo[1:2], tail))
    return jnp.concatenate([rolled[:tm - SUBLANES], tail], axis=0)


def _conv_out(taps, wc, bc):
    u, u1, u2 = taps
    return wc[0:1] * u2 + wc[1:2] * u1 + wc[2:3] * u + bc


def _ffn_down_loss(gate, val, x1, target, w_down, g_final, tm):
    T = x1.shape[0]
    half = D_FF // 2

    def body(gt_ref, vl_ref, x1_ref, t_ref, wd_ref, g_ref, dx2_ref, loss_ref, gg_ref):
        i = pl.program_id(0)
        acc = jnp.zeros((tm, D_MODEL), F32)
        for j in range(2):
            gc = slice(j * half, (j + 1) * half)
            gate = gt_ref[:, gc].astype(F32)
            act = (gate * _sigmoid(gate) * vl_ref[:, gc].astype(F32)).astype(BF16)
            acc = acc + _dot(act, wd_ref[gc, :])
        x2 = x1_ref[...] + acc
        r = _rms_r(x2)
        n = x2 * r
        g = g_ref[...]
        diff = n * g - t_ref[...]
        dy = diff * (1.0 / D_MODEL)
        dx2_ref[...] = _rms_bwd(dy, n, r, g)

        @pl.when(i == 0)
        def _():
            loss_ref[...] = jnp.zeros_like(loss_ref)
            gg_ref[...] = jnp.zeros_like(gg_ref)

        loss_ref[...] += 0.5 * jnp.sum(jnp.mean(diff * diff, axis=-1, keepdims=True), axis=0, keepdims=True)
        gg_ref[...] += jnp.sum(dy * n, axis=0, keepdims=True)

    return pl.pallas_call(
        body, name="ffn_down_loss", grid=(T // tm,),
        in_specs=[_row(tm, D_FF), _row(tm, D_FF), _row(tm, D_MODEL), _row(tm, D_MODEL),
                  _resident(w_down.shape), _full(g_final.shape)],
        out_specs=[_row(tm, D_MODEL), _full((1, 1)), _full((1, D_MODEL))],
        out_shape=[_sds((T, D_MODEL), F32), _sds((1, 1), F32), _sds((1, D_MODEL), F32)],
        compiler_params=_cp(("arbitrary",), 48),
    )(gate, val, x1, target, w_down, g_final)


def _ffn_bwd_act(gate, val, dx2, w_down, tm):
    T = dx2.shape[0]
    half = D_FF // 2
    nt = T // tm

    def body(g_ref, v_ref, dx_ref, wd_ref, dg_ref, dv_ref, gwd_out, gbg_ref, gbv_ref, gwd_ref):
        i = pl.program_id(1)
        gate = g_ref[...].astype(F32)
        val = v_ref[...].astype(F32)
        sg = _sigmoid(gate)
        silu = gate * sg
        dx = dx_ref[...].astype(BF16)
        d_act = _dot_nt(dx, wd_ref[...])
        d_val = d_act * silu
        d_gate = d_act * val * (sg * (1.0 + gate * (1.0 - sg)))
        dg_ref[...] = d_gate.astype(BF16)
        dv_ref[...] = d_val.astype(BF16)

        @pl.when(i == 0)
        def _():
            for r in (gwd_ref, gbg_ref, gbv_ref):
                r[...] = jnp.zeros_like(r)

        gwd_ref[...] += _dot_tn((silu * val).astype(BF16), dx)
        gbg_ref[...] += jnp.sum(d_gate, axis=0, keepdims=True)
        gbv_ref[...] += jnp.sum(d_val, axis=0, keepdims=True)

        @pl.when(i == nt - 1)
        def _():
            gwd_out[...] = gwd_ref[...].astype(BF16)

    tile = pl.BlockSpec((tm, half), lambda j, i: (i, j))
    vec = pl.BlockSpec((1, half), lambda j, i: (0, j))
    wrows = pl.BlockSpec((half, D_MODEL), lambda j, i: (j, 0))
    return pl.pallas_call(
        body, name="ffn_bwd_act", grid=(2, nt),
        in_specs=[tile, tile, pl.BlockSpec((tm, D_MODEL), lambda j, i: (i, 0)), wrows],
        out_specs=[tile, tile, wrows, vec, vec],
        out_shape=[_sds((T, D_FF), BF16), _sds((T, D_FF), BF16), _sds((D_FF, D_MODEL), BF16),
                   _sds((1, D_FF), F32), _sds((1, D_FF), F32)],
        scratch_shapes=[pltpu.VMEM((half, D_MODEL), F32)],
        compiler_params=_cp(("arbitrary", "arbitrary"), 56),
    )(gate, val, dx2, w_down)


def _ffn_bwd_up(d_gate, d_val, upre, dx2, x1, g_ffn, w_conv, w_up, tm, seq):
    T = dx2.shape[0]
    tiles_per_seq = seq // tm
    k16 = tm // BF16_ROWS
    n16 = T // BF16_ROWS
    cw = D_FF // 2

    def body(dg_ref, dv_ref, hg_ref, hv_ref, u_ref, dx2_ref, x1_ref, g_ref, wc_ref, wu_ref, du_ref, dx1_ref, gg_ref, gwc_ref):
        i = pl.program_id(0)
        at_end = (i % tiles_per_seq) == tiles_per_seq - 1

        @pl.when(i == 0)
        def _():
            gg_ref[...] = jnp.zeros_like(gg_ref)
            gwc_ref[...] = jnp.zeros_like(gwc_ref)

        dh = jnp.zeros((tm, D_MODEL), F32)
        for j in range(4):
            src, hsrc = (dg_ref, hg_ref) if j < 2 else (dv_ref, hv_ref)
            ls = slice((j % 2) * cw, (j % 2 + 1) * cw)
            cs = slice(j * cw, (j + 1) * cw)
            d = src[:, ls].astype(F32)
            hl = hsrc[:, ls].astype(F32)[0:2]
            hl = jnp.where(at_end, 0.0, hl)
            wc = wc_ref[:, cs]
            d1 = _shift_up(d, hl, 1)
            d2 = _shift_up(d, hl, 2)
            du = (wc[2:3] * d + wc[1:2] * d1 + wc[0:1] * d2).astype(BF16)
            du_ref[:, cs] = du
            dh = dh + _dot_nt(du, wu_ref[j])
            u = u_ref[:, cs].astype(F32)
            gwc_ref[0:1, cs] += jnp.sum(d2 * u, axis=0, keepdims=True)
            gwc_ref[1:2, cs] += jnp.sum(d1 * u, axis=0, keepdims=True)
            gwc_ref[2:3, cs] += jnp.sum(d * u, axis=0, keepdims=True)
        x = x1_ref[...]
        r = _rms_r(x)
        n = x * r
        dx1_ref[...] = dx2_ref[...] + _rms_bwd(dh, n, r, g_ref[...])
        gg_ref[...] += jnp.sum(dh * n, axis=0, keepdims=True)

    nxt = pl.BlockSpec((BF16_ROWS, D_FF), lambda i: (jnp.minimum((i + 1) * k16, n16 - 1), 0))
    return pl.pallas_call(
        body, name="ffn_bwd_up", grid=(T // tm,),
        in_specs=[_row(tm, D_FF), _row(tm, D_FF), nxt, nxt, _row(tm, 2 * D_FF), _row(tm, D_MODEL), _row(tm, D_MODEL),
                  _full(g_ffn.shape), _full(w_conv.shape), _resident(w_up.shape)],
        out_specs=[_row(tm, 2 * D_FF), _row(tm, D_MODEL), _full((1, D_MODEL)), _full((3, 2 * D_FF))],
        out_shape=[_sds((T, 2 * D_FF), BF16), _sds((T, D_MODEL), F32), _sds((1, D_MODEL), F32), _sds((3, 2 * D_FF), F32)],
        compiler_params=_cp(("arbitrary",), 56),
    )(d_gate, d_val, d_gate, d_val, upre, dx2, x1, g_ffn, w_conv, w_up)


def _matmul_tn(a, b, tn, tk, name):
    T, M = a.shape
    N = b.shape[1]
    nk = T // tk

    def body(a_ref, b_ref, o_ref, acc_ref):
        k = pl.program_id(1)

        @pl.when(k == 0)
        def _():
            acc_ref[...] = jnp.zeros_like(acc_ref)

        acc_ref[...] += _dot_tn(a_ref[...], b_ref[...])

        @pl.when(k == nk - 1)
        def _():
            o_ref[...] = acc_ref[...].astype(BF16)

    return pl.pallas_call(
        body, name=name, grid=(N // tn, nk),
        in_specs=[pl.BlockSpec((tk, M), lambda j, k: (k, 0)), pl.BlockSpec((tk, tn), lambda j, k: (k, j))],
        out_specs=pl.BlockSpec((M, tn), lambda j, k: (0, j)), out_shape=_sds((M, N), BF16),
        scratch_shapes=[pltpu.VMEM((M, tn), F32)],
        compiler_params=_cp(("arbitrary", "arbitrary"), 48),
    )(a, b)


def _merge_bwd(dx1, merged, y_a, y_b, proj_g, w_pa, w_pb, w_out, tm, after=None):
    T = dx1.shape[0]

    nt = T // tm
    pshape = (A_WIDTH, D_MODEL)
    order = [] if after is None else [after]

    def body(*refs):
        dx_ref, mg_ref, ya_ref, yb_ref, g_ref, wpa_ref, wpb_ref, wo_ref = refs[:8]
        dg_ref, dya_ref, dyb_ref, gwo_out, gwpa_out, gwpb_out, gwo_ref, gwpa_ref, gwpb_ref = refs[8 + len(order):]
        i = pl.program_id(0)
        dx = dx_ref[...].astype(BF16)
        dm = _dot_nt(dx, wo_ref[...])
        g = g_ref[...].astype(F32)
        ya = ya_ref[...]
        yb = yb_ref[...]
        pa = _dot_stacked(ya, wpa_ref)
        pb = _dot_stacked(yb, wpb_ref)
        sa = _sigmoid(g[:, :D_MODEL])
        sb = _sigmoid(g[:, D_MODEL:])
        dpa = (dm * sa).astype(BF16)
        dpb = (dm * sb).astype(BF16)
        dg_ref[:, :D_MODEL] = (dm * pa * (sa * (1.0 - sa))).astype(BF16)
        dg_ref[:, D_MODEL:] = (dm * pb * (sb * (1.0 - sb))).astype(BF16)
        dya_ref[...] = _dot_nt_stacked(dpa, wpa_ref).astype(BF16)
        dyb_ref[...] = _dot_nt_stacked(dpb, wpb_ref).astype(BF16)

        @pl.when(i == 0)
        def _():
            for r in (gwo_ref, gwpa_ref, gwpb_ref):
                r[...] = jnp.zeros_like(r)

        gwo_ref[...] += _dot_tn(mg_ref[...], dx)
        gwpa_ref[...] += _dot_tn(ya, dpa)
        gwpb_ref[...] += _dot_tn(yb, dpb)

        @pl.when(i == nt - 1)
        def _():
            gwo_out[...] = gwo_ref[...].astype(BF16)
            gwpa_out[...] = gwpa_ref[...].astype(BF16)
            gwpb_out[...] = gwpb_ref[...].astype(BF16)

    return pl.pallas_call(
        body, name="merge_bwd", grid=(nt,),
        in_specs=[_row(tm, D_MODEL), _row(tm, D_MODEL), _row(tm, A_WIDTH), _row(tm, Q_DIM), _row(tm, G_DIM),
                  _resident(w_pa.shape), _resident(w_pb.shape), _resident(w_out.shape)] + [ANY] * len(order),
        out_specs=[_row(tm, G_DIM), _row(tm, A_WIDTH), _row(tm, Q_DIM),
                   _full(w_out.shape), _full(pshape), _full(pshape)],
        out_shape=[_sds((T, G_DIM), BF16), _sds((T, A_WIDTH), BF16), _sds((T, Q_DIM), BF16),
                   _sds(w_out.shape, BF16), _sds(pshape, BF16), _sds(pshape, BF16)],
        scratch_shapes=[pltpu.VMEM(w_out.shape, F32), pltpu.VMEM(pshape, F32), pltpu.VMEM(pshape, F32)],
        compiler_params=_cp(("arbitrary",), 56),
    )(dx1, merged, y_a, y_b, proj_g, w_pa, w_pb, w_out, *order)


def _sgu_bwd(proj_a, d_ya, g_sgu, w_s, b_st, tm, after=None):
    T = proj_a.shape[0]
    order = [] if after is None else [after]

    def body(*refs):
        p_ref, dy_ref, g_ref, ws_ref, bs_ref = refs[:5]
        dp_ref, gws_ref, gbs_ref, gg_ref = refs[5 + len(order):]
        tril = _tril()
        g = g_ref[...]
        pu, pv, u, tu, vv, tv, rv, vn = _sgu_parts(p_ref[...].astype(F32), g)
        dy = dy_ref[...].astype(F32)

        @pl.when(pl.program_id(0) == 0)
        def _():
            for r in (gws_ref, gbs_ref, gg_ref):
                r[...] = jnp.zeros_like(r)

        du_cols = []
        dvn_cols = []
        for gi in range(A_GROUPS):
            wm = jnp.where(tril, ws_ref[gi], 0.0).astype(BF16)
            wmt = wm.astype(F32).T.astype(BF16)
            bcol = bs_ref[:, gi:gi + 1]
            cs = slice(gi * CHUNK, (gi + 1) * CHUNK)
            du_rows = []
            dvn_rows = []
            gw = jnp.zeros((CHUNK, CHUNK), F32)
            gb = jnp.zeros((CHUNK, 1), F32)
            for c in range(tm // CHUNK):
                rs = slice(c * CHUNK, (c + 1) * CHUNK)
                vn_c = vn[rs, cs]
                s = _dot(wm, vn_c) + bcol
                dy_c = dy[rs, cs]
                ds = dy_c * u[rs, cs]
                du_rows.append(dy_c * s)
                dsb = ds.astype(BF16)
                gw = gw + _dot_nt(dsb, vn_c)
                gb = gb + jnp.sum(ds, axis=-1, keepdims=True)
                dvn_rows.append(_dot(wmt, dsb))
            gws_ref[gi] += jnp.where(tril, gw, 0.0)
            gbs_ref[:, gi:gi + 1] += gb
            du_cols.append(jnp.concatenate(du_rows, axis=0))
            dvn_cols.append(jnp.concatenate(dvn_rows, axis=0))
        du = jnp.concatenate(du_cols, axis=1)
        dvn = jnp.concatenate(dvn_cols, axis=1)
        vhat = vv * rv
        gg_ref[...] += jnp.sum(dvn * vhat, axis=0, keepdims=True)
        dvv = _rms_bwd(dvn, vhat, rv, g)
        dp_ref[:, :A_WIDTH] = (du * _gelu_grad(pu, tu)).astype(BF16)
        dp_ref[:, A_WIDTH:] = (dvv * _gelu_grad(pv, tv)).astype(BF16)

    return pl.pallas_call(
        body, name="sgu_bwd", grid=(T // tm,),
        in_specs=[_row(tm, A_DIM), _row(tm, A_WIDTH), _full(g_sgu.shape), _full(w_s.shape), _full(b_st.shape)] + [ANY] * len(order),
        out_specs=[_row(tm, A_DIM), _full(w_s.shape), _full(b_st.shape), _full(g_sgu.shape)],
        out_shape=[_sds((T, A_DIM), BF16), _sds(w_s.shape, F32), _sds(b_st.shape, F32), _sds(g_sgu.shape, F32)],
        compiler_params=_cp(("arbitrary",)),
    )(proj_a, d_ya, g_sgu, w_s, b_st, *order)


def _attn_bwd(proj_b, d_yb, sinks, rel_bias, n_seq, seq):
    nb = seq // CHUNK
    bk = jnp.asarray(_band_buckets())

    def body(qkv_ref, do_ref, bk_ref, rel_ref, sink_ref, d_ref, gs_ref, gr_ref,
             bias_scr, sink_scr, kvar_scr, dbias_scr, dk_scr, dv_scr, ds_scr):
        b = pl.program_id(0)
        _attn_setup(bias_scr, sink_scr, kvar_scr, qkv_ref, bk_ref, rel_ref, sink_ref)
        ones = jnp.ones((2 * CHUNK, LANES), BF16)

        @pl.when(b == 0)
        def _():
            dbias_scr[...] = jnp.zeros_like(dbias_scr)
            ds_scr[...] = jnp.zeros_like(ds_scr)

        dk_scr[...] = jnp.zeros_like(dk_scr)
        dv_scr[...] = jnp.zeros_like(dv_scr)

        def transposed(a):
            return a.astype(F32).T.astype(BF16)

        def blk(n, carry):
            r0, kv, vv = _attn_block_inputs(kvar_scr, n)
            prob, psink = _attn_probs(qkv_ref, r0, n, kv, bias_scr, sink_scr, ones)
            dp = jnp.concatenate([_dot_nt(do_ref[pl.ds(r0, CHUNK), (h // 2) * LANES:(h // 2 + 1) * LANES], vv[h // 4][h % 2])
                                  for h in range(N_HEADS)], axis=0)
            delta = _rowsum(prob * dp, ones)
            dsc = prob * (dp - _both(delta))
            ds_scr[...] += psink * delta
            dbias_scr[...] += dsc
            dsb = (dsc * (HEAD_DIM ** -0.5)).astype(BF16)
            pb = prob.astype(BF16)
            dkt = [jnp.zeros((HEAD_DIM, 2 * CHUNK), F32) for _ in range(2)]
            dvt = [jnp.zeros((HEAD_DIM, 2 * CHUNK), F32) for _ in range(2)]
            for pr in range(N_HEADS // 2):
                ps = slice(pr * LANES, (pr + 1) * LANES)
                qpt = transposed(qkv_ref[pl.ds(r0, CHUNK), ps])
                dopt = transposed(do_ref[pl.ds(r0, CHUNK), ps])
                kvh = pr // 2
                dq = jnp.zeros((CHUNK, LANES), F32)
                for hh in range(2):
                    hr = _head_rows(2 * pr + hh)
                    rows = slice(hh * HEAD_DIM, (hh + 1) * HEAD_DIM)
                    dq = dq + _dot(dsb[hr], kv[kvh][hh])
                    dkt[kvh] = dkt[kvh] + _dot(qpt, dsb[hr])[rows]
                    dvt[kvh] = dvt[kvh] + _dot(dopt, pb[hr])[rows]
                d_ref[pl.ds(r0, CHUNK), ps] = dq.astype(BF16)
            dk_scr[:, pl.ds(r0, 2 * CHUNK)] += jnp.concatenate(dkt, axis=0)
            dv_scr[:, pl.ds(r0, 2 * CHUNK)] += jnp.concatenate(dvt, axis=0)
            return carry

        lax.fori_loop(0, nb, blk, 0)
        for n in range(nb):
            rows = slice(n * CHUNK, (n + 1) * CHUNK)
            cols = slice((n + 1) * CHUNK, (n + 2) * CHUNK)
            d_ref[rows, Q_DIM:Q_DIM + KV_DIM] = dk_scr[:, cols].T.astype(BF16)
            d_ref[rows, Q_DIM + KV_DIM:] = dv_scr[:, cols].T.astype(BF16)

        @pl.when(b == n_seq - 1)
        def _():
            bkv = bk_ref[...]
            for h in range(N_HEADS):
                gs_ref[0:1, h:h + 1] = -jnp.sum(ds_scr[_head_rows(h), 0:1], axis=0, keepdims=True)
                db = dbias_scr[_head_rows(h), :]
                for bb in range(N_BUCKETS):
                    part = jnp.sum(jnp.where(bkv == bb, db, 0.0), axis=-1, keepdims=True)
                    gr_ref[bb:bb + 1, h:h + 1] = jnp.sum(part, axis=0, keepdims=True)

    smem = pl.BlockSpec(memory_space=pltpu.SMEM)
    return pl.pallas_call(
        body, name="attn_bwd", grid=(n_seq,),
        in_specs=[_row(seq, B_DIM), _row(seq, Q_DIM), _full(bk.shape), smem, smem],
        out_specs=[_row(seq, B_DIM), _full((1, N_HEADS)), _full((N_BUCKETS, N_HEADS))],
        out_shape=[_sds((n_seq * seq, B_DIM), BF16), _sds((1, N_HEADS), F32), _sds((N_BUCKETS, N_HEADS), F32)],
        scratch_shapes=[pltpu.VMEM((HEAD_ROWS, 2 * CHUNK), F32), pltpu.VMEM((HEAD_ROWS, LANES), F32),
                        pltpu.VMEM((8, seq, KV_DIM), BF16), pltpu.VMEM((HEAD_ROWS, 2 * CHUNK), F32),
                        pltpu.VMEM((KV_DIM, seq + CHUNK), F32), pltpu.VMEM((KV_DIM, seq + CHUNK), F32),
                        pltpu.VMEM((HEAD_ROWS, LANES), F32)],
        compiler_params=_cp(("arbitrary",), 40),
    )(proj_b, d_yb, bk, rel_bias, sinks)


def _inproj_bwd(d_g, d_a, d_b, x2, dx1, g_mix, w_g, w_a, w_b, tm, after=None):
    T = x2.shape[0]
    order = [] if after is None else [after]

    def body(*refs):
        dg_ref, da_ref, db_ref, x_ref, dx1_ref, g_ref, wg_ref, wa_ref, wb_ref = refs[:9]
        gx_ref, gg_ref = refs[9 + len(order):]
        dh = _dot_nt(dg_ref[...], wg_ref[...]) + _dot_nt(da_ref[...], wa_ref[...]) + _dot_nt(db_ref[...], wb_ref[...])
        x = x_ref[...]
        r = _rms_r(x)
        n = x * r
        gx_ref[...] = dx1_ref[...] + _rms_bwd(dh, n, r, g_ref[...])

        @pl.when(pl.program_id(0) == 0)
        def _():
            gg_ref[...] = jnp.zeros_like(gg_ref)

        gg_ref[...] += jnp.sum(dh * n, axis=0, keepdims=True)

    return pl.pallas_call(
        body, name="inproj_bwd", grid=(T // tm,),
        in_specs=[_row(tm, G_DIM), _row(tm, A_DIM), _row(tm, B_DIM), _row(tm, D_MODEL), _row(tm, D_MODEL),
                  _full(g_mix.shape), _resident(w_g.shape), _resident(w_a.shape), _resident(w_b.shape)] + [ANY] * len(order),
        out_specs=[_row(tm, D_MODEL), _full((1, D_MODEL))],
        out_shape=[_sds((T, D_MODEL), F32), _sds((1, D_MODEL), F32)],
        compiler_params=_cp(("arbitrary",), 48),
    )(d_g, d_a, d_b, x2, dx1, g_mix, w_g, w_a, w_b, *order)


def _local_step(x, target, g_mix, g_sgu, w_s, b_s, sinks, rel_bias, g_ffn, b_conv, g_final,
                w_g, w_a, w_b, w_conv, late_weights, on_grads, after=None):
    n_seq, seq, _ = x.shape
    T = n_seq * seq
    tm = min(ROW_TILE, seq)
    tw = min(GRAD_ROW_TILE, T)
    tf = min(WIDE_ROW_TILE, seq)
    x2 = x.reshape(T, D_MODEL)
    tgt = target.reshape(T, D_MODEL)
    b_st = b_s.T
    g_fin = g_final.reshape(1, D_MODEL)

    proj_g, proj_a, proj_b, h = _inproj(x2, g_mix, w_g, w_a, w_b, tm, after)
    y_a = _sgu_fwd(proj_a, g_sgu, w_s, b_st, tm)
    y_b = _attn_fwd(proj_b, sinks, rel_bias, n_seq, seq)
    w_pa, w_pb, w_out, w_up, w_down = late_weights(y_b)
    x1, merged = _merge_fwd(x2, y_a, y_b, proj_g, w_pa, w_pb, w_out, tm)
    upre, h2, gate, val = _upproj(x1, g_ffn, w_up, w_conv, b_conv, tf, seq)
    dx2, loss, gg_final = _ffn_down_loss(gate, val, x1, tgt, w_down, g_fin, tm)

    d_gate, d_val, gw_down, gb_g, gb_v = _ffn_bwd_act(gate, val, dx2, w_down, tw)
    gb_conv = jnp.concatenate([gb_g, gb_v], axis=1)
    d_upre, dx1, gg_ffn, gw_conv = _ffn_bwd_up(d_gate, d_val, upre, dx2, x1, g_ffn, w_conv, w_up, tf, seq)
    gw_up = _matmul_tn(h2, d_upre, 2 * D_FF // 4, min(2 * GRAD_ROW_TILE, T), "grad_w_up")
    sent = on_grads("ffn", dict(w_up=gw_up, w_down=gw_down))
    d_g, d_ya, d_yb, gw_out, gw_pa, gw_pb = _merge_bwd(dx1, merged, y_a, y_b, proj_g, w_pa, w_pb, w_out, tf, sent)
    sent = on_grads("proj", dict(w_pa=gw_pa, w_pb=gw_pb, w_out=gw_out))
    d_a, gw_s, gb_st, gg_sgu = _sgu_bwd(proj_a, d_ya, g_sgu, w_s, b_st, tm, sent)
    d_b, g_sinks, g_rel = _attn_bwd(proj_b, _tie(d_yb, d_a), sinks, rel_bias, n_seq, seq)
    gw_g = _matmul_tn(h, _tie(d_g, d_b), D_MODEL, min(2 * GRAD_ROW_TILE, T), "grad_w_in_gate")
    gw_a = _matmul_tn(h, _tie(d_a, gw_g), A_DIM, min(2 * GRAD_ROW_TILE, T), "grad_w_in_a")
    gw_b = _matmul_tn(h, _tie(d_b, gw_a), B_DIM, min(2 * GRAD_ROW_TILE, T), "grad_w_in_b")
    gw_in = jnp.concatenate([gw_a, gw_b, gw_g], axis=1).reshape(D_MODEL, N_CHIPS, -1).transpose(1, 0, 2)
    sent = on_grads("in", dict(w_in=gw_in))
    grad_x, gg_mix = _inproj_bwd(d_g, d_a, d_b, x2, dx1, g_mix, w_g, w_a, w_b, tm, sent)

    small = dict(g_mix=gg_mix, g_sgu=gg_sgu, w_s=gw_s, b_s=gb_st.T, sinks=g_sinks, rel_bias=g_rel,
                 g_ffn=gg_ffn, b_conv=gb_conv, g_final=gg_final, w_conv=gw_conv)
    big = dict(w_in=gw_in, w_pa=gw_pa, w_pb=gw_pb, w_out=gw_out, w_up=gw_up, w_down=gw_down)
    return loss, grad_x.reshape(x.shape), small, big


_MIXER = ("w_in", "w_pa", "w_pb", "w_out")
_FFN = ("w_up", "w_down")
_BIG = _MIXER + _FFN

_SMALL = (("loss", (1, 1)), ("g_final", (1, D_MODEL)), ("g_mix", (1, D_MODEL)), ("g_ffn", (1, D_MODEL)),
          ("g_sgu", (1, A_WIDTH)), ("b_s", (A_GROUPS, CHUNK)), ("sinks", (1, N_HEADS)), ("rel_bias", (N_BUCKETS, N_HEADS)),
          ("b_conv", (1, 2 * D_FF)), ("w_conv", (3, 2 * D_FF)), ("w_s", (A_GROUPS, CHUNK, CHUNK)))
SMALL_ROWS = 96


def _pack_small(vals):
    flat = jnp.concatenate([vals[n].astype(F32).reshape(-1) for n, _ in _SMALL])
    flat = jnp.pad(flat, (0, SMALL_ROWS * D_MODEL - flat.shape[0]))
    return flat.reshape(SMALL_ROWS, D_MODEL)


def _unpack_small(buf):
    flat = buf.reshape(-1)
    out = {}
    off = 0
    for n, shp in _SMALL:
        k = int(np.prod(shp))
        out[n] = flat[off:off + k].reshape(shp)
        off += k
    return out


def _mesh_pos():
    return lax.axis_index("x"), lax.axis_index("y"), lax.axis_index("c")


def _other_chips(x, y):
    return [(1 - x, y), (x, 1 - y), (1 - x, 1 - y)]


def _remote(src, dst, send_sem, recv_sem, to):
    return pltpu.make_async_remote_copy(src_ref=src, dst_ref=dst, send_sem=send_sem, recv_sem=recv_sem,
                                        device_id=to, device_id_type=MESH)


def _own_slot(own, n, at):
    return lax.dynamic_update_slice(lax.empty((n,) + own.shape, own.dtype), own[None], (at,) + (0,) * own.ndim)


def _allgather_weights(stacks, wc_stack):
    names = list(stacks)
    n = len(names)

    def body(*refs):
        ins, outs = refs[:n + 1], refs[n + 1:2 * n + 2]
        send_sems, recv_sems = refs[2 * n + 2:]
        x, y, c = _mesh_pos()
        me = 2 * x + y
        sibling = (x, y, 1 - c)
        chips = _other_chips(x, y)

        def half(ref, chip, hc):
            hr = ref.shape[1] // 2
            return ref.at[chip, pl.ds(hc * hr, hr), :]

        first = []
        for k in range(n):
            first += [_remote(half(ins[k], me, c), half(outs[k], me, c), send_sems.at[6 * k + j], recv_sems.at[6 * k + j], (cx, cy, c))
                      for j, (cx, cy) in enumerate(chips)]
        first += [_remote(ins[n].at[me], outs[n].at[me], send_sems.at[6 * n + j], recv_sems.at[6 * n + j], (cx, cy, c))
                  for j, (cx, cy) in enumerate(chips)]
        for cp in first:
            cp.start()
        passed = []
        for k in range(n):
            for j, (cx, cy) in enumerate(chips):
                landed = half(outs[k], 2 * cx + cy, c)
                _remote(landed, landed, send_sems.at[6 * k + j], recv_sems.at[6 * k + j], (x, y, c)).wait_recv()
                passed.append(_remote(landed, landed, send_sems.at[6 * k + 3 + j], recv_sems.at[6 * k + 3 + j], sibling))
                passed[-1].start()
        for k in range(n):
            for j, (cx, cy) in enumerate(chips):
                theirs = half(outs[k], 2 * cx + cy, 1 - c)
                _remote(theirs, theirs, send_sems.at[6 * k + 3 + j], recv_sems.at[6 * k + 3 + j], (x, y, c)).wait_recv()
        for j, (cx, cy) in enumerate(chips):
            slot = outs[n].at[2 * cx + cy]
            _remote(slot, slot, send_sems.at[6 * n + j], recv_sems.at[6 * n + j], (x, y, c)).wait_recv()
        for cp in first + passed:
            cp.wait_send()

    arrays = [stacks[k] for k in names] + [wc_stack]
    outs = pl.pallas_call(
        body, name="allgather_weights",
        in_specs=[HBM] * (n + 1), out_specs=[HBM] * (n + 1), input_output_aliases={k: k for k in range(n + 1)},
        out_shape=[_sds(a.shape, a.dtype) for a in arrays],
        scratch_shapes=[pltpu.SemaphoreType.DMA((6 * n + 3,)), pltpu.SemaphoreType.DMA((6 * n + 3,))],
    )(*arrays)
    return dict(zip(names, outs[:n])), outs[n]


_KIND = {"w_in": "stack", "w_pa": "col", "w_pb": "col", "w_up": "col", "w_out": "row", "w_down": "row"}


def _half_view(ref, kind, h):
    if kind == "stack":
        k = ref.shape[1] // 2
        return ref.at[:, pl.ds(h * k, k), :]
    if kind == "col":
        k = ref.shape[0] // 2
        return ref.at[pl.ds(h * k, k), :]
    k = ref.shape[1] // 2
    return ref.at[:, pl.ds(h * k, k)]


def _shard_view(ref, kind, i):
    if kind == "stack":
        return ref.at[i]
    if kind == "col":
        k = ref.shape[1] // N_CHIPS
        return ref.at[:, pl.ds(i * k, k)]
    k = ref.shape[0] // N_CHIPS
    return ref.at[pl.ds(i * k, k), :]


def _region_view(ref, kind, h):
    if kind == "row":
        k = ref.shape[1] // 2
        return ref.at[:, pl.ds(h * k, k)]
    k = ref.shape[0] // 2
    return ref.at[pl.ds(h * k, k), :]


def _half_shape(shape, kind):
    if kind == "stack":
        return (shape[0], shape[1] // 2, shape[2])
    return (shape[0] // 2, shape[1]) if kind == "col" else (shape[0], shape[1] // 2)


def _part_shape(half_shape, kind):
    if kind == "stack":
        return tuple(half_shape[1:])
    k, w = half_shape
    return (k, w // N_CHIPS) if kind == "col" else (k // N_CHIPS, w)


_DATAFLOW = pltpu.SideEffectType.DATAFLOW_SIDE_EFFECTING
_TOKEN = (SUBLANES, LANES)


def _split_start(name, arrays, n_sems, issue, after=None):
    n = len(arrays)
    order = [] if after is None else [after]

    def body(*refs):
        base = n + len(order)
        issue(refs[:n], refs[base], refs[base + 1])
        refs[-1][...] = jnp.zeros(_TOKEN, F32)

    outs = pl.pallas_call(
        body, name=name,
        in_specs=[HBM] * n + [ANY] * len(order), out_specs=[SEM, SEM] + [HBM] * n + [pl.BlockSpec(memory_space=pltpu.VMEM)],
        out_shape=[pltpu.SemaphoreType.DMA((n_sems,)), pltpu.SemaphoreType.DMA((n_sems,))]
        + [pltpu.HBM(a.shape, a.dtype) for a in arrays] + [_sds(_TOKEN, F32)],
        input_output_aliases={k: 2 + k for k in range(n)},
        compiler_params=pltpu.CompilerParams(has_side_effects=_DATAFLOW),
    )(*[pltpu.with_memory_space_constraint(a, pltpu.HBM) for a in arrays], *order)
    return outs[0], outs[1], list(outs[2:2 + n]), outs[-1]


def _split_wait(name, started, waits, after):
    send_sems, recv_sems, arrays, _ = started
    n = len(arrays)

    def body(*refs):
        waits(refs[:n], refs[n], refs[n + 1])

    return pl.pallas_call(
        body, name=name,
        in_specs=[HBM] * n + [SEM, SEM, ANY], out_specs=[HBM] * n,
        out_shape=[pltpu.HBM(a.shape, a.dtype) for a in arrays],
        input_output_aliases={k: k for k in range(n)},
        compiler_params=pltpu.CompilerParams(has_side_effects=_DATAFLOW),
    )(*arrays, send_sems, recv_sems, after)


def _wait_both(src, dst, send_sem, recv_sem):
    x, y, c = _mesh_pos()
    cp = _remote(src, dst, send_sem, recv_sem, (x, y, c))
    cp.wait_send()
    cp.wait_recv()


def _pair_exchange_start(parts, tag, after):
    names = list(parts)
    n = len(names)
    lands = [lax.empty(_half_shape(parts[k].shape, _KIND[k]), parts[k].dtype) for k in names]

    def issue(refs, send_sems, recv_sems):
        x, y, c = _mesh_pos()
        for hc in range(2):
            @pl.when(c == hc)
            def _():
                for k in range(n):
                    _remote(_half_view(refs[k], _KIND[names[k]], 1 - hc), refs[n + k], send_sems.at[k], recv_sems.at[k],
                            (x, y, 1 - c)).start()

    return names, _split_start("grad_pair_exchange_start_" + tag, [parts[k] for k in names] + lands, n, issue, after)


def _pair_exchange_wait(pending, tag, after):
    names, started = pending
    n = len(names)

    def waits(refs, send_sems, recv_sems):
        for k in range(n):
            _wait_both(_half_view(refs[k], _KIND[names[k]], 0), refs[n + k], send_sems.at[k], recv_sems.at[k])

    outs = _split_wait("grad_pair_exchange_wait_" + tag, started, waits, after)
    return dict(zip(names, outs[:n])), dict(zip(names, outs[n:]))


def _half_blocks(shape, kind):
    if kind == "stack":
        _, k, w = shape
        tr = 256
        nb = k // 2 // tr
        return (N_CHIPS, nb), (1, tr, w), (lambda i, r, s: (i, r, 0)), (lambda i, r, s: (i, s[1] * nb + r, 0))
    k, w = shape
    if kind == "col":
        tr = 256 if w <= 2 * D_MODEL else 128
        nb = k // 2 // tr
        return (nb,), (tr, w), (lambda r, s: (r, 0)), (lambda r, s: (s[1] * nb + r, 0))
    tr = k // N_CHIPS
    return (N_CHIPS,), (tr, w // 2), (lambda r, s: (r, 0)), (lambda r, s: (r, s[1]))


def _pair_add(part, from_sibling, name, pos):
    kind = _KIND[name]
    grid, block, half_map, full_map = _half_blocks(part.shape, kind)

    def body(s_ref, p_ref, q_ref, o_ref):
        o_ref[...] = (p_ref[...].astype(F32) + q_ref[...].astype(F32)).astype(BF16)

    return pl.pallas_call(
        body, name="grad_pair_add_" + name,
        grid_spec=pltpu.PrefetchScalarGridSpec(
            num_scalar_prefetch=1, grid=grid,
            in_specs=[pl.BlockSpec(block, full_map), pl.BlockSpec(block, half_map)],
            out_specs=pl.BlockSpec(block, half_map)),
        out_shape=_sds(from_sibling.shape, BF16),
        compiler_params=_cp(("arbitrary",) * len(grid)),
    )(pos, part, from_sibling)


def _chip_exchange_start(sums, tag, after):
    names = list(sums)
    n = len(names)
    lands = [lax.empty((3,) + _part_shape(sums[k].shape, _KIND[k]), sums[k].dtype) for k in names]

    def issue(refs, send_sems, recv_sems):
        x, y, c = _mesh_pos()
        me = 2 * x + y
        for i in range(N_CHIPS):
            xi, yi = i // 2, i % 2
            j = jnp.where(xi != x, jnp.where(yi != y, 2, 0), 1)

            @pl.when(i != me)
            def _():
                for k in range(n):
                    _remote(_shard_view(refs[k], _KIND[names[k]], i), refs[n + k].at[j], send_sems.at[3 * k + j],
                            recv_sems.at[3 * k + j], (xi, yi, c)).start()

    return names, _split_start("grad_chip_exchange_start_" + tag, [sums[k] for k in names] + lands, 3 * n, issue, after)


def _chip_exchange_wait(pending, tag, after):
    names, started = pending
    n = len(names)

    def waits(refs, send_sems, recv_sems):
        for k in range(n):
            for j in range(3):
                _wait_both(_shard_view(refs[k], _KIND[names[k]], 0), refs[n + k].at[j], send_sems.at[3 * k + j], recv_sems.at[3 * k + j])

    return dict(zip(names, _split_wait("grad_chip_exchange_wait_" + tag, started, waits, after)[n:]))


def _allgather_start(stacks, after):
    names = list(stacks)

    def issue(refs, send_sems, recv_sems):
        x, y, c = _mesh_pos()
        me = 2 * x + y
        for k, st in enumerate(refs):
            hr = st.shape[1] // 2
            mine = st.at[me, pl.ds(c * hr, hr), :]
            for j, (cx, cy) in enumerate(_other_chips(x, y)):
                _remote(mine, mine, send_sems.at[3 * k + j], recv_sems.at[3 * k + j], (cx, cy, c)).start()

    return names, _split_start("allgather_start", [stacks[k] for k in names], 3 * len(names), issue, after)


def _allgather_wait(pending, after):
    names, started = pending

    def waits(refs, send_sems, recv_sems):
        for k, st in enumerate(refs):
            slot = st.at[0, pl.ds(0, st.shape[1] // 2), :]
            for j in range(3):
                _wait_both(slot, slot, send_sems.at[3 * k + j], recv_sems.at[3 * k + j])

    return dict(zip(names, _split_wait("allgather_wait", started, waits, after)))


def _allgather_forward(stacks):
    names = list(stacks)
    n = len(names)

    def body(*refs):
        ins, outs = refs[:n], refs[n:2 * n]
        send_sems, recv_sems = refs[2 * n:]
        x, y, c = _mesh_pos()
        copies = []
        for k in range(n):
            hr = ins[k].shape[1] // 2
            for j, (cx, cy) in enumerate(_other_chips(x, y)):
                chip = 2 * cx + cy
                copies.append(_remote(ins[k].at[chip, pl.ds(c * hr, hr), :], outs[k].at[chip, pl.ds(c * hr, hr), :],
                                      send_sems.at[3 * k + j], recv_sems.at[3 * k + j], (x, y, 1 - c)))
        for cp in copies:
            cp.start()
        for cp in copies:
            cp.wait()

    arrays = [stacks[k] for k in names]
    outs = pl.pallas_call(
        body, name="allgather_forward", in_specs=[HBM] * n, out_specs=[HBM] * n,
        input_output_aliases={k: k for k in range(n)},
        out_shape=[_sds(a.shape, a.dtype) for a in arrays],
        scratch_shapes=[pltpu.SemaphoreType.DMA((3 * n,)), pltpu.SemaphoreType.DMA((3 * n,))],
    )(*arrays)
    return dict(zip(names, outs))


def _owner_sum(part, from_sibling, from_chips, name, pos, shard_shape):
    kind = _KIND[name]
    _, pk, pw = from_chips.shape
    if kind == "row":
        tr, nb = pk, 1
        p_spec = pl.BlockSpec((tr, pw), lambda r, s: (s[0], s[1]))
        q_spec = pl.BlockSpec((tr, pw), lambda r, s: (s[0], 0))
        o_spec = pl.BlockSpec((tr, pw), lambda r, s: (0, s[1]))
    else:
        tr = 256
        nb = pk // tr
        if kind == "stack":
            p_spec = pl.BlockSpec((None, tr, pw), lambda r, s: (s[0], s[1] * nb + r, 0))
            q_spec = pl.BlockSpec((None, tr, pw), lambda r, s: (s[0], r, 0))
        else:
            p_spec = pl.BlockSpec((tr, pw), lambda r, s: (s[1] * nb + r, s[0]))
            q_spec = pl.BlockSpec((tr, pw), lambda r, s: (r, s[0]))
        o_spec = pl.BlockSpec((tr, pw), lambda r, s: (s[1] * nb + r, 0))

    def body(s_ref, p_ref, q_ref, r_ref, o_ref):
        acc = p_ref[...].astype(F32) + q_ref[...].astype(F32)
        for j in range(3):
            acc = acc + r_ref[j].astype(F32)
        o_ref[...] = acc

    return pl.pallas_call(
        body, name="grad_owner_sum_" + name,
        grid_spec=pltpu.PrefetchScalarGridSpec(
            num_scalar_prefetch=1, grid=(nb,),
            in_specs=[p_spec, q_spec, pl.BlockSpec((3, tr, pw), lambda r, s: (0, r, 0))],
            out_specs=o_spec),
        out_shape=_sds(shard_shape, F32),
        compiler_params=_cp(("arbitrary",), 32),
    )(pos, part, from_sibling, from_chips)


def _pair_share_start(shards, tag, after):
    names = list(shards)

    def issue(refs, send_sems, recv_sems):
        x, y, c = _mesh_pos()
        for hc in range(2):
            @pl.when(c == hc)
            def _():
                for k, g in enumerate(refs):
                    mine = _region_view(g, _KIND[names[k]], hc)
                    _remote(mine, mine, send_sems.at[k], recv_sems.at[k], (x, y, 1 - c)).start()

    return names, _split_start("grad_pair_share_start_" + tag, [shards[k] for k in names], len(names), issue, after)


def _pair_share_wait(pending, tag, after):
    names, started = pending

    def waits(refs, send_sems, recv_sems):
        for k, g in enumerate(refs):
            region = _region_view(g, _KIND[names[k]], 0)
            _wait_both(region, region, send_sems.at[k], recv_sems.at[k])

    return dict(zip(names, _split_wait("grad_pair_share_wait_" + tag, started, waits, after)))


def _small_exchange_start(slots, after):
    def issue(refs, send_sems, recv_sems):
        x, y, c = _mesh_pos()
        mine = refs[0].at[4 * x + 2 * y + c]
        k = 0
        for px in range(2):
            for py in range(2):
                for pc in range(2):
                    if px + py + pc:
                        peer = (1 - x if px else x, 1 - y if py else y, 1 - c if pc else c)
                        _remote(mine, mine, send_sems.at[k], recv_sems.at[k], peer).start()
                        k += 1

    return _split_start("small_exchange_start", [slots], N_DEV - 1, issue, after)


def _small_exchange_wait(started, after):
    def waits(refs, send_sems, recv_sems):
        slot = refs[0].at[0]
        for k in range(N_DEV - 1):
            _wait_both(slot, slot, send_sems.at[k], recv_sems.at[k])

    return _split_wait("small_exchange_wait", started, waits, after)[0]


def _adam_math(w, g, m, v):
    m = ADAM_B1 * m + (1.0 - ADAM_B1) * g
    v = ADAM_B2 * v + (1.0 - ADAM_B2) * (g * g)
    m_hat = m / (1.0 - ADAM_B1 ** ADAM_STEP)
    v_hat = v / (1.0 - ADAM_B2 ** ADAM_STEP)
    delta = -ADAM_LR * (m_hat / (jnp.sqrt(v_hat) + ADAM_EPS) + ADAM_WD * w)
    return delta, m, v


def _adamw(w, g, m, v, name):
    rows, cols = w.shape
    tr = rows
    for cand in (256, 128, 64, 32, 16, 8):
        if rows % cand == 0 and rows > cand:
            tr = cand
            break

    def body(w_ref, g_ref, m_ref, v_ref, d_ref, nm_ref, nv_ref):
        d, nm, nv = _adam_math(w_ref[...], g_ref[...], m_ref[...], v_ref[...])
        d_ref[...] = d
        nm_ref[...] = nm
        nv_ref[...] = nv

    spec = pl.BlockSpec((tr, cols), lambda i: (i, 0))
    return pl.pallas_call(
        body, name=name, grid=(rows // tr,), in_specs=[spec] * 4, out_specs=[spec] * 3,
        out_shape=[_sds(w.shape, F32)] * 3, compiler_params=_cp(("arbitrary",)),
    )(w, g, m, v)


def _small_sum_adamw(gathered, w, m, v):
    def body(a_ref, w_ref, m_ref, v_ref, g_ref, d_ref, nm_ref, nv_ref):
        g = a_ref[0]
        for k in range(1, N_DEV):
            g = g + a_ref[k]
        g_ref[...] = g
        d, nm, nv = _adam_math(w_ref[...], g, m_ref[...], v_ref[...])
        d_ref[...] = d
        nm_ref[...] = nm
        nv_ref[...] = nv

    return pl.pallas_call(
        body, name="small_sum_adamw", out_shape=[_sds(w.shape, F32)] * 4,
    )(gathered, w, m, v)


_NAMES = ("g_mix", "w_in", "g_sgu", "w_s", "b_s", "sinks", "rel_bias", "w_pa", "w_pb", "w_out",
          "g_ffn", "w_up", "w_conv", "b_conv", "w_down", "g_final")

def kernel(x, g_mix, w_in, g_sgu, w_s, b_s, sinks, rel_bias, w_pa, w_pb, w_out, g_ffn, w_up, w_conv, b_conv, w_down, g_final, loss_target, m_g_mix, m_w_in, m_g_sgu, m_w_s, m_b_s, m_sinks, m_rel_bias, m_w_pa, m_w_pb, m_w_out, m_g_ffn, m_w_up, m_w_conv, m_b_conv, m_w_down, m_g_final, v_g_mix, v_w_in, v_g_sgu, v_w_s, v_b_s, v_sinks, v_rel_bias, v_w_pa, v_w_pb, v_w_out, v_g_ffn, v_w_up, v_w_conv, v_b_conv, v_w_down, v_g_final):
    w = dict(g_mix=g_mix, w_in=w_in, g_sgu=g_sgu, w_s=w_s, b_s=b_s, sinks=sinks, rel_bias=rel_bias, w_pa=w_pa, w_pb=w_pb,
             w_out=w_out, g_ffn=g_ffn, w_up=w_up, w_conv=w_conv, b_conv=b_conv, w_down=w_down, g_final=g_final)
    m = dict(g_mix=m_g_mix, w_in=m_w_in, g_sgu=m_g_sgu, w_s=m_w_s, b_s=m_b_s, sinks=m_sinks, rel_bias=m_rel_bias, w_pa=m_w_pa,
             w_pb=m_w_pb, w_out=m_w_out, g_ffn=m_g_ffn, w_up=m_w_up, w_conv=m_w_conv, b_conv=m_b_conv, w_down=m_w_down,
             g_final=m_g_final)
    v = dict(g_mix=v_g_mix, w_in=v_w_in, g_sgu=v_g_sgu, w_s=v_w_s, b_s=v_b_s, sinks=v_sinks, rel_bias=v_rel_bias, w_pa=v_w_pa,
             w_pb=v_w_pb, w_out=v_w_out, g_ffn=v_g_ffn, w_up=v_w_up, w_conv=v_w_conv, b_conv=v_b_conv, w_down=v_w_down,
             g_final=v_g_final)
    xi, yi, ci = _mesh_pos()
    me = 2 * xi + yi

    shard = {n: w[n][0] for n in _BIG}
    shard_shapes = {n: shard[n].shape for n in _BIG}
    wc_shard = w["w_conv"][0]
    wc_pad = jnp.pad(wc_shard, ((0, 5), (0, 0)))
    own = {n: _own_slot(shard[n].astype(BF16), N_CHIPS, me) for n in _BIG}
    stacks, wc_all = _allgather_weights({"w_in": own["w_in"]}, _own_slot(wc_pad, N_CHIPS, me))
    late_gather = _allgather_start({n: own[n] for n in _BIG[1:]}, stacks["w_in"])
    w_conv_full = jnp.concatenate([wc_all[i, :3] for i in range(N_CHIPS)], axis=1)
    w_in_full = stacks["w_in"].transpose(1, 0, 2).reshape(D_MODEL, -1)
    w_a = w_in_full[:, :A_DIM]
    w_b = w_in_full[:, A_DIM:A_DIM + B_DIM]
    w_g = w_in_full[:, A_DIM + B_DIM:]
    pos = jnp.stack([me, ci])

    def late_weights(done):
        st = _allgather_forward(_allgather_wait(late_gather, done))
        return st["w_pa"], st["w_pb"], st["w_out"].reshape(D_MODEL, D_MODEL), st["w_up"], st["w_down"].reshape(D_FF, D_MODEL)

    groups = {}

    def stage1(group, parts):
        groups[group] = dict(parts=parts, pair=_pair_exchange_start(parts, group, None))
        return groups[group]["pair"][1][-1]

    def stage2(group, after, order_after):
        g = groups[group]
        g["parts"], g["sib"] = _pair_exchange_wait(g["pair"], group, after)
        g["chip"] = _chip_exchange_start({n: _pair_add(g["parts"][n], g["sib"][n], n, pos) for n in g["parts"]}, group, order_after)
        return g["chip"][1][-1]

    def stage3(group, after, order_after):
        g = groups[group]
        got = _chip_exchange_wait(g["chip"], group, after)
        g["share"] = _pair_share_start(
            {n: _owner_sum(g["parts"][n], g["sib"][n], got[n], n, pos, shard_shapes[n]) for n in g["parts"]}, group, order_after)
        return g["share"][1][-1]

    grads, deltas, new_m, new_v = {}, {}, {}, {}

    def stage4(group, after):
        g_shard = _pair_share_wait(groups[group]["share"], group, after)
        last = None
        for n in g_shard:
            g = _tie(g_shard[n], last)
            if n == "w_in":
                gt = g.T
                d, nm, nv = _adamw(shard[n].T, gt, m[n][0].T, v[n][0].T, "adamw_" + n)
                grads[n], deltas[n], new_m[n], new_v[n] = gt.T[None], d.T[None], nm.T[None], nv.T[None]
            else:
                d, nm, nv = _adamw(shard[n], g, m[n][0], v[n][0], "adamw_" + n)
                grads[n], deltas[n], new_m[n], new_v[n] = g[None], d[None], nm[None], nv[None]
            last = nv
        return last

    def on_grads(group, parts):
        token = stage1(group, parts)
        some = next(iter(parts.values()))
        if group == "proj":
            token = stage2("ffn", some, token)
        if group == "in":
            token = stage2("proj", some, token)
            token = stage3("ffn", some, token)
            token = stage2("in", token, token)
        return token

    loss, grad_x, small, big = _local_step(
        x, loss_target, w["g_mix"], w["g_sgu"], w["w_s"][0], w["b_s"][0], w["sinks"], w["rel_bias"], w["g_ffn"],
        w["b_conv"], w["g_final"], w_g, w_a, w_b, w_conv_full, late_weights, on_grads, late_gather[1][-1])

    small["loss"] = loss
    small_gather = _small_exchange_start(_own_slot(_pack_small(small), N_DEV, 2 * me + ci), grad_x)
    token = stage3("proj", grad_x, small_gather[-1])
    done = stage4("ffn", token)
    done = stage4("proj", done)
    token = stage3("in", done, None)
    sw = {n: (jnp.zeros((1, 1), F32) if n in ("loss", "w_conv") else w[n]) for n, _ in _SMALL}
    sm = {n: (jnp.zeros((1, 1), F32) if n in ("loss", "w_conv") else m[n]) for n, _ in _SMALL}
    sv = {n: (jnp.zeros((1, 1), F32) if n in ("loss", "w_conv") else v[n]) for n, _ in _SMALL}
    for d in (sw, sm, sv):
        d["w_conv"] = jnp.zeros((3, 2 * D_FF), F32)
    all_small = _small_exchange_wait(small_gather, token)
    s_g, s_d, s_m, s_v = [_unpack_small(a) for a in _small_sum_adamw(all_small, _pack_small(sw), _pack_small(sm), _pack_small(sv))]
    stage4("in", all_small)
    wcols = wc_shard.shape[1]
    g_wc = lax.dynamic_slice(s_g["w_conv"], (0, me * wcols), (3, wcols))
    d, nm, nv = _adamw(wc_shard, g_wc, m["w_conv"][0], v["w_conv"][0], "adamw_w_conv")
    grads["w_conv"], deltas["w_conv"], new_m["w_conv"], new_v["w_conv"] = g_wc[None], d[None], nm[None], nv[None]
    for n, _ in _SMALL:
        if n in ("loss", "w_conv"):
            continue
        shp = w[n].shape
        grads[n], deltas[n], new_m[n], new_v[n] = (s_g[n].reshape(shp), s_d[n].reshape(shp), s_m[n].reshape(shp),
                                                    s_v[n].reshape(shp))

    return (s_g["loss"].reshape(()), grad_x, *[grads[n] for n in _NAMES], *[deltas[n] for n in _NAMES],
            *[new_m[n] for n in _NAMES], *[new_v[n] for n in _NAMES])
```

```python
import functools

import numpy as np
import jax
import jax.numpy as jnp
from jax import lax
from jax.experimental import pallas as pl
from jax.experimental.pallas import tpu as pltpu

F32 = jnp.float32
BF16 = jnp.bfloat16

D_MODEL = 1024
CHUNK = 128
A_GROUPS = 4
A_WIDTH = 512
N_HEADS = 8
HEAD_DIM = 64
Q_DIM = 512
KV_DIM = 128
N_BUCKETS = 32
MAX_DISTANCE = 128
D_FF = 2816
EPS = 1e-6
NEG_INF = -1e30
G_DIM = 2 * D_MODEL
A_DIM = 2 * A_WIDTH
B_DIM = Q_DIM + 2 * KV_DIM
LANES = 128
SUBLANES = 8
ROW_TILE = 512
WIDE_ROW_TILE = 256
GRAD_ROW_TILE = 512
BF16_ROWS = 16
N_CHIPS = 4
N_DEV = 8

ADAM_LR = 0.001
ADAM_B1 = 0.9
ADAM_B2 = 0.999
ADAM_EPS = 1e-08
ADAM_WD = 0.01
ADAM_STEP = 10

MESH = pl.DeviceIdType.MESH
_GELU_C = 0.7978845608028654
_GELU_A = 0.044715


def _cp(sem=None, vmem_mb=None):
    kw = {}
    if sem is not None:
        kw["dimension_semantics"] = sem
    if vmem_mb is not None:
        kw["vmem_limit_bytes"] = vmem_mb << 20
    return pltpu.CompilerParams(**kw)


def _dot(a, b):
    return jnp.dot(a, b, preferred_element_type=F32)


def _dot_nt(a, b):
    return lax.dot_general(a, b, (((1,), (1,)), ((), ())), preferred_element_type=F32)


def _dot_tn(a, b):
    return lax.dot_general(a, b, (((0,), (0,)), ((), ())), preferred_element_type=F32)


def _rms_r(x):
    return lax.rsqrt(jnp.mean(x * x, axis=-1, keepdims=True) + EPS)


def _rms_bwd(dh, n, r, g):
    dn = dh * g
    return r * (dn - n * jnp.mean(dn * n, axis=-1, keepdims=True))


def _gelu(x):
    t = jnp.tanh(_GELU_C * (x + _GELU_A * (x * x * x)))
    return 0.5 * x * (1.0 + t), t


def _gelu_grad(x, t):
    return 0.5 * (1.0 + t) + 0.5 * x * (1.0 - t * t) * (_GELU_C * (1.0 + 3.0 * _GELU_A * x * x))


def _sigmoid(x):
    return 1.0 / (1.0 + jnp.exp(-x))


def _tie(x, dep):
    return x if dep is None else lax.optimization_barrier((x, dep))[0]


def _row(tm, w):
    return pl.BlockSpec((tm, w), lambda i: (i, 0))


def _full(shape):
    nd = len(shape)
    return pl.BlockSpec(tuple(shape), lambda *_: (0,) * nd)


def _resident(shape):
    nd = len(shape)
    return pl.BlockSpec(tuple(shape), lambda *_: (0,) * nd, pipeline_mode=pl.Buffered(1))


def _sds(shape, dtype):
    return jax.ShapeDtypeStruct(tuple(shape), dtype)


def _hbm(*arrays):
    return [pltpu.with_memory_space_constraint(a, pltpu.HBM) for a in arrays]


HBM = pl.BlockSpec(memory_space=pltpu.HBM)
ANY = pl.BlockSpec(memory_space=pl.ANY)
SEM = pl.BlockSpec(memory_space=pltpu.SEMAPHORE)


def _band_buckets():
    i = np.arange(CHUNK)[:, None]
    j = np.arange(2 * CHUNK)[None, :]
    dist = i + CHUNK - j
    valid = (dist >= 0) & (dist < CHUNK)
    d = np.clip(dist, 0, None)
    max_exact = N_BUCKETS // 2
    large = max_exact + (np.log(np.maximum(d, 1) / max_exact) / np.log(MAX_DISTANCE / max_exact)
                         * (N_BUCKETS - max_exact)).astype(np.int32)
    large = np.minimum(large, N_BUCKETS - 1)
    buckets = np.where(d < max_exact, d, large).astype(np.int32)
    return np.where(valid, buckets, -1).astype(np.int32)


_A_COLS = slice(0, A_DIM)
_B_COLS = slice(A_DIM, A_DIM + B_DIM)
_G_COLS = slice(A_DIM + B_DIM, A_DIM + B_DIM + G_DIM)


def _inproj(x2, g_mix, w_in, tm, after=None):
    T = x2.shape[0]
    order = [] if after is None else [after]

    def body(*refs):
        x_ref, g_ref, w_ref = refs[:3]
        pg_ref, pa_ref, pb_ref, h_ref = refs[3 + len(order):]
        x = x_ref[...]
        h = (x * _rms_r(x) * g_ref[...]).astype(BF16)
        h_ref[...] = h
        pg_ref[...] = _dot(h, w_ref[:, _G_COLS]).astype(BF16)
        pa_ref[...] = _dot(h, w_ref[:, _A_COLS]).astype(BF16)
        pb_ref[...] = _dot(h, w_ref[:, _B_COLS]).astype(BF16)

    return pl.pallas_call(
        body, name="inproj", grid=(T // tm,),
        in_specs=[_row(tm, D_MODEL), _full(g_mix.shape), _resident(w_in.shape)] + [ANY] * len(order),
        out_specs=[_row(tm, G_DIM), _row(tm, A_DIM), _row(tm, B_DIM), _row(tm, D_MODEL)],
        out_shape=[_sds((T, G_DIM), BF16), _sds((T, A_DIM), BF16), _sds((T, B_DIM), BF16), _sds((T, D_MODEL), BF16)],
        compiler_params=_cp(("arbitrary",), 48),
    )(*_hbm(x2, g_mix, w_in), *order)


def _sgu_parts(p, g):
    pu = p[:, :A_WIDTH]
    pv = p[:, A_WIDTH:]
    u, tu = _gelu(pu)
    vv, tv = _gelu(pv)
    rv = _rms_r(vv)
    vn = (vv * rv * g).astype(BF16)
    return pu, pv, u, tu, vv, tv, rv, vn


def _tril():
    r = lax.broadcasted_iota(jnp.int32, (CHUNK, CHUNK), 0)
    c = lax.broadcasted_iota(jnp.int32, (CHUNK, CHUNK), 1)
    return r >= c


def _sgu_fwd(proj_a, g_sgu, w_s, b_st, tm):
    T = proj_a.shape[0]

    def body(p_ref, g_ref, ws_ref, bs_ref, y_ref):
        tril = _tril()
        _, _, u, _, _, _, _, vn = _sgu_parts(p_ref[...].astype(F32), g_ref[...])
        for gi in range(A_GROUPS):
            wm = jnp.where(tril, ws_ref[gi], 0.0).astype(BF16)
            bcol = bs_ref[:, gi:gi + 1]
            cs = slice(gi * CHUNK, (gi + 1) * CHUNK)
            for c in range(tm // CHUNK):
                rs = slice(c * CHUNK, (c + 1) * CHUNK)
                s = _dot(wm, vn[rs, cs]) + bcol
                y_ref[rs, cs] = (u[rs, cs] * s).astype(BF16)

    return pl.pallas_call(
        body, name="sgu_fwd", grid=(T // tm,),
        in_specs=[_row(tm, A_DIM), _full(g_sgu.shape), _full(w_s.shape), _full(b_st.shape)],
        out_specs=_row(tm, A_WIDTH), out_shape=_sds((T, A_WIDTH), BF16),
        compiler_params=_cp(("arbitrary",)),
    )(*_hbm(proj_a, g_sgu, w_s, b_st))


HEAD_ROWS = N_HEADS * CHUNK


def _head_rows(h):
    return slice(h * CHUNK, (h + 1) * CHUNK)


def _attn_setup(bias_scr, sink_scr, kvar_scr, qkv_ref, bk_ref, rel_ref, sink_ref):
    bk = bk_ref[...]
    for h in range(N_HEADS):
        acc = jnp.full((CHUNK, 2 * CHUNK), NEG_INF, F32)
        for b in range(N_BUCKETS):
            acc = jnp.where(bk == b, rel_ref[b, h], acc)
        bias_scr[_head_rows(h), :] = acc
        sink_scr[_head_rows(h), :] = jnp.full((CHUNK, LANES), sink_ref[0, h], F32)
    seq = qkv_ref.shape[0]
    rows_per = 2 * CHUNK
    for is_v in range(2):
        c0 = Q_DIM + is_v * KV_DIM
        for r in range(seq // rows_per):
            rs = slice(r * rows_per, (r + 1) * rows_per)
            a = qkv_ref[rs, c0:c0 + KV_DIM].astype(F32)
            lane = lax.broadcasted_iota(jnp.int32, a.shape, 1)
            lo = jnp.where(lane < HEAD_DIM, a, 0.0)
            hi = jnp.where(lane >= HEAD_DIM, a, 0.0)
            kvar_scr[4 * is_v + 0, rs, :] = lo.astype(BF16)
            kvar_scr[4 * is_v + 1, rs, :] = pltpu.roll(lo, HEAD_DIM, 1).astype(BF16)
            kvar_scr[4 * is_v + 2, rs, :] = pltpu.roll(hi, HEAD_DIM, 1).astype(BF16)
            kvar_scr[4 * is_v + 3, rs, :] = hi.astype(BF16)


def _rowsum(a, ones):
    hi = a.astype(BF16)
    lo = (a - hi.astype(F32)).astype(BF16)
    return _dot(hi, ones) + _dot(lo, ones)


def _both(a):
    return jnp.concatenate([a, a], axis=1)


def _attn_probs(qkv_ref, r0, n, kv, bias_scr, sink_scr, ones):
    s = jnp.concatenate([_dot_nt(qkv_ref[pl.ds(r0, CHUNK), (h // 2) * LANES:(h // 2 + 1) * LANES], kv[h // 4][h % 2])
                         for h in range(N_HEADS)], axis=0)
    s = s * (HEAD_DIM ** -0.5) + bias_scr[...]
    col = lax.broadcasted_iota(jnp.int32, s.shape, 1)
    s = jnp.where((col < CHUNK) & (n == 0), NEG_INF, s)
    sink = sink_scr[...]
    m = jnp.maximum(jnp.max(s, axis=-1, keepdims=True), sink)
    p = jnp.exp(s - _both(m))
    es = jnp.exp(sink - m)
    inv = 1.0 / (_rowsum(p, ones) + es)
    return p * _both(inv), es * inv


def _attn_block_inputs(kvar_scr, n):
    r0 = pl.multiple_of(n * CHUNK, CHUNK)
    rp = pl.multiple_of(jnp.maximum(n - 1, 0) * CHUNK, CHUNK)

    def both(idx):
        return jnp.concatenate([kvar_scr[idx, pl.ds(rp, CHUNK), :], kvar_scr[idx, pl.ds(r0, CHUNK), :]], axis=0)

    kv = ((both(0), both(1)), (both(2), both(3)))
    vv = ((both(4), both(5)), (both(6), both(7)))
    return r0, kv, vv


def _attn_fwd(proj_b, sinks, rel_bias, n_seq, seq):
    nb = seq // CHUNK
    bk = jnp.asarray(_band_buckets())

    def body(qkv_ref, bk_ref, rel_ref, sink_ref, o_ref, bias_scr, sink_scr, kvar_scr):
        _attn_setup(bias_scr, sink_scr, kvar_scr, qkv_ref, bk_ref, rel_ref, sink_ref)
        ones = jnp.ones((2 * CHUNK, LANES), BF16)

        def blk(n, carry):
            r0, kv, vv = _attn_block_inputs(kvar_scr, n)
            prob, _ = _attn_probs(qkv_ref, r0, n, kv, bias_scr, sink_scr, ones)
            pb = prob.astype(BF16)
            for pr in range(N_HEADS // 2):
                acc = _dot(pb[_head_rows(2 * pr)], vv[pr // 2][0]) + _dot(pb[_head_rows(2 * pr + 1)], vv[pr // 2][1])
                o_ref[pl.ds(r0, CHUNK), pr * LANES:(pr + 1) * LANES] = acc.astype(BF16)
            return carry

        lax.fori_loop(0, nb, blk, 0)

    smem = pl.BlockSpec(memory_space=pltpu.SMEM)
    return pl.pallas_call(
        body, name="attn_fwd", grid=(n_seq,),
        in_specs=[_row(seq, B_DIM), _full(bk.shape), smem, smem],
        out_specs=_row(seq, Q_DIM), out_shape=_sds((n_seq * seq, Q_DIM), BF16),
        scratch_shapes=[pltpu.VMEM((HEAD_ROWS, 2 * CHUNK), F32), pltpu.VMEM((HEAD_ROWS, LANES), F32),
                        pltpu.VMEM((8, seq, KV_DIM), BF16)],
        compiler_params=_cp(("arbitrary",), 40),
    )(*_hbm(proj_b, bk), rel_bias, sinks)


def _dot_stacked(a, w_ref):
    return jnp.concatenate([_dot(a, w_ref[i]) for i in range(N_CHIPS)], axis=1)


def _dot_nt_stacked(a, w_ref):
    w = w_ref.shape[2]
    acc = _dot_nt(a[:, :w], w_ref[0])
    for i in range(1, N_CHIPS):
        acc = acc + _dot_nt(a[:, i * w:(i + 1) * w], w_ref[i])
    return acc


def _merge_fwd(x2, y_a, y_b, proj_g, w_pa, w_pb, w_out, tm):
    T = x2.shape[0]

    def body(x_ref, ya_ref, yb_ref, g_ref, wpa_ref, wpb_ref, wo_ref, x1_ref, mg_ref):
        g = g_ref[...].astype(F32)
        pa = _dot_stacked(ya_ref[...], wpa_ref)
        pb = _dot_stacked(yb_ref[...], wpb_ref)
        merged = (_sigmoid(g[:, :D_MODEL]) * pa + _sigmoid(g[:, D_MODEL:]) * pb).astype(BF16)
        mg_ref[...] = merged
        x1_ref[...] = x_ref[...] + _dot(merged, wo_ref[...])

    return pl.pallas_call(
        body, name="merge_fwd", grid=(T // tm,),
        in_specs=[_row(tm, D_MODEL), _row(tm, A_WIDTH), _row(tm, Q_DIM), _row(tm, G_DIM),
                  _resident(w_pa.shape), _resident(w_pb.shape), _resident(w_out.shape)],
        out_specs=[_row(tm, D_MODEL), _row(tm, D_MODEL)],
        out_shape=[_sds((T, D_MODEL), F32), _sds((T, D_MODEL), BF16)],
        compiler_params=_cp(("arbitrary",), 40),
    )(*_hbm(x2, y_a, y_b, proj_g, w_pa, w_pb, w_out))


def _upproj(x1, g_ffn, w_up, w_conv, b_conv, tm, seq):
    T = x1.shape[0]
    cw = w_up.shape[2]
    tiles_per_seq = seq // tm

    def body(x_ref, g_ref, w_ref, wc_ref, bc_ref, u_ref, h_ref, gate_ref, val_ref, tail_scr):
        at_start = (pl.program_id(0) % tiles_per_seq) == 0
        x = x_ref[...]
        h = (x * _rms_r(x) * g_ref[...]).astype(BF16)
        h_ref[...] = h
        for i in range(N_CHIPS):
            cs = slice(i * cw, (i + 1) * cw)
            u = _dot(h, w_ref[i])
            u_ref[:, cs] = u.astype(BF16)
            hl = jnp.where(at_start, 0.0, tail_scr[SUBLANES - 2:SUBLANES, cs])
            tail_scr[:, cs] = u[tm - SUBLANES:]
            up = _conv_out((u, _shift_down(u, hl, 1), _shift_down(u, hl, 2)), wc_ref[:, cs], bc_ref[:, cs])
            out_ref = gate_ref if i < N_CHIPS // 2 else val_ref
            out_ref[:, (i % 2) * cw:(i % 2 + 1) * cw] = up.astype(BF16)

    return pl.pallas_call(
        body, name="upproj", grid=(T // tm,),
        in_specs=[_row(tm, D_MODEL), _full(g_ffn.shape), _resident(w_up.shape), _full(w_conv.shape), _full(b_conv.shape)],
        out_specs=[_row(tm, 2 * D_FF), _row(tm, D_MODEL), _row(tm, D_FF), _row(tm, D_FF)],
        out_shape=[_sds((T, 2 * D_FF), BF16), _sds((T, D_MODEL), BF16), _sds((T, D_FF), BF16), _sds((T, D_FF), BF16)],
        scratch_shapes=[pltpu.VMEM((SUBLANES, 2 * D_FF), F32)],
        compiler_params=_cp(("arbitrary",), 56),
    )(*_hbm(x1, g_ffn, w_up, w_conv, b_conv))


def _shift_down(u, halo, k):
    rolled = pltpu.roll(u, k, 0)
    head = rolled[:SUBLANES]
    row = lax.broadcasted_iota(jnp.int32, head.shape, 0)
    if k == 1:
        head = jnp.where(row == 0, halo[1:2], head)
    else:
        head = jnp.where(row == 0, halo[0:1], jnp.where(row == 1, halo[1:2], head))
    return jnp.concatenate([head, rolled[SUBLANES:]], axis=0)


def _shift_up(d, halo, k):
    tm = d.shape[0]
    rolled = pltpu.roll(d, tm - k, 0)
    tail = rolled[tm - SUBLANES:]
    row = lax.broadcasted_iota(jnp.int32, tail.shape, 0)
    if k == 1:
        tail = jnp.where(row == SUBLANES - 1, halo[0:1], tail)
    else:
        tail = jnp.where(row == SUBLANES - 2, halo[0:1], jnp.where(row == SUBLANES - 1, halo[1:2], tail))
    return jnp.concatenate([rolled[:tm - SUBLANES], tail], axis=0)


def _conv_out(taps, wc, bc):
    u, u1, u2 = taps
    return wc[0:1] * u2 + wc[1:2] * u1 + wc[2:3] * u + bc


def _ffn_down_loss(gate, val, x1, target, w_down, g_final, tm):
    T = x1.shape[0]
    half = D_FF // 2

    def body(gt_ref, vl_ref, x1_ref, t_ref, wd_ref, g_ref, dx2_ref, loss_ref, gg_ref):
        i = pl.program_id(0)
        acc = jnp.zeros((tm, D_MODEL), F32)
        for j in range(2):
            gc = slice(j * half, (j + 1) * half)
            gate = gt_ref[:, gc].astype(F32)
            act = (gate * _sigmoid(gate) * vl_ref[:, gc].astype(F32)).astype(BF16)
            acc = acc + _dot(act, wd_ref[gc, :])
        x2 = x1_ref[...] + acc
        r = _rms_r(x2)
        n = x2 * r
        g = g_ref[...]
        diff = n * g - t_ref[...]
        dy = diff * (1.0 / D_MODEL)
        dx2_ref[...] = _rms_bwd(dy, n, r, g)

        @pl.when(i == 0)
        def _():
            loss_ref[...] = jnp.zeros_like(loss_ref)
            gg_ref[...] = jnp.zeros_like(gg_ref)

        loss_ref[...] += 0.5 * jnp.sum(jnp.mean(diff * diff, axis=-1, keepdims=True), axis=0, keepdims=True)
        gg_ref[...] += jnp.sum(dy * n, axis=0, keepdims=True)

    return pl.pallas_call(
        body, name="ffn_down_loss", grid=(T // tm,),
        in_specs=[_row(tm, D_FF), _row(tm, D_FF), _row(tm, D_MODEL), _row(tm, D_MODEL),
                  _resident(w_down.shape), _full(g_final.shape)],
        out_specs=[_row(tm, D_MODEL), _full((1, 1)), _full((1, D_MODEL))],
        out_shape=[_sds((T, D_MODEL), F32), _sds((1, 1), F32), _sds((1, D_MODEL), F32)],
        compiler_params=_cp(("arbitrary",), 48),
    )(*_hbm(gate, val, x1, target, w_down, g_final))


def _ffn_bwd_act(gate, val, dx2, w_down, tm):
    T = dx2.shape[0]
    half = D_FF // 2
    nt = T // tm

    def body(g_ref, v_ref, dx_ref, wd_ref, dg_ref, dv_ref, gwd_out, gbg_ref, gbv_ref, gwd_ref):
        i = pl.program_id(1)
        gate = g_ref[...].astype(F32)
        val = v_ref[...].astype(F32)
        sg = _sigmoid(gate)
        silu = gate * sg
        dx = dx_ref[...].astype(BF16)
        d_act = _dot_nt(dx, wd_ref[...])
        d_val = d_act * silu
        d_gate = d_act * val * (sg * (1.0 + gate * (1.0 - sg)))
        dg_ref[...] = d_gate.astype(BF16)
        dv_ref[...] = d_val.astype(BF16)

        @pl.when(i == 0)
        def _():
            for r in (gwd_ref, gbg_ref, gbv_ref):
                r[...] = jnp.zeros_like(r)

        gwd_ref[...] += _dot_tn((silu * val).astype(BF16), dx)
        gbg_ref[...] += jnp.sum(d_gate, axis=0, keepdims=True)
        gbv_ref[...] += jnp.sum(d_val, axis=0, keepdims=True)

        @pl.when(i == nt - 1)
        def _():
            gwd_out[...] = gwd_ref[...].astype(BF16)

    tile = pl.BlockSpec((tm, half), lambda j, i: (i, j))
    vec = pl.BlockSpec((1, half), lambda j, i: (0, j))
    wrows = pl.BlockSpec((half, D_MODEL), lambda j, i: (j, 0))
    return pl.pallas_call(
        body, name="ffn_bwd_act", grid=(2, nt),
        in_specs=[tile, tile, pl.BlockSpec((tm, D_MODEL), lambda j, i: (i, 0)), wrows],
        out_specs=[tile, tile, wrows, vec, vec],
        out_shape=[_sds((T, D_FF), BF16), _sds((T, D_FF), BF16), _sds((D_FF, D_MODEL), BF16),
                   _sds((1, D_FF), F32), _sds((1, D_FF), F32)],
        scratch_shapes=[pltpu.VMEM((half, D_MODEL), F32)],
        compiler_params=_cp(("arbitrary", "arbitrary"), 56),
    )(*_hbm(gate, val, dx2, w_down))


def _ffn_bwd_up(d_gate, d_val, upre, dx2, x1, g_ffn, w_conv, w_up, tm, seq):
    T = dx2.shape[0]
    tiles_per_seq = seq // tm
    k16 = tm // BF16_ROWS
    n16 = T // BF16_ROWS
    cw = D_FF // 2

    def body(dg_ref, dv_ref, hg_ref, hv_ref, u_ref, dx2_ref, x1_ref, g_ref, wc_ref, wu_ref, du_ref, dx1_ref, gg_ref, gwc_ref):
        i = pl.program_id(0)
        at_end = (i % tiles_per_seq) == tiles_per_seq - 1

        @pl.when(i == 0)
        def _():
            gg_ref[...] = jnp.zeros_like(gg_ref)
            gwc_ref[...] = jnp.zeros_like(gwc_ref)

        dh = jnp.zeros((tm, D_MODEL), F32)
        for j in range(4):
            src, hsrc = (dg_ref, hg_ref) if j < 2 else (dv_ref, hv_ref)
            ls = slice((j % 2) * cw, (j % 2 + 1) * cw)
            cs = slice(j * cw, (j + 1) * cw)
            d = src[:, ls].astype(F32)
            hl = hsrc[:, ls].astype(F32)[0:2]
            hl = jnp.where(at_end, 0.0, hl)
            wc = wc_ref[:, cs]
            d1 = _shift_up(d, hl, 1)
            d2 = _shift_up(d, hl, 2)
            du = (wc[2:3] * d + wc[1:2] * d1 + wc[0:1] * d2).astype(BF16)
            du_ref[:, cs] = du
            dh = dh + _dot_nt(du, wu_ref[j])
            u = u_ref[:, cs].astype(F32)
            gwc_ref[0:1, cs] += jnp.sum(d2 * u, axis=0, keepdims=True)
            gwc_ref[1:2, cs] += jnp.sum(d1 * u, axis=0, keepdims=True)
            gwc_ref[2:3, cs] += jnp.sum(d * u, axis=0, keepdims=True)
        x = x1_ref[...]
        r = _rms_r(x)
        n = x * r
        dx1_ref[...] = dx2_ref[...] + _rms_bwd(dh, n, r, g_ref[...])
        gg_ref[...] += jnp.sum(dh * n, axis=0, keepdims=True)

    nxt = pl.BlockSpec((BF16_ROWS, D_FF), lambda i: (jnp.minimum((i + 1) * k16, n16 - 1), 0))
    return pl.pallas_call(
        body, name="ffn_bwd_up", grid=(T // tm,),
        in_specs=[_row(tm, D_FF), _row(tm, D_FF), nxt, nxt, _row(tm, 2 * D_FF), _row(tm, D_MODEL), _row(tm, D_MODEL),
                  _full(g_ffn.shape), _full(w_conv.shape), _resident(w_up.shape)],
        out_specs=[_row(tm, 2 * D_FF), _row(tm, D_MODEL), _full((1, D_MODEL)), _full((3, 2 * D_FF))],
        out_shape=[_sds((T, 2 * D_FF), BF16), _sds((T, D_MODEL), F32), _sds((1, D_MODEL), F32), _sds((3, 2 * D_FF), F32)],
        compiler_params=_cp(("arbitrary",), 56),
    )(*_hbm(d_gate, d_val, d_gate, d_val, upre, dx2, x1, g_ffn, w_conv, w_up))


def _matmul_tn(a, b, tn, tk, name):
    T, M = a.shape
    N = b.shape[1]
    nk = T // tk

    def body(a_ref, b_ref, o_ref, acc_ref):
        k = pl.program_id(1)

        @pl.when(k == 0)
        def _():
            acc_ref[...] = jnp.zeros_like(acc_ref)

        acc_ref[...] += _dot_tn(a_ref[...], b_ref[...])

        @pl.when(k == nk - 1)
        def _():
            o_ref[...] = acc_ref[...].astype(BF16)

    return pl.pallas_call(
        body, name=name, grid=(N // tn, nk),
        in_specs=[pl.BlockSpec((tk, M), lambda j, k: (k, 0)), pl.BlockSpec((tk, tn), lambda j, k: (k, j))],
        out_specs=pl.BlockSpec((M, tn), lambda j, k: (0, j)), out_shape=_sds((M, N), BF16),
        scratch_shapes=[pltpu.VMEM((M, tn), F32)],
        compiler_params=_cp(("arbitrary", "arbitrary"), 48),
    )(*_hbm(a, b))


def _merge_bwd(dx1, merged, y_a, y_b, proj_g, w_pa, w_pb, w_out, tm, after=None):
    T = dx1.shape[0]

    nt = T // tm
    pshape = (A_WIDTH, D_MODEL)
    order = [] if after is None else [after]

    def body(*refs):
        dx_ref, mg_ref, ya_ref, yb_ref, g_ref, wpa_ref, wpb_ref, wo_ref = refs[:8]
        dg_ref, dya_ref, dyb_ref, gwo_out, gwpa_out, gwpb_out, gwo_ref, gwpa_ref, gwpb_ref = refs[8 + len(order):]
        i = pl.program_id(0)
        dx = dx_ref[...].astype(BF16)
        dm = _dot_nt(dx, wo_ref[...])
        g = g_ref[...].astype(F32)
        ya = ya_ref[...]
        yb = yb_ref[...]
        pa = _dot_stacked(ya, wpa_ref)
        pb = _dot_stacked(yb, wpb_ref)
        sa = _sigmoid(g[:, :D_MODEL])
        sb = _sigmoid(g[:, D_MODEL:])
        dpa = (dm * sa).astype(BF16)
        dpb = (dm * sb).astype(BF16)
        dg_ref[:, :D_MODEL] = (dm * pa * (sa * (1.0 - sa))).astype(BF16)
        dg_ref[:, D_MODEL:] = (dm * pb * (sb * (1.0 - sb))).astype(BF16)
        dya_ref[...] = _dot_nt_stacked(dpa, wpa_ref).astype(BF16)
        dyb_ref[...] = _dot_nt_stacked(dpb, wpb_ref).astype(BF16)

        @pl.when(i == 0)
        def _():
            for r in (gwo_ref, gwpa_ref, gwpb_ref):
                r[...] = jnp.zeros_like(r)

        gwo_ref[...] += _dot_tn(mg_ref[...], dx)
        gwpa_ref[...] += _dot_tn(ya, dpa)
        gwpb_ref[...] += _dot_tn(yb, dpb)

        @pl.when(i == nt - 1)
        def _():
            gwo_out[...] = gwo_ref[...].astype(BF16)
            gwpa_out[...] = gwpa_ref[...].astype(BF16)
            gwpb_out[...] = gwpb_ref[...].astype(BF16)

    return pl.pallas_call(
        body, name="merge_bwd", grid=(nt,),
        in_specs=[_row(tm, D_MODEL), _row(tm, D_MODEL), _row(tm, A_WIDTH), _row(tm, Q_DIM), _row(tm, G_DIM),
                  _resident(w_pa.shape), _resident(w_pb.shape), _resident(w_out.shape)] + [ANY] * len(order),
        out_specs=[_row(tm, G_DIM), _row(tm, A_WIDTH), _row(tm, Q_DIM),
                   _full(w_out.shape), _full(pshape), _full(pshape)],
        out_shape=[_sds((T, G_DIM), BF16), _sds((T, A_WIDTH), BF16), _sds((T, Q_DIM), BF16),
                   _sds(w_out.shape, BF16), _sds(pshape, BF16), _sds(pshape, BF16)],
        scratch_shapes=[pltpu.VMEM(w_out.shape, F32), pltpu.VMEM(pshape, F32), pltpu.VMEM(pshape, F32)],
        compiler_params=_cp(("arbitrary",), 56),
    )(*_hbm(dx1, merged, y_a, y_b, proj_g, w_pa, w_pb, w_out), *order)


def _sgu_bwd(proj_a, d_ya, g_sgu, w_s, b_st, tm, after=None):
    T = proj_a.shape[0]
    order = [] if after is None else [after]

    def body(*refs):
        p_ref, dy_ref, g_ref, ws_ref, bs_ref = refs[:5]
        dp_ref, gws_ref, gbs_ref, gg_ref = refs[5 + len(order):]
        tril = _tril()
        g = g_ref[...]
        pu, pv, u, tu, vv, tv, rv, vn = _sgu_parts(p_ref[...].astype(F32), g)
        dy = dy_ref[...].astype(F32)

        @pl.when(pl.program_id(0) == 0)
        def _():
            for r in (gws_ref, gbs_ref, gg_ref):
                r[...] = jnp.zeros_like(r)

        du_cols = []
        dvn_cols = []
        for gi in range(A_GROUPS):
            wm = jnp.where(tril, ws_ref[gi], 0.0).astype(BF16)
            wmt = wm.astype(F32).T.astype(BF16)
            bcol = bs_ref[:, gi:gi + 1]
            cs = slice(gi * CHUNK, (gi + 1) * CHUNK)
            du_rows = []
            dvn_rows = []
            gw = jnp.zeros((CHUNK, CHUNK), F32)
            gb = jnp.zeros((CHUNK, 1), F32)
            for c in range(tm // CHUNK):
                rs = slice(c * CHUNK, (c + 1) * CHUNK)
                vn_c = vn[rs, cs]
                s = _dot(wm, vn_c) + bcol
                dy_c = dy[rs, cs]
                ds = dy_c * u[rs, cs]
                du_rows.append(dy_c * s)
                dsb = ds.astype(BF16)
                gw = gw + _dot_nt(dsb, vn_c)
                gb = gb + jnp.sum(ds, axis=-1, keepdims=True)
                dvn_rows.append(_dot(wmt, dsb))
            gws_ref[gi] += jnp.where(tril, gw, 0.0)
            gbs_ref[:, gi:gi + 1] += gb
            du_cols.append(jnp.concatenate(du_rows, axis=0))
            dvn_cols.append(jnp.concatenate(dvn_rows, axis=0))
        du = jnp.concatenate(du_cols, axis=1)
        dvn = jnp.concatenate(dvn_cols, axis=1)
        vhat = vv * rv
        gg_ref[...] += jnp.sum(dvn * vhat, axis=0, keepdims=True)
        dvv = _rms_bwd(dvn, vhat, rv, g)
        dp_ref[:, :A_WIDTH] = (du * _gelu_grad(pu, tu)).astype(BF16)
        dp_ref[:, A_WIDTH:] = (dvv * _gelu_grad(pv, tv)).astype(BF16)

    return pl.pallas_call(
        body, name="sgu_bwd", grid=(T // tm,),
        in_specs=[_row(tm, A_DIM), _row(tm, A_WIDTH), _full(g_sgu.shape), _full(w_s.shape), _full(b_st.shape)] + [ANY] * len(order),
        out_specs=[_row(tm, A_DIM), _full(w_s.shape), _full(b_st.shape), _full(g_sgu.shape)],
        out_shape=[_sds((T, A_DIM), BF16), _sds(w_s.shape, F32), _sds(b_st.shape, F32), _sds(g_sgu.shape, F32)],
        compiler_params=_cp(("arbitrary",)),
    )(*_hbm(proj_a, d_ya, g_sgu, w_s, b_st), *order)


def _attn_bwd(proj_b, d_yb, sinks, rel_bias, n_seq, seq):
    nb = seq // CHUNK
    bk = jnp.asarray(_band_buckets())

    def body(qkv_ref, do_ref, bk_ref, rel_ref, sink_ref, d_ref, gs_ref, gr_ref,
             bias_scr, sink_scr, kvar_scr, dbias_scr, dk_scr, dv_scr, ds_scr):
        b = pl.program_id(0)
        _attn_setup(bias_scr, sink_scr, kvar_scr, qkv_ref, bk_ref, rel_ref, sink_ref)
        ones = jnp.ones((2 * CHUNK, LANES), BF16)

        @pl.when(b == 0)
        def _():
            dbias_scr[...] = jnp.zeros_like(dbias_scr)
            ds_scr[...] = jnp.zeros_like(ds_scr)

        dk_scr[...] = jnp.zeros_like(dk_scr)
        dv_scr[...] = jnp.zeros_like(dv_scr)

        def transposed(a):
            return a.astype(F32).T.astype(BF16)

        def blk(n, carry):
            r0, kv, vv = _attn_block_inputs(kvar_scr, n)
            prob, psink = _attn_probs(qkv_ref, r0, n, kv, bias_scr, sink_scr, ones)
            dp = jnp.concatenate([_dot_nt(do_ref[pl.ds(r0, CHUNK), (h // 2) * LANES:(h // 2 + 1) * LANES], vv[h // 4][h % 2])
                                  for h in range(N_HEADS)], axis=0)
            delta = _rowsum(prob * dp, ones)
            dsc = prob * (dp - _both(delta))
            ds_scr[...] += psink * delta
            dbias_scr[...] += dsc
            dsb = (dsc * (HEAD_DIM ** -0.5)).astype(BF16)
            pb = prob.astype(BF16)
            dkt = [jnp.zeros((HEAD_DIM, 2 * CHUNK), F32) for _ in range(2)]
            dvt = [jnp.zeros((HEAD_DIM, 2 * CHUNK), F32) for _ in range(2)]
            for pr in range(N_HEADS // 2):
                ps = slice(pr * LANES, (pr + 1) * LANES)
                qpt = transposed(qkv_ref[pl.ds(r0, CHUNK), ps])
                dopt = transposed(do_ref[pl.ds(r0, CHUNK), ps])
                kvh = pr // 2
                dq = jnp.zeros((CHUNK, LANES), F32)
                for hh in range(2):
                    hr = _head_rows(2 * pr + hh)
                    rows = slice(hh * HEAD_DIM, (hh + 1) * HEAD_DIM)
                    dq = dq + _dot(dsb[hr], kv[kvh][hh])
                    dkt[kvh] = dkt[kvh] + _dot(qpt, dsb[hr])[rows]
                    dvt[kvh] = dvt[kvh] + _dot(dopt, pb[hr])[rows]
                d_ref[pl.ds(r0, CHUNK), ps] = dq.astype(BF16)
            dk_scr[:, pl.ds(r0, 2 * CHUNK)] += jnp.concatenate(dkt, axis=0)
            dv_scr[:, pl.ds(r0, 2 * CHUNK)] += jnp.concatenate(dvt, axis=0)
            return carry

        lax.fori_loop(0, nb, blk, 0)
        for n in range(nb):
            rows = slice(n * CHUNK, (n + 1) * CHUNK)
            cols = slice((n + 1) * CHUNK, (n + 2) * CHUNK)
            d_ref[rows, Q_DIM:Q_DIM + KV_DIM] = dk_scr[:, cols].T.astype(BF16)
            d_ref[rows, Q_DIM + KV_DIM:] = dv_scr[:, cols].T.astype(BF16)

        @pl.when(b == n_seq - 1)
        def _():
            bkv = bk_ref[...]
            for h in range(N_HEADS):
                gs_ref[0:1, h:h + 1] = -jnp.sum(ds_scr[_head_rows(h), 0:1], axis=0, keepdims=True)
                db = dbias_scr[_head_rows(h), :]
                for bb in range(N_BUCKETS):
                    part = jnp.sum(jnp.where(bkv == bb, db, 0.0), axis=-1, keepdims=True)
                    gr_ref[bb:bb + 1, h:h + 1] = jnp.sum(part, axis=0, keepdims=True)

    smem = pl.BlockSpec(memory_space=pltpu.SMEM)
    return pl.pallas_call(
        body, name="attn_bwd", grid=(n_seq,),
        in_specs=[_row(seq, B_DIM), _row(seq, Q_DIM), _full(bk.shape), smem, smem],
        out_specs=[_row(seq, B_DIM), _full((1, N_HEADS)), _full((N_BUCKETS, N_HEADS))],
        out_shape=[_sds((n_seq * seq, B_DIM), BF16), _sds((1, N_HEADS), F32), _sds((N_BUCKETS, N_HEADS), F32)],
        scratch_shapes=[pltpu.VMEM((HEAD_ROWS, 2 * CHUNK), F32), pltpu.VMEM((HEAD_ROWS, LANES), F32),
                        pltpu.VMEM((8, seq, KV_DIM), BF16), pltpu.VMEM((HEAD_ROWS, 2 * CHUNK), F32),
                        pltpu.VMEM((KV_DIM, seq + CHUNK), F32), pltpu.VMEM((KV_DIM, seq + CHUNK), F32),
                        pltpu.VMEM((HEAD_ROWS, LANES), F32)],
        compiler_params=_cp(("arbitrary",), 40),
    )(*_hbm(proj_b, d_yb, bk), rel_bias, sinks)


def _inproj_bwd(d_g, d_a, d_b, x2, dx1, g_mix, w_in, tm, after=None):
    T = x2.shape[0]
    order = [] if after is None else [after]

    def body(*refs):
        dg_ref, da_ref, db_ref, x_ref, dx1_ref, g_ref, w_ref = refs[:7]
        gx_ref, gg_ref = refs[7 + len(order):]
        dh = (_dot_nt(dg_ref[...], w_ref[:, _G_COLS]) + _dot_nt(da_ref[...], w_ref[:, _A_COLS])
              + _dot_nt(db_ref[...], w_ref[:, _B_COLS]))
        x = x_ref[...]
        r = _rms_r(x)
        n = x * r
        gx_ref[...] = dx1_ref[...] + _rms_bwd(dh, n, r, g_ref[...])

        @pl.when(pl.program_id(0) == 0)
        def _():
            gg_ref[...] = jnp.zeros_like(gg_ref)

        gg_ref[...] += jnp.sum(dh * n, axis=0, keepdims=True)

    return pl.pallas_call(
        body, name="inproj_bwd", grid=(T // tm,),
        in_specs=[_row(tm, G_DIM), _row(tm, A_DIM), _row(tm, B_DIM), _row(tm, D_MODEL), _row(tm, D_MODEL),
                  _full(g_mix.shape), _resident(w_in.shape)] + [ANY] * len(order),
        out_specs=[_row(tm, D_MODEL), _full((1, D_MODEL))],
        out_shape=[_sds((T, D_MODEL), F32), _sds((1, D_MODEL), F32)],
        compiler_params=_cp(("arbitrary",), 48),
    )(*_hbm(d_g, d_a, d_b, x2, dx1, g_mix, w_in), *order)


def _local_step(x, target, g_mix, g_sgu, w_s, b_s, sinks, rel_bias, g_ffn, b_conv, g_final,
                w_in, w_conv, late_weights, on_grads, after=None):
    n_seq, seq, _ = x.shape
    T = n_seq * seq
    tm = min(ROW_TILE, seq)
    tw = min(GRAD_ROW_TILE, T)
    tf = min(WIDE_ROW_TILE, seq)
    x2 = x.reshape(T, D_MODEL)
    tgt = target.reshape(T, D_MODEL)
    b_st = b_s.T
    g_fin = g_final.reshape(1, D_MODEL)

    proj_g, proj_a, proj_b, h = _inproj(x2, g_mix, w_in, tm, after)
    y_a = _sgu_fwd(proj_a, g_sgu, w_s, b_st, tm)
    y_b = _attn_fwd(proj_b, sinks, rel_bias, n_seq, seq)
    w_pa, w_pb, w_out, w_up, w_down = late_weights(y_b)
    x1, merged = _merge_fwd(x2, y_a, y_b, proj_g, w_pa, w_pb, w_out, tm)
    upre, h2, gate, val = _upproj(x1, g_ffn, w_up, w_conv, b_conv, tf, seq)
    dx2, loss, gg_final = _ffn_down_loss(gate, val, x1, tgt, w_down, g_fin, tm)

    d_gate, d_val, gw_down, gb_g, gb_v = _ffn_bwd_act(gate, val, dx2, w_down, tw)
    gb_conv = jnp.concatenate([gb_g, gb_v], axis=1)
    d_upre, dx1, gg_ffn, gw_conv = _ffn_bwd_up(d_gate, d_val, upre, dx2, x1, g_ffn, w_conv, w_up, tf, seq)
    gw_up = _matmul_tn(h2, d_upre, 2 * D_FF // 4, min(2 * GRAD_ROW_TILE, T), "grad_w_up")
    sent = on_grads("ffn", dict(w_up=gw_up, w_down=gw_down))
    d_g, d_ya, d_yb, gw_out, gw_pa, gw_pb = _merge_bwd(dx1, merged, y_a, y_b, proj_g, w_pa, w_pb, w_out, tf, sent)
    sent = on_grads("proj", dict(w_pa=gw_pa, w_pb=gw_pb, w_out=gw_out))
    d_a, gw_s, gb_st, gg_sgu = _sgu_bwd(proj_a, d_ya, g_sgu, w_s, b_st, tm, sent)
    d_b, g_sinks, g_rel = _attn_bwd(proj_b, _tie(d_yb, d_a), sinks, rel_bias, n_seq, seq)
    gw_g = _matmul_tn(h, _tie(d_g, d_b), D_MODEL, min(2 * GRAD_ROW_TILE, T), "grad_w_in_gate")
    gw_a = _matmul_tn(h, _tie(d_a, gw_g), A_DIM, min(2 * GRAD_ROW_TILE, T), "grad_w_in_a")
    gw_b = _matmul_tn(h, _tie(d_b, gw_a), B_DIM, min(2 * GRAD_ROW_TILE, T), "grad_w_in_b")
    gw_in = jnp.concatenate([gw_a, gw_b, gw_g], axis=1).reshape(D_MODEL, N_CHIPS, -1).transpose(1, 0, 2)
    sent = on_grads("in", dict(w_in=gw_in))
    grad_x, gg_mix = _inproj_bwd(d_g, d_a, d_b, x2, dx1, g_mix, w_in, tm, sent)

    small = dict(g_mix=gg_mix, g_sgu=gg_sgu, w_s=gw_s, b_s=gb_st.T, sinks=g_sinks, rel_bias=g_rel,
                 g_ffn=gg_ffn, b_conv=gb_conv, g_final=gg_final, w_conv=gw_conv)
    big = dict(w_in=gw_in, w_pa=gw_pa, w_pb=gw_pb, w_out=gw_out, w_up=gw_up, w_down=gw_down)
    return loss, grad_x.reshape(x.shape), small, big


_MIXER = ("w_in", "w_pa", "w_pb", "w_out")
_FFN = ("w_up", "w_down")
_BIG = _MIXER + _FFN

_SMALL = (("loss", (1, 1)), ("g_final", (1, D_MODEL)), ("g_mix", (1, D_MODEL)), ("g_ffn", (1, D_MODEL)),
          ("g_sgu", (1, A_WIDTH)), ("b_s", (A_GROUPS, CHUNK)), ("sinks", (1, N_HEADS)), ("rel_bias", (N_BUCKETS, N_HEADS)),
          ("b_conv", (1, 2 * D_FF)), ("w_conv", (3, 2 * D_FF)), ("w_s", (A_GROUPS, CHUNK, CHUNK)))
SMALL_ROWS = 96


def _pack_small(vals):
    flat = jnp.concatenate([vals[n].astype(F32).reshape(-1) for n, _ in _SMALL])
    flat = jnp.pad(flat, (0, SMALL_ROWS * D_MODEL - flat.shape[0]))
    return flat.reshape(SMALL_ROWS, D_MODEL)


def _unpack_small(buf):
    flat = buf.reshape(-1)
    out = {}
    off = 0
    for n, shp in _SMALL:
        k = int(np.prod(shp))
        out[n] = flat[off:off + k].reshape(shp)
        off += k
    return out


def _mesh_pos():
    return lax.axis_index("x"), lax.axis_index("y"), lax.axis_index("c")


def _other_chips(x, y):
    return [(1 - x, y), (x, 1 - y), (1 - x, 1 - y)]


def _remote(src, dst, send_sem, recv_sem, to):
    return pltpu.make_async_remote_copy(src_ref=src, dst_ref=dst, send_sem=send_sem, recv_sem=recv_sem,
                                        device_id=to, device_id_type=MESH)


def _own_slot(own, n, at):
    return lax.dynamic_update_slice(lax.empty((n,) + own.shape, own.dtype), own[None], (at,) + (0,) * own.ndim)


def _allgather_weights(stacks, wc_stack):
    names = list(stacks)
    n = len(names)

    def body(*refs):
        ins, outs = refs[:n + 1], refs[n + 1:2 * n + 2]
        send_sems, recv_sems = refs[2 * n + 2:]
        x, y, c = _mesh_pos()
        me = 2 * x + y
        sibling = (x, y, 1 - c)
        chips = _other_chips(x, y)

        def half(ref, chip, hc):
            hr = ref.shape[1] // 2
            return ref.at[chip, pl.ds(hc * hr, hr), :]

        first = []
        for k in range(n):
            first += [_remote(half(ins[k], me, c), half(outs[k], me, c), send_sems.at[6 * k + j], recv_sems.at[6 * k + j], (cx, cy, c))
                      for j, (cx, cy) in enumerate(chips)]
        first += [_remote(ins[n].at[me], outs[n].at[me], send_sems.at[6 * n + j], recv_sems.at[6 * n + j], (cx, cy, c))
                  for j, (cx, cy) in enumerate(chips)]
        for cp in first:
            cp.start()
        passed = []
        for k in range(n):
            for j, (cx, cy) in enumerate(chips):
                landed = half(outs[k], 2 * cx + cy, c)
                _remote(landed, landed, send_sems.at[6 * k + j], recv_sems.at[6 * k + j], (x, y, c)).wait_recv()
                passed.append(_remote(landed, landed, send_sems.at[6 * k + 3 + j], recv_sems.at[6 * k + 3 + j], sibling))
                passed[-1].start()
        for k in range(n):
            for j, (cx, cy) in enumerate(chips):
                theirs = half(outs[k], 2 * cx + cy, 1 - c)
                _remote(theirs, theirs, send_sems.at[6 * k + 3 + j], recv_sems.at[6 * k + 3 + j], (x, y, c)).wait_recv()
        for j, (cx, cy) in enumerate(chips):
            slot = outs[n].at[2 * cx + cy]
            _remote(slot, slot, send_sems.at[6 * n + j], recv_sems.at[6 * n + j], (x, y, c)).wait_recv()
        for cp in first + passed:
            cp.wait_send()

    arrays = [stacks[k] for k in names] + [wc_stack]
    outs = pl.pallas_call(
        body, name="allgather_weights",
        in_specs=[HBM] * (n + 1), out_specs=[HBM] * (n + 1), input_output_aliases={k: k for k in range(n + 1)},
        out_shape=[_sds(a.shape, a.dtype) for a in arrays],
        scratch_shapes=[pltpu.SemaphoreType.DMA((6 * n + 3,)), pltpu.SemaphoreType.DMA((6 * n + 3,))],
    )(*arrays)
    return dict(zip(names, outs[:n])), outs[n]


_KIND = {"w_in": "stack", "w_pa": "col", "w_pb": "col", "w_up": "col", "w_out": "row", "w_down": "row"}


def _half_view(ref, kind, h):
    if kind == "stack":
        k = ref.shape[1] // 2
        return ref.at[:, pl.ds(h * k, k), :]
    if kind == "col":
        k = ref.shape[0] // 2
        return ref.at[pl.ds(h * k, k), :]
    k = ref.shape[1] // 2
    return ref.at[:, pl.ds(h * k, k)]


def _shard_view(ref, kind, i):
    if kind == "stack":
        return ref.at[i]
    if kind == "col":
        k = ref.shape[1] // N_CHIPS
        return ref.at[:, pl.ds(i * k, k)]
    k = ref.shape[0] // N_CHIPS
    return ref.at[pl.ds(i * k, k), :]


def _region_view(ref, kind, h):
    if kind == "row":
        k = ref.shape[1] // 2
        return ref.at[:, pl.ds(h * k, k)]
    k = ref.shape[0] // 2
    return ref.at[pl.ds(h * k, k), :]


def _half_shape(shape, kind):
    if kind == "stack":
        return (shape[0], shape[1] // 2, shape[2])
    return (shape[0] // 2, shape[1]) if kind == "col" else (shape[0], shape[1] // 2)


def _part_shape(half_shape, kind):
    if kind == "stack":
        return tuple(half_shape[1:])
    k, w = half_shape
    return (k, w // N_CHIPS) if kind == "col" else (k // N_CHIPS, w)


_DATAFLOW = pltpu.SideEffectType.DATAFLOW_SIDE_EFFECTING
_TOKEN = (SUBLANES, LANES)


def _split_start(name, arrays, n_sems, issue, after=None):
    n = len(arrays)
    order = [] if after is None else [after]

    def body(*refs):
        base = n + len(order)
        issue(refs[:n], refs[base], refs[base + 1])
        refs[-1][...] = jnp.zeros(_TOKEN, F32)

    outs = pl.pallas_call(
        body, name=name,
        in_specs=[HBM] * n + [ANY] * len(order), out_specs=[SEM, SEM] + [HBM] * n + [pl.BlockSpec(memory_space=pltpu.VMEM)],
        out_shape=[pltpu.SemaphoreType.DMA((n_sems,)), pltpu.SemaphoreType.DMA((n_sems,))]
        + [pltpu.HBM(a.shape, a.dtype) for a in arrays] + [_sds(_TOKEN, F32)],
        input_output_aliases={k: 2 + k for k in range(n)},
        compiler_params=pltpu.CompilerParams(has_side_effects=_DATAFLOW),
    )(*[pltpu.with_memory_space_constraint(a, pltpu.HBM) for a in arrays], *order)
    return outs[0], outs[1], list(outs[2:2 + n]), outs[-1]


def _split_wait(name, started, waits, after):
    send_sems, recv_sems, arrays, _ = started
    n = len(arrays)

    def body(*refs):
        waits(refs[:n], refs[n], refs[n + 1])

    return pl.pallas_call(
        body, name=name,
        in_specs=[HBM] * n + [SEM, SEM, ANY], out_specs=[HBM] * n,
        out_shape=[pltpu.HBM(a.shape, a.dtype) for a in arrays],
        input_output_aliases={k: k for k in range(n)},
        compiler_params=pltpu.CompilerParams(has_side_effects=_DATAFLOW),
    )(*arrays, send_sems, recv_sems, after)


def _wait_both(src, dst, send_sem, recv_sem):
    x, y, c = _mesh_pos()
    cp = _remote(src, dst, send_sem, recv_sem, (x, y, c))
    cp.wait_send()
    cp.wait_recv()


def _pair_exchange_start(parts, tag, after):
    names = list(parts)
    n = len(names)
    lands = [lax.empty(_half_shape(parts[k].shape, _KIND[k]), parts[k].dtype) for k in names]

    def issue(refs, send_sems, recv_sems):
        x, y, c = _mesh_pos()
        for hc in range(2):
            @pl.when(c == hc)
            def _():
                for k in range(n):
                    _remote(_half_view(refs[k], _KIND[names[k]], 1 - hc), refs[n + k], send_sems.at[k], recv_sems.at[k],
                            (x, y, 1 - c)).start()

    return names, _split_start("grad_pair_exchange_start_" + tag, [parts[k] for k in names] + lands, n, issue, after)


def _pair_exchange_wait(pending, tag, after):
    names, started = pending
    n = len(names)

    def waits(refs, send_sems, recv_sems):
        for k in range(n):
            _wait_both(_half_view(refs[k], _KIND[names[k]], 0), refs[n + k], send_sems.at[k], recv_sems.at[k])

    outs = _split_wait("grad_pair_exchange_wait_" + tag, started, waits, after)
    return dict(zip(names, outs[:n])), dict(zip(names, outs[n:]))


def _half_blocks(shape, kind):
    if kind == "stack":
        _, k, w = shape
        tr = 256
        nb = k // 2 // tr
        return (N_CHIPS, nb), (1, tr, w), (lambda i, r, s: (i, r, 0)), (lambda i, r, s: (i, s[1] * nb + r, 0))
    k, w = shape
    if kind == "col":
        tr = 256 if w <= 2 * D_MODEL else 128
        nb = k // 2 // tr
        return (nb,), (tr, w), (lambda r, s: (r, 0)), (lambda r, s: (s[1] * nb + r, 0))
    tr = k // N_CHIPS
    return (N_CHIPS,), (tr, w // 2), (lambda r, s: (r, 0)), (lambda r, s: (r, s[1]))


def _pair_add(part, from_sibling, name, pos):
    kind = _KIND[name]
    grid, block, half_map, full_map = _half_blocks(part.shape, kind)

    def body(s_ref, p_ref, q_ref, o_ref):
        o_ref[...] = (p_ref[...].astype(F32) + q_ref[...].astype(F32)).astype(BF16)

    return pl.pallas_call(
        body, name="grad_pair_add_" + name,
        grid_spec=pltpu.PrefetchScalarGridSpec(
            num_scalar_prefetch=1, grid=grid,
            in_specs=[pl.BlockSpec(block, full_map), pl.BlockSpec(block, half_map)],
            out_specs=pl.BlockSpec(block, half_map)),
        out_shape=_sds(from_sibling.shape, BF16),
        compiler_params=_cp(("arbitrary",) * len(grid)),
    )(pos, *_hbm(part, from_sibling))


def _chip_exchange_start(sums, tag, after):
    names = list(sums)
    n = len(names)
    lands = [lax.empty((3,) + _part_shape(sums[k].shape, _KIND[k]), sums[k].dtype) for k in names]

    def issue(refs, send_sems, recv_sems):
        x, y, c = _mesh_pos()
        me = 2 * x + y
        for i in range(N_CHIPS):
            xi, yi = i // 2, i % 2
            j = jnp.where(xi != x, jnp.where(yi != y, 2, 0), 1)

            @pl.when(i != me)
            def _():
                for k in range(n):
                    _remote(_shard_view(refs[k], _KIND[names[k]], i), refs[n + k].at[j], send_sems.at[3 * k + j],
                            recv_sems.at[3 * k + j], (xi, yi, c)).start()

    return names, _split_start("grad_chip_exchange_start_" + tag, [sums[k] for k in names] + lands, 3 * n, issue, after)


def _chip_exchange_wait(pending, tag, after):
    names, started = pending
    n = len(names)

    def waits(refs, send_sems, recv_sems):
        for k in range(n):
            for j in range(3):
                _wait_both(_shard_view(refs[k], _KIND[names[k]], 0), refs[n + k].at[j], send_sems.at[3 * k + j], recv_sems.at[3 * k + j])

    return dict(zip(names, _split_wait("grad_chip_exchange_wait_" + tag, started, waits, after)[n:]))


def _allgather_start(stacks, after):
    names = list(stacks)

    def issue(refs, send_sems, recv_sems):
        x, y, c = _mesh_pos()
        me = 2 * x + y
        for k, st in enumerate(refs):
            hr = st.shape[1] // 2
            mine = st.at[me, pl.ds(c * hr, hr), :]
            for j, (cx, cy) in enumerate(_other_chips(x, y)):
                _remote(mine, mine, send_sems.at[3 * k + j], recv_sems.at[3 * k + j], (cx, cy, c)).start()

    return names, _split_start("allgather_start", [stacks[k] for k in names], 3 * len(names), issue, after)


def _allgather_wait(pending, after):
    names, started = pending

    def waits(refs, send_sems, recv_sems):
        for k, st in enumerate(refs):
            slot = st.at[0, pl.ds(0, st.shape[1] // 2), :]
            for j in range(3):
                _wait_both(slot, slot, send_sems.at[3 * k + j], recv_sems.at[3 * k + j])

    return dict(zip(names, _split_wait("allgather_wait", started, waits, after)))


def _allgather_forward(stacks):
    names = list(stacks)
    n = len(names)

    def body(*refs):
        ins, outs = refs[:n], refs[n:2 * n]
        send_sems, recv_sems = refs[2 * n:]
        x, y, c = _mesh_pos()
        copies = []
        for k in range(n):
            hr = ins[k].shape[1] // 2
            for j, (cx, cy) in enumerate(_other_chips(x, y)):
                chip = 2 * cx + cy
                copies.append(_remote(ins[k].at[chip, pl.ds(c * hr, hr), :], outs[k].at[chip, pl.ds(c * hr, hr), :],
                                      send_sems.at[3 * k + j], recv_sems.at[3 * k + j], (x, y, 1 - c)))
        for cp in copies:
            cp.start()
        for cp in copies:
            cp.wait()

    arrays = [stacks[k] for k in names]
    outs = pl.pallas_call(
        body, name="allgather_forward", in_specs=[HBM] * n, out_specs=[HBM] * n,
        input_output_aliases={k: k for k in range(n)},
        out_shape=[_sds(a.shape, a.dtype) for a in arrays],
        scratch_shapes=[pltpu.SemaphoreType.DMA((3 * n,)), pltpu.SemaphoreType.DMA((3 * n,))],
    )(*arrays)
    return dict(zip(names, outs))


def _owner_sum(part, from_sibling, from_chips, name, pos, shard_shape):
    kind = _KIND[name]
    _, pk, pw = from_chips.shape
    if kind == "row":
        tr, nb = pk, 1
        p_spec = pl.BlockSpec((tr, pw), lambda r, s: (s[0], s[1]))
        q_spec = pl.BlockSpec((tr, pw), lambda r, s: (s[0], 0))
        o_spec = pl.BlockSpec((tr, pw), lambda r, s: (0, s[1]))
    else:
        tr = 256
        nb = pk // tr
        if kind == "stack":
            p_spec = pl.BlockSpec((None, tr, pw), lambda r, s: (s[0], s[1] * nb + r, 0))
            q_spec = pl.BlockSpec((None, tr, pw), lambda r, s: (s[0], r, 0))
        else:
            p_spec = pl.BlockSpec((tr, pw), lambda r, s: (s[1] * nb + r, s[0]))
            q_spec = pl.BlockSpec((tr, pw), lambda r, s: (r, s[0]))
        o_spec = pl.BlockSpec((tr, pw), lambda r, s: (s[1] * nb + r, 0))

    def body(s_ref, p_ref, q_ref, r_ref, o_ref):
        acc = p_ref[...].astype(F32) + q_ref[...].astype(F32)
        for j in range(3):
            acc = acc + r_ref[j].astype(F32)
        o_ref[...] = acc

    return pl.pallas_call(
        body, name="grad_owner_sum_" + name,
        grid_spec=pltpu.PrefetchScalarGridSpec(
            num_scalar_prefetch=1, grid=(nb,),
            in_specs=[p_spec, q_spec, pl.BlockSpec((3, tr, pw), lambda r, s: (0, r, 0))],
            out_specs=o_spec),
        out_shape=_sds(shard_shape, F32),
        compiler_params=_cp(("arbitrary",), 32),
    )(pos, *_hbm(part, from_sibling, from_chips))


def _pair_share_start(shards, tag, after):
    names = list(shards)

    def issue(refs, send_sems, recv_sems):
        x, y, c = _mesh_pos()
        for hc in range(2):
            @pl.when(c == hc)
            def _():
                for k, g in enumerate(refs):
                    mine = _region_view(g, _KIND[names[k]], hc)
                    _remote(mine, mine, send_sems.at[k], recv_sems.at[k], (x, y, 1 - c)).start()

    return names, _split_start("grad_pair_share_start_" + tag, [shards[k] for k in names], len(names), issue, after)


def _pair_share_wait(pending, tag, after):
    names, started = pending

    def waits(refs, send_sems, recv_sems):
        for k, g in enumerate(refs):
            region = _region_view(g, _KIND[names[k]], 0)
            _wait_both(region, region, send_sems.at[k], recv_sems.at[k])

    return dict(zip(names, _split_wait("grad_pair_share_wait_" + tag, started, waits, after)))


def _small_exchange_start(slots, after):
    def issue(refs, send_sems, recv_sems):
        x, y, c = _mesh_pos()
        mine = refs[0].at[4 * x + 2 * y + c]
        k = 0
        for px in range(2):
            for py in range(2):
                for pc in range(2):
                    if px + py + pc:
                        peer = (1 - x if px else x, 1 - y if py else y, 1 - c if pc else c)
                        _remote(mine, mine, send_sems.at[k], recv_sems.at[k], peer).start()
                        k += 1

    return _split_start("small_exchange_start", [slots], N_DEV - 1, issue, after)


def _small_exchange_wait(started, after):
    def waits(refs, send_sems, recv_sems):
        slot = refs[0].at[0]
        for k in range(N_DEV - 1):
            _wait_both(slot, slot, send_sems.at[k], recv_sems.at[k])

    return _split_wait("small_exchange_wait", started, waits, after)[0]


def _adam_math(w, g, m, v):
    m = ADAM_B1 * m + (1.0 - ADAM_B1) * g
    v = ADAM_B2 * v + (1.0 - ADAM_B2) * (g * g)
    m_hat = m / (1.0 - ADAM_B1 ** ADAM_STEP)
    v_hat = v / (1.0 - ADAM_B2 ** ADAM_STEP)
    delta = -ADAM_LR * (m_hat / (jnp.sqrt(v_hat) + ADAM_EPS) + ADAM_WD * w)
    return delta, m, v


def _adamw(w, g, m, v, name):
    rows, cols = w.shape
    tr = rows
    for cand in (256, 128, 64, 32, 16, 8):
        if rows % cand == 0 and rows > cand:
            tr = cand
            break

    def body(w_ref, g_ref, m_ref, v_ref, d_ref, nm_ref, nv_ref):
        d, nm, nv = _adam_math(w_ref[...], g_ref[...], m_ref[...], v_ref[...])
        d_ref[...] = d
        nm_ref[...] = nm
        nv_ref[...] = nv

    spec = pl.BlockSpec((tr, cols), lambda i: (i, 0))
    return pl.pallas_call(
        body, name=name, grid=(rows // tr,), in_specs=[spec] * 4, out_specs=[spec] * 3,
        out_shape=[_sds(w.shape, F32)] * 3, compiler_params=_cp(("arbitrary",)),
    )(*_hbm(w, g, m, v))


def _small_sum_adamw(gathered, w, m, v):
    def body(a_ref, w_ref, m_ref, v_ref, g_ref, d_ref, nm_ref, nv_ref):
        g = a_ref[0]
        for k in range(1, N_DEV):
            g = g + a_ref[k]
        g_ref[...] = g
        d, nm, nv = _adam_math(w_ref[...], g, m_ref[...], v_ref[...])
        d_ref[...] = d
        nm_ref[...] = nm
        nv_ref[...] = nv

    return pl.pallas_call(
        body, name="small_sum_adamw", out_shape=[_sds(w.shape, F32)] * 4,
    )(gathered, w, m, v)


_NAMES = ("g_mix", "w_in", "g_sgu", "w_s", "b_s", "sinks", "rel_bias", "w_pa", "w_pb", "w_out",
          "g_ffn", "w_up", "w_conv", "b_conv", "w_down", "g_final")

def kernel(x, g_mix, w_in, g_sgu, w_s, b_s, sinks, rel_bias, w_pa, w_pb, w_out, g_ffn, w_up, w_conv, b_conv, w_down, g_final, loss_target, m_g_mix, m_w_in, m_g_sgu, m_w_s, m_b_s, m_sinks, m_rel_bias, m_w_pa, m_w_pb, m_w_out, m_g_ffn, m_w_up, m_w_conv, m_b_conv, m_w_down, m_g_final, v_g_mix, v_w_in, v_g_sgu, v_w_s, v_b_s, v_sinks, v_rel_bias, v_w_pa, v_w_pb, v_w_out, v_g_ffn, v_w_up, v_w_conv, v_b_conv, v_w_down, v_g_final):
    w = dict(g_mix=g_mix, w_in=w_in, g_sgu=g_sgu, w_s=w_s, b_s=b_s, sinks=sinks, rel_bias=rel_bias, w_pa=w_pa, w_pb=w_pb,
             w_out=w_out, g_ffn=g_ffn, w_up=w_up, w_conv=w_conv, b_conv=b_conv, w_down=w_down, g_final=g_final)
    m = dict(g_mix=m_g_mix, w_in=m_w_in, g_sgu=m_g_sgu, w_s=m_w_s, b_s=m_b_s, sinks=m_sinks, rel_bias=m_rel_bias, w_pa=m_w_pa,
             w_pb=m_w_pb, w_out=m_w_out, g_ffn=m_g_ffn, w_up=m_w_up, w_conv=m_w_conv, b_conv=m_b_conv, w_down=m_w_down,
             g_final=m_g_final)
    v = dict(g_mix=v_g_mix, w_in=v_w_in, g_sgu=v_g_sgu, w_s=v_w_s, b_s=v_b_s, sinks=v_sinks, rel_bias=v_rel_bias, w_pa=v_w_pa,
             w_pb=v_w_pb, w_out=v_w_out, g_ffn=v_g_ffn, w_up=v_w_up, w_conv=v_w_conv, b_conv=v_b_conv, w_down=v_w_down,
             g_final=v_g_final)
    xi, yi, ci = _mesh_pos()
    me = 2 * xi + yi

    shard = {n: w[n][0] for n in _BIG}
    shard_shapes = {n: shard[n].shape for n in _BIG}
    wc_shard = w["w_conv"][0]
    wc_pad = jnp.pad(wc_shard, ((0, 5), (0, 0)))
    own = {n: _own_slot(shard[n].astype(BF16), N_CHIPS, me) for n in _BIG}
    stacks, wc_all = _allgather_weights({"w_in": own["w_in"]}, _own_slot(wc_pad, N_CHIPS, me))
    late_gather = _allgather_start({n: own[n] for n in _BIG[1:]}, stacks["w_in"])
    w_conv_full = jnp.concatenate([wc_all[i, :3] for i in range(N_CHIPS)], axis=1)
    w_in_full = stacks["w_in"].transpose(1, 0, 2).reshape(D_MODEL, -1)
    pos = jnp.stack([me, ci])

    def late_weights(done):
        st = _allgather_forward(_allgather_wait(late_gather, done))
        return st["w_pa"], st["w_pb"], st["w_out"].reshape(D_MODEL, D_MODEL), st["w_up"], st["w_down"].reshape(D_FF, D_MODEL)

    groups = {}

    def stage1(group, parts):
        groups[group] = dict(parts=parts, pair=_pair_exchange_start(parts, group, None))
        return groups[group]["pair"][1][-1]

    def stage2(group, after, order_after):
        g = groups[group]
        g["parts"], g["sib"] = _pair_exchange_wait(g["pair"], group, after)
        g["chip"] = _chip_exchange_start({n: _pair_add(g["parts"][n], g["sib"][n], n, pos) for n in g["parts"]}, group, order_after)
        return g["chip"][1][-1]

    def stage3(group, after, order_after):
        g = groups[group]
        got = _chip_exchange_wait(g["chip"], group, after)
        g["share"] = _pair_share_start(
            {n: _owner_sum(g["parts"][n], g["sib"][n], got[n], n, pos, shard_shapes[n]) for n in g["parts"]}, group, order_after)
        return g["share"][1][-1]

    grads, deltas, new_m, new_v = {}, {}, {}, {}

    def stage4(group, after):
        g_shard = _pair_share_wait(groups[group]["share"], group, after)
        last = None
        for n in g_shard:
            g = _tie(g_shard[n], last)
            if n == "w_in":
                gt = g.T
                d, nm, nv = _adamw(shard[n].T, gt, m[n][0].T, v[n][0].T, "adamw_" + n)
                grads[n], deltas[n], new_m[n], new_v[n] = gt.T[None], d.T[None], nm.T[None], nv.T[None]
            else:
                d, nm, nv = _adamw(shard[n], g, m[n][0], v[n][0], "adamw_" + n)
                grads[n], deltas[n], new_m[n], new_v[n] = g[None], d[None], nm[None], nv[None]
            last = nv
        return last

    def on_grads(group, parts):
        token = stage1(group, parts)
        some = next(iter(parts.values()))
        if group == "proj":
            token = stage2("ffn", some, token)
        if group == "in":
            token = stage2("proj", some, token)
            token = stage3("ffn", some, token)
            token = stage2("in", token, token)
        return token

    loss, grad_x, small, big = _local_step(
        x, loss_target, w["g_mix"], w["g_sgu"], w["w_s"][0], w["b_s"][0], w["sinks"], w["rel_bias"], w["g_ffn"],
        w["b_conv"], w["g_final"], w_in_full, w_conv_full, late_weights, on_grads, late_gather[1][-1])

    small["loss"] = loss
    small_gather = _small_exchange_start(_own_slot(_pack_small(small), N_DEV, 2 * me + ci), grad_x)
    token = stage3("proj", grad_x, small_gather[-1])
    done = stage4("ffn", token)
    done = stage4("proj", done)
    token = stage3("in", done, None)
    sw = {n: (jnp.zeros((1, 1), F32) if n in ("loss", "w_conv") else w[n]) for n, _ in _SMALL}
    sm = {n: (jnp.zeros((1, 1), F32) if n in ("loss", "w_conv") else m[n]) for n, _ in _SMALL}
    sv = {n: (jnp.zeros((1, 1), F32) if n in ("loss", "w_conv") else v[n]) for n, _ in _SMALL}
    for d in (sw, sm, sv):
        d["w_conv"] = jnp.zeros((3, 2 * D_FF), F32)
    all_small = _small_exchange_wait(small_gather, token)
    s_g, s_d, s_m, s_v = [_unpack_small(a) for a in _small_sum_adamw(all_small, _pack_small(sw), _pack_small(sm), _pack_small(sv))]
    stage4("in", all_small)
    wcols = wc_shard.shape[1]
    g_wc = lax.dynamic_slice(s_g["w_conv"], (0, me * wcols), (3, wcols))
    d, nm, nv = _adamw(wc_shard, g_wc, m["w_conv"][0], v["w_conv"][0], "adamw_w_conv")
    grads["w_conv"], deltas["w_conv"], new_m["w_conv"], new_v["w_conv"] = g_wc[None], d[None], nm[None], nv[None]
    for n, _ in _SMALL:
        if n in ("loss", "w_conv"):
            continue
        shp = w[n].shape
        grads[n], deltas[n], new_m[n], new_v[n] = (s_g[n].reshape(shp), s_d[n].reshape(shp), s_m[n].reshape(shp),
                                                    s_v[n].reshape(shp))

    return (s_g["loss"].reshape(()), grad_x, *[grads[n] for n in _NAMES], *[deltas[n] for n in _NAMES],
            *[new_m[n] for n in _NAMES], *[new_v[n] for n in _NAMES])
```

```python
import functools

import numpy as np
import jax
import jax.numpy as jnp
from jax import lax
from jax.experimental import pallas as pl
from jax.experimental.pallas import tpu as pltpu

F32 = jnp.float32
BF16 = jnp.bfloat16

D_MODEL = 1024
CHUNK = 128
A_GROUPS = 4
A_WIDTH = 512
N_HEADS = 8
HEAD_DIM = 64
Q_DIM = 512
KV_DIM = 128
N_BUCKETS = 32
MAX_DISTANCE = 128
D_FF = 2816
EPS = 1e-6
NEG_INF = -1e30
G_DIM = 2 * D_MODEL
A_DIM = 2 * A_WIDTH
B_DIM = Q_DIM + 2 * KV_DIM
LANES = 128
SUBLANES = 8
ROW_TILE = 512
WIDE_ROW_TILE = 256
GRAD_ROW_TILE = 512
BF16_ROWS = 16
N_CHIPS = 4
N_DEV = 8

ADAM_LR = 0.001
ADAM_B1 = 0.9
ADAM_B2 = 0.999
ADAM_EPS = 1e-08
ADAM_WD = 0.01
ADAM_STEP = 10

MESH = pl.DeviceIdType.MESH
_GELU_C = 0.7978845608028654
_GELU_A = 0.044715


def _cp(sem=None, vmem_mb=None):
    kw = {}
    if sem is not None:
        kw["dimension_semantics"] = sem
    if vmem_mb is not None:
        kw["vmem_limit_bytes"] = vmem_mb << 20
    return pltpu.CompilerParams(**kw)


def _dot(a, b):
    return jnp.dot(a, b, preferred_element_type=F32)


def _dot_nt(a, b):
    return lax.dot_general(a, b, (((1,), (1,)), ((), ())), preferred_element_type=F32)


def _dot_tn(a, b):
    return lax.dot_general(a, b, (((0,), (0,)), ((), ())), preferred_element_type=F32)


def _rms_r(x):
    return lax.rsqrt(jnp.mean(x * x, axis=-1, keepdims=True) + EPS)


def _rms_bwd(dh, n, r, g):
    dn = dh * g
    return r * (dn - n * jnp.mean(dn * n, axis=-1, keepdims=True))


def _gelu(x):
    t = jnp.tanh(_GELU_C * (x + _GELU_A * (x * x * x)))
    return 0.5 * x * (1.0 + t), t


def _gelu_grad(x, t):
    return 0.5 * (1.0 + t) + 0.5 * x * (1.0 - t * t) * (_GELU_C * (1.0 + 3.0 * _GELU_A * x * x))


def _sigmoid(x):
    return 1.0 / (1.0 + jnp.exp(-x))


def _tie(x, dep):
    return x if dep is None else lax.optimization_barrier((x, dep))[0]


def _row(tm, w):
    return pl.BlockSpec((tm, w), lambda i: (i, 0))


def _full(shape):
    nd = len(shape)
    return pl.BlockSpec(tuple(shape), lambda *_: (0,) * nd)


def _resident(shape):
    nd = len(shape)
    return pl.BlockSpec(tuple(shape), lambda *_: (0,) * nd, pipeline_mode=pl.Buffered(1))


def _sds(shape, dtype):
    return jax.ShapeDtypeStruct(tuple(shape), dtype)


def _hbm(*arrays):
    return [pltpu.with_memory_space_constraint(a, pltpu.HBM) for a in arrays]


HBM = pl.BlockSpec(memory_space=pltpu.HBM)
ANY = pl.BlockSpec(memory_space=pl.ANY)
SEM = pl.BlockSpec(memory_space=pltpu.SEMAPHORE)


def _band_buckets():
    i = np.arange(CHUNK)[:, None]
    j = np.arange(2 * CHUNK)[None, :]
    dist = i + CHUNK - j
    valid = (dist >= 0) & (dist < CHUNK)
    d = np.clip(dist, 0, None)
    max_exact = N_BUCKETS // 2
    large = max_exact + (np.log(np.maximum(d, 1) / max_exact) / np.log(MAX_DISTANCE / max_exact)
                         * (N_BUCKETS - max_exact)).astype(np.int32)
    large = np.minimum(large, N_BUCKETS - 1)
    buckets = np.where(d < max_exact, d, large).astype(np.int32)
    return np.where(valid, buckets, -1).astype(np.int32)


_A_COLS = slice(0, A_DIM)
_B_COLS = slice(A_DIM, A_DIM + B_DIM)
_G_COLS = slice(A_DIM + B_DIM, A_DIM + B_DIM + G_DIM)


def _inproj(x2, g_mix, w_in, tm, after=None):
    T = x2.shape[0]
    order = [] if after is None else [after]

    def body(*refs):
        x_ref, g_ref, w_ref = refs[:3]
        pg_ref, pa_ref, pb_ref, h_ref = refs[3 + len(order):]
        x = x_ref[...]
        h = (x * _rms_r(x) * g_ref[...]).astype(BF16)
        h_ref[...] = h
        pg_ref[...] = _dot(h, w_ref[:, _G_COLS]).astype(BF16)
        pa_ref[...] = _dot(h, w_ref[:, _A_COLS]).astype(BF16)
        pb_ref[...] = _dot(h, w_ref[:, _B_COLS]).astype(BF16)

    return pl.pallas_call(
        body, name="inproj", grid=(T // tm,),
        in_specs=[_row(tm, D_MODEL), _full(g_mix.shape), _resident(w_in.shape)] + [ANY] * len(order),
        out_specs=[_row(tm, G_DIM), _row(tm, A_DIM), _row(tm, B_DIM), _row(tm, D_MODEL)],
        out_shape=[_sds((T, G_DIM), BF16), _sds((T, A_DIM), BF16), _sds((T, B_DIM), BF16), _sds((T, D_MODEL), BF16)],
        compiler_params=_cp(("arbitrary",), 48),
    )(*_hbm(x2, g_mix, w_in), *order)


def _sgu_parts(p, g):
    pu = p[:, :A_WIDTH]
    pv = p[:, A_WIDTH:]
    u, tu = _gelu(pu)
    vv, tv = _gelu(pv)
    rv = _rms_r(vv)
    vn = (vv * rv * g).astype(BF16)
    return pu, pv, u, tu, vv, tv, rv, vn


def _tril():
    r = lax.broadcasted_iota(jnp.int32, (CHUNK, CHUNK), 0)
    c = lax.broadcasted_iota(jnp.int32, (CHUNK, CHUNK), 1)
    return r >= c


def _sgu_fwd(proj_a, g_sgu, w_s, b_st, tm):
    T = proj_a.shape[0]

    def body(p_ref, g_ref, ws_ref, bs_ref, y_ref):
        tril = _tril()
        _, _, u, _, _, _, _, vn = _sgu_parts(p_ref[...].astype(F32), g_ref[...])
        for gi in range(A_GROUPS):
            wm = jnp.where(tril, ws_ref[gi], 0.0).astype(BF16)
            bcol = bs_ref[:, gi:gi + 1]
            cs = slice(gi * CHUNK, (gi + 1) * CHUNK)
            for c in range(tm // CHUNK):
                rs = slice(c * CHUNK, (c + 1) * CHUNK)
                s = _dot(wm, vn[rs, cs]) + bcol
                y_ref[rs, cs] = (u[rs, cs] * s).astype(BF16)

    return pl.pallas_call(
        body, name="sgu_fwd", grid=(T // tm,),
        in_specs=[_row(tm, A_DIM), _full(g_sgu.shape), _full(w_s.shape), _full(b_st.shape)],
        out_specs=_row(tm, A_WIDTH), out_shape=_sds((T, A_WIDTH), BF16),
        compiler_params=_cp(("arbitrary",)),
    )(*_hbm(proj_a, g_sgu, w_s, b_st))


HEAD_ROWS = N_HEADS * CHUNK


def _head_rows(h):
    return slice(h * CHUNK, (h + 1) * CHUNK)


def _attn_setup(bias_scr, sink_scr, kvar_scr, qkv_ref, bk_ref, rel_ref, sink_ref):
    bk = bk_ref[...]
    for h in range(N_HEADS):
        acc = jnp.full((CHUNK, 2 * CHUNK), NEG_INF, F32)
        for b in range(N_BUCKETS):
            acc = jnp.where(bk == b, rel_ref[b, h], acc)
        bias_scr[_head_rows(h), :] = acc
        sink_scr[_head_rows(h), :] = jnp.full((CHUNK, LANES), sink_ref[0, h], F32)
    seq = qkv_ref.shape[0]
    rows_per = 2 * CHUNK
    for is_v in range(2):
        c0 = Q_DIM + is_v * KV_DIM
        for r in range(seq // rows_per):
            rs = slice(r * rows_per, (r + 1) * rows_per)
            a = qkv_ref[rs, c0:c0 + KV_DIM].astype(F32)
            lane = lax.broadcasted_iota(jnp.int32, a.shape, 1)
            lo = jnp.where(lane < HEAD_DIM, a, 0.0)
            hi = jnp.where(lane >= HEAD_DIM, a, 0.0)
            kvar_scr[4 * is_v + 0, rs, :] = lo.astype(BF16)
            kvar_scr[4 * is_v + 1, rs, :] = pltpu.roll(lo, HEAD_DIM, 1).astype(BF16)
            kvar_scr[4 * is_v + 2, rs, :] = pltpu.roll(hi, HEAD_DIM, 1).astype(BF16)
            kvar_scr[4 * is_v + 3, rs, :] = hi.astype(BF16)


def _rowsum(a, ones):
    hi = a.astype(BF16)
    lo = (a - hi.astype(F32)).astype(BF16)
    return _dot(hi, ones) + _dot(lo, ones)


def _both(a):
    return jnp.concatenate([a, a], axis=1)


def _attn_probs(qkv_ref, r0, n, kv, bias_scr, sink_scr, ones):
    s = jnp.concatenate([_dot_nt(qkv_ref[pl.ds(r0, CHUNK), (h // 2) * LANES:(h // 2 + 1) * LANES], kv[h // 4][h % 2])
                         for h in range(N_HEADS)], axis=0)
    s = s * (HEAD_DIM ** -0.5) + bias_scr[...]
    col = lax.broadcasted_iota(jnp.int32, s.shape, 1)
    s = jnp.where((col < CHUNK) & (n == 0), NEG_INF, s)
    sink = sink_scr[...]
    m = jnp.maximum(jnp.max(s, axis=-1, keepdims=True), sink)
    p = jnp.exp(s - _both(m))
    es = jnp.exp(sink - m)
    inv = 1.0 / (_rowsum(p, ones) + es)
    return p * _both(inv), es * inv


def _attn_block_inputs(kvar_scr, n):
    r0 = pl.multiple_of(n * CHUNK, CHUNK)
    rp = pl.multiple_of(jnp.maximum(n - 1, 0) * CHUNK, CHUNK)

    def both(idx):
        return jnp.concatenate([kvar_scr[idx, pl.ds(rp, CHUNK), :], kvar_scr[idx, pl.ds(r0, CHUNK), :]], axis=0)

    kv = ((both(0), both(1)), (both(2), both(3)))
    vv = ((both(4), both(5)), (both(6), both(7)))
    return r0, kv, vv


def _attn_fwd(proj_b, sinks, rel_bias, n_seq, seq):
    nb = seq // CHUNK
    bk = jnp.asarray(_band_buckets())

    def body(qkv_ref, bk_ref, rel_ref, sink_ref, o_ref, bias_scr, sink_scr, kvar_scr):
        _attn_setup(bias_scr, sink_scr, kvar_scr, qkv_ref, bk_ref, rel_ref, sink_ref)
        ones = jnp.ones((2 * CHUNK, LANES), BF16)

        def blk(n, carry):
            r0, kv, vv = _attn_block_inputs(kvar_scr, n)
            prob, _ = _attn_probs(qkv_ref, r0, n, kv, bias_scr, sink_scr, ones)
            pb = prob.astype(BF16)
            for pr in range(N_HEADS // 2):
                acc = _dot(pb[_head_rows(2 * pr)], vv[pr // 2][0]) + _dot(pb[_head_rows(2 * pr + 1)], vv[pr // 2][1])
                o_ref[pl.ds(r0, CHUNK), pr * LANES:(pr + 1) * LANES] = acc.astype(BF16)
            return carry

        lax.fori_loop(0, nb, blk, 0)

    smem = pl.BlockSpec(memory_space=pltpu.SMEM)
    return pl.pallas_call(
        body, name="attn_fwd", grid=(n_seq,),
        in_specs=[_row(seq, B_DIM), _full(bk.shape), smem, smem],
        out_specs=_row(seq, Q_DIM), out_shape=_sds((n_seq * seq, Q_DIM), BF16),
        scratch_shapes=[pltpu.VMEM((HEAD_ROWS, 2 * CHUNK), F32), pltpu.VMEM((HEAD_ROWS, LANES), F32),
                        pltpu.VMEM((8, seq, KV_DIM), BF16)],
        compiler_params=_cp(("arbitrary",), 40),
    )(*_hbm(proj_b, bk), rel_bias, sinks)


def _dot_stacked(a, w_ref):
    return jnp.concatenate([_dot(a, w_ref[i]) for i in range(N_CHIPS)], axis=1)


def _dot_nt_stacked(a, w_ref):
    w = w_ref.shape[2]
    acc = _dot_nt(a[:, :w], w_ref[0])
    for i in range(1, N_CHIPS):
        acc = acc + _dot_nt(a[:, i * w:(i + 1) * w], w_ref[i])
    return acc


def _merge_fwd(x2, y_a, y_b, proj_g, w_pa, w_pb, w_out, tm):
    T = x2.shape[0]

    def body(x_ref, ya_ref, yb_ref, g_ref, wpa_ref, wpb_ref, wo_ref, x1_ref, mg_ref):
        g = g_ref[...].astype(F32)
        pa = _dot_stacked(ya_ref[...], wpa_ref)
        pb = _dot_stacked(yb_ref[...], wpb_ref)
        merged = (_sigmoid(g[:, :D_MODEL]) * pa + _sigmoid(g[:, D_MODEL:]) * pb).astype(BF16)
        mg_ref[...] = merged
        x1_ref[...] = x_ref[...] + _dot(merged, wo_ref[...])

    return pl.pallas_call(
        body, name="merge_fwd", grid=(T // tm,),
        in_specs=[_row(tm, D_MODEL), _row(tm, A_WIDTH), _row(tm, Q_DIM), _row(tm, G_DIM),
                  _resident(w_pa.shape), _resident(w_pb.shape), _resident(w_out.shape)],
        out_specs=[_row(tm, D_MODEL), _row(tm, D_MODEL)],
        out_shape=[_sds((T, D_MODEL), F32), _sds((T, D_MODEL), BF16)],
        compiler_params=_cp(("arbitrary",), 40),
    )(*_hbm(x2, y_a, y_b, proj_g, w_pa, w_pb, w_out))


def _upproj(x1, g_ffn, w_up, w_conv, b_conv, tm, seq):
    T = x1.shape[0]
    cw = w_up.shape[2]
    tiles_per_seq = seq // tm

    def body(x_ref, g_ref, w_ref, wc_ref, bc_ref, u_ref, h_ref, gate_ref, val_ref, tail_scr):
        at_start = (pl.program_id(0) % tiles_per_seq) == 0
        x = x_ref[...]
        h = (x * _rms_r(x) * g_ref[...]).astype(BF16)
        h_ref[...] = h
        for i in range(N_CHIPS):
            cs = slice(i * cw, (i + 1) * cw)
            u = _dot(h, w_ref[i])
            u_ref[:, cs] = u.astype(BF16)
            hl = jnp.where(at_start, 0.0, tail_scr[SUBLANES - 2:SUBLANES, cs])
            tail_scr[:, cs] = u[tm - SUBLANES:]
            up = _conv_out((u, _shift_down(u, hl, 1), _shift_down(u, hl, 2)), wc_ref[:, cs], bc_ref[:, cs])
            out_ref = gate_ref if i < N_CHIPS // 2 else val_ref
            out_ref[:, (i % 2) * cw:(i % 2 + 1) * cw] = up.astype(BF16)

    return pl.pallas_call(
        body, name="upproj", grid=(T // tm,),
        in_specs=[_row(tm, D_MODEL), _full(g_ffn.shape), _resident(w_up.shape), _full(w_conv.shape), _full(b_conv.shape)],
        out_specs=[_row(tm, 2 * D_FF), _row(tm, D_MODEL), _row(tm, D_FF), _row(tm, D_FF)],
        out_shape=[_sds((T, 2 * D_FF), BF16), _sds((T, D_MODEL), BF16), _sds((T, D_FF), BF16), _sds((T, D_FF), BF16)],
        scratch_shapes=[pltpu.VMEM((SUBLANES, 2 * D_FF), F32)],
        compiler_params=_cp(("arbitrary",), 56),
    )(*_hbm(x1, g_ffn, w_up, w_conv, b_conv))


def _shift_down(u, halo, k):
    rolled = pltpu.roll(u, k, 0)
    head = rolled[:SUBLANES]
    row = lax.broadcasted_iota(jnp.int32, head.shape, 0)
    if k == 1:
        head = jnp.where(row == 0, halo[1:2], head)
    else:
        head = jnp.where(row == 0, halo[0:1], jnp.where(row == 1, halo[1:2], head))
    return jnp.concatenate([head, rolled[SUBLANES:]], axis=0)


def _shift_up(d, halo, k):
    tm = d.shape[0]
    rolled = pltpu.roll(d, tm - k, 0)
    tail = rolled[tm - SUBLANES:]
    row = lax.broadcasted_iota(jnp.int32, tail.shape, 0)
    if k == 1:
        tail = jnp.where(row == SUBLANES - 1, halo[0:1], tail)
    else:
        tail = jnp.where(row == SUBLANES - 2, halo[0:1], jnp.where(row == SUBLANES - 1, halo[1:2], tail))
    return jnp.concatenate([rolled[:tm - SUBLANES], tail], axis=0)


def _conv_out(taps, wc, bc):
    u, u1, u2 = taps
    return wc[0:1] * u2 + wc[1:2] * u1 + wc[2:3] * u + bc


def _ffn_down_loss(gate, val, x1, target, w_down, g_final, tm):
    T = x1.shape[0]
    half = D_FF // 2

    def body(gt_ref, vl_ref, x1_ref, t_ref, wd_ref, g_ref, dx2_ref, loss_ref, gg_ref):
        i = pl.program_id(0)
        acc = jnp.zeros((tm, D_MODEL), F32)
        for j in range(2):
            gc = slice(j * half, (j + 1) * half)
            gate = gt_ref[:, gc].astype(F32)
            act = (gate * _sigmoid(gate) * vl_ref[:, gc].astype(F32)).astype(BF16)
            acc = acc + _dot(act, wd_ref[gc, :])
        x2 = x1_ref[...] + acc
        r = _rms_r(x2)
        n = x2 * r
        g = g_ref[...]
        diff = n * g - t_ref[...]
        dy = diff * (1.0 / D_MODEL)
        dx2_ref[...] = _rms_bwd(dy, n, r, g)

        @pl.when(i == 0)
        def _():
            loss_ref[...] = jnp.zeros_like(loss_ref)
            gg_ref[...] = jnp.zeros_like(gg_ref)

        loss_ref[...] += 0.5 * jnp.sum(jnp.mean(diff * diff, axis=-1, keepdims=True), axis=0, keepdims=True)
        gg_ref[...] += jnp.sum(dy * n, axis=0, keepdims=True)

    return pl.pallas_call(
        body, name="ffn_down_loss", grid=(T // tm,),
        in_specs=[_row(tm, D_FF), _row(tm, D_FF), _row(tm, D_MODEL), _row(tm, D_MODEL),
                  _resident(w_down.shape), _full(g_final.shape)],
        out_specs=[_row(tm, D_MODEL), _full((1, 1)), _full((1, D_MODEL))],
        out_shape=[_sds((T, D_MODEL), F32), _sds((1, 1), F32), _sds((1, D_MODEL), F32)],
        compiler_params=_cp(("arbitrary",), 48),
    )(*_hbm(gate, val, x1, target, w_down, g_final))


def _ffn_bwd_act(gate, val, dx2, w_down, tm):
    T = dx2.shape[0]
    half = D_FF // 2
    nt = T // tm

    def body(g_ref, v_ref, dx_ref, wd_ref, dg_ref, dv_ref, gwd_out, gbg_ref, gbv_ref, gwd_ref):
        i = pl.program_id(1)
        gate = g_ref[...].astype(F32)
        val = v_ref[...].astype(F32)
        sg = _sigmoid(gate)
        silu = gate * sg
        dx = dx_ref[...].astype(BF16)
        d_act = _dot_nt(dx, wd_ref[...])
        d_val = d_act * silu
        d_gate = d_act * val * (sg * (1.0 + gate * (1.0 - sg)))
        dg_ref[...] = d_gate.astype(BF16)
        dv_ref[...] = d_val.astype(BF16)

        @pl.when(i == 0)
        def _():
            for r in (gwd_ref, gbg_ref, gbv_ref):
                r[...] = jnp.zeros_like(r)

        gwd_ref[...] += _dot_tn((silu * val).astype(BF16), dx)
        gbg_ref[...] += jnp.sum(d_gate, axis=0, keepdims=True)
        gbv_ref[...] += jnp.sum(d_val, axis=0, keepdims=True)

        @pl.when(i == nt - 1)
        def _():
            gwd_out[...] = gwd_ref[...].astype(BF16)

    tile = pl.BlockSpec((tm, half), lambda j, i: (i, j))
    vec = pl.BlockSpec((1, half), lambda j, i: (0, j))
    wrows = pl.BlockSpec((half, D_MODEL), lambda j, i: (j, 0))
    return pl.pallas_call(
        body, name="ffn_bwd_act", grid=(2, nt),
        in_specs=[tile, tile, pl.BlockSpec((tm, D_MODEL), lambda j, i: (i, 0)), wrows],
        out_specs=[tile, tile, wrows, vec, vec],
        out_shape=[_sds((T, D_FF), BF16), _sds((T, D_FF), BF16), _sds((D_FF, D_MODEL), BF16),
                   _sds((1, D_FF), F32), _sds((1, D_FF), F32)],
        scratch_shapes=[pltpu.VMEM((half, D_MODEL), F32)],
        compiler_params=_cp(("arbitrary", "arbitrary"), 56),
    )(*_hbm(gate, val, dx2, w_down))


def _ffn_bwd_up(d_gate, d_val, upre, dx2, x1, g_ffn, w_conv, w_up, tm, seq):
    T = dx2.shape[0]
    tiles_per_seq = seq // tm
    k16 = tm // BF16_ROWS
    n16 = T // BF16_ROWS
    cw = D_FF // 2

    def body(dg_ref, dv_ref, hg_ref, hv_ref, u_ref, dx2_ref, x1_ref, g_ref, wc_ref, wu_ref, du_ref, dx1_ref, gg_ref, gwc_ref):
        i = pl.program_id(0)
        at_end = (i % tiles_per_seq) == tiles_per_seq - 1

        @pl.when(i == 0)
        def _():
            gg_ref[...] = jnp.zeros_like(gg_ref)
            gwc_ref[...] = jnp.zeros_like(gwc_ref)

        dh = jnp.zeros((tm, D_MODEL), F32)
        for j in range(4):
            src, hsrc = (dg_ref, hg_ref) if j < 2 else (dv_ref, hv_ref)
            ls = slice((j % 2) * cw, (j % 2 + 1) * cw)
            cs = slice(j * cw, (j + 1) * cw)
            d = src[:, ls].astype(F32)
            hl = hsrc[:, ls].astype(F32)[0:2]
            hl = jnp.where(at_end, 0.0, hl)
            wc = wc_ref[:, cs]
            d1 = _shift_up(d, hl, 1)
            d2 = _shift_up(d, hl, 2)
            du = (wc[2:3] * d + wc[1:2] * d1 + wc[0:1] * d2).astype(BF16)
            du_ref[:, cs] = du
            dh = dh + _dot_nt(du, wu_ref[j])
            u = u_ref[:, cs].astype(F32)
            gwc_ref[0:1, cs] += jnp.sum(d2 * u, axis=0, keepdims=True)
            gwc_ref[1:2, cs] += jnp.sum(d1 * u, axis=0, keepdims=True)
            gwc_ref[2:3, cs] += jnp.sum(d * u, axis=0, keepdims=True)
        x = x1_ref[...]
        r = _rms_r(x)
        n = x * r
        dx1_ref[...] = dx2_ref[...] + _rms_bwd(dh, n, r, g_ref[...])
        gg_ref[...] += jnp.sum(dh * n, axis=0, keepdims=True)

    nxt = pl.BlockSpec((BF16_ROWS, D_FF), lambda i: (jnp.minimum((i + 1) * k16, n16 - 1), 0))
    return pl.pallas_call(
        body, name="ffn_bwd_up", grid=(T // tm,),
        in_specs=[_row(tm, D_FF), _row(tm, D_FF), nxt, nxt, _row(tm, 2 * D_FF), _row(tm, D_MODEL), _row(tm, D_MODEL),
                  _full(g_ffn.shape), _full(w_conv.shape), _resident(w_up.shape)],
        out_specs=[_row(tm, 2 * D_FF), _row(tm, D_MODEL), _full((1, D_MODEL)), _full((3, 2 * D_FF))],
        out_shape=[_sds((T, 2 * D_FF), BF16), _sds((T, D_MODEL), F32), _sds((1, D_MODEL), F32), _sds((3, 2 * D_FF), F32)],
        compiler_params=_cp(("arbitrary",), 56),
    )(*_hbm(d_gate, d_val, d_gate, d_val, upre, dx2, x1, g_ffn, w_conv, w_up))


def _matmul_tn(a, b, tn, tk, name):
    T, M = a.shape
    N = b.shape[1]
    nk = T // tk

    def body(a_ref, b_ref, o_ref, acc_ref):
        k = pl.program_id(1)

        @pl.when(k == 0)
        def _():
            acc_ref[...] = jnp.zeros_like(acc_ref)

        acc_ref[...] += _dot_tn(a_ref[...], b_ref[...])

        @pl.when(k == nk - 1)
        def _():
            o_ref[...] = acc_ref[...].astype(BF16)

    return pl.pallas_call(
        body, name=name, grid=(N // tn, nk),
        in_specs=[pl.BlockSpec((tk, M), lambda j, k: (k, 0)), pl.BlockSpec((tk, tn), lambda j, k: (k, j))],
        out_specs=pl.BlockSpec((M, tn), lambda j, k: (0, j)), out_shape=_sds((M, N), BF16),
        scratch_shapes=[pltpu.VMEM((M, tn), F32)],
        compiler_params=_cp(("arbitrary", "arbitrary"), 48),
    )(*_hbm(a, b))


def _merge_bwd(dx1, merged, y_a, y_b, proj_g, w_pa, w_pb, w_out, tm, after=None):
    T = dx1.shape[0]

    nt = T // tm
    pshape = (A_WIDTH, D_MODEL)
    order = [] if after is None else [after]

    def body(*refs):
        dx_ref, mg_ref, ya_ref, yb_ref, g_ref, wpa_ref, wpb_ref, wo_ref = refs[:8]
        dg_ref, dya_ref, dyb_ref, gwo_out, gwpa_out, gwpb_out, gwo_ref, gwpa_ref, gwpb_ref = refs[8 + len(order):]
        i = pl.program_id(0)
        dx = dx_ref[...].astype(BF16)
        dm = _dot_nt(dx, wo_ref[...])
        g = g_ref[...].astype(F32)
        ya = ya_ref[...]
        yb = yb_ref[...]
        pa = _dot_stacked(ya, wpa_ref)
        pb = _dot_stacked(yb, wpb_ref)
        sa = _sigmoid(g[:, :D_MODEL])
        sb = _sigmoid(g[:, D_MODEL:])
        dpa = (dm * sa).astype(BF16)
        dpb = (dm * sb).astype(BF16)
        dg_ref[:, :D_MODEL] = (dm * pa * (sa * (1.0 - sa))).astype(BF16)
        dg_ref[:, D_MODEL:] = (dm * pb * (sb * (1.0 - sb))).astype(BF16)
        dya_ref[...] = _dot_nt_stacked(dpa, wpa_ref).astype(BF16)
        dyb_ref[...] = _dot_nt_stacked(dpb, wpb_ref).astype(BF16)

        @pl.when(i == 0)
        def _():
            for r in (gwo_ref, gwpa_ref, gwpb_ref):
                r[...] = jnp.zeros_like(r)

        gwo_ref[...] += _dot_tn(mg_ref[...], dx)
        gwpa_ref[...] += _dot_tn(ya, dpa)
        gwpb_ref[...] += _dot_tn(yb, dpb)

        @pl.when(i == nt - 1)
        def _():
            gwo_out[...] = gwo_ref[...].astype(BF16)
            gwpa_out[...] = gwpa_ref[...].astype(BF16)
            gwpb_out[...] = gwpb_ref[...].astype(BF16)

    return pl.pallas_call(
        body, name="merge_bwd", grid=(nt,),
        in_specs=[_row(tm, D_MODEL), _row(tm, D_MODEL), _row(tm, A_WIDTH), _row(tm, Q_DIM), _row(tm, G_DIM),
                  _resident(w_pa.shape), _resident(w_pb.shape), _resident(w_out.shape)] + [ANY] * len(order),
        out_specs=[_row(tm, G_DIM), _row(tm, A_WIDTH), _row(tm, Q_DIM),
                   _full(w_out.shape), _full(pshape), _full(pshape)],
        out_shape=[_sds((T, G_DIM), BF16), _sds((T, A_WIDTH), BF16), _sds((T, Q_DIM), BF16),
                   _sds(w_out.shape, BF16), _sds(pshape, BF16), _sds(pshape, BF16)],
        scratch_shapes=[pltpu.VMEM(w_out.shape, F32), pltpu.VMEM(pshape, F32), pltpu.VMEM(pshape, F32)],
        compiler_params=_cp(("arbitrary",), 56),
    )(*_hbm(dx1, merged, y_a, y_b, proj_g, w_pa, w_pb, w_out), *order)


def _sgu_bwd(proj_a, d_ya, g_sgu, w_s, b_st, tm, after=None):
    T = proj_a.shape[0]
    order = [] if after is None else [after]

    def body(*refs):
        p_ref, dy_ref, g_ref, ws_ref, bs_ref = refs[:5]
        dp_ref, gws_ref, gbs_ref, gg_ref = refs[5 + len(order):]
        tril = _tril()
        g = g_ref[...]
        pu, pv, u, tu, vv, tv, rv, vn = _sgu_parts(p_ref[...].astype(F32), g)
        dy = dy_ref[...].astype(F32)

        @pl.when(pl.program_id(0) == 0)
        def _():
            for r in (gws_ref, gbs_ref, gg_ref):
                r[...] = jnp.zeros_like(r)

        du_cols = []
        dvn_cols = []
        for gi in range(A_GROUPS):
            wm = jnp.where(tril, ws_ref[gi], 0.0).astype(BF16)
            wmt = wm.astype(F32).T.astype(BF16)
            bcol = bs_ref[:, gi:gi + 1]
            cs = slice(gi * CHUNK, (gi + 1) * CHUNK)
            du_rows = []
            dvn_rows = []
            gw = jnp.zeros((CHUNK, CHUNK), F32)
            gb = jnp.zeros((CHUNK, 1), F32)
            for c in range(tm // CHUNK):
                rs = slice(c * CHUNK, (c + 1) * CHUNK)
                vn_c = vn[rs, cs]
                s = _dot(wm, vn_c) + bcol
                dy_c = dy[rs, cs]
                ds = dy_c * u[rs, cs]
                du_rows.append(dy_c * s)
                dsb = ds.astype(BF16)
                gw = gw + _dot_nt(dsb, vn_c)
                gb = gb + jnp.sum(ds, axis=-1, keepdims=True)
                dvn_rows.append(_dot(wmt, dsb))
            gws_ref[gi] += jnp.where(tril, gw, 0.0)
            gbs_ref[:, gi:gi + 1] += gb
            du_cols.append(jnp.concatenate(du_rows, axis=0))
            dvn_cols.append(jnp.concatenate(dvn_rows, axis=0))
        du = jnp.concatenate(du_cols, axis=1)
        dvn = jnp.concatenate(dvn_cols, axis=1)
        vhat = vv * rv
        gg_ref[...] += jnp.sum(dvn * vhat, axis=0, keepdims=True)
        dvv = _rms_bwd(dvn, vhat, rv, g)
        dp_ref[:, :A_WIDTH] = (du * _gelu_grad(pu, tu)).astype(BF16)
        dp_ref[:, A_WIDTH:] = (dvv * _gelu_grad(pv, tv)).astype(BF16)

    return pl.pallas_call(
        body, name="sgu_bwd", grid=(T // tm,),
        in_specs=[_row(tm, A_DIM), _row(tm, A_WIDTH), _full(g_sgu.shape), _full(w_s.shape), _full(b_st.shape)] + [ANY] * len(order),
        out_specs=[_row(tm, A_DIM), _full(w_s.shape), _full(b_st.shape), _full(g_sgu.shape)],
        out_shape=[_sds((T, A_DIM), BF16), _sds(w_s.shape, F32), _sds(b_st.shape, F32), _sds(g_sgu.shape, F32)],
        compiler_params=_cp(("arbitrary",)),
    )(*_hbm(proj_a, d_ya, g_sgu, w_s, b_st), *order)


def _attn_bwd(proj_b, d_yb, sinks, rel_bias, n_seq, seq):
    nb = seq // CHUNK
    bk = jnp.asarray(_band_buckets())

    def body(qkv_ref, do_ref, bk_ref, rel_ref, sink_ref, d_ref, gs_ref, gr_ref,
             bias_scr, sink_scr, kvar_scr, dbias_scr, dk_scr, dv_scr, ds_scr):
        b = pl.program_id(0)
        _attn_setup(bias_scr, sink_scr, kvar_scr, qkv_ref, bk_ref, rel_ref, sink_ref)
        ones = jnp.ones((2 * CHUNK, LANES), BF16)

        @pl.when(b == 0)
        def _():
            dbias_scr[...] = jnp.zeros_like(dbias_scr)
            ds_scr[...] = jnp.zeros_like(ds_scr)

        dk_scr[...] = jnp.zeros_like(dk_scr)
        dv_scr[...] = jnp.zeros_like(dv_scr)

        def transposed(a):
            return a.astype(F32).T.astype(BF16)

        def blk(n, carry):
            r0, kv, vv = _attn_block_inputs(kvar_scr, n)
            prob, psink = _attn_probs(qkv_ref, r0, n, kv, bias_scr, sink_scr, ones)
            dp = jnp.concatenate([_dot_nt(do_ref[pl.ds(r0, CHUNK), (h // 2) * LANES:(h // 2 + 1) * LANES], vv[h // 4][h % 2])
                                  for h in range(N_HEADS)], axis=0)
            delta = _rowsum(prob * dp, ones)
            dsc = prob * (dp - _both(delta))
            ds_scr[...] += psink * delta
            dbias_scr[...] += dsc
            dsb = (dsc * (HEAD_DIM ** -0.5)).astype(BF16)
            pb = prob.astype(BF16)
            dkt = [jnp.zeros((HEAD_DIM, 2 * CHUNK), F32) for _ in range(2)]
            dvt = [jnp.zeros((HEAD_DIM, 2 * CHUNK), F32) for _ in range(2)]
            for pr in range(N_HEADS // 2):
                ps = slice(pr * LANES, (pr + 1) * LANES)
                qpt = transposed(qkv_ref[pl.ds(r0, CHUNK), ps])
                dopt = transposed(do_ref[pl.ds(r0, CHUNK), ps])
                kvh = pr // 2
                dq = jnp.zeros((CHUNK, LANES), F32)
                for hh in range(2):
                    hr = _head_rows(2 * pr + hh)
                    rows = slice(hh * HEAD_DIM, (hh + 1) * HEAD_DIM)
                    dq = dq + _dot(dsb[hr], kv[kvh][hh])
                    dkt[kvh] = dkt[kvh] + _dot(qpt, dsb[hr])[rows]
                    dvt[kvh] = dvt[kvh] + _dot(dopt, pb[hr])[rows]
                d_ref[pl.ds(r0, CHUNK), ps] = dq.astype(BF16)
            dk_scr[:, pl.ds(r0, 2 * CHUNK)] += jnp.concatenate(dkt, axis=0)
            dv_scr[:, pl.ds(r0, 2 * CHUNK)] += jnp.concatenate(dvt, axis=0)
            return carry

        lax.fori_loop(0, nb, blk, 0)
        for n in range(nb):
            rows = slice(n * CHUNK, (n + 1) * CHUNK)
            cols = slice((n + 1) * CHUNK, (n + 2) * CHUNK)
            d_ref[rows, Q_DIM:Q_DIM + KV_DIM] = dk_scr[:, cols].T.astype(BF16)
            d_ref[rows, Q_DIM + KV_DIM:] = dv_scr[:, cols].T.astype(BF16)

        @pl.when(b == n_seq - 1)
        def _():
            bkv = bk_ref[...]
            for h in range(N_HEADS):
                gs_ref[0:1, h:h + 1] = -jnp.sum(ds_scr[_head_rows(h), 0:1], axis=0, keepdims=True)
                db = dbias_scr[_head_rows(h), :]
                for bb in range(N_BUCKETS):
                    part = jnp.sum(jnp.where(bkv == bb, db, 0.0), axis=-1, keepdims=True)
                    gr_ref[bb:bb + 1, h:h + 1] = jnp.sum(part, axis=0, keepdims=True)

    smem = pl.BlockSpec(memory_space=pltpu.SMEM)
    return pl.pallas_call(
        body, name="attn_bwd", grid=(n_seq,),
        in_specs=[_row(seq, B_DIM), _row(seq, Q_DIM), _full(bk.shape), smem, smem],
        out_specs=[_row(seq, B_DIM), _full((1, N_HEADS)), _full((N_BUCKETS, N_HEADS))],
        out_shape=[_sds((n_seq * seq, B_DIM), BF16), _sds((1, N_HEADS), F32), _sds((N_BUCKETS, N_HEADS), F32)],
        scratch_shapes=[pltpu.VMEM((HEAD_ROWS, 2 * CHUNK), F32), pltpu.VMEM((HEAD_ROWS, LANES), F32),
                        pltpu.VMEM((8, seq, KV_DIM), BF16), pltpu.VMEM((HEAD_ROWS, 2 * CHUNK), F32),
                        pltpu.VMEM((KV_DIM, seq + CHUNK), F32), pltpu.VMEM((KV_DIM, seq + CHUNK), F32),
                        pltpu.VMEM((HEAD_ROWS, LANES), F32)],
        compiler_params=_cp(("arbitrary",), 40),
    )(*_hbm(proj_b, d_yb, bk), rel_bias, sinks)


def _inproj_bwd(d_g, d_a, d_b, x2, dx1, g_mix, w_in, tm, after=None):
    T = x2.shape[0]
    order = [] if after is None else [after]

    def body(*refs):
        dg_ref, da_ref, db_ref, x_ref, dx1_ref, g_ref, w_ref = refs[:7]
        gx_ref, gg_ref = refs[7 + len(order):]
        dh = (_dot_nt(dg_ref[...], w_ref[:, _G_COLS]) + _dot_nt(da_ref[...], w_ref[:, _A_COLS])
              + _dot_nt(db_ref[...], w_ref[:, _B_COLS]))
        x = x_ref[...]
        r = _rms_r(x)
        n = x * r
        gx_ref[...] = dx1_ref[...] + _rms_bwd(dh, n, r, g_ref[...])

        @pl.when(pl.program_id(0) == 0)
        def _():
            gg_ref[...] = jnp.zeros_like(gg_ref)

        gg_ref[...] += jnp.sum(dh * n, axis=0, keepdims=True)

    return pl.pallas_call(
        body, name="inproj_bwd", grid=(T // tm,),
        in_specs=[_row(tm, G_DIM), _row(tm, A_DIM), _row(tm, B_DIM), _row(tm, D_MODEL), _row(tm, D_MODEL),
                  _full(g_mix.shape), _resident(w_in.shape)] + [ANY] * len(order),
        out_specs=[_row(tm, D_MODEL), _full((1, D_MODEL))],
        out_shape=[_sds((T, D_MODEL), F32), _sds((1, D_MODEL), F32)],
        compiler_params=_cp(("arbitrary",), 48),
    )(*_hbm(d_g, d_a, d_b, x2, dx1, g_mix, w_in), *order)


IN_SHARD = (A_DIM + B_DIM + G_DIM) // N_CHIPS


def _unstack_w_in(stack):
    tr = 256

    def body(s_ref, o_ref):
        for i in range(N_CHIPS):
            o_ref[:, i * IN_SHARD:(i + 1) * IN_SHARD] = s_ref[i]

    return pl.pallas_call(
        body, name="unstack_w_in", grid=(D_MODEL // tr,),
        in_specs=[pl.BlockSpec((N_CHIPS, tr, IN_SHARD), lambda r: (0, r, 0))],
        out_specs=pl.BlockSpec((tr, N_CHIPS * IN_SHARD), lambda r: (r, 0)),
        out_shape=_sds((D_MODEL, N_CHIPS * IN_SHARD), stack.dtype),
        compiler_params=_cp(("arbitrary",)),
    )(*_hbm(stack))


def _stack_grad_w_in(gw_a, gw_b, gw_g):
    tr = 256

    def body(a_ref, b_ref, g_ref, o_ref):
        full = jnp.concatenate([a_ref[...], b_ref[...], g_ref[...]], axis=1)
        for i in range(N_CHIPS):
            o_ref[i] = full[:, i * IN_SHARD:(i + 1) * IN_SHARD]

    return pl.pallas_call(
        body, name="stack_grad_w_in", grid=(D_MODEL // tr,),
        in_specs=[_row(tr, A_DIM), _row(tr, B_DIM), _row(tr, G_DIM)],
        out_specs=pl.BlockSpec((N_CHIPS, tr, IN_SHARD), lambda r: (0, r, 0)),
        out_shape=_sds((N_CHIPS, D_MODEL, IN_SHARD), gw_a.dtype),
        compiler_params=_cp(("arbitrary",)),
    )(*_hbm(gw_a, gw_b, gw_g))


def _local_step(x, target, g_mix, g_sgu, w_s, b_s, sinks, rel_bias, g_ffn, b_conv, g_final,
                w_in, w_conv, late_weights, on_grads, after=None):
    n_seq, seq, _ = x.shape
    T = n_seq * seq
    tm = min(ROW_TILE, seq)
    tw = min(GRAD_ROW_TILE, T)
    tf = min(WIDE_ROW_TILE, seq)
    x2 = x.reshape(T, D_MODEL)
    tgt = target.reshape(T, D_MODEL)
    b_st = b_s.T
    g_fin = g_final.reshape(1, D_MODEL)

    proj_g, proj_a, proj_b, h = _inproj(x2, g_mix, w_in, tm, after)
    y_a = _sgu_fwd(proj_a, g_sgu, w_s, b_st, tm)
    y_b = _attn_fwd(proj_b, sinks, rel_bias, n_seq, seq)
    w_pa, w_pb, w_out, w_up, w_down = late_weights(y_b)
    x1, merged = _merge_fwd(x2, y_a, y_b, proj_g, w_pa, w_pb, w_out, tm)
    upre, h2, gate, val = _upproj(x1, g_ffn, w_up, w_conv, b_conv, tf, seq)
    dx2, loss, gg_final = _ffn_down_loss(gate, val, x1, tgt, w_down, g_fin, tm)

    d_gate, d_val, gw_down, gb_g, gb_v = _ffn_bwd_act(gate, val, dx2, w_down, tw)
    gb_conv = jnp.concatenate([gb_g, gb_v], axis=1)
    d_upre, dx1, gg_ffn, gw_conv = _ffn_bwd_up(d_gate, d_val, upre, dx2, x1, g_ffn, w_conv, w_up, tf, seq)
    gw_up = _matmul_tn(h2, d_upre, 2 * D_FF // 4, min(2 * GRAD_ROW_TILE, T), "grad_w_up")
    sent = on_grads("ffn", dict(w_up=gw_up, w_down=gw_down))
    d_g, d_ya, d_yb, gw_out, gw_pa, gw_pb = _merge_bwd(dx1, merged, y_a, y_b, proj_g, w_pa, w_pb, w_out, tf, sent)
    sent = on_grads("proj", dict(w_pa=gw_pa, w_pb=gw_pb, w_out=gw_out))
    d_a, gw_s, gb_st, gg_sgu = _sgu_bwd(proj_a, d_ya, g_sgu, w_s, b_st, tm, sent)
    d_b, g_sinks, g_rel = _attn_bwd(proj_b, _tie(d_yb, d_a), sinks, rel_bias, n_seq, seq)
    gw_g = _matmul_tn(h, _tie(d_g, d_b), D_MODEL, min(2 * GRAD_ROW_TILE, T), "grad_w_in_gate")
    gw_a = _matmul_tn(h, _tie(d_a, gw_g), A_DIM, min(2 * GRAD_ROW_TILE, T), "grad_w_in_a")
    gw_b = _matmul_tn(h, _tie(d_b, gw_a), B_DIM, min(2 * GRAD_ROW_TILE, T), "grad_w_in_b")
    gw_in = _stack_grad_w_in(gw_a, gw_b, gw_g)
    sent = on_grads("in", dict(w_in=gw_in))
    grad_x, gg_mix = _inproj_bwd(d_g, d_a, d_b, x2, dx1, g_mix, w_in, tm, sent)

    small = dict(g_mix=gg_mix, g_sgu=gg_sgu, w_s=gw_s, b_s=gb_st.T, sinks=g_sinks, rel_bias=g_rel,
                 g_ffn=gg_ffn, b_conv=gb_conv, g_final=gg_final, w_conv=gw_conv)
    big = dict(w_in=gw_in, w_pa=gw_pa, w_pb=gw_pb, w_out=gw_out, w_up=gw_up, w_down=gw_down)
    return loss, grad_x.reshape(x.shape), small, big


_MIXER = ("w_in", "w_pa", "w_pb", "w_out")
_FFN = ("w_up", "w_down")
_BIG = _MIXER + _FFN

_SMALL = (("loss", (1, 1)), ("g_final", (1, D_MODEL)), ("g_mix", (1, D_MODEL)), ("g_ffn", (1, D_MODEL)),
          ("g_sgu", (1, A_WIDTH)), ("b_s", (A_GROUPS, CHUNK)), ("sinks", (1, N_HEADS)), ("rel_bias", (N_BUCKETS, N_HEADS)),
          ("b_conv", (1, 2 * D_FF)), ("w_conv", (3, 2 * D_FF)), ("w_s", (A_GROUPS, CHUNK, CHUNK)))
SMALL_ROWS = 96


def _pack_small(vals):
    flat = jnp.concatenate([vals[n].astype(F32).reshape(-1) for n, _ in _SMALL])
    flat = jnp.pad(flat, (0, SMALL_ROWS * D_MODEL - flat.shape[0]))
    return flat.reshape(SMALL_ROWS, D_MODEL)


def _unpack_small(buf):
    flat = buf.reshape(-1)
    out = {}
    off = 0
    for n, shp in _SMALL:
        k = int(np.prod(shp))
        out[n] = flat[off:off + k].reshape(shp)
        off += k
    return out


def _mesh_pos():
    return lax.axis_index("x"), lax.axis_index("y"), lax.axis_index("c")


def _other_chips(x, y):
    return [(1 - x, y), (x, 1 - y), (1 - x, 1 - y)]


def _remote(src, dst, send_sem, recv_sem, to):
    return pltpu.make_async_remote_copy(src_ref=src, dst_ref=dst, send_sem=send_sem, recv_sem=recv_sem,
                                        device_id=to, device_id_type=MESH)


def _own_slot(own, n, at):
    return lax.dynamic_update_slice(lax.empty((n,) + own.shape, own.dtype), own[None], (at,) + (0,) * own.ndim)


def _allgather_weights(stacks, wc_stack):
    names = list(stacks)
    n = len(names)

    def body(*refs):
        ins, outs = refs[:n + 1], refs[n + 1:2 * n + 2]
        send_sems, recv_sems = refs[2 * n + 2:]
        x, y, c = _mesh_pos()
        me = 2 * x + y
        sibling = (x, y, 1 - c)
        chips = _other_chips(x, y)

        def half(ref, chip, hc):
            hr = ref.shape[1] // 2
            return ref.at[chip, pl.ds(hc * hr, hr), :]

        first = []
        for k in range(n):
            first += [_remote(half(ins[k], me, c), half(outs[k], me, c), send_sems.at[6 * k + j], recv_sems.at[6 * k + j], (cx, cy, c))
                      for j, (cx, cy) in enumerate(chips)]
        first += [_remote(ins[n].at[me], outs[n].at[me], send_sems.at[6 * n + j], recv_sems.at[6 * n + j], (cx, cy, c))
                  for j, (cx, cy) in enumerate(chips)]
        for cp in first:
            cp.start()
        passed = []
        for k in range(n):
            for j, (cx, cy) in enumerate(chips):
                landed = half(outs[k], 2 * cx + cy, c)
                _remote(landed, landed, send_sems.at[6 * k + j], recv_sems.at[6 * k + j], (x, y, c)).wait_recv()
                passed.append(_remote(landed, landed, send_sems.at[6 * k + 3 + j], recv_sems.at[6 * k + 3 + j], sibling))
                passed[-1].start()
        for k in range(n):
            for j, (cx, cy) in enumerate(chips):
                theirs = half(outs[k], 2 * cx + cy, 1 - c)
                _remote(theirs, theirs, send_sems.at[6 * k + 3 + j], recv_sems.at[6 * k + 3 + j], (x, y, c)).wait_recv()
        for j, (cx, cy) in enumerate(chips):
            slot = outs[n].at[2 * cx + cy]
            _remote(slot, slot, send_sems.at[6 * n + j], recv_sems.at[6 * n + j], (x, y, c)).wait_recv()
        for cp in first + passed:
            cp.wait_send()

    arrays = [stacks[k] for k in names] + [wc_stack]
    outs = pl.pallas_call(
        body, name="allgather_weights",
        in_specs=[HBM] * (n + 1), out_specs=[HBM] * (n + 1), input_output_aliases={k: k for k in range(n + 1)},
        out_shape=[_sds(a.shape, a.dtype) for a in arrays],
        scratch_shapes=[pltpu.SemaphoreType.DMA((6 * n + 3,)), pltpu.SemaphoreType.DMA((6 * n + 3,))],
    )(*arrays)
    return dict(zip(names, outs[:n])), outs[n]


_KIND = {"w_in": "stack", "w_pa": "col", "w_pb": "col", "w_up": "col", "w_out": "row", "w_down": "row"}


def _half_view(ref, kind, h):
    if kind == "stack":
        k = ref.shape[1] // 2
        return ref.at[:, pl.ds(h * k, k), :]
    if kind == "col":
        k = ref.shape[0] // 2
        return ref.at[pl.ds(h * k, k), :]
    k = ref.shape[1] // 2
    return ref.at[:, pl.ds(h * k, k)]


def _shard_view(ref, kind, i):
    if kind == "stack":
        return ref.at[i]
    if kind == "col":
        k = ref.shape[1] // N_CHIPS
        return ref.at[:, pl.ds(i * k, k)]
    k = ref.shape[0] // N_CHIPS
    return ref.at[pl.ds(i * k, k), :]


def _region_view(ref, kind, h):
    if kind == "row":
        k = ref.shape[1] // 2
        return ref.at[:, pl.ds(h * k, k)]
    k = ref.shape[0] // 2
    return ref.at[pl.ds(h * k, k), :]


def _half_shape(shape, kind):
    if kind == "stack":
        return (shape[0], shape[1] // 2, shape[2])
    return (shape[0] // 2, shape[1]) if kind == "col" else (shape[0], shape[1] // 2)


def _part_shape(half_shape, kind):
    if kind == "stack":
        return tuple(half_shape[1:])
    k, w = half_shape
    return (k, w // N_CHIPS) if kind == "col" else (k // N_CHIPS, w)


_DATAFLOW = pltpu.SideEffectType.DATAFLOW_SIDE_EFFECTING
_TOKEN = (SUBLANES, LANES)


def _split_start(name, arrays, n_sems, issue, after=None):
    n = len(arrays)
    order = [] if after is None else [after]

    def body(*refs):
        base = n + len(order)
        issue(refs[:n], refs[base], refs[base + 1])
        refs[-1][...] = jnp.zeros(_TOKEN, F32)

    outs = pl.pallas_call(
        body, name=name,
        in_specs=[HBM] * n + [ANY] * len(order), out_specs=[SEM, SEM] + [HBM] * n + [pl.BlockSpec(memory_space=pltpu.VMEM)],
        out_shape=[pltpu.SemaphoreType.DMA((n_sems,)), pltpu.SemaphoreType.DMA((n_sems,))]
        + [pltpu.HBM(a.shape, a.dtype) for a in arrays] + [_sds(_TOKEN, F32)],
        input_output_aliases={k: 2 + k for k in range(n)},
        compiler_params=pltpu.CompilerParams(has_side_effects=_DATAFLOW),
    )(*[pltpu.with_memory_space_constraint(a, pltpu.HBM) for a in arrays], *order)
    return outs[0], outs[1], list(outs[2:2 + n]), outs[-1]


def _split_wait(name, started, waits, after):
    send_sems, recv_sems, arrays, _ = started
    n = len(arrays)

    def body(*refs):
        waits(refs[:n], refs[n], refs[n + 1])

    return pl.pallas_call(
        body, name=name,
        in_specs=[HBM] * n + [SEM, SEM, ANY], out_specs=[HBM] * n,
        out_shape=[pltpu.HBM(a.shape, a.dtype) for a in arrays],
        input_output_aliases={k: k for k in range(n)},
        compiler_params=pltpu.CompilerParams(has_side_effects=_DATAFLOW),
    )(*arrays, send_sems, recv_sems, after)


def _wait_both(src, dst, send_sem, recv_sem):
    x, y, c = _mesh_pos()
    cp = _remote(src, dst, send_sem, recv_sem, (x, y, c))
    cp.wait_send()
    cp.wait_recv()


def _pair_exchange_start(parts, tag, after):
    names = list(parts)
    n = len(names)
    lands = [lax.empty(_half_shape(parts[k].shape, _KIND[k]), parts[k].dtype) for k in names]

    def issue(refs, send_sems, recv_sems):
        x, y, c = _mesh_pos()
        for hc in range(2):
            @pl.when(c == hc)
            def _():
                for k in range(n):
                    _remote(_half_view(refs[k], _KIND[names[k]], 1 - hc), refs[n + k], send_sems.at[k], recv_sems.at[k],
                            (x, y, 1 - c)).start()

    return names, _split_start("grad_pair_exchange_start_" + tag, [parts[k] for k in names] + lands, n, issue, after)


def _pair_exchange_wait(pending, tag, after):
    names, started = pending
    n = len(names)

    def waits(refs, send_sems, recv_sems):
        for k in range(n):
            _wait_both(_half_view(refs[k], _KIND[names[k]], 0), refs[n + k], send_sems.at[k], recv_sems.at[k])

    outs = _split_wait("grad_pair_exchange_wait_" + tag, started, waits, after)
    return dict(zip(names, outs[:n])), dict(zip(names, outs[n:]))


def _half_blocks(shape, kind):
    if kind == "stack":
        _, k, w = shape
        tr = k // 2
        nb = 1
        return (N_CHIPS, nb), (1, tr, w), (lambda i, r, s: (i, r, 0)), (lambda i, r, s: (i, s[1] * nb + r, 0))
    k, w = shape
    if kind == "col":
        tr = 256
        nb = k // 2 // tr
        return (nb,), (tr, w), (lambda r, s: (r, 0)), (lambda r, s: (s[1] * nb + r, 0))
    tr = k // N_CHIPS
    return (N_CHIPS,), (tr, w // 2), (lambda r, s: (r, 0)), (lambda r, s: (r, s[1]))


def _pair_add(part, from_sibling, name, pos):
    kind = _KIND[name]
    grid, block, half_map, full_map = _half_blocks(part.shape, kind)

    def body(s_ref, p_ref, q_ref, o_ref):
        o_ref[...] = (p_ref[...].astype(F32) + q_ref[...].astype(F32)).astype(BF16)

    return pl.pallas_call(
        body, name="grad_pair_add_" + name,
        grid_spec=pltpu.PrefetchScalarGridSpec(
            num_scalar_prefetch=1, grid=grid,
            in_specs=[pl.BlockSpec(block, full_map), pl.BlockSpec(block, half_map)],
            out_specs=pl.BlockSpec(block, half_map)),
        out_shape=_sds(from_sibling.shape, BF16),
        compiler_params=_cp(("arbitrary",) * len(grid), 40),
    )(pos, *_hbm(part, from_sibling))


def _chip_exchange_start(sums, tag, after):
    names = list(sums)
    n = len(names)
    lands = [lax.empty((3,) + _part_shape(sums[k].shape, _KIND[k]), sums[k].dtype) for k in names]

    def issue(refs, send_sems, recv_sems):
        x, y, c = _mesh_pos()
        me = 2 * x + y
        for i in range(N_CHIPS):
            xi, yi = i // 2, i % 2
            j = jnp.where(xi != x, jnp.where(yi != y, 2, 0), 1)

            @pl.when(i != me)
            def _():
                for k in range(n):
                    _remote(_shard_view(refs[k], _KIND[names[k]], i), refs[n + k].at[j], send_sems.at[3 * k + j],
                            recv_sems.at[3 * k + j], (xi, yi, c)).start()

    return names, _split_start("grad_chip_exchange_start_" + tag, [sums[k] for k in names] + lands, 3 * n, issue, after)


def _chip_exchange_wait(pending, tag, after):
    names, started = pending
    n = len(names)

    def waits(refs, send_sems, recv_sems):
        for k in range(n):
            for j in range(3):
                _wait_both(_shard_view(refs[k], _KIND[names[k]], 0), refs[n + k].at[j], send_sems.at[3 * k + j], recv_sems.at[3 * k + j])

    return dict(zip(names, _split_wait("grad_chip_exchange_wait_" + tag, started, waits, after)[n:]))


def _allgather_start(stacks, after):
    names = list(stacks)

    def issue(refs, send_sems, recv_sems):
        x, y, c = _mesh_pos()
        me = 2 * x + y
        for k, st in enumerate(refs):
            hr = st.shape[1] // 2
            mine = st.at[me, pl.ds(c * hr, hr), :]
            for j, (cx, cy) in enumerate(_other_chips(x, y)):
                _remote(mine, mine, send_sems.at[3 * k + j], recv_sems.at[3 * k + j], (cx, cy, c)).start()

    return names, _split_start("allgather_start", [stacks[k] for k in names], 3 * len(names), issue, after)


def _allgather_wait(pending, after):
    names, started = pending

    def waits(refs, send_sems, recv_sems):
        for k, st in enumerate(refs):
            slot = st.at[0, pl.ds(0, st.shape[1] // 2), :]
            for j in range(3):
                _wait_both(slot, slot, send_sems.at[3 * k + j], recv_sems.at[3 * k + j])

    return dict(zip(names, _split_wait("allgather_wait", started, waits, after)))


def _allgather_forward(stacks):
    names = list(stacks)
    n = len(names)

    def body(*refs):
        ins, outs = refs[:n], refs[n:2 * n]
        send_sems, recv_sems = refs[2 * n:]
        x, y, c = _mesh_pos()
        copies = []
        for k in range(n):
            hr = ins[k].shape[1] // 2
            for j, (cx, cy) in enumerate(_other_chips(x, y)):
                chip = 2 * cx + cy
                copies.append(_remote(ins[k].at[chip, pl.ds(c * hr, hr), :], outs[k].at[chip, pl.ds(c * hr, hr), :],
                                      send_sems.at[3 * k + j], recv_sems.at[3 * k + j], (x, y, 1 - c)))
        for cp in copies:
            cp.start()
        for cp in copies:
            cp.wait()

    arrays = [stacks[k] for k in names]
    outs = pl.pallas_call(
        body, name="allgather_forward", in_specs=[HBM] * n, out_specs=[HBM] * n,
        input_output_aliases={k: k for k in range(n)},
        out_shape=[_sds(a.shape, a.dtype) for a in arrays],
        scratch_shapes=[pltpu.SemaphoreType.DMA((3 * n,)), pltpu.SemaphoreType.DMA((3 * n,))],
    )(*arrays)
    return dict(zip(names, outs))


def _owner_sum(part, from_sibling, from_chips, name, pos, shard_shape):
    kind = _KIND[name]
    _, pk, pw = from_chips.shape
    if kind == "row":
        tr, nb = pk, 1
        p_spec = pl.BlockSpec((tr, pw), lambda r, s: (s[0], s[1]))
        q_spec = pl.BlockSpec((tr, pw), lambda r, s: (s[0], 0))
        o_spec = pl.BlockSpec((tr, pw), lambda r, s: (0, s[1]))
    else:
        tr = 256
        nb = pk // tr
        if kind == "stack":
            p_spec = pl.BlockSpec((None, tr, pw), lambda r, s: (s[0], s[1] * nb + r, 0))
            q_spec = pl.BlockSpec((None, tr, pw), lambda r, s: (s[0], r, 0))
        else:
            p_spec = pl.BlockSpec((tr, pw), lambda r, s: (s[1] * nb + r, s[0]))
            q_spec = pl.BlockSpec((tr, pw), lambda r, s: (r, s[0]))
        o_spec = pl.BlockSpec((tr, pw), lambda r, s: (s[1] * nb + r, 0))

    def body(s_ref, p_ref, q_ref, r_ref, o_ref):
        acc = p_ref[...].astype(F32) + q_ref[...].astype(F32)
        for j in range(3):
            acc = acc + r_ref[j].astype(F32)
        o_ref[...] = acc

    return pl.pallas_call(
        body, name="grad_owner_sum_" + name,
        grid_spec=pltpu.PrefetchScalarGridSpec(
            num_scalar_prefetch=1, grid=(nb,),
            in_specs=[p_spec, q_spec, pl.BlockSpec((3, tr, pw), lambda r, s: (0, r, 0))],
            out_specs=o_spec),
        out_shape=_sds(shard_shape, F32),
        compiler_params=_cp(("arbitrary",), 32),
    )(pos, *_hbm(part, from_sibling, from_chips))


def _pair_share_start(shards, tag, after):
    names = list(shards)

    def issue(refs, send_sems, recv_sems):
        x, y, c = _mesh_pos()
        for hc in range(2):
            @pl.when(c == hc)
            def _():
                for k, g in enumerate(refs):
                    mine = _region_view(g, _KIND[names[k]], hc)
                    _remote(mine, mine, send_sems.at[k], recv_sems.at[k], (x, y, 1 - c)).start()

    return names, _split_start("grad_pair_share_start_" + tag, [shards[k] for k in names], len(names), issue, after)


def _pair_share_wait(pending, tag, after):
    names, started = pending

    def waits(refs, send_sems, recv_sems):
        for k, g in enumerate(refs):
            region = _region_view(g, _KIND[names[k]], 0)
            _wait_both(region, region, send_sems.at[k], recv_sems.at[k])

    return dict(zip(names, _split_wait("grad_pair_share_wait_" + tag, started, waits, after)))


def _small_exchange_start(slots, after):
    def issue(refs, send_sems, recv_sems):
        x, y, c = _mesh_pos()
        mine = refs[0].at[4 * x + 2 * y + c]
        k = 0
        for px in range(2):
            for py in range(2):
                for pc in range(2):
                    if px + py + pc:
                        peer = (1 - x if px else x, 1 - y if py else y, 1 - c if pc else c)
                        _remote(mine, mine, send_sems.at[k], recv_sems.at[k], peer).start()
                        k += 1

    return _split_start("small_exchange_start", [slots], N_DEV - 1, issue, after)


def _small_exchange_wait(started, after):
    def waits(refs, send_sems, recv_sems):
        slot = refs[0].at[0]
        for k in range(N_DEV - 1):
            _wait_both(slot, slot, send_sems.at[k], recv_sems.at[k])

    return _split_wait("small_exchange_wait", started, waits, after)[0]


def _adam_math(w, g, m, v):
    m = ADAM_B1 * m + (1.0 - ADAM_B1) * g
    v = ADAM_B2 * v + (1.0 - ADAM_B2) * (g * g)
    m_hat = m / (1.0 - ADAM_B1 ** ADAM_STEP)
    v_hat = v / (1.0 - ADAM_B2 ** ADAM_STEP)
    delta = -ADAM_LR * (m_hat / (jnp.sqrt(v_hat) + ADAM_EPS) + ADAM_WD * w)
    return delta, m, v


def _adamw(w, g, m, v, name):
    rows, cols = w.shape
    fits = [t for t in range(SUBLANES, rows, SUBLANES) if rows % t == 0 and t * cols * 4 <= (3 << 19)]
    tr = max(fits) if fits else rows

    def body(w_ref, g_ref, m_ref, v_ref, d_ref, nm_ref, nv_ref):
        d, nm, nv = _adam_math(w_ref[...], g_ref[...], m_ref[...], v_ref[...])
        d_ref[...] = d
        nm_ref[...] = nm
        nv_ref[...] = nv

    spec = pl.BlockSpec((tr, cols), lambda i: (i, 0))
    return pl.pallas_call(
        body, name=name, grid=(rows // tr,), in_specs=[spec] * 4, out_specs=[spec] * 3,
        out_shape=[_sds(w.shape, F32)] * 3, compiler_params=_cp(("arbitrary",)),
    )(*_hbm(w, g, m, v))


def _small_sum_adamw(gathered, w, m, v):
    def body(a_ref, w_ref, m_ref, v_ref, g_ref, d_ref, nm_ref, nv_ref):
        g = a_ref[0]
        for k in range(1, N_DEV):
            g = g + a_ref[k]
        g_ref[...] = g
        d, nm, nv = _adam_math(w_ref[...], g, m_ref[...], v_ref[...])
        d_ref[...] = d
        nm_ref[...] = nm
        nv_ref[...] = nv

    return pl.pallas_call(
        body, name="small_sum_adamw", out_shape=[_sds(w.shape, F32)] * 4,
    )(gathered, w, m, v)


_NAMES = ("g_mix", "w_in", "g_sgu", "w_s", "b_s", "sinks", "rel_bias", "w_pa", "w_pb", "w_out",
          "g_ffn", "w_up", "w_conv", "b_conv", "w_down", "g_final")

def kernel(x, g_mix, w_in, g_sgu, w_s, b_s, sinks, rel_bias, w_pa, w_pb, w_out, g_ffn, w_up, w_conv, b_conv, w_down, g_final, loss_target, m_g_mix, m_w_in, m_g_sgu, m_w_s, m_b_s, m_sinks, m_rel_bias, m_w_pa, m_w_pb, m_w_out, m_g_ffn, m_w_up, m_w_conv, m_b_conv, m_w_down, m_g_final, v_g_mix, v_w_in, v_g_sgu, v_w_s, v_b_s, v_sinks, v_rel_bias, v_w_pa, v_w_pb, v_w_out, v_g_ffn, v_w_up, v_w_conv, v_b_conv, v_w_down, v_g_final):
    w = dict(g_mix=g_mix, w_in=w_in, g_sgu=g_sgu, w_s=w_s, b_s=b_s, sinks=sinks, rel_bias=rel_bias, w_pa=w_pa, w_pb=w_pb,
             w_out=w_out, g_ffn=g_ffn, w_up=w_up, w_conv=w_conv, b_conv=b_conv, w_down=w_down, g_final=g_final)
    m = dict(g_mix=m_g_mix, w_in=m_w_in, g_sgu=m_g_sgu, w_s=m_w_s, b_s=m_b_s, sinks=m_sinks, rel_bias=m_rel_bias, w_pa=m_w_pa,
             w_pb=m_w_pb, w_out=m_w_out, g_ffn=m_g_ffn, w_up=m_w_up, w_conv=m_w_conv, b_conv=m_b_conv, w_down=m_w_down,
             g_final=m_g_final)
    v = dict(g_mix=v_g_mix, w_in=v_w_in, g_sgu=v_g_sgu, w_s=v_w_s, b_s=v_b_s, sinks=v_sinks, rel_bias=v_rel_bias, w_pa=v_w_pa,
             w_pb=v_w_pb, w_out=v_w_out, g_ffn=v_g_ffn, w_up=v_w_up, w_conv=v_w_conv, b_conv=v_b_conv, w_down=v_w_down,
             g_final=v_g_final)
    xi, yi, ci = _mesh_pos()
    me = 2 * xi + yi

    shard = {n: w[n][0] for n in _BIG}
    shard_shapes = {n: shard[n].shape for n in _BIG}
    wc_shard = w["w_conv"][0]
    wc_pad = jnp.pad(wc_shard, ((0, 5), (0, 0)))
    own = {n: _own_slot(shard[n].astype(BF16), N_CHIPS, me) for n in _BIG}
    stacks, wc_all = _allgather_weights({"w_in": own["w_in"]}, _own_slot(wc_pad, N_CHIPS, me))
    late_gather = _allgather_start({n: own[n] for n in _BIG[1:]}, stacks["w_in"])
    w_conv_full = jnp.concatenate([wc_all[i, :3] for i in range(N_CHIPS)], axis=1)
    w_in_full = _unstack_w_in(stacks["w_in"])
    pos = jnp.stack([me, ci])

    def late_weights(done):
        st = _allgather_forward(_allgather_wait(late_gather, done))
        return st["w_pa"], st["w_pb"], st["w_out"].reshape(D_MODEL, D_MODEL), st["w_up"], st["w_down"].reshape(D_FF, D_MODEL)

    groups = {}

    def stage1(group, parts):
        groups[group] = dict(parts=parts, pair=_pair_exchange_start(parts, group, None))
        return groups[group]["pair"][1][-1]

    def stage2(group, after, order_after):
        g = groups[group]
        g["parts"], g["sib"] = _pair_exchange_wait(g["pair"], group, after)
        g["chip"] = _chip_exchange_start({n: _pair_add(g["parts"][n], g["sib"][n], n, pos) for n in g["parts"]}, group, order_after)
        return g["chip"][1][-1]

    def stage3(group, after, order_after):
        g = groups[group]
        got = _chip_exchange_wait(g["chip"], group, after)
        g["share"] = _pair_share_start(
            {n: _owner_sum(g["parts"][n], g["sib"][n], got[n], n, pos, shard_shapes[n]) for n in g["parts"]}, group, order_after)
        return g["share"][1][-1]

    grads, deltas, new_m, new_v = {}, {}, {}, {}

    def stage4(group, after):
        g_shard = _pair_share_wait(groups[group]["share"], group, after)
        last = None
        for n in g_shard:
            g = _tie(g_shard[n], last)
            if n == "w_in":
                gt = g.T
                d, nm, nv = _adamw(shard[n].T, gt, m[n][0].T, v[n][0].T, "adamw_" + n)
                grads[n], deltas[n], new_m[n], new_v[n] = gt.T[None], d.T[None], nm.T[None], nv.T[None]
            else:
                d, nm, nv = _adamw(shard[n], g, m[n][0], v[n][0], "adamw_" + n)
                grads[n], deltas[n], new_m[n], new_v[n] = g[None], d[None], nm[None], nv[None]
            last = nv
        return last

    def on_grads(group, parts):
        token = stage1(group, parts)
        some = next(iter(parts.values()))
        if group == "proj":
            token = stage2("ffn", some, token)
        if group == "in":
            token = stage2("proj", some, token)
            token = stage3("ffn", some, token)
            token = stage2("in", token, token)
        return token

    loss, grad_x, small, big = _local_step(
        x, loss_target, w["g_mix"], w["g_sgu"], w["w_s"][0], w["b_s"][0], w["sinks"], w["rel_bias"], w["g_ffn"],
        w["b_conv"], w["g_final"], w_in_full, w_conv_full, late_weights, on_grads, late_gather[1][-1])

    small["loss"] = loss
    small_gather = _small_exchange_start(_own_slot(_pack_small(small), N_DEV, 2 * me + ci), grad_x)
    token = stage3("proj", grad_x, small_gather[-1])
    done = stage4("ffn", token)
    done = stage4("proj", done)
    token = stage3("in", done, None)
    sw = {n: (jnp.zeros((1, 1), F32) if n in ("loss", "w_conv") else w[n]) for n, _ in _SMALL}
    sm = {n: (jnp.zeros((1, 1), F32) if n in ("loss", "w_conv") else m[n]) for n, _ in _SMALL}
    sv = {n: (jnp.zeros((1, 1), F32) if n in ("loss", "w_conv") else v[n]) for n, _ in _SMALL}
    for d in (sw, sm, sv):
        d["w_conv"] = jnp.zeros((3, 2 * D_FF), F32)
    all_small = _small_exchange_wait(small_gather, token)
    s_g, s_d, s_m, s_v = [_unpack_small(a) for a in _small_sum_adamw(all_small, _pack_small(sw), _pack_small(sm), _pack_small(sv))]
    stage4("in", all_small)
    wcols = wc_shard.shape[1]
    g_wc = lax.dynamic_slice(s_g["w_conv"], (0, me * wcols), (3, wcols))
    d, nm, nv = _adamw(wc_shard, g_wc, m["w_conv"][0], v["w_conv"][0], "adamw_w_conv")
    grads["w_conv"], deltas["w_conv"], new_m["w_conv"], new_v["w_conv"] = g_wc[None], d[None], nm[None], nv[None]
    for n, _ in _SMALL:
        if n in ("loss", "w_conv"):
            continue
        shp = w[n].shape
        grads[n], deltas[n], new_m[n], new_v[n] = (s_g[n].reshape(shp), s_d[n].reshape(shp), s_m[n].reshape(shp),
                                                    s_v[n].reshape(shp))

    return (s_g["loss"].reshape(()), grad_x, *[grads[n] for n in _NAMES], *[deltas[n] for n in _NAMES],
            *[new_m[n] for n in _NAMES], *[new_v[n] for n in _NAMES])
```

```python
import functools

import numpy as np
import jax
import jax.numpy as jnp
from jax import lax
from jax.experimental import pallas as pl
from jax.experimental.pallas import tpu as pltpu

F32 = jnp.float32
BF16 = jnp.bfloat16

D_MODEL = 1024
CHUNK = 128
A_GROUPS = 4
A_WIDTH = 512
N_HEADS = 8
HEAD_DIM = 64
Q_DIM = 512
KV_DIM = 128
N_BUCKETS = 32
MAX_DISTANCE = 128
D_FF = 2816
EPS = 1e-6
NEG_INF = -1e30
G_DIM = 2 * D_MODEL
A_DIM = 2 * A_WIDTH
B_DIM = Q_DIM + 2 * KV_DIM
LANES = 128
SUBLANES = 8
ROW_TILE = 512
WIDE_ROW_TILE = 256
COL_CHUNK = 512
GRAD_ROW_TILE = 512
BF16_ROWS = 16
N_CHIPS = 4
N_DEV = 8

ADAM_LR = 0.001
ADAM_B1 = 0.9
ADAM_B2 = 0.999
ADAM_EPS = 1e-08
ADAM_WD = 0.01
ADAM_STEP = 10

MESH = pl.DeviceIdType.MESH
_GELU_C = 0.7978845608028654
_GELU_A = 0.044715


def _cp(sem=None, vmem_mb=None):
    kw = {}
    if sem is not None:
        kw["dimension_semantics"] = sem
    if vmem_mb is not None:
        kw["vmem_limit_bytes"] = vmem_mb << 20
    return pltpu.CompilerParams(**kw)


def _dot(a, b):
    return jnp.dot(a, b, preferred_element_type=F32)


def _dot_nt(a, b):
    return lax.dot_general(a, b, (((1,), (1,)), ((), ())), preferred_element_type=F32)


def _dot_tn(a, b):
    return lax.dot_general(a, b, (((0,), (0,)), ((), ())), preferred_element_type=F32)


def _rms_r(x):
    return lax.rsqrt(jnp.mean(x * x, axis=-1, keepdims=True) + EPS)


def _rms_bwd(dh, n, r, g):
    dn = dh * g
    return r * (dn - n * jnp.mean(dn * n, axis=-1, keepdims=True))


def _gelu(x):
    t = jnp.tanh(_GELU_C * (x + _GELU_A * (x * x * x)))
    return 0.5 * x * (1.0 + t), t


def _gelu_grad(x, t):
    return 0.5 * (1.0 + t) + 0.5 * x * (1.0 - t * t) * (_GELU_C * (1.0 + 3.0 * _GELU_A * x * x))


def _sigmoid(x):
    return 1.0 / (1.0 + jnp.exp(-x))


def _tie(x, dep):
    return x if dep is None else lax.optimization_barrier((x, dep))[0]


def _row(tm, w):
    return pl.BlockSpec((tm, w), lambda i: (i, 0))


def _full(shape):
    nd = len(shape)
    return pl.BlockSpec(tuple(shape), lambda *_: (0,) * nd)


def _resident(shape):
    nd = len(shape)
    return pl.BlockSpec(tuple(shape), lambda *_: (0,) * nd, pipeline_mode=pl.Buffered(1))


def _sds(shape, dtype):
    return jax.ShapeDtypeStruct(tuple(shape), dtype)


def _hbm(*arrays):
    return [pltpu.with_memory_space_constraint(a, pltpu.HBM) for a in arrays]


HBM = pl.BlockSpec(memory_space=pltpu.HBM)
ANY = pl.BlockSpec(memory_space=pl.ANY)
SEM = pl.BlockSpec(memory_space=pltpu.SEMAPHORE)


def _band_buckets():
    i = np.arange(CHUNK)[:, None]
    j = np.arange(2 * CHUNK)[None, :]
    dist = i + CHUNK - j
    valid = (dist >= 0) & (dist < CHUNK)
    d = np.clip(dist, 0, None)
    max_exact = N_BUCKETS // 2
    large = max_exact + (np.log(np.maximum(d, 1) / max_exact) / np.log(MAX_DISTANCE / max_exact)
                         * (N_BUCKETS - max_exact)).astype(np.int32)
    large = np.minimum(large, N_BUCKETS - 1)
    buckets = np.where(d < max_exact, d, large).astype(np.int32)
    return np.where(valid, buckets, -1).astype(np.int32)


_A_COLS = slice(0, A_DIM)
_B_COLS = slice(A_DIM, A_DIM + B_DIM)
_G_COLS = slice(A_DIM + B_DIM, A_DIM + B_DIM + G_DIM)


def _inproj(x2, g_mix, w_in, tm, after=None):
    T = x2.shape[0]
    order = [] if after is None else [after]

    def body(*refs):
        x_ref, g_ref, w_ref = refs[:3]
        pg_ref, pa_ref, pb_ref, h_ref = refs[3 + len(order):]
        x = x_ref[...]
        h = (x * _rms_r(x) * g_ref[...]).astype(BF16)
        h_ref[...] = h
        pg_ref[...] = _dot(h, w_ref[:, _G_COLS]).astype(BF16)
        pa_ref[...] = _dot(h, w_ref[:, _A_COLS]).astype(BF16)
        pb_ref[...] = _dot(h, w_ref[:, _B_COLS]).astype(BF16)

    return pl.pallas_call(
        body, name="inproj", grid=(T // tm,),
        in_specs=[_row(tm, D_MODEL), _full(g_mix.shape), _resident(w_in.shape)] + [ANY] * len(order),
        out_specs=[_row(tm, G_DIM), _row(tm, A_DIM), _row(tm, B_DIM), _row(tm, D_MODEL)],
        out_shape=[_sds((T, G_DIM), BF16), _sds((T, A_DIM), BF16), _sds((T, B_DIM), BF16), _sds((T, D_MODEL), BF16)],
        compiler_params=_cp(("arbitrary",), 48),
    )(*_hbm(x2, g_mix, w_in), *order)


def _sgu_parts(p, g):
    pu = p[:, :A_WIDTH]
    pv = p[:, A_WIDTH:]
    u, tu = _gelu(pu)
    vv, tv = _gelu(pv)
    rv = _rms_r(vv)
    vn = (vv * rv * g).astype(BF16)
    return pu, pv, u, tu, vv, tv, rv, vn


def _tril():
    r = lax.broadcasted_iota(jnp.int32, (CHUNK, CHUNK), 0)
    c = lax.broadcasted_iota(jnp.int32, (CHUNK, CHUNK), 1)
    return r >= c


def _sgu_fwd(proj_a, g_sgu, w_s, b_st, tm):
    T = proj_a.shape[0]

    def body(p_ref, g_ref, ws_ref, bs_ref, y_ref):
        tril = _tril()
        _, _, u, _, _, _, _, vn = _sgu_parts(p_ref[...].astype(F32), g_ref[...])
        for gi in range(A_GROUPS):
            wm = jnp.where(tril, ws_ref[gi], 0.0).astype(BF16)
            bcol = bs_ref[:, gi:gi + 1]
            cs = slice(gi * CHUNK, (gi + 1) * CHUNK)
            for c in range(tm // CHUNK):
                rs = slice(c * CHUNK, (c + 1) * CHUNK)
                s = _dot(wm, vn[rs, cs]) + bcol
                y_ref[rs, cs] = (u[rs, cs] * s).astype(BF16)

    return pl.pallas_call(
        body, name="sgu_fwd", grid=(T // tm,),
        in_specs=[_row(tm, A_DIM), _full(g_sgu.shape), _full(w_s.shape), _full(b_st.shape)],
        out_specs=_row(tm, A_WIDTH), out_shape=_sds((T, A_WIDTH), BF16),
        compiler_params=_cp(("arbitrary",)),
    )(*_hbm(proj_a, g_sgu, w_s, b_st))


HEAD_ROWS = N_HEADS * CHUNK


def _head_rows(h):
    return slice(h * CHUNK, (h + 1) * CHUNK)


def _attn_setup(bias_scr, sink_scr, kvar_scr, qkv_ref, bk_ref, rel_ref, sink_ref):
    bk = bk_ref[...]
    for h in range(N_HEADS):
        acc = jnp.full((CHUNK, 2 * CHUNK), NEG_INF, F32)
        for b in range(N_BUCKETS):
            acc = jnp.where(bk == b, rel_ref[b, h], acc)
        bias_scr[_head_rows(h), :] = acc
        sink_scr[_head_rows(h), :] = jnp.full((CHUNK, LANES), sink_ref[0, h], F32)
    seq = qkv_ref.shape[0]
    rows_per = 2 * CHUNK
    for is_v in range(2):
        c0 = Q_DIM + is_v * KV_DIM
        for r in range(seq // rows_per):
            rs = slice(r * rows_per, (r + 1) * rows_per)
            a = qkv_ref[rs, c0:c0 + KV_DIM].astype(F32)
            lane = lax.broadcasted_iota(jnp.int32, a.shape, 1)
            lo = jnp.where(lane < HEAD_DIM, a, 0.0)
            hi = jnp.where(lane >= HEAD_DIM, a, 0.0)
            kvar_scr[4 * is_v + 0, rs, :] = lo.astype(BF16)
            kvar_scr[4 * is_v + 1, rs, :] = pltpu.roll(lo, HEAD_DIM, 1).astype(BF16)
            kvar_scr[4 * is_v + 2, rs, :] = pltpu.roll(hi, HEAD_DIM, 1).astype(BF16)
            kvar_scr[4 * is_v + 3, rs, :] = hi.astype(BF16)


def _rowsum(a, ones):
    hi = a.astype(BF16)
    lo = (a - hi.astype(F32)).astype(BF16)
    return _dot(hi, ones) + _dot(lo, ones)


def _both(a):
    return jnp.concatenate([a, a], axis=1)


def _attn_probs(qkv_ref, r0, n, kv, bias_scr, sink_scr, ones):
    s = jnp.concatenate([_dot_nt(qkv_ref[pl.ds(r0, CHUNK), (h // 2) * LANES:(h // 2 + 1) * LANES], kv[h // 4][h % 2])
                         for h in range(N_HEADS)], axis=0)
    s = s * (HEAD_DIM ** -0.5) + bias_scr[...]
    col = lax.broadcasted_iota(jnp.int32, s.shape, 1)
    s = jnp.where((col < CHUNK) & (n == 0), NEG_INF, s)
    sink = sink_scr[...]
    m = jnp.maximum(jnp.max(s, axis=-1, keepdims=True), sink)
    p = jnp.exp(s - _both(m))
    es = jnp.exp(sink - m)
    inv = 1.0 / (_rowsum(p, ones) + es)
    return p * _both(inv), es * inv


def _attn_block_inputs(kvar_scr, n):
    r0 = pl.multiple_of(n * CHUNK, CHUNK)
    rp = pl.multiple_of(jnp.maximum(n - 1, 0) * CHUNK, CHUNK)

    def both(idx):
        return jnp.concatenate([kvar_scr[idx, pl.ds(rp, CHUNK), :], kvar_scr[idx, pl.ds(r0, CHUNK), :]], axis=0)

    kv = ((both(0), both(1)), (both(2), both(3)))
    vv = ((both(4), both(5)), (both(6), both(7)))
    return r0, kv, vv


def _attn_fwd(proj_b, sinks, rel_bias, n_seq, seq):
    nb = seq // CHUNK
    bk = jnp.asarray(_band_buckets())

    def body(qkv_ref, bk_ref, rel_ref, sink_ref, o_ref, bias_scr, sink_scr, kvar_scr):
        _attn_setup(bias_scr, sink_scr, kvar_scr, qkv_ref, bk_ref, rel_ref, sink_ref)
        ones = jnp.ones((2 * CHUNK, LANES), BF16)

        def blk(n, carry):
            r0, kv, vv = _attn_block_inputs(kvar_scr, n)
            prob, _ = _attn_probs(qkv_ref, r0, n, kv, bias_scr, sink_scr, ones)
            pb = prob.astype(BF16)
            for pr in range(N_HEADS // 2):
                acc = _dot(pb[_head_rows(2 * pr)], vv[pr // 2][0]) + _dot(pb[_head_rows(2 * pr + 1)], vv[pr // 2][1])
                o_ref[pl.ds(r0, CHUNK), pr * LANES:(pr + 1) * LANES] = acc.astype(BF16)
            return carry

        lax.fori_loop(0, nb, blk, 0)

    smem = pl.BlockSpec(memory_space=pltpu.SMEM)
    return pl.pallas_call(
        body, name="attn_fwd", grid=(n_seq,),
        in_specs=[_row(seq, B_DIM), _full(bk.shape), smem, smem],
        out_specs=_row(seq, Q_DIM), out_shape=_sds((n_seq * seq, Q_DIM), BF16),
        scratch_shapes=[pltpu.VMEM((HEAD_ROWS, 2 * CHUNK), F32), pltpu.VMEM((HEAD_ROWS, LANES), F32),
                        pltpu.VMEM((8, seq, KV_DIM), BF16)],
        compiler_params=_cp(("arbitrary",), 40),
    )(*_hbm(proj_b, bk), rel_bias, sinks)


def _dot_stacked(a, w_ref):
    return jnp.concatenate([_dot(a, w_ref[i]) for i in range(N_CHIPS)], axis=1)


def _dot_nt_stacked(a, w_ref):
    w = w_ref.shape[2]
    acc = _dot_nt(a[:, :w], w_ref[0])
    for i in range(1, N_CHIPS):
        acc = acc + _dot_nt(a[:, i * w:(i + 1) * w], w_ref[i])
    return acc


def _merge_fwd(x2, y_a, y_b, proj_g, w_pa, w_pb, w_out, tm):
    T = x2.shape[0]

    def body(x_ref, ya_ref, yb_ref, g_ref, wpa_ref, wpb_ref, wo_ref, x1_ref, mg_ref):
        g = g_ref[...].astype(F32)
        pa = _dot_stacked(ya_ref[...], wpa_ref)
        pb = _dot_stacked(yb_ref[...], wpb_ref)
        merged = (_sigmoid(g[:, :D_MODEL]) * pa + _sigmoid(g[:, D_MODEL:]) * pb).astype(BF16)
        mg_ref[...] = merged
        x1_ref[...] = x_ref[...] + _dot(merged, wo_ref[...])

    return pl.pallas_call(
        body, name="merge_fwd", grid=(T // tm,),
        in_specs=[_row(tm, D_MODEL), _row(tm, A_WIDTH), _row(tm, Q_DIM), _row(tm, G_DIM),
                  _resident(w_pa.shape), _resident(w_pb.shape), _resident(w_out.shape)],
        out_specs=[_row(tm, D_MODEL), _row(tm, D_MODEL)],
        out_shape=[_sds((T, D_MODEL), F32), _sds((T, D_MODEL), BF16)],
        compiler_params=_cp(("arbitrary",), 40),
    )(*_hbm(x2, y_a, y_b, proj_g, w_pa, w_pb, w_out))


def _upproj(x1, g_ffn, w_up, w_conv, b_conv, tm, seq):
    T = x1.shape[0]
    cw = w_up.shape[2]
    tiles_per_seq = seq // tm

    def body(x_ref, g_ref, w_ref, wc_ref, bc_ref, u_ref, h_ref, gate_ref, val_ref, tail_scr):
        at_start = (pl.program_id(0) % tiles_per_seq) == 0
        x = x_ref[...]
        h = (x * _rms_r(x) * g_ref[...]).astype(BF16)
        h_ref[...] = h
        for i in range(N_CHIPS):
            cs = slice(i * cw, (i + 1) * cw)
            u = _dot(h, w_ref[i])
            u_ref[:, cs] = u.astype(BF16)
            hl = jnp.where(at_start, 0.0, tail_scr[SUBLANES - 2:SUBLANES, cs])
            tail_scr[:, cs] = u[tm - SUBLANES:]
            up = _conv_out((u, _shift_down(u, hl, 1), _shift_down(u, hl, 2)), wc_ref[:, cs], bc_ref[:, cs])
            out_ref = gate_ref if i < N_CHIPS // 2 else val_ref
            out_ref[:, (i % 2) * cw:(i % 2 + 1) * cw] = up.astype(BF16)

    return pl.pallas_call(
        body, name="upproj", grid=(T // tm,),
        in_specs=[_row(tm, D_MODEL), _full(g_ffn.shape), _resident(w_up.shape), _full(w_conv.shape), _full(b_conv.shape)],
        out_specs=[_row(tm, 2 * D_FF), _row(tm, D_MODEL), _row(tm, D_FF), _row(tm, D_FF)],
        out_shape=[_sds((T, 2 * D_FF), BF16), _sds((T, D_MODEL), BF16), _sds((T, D_FF), BF16), _sds((T, D_FF), BF16)],
        scratch_shapes=[pltpu.VMEM((SUBLANES, 2 * D_FF), F32)],
        compiler_params=_cp(("arbitrary",), 56),
    )(*_hbm(x1, g_ffn, w_up, w_conv, b_conv))


def _shift_down(u, halo, k):
    rolled = pltpu.roll(u, k, 0)
    head = rolled[:SUBLANES]
    row = lax.broadcasted_iota(jnp.int32, head.shape, 0)
    if k == 1:
        head = jnp.where(row == 0, halo[1:2], head)
    else:
        head = jnp.where(row == 0, halo[0:1], jnp.where(row == 1, halo[1:2], head))
    return jnp.concatenate([head, rolled[SUBLANES:]], axis=0)


def _shift_up(d, halo, k):
    tm = d.shape[0]
    rolled = pltpu.roll(d, tm - k, 0)
    tail = rolled[tm - SUBLANES:]
    row = lax.broadcasted_iota(jnp.int32, tail.shape, 0)
    if k == 1:
        tail = jnp.where(row == SUBLANES - 1, halo[0:1], tail)
    else:
        tail = jnp.where(row == SUBLANES - 2, halo[0:1], jnp.where(row == SUBLANES - 1, halo[1:2], tail))
    return jnp.concatenate([rolled[:tm - SUBLANES], tail], axis=0)


def _conv_out(taps, wc, bc):
    u, u1, u2 = taps
    return wc[0:1] * u2 + wc[1:2] * u1 + wc[2:3] * u + bc


def _ffn_down_loss(gate, val, x1, target, w_down, g_final, tm):
    T = x1.shape[0]
    half = D_FF // 2

    def body(gt_ref, vl_ref, x1_ref, t_ref, wd_ref, g_ref, dx2_ref, loss_ref, gg_ref):
        i = pl.program_id(0)
        acc = jnp.zeros((tm, D_MODEL), F32)
        for j in range(2):
            gc = slice(j * half, (j + 1) * half)
            gate = gt_ref[:, gc].astype(F32)
            act = (gate * _sigmoid(gate) * vl_ref[:, gc].astype(F32)).astype(BF16)
            acc = acc + _dot(act, wd_ref[gc, :])
        x2 = x1_ref[...] + acc
        r = _rms_r(x2)
        n = x2 * r
        g = g_ref[...]
        diff = n * g - t_ref[...]
        dy = diff * (1.0 / D_MODEL)
        dx2_ref[...] = _rms_bwd(dy, n, r, g)

        @pl.when(i == 0)
        def _():
            loss_ref[...] = jnp.zeros_like(loss_ref)
            gg_ref[...] = jnp.zeros_like(gg_ref)

        loss_ref[...] += 0.5 * jnp.sum(jnp.mean(diff * diff, axis=-1, keepdims=True), axis=0, keepdims=True)
        gg_ref[...] += jnp.sum(dy * n, axis=0, keepdims=True)

    return pl.pallas_call(
        body, name="ffn_down_loss", grid=(T // tm,),
        in_specs=[_row(tm, D_FF), _row(tm, D_FF), _row(tm, D_MODEL), _row(tm, D_MODEL),
                  _resident(w_down.shape), _full(g_final.shape)],
        out_specs=[_row(tm, D_MODEL), _full((1, 1)), _full((1, D_MODEL))],
        out_shape=[_sds((T, D_MODEL), F32), _sds((1, 1), F32), _sds((1, D_MODEL), F32)],
        compiler_params=_cp(("arbitrary",), 48),
    )(*_hbm(gate, val, x1, target, w_down, g_final))


def _ffn_bwd_act(gate, val, dx2, w_down, tm):
    T = dx2.shape[0]
    half = D_FF // 2
    nt = T // tm

    def body(g_ref, v_ref, dx_ref, wd_ref, dg_ref, dv_ref, gwd_out, gbg_ref, gbv_ref, gwd_ref):
        i = pl.program_id(1)

        @pl.when(i == 0)
        def _():
            for r in (gwd_ref, gbg_ref, gbv_ref):
                r[...] = jnp.zeros_like(r)

        dx = dx_ref[...].astype(BF16)
        for c0 in range(0, half, COL_CHUNK):
            cs = slice(c0, min(c0 + COL_CHUNK, half))
            gate = g_ref[:, cs].astype(F32)
            val = v_ref[:, cs].astype(F32)
            sg = _sigmoid(gate)
            silu = gate * sg
            d_act = _dot_nt(dx, wd_ref[cs, :])
            d_val = d_act * silu
            d_gate = d_act * val * (sg * (1.0 + gate * (1.0 - sg)))
            dg_ref[:, cs] = d_gate.astype(BF16)
            dv_ref[:, cs] = d_val.astype(BF16)
            gwd_ref[cs, :] += _dot_tn((silu * val).astype(BF16), dx)
            gbg_ref[:, cs] += jnp.sum(d_gate, axis=0, keepdims=True)
            gbv_ref[:, cs] += jnp.sum(d_val, axis=0, keepdims=True)

        @pl.when(i == nt - 1)
        def _():
            gwd_out[...] = gwd_ref[...].astype(BF16)

    tile = pl.BlockSpec((tm, half), lambda j, i: (i, j))
    vec = pl.BlockSpec((1, half), lambda j, i: (0, j))
    wrows = pl.BlockSpec((half, D_MODEL), lambda j, i: (j, 0))
    return pl.pallas_call(
        body, name="ffn_bwd_act", grid=(2, nt),
        in_specs=[tile, tile, pl.BlockSpec((tm, D_MODEL), lambda j, i: (i, 0)), wrows],
        out_specs=[tile, tile, wrows, vec, vec],
        out_shape=[_sds((T, D_FF), BF16), _sds((T, D_FF), BF16), _sds((D_FF, D_MODEL), BF16),
                   _sds((1, D_FF), F32), _sds((1, D_FF), F32)],
        scratch_shapes=[pltpu.VMEM((half, D_MODEL), F32)],
        compiler_params=_cp(("arbitrary", "arbitrary"), 56),
    )(*_hbm(gate, val, dx2, w_down))


def _ffn_bwd_up(d_gate, d_val, upre, dx2, x1, g_ffn, w_conv, w_up, tm, seq):
    T = dx2.shape[0]
    tiles_per_seq = seq // tm
    k16 = tm // BF16_ROWS
    n16 = T // BF16_ROWS
    cw = D_FF // 2

    def body(dg_ref, dv_ref, hg_ref, hv_ref, u_ref, dx2_ref, x1_ref, g_ref, wc_ref, wu_ref, du_ref, dx1_ref, gg_ref, gwc_ref):
        i = pl.program_id(0)
        at_end = (i % tiles_per_seq) == tiles_per_seq - 1

        @pl.when(i == 0)
        def _():
            gg_ref[...] = jnp.zeros_like(gg_ref)
            gwc_ref[...] = jnp.zeros_like(gwc_ref)

        dh = jnp.zeros((tm, D_MODEL), F32)
        for j in range(4):
            src, hsrc = (dg_ref, hg_ref) if j < 2 else (dv_ref, hv_ref)
            ls = slice((j % 2) * cw, (j % 2 + 1) * cw)
            cs = slice(j * cw, (j + 1) * cw)
            d = src[:, ls].astype(F32)
            hl = hsrc[:, ls].astype(F32)[0:2]
            hl = jnp.where(at_end, 0.0, hl)
            wc = wc_ref[:, cs]
            d1 = _shift_up(d, hl, 1)
            d2 = _shift_up(d, hl, 2)
            du = (wc[2:3] * d + wc[1:2] * d1 + wc[0:1] * d2).astype(BF16)
            du_ref[:, cs] = du
            dh = dh + _dot_nt(du, wu_ref[j])
            u = u_ref[:, cs].astype(F32)
            gwc_ref[0:1, cs] += jnp.sum(d2 * u, axis=0, keepdims=True)
            gwc_ref[1:2, cs] += jnp.sum(d1 * u, axis=0, keepdims=True)
            gwc_ref[2:3, cs] += jnp.sum(d * u, axis=0, keepdims=True)
        x = x1_ref[...]
        r = _rms_r(x)
        n = x * r
        dx1_ref[...] = dx2_ref[...] + _rms_bwd(dh, n, r, g_ref[...])
        gg_ref[...] += jnp.sum(dh * n, axis=0, keepdims=True)

    nxt = pl.BlockSpec((BF16_ROWS, D_FF), lambda i: (jnp.minimum((i + 1) * k16, n16 - 1), 0))
    return pl.pallas_call(
        body, name="ffn_bwd_up", grid=(T // tm,),
        in_specs=[_row(tm, D_FF), _row(tm, D_FF), nxt, nxt, _row(tm, 2 * D_FF), _row(tm, D_MODEL), _row(tm, D_MODEL),
                  _full(g_ffn.shape), _full(w_conv.shape), _resident(w_up.shape)],
        out_specs=[_row(tm, 2 * D_FF), _row(tm, D_MODEL), _full((1, D_MODEL)), _full((3, 2 * D_FF))],
        out_shape=[_sds((T, 2 * D_FF), BF16), _sds((T, D_MODEL), F32), _sds((1, D_MODEL), F32), _sds((3, 2 * D_FF), F32)],
        compiler_params=_cp(("arbitrary",), 56),
    )(*_hbm(d_gate, d_val, d_gate, d_val, upre, dx2, x1, g_ffn, w_conv, w_up))


def _matmul_tn(a, b, tn, tk, name):
    T, M = a.shape
    N = b.shape[1]
    nk = T // tk

    def body(a_ref, b_ref, o_ref, acc_ref):
        k = pl.program_id(1)

        @pl.when(k == 0)
        def _():
            acc_ref[...] = jnp.zeros_like(acc_ref)

        acc_ref[...] += _dot_tn(a_ref[...], b_ref[...])

        @pl.when(k == nk - 1)
        def _():
            o_ref[...] = acc_ref[...].astype(BF16)

    return pl.pallas_call(
        body, name=name, grid=(N // tn, nk),
        in_specs=[pl.BlockSpec((tk, M), lambda j, k: (k, 0)), pl.BlockSpec((tk, tn), lambda j, k: (k, j))],
        out_specs=pl.BlockSpec((M, tn), lambda j, k: (0, j)), out_shape=_sds((M, N), BF16),
        scratch_shapes=[pltpu.VMEM((M, tn), F32)],
        compiler_params=_cp(("arbitrary", "arbitrary"), 48),
    )(*_hbm(a, b))


def _merge_bwd(dx1, merged, y_a, y_b, proj_g, w_pa, w_pb, w_out, tm, after=None):
    T = dx1.shape[0]

    nt = T // tm
    pshape = (A_WIDTH, D_MODEL)
    order = [] if after is None else [after]

    def body(*refs):
        dx_ref, mg_ref, ya_ref, yb_ref, g_ref, wpa_ref, wpb_ref, wo_ref = refs[:8]
        dg_ref, dya_ref, dyb_ref, gwo_out, gwpa_out, gwpb_out, gwo_ref, gwpa_ref, gwpb_ref = refs[8 + len(order):]
        i = pl.program_id(0)
        dx = dx_ref[...].astype(BF16)
        dm = _dot_nt(dx, wo_ref[...])
        g = g_ref[...].astype(F32)
        ya = ya_ref[...]
        yb = yb_ref[...]
        pa = _dot_stacked(ya, wpa_ref)
        pb = _dot_stacked(yb, wpb_ref)
        sa = _sigmoid(g[:, :D_MODEL])
        sb = _sigmoid(g[:, D_MODEL:])
        dpa = (dm * sa).astype(BF16)
        dpb = (dm * sb).astype(BF16)
        dg_ref[:, :D_MODEL] = (dm * pa * (sa * (1.0 - sa))).astype(BF16)
        dg_ref[:, D_MODEL:] = (dm * pb * (sb * (1.0 - sb))).astype(BF16)
        dya_ref[...] = _dot_nt_stacked(dpa, wpa_ref).astype(BF16)
        dyb_ref[...] = _dot_nt_stacked(dpb, wpb_ref).astype(BF16)

        @pl.when(i == 0)
        def _():
            for r in (gwo_ref, gwpa_ref, gwpb_ref):
                r[...] = jnp.zeros_like(r)

        gwo_ref[...] += _dot_tn(mg_ref[...], dx)
        gwpa_ref[...] += _dot_tn(ya, dpa)
        gwpb_ref[...] += _dot_tn(yb, dpb)

        @pl.when(i == nt - 1)
        def _():
            gwo_out[...] = gwo_ref[...].astype(BF16)
            gwpa_out[...] = gwpa_ref[...].astype(BF16)
            gwpb_out[...] = gwpb_ref[...].astype(BF16)

    return pl.pallas_call(
        body, name="merge_bwd", grid=(nt,),
        in_specs=[_row(tm, D_MODEL), _row(tm, D_MODEL), _row(tm, A_WIDTH), _row(tm, Q_DIM), _row(tm, G_DIM),
                  _resident(w_pa.shape), _resident(w_pb.shape), _resident(w_out.shape)] + [ANY] * len(order),
        out_specs=[_row(tm, G_DIM), _row(tm, A_WIDTH), _row(tm, Q_DIM),
                   _full(w_out.shape), _full(pshape), _full(pshape)],
        out_shape=[_sds((T, G_DIM), BF16), _sds((T, A_WIDTH), BF16), _sds((T, Q_DIM), BF16),
                   _sds(w_out.shape, BF16), _sds(pshape, BF16), _sds(pshape, BF16)],
        scratch_shapes=[pltpu.VMEM(w_out.shape, F32), pltpu.VMEM(pshape, F32), pltpu.VMEM(pshape, F32)],
        compiler_params=_cp(("arbitrary",), 56),
    )(*_hbm(dx1, merged, y_a, y_b, proj_g, w_pa, w_pb, w_out), *order)


def _sgu_bwd(proj_a, d_ya, g_sgu, w_s, b_st, tm, after=None):
    T = proj_a.shape[0]
    order = [] if after is None else [after]

    def body(*refs):
        p_ref, dy_ref, g_ref, ws_ref, bs_ref = refs[:5]
        dp_ref, gws_ref, gbs_ref, gg_ref = refs[5 + len(order):]
        tril = _tril()
        g = g_ref[...]
        pu, pv, u, tu, vv, tv, rv, vn = _sgu_parts(p_ref[...].astype(F32), g)
        dy = dy_ref[...].astype(F32)

        @pl.when(pl.program_id(0) == 0)
        def _():
            for r in (gws_ref, gbs_ref, gg_ref):
                r[...] = jnp.zeros_like(r)

        du_cols = []
        dvn_cols = []
        for gi in range(A_GROUPS):
            wm = jnp.where(tril, ws_ref[gi], 0.0).astype(BF16)
            wmt = wm.astype(F32).T.astype(BF16)
            bcol = bs_ref[:, gi:gi + 1]
            cs = slice(gi * CHUNK, (gi + 1) * CHUNK)
            du_rows = []
            dvn_rows = []
            gw = jnp.zeros((CHUNK, CHUNK), F32)
            gb = jnp.zeros((CHUNK, 1), F32)
            for c in range(tm // CHUNK):
                rs = slice(c * CHUNK, (c + 1) * CHUNK)
                vn_c = vn[rs, cs]
                s = _dot(wm, vn_c) + bcol
                dy_c = dy[rs, cs]
                ds = dy_c * u[rs, cs]
                du_rows.append(dy_c * s)
                dsb = ds.astype(BF16)
                gw = gw + _dot_nt(dsb, vn_c)
                gb = gb + jnp.sum(ds, axis=-1, keepdims=True)
                dvn_rows.append(_dot(wmt, dsb))
            gws_ref[gi] += jnp.where(tril, gw, 0.0)
            gbs_ref[:, gi:gi + 1] += gb
            du_cols.append(jnp.concatenate(du_rows, axis=0))
            dvn_cols.append(jnp.concatenate(dvn_rows, axis=0))
        du = jnp.concatenate(du_cols, axis=1)
        dvn = jnp.concatenate(dvn_cols, axis=1)
        vhat = vv * rv
        gg_ref[...] += jnp.sum(dvn * vhat, axis=0, keepdims=True)
        dvv = _rms_bwd(dvn, vhat, rv, g)
        dp_ref[:, :A_WIDTH] = (du * _gelu_grad(pu, tu)).astype(BF16)
        dp_ref[:, A_WIDTH:] = (dvv * _gelu_grad(pv, tv)).astype(BF16)

    return pl.pallas_call(
        body, name="sgu_bwd", grid=(T // tm,),
        in_specs=[_row(tm, A_DIM), _row(tm, A_WIDTH), _full(g_sgu.shape), _full(w_s.shape), _full(b_st.shape)] + [ANY] * len(order),
        out_specs=[_row(tm, A_DIM), _full(w_s.shape), _full(b_st.shape), _full(g_sgu.shape)],
        out_shape=[_sds((T, A_DIM), BF16), _sds(w_s.shape, F32), _sds(b_st.shape, F32), _sds(g_sgu.shape, F32)],
        compiler_params=_cp(("arbitrary",)),
    )(*_hbm(proj_a, d_ya, g_sgu, w_s, b_st), *order)


def _attn_bwd(proj_b, d_yb, sinks, rel_bias, n_seq, seq):
    nb = seq // CHUNK
    bk = jnp.asarray(_band_buckets())

    def body(qkv_ref, do_ref, bk_ref, rel_ref, sink_ref, d_ref, gs_ref, gr_ref,
             bias_scr, sink_scr, kvar_scr, dbias_scr, dk_scr, dv_scr, ds_scr):
        b = pl.program_id(0)
        _attn_setup(bias_scr, sink_scr, kvar_scr, qkv_ref, bk_ref, rel_ref, sink_ref)
        ones = jnp.ones((2 * CHUNK, LANES), BF16)

        @pl.when(b == 0)
        def _():
            dbias_scr[...] = jnp.zeros_like(dbias_scr)
            ds_scr[...] = jnp.zeros_like(ds_scr)

        dk_scr[...] = jnp.zeros_like(dk_scr)
        dv_scr[...] = jnp.zeros_like(dv_scr)

        def transposed(a):
            return a.astype(F32).T.astype(BF16)

        def blk(n, carry):
            r0, kv, vv = _attn_block_inputs(kvar_scr, n)
            prob, psink = _attn_probs(qkv_ref, r0, n, kv, bias_scr, sink_scr, ones)
            dp = jnp.concatenate([_dot_nt(do_ref[pl.ds(r0, CHUNK), (h // 2) * LANES:(h // 2 + 1) * LANES], vv[h // 4][h % 2])
                                  for h in range(N_HEADS)], axis=0)
            delta = _rowsum(prob * dp, ones)
            dsc = prob * (dp - _both(delta))
            ds_scr[...] += psink * delta
            dbias_scr[...] += dsc
            dsb = (dsc * (HEAD_DIM ** -0.5)).astype(BF16)
            pb = prob.astype(BF16)
            dkt = [jnp.zeros((HEAD_DIM, 2 * CHUNK), F32) for _ in range(2)]
            dvt = [jnp.zeros((HEAD_DIM, 2 * CHUNK), F32) for _ in range(2)]
            for pr in range(N_HEADS // 2):
                ps = slice(pr * LANES, (pr + 1) * LANES)
                qpt = transposed(qkv_ref[pl.ds(r0, CHUNK), ps])
                dopt = transposed(do_ref[pl.ds(r0, CHUNK), ps])
                kvh = pr // 2
                dq = jnp.zeros((CHUNK, LANES), F32)
                for hh in range(2):
                    hr = _head_rows(2 * pr + hh)
                    rows = slice(hh * HEAD_DIM, (hh + 1) * HEAD_DIM)
                    dq = dq + _dot(dsb[hr], kv[kvh][hh])
                    dkt[kvh] = dkt[kvh] + _dot(qpt, dsb[hr])[rows]
                    dvt[kvh] = dvt[kvh] + _dot(dopt, pb[hr])[rows]
                d_ref[pl.ds(r0, CHUNK), ps] = dq.astype(BF16)
            dk_scr[:, pl.ds(r0, 2 * CHUNK)] += jnp.concatenate(dkt, axis=0)
            dv_scr[:, pl.ds(r0, 2 * CHUNK)] += jnp.concatenate(dvt, axis=0)
            return carry

        lax.fori_loop(0, nb, blk, 0)
        for n in range(nb):
            rows = slice(n * CHUNK, (n + 1) * CHUNK)
            cols = slice((n + 1) * CHUNK, (n + 2) * CHUNK)
            d_ref[rows, Q_DIM:Q_DIM + KV_DIM] = dk_scr[:, cols].T.astype(BF16)
            d_ref[rows, Q_DIM + KV_DIM:] = dv_scr[:, cols].T.astype(BF16)

        @pl.when(b == n_seq - 1)
        def _():
            bkv = bk_ref[...]
            for h in range(N_HEADS):
                gs_ref[0:1, h:h + 1] = -jnp.sum(ds_scr[_head_rows(h), 0:1], axis=0, keepdims=True)
                db = dbias_scr[_head_rows(h), :]
                for bb in range(N_BUCKETS):
                    part = jnp.sum(jnp.where(bkv == bb, db, 0.0), axis=-1, keepdims=True)
                    gr_ref[bb:bb + 1, h:h + 1] = jnp.sum(part, axis=0, keepdims=True)

    smem = pl.BlockSpec(memory_space=pltpu.SMEM)
    return pl.pallas_call(
        body, name="attn_bwd", grid=(n_seq,),
        in_specs=[_row(seq, B_DIM), _row(seq, Q_DIM), _full(bk.shape), smem, smem],
        out_specs=[_row(seq, B_DIM), _full((1, N_HEADS)), _full((N_BUCKETS, N_HEADS))],
        out_shape=[_sds((n_seq * seq, B_DIM), BF16), _sds((1, N_HEADS), F32), _sds((N_BUCKETS, N_HEADS), F32)],
        scratch_shapes=[pltpu.VMEM((HEAD_ROWS, 2 * CHUNK), F32), pltpu.VMEM((HEAD_ROWS, LANES), F32),
                        pltpu.VMEM((8, seq, KV_DIM), BF16), pltpu.VMEM((HEAD_ROWS, 2 * CHUNK), F32),
                        pltpu.VMEM((KV_DIM, seq + CHUNK), F32), pltpu.VMEM((KV_DIM, seq + CHUNK), F32),
                        pltpu.VMEM((HEAD_ROWS, LANES), F32)],
        compiler_params=_cp(("arbitrary",), 40),
    )(*_hbm(proj_b, d_yb, bk), rel_bias, sinks)


def _inproj_bwd(d_g, d_a, d_b, x2, dx1, g_mix, w_in, tm, after=None):
    T = x2.shape[0]
    order = [] if after is None else [after]

    def body(*refs):
        dg_ref, da_ref, db_ref, x_ref, dx1_ref, g_ref, w_ref = refs[:7]
        gx_ref, gg_ref = refs[7 + len(order):]
        dh = (_dot_nt(dg_ref[...], w_ref[:, _G_COLS]) + _dot_nt(da_ref[...], w_ref[:, _A_COLS])
              + _dot_nt(db_ref[...], w_ref[:, _B_COLS]))
        x = x_ref[...]
        r = _rms_r(x)
        n = x * r
        gx_ref[...] = dx1_ref[...] + _rms_bwd(dh, n, r, g_ref[...])

        @pl.when(pl.program_id(0) == 0)
        def _():
            gg_ref[...] = jnp.zeros_like(gg_ref)

        gg_ref[...] += jnp.sum(dh * n, axis=0, keepdims=True)

    return pl.pallas_call(
        body, name="inproj_bwd", grid=(T // tm,),
        in_specs=[_row(tm, G_DIM), _row(tm, A_DIM), _row(tm, B_DIM), _row(tm, D_MODEL), _row(tm, D_MODEL),
                  _full(g_mix.shape), _resident(w_in.shape)] + [ANY] * len(order),
        out_specs=[_row(tm, D_MODEL), _full((1, D_MODEL))],
        out_shape=[_sds((T, D_MODEL), F32), _sds((1, D_MODEL), F32)],
        compiler_params=_cp(("arbitrary",), 48),
    )(*_hbm(d_g, d_a, d_b, x2, dx1, g_mix, w_in), *order)


IN_SHARD = (A_DIM + B_DIM + G_DIM) // N_CHIPS


def _unstack_w_in(stack):
    tr = 256

    def body(s_ref, o_ref):
        for i in range(N_CHIPS):
            o_ref[:, i * IN_SHARD:(i + 1) * IN_SHARD] = s_ref[i]

    return pl.pallas_call(
        body, name="unstack_w_in", grid=(D_MODEL // tr,),
        in_specs=[pl.BlockSpec((N_CHIPS, tr, IN_SHARD), lambda r: (0, r, 0))],
        out_specs=pl.BlockSpec((tr, N_CHIPS * IN_SHARD), lambda r: (r, 0)),
        out_shape=_sds((D_MODEL, N_CHIPS * IN_SHARD), stack.dtype),
        compiler_params=_cp(("arbitrary",)),
    )(*_hbm(stack))


def _stack_grad_w_in(gw_a, gw_b, gw_g):
    tr = 256

    def body(a_ref, b_ref, g_ref, o_ref):
        full = jnp.concatenate([a_ref[...], b_ref[...], g_ref[...]], axis=1)
        for i in range(N_CHIPS):
            o_ref[i] = full[:, i * IN_SHARD:(i + 1) * IN_SHARD]

    return pl.pallas_call(
        body, name="stack_grad_w_in", grid=(D_MODEL // tr,),
        in_specs=[_row(tr, A_DIM), _row(tr, B_DIM), _row(tr, G_DIM)],
        out_specs=pl.BlockSpec((N_CHIPS, tr, IN_SHARD), lambda r: (0, r, 0)),
        out_shape=_sds((N_CHIPS, D_MODEL, IN_SHARD), gw_a.dtype),
        compiler_params=_cp(("arbitrary",)),
    )(*_hbm(gw_a, gw_b, gw_g))


def _local_step(x, target, g_mix, g_sgu, w_s, b_s, sinks, rel_bias, g_ffn, b_conv, g_final,
                w_in, w_conv, late_weights, on_grads, after=None):
    n_seq, seq, _ = x.shape
    T = n_seq * seq
    tm = min(ROW_TILE, seq)
    tw = min(GRAD_ROW_TILE, T)
    tf = min(WIDE_ROW_TILE, seq)
    x2 = x.reshape(T, D_MODEL)
    tgt = target.reshape(T, D_MODEL)
    b_st = b_s.T
    g_fin = g_final.reshape(1, D_MODEL)

    proj_g, proj_a, proj_b, h = _inproj(x2, g_mix, w_in, tm, after)
    y_a = _sgu_fwd(proj_a, g_sgu, w_s, b_st, tm)
    y_b = _attn_fwd(proj_b, sinks, rel_bias, n_seq, seq)
    w_pa, w_pb, w_out, w_up, w_down = late_weights(y_b)
    x1, merged = _merge_fwd(x2, y_a, y_b, proj_g, w_pa, w_pb, w_out, tm)
    upre, h2, gate, val = _upproj(x1, g_ffn, w_up, w_conv, b_conv, tf, seq)
    dx2, loss, gg_final = _ffn_down_loss(gate, val, x1, tgt, w_down, g_fin, tm)

    d_gate, d_val, gw_down, gb_g, gb_v = _ffn_bwd_act(gate, val, dx2, w_down, tw)
    gb_conv = jnp.concatenate([gb_g, gb_v], axis=1)
    d_upre, dx1, gg_ffn, gw_conv = _ffn_bwd_up(d_gate, d_val, upre, dx2, x1, g_ffn, w_conv, w_up, tf, seq)
    gw_up = _matmul_tn(h2, d_upre, 2 * D_FF // 4, min(2 * GRAD_ROW_TILE, T), "grad_w_up")
    sent = on_grads("ffn", dict(w_up=gw_up, w_down=gw_down))
    d_g, d_ya, d_yb, gw_out, gw_pa, gw_pb = _merge_bwd(dx1, merged, y_a, y_b, proj_g, w_pa, w_pb, w_out, tf, sent)
    sent = on_grads("proj", dict(w_pa=gw_pa, w_pb=gw_pb, w_out=gw_out))
    d_a, gw_s, gb_st, gg_sgu = _sgu_bwd(proj_a, d_ya, g_sgu, w_s, b_st, tm, sent)
    d_b, g_sinks, g_rel = _attn_bwd(proj_b, _tie(d_yb, d_a), sinks, rel_bias, n_seq, seq)
    gw_g = _matmul_tn(h, _tie(d_g, d_b), D_MODEL, min(2 * GRAD_ROW_TILE, T), "grad_w_in_gate")
    gw_a = _matmul_tn(h, _tie(d_a, gw_g), A_DIM, min(2 * GRAD_ROW_TILE, T), "grad_w_in_a")
    gw_b = _matmul_tn(h, _tie(d_b, gw_a), B_DIM, min(2 * GRAD_ROW_TILE, T), "grad_w_in_b")
    gw_in = _stack_grad_w_in(gw_a, gw_b, gw_g)
    sent = on_grads("in", dict(w_in=gw_in))
    grad_x, gg_mix = _inproj_bwd(d_g, d_a, d_b, x2, dx1, g_mix, w_in, tm, sent)

    small = dict(g_mix=gg_mix, g_sgu=gg_sgu, w_s=gw_s, b_s=gb_st.T, sinks=g_sinks, rel_bias=g_rel,
                 g_ffn=gg_ffn, b_conv=gb_conv, g_final=gg_final, w_conv=gw_conv)
    big = dict(w_in=gw_in, w_pa=gw_pa, w_pb=gw_pb, w_out=gw_out, w_up=gw_up, w_down=gw_down)
    return loss, grad_x.reshape(x.shape), small, big


_MIXER = ("w_in", "w_pa", "w_pb", "w_out")
_FFN = ("w_up", "w_down")
_BIG = _MIXER + _FFN

_SMALL = (("loss", (1, 1)), ("g_final", (1, D_MODEL)), ("g_mix", (1, D_MODEL)), ("g_ffn", (1, D_MODEL)),
          ("g_sgu", (1, A_WIDTH)), ("b_s", (A_GROUPS, CHUNK)), ("sinks", (1, N_HEADS)), ("rel_bias", (N_BUCKETS, N_HEADS)),
          ("b_conv", (1, 2 * D_FF)), ("w_conv", (3, 2 * D_FF)), ("w_s", (A_GROUPS, CHUNK, CHUNK)))
SMALL_ROWS = 96


def _pack_small(vals):
    flat = jnp.concatenate([vals[n].astype(F32).reshape(-1) for n, _ in _SMALL])
    flat = jnp.pad(flat, (0, SMALL_ROWS * D_MODEL - flat.shape[0]))
    return flat.reshape(SMALL_ROWS, D_MODEL)


def _unpack_small(buf):
    flat = buf.reshape(-1)
    out = {}
    off = 0
    for n, shp in _SMALL:
        k = int(np.prod(shp))
        out[n] = flat[off:off + k].reshape(shp)
        off += k
    return out


def _mesh_pos():
    return lax.axis_index("x"), lax.axis_index("y"), lax.axis_index("c")


def _other_chips(x, y):
    return [(1 - x, y), (x, 1 - y), (1 - x, 1 - y)]


def _remote(src, dst, send_sem, recv_sem, to):
    return pltpu.make_async_remote_copy(src_ref=src, dst_ref=dst, send_sem=send_sem, recv_sem=recv_sem,
                                        device_id=to, device_id_type=MESH)


def _own_slot(own, n, at):
    return lax.dynamic_update_slice(lax.empty((n,) + own.shape, own.dtype), own[None], (at,) + (0,) * own.ndim)


def _allgather_weights(stacks, wc_stack):
    names = list(stacks)
    n = len(names)

    def body(*refs):
        ins, outs = refs[:n + 1], refs[n + 1:2 * n + 2]
        send_sems, recv_sems = refs[2 * n + 2:]
        x, y, c = _mesh_pos()
        me = 2 * x + y
        sibling = (x, y, 1 - c)
        chips = _other_chips(x, y)

        def half(ref, chip, hc):
            hr = ref.shape[1] // 2
            return ref.at[chip, pl.ds(hc * hr, hr), :]

        first = []
        for k in range(n):
            first += [_remote(half(ins[k], me, c), half(outs[k], me, c), send_sems.at[6 * k + j], recv_sems.at[6 * k + j], (cx, cy, c))
                      for j, (cx, cy) in enumerate(chips)]
        first += [_remote(ins[n].at[me], outs[n].at[me], send_sems.at[6 * n + j], recv_sems.at[6 * n + j], (cx, cy, c))
                  for j, (cx, cy) in enumerate(chips)]
        for cp in first:
            cp.start()
        passed = []
        for k in range(n):
            for j, (cx, cy) in enumerate(chips):
                landed = half(outs[k], 2 * cx + cy, c)
                _remote(landed, landed, send_sems.at[6 * k + j], recv_sems.at[6 * k + j], (x, y, c)).wait_recv()
                passed.append(_remote(landed, landed, send_sems.at[6 * k + 3 + j], recv_sems.at[6 * k + 3 + j], sibling))
                passed[-1].start()
        for k in range(n):
            for j, (cx, cy) in enumerate(chips):
                theirs = half(outs[k], 2 * cx + cy, 1 - c)
                _remote(theirs, theirs, send_sems.at[6 * k + 3 + j], recv_sems.at[6 * k + 3 + j], (x, y, c)).wait_recv()
        for j, (cx, cy) in enumerate(chips):
            slot = outs[n].at[2 * cx + cy]
            _remote(slot, slot, send_sems.at[6 * n + j], recv_sems.at[6 * n + j], (x, y, c)).wait_recv()
        for cp in first + passed:
            cp.wait_send()

    arrays = [stacks[k] for k in names] + [wc_stack]
    outs = pl.pallas_call(
        body, name="allgather_weights",
        in_specs=[HBM] * (n + 1), out_specs=[HBM] * (n + 1), input_output_aliases={k: k for k in range(n + 1)},
        out_shape=[_sds(a.shape, a.dtype) for a in arrays],
        scratch_shapes=[pltpu.SemaphoreType.DMA((6 * n + 3,)), pltpu.SemaphoreType.DMA((6 * n + 3,))],
    )(*arrays)
    return dict(zip(names, outs[:n])), outs[n]


_KIND = {"w_in": "stack", "w_pa": "col", "w_pb": "col", "w_up": "col", "w_out": "row", "w_down": "row"}


def _half_view(ref, kind, h):
    if kind == "stack":
        k = ref.shape[1] // 2
        return ref.at[:, pl.ds(h * k, k), :]
    if kind == "col":
        k = ref.shape[0] // 2
        return ref.at[pl.ds(h * k, k), :]
    k = ref.shape[1] // 2
    return ref.at[:, pl.ds(h * k, k)]


def _shard_view(ref, kind, i):
    if kind == "stack":
        return ref.at[i]
    if kind == "col":
        k = ref.shape[1] // N_CHIPS
        return ref.at[:, pl.ds(i * k, k)]
    k = ref.shape[0] // N_CHIPS
    return ref.at[pl.ds(i * k, k), :]


def _region_view(ref, kind, h):
    if kind == "row":
        k = ref.shape[1] // 2
        return ref.at[:, pl.ds(h * k, k)]
    k = ref.shape[0] // 2
    return ref.at[pl.ds(h * k, k), :]


def _half_shape(shape, kind):
    if kind == "stack":
        return (shape[0], shape[1] // 2, shape[2])
    return (shape[0] // 2, shape[1]) if kind == "col" else (shape[0], shape[1] // 2)


def _part_shape(half_shape, kind):
    if kind == "stack":
        return tuple(half_shape[1:])
    k, w = half_shape
    return (k, w // N_CHIPS) if kind == "col" else (k // N_CHIPS, w)


_DATAFLOW = pltpu.SideEffectType.DATAFLOW_SIDE_EFFECTING
_TOKEN = (SUBLANES, LANES)


def _split_start(name, arrays, n_sems, issue, after=None):
    n = len(arrays)
    order = [] if after is None else [after]

    def body(*refs):
        base = n + len(order)
        issue(refs[:n], refs[base], refs[base + 1])
        refs[-1][...] = jnp.zeros(_TOKEN, F32)

    outs = pl.pallas_call(
        body, name=name,
        in_specs=[HBM] * n + [ANY] * len(order), out_specs=[SEM, SEM] + [HBM] * n + [pl.BlockSpec(memory_space=pltpu.VMEM)],
        out_shape=[pltpu.SemaphoreType.DMA((n_sems,)), pltpu.SemaphoreType.DMA((n_sems,))]
        + [pltpu.HBM(a.shape, a.dtype) for a in arrays] + [_sds(_TOKEN, F32)],
        input_output_aliases={k: 2 + k for k in range(n)},
        compiler_params=pltpu.CompilerParams(has_side_effects=_DATAFLOW),
    )(*[pltpu.with_memory_space_constraint(a, pltpu.HBM) for a in arrays], *order)
    return outs[0], outs[1], list(outs[2:2 + n]), outs[-1]


def _split_wait(name, started, waits, after):
    send_sems, recv_sems, arrays, _ = started
    n = len(arrays)

    def body(*refs):
        waits(refs[:n], refs[n], refs[n + 1])

    return pl.pallas_call(
        body, name=name,
        in_specs=[HBM] * n + [SEM, SEM, ANY], out_specs=[HBM] * n,
        out_shape=[pltpu.HBM(a.shape, a.dtype) for a in arrays],
        input_output_aliases={k: k for k in range(n)},
        compiler_params=pltpu.CompilerParams(has_side_effects=_DATAFLOW),
    )(*arrays, send_sems, recv_sems, after)


def _wait_both(src, dst, send_sem, recv_sem):
    x, y, c = _mesh_pos()
    cp = _remote(src, dst, send_sem, recv_sem, (x, y, c))
    cp.wait_send()
    cp.wait_recv()


def _pair_exchange_start(parts, tag, after):
    names = list(parts)
    n = len(names)
    lands = [lax.empty(_half_shape(parts[k].shape, _KIND[k]), parts[k].dtype) for k in names]

    def issue(refs, send_sems, recv_sems):
        x, y, c = _mesh_pos()
        for hc in range(2):
            @pl.when(c == hc)
            def _():
                for k in range(n):
                    _remote(_half_view(refs[k], _KIND[names[k]], 1 - hc), refs[n + k], send_sems.at[k], recv_sems.at[k],
                            (x, y, 1 - c)).start()

    return names, _split_start("grad_pair_exchange_start_" + tag, [parts[k] for k in names] + lands, n, issue, after)


def _pair_exchange_wait(pending, tag, after):
    names, started = pending
    n = len(names)

    def waits(refs, send_sems, recv_sems):
        for k in range(n):
            _wait_both(_half_view(refs[k], _KIND[names[k]], 0), refs[n + k], send_sems.at[k], recv_sems.at[k])

    outs = _split_wait("grad_pair_exchange_wait_" + tag, started, waits, after)
    return dict(zip(names, outs[:n])), dict(zip(names, outs[n:]))


def _half_blocks(shape, kind):
    if kind == "stack":
        _, k, w = shape
        tr = k // 2
        nb = 1
        return (N_CHIPS, nb), (1, tr, w), (lambda i, r, s: (i, r, 0)), (lambda i, r, s: (i, s[1] * nb + r, 0))
    k, w = shape
    if kind == "col":
        tr = 256
        nb = k // 2 // tr
        return (nb,), (tr, w), (lambda r, s: (r, 0)), (lambda r, s: (s[1] * nb + r, 0))
    tr = k // N_CHIPS
    return (N_CHIPS,), (tr, w // 2), (lambda r, s: (r, 0)), (lambda r, s: (r, s[1]))


def _pair_add(part, from_sibling, name, pos):
    kind = _KIND[name]
    grid, block, half_map, full_map = _half_blocks(part.shape, kind)

    def body(s_ref, p_ref, q_ref, o_ref):
        o_ref[...] = (p_ref[...].astype(F32) + q_ref[...].astype(F32)).astype(BF16)

    return pl.pallas_call(
        body, name="grad_pair_add_" + name,
        grid_spec=pltpu.PrefetchScalarGridSpec(
            num_scalar_prefetch=1, grid=grid,
            in_specs=[pl.BlockSpec(block, full_map), pl.BlockSpec(block, half_map)],
            out_specs=pl.BlockSpec(block, half_map)),
        out_shape=_sds(from_sibling.shape, BF16),
        compiler_params=_cp(("arbitrary",) * len(grid), 40),
    )(pos, *_hbm(part, from_sibling))


def _chip_exchange_start(sums, tag, after):
    names = list(sums)
    n = len(names)
    lands = [lax.empty((3,) + _part_shape(sums[k].shape, _KIND[k]), sums[k].dtype) for k in names]

    def issue(refs, send_sems, recv_sems):
        x, y, c = _mesh_pos()
        me = 2 * x + y
        for i in range(N_CHIPS):
            xi, yi = i // 2, i % 2
            j = jnp.where(xi != x, jnp.where(yi != y, 2, 0), 1)

            @pl.when(i != me)
            def _():
                for k in range(n):
                    _remote(_shard_view(refs[k], _KIND[names[k]], i), refs[n + k].at[j], send_sems.at[3 * k + j],
                            recv_sems.at[3 * k + j], (xi, yi, c)).start()

    return names, _split_start("grad_chip_exchange_start_" + tag, [sums[k] for k in names] + lands, 3 * n, issue, after)


def _chip_exchange_wait(pending, tag, after):
    names, started = pending
    n = len(names)

    def waits(refs, send_sems, recv_sems):
        for k in range(n):
            for j in range(3):
                _wait_both(_shard_view(refs[k], _KIND[names[k]], 0), refs[n + k].at[j], send_sems.at[3 * k + j], recv_sems.at[3 * k + j])

    return dict(zip(names, _split_wait("grad_chip_exchange_wait_" + tag, started, waits, after)[n:]))


def _allgather_start(stacks, after):
    names = list(stacks)

    def issue(refs, send_sems, recv_sems):
        x, y, c = _mesh_pos()
        me = 2 * x + y
        for k, st in enumerate(refs):
            hr = st.shape[1] // 2
            mine = st.at[me, pl.ds(c * hr, hr), :]
            for j, (cx, cy) in enumerate(_other_chips(x, y)):
                _remote(mine, mine, send_sems.at[3 * k + j], recv_sems.at[3 * k + j], (cx, cy, c)).start()

    return names, _split_start("allgather_start", [stacks[k] for k in names], 3 * len(names), issue, after)


def _allgather_wait(pending, after):
    names, started = pending

    def waits(refs, send_sems, recv_sems):
        for k, st in enumerate(refs):
            slot = st.at[0, pl.ds(0, st.shape[1] // 2), :]
            for j in range(3):
                _wait_both(slot, slot, send_sems.at[3 * k + j], recv_sems.at[3 * k + j])

    return dict(zip(names, _split_wait("allgather_wait", started, waits, after)))


def _allgather_forward(stacks):
    names = list(stacks)
    n = len(names)

    def body(*refs):
        ins, outs = refs[:n], refs[n:2 * n]
        send_sems, recv_sems = refs[2 * n:]
        x, y, c = _mesh_pos()
        copies = []
        for k in range(n):
            hr = ins[k].shape[1] // 2
            for j, (cx, cy) in enumerate(_other_chips(x, y)):
                chip = 2 * cx + cy
                copies.append(_remote(ins[k].at[chip, pl.ds(c * hr, hr), :], outs[k].at[chip, pl.ds(c * hr, hr), :],
                                      send_sems.at[3 * k + j], recv_sems.at[3 * k + j], (x, y, 1 - c)))
        for cp in copies:
            cp.start()
        for cp in copies:
            cp.wait()

    arrays = [stacks[k] for k in names]
    outs = pl.pallas_call(
        body, name="allgather_forward", in_specs=[HBM] * n, out_specs=[HBM] * n,
        input_output_aliases={k: k for k in range(n)},
        out_shape=[_sds(a.shape, a.dtype) for a in arrays],
        scratch_shapes=[pltpu.SemaphoreType.DMA((3 * n,)), pltpu.SemaphoreType.DMA((3 * n,))],
    )(*arrays)
    return dict(zip(names, outs))


def _owner_sum(part, from_sibling, from_chips, name, pos, shard_shape):
    kind = _KIND[name]
    _, pk, pw = from_chips.shape
    if kind == "row":
        tr, nb = pk, 1
        p_spec = pl.BlockSpec((tr, pw), lambda r, s: (s[0], s[1]))
        q_spec = pl.BlockSpec((tr, pw), lambda r, s: (s[0], 0))
        o_spec = pl.BlockSpec((tr, pw), lambda r, s: (0, s[1]))
    else:
        tr = 256
        nb = pk // tr
        if kind == "stack":
            p_spec = pl.BlockSpec((None, tr, pw), lambda r, s: (s[0], s[1] * nb + r, 0))
            q_spec = pl.BlockSpec((None, tr, pw), lambda r, s: (s[0], r, 0))
        else:
            p_spec = pl.BlockSpec((tr, pw), lambda r, s: (s[1] * nb + r, s[0]))
            q_spec = pl.BlockSpec((tr, pw), lambda r, s: (r, s[0]))
        o_spec = pl.BlockSpec((tr, pw), lambda r, s: (s[1] * nb + r, 0))

    def body(s_ref, p_ref, q_ref, r_ref, o_ref):
        acc = p_ref[...].astype(F32) + q_ref[...].astype(F32)
        for j in range(3):
            acc = acc + r_ref[j].astype(F32)
        o_ref[...] = acc

    return pl.pallas_call(
        body, name="grad_owner_sum_" + name,
        grid_spec=pltpu.PrefetchScalarGridSpec(
            num_scalar_prefetch=1, grid=(nb,),
            in_specs=[p_spec, q_spec, pl.BlockSpec((3, tr, pw), lambda r, s: (0, r, 0))],
            out_specs=o_spec),
        out_shape=_sds(shard_shape, F32),
        compiler_params=_cp(("arbitrary",), 32),
    )(pos, *_hbm(part, from_sibling, from_chips))


def _pair_share_start(shards, tag, after):
    names = list(shards)

    def issue(refs, send_sems, recv_sems):
        x, y, c = _mesh_pos()
        for hc in range(2):
            @pl.when(c == hc)
            def _():
                for k, g in enumerate(refs):
                    mine = _region_view(g, _KIND[names[k]], hc)
                    _remote(mine, mine, send_sems.at[k], recv_sems.at[k], (x, y, 1 - c)).start()

    return names, _split_start("grad_pair_share_start_" + tag, [shards[k] for k in names], len(names), issue, after)


def _pair_share_wait(pending, tag, after):
    names, started = pending

    def waits(refs, send_sems, recv_sems):
        for k, g in enumerate(refs):
            region = _region_view(g, _KIND[names[k]], 0)
            _wait_both(region, region, send_sems.at[k], recv_sems.at[k])

    return dict(zip(names, _split_wait("grad_pair_share_wait_" + tag, started, waits, after)))


def _small_exchange_start(slots, after):
    def issue(refs, send_sems, recv_sems):
        x, y, c = _mesh_pos()
        mine = refs[0].at[4 * x + 2 * y + c]
        k = 0
        for px in range(2):
            for py in range(2):
                for pc in range(2):
                    if px + py + pc:
                        peer = (1 - x if px else x, 1 - y if py else y, 1 - c if pc else c)
                        _remote(mine, mine, send_sems.at[k], recv_sems.at[k], peer).start()
                        k += 1

    return _split_start("small_exchange_start", [slots], N_DEV - 1, issue, after)


def _small_exchange_wait(started, after):
    def waits(refs, send_sems, recv_sems):
        slot = refs[0].at[0]
        for k in range(N_DEV - 1):
            _wait_both(slot, slot, send_sems.at[k], recv_sems.at[k])

    return _split_wait("small_exchange_wait", started, waits, after)[0]


def _adam_math(w, g, m, v):
    m = ADAM_B1 * m + (1.0 - ADAM_B1) * g
    v = ADAM_B2 * v + (1.0 - ADAM_B2) * (g * g)
    m_hat = m / (1.0 - ADAM_B1 ** ADAM_STEP)
    v_hat = v / (1.0 - ADAM_B2 ** ADAM_STEP)
    delta = -ADAM_LR * (m_hat / (jnp.sqrt(v_hat) + ADAM_EPS) + ADAM_WD * w)
    return delta, m, v


def _adamw(w, g, m, v, name):
    rows, cols = w.shape
    fits = [t for t in range(SUBLANES, rows, SUBLANES) if rows % t == 0 and t * cols * 4 <= (3 << 19)]
    tr = max(fits) if fits else rows

    def body(w_ref, g_ref, m_ref, v_ref, d_ref, nm_ref, nv_ref, go_ref):
        g = g_ref[...]
        d, nm, nv = _adam_math(w_ref[...], g, m_ref[...], v_ref[...])
        d_ref[...] = d
        nm_ref[...] = nm
        nv_ref[...] = nv
        go_ref[...] = g

    spec = pl.BlockSpec((tr, cols), lambda i: (i, 0))
    return pl.pallas_call(
        body, name=name, grid=(rows // tr,), in_specs=[spec] * 4, out_specs=[spec] * 4,
        out_shape=[_sds(w.shape, F32)] * 4, compiler_params=_cp(("arbitrary",)),
    )(*_hbm(w, g, m, v))


def _small_sum_adamw(gathered, w, m, v):
    def body(a_ref, w_ref, m_ref, v_ref, g_ref, d_ref, nm_ref, nv_ref):
        g = a_ref[0]
        for k in range(1, N_DEV):
            g = g + a_ref[k]
        g_ref[...] = g
        d, nm, nv = _adam_math(w_ref[...], g, m_ref[...], v_ref[...])
        d_ref[...] = d
        nm_ref[...] = nm
        nv_ref[...] = nv

    return pl.pallas_call(
        body, name="small_sum_adamw", out_shape=[_sds(w.shape, F32)] * 4,
    )(gathered, w, m, v)


_NAMES = ("g_mix", "w_in", "g_sgu", "w_s", "b_s", "sinks", "rel_bias", "w_pa", "w_pb", "w_out",
          "g_ffn", "w_up", "w_conv", "b_conv", "w_down", "g_final")

def kernel(x, g_mix, w_in, g_sgu, w_s, b_s, sinks, rel_bias, w_pa, w_pb, w_out, g_ffn, w_up, w_conv, b_conv, w_down, g_final, loss_target, m_g_mix, m_w_in, m_g_sgu, m_w_s, m_b_s, m_sinks, m_rel_bias, m_w_pa, m_w_pb, m_w_out, m_g_ffn, m_w_up, m_w_conv, m_b_conv, m_w_down, m_g_final, v_g_mix, v_w_in, v_g_sgu, v_w_s, v_b_s, v_sinks, v_rel_bias, v_w_pa, v_w_pb, v_w_out, v_g_ffn, v_w_up, v_w_conv, v_b_conv, v_w_down, v_g_final):
    w = dict(g_mix=g_mix, w_in=w_in, g_sgu=g_sgu, w_s=w_s, b_s=b_s, sinks=sinks, rel_bias=rel_bias, w_pa=w_pa, w_pb=w_pb,
             w_out=w_out, g_ffn=g_ffn, w_up=w_up, w_conv=w_conv, b_conv=b_conv, w_down=w_down, g_final=g_final)
    m = dict(g_mix=m_g_mix, w_in=m_w_in, g_sgu=m_g_sgu, w_s=m_w_s, b_s=m_b_s, sinks=m_sinks, rel_bias=m_rel_bias, w_pa=m_w_pa,
             w_pb=m_w_pb, w_out=m_w_out, g_ffn=m_g_ffn, w_up=m_w_up, w_conv=m_w_conv, b_conv=m_b_conv, w_down=m_w_down,
             g_final=m_g_final)
    v = dict(g_mix=v_g_mix, w_in=v_w_in, g_sgu=v_g_sgu, w_s=v_w_s, b_s=v_b_s, sinks=v_sinks, rel_bias=v_rel_bias, w_pa=v_w_pa,
             w_pb=v_w_pb, w_out=v_w_out, g_ffn=v_g_ffn, w_up=v_w_up, w_conv=v_w_conv, b_conv=v_b_conv, w_down=v_w_down,
             g_final=v_g_final)
    xi, yi, ci = _mesh_pos()
    me = 2 * xi + yi

    shard = {n: w[n][0] for n in _BIG}
    shard_shapes = {n: shard[n].shape for n in _BIG}
    wc_shard = w["w_conv"][0]
    wc_pad = jnp.pad(wc_shard, ((0, 5), (0, 0)))
    own = {n: _own_slot(shard[n].astype(BF16), N_CHIPS, me) for n in _BIG}
    stacks, wc_all = _allgather_weights({"w_in": own["w_in"]}, _own_slot(wc_pad, N_CHIPS, me))
    late_gather = _allgather_start({n: own[n] for n in _BIG[1:]}, stacks["w_in"])
    w_conv_full = jnp.concatenate([wc_all[i, :3] for i in range(N_CHIPS)], axis=1)
    w_in_full = _unstack_w_in(stacks["w_in"])
    pos = jnp.stack([me, ci])

    def late_weights(done):
        st = _allgather_forward(_allgather_wait(late_gather, done))
        return st["w_pa"], st["w_pb"], st["w_out"].reshape(D_MODEL, D_MODEL), st["w_up"], st["w_down"].reshape(D_FF, D_MODEL)

    groups = {}

    def stage1(group, parts):
        groups[group] = dict(parts=parts, pair=_pair_exchange_start(parts, group, None))
        return groups[group]["pair"][1][-1]

    def stage2(group, after, order_after):
        g = groups[group]
        g["parts"], g["sib"] = _pair_exchange_wait(g["pair"], group, after)
        g["chip"] = _chip_exchange_start({n: _pair_add(g["parts"][n], g["sib"][n], n, pos) for n in g["parts"]}, group, order_after)
        return g["chip"][1][-1]

    def stage3(group, after, order_after):
        g = groups[group]
        got = _chip_exchange_wait(g["chip"], group, after)
        g["share"] = _pair_share_start(
            {n: _owner_sum(g["parts"][n], g["sib"][n], got[n], n, pos, shard_shapes[n]) for n in g["parts"]}, group, order_after)
        return g["share"][1][-1]

    grads, deltas, new_m, new_v = {}, {}, {}, {}

    def stage4(group, after):
        g_shard = _pair_share_wait(groups[group]["share"], group, after)
        last = None
        for n in g_shard:
            g = _tie(g_shard[n], last)
            if n == "w_in":
                d, nm, nv, gt = _adamw(shard[n].T, g.T, m[n][0].T, v[n][0].T, "adamw_" + n)
                grads[n], deltas[n], new_m[n], new_v[n] = gt.T[None], d.T[None], nm.T[None], nv.T[None]
            else:
                d, nm, nv, go = _adamw(shard[n], g, m[n][0], v[n][0], "adamw_" + n)
                grads[n], deltas[n], new_m[n], new_v[n] = go[None], d[None], nm[None], nv[None]
            last = nv
        return last

    def on_grads(group, parts):
        token = stage1(group, parts)
        some = next(iter(parts.values()))
        if group == "proj":
            token = stage2("ffn", some, token)
        if group == "in":
            token = stage2("proj", some, token)
            token = stage3("ffn", some, token)
            token = stage2("in", token, token)
        return token

    loss, grad_x, small, big = _local_step(
        x, loss_target, w["g_mix"], w["g_sgu"], w["w_s"][0], w["b_s"][0], w["sinks"], w["rel_bias"], w["g_ffn"],
        w["b_conv"], w["g_final"], w_in_full, w_conv_full, late_weights, on_grads, late_gather[1][-1])

    small["loss"] = loss
    small_gather = _small_exchange_start(_own_slot(_pack_small(small), N_DEV, 2 * me + ci), grad_x)
    token = stage3("proj", grad_x, small_gather[-1])
    done = stage4("ffn", token)
    done = stage4("proj", done)
    token = stage3("in", done, None)
    sw = {n: (jnp.zeros((1, 1), F32) if n in ("loss", "w_conv") else w[n]) for n, _ in _SMALL}
    sm = {n: (jnp.zeros((1, 1), F32) if n in ("loss", "w_conv") else m[n]) for n, _ in _SMALL}
    sv = {n: (jnp.zeros((1, 1), F32) if n in ("loss", "w_conv") else v[n]) for n, _ in _SMALL}
    for d in (sw, sm, sv):
        d["w_conv"] = jnp.zeros((3, 2 * D_FF), F32)
    all_small = _small_exchange_wait(small_gather, token)
    s_g, s_d, s_m, s_v = [_unpack_small(a) for a in _small_sum_adamw(all_small, _pack_small(sw), _pack_small(sm), _pack_small(sv))]
    stage4("in", all_small)
    wcols = wc_shard.shape[1]
    g_wc = lax.dynamic_slice(s_g["w_conv"], (0, me * wcols), (3, wcols))
    d, nm, nv, _ = _adamw(wc_shard, g_wc, m["w_conv"][0], v["w_conv"][0], "adamw_w_conv")
    grads["w_conv"], deltas["w_conv"], new_m["w_conv"], new_v["w_conv"] = g_wc[None], d[None], nm[None], nv[None]
    for n, _ in _SMALL:
        if n in ("loss", "w_conv"):
            continue
        shp = w[n].shape
        grads[n], deltas[n], new_m[n], new_v[n] = (s_g[n].reshape(shp), s_d[n].reshape(shp), s_m[n].reshape(shp),
                                                    s_v[n].reshape(shp))

    return (s_g["loss"].reshape(()), grad_x, *[grads[n] for n in _NAMES], *[deltas[n] for n in _NAMES],
            *[new_m[n] for n in _NAMES], *[new_v[n] for n in _NAMES])
```

```python
import functools

import numpy as np
import jax
import jax.numpy as jnp
from jax import lax
from jax.experimental import pallas as pl
from jax.experimental.pallas import tpu as pltpu

F32 = jnp.float32
BF16 = jnp.bfloat16

D_MODEL = 1024
CHUNK = 128
A_GROUPS = 4
A_WIDTH = 512
N_HEADS = 8
HEAD_DIM = 64
Q_DIM = 512
KV_DIM = 128
N_BUCKETS = 32
MAX_DISTANCE = 128
D_FF = 2816
EPS = 1e-6
NEG_INF = -1e30
G_DIM = 2 * D_MODEL
A_DIM = 2 * A_WIDTH
B_DIM = Q_DIM + 2 * KV_DIM
LANES = 128
SUBLANES = 8
ROW_TILE = 512
WIDE_ROW_TILE = 256
COL_CHUNK = 512
GRAD_ROW_TILE = 512
BF16_ROWS = 16
N_CHIPS = 4
N_DEV = 8

ADAM_LR = 0.001
ADAM_B1 = 0.9
ADAM_B2 = 0.999
ADAM_EPS = 1e-08
ADAM_WD = 0.01
ADAM_STEP = 10

MESH = pl.DeviceIdType.MESH
_GELU_C = 0.7978845608028654
_GELU_A = 0.044715


def _cp(sem=None, vmem_mb=None):
    kw = {}
    if sem is not None:
        kw["dimension_semantics"] = sem
    if vmem_mb is not None:
        kw["vmem_limit_bytes"] = vmem_mb << 20
    return pltpu.CompilerParams(**kw)


def _dot(a, b):
    return jnp.dot(a, b, preferred_element_type=F32)


def _dot_nt(a, b):
    return lax.dot_general(a, b, (((1,), (1,)), ((), ())), preferred_element_type=F32)


def _dot_tn(a, b):
    return lax.dot_general(a, b, (((0,), (0,)), ((), ())), preferred_element_type=F32)


def _rms_r(x):
    return lax.rsqrt(jnp.mean(x * x, axis=-1, keepdims=True) + EPS)


def _rms_bwd(dh, n, r, g):
    dn = dh * g
    return r * (dn - n * jnp.mean(dn * n, axis=-1, keepdims=True))


def _gelu(x):
    t = jnp.tanh(_GELU_C * (x + _GELU_A * (x * x * x)))
    return 0.5 * x * (1.0 + t), t


def _gelu_grad(x, t):
    return 0.5 * (1.0 + t) + 0.5 * x * (1.0 - t * t) * (_GELU_C * (1.0 + 3.0 * _GELU_A * x * x))


def _sigmoid(x):
    return 1.0 / (1.0 + jnp.exp(-x))


def _tie(x, dep):
    return x if dep is None else lax.optimization_barrier((x, dep))[0]


def _row(tm, w):
    return pl.BlockSpec((tm, w), lambda i: (i, 0))


def _full(shape):
    nd = len(shape)
    return pl.BlockSpec(tuple(shape), lambda *_: (0,) * nd)


def _resident(shape):
    nd = len(shape)
    return pl.BlockSpec(tuple(shape), lambda *_: (0,) * nd, pipeline_mode=pl.Buffered(1))


def _sds(shape, dtype):
    return jax.ShapeDtypeStruct(tuple(shape), dtype)


def _hbm(*arrays):
    return [pltpu.with_memory_space_constraint(a, pltpu.HBM) for a in arrays]


HBM = pl.BlockSpec(memory_space=pltpu.HBM)
ANY = pl.BlockSpec(memory_space=pl.ANY)
SEM = pl.BlockSpec(memory_space=pltpu.SEMAPHORE)


def _band_buckets():
    i = np.arange(CHUNK)[:, None]
    j = np.arange(2 * CHUNK)[None, :]
    dist = i + CHUNK - j
    valid = (dist >= 0) & (dist < CHUNK)
    d = np.clip(dist, 0, None)
    max_exact = N_BUCKETS // 2
    large = max_exact + (np.log(np.maximum(d, 1) / max_exact) / np.log(MAX_DISTANCE / max_exact)
                         * (N_BUCKETS - max_exact)).astype(np.int32)
    large = np.minimum(large, N_BUCKETS - 1)
    buckets = np.where(d < max_exact, d, large).astype(np.int32)
    return np.where(valid, buckets, -1).astype(np.int32)


_A_COLS = slice(0, A_DIM)
_B_COLS = slice(A_DIM, A_DIM + B_DIM)
_G_COLS = slice(A_DIM + B_DIM, A_DIM + B_DIM + G_DIM)


def _inproj(x2, g_mix, w_in, tm, after=None):
    T = x2.shape[0]
    order = [] if after is None else [after]

    def body(*refs):
        x_ref, g_ref, w_ref = refs[:3]
        pg_ref, pa_ref, pb_ref, h_ref = refs[3 + len(order):]
        x = x_ref[...]
        h = (x * _rms_r(x) * g_ref[...]).astype(BF16)
        h_ref[...] = h
        pg_ref[...] = _dot(h, w_ref[:, _G_COLS]).astype(BF16)
        pa_ref[...] = _dot(h, w_ref[:, _A_COLS]).astype(BF16)
        pb_ref[...] = _dot(h, w_ref[:, _B_COLS]).astype(BF16)

    return pl.pallas_call(
        body, name="inproj", grid=(T // tm,),
        in_specs=[_row(tm, D_MODEL), _full(g_mix.shape), _resident(w_in.shape)] + [ANY] * len(order),
        out_specs=[_row(tm, G_DIM), _row(tm, A_DIM), _row(tm, B_DIM), _row(tm, D_MODEL)],
        out_shape=[_sds((T, G_DIM), BF16), _sds((T, A_DIM), BF16), _sds((T, B_DIM), BF16), _sds((T, D_MODEL), BF16)],
        compiler_params=_cp(("arbitrary",), 48),
    )(*_hbm(x2, g_mix, w_in), *order)


def _sgu_parts(p, g):
    pu = p[:, :A_WIDTH]
    pv = p[:, A_WIDTH:]
    u, tu = _gelu(pu)
    vv, tv = _gelu(pv)
    rv = _rms_r(vv)
    vn = (vv * rv * g).astype(BF16)
    return pu, pv, u, tu, vv, tv, rv, vn


def _tril():
    r = lax.broadcasted_iota(jnp.int32, (CHUNK, CHUNK), 0)
    c = lax.broadcasted_iota(jnp.int32, (CHUNK, CHUNK), 1)
    return r >= c


def _sgu_fwd(proj_a, g_sgu, w_s, b_st, tm):
    T = proj_a.shape[0]

    def body(p_ref, g_ref, ws_ref, bs_ref, y_ref):
        tril = _tril()
        _, _, u, _, _, _, _, vn = _sgu_parts(p_ref[...].astype(F32), g_ref[...])
        for gi in range(A_GROUPS):
            wm = jnp.where(tril, ws_ref[gi], 0.0).astype(BF16)
            bcol = bs_ref[:, gi:gi + 1]
            cs = slice(gi * CHUNK, (gi + 1) * CHUNK)
            for c in range(tm // CHUNK):
                rs = slice(c * CHUNK, (c + 1) * CHUNK)
                s = _dot(wm, vn[rs, cs]) + bcol
                y_ref[rs, cs] = (u[rs, cs] * s).astype(BF16)

    return pl.pallas_call(
        body, name="sgu_fwd", grid=(T // tm,),
        in_specs=[_row(tm, A_DIM), _full(g_sgu.shape), _full(w_s.shape), _full(b_st.shape)],
        out_specs=_row(tm, A_WIDTH), out_shape=_sds((T, A_WIDTH), BF16),
        compiler_params=_cp(("arbitrary",)),
    )(*_hbm(proj_a, g_sgu, w_s, b_st))


HEAD_ROWS = N_HEADS * CHUNK


def _head_rows(h):
    return slice(h * CHUNK, (h + 1) * CHUNK)


def _attn_setup(bias_scr, sink_scr, kvar_scr, qkv_ref, bk_ref, rel_ref, sink_ref):
    bk = bk_ref[...]
    for h in range(N_HEADS):
        acc = jnp.full((CHUNK, 2 * CHUNK), NEG_INF, F32)
        for b in range(N_BUCKETS):
            acc = jnp.where(bk == b, rel_ref[b, h], acc)
        bias_scr[_head_rows(h), :] = acc
        sink_scr[_head_rows(h), :] = jnp.full((CHUNK, LANES), sink_ref[0, h], F32)
    seq = qkv_ref.shape[0]
    rows_per = 2 * CHUNK
    for is_v in range(2):
        c0 = Q_DIM + is_v * KV_DIM
        for r in range(seq // rows_per):
            rs = slice(r * rows_per, (r + 1) * rows_per)
            a = qkv_ref[rs, c0:c0 + KV_DIM].astype(F32)
            lane = lax.broadcasted_iota(jnp.int32, a.shape, 1)
            lo = jnp.where(lane < HEAD_DIM, a, 0.0)
            hi = jnp.where(lane >= HEAD_DIM, a, 0.0)
            kvar_scr[4 * is_v + 0, rs, :] = lo.astype(BF16)
            kvar_scr[4 * is_v + 1, rs, :] = pltpu.roll(lo, HEAD_DIM, 1).astype(BF16)
            kvar_scr[4 * is_v + 2, rs, :] = pltpu.roll(hi, HEAD_DIM, 1).astype(BF16)
            kvar_scr[4 * is_v + 3, rs, :] = hi.astype(BF16)


def _rowsum(a, ones):
    hi = a.astype(BF16)
    lo = (a - hi.astype(F32)).astype(BF16)
    return _dot(hi, ones) + _dot(lo, ones)


def _both(a):
    return jnp.concatenate([a, a], axis=1)


def _attn_probs(qkv_ref, r0, n, kv, bias_scr, sink_scr, ones):
    s = jnp.concatenate([_dot_nt(qkv_ref[pl.ds(r0, CHUNK), (h // 2) * LANES:(h // 2 + 1) * LANES], kv[h // 4][h % 2])
                         for h in range(N_HEADS)], axis=0)
    s = s * (HEAD_DIM ** -0.5) + bias_scr[...]
    col = lax.broadcasted_iota(jnp.int32, s.shape, 1)
    s = jnp.where((col < CHUNK) & (n == 0), NEG_INF, s)
    sink = sink_scr[...]
    m = jnp.maximum(jnp.max(s, axis=-1, keepdims=True), sink)
    p = jnp.exp(s - _both(m))
    es = jnp.exp(sink - m)
    inv = 1.0 / (_rowsum(p, ones) + es)
    return p * _both(inv), es * inv


def _attn_block_inputs(kvar_scr, n):
    r0 = pl.multiple_of(n * CHUNK, CHUNK)
    rp = pl.multiple_of(jnp.maximum(n - 1, 0) * CHUNK, CHUNK)

    def both(idx):
        return jnp.concatenate([kvar_scr[idx, pl.ds(rp, CHUNK), :], kvar_scr[idx, pl.ds(r0, CHUNK), :]], axis=0)

    kv = ((both(0), both(1)), (both(2), both(3)))
    vv = ((both(4), both(5)), (both(6), both(7)))
    return r0, kv, vv


def _attn_fwd(proj_b, sinks, rel_bias, n_seq, seq):
    nb = seq // CHUNK
    bk = jnp.asarray(_band_buckets())

    def body(qkv_ref, bk_ref, rel_ref, sink_ref, o_ref, bias_scr, sink_scr, kvar_scr):
        _attn_setup(bias_scr, sink_scr, kvar_scr, qkv_ref, bk_ref, rel_ref, sink_ref)
        ones = jnp.ones((2 * CHUNK, LANES), BF16)

        def blk(n, carry):
            r0, kv, vv = _attn_block_inputs(kvar_scr, n)
            prob, _ = _attn_probs(qkv_ref, r0, n, kv, bias_scr, sink_scr, ones)
            pb = prob.astype(BF16)
            for pr in range(N_HEADS // 2):
                acc = _dot(pb[_head_rows(2 * pr)], vv[pr // 2][0]) + _dot(pb[_head_rows(2 * pr + 1)], vv[pr // 2][1])
                o_ref[pl.ds(r0, CHUNK), pr * LANES:(pr + 1) * LANES] = acc.astype(BF16)
            return carry

        lax.fori_loop(0, nb, blk, 0)

    smem = pl.BlockSpec(memory_space=pltpu.SMEM)
    return pl.pallas_call(
        body, name="attn_fwd", grid=(n_seq,),
        in_specs=[_row(seq, B_DIM), _full(bk.shape), smem, smem],
        out_specs=_row(seq, Q_DIM), out_shape=_sds((n_seq * seq, Q_DIM), BF16),
        scratch_shapes=[pltpu.VMEM((HEAD_ROWS, 2 * CHUNK), F32), pltpu.VMEM((HEAD_ROWS, LANES), F32),
                        pltpu.VMEM((8, seq, KV_DIM), BF16)],
        compiler_params=_cp(("arbitrary",), 40),
    )(*_hbm(proj_b, bk), rel_bias, sinks)


def _dot_stacked(a, w_ref):
    return jnp.concatenate([_dot(a, w_ref[i]) for i in range(N_CHIPS)], axis=1)


def _dot_nt_stacked(a, w_ref):
    w = w_ref.shape[2]
    acc = _dot_nt(a[:, :w], w_ref[0])
    for i in range(1, N_CHIPS):
        acc = acc + _dot_nt(a[:, i * w:(i + 1) * w], w_ref[i])
    return acc


def _merge_fwd(x2, y_a, y_b, proj_g, w_pa, w_pb, w_out, tm):
    T = x2.shape[0]

    def body(x_ref, ya_ref, yb_ref, g_ref, wpa_ref, wpb_ref, wo_ref, x1_ref, mg_ref):
        g = g_ref[...].astype(F32)
        pa = _dot_stacked(ya_ref[...], wpa_ref)
        pb = _dot_stacked(yb_ref[...], wpb_ref)
        merged = (_sigmoid(g[:, :D_MODEL]) * pa + _sigmoid(g[:, D_MODEL:]) * pb).astype(BF16)
        mg_ref[...] = merged
        x1_ref[...] = x_ref[...] + _dot(merged, wo_ref[...])

    return pl.pallas_call(
        body, name="merge_fwd", grid=(T // tm,),
        in_specs=[_row(tm, D_MODEL), _row(tm, A_WIDTH), _row(tm, Q_DIM), _row(tm, G_DIM),
                  _resident(w_pa.shape), _resident(w_pb.shape), _resident(w_out.shape)],
        out_specs=[_row(tm, D_MODEL), _row(tm, D_MODEL)],
        out_shape=[_sds((T, D_MODEL), F32), _sds((T, D_MODEL), BF16)],
        compiler_params=_cp(("arbitrary",), 40),
    )(*_hbm(x2, y_a, y_b, proj_g, w_pa, w_pb, w_out))


def _upproj(x1, g_ffn, w_up, w_conv, b_conv, tm, seq):
    T = x1.shape[0]
    cw = w_up.shape[2]
    tiles_per_seq = seq // tm

    def body(x_ref, g_ref, w_ref, wc_ref, bc_ref, u_ref, h_ref, gate_ref, val_ref, tail_scr):
        at_start = (pl.program_id(0) % tiles_per_seq) == 0
        x = x_ref[...]
        h = (x * _rms_r(x) * g_ref[...]).astype(BF16)
        h_ref[...] = h
        for i in range(N_CHIPS):
            cs = slice(i * cw, (i + 1) * cw)
            u = _dot(h, w_ref[i])
            u_ref[:, cs] = u.astype(BF16)
            hl = jnp.where(at_start, 0.0, tail_scr[SUBLANES - 2:SUBLANES, cs])
            tail_scr[:, cs] = u[tm - SUBLANES:]
            up = _conv_out((u, _shift_down(u, hl, 1), _shift_down(u, hl, 2)), wc_ref[:, cs], bc_ref[:, cs])
            out_ref = gate_ref if i < N_CHIPS // 2 else val_ref
            out_ref[:, (i % 2) * cw:(i % 2 + 1) * cw] = up.astype(BF16)

    return pl.pallas_call(
        body, name="upproj", grid=(T // tm,),
        in_specs=[_row(tm, D_MODEL), _full(g_ffn.shape), _resident(w_up.shape), _full(w_conv.shape), _full(b_conv.shape)],
        out_specs=[_row(tm, 2 * D_FF), _row(tm, D_MODEL), _row(tm, D_FF), _row(tm, D_FF)],
        out_shape=[_sds((T, 2 * D_FF), BF16), _sds((T, D_MODEL), BF16), _sds((T, D_FF), BF16), _sds((T, D_FF), BF16)],
        scratch_shapes=[pltpu.VMEM((SUBLANES, 2 * D_FF), F32)],
        compiler_params=_cp(("arbitrary",), 56),
    )(*_hbm(x1, g_ffn, w_up, w_conv, b_conv))


def _shift_down(u, halo, k):
    rolled = pltpu.roll(u, k, 0)
    head = rolled[:SUBLANES]
    row = lax.broadcasted_iota(jnp.int32, head.shape, 0)
    if k == 1:
        head = jnp.where(row == 0, halo[1:2], head)
    else:
        head = jnp.where(row == 0, halo[0:1], jnp.where(row == 1, halo[1:2], head))
    return jnp.concatenate([head, rolled[SUBLANES:]], axis=0)


def _shift_up(d, halo, k):
    tm = d.shape[0]
    rolled = pltpu.roll(d, tm - k, 0)
    tail = rolled[tm - SUBLANES:]
    row = lax.broadcasted_iota(jnp.int32, tail.shape, 0)
    if k == 1:
        tail = jnp.where(row == SUBLANES - 1, halo[0:1], tail)
    else:
        tail = jnp.where(row == SUBLANES - 2, halo[0:1], jnp.where(row == SUBLANES - 1, halo[1:2], tail))
    return jnp.concatenate([rolled[:tm - SUBLANES], tail], axis=0)


def _conv_out(taps, wc, bc):
    u, u1, u2 = taps
    return wc[0:1] * u2 + wc[1:2] * u1 + wc[2:3] * u + bc


def _ffn_down_loss(gate, val, x1, target, w_down, g_final, tm):
    T = x1.shape[0]
    half = D_FF // 2

    def body(gt_ref, vl_ref, x1_ref, t_ref, wd_ref, g_ref, dx2_ref, loss_ref, gg_ref):
        i = pl.program_id(0)
        acc = jnp.zeros((tm, D_MODEL), F32)
        for j in range(2):
            gc = slice(j * half, (j + 1) * half)
            gate = gt_ref[:, gc].astype(F32)
            act = (gate * _sigmoid(gate) * vl_ref[:, gc].astype(F32)).astype(BF16)
            acc = acc + _dot(act, wd_ref[gc, :])
        x2 = x1_ref[...] + acc
        r = _rms_r(x2)
        n = x2 * r
        g = g_ref[...]
        diff = n * g - t_ref[...]
        dy = diff * (1.0 / D_MODEL)
        dx2_ref[...] = _rms_bwd(dy, n, r, g)

        @pl.when(i == 0)
        def _():
            loss_ref[...] = jnp.zeros_like(loss_ref)
            gg_ref[...] = jnp.zeros_like(gg_ref)

        loss_ref[...] += 0.5 * jnp.sum(jnp.mean(diff * diff, axis=-1, keepdims=True), axis=0, keepdims=True)
        gg_ref[...] += jnp.sum(dy * n, axis=0, keepdims=True)

    return pl.pallas_call(
        body, name="ffn_down_loss", grid=(T // tm,),
        in_specs=[_row(tm, D_FF), _row(tm, D_FF), _row(tm, D_MODEL), _row(tm, D_MODEL),
                  _resident(w_down.shape), _full(g_final.shape)],
        out_specs=[_row(tm, D_MODEL), _full((1, 1)), _full((1, D_MODEL))],
        out_shape=[_sds((T, D_MODEL), F32), _sds((1, 1), F32), _sds((1, D_MODEL), F32)],
        compiler_params=_cp(("arbitrary",), 48),
    )(*_hbm(gate, val, x1, target, w_down, g_final))


def _ffn_bwd_act(gate, val, dx2, w_down, tm):
    T = dx2.shape[0]
    half = D_FF // 2
    nt = T // tm

    def body(g_ref, v_ref, dx_ref, wd_ref, dg_ref, dv_ref, gwd_out, gbg_ref, gbv_ref, gwd_ref):
        i = pl.program_id(1)

        @pl.when(i == 0)
        def _():
            for r in (gwd_ref, gbg_ref, gbv_ref):
                r[...] = jnp.zeros_like(r)

        dx = dx_ref[...].astype(BF16)
        for c0 in range(0, half, COL_CHUNK):
            cs = slice(c0, min(c0 + COL_CHUNK, half))
            gate = g_ref[:, cs].astype(F32)
            val = v_ref[:, cs].astype(F32)
            sg = _sigmoid(gate)
            silu = gate * sg
            d_act = _dot_nt(dx, wd_ref[cs, :])
            d_val = d_act * silu
            d_gate = d_act * val * (sg * (1.0 + gate * (1.0 - sg)))
            dg_ref[:, cs] = d_gate.astype(BF16)
            dv_ref[:, cs] = d_val.astype(BF16)
            gwd_ref[cs, :] += _dot_tn((silu * val).astype(BF16), dx)
            gbg_ref[:, cs] += jnp.sum(d_gate, axis=0, keepdims=True)
            gbv_ref[:, cs] += jnp.sum(d_val, axis=0, keepdims=True)

        @pl.when(i == nt - 1)
        def _():
            gwd_out[...] = gwd_ref[...].astype(BF16)

    tile = pl.BlockSpec((tm, half), lambda j, i: (i, j))
    vec = pl.BlockSpec((1, half), lambda j, i: (0, j))
    wrows = pl.BlockSpec((half, D_MODEL), lambda j, i: (j, 0))
    return pl.pallas_call(
        body, name="ffn_bwd_act", grid=(2, nt),
        in_specs=[tile, tile, pl.BlockSpec((tm, D_MODEL), lambda j, i: (i, 0)), wrows],
        out_specs=[tile, tile, wrows, vec, vec],
        out_shape=[_sds((T, D_FF), BF16), _sds((T, D_FF), BF16), _sds((D_FF, D_MODEL), BF16),
                   _sds((1, D_FF), F32), _sds((1, D_FF), F32)],
        scratch_shapes=[pltpu.VMEM((half, D_MODEL), F32)],
        compiler_params=_cp(("arbitrary", "arbitrary"), 56),
    )(*_hbm(gate, val, dx2, w_down))


def _ffn_bwd_up(d_gate, d_val, upre, dx2, x1, g_ffn, w_conv, w_up, tm, seq):
    T = dx2.shape[0]
    tiles_per_seq = seq // tm
    k16 = tm // BF16_ROWS
    n16 = T // BF16_ROWS
    cw = D_FF // 2

    def body(dg_ref, dv_ref, hg_ref, hv_ref, u_ref, dx2_ref, x1_ref, g_ref, wc_ref, wu_ref, du_ref, dx1_ref, gg_ref, gwc_ref):
        i = pl.program_id(0)
        at_end = (i % tiles_per_seq) == tiles_per_seq - 1

        @pl.when(i == 0)
        def _():
            gg_ref[...] = jnp.zeros_like(gg_ref)
            gwc_ref[...] = jnp.zeros_like(gwc_ref)

        dh = jnp.zeros((tm, D_MODEL), F32)
        for j in range(4):
            src, hsrc = (dg_ref, hg_ref) if j < 2 else (dv_ref, hv_ref)
            ls = slice((j % 2) * cw, (j % 2 + 1) * cw)
            cs = slice(j * cw, (j + 1) * cw)
            d = src[:, ls].astype(F32)
            hl = hsrc[:, ls].astype(F32)[0:2]
            hl = jnp.where(at_end, 0.0, hl)
            wc = wc_ref[:, cs]
            d1 = _shift_up(d, hl, 1)
            d2 = _shift_up(d, hl, 2)
            du = (wc[2:3] * d + wc[1:2] * d1 + wc[0:1] * d2).astype(BF16)
            du_ref[:, cs] = du
            dh = dh + _dot_nt(du, wu_ref[j])
            u = u_ref[:, cs].astype(F32)
            gwc_ref[0:1, cs] += jnp.sum(d2 * u, axis=0, keepdims=True)
            gwc_ref[1:2, cs] += jnp.sum(d1 * u, axis=0, keepdims=True)
            gwc_ref[2:3, cs] += jnp.sum(d * u, axis=0, keepdims=True)
        x = x1_ref[...]
        r = _rms_r(x)
        n = x * r
        dx1_ref[...] = dx2_ref[...] + _rms_bwd(dh, n, r, g_ref[...])
        gg_ref[...] += jnp.sum(dh * n, axis=0, keepdims=True)

    nxt = pl.BlockSpec((BF16_ROWS, D_FF), lambda i: (jnp.minimum((i + 1) * k16, n16 - 1), 0))
    return pl.pallas_call(
        body, name="ffn_bwd_up", grid=(T // tm,),
        in_specs=[_row(tm, D_FF), _row(tm, D_FF), nxt, nxt, _row(tm, 2 * D_FF), _row(tm, D_MODEL), _row(tm, D_MODEL),
                  _full(g_ffn.shape), _full(w_conv.shape), _resident(w_up.shape)],
        out_specs=[_row(tm, 2 * D_FF), _row(tm, D_MODEL), _full((1, D_MODEL)), _full((3, 2 * D_FF))],
        out_shape=[_sds((T, 2 * D_FF), BF16), _sds((T, D_MODEL), F32), _sds((1, D_MODEL), F32), _sds((3, 2 * D_FF), F32)],
        compiler_params=_cp(("arbitrary",), 56),
    )(*_hbm(d_gate, d_val, d_gate, d_val, upre, dx2, x1, g_ffn, w_conv, w_up))


def _matmul_tn(a, b, tn, tk, name):
    T, M = a.shape
    N = b.shape[1]
    nk = T // tk

    def body(a_ref, b_ref, o_ref, acc_ref):
        k = pl.program_id(1)

        @pl.when(k == 0)
        def _():
            acc_ref[...] = jnp.zeros_like(acc_ref)

        acc_ref[...] += _dot_tn(a_ref[...], b_ref[...])

        @pl.when(k == nk - 1)
        def _():
            o_ref[...] = acc_ref[...].astype(BF16)

    return pl.pallas_call(
        body, name=name, grid=(N // tn, nk),
        in_specs=[pl.BlockSpec((tk, M), lambda j, k: (k, 0)), pl.BlockSpec((tk, tn), lambda j, k: (k, j))],
        out_specs=pl.BlockSpec((M, tn), lambda j, k: (0, j)), out_shape=_sds((M, N), BF16),
        scratch_shapes=[pltpu.VMEM((M, tn), F32)],
        compiler_params=_cp(("arbitrary", "arbitrary"), 48),
    )(*_hbm(a, b))


def _merge_bwd(dx1, merged, y_a, y_b, proj_g, w_pa, w_pb, w_out, tm, after=None):
    T = dx1.shape[0]

    nt = T // tm
    pshape = (A_WIDTH, D_MODEL)
    order = [] if after is None else [after]

    def body(*refs):
        dx_ref, mg_ref, ya_ref, yb_ref, g_ref, wpa_ref, wpb_ref, wo_ref = refs[:8]
        dg_ref, dya_ref, dyb_ref, gwo_out, gwpa_out, gwpb_out, gwo_ref, gwpa_ref, gwpb_ref = refs[8 + len(order):]
        i = pl.program_id(0)
        dx = dx_ref[...].astype(BF16)
        dm = _dot_nt(dx, wo_ref[...])
        g = g_ref[...].astype(F32)
        ya = ya_ref[...]
        yb = yb_ref[...]
        pa = _dot_stacked(ya, wpa_ref)
        pb = _dot_stacked(yb, wpb_ref)
        sa = _sigmoid(g[:, :D_MODEL])
        sb = _sigmoid(g[:, D_MODEL:])
        dpa = (dm * sa).astype(BF16)
        dpb = (dm * sb).astype(BF16)
        dg_ref[:, :D_MODEL] = (dm * pa * (sa * (1.0 - sa))).astype(BF16)
        dg_ref[:, D_MODEL:] = (dm * pb * (sb * (1.0 - sb))).astype(BF16)
        dya_ref[...] = _dot_nt_stacked(dpa, wpa_ref).astype(BF16)
        dyb_ref[...] = _dot_nt_stacked(dpb, wpb_ref).astype(BF16)

        @pl.when(i == 0)
        def _():
            for r in (gwo_ref, gwpa_ref, gwpb_ref):
                r[...] = jnp.zeros_like(r)

        gwo_ref[...] += _dot_tn(mg_ref[...], dx)
        gwpa_ref[...] += _dot_tn(ya, dpa)
        gwpb_ref[...] += _dot_tn(yb, dpb)

        @pl.when(i == nt - 1)
        def _():
            gwo_out[...] = gwo_ref[...].astype(BF16)
            gwpa_out[...] = gwpa_ref[...].astype(BF16)
            gwpb_out[...] = gwpb_ref[...].astype(BF16)

    return pl.pallas_call(
        body, name="merge_bwd", grid=(nt,),
        in_specs=[_row(tm, D_MODEL), _row(tm, D_MODEL), _row(tm, A_WIDTH), _row(tm, Q_DIM), _row(tm, G_DIM),
                  _resident(w_pa.shape), _resident(w_pb.shape), _resident(w_out.shape)] + [ANY] * len(order),
        out_specs=[_row(tm, G_DIM), _row(tm, A_WIDTH), _row(tm, Q_DIM),
                   _full(w_out.shape), _full(pshape), _full(pshape)],
        out_shape=[_sds((T, G_DIM), BF16), _sds((T, A_WIDTH), BF16), _sds((T, Q_DIM), BF16),
                   _sds(w_out.shape, BF16), _sds(pshape, BF16), _sds(pshape, BF16)],
        scratch_shapes=[pltpu.VMEM(w_out.shape, F32), pltpu.VMEM(pshape, F32), pltpu.VMEM(pshape, F32)],
        compiler_params=_cp(("arbitrary",), 56),
    )(*_hbm(dx1, merged, y_a, y_b, proj_g, w_pa, w_pb, w_out), *order)


def _sgu_bwd(proj_a, d_ya, g_sgu, w_s, b_st, tm, after=None):
    T = proj_a.shape[0]
    order = [] if after is None else [after]

    def body(*refs):
        p_ref, dy_ref, g_ref, ws_ref, bs_ref = refs[:5]
        dp_ref, gws_ref, gbs_ref, gg_ref = refs[5 + len(order):]
        tril = _tril()
        g = g_ref[...]
        pu, pv, u, tu, vv, tv, rv, vn = _sgu_parts(p_ref[...].astype(F32), g)
        dy = dy_ref[...].astype(F32)

        @pl.when(pl.program_id(0) == 0)
        def _():
            for r in (gws_ref, gbs_ref, gg_ref):
                r[...] = jnp.zeros_like(r)

        du_cols = []
        dvn_cols = []
        for gi in range(A_GROUPS):
            wm = jnp.where(tril, ws_ref[gi], 0.0).astype(BF16)
            wmt = wm.astype(F32).T.astype(BF16)
            bcol = bs_ref[:, gi:gi + 1]
            cs = slice(gi * CHUNK, (gi + 1) * CHUNK)
            du_rows = []
            dvn_rows = []
            gw = jnp.zeros((CHUNK, CHUNK), F32)
            gb = jnp.zeros((CHUNK, 1), F32)
            for c in range(tm // CHUNK):
                rs = slice(c * CHUNK, (c + 1) * CHUNK)
                vn_c = vn[rs, cs]
                s = _dot(wm, vn_c) + bcol
                dy_c = dy[rs, cs]
                ds = dy_c * u[rs, cs]
                du_rows.append(dy_c * s)
                dsb = ds.astype(BF16)
                gw = gw + _dot_nt(dsb, vn_c)
                gb = gb + jnp.sum(ds, axis=-1, keepdims=True)
                dvn_rows.append(_dot(wmt, dsb))
            gws_ref[gi] += jnp.where(tril, gw, 0.0)
            gbs_ref[:, gi:gi + 1] += gb
            du_cols.append(jnp.concatenate(du_rows, axis=0))
            dvn_cols.append(jnp.concatenate(dvn_rows, axis=0))
        du = jnp.concatenate(du_cols, axis=1)
        dvn = jnp.concatenate(dvn_cols, axis=1)
        vhat = vv * rv
        gg_ref[...] += jnp.sum(dvn * vhat, axis=0, keepdims=True)
        dvv = _rms_bwd(dvn, vhat, rv, g)
        dp_ref[:, :A_WIDTH] = (du * _gelu_grad(pu, tu)).astype(BF16)
        dp_ref[:, A_WIDTH:] = (dvv * _gelu_grad(pv, tv)).astype(BF16)

    return pl.pallas_call(
        body, name="sgu_bwd", grid=(T // tm,),
        in_specs=[_row(tm, A_DIM), _row(tm, A_WIDTH), _full(g_sgu.shape), _full(w_s.shape), _full(b_st.shape)] + [ANY] * len(order),
        out_specs=[_row(tm, A_DIM), _full(w_s.shape), _full(b_st.shape), _full(g_sgu.shape)],
        out_shape=[_sds((T, A_DIM), BF16), _sds(w_s.shape, F32), _sds(b_st.shape, F32), _sds(g_sgu.shape, F32)],
        compiler_params=_cp(("arbitrary",)),
    )(*_hbm(proj_a, d_ya, g_sgu, w_s, b_st), *order)


def _attn_bwd(proj_b, d_yb, sinks, rel_bias, n_seq, seq):
    nb = seq // CHUNK
    bk = jnp.asarray(_band_buckets())

    def body(qkv_ref, do_ref, bk_ref, rel_ref, sink_ref, d_ref, gs_ref, gr_ref,
             bias_scr, sink_scr, kvar_scr, dbias_scr, dk_scr, dv_scr, ds_scr):
        b = pl.program_id(0)
        _attn_setup(bias_scr, sink_scr, kvar_scr, qkv_ref, bk_ref, rel_ref, sink_ref)
        ones = jnp.ones((2 * CHUNK, LANES), BF16)

        @pl.when(b == 0)
        def _():
            dbias_scr[...] = jnp.zeros_like(dbias_scr)
            ds_scr[...] = jnp.zeros_like(ds_scr)

        dk_scr[...] = jnp.zeros_like(dk_scr)
        dv_scr[...] = jnp.zeros_like(dv_scr)

        def transposed(a):
            return a.astype(F32).T.astype(BF16)

        def blk(n, carry):
            r0, kv, vv = _attn_block_inputs(kvar_scr, n)
            prob, psink = _attn_probs(qkv_ref, r0, n, kv, bias_scr, sink_scr, ones)
            dp = jnp.concatenate([_dot_nt(do_ref[pl.ds(r0, CHUNK), (h // 2) * LANES:(h // 2 + 1) * LANES], vv[h // 4][h % 2])
                                  for h in range(N_HEADS)], axis=0)
            delta = _rowsum(prob * dp, ones)
            dsc = prob * (dp - _both(delta))
            ds_scr[...] += psink * delta
            dbias_scr[...] += dsc
            dsb = (dsc * (HEAD_DIM ** -0.5)).astype(BF16)
            pb = prob.astype(BF16)
            dkt = [jnp.zeros((HEAD_DIM, 2 * CHUNK), F32) for _ in range(2)]
            dvt = [jnp.zeros((HEAD_DIM, 2 * CHUNK), F32) for _ in range(2)]
            for pr in range(N_HEADS // 2):
                ps = slice(pr * LANES, (pr + 1) * LANES)
                qpt = transposed(qkv_ref[pl.ds(r0, CHUNK), ps])
                dopt = transposed(do_ref[pl.ds(r0, CHUNK), ps])
                kvh = pr // 2
                dq = jnp.zeros((CHUNK, LANES), F32)
                for hh in range(2):
                    hr = _head_rows(2 * pr + hh)
                    rows = slice(hh * HEAD_DIM, (hh + 1) * HEAD_DIM)
                    dq = dq + _dot(dsb[hr], kv[kvh][hh])
                    dkt[kvh] = dkt[kvh] + _dot(qpt, dsb[hr])[rows]
                    dvt[kvh] = dvt[kvh] + _dot(dopt, pb[hr])[rows]
                d_ref[pl.ds(r0, CHUNK), ps] = dq.astype(BF16)
            dk_scr[:, pl.ds(r0, 2 * CHUNK)] += jnp.concatenate(dkt, axis=0)
            dv_scr[:, pl.ds(r0, 2 * CHUNK)] += jnp.concatenate(dvt, axis=0)
            return carry

        lax.fori_loop(0, nb, blk, 0)
        for n in range(nb):
            rows = slice(n * CHUNK, (n + 1) * CHUNK)
            cols = slice((n + 1) * CHUNK, (n + 2) * CHUNK)
            d_ref[rows, Q_DIM:Q_DIM + KV_DIM] = dk_scr[:, cols].T.astype(BF16)
            d_ref[rows, Q_DIM + KV_DIM:] = dv_scr[:, cols].T.astype(BF16)

        @pl.when(b == n_seq - 1)
        def _():
            bkv = bk_ref[...]
            for h in range(N_HEADS):
                gs_ref[0:1, h:h + 1] = -jnp.sum(ds_scr[_head_rows(h), 0:1], axis=0, keepdims=True)
                db = dbias_scr[_head_rows(h), :]
                for bb in range(N_BUCKETS):
                    part = jnp.sum(jnp.where(bkv == bb, db, 0.0), axis=-1, keepdims=True)
                    gr_ref[bb:bb + 1, h:h + 1] = jnp.sum(part, axis=0, keepdims=True)

    smem = pl.BlockSpec(memory_space=pltpu.SMEM)
    return pl.pallas_call(
        body, name="attn_bwd", grid=(n_seq,),
        in_specs=[_row(seq, B_DIM), _row(seq, Q_DIM), _full(bk.shape), smem, smem],
        out_specs=[_row(seq, B_DIM), _full((1, N_HEADS)), _full((N_BUCKETS, N_HEADS))],
        out_shape=[_sds((n_seq * seq, B_DIM), BF16), _sds((1, N_HEADS), F32), _sds((N_BUCKETS, N_HEADS), F32)],
        scratch_shapes=[pltpu.VMEM((HEAD_ROWS, 2 * CHUNK), F32), pltpu.VMEM((HEAD_ROWS, LANES), F32),
                        pltpu.VMEM((8, seq, KV_DIM), BF16), pltpu.VMEM((HEAD_ROWS, 2 * CHUNK), F32),
                        pltpu.VMEM((KV_DIM, seq + CHUNK), F32), pltpu.VMEM((KV_DIM, seq + CHUNK), F32),
                        pltpu.VMEM((HEAD_ROWS, LANES), F32)],
        compiler_params=_cp(("arbitrary",), 40),
    )(*_hbm(proj_b, d_yb, bk), rel_bias, sinks)


def _inproj_bwd(d_g, d_a, d_b, x2, dx1, g_mix, w_in, tm, after=None):
    T = x2.shape[0]
    order = [] if after is None else [after]

    def body(*refs):
        dg_ref, da_ref, db_ref, x_ref, dx1_ref, g_ref, w_ref = refs[:7]
        gx_ref, gg_ref = refs[7 + len(order):]
        dh = (_dot_nt(dg_ref[...], w_ref[:, _G_COLS]) + _dot_nt(da_ref[...], w_ref[:, _A_COLS])
              + _dot_nt(db_ref[...], w_ref[:, _B_COLS]))
        x = x_ref[...]
        r = _rms_r(x)
        n = x * r
        gx_ref[...] = dx1_ref[...] + _rms_bwd(dh, n, r, g_ref[...])

        @pl.when(pl.program_id(0) == 0)
        def _():
            gg_ref[...] = jnp.zeros_like(gg_ref)

        gg_ref[...] += jnp.sum(dh * n, axis=0, keepdims=True)

    return pl.pallas_call(
        body, name="inproj_bwd", grid=(T // tm,),
        in_specs=[_row(tm, G_DIM), _row(tm, A_DIM), _row(tm, B_DIM), _row(tm, D_MODEL), _row(tm, D_MODEL),
                  _full(g_mix.shape), _resident(w_in.shape)] + [ANY] * len(order),
        out_specs=[_row(tm, D_MODEL), _full((1, D_MODEL))],
        out_shape=[_sds((T, D_MODEL), F32), _sds((1, D_MODEL), F32)],
        compiler_params=_cp(("arbitrary",), 48),
    )(*_hbm(d_g, d_a, d_b, x2, dx1, g_mix, w_in), *order)


IN_SHARD = (A_DIM + B_DIM + G_DIM) // N_CHIPS


def _unstack_w_in(stack):
    tr = 256

    def body(s_ref, o_ref):
        for i in range(N_CHIPS):
            o_ref[:, i * IN_SHARD:(i + 1) * IN_SHARD] = s_ref[i]

    return pl.pallas_call(
        body, name="unstack_w_in", grid=(D_MODEL // tr,),
        in_specs=[pl.BlockSpec((N_CHIPS, tr, IN_SHARD), lambda r: (0, r, 0))],
        out_specs=pl.BlockSpec((tr, N_CHIPS * IN_SHARD), lambda r: (r, 0)),
        out_shape=_sds((D_MODEL, N_CHIPS * IN_SHARD), stack.dtype),
        compiler_params=_cp(("arbitrary",)),
    )(*_hbm(stack))


def _stack_grad_w_in(gw_a, gw_b, gw_g):
    tr = 256

    def body(a_ref, b_ref, g_ref, o_ref):
        full = jnp.concatenate([a_ref[...], b_ref[...], g_ref[...]], axis=1)
        for i in range(N_CHIPS):
            o_ref[i] = full[:, i * IN_SHARD:(i + 1) * IN_SHARD]

    return pl.pallas_call(
        body, name="stack_grad_w_in", grid=(D_MODEL // tr,),
        in_specs=[_row(tr, A_DIM), _row(tr, B_DIM), _row(tr, G_DIM)],
        out_specs=pl.BlockSpec((N_CHIPS, tr, IN_SHARD), lambda r: (0, r, 0)),
        out_shape=_sds((N_CHIPS, D_MODEL, IN_SHARD), gw_a.dtype),
        compiler_params=_cp(("arbitrary",)),
    )(*_hbm(gw_a, gw_b, gw_g))


def _local_step(x, target, g_mix, g_sgu, w_s, b_s, sinks, rel_bias, g_ffn, b_conv, g_final,
                w_in, w_conv, proj_weights, ffn_weights, on_grads, after=None):
    n_seq, seq, _ = x.shape
    T = n_seq * seq
    tm = min(ROW_TILE, seq)
    tw = min(GRAD_ROW_TILE, T)
    tf = min(WIDE_ROW_TILE, seq)
    x2 = x.reshape(T, D_MODEL)
    tgt = target.reshape(T, D_MODEL)
    b_st = b_s.T
    g_fin = g_final.reshape(1, D_MODEL)

    proj_g, proj_a, proj_b, h = _inproj(x2, g_mix, w_in, tm, after)
    y_a = _sgu_fwd(proj_a, g_sgu, w_s, b_st, tm)
    y_b = _attn_fwd(proj_b, sinks, rel_bias, n_seq, seq)
    w_pa, w_pb, w_out = proj_weights(y_b)
    x1, merged = _merge_fwd(x2, y_a, y_b, proj_g, w_pa, w_pb, w_out, tm)
    w_up, w_down = ffn_weights(x1)
    upre, h2, gate, val = _upproj(x1, g_ffn, w_up, w_conv, b_conv, tf, seq)
    dx2, loss, gg_final = _ffn_down_loss(gate, val, x1, tgt, w_down, g_fin, tm)

    d_gate, d_val, gw_down, gb_g, gb_v = _ffn_bwd_act(gate, val, dx2, w_down, tw)
    gb_conv = jnp.concatenate([gb_g, gb_v], axis=1)
    d_upre, dx1, gg_ffn, gw_conv = _ffn_bwd_up(d_gate, d_val, upre, dx2, x1, g_ffn, w_conv, w_up, tf, seq)
    gw_up = _matmul_tn(h2, d_upre, 2 * D_FF // 4, min(2 * GRAD_ROW_TILE, T), "grad_w_up")
    sent = on_grads("ffn", dict(w_up=gw_up, w_down=gw_down))
    d_g, d_ya, d_yb, gw_out, gw_pa, gw_pb = _merge_bwd(dx1, merged, y_a, y_b, proj_g, w_pa, w_pb, w_out, tf, sent)
    sent = on_grads("proj", dict(w_pa=gw_pa, w_pb=gw_pb, w_out=gw_out))
    d_a, gw_s, gb_st, gg_sgu = _sgu_bwd(proj_a, d_ya, g_sgu, w_s, b_st, tm, sent)
    d_b, g_sinks, g_rel = _attn_bwd(proj_b, _tie(d_yb, d_a), sinks, rel_bias, n_seq, seq)
    gw_g = _matmul_tn(h, _tie(d_g, d_b), D_MODEL, min(2 * GRAD_ROW_TILE, T), "grad_w_in_gate")
    gw_a = _matmul_tn(h, _tie(d_a, gw_g), A_DIM, min(2 * GRAD_ROW_TILE, T), "grad_w_in_a")
    gw_b = _matmul_tn(h, _tie(d_b, gw_a), B_DIM, min(2 * GRAD_ROW_TILE, T), "grad_w_in_b")
    gw_in = _stack_grad_w_in(gw_a, gw_b, gw_g)
    sent = on_grads("in", dict(w_in=gw_in))
    grad_x, gg_mix = _inproj_bwd(d_g, d_a, d_b, x2, dx1, g_mix, w_in, tm, sent)

    small = dict(g_mix=gg_mix, g_sgu=gg_sgu, w_s=gw_s, b_s=gb_st.T, sinks=g_sinks, rel_bias=g_rel,
                 g_ffn=gg_ffn, b_conv=gb_conv, g_final=gg_final, w_conv=gw_conv)
    big = dict(w_in=gw_in, w_pa=gw_pa, w_pb=gw_pb, w_out=gw_out, w_up=gw_up, w_down=gw_down)
    return loss, grad_x.reshape(x.shape), small, big


_MIXER = ("w_in", "w_pa", "w_pb", "w_out")
_FFN = ("w_up", "w_down")
_BIG = _MIXER + _FFN

_SMALL = (("loss", (1, 1)), ("g_final", (1, D_MODEL)), ("g_mix", (1, D_MODEL)), ("g_ffn", (1, D_MODEL)),
          ("g_sgu", (1, A_WIDTH)), ("b_s", (A_GROUPS, CHUNK)), ("sinks", (1, N_HEADS)), ("rel_bias", (N_BUCKETS, N_HEADS)),
          ("b_conv", (1, 2 * D_FF)), ("w_conv", (3, 2 * D_FF)), ("w_s", (A_GROUPS, CHUNK, CHUNK)))
SMALL_ROWS = 96


def _pack_small(vals):
    flat = jnp.concatenate([vals[n].astype(F32).reshape(-1) for n, _ in _SMALL])
    flat = jnp.pad(flat, (0, SMALL_ROWS * D_MODEL - flat.shape[0]))
    return flat.reshape(SMALL_ROWS, D_MODEL)


def _unpack_small(buf):
    flat = buf.reshape(-1)
    out = {}
    off = 0
    for n, shp in _SMALL:
        k = int(np.prod(shp))
        out[n] = flat[off:off + k].reshape(shp)
        off += k
    return out


def _mesh_pos():
    return lax.axis_index("x"), lax.axis_index("y"), lax.axis_index("c")


def _other_chips(x, y):
    return [(1 - x, y), (x, 1 - y), (1 - x, 1 - y)]


def _remote(src, dst, send_sem, recv_sem, to):
    return pltpu.make_async_remote_copy(src_ref=src, dst_ref=dst, send_sem=send_sem, recv_sem=recv_sem,
                                        device_id=to, device_id_type=MESH)


def _own_slot(own, n, at):
    return lax.dynamic_update_slice(lax.empty((n,) + own.shape, own.dtype), own[None], (at,) + (0,) * own.ndim)


def _allgather_weights(stacks, wc_stack):
    names = list(stacks)
    n = len(names)

    def body(*refs):
        ins, outs = refs[:n + 1], refs[n + 1:2 * n + 2]
        send_sems, recv_sems = refs[2 * n + 2:]
        x, y, c = _mesh_pos()
        me = 2 * x + y
        sibling = (x, y, 1 - c)
        chips = _other_chips(x, y)

        def half(ref, chip, hc):
            hr = ref.shape[1] // 2
            return ref.at[chip, pl.ds(hc * hr, hr), :]

        first = []
        for k in range(n):
            first += [_remote(half(ins[k], me, c), half(outs[k], me, c), send_sems.at[6 * k + j], recv_sems.at[6 * k + j], (cx, cy, c))
                      for j, (cx, cy) in enumerate(chips)]
        first += [_remote(ins[n].at[me], outs[n].at[me], send_sems.at[6 * n + j], recv_sems.at[6 * n + j], (cx, cy, c))
                  for j, (cx, cy) in enumerate(chips)]
        for cp in first:
            cp.start()
        passed = []
        for k in range(n):
            for j, (cx, cy) in enumerate(chips):
                landed = half(outs[k], 2 * cx + cy, c)
                _remote(landed, landed, send_sems.at[6 * k + j], recv_sems.at[6 * k + j], (x, y, c)).wait_recv()
                passed.append(_remote(landed, landed, send_sems.at[6 * k + 3 + j], recv_sems.at[6 * k + 3 + j], sibling))
                passed[-1].start()
        for k in range(n):
            for j, (cx, cy) in enumerate(chips):
                theirs = half(outs[k], 2 * cx + cy, 1 - c)
                _remote(theirs, theirs, send_sems.at[6 * k + 3 + j], recv_sems.at[6 * k + 3 + j], (x, y, c)).wait_recv()
        for j, (cx, cy) in enumerate(chips):
            slot = outs[n].at[2 * cx + cy]
            _remote(slot, slot, send_sems.at[6 * n + j], recv_sems.at[6 * n + j], (x, y, c)).wait_recv()
        for cp in first + passed:
            cp.wait_send()

    arrays = [stacks[k] for k in names] + [wc_stack]
    outs = pl.pallas_call(
        body, name="allgather_weights",
        in_specs=[HBM] * (n + 1), out_specs=[HBM] * (n + 1), input_output_aliases={k: k for k in range(n + 1)},
        out_shape=[_sds(a.shape, a.dtype) for a in arrays],
        scratch_shapes=[pltpu.SemaphoreType.DMA((6 * n + 3,)), pltpu.SemaphoreType.DMA((6 * n + 3,))],
    )(*arrays)
    return dict(zip(names, outs[:n])), outs[n]


_KIND = {"w_in": "stack", "w_pa": "col", "w_pb": "col", "w_up": "col", "w_out": "row", "w_down": "row"}


def _half_view(ref, kind, h):
    if kind == "stack":
        k = ref.shape[1] // 2
        return ref.at[:, pl.ds(h * k, k), :]
    if kind == "col":
        k = ref.shape[0] // 2
        return ref.at[pl.ds(h * k, k), :]
    k = ref.shape[1] // 2
    return ref.at[:, pl.ds(h * k, k)]


def _shard_view(ref, kind, i):
    if kind == "stack":
        return ref.at[i]
    if kind == "col":
        k = ref.shape[1] // N_CHIPS
        return ref.at[:, pl.ds(i * k, k)]
    k = ref.shape[0] // N_CHIPS
    return ref.at[pl.ds(i * k, k), :]


def _region_view(ref, kind, h):
    if kind == "row":
        k = ref.shape[1] // 2
        return ref.at[:, pl.ds(h * k, k)]
    k = ref.shape[0] // 2
    return ref.at[pl.ds(h * k, k), :]


def _half_shape(shape, kind):
    if kind == "stack":
        return (shape[0], shape[1] // 2, shape[2])
    return (shape[0] // 2, shape[1]) if kind == "col" else (shape[0], shape[1] // 2)


def _part_shape(half_shape, kind):
    if kind == "stack":
        return tuple(half_shape[1:])
    k, w = half_shape
    return (k, w // N_CHIPS) if kind == "col" else (k // N_CHIPS, w)


_DATAFLOW = pltpu.SideEffectType.DATAFLOW_SIDE_EFFECTING
_TOKEN = (SUBLANES, LANES)


def _split_start(name, arrays, n_sems, issue, after=None):
    n = len(arrays)
    order = [] if after is None else [after]

    def body(*refs):
        base = n + len(order)
        issue(refs[:n], refs[base], refs[base + 1])
        refs[-1][...] = jnp.zeros(_TOKEN, F32)

    outs = pl.pallas_call(
        body, name=name,
        in_specs=[HBM] * n + [ANY] * len(order), out_specs=[SEM, SEM] + [HBM] * n + [pl.BlockSpec(memory_space=pltpu.VMEM)],
        out_shape=[pltpu.SemaphoreType.DMA((n_sems,)), pltpu.SemaphoreType.DMA((n_sems,))]
        + [pltpu.HBM(a.shape, a.dtype) for a in arrays] + [_sds(_TOKEN, F32)],
        input_output_aliases={k: 2 + k for k in range(n)},
        compiler_params=pltpu.CompilerParams(has_side_effects=_DATAFLOW),
    )(*[pltpu.with_memory_space_constraint(a, pltpu.HBM) for a in arrays], *order)
    return outs[0], outs[1], list(outs[2:2 + n]), outs[-1]


def _split_wait(name, started, waits, after):
    send_sems, recv_sems, arrays, _ = started
    n = len(arrays)

    def body(*refs):
        waits(refs[:n], refs[n], refs[n + 1])

    return pl.pallas_call(
        body, name=name,
        in_specs=[HBM] * n + [SEM, SEM, ANY], out_specs=[HBM] * n,
        out_shape=[pltpu.HBM(a.shape, a.dtype) for a in arrays],
        input_output_aliases={k: k for k in range(n)},
        compiler_params=pltpu.CompilerParams(has_side_effects=_DATAFLOW),
    )(*arrays, send_sems, recv_sems, after)


def _wait_both(src, dst, send_sem, recv_sem):
    x, y, c = _mesh_pos()
    cp = _remote(src, dst, send_sem, recv_sem, (x, y, c))
    cp.wait_send()
    cp.wait_recv()


def _pair_exchange_start(parts, tag, after):
    names = list(parts)
    n = len(names)
    lands = [lax.empty(_half_shape(parts[k].shape, _KIND[k]), parts[k].dtype) for k in names]

    def issue(refs, send_sems, recv_sems):
        x, y, c = _mesh_pos()
        for hc in range(2):
            @pl.when(c == hc)
            def _():
                for k in range(n):
                    _remote(_half_view(refs[k], _KIND[names[k]], 1 - hc), refs[n + k], send_sems.at[k], recv_sems.at[k],
                            (x, y, 1 - c)).start()

    return names, _split_start("grad_pair_exchange_start_" + tag, [parts[k] for k in names] + lands, n, issue, after)


def _pair_exchange_wait(pending, tag, after):
    names, started = pending
    n = len(names)

    def waits(refs, send_sems, recv_sems):
        for k in range(n):
            _wait_both(_half_view(refs[k], _KIND[names[k]], 0), refs[n + k], send_sems.at[k], recv_sems.at[k])

    outs = _split_wait("grad_pair_exchange_wait_" + tag, started, waits, after)
    return dict(zip(names, outs[:n])), dict(zip(names, outs[n:]))


def _half_blocks(shape, kind):
    if kind == "stack":
        _, k, w = shape
        tr = k // 2
        nb = 1
        return (N_CHIPS, nb), (1, tr, w), (lambda i, r, s: (i, r, 0)), (lambda i, r, s: (i, s[1] * nb + r, 0))
    k, w = shape
    if kind == "col":
        tr = 256
        nb = k // 2 // tr
        return (nb,), (tr, w), (lambda r, s: (r, 0)), (lambda r, s: (s[1] * nb + r, 0))
    tr = k // N_CHIPS
    return (N_CHIPS,), (tr, w // 2), (lambda r, s: (r, 0)), (lambda r, s: (r, s[1]))


def _pair_add(part, from_sibling, name, pos):
    kind = _KIND[name]
    grid, block, half_map, full_map = _half_blocks(part.shape, kind)

    def body(s_ref, p_ref, q_ref, o_ref):
        o_ref[...] = (p_ref[...].astype(F32) + q_ref[...].astype(F32)).astype(BF16)

    return pl.pallas_call(
        body, name="grad_pair_add_" + name,
        grid_spec=pltpu.PrefetchScalarGridSpec(
            num_scalar_prefetch=1, grid=grid,
            in_specs=[pl.BlockSpec(block, full_map), pl.BlockSpec(block, half_map)],
            out_specs=pl.BlockSpec(block, half_map)),
        out_shape=_sds(from_sibling.shape, BF16),
        compiler_params=_cp(("arbitrary",) * len(grid), 40),
    )(pos, *_hbm(part, from_sibling))


def _chip_exchange_start(sums, tag, after):
    names = list(sums)
    n = len(names)
    lands = [lax.empty((3,) + _part_shape(sums[k].shape, _KIND[k]), sums[k].dtype) for k in names]

    def issue(refs, send_sems, recv_sems):
        x, y, c = _mesh_pos()
        me = 2 * x + y
        for i in range(N_CHIPS):
            xi, yi = i // 2, i % 2
            j = jnp.where(xi != x, jnp.where(yi != y, 2, 0), 1)

            @pl.when(i != me)
            def _():
                for k in range(n):
                    _remote(_shard_view(refs[k], _KIND[names[k]], i), refs[n + k].at[j], send_sems.at[3 * k + j],
                            recv_sems.at[3 * k + j], (xi, yi, c)).start()

    return names, _split_start("grad_chip_exchange_start_" + tag, [sums[k] for k in names] + lands, 3 * n, issue, after)


def _chip_exchange_wait(pending, tag, after):
    names, started = pending
    n = len(names)

    def waits(refs, send_sems, recv_sems):
        for k in range(n):
            for j in range(3):
                _wait_both(_shard_view(refs[k], _KIND[names[k]], 0), refs[n + k].at[j], send_sems.at[3 * k + j], recv_sems.at[3 * k + j])

    return dict(zip(names, _split_wait("grad_chip_exchange_wait_" + tag, started, waits, after)[n:]))


def _allgather_start(stacks, tag, after):
    names = list(stacks)

    def issue(refs, send_sems, recv_sems):
        x, y, c = _mesh_pos()
        me = 2 * x + y
        for k, st in enumerate(refs):
            hr = st.shape[1] // 2
            mine = st.at[me, pl.ds(c * hr, hr), :]
            for j, (cx, cy) in enumerate(_other_chips(x, y)):
                _remote(mine, mine, send_sems.at[3 * k + j], recv_sems.at[3 * k + j], (cx, cy, c)).start()

    return names, _split_start("allgather_start_" + tag, [stacks[k] for k in names], 3 * len(names), issue, after)


def _allgather_wait(pending, tag, after):
    names, started = pending

    def waits(refs, send_sems, recv_sems):
        for k, st in enumerate(refs):
            slot = st.at[0, pl.ds(0, st.shape[1] // 2), :]
            for j in range(3):
                _wait_both(slot, slot, send_sems.at[3 * k + j], recv_sems.at[3 * k + j])

    return dict(zip(names, _split_wait("allgather_wait_" + tag, started, waits, after)))


def _allgather_forward(stacks, tag):
    names = list(stacks)
    n = len(names)

    def body(*refs):
        ins, outs = refs[:n], refs[n:2 * n]
        send_sems, recv_sems = refs[2 * n:]
        x, y, c = _mesh_pos()
        copies = []
        for k in range(n):
            hr = ins[k].shape[1] // 2
            for j, (cx, cy) in enumerate(_other_chips(x, y)):
                chip = 2 * cx + cy
                copies.append(_remote(ins[k].at[chip, pl.ds(c * hr, hr), :], outs[k].at[chip, pl.ds(c * hr, hr), :],
                                      send_sems.at[3 * k + j], recv_sems.at[3 * k + j], (x, y, 1 - c)))
        for cp in copies:
            cp.start()
        for cp in copies:
            cp.wait()

    arrays = [stacks[k] for k in names]
    outs = pl.pallas_call(
        body, name="allgather_forward_" + tag, in_specs=[HBM] * n, out_specs=[HBM] * n,
        input_output_aliases={k: k for k in range(n)},
        out_shape=[_sds(a.shape, a.dtype) for a in arrays],
        scratch_shapes=[pltpu.SemaphoreType.DMA((3 * n,)), pltpu.SemaphoreType.DMA((3 * n,))],
    )(*arrays)
    return dict(zip(names, outs))


def _owner_sum(part, from_sibling, from_chips, name, pos, shard_shape):
    kind = _KIND[name]
    _, pk, pw = from_chips.shape
    if kind == "row":
        tr, nb = pk, 1
        p_spec = pl.BlockSpec((tr, pw), lambda r, s: (s[0], s[1]))
        q_spec = pl.BlockSpec((tr, pw), lambda r, s: (s[0], 0))
        o_spec = pl.BlockSpec((tr, pw), lambda r, s: (0, s[1]))
    else:
        tr = 256
        nb = pk // tr
        if kind == "stack":
            p_spec = pl.BlockSpec((None, tr, pw), lambda r, s: (s[0], s[1] * nb + r, 0))
            q_spec = pl.BlockSpec((None, tr, pw), lambda r, s: (s[0], r, 0))
        else:
            p_spec = pl.BlockSpec((tr, pw), lambda r, s: (s[1] * nb + r, s[0]))
            q_spec = pl.BlockSpec((tr, pw), lambda r, s: (r, s[0]))
        o_spec = pl.BlockSpec((tr, pw), lambda r, s: (s[1] * nb + r, 0))

    def body(s_ref, p_ref, q_ref, r_ref, o_ref):
        acc = p_ref[...].astype(F32) + q_ref[...].astype(F32)
        for j in range(3):
            acc = acc + r_ref[j].astype(F32)
        o_ref[...] = acc

    return pl.pallas_call(
        body, name="grad_owner_sum_" + name,
        grid_spec=pltpu.PrefetchScalarGridSpec(
            num_scalar_prefetch=1, grid=(nb,),
            in_specs=[p_spec, q_spec, pl.BlockSpec((3, tr, pw), lambda r, s: (0, r, 0))],
            out_specs=o_spec),
        out_shape=_sds(shard_shape, F32),
        compiler_params=_cp(("arbitrary",), 32),
    )(pos, *_hbm(part, from_sibling, from_chips))


def _pair_share_start(shards, tag, after):
    names = list(shards)

    def issue(refs, send_sems, recv_sems):
        x, y, c = _mesh_pos()
        for hc in range(2):
            @pl.when(c == hc)
            def _():
                for k, g in enumerate(refs):
                    mine = _region_view(g, _KIND[names[k]], hc)
                    _remote(mine, mine, send_sems.at[k], recv_sems.at[k], (x, y, 1 - c)).start()

    return names, _split_start("grad_pair_share_start_" + tag, [shards[k] for k in names], len(names), issue, after)


def _pair_share_wait(pending, tag, after):
    names, started = pending

    def waits(refs, send_sems, recv_sems):
        for k, g in enumerate(refs):
            region = _region_view(g, _KIND[names[k]], 0)
            _wait_both(region, region, send_sems.at[k], recv_sems.at[k])

    return dict(zip(names, _split_wait("grad_pair_share_wait_" + tag, started, waits, after)))


def _small_exchange_start(slots, after):
    def issue(refs, send_sems, recv_sems):
        x, y, c = _mesh_pos()
        mine = refs[0].at[4 * x + 2 * y + c]
        k = 0
        for px in range(2):
            for py in range(2):
                for pc in range(2):
                    if px + py + pc:
                        peer = (1 - x if px else x, 1 - y if py else y, 1 - c if pc else c)
                        _remote(mine, mine, send_sems.at[k], recv_sems.at[k], peer).start()
                        k += 1

    return _split_start("small_exchange_start", [slots], N_DEV - 1, issue, after)


def _small_exchange_wait(started, after):
    def waits(refs, send_sems, recv_sems):
        slot = refs[0].at[0]
        for k in range(N_DEV - 1):
            _wait_both(slot, slot, send_sems.at[k], recv_sems.at[k])

    return _split_wait("small_exchange_wait", started, waits, after)[0]


def _adam_math(w, g, m, v):
    m = ADAM_B1 * m + (1.0 - ADAM_B1) * g
    v = ADAM_B2 * v + (1.0 - ADAM_B2) * (g * g)
    m_hat = m / (1.0 - ADAM_B1 ** ADAM_STEP)
    v_hat = v / (1.0 - ADAM_B2 ** ADAM_STEP)
    delta = -ADAM_LR * (m_hat / (jnp.sqrt(v_hat) + ADAM_EPS) + ADAM_WD * w)
    return delta, m, v


def _adamw(w, g, m, v, name):
    rows, cols = w.shape
    fits = [t for t in range(SUBLANES, rows, SUBLANES) if rows % t == 0 and t * cols * 4 <= (3 << 19)]
    tr = max(fits) if fits else rows

    def body(w_ref, g_ref, m_ref, v_ref, d_ref, nm_ref, nv_ref, go_ref):
        g = g_ref[...]
        d, nm, nv = _adam_math(w_ref[...], g, m_ref[...], v_ref[...])
        d_ref[...] = d
        nm_ref[...] = nm
        nv_ref[...] = nv
        go_ref[...] = g

    spec = pl.BlockSpec((tr, cols), lambda i: (i, 0))
    return pl.pallas_call(
        body, name=name, grid=(rows // tr,), in_specs=[spec] * 4, out_specs=[spec] * 4,
        out_shape=[_sds(w.shape, F32)] * 4, compiler_params=_cp(("arbitrary",)),
    )(*_hbm(w, g, m, v))


def _small_sum_adamw(gathered, w, m, v):
    def body(a_ref, w_ref, m_ref, v_ref, g_ref, d_ref, nm_ref, nv_ref):
        g = a_ref[0]
        for k in range(1, N_DEV):
            g = g + a_ref[k]
        g_ref[...] = g
        d, nm, nv = _adam_math(w_ref[...], g, m_ref[...], v_ref[...])
        d_ref[...] = d
        nm_ref[...] = nm
        nv_ref[...] = nv

    return pl.pallas_call(
        body, name="small_sum_adamw", out_shape=[_sds(w.shape, F32)] * 4,
    )(gathered, w, m, v)


_NAMES = ("g_mix", "w_in", "g_sgu", "w_s", "b_s", "sinks", "rel_bias", "w_pa", "w_pb", "w_out",
          "g_ffn", "w_up", "w_conv", "b_conv", "w_down", "g_final")

def kernel(x, g_mix, w_in, g_sgu, w_s, b_s, sinks, rel_bias, w_pa, w_pb, w_out, g_ffn, w_up, w_conv, b_conv, w_down, g_final, loss_target, m_g_mix, m_w_in, m_g_sgu, m_w_s, m_b_s, m_sinks, m_rel_bias, m_w_pa, m_w_pb, m_w_out, m_g_ffn, m_w_up, m_w_conv, m_b_conv, m_w_down, m_g_final, v_g_mix, v_w_in, v_g_sgu, v_w_s, v_b_s, v_sinks, v_rel_bias, v_w_pa, v_w_pb, v_w_out, v_g_ffn, v_w_up, v_w_conv, v_b_conv, v_w_down, v_g_final):
    w = dict(g_mix=g_mix, w_in=w_in, g_sgu=g_sgu, w_s=w_s, b_s=b_s, sinks=sinks, rel_bias=rel_bias, w_pa=w_pa, w_pb=w_pb,
             w_out=w_out, g_ffn=g_ffn, w_up=w_up, w_conv=w_conv, b_conv=b_conv, w_down=w_down, g_final=g_final)
    m = dict(g_mix=m_g_mix, w_in=m_w_in, g_sgu=m_g_sgu, w_s=m_w_s, b_s=m_b_s, sinks=m_sinks, rel_bias=m_rel_bias, w_pa=m_w_pa,
             w_pb=m_w_pb, w_out=m_w_out, g_ffn=m_g_ffn, w_up=m_w_up, w_conv=m_w_conv, b_conv=m_b_conv, w_down=m_w_down,
             g_final=m_g_final)
    v = dict(g_mix=v_g_mix, w_in=v_w_in, g_sgu=v_g_sgu, w_s=v_w_s, b_s=v_b_s, sinks=v_sinks, rel_bias=v_rel_bias, w_pa=v_w_pa,
             w_pb=v_w_pb, w_out=v_w_out, g_ffn=v_g_ffn, w_up=v_w_up, w_conv=v_w_conv, b_conv=v_b_conv, w_down=v_w_down,
             g_final=v_g_final)
    xi, yi, ci = _mesh_pos()
    me = 2 * xi + yi

    shard = {n: w[n][0] for n in _BIG}
    shard_shapes = {n: shard[n].shape for n in _BIG}
    wc_shard = w["w_conv"][0]
    wc_pad = jnp.pad(wc_shard, ((0, 5), (0, 0)))
    own = {n: _own_slot(shard[n].astype(BF16), N_CHIPS, me) for n in _BIG}
    stacks, wc_all = _allgather_weights({"w_in": own["w_in"]}, _own_slot(wc_pad, N_CHIPS, me))
    proj_gather = _allgather_start({n: own[n] for n in _MIXER[1:]}, "proj", stacks["w_in"])
    ffn_gather = _allgather_start({n: own[n] for n in _FFN}, "ffn", proj_gather[1][-1])
    w_conv_full = jnp.concatenate([wc_all[i, :3] for i in range(N_CHIPS)], axis=1)
    w_in_full = _unstack_w_in(stacks["w_in"])
    pos = jnp.stack([me, ci])

    def proj_weights(done):
        st = _allgather_forward(_allgather_wait(proj_gather, "proj", done), "proj")
        return st["w_pa"], st["w_pb"], st["w_out"].reshape(D_MODEL, D_MODEL)

    def ffn_weights(done):
        st = _allgather_forward(_allgather_wait(ffn_gather, "ffn", done), "ffn")
        return st["w_up"], st["w_down"].reshape(D_FF, D_MODEL)

    groups = {}

    def stage1(group, parts):
        groups[group] = dict(parts=parts, pair=_pair_exchange_start(parts, group, None))
        return groups[group]["pair"][1][-1]

    def stage2(group, after, order_after):
        g = groups[group]
        g["parts"], g["sib"] = _pair_exchange_wait(g["pair"], group, after)
        g["chip"] = _chip_exchange_start({n: _pair_add(g["parts"][n], g["sib"][n], n, pos) for n in g["parts"]}, group, order_after)
        return g["chip"][1][-1]

    def stage3(group, after, order_after):
        g = groups[group]
        got = _chip_exchange_wait(g["chip"], group, after)
        g["share"] = _pair_share_start(
            {n: _owner_sum(g["parts"][n], g["sib"][n], got[n], n, pos, shard_shapes[n]) for n in g["parts"]}, group, order_after)
        return g["share"][1][-1]

    grads, deltas, new_m, new_v = {}, {}, {}, {}

    def stage4(group, after):
        g_shard = _pair_share_wait(groups[group]["share"], group, after)
        last = None
        for n in g_shard:
            g = _tie(g_shard[n], last)
            if n == "w_in":
                d, nm, nv, gt = _adamw(shard[n].T, g.T, m[n][0].T, v[n][0].T, "adamw_" + n)
                grads[n], deltas[n], new_m[n], new_v[n] = gt.T[None], d.T[None], nm.T[None], nv.T[None]
            else:
                d, nm, nv, go = _adamw(shard[n], g, m[n][0], v[n][0], "adamw_" + n)
                grads[n], deltas[n], new_m[n], new_v[n] = go[None], d[None], nm[None], nv[None]
            last = nv
        return last

    def on_grads(group, parts):
        token = stage1(group, parts)
        some = next(iter(parts.values()))
        if group == "proj":
            token = stage2("ffn", some, token)
        if group == "in":
            token = stage2("proj", some, token)
            token = stage3("ffn", some, token)
            token = stage2("in", token, token)
        return token

    loss, grad_x, small, big = _local_step(
        x, loss_target, w["g_mix"], w["g_sgu"], w["w_s"][0], w["b_s"][0], w["sinks"], w["rel_bias"], w["g_ffn"],
        w["b_conv"], w["g_final"], w_in_full, w_conv_full, proj_weights, ffn_weights, on_grads, ffn_gather[1][-1])

    small["loss"] = loss
    small_gather = _small_exchange_start(_own_slot(_pack_small(small), N_DEV, 2 * me + ci), grad_x)
    token = stage3("proj", grad_x, small_gather[-1])
    done = stage4("ffn", token)
    done = stage4("proj", done)
    token = stage3("in", done, None)
    sw = {n: (jnp.zeros((1, 1), F32) if n in ("loss", "w_conv") else w[n]) for n, _ in _SMALL}
    sm = {n: (jnp.zeros((1, 1), F32) if n in ("loss", "w_conv") else m[n]) for n, _ in _SMALL}
    sv = {n: (jnp.zeros((1, 1), F32) if n in ("loss", "w_conv") else v[n]) for n, _ in _SMALL}
    for d in (sw, sm, sv):
        d["w_conv"] = jnp.zeros((3, 2 * D_FF), F32)
    all_small = _small_exchange_wait(small_gather, token)
    s_g, s_d, s_m, s_v = [_unpack_small(a) for a in _small_sum_adamw(all_small, _pack_small(sw), _pack_small(sm), _pack_small(sv))]
    stage4("in", all_small)
    wcols = wc_shard.shape[1]
    g_wc = lax.dynamic_slice(s_g["w_conv"], (0, me * wcols), (3, wcols))
    d, nm, nv, _ = _adamw(wc_shard, g_wc, m["w_conv"][0], v["w_conv"][0], "adamw_w_conv")
    grads["w_conv"], deltas["w_conv"], new_m["w_conv"], new_v["w_conv"] = g_wc[None], d[None], nm[None], nv[None]
    for n, _ in _SMALL:
        if n in ("loss", "w_conv"):
            continue
        shp = w[n].shape
        grads[n], deltas[n], new_m[n], new_v[n] = (s_g[n].reshape(shp), s_d[n].reshape(shp), s_m[n].reshape(shp),
                                                    s_v[n].reshape(shp))

    return (s_g["loss"].reshape(()), grad_x, *[grads[n] for n in _NAMES], *[deltas[n] for n in _NAMES],
            *[new_m[n] for n in _NAMES], *[new_v[n] for n in _NAMES])
```

```python
import functools

import numpy as np
import jax
import jax.numpy as jnp
from jax import lax
from jax.experimental import pallas as pl
from jax.experimental.pallas import tpu as pltpu

F32 = jnp.float32
BF16 = jnp.bfloat16

D_MODEL = 1024
CHUNK = 128
A_GROUPS = 4
A_WIDTH = 512
N_HEADS = 8
HEAD_DIM = 64
Q_DIM = 512
KV_DIM = 128
N_BUCKETS = 32
MAX_DISTANCE = 128
D_FF = 2816
EPS = 1e-6
NEG_INF = -1e30
G_DIM = 2 * D_MODEL
A_DIM = 2 * A_WIDTH
B_DIM = Q_DIM + 2 * KV_DIM
LANES = 128
SUBLANES = 8
ROW_TILE = 512
WIDE_ROW_TILE = 256
COL_CHUNK = 512
GRAD_ROW_TILE = 512
BF16_ROWS = 16
N_CHIPS = 4
N_DEV = 8

ADAM_LR = 0.001
ADAM_B1 = 0.9
ADAM_B2 = 0.999
ADAM_EPS = 1e-08
ADAM_WD = 0.01
ADAM_STEP = 10

MESH = pl.DeviceIdType.MESH
_GELU_C = 0.7978845608028654
_GELU_A = 0.044715


def _cp(sem=None, vmem_mb=None):
    kw = {}
    if sem is not None:
        kw["dimension_semantics"] = sem
    if vmem_mb is not None:
        kw["vmem_limit_bytes"] = vmem_mb << 20
    return pltpu.CompilerParams(**kw)


def _dot(a, b):
    return jnp.dot(a, b, preferred_element_type=F32)


def _dot_nt(a, b):
    return lax.dot_general(a, b, (((1,), (1,)), ((), ())), preferred_element_type=F32)


def _dot_tn(a, b):
    return lax.dot_general(a, b, (((0,), (0,)), ((), ())), preferred_element_type=F32)


def _rms_r(x):
    return lax.rsqrt(jnp.mean(x * x, axis=-1, keepdims=True) + EPS)


def _rms_bwd(dh, n, r, g):
    dn = dh * g
    return r * (dn - n * jnp.mean(dn * n, axis=-1, keepdims=True))


def _gelu(x):
    t = jnp.tanh(_GELU_C * (x + _GELU_A * (x * x * x)))
    return 0.5 * x * (1.0 + t), t


def _gelu_grad(x, t):
    return 0.5 * (1.0 + t) + 0.5 * x * (1.0 - t * t) * (_GELU_C * (1.0 + 3.0 * _GELU_A * x * x))


def _sigmoid(x):
    return 1.0 / (1.0 + jnp.exp(-x))


def _tie(x, dep):
    return x if dep is None else lax.optimization_barrier((x, dep))[0]


def _row(tm, w):
    return pl.BlockSpec((tm, w), lambda i: (i, 0))


def _full(shape):
    nd = len(shape)
    return pl.BlockSpec(tuple(shape), lambda *_: (0,) * nd)


def _resident(shape):
    nd = len(shape)
    return pl.BlockSpec(tuple(shape), lambda *_: (0,) * nd, pipeline_mode=pl.Buffered(1))


def _sds(shape, dtype):
    return jax.ShapeDtypeStruct(tuple(shape), dtype)


def _hbm(*arrays):
    return [pltpu.with_memory_space_constraint(a, pltpu.HBM) for a in arrays]


HBM = pl.BlockSpec(memory_space=pltpu.HBM)
ANY = pl.BlockSpec(memory_space=pl.ANY)
SEM = pl.BlockSpec(memory_space=pltpu.SEMAPHORE)


def _band_buckets():
    i = np.arange(CHUNK)[:, None]
    j = np.arange(2 * CHUNK)[None, :]
    dist = i + CHUNK - j
    valid = (dist >= 0) & (dist < CHUNK)
    d = np.clip(dist, 0, None)
    max_exact = N_BUCKETS // 2
    large = max_exact + (np.log(np.maximum(d, 1) / max_exact) / np.log(MAX_DISTANCE / max_exact)
                         * (N_BUCKETS - max_exact)).astype(np.int32)
    large = np.minimum(large, N_BUCKETS - 1)
    buckets = np.where(d < max_exact, d, large).astype(np.int32)
    return np.where(valid, buckets, -1).astype(np.int32)


_A_COLS = slice(0, A_DIM)
_B_COLS = slice(A_DIM, A_DIM + B_DIM)
_G_COLS = slice(A_DIM + B_DIM, A_DIM + B_DIM + G_DIM)


def _inproj(x2, g_mix, w_in, tm, after=None):
    T = x2.shape[0]
    order = [] if after is None else [after]

    def body(*refs):
        x_ref, g_ref, w_ref = refs[:3]
        pg_ref, pa_ref, pb_ref, h_ref = refs[3 + len(order):]
        x = x_ref[...]
        h = (x * _rms_r(x) * g_ref[...]).astype(BF16)
        h_ref[...] = h
        pg_ref[...] = _dot(h, w_ref[:, _G_COLS]).astype(BF16)
        pa_ref[...] = _dot(h, w_ref[:, _A_COLS]).astype(BF16)
        pb_ref[...] = _dot(h, w_ref[:, _B_COLS]).astype(BF16)

    return pl.pallas_call(
        body, name="inproj", grid=(T // tm,),
        in_specs=[_row(tm, D_MODEL), _full(g_mix.shape), _resident(w_in.shape)] + [ANY] * len(order),
        out_specs=[_row(tm, G_DIM), _row(tm, A_DIM), _row(tm, B_DIM), _row(tm, D_MODEL)],
        out_shape=[_sds((T, G_DIM), BF16), _sds((T, A_DIM), BF16), _sds((T, B_DIM), BF16), _sds((T, D_MODEL), BF16)],
        compiler_params=_cp(("arbitrary",), 48),
    )(*_hbm(x2, g_mix, w_in), *order)


def _sgu_parts(p, g):
    pu = p[:, :A_WIDTH]
    pv = p[:, A_WIDTH:]
    u, tu = _gelu(pu)
    vv, tv = _gelu(pv)
    rv = _rms_r(vv)
    vn = (vv * rv * g).astype(BF16)
    return pu, pv, u, tu, vv, tv, rv, vn


def _tril():
    r = lax.broadcasted_iota(jnp.int32, (CHUNK, CHUNK), 0)
    c = lax.broadcasted_iota(jnp.int32, (CHUNK, CHUNK), 1)
    return r >= c


def _sgu_fwd(proj_a, g_sgu, w_s, b_st, tm):
    T = proj_a.shape[0]

    def body(p_ref, g_ref, ws_ref, bs_ref, y_ref):
        tril = _tril()
        _, _, u, _, _, _, _, vn = _sgu_parts(p_ref[...].astype(F32), g_ref[...])
        for gi in range(A_GROUPS):
            wm = jnp.where(tril, ws_ref[gi], 0.0).astype(BF16)
            bcol = bs_ref[:, gi:gi + 1]
            cs = slice(gi * CHUNK, (gi + 1) * CHUNK)
            for c in range(tm // CHUNK):
                rs = slice(c * CHUNK, (c + 1) * CHUNK)
                s = _dot(wm, vn[rs, cs]) + bcol
                y_ref[rs, cs] = (u[rs, cs] * s).astype(BF16)

    return pl.pallas_call(
        body, name="sgu_fwd", grid=(T // tm,),
        in_specs=[_row(tm, A_DIM), _full(g_sgu.shape), _full(w_s.shape), _full(b_st.shape)],
        out_specs=_row(tm, A_WIDTH), out_shape=_sds((T, A_WIDTH), BF16),
        compiler_params=_cp(("arbitrary",)),
    )(*_hbm(proj_a, g_sgu, w_s, b_st))


HEAD_ROWS = N_HEADS * CHUNK


def _head_rows(h):
    return slice(h * CHUNK, (h + 1) * CHUNK)


def _attn_setup(bias_scr, sink_scr, kvar_scr, qkv_ref, bk_ref, rel_ref, sink_ref):
    bk = bk_ref[...]
    for h in range(N_HEADS):
        acc = jnp.full((CHUNK, 2 * CHUNK), NEG_INF, F32)
        for b in range(N_BUCKETS):
            acc = jnp.where(bk == b, rel_ref[b, h], acc)
        bias_scr[_head_rows(h), :] = acc
        sink_scr[_head_rows(h), :] = jnp.full((CHUNK, LANES), sink_ref[0, h], F32)
    seq = qkv_ref.shape[0]
    rows_per = 2 * CHUNK
    for is_v in range(2):
        c0 = Q_DIM + is_v * KV_DIM
        for r in range(seq // rows_per):
            rs = slice(r * rows_per, (r + 1) * rows_per)
            a = qkv_ref[rs, c0:c0 + KV_DIM].astype(F32)
            lane = lax.broadcasted_iota(jnp.int32, a.shape, 1)
            lo = jnp.where(lane < HEAD_DIM, a, 0.0)
            hi = jnp.where(lane >= HEAD_DIM, a, 0.0)
            kvar_scr[4 * is_v + 0, rs, :] = lo.astype(BF16)
            kvar_scr[4 * is_v + 1, rs, :] = pltpu.roll(lo, HEAD_DIM, 1).astype(BF16)
            kvar_scr[4 * is_v + 2, rs, :] = pltpu.roll(hi, HEAD_DIM, 1).astype(BF16)
            kvar_scr[4 * is_v + 3, rs, :] = hi.astype(BF16)


def _rowsum(a, ones):
    hi = a.astype(BF16)
    lo = (a - hi.astype(F32)).astype(BF16)
    return _dot(hi, ones) + _dot(lo, ones)


def _both(a):
    return jnp.concatenate([a, a], axis=1)


def _attn_probs(qkv_ref, r0, n, kv, bias_scr, sink_scr, ones):
    s = jnp.concatenate([_dot_nt(qkv_ref[pl.ds(r0, CHUNK), (h // 2) * LANES:(h // 2 + 1) * LANES], kv[h // 4][h % 2])
                         for h in range(N_HEADS)], axis=0)
    s = s * (HEAD_DIM ** -0.5) + bias_scr[...]
    col = lax.broadcasted_iota(jnp.int32, s.shape, 1)
    s = jnp.where((col < CHUNK) & (n == 0), NEG_INF, s)
    sink = sink_scr[...]
    m = jnp.maximum(jnp.max(s, axis=-1, keepdims=True), sink)
    p = jnp.exp(s - _both(m))
    es = jnp.exp(sink - m)
    inv = 1.0 / (_rowsum(p, ones) + es)
    return p * _both(inv), es * inv


def _attn_block_inputs(kvar_scr, n):
    r0 = pl.multiple_of(n * CHUNK, CHUNK)
    rp = pl.multiple_of(jnp.maximum(n - 1, 0) * CHUNK, CHUNK)

    def both(idx):
        return jnp.concatenate([kvar_scr[idx, pl.ds(rp, CHUNK), :], kvar_scr[idx, pl.ds(r0, CHUNK), :]], axis=0)

    kv = ((both(0), both(1)), (both(2), both(3)))
    vv = ((both(4), both(5)), (both(6), both(7)))
    return r0, kv, vv


def _attn_fwd(proj_b, sinks, rel_bias, n_seq, seq):
    nb = seq // CHUNK
    bk = jnp.asarray(_band_buckets())

    def body(qkv_ref, bk_ref, rel_ref, sink_ref, o_ref, bias_scr, sink_scr, kvar_scr):
        _attn_setup(bias_scr, sink_scr, kvar_scr, qkv_ref, bk_ref, rel_ref, sink_ref)
        ones = jnp.ones((2 * CHUNK, LANES), BF16)

        def blk(n, carry):
            r0, kv, vv = _attn_block_inputs(kvar_scr, n)
            prob, _ = _attn_probs(qkv_ref, r0, n, kv, bias_scr, sink_scr, ones)
            pb = prob.astype(BF16)
            for pr in range(N_HEADS // 2):
                acc = _dot(pb[_head_rows(2 * pr)], vv[pr // 2][0]) + _dot(pb[_head_rows(2 * pr + 1)], vv[pr // 2][1])
                o_ref[pl.ds(r0, CHUNK), pr * LANES:(pr + 1) * LANES] = acc.astype(BF16)
            return carry

        lax.fori_loop(0, nb, blk, 0)

    smem = pl.BlockSpec(memory_space=pltpu.SMEM)
    return pl.pallas_call(
        body, name="attn_fwd", grid=(n_seq,),
        in_specs=[_row(seq, B_DIM), _full(bk.shape), smem, smem],
        out_specs=_row(seq, Q_DIM), out_shape=_sds((n_seq * seq, Q_DIM), BF16),
        scratch_shapes=[pltpu.VMEM((HEAD_ROWS, 2 * CHUNK), F32), pltpu.VMEM((HEAD_ROWS, LANES), F32),
                        pltpu.VMEM((8, seq, KV_DIM), BF16)],
        compiler_params=_cp(("arbitrary",), 40),
    )(*_hbm(proj_b, bk), rel_bias, sinks)


def _dot_stacked(a, w_ref):
    return jnp.concatenate([_dot(a, w_ref[i]) for i in range(N_CHIPS)], axis=1)


def _dot_nt_stacked(a, w_ref):
    w = w_ref.shape[2]
    acc = _dot_nt(a[:, :w], w_ref[0])
    for i in range(1, N_CHIPS):
        acc = acc + _dot_nt(a[:, i * w:(i + 1) * w], w_ref[i])
    return acc


def _merge_fwd(x2, y_a, y_b, proj_g, w_pa, w_pb, w_out, tm):
    T = x2.shape[0]

    def body(x_ref, ya_ref, yb_ref, g_ref, wpa_ref, wpb_ref, wo_ref, x1_ref, mg_ref):
        g = g_ref[...].astype(F32)
        pa = _dot_stacked(ya_ref[...], wpa_ref)
        pb = _dot_stacked(yb_ref[...], wpb_ref)
        merged = (_sigmoid(g[:, :D_MODEL]) * pa + _sigmoid(g[:, D_MODEL:]) * pb).astype(BF16)
        mg_ref[...] = merged
        x1_ref[...] = x_ref[...] + _dot(merged, wo_ref[...])

    return pl.pallas_call(
        body, name="merge_fwd", grid=(T // tm,),
        in_specs=[_row(tm, D_MODEL), _row(tm, A_WIDTH), _row(tm, Q_DIM), _row(tm, G_DIM),
                  _resident(w_pa.shape), _resident(w_pb.shape), _resident(w_out.shape)],
        out_specs=[_row(tm, D_MODEL), _row(tm, D_MODEL)],
        out_shape=[_sds((T, D_MODEL), F32), _sds((T, D_MODEL), BF16)],
        compiler_params=_cp(("arbitrary",), 40),
    )(*_hbm(x2, y_a, y_b, proj_g, w_pa, w_pb, w_out))


def _upproj(x1, g_ffn, w_up, w_conv, b_conv, tm, seq):
    T = x1.shape[0]
    cw = w_up.shape[2]
    tiles_per_seq = seq // tm

    def body(x_ref, g_ref, w_ref, wc_ref, bc_ref, u_ref, h_ref, gate_ref, val_ref, tail_scr):
        at_start = (pl.program_id(0) % tiles_per_seq) == 0
        x = x_ref[...]
        h = (x * _rms_r(x) * g_ref[...]).astype(BF16)
        h_ref[...] = h
        for i in range(N_CHIPS):
            cs = slice(i * cw, (i + 1) * cw)
            u = _dot(h, w_ref[i])
            u_ref[:, cs] = u.astype(BF16)
            hl = jnp.where(at_start, 0.0, tail_scr[SUBLANES - 2:SUBLANES, cs])
            tail_scr[:, cs] = u[tm - SUBLANES:]
            up = _conv_out((u, _shift_down(u, hl, 1), _shift_down(u, hl, 2)), wc_ref[:, cs], bc_ref[:, cs])
            out_ref = gate_ref if i < N_CHIPS // 2 else val_ref
            out_ref[:, (i % 2) * cw:(i % 2 + 1) * cw] = up.astype(BF16)

    return pl.pallas_call(
        body, name="upproj", grid=(T // tm,),
        in_specs=[_row(tm, D_MODEL), _full(g_ffn.shape), _resident(w_up.shape), _full(w_conv.shape), _full(b_conv.shape)],
        out_specs=[_row(tm, 2 * D_FF), _row(tm, D_MODEL), _row(tm, D_FF), _row(tm, D_FF)],
        out_shape=[_sds((T, 2 * D_FF), BF16), _sds((T, D_MODEL), BF16), _sds((T, D_FF), BF16), _sds((T, D_FF), BF16)],
        scratch_shapes=[pltpu.VMEM((SUBLANES, 2 * D_FF), F32)],
        compiler_params=_cp(("arbitrary",), 56),
    )(*_hbm(x1, g_ffn, w_up, w_conv, b_conv))


def _shift_down(u, halo, k):
    rolled = pltpu.roll(u, k, 0)
    head = rolled[:SUBLANES]
    row = lax.broadcasted_iota(jnp.int32, head.shape, 0)
    if k == 1:
        head = jnp.where(row == 0, halo[1:2], head)
    else:
        head = jnp.where(row == 0, halo[0:1], jnp.where(row == 1, halo[1:2], head))
    return jnp.concatenate([head, rolled[SUBLANES:]], axis=0)


def _shift_up(d, halo, k):
    tm = d.shape[0]
    rolled = pltpu.roll(d, tm - k, 0)
    tail = rolled[tm - SUBLANES:]
    row = lax.broadcasted_iota(jnp.int32, tail.shape, 0)
    if k == 1:
        tail = jnp.where(row == SUBLANES - 1, halo[0:1], tail)
    else:
        tail = jnp.where(row == SUBLANES - 2, halo[0:1], jnp.where(row == SUBLANES - 1, halo[1:2], tail))
    return jnp.concatenate([rolled[:tm - SUBLANES], tail], axis=0)


def _conv_out(taps, wc, bc):
    u, u1, u2 = taps
    return wc[0:1] * u2 + wc[1:2] * u1 + wc[2:3] * u + bc


def _ffn_down_loss(gate, val, x1, target, w_down, g_final, tm):
    T = x1.shape[0]
    half = D_FF // 2

    def body(gt_ref, vl_ref, x1_ref, t_ref, wd_ref, g_ref, dx2_ref, loss_ref, gg_ref):
        i = pl.program_id(0)
        acc = jnp.zeros((tm, D_MODEL), F32)
        for j in range(2):
            gc = slice(j * half, (j + 1) * half)
            gate = gt_ref[:, gc].astype(F32)
            act = (gate * _sigmoid(gate) * vl_ref[:, gc].astype(F32)).astype(BF16)
            acc = acc + _dot(act, wd_ref[gc, :])
        x2 = x1_ref[...] + acc
        r = _rms_r(x2)
        n = x2 * r
        g = g_ref[...]
        diff = n * g - t_ref[...]
        dy = diff * (1.0 / D_MODEL)
        dx2_ref[...] = _rms_bwd(dy, n, r, g)

        @pl.when(i == 0)
        def _():
            loss_ref[...] = jnp.zeros_like(loss_ref)
            gg_ref[...] = jnp.zeros_like(gg_ref)

        loss_ref[...] += 0.5 * jnp.sum(jnp.mean(diff * diff, axis=-1, keepdims=True), axis=0, keepdims=True)
        gg_ref[...] += jnp.sum(dy * n, axis=0, keepdims=True)

    return pl.pallas_call(
        body, name="ffn_down_loss", grid=(T // tm,),
        in_specs=[_row(tm, D_FF), _row(tm, D_FF), _row(tm, D_MODEL), _row(tm, D_MODEL),
                  _resident(w_down.shape), _full(g_final.shape)],
        out_specs=[_row(tm, D_MODEL), _full((1, 1)), _full((1, D_MODEL))],
        out_shape=[_sds((T, D_MODEL), F32), _sds((1, 1), F32), _sds((1, D_MODEL), F32)],
        compiler_params=_cp(("arbitrary",), 48),
    )(*_hbm(gate, val, x1, target, w_down, g_final))


def _ffn_bwd_act(gate, val, dx2, w_down, tm):
    T = dx2.shape[0]
    half = D_FF // 2
    nt = T // tm

    def body(g_ref, v_ref, dx_ref, wd_ref, dg_ref, dv_ref, gwd_out, gbg_ref, gbv_ref, gwd_ref):
        i = pl.program_id(1)

        @pl.when(i == 0)
        def _():
            for r in (gwd_ref, gbg_ref, gbv_ref):
                r[...] = jnp.zeros_like(r)

        dx = dx_ref[...].astype(BF16)
        for c0 in range(0, half, COL_CHUNK):
            cs = slice(c0, min(c0 + COL_CHUNK, half))
            gate = g_ref[:, cs].astype(F32)
            val = v_ref[:, cs].astype(F32)
            sg = _sigmoid(gate)
            silu = gate * sg
            d_act = _dot_nt(dx, wd_ref[cs, :])
            d_val = d_act * silu
            d_gate = d_act * val * (sg * (1.0 + gate * (1.0 - sg)))
            dg_ref[:, cs] = d_gate.astype(BF16)
            dv_ref[:, cs] = d_val.astype(BF16)
            gwd_ref[cs, :] += _dot_tn((silu * val).astype(BF16), dx)
            gbg_ref[:, cs] += jnp.sum(d_gate, axis=0, keepdims=True)
            gbv_ref[:, cs] += jnp.sum(d_val, axis=0, keepdims=True)

        @pl.when(i == nt - 1)
        def _():
            gwd_out[...] = gwd_ref[...].astype(BF16)

    tile = pl.BlockSpec((tm, half), lambda j, i: (i, j))
    vec = pl.BlockSpec((1, half), lambda j, i: (0, j))
    wrows = pl.BlockSpec((half, D_MODEL), lambda j, i: (j, 0))
    return pl.pallas_call(
        body, name="ffn_bwd_act", grid=(2, nt),
        in_specs=[tile, tile, pl.BlockSpec((tm, D_MODEL), lambda j, i: (i, 0)), wrows],
        out_specs=[tile, tile, wrows, vec, vec],
        out_shape=[_sds((T, D_FF), BF16), _sds((T, D_FF), BF16), _sds((D_FF, D_MODEL), BF16),
                   _sds((1, D_FF), F32), _sds((1, D_FF), F32)],
        scratch_shapes=[pltpu.VMEM((half, D_MODEL), F32)],
        compiler_params=_cp(("arbitrary", "arbitrary"), 56),
    )(*_hbm(gate, val, dx2, w_down))


def _ffn_bwd_up(d_gate, d_val, upre, dx2, x1, g_ffn, w_conv, w_up, tm, seq):
    T = dx2.shape[0]
    tiles_per_seq = seq // tm
    k16 = tm // BF16_ROWS
    n16 = T // BF16_ROWS
    cw = D_FF // 2

    def body(dg_ref, dv_ref, hg_ref, hv_ref, u_ref, dx2_ref, x1_ref, g_ref, wc_ref, wu_ref, du_ref, dx1_ref, gg_ref, gwc_ref):
        i = pl.program_id(0)
        at_end = (i % tiles_per_seq) == tiles_per_seq - 1

        @pl.when(i == 0)
        def _():
            gg_ref[...] = jnp.zeros_like(gg_ref)
            gwc_ref[...] = jnp.zeros_like(gwc_ref)

        dh = jnp.zeros((tm, D_MODEL), F32)
        for j in range(4):
            src, hsrc = (dg_ref, hg_ref) if j < 2 else (dv_ref, hv_ref)
            ls = slice((j % 2) * cw, (j % 2 + 1) * cw)
            cs = slice(j * cw, (j + 1) * cw)
            d = src[:, ls].astype(F32)
            hl = hsrc[:, ls].astype(F32)[0:2]
            hl = jnp.where(at_end, 0.0, hl)
            wc = wc_ref[:, cs]
            d1 = _shift_up(d, hl, 1)
            d2 = _shift_up(d, hl, 2)
            du = (wc[2:3] * d + wc[1:2] * d1 + wc[0:1] * d2).astype(BF16)
            du_ref[:, cs] = du
            dh = dh + _dot_nt(du, wu_ref[j])
            u = u_ref[:, cs].astype(F32)
            gwc_ref[0:1, cs] += jnp.sum(d2 * u, axis=0, keepdims=True)
            gwc_ref[1:2, cs] += jnp.sum(d1 * u, axis=0, keepdims=True)
            gwc_ref[2:3, cs] += jnp.sum(d * u, axis=0, keepdims=True)
        x = x1_ref[...]
        r = _rms_r(x)
        n = x * r
        dx1_ref[...] = dx2_ref[...] + _rms_bwd(dh, n, r, g_ref[...])
        gg_ref[...] += jnp.sum(dh * n, axis=0, keepdims=True)

    nxt = pl.BlockSpec((BF16_ROWS, D_FF), lambda i: (jnp.minimum((i + 1) * k16, n16 - 1), 0))
    return pl.pallas_call(
        body, name="ffn_bwd_up", grid=(T // tm,),
        in_specs=[_row(tm, D_FF), _row(tm, D_FF), nxt, nxt, _row(tm, 2 * D_FF), _row(tm, D_MODEL), _row(tm, D_MODEL),
                  _full(g_ffn.shape), _full(w_conv.shape), _resident(w_up.shape)],
        out_specs=[_row(tm, 2 * D_FF), _row(tm, D_MODEL), _full((1, D_MODEL)), _full((3, 2 * D_FF))],
        out_shape=[_sds((T, 2 * D_FF), BF16), _sds((T, D_MODEL), F32), _sds((1, D_MODEL), F32), _sds((3, 2 * D_FF), F32)],
        compiler_params=_cp(("arbitrary",), 56),
    )(*_hbm(d_gate, d_val, d_gate, d_val, upre, dx2, x1, g_ffn, w_conv, w_up))


def _matmul_tn(a, b, tn, tk, name):
    T, M = a.shape
    N = b.shape[1]
    nk = T // tk

    def body(a_ref, b_ref, o_ref, acc_ref):
        k = pl.program_id(1)

        @pl.when(k == 0)
        def _():
            acc_ref[...] = jnp.zeros_like(acc_ref)

        acc_ref[...] += _dot_tn(a_ref[...], b_ref[...])

        @pl.when(k == nk - 1)
        def _():
            o_ref[...] = acc_ref[...].astype(BF16)

    return pl.pallas_call(
        body, name=name, grid=(N // tn, nk),
        in_specs=[pl.BlockSpec((tk, M), lambda j, k: (k, 0)), pl.BlockSpec((tk, tn), lambda j, k: (k, j))],
        out_specs=pl.BlockSpec((M, tn), lambda j, k: (0, j)), out_shape=_sds((M, N), BF16),
        scratch_shapes=[pltpu.VMEM((M, tn), F32)],
        compiler_params=_cp(("arbitrary", "arbitrary"), 48),
    )(*_hbm(a, b))


def _merge_bwd(dx1, merged, y_a, y_b, proj_g, w_pa, w_pb, w_out, tm, after=None):
    T = dx1.shape[0]

    nt = T // tm
    pshape = (A_WIDTH, D_MODEL)
    order = [] if after is None else [after]

    def body(*refs):
        dx_ref, mg_ref, ya_ref, yb_ref, g_ref, wpa_ref, wpb_ref, wo_ref = refs[:8]
        dg_ref, dya_ref, dyb_ref, gwo_out, gwpa_out, gwpb_out, gwo_ref, gwpa_ref, gwpb_ref = refs[8 + len(order):]
        i = pl.program_id(0)
        dx = dx_ref[...].astype(BF16)
        dm = _dot_nt(dx, wo_ref[...])
        g = g_ref[...].astype(F32)
        ya = ya_ref[...]
        yb = yb_ref[...]
        pa = _dot_stacked(ya, wpa_ref)
        pb = _dot_stacked(yb, wpb_ref)
        sa = _sigmoid(g[:, :D_MODEL])
        sb = _sigmoid(g[:, D_MODEL:])
        dpa = (dm * sa).astype(BF16)
        dpb = (dm * sb).astype(BF16)
        dg_ref[:, :D_MODEL] = (dm * pa * (sa * (1.0 - sa))).astype(BF16)
        dg_ref[:, D_MODEL:] = (dm * pb * (sb * (1.0 - sb))).astype(BF16)
        dya_ref[...] = _dot_nt_stacked(dpa, wpa_ref).astype(BF16)
        dyb_ref[...] = _dot_nt_stacked(dpb, wpb_ref).astype(BF16)

        @pl.when(i == 0)
        def _():
            for r in (gwo_ref, gwpa_ref, gwpb_ref):
                r[...] = jnp.zeros_like(r)

        gwo_ref[...] += _dot_tn(mg_ref[...], dx)
        gwpa_ref[...] += _dot_tn(ya, dpa)
        gwpb_ref[...] += _dot_tn(yb, dpb)

        @pl.when(i == nt - 1)
        def _():
            gwo_out[...] = gwo_ref[...].astype(BF16)
            gwpa_out[...] = gwpa_ref[...].astype(BF16)
            gwpb_out[...] = gwpb_ref[...].astype(BF16)

    return pl.pallas_call(
        body, name="merge_bwd", grid=(nt,),
        in_specs=[_row(tm, D_MODEL), _row(tm, D_MODEL), _row(tm, A_WIDTH), _row(tm, Q_DIM), _row(tm, G_DIM),
                  _resident(w_pa.shape), _resident(w_pb.shape), _resident(w_out.shape)] + [ANY] * len(order),
        out_specs=[_row(tm, G_DIM), _row(tm, A_WIDTH), _row(tm, Q_DIM),
                   _full(w_out.shape), _full(pshape), _full(pshape)],
        out_shape=[_sds((T, G_DIM), BF16), _sds((T, A_WIDTH), BF16), _sds((T, Q_DIM), BF16),
                   _sds(w_out.shape, BF16), _sds(pshape, BF16), _sds(pshape, BF16)],
        scratch_shapes=[pltpu.VMEM(w_out.shape, F32), pltpu.VMEM(pshape, F32), pltpu.VMEM(pshape, F32)],
        compiler_params=_cp(("arbitrary",), 56),
    )(*_hbm(dx1, merged, y_a, y_b, proj_g, w_pa, w_pb, w_out), *order)


def _sgu_bwd(proj_a, d_ya, g_sgu, w_s, b_st, tm, after=None):
    T = proj_a.shape[0]
    order = [] if after is None else [after]

    def body(*refs):
        p_ref, dy_ref, g_ref, ws_ref, bs_ref = refs[:5]
        dp_ref, gws_ref, gbs_ref, gg_ref = refs[5 + len(order):]
        tril = _tril()
        g = g_ref[...]
        pu, pv, u, tu, vv, tv, rv, vn = _sgu_parts(p_ref[...].astype(F32), g)
        dy = dy_ref[...].astype(F32)

        @pl.when(pl.program_id(0) == 0)
        def _():
            for r in (gws_ref, gbs_ref, gg_ref):
                r[...] = jnp.zeros_like(r)

        du_cols = []
        dvn_cols = []
        for gi in range(A_GROUPS):
            wm = jnp.where(tril, ws_ref[gi], 0.0).astype(BF16)
            wmt = wm.astype(F32).T.astype(BF16)
            bcol = bs_ref[:, gi:gi + 1]
            cs = slice(gi * CHUNK, (gi + 1) * CHUNK)
            du_rows = []
            dvn_rows = []
            gw = jnp.zeros((CHUNK, CHUNK), F32)
            gb = jnp.zeros((CHUNK, 1), F32)
            for c in range(tm // CHUNK):
                rs = slice(c * CHUNK, (c + 1) * CHUNK)
                vn_c = vn[rs, cs]
                s = _dot(wm, vn_c) + bcol
                dy_c = dy[rs, cs]
                ds = dy_c * u[rs, cs]
                du_rows.append(dy_c * s)
                dsb = ds.astype(BF16)
                gw = gw + _dot_nt(dsb, vn_c)
                gb = gb + jnp.sum(ds, axis=-1, keepdims=True)
                dvn_rows.append(_dot(wmt, dsb))
            gws_ref[gi] += jnp.where(tril, gw, 0.0)
            gbs_ref[:, gi:gi + 1] += gb
            du_cols.append(jnp.concatenate(du_rows, axis=0))
            dvn_cols.append(jnp.concatenate(dvn_rows, axis=0))
        du = jnp.concatenate(du_cols, axis=1)
        dvn = jnp.concatenate(dvn_cols, axis=1)
        vhat = vv * rv
        gg_ref[...] += jnp.sum(dvn * vhat, axis=0, keepdims=True)
        dvv = _rms_bwd(dvn, vhat, rv, g)
        dp_ref[:, :A_WIDTH] = (du * _gelu_grad(pu, tu)).astype(BF16)
        dp_ref[:, A_WIDTH:] = (dvv * _gelu_grad(pv, tv)).astype(BF16)

    return pl.pallas_call(
        body, name="sgu_bwd", grid=(T // tm,),
        in_specs=[_row(tm, A_DIM), _row(tm, A_WIDTH), _full(g_sgu.shape), _full(w_s.shape), _full(b_st.shape)] + [ANY] * len(order),
        out_specs=[_row(tm, A_DIM), _full(w_s.shape), _full(b_st.shape), _full(g_sgu.shape)],
        out_shape=[_sds((T, A_DIM), BF16), _sds(w_s.shape, F32), _sds(b_st.shape, F32), _sds(g_sgu.shape, F32)],
        compiler_params=_cp(("arbitrary",)),
    )(*_hbm(proj_a, d_ya, g_sgu, w_s, b_st), *order)


def _attn_bwd(proj_b, d_yb, sinks, rel_bias, n_seq, seq):
    nb = seq // CHUNK
    bk = jnp.asarray(_band_buckets())

    def body(qkv_ref, do_ref, bk_ref, rel_ref, sink_ref, d_ref, gs_ref, gr_ref,
             bias_scr, sink_scr, kvar_scr, dbias_scr, dk_scr, dv_scr, ds_scr):
        b = pl.program_id(0)
        _attn_setup(bias_scr, sink_scr, kvar_scr, qkv_ref, bk_ref, rel_ref, sink_ref)
        ones = jnp.ones((2 * CHUNK, LANES), BF16)

        @pl.when(b == 0)
        def _():
            dbias_scr[...] = jnp.zeros_like(dbias_scr)
            ds_scr[...] = jnp.zeros_like(ds_scr)

        dk_scr[...] = jnp.zeros_like(dk_scr)
        dv_scr[...] = jnp.zeros_like(dv_scr)

        def transposed(a):
            return a.astype(F32).T.astype(BF16)

        def blk(n, carry):
            r0, kv, vv = _attn_block_inputs(kvar_scr, n)
            prob, psink = _attn_probs(qkv_ref, r0, n, kv, bias_scr, sink_scr, ones)
            dp = jnp.concatenate([_dot_nt(do_ref[pl.ds(r0, CHUNK), (h // 2) * LANES:(h // 2 + 1) * LANES], vv[h // 4][h % 2])
                                  for h in range(N_HEADS)], axis=0)
            delta = _rowsum(prob * dp, ones)
            dsc = prob * (dp - _both(delta))
            ds_scr[...] += psink * delta
            dbias_scr[...] += dsc
            dsb = (dsc * (HEAD_DIM ** -0.5)).astype(BF16)
            pb = prob.astype(BF16)
            dkt = [jnp.zeros((HEAD_DIM, 2 * CHUNK), F32) for _ in range(2)]
            dvt = [jnp.zeros((HEAD_DIM, 2 * CHUNK), F32) for _ in range(2)]
            for pr in range(N_HEADS // 2):
                ps = slice(pr * LANES, (pr + 1) * LANES)
                qpt = transposed(qkv_ref[pl.ds(r0, CHUNK), ps])
                dopt = transposed(do_ref[pl.ds(r0, CHUNK), ps])
                kvh = pr // 2
                dq = jnp.zeros((CHUNK, LANES), F32)
                for hh in range(2):
                    hr = _head_rows(2 * pr + hh)
                    rows = slice(hh * HEAD_DIM, (hh + 1) * HEAD_DIM)
                    dq = dq + _dot(dsb[hr], kv[kvh][hh])
                    dkt[kvh] = dkt[kvh] + _dot(qpt, dsb[hr])[rows]
                    dvt[kvh] = dvt[kvh] + _dot(dopt, pb[hr])[rows]
                d_ref[pl.ds(r0, CHUNK), ps] = dq.astype(BF16)
            dk_scr[:, pl.ds(r0, 2 * CHUNK)] += jnp.concatenate(dkt, axis=0)
            dv_scr[:, pl.ds(r0, 2 * CHUNK)] += jnp.concatenate(dvt, axis=0)
            return carry

        lax.fori_loop(0, nb, blk, 0)
        for n in range(nb):
            rows = slice(n * CHUNK, (n + 1) * CHUNK)
            cols = slice((n + 1) * CHUNK, (n + 2) * CHUNK)
            d_ref[rows, Q_DIM:Q_DIM + KV_DIM] = dk_scr[:, cols].T.astype(BF16)
            d_ref[rows, Q_DIM + KV_DIM:] = dv_scr[:, cols].T.astype(BF16)

        @pl.when(b == n_seq - 1)
        def _():
            bkv = bk_ref[...]
            for h in range(N_HEADS):
                gs_ref[0:1, h:h + 1] = -jnp.sum(ds_scr[_head_rows(h), 0:1], axis=0, keepdims=True)
                db = dbias_scr[_head_rows(h), :]
                for bb in range(N_BUCKETS):
                    part = jnp.sum(jnp.where(bkv == bb, db, 0.0), axis=-1, keepdims=True)
                    gr_ref[bb:bb + 1, h:h + 1] = jnp.sum(part, axis=0, keepdims=True)

    smem = pl.BlockSpec(memory_space=pltpu.SMEM)
    return pl.pallas_call(
        body, name="attn_bwd", grid=(n_seq,),
        in_specs=[_row(seq, B_DIM), _row(seq, Q_DIM), _full(bk.shape), smem, smem],
        out_specs=[_row(seq, B_DIM), _full((1, N_HEADS)), _full((N_BUCKETS, N_HEADS))],
        out_shape=[_sds((n_seq * seq, B_DIM), BF16), _sds((1, N_HEADS), F32), _sds((N_BUCKETS, N_HEADS), F32)],
        scratch_shapes=[pltpu.VMEM((HEAD_ROWS, 2 * CHUNK), F32), pltpu.VMEM((HEAD_ROWS, LANES), F32),
                        pltpu.VMEM((8, seq, KV_DIM), BF16), pltpu.VMEM((HEAD_ROWS, 2 * CHUNK), F32),
                        pltpu.VMEM((KV_DIM, seq + CHUNK), F32), pltpu.VMEM((KV_DIM, seq + CHUNK), F32),
                        pltpu.VMEM((HEAD_ROWS, LANES), F32)],
        compiler_params=_cp(("arbitrary",), 40),
    )(*_hbm(proj_b, d_yb, bk), rel_bias, sinks)


def _inproj_bwd(d_g, d_a, d_b, x2, dx1, g_mix, w_in, tm, after=None):
    T = x2.shape[0]
    order = [] if after is None else [after]

    def body(*refs):
        dg_ref, da_ref, db_ref, x_ref, dx1_ref, g_ref, w_ref = refs[:7]
        gx_ref, gg_ref = refs[7 + len(order):]
        dh = (_dot_nt(dg_ref[...], w_ref[:, _G_COLS]) + _dot_nt(da_ref[...], w_ref[:, _A_COLS])
              + _dot_nt(db_ref[...], w_ref[:, _B_COLS]))
        x = x_ref[...]
        r = _rms_r(x)
        n = x * r
        gx_ref[...] = dx1_ref[...] + _rms_bwd(dh, n, r, g_ref[...])

        @pl.when(pl.program_id(0) == 0)
        def _():
            gg_ref[...] = jnp.zeros_like(gg_ref)

        gg_ref[...] += jnp.sum(dh * n, axis=0, keepdims=True)

    return pl.pallas_call(
        body, name="inproj_bwd", grid=(T // tm,),
        in_specs=[_row(tm, G_DIM), _row(tm, A_DIM), _row(tm, B_DIM), _row(tm, D_MODEL), _row(tm, D_MODEL),
                  _full(g_mix.shape), _resident(w_in.shape)] + [ANY] * len(order),
        out_specs=[_row(tm, D_MODEL), _full((1, D_MODEL))],
        out_shape=[_sds((T, D_MODEL), F32), _sds((1, D_MODEL), F32)],
        compiler_params=_cp(("arbitrary",), 48),
    )(*_hbm(d_g, d_a, d_b, x2, dx1, g_mix, w_in), *order)


IN_SHARD = (A_DIM + B_DIM + G_DIM) // N_CHIPS


def _unstack_w_in(stack):
    tr = 256

    def body(s_ref, o_ref):
        for i in range(N_CHIPS):
            o_ref[:, i * IN_SHARD:(i + 1) * IN_SHARD] = s_ref[i]

    return pl.pallas_call(
        body, name="unstack_w_in", grid=(D_MODEL // tr,),
        in_specs=[pl.BlockSpec((N_CHIPS, tr, IN_SHARD), lambda r: (0, r, 0))],
        out_specs=pl.BlockSpec((tr, N_CHIPS * IN_SHARD), lambda r: (r, 0)),
        out_shape=_sds((D_MODEL, N_CHIPS * IN_SHARD), stack.dtype),
        compiler_params=_cp(("arbitrary",)),
    )(*_hbm(stack))


def _stack_grad_w_in(gw_a, gw_b, gw_g):
    tr = 256

    def body(a_ref, b_ref, g_ref, o_ref):
        full = jnp.concatenate([a_ref[...], b_ref[...], g_ref[...]], axis=1)
        for i in range(N_CHIPS):
            o_ref[i] = full[:, i * IN_SHARD:(i + 1) * IN_SHARD]

    return pl.pallas_call(
        body, name="stack_grad_w_in", grid=(D_MODEL // tr,),
        in_specs=[_row(tr, A_DIM), _row(tr, B_DIM), _row(tr, G_DIM)],
        out_specs=pl.BlockSpec((N_CHIPS, tr, IN_SHARD), lambda r: (0, r, 0)),
        out_shape=_sds((N_CHIPS, D_MODEL, IN_SHARD), gw_a.dtype),
        compiler_params=_cp(("arbitrary",)),
    )(*_hbm(gw_a, gw_b, gw_g))


def _local_step(x, target, g_mix, g_sgu, w_s, b_s, sinks, rel_bias, g_ffn, b_conv, g_final,
                w_in, w_conv, proj_weights, ffn_weights, on_grads, after=None):
    n_seq, seq, _ = x.shape
    T = n_seq * seq
    tm = min(ROW_TILE, seq)
    tw = min(GRAD_ROW_TILE, T)
    tf = min(WIDE_ROW_TILE, seq)
    x2 = x.reshape(T, D_MODEL)
    tgt = target.reshape(T, D_MODEL)
    b_st = b_s.T
    g_fin = g_final.reshape(1, D_MODEL)

    proj_g, proj_a, proj_b, h = _inproj(x2, g_mix, w_in, tm, after)
    y_a = _sgu_fwd(proj_a, g_sgu, w_s, b_st, tm)
    y_b = _attn_fwd(proj_b, sinks, rel_bias, n_seq, seq)
    w_pa, w_pb, w_out = proj_weights(y_b)
    x1, merged = _merge_fwd(x2, y_a, y_b, proj_g, w_pa, w_pb, w_out, tm)
    w_up, w_down = ffn_weights(x1)
    upre, h2, gate, val = _upproj(x1, g_ffn, w_up, w_conv, b_conv, tf, seq)
    dx2, loss, gg_final = _ffn_down_loss(gate, val, x1, tgt, w_down, g_fin, tm)

    d_gate, d_val, gw_down, gb_g, gb_v = _ffn_bwd_act(gate, val, dx2, w_down, tw)
    gb_conv = jnp.concatenate([gb_g, gb_v], axis=1)
    d_upre, dx1, gg_ffn, gw_conv = _ffn_bwd_up(d_gate, d_val, upre, dx2, x1, g_ffn, w_conv, w_up, tf, seq)
    gw_up = _matmul_tn(h2, d_upre, 2 * D_FF // 4, min(2 * GRAD_ROW_TILE, T), "grad_w_up")
    sent = on_grads("ffn", dict(w_up=gw_up, w_down=gw_down))
    d_g, d_ya, d_yb, gw_out, gw_pa, gw_pb = _merge_bwd(dx1, merged, y_a, y_b, proj_g, w_pa, w_pb, w_out, tf, sent)
    sent = on_grads("proj", dict(w_pa=gw_pa, w_pb=gw_pb, w_out=gw_out))
    d_a, gw_s, gb_st, gg_sgu = _sgu_bwd(proj_a, d_ya, g_sgu, w_s, b_st, tm, sent)
    d_b, g_sinks, g_rel = _attn_bwd(proj_b, _tie(d_yb, d_a), sinks, rel_bias, n_seq, seq)
    gw_g = _matmul_tn(h, _tie(d_g, d_b), D_MODEL, min(2 * GRAD_ROW_TILE, T), "grad_w_in_gate")
    gw_a = _matmul_tn(h, _tie(d_a, gw_g), A_DIM, min(2 * GRAD_ROW_TILE, T), "grad_w_in_a")
    gw_b = _matmul_tn(h, _tie(d_b, gw_a), B_DIM, min(2 * GRAD_ROW_TILE, T), "grad_w_in_b")
    gw_in = _stack_grad_w_in(gw_a, gw_b, gw_g)
    sent = on_grads("in", dict(w_in=gw_in))
    grad_x, gg_mix = _inproj_bwd(d_g, d_a, d_b, x2, dx1, g_mix, w_in, tm, sent)

    small = dict(g_mix=gg_mix, g_sgu=gg_sgu, w_s=gw_s, b_s=gb_st.T, sinks=g_sinks, rel_bias=g_rel,
                 g_ffn=gg_ffn, b_conv=gb_conv, g_final=gg_final, w_conv=gw_conv)
    big = dict(w_in=gw_in, w_pa=gw_pa, w_pb=gw_pb, w_out=gw_out, w_up=gw_up, w_down=gw_down)
    return loss, grad_x.reshape(x.shape), small, big


_MIXER = ("w_in", "w_pa", "w_pb", "w_out")
_FFN = ("w_up", "w_down")
_BIG = _MIXER + _FFN

CONV_ROWS = 6
_SMALL_AT = dict(loss=(0, 1, 1), g_final=(1, 1, D_MODEL), g_mix=(2, 1, D_MODEL), g_ffn=(3, 1, D_MODEL), g_sgu=(4, 1, A_WIDTH),
                 sinks=(5, 1, N_HEADS), b_s=(8, A_GROUPS, CHUNK), rel_bias=(16, N_BUCKETS, N_HEADS),
                 b_conv=(48, CONV_ROWS, D_MODEL), w_conv=(56, 3 * CONV_ROWS, D_MODEL), w_s=(80, A_GROUPS * CHUNK * CHUNK // D_MODEL, D_MODEL))
_SMALL_IN_CALL = ("g_final", "g_mix", "g_ffn", "g_sgu", "sinks", "b_s", "rel_bias")
SMALL_ROWS = 144


def _pack_small(vals):
    def wide(a):
        return jnp.pad(a, ((0, 0), (0, CONV_ROWS * D_MODEL - a.shape[1]))).reshape(-1, D_MODEL)

    laid = dict(vals, b_conv=wide(vals["b_conv"]), w_conv=wide(vals["w_conv"]), w_s=vals["w_s"].reshape(-1, D_MODEL))
    rows, at = [], 0
    for n, (r0, nr, nc) in _SMALL_AT.items():
        if r0 > at:
            rows.append(jnp.zeros((r0 - at, D_MODEL), F32))
        rows.append(jnp.pad(laid[n].astype(F32).reshape(nr, nc), ((0, 0), (0, D_MODEL - nc))))
        at = r0 + nr
    return jnp.concatenate(rows, axis=0)


def _unwide(a, r):
    return a.reshape(r, CONV_ROWS * D_MODEL)[:, :2 * D_FF]


def _mesh_pos():
    return lax.axis_index("x"), lax.axis_index("y"), lax.axis_index("c")


def _other_chips(x, y):
    return [(1 - x, y), (x, 1 - y), (1 - x, 1 - y)]


def _remote(src, dst, send_sem, recv_sem, to):
    return pltpu.make_async_remote_copy(src_ref=src, dst_ref=dst, send_sem=send_sem, recv_sem=recv_sem,
                                        device_id=to, device_id_type=MESH)


def _own_slot(own, n, at):
    return lax.dynamic_update_slice(lax.empty((n,) + own.shape, own.dtype), own[None], (at,) + (0,) * own.ndim)


def _allgather_weights(stacks, wc_stack):
    names = list(stacks)
    n = len(names)

    def body(*refs):
        ins, outs = refs[:n + 1], refs[n + 1:2 * n + 2]
        send_sems, recv_sems = refs[2 * n + 2:]
        x, y, c = _mesh_pos()
        me = 2 * x + y
        sibling = (x, y, 1 - c)
        chips = _other_chips(x, y)

        def half(ref, chip, hc):
            hr = ref.shape[1] // 2
            return ref.at[chip, pl.ds(hc * hr, hr), :]

        first = []
        for k in range(n):
            first += [_remote(half(ins[k], me, c), half(outs[k], me, c), send_sems.at[6 * k + j], recv_sems.at[6 * k + j], (cx, cy, c))
                      for j, (cx, cy) in enumerate(chips)]
        first += [_remote(ins[n].at[me], outs[n].at[me], send_sems.at[6 * n + j], recv_sems.at[6 * n + j], (cx, cy, c))
                  for j, (cx, cy) in enumerate(chips)]
        for cp in first:
            cp.start()
        passed = []
        for k in range(n):
            for j, (cx, cy) in enumerate(chips):
                landed = half(outs[k], 2 * cx + cy, c)
                _remote(landed, landed, send_sems.at[6 * k + j], recv_sems.at[6 * k + j], (x, y, c)).wait_recv()
                passed.append(_remote(landed, landed, send_sems.at[6 * k + 3 + j], recv_sems.at[6 * k + 3 + j], sibling))
                passed[-1].start()
        for k in range(n):
            for j, (cx, cy) in enumerate(chips):
                theirs = half(outs[k], 2 * cx + cy, 1 - c)
                _remote(theirs, theirs, send_sems.at[6 * k + 3 + j], recv_sems.at[6 * k + 3 + j], (x, y, c)).wait_recv()
        for j, (cx, cy) in enumerate(chips):
            slot = outs[n].at[2 * cx + cy]
            _remote(slot, slot, send_sems.at[6 * n + j], recv_sems.at[6 * n + j], (x, y, c)).wait_recv()
        for cp in first + passed:
            cp.wait_send()

    arrays = [stacks[k] for k in names] + [wc_stack]
    outs = pl.pallas_call(
        body, name="allgather_weights",
        in_specs=[HBM] * (n + 1), out_specs=[HBM] * (n + 1), input_output_aliases={k: k for k in range(n + 1)},
        out_shape=[_sds(a.shape, a.dtype) for a in arrays],
        scratch_shapes=[pltpu.SemaphoreType.DMA((6 * n + 3,)), pltpu.SemaphoreType.DMA((6 * n + 3,))],
    )(*arrays)
    return dict(zip(names, outs[:n])), outs[n]


_KIND = {"w_in": "stack", "w_pa": "col", "w_pb": "col", "w_up": "col", "w_out": "row", "w_down": "row"}


def _half_view(ref, kind, h):
    if kind == "stack":
        k = ref.shape[1] // 2
        return ref.at[:, pl.ds(h * k, k), :]
    if kind == "col":
        k = ref.shape[0] // 2
        return ref.at[pl.ds(h * k, k), :]
    k = ref.shape[1] // 2
    return ref.at[:, pl.ds(h * k, k)]


def _shard_view(ref, kind, i):
    if kind == "stack":
        return ref.at[i]
    if kind == "col":
        k = ref.shape[1] // N_CHIPS
        return ref.at[:, pl.ds(i * k, k)]
    k = ref.shape[0] // N_CHIPS
    return ref.at[pl.ds(i * k, k), :]


def _region_view(ref, kind, h):
    if kind == "row":
        k = ref.shape[1] // 2
        return ref.at[:, pl.ds(h * k, k)]
    k = ref.shape[0] // 2
    return ref.at[pl.ds(h * k, k), :]


def _half_shape(shape, kind):
    if kind == "stack":
        return (shape[0], shape[1] // 2, shape[2])
    return (shape[0] // 2, shape[1]) if kind == "col" else (shape[0], shape[1] // 2)


def _part_shape(half_shape, kind):
    if kind == "stack":
        return tuple(half_shape[1:])
    k, w = half_shape
    return (k, w // N_CHIPS) if kind == "col" else (k // N_CHIPS, w)


_DATAFLOW = pltpu.SideEffectType.DATAFLOW_SIDE_EFFECTING
_TOKEN = (SUBLANES, LANES)


def _split_start(name, arrays, n_sems, issue, after=None):
    n = len(arrays)
    order = [] if after is None else [after]

    def body(*refs):
        base = n + len(order)
        issue(refs[:n], refs[base], refs[base + 1])
        refs[-1][...] = jnp.zeros(_TOKEN, F32)

    outs = pl.pallas_call(
        body, name=name,
        in_specs=[HBM] * n + [ANY] * len(order), out_specs=[SEM, SEM] + [HBM] * n + [pl.BlockSpec(memory_space=pltpu.VMEM)],
        out_shape=[pltpu.SemaphoreType.DMA((n_sems,)), pltpu.SemaphoreType.DMA((n_sems,))]
        + [pltpu.HBM(a.shape, a.dtype) for a in arrays] + [_sds(_TOKEN, F32)],
        input_output_aliases={k: 2 + k for k in range(n)},
        compiler_params=pltpu.CompilerParams(has_side_effects=_DATAFLOW),
    )(*[pltpu.with_memory_space_constraint(a, pltpu.HBM) for a in arrays], *order)
    return outs[0], outs[1], list(outs[2:2 + n]), outs[-1]


def _split_wait(name, started, waits, after):
    send_sems, recv_sems, arrays, _ = started
    n = len(arrays)

    def body(*refs):
        waits(refs[:n], refs[n], refs[n + 1])

    return pl.pallas_call(
        body, name=name,
        in_specs=[HBM] * n + [SEM, SEM, ANY], out_specs=[HBM] * n,
        out_shape=[pltpu.HBM(a.shape, a.dtype) for a in arrays],
        input_output_aliases={k: k for k in range(n)},
        compiler_params=pltpu.CompilerParams(has_side_effects=_DATAFLOW),
    )(*arrays, send_sems, recv_sems, after)


def _wait_both(src, dst, send_sem, recv_sem):
    x, y, c = _mesh_pos()
    cp = _remote(src, dst, send_sem, recv_sem, (x, y, c))
    cp.wait_send()
    cp.wait_recv()


def _pair_exchange_start(parts, tag, after):
    names = list(parts)
    n = len(names)
    lands = [lax.empty(_half_shape(parts[k].shape, _KIND[k]), parts[k].dtype) for k in names]

    def issue(refs, send_sems, recv_sems):
        x, y, c = _mesh_pos()
        for hc in range(2):
            @pl.when(c == hc)
            def _():
                for k in range(n):
                    _remote(_half_view(refs[k], _KIND[names[k]], 1 - hc), refs[n + k], send_sems.at[k], recv_sems.at[k],
                            (x, y, 1 - c)).start()

    return names, _split_start("grad_pair_exchange_start_" + tag, [parts[k] for k in names] + lands, n, issue, after)


def _pair_exchange_wait(pending, tag, after):
    names, started = pending
    n = len(names)

    def waits(refs, send_sems, recv_sems):
        for k in range(n):
            _wait_both(_half_view(refs[k], _KIND[names[k]], 0), refs[n + k], send_sems.at[k], recv_sems.at[k])

    outs = _split_wait("grad_pair_exchange_wait_" + tag, started, waits, after)
    return dict(zip(names, outs[:n])), dict(zip(names, outs[n:]))


def _half_blocks(shape, kind):
    if kind == "stack":
        _, k, w = shape
        tr = k // 2
        nb = 1
        return (N_CHIPS, nb), (1, tr, w), (lambda i, r, s: (i, r, 0)), (lambda i, r, s: (i, s[1] * nb + r, 0))
    k, w = shape
    if kind == "col":
        tr = 256
        nb = k // 2 // tr
        return (nb,), (tr, w), (lambda r, s: (r, 0)), (lambda r, s: (s[1] * nb + r, 0))
    tr = k // N_CHIPS
    return (N_CHIPS,), (tr, w // 2), (lambda r, s: (r, 0)), (lambda r, s: (r, s[1]))


def _pair_add(part, from_sibling, name, pos):
    kind = _KIND[name]
    grid, block, half_map, full_map = _half_blocks(part.shape, kind)

    def body(s_ref, p_ref, q_ref, o_ref):
        o_ref[...] = (p_ref[...].astype(F32) + q_ref[...].astype(F32)).astype(BF16)

    return pl.pallas_call(
        body, name="grad_pair_add_" + name,
        grid_spec=pltpu.PrefetchScalarGridSpec(
            num_scalar_prefetch=1, grid=grid,
            in_specs=[pl.BlockSpec(block, full_map), pl.BlockSpec(block, half_map)],
            out_specs=pl.BlockSpec(block, half_map)),
        out_shape=_sds(from_sibling.shape, BF16),
        compiler_params=_cp(("arbitrary",) * len(grid), 40),
    )(pos, *_hbm(part, from_sibling))


def _chip_exchange_start(sums, tag, after):
    names = list(sums)
    n = len(names)
    lands = [lax.empty((3,) + _part_shape(sums[k].shape, _KIND[k]), sums[k].dtype) for k in names]

    def issue(refs, send_sems, recv_sems):
        x, y, c = _mesh_pos()
        me = 2 * x + y
        for i in range(N_CHIPS):
            xi, yi = i // 2, i % 2
            j = jnp.where(xi != x, jnp.where(yi != y, 2, 0), 1)

            @pl.when(i != me)
            def _():
                for k in range(n):
                    _remote(_shard_view(refs[k], _KIND[names[k]], i), refs[n + k].at[j], send_sems.at[3 * k + j],
                            recv_sems.at[3 * k + j], (xi, yi, c)).start()

    return names, _split_start("grad_chip_exchange_start_" + tag, [sums[k] for k in names] + lands, 3 * n, issue, after)


def _chip_exchange_wait(pending, tag, after):
    names, started = pending
    n = len(names)

    def waits(refs, send_sems, recv_sems):
        for k in range(n):
            for j in range(3):
                _wait_both(_shard_view(refs[k], _KIND[names[k]], 0), refs[n + k].at[j], send_sems.at[3 * k + j], recv_sems.at[3 * k + j])

    return dict(zip(names, _split_wait("grad_chip_exchange_wait_" + tag, started, waits, after)[n:]))


def _allgather_start(stacks, tag, after):
    names = list(stacks)

    def issue(refs, send_sems, recv_sems):
        x, y, c = _mesh_pos()
        me = 2 * x + y
        for k, st in enumerate(refs):
            hr = st.shape[1] // 2
            mine = st.at[me, pl.ds(c * hr, hr), :]
            for j, (cx, cy) in enumerate(_other_chips(x, y)):
                _remote(mine, mine, send_sems.at[3 * k + j], recv_sems.at[3 * k + j], (cx, cy, c)).start()

    return names, _split_start("allgather_start_" + tag, [stacks[k] for k in names], 3 * len(names), issue, after)


def _allgather_wait(pending, tag, after):
    names, started = pending

    def waits(refs, send_sems, recv_sems):
        for k, st in enumerate(refs):
            slot = st.at[0, pl.ds(0, st.shape[1] // 2), :]
            for j in range(3):
                _wait_both(slot, slot, send_sems.at[3 * k + j], recv_sems.at[3 * k + j])

    return dict(zip(names, _split_wait("allgather_wait_" + tag, started, waits, after)))


def _allgather_forward(stacks, tag):
    names = list(stacks)
    n = len(names)

    def body(*refs):
        ins, outs = refs[:n], refs[n:2 * n]
        send_sems, recv_sems = refs[2 * n:]
        x, y, c = _mesh_pos()
        copies = []
        for k in range(n):
            hr = ins[k].shape[1] // 2
            for j, (cx, cy) in enumerate(_other_chips(x, y)):
                chip = 2 * cx + cy
                copies.append(_remote(ins[k].at[chip, pl.ds(c * hr, hr), :], outs[k].at[chip, pl.ds(c * hr, hr), :],
                                      send_sems.at[3 * k + j], recv_sems.at[3 * k + j], (x, y, 1 - c)))
        for cp in copies:
            cp.start()
        for cp in copies:
            cp.wait()

    arrays = [stacks[k] for k in names]
    outs = pl.pallas_call(
        body, name="allgather_forward_" + tag, in_specs=[HBM] * n, out_specs=[HBM] * n,
        input_output_aliases={k: k for k in range(n)},
        out_shape=[_sds(a.shape, a.dtype) for a in arrays],
        scratch_shapes=[pltpu.SemaphoreType.DMA((3 * n,)), pltpu.SemaphoreType.DMA((3 * n,))],
    )(*arrays)
    return dict(zip(names, outs))


def _owner_sum(part, from_sibling, from_chips, name, pos, shard_shape):
    kind = _KIND[name]
    _, pk, pw = from_chips.shape
    if kind == "row":
        tr, nb = pk, 1
        p_spec = pl.BlockSpec((tr, pw), lambda r, s: (s[0], s[1]))
        q_spec = pl.BlockSpec((tr, pw), lambda r, s: (s[0], 0))
        o_spec = pl.BlockSpec((tr, pw), lambda r, s: (0, s[1]))
    else:
        tr = 256
        nb = pk // tr
        if kind == "stack":
            p_spec = pl.BlockSpec((None, tr, pw), lambda r, s: (s[0], s[1] * nb + r, 0))
            q_spec = pl.BlockSpec((None, tr, pw), lambda r, s: (s[0], r, 0))
        else:
            p_spec = pl.BlockSpec((tr, pw), lambda r, s: (s[1] * nb + r, s[0]))
            q_spec = pl.BlockSpec((tr, pw), lambda r, s: (r, s[0]))
        o_spec = pl.BlockSpec((tr, pw), lambda r, s: (s[1] * nb + r, 0))

    def body(s_ref, p_ref, q_ref, r_ref, o_ref):
        acc = p_ref[...].astype(F32) + q_ref[...].astype(F32)
        for j in range(3):
            acc = acc + r_ref[j].astype(F32)
        o_ref[...] = acc

    return pl.pallas_call(
        body, name="grad_owner_sum_" + name,
        grid_spec=pltpu.PrefetchScalarGridSpec(
            num_scalar_prefetch=1, grid=(nb,),
            in_specs=[p_spec, q_spec, pl.BlockSpec((3, tr, pw), lambda r, s: (0, r, 0))],
            out_specs=o_spec),
        out_shape=_sds(shard_shape, F32),
        compiler_params=_cp(("arbitrary",), 32),
    )(pos, *_hbm(part, from_sibling, from_chips))


def _pair_share_start(shards, tag, after):
    names = list(shards)

    def issue(refs, send_sems, recv_sems):
        x, y, c = _mesh_pos()
        for hc in range(2):
            @pl.when(c == hc)
            def _():
                for k, g in enumerate(refs):
                    mine = _region_view(g, _KIND[names[k]], hc)
                    _remote(mine, mine, send_sems.at[k], recv_sems.at[k], (x, y, 1 - c)).start()

    return names, _split_start("grad_pair_share_start_" + tag, [shards[k] for k in names], len(names), issue, after)


def _pair_share_wait(pending, tag, after):
    names, started = pending

    def waits(refs, send_sems, recv_sems):
        for k, g in enumerate(refs):
            region = _region_view(g, _KIND[names[k]], 0)
            _wait_both(region, region, send_sems.at[k], recv_sems.at[k])

    return dict(zip(names, _split_wait("grad_pair_share_wait_" + tag, started, waits, after)))


def _small_exchange_start(slots, after):
    def issue(refs, send_sems, recv_sems):
        x, y, c = _mesh_pos()
        mine = refs[0].at[4 * x + 2 * y + c]
        k = 0
        for px in range(2):
            for py in range(2):
                for pc in range(2):
                    if px + py + pc:
                        peer = (1 - x if px else x, 1 - y if py else y, 1 - c if pc else c)
                        _remote(mine, mine, send_sems.at[k], recv_sems.at[k], peer).start()
                        k += 1

    return _split_start("small_exchange_start", [slots], N_DEV - 1, issue, after)


def _small_exchange_wait(started, after):
    def waits(refs, send_sems, recv_sems):
        slot = refs[0].at[0]
        for k in range(N_DEV - 1):
            _wait_both(slot, slot, send_sems.at[k], recv_sems.at[k])

    return _split_wait("small_exchange_wait", started, waits, after)[0]


def _adam_math(w, g, m, v):
    m = ADAM_B1 * m + (1.0 - ADAM_B1) * g
    v = ADAM_B2 * v + (1.0 - ADAM_B2) * (g * g)
    m_hat = m / (1.0 - ADAM_B1 ** ADAM_STEP)
    v_hat = v / (1.0 - ADAM_B2 ** ADAM_STEP)
    delta = -ADAM_LR * (m_hat / (jnp.sqrt(v_hat) + ADAM_EPS) + ADAM_WD * w)
    return delta, m, v


def _adamw(w, g, m, v, name):
    rows, cols = w.shape
    fits = [t for t in range(SUBLANES, rows, SUBLANES) if rows % t == 0 and t * cols * 4 <= (3 << 19)]
    tr = max(fits) if fits else rows

    def body(w_ref, g_ref, m_ref, v_ref, d_ref, nm_ref, nv_ref, go_ref):
        g = g_ref[...]
        d, nm, nv = _adam_math(w_ref[...], g, m_ref[...], v_ref[...])
        d_ref[...] = d
        nm_ref[...] = nm
        nv_ref[...] = nv
        go_ref[...] = g

    spec = pl.BlockSpec((tr, cols), lambda i: (i, 0))
    return pl.pallas_call(
        body, name=name, grid=(rows // tr,), in_specs=[spec] * 4, out_specs=[spec] * 4,
        out_shape=[_sds(w.shape, F32)] * 4, compiler_params=_cp(("arbitrary",)),
    )(*_hbm(w, g, m, v))


def _small_sum_adamw(gathered, w, m, v):
    names = _SMALL_IN_CALL
    n = len(names)

    def body(*refs):
        a_ref = refs[0]
        w_refs, m_refs, v_refs = refs[1:1 + n], refs[1 + n:1 + 2 * n], refs[1 + 2 * n:1 + 3 * n]
        sum_ref = refs[1 + 3 * n]
        outs = refs[2 + 3 * n:]
        g = a_ref[0]
        for k in range(1, N_DEV):
            g = g + a_ref[k]
        sum_ref[...] = g
        for i, name in enumerate(names):
            r0, nr, nc = _SMALL_AT[name]
            gp = g[r0:r0 + nr, 0:nc]
            d, nm, nv = _adam_math(w_refs[i][...], gp, m_refs[i][...], v_refs[i][...])
            for k, val in enumerate((gp, d, nm, nv)):
                outs[4 * i + k][...] = val

    shapes = [w[k].shape for k in names]
    res = pl.pallas_call(
        body, name="small_sum_adamw",
        out_shape=[_sds((SMALL_ROWS, D_MODEL), F32)] + [_sds(s, F32) for s in shapes for _ in range(4)],
    )(gathered, *[w[k] for k in names], *[m[k] for k in names], *[v[k] for k in names])
    return res[0], {k: tuple(res[1 + 4 * i:5 + 4 * i]) for i, k in enumerate(names)}


_NAMES = ("g_mix", "w_in", "g_sgu", "w_s", "b_s", "sinks", "rel_bias", "w_pa", "w_pb", "w_out",
          "g_ffn", "w_up", "w_conv", "b_conv", "w_down", "g_final")

def kernel(x, g_mix, w_in, g_sgu, w_s, b_s, sinks, rel_bias, w_pa, w_pb, w_out, g_ffn, w_up, w_conv, b_conv, w_down, g_final, loss_target, m_g_mix, m_w_in, m_g_sgu, m_w_s, m_b_s, m_sinks, m_rel_bias, m_w_pa, m_w_pb, m_w_out, m_g_ffn, m_w_up, m_w_conv, m_b_conv, m_w_down, m_g_final, v_g_mix, v_w_in, v_g_sgu, v_w_s, v_b_s, v_sinks, v_rel_bias, v_w_pa, v_w_pb, v_w_out, v_g_ffn, v_w_up, v_w_conv, v_b_conv, v_w_down, v_g_final):
    w = dict(g_mix=g_mix, w_in=w_in, g_sgu=g_sgu, w_s=w_s, b_s=b_s, sinks=sinks, rel_bias=rel_bias, w_pa=w_pa, w_pb=w_pb,
             w_out=w_out, g_ffn=g_ffn, w_up=w_up, w_conv=w_conv, b_conv=b_conv, w_down=w_down, g_final=g_final)
    m = dict(g_mix=m_g_mix, w_in=m_w_in, g_sgu=m_g_sgu, w_s=m_w_s, b_s=m_b_s, sinks=m_sinks, rel_bias=m_rel_bias, w_pa=m_w_pa,
             w_pb=m_w_pb, w_out=m_w_out, g_ffn=m_g_ffn, w_up=m_w_up, w_conv=m_w_conv, b_conv=m_b_conv, w_down=m_w_down,
             g_final=m_g_final)
    v = dict(g_mix=v_g_mix, w_in=v_w_in, g_sgu=v_g_sgu, w_s=v_w_s, b_s=v_b_s, sinks=v_sinks, rel_bias=v_rel_bias, w_pa=v_w_pa,
             w_pb=v_w_pb, w_out=v_w_out, g_ffn=v_g_ffn, w_up=v_w_up, w_conv=v_w_conv, b_conv=v_b_conv, w_down=v_w_down,
             g_final=v_g_final)
    xi, yi, ci = _mesh_pos()
    me = 2 * xi + yi

    shard = {n: w[n][0] for n in _BIG}
    shard_shapes = {n: shard[n].shape for n in _BIG}
    wc_shard = w["w_conv"][0]
    wc_pad = jnp.pad(wc_shard, ((0, 5), (0, 0)))
    own = {n: _own_slot(shard[n].astype(BF16), N_CHIPS, me) for n in _BIG}
    stacks, wc_all = _allgather_weights({"w_in": own["w_in"]}, _own_slot(wc_pad, N_CHIPS, me))
    proj_gather = _allgather_start({n: own[n] for n in _MIXER[1:]}, "proj", stacks["w_in"])
    ffn_gather = _allgather_start({n: own[n] for n in _FFN}, "ffn", proj_gather[1][-1])
    w_conv_full = jnp.concatenate([wc_all[i, :3] for i in range(N_CHIPS)], axis=1)
    w_in_full = _unstack_w_in(stacks["w_in"])
    pos = jnp.stack([me, ci])

    def proj_weights(done):
        st = _allgather_forward(_allgather_wait(proj_gather, "proj", done), "proj")
        return st["w_pa"], st["w_pb"], st["w_out"].reshape(D_MODEL, D_MODEL)

    def ffn_weights(done):
        st = _allgather_forward(_allgather_wait(ffn_gather, "ffn", done), "ffn")
        return st["w_up"], st["w_down"].reshape(D_FF, D_MODEL)

    groups = {}

    def stage1(group, parts):
        groups[group] = dict(parts=parts, pair=_pair_exchange_start(parts, group, None))
        return groups[group]["pair"][1][-1]

    def stage2(group, after, order_after):
        g = groups[group]
        g["parts"], g["sib"] = _pair_exchange_wait(g["pair"], group, after)
        g["chip"] = _chip_exchange_start({n: _pair_add(g["parts"][n], g["sib"][n], n, pos) for n in g["parts"]}, group, order_after)
        return g["chip"][1][-1]

    def stage3(group, after, order_after):
        g = groups[group]
        got = _chip_exchange_wait(g["chip"], group, after)
        g["share"] = _pair_share_start(
            {n: _owner_sum(g["parts"][n], g["sib"][n], got[n], n, pos, shard_shapes[n]) for n in g["parts"]}, group, order_after)
        return g["share"][1][-1]

    grads, deltas, new_m, new_v = {}, {}, {}, {}

    def stage4(group, after):
        g_shard = _pair_share_wait(groups[group]["share"], group, after)
        last = None
        for n in g_shard:
            g = _tie(g_shard[n], last)
            if n == "w_in":
                d, nm, nv, gt = _adamw(shard[n].T, g.T, m[n][0].T, v[n][0].T, "adamw_" + n)
                grads[n], deltas[n], new_m[n], new_v[n] = gt.T[None], d.T[None], nm.T[None], nv.T[None]
            else:
                d, nm, nv, go = _adamw(shard[n], g, m[n][0], v[n][0], "adamw_" + n)
                grads[n], deltas[n], new_m[n], new_v[n] = go[None], d[None], nm[None], nv[None]
            last = nv
        return last

    def on_grads(group, parts):
        token = stage1(group, parts)
        some = next(iter(parts.values()))
        if group == "proj":
            token = stage2("ffn", some, token)
        if group == "in":
            token = stage2("proj", some, token)
            token = stage3("ffn", some, token)
            token = stage2("in", token, token)
        return token

    loss, grad_x, small, big = _local_step(
        x, loss_target, w["g_mix"], w["g_sgu"], w["w_s"][0], w["b_s"][0], w["sinks"], w["rel_bias"], w["g_ffn"],
        w["b_conv"], w["g_final"], w_in_full, w_conv_full, proj_weights, ffn_weights, on_grads, ffn_gather[1][-1])

    small["loss"] = loss
    small_gather = _small_exchange_start(_own_slot(_pack_small(small), N_DEV, 2 * me + ci), grad_x)
    token = stage3("proj", grad_x, small_gather[-1])
    done = stage4("ffn", token)
    done = stage4("proj", done)
    token = stage3("in", done, None)
    all_small = _small_exchange_wait(small_gather, token)
    two_d = {n: (lambda a, n=n: a.reshape(_SMALL_AT[n][1:])) for n in _SMALL_IN_CALL}
    s_sum, s_out = _small_sum_adamw(all_small, *[{n: two_d[n](p[n]) for n in _SMALL_IN_CALL} for p in (w, m, v)])
    stage4("in", all_small)
    for n in _SMALL_IN_CALL:
        grads[n], deltas[n], new_m[n], new_v[n] = [a.reshape(w[n].shape) for a in s_out[n]]

    def rows(n):
        r0, nr, _ = _SMALL_AT[n]
        return s_sum[r0:r0 + nr]

    wcols = wc_shard.shape[1]
    g_wc = lax.dynamic_slice(_unwide(rows("w_conv"), 3), (0, me * wcols), (3, wcols))
    d, nm, nv, _ = _adamw(wc_shard, g_wc, m["w_conv"][0], v["w_conv"][0], "adamw_w_conv")
    grads["w_conv"], deltas["w_conv"], new_m["w_conv"], new_v["w_conv"] = g_wc[None], d[None], nm[None], nv[None]
    d, nm, nv, go = _adamw(w["b_conv"], _unwide(rows("b_conv"), 1), m["b_conv"], v["b_conv"], "adamw_b_conv")
    grads["b_conv"], deltas["b_conv"], new_m["b_conv"], new_v["b_conv"] = go, d, nm, nv
    flat_s = (A_GROUPS * CHUNK, CHUNK)
    d, nm, nv, go = _adamw(w["w_s"].reshape(flat_s), rows("w_s").reshape(flat_s), m["w_s"].reshape(flat_s),
                           v["w_s"].reshape(flat_s), "adamw_w_s")
    grads["w_s"], deltas["w_s"], new_m["w_s"], new_v["w_s"] = [a.reshape(w["w_s"].shape) for a in (go, d, nm, nv)]

    return (s_sum[0, 0], grad_x, *[grads[n] for n in _NAMES], *[deltas[n] for n in _NAMES],
            *[new_m[n] for n in _NAMES], *[new_v[n] for n in _NAMES])
```

```python
import functools

import numpy as np
import jax
import jax.numpy as jnp
from jax import lax
from jax.experimental import pallas as pl
from jax.experimental.pallas import tpu as pltpu

F32 = jnp.float32
BF16 = jnp.bfloat16

D_MODEL = 1024
CHUNK = 128
A_GROUPS = 4
A_WIDTH = 512
N_HEADS = 8
HEAD_DIM = 64
Q_DIM = 512
KV_DIM = 128
N_BUCKETS = 32
MAX_DISTANCE = 128
D_FF = 2816
EPS = 1e-6
NEG_INF = -1e30
G_DIM = 2 * D_MODEL
A_DIM = 2 * A_WIDTH
B_DIM = Q_DIM + 2 * KV_DIM
LANES = 128
SUBLANES = 8
ROW_TILE = 512
WIDE_ROW_TILE = 256
COL_CHUNK = 512
GRAD_ROW_TILE = 512
BF16_ROWS = 16
N_CHIPS = 4
N_DEV = 8

ADAM_LR = 0.001
ADAM_B1 = 0.9
ADAM_B2 = 0.999
ADAM_EPS = 1e-08
ADAM_WD = 0.01
ADAM_STEP = 10

MESH = pl.DeviceIdType.MESH
_GELU_C = 0.7978845608028654
_GELU_A = 0.044715


def _cp(sem=None, vmem_mb=None):
    kw = {}
    if sem is not None:
        kw["dimension_semantics"] = sem
    if vmem_mb is not None:
        kw["vmem_limit_bytes"] = vmem_mb << 20
    return pltpu.CompilerParams(**kw)


def _dot(a, b):
    return jnp.dot(a, b, preferred_element_type=F32)


def _dot_nt(a, b):
    return lax.dot_general(a, b, (((1,), (1,)), ((), ())), preferred_element_type=F32)


def _dot_tn(a, b):
    return lax.dot_general(a, b, (((0,), (0,)), ((), ())), preferred_element_type=F32)


def _rms_r(x):
    return lax.rsqrt(jnp.mean(x * x, axis=-1, keepdims=True) + EPS)


def _rms_bwd(dh, n, r, g):
    dn = dh * g
    return r * (dn - n * jnp.mean(dn * n, axis=-1, keepdims=True))


def _gelu(x):
    t = jnp.tanh(_GELU_C * (x + _GELU_A * (x * x * x)))
    return 0.5 * x * (1.0 + t), t


def _gelu_grad(x, t):
    return 0.5 * (1.0 + t) + 0.5 * x * (1.0 - t * t) * (_GELU_C * (1.0 + 3.0 * _GELU_A * x * x))


def _sigmoid(x):
    return 1.0 / (1.0 + jnp.exp(-x))


def _tie(x, dep):
    return x if dep is None else lax.optimization_barrier((x, dep))[0]


def _row(tm, w):
    return pl.BlockSpec((tm, w), lambda i: (i, 0))


def _full(shape):
    nd = len(shape)
    return pl.BlockSpec(tuple(shape), lambda *_: (0,) * nd)


def _resident(shape):
    nd = len(shape)
    return pl.BlockSpec(tuple(shape), lambda *_: (0,) * nd, pipeline_mode=pl.Buffered(1))


def _sds(shape, dtype):
    return jax.ShapeDtypeStruct(tuple(shape), dtype)


def _hbm(*arrays):
    return [pltpu.with_memory_space_constraint(a, pltpu.HBM) for a in arrays]


HBM = pl.BlockSpec(memory_space=pltpu.HBM)
ANY = pl.BlockSpec(memory_space=pl.ANY)
SEM = pl.BlockSpec(memory_space=pltpu.SEMAPHORE)


def _band_buckets():
    i = np.arange(CHUNK)[:, None]
    j = np.arange(2 * CHUNK)[None, :]
    dist = i + CHUNK - j
    valid = (dist >= 0) & (dist < CHUNK)
    d = np.clip(dist, 0, None)
    max_exact = N_BUCKETS // 2
    large = max_exact + (np.log(np.maximum(d, 1) / max_exact) / np.log(MAX_DISTANCE / max_exact)
                         * (N_BUCKETS - max_exact)).astype(np.int32)
    large = np.minimum(large, N_BUCKETS - 1)
    buckets = np.where(d < max_exact, d, large).astype(np.int32)
    return np.where(valid, buckets, -1).astype(np.int32)


_A_COLS = slice(0, A_DIM)
_B_COLS = slice(A_DIM, A_DIM + B_DIM)
_G_COLS = slice(A_DIM + B_DIM, A_DIM + B_DIM + G_DIM)


def _inproj(x2, g_mix, w_in, tm, after=None):
    T = x2.shape[0]
    order = [] if after is None else [after]

    def body(*refs):
        x_ref, g_ref, w_ref = refs[:3]
        pg_ref, pa_ref, pb_ref, h_ref = refs[3 + len(order):]
        x = x_ref[...]
        h = (x * _rms_r(x) * g_ref[...]).astype(BF16)
        h_ref[...] = h
        pg_ref[...] = _dot(h, w_ref[:, _G_COLS]).astype(BF16)
        pa_ref[...] = _dot(h, w_ref[:, _A_COLS]).astype(BF16)
        pb_ref[...] = _dot(h, w_ref[:, _B_COLS]).astype(BF16)

    return pl.pallas_call(
        body, name="inproj", grid=(T // tm,),
        in_specs=[_row(tm, D_MODEL), _full(g_mix.shape), _resident(w_in.shape)] + [ANY] * len(order),
        out_specs=[_row(tm, G_DIM), _row(tm, A_DIM), _row(tm, B_DIM), _row(tm, D_MODEL)],
        out_shape=[_sds((T, G_DIM), BF16), _sds((T, A_DIM), BF16), _sds((T, B_DIM), BF16), _sds((T, D_MODEL), BF16)],
        compiler_params=_cp(("arbitrary",), 48),
    )(*_hbm(x2, g_mix, w_in), *order)


def _sgu_parts(p, g):
    pu = p[:, :A_WIDTH]
    pv = p[:, A_WIDTH:]
    u, tu = _gelu(pu)
    vv, tv = _gelu(pv)
    rv = _rms_r(vv)
    vn = (vv * rv * g).astype(BF16)
    return pu, pv, u, tu, vv, tv, rv, vn


def _tril():
    r = lax.broadcasted_iota(jnp.int32, (CHUNK, CHUNK), 0)
    c = lax.broadcasted_iota(jnp.int32, (CHUNK, CHUNK), 1)
    return r >= c


def _sgu_fwd(proj_a, g_sgu, w_s, b_st, tm):
    T = proj_a.shape[0]

    def body(p_ref, g_ref, ws_ref, bs_ref, y_ref):
        tril = _tril()
        _, _, u, _, _, _, _, vn = _sgu_parts(p_ref[...].astype(F32), g_ref[...])
        for gi in range(A_GROUPS):
            wm = jnp.where(tril, ws_ref[gi], 0.0).astype(BF16)
            bcol = bs_ref[:, gi:gi + 1]
            cs = slice(gi * CHUNK, (gi + 1) * CHUNK)
            for c in range(tm // CHUNK):
                rs = slice(c * CHUNK, (c + 1) * CHUNK)
                s = _dot(wm, vn[rs, cs]) + bcol
                y_ref[rs, cs] = (u[rs, cs] * s).astype(BF16)

    return pl.pallas_call(
        body, name="sgu_fwd", grid=(T // tm,),
        in_specs=[_row(tm, A_DIM), _full(g_sgu.shape), _full(w_s.shape), _full(b_st.shape)],
        out_specs=_row(tm, A_WIDTH), out_shape=_sds((T, A_WIDTH), BF16),
        compiler_params=_cp(("arbitrary",)),
    )(*_hbm(proj_a, g_sgu, w_s, b_st))


HEAD_ROWS = N_HEADS * CHUNK


def _head_rows(h):
    return slice(h * CHUNK, (h + 1) * CHUNK)


def _attn_setup(bias_scr, sink_scr, kvar_scr, qkv_ref, bk_ref, rel_ref, sink_ref):
    bk = bk_ref[...]
    for h in range(N_HEADS):
        acc = jnp.full((CHUNK, 2 * CHUNK), NEG_INF, F32)
        for b in range(N_BUCKETS):
            acc = jnp.where(bk == b, rel_ref[b, h], acc)
        bias_scr[_head_rows(h), :] = acc
        sink_scr[_head_rows(h), :] = jnp.full((CHUNK, LANES), sink_ref[0, h], F32)
    seq = qkv_ref.shape[0]
    rows_per = 2 * CHUNK
    for is_v in range(2):
        c0 = Q_DIM + is_v * KV_DIM
        for r in range(seq // rows_per):
            rs = slice(r * rows_per, (r + 1) * rows_per)
            a = qkv_ref[rs, c0:c0 + KV_DIM].astype(F32)
            lane = lax.broadcasted_iota(jnp.int32, a.shape, 1)
            lo = jnp.where(lane < HEAD_DIM, a, 0.0)
            hi = jnp.where(lane >= HEAD_DIM, a, 0.0)
            kvar_scr[4 * is_v + 0, rs, :] = lo.astype(BF16)
            kvar_scr[4 * is_v + 1, rs, :] = pltpu.roll(lo, HEAD_DIM, 1).astype(BF16)
            kvar_scr[4 * is_v + 2, rs, :] = pltpu.roll(hi, HEAD_DIM, 1).astype(BF16)
            kvar_scr[4 * is_v + 3, rs, :] = hi.astype(BF16)


def _rowsum(a, ones):
    hi = a.astype(BF16)
    lo = (a - hi.astype(F32)).astype(BF16)
    return _dot(hi, ones) + _dot(lo, ones)


def _both(a):
    return jnp.concatenate([a, a], axis=1)


def _attn_probs(qkv_ref, r0, n, kv, bias_scr, sink_scr, ones):
    s = jnp.concatenate([_dot_nt(qkv_ref[pl.ds(r0, CHUNK), (h // 2) * LANES:(h // 2 + 1) * LANES], kv[h // 4][h % 2])
                         for h in range(N_HEADS)], axis=0)
    s = s * (HEAD_DIM ** -0.5) + bias_scr[...]
    col = lax.broadcasted_iota(jnp.int32, s.shape, 1)
    s = jnp.where((col < CHUNK) & (n == 0), NEG_INF, s)
    sink = sink_scr[...]
    m = jnp.maximum(jnp.max(s, axis=-1, keepdims=True), sink)
    p = jnp.exp(s - _both(m))
    es = jnp.exp(sink - m)
    inv = 1.0 / (_rowsum(p, ones) + es)
    return p * _both(inv), es * inv


def _attn_block_inputs(kvar_scr, n):
    r0 = pl.multiple_of(n * CHUNK, CHUNK)
    rp = pl.multiple_of(jnp.maximum(n - 1, 0) * CHUNK, CHUNK)

    def both(idx):
        return jnp.concatenate([kvar_scr[idx, pl.ds(rp, CHUNK), :], kvar_scr[idx, pl.ds(r0, CHUNK), :]], axis=0)

    kv = ((both(0), both(1)), (both(2), both(3)))
    vv = ((both(4), both(5)), (both(6), both(7)))
    return r0, kv, vv


def _attn_fwd(proj_b, sinks, rel_bias, n_seq, seq):
    nb = seq // CHUNK
    bk = jnp.asarray(_band_buckets())

    def body(qkv_ref, bk_ref, rel_ref, sink_ref, o_ref, bias_scr, sink_scr, kvar_scr):
        _attn_setup(bias_scr, sink_scr, kvar_scr, qkv_ref, bk_ref, rel_ref, sink_ref)
        ones = jnp.ones((2 * CHUNK, LANES), BF16)

        def blk(n, carry):
            r0, kv, vv = _attn_block_inputs(kvar_scr, n)
            prob, _ = _attn_probs(qkv_ref, r0, n, kv, bias_scr, sink_scr, ones)
            pb = prob.astype(BF16)
            for pr in range(N_HEADS // 2):
                acc = _dot(pb[_head_rows(2 * pr)], vv[pr // 2][0]) + _dot(pb[_head_rows(2 * pr + 1)], vv[pr // 2][1])
                o_ref[pl.ds(r0, CHUNK), pr * LANES:(pr + 1) * LANES] = acc.astype(BF16)
            return carry

        lax.fori_loop(0, nb, blk, 0)

    smem = pl.BlockSpec(memory_space=pltpu.SMEM)
    return pl.pallas_call(
        body, name="attn_fwd", grid=(n_seq,),
        in_specs=[_row(seq, B_DIM), _full(bk.shape), smem, smem],
        out_specs=_row(seq, Q_DIM), out_shape=_sds((n_seq * seq, Q_DIM), BF16),
        scratch_shapes=[pltpu.VMEM((HEAD_ROWS, 2 * CHUNK), F32), pltpu.VMEM((HEAD_ROWS, LANES), F32),
                        pltpu.VMEM((8, seq, KV_DIM), BF16)],
        compiler_params=_cp(("arbitrary",), 40),
    )(*_hbm(proj_b, bk), rel_bias, sinks)


def _dot_stacked(a, w_ref):
    return jnp.concatenate([_dot(a, w_ref[i]) for i in range(N_CHIPS)], axis=1)


def _dot_nt_stacked(a, w_ref):
    w = w_ref.shape[2]
    acc = _dot_nt(a[:, :w], w_ref[0])
    for i in range(1, N_CHIPS):
        acc = acc + _dot_nt(a[:, i * w:(i + 1) * w], w_ref[i])
    return acc


def _merge_fwd(x2, y_a, y_b, proj_g, w_pa, w_pb, w_out, tm):
    T = x2.shape[0]

    def body(x_ref, ya_ref, yb_ref, g_ref, wpa_ref, wpb_ref, wo_ref, x1_ref, mg_ref):
        g = g_ref[...].astype(F32)
        pa = _dot_stacked(ya_ref[...], wpa_ref)
        pb = _dot_stacked(yb_ref[...], wpb_ref)
        merged = (_sigmoid(g[:, :D_MODEL]) * pa + _sigmoid(g[:, D_MODEL:]) * pb).astype(BF16)
        mg_ref[...] = merged
        x1_ref[...] = x_ref[...] + _dot(merged, wo_ref[...])

    return pl.pallas_call(
        body, name="merge_fwd", grid=(T // tm,),
        in_specs=[_row(tm, D_MODEL), _row(tm, A_WIDTH), _row(tm, Q_DIM), _row(tm, G_DIM),
                  _resident(w_pa.shape), _resident(w_pb.shape), _resident(w_out.shape)],
        out_specs=[_row(tm, D_MODEL), _row(tm, D_MODEL)],
        out_shape=[_sds((T, D_MODEL), F32), _sds((T, D_MODEL), BF16)],
        compiler_params=_cp(("arbitrary",), 40),
    )(*_hbm(x2, y_a, y_b, proj_g, w_pa, w_pb, w_out))


def _upproj(x1, g_ffn, w_up, w_conv, b_conv, tm, seq):
    T = x1.shape[0]
    cw = w_up.shape[2]
    tiles_per_seq = seq // tm

    def body(x_ref, g_ref, w_ref, wc_ref, bc_ref, u_ref, h_ref, gate_ref, val_ref, tail_scr):
        at_start = (pl.program_id(0) % tiles_per_seq) == 0
        x = x_ref[...]
        h = (x * _rms_r(x) * g_ref[...]).astype(BF16)
        h_ref[...] = h
        for i in range(N_CHIPS):
            cs = slice(i * cw, (i + 1) * cw)
            u = _dot(h, w_ref[i])
            u_ref[:, cs] = u.astype(BF16)
            hl = jnp.where(at_start, 0.0, tail_scr[SUBLANES - 2:SUBLANES, cs])
            tail_scr[:, cs] = u[tm - SUBLANES:]
            up = _conv_out((u, _shift_down(u, hl, 1), _shift_down(u, hl, 2)), wc_ref[:, cs], bc_ref[:, cs])
            out_ref = gate_ref if i < N_CHIPS // 2 else val_ref
            out_ref[:, (i % 2) * cw:(i % 2 + 1) * cw] = up.astype(BF16)

    return pl.pallas_call(
        body, name="upproj", grid=(T // tm,),
        in_specs=[_row(tm, D_MODEL), _full(g_ffn.shape), _resident(w_up.shape), _full(w_conv.shape), _full(b_conv.shape)],
        out_specs=[_row(tm, 2 * D_FF), _row(tm, D_MODEL), _row(tm, D_FF), _row(tm, D_FF)],
        out_shape=[_sds((T, 2 * D_FF), BF16), _sds((T, D_MODEL), BF16), _sds((T, D_FF), BF16), _sds((T, D_FF), BF16)],
        scratch_shapes=[pltpu.VMEM((SUBLANES, 2 * D_FF), F32)],
        compiler_params=_cp(("arbitrary",), 56),
    )(*_hbm(x1, g_ffn, w_up, w_conv, b_conv))


def _shift_down(u, halo, k):
    rolled = pltpu.roll(u, k, 0)
    head = rolled[:SUBLANES]
    row = lax.broadcasted_iota(jnp.int32, head.shape, 0)
    if k == 1:
        head = jnp.where(row == 0, halo[1:2], head)
    else:
        head = jnp.where(row == 0, halo[0:1], jnp.where(row == 1, halo[1:2], head))
    return jnp.concatenate([head, rolled[SUBLANES:]], axis=0)


def _shift_up(d, halo, k):
    tm = d.shape[0]
    rolled = pltpu.roll(d, tm - k, 0)
    tail = rolled[tm - SUBLANES:]
    row = lax.broadcasted_iota(jnp.int32, tail.shape, 0)
    if k == 1:
        tail = jnp.where(row == SUBLANES - 1, halo[0:1], tail)
    else:
        tail = jnp.where(row == SUBLANES - 2, halo[0:1], jnp.where(row == SUBLANES - 1, halo[1:2], tail))
    return jnp.concatenate([rolled[:tm - SUBLANES], tail], axis=0)


def _conv_out(taps, wc, bc):
    u, u1, u2 = taps
    return wc[0:1] * u2 + wc[1:2] * u1 + wc[2:3] * u + bc


def _ffn_down_loss(gate, val, x1, target, w_down, g_final, tm):
    T = x1.shape[0]
    half = D_FF // 2

    def body(gt_ref, vl_ref, x1_ref, t_ref, wd_ref, g_ref, dx2_ref, loss_ref, gg_ref):
        i = pl.program_id(0)
        acc = jnp.zeros((tm, D_MODEL), F32)
        for j in range(2):
            gc = slice(j * half, (j + 1) * half)
            gate = gt_ref[:, gc].astype(F32)
            act = (gate * _sigmoid(gate) * vl_ref[:, gc].astype(F32)).astype(BF16)
            acc = acc + _dot(act, wd_ref[gc, :])
        x2 = x1_ref[...] + acc
        r = _rms_r(x2)
        n = x2 * r
        g = g_ref[...]
        diff = n * g - t_ref[...]
        dy = diff * (1.0 / D_MODEL)
        dx2_ref[...] = _rms_bwd(dy, n, r, g)

        @pl.when(i == 0)
        def _():
            loss_ref[...] = jnp.zeros_like(loss_ref)
            gg_ref[...] = jnp.zeros_like(gg_ref)

        loss_ref[...] += 0.5 * jnp.sum(jnp.mean(diff * diff, axis=-1, keepdims=True), axis=0, keepdims=True)
        gg_ref[...] += jnp.sum(dy * n, axis=0, keepdims=True)

    return pl.pallas_call(
        body, name="ffn_down_loss", grid=(T // tm,),
        in_specs=[_row(tm, D_FF), _row(tm, D_FF), _row(tm, D_MODEL), _row(tm, D_MODEL),
                  _resident(w_down.shape), _full(g_final.shape)],
        out_specs=[_row(tm, D_MODEL), _full((1, 1)), _full((1, D_MODEL))],
        out_shape=[_sds((T, D_MODEL), F32), _sds((1, 1), F32), _sds((1, D_MODEL), F32)],
        compiler_params=_cp(("arbitrary",), 48),
    )(*_hbm(gate, val, x1, target, w_down, g_final))


def _ffn_bwd_act(gate, val, dx2, w_down, tm):
    T = dx2.shape[0]
    half = D_FF // 2
    nt = T // tm

    def body(g_ref, v_ref, dx_ref, wd_ref, dg_ref, dv_ref, gwd_out, gbg_ref, gbv_ref, gwd_ref):
        i = pl.program_id(1)

        @pl.when(i == 0)
        def _():
            for r in (gwd_ref, gbg_ref, gbv_ref):
                r[...] = jnp.zeros_like(r)

        dx = dx_ref[...].astype(BF16)
        for c0 in range(0, half, COL_CHUNK):
            cs = slice(c0, min(c0 + COL_CHUNK, half))
            gate = g_ref[:, cs].astype(F32)
            val = v_ref[:, cs].astype(F32)
            sg = _sigmoid(gate)
            silu = gate * sg
            d_act = _dot_nt(dx, wd_ref[cs, :])
            d_val = d_act * silu
            d_gate = d_act * val * (sg * (1.0 + gate * (1.0 - sg)))
            dg_ref[:, cs] = d_gate.astype(BF16)
            dv_ref[:, cs] = d_val.astype(BF16)
            gwd_ref[cs, :] += _dot_tn((silu * val).astype(BF16), dx)
            gbg_ref[:, cs] += jnp.sum(d_gate, axis=0, keepdims=True)
            gbv_ref[:, cs] += jnp.sum(d_val, axis=0, keepdims=True)

        @pl.when(i == nt - 1)
        def _():
            gwd_out[...] = gwd_ref[...].astype(BF16)

    tile = pl.BlockSpec((tm, half), lambda j, i: (i, j))
    vec = pl.BlockSpec((1, half), lambda j, i: (0, j))
    wrows = pl.BlockSpec((half, D_MODEL), lambda j, i: (j, 0))
    return pl.pallas_call(
        body, name="ffn_bwd_act", grid=(2, nt),
        in_specs=[tile, tile, pl.BlockSpec((tm, D_MODEL), lambda j, i: (i, 0)), wrows],
        out_specs=[tile, tile, wrows, vec, vec],
        out_shape=[_sds((T, D_FF), BF16), _sds((T, D_FF), BF16), _sds((D_FF, D_MODEL), BF16),
                   _sds((1, D_FF), F32), _sds((1, D_FF), F32)],
        scratch_shapes=[pltpu.VMEM((half, D_MODEL), F32)],
        compiler_params=_cp(("arbitrary", "arbitrary"), 56),
    )(*_hbm(gate, val, dx2, w_down))


def _ffn_bwd_up(d_gate, d_val, upre, dx2, x1, g_ffn, w_conv, w_up, tm, seq):
    T = dx2.shape[0]
    tiles_per_seq = seq // tm
    k16 = tm // BF16_ROWS
    n16 = T // BF16_ROWS
    cw = D_FF // 2

    def body(dg_ref, dv_ref, hg_ref, hv_ref, u_ref, dx2_ref, x1_ref, g_ref, wc_ref, wu_ref, du_ref, dx1_ref, gg_ref, gwc_ref):
        i = pl.program_id(0)
        at_end = (i % tiles_per_seq) == tiles_per_seq - 1

        @pl.when(i == 0)
        def _():
            gg_ref[...] = jnp.zeros_like(gg_ref)
            gwc_ref[...] = jnp.zeros_like(gwc_ref)

        dh = jnp.zeros((tm, D_MODEL), F32)
        for j in range(4):
            src, hsrc = (dg_ref, hg_ref) if j < 2 else (dv_ref, hv_ref)
            ls = slice((j % 2) * cw, (j % 2 + 1) * cw)
            cs = slice(j * cw, (j + 1) * cw)
            d = src[:, ls].astype(F32)
            hl = hsrc[:, ls].astype(F32)[0:2]
            hl = jnp.where(at_end, 0.0, hl)
            wc = wc_ref[:, cs]
            d1 = _shift_up(d, hl, 1)
            d2 = _shift_up(d, hl, 2)
            du = (wc[2:3] * d + wc[1:2] * d1 + wc[0:1] * d2).astype(BF16)
            du_ref[:, cs] = du
            dh = dh + _dot_nt(du, wu_ref[j])
            u = u_ref[:, cs].astype(F32)
            gwc_ref[0:1, cs] += jnp.sum(d2 * u, axis=0, keepdims=True)
            gwc_ref[1:2, cs] += jnp.sum(d1 * u, axis=0, keepdims=True)
            gwc_ref[2:3, cs] += jnp.sum(d * u, axis=0, keepdims=True)
        x = x1_ref[...]
        r = _rms_r(x)
        n = x * r
        dx1_ref[...] = dx2_ref[...] + _rms_bwd(dh, n, r, g_ref[...])
        gg_ref[...] += jnp.sum(dh * n, axis=0, keepdims=True)

    nxt = pl.BlockSpec((BF16_ROWS, D_FF), lambda i: (jnp.minimum((i + 1) * k16, n16 - 1), 0))
    return pl.pallas_call(
        body, name="ffn_bwd_up", grid=(T // tm,),
        in_specs=[_row(tm, D_FF), _row(tm, D_FF), nxt, nxt, _row(tm, 2 * D_FF), _row(tm, D_MODEL), _row(tm, D_MODEL),
                  _full(g_ffn.shape), _full(w_conv.shape), _resident(w_up.shape)],
        out_specs=[_row(tm, 2 * D_FF), _row(tm, D_MODEL), _full((1, D_MODEL)), _full((3, 2 * D_FF))],
        out_shape=[_sds((T, 2 * D_FF), BF16), _sds((T, D_MODEL), F32), _sds((1, D_MODEL), F32), _sds((3, 2 * D_FF), F32)],
        compiler_params=_cp(("arbitrary",), 56),
    )(*_hbm(d_gate, d_val, d_gate, d_val, upre, dx2, x1, g_ffn, w_conv, w_up))


def _matmul_tn(a, b, tn, tk, name):
    T, M = a.shape
    N = b.shape[1]
    nk = T // tk

    def body(a_ref, b_ref, o_ref, acc_ref):
        k = pl.program_id(1)

        @pl.when(k == 0)
        def _():
            acc_ref[...] = jnp.zeros_like(acc_ref)

        acc_ref[...] += _dot_tn(a_ref[...], b_ref[...])

        @pl.when(k == nk - 1)
        def _():
            o_ref[...] = acc_ref[...].astype(BF16)

    return pl.pallas_call(
        body, name=name, grid=(N // tn, nk),
        in_specs=[pl.BlockSpec((tk, M), lambda j, k: (k, 0)), pl.BlockSpec((tk, tn), lambda j, k: (k, j))],
        out_specs=pl.BlockSpec((M, tn), lambda j, k: (0, j)), out_shape=_sds((M, N), BF16),
        scratch_shapes=[pltpu.VMEM((M, tn), F32)],
        compiler_params=_cp(("arbitrary", "arbitrary"), 48),
    )(*_hbm(a, b))


def _merge_bwd(dx1, merged, y_a, y_b, proj_g, w_pa, w_pb, w_out, tm, after=None):
    T = dx1.shape[0]

    nt = T // tm
    pshape = (A_WIDTH, D_MODEL)
    order = [] if after is None else [after]

    def body(*refs):
        dx_ref, mg_ref, ya_ref, yb_ref, g_ref, wpa_ref, wpb_ref, wo_ref = refs[:8]
        dg_ref, dya_ref, dyb_ref, gwo_out, gwpa_out, gwpb_out, gwo_ref, gwpa_ref, gwpb_ref = refs[8 + len(order):]
        i = pl.program_id(0)
        dx = dx_ref[...].astype(BF16)
        dm = _dot_nt(dx, wo_ref[...])
        g = g_ref[...].astype(F32)
        ya = ya_ref[...]
        yb = yb_ref[...]
        pa = _dot_stacked(ya, wpa_ref)
        pb = _dot_stacked(yb, wpb_ref)
        sa = _sigmoid(g[:, :D_MODEL])
        sb = _sigmoid(g[:, D_MODEL:])
        dpa = (dm * sa).astype(BF16)
        dpb = (dm * sb).astype(BF16)
        dg_ref[:, :D_MODEL] = (dm * pa * (sa * (1.0 - sa))).astype(BF16)
        dg_ref[:, D_MODEL:] = (dm * pb * (sb * (1.0 - sb))).astype(BF16)
        dya_ref[...] = _dot_nt_stacked(dpa, wpa_ref).astype(BF16)
        dyb_ref[...] = _dot_nt_stacked(dpb, wpb_ref).astype(BF16)

        @pl.when(i == 0)
        def _():
            for r in (gwo_ref, gwpa_ref, gwpb_ref):
                r[...] = jnp.zeros_like(r)

        gwo_ref[...] += _dot_tn(mg_ref[...], dx)
        gwpa_ref[...] += _dot_tn(ya, dpa)
        gwpb_ref[...] += _dot_tn(yb, dpb)

        @pl.when(i == nt - 1)
        def _():
            gwo_out[...] = gwo_ref[...].astype(BF16)
            gwpa_out[...] = gwpa_ref[...].astype(BF16)
            gwpb_out[...] = gwpb_ref[...].astype(BF16)

    return pl.pallas_call(
        body, name="merge_bwd", grid=(nt,),
        in_specs=[_row(tm, D_MODEL), _row(tm, D_MODEL), _row(tm, A_WIDTH), _row(tm, Q_DIM), _row(tm, G_DIM),
                  _resident(w_pa.shape), _resident(w_pb.shape), _resident(w_out.shape)] + [ANY] * len(order),
        out_specs=[_row(tm, G_DIM), _row(tm, A_WIDTH), _row(tm, Q_DIM),
                   _full(w_out.shape), _full(pshape), _full(pshape)],
        out_shape=[_sds((T, G_DIM), BF16), _sds((T, A_WIDTH), BF16), _sds((T, Q_DIM), BF16),
                   _sds(w_out.shape, BF16), _sds(pshape, BF16), _sds(pshape, BF16)],
        scratch_shapes=[pltpu.VMEM(w_out.shape, F32), pltpu.VMEM(pshape, F32), pltpu.VMEM(pshape, F32)],
        compiler_params=_cp(("arbitrary",), 56),
    )(*_hbm(dx1, merged, y_a, y_b, proj_g, w_pa, w_pb, w_out), *order)


def _sgu_bwd(proj_a, d_ya, g_sgu, w_s, b_st, tm, after=None):
    T = proj_a.shape[0]
    order = [] if after is None else [after]

    def body(*refs):
        p_ref, dy_ref, g_ref, ws_ref, bs_ref = refs[:5]
        dp_ref, gws_ref, gbs_ref, gg_ref = refs[5 + len(order):]
        tril = _tril()
        g = g_ref[...]
        pu, pv, u, tu, vv, tv, rv, vn = _sgu_parts(p_ref[...].astype(F32), g)
        dy = dy_ref[...].astype(F32)

        @pl.when(pl.program_id(0) == 0)
        def _():
            for r in (gws_ref, gbs_ref, gg_ref):
                r[...] = jnp.zeros_like(r)

        du_cols = []
        dvn_cols = []
        for gi in range(A_GROUPS):
            wm = jnp.where(tril, ws_ref[gi], 0.0).astype(BF16)
            wmt = wm.astype(F32).T.astype(BF16)
            bcol = bs_ref[:, gi:gi + 1]
            cs = slice(gi * CHUNK, (gi + 1) * CHUNK)
            du_rows = []
            dvn_rows = []
            gw = jnp.zeros((CHUNK, CHUNK), F32)
            gb = jnp.zeros((CHUNK, 1), F32)
            for c in range(tm // CHUNK):
                rs = slice(c * CHUNK, (c + 1) * CHUNK)
                vn_c = vn[rs, cs]
                s = _dot(wm, vn_c) + bcol
                dy_c = dy[rs, cs]
                ds = dy_c * u[rs, cs]
                du_rows.append(dy_c * s)
                dsb = ds.astype(BF16)
                gw = gw + _dot_nt(dsb, vn_c)
                gb = gb + jnp.sum(ds, axis=-1, keepdims=True)
                dvn_rows.append(_dot(wmt, dsb))
            gws_ref[gi] += jnp.where(tril, gw, 0.0)
            gbs_ref[:, gi:gi + 1] += gb
            du_cols.append(jnp.concatenate(du_rows, axis=0))
            dvn_cols.append(jnp.concatenate(dvn_rows, axis=0))
        du = jnp.concatenate(du_cols, axis=1)
        dvn = jnp.concatenate(dvn_cols, axis=1)
        vhat = vv * rv
        gg_ref[...] += jnp.sum(dvn * vhat, axis=0, keepdims=True)
        dvv = _rms_bwd(dvn, vhat, rv, g)
        dp_ref[:, :A_WIDTH] = (du * _gelu_grad(pu, tu)).astype(BF16)
        dp_ref[:, A_WIDTH:] = (dvv * _gelu_grad(pv, tv)).astype(BF16)

    return pl.pallas_call(
        body, name="sgu_bwd", grid=(T // tm,),
        in_specs=[_row(tm, A_DIM), _row(tm, A_WIDTH), _full(g_sgu.shape), _full(w_s.shape), _full(b_st.shape)] + [ANY] * len(order),
        out_specs=[_row(tm, A_DIM), _full(w_s.shape), _full(b_st.shape), _full(g_sgu.shape)],
        out_shape=[_sds((T, A_DIM), BF16), _sds(w_s.shape, F32), _sds(b_st.shape, F32), _sds(g_sgu.shape, F32)],
        compiler_params=_cp(("arbitrary",)),
    )(*_hbm(proj_a, d_ya, g_sgu, w_s, b_st), *order)


def _attn_bwd(proj_b, d_yb, sinks, rel_bias, n_seq, seq):
    nb = seq // CHUNK
    bk = jnp.asarray(_band_buckets())

    def body(qkv_ref, do_ref, bk_ref, rel_ref, sink_ref, d_ref, gs_ref, gr_ref,
             bias_scr, sink_scr, kvar_scr, dbias_scr, dk_scr, dv_scr, ds_scr):
        b = pl.program_id(0)
        _attn_setup(bias_scr, sink_scr, kvar_scr, qkv_ref, bk_ref, rel_ref, sink_ref)
        ones = jnp.ones((2 * CHUNK, LANES), BF16)

        @pl.when(b == 0)
        def _():
            dbias_scr[...] = jnp.zeros_like(dbias_scr)
            ds_scr[...] = jnp.zeros_like(ds_scr)

        dk_scr[...] = jnp.zeros_like(dk_scr)
        dv_scr[...] = jnp.zeros_like(dv_scr)

        def transposed(a):
            return a.astype(F32).T.astype(BF16)

        def blk(n, carry):
            r0, kv, vv = _attn_block_inputs(kvar_scr, n)
            prob, psink = _attn_probs(qkv_ref, r0, n, kv, bias_scr, sink_scr, ones)
            dp = jnp.concatenate([_dot_nt(do_ref[pl.ds(r0, CHUNK), (h // 2) * LANES:(h // 2 + 1) * LANES], vv[h // 4][h % 2])
                                  for h in range(N_HEADS)], axis=0)
            delta = _rowsum(prob * dp, ones)
            dsc = prob * (dp - _both(delta))
            ds_scr[...] += psink * delta
            dbias_scr[...] += dsc
            dsb = (dsc * (HEAD_DIM ** -0.5)).astype(BF16)
            pb = prob.astype(BF16)
            dkt = [jnp.zeros((HEAD_DIM, 2 * CHUNK), F32) for _ in range(2)]
            dvt = [jnp.zeros((HEAD_DIM, 2 * CHUNK), F32) for _ in range(2)]
            for pr in range(N_HEADS // 2):
                ps = slice(pr * LANES, (pr + 1) * LANES)
                qpt = transposed(qkv_ref[pl.ds(r0, CHUNK), ps])
                dopt = transposed(do_ref[pl.ds(r0, CHUNK), ps])
                kvh = pr // 2
                dq = jnp.zeros((CHUNK, LANES), F32)
                for hh in range(2):
                    hr = _head_rows(2 * pr + hh)
                    rows = slice(hh * HEAD_DIM, (hh + 1) * HEAD_DIM)
                    dq = dq + _dot(dsb[hr], kv[kvh][hh])
                    dkt[kvh] = dkt[kvh] + _dot(qpt, dsb[hr])[rows]
                    dvt[kvh] = dvt[kvh] + _dot(dopt, pb[hr])[rows]
                d_ref[pl.ds(r0, CHUNK), ps] = dq.astype(BF16)
            dk_scr[:, pl.ds(r0, 2 * CHUNK)] += jnp.concatenate(dkt, axis=0)
            dv_scr[:, pl.ds(r0, 2 * CHUNK)] += jnp.concatenate(dvt, axis=0)
            return carry

        lax.fori_loop(0, nb, blk, 0)
        for n in range(nb):
            rows = slice(n * CHUNK, (n + 1) * CHUNK)
            cols = slice((n + 1) * CHUNK, (n + 2) * CHUNK)
            d_ref[rows, Q_DIM:Q_DIM + KV_DIM] = dk_scr[:, cols].T.astype(BF16)
            d_ref[rows, Q_DIM + KV_DIM:] = dv_scr[:, cols].T.astype(BF16)

        @pl.when(b == n_seq - 1)
        def _():
            bkv = bk_ref[...]
            for h in range(N_HEADS):
                gs_ref[0:1, h:h + 1] = -jnp.sum(ds_scr[_head_rows(h), 0:1], axis=0, keepdims=True)
                db = dbias_scr[_head_rows(h), :]
                for bb in range(N_BUCKETS):
                    part = jnp.sum(jnp.where(bkv == bb, db, 0.0), axis=-1, keepdims=True)
                    gr_ref[bb:bb + 1, h:h + 1] = jnp.sum(part, axis=0, keepdims=True)

    smem = pl.BlockSpec(memory_space=pltpu.SMEM)
    return pl.pallas_call(
        body, name="attn_bwd", grid=(n_seq,),
        in_specs=[_row(seq, B_DIM), _row(seq, Q_DIM), _full(bk.shape), smem, smem],
        out_specs=[_row(seq, B_DIM), _full((1, N_HEADS)), _full((N_BUCKETS, N_HEADS))],
        out_shape=[_sds((n_seq * seq, B_DIM), BF16), _sds((1, N_HEADS), F32), _sds((N_BUCKETS, N_HEADS), F32)],
        scratch_shapes=[pltpu.VMEM((HEAD_ROWS, 2 * CHUNK), F32), pltpu.VMEM((HEAD_ROWS, LANES), F32),
                        pltpu.VMEM((8, seq, KV_DIM), BF16), pltpu.VMEM((HEAD_ROWS, 2 * CHUNK), F32),
                        pltpu.VMEM((KV_DIM, seq + CHUNK), F32), pltpu.VMEM((KV_DIM, seq + CHUNK), F32),
                        pltpu.VMEM((HEAD_ROWS, LANES), F32)],
        compiler_params=_cp(("arbitrary",), 40),
    )(*_hbm(proj_b, d_yb, bk), rel_bias, sinks)


def _inproj_bwd(d_g, d_a, d_b, x2, dx1, g_mix, w_in, tm, after=None):
    T = x2.shape[0]
    order = [] if after is None else [after]

    def body(*refs):
        dg_ref, da_ref, db_ref, x_ref, dx1_ref, g_ref, w_ref = refs[:7]
        gx_ref, gg_ref = refs[7 + len(order):]
        dh = (_dot_nt(dg_ref[...], w_ref[:, _G_COLS]) + _dot_nt(da_ref[...], w_ref[:, _A_COLS])
              + _dot_nt(db_ref[...], w_ref[:, _B_COLS]))
        x = x_ref[...]
        r = _rms_r(x)
        n = x * r
        gx_ref[...] = dx1_ref[...] + _rms_bwd(dh, n, r, g_ref[...])

        @pl.when(pl.program_id(0) == 0)
        def _():
            gg_ref[...] = jnp.zeros_like(gg_ref)

        gg_ref[...] += jnp.sum(dh * n, axis=0, keepdims=True)

    return pl.pallas_call(
        body, name="inproj_bwd", grid=(T // tm,),
        in_specs=[_row(tm, G_DIM), _row(tm, A_DIM), _row(tm, B_DIM), _row(tm, D_MODEL), _row(tm, D_MODEL),
                  _full(g_mix.shape), _resident(w_in.shape)] + [ANY] * len(order),
        out_specs=[_row(tm, D_MODEL), _full((1, D_MODEL))],
        out_shape=[_sds((T, D_MODEL), F32), _sds((1, D_MODEL), F32)],
        compiler_params=_cp(("arbitrary",), 48),
    )(*_hbm(d_g, d_a, d_b, x2, dx1, g_mix, w_in), *order)


IN_SHARD = (A_DIM + B_DIM + G_DIM) // N_CHIPS


def _unstack_w_in(stack):
    tr = 256

    def body(s_ref, o_ref):
        for i in range(N_CHIPS):
            o_ref[:, i * IN_SHARD:(i + 1) * IN_SHARD] = s_ref[i]

    return pl.pallas_call(
        body, name="unstack_w_in", grid=(D_MODEL // tr,),
        in_specs=[pl.BlockSpec((N_CHIPS, tr, IN_SHARD), lambda r: (0, r, 0))],
        out_specs=pl.BlockSpec((tr, N_CHIPS * IN_SHARD), lambda r: (r, 0)),
        out_shape=_sds((D_MODEL, N_CHIPS * IN_SHARD), stack.dtype),
        compiler_params=_cp(("arbitrary",)),
    )(*_hbm(stack))


def _stack_grad_w_in(gw_a, gw_b, gw_g):
    tr = 256

    def body(a_ref, b_ref, g_ref, o_ref):
        full = jnp.concatenate([a_ref[...], b_ref[...], g_ref[...]], axis=1)
        for i in range(N_CHIPS):
            o_ref[i] = full[:, i * IN_SHARD:(i + 1) * IN_SHARD]

    return pl.pallas_call(
        body, name="stack_grad_w_in", grid=(D_MODEL // tr,),
        in_specs=[_row(tr, A_DIM), _row(tr, B_DIM), _row(tr, G_DIM)],
        out_specs=pl.BlockSpec((N_CHIPS, tr, IN_SHARD), lambda r: (0, r, 0)),
        out_shape=_sds((N_CHIPS, D_MODEL, IN_SHARD), gw_a.dtype),
        compiler_params=_cp(("arbitrary",)),
    )(*_hbm(gw_a, gw_b, gw_g))


def _local_step(x, target, g_mix, g_sgu, w_s, b_s, sinks, rel_bias, g_ffn, b_conv, g_final,
                w_in, w_conv, proj_weights, ffn_weights, on_grads, after=None):
    n_seq, seq, _ = x.shape
    T = n_seq * seq
    tm = min(ROW_TILE, seq)
    tw = min(GRAD_ROW_TILE, T)
    tf = min(WIDE_ROW_TILE, seq)
    x2 = x.reshape(T, D_MODEL)
    tgt = target.reshape(T, D_MODEL)
    b_st = b_s.T
    g_fin = g_final.reshape(1, D_MODEL)

    proj_g, proj_a, proj_b, h = _inproj(x2, g_mix, w_in, tm, after)
    y_a = _sgu_fwd(proj_a, g_sgu, w_s, b_st, tm)
    y_b = _attn_fwd(proj_b, sinks, rel_bias, n_seq, seq)
    w_pa, w_pb, w_out = proj_weights(y_b)
    x1, merged = _merge_fwd(x2, y_a, y_b, proj_g, w_pa, w_pb, w_out, tm)
    w_up, w_down = ffn_weights(x1)
    upre, h2, gate, val = _upproj(x1, g_ffn, w_up, w_conv, b_conv, tf, seq)
    dx2, loss, gg_final = _ffn_down_loss(gate, val, x1, tgt, w_down, g_fin, tm)

    d_gate, d_val, gw_down, gb_g, gb_v = _ffn_bwd_act(gate, val, dx2, w_down, tw)
    gb_conv = jnp.concatenate([gb_g, gb_v], axis=1)
    d_upre, dx1, gg_ffn, gw_conv = _ffn_bwd_up(d_gate, d_val, upre, dx2, x1, g_ffn, w_conv, w_up, tf, seq)
    gw_up = _matmul_tn(h2, d_upre, 2 * D_FF // 4, min(2 * GRAD_ROW_TILE, T), "grad_w_up")
    sent = on_grads("ffn", dict(w_up=gw_up, w_down=gw_down))
    d_g, d_ya, d_yb, gw_out, gw_pa, gw_pb = _merge_bwd(dx1, merged, y_a, y_b, proj_g, w_pa, w_pb, w_out, tf, sent)
    sent = on_grads("proj", dict(w_pa=gw_pa, w_pb=gw_pb, w_out=gw_out))
    d_a, gw_s, gb_st, gg_sgu = _sgu_bwd(proj_a, d_ya, g_sgu, w_s, b_st, tm, sent)
    d_b, g_sinks, g_rel = _attn_bwd(proj_b, _tie(d_yb, d_a), sinks, rel_bias, n_seq, seq)
    gw_g = _matmul_tn(h, _tie(d_g, d_b), D_MODEL, min(2 * GRAD_ROW_TILE, T), "grad_w_in_gate")
    gw_a = _matmul_tn(h, _tie(d_a, gw_g), A_DIM, min(2 * GRAD_ROW_TILE, T), "grad_w_in_a")
    gw_b = _matmul_tn(h, _tie(d_b, gw_a), B_DIM, min(2 * GRAD_ROW_TILE, T), "grad_w_in_b")
    gw_in = _stack_grad_w_in(gw_a, gw_b, gw_g)
    sent = on_grads("in", dict(w_in=gw_in))
    grad_x, gg_mix = _inproj_bwd(d_g, d_a, d_b, x2, dx1, g_mix, w_in, tm, sent)

    small = dict(g_mix=gg_mix, g_sgu=gg_sgu, w_s=gw_s, b_s=gb_st.T, sinks=g_sinks, rel_bias=g_rel,
                 g_ffn=gg_ffn, b_conv=gb_conv, g_final=gg_final, w_conv=gw_conv)
    big = dict(w_in=gw_in, w_pa=gw_pa, w_pb=gw_pb, w_out=gw_out, w_up=gw_up, w_down=gw_down)
    return loss, grad_x.reshape(x.shape), small, big


_MIXER = ("w_in", "w_pa", "w_pb", "w_out")
_FFN = ("w_up", "w_down")
_BIG = _MIXER + _FFN

CONV_ROWS = 6
_SMALL_AT = dict(loss=(0, 1, 1), g_final=(1, 1, D_MODEL), g_mix=(2, 1, D_MODEL), g_ffn=(3, 1, D_MODEL), g_sgu=(4, 1, A_WIDTH),
                 sinks=(5, 1, N_HEADS), b_s=(8, A_GROUPS, CHUNK), rel_bias=(16, N_BUCKETS, N_HEADS),
                 b_conv=(48, CONV_ROWS, D_MODEL), w_conv=(56, 3 * CONV_ROWS, D_MODEL), w_s=(80, A_GROUPS * CHUNK * CHUNK // D_MODEL, D_MODEL))
_SMALL_IN_CALL = ("g_final", "g_mix", "g_ffn", "g_sgu", "sinks", "b_s", "rel_bias")
SMALL_ROWS = 144


def _pack_small(vals):
    def wide(a):
        return jnp.pad(a, ((0, 0), (0, CONV_ROWS * D_MODEL - a.shape[1]))).reshape(-1, D_MODEL)

    laid = dict(vals, b_conv=wide(vals["b_conv"]), w_conv=wide(vals["w_conv"]), w_s=vals["w_s"].reshape(-1, D_MODEL))
    rows, at = [], 0
    for n, (r0, nr, nc) in _SMALL_AT.items():
        if r0 > at:
            rows.append(jnp.zeros((r0 - at, D_MODEL), F32))
        rows.append(jnp.pad(laid[n].astype(F32).reshape(nr, nc), ((0, 0), (0, D_MODEL - nc))))
        at = r0 + nr
    return jnp.concatenate(rows, axis=0)


def _unwide(a, r):
    return a.reshape(r, CONV_ROWS * D_MODEL)[:, :2 * D_FF]


def _mesh_pos():
    return lax.axis_index("x"), lax.axis_index("y"), lax.axis_index("c")


def _other_chips(x, y):
    return [(1 - x, y), (x, 1 - y), (1 - x, 1 - y)]


def _remote(src, dst, send_sem, recv_sem, to):
    return pltpu.make_async_remote_copy(src_ref=src, dst_ref=dst, send_sem=send_sem, recv_sem=recv_sem,
                                        device_id=to, device_id_type=MESH)


def _own_slot(own, n, at):
    return lax.dynamic_update_slice(lax.empty((n,) + own.shape, own.dtype), own[None], (at,) + (0,) * own.ndim)


def _allgather_weights(stacks, wc_stack):
    names = list(stacks)
    n = len(names)

    def body(*refs):
        ins, outs = refs[:n + 1], refs[n + 1:2 * n + 2]
        send_sems, recv_sems = refs[2 * n + 2:]
        x, y, c = _mesh_pos()
        me = 2 * x + y
        sibling = (x, y, 1 - c)
        chips = _other_chips(x, y)

        def half(ref, chip, hc):
            hr = ref.shape[1] // 2
            return ref.at[chip, pl.ds(hc * hr, hr), :]

        first = []
        for k in range(n):
            first += [_remote(half(ins[k], me, c), half(outs[k], me, c), send_sems.at[6 * k + j], recv_sems.at[6 * k + j], (cx, cy, c))
                      for j, (cx, cy) in enumerate(chips)]
        first += [_remote(ins[n].at[me], outs[n].at[me], send_sems.at[6 * n + j], recv_sems.at[6 * n + j], (cx, cy, c))
                  for j, (cx, cy) in enumerate(chips)]
        for cp in first:
            cp.start()
        passed = []
        for k in range(n):
            for j, (cx, cy) in enumerate(chips):
                landed = half(outs[k], 2 * cx + cy, c)
                _remote(landed, landed, send_sems.at[6 * k + j], recv_sems.at[6 * k + j], (x, y, c)).wait_recv()
                passed.append(_remote(landed, landed, send_sems.at[6 * k + 3 + j], recv_sems.at[6 * k + 3 + j], sibling))
                passed[-1].start()
        for k in range(n):
            for j, (cx, cy) in enumerate(chips):
                theirs = half(outs[k], 2 * cx + cy, 1 - c)
                _remote(theirs, theirs, send_sems.at[6 * k + 3 + j], recv_sems.at[6 * k + 3 + j], (x, y, c)).wait_recv()
        for j, (cx, cy) in enumerate(chips):
            slot = outs[n].at[2 * cx + cy]
            _remote(slot, slot, send_sems.at[6 * n + j], recv_sems.at[6 * n + j], (x, y, c)).wait_recv()
        for cp in first + passed:
            cp.wait_send()

    arrays = [stacks[k] for k in names] + [wc_stack]
    outs = pl.pallas_call(
        body, name="allgather_weights",
        in_specs=[HBM] * (n + 1), out_specs=[HBM] * (n + 1), input_output_aliases={k: k for k in range(n + 1)},
        out_shape=[_sds(a.shape, a.dtype) for a in arrays],
        scratch_shapes=[pltpu.SemaphoreType.DMA((6 * n + 3,)), pltpu.SemaphoreType.DMA((6 * n + 3,))],
    )(*arrays)
    return dict(zip(names, outs[:n])), outs[n]


_KIND = {"w_in": "stack", "w_pa": "col", "w_pb": "col", "w_up": "col", "w_out": "row", "w_down": "row"}


def _half_view(ref, kind, h):
    if kind == "stack":
        k = ref.shape[1] // 2
        return ref.at[:, pl.ds(h * k, k), :]
    if kind == "col":
        k = ref.shape[0] // 2
        return ref.at[pl.ds(h * k, k), :]
    k = ref.shape[1] // 2
    return ref.at[:, pl.ds(h * k, k)]


def _shard_view(ref, kind, i):
    if kind == "stack":
        return ref.at[i]
    if kind == "col":
        k = ref.shape[1] // N_CHIPS
        return ref.at[:, pl.ds(i * k, k)]
    k = ref.shape[0] // N_CHIPS
    return ref.at[pl.ds(i * k, k), :]


def _region_view(ref, kind, h):
    if kind == "row":
        k = ref.shape[1] // 2
        return ref.at[:, pl.ds(h * k, k)]
    k = ref.shape[0] // 2
    return ref.at[pl.ds(h * k, k), :]


def _half_shape(shape, kind):
    if kind == "stack":
        return (shape[0], shape[1] // 2, shape[2])
    return (shape[0] // 2, shape[1]) if kind == "col" else (shape[0], shape[1] // 2)


def _part_shape(half_shape, kind):
    if kind == "stack":
        return tuple(half_shape[1:])
    k, w = half_shape
    return (k, w // N_CHIPS) if kind == "col" else (k // N_CHIPS, w)


_DATAFLOW = pltpu.SideEffectType.DATAFLOW_SIDE_EFFECTING
_TOKEN = (SUBLANES, LANES)


_COLLECTIVE = {k: i for i, k in enumerate(
    [kind + "_" + g for kind in ("pair", "chip", "share") for g in ("ffn", "proj", "in")] + ["gather_proj", "gather_ffn"])}


def _sibling_peers(x, y, c):
    return [(x, y, 1 - c)]


def _chip_peers(x, y, c):
    return [(cx, cy, c) for cx, cy in _other_chips(x, y)]


def _split_start(name, arrays, n_sems, issue, after=None, handshake=None):
    n = len(arrays)
    order = [] if after is None else [after]

    def body(*refs):
        base = n + len(order)
        if handshake is not None:
            peers = handshake[1](*_mesh_pos())
            barrier = pltpu.get_barrier_semaphore()
            for peer in peers:
                pl.semaphore_signal(barrier, inc=1, device_id=peer, device_id_type=MESH)
            pl.semaphore_wait(barrier, len(peers))
        issue(refs[:n], refs[base], refs[base + 1])
        refs[-1][...] = jnp.zeros(_TOKEN, F32)

    params = dict(has_side_effects=_DATAFLOW)
    if handshake is not None:
        params["collective_id"] = handshake[0]
    outs = pl.pallas_call(
        body, name=name,
        in_specs=[HBM] * n + [ANY] * len(order), out_specs=[SEM, SEM] + [HBM] * n + [pl.BlockSpec(memory_space=pltpu.VMEM)],
        out_shape=[pltpu.SemaphoreType.DMA((n_sems,)), pltpu.SemaphoreType.DMA((n_sems,))]
        + [pltpu.HBM(a.shape, a.dtype) for a in arrays] + [_sds(_TOKEN, F32)],
        input_output_aliases={k: 2 + k for k in range(n)},
        compiler_params=pltpu.CompilerParams(**params),
    )(*[pltpu.with_memory_space_constraint(a, pltpu.HBM) for a in arrays], *order)
    return outs[0], outs[1], list(outs[2:2 + n]), outs[-1]


def _split_wait(name, started, waits, after):
    send_sems, recv_sems, arrays, _ = started
    n = len(arrays)

    def body(*refs):
        waits(refs[:n], refs[n], refs[n + 1])

    return pl.pallas_call(
        body, name=name,
        in_specs=[HBM] * n + [SEM, SEM, ANY], out_specs=[HBM] * n,
        out_shape=[pltpu.HBM(a.shape, a.dtype) for a in arrays],
        input_output_aliases={k: k for k in range(n)},
        compiler_params=pltpu.CompilerParams(has_side_effects=_DATAFLOW),
    )(*arrays, send_sems, recv_sems, after)


def _wait_both(src, dst, send_sem, recv_sem):
    x, y, c = _mesh_pos()
    cp = _remote(src, dst, send_sem, recv_sem, (x, y, c))
    cp.wait_send()
    cp.wait_recv()


def _pair_exchange_start(parts, tag, after):
    names = list(parts)
    n = len(names)
    lands = [lax.empty(_half_shape(parts[k].shape, _KIND[k]), parts[k].dtype) for k in names]

    def issue(refs, send_sems, recv_sems):
        x, y, c = _mesh_pos()
        for hc in range(2):
            @pl.when(c == hc)
            def _():
                for k in range(n):
                    _remote(_half_view(refs[k], _KIND[names[k]], 1 - hc), refs[n + k], send_sems.at[k], recv_sems.at[k],
                            (x, y, 1 - c)).start()

    return names, _split_start("grad_pair_exchange_start_" + tag, [parts[k] for k in names] + lands, n, issue, after,
                               (_COLLECTIVE["pair_" + tag], _sibling_peers))


def _pair_exchange_wait(pending, tag, after):
    names, started = pending
    n = len(names)

    def waits(refs, send_sems, recv_sems):
        for k in range(n):
            _wait_both(_half_view(refs[k], _KIND[names[k]], 0), refs[n + k], send_sems.at[k], recv_sems.at[k])

    outs = _split_wait("grad_pair_exchange_wait_" + tag, started, waits, after)
    return dict(zip(names, outs[:n])), dict(zip(names, outs[n:]))


def _half_blocks(shape, kind):
    if kind == "stack":
        _, k, w = shape
        tr = k // 2
        nb = 1
        return (N_CHIPS, nb), (1, tr, w), (lambda i, r, s: (i, r, 0)), (lambda i, r, s: (i, s[1] * nb + r, 0))
    k, w = shape
    if kind == "col":
        tr = 256
        nb = k // 2 // tr
        return (nb,), (tr, w), (lambda r, s: (r, 0)), (lambda r, s: (s[1] * nb + r, 0))
    tr = k // N_CHIPS
    return (N_CHIPS,), (tr, w // 2), (lambda r, s: (r, 0)), (lambda r, s: (r, s[1]))


def _pair_add(part, from_sibling, name, pos):
    kind = _KIND[name]
    grid, block, half_map, full_map = _half_blocks(part.shape, kind)

    def body(s_ref, p_ref, q_ref, o_ref):
        o_ref[...] = (p_ref[...].astype(F32) + q_ref[...].astype(F32)).astype(BF16)

    return pl.pallas_call(
        body, name="grad_pair_add_" + name,
        grid_spec=pltpu.PrefetchScalarGridSpec(
            num_scalar_prefetch=1, grid=grid,
            in_specs=[pl.BlockSpec(block, full_map), pl.BlockSpec(block, half_map)],
            out_specs=pl.BlockSpec(block, half_map)),
        out_shape=_sds(from_sibling.shape, BF16),
        compiler_params=_cp(("arbitrary",) * len(grid), 40),
    )(pos, *_hbm(part, from_sibling))


def _chip_exchange_start(sums, tag, after):
    names = list(sums)
    n = len(names)
    lands = [lax.empty((3,) + _part_shape(sums[k].shape, _KIND[k]), sums[k].dtype) for k in names]

    def issue(refs, send_sems, recv_sems):
        x, y, c = _mesh_pos()
        me = 2 * x + y
        for i in range(N_CHIPS):
            xi, yi = i // 2, i % 2
            j = jnp.where(xi != x, jnp.where(yi != y, 2, 0), 1)

            @pl.when(i != me)
            def _():
                for k in range(n):
                    _remote(_shard_view(refs[k], _KIND[names[k]], i), refs[n + k].at[j], send_sems.at[3 * k + j],
                            recv_sems.at[3 * k + j], (xi, yi, c)).start()

    return names, _split_start("grad_chip_exchange_start_" + tag, [sums[k] for k in names] + lands, 3 * n, issue, after,
                               (_COLLECTIVE["chip_" + tag], _chip_peers))


def _chip_exchange_wait(pending, tag, after):
    names, started = pending
    n = len(names)

    def waits(refs, send_sems, recv_sems):
        for k in range(n):
            for j in range(3):
                _wait_both(_shard_view(refs[k], _KIND[names[k]], 0), refs[n + k].at[j], send_sems.at[3 * k + j], recv_sems.at[3 * k + j])

    return dict(zip(names, _split_wait("grad_chip_exchange_wait_" + tag, started, waits, after)[n:]))


def _allgather_start(stacks, tag, after):
    names = list(stacks)

    def issue(refs, send_sems, recv_sems):
        x, y, c = _mesh_pos()
        me = 2 * x + y
        for k, st in enumerate(refs):
            hr = st.shape[1] // 2
            mine = st.at[me, pl.ds(c * hr, hr), :]
            for j, (cx, cy) in enumerate(_other_chips(x, y)):
                _remote(mine, mine, send_sems.at[3 * k + j], recv_sems.at[3 * k + j], (cx, cy, c)).start()

    return names, _split_start("allgather_start_" + tag, [stacks[k] for k in names], 3 * len(names), issue, after,
                               (_COLLECTIVE["gather_" + tag], _chip_peers))


def _allgather_wait(pending, tag, after):
    names, started = pending

    def waits(refs, send_sems, recv_sems):
        for k, st in enumerate(refs):
            slot = st.at[0, pl.ds(0, st.shape[1] // 2), :]
            for j in range(3):
                _wait_both(slot, slot, send_sems.at[3 * k + j], recv_sems.at[3 * k + j])

    return dict(zip(names, _split_wait("allgather_wait_" + tag, started, waits, after)))


def _allgather_forward(stacks, tag):
    names = list(stacks)
    n = len(names)

    def body(*refs):
        ins, outs = refs[:n], refs[n:2 * n]
        send_sems, recv_sems = refs[2 * n:]
        x, y, c = _mesh_pos()
        copies = []
        for k in range(n):
            hr = ins[k].shape[1] // 2
            for j, (cx, cy) in enumerate(_other_chips(x, y)):
                chip = 2 * cx + cy
                copies.append(_remote(ins[k].at[chip, pl.ds(c * hr, hr), :], outs[k].at[chip, pl.ds(c * hr, hr), :],
                                      send_sems.at[3 * k + j], recv_sems.at[3 * k + j], (x, y, 1 - c)))
        for cp in copies:
            cp.start()
        for cp in copies:
            cp.wait()

    arrays = [stacks[k] for k in names]
    outs = pl.pallas_call(
        body, name="allgather_forward_" + tag, in_specs=[HBM] * n, out_specs=[HBM] * n,
        input_output_aliases={k: k for k in range(n)},
        out_shape=[_sds(a.shape, a.dtype) for a in arrays],
        scratch_shapes=[pltpu.SemaphoreType.DMA((3 * n,)), pltpu.SemaphoreType.DMA((3 * n,))],
    )(*arrays)
    return dict(zip(names, outs))


def _owner_sum(part, from_sibling, from_chips, name, pos, shard_shape):
    kind = _KIND[name]
    _, pk, pw = from_chips.shape
    if kind == "row":
        tr, nb = pk, 1
        p_spec = pl.BlockSpec((tr, pw), lambda r, s: (s[0], s[1]))
        q_spec = pl.BlockSpec((tr, pw), lambda r, s: (s[0], 0))
        o_spec = pl.BlockSpec((tr, pw), lambda r, s: (0, s[1]))
    else:
        tr = 256
        nb = pk // tr
        if kind == "stack":
            p_spec = pl.BlockSpec((None, tr, pw), lambda r, s: (s[0], s[1] * nb + r, 0))
            q_spec = pl.BlockSpec((None, tr, pw), lambda r, s: (s[0], r, 0))
        else:
            p_spec = pl.BlockSpec((tr, pw), lambda r, s: (s[1] * nb + r, s[0]))
            q_spec = pl.BlockSpec((tr, pw), lambda r, s: (r, s[0]))
        o_spec = pl.BlockSpec((tr, pw), lambda r, s: (s[1] * nb + r, 0))

    def body(s_ref, p_ref, q_ref, r_ref, o_ref):
        acc = p_ref[...].astype(F32) + q_ref[...].astype(F32)
        for j in range(3):
            acc = acc + r_ref[j].astype(F32)
        o_ref[...] = acc

    return pl.pallas_call(
        body, name="grad_owner_sum_" + name,
        grid_spec=pltpu.PrefetchScalarGridSpec(
            num_scalar_prefetch=1, grid=(nb,),
            in_specs=[p_spec, q_spec, pl.BlockSpec((3, tr, pw), lambda r, s: (0, r, 0))],
            out_specs=o_spec),
        out_shape=_sds(shard_shape, F32),
        compiler_params=_cp(("arbitrary",), 32),
    )(pos, *_hbm(part, from_sibling, from_chips))


def _pair_share_start(shards, tag, after):
    names = list(shards)

    def issue(refs, send_sems, recv_sems):
        x, y, c = _mesh_pos()
        for hc in range(2):
            @pl.when(c == hc)
            def _():
                for k, g in enumerate(refs):
                    mine = _region_view(g, _KIND[names[k]], hc)
                    _remote(mine, mine, send_sems.at[k], recv_sems.at[k], (x, y, 1 - c)).start()

    return names, _split_start("grad_pair_share_start_" + tag, [shards[k] for k in names], len(names), issue, after,
                               (_COLLECTIVE["share_" + tag], _sibling_peers))


def _pair_share_wait(pending, tag, after):
    names, started = pending

    def waits(refs, send_sems, recv_sems):
        for k, g in enumerate(refs):
            region = _region_view(g, _KIND[names[k]], 0)
            _wait_both(region, region, send_sems.at[k], recv_sems.at[k])

    return dict(zip(names, _split_wait("grad_pair_share_wait_" + tag, started, waits, after)))


def _small_exchange_start(slots, after):
    def issue(refs, send_sems, recv_sems):
        x, y, c = _mesh_pos()
        mine = refs[0].at[4 * x + 2 * y + c]
        k = 0
        for px in range(2):
            for py in range(2):
                for pc in range(2):
                    if px + py + pc:
                        peer = (1 - x if px else x, 1 - y if py else y, 1 - c if pc else c)
                        _remote(mine, mine, send_sems.at[k], recv_sems.at[k], peer).start()
                        k += 1

    return _split_start("small_exchange_start", [slots], N_DEV - 1, issue, after)


def _small_exchange_wait(started, after):
    def waits(refs, send_sems, recv_sems):
        slot = refs[0].at[0]
        for k in range(N_DEV - 1):
            _wait_both(slot, slot, send_sems.at[k], recv_sems.at[k])

    return _split_wait("small_exchange_wait", started, waits, after)[0]


def _adam_math(w, g, m, v):
    m = ADAM_B1 * m + (1.0 - ADAM_B1) * g
    v = ADAM_B2 * v + (1.0 - ADAM_B2) * (g * g)
    m_hat = m / (1.0 - ADAM_B1 ** ADAM_STEP)
    v_hat = v / (1.0 - ADAM_B2 ** ADAM_STEP)
    delta = -ADAM_LR * (m_hat / (jnp.sqrt(v_hat) + ADAM_EPS) + ADAM_WD * w)
    return delta, m, v


def _adamw(w, g, m, v, name):
    rows, cols = w.shape
    fits = [t for t in range(SUBLANES, rows, SUBLANES) if rows % t == 0 and t * cols * 4 <= (3 << 19)]
    tr = max(fits) if fits else rows

    def body(w_ref, g_ref, m_ref, v_ref, d_ref, nm_ref, nv_ref, go_ref):
        g = g_ref[...]
        d, nm, nv = _adam_math(w_ref[...], g, m_ref[...], v_ref[...])
        d_ref[...] = d
        nm_ref[...] = nm
        nv_ref[...] = nv
        go_ref[...] = g

    spec = pl.BlockSpec((tr, cols), lambda i: (i, 0))
    return pl.pallas_call(
        body, name=name, grid=(rows // tr,), in_specs=[spec] * 4, out_specs=[spec] * 4,
        out_shape=[_sds(w.shape, F32)] * 4, compiler_params=_cp(("arbitrary",)),
    )(*_hbm(w, g, m, v))


def _small_sum_adamw(gathered, w, m, v):
    names = _SMALL_IN_CALL
    n = len(names)

    def body(*refs):
        a_ref = refs[0]
        w_refs, m_refs, v_refs = refs[1:1 + n], refs[1 + n:1 + 2 * n], refs[1 + 2 * n:1 + 3 * n]
        sum_ref = refs[1 + 3 * n]
        outs = refs[2 + 3 * n:]
        g = a_ref[0]
        for k in range(1, N_DEV):
            g = g + a_ref[k]
        sum_ref[...] = g
        for i, name in enumerate(names):
            r0, nr, nc = _SMALL_AT[name]
            gp = g[r0:r0 + nr, 0:nc]
            d, nm, nv = _adam_math(w_refs[i][...], gp, m_refs[i][...], v_refs[i][...])
            for k, val in enumerate((gp, d, nm, nv)):
                outs[4 * i + k][...] = val

    shapes = [w[k].shape for k in names]
    res = pl.pallas_call(
        body, name="small_sum_adamw",
        out_shape=[_sds((SMALL_ROWS, D_MODEL), F32)] + [_sds(s, F32) for s in shapes for _ in range(4)],
    )(gathered, *[w[k] for k in names], *[m[k] for k in names], *[v[k] for k in names])
    return res[0], {k: tuple(res[1 + 4 * i:5 + 4 * i]) for i, k in enumerate(names)}


_NAMES = ("g_mix", "w_in", "g_sgu", "w_s", "b_s", "sinks", "rel_bias", "w_pa", "w_pb", "w_out",
          "g_ffn", "w_up", "w_conv", "b_conv", "w_down", "g_final")

def kernel(x, g_mix, w_in, g_sgu, w_s, b_s, sinks, rel_bias, w_pa, w_pb, w_out, g_ffn, w_up, w_conv, b_conv, w_down, g_final, loss_target, m_g_mix, m_w_in, m_g_sgu, m_w_s, m_b_s, m_sinks, m_rel_bias, m_w_pa, m_w_pb, m_w_out, m_g_ffn, m_w_up, m_w_conv, m_b_conv, m_w_down, m_g_final, v_g_mix, v_w_in, v_g_sgu, v_w_s, v_b_s, v_sinks, v_rel_bias, v_w_pa, v_w_pb, v_w_out, v_g_ffn, v_w_up, v_w_conv, v_b_conv, v_w_down, v_g_final):
    w = dict(g_mix=g_mix, w_in=w_in, g_sgu=g_sgu, w_s=w_s, b_s=b_s, sinks=sinks, rel_bias=rel_bias, w_pa=w_pa, w_pb=w_pb,
             w_out=w_out, g_ffn=g_ffn, w_up=w_up, w_conv=w_conv, b_conv=b_conv, w_down=w_down, g_final=g_final)
    m = dict(g_mix=m_g_mix, w_in=m_w_in, g_sgu=m_g_sgu, w_s=m_w_s, b_s=m_b_s, sinks=m_sinks, rel_bias=m_rel_bias, w_pa=m_w_pa,
             w_pb=m_w_pb, w_out=m_w_out, g_ffn=m_g_ffn, w_up=m_w_up, w_conv=m_w_conv, b_conv=m_b_conv, w_down=m_w_down,
             g_final=m_g_final)
    v = dict(g_mix=v_g_mix, w_in=v_w_in, g_sgu=v_g_sgu, w_s=v_w_s, b_s=v_b_s, sinks=v_sinks, rel_bias=v_rel_bias, w_pa=v_w_pa,
             w_pb=v_w_pb, w_out=v_w_out, g_ffn=v_g_ffn, w_up=v_w_up, w_conv=v_w_conv, b_conv=v_b_conv, w_down=v_w_down,
             g_final=v_g_final)
    xi, yi, ci = _mesh_pos()
    me = 2 * xi + yi

    shard = {n: w[n][0] for n in _BIG}
    shard_shapes = {n: shard[n].shape for n in _BIG}
    wc_shard = w["w_conv"][0]
    wc_pad = jnp.pad(wc_shard, ((0, 5), (0, 0)))
    own = {n: _own_slot(shard[n].astype(BF16), N_CHIPS, me) for n in _BIG}
    stacks, wc_all = _allgather_weights({"w_in": own["w_in"]}, _own_slot(wc_pad, N_CHIPS, me))
    proj_gather = _allgather_start({n: own[n] for n in _MIXER[1:]}, "proj", stacks["w_in"])
    ffn_gather = _allgather_start({n: own[n] for n in _FFN}, "ffn", proj_gather[1][-1])
    w_conv_full = jnp.concatenate([wc_all[i, :3] for i in range(N_CHIPS)], axis=1)
    w_in_full = _unstack_w_in(stacks["w_in"])
    pos = jnp.stack([me, ci])

    def proj_weights(done):
        st = _allgather_forward(_allgather_wait(proj_gather, "proj", done), "proj")
        return st["w_pa"], st["w_pb"], st["w_out"].reshape(D_MODEL, D_MODEL)

    def ffn_weights(done):
        st = _allgather_forward(_allgather_wait(ffn_gather, "ffn", done), "ffn")
        return st["w_up"], st["w_down"].reshape(D_FF, D_MODEL)

    groups = {}

    def stage1(group, parts):
        groups[group] = dict(parts=parts, pair=_pair_exchange_start(parts, group, None))
        return groups[group]["pair"][1][-1]

    def stage2(group, after, order_after):
        g = groups[group]
        g["parts"], g["sib"] = _pair_exchange_wait(g["pair"], group, after)
        g["chip"] = _chip_exchange_start({n: _pair_add(g["parts"][n], g["sib"][n], n, pos) for n in g["parts"]}, group, order_after)
        return g["chip"][1][-1]

    def stage3(group, after, order_after):
        g = groups[group]
        got = _chip_exchange_wait(g["chip"], group, after)
        g["share"] = _pair_share_start(
            {n: _owner_sum(g["parts"][n], g["sib"][n], got[n], n, pos, shard_shapes[n]) for n in g["parts"]}, group, order_after)
        return g["share"][1][-1]

    grads, deltas, new_m, new_v = {}, {}, {}, {}

    def stage4(group, after):
        g_shard = _pair_share_wait(groups[group]["share"], group, after)
        last = None
        for n in g_shard:
            g = _tie(g_shard[n], last)
            if n == "w_in":
                d, nm, nv, gt = _adamw(shard[n].T, g.T, m[n][0].T, v[n][0].T, "adamw_" + n)
                grads[n], deltas[n], new_m[n], new_v[n] = gt.T[None], d.T[None], nm.T[None], nv.T[None]
            else:
                d, nm, nv, go = _adamw(shard[n], g, m[n][0], v[n][0], "adamw_" + n)
                grads[n], deltas[n], new_m[n], new_v[n] = go[None], d[None], nm[None], nv[None]
            last = nv
        return last

    def on_grads(group, parts):
        token = stage1(group, parts)
        some = next(iter(parts.values()))
        if group == "proj":
            token = stage2("ffn", some, token)
        if group == "in":
            token = stage2("proj", some, token)
            token = stage3("ffn", some, token)
            token = stage2("in", token, token)
        return token

    loss, grad_x, small, big = _local_step(
        x, loss_target, w["g_mix"], w["g_sgu"], w["w_s"][0], w["b_s"][0], w["sinks"], w["rel_bias"], w["g_ffn"],
        w["b_conv"], w["g_final"], w_in_full, w_conv_full, proj_weights, ffn_weights, on_grads, ffn_gather[1][-1])

    small["loss"] = loss
    small_gather = _small_exchange_start(_own_slot(_pack_small(small), N_DEV, 2 * me + ci), grad_x)
    token = stage3("proj", grad_x, small_gather[-1])
    done = stage4("ffn", token)
    done = stage4("proj", done)
    token = stage3("in", done, None)
    all_small = _small_exchange_wait(small_gather, token)
    two_d = {n: (lambda a, n=n: a.reshape(_SMALL_AT[n][1:])) for n in _SMALL_IN_CALL}
    s_sum, s_out = _small_sum_adamw(all_small, *[{n: two_d[n](p[n]) for n in _SMALL_IN_CALL} for p in (w, m, v)])
    stage4("in", all_small)
    for n in _SMALL_IN_CALL:
        grads[n], deltas[n], new_m[n], new_v[n] = [a.reshape(w[n].shape) for a in s_out[n]]

    def rows(n):
        r0, nr, _ = _SMALL_AT[n]
        return s_sum[r0:r0 + nr]

    wcols = wc_shard.shape[1]
    g_wc = lax.dynamic_slice(_unwide(rows("w_conv"), 3), (0, me * wcols), (3, wcols))
    d, nm, nv, _ = _adamw(wc_shard, g_wc, m["w_conv"][0], v["w_conv"][0], "adamw_w_conv")
    grads["w_conv"], deltas["w_conv"], new_m["w_conv"], new_v["w_conv"] = g_wc[None], d[None], nm[None], nv[None]
    d, nm, nv, go = _adamw(w["b_conv"], _unwide(rows("b_conv"), 1), m["b_conv"], v["b_conv"], "adamw_b_conv")
    grads["b_conv"], deltas["b_conv"], new_m["b_conv"], new_v["b_conv"] = go, d, nm, nv
    flat_s = (A_GROUPS * CHUNK, CHUNK)
    d, nm, nv, go = _adamw(w["w_s"].reshape(flat_s), rows("w_s").reshape(flat_s), m["w_s"].reshape(flat_s),
                           v["w_s"].reshape(flat_s), "adamw_w_s")
    grads["w_s"], deltas["w_s"], new_m["w_s"], new_v["w_s"] = [a.reshape(w["w_s"].shape) for a in (go, d, nm, nv)]

    return (s_sum[0, 0], grad_x, *[grads[n] for n in _NAMES], *[deltas[n] for n in _NAMES],
            *[new_m[n] for n in _NAMES], *[new_v[n] for n in _NAMES])
```

```python
import functools

import numpy as np
import jax
import jax.numpy as jnp
from jax import lax
from jax.experimental import pallas as pl
from jax.experimental.pallas import tpu as pltpu

F32 = jnp.float32
BF16 = jnp.bfloat16

D_MODEL = 1024
CHUNK = 128
A_GROUPS = 4
A_WIDTH = 512
N_HEADS = 8
HEAD_DIM = 64
Q_DIM = 512
KV_DIM = 128
N_BUCKETS = 32
MAX_DISTANCE = 128
D_FF = 2816
EPS = 1e-6
NEG_INF = -1e30
G_DIM = 2 * D_MODEL
A_DIM = 2 * A_WIDTH
B_DIM = Q_DIM + 2 * KV_DIM
LANES = 128
SUBLANES = 8
ROW_TILE = 512
WIDE_ROW_TILE = 256
COL_CHUNK = 512
GRAD_ROW_TILE = 512
BF16_ROWS = 16
N_CHIPS = 4
N_DEV = 8

ADAM_LR = 0.001
ADAM_B1 = 0.9
ADAM_B2 = 0.999
ADAM_EPS = 1e-08
ADAM_WD = 0.01
ADAM_STEP = 10

MESH = pl.DeviceIdType.MESH
_GELU_C = 0.7978845608028654
_GELU_A = 0.044715


def _cp(sem=None, vmem_mb=None):
    kw = {}
    if sem is not None:
        kw["dimension_semantics"] = sem
    if vmem_mb is not None:
        kw["vmem_limit_bytes"] = vmem_mb << 20
    return pltpu.CompilerParams(**kw)


def _dot(a, b):
    return jnp.dot(a, b, preferred_element_type=F32)


def _dot_nt(a, b):
    return lax.dot_general(a, b, (((1,), (1,)), ((), ())), preferred_element_type=F32)


def _dot_tn(a, b):
    return lax.dot_general(a, b, (((0,), (0,)), ((), ())), preferred_element_type=F32)


def _rms_r(x):
    return lax.rsqrt(jnp.mean(x * x, axis=-1, keepdims=True) + EPS)


def _rms_bwd(dh, n, r, g):
    dn = dh * g
    return r * (dn - n * jnp.mean(dn * n, axis=-1, keepdims=True))


def _gelu(x):
    t = jnp.tanh(_GELU_C * (x + _GELU_A * (x * x * x)))
    return 0.5 * x * (1.0 + t), t


def _gelu_grad(x, t):
    return 0.5 * (1.0 + t) + 0.5 * x * (1.0 - t * t) * (_GELU_C * (1.0 + 3.0 * _GELU_A * x * x))


def _sigmoid(x):
    return 1.0 / (1.0 + jnp.exp(-x))


def _tie(x, dep):
    return x if dep is None else lax.optimization_barrier((x, dep))[0]


def _row(tm, w):
    return pl.BlockSpec((tm, w), lambda i: (i, 0))


def _full(shape):
    nd = len(shape)
    return pl.BlockSpec(tuple(shape), lambda *_: (0,) * nd)


def _resident(shape):
    nd = len(shape)
    return pl.BlockSpec(tuple(shape), lambda *_: (0,) * nd, pipeline_mode=pl.Buffered(1))


def _sds(shape, dtype):
    return jax.ShapeDtypeStruct(tuple(shape), dtype)


def _hbm(*arrays):
    return [pltpu.with_memory_space_constraint(a, pltpu.HBM) for a in arrays]


HBM = pl.BlockSpec(memory_space=pltpu.HBM)
ANY = pl.BlockSpec(memory_space=pl.ANY)
SEM = pl.BlockSpec(memory_space=pltpu.SEMAPHORE)


def _band_buckets():
    i = np.arange(CHUNK)[:, None]
    j = np.arange(2 * CHUNK)[None, :]
    dist = i + CHUNK - j
    valid = (dist >= 0) & (dist < CHUNK)
    d = np.clip(dist, 0, None)
    max_exact = N_BUCKETS // 2
    large = max_exact + (np.log(np.maximum(d, 1) / max_exact) / np.log(MAX_DISTANCE / max_exact)
                         * (N_BUCKETS - max_exact)).astype(np.int32)
    large = np.minimum(large, N_BUCKETS - 1)
    buckets = np.where(d < max_exact, d, large).astype(np.int32)
    return np.where(valid, buckets, -1).astype(np.int32)


_A_COLS = slice(0, A_DIM)
_B_COLS = slice(A_DIM, A_DIM + B_DIM)
_G_COLS = slice(A_DIM + B_DIM, A_DIM + B_DIM + G_DIM)


def _inproj(x2, g_mix, w_in, tm, after=None):
    T = x2.shape[0]
    order = [] if after is None else [after]

    def body(*refs):
        x_ref, g_ref, w_ref = refs[:3]
        pg_ref, pa_ref, pb_ref, h_ref = refs[3 + len(order):]
        x = x_ref[...]
        h = (x * _rms_r(x) * g_ref[...]).astype(BF16)
        h_ref[...] = h
        pg_ref[...] = _dot(h, w_ref[:, _G_COLS]).astype(BF16)
        pa_ref[...] = _dot(h, w_ref[:, _A_COLS]).astype(BF16)
        pb_ref[...] = _dot(h, w_ref[:, _B_COLS]).astype(BF16)

    return pl.pallas_call(
        body, name="inproj", grid=(T // tm,),
        in_specs=[_row(tm, D_MODEL), _full(g_mix.shape), _resident(w_in.shape)] + [ANY] * len(order),
        out_specs=[_row(tm, G_DIM), _row(tm, A_DIM), _row(tm, B_DIM), _row(tm, D_MODEL)],
        out_shape=[_sds((T, G_DIM), BF16), _sds((T, A_DIM), BF16), _sds((T, B_DIM), BF16), _sds((T, D_MODEL), BF16)],
        compiler_params=_cp(("arbitrary",), 48),
    )(*_hbm(x2, g_mix, w_in), *order)


def _sgu_parts(p, g):
    pu = p[:, :A_WIDTH]
    pv = p[:, A_WIDTH:]
    u, tu = _gelu(pu)
    vv, tv = _gelu(pv)
    rv = _rms_r(vv)
    vn = (vv * rv * g).astype(BF16)
    return pu, pv, u, tu, vv, tv, rv, vn


def _tril():
    r = lax.broadcasted_iota(jnp.int32, (CHUNK, CHUNK), 0)
    c = lax.broadcasted_iota(jnp.int32, (CHUNK, CHUNK), 1)
    return r >= c


def _sgu_fwd(proj_a, g_sgu, w_s, b_st, tm):
    T = proj_a.shape[0]

    def body(p_ref, g_ref, ws_ref, bs_ref, y_ref):
        tril = _tril()
        _, _, u, _, _, _, _, vn = _sgu_parts(p_ref[...].astype(F32), g_ref[...])
        for gi in range(A_GROUPS):
            wm = jnp.where(tril, ws_ref[gi], 0.0).astype(BF16)
            bcol = bs_ref[:, gi:gi + 1]
            cs = slice(gi * CHUNK, (gi + 1) * CHUNK)
            for c in range(tm // CHUNK):
                rs = slice(c * CHUNK, (c + 1) * CHUNK)
                s = _dot(wm, vn[rs, cs]) + bcol
                y_ref[rs, cs] = (u[rs, cs] * s).astype(BF16)

    return pl.pallas_call(
        body, name="sgu_fwd", grid=(T // tm,),
        in_specs=[_row(tm, A_DIM), _full(g_sgu.shape), _full(w_s.shape), _full(b_st.shape)],
        out_specs=_row(tm, A_WIDTH), out_shape=_sds((T, A_WIDTH), BF16),
        compiler_params=_cp(("arbitrary",)),
    )(*_hbm(proj_a, g_sgu, w_s, b_st))


HEAD_ROWS = N_HEADS * CHUNK


def _head_rows(h):
    return slice(h * CHUNK, (h + 1) * CHUNK)


def _attn_setup(bias_scr, sink_scr, kvar_scr, qkv_ref, bk_ref, rel_ref, sink_ref):
    @pl.when(pl.program_id(0) == 0)
    def _():
        bk = bk_ref[...]
        for h in range(N_HEADS):
            acc = jnp.full((CHUNK, 2 * CHUNK), NEG_INF, F32)
            for b in range(N_BUCKETS):
                acc = jnp.where(bk == b, rel_ref[b, h], acc)
            bias_scr[_head_rows(h), :] = acc
            sink_scr[_head_rows(h), :] = jnp.full((CHUNK, LANES), sink_ref[0, h], F32)

    seq = qkv_ref.shape[0]
    rows_per = 2 * CHUNK
    for is_v in range(2):
        c0 = Q_DIM + is_v * KV_DIM
        for r in range(seq // rows_per):
            rs = slice(r * rows_per, (r + 1) * rows_per)
            a = qkv_ref[rs, c0:c0 + KV_DIM].astype(F32)
            lane = lax.broadcasted_iota(jnp.int32, a.shape, 1)
            lo = jnp.where(lane < HEAD_DIM, a, 0.0)
            hi = jnp.where(lane >= HEAD_DIM, a, 0.0)
            kvar_scr[4 * is_v + 0, rs, :] = lo.astype(BF16)
            kvar_scr[4 * is_v + 1, rs, :] = pltpu.roll(lo, HEAD_DIM, 1).astype(BF16)
            kvar_scr[4 * is_v + 2, rs, :] = pltpu.roll(hi, HEAD_DIM, 1).astype(BF16)
            kvar_scr[4 * is_v + 3, rs, :] = hi.astype(BF16)


def _rowsum(a, ones):
    hi = a.astype(BF16)
    lo = (a - hi.astype(F32)).astype(BF16)
    return _dot(hi, ones) + _dot(lo, ones)


def _both(a):
    return jnp.concatenate([a, a], axis=1)


def _attn_probs(qkv_ref, r0, n, kv, bias_scr, sink_scr, ones):
    s = jnp.concatenate([_dot_nt(qkv_ref[pl.ds(r0, CHUNK), (h // 2) * LANES:(h // 2 + 1) * LANES], kv[h // 4][h % 2])
                         for h in range(N_HEADS)], axis=0)
    s = s * (HEAD_DIM ** -0.5) + bias_scr[...]
    col = lax.broadcasted_iota(jnp.int32, s.shape, 1)
    s = jnp.where((col < CHUNK) & (n == 0), NEG_INF, s)
    sink = sink_scr[...]
    m = jnp.maximum(jnp.max(s, axis=-1, keepdims=True), sink)
    p = jnp.exp(s - _both(m))
    es = jnp.exp(sink - m)
    inv = 1.0 / (_rowsum(p, ones) + es)
    return p * _both(inv), es * inv


def _attn_block_inputs(kvar_scr, n):
    r0 = pl.multiple_of(n * CHUNK, CHUNK)
    rp = pl.multiple_of(jnp.maximum(n - 1, 0) * CHUNK, CHUNK)

    def both(idx):
        return jnp.concatenate([kvar_scr[idx, pl.ds(rp, CHUNK), :], kvar_scr[idx, pl.ds(r0, CHUNK), :]], axis=0)

    kv = ((both(0), both(1)), (both(2), both(3)))
    vv = ((both(4), both(5)), (both(6), both(7)))
    return r0, kv, vv


def _attn_fwd(proj_b, sinks, rel_bias, n_seq, seq):
    nb = seq // CHUNK
    bk = jnp.asarray(_band_buckets())

    def body(qkv_ref, bk_ref, rel_ref, sink_ref, o_ref, bias_scr, sink_scr, kvar_scr):
        _attn_setup(bias_scr, sink_scr, kvar_scr, qkv_ref, bk_ref, rel_ref, sink_ref)
        ones = jnp.ones((2 * CHUNK, LANES), BF16)

        def blk(n, carry):
            r0, kv, vv = _attn_block_inputs(kvar_scr, n)
            prob, _ = _attn_probs(qkv_ref, r0, n, kv, bias_scr, sink_scr, ones)
            pb = prob.astype(BF16)
            for pr in range(N_HEADS // 2):
                acc = _dot(pb[_head_rows(2 * pr)], vv[pr // 2][0]) + _dot(pb[_head_rows(2 * pr + 1)], vv[pr // 2][1])
                o_ref[pl.ds(r0, CHUNK), pr * LANES:(pr + 1) * LANES] = acc.astype(BF16)
            return carry

        lax.fori_loop(0, nb, blk, 0)

    smem = pl.BlockSpec(memory_space=pltpu.SMEM)
    return pl.pallas_call(
        body, name="attn_fwd", grid=(n_seq,),
        in_specs=[_row(seq, B_DIM), _full(bk.shape), smem, smem],
        out_specs=_row(seq, Q_DIM), out_shape=_sds((n_seq * seq, Q_DIM), BF16),
        scratch_shapes=[pltpu.VMEM((HEAD_ROWS, 2 * CHUNK), F32), pltpu.VMEM((HEAD_ROWS, LANES), F32),
                        pltpu.VMEM((8, seq, KV_DIM), BF16)],
        compiler_params=_cp(("arbitrary",), 40),
    )(*_hbm(proj_b, bk), rel_bias, sinks)


def _dot_stacked(a, w_ref):
    return jnp.concatenate([_dot(a, w_ref[i]) for i in range(N_CHIPS)], axis=1)


def _dot_nt_stacked(a, w_ref):
    w = w_ref.shape[2]
    acc = _dot_nt(a[:, :w], w_ref[0])
    for i in range(1, N_CHIPS):
        acc = acc + _dot_nt(a[:, i * w:(i + 1) * w], w_ref[i])
    return acc


def _merge_fwd(x2, y_a, y_b, proj_g, w_pa, w_pb, w_out, tm):
    T = x2.shape[0]

    def body(x_ref, ya_ref, yb_ref, g_ref, wpa_ref, wpb_ref, wo_ref, x1_ref, mg_ref):
        g = g_ref[...].astype(F32)
        pa = _dot_stacked(ya_ref[...], wpa_ref)
        pb = _dot_stacked(yb_ref[...], wpb_ref)
        merged = (_sigmoid(g[:, :D_MODEL]) * pa + _sigmoid(g[:, D_MODEL:]) * pb).astype(BF16)
        mg_ref[...] = merged
        x1_ref[...] = x_ref[...] + _dot(merged, wo_ref[...])

    return pl.pallas_call(
        body, name="merge_fwd", grid=(T // tm,),
        in_specs=[_row(tm, D_MODEL), _row(tm, A_WIDTH), _row(tm, Q_DIM), _row(tm, G_DIM),
                  _resident(w_pa.shape), _resident(w_pb.shape), _resident(w_out.shape)],
        out_specs=[_row(tm, D_MODEL), _row(tm, D_MODEL)],
        out_shape=[_sds((T, D_MODEL), F32), _sds((T, D_MODEL), BF16)],
        compiler_params=_cp(("arbitrary",), 40),
    )(*_hbm(x2, y_a, y_b, proj_g, w_pa, w_pb, w_out))


def _upproj(x1, g_ffn, w_up, w_conv, b_conv, tm, seq):
    T = x1.shape[0]
    cw = w_up.shape[2]
    tiles_per_seq = seq // tm

    def body(x_ref, g_ref, w_ref, wc_ref, bc_ref, u_ref, h_ref, gate_ref, val_ref, tail_scr):
        at_start = (pl.program_id(0) % tiles_per_seq) == 0
        x = x_ref[...]
        h = (x * _rms_r(x) * g_ref[...]).astype(BF16)
        h_ref[...] = h
        for i in range(N_CHIPS):
            cs = slice(i * cw, (i + 1) * cw)
            u = _dot(h, w_ref[i])
            u_ref[:, cs] = u.astype(BF16)
            hl = jnp.where(at_start, 0.0, tail_scr[SUBLANES - 2:SUBLANES, cs])
            tail_scr[:, cs] = u[tm - SUBLANES:]
            up = _conv_out((u, _shift_down(u, hl, 1), _shift_down(u, hl, 2)), wc_ref[:, cs], bc_ref[:, cs])
            out_ref = gate_ref if i < N_CHIPS // 2 else val_ref
            out_ref[:, (i % 2) * cw:(i % 2 + 1) * cw] = up.astype(BF16)

    return pl.pallas_call(
        body, name="upproj", grid=(T // tm,),
        in_specs=[_row(tm, D_MODEL), _full(g_ffn.shape), _resident(w_up.shape), _full(w_conv.shape), _full(b_conv.shape)],
        out_specs=[_row(tm, 2 * D_FF), _row(tm, D_MODEL), _row(tm, D_FF), _row(tm, D_FF)],
        out_shape=[_sds((T, 2 * D_FF), BF16), _sds((T, D_MODEL), BF16), _sds((T, D_FF), BF16), _sds((T, D_FF), BF16)],
        scratch_shapes=[pltpu.VMEM((SUBLANES, 2 * D_FF), F32)],
        compiler_params=_cp(("arbitrary",), 56),
    )(*_hbm(x1, g_ffn, w_up, w_conv, b_conv))


def _shift_down(u, halo, k):
    rolled = pltpu.roll(u, k, 0)
    head = rolled[:SUBLANES]
    row = lax.broadcasted_iota(jnp.int32, head.shape, 0)
    if k == 1:
        head = jnp.where(row == 0, halo[1:2], head)
    else:
        head = jnp.where(row == 0, halo[0:1], jnp.where(row == 1, halo[1:2], head))
    return jnp.concatenate([head, rolled[SUBLANES:]], axis=0)


def _shift_up(d, halo, k):
    tm = d.shape[0]
    rolled = pltpu.roll(d, tm - k, 0)
    tail = rolled[tm - SUBLANES:]
    row = lax.broadcasted_iota(jnp.int32, tail.shape, 0)
    if k == 1:
        tail = jnp.where(row == SUBLANES - 1, halo[0:1], tail)
    else:
        tail = jnp.where(row == SUBLANES - 2, halo[0:1], jnp.where(row == SUBLANES - 1, halo[1:2], tail))
    return jnp.concatenate([rolled[:tm - SUBLANES], tail], axis=0)


def _conv_out(taps, wc, bc):
    u, u1, u2 = taps
    return wc[0:1] * u2 + wc[1:2] * u1 + wc[2:3] * u + bc


def _ffn_down_loss(gate, val, x1, target, w_down, g_final, tm):
    T = x1.shape[0]
    half = D_FF // 2

    def body(gt_ref, vl_ref, x1_ref, t_ref, wd_ref, g_ref, dx2_ref, loss_ref, gg_ref):
        i = pl.program_id(0)
        acc = jnp.zeros((tm, D_MODEL), F32)
        for j in range(2):
            gc = slice(j * half, (j + 1) * half)
            gate = gt_ref[:, gc].astype(F32)
            act = (gate * _sigmoid(gate) * vl_ref[:, gc].astype(F32)).astype(BF16)
            acc = acc + _dot(act, wd_ref[gc, :])
        x2 = x1_ref[...] + acc
        r = _rms_r(x2)
        n = x2 * r
        g = g_ref[...]
        diff = n * g - t_ref[...]
        dy = diff * (1.0 / D_MODEL)
        dx2_ref[...] = _rms_bwd(dy, n, r, g)

        @pl.when(i == 0)
        def _():
            loss_ref[...] = jnp.zeros_like(loss_ref)
            gg_ref[...] = jnp.zeros_like(gg_ref)

        loss_ref[...] += 0.5 * jnp.sum(jnp.mean(diff * diff, axis=-1, keepdims=True), axis=0, keepdims=True)
        gg_ref[...] += jnp.sum(dy * n, axis=0, keepdims=True)

    return pl.pallas_call(
        body, name="ffn_down_loss", grid=(T // tm,),
        in_specs=[_row(tm, D_FF), _row(tm, D_FF), _row(tm, D_MODEL), _row(tm, D_MODEL),
                  _resident(w_down.shape), _full(g_final.shape)],
        out_specs=[_row(tm, D_MODEL), _full((1, 1)), _full((1, D_MODEL))],
        out_shape=[_sds((T, D_MODEL), F32), _sds((1, 1), F32), _sds((1, D_MODEL), F32)],
        compiler_params=_cp(("arbitrary",), 48),
    )(*_hbm(gate, val, x1, target, w_down, g_final))


def _ffn_bwd_act(gate, val, dx2, w_down, tm):
    T = dx2.shape[0]
    half = D_FF // 2
    nt = T // tm

    def body(g_ref, v_ref, dx_ref, wd_ref, dg_ref, dv_ref, gwd_out, gbg_ref, gbv_ref, gwd_ref):
        i = pl.program_id(1)

        @pl.when(i == 0)
        def _():
            for r in (gwd_ref, gbg_ref, gbv_ref):
                r[...] = jnp.zeros_like(r)

        dx = dx_ref[...].astype(BF16)
        for c0 in range(0, half, COL_CHUNK):
            cs = slice(c0, min(c0 + COL_CHUNK, half))
            gate = g_ref[:, cs].astype(F32)
            val = v_ref[:, cs].astype(F32)
            sg = _sigmoid(gate)
            silu = gate * sg
            d_act = _dot_nt(dx, wd_ref[cs, :])
            d_val = d_act * silu
            d_gate = d_act * val * (sg * (1.0 + gate * (1.0 - sg)))
            dg_ref[:, cs] = d_gate.astype(BF16)
            dv_ref[:, cs] = d_val.astype(BF16)
            gwd_ref[cs, :] += _dot_tn((silu * val).astype(BF16), dx)
            gbg_ref[:, cs] += jnp.sum(d_gate, axis=0, keepdims=True)
            gbv_ref[:, cs] += jnp.sum(d_val, axis=0, keepdims=True)

        @pl.when(i == nt - 1)
        def _():
            gwd_out[...] = gwd_ref[...].astype(BF16)

    tile = pl.BlockSpec((tm, half), lambda j, i: (i, j))
    vec = pl.BlockSpec((1, half), lambda j, i: (0, j))
    wrows = pl.BlockSpec((half, D_MODEL), lambda j, i: (j, 0))
    return pl.pallas_call(
        body, name="ffn_bwd_act", grid=(2, nt),
        in_specs=[tile, tile, pl.BlockSpec((tm, D_MODEL), lambda j, i: (i, 0)), wrows],
        out_specs=[tile, tile, wrows, vec, vec],
        out_shape=[_sds((T, D_FF), BF16), _sds((T, D_FF), BF16), _sds((D_FF, D_MODEL), BF16),
                   _sds((1, D_FF), F32), _sds((1, D_FF), F32)],
        scratch_shapes=[pltpu.VMEM((half, D_MODEL), F32)],
        compiler_params=_cp(("arbitrary", "arbitrary"), 56),
    )(*_hbm(gate, val, dx2, w_down))


def _ffn_bwd_up(d_gate, d_val, upre, dx2, x1, g_ffn, w_conv, w_up, tm, seq):
    T = dx2.shape[0]
    tiles_per_seq = seq // tm
    k16 = tm // BF16_ROWS
    n16 = T // BF16_ROWS
    cw = D_FF // 2

    def body(dg_ref, dv_ref, hg_ref, hv_ref, u_ref, dx2_ref, x1_ref, g_ref, wc_ref, wu_ref, du_ref, dx1_ref, gg_ref, gwc_ref):
        i = pl.program_id(0)
        at_end = (i % tiles_per_seq) == tiles_per_seq - 1

        @pl.when(i == 0)
        def _():
            gg_ref[...] = jnp.zeros_like(gg_ref)
            gwc_ref[...] = jnp.zeros_like(gwc_ref)

        dh = jnp.zeros((tm, D_MODEL), F32)
        for j in range(4):
            src, hsrc = (dg_ref, hg_ref) if j < 2 else (dv_ref, hv_ref)
            ls = slice((j % 2) * cw, (j % 2 + 1) * cw)
            cs = slice(j * cw, (j + 1) * cw)
            d = src[:, ls].astype(F32)
            hl = hsrc[:, ls].astype(F32)[0:2]
            hl = jnp.where(at_end, 0.0, hl)
            wc = wc_ref[:, cs]
            d1 = _shift_up(d, hl, 1)
            d2 = _shift_up(d, hl, 2)
            du = (wc[2:3] * d + wc[1:2] * d1 + wc[0:1] * d2).astype(BF16)
            du_ref[:, cs] = du
            dh = dh + _dot_nt(du, wu_ref[j])
            u = u_ref[:, cs].astype(F32)
            gwc_ref[0:1, cs] += jnp.sum(d2 * u, axis=0, keepdims=True)
            gwc_ref[1:2, cs] += jnp.sum(d1 * u, axis=0, keepdims=True)
            gwc_ref[2:3, cs] += jnp.sum(d * u, axis=0, keepdims=True)
        x = x1_ref[...]
        r = _rms_r(x)
        n = x * r
        dx1_ref[...] = dx2_ref[...] + _rms_bwd(dh, n, r, g_ref[...])
        gg_ref[...] += jnp.sum(dh * n, axis=0, keepdims=True)

    nxt = pl.BlockSpec((BF16_ROWS, D_FF), lambda i: (jnp.minimum((i + 1) * k16, n16 - 1), 0))
    return pl.pallas_call(
        body, name="ffn_bwd_up", grid=(T // tm,),
        in_specs=[_row(tm, D_FF), _row(tm, D_FF), nxt, nxt, _row(tm, 2 * D_FF), _row(tm, D_MODEL), _row(tm, D_MODEL),
                  _full(g_ffn.shape), _full(w_conv.shape), _resident(w_up.shape)],
        out_specs=[_row(tm, 2 * D_FF), _row(tm, D_MODEL), _full((1, D_MODEL)), _full((3, 2 * D_FF))],
        out_shape=[_sds((T, 2 * D_FF), BF16), _sds((T, D_MODEL), F32), _sds((1, D_MODEL), F32), _sds((3, 2 * D_FF), F32)],
        compiler_params=_cp(("arbitrary",), 56),
    )(*_hbm(d_gate, d_val, d_gate, d_val, upre, dx2, x1, g_ffn, w_conv, w_up))


def _matmul_tn(a, b, tn, tk, name):
    T, M = a.shape
    N = b.shape[1]
    nk = T // tk

    def body(a_ref, b_ref, o_ref, acc_ref):
        k = pl.program_id(1)

        @pl.when(k == 0)
        def _():
            acc_ref[...] = jnp.zeros_like(acc_ref)

        acc_ref[...] += _dot_tn(a_ref[...], b_ref[...])

        @pl.when(k == nk - 1)
        def _():
            o_ref[...] = acc_ref[...].astype(BF16)

    return pl.pallas_call(
        body, name=name, grid=(N // tn, nk),
        in_specs=[pl.BlockSpec((tk, M), lambda j, k: (k, 0)), pl.BlockSpec((tk, tn), lambda j, k: (k, j))],
        out_specs=pl.BlockSpec((M, tn), lambda j, k: (0, j)), out_shape=_sds((M, N), BF16),
        scratch_shapes=[pltpu.VMEM((M, tn), F32)],
        compiler_params=_cp(("arbitrary", "arbitrary"), 48),
    )(*_hbm(a, b))


def _merge_bwd(dx1, merged, y_a, y_b, proj_g, w_pa, w_pb, w_out, tm, after=None):
    T = dx1.shape[0]

    nt = T // tm
    pshape = (A_WIDTH, D_MODEL)
    order = [] if after is None else [after]

    def body(*refs):
        dx_ref, mg_ref, ya_ref, yb_ref, g_ref, wpa_ref, wpb_ref, wo_ref = refs[:8]
        dg_ref, dya_ref, dyb_ref, gwo_out, gwpa_out, gwpb_out, gwo_ref, gwpa_ref, gwpb_ref = refs[8 + len(order):]
        i = pl.program_id(0)
        dx = dx_ref[...].astype(BF16)
        dm = _dot_nt(dx, wo_ref[...])
        g = g_ref[...].astype(F32)
        ya = ya_ref[...]
        yb = yb_ref[...]
        pa = _dot_stacked(ya, wpa_ref)
        pb = _dot_stacked(yb, wpb_ref)
        sa = _sigmoid(g[:, :D_MODEL])
        sb = _sigmoid(g[:, D_MODEL:])
        dpa = (dm * sa).astype(BF16)
        dpb = (dm * sb).astype(BF16)
        dg_ref[:, :D_MODEL] = (dm * pa * (sa * (1.0 - sa))).astype(BF16)
        dg_ref[:, D_MODEL:] = (dm * pb * (sb * (1.0 - sb))).astype(BF16)
        dya_ref[...] = _dot_nt_stacked(dpa, wpa_ref).astype(BF16)
        dyb_ref[...] = _dot_nt_stacked(dpb, wpb_ref).astype(BF16)

        @pl.when(i == 0)
        def _():
            for r in (gwo_ref, gwpa_ref, gwpb_ref):
                r[...] = jnp.zeros_like(r)

        gwo_ref[...] += _dot_tn(mg_ref[...], dx)
        gwpa_ref[...] += _dot_tn(ya, dpa)
        gwpb_ref[...] += _dot_tn(yb, dpb)

        @pl.when(i == nt - 1)
        def _():
            gwo_out[...] = gwo_ref[...].astype(BF16)
            gwpa_out[...] = gwpa_ref[...].astype(BF16)
            gwpb_out[...] = gwpb_ref[...].astype(BF16)

    return pl.pallas_call(
        body, name="merge_bwd", grid=(nt,),
        in_specs=[_row(tm, D_MODEL), _row(tm, D_MODEL), _row(tm, A_WIDTH), _row(tm, Q_DIM), _row(tm, G_DIM),
                  _resident(w_pa.shape), _resident(w_pb.shape), _resident(w_out.shape)] + [ANY] * len(order),
        out_specs=[_row(tm, G_DIM), _row(tm, A_WIDTH), _row(tm, Q_DIM),
                   _full(w_out.shape), _full(pshape), _full(pshape)],
        out_shape=[_sds((T, G_DIM), BF16), _sds((T, A_WIDTH), BF16), _sds((T, Q_DIM), BF16),
                   _sds(w_out.shape, BF16), _sds(pshape, BF16), _sds(pshape, BF16)],
        scratch_shapes=[pltpu.VMEM(w_out.shape, F32), pltpu.VMEM(pshape, F32), pltpu.VMEM(pshape, F32)],
        compiler_params=_cp(("arbitrary",), 56),
    )(*_hbm(dx1, merged, y_a, y_b, proj_g, w_pa, w_pb, w_out), *order)


def _sgu_bwd(proj_a, d_ya, g_sgu, w_s, b_st, tm, after=None):
    T = proj_a.shape[0]
    order = [] if after is None else [after]

    def body(*refs):
        p_ref, dy_ref, g_ref, ws_ref, bs_ref = refs[:5]
        dp_ref, gws_ref, gbs_ref, gg_ref = refs[5 + len(order):]
        tril = _tril()
        g = g_ref[...]
        pu, pv, u, tu, vv, tv, rv, vn = _sgu_parts(p_ref[...].astype(F32), g)
        dy = dy_ref[...].astype(F32)

        @pl.when(pl.program_id(0) == 0)
        def _():
            for r in (gws_ref, gbs_ref, gg_ref):
                r[...] = jnp.zeros_like(r)

        du_cols = []
        dvn_cols = []
        for gi in range(A_GROUPS):
            wm = jnp.where(tril, ws_ref[gi], 0.0).astype(BF16)
            wmt = wm.astype(F32).T.astype(BF16)
            bcol = bs_ref[:, gi:gi + 1]
            cs = slice(gi * CHUNK, (gi + 1) * CHUNK)
            du_rows = []
            dvn_rows = []
            gw = jnp.zeros((CHUNK, CHUNK), F32)
            gb = jnp.zeros((CHUNK, 1), F32)
            for c in range(tm // CHUNK):
                rs = slice(c * CHUNK, (c + 1) * CHUNK)
                vn_c = vn[rs, cs]
                s = _dot(wm, vn_c) + bcol
                dy_c = dy[rs, cs]
                ds = dy_c * u[rs, cs]
                du_rows.append(dy_c * s)
                dsb = ds.astype(BF16)
                gw = gw + _dot_nt(dsb, vn_c)
                gb = gb + jnp.sum(ds, axis=-1, keepdims=True)
                dvn_rows.append(_dot(wmt, dsb))
            gws_ref[gi] += jnp.where(tril, gw, 0.0)
            gbs_ref[:, gi:gi + 1] += gb
            du_cols.append(jnp.concatenate(du_rows, axis=0))
            dvn_cols.append(jnp.concatenate(dvn_rows, axis=0))
        du = jnp.concatenate(du_cols, axis=1)
        dvn = jnp.concatenate(dvn_cols, axis=1)
        vhat = vv * rv
        gg_ref[...] += jnp.sum(dvn * vhat, axis=0, keepdims=True)
        dvv = _rms_bwd(dvn, vhat, rv, g)
        dp_ref[:, :A_WIDTH] = (du * _gelu_grad(pu, tu)).astype(BF16)
        dp_ref[:, A_WIDTH:] = (dvv * _gelu_grad(pv, tv)).astype(BF16)

    return pl.pallas_call(
        body, name="sgu_bwd", grid=(T // tm,),
        in_specs=[_row(tm, A_DIM), _row(tm, A_WIDTH), _full(g_sgu.shape), _full(w_s.shape), _full(b_st.shape)] + [ANY] * len(order),
        out_specs=[_row(tm, A_DIM), _full(w_s.shape), _full(b_st.shape), _full(g_sgu.shape)],
        out_shape=[_sds((T, A_DIM), BF16), _sds(w_s.shape, F32), _sds(b_st.shape, F32), _sds(g_sgu.shape, F32)],
        compiler_params=_cp(("arbitrary",)),
    )(*_hbm(proj_a, d_ya, g_sgu, w_s, b_st), *order)


def _attn_bwd(proj_b, d_yb, sinks, rel_bias, n_seq, seq):
    nb = seq // CHUNK
    bk = jnp.asarray(_band_buckets())

    def body(qkv_ref, do_ref, bk_ref, rel_ref, sink_ref, d_ref, gs_ref, gr_ref,
             bias_scr, sink_scr, kvar_scr, dbias_scr, dk_scr, dv_scr, ds_scr):
        b = pl.program_id(0)
        _attn_setup(bias_scr, sink_scr, kvar_scr, qkv_ref, bk_ref, rel_ref, sink_ref)
        ones = jnp.ones((2 * CHUNK, LANES), BF16)

        @pl.when(b == 0)
        def _():
            dbias_scr[...] = jnp.zeros_like(dbias_scr)
            ds_scr[...] = jnp.zeros_like(ds_scr)

        dk_scr[...] = jnp.zeros_like(dk_scr)
        dv_scr[...] = jnp.zeros_like(dv_scr)

        def transposed(a):
            return a.astype(F32).T.astype(BF16)

        def blk(n, carry):
            r0, kv, vv = _attn_block_inputs(kvar_scr, n)
            prob, psink = _attn_probs(qkv_ref, r0, n, kv, bias_scr, sink_scr, ones)
            dp = jnp.concatenate([_dot_nt(do_ref[pl.ds(r0, CHUNK), (h // 2) * LANES:(h // 2 + 1) * LANES], vv[h // 4][h % 2])
                                  for h in range(N_HEADS)], axis=0)
            delta = _rowsum(prob * dp, ones)
            dsc = prob * (dp - _both(delta))
            ds_scr[...] += psink * delta
            dbias_scr[...] += dsc
            dsb = (dsc * (HEAD_DIM ** -0.5)).astype(BF16)
            pb = prob.astype(BF16)
            dkt = [jnp.zeros((HEAD_DIM, 2 * CHUNK), F32) for _ in range(2)]
            dvt = [jnp.zeros((HEAD_DIM, 2 * CHUNK), F32) for _ in range(2)]
            for pr in range(N_HEADS // 2):
                ps = slice(pr * LANES, (pr + 1) * LANES)
                qpt = transposed(qkv_ref[pl.ds(r0, CHUNK), ps])
                dopt = transposed(do_ref[pl.ds(r0, CHUNK), ps])
                kvh = pr // 2
                dq = jnp.zeros((CHUNK, LANES), F32)
                for hh in range(2):
                    hr = _head_rows(2 * pr + hh)
                    rows = slice(hh * HEAD_DIM, (hh + 1) * HEAD_DIM)
                    dq = dq + _dot(dsb[hr], kv[kvh][hh])
                    dkt[kvh] = dkt[kvh] + _dot(qpt, dsb[hr])[rows]
                    dvt[kvh] = dvt[kvh] + _dot(dopt, pb[hr])[rows]
                d_ref[pl.ds(r0, CHUNK), ps] = dq.astype(BF16)
            dk_scr[:, pl.ds(r0, 2 * CHUNK)] += jnp.concatenate(dkt, axis=0)
            dv_scr[:, pl.ds(r0, 2 * CHUNK)] += jnp.concatenate(dvt, axis=0)
            return carry

        lax.fori_loop(0, nb, blk, 0)
        for n in range(nb):
            rows = slice(n * CHUNK, (n + 1) * CHUNK)
            cols = slice((n + 1) * CHUNK, (n + 2) * CHUNK)
            d_ref[rows, Q_DIM:Q_DIM + KV_DIM] = dk_scr[:, cols].T.astype(BF16)
            d_ref[rows, Q_DIM + KV_DIM:] = dv_scr[:, cols].T.astype(BF16)

        @pl.when(b == n_seq - 1)
        def _():
            bkv = bk_ref[...]
            for h in range(N_HEADS):
                gs_ref[0:1, h:h + 1] = -jnp.sum(ds_scr[_head_rows(h), 0:1], axis=0, keepdims=True)
                db = dbias_scr[_head_rows(h), :]
                for bb in range(N_BUCKETS):
                    part = jnp.sum(jnp.where(bkv == bb, db, 0.0), axis=-1, keepdims=True)
                    gr_ref[bb:bb + 1, h:h + 1] = jnp.sum(part, axis=0, keepdims=True)

    smem = pl.BlockSpec(memory_space=pltpu.SMEM)
    return pl.pallas_call(
        body, name="attn_bwd", grid=(n_seq,),
        in_specs=[_row(seq, B_DIM), _row(seq, Q_DIM), _full(bk.shape), smem, smem],
        out_specs=[_row(seq, B_DIM), _full((1, N_HEADS)), _full((N_BUCKETS, N_HEADS))],
        out_shape=[_sds((n_seq * seq, B_DIM), BF16), _sds((1, N_HEADS), F32), _sds((N_BUCKETS, N_HEADS), F32)],
        scratch_shapes=[pltpu.VMEM((HEAD_ROWS, 2 * CHUNK), F32), pltpu.VMEM((HEAD_ROWS, LANES), F32),
                        pltpu.VMEM((8, seq, KV_DIM), BF16), pltpu.VMEM((HEAD_ROWS, 2 * CHUNK), F32),
                        pltpu.VMEM((KV_DIM, seq + CHUNK), F32), pltpu.VMEM((KV_DIM, seq + CHUNK), F32),
                        pltpu.VMEM((HEAD_ROWS, LANES), F32)],
        compiler_params=_cp(("arbitrary",), 40),
    )(*_hbm(proj_b, d_yb, bk), rel_bias, sinks)


def _inproj_bwd(d_g, d_a, d_b, x2, dx1, g_mix, w_in, tm, after=None):
    T = x2.shape[0]
    order = [] if after is None else [after]

    def body(*refs):
        dg_ref, da_ref, db_ref, x_ref, dx1_ref, g_ref, w_ref = refs[:7]
        gx_ref, gg_ref = refs[7 + len(order):]
        dh = (_dot_nt(dg_ref[...], w_ref[:, _G_COLS]) + _dot_nt(da_ref[...], w_ref[:, _A_COLS])
              + _dot_nt(db_ref[...], w_ref[:, _B_COLS]))
        x = x_ref[...]
        r = _rms_r(x)
        n = x * r
        gx_ref[...] = dx1_ref[...] + _rms_bwd(dh, n, r, g_ref[...])

        @pl.when(pl.program_id(0) == 0)
        def _():
            gg_ref[...] = jnp.zeros_like(gg_ref)

        gg_ref[...] += jnp.sum(dh * n, axis=0, keepdims=True)

    return pl.pallas_call(
        body, name="inproj_bwd", grid=(T // tm,),
        in_specs=[_row(tm, G_DIM), _row(tm, A_DIM), _row(tm, B_DIM), _row(tm, D_MODEL), _row(tm, D_MODEL),
                  _full(g_mix.shape), _resident(w_in.shape)] + [ANY] * len(order),
        out_specs=[_row(tm, D_MODEL), _full((1, D_MODEL))],
        out_shape=[_sds((T, D_MODEL), F32), _sds((1, D_MODEL), F32)],
        compiler_params=_cp(("arbitrary",), 48),
    )(*_hbm(d_g, d_a, d_b, x2, dx1, g_mix, w_in), *order)


IN_SHARD = (A_DIM + B_DIM + G_DIM) // N_CHIPS


def _unstack_w_in(stack):
    tr = 256

    def body(s_ref, o_ref):
        for i in range(N_CHIPS):
            o_ref[:, i * IN_SHARD:(i + 1) * IN_SHARD] = s_ref[i]

    return pl.pallas_call(
        body, name="unstack_w_in", grid=(D_MODEL // tr,),
        in_specs=[pl.BlockSpec((N_CHIPS, tr, IN_SHARD), lambda r: (0, r, 0))],
        out_specs=pl.BlockSpec((tr, N_CHIPS * IN_SHARD), lambda r: (r, 0)),
        out_shape=_sds((D_MODEL, N_CHIPS * IN_SHARD), stack.dtype),
        compiler_params=_cp(("arbitrary",)),
    )(*_hbm(stack))


def _stack_grad_w_in(gw_a, gw_b, gw_g):
    tr = 256

    def body(a_ref, b_ref, g_ref, o_ref):
        full = jnp.concatenate([a_ref[...], b_ref[...], g_ref[...]], axis=1)
        for i in range(N_CHIPS):
            o_ref[i] = full[:, i * IN_SHARD:(i + 1) * IN_SHARD]

    return pl.pallas_call(
        body, name="stack_grad_w_in", grid=(D_MODEL // tr,),
        in_specs=[_row(tr, A_DIM), _row(tr, B_DIM), _row(tr, G_DIM)],
        out_specs=pl.BlockSpec((N_CHIPS, tr, IN_SHARD), lambda r: (0, r, 0)),
        out_shape=_sds((N_CHIPS, D_MODEL, IN_SHARD), gw_a.dtype),
        compiler_params=_cp(("arbitrary",)),
    )(*_hbm(gw_a, gw_b, gw_g))


def _local_step(x, target, g_mix, g_sgu, w_s, b_s, sinks, rel_bias, g_ffn, b_conv, g_final,
                w_in, w_conv, proj_weights, ffn_weights, on_grads, after=None):
    n_seq, seq, _ = x.shape
    T = n_seq * seq
    tm = min(ROW_TILE, seq)
    tw = min(GRAD_ROW_TILE, T)
    tf = min(WIDE_ROW_TILE, seq)
    x2 = x.reshape(T, D_MODEL)
    tgt = target.reshape(T, D_MODEL)
    b_st = b_s.T
    g_fin = g_final.reshape(1, D_MODEL)

    proj_g, proj_a, proj_b, h = _inproj(x2, g_mix, w_in, tm, after)
    y_a = _sgu_fwd(proj_a, g_sgu, w_s, b_st, tm)
    y_b = _attn_fwd(proj_b, sinks, rel_bias, n_seq, seq)
    w_pa, w_pb, w_out = proj_weights(y_b)
    x1, merged = _merge_fwd(x2, y_a, y_b, proj_g, w_pa, w_pb, w_out, tm)
    w_up, w_down = ffn_weights(x1)
    upre, h2, gate, val = _upproj(x1, g_ffn, w_up, w_conv, b_conv, tf, seq)
    dx2, loss, gg_final = _ffn_down_loss(gate, val, x1, tgt, w_down, g_fin, tm)

    d_gate, d_val, gw_down, gb_g, gb_v = _ffn_bwd_act(gate, val, dx2, w_down, tw)
    gb_conv = jnp.concatenate([gb_g, gb_v], axis=1)
    d_upre, dx1, gg_ffn, gw_conv = _ffn_bwd_up(d_gate, d_val, upre, dx2, x1, g_ffn, w_conv, w_up, tf, seq)
    gw_up = _matmul_tn(h2, d_upre, 2 * D_FF // 4, min(2 * GRAD_ROW_TILE, T), "grad_w_up")
    sent = on_grads("ffn", dict(w_up=gw_up, w_down=gw_down))
    d_g, d_ya, d_yb, gw_out, gw_pa, gw_pb = _merge_bwd(dx1, merged, y_a, y_b, proj_g, w_pa, w_pb, w_out, tf, sent)
    sent = on_grads("proj", dict(w_pa=gw_pa, w_pb=gw_pb, w_out=gw_out))
    d_a, gw_s, gb_st, gg_sgu = _sgu_bwd(proj_a, d_ya, g_sgu, w_s, b_st, tm, sent)
    d_b, g_sinks, g_rel = _attn_bwd(proj_b, _tie(d_yb, d_a), sinks, rel_bias, n_seq, seq)
    gw_g = _matmul_tn(h, _tie(d_g, d_b), D_MODEL, min(2 * GRAD_ROW_TILE, T), "grad_w_in_gate")
    gw_a = _matmul_tn(h, _tie(d_a, gw_g), A_DIM, min(2 * GRAD_ROW_TILE, T), "grad_w_in_a")
    gw_b = _matmul_tn(h, _tie(d_b, gw_a), B_DIM, min(2 * GRAD_ROW_TILE, T), "grad_w_in_b")
    gw_in = _stack_grad_w_in(gw_a, gw_b, gw_g)
    sent = on_grads("in", dict(w_in=gw_in))
    grad_x, gg_mix = _inproj_bwd(d_g, d_a, d_b, x2, dx1, g_mix, w_in, tm, sent)

    small = dict(g_mix=gg_mix, g_sgu=gg_sgu, w_s=gw_s, b_s=gb_st.T, sinks=g_sinks, rel_bias=g_rel,
                 g_ffn=gg_ffn, b_conv=gb_conv, g_final=gg_final, w_conv=gw_conv)
    big = dict(w_in=gw_in, w_pa=gw_pa, w_pb=gw_pb, w_out=gw_out, w_up=gw_up, w_down=gw_down)
    return loss, grad_x.reshape(x.shape), small, big


_MIXER = ("w_in", "w_pa", "w_pb", "w_out")
_FFN = ("w_up", "w_down")
_BIG = _MIXER + _FFN

CONV_ROWS = 6
_SMALL_AT = dict(loss=(0, 1, 1), g_final=(1, 1, D_MODEL), g_mix=(2, 1, D_MODEL), g_ffn=(3, 1, D_MODEL), g_sgu=(4, 1, A_WIDTH),
                 sinks=(5, 1, N_HEADS), b_s=(8, A_GROUPS, CHUNK), rel_bias=(16, N_BUCKETS, N_HEADS),
                 b_conv=(48, CONV_ROWS, D_MODEL), w_conv=(56, 3 * CONV_ROWS, D_MODEL), w_s=(80, A_GROUPS * CHUNK * CHUNK // D_MODEL, D_MODEL))
_SMALL_IN_CALL = ("g_final", "g_mix", "g_ffn", "g_sgu", "sinks", "b_s", "rel_bias")
SMALL_ROWS = 144


def _pack_small(vals):
    def wide(a):
        return jnp.pad(a, ((0, 0), (0, CONV_ROWS * D_MODEL - a.shape[1]))).reshape(-1, D_MODEL)

    laid = dict(vals, b_conv=wide(vals["b_conv"]), w_conv=wide(vals["w_conv"]), w_s=vals["w_s"].reshape(-1, D_MODEL))
    rows, at = [], 0
    for n, (r0, nr, nc) in _SMALL_AT.items():
        if r0 > at:
            rows.append(jnp.zeros((r0 - at, D_MODEL), F32))
        rows.append(jnp.pad(laid[n].astype(F32).reshape(nr, nc), ((0, 0), (0, D_MODEL - nc))))
        at = r0 + nr
    return jnp.concatenate(rows, axis=0)


def _unwide(a, r):
    return a.reshape(r, CONV_ROWS * D_MODEL)[:, :2 * D_FF]


def _mesh_pos():
    return lax.axis_index("x"), lax.axis_index("y"), lax.axis_index("c")


def _other_chips(x, y):
    return [(1 - x, y), (x, 1 - y), (1 - x, 1 - y)]


def _remote(src, dst, send_sem, recv_sem, to):
    return pltpu.make_async_remote_copy(src_ref=src, dst_ref=dst, send_sem=send_sem, recv_sem=recv_sem,
                                        device_id=to, device_id_type=MESH)


def _own_slot(own, n, at):
    return lax.dynamic_update_slice(lax.empty((n,) + own.shape, own.dtype), own[None], (at,) + (0,) * own.ndim)


def _allgather_weights(stacks, wc_stack):
    names = list(stacks)
    n = len(names)

    def body(*refs):
        ins, outs = refs[:n + 1], refs[n + 1:2 * n + 2]
        send_sems, recv_sems = refs[2 * n + 2:]
        x, y, c = _mesh_pos()
        _handshake(_chip_peers(x, y, c) + _sibling_peers(x, y, c))
        me = 2 * x + y
        sibling = (x, y, 1 - c)
        chips = _other_chips(x, y)

        def half(ref, chip, hc):
            hr = ref.shape[1] // 2
            return ref.at[chip, pl.ds(hc * hr, hr), :]

        first = []
        for k in range(n):
            first += [_remote(half(ins[k], me, c), half(outs[k], me, c), send_sems.at[6 * k + j], recv_sems.at[6 * k + j], (cx, cy, c))
                      for j, (cx, cy) in enumerate(chips)]
        first += [_remote(ins[n].at[me], outs[n].at[me], send_sems.at[6 * n + j], recv_sems.at[6 * n + j], (cx, cy, c))
                  for j, (cx, cy) in enumerate(chips)]
        for cp in first:
            cp.start()
        passed = []
        for k in range(n):
            for j, (cx, cy) in enumerate(chips):
                landed = half(outs[k], 2 * cx + cy, c)
                _remote(landed, landed, send_sems.at[6 * k + j], recv_sems.at[6 * k + j], (x, y, c)).wait_recv()
                passed.append(_remote(landed, landed, send_sems.at[6 * k + 3 + j], recv_sems.at[6 * k + 3 + j], sibling))
                passed[-1].start()
        for k in range(n):
            for j, (cx, cy) in enumerate(chips):
                theirs = half(outs[k], 2 * cx + cy, 1 - c)
                _remote(theirs, theirs, send_sems.at[6 * k + 3 + j], recv_sems.at[6 * k + 3 + j], (x, y, c)).wait_recv()
        for j, (cx, cy) in enumerate(chips):
            slot = outs[n].at[2 * cx + cy]
            _remote(slot, slot, send_sems.at[6 * n + j], recv_sems.at[6 * n + j], (x, y, c)).wait_recv()
        for cp in first + passed:
            cp.wait_send()

    arrays = [stacks[k] for k in names] + [wc_stack]
    outs = pl.pallas_call(
        body, name="allgather_weights",
        in_specs=[HBM] * (n + 1), out_specs=[HBM] * (n + 1), input_output_aliases={k: k for k in range(n + 1)},
        out_shape=[_sds(a.shape, a.dtype) for a in arrays],
        scratch_shapes=[pltpu.SemaphoreType.DMA((6 * n + 3,)), pltpu.SemaphoreType.DMA((6 * n + 3,))],
        compiler_params=pltpu.CompilerParams(collective_id=_COLLECTIVE["gather_in"]),
    )(*arrays)
    return dict(zip(names, outs[:n])), outs[n]


_KIND = {"w_in": "stack", "w_pa": "col", "w_pb": "col", "w_up": "col", "w_out": "row", "w_down": "row"}


def _half_view(ref, kind, h):
    if kind == "stack":
        k = ref.shape[1] // 2
        return ref.at[:, pl.ds(h * k, k), :]
    if kind == "col":
        k = ref.shape[0] // 2
        return ref.at[pl.ds(h * k, k), :]
    k = ref.shape[1] // 2
    return ref.at[:, pl.ds(h * k, k)]


def _shard_view(ref, kind, i):
    if kind == "stack":
        return ref.at[i]
    if kind == "col":
        k = ref.shape[1] // N_CHIPS
        return ref.at[:, pl.ds(i * k, k)]
    k = ref.shape[0] // N_CHIPS
    return ref.at[pl.ds(i * k, k), :]


def _region_view(ref, kind, h):
    if kind == "row":
        k = ref.shape[1] // 2
        return ref.at[:, pl.ds(h * k, k)]
    k = ref.shape[0] // 2
    return ref.at[pl.ds(h * k, k), :]


def _half_shape(shape, kind):
    if kind == "stack":
        return (shape[0], shape[1] // 2, shape[2])
    return (shape[0] // 2, shape[1]) if kind == "col" else (shape[0], shape[1] // 2)


def _part_shape(half_shape, kind):
    if kind == "stack":
        return tuple(half_shape[1:])
    k, w = half_shape
    return (k, w // N_CHIPS) if kind == "col" else (k // N_CHIPS, w)


_DATAFLOW = pltpu.SideEffectType.DATAFLOW_SIDE_EFFECTING
_TOKEN = (SUBLANES, LANES)


_COLLECTIVE = {k: i for i, k in enumerate(
    [kind + "_" + g for kind in ("pair", "chip", "share") for g in ("ffn", "proj", "in")]
    + ["gather_proj", "gather_ffn", "gather_in", "forward_proj", "forward_ffn"])}


def _sibling_peers(x, y, c):
    return [(x, y, 1 - c)]


def _chip_peers(x, y, c):
    return [(cx, cy, c) for cx, cy in _other_chips(x, y)]


def _handshake(peers):
    barrier = pltpu.get_barrier_semaphore()
    for peer in peers:
        pl.semaphore_signal(barrier, inc=1, device_id=peer, device_id_type=MESH)
    pl.semaphore_wait(barrier, len(peers))


def _split_start(name, arrays, n_sems, issue, after=None, handshake=None):
    n = len(arrays)
    order = [] if after is None else [after]

    def body(*refs):
        base = n + len(order)
        if handshake is not None:
            _handshake(handshake[1](*_mesh_pos()))
        issue(refs[:n], refs[base], refs[base + 1])
        refs[-1][...] = jnp.zeros(_TOKEN, F32)

    params = dict(has_side_effects=_DATAFLOW)
    if handshake is not None:
        params["collective_id"] = handshake[0]
    outs = pl.pallas_call(
        body, name=name,
        in_specs=[HBM] * n + [ANY] * len(order), out_specs=[SEM, SEM] + [HBM] * n + [pl.BlockSpec(memory_space=pltpu.VMEM)],
        out_shape=[pltpu.SemaphoreType.DMA((n_sems,)), pltpu.SemaphoreType.DMA((n_sems,))]
        + [pltpu.HBM(a.shape, a.dtype) for a in arrays] + [_sds(_TOKEN, F32)],
        input_output_aliases={k: 2 + k for k in range(n)},
        compiler_params=pltpu.CompilerParams(**params),
    )(*[pltpu.with_memory_space_constraint(a, pltpu.HBM) for a in arrays], *order)
    return outs[0], outs[1], list(outs[2:2 + n]), outs[-1]


def _split_wait(name, started, waits, after):
    send_sems, recv_sems, arrays, _ = started
    n = len(arrays)

    def body(*refs):
        waits(refs[:n], refs[n], refs[n + 1])

    return pl.pallas_call(
        body, name=name,
        in_specs=[HBM] * n + [SEM, SEM, ANY], out_specs=[HBM] * n,
        out_shape=[pltpu.HBM(a.shape, a.dtype) for a in arrays],
        input_output_aliases={k: k for k in range(n)},
        compiler_params=pltpu.CompilerParams(has_side_effects=_DATAFLOW),
    )(*arrays, send_sems, recv_sems, after)


def _wait_both(src, dst, send_sem, recv_sem):
    x, y, c = _mesh_pos()
    cp = _remote(src, dst, send_sem, recv_sem, (x, y, c))
    cp.wait_send()
    cp.wait_recv()


def _pair_exchange_start(parts, tag, after):
    names = list(parts)
    n = len(names)
    lands = [lax.empty(_half_shape(parts[k].shape, _KIND[k]), parts[k].dtype) for k in names]

    def issue(refs, send_sems, recv_sems):
        x, y, c = _mesh_pos()
        for hc in range(2):
            @pl.when(c == hc)
            def _():
                for k in range(n):
                    _remote(_half_view(refs[k], _KIND[names[k]], 1 - hc), refs[n + k], send_sems.at[k], recv_sems.at[k],
                            (x, y, 1 - c)).start()

    return names, _split_start("grad_pair_exchange_start_" + tag, [parts[k] for k in names] + lands, n, issue, after,
                               (_COLLECTIVE["pair_" + tag], _sibling_peers))


def _pair_exchange_wait(pending, tag, after):
    names, started = pending
    n = len(names)

    def waits(refs, send_sems, recv_sems):
        for k in range(n):
            _wait_both(_half_view(refs[k], _KIND[names[k]], 0), refs[n + k], send_sems.at[k], recv_sems.at[k])

    outs = _split_wait("grad_pair_exchange_wait_" + tag, started, waits, after)
    return dict(zip(names, outs[:n])), dict(zip(names, outs[n:]))


def _half_blocks(shape, kind):
    if kind == "stack":
        _, k, w = shape
        tr = k // 2
        nb = 1
        return (N_CHIPS, nb), (1, tr, w), (lambda i, r, s: (i, r, 0)), (lambda i, r, s: (i, s[1] * nb + r, 0))
    k, w = shape
    if kind == "col":
        tr = 256
        nb = k // 2 // tr
        return (nb,), (tr, w), (lambda r, s: (r, 0)), (lambda r, s: (s[1] * nb + r, 0))
    tr = k // N_CHIPS
    return (N_CHIPS,), (tr, w // 2), (lambda r, s: (r, 0)), (lambda r, s: (r, s[1]))


def _pair_add(part, from_sibling, name, pos):
    kind = _KIND[name]
    grid, block, half_map, full_map = _half_blocks(part.shape, kind)

    def body(s_ref, p_ref, q_ref, o_ref):
        o_ref[...] = (p_ref[...].astype(F32) + q_ref[...].astype(F32)).astype(BF16)

    return pl.pallas_call(
        body, name="grad_pair_add_" + name,
        grid_spec=pltpu.PrefetchScalarGridSpec(
            num_scalar_prefetch=1, grid=grid,
            in_specs=[pl.BlockSpec(block, full_map), pl.BlockSpec(block, half_map)],
            out_specs=pl.BlockSpec(block, half_map)),
        out_shape=_sds(from_sibling.shape, BF16),
        compiler_params=_cp(("arbitrary",) * len(grid), 40),
    )(pos, *_hbm(part, from_sibling))


def _chip_exchange_start(sums, tag, after):
    names = list(sums)
    n = len(names)
    lands = [lax.empty((3,) + _part_shape(sums[k].shape, _KIND[k]), sums[k].dtype) for k in names]

    def issue(refs, send_sems, recv_sems):
        x, y, c = _mesh_pos()
        me = 2 * x + y
        for i in range(N_CHIPS):
            xi, yi = i // 2, i % 2
            j = jnp.where(xi != x, jnp.where(yi != y, 2, 0), 1)

            @pl.when(i != me)
            def _():
                for k in range(n):
                    _remote(_shard_view(refs[k], _KIND[names[k]], i), refs[n + k].at[j], send_sems.at[3 * k + j],
                            recv_sems.at[3 * k + j], (xi, yi, c)).start()

    return names, _split_start("grad_chip_exchange_start_" + tag, [sums[k] for k in names] + lands, 3 * n, issue, after,
                               (_COLLECTIVE["chip_" + tag], _chip_peers))


def _chip_exchange_wait(pending, tag, after):
    names, started = pending
    n = len(names)

    def waits(refs, send_sems, recv_sems):
        for k in range(n):
            for j in range(3):
                _wait_both(_shard_view(refs[k], _KIND[names[k]], 0), refs[n + k].at[j], send_sems.at[3 * k + j], recv_sems.at[3 * k + j])

    return dict(zip(names, _split_wait("grad_chip_exchange_wait_" + tag, started, waits, after)[n:]))


def _allgather_start(stacks, tag, after):
    names = list(stacks)

    def issue(refs, send_sems, recv_sems):
        x, y, c = _mesh_pos()
        me = 2 * x + y
        for k, st in enumerate(refs):
            hr = st.shape[1] // 2
            mine = st.at[me, pl.ds(c * hr, hr), :]
            for j, (cx, cy) in enumerate(_other_chips(x, y)):
                _remote(mine, mine, send_sems.at[3 * k + j], recv_sems.at[3 * k + j], (cx, cy, c)).start()

    return names, _split_start("allgather_start_" + tag, [stacks[k] for k in names], 3 * len(names), issue, after,
                               (_COLLECTIVE["gather_" + tag], _chip_peers))


def _allgather_wait(pending, tag, after):
    names, started = pending

    def waits(refs, send_sems, recv_sems):
        for k, st in enumerate(refs):
            slot = st.at[0, pl.ds(0, st.shape[1] // 2), :]
            for j in range(3):
                _wait_both(slot, slot, send_sems.at[3 * k + j], recv_sems.at[3 * k + j])

    return dict(zip(names, _split_wait("allgather_wait_" + tag, started, waits, after)))


def _allgather_forward(stacks, tag):
    names = list(stacks)
    n = len(names)

    def body(*refs):
        ins, outs = refs[:n], refs[n:2 * n]
        send_sems, recv_sems = refs[2 * n:]
        x, y, c = _mesh_pos()
        _handshake(_sibling_peers(x, y, c))
        copies = []
        for k in range(n):
            hr = ins[k].shape[1] // 2
            for j, (cx, cy) in enumerate(_other_chips(x, y)):
                chip = 2 * cx + cy
                copies.append(_remote(ins[k].at[chip, pl.ds(c * hr, hr), :], outs[k].at[chip, pl.ds(c * hr, hr), :],
                                      send_sems.at[3 * k + j], recv_sems.at[3 * k + j], (x, y, 1 - c)))
        for cp in copies:
            cp.start()
        for cp in copies:
            cp.wait()

    arrays = [stacks[k] for k in names]
    outs = pl.pallas_call(
        body, name="allgather_forward_" + tag, in_specs=[HBM] * n, out_specs=[HBM] * n,
        input_output_aliases={k: k for k in range(n)},
        out_shape=[_sds(a.shape, a.dtype) for a in arrays],
        scratch_shapes=[pltpu.SemaphoreType.DMA((3 * n,)), pltpu.SemaphoreType.DMA((3 * n,))],
        compiler_params=pltpu.CompilerParams(collective_id=_COLLECTIVE["forward_" + tag]),
    )(*arrays)
    return dict(zip(names, outs))


def _owner_sum(part, from_sibling, from_chips, name, pos, shard_shape):
    kind = _KIND[name]
    _, pk, pw = from_chips.shape
    if kind == "row":
        tr, nb = pk, 1
        p_spec = pl.BlockSpec((tr, pw), lambda r, s: (s[0], s[1]))
        q_spec = pl.BlockSpec((tr, pw), lambda r, s: (s[0], 0))
        o_spec = pl.BlockSpec((tr, pw), lambda r, s: (0, s[1]))
    else:
        tr = 256
        nb = pk // tr
        if kind == "stack":
            p_spec = pl.BlockSpec((None, tr, pw), lambda r, s: (s[0], s[1] * nb + r, 0))
            q_spec = pl.BlockSpec((None, tr, pw), lambda r, s: (s[0], r, 0))
        else:
            p_spec = pl.BlockSpec((tr, pw), lambda r, s: (s[1] * nb + r, s[0]))
            q_spec = pl.BlockSpec((tr, pw), lambda r, s: (r, s[0]))
        o_spec = pl.BlockSpec((tr, pw), lambda r, s: (s[1] * nb + r, 0))

    def body(s_ref, p_ref, q_ref, r_ref, o_ref):
        acc = p_ref[...].astype(F32) + q_ref[...].astype(F32)
        for j in range(3):
            acc = acc + r_ref[j].astype(F32)
        o_ref[...] = acc

    return pl.pallas_call(
        body, name="grad_owner_sum_" + name,
        grid_spec=pltpu.PrefetchScalarGridSpec(
            num_scalar_prefetch=1, grid=(nb,),
            in_specs=[p_spec, q_spec, pl.BlockSpec((3, tr, pw), lambda r, s: (0, r, 0))],
            out_specs=o_spec),
        out_shape=_sds(shard_shape, F32),
        compiler_params=_cp(("arbitrary",), 32),
    )(pos, *_hbm(part, from_sibling, from_chips))


def _pair_share_start(shards, tag, after):
    names = list(shards)

    def issue(refs, send_sems, recv_sems):
        x, y, c = _mesh_pos()
        for hc in range(2):
            @pl.when(c == hc)
            def _():
                for k, g in enumerate(refs):
                    mine = _region_view(g, _KIND[names[k]], hc)
                    _remote(mine, mine, send_sems.at[k], recv_sems.at[k], (x, y, 1 - c)).start()

    return names, _split_start("grad_pair_share_start_" + tag, [shards[k] for k in names], len(names), issue, after,
                               (_COLLECTIVE["share_" + tag], _sibling_peers))


def _pair_share_wait(pending, tag, after):
    names, started = pending

    def waits(refs, send_sems, recv_sems):
        for k, g in enumerate(refs):
            region = _region_view(g, _KIND[names[k]], 0)
            _wait_both(region, region, send_sems.at[k], recv_sems.at[k])

    return dict(zip(names, _split_wait("grad_pair_share_wait_" + tag, started, waits, after)))


def _small_exchange_start(slots, after):
    def issue(refs, send_sems, recv_sems):
        x, y, c = _mesh_pos()
        mine = refs[0].at[4 * x + 2 * y + c]
        k = 0
        for px in range(2):
            for py in range(2):
                for pc in range(2):
                    if px + py + pc:
                        peer = (1 - x if px else x, 1 - y if py else y, 1 - c if pc else c)
                        _remote(mine, mine, send_sems.at[k], recv_sems.at[k], peer).start()
                        k += 1

    return _split_start("small_exchange_start", [slots], N_DEV - 1, issue, after)


def _small_exchange_wait(started, after):
    def waits(refs, send_sems, recv_sems):
        slot = refs[0].at[0]
        for k in range(N_DEV - 1):
            _wait_both(slot, slot, send_sems.at[k], recv_sems.at[k])

    return _split_wait("small_exchange_wait", started, waits, after)[0]


def _adam_math(w, g, m, v):
    m = ADAM_B1 * m + (1.0 - ADAM_B1) * g
    v = ADAM_B2 * v + (1.0 - ADAM_B2) * (g * g)
    m_hat = m / (1.0 - ADAM_B1 ** ADAM_STEP)
    v_hat = v / (1.0 - ADAM_B2 ** ADAM_STEP)
    delta = -ADAM_LR * (m_hat / (jnp.sqrt(v_hat) + ADAM_EPS) + ADAM_WD * w)
    return delta, m, v


def _adamw(w, g, m, v, name):
    rows, cols = w.shape
    fits = [t for t in range(SUBLANES, rows, SUBLANES) if rows % t == 0 and t * cols * 4 <= (3 << 19)]
    tr = max(fits) if fits else rows

    def body(w_ref, g_ref, m_ref, v_ref, d_ref, nm_ref, nv_ref, go_ref):
        g = g_ref[...]
        d, nm, nv = _adam_math(w_ref[...], g, m_ref[...], v_ref[...])
        d_ref[...] = d
        nm_ref[...] = nm
        nv_ref[...] = nv
        go_ref[...] = g

    spec = pl.BlockSpec((tr, cols), lambda i: (i, 0))
    return pl.pallas_call(
        body, name=name, grid=(rows // tr,), in_specs=[spec] * 4, out_specs=[spec] * 4,
        out_shape=[_sds(w.shape, F32)] * 4, compiler_params=_cp(("arbitrary",)),
    )(*_hbm(w, g, m, v))


def _small_sum_adamw(gathered, w, m, v):
    names = _SMALL_IN_CALL
    n = len(names)

    def body(*refs):
        a_ref = refs[0]
        w_refs, m_refs, v_refs = refs[1:1 + n], refs[1 + n:1 + 2 * n], refs[1 + 2 * n:1 + 3 * n]
        sum_ref = refs[1 + 3 * n]
        outs = refs[2 + 3 * n:]
        g = a_ref[0]
        for k in range(1, N_DEV):
            g = g + a_ref[k]
        sum_ref[...] = g
        for i, name in enumerate(names):
            r0, nr, nc = _SMALL_AT[name]
            gp = g[r0:r0 + nr, 0:nc]
            d, nm, nv = _adam_math(w_refs[i][...], gp, m_refs[i][...], v_refs[i][...])
            for k, val in enumerate((gp, d, nm, nv)):
                outs[4 * i + k][...] = val

    shapes = [w[k].shape for k in names]
    res = pl.pallas_call(
        body, name="small_sum_adamw",
        out_shape=[_sds((SMALL_ROWS, D_MODEL), F32)] + [_sds(s, F32) for s in shapes for _ in range(4)],
    )(gathered, *[w[k] for k in names], *[m[k] for k in names], *[v[k] for k in names])
    return res[0], {k: tuple(res[1 + 4 * i:5 + 4 * i]) for i, k in enumerate(names)}


_NAMES = ("g_mix", "w_in", "g_sgu", "w_s", "b_s", "sinks", "rel_bias", "w_pa", "w_pb", "w_out",
          "g_ffn", "w_up", "w_conv", "b_conv", "w_down", "g_final")

def kernel(x, g_mix, w_in, g_sgu, w_s, b_s, sinks, rel_bias, w_pa, w_pb, w_out, g_ffn, w_up, w_conv, b_conv, w_down, g_final, loss_target, m_g_mix, m_w_in, m_g_sgu, m_w_s, m_b_s, m_sinks, m_rel_bias, m_w_pa, m_w_pb, m_w_out, m_g_ffn, m_w_up, m_w_conv, m_b_conv, m_w_down, m_g_final, v_g_mix, v_w_in, v_g_sgu, v_w_s, v_b_s, v_sinks, v_rel_bias, v_w_pa, v_w_pb, v_w_out, v_g_ffn, v_w_up, v_w_conv, v_b_conv, v_w_down, v_g_final):
    w = dict(g_mix=g_mix, w_in=w_in, g_sgu=g_sgu, w_s=w_s, b_s=b_s, sinks=sinks, rel_bias=rel_bias, w_pa=w_pa, w_pb=w_pb,
             w_out=w_out, g_ffn=g_ffn, w_up=w_up, w_conv=w_conv, b_conv=b_conv, w_down=w_down, g_final=g_final)
    m = dict(g_mix=m_g_mix, w_in=m_w_in, g_sgu=m_g_sgu, w_s=m_w_s, b_s=m_b_s, sinks=m_sinks, rel_bias=m_rel_bias, w_pa=m_w_pa,
             w_pb=m_w_pb, w_out=m_w_out, g_ffn=m_g_ffn, w_up=m_w_up, w_conv=m_w_conv, b_conv=m_b_conv, w_down=m_w_down,
             g_final=m_g_final)
    v = dict(g_mix=v_g_mix, w_in=v_w_in, g_sgu=v_g_sgu, w_s=v_w_s, b_s=v_b_s, sinks=v_sinks, rel_bias=v_rel_bias, w_pa=v_w_pa,
             w_pb=v_w_pb, w_out=v_w_out, g_ffn=v_g_ffn, w_up=v_w_up, w_conv=v_w_conv, b_conv=v_b_conv, w_down=v_w_down,
             g_final=v_g_final)
    xi, yi, ci = _mesh_pos()
    me = 2 * xi + yi

    shard = {n: w[n][0] for n in _BIG}
    shard_shapes = {n: shard[n].shape for n in _BIG}
    wc_shard = w["w_conv"][0]
    wc_pad = jnp.pad(wc_shard, ((0, 5), (0, 0)))
    own = {n: _own_slot(shard[n].astype(BF16), N_CHIPS, me) for n in _BIG}
    stacks, wc_all = _allgather_weights({"w_in": own["w_in"]}, _own_slot(wc_pad, N_CHIPS, me))
    proj_gather = _allgather_start({n: own[n] for n in _MIXER[1:]}, "proj", stacks["w_in"])
    ffn_gather = _allgather_start({n: own[n] for n in _FFN}, "ffn", proj_gather[1][-1])
    w_conv_full = jnp.concatenate([wc_all[i, :3] for i in range(N_CHIPS)], axis=1)
    w_in_full = _unstack_w_in(stacks["w_in"])
    pos = jnp.stack([me, ci])

    def proj_weights(done):
        st = _allgather_forward(_allgather_wait(proj_gather, "proj", done), "proj")
        return st["w_pa"], st["w_pb"], st["w_out"].reshape(D_MODEL, D_MODEL)

    def ffn_weights(done):
        st = _allgather_forward(_allgather_wait(ffn_gather, "ffn", done), "ffn")
        return st["w_up"], st["w_down"].reshape(D_FF, D_MODEL)

    groups = {}

    def stage1(group, parts):
        groups[group] = dict(parts=parts, pair=_pair_exchange_start(parts, group, None))
        return groups[group]["pair"][1][-1]

    def stage2(group, after, order_after):
        g = groups[group]
        g["parts"], g["sib"] = _pair_exchange_wait(g["pair"], group, after)
        g["chip"] = _chip_exchange_start({n: _pair_add(g["parts"][n], g["sib"][n], n, pos) for n in g["parts"]}, group, order_after)
        return g["chip"][1][-1]

    def stage3(group, after, order_after):
        g = groups[group]
        got = _chip_exchange_wait(g["chip"], group, after)
        g["share"] = _pair_share_start(
            {n: _owner_sum(g["parts"][n], g["sib"][n], got[n], n, pos, shard_shapes[n]) for n in g["parts"]}, group, order_after)
        return g["share"][1][-1]

    grads, deltas, new_m, new_v = {}, {}, {}, {}

    def stage4(group, after):
        g_shard = _pair_share_wait(groups[group]["share"], group, after)
        last = None
        for n in g_shard:
            g = _tie(g_shard[n], last)
            if n == "w_in":
                d, nm, nv, gt = _adamw(shard[n].T, g.T, m[n][0].T, v[n][0].T, "adamw_" + n)
                grads[n], deltas[n], new_m[n], new_v[n] = gt.T[None], d.T[None], nm.T[None], nv.T[None]
            else:
                d, nm, nv, go = _adamw(shard[n], g, m[n][0], v[n][0], "adamw_" + n)
                grads[n], deltas[n], new_m[n], new_v[n] = go[None], d[None], nm[None], nv[None]
            last = nv
        return last

    def on_grads(group, parts):
        token = stage1(group, parts)
        some = next(iter(parts.values()))
        if group == "proj":
            token = stage2("ffn", some, token)
        if group == "in":
            token = stage2("proj", some, token)
            token = stage3("ffn", some, token)
            token = stage2("in", token, token)
        return token

    loss, grad_x, small, big = _local_step(
        x, loss_target, w["g_mix"], w["g_sgu"], w["w_s"][0], w["b_s"][0], w["sinks"], w["rel_bias"], w["g_ffn"],
        w["b_conv"], w["g_final"], w_in_full, w_conv_full, proj_weights, ffn_weights, on_grads, ffn_gather[1][-1])

    small["loss"] = loss
    small_gather = _small_exchange_start(_own_slot(_pack_small(small), N_DEV, 2 * me + ci), grad_x)
    token = stage3("proj", grad_x, small_gather[-1])
    done = stage4("ffn", token)
    done = stage4("proj", done)
    token = stage3("in", done, None)
    all_small = _small_exchange_wait(small_gather, token)
    two_d = {n: (lambda a, n=n: a.reshape(_SMALL_AT[n][1:])) for n in _SMALL_IN_CALL}
    s_sum, s_out = _small_sum_adamw(all_small, *[{n: two_d[n](p[n]) for n in _SMALL_IN_CALL} for p in (w, m, v)])
    stage4("in", all_small)
    for n in _SMALL_IN_CALL:
        grads[n], deltas[n], new_m[n], new_v[n] = [a.reshape(w[n].shape) for a in s_out[n]]

    def rows(n):
        r0, nr, _ = _SMALL_AT[n]
        return s_sum[r0:r0 + nr]

    wcols = wc_shard.shape[1]
    g_wc = lax.dynamic_slice(_unwide(rows("w_conv"), 3), (0, me * wcols), (3, wcols))
    d, nm, nv, _ = _adamw(wc_shard, g_wc, m["w_conv"][0], v["w_conv"][0], "adamw_w_conv")
    grads["w_conv"], deltas["w_conv"], new_m["w_conv"], new_v["w_conv"] = g_wc[None], d[None], nm[None], nv[None]
    d, nm, nv, go = _adamw(w["b_conv"], _unwide(rows("b_conv"), 1), m["b_conv"], v["b_conv"], "adamw_b_conv")
    grads["b_conv"], deltas["b_conv"], new_m["b_conv"], new_v["b_conv"] = go, d, nm, nv
    flat_s = (A_GROUPS * CHUNK, CHUNK)
    d, nm, nv, go = _adamw(w["w_s"].reshape(flat_s), rows("w_s").reshape(flat_s), m["w_s"].reshape(flat_s),
                           v["w_s"].reshape(flat_s), "adamw_w_s")
    grads["w_s"], deltas["w_s"], new_m["w_s"], new_v["w_s"] = [a.reshape(w["w_s"].shape) for a in (go, d, nm, nv)]

    return (s_sum[0, 0], grad_x, *[grads[n] for n in _NAMES], *[deltas[n] for n in _NAMES],
            *[new_m[n] for n in _NAMES], *[new_v[n] for n in _NAMES])
```

```python
import functools

import numpy as np
import jax
import jax.numpy as jnp
from jax import lax
from jax.experimental import pallas as pl
from jax.experimental.pallas import tpu as pltpu

F32 = jnp.float32
BF16 = jnp.bfloat16

D_MODEL = 1024
CHUNK = 128
A_GROUPS = 4
A_WIDTH = 512
N_HEADS = 8
HEAD_DIM = 64
Q_DIM = 512
KV_DIM = 128
N_BUCKETS = 32
MAX_DISTANCE = 128
D_FF = 2816
EPS = 1e-6
NEG_INF = -1e30
G_DIM = 2 * D_MODEL
A_DIM = 2 * A_WIDTH
B_DIM = Q_DIM + 2 * KV_DIM
LANES = 128
SUBLANES = 8
ROW_TILE = 512
WIDE_ROW_TILE = 256
COL_CHUNK = 512
GRAD_ROW_TILE = 512
BF16_ROWS = 16
N_CHIPS = 4
N_DEV = 8

ADAM_LR = 0.001
ADAM_B1 = 0.9
ADAM_B2 = 0.999
ADAM_EPS = 1e-08
ADAM_WD = 0.01
ADAM_STEP = 10

MESH = pl.DeviceIdType.MESH
_GELU_C = 0.7978845608028654
_GELU_A = 0.044715


def _cp(sem=None, vmem_mb=None):
    kw = {}
    if sem is not None:
        kw["dimension_semantics"] = sem
    if vmem_mb is not None:
        kw["vmem_limit_bytes"] = vmem_mb << 20
    return pltpu.CompilerParams(**kw)


def _dot(a, b):
    return jnp.dot(a, b, preferred_element_type=F32)


def _dot_nt(a, b):
    return lax.dot_general(a, b, (((1,), (1,)), ((), ())), preferred_element_type=F32)


def _dot_tn(a, b):
    return lax.dot_general(a, b, (((0,), (0,)), ((), ())), preferred_element_type=F32)


def _rms_r(x):
    return lax.rsqrt(jnp.mean(x * x, axis=-1, keepdims=True) + EPS)


def _rms_bwd(dh, n, r, g):
    dn = dh * g
    return r * (dn - n * jnp.mean(dn * n, axis=-1, keepdims=True))


def _gelu(x):
    t = jnp.tanh(_GELU_C * (x + _GELU_A * (x * x * x)))
    return 0.5 * x * (1.0 + t), t


def _gelu_grad(x, t):
    return 0.5 * (1.0 + t) + 0.5 * x * (1.0 - t * t) * (_GELU_C * (1.0 + 3.0 * _GELU_A * x * x))


def _sigmoid(x):
    return 1.0 / (1.0 + jnp.exp(-x))


def _tie(x, dep):
    return x if dep is None else lax.optimization_barrier((x, dep))[0]


def _row(tm, w):
    return pl.BlockSpec((tm, w), lambda i: (i, 0))


def _full(shape):
    nd = len(shape)
    return pl.BlockSpec(tuple(shape), lambda *_: (0,) * nd)


def _resident(shape):
    nd = len(shape)
    return pl.BlockSpec(tuple(shape), lambda *_: (0,) * nd, pipeline_mode=pl.Buffered(1))


def _sds(shape, dtype):
    return jax.ShapeDtypeStruct(tuple(shape), dtype)


def _hbm(*arrays):
    return [pltpu.with_memory_space_constraint(a, pltpu.HBM) for a in arrays]


HBM = pl.BlockSpec(memory_space=pltpu.HBM)
ANY = pl.BlockSpec(memory_space=pl.ANY)
SEM = pl.BlockSpec(memory_space=pltpu.SEMAPHORE)


def _band_buckets():
    i = np.arange(CHUNK)[:, None]
    j = np.arange(2 * CHUNK)[None, :]
    dist = i + CHUNK - j
    valid = (dist >= 0) & (dist < CHUNK)
    d = np.clip(dist, 0, None)
    max_exact = N_BUCKETS // 2
    large = max_exact + (np.log(np.maximum(d, 1) / max_exact) / np.log(MAX_DISTANCE / max_exact)
                         * (N_BUCKETS - max_exact)).astype(np.int32)
    large = np.minimum(large, N_BUCKETS - 1)
    buckets = np.where(d < max_exact, d, large).astype(np.int32)
    return np.where(valid, buckets, -1).astype(np.int32)


_A_COLS = slice(0, A_DIM)
_B_COLS = slice(A_DIM, A_DIM + B_DIM)
_G_COLS = slice(A_DIM + B_DIM, A_DIM + B_DIM + G_DIM)


def _inproj(x2, g_mix, w_in, tm, after=None):
    T = x2.shape[0]
    order = [] if after is None else [after]

    def body(*refs):
        x_ref, g_ref, w_ref = refs[:3]
        pg_ref, pa_ref, pb_ref, h_ref = refs[3 + len(order):]
        x = x_ref[...]
        h = (x * _rms_r(x) * g_ref[...]).astype(BF16)
        h_ref[...] = h
        pg_ref[...] = _dot(h, w_ref[:, _G_COLS]).astype(BF16)
        pa_ref[...] = _dot(h, w_ref[:, _A_COLS]).astype(BF16)
        pb_ref[...] = _dot(h, w_ref[:, _B_COLS]).astype(BF16)

    return pl.pallas_call(
        body, name="inproj", grid=(T // tm,),
        in_specs=[_row(tm, D_MODEL), _full(g_mix.shape), _resident(w_in.shape)] + [ANY] * len(order),
        out_specs=[_row(tm, G_DIM), _row(tm, A_DIM), _row(tm, B_DIM), _row(tm, D_MODEL)],
        out_shape=[_sds((T, G_DIM), BF16), _sds((T, A_DIM), BF16), _sds((T, B_DIM), BF16), _sds((T, D_MODEL), BF16)],
        compiler_params=_cp(("arbitrary",), 48),
    )(*_hbm(x2, g_mix, w_in), *order)


def _sgu_parts(p, g):
    pu = p[:, :A_WIDTH]
    pv = p[:, A_WIDTH:]
    u, tu = _gelu(pu)
    vv, tv = _gelu(pv)
    rv = _rms_r(vv)
    vn = (vv * rv * g).astype(BF16)
    return pu, pv, u, tu, vv, tv, rv, vn


def _tril():
    r = lax.broadcasted_iota(jnp.int32, (CHUNK, CHUNK), 0)
    c = lax.broadcasted_iota(jnp.int32, (CHUNK, CHUNK), 1)
    return r >= c


def _sgu_fwd(proj_a, g_sgu, w_s, b_st, tm):
    T = proj_a.shape[0]

    def body(p_ref, g_ref, ws_ref, bs_ref, y_ref):
        tril = _tril()
        _, _, u, _, _, _, _, vn = _sgu_parts(p_ref[...].astype(F32), g_ref[...])
        for gi in range(A_GROUPS):
            wm = jnp.where(tril, ws_ref[gi], 0.0).astype(BF16)
            bcol = bs_ref[:, gi:gi + 1]
            cs = slice(gi * CHUNK, (gi + 1) * CHUNK)
            for c in range(tm // CHUNK):
                rs = slice(c * CHUNK, (c + 1) * CHUNK)
                s = _dot(wm, vn[rs, cs]) + bcol
                y_ref[rs, cs] = (u[rs, cs] * s).astype(BF16)

    return pl.pallas_call(
        body, name="sgu_fwd", grid=(T // tm,),
        in_specs=[_row(tm, A_DIM), _full(g_sgu.shape), _full(w_s.shape), _full(b_st.shape)],
        out_specs=_row(tm, A_WIDTH), out_shape=_sds((T, A_WIDTH), BF16),
        compiler_params=_cp(("arbitrary",)),
    )(*_hbm(proj_a, g_sgu, w_s, b_st))


HEAD_ROWS = N_HEADS * CHUNK


def _head_rows(h):
    return slice(h * CHUNK, (h + 1) * CHUNK)


def _attn_setup(bias_scr, sink_scr, kvar_scr, qkv_ref, bk_ref, rel_ref, sink_ref):
    @pl.when(pl.program_id(0) == 0)
    def _():
        bk = bk_ref[...]
        for h in range(N_HEADS):
            acc = jnp.full((CHUNK, 2 * CHUNK), NEG_INF, F32)
            for b in range(N_BUCKETS):
                acc = jnp.where(bk == b, rel_ref[b, h], acc)
            bias_scr[_head_rows(h), :] = acc
            sink_scr[_head_rows(h), :] = jnp.full((CHUNK, LANES), sink_ref[0, h], F32)

    seq = qkv_ref.shape[0]
    rows_per = 2 * CHUNK
    for is_v in range(2):
        c0 = Q_DIM + is_v * KV_DIM
        for r in range(seq // rows_per):
            rs = slice(r * rows_per, (r + 1) * rows_per)
            a = qkv_ref[rs, c0:c0 + KV_DIM].astype(F32)
            lane = lax.broadcasted_iota(jnp.int32, a.shape, 1)
            lo = jnp.where(lane < HEAD_DIM, a, 0.0)
            hi = jnp.where(lane >= HEAD_DIM, a, 0.0)
            kvar_scr[4 * is_v + 0, rs, :] = lo.astype(BF16)
            kvar_scr[4 * is_v + 1, rs, :] = pltpu.roll(lo, HEAD_DIM, 1).astype(BF16)
            kvar_scr[4 * is_v + 2, rs, :] = pltpu.roll(hi, HEAD_DIM, 1).astype(BF16)
            kvar_scr[4 * is_v + 3, rs, :] = hi.astype(BF16)


def _rowsum(a, ones):
    hi = a.astype(BF16)
    lo = (a - hi.astype(F32)).astype(BF16)
    return _dot(hi, ones) + _dot(lo, ones)


def _both(a):
    return jnp.concatenate([a, a], axis=1)


def _attn_probs(qkv_ref, r0, n, kv, bias_scr, sink_scr, ones):
    s = jnp.concatenate([_dot_nt(qkv_ref[pl.ds(r0, CHUNK), (h // 2) * LANES:(h // 2 + 1) * LANES], kv[h // 4][h % 2])
                         for h in range(N_HEADS)], axis=0)
    s = s * (HEAD_DIM ** -0.5) + bias_scr[...]
    col = lax.broadcasted_iota(jnp.int32, s.shape, 1)
    s = jnp.where((col < CHUNK) & (n == 0), NEG_INF, s)
    sink = sink_scr[...]
    m = jnp.maximum(jnp.max(s, axis=-1, keepdims=True), sink)
    p = jnp.exp(s - _both(m))
    es = jnp.exp(sink - m)
    inv = 1.0 / (_dot(p.astype(BF16), ones) + es)
    return p * _both(inv), es * inv


def _attn_block_inputs(kvar_scr, n):
    r0 = pl.multiple_of(n * CHUNK, CHUNK)
    rp = pl.multiple_of(jnp.maximum(n - 1, 0) * CHUNK, CHUNK)

    def both(idx):
        return jnp.concatenate([kvar_scr[idx, pl.ds(rp, CHUNK), :], kvar_scr[idx, pl.ds(r0, CHUNK), :]], axis=0)

    kv = ((both(0), both(1)), (both(2), both(3)))
    vv = ((both(4), both(5)), (both(6), both(7)))
    return r0, kv, vv


def _attn_fwd(proj_b, sinks, rel_bias, n_seq, seq):
    nb = seq // CHUNK
    bk = jnp.asarray(_band_buckets())

    def body(qkv_ref, bk_ref, rel_ref, sink_ref, o_ref, bias_scr, sink_scr, kvar_scr):
        _attn_setup(bias_scr, sink_scr, kvar_scr, qkv_ref, bk_ref, rel_ref, sink_ref)
        ones = jnp.ones((2 * CHUNK, LANES), BF16)

        def blk(n, carry):
            r0, kv, vv = _attn_block_inputs(kvar_scr, n)
            prob, _ = _attn_probs(qkv_ref, r0, n, kv, bias_scr, sink_scr, ones)
            pb = prob.astype(BF16)
            for pr in range(N_HEADS // 2):
                acc = _dot(pb[_head_rows(2 * pr)], vv[pr // 2][0]) + _dot(pb[_head_rows(2 * pr + 1)], vv[pr // 2][1])
                o_ref[pl.ds(r0, CHUNK), pr * LANES:(pr + 1) * LANES] = acc.astype(BF16)
            return carry

        lax.fori_loop(0, nb, blk, 0)

    smem = pl.BlockSpec(memory_space=pltpu.SMEM)
    return pl.pallas_call(
        body, name="attn_fwd", grid=(n_seq,),
        in_specs=[_row(seq, B_DIM), _full(bk.shape), smem, smem],
        out_specs=_row(seq, Q_DIM), out_shape=_sds((n_seq * seq, Q_DIM), BF16),
        scratch_shapes=[pltpu.VMEM((HEAD_ROWS, 2 * CHUNK), F32), pltpu.VMEM((HEAD_ROWS, LANES), F32),
                        pltpu.VMEM((8, seq, KV_DIM), BF16)],
        compiler_params=_cp(("arbitrary",), 40),
    )(*_hbm(proj_b, bk), rel_bias, sinks)


def _dot_stacked(a, w_ref):
    return jnp.concatenate([_dot(a, w_ref[i]) for i in range(N_CHIPS)], axis=1)


def _dot_nt_stacked(a, w_ref):
    w = w_ref.shape[2]
    acc = _dot_nt(a[:, :w], w_ref[0])
    for i in range(1, N_CHIPS):
        acc = acc + _dot_nt(a[:, i * w:(i + 1) * w], w_ref[i])
    return acc


def _merge_fwd(x2, y_a, y_b, proj_g, w_pa, w_pb, w_out, tm):
    T = x2.shape[0]

    def body(x_ref, ya_ref, yb_ref, g_ref, wpa_ref, wpb_ref, wo_ref, x1_ref, mg_ref):
        g = g_ref[...].astype(F32)
        pa = _dot_stacked(ya_ref[...], wpa_ref)
        pb = _dot_stacked(yb_ref[...], wpb_ref)
        merged = (_sigmoid(g[:, :D_MODEL]) * pa + _sigmoid(g[:, D_MODEL:]) * pb).astype(BF16)
        mg_ref[...] = merged
        x1_ref[...] = x_ref[...] + _dot(merged, wo_ref[...])

    return pl.pallas_call(
        body, name="merge_fwd", grid=(T // tm,),
        in_specs=[_row(tm, D_MODEL), _row(tm, A_WIDTH), _row(tm, Q_DIM), _row(tm, G_DIM),
                  _resident(w_pa.shape), _resident(w_pb.shape), _resident(w_out.shape)],
        out_specs=[_row(tm, D_MODEL), _row(tm, D_MODEL)],
        out_shape=[_sds((T, D_MODEL), F32), _sds((T, D_MODEL), BF16)],
        compiler_params=_cp(("arbitrary",), 40),
    )(*_hbm(x2, y_a, y_b, proj_g, w_pa, w_pb, w_out))


def _upproj(x1, g_ffn, w_up, w_conv, b_conv, tm, seq):
    T = x1.shape[0]
    cw = w_up.shape[2]
    tiles_per_seq = seq // tm

    def body(x_ref, g_ref, w_ref, wc_ref, bc_ref, u_ref, h_ref, gate_ref, val_ref, tail_scr):
        at_start = (pl.program_id(0) % tiles_per_seq) == 0
        x = x_ref[...]
        h = (x * _rms_r(x) * g_ref[...]).astype(BF16)
        h_ref[...] = h
        for i in range(N_CHIPS):
            cs = slice(i * cw, (i + 1) * cw)
            u = _dot(h, w_ref[i])
            u_ref[:, cs] = u.astype(BF16)
            hl = jnp.where(at_start, 0.0, tail_scr[SUBLANES - 2:SUBLANES, cs])
            tail_scr[:, cs] = u[tm - SUBLANES:]
            up = _conv_out((u, _shift_down(u, hl, 1), _shift_down(u, hl, 2)), wc_ref[:, cs], bc_ref[:, cs])
            out_ref = gate_ref if i < N_CHIPS // 2 else val_ref
            out_ref[:, (i % 2) * cw:(i % 2 + 1) * cw] = up.astype(BF16)

    return pl.pallas_call(
        body, name="upproj", grid=(T // tm,),
        in_specs=[_row(tm, D_MODEL), _full(g_ffn.shape), _resident(w_up.shape), _full(w_conv.shape), _full(b_conv.shape)],
        out_specs=[_row(tm, 2 * D_FF), _row(tm, D_MODEL), _row(tm, D_FF), _row(tm, D_FF)],
        out_shape=[_sds((T, 2 * D_FF), BF16), _sds((T, D_MODEL), BF16), _sds((T, D_FF), BF16), _sds((T, D_FF), BF16)],
        scratch_shapes=[pltpu.VMEM((SUBLANES, 2 * D_FF), F32)],
        compiler_params=_cp(("arbitrary",), 56),
    )(*_hbm(x1, g_ffn, w_up, w_conv, b_conv))


def _shift_down(u, halo, k):
    rolled = pltpu.roll(u, k, 0)
    head = rolled[:SUBLANES]
    row = lax.broadcasted_iota(jnp.int32, head.shape, 0)
    if k == 1:
        head = jnp.where(row == 0, halo[1:2], head)
    else:
        head = jnp.where(row == 0, halo[0:1], jnp.where(row == 1, halo[1:2], head))
    return jnp.concatenate([head, rolled[SUBLANES:]], axis=0)


def _shift_up(d, halo, k):
    tm = d.shape[0]
    rolled = pltpu.roll(d, tm - k, 0)
    tail = rolled[tm - SUBLANES:]
    row = lax.broadcasted_iota(jnp.int32, tail.shape, 0)
    if k == 1:
        tail = jnp.where(row == SUBLANES - 1, halo[0:1], tail)
    else:
        tail = jnp.where(row == SUBLANES - 2, halo[0:1], jnp.where(row == SUBLANES - 1, halo[1:2], tail))
    return jnp.concatenate([rolled[:tm - SUBLANES], tail], axis=0)


def _conv_out(taps, wc, bc):
    u, u1, u2 = taps
    return wc[0:1] * u2 + wc[1:2] * u1 + wc[2:3] * u + bc


def _ffn_down_loss(gate, val, x1, target, w_down, g_final, tm):
    T = x1.shape[0]
    half = D_FF // 2

    def body(gt_ref, vl_ref, x1_ref, t_ref, wd_ref, g_ref, dx2_ref, loss_ref, gg_ref):
        i = pl.program_id(0)
        acc = jnp.zeros((tm, D_MODEL), F32)
        for j in range(2):
            gc = slice(j * half, (j + 1) * half)
            gate = gt_ref[:, gc].astype(F32)
            act = (gate * _sigmoid(gate) * vl_ref[:, gc].astype(F32)).astype(BF16)
            acc = acc + _dot(act, wd_ref[gc, :])
        x2 = x1_ref[...] + acc
        r = _rms_r(x2)
        n = x2 * r
        g = g_ref[...]
        diff = n * g - t_ref[...]
        dy = diff * (1.0 / D_MODEL)
        dx2_ref[...] = _rms_bwd(dy, n, r, g)

        @pl.when(i == 0)
        def _():
            loss_ref[...] = jnp.zeros_like(loss_ref)
            gg_ref[...] = jnp.zeros_like(gg_ref)

        loss_ref[...] += 0.5 * jnp.sum(jnp.mean(diff * diff, axis=-1, keepdims=True), axis=0, keepdims=True)
        gg_ref[...] += jnp.sum(dy * n, axis=0, keepdims=True)

    return pl.pallas_call(
        body, name="ffn_down_loss", grid=(T // tm,),
        in_specs=[_row(tm, D_FF), _row(tm, D_FF), _row(tm, D_MODEL), _row(tm, D_MODEL),
                  _resident(w_down.shape), _full(g_final.shape)],
        out_specs=[_row(tm, D_MODEL), _full((1, 1)), _full((1, D_MODEL))],
        out_shape=[_sds((T, D_MODEL), F32), _sds((1, 1), F32), _sds((1, D_MODEL), F32)],
        compiler_params=_cp(("arbitrary",), 48),
    )(*_hbm(gate, val, x1, target, w_down, g_final))


def _ffn_bwd_act(gate, val, dx2, w_down, tm):
    T = dx2.shape[0]
    half = D_FF // 2
    nt = T // tm

    def body(g_ref, v_ref, dx_ref, wd_ref, dg_ref, dv_ref, gwd_out, gbg_ref, gbv_ref, gwd_ref):
        i = pl.program_id(1)

        @pl.when(i == 0)
        def _():
            for r in (gwd_ref, gbg_ref, gbv_ref):
                r[...] = jnp.zeros_like(r)

        dx = dx_ref[...].astype(BF16)
        for c0 in range(0, half, COL_CHUNK):
            cs = slice(c0, min(c0 + COL_CHUNK, half))
            gate = g_ref[:, cs].astype(F32)
            val = v_ref[:, cs].astype(F32)
            sg = _sigmoid(gate)
            silu = gate * sg
            d_act = _dot_nt(dx, wd_ref[cs, :])
            d_val = d_act * silu
            d_gate = d_act * val * (sg * (1.0 + gate * (1.0 - sg)))
            dg_ref[:, cs] = d_gate.astype(BF16)
            dv_ref[:, cs] = d_val.astype(BF16)
            gwd_ref[cs, :] += _dot_tn((silu * val).astype(BF16), dx)
            gbg_ref[:, cs] += jnp.sum(d_gate, axis=0, keepdims=True)
            gbv_ref[:, cs] += jnp.sum(d_val, axis=0, keepdims=True)

        @pl.when(i == nt - 1)
        def _():
            gwd_out[...] = gwd_ref[...].astype(BF16)

    tile = pl.BlockSpec((tm, half), lambda j, i: (i, j))
    vec = pl.BlockSpec((1, half), lambda j, i: (0, j))
    wrows = pl.BlockSpec((half, D_MODEL), lambda j, i: (j, 0))
    return pl.pallas_call(
        body, name="ffn_bwd_act", grid=(2, nt),
        in_specs=[tile, tile, pl.BlockSpec((tm, D_MODEL), lambda j, i: (i, 0)), wrows],
        out_specs=[tile, tile, wrows, vec, vec],
        out_shape=[_sds((T, D_FF), BF16), _sds((T, D_FF), BF16), _sds((D_FF, D_MODEL), BF16),
                   _sds((1, D_FF), F32), _sds((1, D_FF), F32)],
        scratch_shapes=[pltpu.VMEM((half, D_MODEL), F32)],
        compiler_params=_cp(("arbitrary", "arbitrary"), 56),
    )(*_hbm(gate, val, dx2, w_down))


def _ffn_bwd_up(d_gate, d_val, upre, dx2, x1, g_ffn, w_conv, w_up, tm, seq):
    T = dx2.shape[0]
    tiles_per_seq = seq // tm
    k16 = tm // BF16_ROWS
    n16 = T // BF16_ROWS
    cw = D_FF // 2

    def body(dg_ref, dv_ref, hg_ref, hv_ref, u_ref, dx2_ref, x1_ref, g_ref, wc_ref, wu_ref, du_ref, dx1_ref, gg_ref, gwc_ref):
        i = pl.program_id(0)
        at_end = (i % tiles_per_seq) == tiles_per_seq - 1

        @pl.when(i == 0)
        def _():
            gg_ref[...] = jnp.zeros_like(gg_ref)
            gwc_ref[...] = jnp.zeros_like(gwc_ref)

        dh = jnp.zeros((tm, D_MODEL), F32)
        for j in range(4):
            src, hsrc = (dg_ref, hg_ref) if j < 2 else (dv_ref, hv_ref)
            ls = slice((j % 2) * cw, (j % 2 + 1) * cw)
            cs = slice(j * cw, (j + 1) * cw)
            d = src[:, ls].astype(F32)
            hl = hsrc[:, ls].astype(F32)[0:2]
            hl = jnp.where(at_end, 0.0, hl)
            wc = wc_ref[:, cs]
            d1 = _shift_up(d, hl, 1)
            d2 = _shift_up(d, hl, 2)
            du = (wc[2:3] * d + wc[1:2] * d1 + wc[0:1] * d2).astype(BF16)
            du_ref[:, cs] = du
            dh = dh + _dot_nt(du, wu_ref[j])
            u = u_ref[:, cs].astype(F32)
            gwc_ref[0:1, cs] += jnp.sum(d2 * u, axis=0, keepdims=True)
            gwc_ref[1:2, cs] += jnp.sum(d1 * u, axis=0, keepdims=True)
            gwc_ref[2:3, cs] += jnp.sum(d * u, axis=0, keepdims=True)
        x = x1_ref[...]
        r = _rms_r(x)
        n = x * r
        dx1_ref[...] = dx2_ref[...] + _rms_bwd(dh, n, r, g_ref[...])
        gg_ref[...] += jnp.sum(dh * n, axis=0, keepdims=True)

    nxt = pl.BlockSpec((BF16_ROWS, D_FF), lambda i: (jnp.minimum((i + 1) * k16, n16 - 1), 0))
    return pl.pallas_call(
        body, name="ffn_bwd_up", grid=(T // tm,),
        in_specs=[_row(tm, D_FF), _row(tm, D_FF), nxt, nxt, _row(tm, 2 * D_FF), _row(tm, D_MODEL), _row(tm, D_MODEL),
                  _full(g_ffn.shape), _full(w_conv.shape), _resident(w_up.shape)],
        out_specs=[_row(tm, 2 * D_FF), _row(tm, D_MODEL), _full((1, D_MODEL)), _full((3, 2 * D_FF))],
        out_shape=[_sds((T, 2 * D_FF), BF16), _sds((T, D_MODEL), F32), _sds((1, D_MODEL), F32), _sds((3, 2 * D_FF), F32)],
        compiler_params=_cp(("arbitrary",), 56),
    )(*_hbm(d_gate, d_val, d_gate, d_val, upre, dx2, x1, g_ffn, w_conv, w_up))


def _matmul_tn(a, b, tn, tk, name):
    T, M = a.shape
    N = b.shape[1]
    nk = T // tk

    def body(a_ref, b_ref, o_ref, acc_ref):
        k = pl.program_id(1)

        @pl.when(k == 0)
        def _():
            acc_ref[...] = jnp.zeros_like(acc_ref)

        acc_ref[...] += _dot_tn(a_ref[...], b_ref[...])

        @pl.when(k == nk - 1)
        def _():
            o_ref[...] = acc_ref[...].astype(BF16)

    return pl.pallas_call(
        body, name=name, grid=(N // tn, nk),
        in_specs=[pl.BlockSpec((tk, M), lambda j, k: (k, 0)), pl.BlockSpec((tk, tn), lambda j, k: (k, j))],
        out_specs=pl.BlockSpec((M, tn), lambda j, k: (0, j)), out_shape=_sds((M, N), BF16),
        scratch_shapes=[pltpu.VMEM((M, tn), F32)],
        compiler_params=_cp(("arbitrary", "arbitrary"), 48),
    )(*_hbm(a, b))


def _merge_bwd(dx1, merged, y_a, y_b, proj_g, w_pa, w_pb, w_out, tm, after=None):
    T = dx1.shape[0]

    nt = T // tm
    pshape = (A_WIDTH, D_MODEL)
    order = [] if after is None else [after]

    def body(*refs):
        dx_ref, mg_ref, ya_ref, yb_ref, g_ref, wpa_ref, wpb_ref, wo_ref = refs[:8]
        dg_ref, dya_ref, dyb_ref, gwo_out, gwpa_out, gwpb_out, gwo_ref, gwpa_ref, gwpb_ref = refs[8 + len(order):]
        i = pl.program_id(0)
        dx = dx_ref[...].astype(BF16)
        dm = _dot_nt(dx, wo_ref[...])
        g = g_ref[...].astype(F32)
        ya = ya_ref[...]
        yb = yb_ref[...]
        pa = _dot_stacked(ya, wpa_ref)
        pb = _dot_stacked(yb, wpb_ref)
        sa = _sigmoid(g[:, :D_MODEL])
        sb = _sigmoid(g[:, D_MODEL:])
        dpa = (dm * sa).astype(BF16)
        dpb = (dm * sb).astype(BF16)
        dg_ref[:, :D_MODEL] = (dm * pa * (sa * (1.0 - sa))).astype(BF16)
        dg_ref[:, D_MODEL:] = (dm * pb * (sb * (1.0 - sb))).astype(BF16)
        dya_ref[...] = _dot_nt_stacked(dpa, wpa_ref).astype(BF16)
        dyb_ref[...] = _dot_nt_stacked(dpb, wpb_ref).astype(BF16)

        @pl.when(i == 0)
        def _():
            for r in (gwo_ref, gwpa_ref, gwpb_ref):
                r[...] = jnp.zeros_like(r)

        gwo_ref[...] += _dot_tn(mg_ref[...], dx)
        gwpa_ref[...] += _dot_tn(ya, dpa)
        gwpb_ref[...] += _dot_tn(yb, dpb)

        @pl.when(i == nt - 1)
        def _():
            gwo_out[...] = gwo_ref[...].astype(BF16)
            gwpa_out[...] = gwpa_ref[...].astype(BF16)
            gwpb_out[...] = gwpb_ref[...].astype(BF16)

    return pl.pallas_call(
        body, name="merge_bwd", grid=(nt,),
        in_specs=[_row(tm, D_MODEL), _row(tm, D_MODEL), _row(tm, A_WIDTH), _row(tm, Q_DIM), _row(tm, G_DIM),
                  _resident(w_pa.shape), _resident(w_pb.shape), _resident(w_out.shape)] + [ANY] * len(order),
        out_specs=[_row(tm, G_DIM), _row(tm, A_WIDTH), _row(tm, Q_DIM),
                   _full(w_out.shape), _full(pshape), _full(pshape)],
        out_shape=[_sds((T, G_DIM), BF16), _sds((T, A_WIDTH), BF16), _sds((T, Q_DIM), BF16),
                   _sds(w_out.shape, BF16), _sds(pshape, BF16), _sds(pshape, BF16)],
        scratch_shapes=[pltpu.VMEM(w_out.shape, F32), pltpu.VMEM(pshape, F32), pltpu.VMEM(pshape, F32)],
        compiler_params=_cp(("arbitrary",), 56),
    )(*_hbm(dx1, merged, y_a, y_b, proj_g, w_pa, w_pb, w_out), *order)


def _sgu_bwd(proj_a, d_ya, g_sgu, w_s, b_st, tm, after=None):
    T = proj_a.shape[0]
    order = [] if after is None else [after]

    def body(*refs):
        p_ref, dy_ref, g_ref, ws_ref, bs_ref = refs[:5]
        dp_ref, gws_ref, gbs_ref, gg_ref = refs[5 + len(order):]
        tril = _tril()
        g = g_ref[...]
        pu, pv, u, tu, vv, tv, rv, vn = _sgu_parts(p_ref[...].astype(F32), g)
        dy = dy_ref[...].astype(F32)

        @pl.when(pl.program_id(0) == 0)
        def _():
            for r in (gws_ref, gbs_ref, gg_ref):
                r[...] = jnp.zeros_like(r)

        du_cols = []
        dvn_cols = []
        for gi in range(A_GROUPS):
            wm = jnp.where(tril, ws_ref[gi], 0.0).astype(BF16)
            wmt = wm.astype(F32).T.astype(BF16)
            bcol = bs_ref[:, gi:gi + 1]
            cs = slice(gi * CHUNK, (gi + 1) * CHUNK)
            du_rows = []
            dvn_rows = []
            gw = jnp.zeros((CHUNK, CHUNK), F32)
            gb = jnp.zeros((CHUNK, 1), F32)
            for c in range(tm // CHUNK):
                rs = slice(c * CHUNK, (c + 1) * CHUNK)
                vn_c = vn[rs, cs]
                s = _dot(wm, vn_c) + bcol
                dy_c = dy[rs, cs]
                ds = dy_c * u[rs, cs]
                du_rows.append(dy_c * s)
                dsb = ds.astype(BF16)
                gw = gw + _dot_nt(dsb, vn_c)
                gb = gb + jnp.sum(ds, axis=-1, keepdims=True)
                dvn_rows.append(_dot(wmt, dsb))
            gws_ref[gi] += jnp.where(tril, gw, 0.0)
            gbs_ref[:, gi:gi + 1] += gb
            du_cols.append(jnp.concatenate(du_rows, axis=0))
            dvn_cols.append(jnp.concatenate(dvn_rows, axis=0))
        du = jnp.concatenate(du_cols, axis=1)
        dvn = jnp.concatenate(dvn_cols, axis=1)
        vhat = vv * rv
        gg_ref[...] += jnp.sum(dvn * vhat, axis=0, keepdims=True)
        dvv = _rms_bwd(dvn, vhat, rv, g)
        dp_ref[:, :A_WIDTH] = (du * _gelu_grad(pu, tu)).astype(BF16)
        dp_ref[:, A_WIDTH:] = (dvv * _gelu_grad(pv, tv)).astype(BF16)

    return pl.pallas_call(
        body, name="sgu_bwd", grid=(T // tm,),
        in_specs=[_row(tm, A_DIM), _row(tm, A_WIDTH), _full(g_sgu.shape), _full(w_s.shape), _full(b_st.shape)] + [ANY] * len(order),
        out_specs=[_row(tm, A_DIM), _full(w_s.shape), _full(b_st.shape), _full(g_sgu.shape)],
        out_shape=[_sds((T, A_DIM), BF16), _sds(w_s.shape, F32), _sds(b_st.shape, F32), _sds(g_sgu.shape, F32)],
        compiler_params=_cp(("arbitrary",)),
    )(*_hbm(proj_a, d_ya, g_sgu, w_s, b_st), *order)


def _attn_bwd(proj_b, d_yb, sinks, rel_bias, n_seq, seq):
    nb = seq // CHUNK
    bk = jnp.asarray(_band_buckets())

    def body(qkv_ref, do_ref, bk_ref, rel_ref, sink_ref, d_ref, gs_ref, gr_ref,
             bias_scr, sink_scr, kvar_scr, dbias_scr, dk_scr, dv_scr, ds_scr):
        b = pl.program_id(0)
        _attn_setup(bias_scr, sink_scr, kvar_scr, qkv_ref, bk_ref, rel_ref, sink_ref)
        ones = jnp.ones((2 * CHUNK, LANES), BF16)

        @pl.when(b == 0)
        def _():
            dbias_scr[...] = jnp.zeros_like(dbias_scr)
            ds_scr[...] = jnp.zeros_like(ds_scr)

        dk_scr[...] = jnp.zeros_like(dk_scr)
        dv_scr[...] = jnp.zeros_like(dv_scr)

        def transposed(a):
            return a.astype(F32).T.astype(BF16)

        def blk(n, carry):
            r0, kv, vv = _attn_block_inputs(kvar_scr, n)
            prob, psink = _attn_probs(qkv_ref, r0, n, kv, bias_scr, sink_scr, ones)
            dp = jnp.concatenate([_dot_nt(do_ref[pl.ds(r0, CHUNK), (h // 2) * LANES:(h // 2 + 1) * LANES], vv[h // 4][h % 2])
                                  for h in range(N_HEADS)], axis=0)
            delta = _rowsum(prob * dp, ones)
            dsc = prob * (dp - _both(delta))
            ds_scr[...] += psink * delta
            dbias_scr[...] += dsc
            dsb = (dsc * (HEAD_DIM ** -0.5)).astype(BF16)
            pb = prob.astype(BF16)
            dkt = [jnp.zeros((HEAD_DIM, 2 * CHUNK), F32) for _ in range(2)]
            dvt = [jnp.zeros((HEAD_DIM, 2 * CHUNK), F32) for _ in range(2)]
            for pr in range(N_HEADS // 2):
                ps = slice(pr * LANES, (pr + 1) * LANES)
                qpt = transposed(qkv_ref[pl.ds(r0, CHUNK), ps])
                dopt = transposed(do_ref[pl.ds(r0, CHUNK), ps])
                kvh = pr // 2
                dq = jnp.zeros((CHUNK, LANES), F32)
                for hh in range(2):
                    hr = _head_rows(2 * pr + hh)
                    rows = slice(hh * HEAD_DIM, (hh + 1) * HEAD_DIM)
                    dq = dq + _dot(dsb[hr], kv[kvh][hh])
                    dkt[kvh] = dkt[kvh] + _dot(qpt, dsb[hr])[rows]
                    dvt[kvh] = dvt[kvh] + _dot(dopt, pb[hr])[rows]
                d_ref[pl.ds(r0, CHUNK), ps] = dq.astype(BF16)
            dk_scr[:, pl.ds(r0, 2 * CHUNK)] += jnp.concatenate(dkt, axis=0)
            dv_scr[:, pl.ds(r0, 2 * CHUNK)] += jnp.concatenate(dvt, axis=0)
            return carry

        lax.fori_loop(0, nb, blk, 0)
        for n in range(nb):
            rows = slice(n * CHUNK, (n + 1) * CHUNK)
            cols = slice((n + 1) * CHUNK, (n + 2) * CHUNK)
            d_ref[rows, Q_DIM:Q_DIM + KV_DIM] = dk_scr[:, cols].T.astype(BF16)
            d_ref[rows, Q_DIM + KV_DIM:] = dv_scr[:, cols].T.astype(BF16)

        @pl.when(b == n_seq - 1)
        def _():
            bkv = bk_ref[...]
            for h in range(N_HEADS):
                gs_ref[0:1, h:h + 1] = -jnp.sum(ds_scr[_head_rows(h), 0:1], axis=0, keepdims=True)
                db = dbias_scr[_head_rows(h), :]
                for bb in range(N_BUCKETS):
                    part = jnp.sum(jnp.where(bkv == bb, db, 0.0), axis=-1, keepdims=True)
                    gr_ref[bb:bb + 1, h:h + 1] = jnp.sum(part, axis=0, keepdims=True)

    smem = pl.BlockSpec(memory_space=pltpu.SMEM)
    return pl.pallas_call(
        body, name="attn_bwd", grid=(n_seq,),
        in_specs=[_row(seq, B_DIM), _row(seq, Q_DIM), _full(bk.shape), smem, smem],
        out_specs=[_row(seq, B_DIM), _full((1, N_HEADS)), _full((N_BUCKETS, N_HEADS))],
        out_shape=[_sds((n_seq * seq, B_DIM), BF16), _sds((1, N_HEADS), F32), _sds((N_BUCKETS, N_HEADS), F32)],
        scratch_shapes=[pltpu.VMEM((HEAD_ROWS, 2 * CHUNK), F32), pltpu.VMEM((HEAD_ROWS, LANES), F32),
                        pltpu.VMEM((8, seq, KV_DIM), BF16), pltpu.VMEM((HEAD_ROWS, 2 * CHUNK), F32),
                        pltpu.VMEM((KV_DIM, seq + CHUNK), F32), pltpu.VMEM((KV_DIM, seq + CHUNK), F32),
                        pltpu.VMEM((HEAD_ROWS, LANES), F32)],
        compiler_params=_cp(("arbitrary",), 40),
    )(*_hbm(proj_b, d_yb, bk), rel_bias, sinks)


def _inproj_bwd(d_g, d_a, d_b, x2, dx1, g_mix, w_in, tm, after=None):
    T = x2.shape[0]
    order = [] if after is None else [after]

    def body(*refs):
        dg_ref, da_ref, db_ref, x_ref, dx1_ref, g_ref, w_ref = refs[:7]
        gx_ref, gg_ref = refs[7 + len(order):]
        dh = (_dot_nt(dg_ref[...], w_ref[:, _G_COLS]) + _dot_nt(da_ref[...], w_ref[:, _A_COLS])
              + _dot_nt(db_ref[...], w_ref[:, _B_COLS]))
        x = x_ref[...]
        r = _rms_r(x)
        n = x * r
        gx_ref[...] = dx1_ref[...] + _rms_bwd(dh, n, r, g_ref[...])

        @pl.when(pl.program_id(0) == 0)
        def _():
            gg_ref[...] = jnp.zeros_like(gg_ref)

        gg_ref[...] += jnp.sum(dh * n, axis=0, keepdims=True)

    return pl.pallas_call(
        body, name="inproj_bwd", grid=(T // tm,),
        in_specs=[_row(tm, G_DIM), _row(tm, A_DIM), _row(tm, B_DIM), _row(tm, D_MODEL), _row(tm, D_MODEL),
                  _full(g_mix.shape), _resident(w_in.shape)] + [ANY] * len(order),
        out_specs=[_row(tm, D_MODEL), _full((1, D_MODEL))],
        out_shape=[_sds((T, D_MODEL), F32), _sds((1, D_MODEL), F32)],
        compiler_params=_cp(("arbitrary",), 48),
    )(*_hbm(d_g, d_a, d_b, x2, dx1, g_mix, w_in), *order)


IN_SHARD = (A_DIM + B_DIM + G_DIM) // N_CHIPS


def _unstack_w_in(stack):
    tr = 256

    def body(s_ref, o_ref):
        for i in range(N_CHIPS):
            o_ref[:, i * IN_SHARD:(i + 1) * IN_SHARD] = s_ref[i]

    return pl.pallas_call(
        body, name="unstack_w_in", grid=(D_MODEL // tr,),
        in_specs=[pl.BlockSpec((N_CHIPS, tr, IN_SHARD), lambda r: (0, r, 0))],
        out_specs=pl.BlockSpec((tr, N_CHIPS * IN_SHARD), lambda r: (r, 0)),
        out_shape=_sds((D_MODEL, N_CHIPS * IN_SHARD), stack.dtype),
        compiler_params=_cp(("arbitrary",)),
    )(*_hbm(stack))


def _stack_grad_w_in(gw_a, gw_b, gw_g):
    tr = 256

    def body(a_ref, b_ref, g_ref, o_ref):
        full = jnp.concatenate([a_ref[...], b_ref[...], g_ref[...]], axis=1)
        for i in range(N_CHIPS):
            o_ref[i] = full[:, i * IN_SHARD:(i + 1) * IN_SHARD]

    return pl.pallas_call(
        body, name="stack_grad_w_in", grid=(D_MODEL // tr,),
        in_specs=[_row(tr, A_DIM), _row(tr, B_DIM), _row(tr, G_DIM)],
        out_specs=pl.BlockSpec((N_CHIPS, tr, IN_SHARD), lambda r: (0, r, 0)),
        out_shape=_sds((N_CHIPS, D_MODEL, IN_SHARD), gw_a.dtype),
        compiler_params=_cp(("arbitrary",)),
    )(*_hbm(gw_a, gw_b, gw_g))


def _local_step(x, target, g_mix, g_sgu, w_s, b_s, sinks, rel_bias, g_ffn, b_conv, g_final,
                w_in, w_conv, proj_weights, ffn_weights, on_grads, after=None):
    n_seq, seq, _ = x.shape
    T = n_seq * seq
    tm = min(ROW_TILE, seq)
    tw = min(GRAD_ROW_TILE, T)
    tf = min(WIDE_ROW_TILE, seq)
    x2 = x.reshape(T, D_MODEL)
    tgt = target.reshape(T, D_MODEL)
    b_st = b_s.T
    g_fin = g_final.reshape(1, D_MODEL)

    proj_g, proj_a, proj_b, h = _inproj(x2, g_mix, w_in, tm, after)
    y_a = _sgu_fwd(proj_a, g_sgu, w_s, b_st, tm)
    y_b = _attn_fwd(proj_b, sinks, rel_bias, n_seq, seq)
    w_pa, w_pb, w_out = proj_weights(y_b)
    x1, merged = _merge_fwd(x2, y_a, y_b, proj_g, w_pa, w_pb, w_out, tm)
    w_up, w_down = ffn_weights(x1)
    upre, h2, gate, val = _upproj(x1, g_ffn, w_up, w_conv, b_conv, tf, seq)
    dx2, loss, gg_final = _ffn_down_loss(gate, val, x1, tgt, w_down, g_fin, tm)

    d_gate, d_val, gw_down, gb_g, gb_v = _ffn_bwd_act(gate, val, dx2, w_down, tw)
    gb_conv = jnp.concatenate([gb_g, gb_v], axis=1)
    d_upre, dx1, gg_ffn, gw_conv = _ffn_bwd_up(d_gate, d_val, upre, dx2, x1, g_ffn, w_conv, w_up, tf, seq)
    gw_up = _matmul_tn(h2, d_upre, 2 * D_FF // 4, min(2 * GRAD_ROW_TILE, T), "grad_w_up")
    sent = on_grads("ffn", dict(w_up=gw_up, w_down=gw_down))
    d_g, d_ya, d_yb, gw_out, gw_pa, gw_pb = _merge_bwd(dx1, merged, y_a, y_b, proj_g, w_pa, w_pb, w_out, tf, sent)
    sent = on_grads("proj", dict(w_pa=gw_pa, w_pb=gw_pb, w_out=gw_out))
    d_a, gw_s, gb_st, gg_sgu = _sgu_bwd(proj_a, d_ya, g_sgu, w_s, b_st, tm, sent)
    d_b, g_sinks, g_rel = _attn_bwd(proj_b, _tie(d_yb, d_a), sinks, rel_bias, n_seq, seq)
    gw_g = _matmul_tn(h, _tie(d_g, d_b), D_MODEL, min(2 * GRAD_ROW_TILE, T), "grad_w_in_gate")
    gw_a = _matmul_tn(h, _tie(d_a, gw_g), A_DIM, min(2 * GRAD_ROW_TILE, T), "grad_w_in_a")
    gw_b = _matmul_tn(h, _tie(d_b, gw_a), B_DIM, min(2 * GRAD_ROW_TILE, T), "grad_w_in_b")
    gw_in = _stack_grad_w_in(gw_a, gw_b, gw_g)
    sent = on_grads("in", dict(w_in=gw_in))
    grad_x, gg_mix = _inproj_bwd(d_g, d_a, d_b, x2, dx1, g_mix, w_in, tm, sent)

    small = dict(g_mix=gg_mix, g_sgu=gg_sgu, w_s=gw_s, b_s=gb_st.T, sinks=g_sinks, rel_bias=g_rel,
                 g_ffn=gg_ffn, b_conv=gb_conv, g_final=gg_final, w_conv=gw_conv)
    big = dict(w_in=gw_in, w_pa=gw_pa, w_pb=gw_pb, w_out=gw_out, w_up=gw_up, w_down=gw_down)
    return loss, grad_x.reshape(x.shape), small, big


_MIXER = ("w_in", "w_pa", "w_pb", "w_out")
_FFN = ("w_up", "w_down")
_BIG = _MIXER + _FFN

CONV_ROWS = 6
_SMALL_AT = dict(loss=(0, 1, 1), g_final=(1, 1, D_MODEL), g_mix=(2, 1, D_MODEL), g_ffn=(3, 1, D_MODEL), g_sgu=(4, 1, A_WIDTH),
                 sinks=(5, 1, N_HEADS), rel_bias=(6, 1, N_BUCKETS * N_HEADS), b_s=(8, A_GROUPS, CHUNK),
                 b_conv=(12, CONV_ROWS, D_MODEL), w_conv=(18, 3 * CONV_ROWS, D_MODEL), w_s=(40, A_GROUPS * CHUNK * CHUNK // D_MODEL, D_MODEL))
_SMALL_IN_CALL = ("g_final", "g_mix", "g_ffn", "g_sgu", "sinks", "b_s")
SMALL_ROWS = 104


def _pack_small(vals):
    def wide(a):
        return jnp.pad(a, ((0, 0), (0, CONV_ROWS * D_MODEL - a.shape[1]))).reshape(-1, D_MODEL)

    laid = dict(vals, b_conv=wide(vals["b_conv"]), w_conv=wide(vals["w_conv"]), w_s=vals["w_s"].reshape(-1, D_MODEL))
    rows, at = [], 0
    for n, (r0, nr, nc) in _SMALL_AT.items():
        if r0 > at:
            rows.append(jnp.zeros((r0 - at, D_MODEL), F32))
        rows.append(jnp.pad(laid[n].astype(F32).reshape(nr, nc), ((0, 0), (0, D_MODEL - nc))))
        at = r0 + nr
    return jnp.concatenate(rows, axis=0)


def _unwide(a, r):
    return a.reshape(r, CONV_ROWS * D_MODEL)[:, :2 * D_FF]


def _mesh_pos():
    return lax.axis_index("x"), lax.axis_index("y"), lax.axis_index("c")


def _other_chips(x, y):
    return [(1 - x, y), (x, 1 - y), (1 - x, 1 - y)]


def _remote(src, dst, send_sem, recv_sem, to):
    return pltpu.make_async_remote_copy(src_ref=src, dst_ref=dst, send_sem=send_sem, recv_sem=recv_sem,
                                        device_id=to, device_id_type=MESH)


def _own_slot(own, n, at):
    return lax.dynamic_update_slice(lax.empty((n,) + own.shape, own.dtype), own[None], (at,) + (0,) * own.ndim)


def _allgather_weights(stacks, wc_stack):
    names = list(stacks)
    n = len(names)

    def body(*refs):
        ins, outs = refs[:n + 1], refs[n + 1:2 * n + 2]
        send_sems, recv_sems = refs[2 * n + 2:]
        x, y, c = _mesh_pos()
        _handshake(_chip_peers(x, y, c) + _sibling_peers(x, y, c))
        me = 2 * x + y
        sibling = (x, y, 1 - c)
        chips = _other_chips(x, y)

        def half(ref, chip, hc):
            hr = ref.shape[1] // 2
            return ref.at[chip, pl.ds(hc * hr, hr), :]

        first = []
        for k in range(n):
            first += [_remote(half(ins[k], me, c), half(outs[k], me, c), send_sems.at[6 * k + j], recv_sems.at[6 * k + j], (cx, cy, c))
                      for j, (cx, cy) in enumerate(chips)]
        first += [_remote(ins[n].at[me], outs[n].at[me], send_sems.at[6 * n + j], recv_sems.at[6 * n + j], (cx, cy, c))
                  for j, (cx, cy) in enumerate(chips)]
        for cp in first:
            cp.start()
        passed = []
        for k in range(n):
            for j, (cx, cy) in enumerate(chips):
                landed = half(outs[k], 2 * cx + cy, c)
                _remote(landed, landed, send_sems.at[6 * k + j], recv_sems.at[6 * k + j], (x, y, c)).wait_recv()
                passed.append(_remote(landed, landed, send_sems.at[6 * k + 3 + j], recv_sems.at[6 * k + 3 + j], sibling))
                passed[-1].start()
        for k in range(n):
            for j, (cx, cy) in enumerate(chips):
                theirs = half(outs[k], 2 * cx + cy, 1 - c)
                _remote(theirs, theirs, send_sems.at[6 * k + 3 + j], recv_sems.at[6 * k + 3 + j], (x, y, c)).wait_recv()
        for j, (cx, cy) in enumerate(chips):
            slot = outs[n].at[2 * cx + cy]
            _remote(slot, slot, send_sems.at[6 * n + j], recv_sems.at[6 * n + j], (x, y, c)).wait_recv()
        for cp in first + passed:
            cp.wait_send()

    arrays = [stacks[k] for k in names] + [wc_stack]
    outs = pl.pallas_call(
        body, name="allgather_weights",
        in_specs=[HBM] * (n + 1), out_specs=[HBM] * (n + 1), input_output_aliases={k: k for k in range(n + 1)},
        out_shape=[_sds(a.shape, a.dtype) for a in arrays],
        scratch_shapes=[pltpu.SemaphoreType.DMA((6 * n + 3,)), pltpu.SemaphoreType.DMA((6 * n + 3,))],
        compiler_params=pltpu.CompilerParams(collective_id=_COLLECTIVE["gather_in"]),
    )(*arrays)
    return dict(zip(names, outs[:n])), outs[n]


_KIND = {"w_in": "stack", "w_pa": "col", "w_pb": "col", "w_up": "col", "w_out": "row", "w_down": "row"}


def _half_view(ref, kind, h):
    if kind == "stack":
        k = ref.shape[1] // 2
        return ref.at[:, pl.ds(h * k, k), :]
    if kind == "col":
        k = ref.shape[0] // 2
        return ref.at[pl.ds(h * k, k), :]
    k = ref.shape[1] // 2
    return ref.at[:, pl.ds(h * k, k)]


def _shard_view(ref, kind, i):
    if kind == "stack":
        return ref.at[i]
    if kind == "col":
        k = ref.shape[1] // N_CHIPS
        return ref.at[:, pl.ds(i * k, k)]
    k = ref.shape[0] // N_CHIPS
    return ref.at[pl.ds(i * k, k), :]


def _region_view(ref, kind, h):
    if kind == "row":
        k = ref.shape[1] // 2
        return ref.at[:, pl.ds(h * k, k)]
    k = ref.shape[0] // 2
    return ref.at[pl.ds(h * k, k), :]


def _half_shape(shape, kind):
    if kind == "stack":
        return (shape[0], shape[1] // 2, shape[2])
    return (shape[0] // 2, shape[1]) if kind == "col" else (shape[0], shape[1] // 2)


def _part_shape(half_shape, kind):
    if kind == "stack":
        return tuple(half_shape[1:])
    k, w = half_shape
    return (k, w // N_CHIPS) if kind == "col" else (k // N_CHIPS, w)


_DATAFLOW = pltpu.SideEffectType.DATAFLOW_SIDE_EFFECTING
_TOKEN = (SUBLANES, LANES)


_COLLECTIVE = {k: i for i, k in enumerate(
    [kind + "_" + g for kind in ("pair", "chip", "share") for g in ("ffn", "proj", "in")]
    + ["gather_proj", "gather_ffn", "gather_in", "forward_proj", "forward_ffn"])}


def _sibling_peers(x, y, c):
    return [(x, y, 1 - c)]


def _chip_peers(x, y, c):
    return [(cx, cy, c) for cx, cy in _other_chips(x, y)]


def _handshake(peers):
    barrier = pltpu.get_barrier_semaphore()
    for peer in peers:
        pl.semaphore_signal(barrier, inc=1, device_id=peer, device_id_type=MESH)
    pl.semaphore_wait(barrier, len(peers))


def _split_start(name, arrays, n_sems, issue, after=None, handshake=None):
    n = len(arrays)
    order = [] if after is None else [after]

    def body(*refs):
        base = n + len(order)
        if handshake is not None:
            _handshake(handshake[1](*_mesh_pos()))
        issue(refs[:n], refs[base], refs[base + 1])
        refs[-1][...] = jnp.zeros(_TOKEN, F32)

    params = dict(has_side_effects=_DATAFLOW)
    if handshake is not None:
        params["collective_id"] = handshake[0]
    outs = pl.pallas_call(
        body, name=name,
        in_specs=[HBM] * n + [ANY] * len(order), out_specs=[SEM, SEM] + [HBM] * n + [pl.BlockSpec(memory_space=pltpu.VMEM)],
        out_shape=[pltpu.SemaphoreType.DMA((n_sems,)), pltpu.SemaphoreType.DMA((n_sems,))]
        + [pltpu.HBM(a.shape, a.dtype) for a in arrays] + [_sds(_TOKEN, F32)],
        input_output_aliases={k: 2 + k for k in range(n)},
        compiler_params=pltpu.CompilerParams(**params),
    )(*[pltpu.with_memory_space_constraint(a, pltpu.HBM) for a in arrays], *order)
    return outs[0], outs[1], list(outs[2:2 + n]), outs[-1]


def _split_wait(name, started, waits, after):
    send_sems, recv_sems, arrays, _ = started
    n = len(arrays)

    def body(*refs):
        waits(refs[:n], refs[n], refs[n + 1])

    return pl.pallas_call(
        body, name=name,
        in_specs=[HBM] * n + [SEM, SEM, ANY], out_specs=[HBM] * n,
        out_shape=[pltpu.HBM(a.shape, a.dtype) for a in arrays],
        input_output_aliases={k: k for k in range(n)},
        compiler_params=pltpu.CompilerParams(has_side_effects=_DATAFLOW),
    )(*arrays, send_sems, recv_sems, after)


def _wait_both(src, dst, send_sem, recv_sem):
    x, y, c = _mesh_pos()
    cp = _remote(src, dst, send_sem, recv_sem, (x, y, c))
    cp.wait_send()
    cp.wait_recv()


def _pair_exchange_start(parts, tag, after):
    names = list(parts)
    n = len(names)
    lands = [lax.empty(_half_shape(parts[k].shape, _KIND[k]), parts[k].dtype) for k in names]

    def issue(refs, send_sems, recv_sems):
        x, y, c = _mesh_pos()
        for hc in range(2):
            @pl.when(c == hc)
            def _():
                for k in range(n):
                    _remote(_half_view(refs[k], _KIND[names[k]], 1 - hc), refs[n + k], send_sems.at[k], recv_sems.at[k],
                            (x, y, 1 - c)).start()

    return names, _split_start("grad_pair_exchange_start_" + tag, [parts[k] for k in names] + lands, n, issue, after,
                               (_COLLECTIVE["pair_" + tag], _sibling_peers))


def _pair_exchange_wait(pending, tag, after):
    names, started = pending
    n = len(names)

    def waits(refs, send_sems, recv_sems):
        for k in range(n):
            _wait_both(_half_view(refs[k], _KIND[names[k]], 0), refs[n + k], send_sems.at[k], recv_sems.at[k])

    outs = _split_wait("grad_pair_exchange_wait_" + tag, started, waits, after)
    return dict(zip(names, outs[:n])), dict(zip(names, outs[n:]))


def _half_blocks(shape, kind):
    if kind == "stack":
        _, k, w = shape
        tr = k // 2
        nb = 1
        return (N_CHIPS, nb), (1, tr, w), (lambda i, r, s: (i, r, 0)), (lambda i, r, s: (i, s[1] * nb + r, 0))
    k, w = shape
    if kind == "col":
        tr = 256
        nb = k // 2 // tr
        return (nb,), (tr, w), (lambda r, s: (r, 0)), (lambda r, s: (s[1] * nb + r, 0))
    tr = k // N_CHIPS
    return (N_CHIPS,), (tr, w // 2), (lambda r, s: (r, 0)), (lambda r, s: (r, s[1]))


def _pair_add(part, from_sibling, name, pos):
    kind = _KIND[name]
    grid, block, half_map, full_map = _half_blocks(part.shape, kind)

    def body(s_ref, p_ref, q_ref, o_ref):
        o_ref[...] = (p_ref[...].astype(F32) + q_ref[...].astype(F32)).astype(BF16)

    return pl.pallas_call(
        body, name="grad_pair_add_" + name,
        grid_spec=pltpu.PrefetchScalarGridSpec(
            num_scalar_prefetch=1, grid=grid,
            in_specs=[pl.BlockSpec(block, full_map), pl.BlockSpec(block, half_map)],
            out_specs=pl.BlockSpec(block, half_map)),
        out_shape=_sds(from_sibling.shape, BF16),
        compiler_params=_cp(("arbitrary",) * len(grid), 40),
    )(pos, *_hbm(part, from_sibling))


def _chip_exchange_start(sums, tag, after):
    names = list(sums)
    n = len(names)
    lands = [lax.empty((3,) + _part_shape(sums[k].shape, _KIND[k]), sums[k].dtype) for k in names]

    def issue(refs, send_sems, recv_sems):
        x, y, c = _mesh_pos()
        me = 2 * x + y
        for i in range(N_CHIPS):
            xi, yi = i // 2, i % 2
            j = jnp.where(xi != x, jnp.where(yi != y, 2, 0), 1)

            @pl.when(i != me)
            def _():
                for k in range(n):
                    _remote(_shard_view(refs[k], _KIND[names[k]], i), refs[n + k].at[j], send_sems.at[3 * k + j],
                            recv_sems.at[3 * k + j], (xi, yi, c)).start()

    return names, _split_start("grad_chip_exchange_start_" + tag, [sums[k] for k in names] + lands, 3 * n, issue, after,
                               (_COLLECTIVE["chip_" + tag], _chip_peers))


def _chip_exchange_wait(pending, tag, after):
    names, started = pending
    n = len(names)

    def waits(refs, send_sems, recv_sems):
        for k in range(n):
            for j in range(3):
                _wait_both(_shard_view(refs[k], _KIND[names[k]], 0), refs[n + k].at[j], send_sems.at[3 * k + j], recv_sems.at[3 * k + j])

    return dict(zip(names, _split_wait("grad_chip_exchange_wait_" + tag, started, waits, after)[n:]))


def _allgather_start(stacks, tag, after):
    names = list(stacks)

    def issue(refs, send_sems, recv_sems):
        x, y, c = _mesh_pos()
        me = 2 * x + y
        for k, st in enumerate(refs):
            hr = st.shape[1] // 2
            mine = st.at[me, pl.ds(c * hr, hr), :]
            for j, (cx, cy) in enumerate(_other_chips(x, y)):
                _remote(mine, mine, send_sems.at[3 * k + j], recv_sems.at[3 * k + j], (cx, cy, c)).start()

    return names, _split_start("allgather_start_" + tag, [stacks[k] for k in names], 3 * len(names), issue, after,
                               (_COLLECTIVE["gather_" + tag], _chip_peers))


def _allgather_wait(pending, tag, after):
    names, started = pending

    def waits(refs, send_sems, recv_sems):
        for k, st in enumerate(refs):
            slot = st.at[0, pl.ds(0, st.shape[1] // 2), :]
            for j in range(3):
                _wait_both(slot, slot, send_sems.at[3 * k + j], recv_sems.at[3 * k + j])

    return dict(zip(names, _split_wait("allgather_wait_" + tag, started, waits, after)))


def _allgather_forward(stacks, tag):
    names = list(stacks)
    n = len(names)

    def body(*refs):
        ins, outs = refs[:n], refs[n:2 * n]
        send_sems, recv_sems = refs[2 * n:]
        x, y, c = _mesh_pos()
        _handshake(_sibling_peers(x, y, c))
        copies = []
        for k in range(n):
            hr = ins[k].shape[1] // 2
            for j, (cx, cy) in enumerate(_other_chips(x, y)):
                chip = 2 * cx + cy
                copies.append(_remote(ins[k].at[chip, pl.ds(c * hr, hr), :], outs[k].at[chip, pl.ds(c * hr, hr), :],
                                      send_sems.at[3 * k + j], recv_sems.at[3 * k + j], (x, y, 1 - c)))
        for cp in copies:
            cp.start()
        for cp in copies:
            cp.wait()

    arrays = [stacks[k] for k in names]
    outs = pl.pallas_call(
        body, name="allgather_forward_" + tag, in_specs=[HBM] * n, out_specs=[HBM] * n,
        input_output_aliases={k: k for k in range(n)},
        out_shape=[_sds(a.shape, a.dtype) for a in arrays],
        scratch_shapes=[pltpu.SemaphoreType.DMA((3 * n,)), pltpu.SemaphoreType.DMA((3 * n,))],
        compiler_params=pltpu.CompilerParams(collective_id=_COLLECTIVE["forward_" + tag]),
    )(*arrays)
    return dict(zip(names, outs))


def _owner_sum(part, from_sibling, from_chips, name, pos, shard_shape):
    kind = _KIND[name]
    _, pk, pw = from_chips.shape
    if kind == "row":
        tr, nb = pk, 1
        p_spec = pl.BlockSpec((tr, pw), lambda r, s: (s[0], s[1]))
        q_spec = pl.BlockSpec((tr, pw), lambda r, s: (s[0], 0))
        o_spec = pl.BlockSpec((tr, pw), lambda r, s: (0, s[1]))
    else:
        tr = 256
        nb = pk // tr
        if kind == "stack":
            p_spec = pl.BlockSpec((None, tr, pw), lambda r, s: (s[0], s[1] * nb + r, 0))
            q_spec = pl.BlockSpec((None, tr, pw), lambda r, s: (s[0], r, 0))
        else:
            p_spec = pl.BlockSpec((tr, pw), lambda r, s: (s[1] * nb + r, s[0]))
            q_spec = pl.BlockSpec((tr, pw), lambda r, s: (r, s[0]))
        o_spec = pl.BlockSpec((tr, pw), lambda r, s: (s[1] * nb + r, 0))

    def body(s_ref, p_ref, q_ref, r_ref, o_ref):
        acc = p_ref[...].astype(F32) + q_ref[...].astype(F32)
        for j in range(3):
            acc = acc + r_ref[j].astype(F32)
        o_ref[...] = acc

    return pl.pallas_call(
        body, name="grad_owner_sum_" + name,
        grid_spec=pltpu.PrefetchScalarGridSpec(
            num_scalar_prefetch=1, grid=(nb,),
            in_specs=[p_spec, q_spec, pl.BlockSpec((3, tr, pw), lambda r, s: (0, r, 0))],
            out_specs=o_spec),
        out_shape=_sds(shard_shape, F32),
        compiler_params=_cp(("arbitrary",), 32),
    )(pos, *_hbm(part, from_sibling, from_chips))


def _pair_share_start(shards, tag, after):
    names = list(shards)

    def issue(refs, send_sems, recv_sems):
        x, y, c = _mesh_pos()
        for hc in range(2):
            @pl.when(c == hc)
            def _():
                for k, g in enumerate(refs):
                    mine = _region_view(g, _KIND[names[k]], hc)
                    _remote(mine, mine, send_sems.at[k], recv_sems.at[k], (x, y, 1 - c)).start()

    return names, _split_start("grad_pair_share_start_" + tag, [shards[k] for k in names], len(names), issue, after,
                               (_COLLECTIVE["share_" + tag], _sibling_peers))


def _pair_share_wait(pending, tag, after):
    names, started = pending

    def waits(refs, send_sems, recv_sems):
        for k, g in enumerate(refs):
            region = _region_view(g, _KIND[names[k]], 0)
            _wait_both(region, region, send_sems.at[k], recv_sems.at[k])

    return dict(zip(names, _split_wait("grad_pair_share_wait_" + tag, started, waits, after)))


def _small_exchange_start(slots, after):
    def issue(refs, send_sems, recv_sems):
        x, y, c = _mesh_pos()
        mine = refs[0].at[4 * x + 2 * y + c]
        k = 0
        for px in range(2):
            for py in range(2):
                for pc in range(2):
                    if px + py + pc:
                        peer = (1 - x if px else x, 1 - y if py else y, 1 - c if pc else c)
                        _remote(mine, mine, send_sems.at[k], recv_sems.at[k], peer).start()
                        k += 1

    return _split_start("small_exchange_start", [slots], N_DEV - 1, issue, after)


def _small_exchange_wait(started, after):
    def waits(refs, send_sems, recv_sems):
        slot = refs[0].at[0]
        for k in range(N_DEV - 1):
            _wait_both(slot, slot, send_sems.at[k], recv_sems.at[k])

    return _split_wait("small_exchange_wait", started, waits, after)[0]


def _adam_math(w, g, m, v):
    m = ADAM_B1 * m + (1.0 - ADAM_B1) * g
    v = ADAM_B2 * v + (1.0 - ADAM_B2) * (g * g)
    m_hat = m / (1.0 - ADAM_B1 ** ADAM_STEP)
    v_hat = v / (1.0 - ADAM_B2 ** ADAM_STEP)
    delta = -ADAM_LR * (m_hat / (jnp.sqrt(v_hat) + ADAM_EPS) + ADAM_WD * w)
    return delta, m, v


def _adamw(w, g, m, v, name):
    rows, cols = w.shape
    fits = [t for t in range(SUBLANES, rows, SUBLANES) if rows % t == 0 and t * cols * 4 <= (3 << 19)]
    tr = max(fits) if fits else rows

    def body(w_ref, g_ref, m_ref, v_ref, d_ref, nm_ref, nv_ref, go_ref):
        g = g_ref[...]
        d, nm, nv = _adam_math(w_ref[...], g, m_ref[...], v_ref[...])
        d_ref[...] = d
        nm_ref[...] = nm
        nv_ref[...] = nv
        go_ref[...] = g

    spec = pl.BlockSpec((tr, cols), lambda i: (i, 0))
    return pl.pallas_call(
        body, name=name, grid=(rows // tr,), in_specs=[spec] * 4, out_specs=[spec] * 4,
        out_shape=[_sds(w.shape, F32)] * 4, compiler_params=_cp(("arbitrary",)),
    )(*_hbm(w, g, m, v))


def _small_sum_adamw(gathered, w, m, v):
    names = _SMALL_IN_CALL
    n = len(names)

    def body(*refs):
        a_ref = refs[0]
        w_refs, m_refs, v_refs = refs[1:1 + n], refs[1 + n:1 + 2 * n], refs[1 + 2 * n:1 + 3 * n]
        sum_ref = refs[1 + 3 * n]
        outs = refs[2 + 3 * n:]
        g = a_ref[0]
        for k in range(1, N_DEV):
            g = g + a_ref[k]
        sum_ref[...] = g
        for i, name in enumerate(names):
            r0, nr, nc = _SMALL_AT[name]
            gp = g[r0:r0 + nr, 0:nc]
            d, nm, nv = _adam_math(w_refs[i][...], gp, m_refs[i][...], v_refs[i][...])
            for k, val in enumerate((gp, d, nm, nv)):
                outs[4 * i + k][...] = val

    shapes = [w[k].shape for k in names]
    res = pl.pallas_call(
        body, name="small_sum_adamw",
        out_shape=[_sds((SMALL_ROWS, D_MODEL), F32)] + [_sds(s, F32) for s in shapes for _ in range(4)],
    )(gathered, *[w[k] for k in names], *[m[k] for k in names], *[v[k] for k in names])
    return res[0], {k: tuple(res[1 + 4 * i:5 + 4 * i]) for i, k in enumerate(names)}


_NAMES = ("g_mix", "w_in", "g_sgu", "w_s", "b_s", "sinks", "rel_bias", "w_pa", "w_pb", "w_out",
          "g_ffn", "w_up", "w_conv", "b_conv", "w_down", "g_final")

def kernel(x, g_mix, w_in, g_sgu, w_s, b_s, sinks, rel_bias, w_pa, w_pb, w_out, g_ffn, w_up, w_conv, b_conv, w_down, g_final, loss_target, m_g_mix, m_w_in, m_g_sgu, m_w_s, m_b_s, m_sinks, m_rel_bias, m_w_pa, m_w_pb, m_w_out, m_g_ffn, m_w_up, m_w_conv, m_b_conv, m_w_down, m_g_final, v_g_mix, v_w_in, v_g_sgu, v_w_s, v_b_s, v_sinks, v_rel_bias, v_w_pa, v_w_pb, v_w_out, v_g_ffn, v_w_up, v_w_conv, v_b_conv, v_w_down, v_g_final):
    w = dict(g_mix=g_mix, w_in=w_in, g_sgu=g_sgu, w_s=w_s, b_s=b_s, sinks=sinks, rel_bias=rel_bias, w_pa=w_pa, w_pb=w_pb,
             w_out=w_out, g_ffn=g_ffn, w_up=w_up, w_conv=w_conv, b_conv=b_conv, w_down=w_down, g_final=g_final)
    m = dict(g_mix=m_g_mix, w_in=m_w_in, g_sgu=m_g_sgu, w_s=m_w_s, b_s=m_b_s, sinks=m_sinks, rel_bias=m_rel_bias, w_pa=m_w_pa,
             w_pb=m_w_pb, w_out=m_w_out, g_ffn=m_g_ffn, w_up=m_w_up, w_conv=m_w_conv, b_conv=m_b_conv, w_down=m_w_down,
             g_final=m_g_final)
    v = dict(g_mix=v_g_mix, w_in=v_w_in, g_sgu=v_g_sgu, w_s=v_w_s, b_s=v_b_s, sinks=v_sinks, rel_bias=v_rel_bias, w_pa=v_w_pa,
             w_pb=v_w_pb, w_out=v_w_out, g_ffn=v_g_ffn, w_up=v_w_up, w_conv=v_w_conv, b_conv=v_b_conv, w_down=v_w_down,
             g_final=v_g_final)
    xi, yi, ci = _mesh_pos()
    me = 2 * xi + yi

    shard = {n: w[n][0] for n in _BIG}
    shard_shapes = {n: shard[n].shape for n in _BIG}
    wc_shard = w["w_conv"][0]
    wc_pad = jnp.pad(wc_shard, ((0, 5), (0, 0)))
    own = {n: _own_slot(shard[n].astype(BF16), N_CHIPS, me) for n in _BIG}
    stacks, wc_all = _allgather_weights({"w_in": own["w_in"]}, _own_slot(wc_pad, N_CHIPS, me))
    proj_gather = _allgather_start({n: own[n] for n in _MIXER[1:]}, "proj", stacks["w_in"])
    ffn_gather = _allgather_start({n: own[n] for n in _FFN}, "ffn", proj_gather[1][-1])
    w_conv_full = jnp.concatenate([wc_all[i, :3] for i in range(N_CHIPS)], axis=1)
    w_in_full = _unstack_w_in(stacks["w_in"])
    pos = jnp.stack([me, ci])

    def proj_weights(done):
        st = _allgather_forward(_allgather_wait(proj_gather, "proj", done), "proj")
        return st["w_pa"], st["w_pb"], st["w_out"].reshape(D_MODEL, D_MODEL)

    def ffn_weights(done):
        st = _allgather_forward(_allgather_wait(ffn_gather, "ffn", done), "ffn")
        return st["w_up"], st["w_down"].reshape(D_FF, D_MODEL)

    groups = {}

    def stage1(group, parts):
        groups[group] = dict(parts=parts, pair=_pair_exchange_start(parts, group, None))
        return groups[group]["pair"][1][-1]

    def stage2(group, after, order_after):
        g = groups[group]
        g["parts"], g["sib"] = _pair_exchange_wait(g["pair"], group, after)
        g["chip"] = _chip_exchange_start({n: _pair_add(g["parts"][n], g["sib"][n], n, pos) for n in g["parts"]}, group, order_after)
        return g["chip"][1][-1]

    def stage3(group, after, order_after):
        g = groups[group]
        got = _chip_exchange_wait(g["chip"], group, after)
        g["share"] = _pair_share_start(
            {n: _owner_sum(g["parts"][n], g["sib"][n], got[n], n, pos, shard_shapes[n]) for n in g["parts"]}, group, order_after)
        return g["share"][1][-1]

    grads, deltas, new_m, new_v = {}, {}, {}, {}

    def stage4(group, after):
        g_shard = _pair_share_wait(groups[group]["share"], group, after)
        last = None
        for n in g_shard:
            g = _tie(g_shard[n], last)
            if n == "w_in":
                d, nm, nv, gt = _adamw(shard[n].T, g.T, m[n][0].T, v[n][0].T, "adamw_" + n)
                grads[n], deltas[n], new_m[n], new_v[n] = gt.T[None], d.T[None], nm.T[None], nv.T[None]
            else:
                d, nm, nv, go = _adamw(shard[n], g, m[n][0], v[n][0], "adamw_" + n)
                grads[n], deltas[n], new_m[n], new_v[n] = go[None], d[None], nm[None], nv[None]
            last = nv
        return last

    def on_grads(group, parts):
        token = stage1(group, parts)
        some = next(iter(parts.values()))
        if group == "proj":
            token = stage2("ffn", some, token)
        if group == "in":
            token = stage2("proj", some, token)
            token = stage3("ffn", some, token)
            token = stage2("in", token, token)
        return token

    loss, grad_x, small, big = _local_step(
        x, loss_target, w["g_mix"], w["g_sgu"], w["w_s"][0], w["b_s"][0], w["sinks"], w["rel_bias"], w["g_ffn"],
        w["b_conv"], w["g_final"], w_in_full, w_conv_full, proj_weights, ffn_weights, on_grads, ffn_gather[1][-1])

    small["loss"] = loss
    small_gather = _small_exchange_start(_own_slot(_pack_small(small), N_DEV, 2 * me + ci), grad_x)
    token = stage3("proj", grad_x, small_gather[-1])
    done = stage4("ffn", token)
    done = stage4("proj", done)
    token = stage3("in", done, None)
    all_small = _small_exchange_wait(small_gather, token)
    two_d = {n: (lambda a, n=n: a.reshape(_SMALL_AT[n][1:])) for n in _SMALL_IN_CALL}
    s_sum, s_out = _small_sum_adamw(all_small, *[{n: two_d[n](p[n]) for n in _SMALL_IN_CALL} for p in (w, m, v)])
    stage4("in", all_small)
    for n in _SMALL_IN_CALL:
        grads[n], deltas[n], new_m[n], new_v[n] = [a.reshape(w[n].shape) for a in s_out[n]]

    def rows(n):
        r0, nr, _ = _SMALL_AT[n]
        return s_sum[r0:r0 + nr]

    wcols = wc_shard.shape[1]
    g_wc = lax.dynamic_slice(_unwide(rows("w_conv"), 3), (0, me * wcols), (3, wcols))
    d, nm, nv, _ = _adamw(wc_shard, g_wc, m["w_conv"][0], v["w_conv"][0], "adamw_w_conv")
    grads["w_conv"], deltas["w_conv"], new_m["w_conv"], new_v["w_conv"] = g_wc[None], d[None], nm[None], nv[None]
    d, nm, nv, go = _adamw(w["b_conv"], _unwide(rows("b_conv"), 1), m["b_conv"], v["b_conv"], "adamw_b_conv")
    grads["b_conv"], deltas["b_conv"], new_m["b_conv"], new_v["b_conv"] = go, d, nm, nv
    g_rb = rows("rel_bias")[:, :N_BUCKETS * N_HEADS].reshape(N_BUCKETS, N_HEADS)
    d, nm, nv, go = _adamw(w["rel_bias"], g_rb, m["rel_bias"], v["rel_bias"], "adamw_rel_bias")
    grads["rel_bias"], deltas["rel_bias"], new_m["rel_bias"], new_v["rel_bias"] = go, d, nm, nv
    flat_s = (A_GROUPS * CHUNK, CHUNK)
    d, nm, nv, go = _adamw(w["w_s"].reshape(flat_s), rows("w_s").reshape(flat_s), m["w_s"].reshape(flat_s),
                           v["w_s"].reshape(flat_s), "adamw_w_s")
    grads["w_s"], deltas["w_s"], new_m["w_s"], new_v["w_s"] = [a.reshape(w["w_s"].shape) for a in (go, d, nm, nv)]

    return (s_sum[0, 0], grad_x, *[grads[n] for n in _NAMES], *[deltas[n] for n in _NAMES],
            *[new_m[n] for n in _NAMES], *[new_v[n] for n in _NAMES])
```

```python
import functools

import numpy as np
import jax
import jax.numpy as jnp
from jax import lax
from jax.experimental import pallas as pl
from jax.experimental.pallas import tpu as pltpu

F32 = jnp.float32
BF16 = jnp.bfloat16

D_MODEL = 1024
CHUNK = 128
A_GROUPS = 4
A_WIDTH = 512
N_HEADS = 8
HEAD_DIM = 64
Q_DIM = 512
KV_DIM = 128
N_BUCKETS = 32
MAX_DISTANCE = 128
D_FF = 2816
EPS = 1e-6
NEG_INF = -1e30
G_DIM = 2 * D_MODEL
A_DIM = 2 * A_WIDTH
B_DIM = Q_DIM + 2 * KV_DIM
LANES = 128
SUBLANES = 8
ROW_TILE = 512
WIDE_ROW_TILE = 256
COL_CHUNK = 512
GRAD_ROW_TILE = 512
BF16_ROWS = 16
N_CHIPS = 4
N_DEV = 8

ADAM_LR = 0.001
ADAM_B1 = 0.9
ADAM_B2 = 0.999
ADAM_EPS = 1e-08
ADAM_WD = 0.01
ADAM_STEP = 10

MESH = pl.DeviceIdType.MESH
_GELU_C = 0.7978845608028654
_GELU_A = 0.044715


def _cp(sem=None, vmem_mb=None):
    kw = {}
    if sem is not None:
        kw["dimension_semantics"] = sem
    if vmem_mb is not None:
        kw["vmem_limit_bytes"] = vmem_mb << 20
    return pltpu.CompilerParams(**kw)


def _dot(a, b):
    return jnp.dot(a, b, preferred_element_type=F32)


def _dot_nt(a, b):
    return lax.dot_general(a, b, (((1,), (1,)), ((), ())), preferred_element_type=F32)


def _dot_tn(a, b):
    return lax.dot_general(a, b, (((0,), (0,)), ((), ())), preferred_element_type=F32)


def _rms_r(x):
    return lax.rsqrt(jnp.mean(x * x, axis=-1, keepdims=True) + EPS)


def _rms_bwd(dh, n, r, g):
    dn = dh * g
    return r * (dn - n * jnp.mean(dn * n, axis=-1, keepdims=True))


def _gelu(x):
    t = jnp.tanh(_GELU_C * (x + _GELU_A * (x * x * x)))
    return 0.5 * x * (1.0 + t), t


def _gelu_grad(x, t):
    return 0.5 * (1.0 + t) + 0.5 * x * (1.0 - t * t) * (_GELU_C * (1.0 + 3.0 * _GELU_A * x * x))


def _sigmoid(x):
    return 1.0 / (1.0 + jnp.exp(-x))


def _tie(x, dep):
    return x if dep is None else lax.optimization_barrier((x, dep))[0]


def _row(tm, w):
    return pl.BlockSpec((tm, w), lambda i: (i, 0))


def _full(shape):
    nd = len(shape)
    return pl.BlockSpec(tuple(shape), lambda *_: (0,) * nd)


def _resident(shape):
    nd = len(shape)
    return pl.BlockSpec(tuple(shape), lambda *_: (0,) * nd, pipeline_mode=pl.Buffered(1))


def _sds(shape, dtype):
    return jax.ShapeDtypeStruct(tuple(shape), dtype)


def _hbm(*arrays):
    return [pltpu.with_memory_space_constraint(a, pltpu.HBM) for a in arrays]


HBM = pl.BlockSpec(memory_space=pltpu.HBM)
ANY = pl.BlockSpec(memory_space=pl.ANY)
SEM = pl.BlockSpec(memory_space=pltpu.SEMAPHORE)


def _band_buckets():
    i = np.arange(CHUNK)[:, None]
    j = np.arange(2 * CHUNK)[None, :]
    dist = i + CHUNK - j
    valid = (dist >= 0) & (dist < CHUNK)
    d = np.clip(dist, 0, None)
    max_exact = N_BUCKETS // 2
    large = max_exact + (np.log(np.maximum(d, 1) / max_exact) / np.log(MAX_DISTANCE / max_exact)
                         * (N_BUCKETS - max_exact)).astype(np.int32)
    large = np.minimum(large, N_BUCKETS - 1)
    buckets = np.where(d < max_exact, d, large).astype(np.int32)
    return np.where(valid, buckets, -1).astype(np.int32)


_A_COLS = slice(0, A_DIM)
_B_COLS = slice(A_DIM, A_DIM + B_DIM)
_G_COLS = slice(A_DIM + B_DIM, A_DIM + B_DIM + G_DIM)


def _inproj(x2, g_mix, w_in, tm, after=None):
    T = x2.shape[0]
    order = [] if after is None else [after]

    def body(*refs):
        x_ref, g_ref, w_ref = refs[:3]
        pg_ref, pa_ref, pb_ref, h_ref = refs[3 + len(order):]
        x = x_ref[...]
        h = (x * _rms_r(x) * g_ref[...]).astype(BF16)
        h_ref[...] = h
        pg_ref[...] = _dot(h, w_ref[:, _G_COLS]).astype(BF16)
        pa_ref[...] = _dot(h, w_ref[:, _A_COLS]).astype(BF16)
        pb_ref[...] = _dot(h, w_ref[:, _B_COLS]).astype(BF16)

    return pl.pallas_call(
        body, name="inproj", grid=(T // tm,),
        in_specs=[_row(tm, D_MODEL), _full(g_mix.shape), _resident(w_in.shape)] + [ANY] * len(order),
        out_specs=[_row(tm, G_DIM), _row(tm, A_DIM), _row(tm, B_DIM), _row(tm, D_MODEL)],
        out_shape=[_sds((T, G_DIM), BF16), _sds((T, A_DIM), BF16), _sds((T, B_DIM), BF16), _sds((T, D_MODEL), BF16)],
        compiler_params=_cp(("arbitrary",), 48),
    )(*_hbm(x2, g_mix, w_in), *order)


def _sgu_parts(p, g):
    pu = p[:, :A_WIDTH]
    pv = p[:, A_WIDTH:]
    u, tu = _gelu(pu)
    vv, tv = _gelu(pv)
    rv = _rms_r(vv)
    vn = (vv * rv * g).astype(BF16)
    return pu, pv, u, tu, vv, tv, rv, vn


def _tril():
    r = lax.broadcasted_iota(jnp.int32, (CHUNK, CHUNK), 0)
    c = lax.broadcasted_iota(jnp.int32, (CHUNK, CHUNK), 1)
    return r >= c


def _sgu_fwd(proj_a, g_sgu, w_s, b_st, tm):
    T = proj_a.shape[0]

    def body(p_ref, g_ref, ws_ref, bs_ref, y_ref):
        tril = _tril()
        _, _, u, _, _, _, _, vn = _sgu_parts(p_ref[...].astype(F32), g_ref[...])
        for gi in range(A_GROUPS):
            wm = jnp.where(tril, ws_ref[gi], 0.0).astype(BF16)
            bcol = bs_ref[:, gi:gi + 1]
            cs = slice(gi * CHUNK, (gi + 1) * CHUNK)
            for c in range(tm // CHUNK):
                rs = slice(c * CHUNK, (c + 1) * CHUNK)
                s = _dot(wm, vn[rs, cs]) + bcol
                y_ref[rs, cs] = (u[rs, cs] * s).astype(BF16)

    return pl.pallas_call(
        body, name="sgu_fwd", grid=(T // tm,),
        in_specs=[_row(tm, A_DIM), _full(g_sgu.shape), _full(w_s.shape), _full(b_st.shape)],
        out_specs=_row(tm, A_WIDTH), out_shape=_sds((T, A_WIDTH), BF16),
        compiler_params=_cp(("arbitrary",)),
    )(*_hbm(proj_a, g_sgu, w_s, b_st))


HEAD_ROWS = N_HEADS * CHUNK


def _head_rows(h):
    return slice(h * CHUNK, (h + 1) * CHUNK)


def _attn_setup(bias_scr, sink_scr, kvar_scr, qkv_ref, bk_ref, rel_ref, sink_ref):
    @pl.when(pl.program_id(0) == 0)
    def _():
        bk = bk_ref[...]
        for h in range(N_HEADS):
            acc = jnp.full((CHUNK, 2 * CHUNK), NEG_INF, F32)
            for b in range(N_BUCKETS):
                acc = jnp.where(bk == b, rel_ref[b, h], acc)
            bias_scr[_head_rows(h), :] = acc
            sink_scr[_head_rows(h), :] = jnp.full((CHUNK, LANES), sink_ref[0, h], F32)

    seq = qkv_ref.shape[0]
    rows_per = 2 * CHUNK
    for is_v in range(2):
        c0 = Q_DIM + is_v * KV_DIM
        for r in range(seq // rows_per):
            rs = slice(r * rows_per, (r + 1) * rows_per)
            a = qkv_ref[rs, c0:c0 + KV_DIM].astype(F32)
            lane = lax.broadcasted_iota(jnp.int32, a.shape, 1)
            lo = jnp.where(lane < HEAD_DIM, a, 0.0)
            hi = jnp.where(lane >= HEAD_DIM, a, 0.0)
            kvar_scr[4 * is_v + 0, rs, :] = lo.astype(BF16)
            kvar_scr[4 * is_v + 1, rs, :] = pltpu.roll(lo, HEAD_DIM, 1).astype(BF16)
            kvar_scr[4 * is_v + 2, rs, :] = pltpu.roll(hi, HEAD_DIM, 1).astype(BF16)
            kvar_scr[4 * is_v + 3, rs, :] = hi.astype(BF16)


def _rowsum(a, ones):
    hi = a.astype(BF16)
    lo = (a - hi.astype(F32)).astype(BF16)
    return _dot(hi, ones) + _dot(lo, ones)


def _both(a):
    return jnp.concatenate([a, a], axis=1)


def _attn_probs(qkv_ref, r0, n, kv, bias_scr, sink_scr, ones):
    s = jnp.concatenate([_dot_nt(qkv_ref[pl.ds(r0, CHUNK), (h // 2) * LANES:(h // 2 + 1) * LANES], kv[h // 4][h % 2])
                         for h in range(N_HEADS)], axis=0)
    s = s * (HEAD_DIM ** -0.5) + bias_scr[...]
    col = lax.broadcasted_iota(jnp.int32, s.shape, 1)
    s = jnp.where((col < CHUNK) & (n == 0), NEG_INF, s)
    sink = sink_scr[...]
    m = jnp.maximum(jnp.max(s, axis=-1, keepdims=True), sink)
    p = jnp.exp(s - _both(m))
    es = jnp.exp(sink - m)
    inv = 1.0 / (_dot(p.astype(BF16), ones) + es)
    return p * _both(inv), es * inv


def _attn_block_inputs(kvar_scr, n):
    r0 = pl.multiple_of(n * CHUNK, CHUNK)
    rp = pl.multiple_of(jnp.maximum(n - 1, 0) * CHUNK, CHUNK)

    def both(idx):
        return jnp.concatenate([kvar_scr[idx, pl.ds(rp, CHUNK), :], kvar_scr[idx, pl.ds(r0, CHUNK), :]], axis=0)

    kv = ((both(0), both(1)), (both(2), both(3)))
    vv = ((both(4), both(5)), (both(6), both(7)))
    return r0, kv, vv


def _attn_fwd(proj_b, sinks, rel_bias, n_seq, seq):
    nb = seq // CHUNK
    bk = jnp.asarray(_band_buckets())

    def body(qkv_ref, bk_ref, rel_ref, sink_ref, o_ref, bias_scr, sink_scr, kvar_scr):
        _attn_setup(bias_scr, sink_scr, kvar_scr, qkv_ref, bk_ref, rel_ref, sink_ref)
        ones = jnp.ones((2 * CHUNK, LANES), BF16)

        def blk(n, carry):
            r0, kv, vv = _attn_block_inputs(kvar_scr, n)
            prob, _ = _attn_probs(qkv_ref, r0, n, kv, bias_scr, sink_scr, ones)
            pb = prob.astype(BF16)
            for pr in range(N_HEADS // 2):
                acc = _dot(pb[_head_rows(2 * pr)], vv[pr // 2][0]) + _dot(pb[_head_rows(2 * pr + 1)], vv[pr // 2][1])
                o_ref[pl.ds(r0, CHUNK), pr * LANES:(pr + 1) * LANES] = acc.astype(BF16)
            return carry

        lax.fori_loop(0, nb, blk, 0)

    smem = pl.BlockSpec(memory_space=pltpu.SMEM)
    return pl.pallas_call(
        body, name="attn_fwd", grid=(n_seq,),
        in_specs=[_row(seq, B_DIM), _full(bk.shape), smem, smem],
        out_specs=_row(seq, Q_DIM), out_shape=_sds((n_seq * seq, Q_DIM), BF16),
        scratch_shapes=[pltpu.VMEM((HEAD_ROWS, 2 * CHUNK), F32), pltpu.VMEM((HEAD_ROWS, LANES), F32),
                        pltpu.VMEM((8, seq, KV_DIM), BF16)],
        compiler_params=_cp(("arbitrary",), 40),
    )(*_hbm(proj_b, bk), rel_bias, sinks)


def _dot_stacked(a, w_ref):
    return jnp.concatenate([_dot(a, w_ref[i]) for i in range(N_CHIPS)], axis=1)


def _dot_nt_stacked(a, w_ref):
    w = w_ref.shape[2]
    acc = _dot_nt(a[:, :w], w_ref[0])
    for i in range(1, N_CHIPS):
        acc = acc + _dot_nt(a[:, i * w:(i + 1) * w], w_ref[i])
    return acc


def _merge_fwd(x2, y_a, y_b, proj_g, w_pa, w_pb, w_out, tm):
    T = x2.shape[0]

    def body(x_ref, ya_ref, yb_ref, g_ref, wpa_ref, wpb_ref, wo_ref, x1_ref, mg_ref):
        g = g_ref[...].astype(F32)
        pa = _dot_stacked(ya_ref[...], wpa_ref)
        pb = _dot_stacked(yb_ref[...], wpb_ref)
        merged = (_sigmoid(g[:, :D_MODEL]) * pa + _sigmoid(g[:, D_MODEL:]) * pb).astype(BF16)
        mg_ref[...] = merged
        x1_ref[...] = x_ref[...] + _dot(merged, wo_ref[...])

    return pl.pallas_call(
        body, name="merge_fwd", grid=(T // tm,),
        in_specs=[_row(tm, D_MODEL), _row(tm, A_WIDTH), _row(tm, Q_DIM), _row(tm, G_DIM),
                  _resident(w_pa.shape), _resident(w_pb.shape), _resident(w_out.shape)],
        out_specs=[_row(tm, D_MODEL), _row(tm, D_MODEL)],
        out_shape=[_sds((T, D_MODEL), F32), _sds((T, D_MODEL), BF16)],
        compiler_params=_cp(("arbitrary",), 40),
    )(*_hbm(x2, y_a, y_b, proj_g, w_pa, w_pb, w_out))


def _upproj(x1, g_ffn, w_up, w_conv, b_conv, tm, seq):
    T = x1.shape[0]
    cw = w_up.shape[2]
    tiles_per_seq = seq // tm

    def body(x_ref, g_ref, w_ref, wc_ref, bc_ref, u_ref, h_ref, gate_ref, val_ref, tail_scr):
        at_start = (pl.program_id(0) % tiles_per_seq) == 0
        x = x_ref[...]
        h = (x * _rms_r(x) * g_ref[...]).astype(BF16)
        h_ref[...] = h
        for i in range(N_CHIPS):
            cs = slice(i * cw, (i + 1) * cw)
            u = _dot(h, w_ref[i])
            u_ref[:, cs] = u.astype(BF16)
            hl = jnp.where(at_start, 0.0, tail_scr[SUBLANES - 2:SUBLANES, cs])
            tail_scr[:, cs] = u[tm - SUBLANES:]
            up = _conv_out((u, _shift_down(u, hl, 1), _shift_down(u, hl, 2)), wc_ref[:, cs], bc_ref[:, cs])
            out_ref = gate_ref if i < N_CHIPS // 2 else val_ref
            out_ref[:, (i % 2) * cw:(i % 2 + 1) * cw] = up.astype(BF16)

    return pl.pallas_call(
        body, name="upproj", grid=(T // tm,),
        in_specs=[_row(tm, D_MODEL), _full(g_ffn.shape), _resident(w_up.shape), _full(w_conv.shape), _full(b_conv.shape)],
        out_specs=[_row(tm, 2 * D_FF), _row(tm, D_MODEL), _row(tm, D_FF), _row(tm, D_FF)],
        out_shape=[_sds((T, 2 * D_FF), BF16), _sds((T, D_MODEL), BF16), _sds((T, D_FF), BF16), _sds((T, D_FF), BF16)],
        scratch_shapes=[pltpu.VMEM((SUBLANES, 2 * D_FF), F32)],
        compiler_params=_cp(("arbitrary",), 56),
    )(*_hbm(x1, g_ffn, w_up, w_conv, b_conv))


def _shift_down(u, halo, k):
    rolled = pltpu.roll(u, k, 0)
    head = rolled[:SUBLANES]
    row = lax.broadcasted_iota(jnp.int32, head.shape, 0)
    if k == 1:
        head = jnp.where(row == 0, halo[1:2], head)
    else:
        head = jnp.where(row == 0, halo[0:1], jnp.where(row == 1, halo[1:2], head))
    return jnp.concatenate([head, rolled[SUBLANES:]], axis=0)


def _shift_up(d, halo, k):
    tm = d.shape[0]
    rolled = pltpu.roll(d, tm - k, 0)
    tail = rolled[tm - SUBLANES:]
    row = lax.broadcasted_iota(jnp.int32, tail.shape, 0)
    if k == 1:
        tail = jnp.where(row == SUBLANES - 1, halo[0:1], tail)
    else:
        tail = jnp.where(row == SUBLANES - 2, halo[0:1], jnp.where(row == SUBLANES - 1, halo[1:2], tail))
    return jnp.concatenate([rolled[:tm - SUBLANES], tail], axis=0)


def _conv_out(taps, wc, bc):
    u, u1, u2 = taps
    return wc[0:1] * u2 + wc[1:2] * u1 + wc[2:3] * u + bc


def _ffn_down_loss(gate, val, x1, target, w_down, g_final, tm):
    T = x1.shape[0]
    half = D_FF // 2

    def body(gt_ref, vl_ref, x1_ref, t_ref, wd_ref, g_ref, dx2_ref, loss_ref, gg_ref):
        i = pl.program_id(0)
        acc = jnp.zeros((tm, D_MODEL), F32)
        for j in range(2):
            gc = slice(j * half, (j + 1) * half)
            gate = gt_ref[:, gc].astype(F32)
            act = (gate * _sigmoid(gate) * vl_ref[:, gc].astype(F32)).astype(BF16)
            acc = acc + _dot(act, wd_ref[gc, :])
        x2 = x1_ref[...] + acc
        r = _rms_r(x2)
        n = x2 * r
        g = g_ref[...]
        diff = n * g - t_ref[...]
        dy = diff * (1.0 / D_MODEL)
        dx2_ref[...] = _rms_bwd(dy, n, r, g)

        @pl.when(i == 0)
        def _():
            loss_ref[...] = jnp.zeros_like(loss_ref)
            gg_ref[...] = jnp.zeros_like(gg_ref)

        loss_ref[...] += 0.5 * jnp.sum(jnp.mean(diff * diff, axis=-1, keepdims=True), axis=0, keepdims=True)
        gg_ref[...] += jnp.sum(dy * n, axis=0, keepdims=True)

    return pl.pallas_call(
        body, name="ffn_down_loss", grid=(T // tm,),
        in_specs=[_row(tm, D_FF), _row(tm, D_FF), _row(tm, D_MODEL), _row(tm, D_MODEL),
                  _resident(w_down.shape), _full(g_final.shape)],
        out_specs=[_row(tm, D_MODEL), _full((1, 1)), _full((1, D_MODEL))],
        out_shape=[_sds((T, D_MODEL), F32), _sds((1, 1), F32), _sds((1, D_MODEL), F32)],
        compiler_params=_cp(("arbitrary",), 48),
    )(*_hbm(gate, val, x1, target, w_down, g_final))


def _ffn_bwd_act(gate, val, dx2, w_down, tm):
    T = dx2.shape[0]
    half = D_FF // 2
    nt = T // tm

    def body(g_ref, v_ref, dx_ref, wd_ref, dg_ref, dv_ref, gwd_out, gbg_ref, gbv_ref, gwd_ref):
        i = pl.program_id(1)

        @pl.when(i == 0)
        def _():
            for r in (gwd_ref, gbg_ref, gbv_ref):
                r[...] = jnp.zeros_like(r)

        dx = dx_ref[...].astype(BF16)
        for c0 in range(0, half, COL_CHUNK):
            cs = slice(c0, min(c0 + COL_CHUNK, half))
            gate = g_ref[:, cs].astype(F32)
            val = v_ref[:, cs].astype(F32)
            sg = _sigmoid(gate)
            silu = gate * sg
            d_act = _dot_nt(dx, wd_ref[cs, :])
            d_val = d_act * silu
            d_gate = d_act * val * (sg * (1.0 + gate * (1.0 - sg)))
            dg_ref[:, cs] = d_gate.astype(BF16)
            dv_ref[:, cs] = d_val.astype(BF16)
            gwd_ref[cs, :] += _dot_tn((silu * val).astype(BF16), dx)
            gbg_ref[:, cs] += jnp.sum(d_gate, axis=0, keepdims=True)
            gbv_ref[:, cs] += jnp.sum(d_val, axis=0, keepdims=True)

        @pl.when(i == nt - 1)
        def _():
            gwd_out[...] = gwd_ref[...].astype(BF16)

    tile = pl.BlockSpec((tm, half), lambda j, i: (i, j))
    vec = pl.BlockSpec((1, half), lambda j, i: (0, j))
    wrows = pl.BlockSpec((half, D_MODEL), lambda j, i: (j, 0))
    return pl.pallas_call(
        body, name="ffn_bwd_act", grid=(2, nt),
        in_specs=[tile, tile, pl.BlockSpec((tm, D_MODEL), lambda j, i: (i, 0)), wrows],
        out_specs=[tile, tile, wrows, vec, vec],
        out_shape=[_sds((T, D_FF), BF16), _sds((T, D_FF), BF16), _sds((D_FF, D_MODEL), BF16),
                   _sds((1, D_FF), F32), _sds((1, D_FF), F32)],
        scratch_shapes=[pltpu.VMEM((half, D_MODEL), F32)],
        compiler_params=_cp(("arbitrary", "arbitrary"), 56),
    )(*_hbm(gate, val, dx2, w_down))


def _ffn_bwd_up(d_gate, d_val, upre, dx2, x1, g_ffn, w_conv, w_up, tm, seq):
    T = dx2.shape[0]
    tiles_per_seq = seq // tm
    k16 = tm // BF16_ROWS
    n16 = T // BF16_ROWS
    cw = D_FF // 2

    def body(dg_ref, dv_ref, hg_ref, hv_ref, u_ref, dx2_ref, x1_ref, g_ref, wc_ref, wu_ref, du_ref, dx1_ref, gg_ref, gwc_ref):
        i = pl.program_id(0)
        at_end = (i % tiles_per_seq) == tiles_per_seq - 1

        @pl.when(i == 0)
        def _():
            gg_ref[...] = jnp.zeros_like(gg_ref)
            gwc_ref[...] = jnp.zeros_like(gwc_ref)

        dh = jnp.zeros((tm, D_MODEL), F32)
        for j in range(4):
            src, hsrc = (dg_ref, hg_ref) if j < 2 else (dv_ref, hv_ref)
            ls = slice((j % 2) * cw, (j % 2 + 1) * cw)
            cs = slice(j * cw, (j + 1) * cw)
            d = src[:, ls].astype(F32)
            hl = hsrc[:, ls].astype(F32)[0:2]
            hl = jnp.where(at_end, 0.0, hl)
            wc = wc_ref[:, cs]
            d1 = _shift_up(d, hl, 1)
            d2 = _shift_up(d, hl, 2)
            du = (wc[2:3] * d + wc[1:2] * d1 + wc[0:1] * d2).astype(BF16)
            du_ref[:, cs] = du
            dh = dh + _dot_nt(du, wu_ref[j])
            u = u_ref[:, cs].astype(F32)
            gwc_ref[0:1, cs] += jnp.sum(d2 * u, axis=0, keepdims=True)
            gwc_ref[1:2, cs] += jnp.sum(d1 * u, axis=0, keepdims=True)
            gwc_ref[2:3, cs] += jnp.sum(d * u, axis=0, keepdims=True)
        x = x1_ref[...]
        r = _rms_r(x)
        n = x * r
        dx1_ref[...] = dx2_ref[...] + _rms_bwd(dh, n, r, g_ref[...])
        gg_ref[...] += jnp.sum(dh * n, axis=0, keepdims=True)

    nxt = pl.BlockSpec((BF16_ROWS, D_FF), lambda i: (jnp.minimum((i + 1) * k16, n16 - 1), 0))
    return pl.pallas_call(
        body, name="ffn_bwd_up", grid=(T // tm,),
        in_specs=[_row(tm, D_FF), _row(tm, D_FF), nxt, nxt, _row(tm, 2 * D_FF), _row(tm, D_MODEL), _row(tm, D_MODEL),
                  _full(g_ffn.shape), _full(w_conv.shape), _resident(w_up.shape)],
        out_specs=[_row(tm, 2 * D_FF), _row(tm, D_MODEL), _full((1, D_MODEL)), _full((3, 2 * D_FF))],
        out_shape=[_sds((T, 2 * D_FF), BF16), _sds((T, D_MODEL), F32), _sds((1, D_MODEL), F32), _sds((3, 2 * D_FF), F32)],
        compiler_params=_cp(("arbitrary",), 56),
    )(*_hbm(d_gate, d_val, d_gate, d_val, upre, dx2, x1, g_ffn, w_conv, w_up))


def _matmul_tn(a, b, tn, tk, name):
    T, M = a.shape
    N = b.shape[1]
    nk = T // tk

    def body(a_ref, b_ref, o_ref, acc_ref):
        k = pl.program_id(1)

        @pl.when(k == 0)
        def _():
            acc_ref[...] = jnp.zeros_like(acc_ref)

        acc_ref[...] += _dot_tn(a_ref[...], b_ref[...])

        @pl.when(k == nk - 1)
        def _():
            o_ref[...] = acc_ref[...].astype(BF16)

    return pl.pallas_call(
        body, name=name, grid=(N // tn, nk),
        in_specs=[pl.BlockSpec((tk, M), lambda j, k: (k, 0)), pl.BlockSpec((tk, tn), lambda j, k: (k, j))],
        out_specs=pl.BlockSpec((M, tn), lambda j, k: (0, j)), out_shape=_sds((M, N), BF16),
        scratch_shapes=[pltpu.VMEM((M, tn), F32)],
        compiler_params=_cp(("arbitrary", "arbitrary"), 48),
    )(*_hbm(a, b))


def _merge_bwd(dx1, merged, y_a, y_b, proj_g, w_pa, w_pb, w_out, tm, after=None):
    T = dx1.shape[0]

    nt = T // tm
    pshape = (A_WIDTH, D_MODEL)
    order = [] if after is None else [after]

    def body(*refs):
        dx_ref, mg_ref, ya_ref, yb_ref, g_ref, wpa_ref, wpb_ref, wo_ref = refs[:8]
        dg_ref, dya_ref, dyb_ref, gwo_out, gwpa_out, gwpb_out, gwo_ref, gwpa_ref, gwpb_ref = refs[8 + len(order):]
        i = pl.program_id(0)
        dx = dx_ref[...].astype(BF16)
        dm = _dot_nt(dx, wo_ref[...])
        g = g_ref[...].astype(F32)
        ya = ya_ref[...]
        yb = yb_ref[...]
        pa = _dot_stacked(ya, wpa_ref)
        pb = _dot_stacked(yb, wpb_ref)
        sa = _sigmoid(g[:, :D_MODEL])
        sb = _sigmoid(g[:, D_MODEL:])
        dpa = (dm * sa).astype(BF16)
        dpb = (dm * sb).astype(BF16)
        dg_ref[:, :D_MODEL] = (dm * pa * (sa * (1.0 - sa))).astype(BF16)
        dg_ref[:, D_MODEL:] = (dm * pb * (sb * (1.0 - sb))).astype(BF16)
        dya_ref[...] = _dot_nt_stacked(dpa, wpa_ref).astype(BF16)
        dyb_ref[...] = _dot_nt_stacked(dpb, wpb_ref).astype(BF16)

        @pl.when(i == 0)
        def _():
            for r in (gwo_ref, gwpa_ref, gwpb_ref):
                r[...] = jnp.zeros_like(r)

        gwo_ref[...] += _dot_tn(mg_ref[...], dx)
        gwpa_ref[...] += _dot_tn(ya, dpa)
        gwpb_ref[...] += _dot_tn(yb, dpb)

        @pl.when(i == nt - 1)
        def _():
            gwo_out[...] = gwo_ref[...].astype(BF16)
            gwpa_out[...] = gwpa_ref[...].astype(BF16)
            gwpb_out[...] = gwpb_ref[...].astype(BF16)

    return pl.pallas_call(
        body, name="merge_bwd", grid=(nt,),
        in_specs=[_row(tm, D_MODEL), _row(tm, D_MODEL), _row(tm, A_WIDTH), _row(tm, Q_DIM), _row(tm, G_DIM),
                  _resident(w_pa.shape), _resident(w_pb.shape), _resident(w_out.shape)] + [ANY] * len(order),
        out_specs=[_row(tm, G_DIM), _row(tm, A_WIDTH), _row(tm, Q_DIM),
                   _full(w_out.shape), _full(pshape), _full(pshape)],
        out_shape=[_sds((T, G_DIM), BF16), _sds((T, A_WIDTH), BF16), _sds((T, Q_DIM), BF16),
                   _sds(w_out.shape, BF16), _sds(pshape, BF16), _sds(pshape, BF16)],
        scratch_shapes=[pltpu.VMEM(w_out.shape, F32), pltpu.VMEM(pshape, F32), pltpu.VMEM(pshape, F32)],
        compiler_params=_cp(("arbitrary",), 56),
    )(*_hbm(dx1, merged, y_a, y_b, proj_g, w_pa, w_pb, w_out), *order)


def _sgu_bwd(proj_a, d_ya, g_sgu, w_s, b_st, tm, after=None):
    T = proj_a.shape[0]
    order = [] if after is None else [after]

    def body(*refs):
        p_ref, dy_ref, g_ref, ws_ref, bs_ref = refs[:5]
        dp_ref, gws_ref, gbs_ref, gg_ref = refs[5 + len(order):]
        tril = _tril()
        g = g_ref[...]
        pu, pv, u, tu, vv, tv, rv, vn = _sgu_parts(p_ref[...].astype(F32), g)
        dy = dy_ref[...].astype(F32)

        @pl.when(pl.program_id(0) == 0)
        def _():
            for r in (gws_ref, gbs_ref, gg_ref):
                r[...] = jnp.zeros_like(r)

        du_cols = []
        dvn_cols = []
        for gi in range(A_GROUPS):
            wm = jnp.where(tril, ws_ref[gi], 0.0).astype(BF16)
            wmt = wm.astype(F32).T.astype(BF16)
            bcol = bs_ref[:, gi:gi + 1]
            cs = slice(gi * CHUNK, (gi + 1) * CHUNK)
            du_rows = []
            dvn_rows = []
            gw = jnp.zeros((CHUNK, CHUNK), F32)
            gb = jnp.zeros((CHUNK, 1), F32)
            for c in range(tm // CHUNK):
                rs = slice(c * CHUNK, (c + 1) * CHUNK)
                vn_c = vn[rs, cs]
                s = _dot(wm, vn_c) + bcol
                dy_c = dy[rs, cs]
                ds = dy_c * u[rs, cs]
                du_rows.append(dy_c * s)
                dsb = ds.astype(BF16)
                gw = gw + _dot_nt(dsb, vn_c)
                gb = gb + jnp.sum(ds, axis=-1, keepdims=True)
                dvn_rows.append(_dot(wmt, dsb))
            gws_ref[gi] += jnp.where(tril, gw, 0.0)
            gbs_ref[:, gi:gi + 1] += gb
            du_cols.append(jnp.concatenate(du_rows, axis=0))
            dvn_cols.append(jnp.concatenate(dvn_rows, axis=0))
        du = jnp.concatenate(du_cols, axis=1)
        dvn = jnp.concatenate(dvn_cols, axis=1)
        vhat = vv * rv
        gg_ref[...] += jnp.sum(dvn * vhat, axis=0, keepdims=True)
        dvv = _rms_bwd(dvn, vhat, rv, g)
        dp_ref[:, :A_WIDTH] = (du * _gelu_grad(pu, tu)).astype(BF16)
        dp_ref[:, A_WIDTH:] = (dvv * _gelu_grad(pv, tv)).astype(BF16)

    return pl.pallas_call(
        body, name="sgu_bwd", grid=(T // tm,),
        in_specs=[_row(tm, A_DIM), _row(tm, A_WIDTH), _full(g_sgu.shape), _full(w_s.shape), _full(b_st.shape)] + [ANY] * len(order),
        out_specs=[_row(tm, A_DIM), _full(w_s.shape), _full(b_st.shape), _full(g_sgu.shape)],
        out_shape=[_sds((T, A_DIM), BF16), _sds(w_s.shape, F32), _sds(b_st.shape, F32), _sds(g_sgu.shape, F32)],
        compiler_params=_cp(("arbitrary",)),
    )(*_hbm(proj_a, d_ya, g_sgu, w_s, b_st), *order)


def _attn_bwd(proj_b, d_yb, sinks, rel_bias, n_seq, seq):
    nb = seq // CHUNK
    bk = jnp.asarray(_band_buckets())

    def body(qkv_ref, do_ref, bk_ref, rel_ref, sink_ref, d_ref, gs_ref, gr_ref,
             bias_scr, sink_scr, kvar_scr, dbias_scr, dk_scr, dv_scr, ds_scr):
        b = pl.program_id(0)
        _attn_setup(bias_scr, sink_scr, kvar_scr, qkv_ref, bk_ref, rel_ref, sink_ref)
        ones = jnp.ones((2 * CHUNK, LANES), BF16)

        @pl.when(b == 0)
        def _():
            dbias_scr[...] = jnp.zeros_like(dbias_scr)
            ds_scr[...] = jnp.zeros_like(ds_scr)

        dk_scr[...] = jnp.zeros_like(dk_scr)
        dv_scr[...] = jnp.zeros_like(dv_scr)

        def transposed(a):
            return a.astype(F32).T.astype(BF16)

        def blk(n, carry):
            r0, kv, vv = _attn_block_inputs(kvar_scr, n)
            prob, psink = _attn_probs(qkv_ref, r0, n, kv, bias_scr, sink_scr, ones)
            dp = jnp.concatenate([_dot_nt(do_ref[pl.ds(r0, CHUNK), (h // 2) * LANES:(h // 2 + 1) * LANES], vv[h // 4][h % 2])
                                  for h in range(N_HEADS)], axis=0)
            delta = _rowsum(prob * dp, ones)
            dsc = prob * (dp - _both(delta))
            ds_scr[...] += psink * delta
            dbias_scr[...] += dsc
            dsb = (dsc * (HEAD_DIM ** -0.5)).astype(BF16)
            pb = prob.astype(BF16)
            dkt = [jnp.zeros((HEAD_DIM, 2 * CHUNK), F32) for _ in range(2)]
            dvt = [jnp.zeros((HEAD_DIM, 2 * CHUNK), F32) for _ in range(2)]
            for pr in range(N_HEADS // 2):
                ps = slice(pr * LANES, (pr + 1) * LANES)
                qpt = transposed(qkv_ref[pl.ds(r0, CHUNK), ps])
                dopt = transposed(do_ref[pl.ds(r0, CHUNK), ps])
                kvh = pr // 2
                dq = jnp.zeros((CHUNK, LANES), F32)
                for hh in range(2):
                    hr = _head_rows(2 * pr + hh)
                    rows = slice(hh * HEAD_DIM, (hh + 1) * HEAD_DIM)
                    dq = dq + _dot(dsb[hr], kv[kvh][hh])
                    dkt[kvh] = dkt[kvh] + _dot(qpt, dsb[hr])[rows]
                    dvt[kvh] = dvt[kvh] + _dot(dopt, pb[hr])[rows]
                d_ref[pl.ds(r0, CHUNK), ps] = dq.astype(BF16)
            dk_scr[:, pl.ds(r0, 2 * CHUNK)] += jnp.concatenate(dkt, axis=0)
            dv_scr[:, pl.ds(r0, 2 * CHUNK)] += jnp.concatenate(dvt, axis=0)
            return carry

        lax.fori_loop(0, nb, blk, 0)
        for n in range(nb):
            rows = slice(n * CHUNK, (n + 1) * CHUNK)
            cols = slice((n + 1) * CHUNK, (n + 2) * CHUNK)
            d_ref[rows, Q_DIM:Q_DIM + KV_DIM] = dk_scr[:, cols].T.astype(BF16)
            d_ref[rows, Q_DIM + KV_DIM:] = dv_scr[:, cols].T.astype(BF16)

        @pl.when(b == n_seq - 1)
        def _():
            bkv = bk_ref[...]
            for h in range(N_HEADS):
                gs_ref[0:1, h:h + 1] = -jnp.sum(ds_scr[_head_rows(h), 0:1], axis=0, keepdims=True)
                db = dbias_scr[_head_rows(h), :]
                for bb in range(N_BUCKETS):
                    part = jnp.sum(jnp.where(bkv == bb, db, 0.0), axis=-1, keepdims=True)
                    gr_ref[bb:bb + 1, h:h + 1] = jnp.sum(part, axis=0, keepdims=True)

    smem = pl.BlockSpec(memory_space=pltpu.SMEM)
    return pl.pallas_call(
        body, name="attn_bwd", grid=(n_seq,),
        in_specs=[_row(seq, B_DIM), _row(seq, Q_DIM), _full(bk.shape), smem, smem],
        out_specs=[_row(seq, B_DIM), _full((1, N_HEADS)), _full((N_BUCKETS, N_HEADS))],
        out_shape=[_sds((n_seq * seq, B_DIM), BF16), _sds((1, N_HEADS), F32), _sds((N_BUCKETS, N_HEADS), F32)],
        scratch_shapes=[pltpu.VMEM((HEAD_ROWS, 2 * CHUNK), F32), pltpu.VMEM((HEAD_ROWS, LANES), F32),
                        pltpu.VMEM((8, seq, KV_DIM), BF16), pltpu.VMEM((HEAD_ROWS, 2 * CHUNK), F32),
                        pltpu.VMEM((KV_DIM, seq + CHUNK), F32), pltpu.VMEM((KV_DIM, seq + CHUNK), F32),
                        pltpu.VMEM((HEAD_ROWS, LANES), F32)],
        compiler_params=_cp(("arbitrary",), 40),
    )(*_hbm(proj_b, d_yb, bk), rel_bias, sinks)


def _inproj_bwd(d_g, d_a, d_b, x2, dx1, g_mix, w_in, tm, after=None):
    T = x2.shape[0]
    order = [] if after is None else [after]

    def body(*refs):
        dg_ref, da_ref, db_ref, x_ref, dx1_ref, g_ref, w_ref = refs[:7]
        gx_ref, gg_ref = refs[7 + len(order):]
        dh = (_dot_nt(dg_ref[...], w_ref[:, _G_COLS]) + _dot_nt(da_ref[...], w_ref[:, _A_COLS])
              + _dot_nt(db_ref[...], w_ref[:, _B_COLS]))
        x = x_ref[...]
        r = _rms_r(x)
        n = x * r
        gx_ref[...] = dx1_ref[...] + _rms_bwd(dh, n, r, g_ref[...])

        @pl.when(pl.program_id(0) == 0)
        def _():
            gg_ref[...] = jnp.zeros_like(gg_ref)

        gg_ref[...] += jnp.sum(dh * n, axis=0, keepdims=True)

    return pl.pallas_call(
        body, name="inproj_bwd", grid=(T // tm,),
        in_specs=[_row(tm, G_DIM), _row(tm, A_DIM), _row(tm, B_DIM), _row(tm, D_MODEL), _row(tm, D_MODEL),
                  _full(g_mix.shape), _resident(w_in.shape)] + [ANY] * len(order),
        out_specs=[_row(tm, D_MODEL), _full((1, D_MODEL))],
        out_shape=[_sds((T, D_MODEL), F32), _sds((1, D_MODEL), F32)],
        compiler_params=_cp(("arbitrary",), 48),
    )(*_hbm(d_g, d_a, d_b, x2, dx1, g_mix, w_in), *order)


IN_SHARD = (A_DIM + B_DIM + G_DIM) // N_CHIPS


def _unstack_w_in(stack):
    tr = 256

    def body(s_ref, o_ref):
        for i in range(N_CHIPS):
            o_ref[:, i * IN_SHARD:(i + 1) * IN_SHARD] = s_ref[i]

    return pl.pallas_call(
        body, name="unstack_w_in", grid=(D_MODEL // tr,),
        in_specs=[pl.BlockSpec((N_CHIPS, tr, IN_SHARD), lambda r: (0, r, 0))],
        out_specs=pl.BlockSpec((tr, N_CHIPS * IN_SHARD), lambda r: (r, 0)),
        out_shape=_sds((D_MODEL, N_CHIPS * IN_SHARD), stack.dtype),
        compiler_params=_cp(("arbitrary",)),
    )(*_hbm(stack))


def _stack_grad_w_in(gw_a, gw_b, gw_g):
    tr = 256

    def body(a_ref, b_ref, g_ref, o_ref):
        full = jnp.concatenate([a_ref[...], b_ref[...], g_ref[...]], axis=1)
        for i in range(N_CHIPS):
            o_ref[i] = full[:, i * IN_SHARD:(i + 1) * IN_SHARD]

    return pl.pallas_call(
        body, name="stack_grad_w_in", grid=(D_MODEL // tr,),
        in_specs=[_row(tr, A_DIM), _row(tr, B_DIM), _row(tr, G_DIM)],
        out_specs=pl.BlockSpec((N_CHIPS, tr, IN_SHARD), lambda r: (0, r, 0)),
        out_shape=_sds((N_CHIPS, D_MODEL, IN_SHARD), gw_a.dtype),
        compiler_params=_cp(("arbitrary",)),
    )(*_hbm(gw_a, gw_b, gw_g))


def _local_step(x, target, g_mix, g_sgu, w_s, b_s, sinks, rel_bias, g_ffn, b_conv, g_final,
                w_in, w_conv, proj_weights, ffn_weights, on_grads, after=None):
    n_seq, seq, _ = x.shape
    T = n_seq * seq
    tm = min(ROW_TILE, seq)
    tw = min(GRAD_ROW_TILE, T)
    tf = min(WIDE_ROW_TILE, seq)
    x2 = x.reshape(T, D_MODEL)
    tgt = target.reshape(T, D_MODEL)
    b_st = b_s.T
    g_fin = g_final.reshape(1, D_MODEL)

    proj_g, proj_a, proj_b, h = _inproj(x2, g_mix, w_in, tm, after)
    y_a = _sgu_fwd(proj_a, g_sgu, w_s, b_st, tm)
    y_b = _attn_fwd(proj_b, sinks, rel_bias, n_seq, seq)
    w_pa, w_pb, w_out = proj_weights(y_b)
    x1, merged = _merge_fwd(x2, y_a, y_b, proj_g, w_pa, w_pb, w_out, tm)
    w_up, w_down = ffn_weights(x1)
    upre, h2, gate, val = _upproj(x1, g_ffn, w_up, w_conv, b_conv, tf, seq)
    dx2, loss, gg_final = _ffn_down_loss(gate, val, x1, tgt, w_down, g_fin, tm)

    d_gate, d_val, gw_down, gb_g, gb_v = _ffn_bwd_act(gate, val, dx2, w_down, tw)
    gb_conv = jnp.concatenate([gb_g, gb_v], axis=1)
    d_upre, dx1, gg_ffn, gw_conv = _ffn_bwd_up(d_gate, d_val, upre, dx2, x1, g_ffn, w_conv, w_up, tf, seq)
    gw_up = _matmul_tn(h2, d_upre, 2 * D_FF // 4, min(2 * GRAD_ROW_TILE, T), "grad_w_up")
    sent = on_grads("ffn", dict(w_up=gw_up, w_down=gw_down))
    d_g, d_ya, d_yb, gw_out, gw_pa, gw_pb = _merge_bwd(dx1, merged, y_a, y_b, proj_g, w_pa, w_pb, w_out, tw, sent)
    sent = on_grads("proj", dict(w_pa=gw_pa, w_pb=gw_pb, w_out=gw_out))
    d_a, gw_s, gb_st, gg_sgu = _sgu_bwd(proj_a, d_ya, g_sgu, w_s, b_st, tm, sent)
    d_b, g_sinks, g_rel = _attn_bwd(proj_b, _tie(d_yb, d_a), sinks, rel_bias, n_seq, seq)
    gw_g = _matmul_tn(h, _tie(d_g, d_b), D_MODEL, min(2 * GRAD_ROW_TILE, T), "grad_w_in_gate")
    gw_a = _matmul_tn(h, _tie(d_a, gw_g), A_DIM, min(2 * GRAD_ROW_TILE, T), "grad_w_in_a")
    gw_b = _matmul_tn(h, _tie(d_b, gw_a), B_DIM, min(2 * GRAD_ROW_TILE, T), "grad_w_in_b")
    gw_in = _stack_grad_w_in(gw_a, gw_b, gw_g)
    sent = on_grads("in", dict(w_in=gw_in))
    grad_x, gg_mix = _inproj_bwd(d_g, d_a, d_b, x2, dx1, g_mix, w_in, tm, sent)

    small = dict(g_mix=gg_mix, g_sgu=gg_sgu, w_s=gw_s, b_s=gb_st.T, sinks=g_sinks, rel_bias=g_rel,
                 g_ffn=gg_ffn, b_conv=gb_conv, g_final=gg_final, w_conv=gw_conv)
    big = dict(w_in=gw_in, w_pa=gw_pa, w_pb=gw_pb, w_out=gw_out, w_up=gw_up, w_down=gw_down)
    return loss, grad_x.reshape(x.shape), small, big


_MIXER = ("w_in", "w_pa", "w_pb", "w_out")
_FFN = ("w_up", "w_down")
_BIG = _MIXER + _FFN

CONV_ROWS = 6
_SMALL_AT = dict(loss=(0, 1, 1), g_final=(1, 1, D_MODEL), g_mix=(2, 1, D_MODEL), g_ffn=(3, 1, D_MODEL), g_sgu=(4, 1, A_WIDTH),
                 sinks=(5, 1, N_HEADS), rel_bias=(6, 1, N_BUCKETS * N_HEADS), b_s=(8, A_GROUPS, CHUNK),
                 b_conv=(12, CONV_ROWS, D_MODEL), w_conv=(18, 3 * CONV_ROWS, D_MODEL), w_s=(40, A_GROUPS * CHUNK * CHUNK // D_MODEL, D_MODEL))
_SMALL_IN_CALL = ("g_final", "g_mix", "g_ffn", "g_sgu", "sinks", "b_s")
SMALL_ROWS = 104


def _pack_small(vals):
    def wide(a):
        return jnp.pad(a, ((0, 0), (0, CONV_ROWS * D_MODEL - a.shape[1]))).reshape(-1, D_MODEL)

    laid = dict(vals, b_conv=wide(vals["b_conv"]), w_conv=wide(vals["w_conv"]), w_s=vals["w_s"].reshape(-1, D_MODEL))
    rows, at = [], 0
    for n, (r0, nr, nc) in _SMALL_AT.items():
        if r0 > at:
            rows.append(jnp.zeros((r0 - at, D_MODEL), F32))
        rows.append(jnp.pad(laid[n].astype(F32).reshape(nr, nc), ((0, 0), (0, D_MODEL - nc))))
        at = r0 + nr
    return jnp.concatenate(rows, axis=0)


def _unwide(a, r):
    return a.reshape(r, CONV_ROWS * D_MODEL)[:, :2 * D_FF]


def _mesh_pos():
    return lax.axis_index("x"), lax.axis_index("y"), lax.axis_index("c")


def _other_chips(x, y):
    return [(1 - x, y), (x, 1 - y), (1 - x, 1 - y)]


def _remote(src, dst, send_sem, recv_sem, to):
    return pltpu.make_async_remote_copy(src_ref=src, dst_ref=dst, send_sem=send_sem, recv_sem=recv_sem,
                                        device_id=to, device_id_type=MESH)


def _own_slot(own, n, at):
    return lax.dynamic_update_slice(lax.empty((n,) + own.shape, own.dtype), own[None], (at,) + (0,) * own.ndim)


def _allgather_weights(stacks, wc_stack):
    names = list(stacks)
    n = len(names)

    def body(*refs):
        ins, outs = refs[:n + 1], refs[n + 1:2 * n + 2]
        send_sems, recv_sems = refs[2 * n + 2:]
        x, y, c = _mesh_pos()
        _handshake(_chip_peers(x, y, c) + _sibling_peers(x, y, c))
        me = 2 * x + y
        sibling = (x, y, 1 - c)
        chips = _other_chips(x, y)

        def half(ref, chip, hc):
            hr = ref.shape[1] // 2
            return ref.at[chip, pl.ds(hc * hr, hr), :]

        first = []
        for k in range(n):
            first += [_remote(half(ins[k], me, c), half(outs[k], me, c), send_sems.at[6 * k + j], recv_sems.at[6 * k + j], (cx, cy, c))
                      for j, (cx, cy) in enumerate(chips)]
        first += [_remote(ins[n].at[me], outs[n].at[me], send_sems.at[6 * n + j], recv_sems.at[6 * n + j], (cx, cy, c))
                  for j, (cx, cy) in enumerate(chips)]
        for cp in first:
            cp.start()
        passed = []
        for k in range(n):
            for j, (cx, cy) in enumerate(chips):
                landed = half(outs[k], 2 * cx + cy, c)
                _remote(landed, landed, send_sems.at[6 * k + j], recv_sems.at[6 * k + j], (x, y, c)).wait_recv()
                passed.append(_remote(landed, landed, send_sems.at[6 * k + 3 + j], recv_sems.at[6 * k + 3 + j], sibling))
                passed[-1].start()
        for k in range(n):
            for j, (cx, cy) in enumerate(chips):
                theirs = half(outs[k], 2 * cx + cy, 1 - c)
                _remote(theirs, theirs, send_sems.at[6 * k + 3 + j], recv_sems.at[6 * k + 3 + j], (x, y, c)).wait_recv()
        for j, (cx, cy) in enumerate(chips):
            slot = outs[n].at[2 * cx + cy]
            _remote(slot, slot, send_sems.at[6 * n + j], recv_sems.at[6 * n + j], (x, y, c)).wait_recv()
        for cp in first + passed:
            cp.wait_send()

    arrays = [stacks[k] for k in names] + [wc_stack]
    outs = pl.pallas_call(
        body, name="allgather_weights",
        in_specs=[HBM] * (n + 1), out_specs=[HBM] * (n + 1), input_output_aliases={k: k for k in range(n + 1)},
        out_shape=[_sds(a.shape, a.dtype) for a in arrays],
        scratch_shapes=[pltpu.SemaphoreType.DMA((6 * n + 3,)), pltpu.SemaphoreType.DMA((6 * n + 3,))],
        compiler_params=pltpu.CompilerParams(collective_id=_COLLECTIVE["gather_in"]),
    )(*arrays)
    return dict(zip(names, outs[:n])), outs[n]


_KIND = {"w_in": "stack", "w_pa": "col", "w_pb": "col", "w_up": "col", "w_out": "row", "w_down": "row"}


def _half_view(ref, kind, h):
    if kind == "stack":
        k = ref.shape[1] // 2
        return ref.at[:, pl.ds(h * k, k), :]
    if kind == "col":
        k = ref.shape[0] // 2
        return ref.at[pl.ds(h * k, k), :]
    k = ref.shape[1] // 2
    return ref.at[:, pl.ds(h * k, k)]


def _shard_view(ref, kind, i):
    if kind == "stack":
        return ref.at[i]
    if kind == "col":
        k = ref.shape[1] // N_CHIPS
        return ref.at[:, pl.ds(i * k, k)]
    k = ref.shape[0] // N_CHIPS
    return ref.at[pl.ds(i * k, k), :]


def _region_view(ref, kind, h):
    if kind == "row":
        k = ref.shape[1] // 2
        return ref.at[:, pl.ds(h * k, k)]
    k = ref.shape[0] // 2
    return ref.at[pl.ds(h * k, k), :]


def _half_shape(shape, kind):
    if kind == "stack":
        return (shape[0], shape[1] // 2, shape[2])
    return (shape[0] // 2, shape[1]) if kind == "col" else (shape[0], shape[1] // 2)


def _part_shape(half_shape, kind):
    if kind == "stack":
        return tuple(half_shape[1:])
    k, w = half_shape
    return (k, w // N_CHIPS) if kind == "col" else (k // N_CHIPS, w)


_DATAFLOW = pltpu.SideEffectType.DATAFLOW_SIDE_EFFECTING
_TOKEN = (SUBLANES, LANES)


_COLLECTIVE = {k: i for i, k in enumerate(
    [kind + "_" + g for kind in ("pair", "chip", "share") for g in ("ffn", "proj", "in")]
    + ["gather_proj", "gather_ffn", "gather_in", "forward_proj", "forward_ffn"])}


def _sibling_peers(x, y, c):
    return [(x, y, 1 - c)]


def _chip_peers(x, y, c):
    return [(cx, cy, c) for cx, cy in _other_chips(x, y)]


def _handshake(peers):
    barrier = pltpu.get_barrier_semaphore()
    for peer in peers:
        pl.semaphore_signal(barrier, inc=1, device_id=peer, device_id_type=MESH)
    pl.semaphore_wait(barrier, len(peers))


def _split_start(name, arrays, n_sems, issue, after=None, handshake=None):
    n = len(arrays)
    order = [] if after is None else [after]

    def body(*refs):
        base = n + len(order)
        if handshake is not None:
            _handshake(handshake[1](*_mesh_pos()))
        issue(refs[:n], refs[base], refs[base + 1])
        refs[-1][...] = jnp.zeros(_TOKEN, F32)

    params = dict(has_side_effects=_DATAFLOW)
    if handshake is not None:
        params["collective_id"] = handshake[0]
    outs = pl.pallas_call(
        body, name=name,
        in_specs=[HBM] * n + [ANY] * len(order), out_specs=[SEM, SEM] + [HBM] * n + [pl.BlockSpec(memory_space=pltpu.VMEM)],
        out_shape=[pltpu.SemaphoreType.DMA((n_sems,)), pltpu.SemaphoreType.DMA((n_sems,))]
        + [pltpu.HBM(a.shape, a.dtype) for a in arrays] + [_sds(_TOKEN, F32)],
        input_output_aliases={k: 2 + k for k in range(n)},
        compiler_params=pltpu.CompilerParams(**params),
    )(*[pltpu.with_memory_space_constraint(a, pltpu.HBM) for a in arrays], *order)
    return outs[0], outs[1], list(outs[2:2 + n]), outs[-1]


def _split_wait(name, started, waits, after):
    send_sems, recv_sems, arrays, _ = started
    n = len(arrays)

    def body(*refs):
        waits(refs[:n], refs[n], refs[n + 1])

    return pl.pallas_call(
        body, name=name,
        in_specs=[HBM] * n + [SEM, SEM, ANY], out_specs=[HBM] * n,
        out_shape=[pltpu.HBM(a.shape, a.dtype) for a in arrays],
        input_output_aliases={k: k for k in range(n)},
        compiler_params=pltpu.CompilerParams(has_side_effects=_DATAFLOW),
    )(*arrays, send_sems, recv_sems, after)


def _wait_both(src, dst, send_sem, recv_sem):
    x, y, c = _mesh_pos()
    cp = _remote(src, dst, send_sem, recv_sem, (x, y, c))
    cp.wait_send()
    cp.wait_recv()


def _pair_exchange_start(parts, tag, after):
    names = list(parts)
    n = len(names)
    lands = [lax.empty(_half_shape(parts[k].shape, _KIND[k]), parts[k].dtype) for k in names]

    def issue(refs, send_sems, recv_sems):
        x, y, c = _mesh_pos()
        for hc in range(2):
            @pl.when(c == hc)
            def _():
                for k in range(n):
                    _remote(_half_view(refs[k], _KIND[names[k]], 1 - hc), refs[n + k], send_sems.at[k], recv_sems.at[k],
                            (x, y, 1 - c)).start()

    return names, _split_start("grad_pair_exchange_start_" + tag, [parts[k] for k in names] + lands, n, issue, after,
                               (_COLLECTIVE["pair_" + tag], _sibling_peers))


def _pair_exchange_wait(pending, tag, after):
    names, started = pending
    n = len(names)

    def waits(refs, send_sems, recv_sems):
        for k in range(n):
            _wait_both(_half_view(refs[k], _KIND[names[k]], 0), refs[n + k], send_sems.at[k], recv_sems.at[k])

    outs = _split_wait("grad_pair_exchange_wait_" + tag, started, waits, after)
    return dict(zip(names, outs[:n])), dict(zip(names, outs[n:]))


def _half_blocks(shape, kind):
    if kind == "stack":
        _, k, w = shape
        tr = k // 2
        nb = 1
        return (N_CHIPS, nb), (1, tr, w), (lambda i, r, s: (i, r, 0)), (lambda i, r, s: (i, s[1] * nb + r, 0))
    k, w = shape
    if kind == "col":
        tr = 256
        nb = k // 2 // tr
        return (nb,), (tr, w), (lambda r, s: (r, 0)), (lambda r, s: (s[1] * nb + r, 0))
    tr = k // N_CHIPS
    return (N_CHIPS,), (tr, w // 2), (lambda r, s: (r, 0)), (lambda r, s: (r, s[1]))


def _pair_add(part, from_sibling, name, pos):
    kind = _KIND[name]
    grid, block, half_map, full_map = _half_blocks(part.shape, kind)

    def body(s_ref, p_ref, q_ref, o_ref):
        o_ref[...] = (p_ref[...].astype(F32) + q_ref[...].astype(F32)).astype(BF16)

    return pl.pallas_call(
        body, name="grad_pair_add_" + name,
        grid_spec=pltpu.PrefetchScalarGridSpec(
            num_scalar_prefetch=1, grid=grid,
            in_specs=[pl.BlockSpec(block, full_map), pl.BlockSpec(block, half_map)],
            out_specs=pl.BlockSpec(block, half_map)),
        out_shape=_sds(from_sibling.shape, BF16),
        compiler_params=_cp(("arbitrary",) * len(grid), 40),
    )(pos, *_hbm(part, from_sibling))


def _chip_exchange_start(sums, tag, after):
    names = list(sums)
    n = len(names)
    lands = [lax.empty((3,) + _part_shape(sums[k].shape, _KIND[k]), sums[k].dtype) for k in names]

    def issue(refs, send_sems, recv_sems):
        x, y, c = _mesh_pos()
        me = 2 * x + y
        for i in range(N_CHIPS):
            xi, yi = i // 2, i % 2
            j = jnp.where(xi != x, jnp.where(yi != y, 2, 0), 1)

            @pl.when(i != me)
            def _():
                for k in range(n):
                    _remote(_shard_view(refs[k], _KIND[names[k]], i), refs[n + k].at[j], send_sems.at[3 * k + j],
                            recv_sems.at[3 * k + j], (xi, yi, c)).start()

    return names, _split_start("grad_chip_exchange_start_" + tag, [sums[k] for k in names] + lands, 3 * n, issue, after,
                               (_COLLECTIVE["chip_" + tag], _chip_peers))


def _chip_exchange_wait(pending, tag, after):
    names, started = pending
    n = len(names)

    def waits(refs, send_sems, recv_sems):
        for k in range(n):
            for j in range(3):
                _wait_both(_shard_view(refs[k], _KIND[names[k]], 0), refs[n + k].at[j], send_sems.at[3 * k + j], recv_sems.at[3 * k + j])

    return dict(zip(names, _split_wait("grad_chip_exchange_wait_" + tag, started, waits, after)[n:]))


def _allgather_start(stacks, tag, after):
    names = list(stacks)

    def issue(refs, send_sems, recv_sems):
        x, y, c = _mesh_pos()
        me = 2 * x + y
        for k, st in enumerate(refs):
            hr = st.shape[1] // 2
            mine = st.at[me, pl.ds(c * hr, hr), :]
            for j, (cx, cy) in enumerate(_other_chips(x, y)):
                _remote(mine, mine, send_sems.at[3 * k + j], recv_sems.at[3 * k + j], (cx, cy, c)).start()

    return names, _split_start("allgather_start_" + tag, [stacks[k] for k in names], 3 * len(names), issue, after,
                               (_COLLECTIVE["gather_" + tag], _chip_peers))


def _allgather_wait(pending, tag, after):
    names, started = pending

    def waits(refs, send_sems, recv_sems):
        for k, st in enumerate(refs):
            slot = st.at[0, pl.ds(0, st.shape[1] // 2), :]
            for j in range(3):
                _wait_both(slot, slot, send_sems.at[3 * k + j], recv_sems.at[3 * k + j])

    return dict(zip(names, _split_wait("allgather_wait_" + tag, started, waits, after)))


def _allgather_forward(stacks, tag):
    names = list(stacks)
    n = len(names)

    def body(*refs):
        ins, outs = refs[:n], refs[n:2 * n]
        send_sems, recv_sems = refs[2 * n:]
        x, y, c = _mesh_pos()
        _handshake(_sibling_peers(x, y, c))
        copies = []
        for k in range(n):
            hr = ins[k].shape[1] // 2
            for j, (cx, cy) in enumerate(_other_chips(x, y)):
                chip = 2 * cx + cy
                copies.append(_remote(ins[k].at[chip, pl.ds(c * hr, hr), :], outs[k].at[chip, pl.ds(c * hr, hr), :],
                                      send_sems.at[3 * k + j], recv_sems.at[3 * k + j], (x, y, 1 - c)))
        for cp in copies:
            cp.start()
        for cp in copies:
            cp.wait()

    arrays = [stacks[k] for k in names]
    outs = pl.pallas_call(
        body, name="allgather_forward_" + tag, in_specs=[HBM] * n, out_specs=[HBM] * n,
        input_output_aliases={k: k for k in range(n)},
        out_shape=[_sds(a.shape, a.dtype) for a in arrays],
        scratch_shapes=[pltpu.SemaphoreType.DMA((3 * n,)), pltpu.SemaphoreType.DMA((3 * n,))],
        compiler_params=pltpu.CompilerParams(collective_id=_COLLECTIVE["forward_" + tag]),
    )(*arrays)
    return dict(zip(names, outs))


def _owner_sum(part, from_sibling, from_chips, name, pos, shard_shape):
    kind = _KIND[name]
    _, pk, pw = from_chips.shape
    if kind == "row":
        tr, nb = pk, 1
        p_spec = pl.BlockSpec((tr, pw), lambda r, s: (s[0], s[1]))
        q_spec = pl.BlockSpec((tr, pw), lambda r, s: (s[0], 0))
        o_spec = pl.BlockSpec((tr, pw), lambda r, s: (0, s[1]))
    else:
        tr = 256
        nb = pk // tr
        if kind == "stack":
            p_spec = pl.BlockSpec((None, tr, pw), lambda r, s: (s[0], s[1] * nb + r, 0))
            q_spec = pl.BlockSpec((None, tr, pw), lambda r, s: (s[0], r, 0))
        else:
            p_spec = pl.BlockSpec((tr, pw), lambda r, s: (s[1] * nb + r, s[0]))
            q_spec = pl.BlockSpec((tr, pw), lambda r, s: (r, s[0]))
        o_spec = pl.BlockSpec((tr, pw), lambda r, s: (s[1] * nb + r, 0))

    def body(s_ref, p_ref, q_ref, r_ref, o_ref):
        acc = p_ref[...].astype(F32) + q_ref[...].astype(F32)
        for j in range(3):
            acc = acc + r_ref[j].astype(F32)
        o_ref[...] = acc

    return pl.pallas_call(
        body, name="grad_owner_sum_" + name,
        grid_spec=pltpu.PrefetchScalarGridSpec(
            num_scalar_prefetch=1, grid=(nb,),
            in_specs=[p_spec, q_spec, pl.BlockSpec((3, tr, pw), lambda r, s: (0, r, 0))],
            out_specs=o_spec),
        out_shape=_sds(shard_shape, F32),
        compiler_params=_cp(("arbitrary",), 32),
    )(pos, *_hbm(part, from_sibling, from_chips))


def _pair_share_start(shards, tag, after):
    names = list(shards)

    def issue(refs, send_sems, recv_sems):
        x, y, c = _mesh_pos()
        for hc in range(2):
            @pl.when(c == hc)
            def _():
                for k, g in enumerate(refs):
                    mine = _region_view(g, _KIND[names[k]], hc)
                    _remote(mine, mine, send_sems.at[k], recv_sems.at[k], (x, y, 1 - c)).start()

    return names, _split_start("grad_pair_share_start_" + tag, [shards[k] for k in names], len(names), issue, after,
                               (_COLLECTIVE["share_" + tag], _sibling_peers))


def _pair_share_wait(pending, tag, after):
    names, started = pending

    def waits(refs, send_sems, recv_sems):
        for k, g in enumerate(refs):
            region = _region_view(g, _KIND[names[k]], 0)
            _wait_both(region, region, send_sems.at[k], recv_sems.at[k])

    return dict(zip(names, _split_wait("grad_pair_share_wait_" + tag, started, waits, after)))


def _small_exchange_start(slots, after):
    def issue(refs, send_sems, recv_sems):
        x, y, c = _mesh_pos()
        mine = refs[0].at[4 * x + 2 * y + c]
        k = 0
        for px in range(2):
            for py in range(2):
                for pc in range(2):
                    if px + py + pc:
                        peer = (1 - x if px else x, 1 - y if py else y, 1 - c if pc else c)
                        _remote(mine, mine, send_sems.at[k], recv_sems.at[k], peer).start()
                        k += 1

    return _split_start("small_exchange_start", [slots], N_DEV - 1, issue, after)


def _small_exchange_wait(started, after):
    def waits(refs, send_sems, recv_sems):
        slot = refs[0].at[0]
        for k in range(N_DEV - 1):
            _wait_both(slot, slot, send_sems.at[k], recv_sems.at[k])

    return _split_wait("small_exchange_wait", started, waits, after)[0]


def _adam_math(w, g, m, v):
    m = ADAM_B1 * m + (1.0 - ADAM_B1) * g
    v = ADAM_B2 * v + (1.0 - ADAM_B2) * (g * g)
    m_hat = m / (1.0 - ADAM_B1 ** ADAM_STEP)
    v_hat = v / (1.0 - ADAM_B2 ** ADAM_STEP)
    delta = -ADAM_LR * (m_hat / (jnp.sqrt(v_hat) + ADAM_EPS) + ADAM_WD * w)
    return delta, m, v


def _adamw(w, g, m, v, name):
    rows, cols = w.shape
    fits = [t for t in range(SUBLANES, rows, SUBLANES) if rows % t == 0 and t * cols * 4 <= (3 << 19)]
    tr = max(fits) if fits else rows

    def body(w_ref, g_ref, m_ref, v_ref, d_ref, nm_ref, nv_ref, go_ref):
        g = g_ref[...]
        d, nm, nv = _adam_math(w_ref[...], g, m_ref[...], v_ref[...])
        d_ref[...] = d
        nm_ref[...] = nm
        nv_ref[...] = nv
        go_ref[...] = g

    spec = pl.BlockSpec((tr, cols), lambda i: (i, 0))
    return pl.pallas_call(
        body, name=name, grid=(rows // tr,), in_specs=[spec] * 4, out_specs=[spec] * 4,
        out_shape=[_sds(w.shape, F32)] * 4, compiler_params=_cp(("arbitrary",)),
    )(*_hbm(w, g, m, v))


def _small_sum_adamw(gathered, w, m, v):
    names = _SMALL_IN_CALL
    n = len(names)

    def body(*refs):
        a_ref = refs[0]
        w_refs, m_refs, v_refs = refs[1:1 + n], refs[1 + n:1 + 2 * n], refs[1 + 2 * n:1 + 3 * n]
        sum_ref = refs[1 + 3 * n]
        outs = refs[2 + 3 * n:]
        g = a_ref[0]
        for k in range(1, N_DEV):
            g = g + a_ref[k]
        sum_ref[...] = g
        for i, name in enumerate(names):
            r0, nr, nc = _SMALL_AT[name]
            gp = g[r0:r0 + nr, 0:nc]
            d, nm, nv = _adam_math(w_refs[i][...], gp, m_refs[i][...], v_refs[i][...])
            for k, val in enumerate((gp, d, nm, nv)):
                outs[4 * i + k][...] = val

    shapes = [w[k].shape for k in names]
    res = pl.pallas_call(
        body, name="small_sum_adamw",
        out_shape=[_sds((SMALL_ROWS, D_MODEL), F32)] + [_sds(s, F32) for s in shapes for _ in range(4)],
    )(gathered, *[w[k] for k in names], *[m[k] for k in names], *[v[k] for k in names])
    return res[0], {k: tuple(res[1 + 4 * i:5 + 4 * i]) for i, k in enumerate(names)}


_NAMES = ("g_mix", "w_in", "g_sgu", "w_s", "b_s", "sinks", "rel_bias", "w_pa", "w_pb", "w_out",
          "g_ffn", "w_up", "w_conv", "b_conv", "w_down", "g_final")

def kernel(x, g_mix, w_in, g_sgu, w_s, b_s, sinks, rel_bias, w_pa, w_pb, w_out, g_ffn, w_up, w_conv, b_conv, w_down, g_final, loss_target, m_g_mix, m_w_in, m_g_sgu, m_w_s, m_b_s, m_sinks, m_rel_bias, m_w_pa, m_w_pb, m_w_out, m_g_ffn, m_w_up, m_w_conv, m_b_conv, m_w_down, m_g_final, v_g_mix, v_w_in, v_g_sgu, v_w_s, v_b_s, v_sinks, v_rel_bias, v_w_pa, v_w_pb, v_w_out, v_g_ffn, v_w_up, v_w_conv, v_b_conv, v_w_down, v_g_final):
    w = dict(g_mix=g_mix, w_in=w_in, g_sgu=g_sgu, w_s=w_s, b_s=b_s, sinks=sinks, rel_bias=rel_bias, w_pa=w_pa, w_pb=w_pb,
             w_out=w_out, g_ffn=g_ffn, w_up=w_up, w_conv=w_conv, b_conv=b_conv, w_down=w_down, g_final=g_final)
    m = dict(g_mix=m_g_mix, w_in=m_w_in, g_sgu=m_g_sgu, w_s=m_w_s, b_s=m_b_s, sinks=m_sinks, rel_bias=m_rel_bias, w_pa=m_w_pa,
             w_pb=m_w_pb, w_out=m_w_out, g_ffn=m_g_ffn, w_up=m_w_up, w_conv=m_w_conv, b_conv=m_b_conv, w_down=m_w_down,
             g_final=m_g_final)
    v = dict(g_mix=v_g_mix, w_in=v_w_in, g_sgu=v_g_sgu, w_s=v_w_s, b_s=v_b_s, sinks=v_sinks, rel_bias=v_rel_bias, w_pa=v_w_pa,
             w_pb=v_w_pb, w_out=v_w_out, g_ffn=v_g_ffn, w_up=v_w_up, w_conv=v_w_conv, b_conv=v_b_conv, w_down=v_w_down,
             g_final=v_g_final)
    xi, yi, ci = _mesh_pos()
    me = 2 * xi + yi

    shard = {n: w[n][0] for n in _BIG}
    shard_shapes = {n: shard[n].shape for n in _BIG}
    wc_shard = w["w_conv"][0]
    wc_pad = jnp.pad(wc_shard, ((0, 5), (0, 0)))
    own = {n: _own_slot(shard[n].astype(BF16), N_CHIPS, me) for n in _BIG}
    stacks, wc_all = _allgather_weights({"w_in": own["w_in"]}, _own_slot(wc_pad, N_CHIPS, me))
    proj_gather = _allgather_start({n: own[n] for n in _MIXER[1:]}, "proj", stacks["w_in"])
    ffn_gather = _allgather_start({n: own[n] for n in _FFN}, "ffn", proj_gather[1][-1])
    w_conv_full = jnp.concatenate([wc_all[i, :3] for i in range(N_CHIPS)], axis=1)
    w_in_full = _unstack_w_in(stacks["w_in"])
    pos = jnp.stack([me, ci])

    def proj_weights(done):
        st = _allgather_forward(_allgather_wait(proj_gather, "proj", done), "proj")
        return st["w_pa"], st["w_pb"], st["w_out"].reshape(D_MODEL, D_MODEL)

    def ffn_weights(done):
        st = _allgather_forward(_allgather_wait(ffn_gather, "ffn", done), "ffn")
        return st["w_up"], st["w_down"].reshape(D_FF, D_MODEL)

    groups = {}

    def stage1(group, parts):
        groups[group] = dict(parts=parts, pair=_pair_exchange_start(parts, group, None))
        return groups[group]["pair"][1][-1]

    def stage2(group, after, order_after):
        g = groups[group]
        g["parts"], g["sib"] = _pair_exchange_wait(g["pair"], group, after)
        g["chip"] = _chip_exchange_start({n: _pair_add(g["parts"][n], g["sib"][n], n, pos) for n in g["parts"]}, group, order_after)
        return g["chip"][1][-1]

    def stage3(group, after, order_after):
        g = groups[group]
        got = _chip_exchange_wait(g["chip"], group, after)
        g["share"] = _pair_share_start(
            {n: _owner_sum(g["parts"][n], g["sib"][n], got[n], n, pos, shard_shapes[n]) for n in g["parts"]}, group, order_after)
        return g["share"][1][-1]

    grads, deltas, new_m, new_v = {}, {}, {}, {}

    def stage4(group, after):
        g_shard = _pair_share_wait(groups[group]["share"], group, after)
        last = None
        for n in g_shard:
            g = _tie(g_shard[n], last)
            if n == "w_in":
                d, nm, nv, gt = _adamw(shard[n].T, g.T, m[n][0].T, v[n][0].T, "adamw_" + n)
                grads[n], deltas[n], new_m[n], new_v[n] = gt.T[None], d.T[None], nm.T[None], nv.T[None]
            else:
                d, nm, nv, go = _adamw(shard[n], g, m[n][0], v[n][0], "adamw_" + n)
                grads[n], deltas[n], new_m[n], new_v[n] = go[None], d[None], nm[None], nv[None]
            last = nv
        return last

    def on_grads(group, parts):
        token = stage1(group, parts)
        some = next(iter(parts.values()))
        if group == "proj":
            token = stage2("ffn", some, token)
        if group == "in":
            token = stage2("proj", some, token)
            token = stage3("ffn", some, token)
            token = stage2("in", token, token)
        return token

    loss, grad_x, small, big = _local_step(
        x, loss_target, w["g_mix"], w["g_sgu"], w["w_s"][0], w["b_s"][0], w["sinks"], w["rel_bias"], w["g_ffn"],
        w["b_conv"], w["g_final"], w_in_full, w_conv_full, proj_weights, ffn_weights, on_grads, ffn_gather[1][-1])

    small["loss"] = loss
    small_gather = _small_exchange_start(_own_slot(_pack_small(small), N_DEV, 2 * me + ci), grad_x)
    token = stage3("proj", grad_x, small_gather[-1])
    done = stage4("ffn", token)
    done = stage4("proj", done)
    token = stage3("in", done, None)
    all_small = _small_exchange_wait(small_gather, token)
    two_d = {n: (lambda a, n=n: a.reshape(_SMALL_AT[n][1:])) for n in _SMALL_IN_CALL}
    s_sum, s_out = _small_sum_adamw(all_small, *[{n: two_d[n](p[n]) for n in _SMALL_IN_CALL} for p in (w, m, v)])
    stage4("in", all_small)
    for n in _SMALL_IN_CALL:
        grads[n], deltas[n], new_m[n], new_v[n] = [a.reshape(w[n].shape) for a in s_out[n]]

    def rows(n):
        r0, nr, _ = _SMALL_AT[n]
        return s_sum[r0:r0 + nr]

    wcols = wc_shard.shape[1]
    g_wc = lax.dynamic_slice(_unwide(rows("w_conv"), 3), (0, me * wcols), (3, wcols))
    d, nm, nv, _ = _adamw(wc_shard, g_wc, m["w_conv"][0], v["w_conv"][0], "adamw_w_conv")
    grads["w_conv"], deltas["w_conv"], new_m["w_conv"], new_v["w_conv"] = g_wc[None], d[None], nm[None], nv[None]
    d, nm, nv, go = _adamw(w["b_conv"], _unwide(rows("b_conv"), 1), m["b_conv"], v["b_conv"], "adamw_b_conv")
    grads["b_conv"], deltas["b_conv"], new_m["b_conv"], new_v["b_conv"] = go, d, nm, nv
    g_rb = rows("rel_bias")[:, :N_BUCKETS * N_HEADS].reshape(N_BUCKETS, N_HEADS)
    d, nm, nv, go = _adamw(w["rel_bias"], g_rb, m["rel_bias"], v["rel_bias"], "adamw_rel_bias")
    grads["rel_bias"], deltas["rel_bias"], new_m["rel_bias"], new_v["rel_bias"] = go, d, nm, nv
    flat_s = (A_GROUPS * CHUNK, CHUNK)
    d, nm, nv, go = _adamw(w["w_s"].reshape(flat_s), rows("w_s").reshape(flat_s), m["w_s"].reshape(flat_s),
                           v["w_s"].reshape(flat_s), "adamw_w_s")
    grads["w_s"], deltas["w_s"], new_m["w_s"], new_v["w_s"] = [a.reshape(w["w_s"].shape) for a in (go, d, nm, nv)]

    return (s_sum[0, 0], grad_x, *[grads[n] for n in _NAMES], *[deltas[n] for n in _NAMES],
            *[new_m[n] for n in _NAMES], *[new_v[n] for n in _NAMES])
```

```python
import functools

import numpy as np
import jax
import jax.numpy as jnp
from jax import lax
from jax.experimental import pallas as pl
from jax.experimental.pallas import tpu as pltpu

F32 = jnp.float32
BF16 = jnp.bfloat16

D_MODEL = 1024
CHUNK = 128
A_GROUPS = 4
A_WIDTH = 512
N_HEADS = 8
HEAD_DIM = 64
Q_DIM = 512
KV_DIM = 128
N_BUCKETS = 32
MAX_DISTANCE = 128
D_FF = 2816
EPS = 1e-6
NEG_INF = -1e30
G_DIM = 2 * D_MODEL
A_DIM = 2 * A_WIDTH
B_DIM = Q_DIM + 2 * KV_DIM
LANES = 128
SUBLANES = 8
ROW_TILE = 512
WIDE_ROW_TILE = 256
COL_CHUNK = 512
GRAD_ROW_TILE = 512
BF16_ROWS = 16
N_CHIPS = 4
N_DEV = 8

ADAM_LR = 0.001
ADAM_B1 = 0.9
ADAM_B2 = 0.999
ADAM_EPS = 1e-08
ADAM_WD = 0.01
ADAM_STEP = 10

MESH = pl.DeviceIdType.MESH
_GELU_C = 0.7978845608028654
_GELU_A = 0.044715


def _cp(sem=None, vmem_mb=None):
    kw = {}
    if sem is not None:
        kw["dimension_semantics"] = sem
    if vmem_mb is not None:
        kw["vmem_limit_bytes"] = vmem_mb << 20
    return pltpu.CompilerParams(**kw)


def _dot(a, b):
    return jnp.dot(a, b, preferred_element_type=F32)


def _dot_nt(a, b):
    return lax.dot_general(a, b, (((1,), (1,)), ((), ())), preferred_element_type=F32)


def _dot_tn(a, b):
    return lax.dot_general(a, b, (((0,), (0,)), ((), ())), preferred_element_type=F32)


def _rms_r(x):
    return lax.rsqrt(jnp.mean(x * x, axis=-1, keepdims=True) + EPS)


def _rms_bwd(dh, n, r, g):
    dn = dh * g
    return r * (dn - n * jnp.mean(dn * n, axis=-1, keepdims=True))


def _gelu(x):
    t = jnp.tanh(_GELU_C * (x + _GELU_A * (x * x * x)))
    return 0.5 * x * (1.0 + t), t


def _gelu_grad(x, t):
    return 0.5 * (1.0 + t) + 0.5 * x * (1.0 - t * t) * (_GELU_C * (1.0 + 3.0 * _GELU_A * x * x))


def _sigmoid(x):
    return 1.0 / (1.0 + jnp.exp(-x))


def _tie(x, dep):
    return x if dep is None else lax.optimization_barrier((x, dep))[0]


def _row(tm, w):
    return pl.BlockSpec((tm, w), lambda i: (i, 0))


def _full(shape):
    nd = len(shape)
    return pl.BlockSpec(tuple(shape), lambda *_: (0,) * nd)


def _resident(shape):
    nd = len(shape)
    return pl.BlockSpec(tuple(shape), lambda *_: (0,) * nd, pipeline_mode=pl.Buffered(1))


def _sds(shape, dtype):
    return jax.ShapeDtypeStruct(tuple(shape), dtype)


def _hbm(*arrays):
    return [pltpu.with_memory_space_constraint(a, pltpu.HBM) for a in arrays]


HBM = pl.BlockSpec(memory_space=pltpu.HBM)
ANY = pl.BlockSpec(memory_space=pl.ANY)
SEM = pl.BlockSpec(memory_space=pltpu.SEMAPHORE)


def _band_buckets():
    i = np.arange(CHUNK)[:, None]
    j = np.arange(2 * CHUNK)[None, :]
    dist = i + CHUNK - j
    valid = (dist >= 0) & (dist < CHUNK)
    d = np.clip(dist, 0, None)
    max_exact = N_BUCKETS // 2
    large = max_exact + (np.log(np.maximum(d, 1) / max_exact) / np.log(MAX_DISTANCE / max_exact)
                         * (N_BUCKETS - max_exact)).astype(np.int32)
    large = np.minimum(large, N_BUCKETS - 1)
    buckets = np.where(d < max_exact, d, large).astype(np.int32)
    return np.where(valid, buckets, -1).astype(np.int32)


_A_COLS = slice(0, A_DIM)
_B_COLS = slice(A_DIM, A_DIM + B_DIM)
_G_COLS = slice(A_DIM + B_DIM, A_DIM + B_DIM + G_DIM)


def _inproj(x2, g_mix, w_in, g_sgu, w_s, b_st, tm, after=None):
    T = x2.shape[0]
    order = [] if after is None else [after]

    def body(*refs):
        x_ref, g_ref, w_ref, gs_ref, ws_ref, bs_ref = refs[:6]
        pg_ref, pa_ref, pb_ref, h_ref, ya_ref = refs[6 + len(order):]
        x = x_ref[...]
        h = (x * _rms_r(x) * g_ref[...]).astype(BF16)
        h_ref[...] = h
        pa = _dot(h, w_ref[:, _A_COLS]).astype(BF16)
        pa_ref[...] = pa
        pb_ref[...] = _dot(h, w_ref[:, _B_COLS]).astype(BF16)
        pg_ref[...] = _dot(h, w_ref[:, _G_COLS]).astype(BF16)
        _sgu_apply(pa.astype(F32), gs_ref[...], ws_ref, bs_ref, ya_ref)

    return pl.pallas_call(
        body, name="inproj", grid=(T // tm,),
        in_specs=[_row(tm, D_MODEL), _full(g_mix.shape), _resident(w_in.shape), _full(g_sgu.shape), _full(w_s.shape),
                  _full(b_st.shape)] + [ANY] * len(order),
        out_specs=[_row(tm, G_DIM), _row(tm, A_DIM), _row(tm, B_DIM), _row(tm, D_MODEL), _row(tm, A_WIDTH)],
        out_shape=[_sds((T, G_DIM), BF16), _sds((T, A_DIM), BF16), _sds((T, B_DIM), BF16), _sds((T, D_MODEL), BF16),
                   _sds((T, A_WIDTH), BF16)],
        compiler_params=_cp(("arbitrary",), 48),
    )(*_hbm(x2, g_mix, w_in, g_sgu, w_s, b_st), *order)


def _sgu_parts(p, g):
    pu = p[:, :A_WIDTH]
    pv = p[:, A_WIDTH:]
    u, tu = _gelu(pu)
    vv, tv = _gelu(pv)
    rv = _rms_r(vv)
    vn = (vv * rv * g).astype(BF16)
    return pu, pv, u, tu, vv, tv, rv, vn


def _tril():
    r = lax.broadcasted_iota(jnp.int32, (CHUNK, CHUNK), 0)
    c = lax.broadcasted_iota(jnp.int32, (CHUNK, CHUNK), 1)
    return r >= c


def _sgu_apply(p, g, ws_ref, bs_ref, y_ref):
    tril = _tril()
    _, _, u, _, _, _, _, vn = _sgu_parts(p, g)
    for gi in range(A_GROUPS):
        wm = jnp.where(tril, ws_ref[gi], 0.0).astype(BF16)
        bcol = bs_ref[:, gi:gi + 1]
        cs = slice(gi * CHUNK, (gi + 1) * CHUNK)
        for c in range(p.shape[0] // CHUNK):
            rs = slice(c * CHUNK, (c + 1) * CHUNK)
            s = _dot(wm, vn[rs, cs]) + bcol
            y_ref[rs, cs] = (u[rs, cs] * s).astype(BF16)


HEAD_ROWS = N_HEADS * CHUNK


def _head_rows(h):
    return slice(h * CHUNK, (h + 1) * CHUNK)


def _attn_setup(bias_scr, sink_scr, kvar_scr, qkv_ref, bk_ref, rel_ref, sink_ref):
    @pl.when(pl.program_id(0) == 0)
    def _():
        bk = bk_ref[...]
        for h in range(N_HEADS):
            acc = jnp.full((CHUNK, 2 * CHUNK), NEG_INF, F32)
            for b in range(N_BUCKETS):
                acc = jnp.where(bk == b, rel_ref[b, h], acc)
            bias_scr[_head_rows(h), :] = acc
            sink_scr[_head_rows(h), :] = jnp.full((CHUNK, LANES), sink_ref[0, h], F32)

    seq = qkv_ref.shape[0]
    rows_per = 2 * CHUNK
    for is_v in range(2):
        c0 = Q_DIM + is_v * KV_DIM
        for r in range(seq // rows_per):
            rs = slice(r * rows_per, (r + 1) * rows_per)
            a = qkv_ref[rs, c0:c0 + KV_DIM].astype(F32)
            lane = lax.broadcasted_iota(jnp.int32, a.shape, 1)
            lo = jnp.where(lane < HEAD_DIM, a, 0.0)
            hi = jnp.where(lane >= HEAD_DIM, a, 0.0)
            kvar_scr[4 * is_v + 0, rs, :] = lo.astype(BF16)
            kvar_scr[4 * is_v + 1, rs, :] = pltpu.roll(lo, HEAD_DIM, 1).astype(BF16)
            kvar_scr[4 * is_v + 2, rs, :] = pltpu.roll(hi, HEAD_DIM, 1).astype(BF16)
            kvar_scr[4 * is_v + 3, rs, :] = hi.astype(BF16)


def _rowsum(a, ones):
    hi = a.astype(BF16)
    lo = (a - hi.astype(F32)).astype(BF16)
    return _dot(hi, ones) + _dot(lo, ones)


def _both(a):
    return jnp.concatenate([a, a], axis=1)


def _attn_probs(qkv_ref, r0, n, kv, bias_scr, sink_scr, ones):
    s = jnp.concatenate([_dot_nt(qkv_ref[pl.ds(r0, CHUNK), (h // 2) * LANES:(h // 2 + 1) * LANES], kv[h // 4][h % 2])
                         for h in range(N_HEADS)], axis=0)
    s = s * (HEAD_DIM ** -0.5) + bias_scr[...]
    col = lax.broadcasted_iota(jnp.int32, s.shape, 1)
    s = jnp.where((col < CHUNK) & (n == 0), NEG_INF, s)
    sink = sink_scr[...]
    m = jnp.maximum(jnp.max(s, axis=-1, keepdims=True), sink)
    p = jnp.exp(s - _both(m))
    es = jnp.exp(sink - m)
    inv = 1.0 / (_dot(p.astype(BF16), ones) + es)
    return p * _both(inv), es * inv


def _attn_block_inputs(kvar_scr, n):
    r0 = pl.multiple_of(n * CHUNK, CHUNK)
    rp = pl.multiple_of(jnp.maximum(n - 1, 0) * CHUNK, CHUNK)

    def both(idx):
        return jnp.concatenate([kvar_scr[idx, pl.ds(rp, CHUNK), :], kvar_scr[idx, pl.ds(r0, CHUNK), :]], axis=0)

    kv = ((both(0), both(1)), (both(2), both(3)))
    vv = ((both(4), both(5)), (both(6), both(7)))
    return r0, kv, vv


def _attn_fwd(proj_b, sinks, rel_bias, n_seq, seq):
    nb = seq // CHUNK
    bk = jnp.asarray(_band_buckets())

    def body(qkv_ref, bk_ref, rel_ref, sink_ref, o_ref, bias_scr, sink_scr, kvar_scr):
        _attn_setup(bias_scr, sink_scr, kvar_scr, qkv_ref, bk_ref, rel_ref, sink_ref)
        ones = jnp.ones((2 * CHUNK, LANES), BF16)

        def blk(n, carry):
            r0, kv, vv = _attn_block_inputs(kvar_scr, n)
            prob, _ = _attn_probs(qkv_ref, r0, n, kv, bias_scr, sink_scr, ones)
            pb = prob.astype(BF16)
            for pr in range(N_HEADS // 2):
                acc = _dot(pb[_head_rows(2 * pr)], vv[pr // 2][0]) + _dot(pb[_head_rows(2 * pr + 1)], vv[pr // 2][1])
                o_ref[pl.ds(r0, CHUNK), pr * LANES:(pr + 1) * LANES] = acc.astype(BF16)
            return carry

        lax.fori_loop(0, nb, blk, 0)

    smem = pl.BlockSpec(memory_space=pltpu.SMEM)
    return pl.pallas_call(
        body, name="attn_fwd", grid=(n_seq,),
        in_specs=[_row(seq, B_DIM), _full(bk.shape), smem, smem],
        out_specs=_row(seq, Q_DIM), out_shape=_sds((n_seq * seq, Q_DIM), BF16),
        scratch_shapes=[pltpu.VMEM((HEAD_ROWS, 2 * CHUNK), F32), pltpu.VMEM((HEAD_ROWS, LANES), F32),
                        pltpu.VMEM((8, seq, KV_DIM), BF16)],
        compiler_params=_cp(("arbitrary",), 40),
    )(*_hbm(proj_b, bk), rel_bias, sinks)


def _dot_stacked(a, w_ref):
    return jnp.concatenate([_dot(a, w_ref[i]) for i in range(N_CHIPS)], axis=1)


def _dot_nt_stacked(a, w_ref):
    w = w_ref.shape[2]
    acc = _dot_nt(a[:, :w], w_ref[0])
    for i in range(1, N_CHIPS):
        acc = acc + _dot_nt(a[:, i * w:(i + 1) * w], w_ref[i])
    return acc


def _merge_fwd(x2, y_a, y_b, proj_g, w_pa, w_pb, w_out, tm):
    T = x2.shape[0]

    def body(x_ref, ya_ref, yb_ref, g_ref, wpa_ref, wpb_ref, wo_ref, x1_ref, mg_ref):
        g = g_ref[...].astype(F32)
        pa = _dot_stacked(ya_ref[...], wpa_ref)
        pb = _dot_stacked(yb_ref[...], wpb_ref)
        merged = (_sigmoid(g[:, :D_MODEL]) * pa + _sigmoid(g[:, D_MODEL:]) * pb).astype(BF16)
        mg_ref[...] = merged
        x1_ref[...] = x_ref[...] + _dot(merged, wo_ref[...])

    return pl.pallas_call(
        body, name="merge_fwd", grid=(T // tm,),
        in_specs=[_row(tm, D_MODEL), _row(tm, A_WIDTH), _row(tm, Q_DIM), _row(tm, G_DIM),
                  _resident(w_pa.shape), _resident(w_pb.shape), _resident(w_out.shape)],
        out_specs=[_row(tm, D_MODEL), _row(tm, D_MODEL)],
        out_shape=[_sds((T, D_MODEL), F32), _sds((T, D_MODEL), BF16)],
        compiler_params=_cp(("arbitrary",), 40),
    )(*_hbm(x2, y_a, y_b, proj_g, w_pa, w_pb, w_out))


def _upproj(x1, g_ffn, w_up, w_conv, b_conv, tm, seq):
    T = x1.shape[0]
    cw = w_up.shape[2]
    tiles_per_seq = seq // tm

    def body(x_ref, g_ref, w_ref, wc_ref, bc_ref, u_ref, h_ref, gate_ref, val_ref, tail_scr):
        at_start = (pl.program_id(0) % tiles_per_seq) == 0
        x = x_ref[...]
        h = (x * _rms_r(x) * g_ref[...]).astype(BF16)
        h_ref[...] = h
        for i in range(N_CHIPS):
            cs = slice(i * cw, (i + 1) * cw)
            u = _dot(h, w_ref[i])
            u_ref[:, cs] = u.astype(BF16)
            hl = jnp.where(at_start, 0.0, tail_scr[SUBLANES - 2:SUBLANES, cs])
            tail_scr[:, cs] = u[tm - SUBLANES:]
            up = _conv_out((u, _shift_down(u, hl, 1), _shift_down(u, hl, 2)), wc_ref[:, cs], bc_ref[:, cs])
            out_ref = gate_ref if i < N_CHIPS // 2 else val_ref
            out_ref[:, (i % 2) * cw:(i % 2 + 1) * cw] = up.astype(BF16)

    return pl.pallas_call(
        body, name="upproj", grid=(T // tm,),
        in_specs=[_row(tm, D_MODEL), _full(g_ffn.shape), _resident(w_up.shape), _full(w_conv.shape), _full(b_conv.shape)],
        out_specs=[_row(tm, 2 * D_FF), _row(tm, D_MODEL), _row(tm, D_FF), _row(tm, D_FF)],
        out_shape=[_sds((T, 2 * D_FF), BF16), _sds((T, D_MODEL), BF16), _sds((T, D_FF), BF16), _sds((T, D_FF), BF16)],
        scratch_shapes=[pltpu.VMEM((SUBLANES, 2 * D_FF), F32)],
        compiler_params=_cp(("arbitrary",), 56),
    )(*_hbm(x1, g_ffn, w_up, w_conv, b_conv))


def _shift_down(u, halo, k):
    rolled = pltpu.roll(u, k, 0)
    head = rolled[:SUBLANES]
    row = lax.broadcasted_iota(jnp.int32, head.shape, 0)
    if k == 1:
        head = jnp.where(row == 0, halo[1:2], head)
    else:
        head = jnp.where(row == 0, halo[0:1], jnp.where(row == 1, halo[1:2], head))
    return jnp.concatenate([head, rolled[SUBLANES:]], axis=0)


def _shift_up(d, halo, k):
    tm = d.shape[0]
    rolled = pltpu.roll(d, tm - k, 0)
    tail = rolled[tm - SUBLANES:]
    row = lax.broadcasted_iota(jnp.int32, tail.shape, 0)
    if k == 1:
        tail = jnp.where(row == SUBLANES - 1, halo[0:1], tail)
    else:
        tail = jnp.where(row == SUBLANES - 2, halo[0:1], jnp.where(row == SUBLANES - 1, halo[1:2], tail))
    return jnp.concatenate([rolled[:tm - SUBLANES], tail], axis=0)


def _conv_out(taps, wc, bc):
    u, u1, u2 = taps
    return wc[0:1] * u2 + wc[1:2] * u1 + wc[2:3] * u + bc


def _ffn_down_loss(gate, val, x1, target, w_down, g_final, tm):
    T = x1.shape[0]
    half = D_FF // 2

    def body(gt_ref, vl_ref, x1_ref, t_ref, wd_ref, g_ref, dx2_ref, loss_ref, gg_ref):
        i = pl.program_id(0)
        acc = jnp.zeros((tm, D_MODEL), F32)
        for j in range(2):
            gc = slice(j * half, (j + 1) * half)
            gate = gt_ref[:, gc].astype(F32)
            act = (gate * _sigmoid(gate) * vl_ref[:, gc].astype(F32)).astype(BF16)
            acc = acc + _dot(act, wd_ref[gc, :])
        x2 = x1_ref[...] + acc
        r = _rms_r(x2)
        n = x2 * r
        g = g_ref[...]
        diff = n * g - t_ref[...]
        dy = diff * (1.0 / D_MODEL)
        dx2_ref[...] = _rms_bwd(dy, n, r, g)

        @pl.when(i == 0)
        def _():
            loss_ref[...] = jnp.zeros_like(loss_ref)
            gg_ref[...] = jnp.zeros_like(gg_ref)

        loss_ref[...] += 0.5 * jnp.sum(jnp.mean(diff * diff, axis=-1, keepdims=True), axis=0, keepdims=True)
        gg_ref[...] += jnp.sum(dy * n, axis=0, keepdims=True)

    return pl.pallas_call(
        body, name="ffn_down_loss", grid=(T // tm,),
        in_specs=[_row(tm, D_FF), _row(tm, D_FF), _row(tm, D_MODEL), _row(tm, D_MODEL),
                  _resident(w_down.shape), _full(g_final.shape)],
        out_specs=[_row(tm, D_MODEL), _full((1, 1)), _full((1, D_MODEL))],
        out_shape=[_sds((T, D_MODEL), F32), _sds((1, 1), F32), _sds((1, D_MODEL), F32)],
        compiler_params=_cp(("arbitrary",), 48),
    )(*_hbm(gate, val, x1, target, w_down, g_final))


def _ffn_bwd_act(gate, val, dx2, w_down, tm):
    T = dx2.shape[0]
    half = D_FF // 2
    nt = T // tm

    def body(g_ref, v_ref, dx_ref, wd_ref, dg_ref, dv_ref, gwd_out, gbg_ref, gbv_ref, gwd_ref):
        i = pl.program_id(1)

        @pl.when(i == 0)
        def _():
            for r in (gwd_ref, gbg_ref, gbv_ref):
                r[...] = jnp.zeros_like(r)

        dx = dx_ref[...].astype(BF16)
        for c0 in range(0, half, COL_CHUNK):
            cs = slice(c0, min(c0 + COL_CHUNK, half))
            gate = g_ref[:, cs].astype(F32)
            val = v_ref[:, cs].astype(F32)
            sg = _sigmoid(gate)
            silu = gate * sg
            d_act = _dot_nt(dx, wd_ref[cs, :])
            d_val = d_act * silu
            d_gate = d_act * val * (sg * (1.0 + gate * (1.0 - sg)))
            dg_ref[:, cs] = d_gate.astype(BF16)
            dv_ref[:, cs] = d_val.astype(BF16)
            gwd_ref[cs, :] += _dot_tn((silu * val).astype(BF16), dx)
            gbg_ref[:, cs] += jnp.sum(d_gate, axis=0, keepdims=True)
            gbv_ref[:, cs] += jnp.sum(d_val, axis=0, keepdims=True)

        @pl.when(i == nt - 1)
        def _():
            gwd_out[...] = gwd_ref[...].astype(BF16)

    tile = pl.BlockSpec((tm, half), lambda j, i: (i, j))
    vec = pl.BlockSpec((1, half), lambda j, i: (0, j))
    wrows = pl.BlockSpec((half, D_MODEL), lambda j, i: (j, 0))
    return pl.pallas_call(
        body, name="ffn_bwd_act", grid=(2, nt),
        in_specs=[tile, tile, pl.BlockSpec((tm, D_MODEL), lambda j, i: (i, 0)), wrows],
        out_specs=[tile, tile, wrows, vec, vec],
        out_shape=[_sds((T, D_FF), BF16), _sds((T, D_FF), BF16), _sds((D_FF, D_MODEL), BF16),
                   _sds((1, D_FF), F32), _sds((1, D_FF), F32)],
        scratch_shapes=[pltpu.VMEM((half, D_MODEL), F32)],
        compiler_params=_cp(("arbitrary", "arbitrary"), 56),
    )(*_hbm(gate, val, dx2, w_down))


def _ffn_bwd_up(d_gate, d_val, upre, dx2, x1, g_ffn, w_conv, w_up, tm, seq):
    T = dx2.shape[0]
    tiles_per_seq = seq // tm
    k16 = tm // BF16_ROWS
    n16 = T // BF16_ROWS
    cw = D_FF // 2

    def body(dg_ref, dv_ref, hg_ref, hv_ref, u_ref, dx2_ref, x1_ref, g_ref, wc_ref, wu_ref, du_ref, dx1_ref, gg_ref, gwc_ref):
        i = pl.program_id(0)
        at_end = (i % tiles_per_seq) == tiles_per_seq - 1

        @pl.when(i == 0)
        def _():
            gg_ref[...] = jnp.zeros_like(gg_ref)
            gwc_ref[...] = jnp.zeros_like(gwc_ref)

        dh = jnp.zeros((tm, D_MODEL), F32)
        for j in range(4):
            src, hsrc = (dg_ref, hg_ref) if j < 2 else (dv_ref, hv_ref)
            ls = slice((j % 2) * cw, (j % 2 + 1) * cw)
            cs = slice(j * cw, (j + 1) * cw)
            d = src[:, ls].astype(F32)
            hl = hsrc[:, ls].astype(F32)[0:2]
            hl = jnp.where(at_end, 0.0, hl)
            wc = wc_ref[:, cs]
            d1 = _shift_up(d, hl, 1)
            d2 = _shift_up(d, hl, 2)
            du = (wc[2:3] * d + wc[1:2] * d1 + wc[0:1] * d2).astype(BF16)
            du_ref[:, cs] = du
            dh = dh + _dot_nt(du, wu_ref[j])
            u = u_ref[:, cs].astype(F32)
            gwc_ref[0:1, cs] += jnp.sum(d2 * u, axis=0, keepdims=True)
            gwc_ref[1:2, cs] += jnp.sum(d1 * u, axis=0, keepdims=True)
            gwc_ref[2:3, cs] += jnp.sum(d * u, axis=0, keepdims=True)
        x = x1_ref[...]
        r = _rms_r(x)
        n = x * r
        dx1_ref[...] = dx2_ref[...] + _rms_bwd(dh, n, r, g_ref[...])
        gg_ref[...] += jnp.sum(dh * n, axis=0, keepdims=True)

    nxt = pl.BlockSpec((BF16_ROWS, D_FF), lambda i: (jnp.minimum((i + 1) * k16, n16 - 1), 0))
    return pl.pallas_call(
        body, name="ffn_bwd_up", grid=(T // tm,),
        in_specs=[_row(tm, D_FF), _row(tm, D_FF), nxt, nxt, _row(tm, 2 * D_FF), _row(tm, D_MODEL), _row(tm, D_MODEL),
                  _full(g_ffn.shape), _full(w_conv.shape), _resident(w_up.shape)],
        out_specs=[_row(tm, 2 * D_FF), _row(tm, D_MODEL), _full((1, D_MODEL)), _full((3, 2 * D_FF))],
        out_shape=[_sds((T, 2 * D_FF), BF16), _sds((T, D_MODEL), F32), _sds((1, D_MODEL), F32), _sds((3, 2 * D_FF), F32)],
        compiler_params=_cp(("arbitrary",), 56),
    )(*_hbm(d_gate, d_val, d_gate, d_val, upre, dx2, x1, g_ffn, w_conv, w_up))


def _matmul_tn(a, b, tn, tk, name):
    T, M = a.shape
    N = b.shape[1]
    nk = T // tk

    def body(a_ref, b_ref, o_ref, acc_ref):
        k = pl.program_id(1)

        @pl.when(k == 0)
        def _():
            acc_ref[...] = jnp.zeros_like(acc_ref)

        acc_ref[...] += _dot_tn(a_ref[...], b_ref[...])

        @pl.when(k == nk - 1)
        def _():
            o_ref[...] = acc_ref[...].astype(BF16)

    return pl.pallas_call(
        body, name=name, grid=(N // tn, nk),
        in_specs=[pl.BlockSpec((tk, M), lambda j, k: (k, 0)), pl.BlockSpec((tk, tn), lambda j, k: (k, j))],
        out_specs=pl.BlockSpec((M, tn), lambda j, k: (0, j)), out_shape=_sds((M, N), BF16),
        scratch_shapes=[pltpu.VMEM((M, tn), F32)],
        compiler_params=_cp(("arbitrary", "arbitrary"), 48),
    )(*_hbm(a, b))


def _merge_bwd(dx1, merged, y_a, y_b, proj_g, proj_a, w_pa, w_pb, w_out, g_sgu, w_s, b_st, tm, after=None):
    T = dx1.shape[0]

    nt = T // tm
    pshape = (A_WIDTH, D_MODEL)
    order = [] if after is None else [after]

    def body(*refs):
        dx_ref, mg_ref, ya_ref, yb_ref, g_ref, p_ref, wpa_ref, wpb_ref, wo_ref, gs_ref, ws_ref, bs_ref = refs[:12]
        (dg_ref, da_ref, dyb_ref, gwo_out, gwpa_out, gwpb_out, gws_ref, gbs_ref, gg_ref,
         gwo_ref, gwpa_ref, gwpb_ref) = refs[12 + len(order):]
        i = pl.program_id(0)

        @pl.when(i == 0)
        def _():
            for r in (gwo_ref, gwpa_ref, gwpb_ref, gws_ref, gbs_ref, gg_ref):
                r[...] = jnp.zeros_like(r)

        dx = dx_ref[...].astype(BF16)
        dm = _dot_nt(dx, wo_ref[...])
        g = g_ref[...].astype(F32)
        ya = ya_ref[...]
        yb = yb_ref[...]
        pa = _dot_stacked(ya, wpa_ref)
        pb = _dot_stacked(yb, wpb_ref)
        sa = _sigmoid(g[:, :D_MODEL])
        sb = _sigmoid(g[:, D_MODEL:])
        dpa = (dm * sa).astype(BF16)
        dpb = (dm * sb).astype(BF16)
        dg_ref[:, :D_MODEL] = (dm * pa * (sa * (1.0 - sa))).astype(BF16)
        dg_ref[:, D_MODEL:] = (dm * pb * (sb * (1.0 - sb))).astype(BF16)
        d_ya = _dot_nt_stacked(dpa, wpa_ref).astype(BF16)
        dyb_ref[...] = _dot_nt_stacked(dpb, wpb_ref).astype(BF16)
        _sgu_bwd_apply(p_ref[...].astype(F32), d_ya.astype(F32), gs_ref[...], ws_ref, bs_ref, da_ref, gws_ref, gbs_ref, gg_ref)
        gwo_ref[...] += _dot_tn(mg_ref[...], dx)
        gwpa_ref[...] += _dot_tn(ya, dpa)
        gwpb_ref[...] += _dot_tn(yb, dpb)

        @pl.when(i == nt - 1)
        def _():
            gwo_out[...] = gwo_ref[...].astype(BF16)
            gwpa_out[...] = gwpa_ref[...].astype(BF16)
            gwpb_out[...] = gwpb_ref[...].astype(BF16)

    return pl.pallas_call(
        body, name="merge_bwd", grid=(nt,),
        in_specs=[_row(tm, D_MODEL), _row(tm, D_MODEL), _row(tm, A_WIDTH), _row(tm, Q_DIM), _row(tm, G_DIM), _row(tm, A_DIM),
                  _resident(w_pa.shape), _resident(w_pb.shape), _resident(w_out.shape),
                  _full(g_sgu.shape), _full(w_s.shape), _full(b_st.shape)] + [ANY] * len(order),
        out_specs=[_row(tm, G_DIM), _row(tm, A_DIM), _row(tm, Q_DIM),
                   _full(w_out.shape), _full(pshape), _full(pshape), _full(w_s.shape), _full(b_st.shape), _full(g_sgu.shape)],
        out_shape=[_sds((T, G_DIM), BF16), _sds((T, A_DIM), BF16), _sds((T, Q_DIM), BF16),
                   _sds(w_out.shape, BF16), _sds(pshape, BF16), _sds(pshape, BF16),
                   _sds(w_s.shape, F32), _sds(b_st.shape, F32), _sds(g_sgu.shape, F32)],
        scratch_shapes=[pltpu.VMEM(w_out.shape, F32), pltpu.VMEM(pshape, F32), pltpu.VMEM(pshape, F32)],
        compiler_params=_cp(("arbitrary",), 56),
    )(*_hbm(dx1, merged, y_a, y_b, proj_g, proj_a, w_pa, w_pb, w_out, g_sgu, w_s, b_st), *order)


def _sgu_bwd_apply(p, dy, g, ws_ref, bs_ref, dp_ref, gws_ref, gbs_ref, gg_ref):
    tril = _tril()
    pu, pv, u, tu, vv, tv, rv, vn = _sgu_parts(p, g)
    du_cols = []
    dvn_cols = []
    for gi in range(A_GROUPS):
        wm = jnp.where(tril, ws_ref[gi], 0.0).astype(BF16)
        wmt = wm.astype(F32).T.astype(BF16)
        bcol = bs_ref[:, gi:gi + 1]
        cs = slice(gi * CHUNK, (gi + 1) * CHUNK)
        du_rows = []
        dvn_rows = []
        gw = jnp.zeros((CHUNK, CHUNK), F32)
        gb = jnp.zeros((CHUNK, 1), F32)
        for c in range(p.shape[0] // CHUNK):
            rs = slice(c * CHUNK, (c + 1) * CHUNK)
            vn_c = vn[rs, cs]
            s = _dot(wm, vn_c) + bcol
            dy_c = dy[rs, cs]
            ds = dy_c * u[rs, cs]
            du_rows.append(dy_c * s)
            dsb = ds.astype(BF16)
            gw = gw + _dot_nt(dsb, vn_c)
            gb = gb + jnp.sum(ds, axis=-1, keepdims=True)
            dvn_rows.append(_dot(wmt, dsb))
        gws_ref[gi] += jnp.where(tril, gw, 0.0)
        gbs_ref[:, gi:gi + 1] += gb
        du_cols.append(jnp.concatenate(du_rows, axis=0))
        dvn_cols.append(jnp.concatenate(dvn_rows, axis=0))
    du = jnp.concatenate(du_cols, axis=1)
    dvn = jnp.concatenate(dvn_cols, axis=1)
    vhat = vv * rv
    gg_ref[...] += jnp.sum(dvn * vhat, axis=0, keepdims=True)
    dvv = _rms_bwd(dvn, vhat, rv, g)
    dp_ref[:, :A_WIDTH] = (du * _gelu_grad(pu, tu)).astype(BF16)
    dp_ref[:, A_WIDTH:] = (dvv * _gelu_grad(pv, tv)).astype(BF16)


def _attn_bwd(proj_b, d_yb, sinks, rel_bias, n_seq, seq):
    nb = seq // CHUNK
    bk = jnp.asarray(_band_buckets())

    def body(qkv_ref, do_ref, bk_ref, rel_ref, sink_ref, d_ref, gs_ref, gr_ref,
             bias_scr, sink_scr, kvar_scr, dbias_scr, dk_scr, dv_scr, ds_scr):
        b = pl.program_id(0)
        _attn_setup(bias_scr, sink_scr, kvar_scr, qkv_ref, bk_ref, rel_ref, sink_ref)
        ones = jnp.ones((2 * CHUNK, LANES), BF16)

        @pl.when(b == 0)
        def _():
            dbias_scr[...] = jnp.zeros_like(dbias_scr)
            ds_scr[...] = jnp.zeros_like(ds_scr)

        dk_scr[...] = jnp.zeros_like(dk_scr)
        dv_scr[...] = jnp.zeros_like(dv_scr)

        def transposed(a):
            return a.astype(F32).T.astype(BF16)

        def blk(n, carry):
            r0, kv, vv = _attn_block_inputs(kvar_scr, n)
            prob, psink = _attn_probs(qkv_ref, r0, n, kv, bias_scr, sink_scr, ones)
            dp = jnp.concatenate([_dot_nt(do_ref[pl.ds(r0, CHUNK), (h // 2) * LANES:(h // 2 + 1) * LANES], vv[h // 4][h % 2])
                                  for h in range(N_HEADS)], axis=0)
            delta = _rowsum(prob * dp, ones)
            dsc = prob * (dp - _both(delta))
            ds_scr[...] += psink * delta
            dbias_scr[...] += dsc
            dsb = (dsc * (HEAD_DIM ** -0.5)).astype(BF16)
            pb = prob.astype(BF16)
            dkt = [jnp.zeros((HEAD_DIM, 2 * CHUNK), F32) for _ in range(2)]
            dvt = [jnp.zeros((HEAD_DIM, 2 * CHUNK), F32) for _ in range(2)]
            for pr in range(N_HEADS // 2):
                ps = slice(pr * LANES, (pr + 1) * LANES)
                qpt = transposed(qkv_ref[pl.ds(r0, CHUNK), ps])
                dopt = transposed(do_ref[pl.ds(r0, CHUNK), ps])
                kvh = pr // 2
                dq = jnp.zeros((CHUNK, LANES), F32)
                for hh in range(2):
                    hr = _head_rows(2 * pr + hh)
                    rows = slice(hh * HEAD_DIM, (hh + 1) * HEAD_DIM)
                    dq = dq + _dot(dsb[hr], kv[kvh][hh])
                    dkt[kvh] = dkt[kvh] + _dot(qpt, dsb[hr])[rows]
                    dvt[kvh] = dvt[kvh] + _dot(dopt, pb[hr])[rows]
                d_ref[pl.ds(r0, CHUNK), ps] = dq.astype(BF16)
            dk_scr[:, pl.ds(r0, 2 * CHUNK)] += jnp.concatenate(dkt, axis=0)
            dv_scr[:, pl.ds(r0, 2 * CHUNK)] += jnp.concatenate(dvt, axis=0)
            return carry

        lax.fori_loop(0, nb, blk, 0)
        for n in range(nb):
            rows = slice(n * CHUNK, (n + 1) * CHUNK)
            cols = slice((n + 1) * CHUNK, (n + 2) * CHUNK)
            d_ref[rows, Q_DIM:Q_DIM + KV_DIM] = dk_scr[:, cols].T.astype(BF16)
            d_ref[rows, Q_DIM + KV_DIM:] = dv_scr[:, cols].T.astype(BF16)

        @pl.when(b == n_seq - 1)
        def _():
            bkv = bk_ref[...]
            for h in range(N_HEADS):
                gs_ref[0:1, h:h + 1] = -jnp.sum(ds_scr[_head_rows(h), 0:1], axis=0, keepdims=True)
                db = dbias_scr[_head_rows(h), :]
                for bb in range(N_BUCKETS):
                    part = jnp.sum(jnp.where(bkv == bb, db, 0.0), axis=-1, keepdims=True)
                    gr_ref[bb:bb + 1, h:h + 1] = jnp.sum(part, axis=0, keepdims=True)

    smem = pl.BlockSpec(memory_space=pltpu.SMEM)
    return pl.pallas_call(
        body, name="attn_bwd", grid=(n_seq,),
        in_specs=[_row(seq, B_DIM), _row(seq, Q_DIM), _full(bk.shape), smem, smem],
        out_specs=[_row(seq, B_DIM), _full((1, N_HEADS)), _full((N_BUCKETS, N_HEADS))],
        out_shape=[_sds((n_seq * seq, B_DIM), BF16), _sds((1, N_HEADS), F32), _sds((N_BUCKETS, N_HEADS), F32)],
        scratch_shapes=[pltpu.VMEM((HEAD_ROWS, 2 * CHUNK), F32), pltpu.VMEM((HEAD_ROWS, LANES), F32),
                        pltpu.VMEM((8, seq, KV_DIM), BF16), pltpu.VMEM((HEAD_ROWS, 2 * CHUNK), F32),
                        pltpu.VMEM((KV_DIM, seq + CHUNK), F32), pltpu.VMEM((KV_DIM, seq + CHUNK), F32),
                        pltpu.VMEM((HEAD_ROWS, LANES), F32)],
        compiler_params=_cp(("arbitrary",), 40),
    )(*_hbm(proj_b, d_yb, bk), rel_bias, sinks)


def _inproj_bwd(d_g, d_a, d_b, x2, dx1, g_mix, w_in, tm, after=None):
    T = x2.shape[0]
    order = [] if after is None else [after]

    def body(*refs):
        dg_ref, da_ref, db_ref, x_ref, dx1_ref, g_ref, w_ref = refs[:7]
        gx_ref, gg_ref = refs[7 + len(order):]
        dh = (_dot_nt(dg_ref[...], w_ref[:, _G_COLS]) + _dot_nt(da_ref[...], w_ref[:, _A_COLS])
              + _dot_nt(db_ref[...], w_ref[:, _B_COLS]))
        x = x_ref[...]
        r = _rms_r(x)
        n = x * r
        gx_ref[...] = dx1_ref[...] + _rms_bwd(dh, n, r, g_ref[...])

        @pl.when(pl.program_id(0) == 0)
        def _():
            gg_ref[...] = jnp.zeros_like(gg_ref)

        gg_ref[...] += jnp.sum(dh * n, axis=0, keepdims=True)

    return pl.pallas_call(
        body, name="inproj_bwd", grid=(T // tm,),
        in_specs=[_row(tm, G_DIM), _row(tm, A_DIM), _row(tm, B_DIM), _row(tm, D_MODEL), _row(tm, D_MODEL),
                  _full(g_mix.shape), _resident(w_in.shape)] + [ANY] * len(order),
        out_specs=[_row(tm, D_MODEL), _full((1, D_MODEL))],
        out_shape=[_sds((T, D_MODEL), F32), _sds((1, D_MODEL), F32)],
        compiler_params=_cp(("arbitrary",), 48),
    )(*_hbm(d_g, d_a, d_b, x2, dx1, g_mix, w_in), *order)


IN_SHARD = (A_DIM + B_DIM + G_DIM) // N_CHIPS


def _unstack_w_in(stack):
    tr = 256

    def body(s_ref, o_ref):
        for i in range(N_CHIPS):
            o_ref[:, i * IN_SHARD:(i + 1) * IN_SHARD] = s_ref[i]

    return pl.pallas_call(
        body, name="unstack_w_in", grid=(D_MODEL // tr,),
        in_specs=[pl.BlockSpec((N_CHIPS, tr, IN_SHARD), lambda r: (0, r, 0))],
        out_specs=pl.BlockSpec((tr, N_CHIPS * IN_SHARD), lambda r: (r, 0)),
        out_shape=_sds((D_MODEL, N_CHIPS * IN_SHARD), stack.dtype),
        compiler_params=_cp(("arbitrary",)),
    )(*_hbm(stack))


def _stack_grad_w_in(gw_a, gw_b, gw_g):
    tr = 256

    def body(a_ref, b_ref, g_ref, o_ref):
        full = jnp.concatenate([a_ref[...], b_ref[...], g_ref[...]], axis=1)
        for i in range(N_CHIPS):
            o_ref[i] = full[:, i * IN_SHARD:(i + 1) * IN_SHARD]

    return pl.pallas_call(
        body, name="stack_grad_w_in", grid=(D_MODEL // tr,),
        in_specs=[_row(tr, A_DIM), _row(tr, B_DIM), _row(tr, G_DIM)],
        out_specs=pl.BlockSpec((N_CHIPS, tr, IN_SHARD), lambda r: (0, r, 0)),
        out_shape=_sds((N_CHIPS, D_MODEL, IN_SHARD), gw_a.dtype),
        compiler_params=_cp(("arbitrary",)),
    )(*_hbm(gw_a, gw_b, gw_g))


def _local_step(x, target, g_mix, g_sgu, w_s, b_s, sinks, rel_bias, g_ffn, b_conv, g_final,
                w_in, w_conv, proj_weights, ffn_weights, on_grads, after=None):
    n_seq, seq, _ = x.shape
    T = n_seq * seq
    tm = min(ROW_TILE, seq)
    tw = min(GRAD_ROW_TILE, T)
    tf = min(WIDE_ROW_TILE, seq)
    x2 = x.reshape(T, D_MODEL)
    tgt = target.reshape(T, D_MODEL)
    b_st = b_s.T
    g_fin = g_final.reshape(1, D_MODEL)

    proj_g, proj_a, proj_b, h, y_a = _inproj(x2, g_mix, w_in, g_sgu, w_s, b_st, tm, after)
    y_b = _attn_fwd(proj_b, sinks, rel_bias, n_seq, seq)
    w_pa, w_pb, w_out = proj_weights(y_b)
    x1, merged = _merge_fwd(x2, y_a, y_b, proj_g, w_pa, w_pb, w_out, tm)
    w_up, w_down = ffn_weights(x1)
    upre, h2, gate, val = _upproj(x1, g_ffn, w_up, w_conv, b_conv, tf, seq)
    dx2, loss, gg_final = _ffn_down_loss(gate, val, x1, tgt, w_down, g_fin, tm)

    d_gate, d_val, gw_down, gb_g, gb_v = _ffn_bwd_act(gate, val, dx2, w_down, tw)
    gb_conv = jnp.concatenate([gb_g, gb_v], axis=1)
    d_upre, dx1, gg_ffn, gw_conv = _ffn_bwd_up(d_gate, d_val, upre, dx2, x1, g_ffn, w_conv, w_up, tf, seq)
    gw_up = _matmul_tn(h2, d_upre, 2 * D_FF // 4, min(2 * GRAD_ROW_TILE, T), "grad_w_up")
    sent = on_grads("ffn", dict(w_up=gw_up, w_down=gw_down))
    d_g, d_a, d_yb, gw_out, gw_pa, gw_pb, gw_s, gb_st, gg_sgu = _merge_bwd(
        dx1, merged, y_a, y_b, proj_g, proj_a, w_pa, w_pb, w_out, g_sgu, w_s, b_st, tw, sent)
    sent = on_grads("proj", dict(w_pa=gw_pa, w_pb=gw_pb, w_out=gw_out))
    d_b, g_sinks, g_rel = _attn_bwd(proj_b, _tie(d_yb, sent), sinks, rel_bias, n_seq, seq)
    gw_g = _matmul_tn(h, _tie(d_g, d_b), D_MODEL, min(2 * GRAD_ROW_TILE, T), "grad_w_in_gate")
    gw_a = _matmul_tn(h, _tie(d_a, gw_g), A_DIM, min(2 * GRAD_ROW_TILE, T), "grad_w_in_a")
    gw_b = _matmul_tn(h, _tie(d_b, gw_a), B_DIM, min(2 * GRAD_ROW_TILE, T), "grad_w_in_b")
    gw_in = _stack_grad_w_in(gw_a, gw_b, gw_g)
    sent = on_grads("in", dict(w_in=gw_in))
    grad_x, gg_mix = _inproj_bwd(d_g, d_a, d_b, x2, dx1, g_mix, w_in, tm, sent)

    small = dict(g_mix=gg_mix, g_sgu=gg_sgu, w_s=gw_s, b_s=gb_st.T, sinks=g_sinks, rel_bias=g_rel,
                 g_ffn=gg_ffn, b_conv=gb_conv, g_final=gg_final, w_conv=gw_conv)
    big = dict(w_in=gw_in, w_pa=gw_pa, w_pb=gw_pb, w_out=gw_out, w_up=gw_up, w_down=gw_down)
    return loss, grad_x.reshape(x.shape), small, big


_MIXER = ("w_in", "w_pa", "w_pb", "w_out")
_FFN = ("w_up", "w_down")
_BIG = _MIXER + _FFN

CONV_ROWS = 6
_SMALL_AT = dict(loss=(0, 1, 1), g_final=(1, 1, D_MODEL), g_mix=(2, 1, D_MODEL), g_ffn=(3, 1, D_MODEL), g_sgu=(4, 1, A_WIDTH),
                 sinks=(5, 1, N_HEADS), rel_bias=(6, 1, N_BUCKETS * N_HEADS), b_s=(8, A_GROUPS, CHUNK),
                 b_conv=(12, CONV_ROWS, D_MODEL), w_conv=(18, 3 * CONV_ROWS, D_MODEL), w_s=(40, A_GROUPS * CHUNK * CHUNK // D_MODEL, D_MODEL))
_SMALL_IN_CALL = ("g_final", "g_mix", "g_ffn", "g_sgu", "sinks", "b_s")
SMALL_ROWS = 104


def _pack_small(vals):
    def wide(a):
        return jnp.pad(a, ((0, 0), (0, CONV_ROWS * D_MODEL - a.shape[1]))).reshape(-1, D_MODEL)

    laid = dict(vals, b_conv=wide(vals["b_conv"]), w_conv=wide(vals["w_conv"]), w_s=vals["w_s"].reshape(-1, D_MODEL))
    rows, at = [], 0
    for n, (r0, nr, nc) in _SMALL_AT.items():
        if r0 > at:
            rows.append(jnp.zeros((r0 - at, D_MODEL), F32))
        rows.append(jnp.pad(laid[n].astype(F32).reshape(nr, nc), ((0, 0), (0, D_MODEL - nc))))
        at = r0 + nr
    return jnp.concatenate(rows, axis=0)


def _unwide(a, r):
    return a.reshape(r, CONV_ROWS * D_MODEL)[:, :2 * D_FF]


def _mesh_pos():
    return lax.axis_index("x"), lax.axis_index("y"), lax.axis_index("c")


def _other_chips(x, y):
    return [(1 - x, y), (x, 1 - y), (1 - x, 1 - y)]


def _remote(src, dst, send_sem, recv_sem, to):
    return pltpu.make_async_remote_copy(src_ref=src, dst_ref=dst, send_sem=send_sem, recv_sem=recv_sem,
                                        device_id=to, device_id_type=MESH)


def _own_slot(own, n, at):
    return lax.dynamic_update_slice(lax.empty((n,) + own.shape, own.dtype), own[None], (at,) + (0,) * own.ndim)


def _allgather_weights(stacks, wc_stack):
    names = list(stacks)
    n = len(names)

    def body(*refs):
        ins, outs = refs[:n + 1], refs[n + 1:2 * n + 2]
        send_sems, recv_sems = refs[2 * n + 2:]
        x, y, c = _mesh_pos()
        _handshake(_chip_peers(x, y, c) + _sibling_peers(x, y, c))
        me = 2 * x + y
        sibling = (x, y, 1 - c)
        chips = _other_chips(x, y)

        def half(ref, chip, hc):
            hr = ref.shape[1] // 2
            return ref.at[chip, pl.ds(hc * hr, hr), :]

        first = []
        for k in range(n):
            first += [_remote(half(ins[k], me, c), half(outs[k], me, c), send_sems.at[6 * k + j], recv_sems.at[6 * k + j], (cx, cy, c))
                      for j, (cx, cy) in enumerate(chips)]
        first += [_remote(ins[n].at[me], outs[n].at[me], send_sems.at[6 * n + j], recv_sems.at[6 * n + j], (cx, cy, c))
                  for j, (cx, cy) in enumerate(chips)]
        for cp in first:
            cp.start()
        passed = []
        for k in range(n):
            for j, (cx, cy) in enumerate(chips):
                landed = half(outs[k], 2 * cx + cy, c)
                _remote(landed, landed, send_sems.at[6 * k + j], recv_sems.at[6 * k + j], (x, y, c)).wait_recv()
                passed.append(_remote(landed, landed, send_sems.at[6 * k + 3 + j], recv_sems.at[6 * k + 3 + j], sibling))
                passed[-1].start()
        for k in range(n):
            for j, (cx, cy) in enumerate(chips):
                theirs = half(outs[k], 2 * cx + cy, 1 - c)
                _remote(theirs, theirs, send_sems.at[6 * k + 3 + j], recv_sems.at[6 * k + 3 + j], (x, y, c)).wait_recv()
        for j, (cx, cy) in enumerate(chips):
            slot = outs[n].at[2 * cx + cy]
            _remote(slot, slot, send_sems.at[6 * n + j], recv_sems.at[6 * n + j], (x, y, c)).wait_recv()
        for cp in first + passed:
            cp.wait_send()

    arrays = [stacks[k] for k in names] + [wc_stack]
    outs = pl.pallas_call(
        body, name="allgather_weights",
        in_specs=[HBM] * (n + 1), out_specs=[HBM] * (n + 1), input_output_aliases={k: k for k in range(n + 1)},
        out_shape=[_sds(a.shape, a.dtype) for a in arrays],
        scratch_shapes=[pltpu.SemaphoreType.DMA((6 * n + 3,)), pltpu.SemaphoreType.DMA((6 * n + 3,))],
        compiler_params=pltpu.CompilerParams(collective_id=_COLLECTIVE["gather_in"]),
    )(*arrays)
    return dict(zip(names, outs[:n])), outs[n]


_KIND = {"w_in": "stack", "w_pa": "col", "w_pb": "col", "w_up": "col", "w_out": "row", "w_down": "row"}


def _half_view(ref, kind, h):
    if kind == "stack":
        k = ref.shape[1] // 2
        return ref.at[:, pl.ds(h * k, k), :]
    if kind == "col":
        k = ref.shape[0] // 2
        return ref.at[pl.ds(h * k, k), :]
    k = ref.shape[1] // 2
    return ref.at[:, pl.ds(h * k, k)]


def _shard_view(ref, kind, i):
    if kind == "stack":
        return ref.at[i]
    if kind == "col":
        k = ref.shape[1] // N_CHIPS
        return ref.at[:, pl.ds(i * k, k)]
    k = ref.shape[0] // N_CHIPS
    return ref.at[pl.ds(i * k, k), :]


def _region_view(ref, kind, h):
    if kind == "row":
        k = ref.shape[1] // 2
        return ref.at[:, pl.ds(h * k, k)]
    k = ref.shape[0] // 2
    return ref.at[pl.ds(h * k, k), :]


def _half_shape(shape, kind):
    if kind == "stack":
        return (shape[0], shape[1] // 2, shape[2])
    return (shape[0] // 2, shape[1]) if kind == "col" else (shape[0], shape[1] // 2)


def _part_shape(half_shape, kind):
    if kind == "stack":
        return tuple(half_shape[1:])
    k, w = half_shape
    return (k, w // N_CHIPS) if kind == "col" else (k // N_CHIPS, w)


_DATAFLOW = pltpu.SideEffectType.DATAFLOW_SIDE_EFFECTING
_TOKEN = (SUBLANES, LANES)


_COLLECTIVE = {k: i for i, k in enumerate(
    [kind + "_" + g for kind in ("pair", "chip", "share") for g in ("ffn", "proj", "in")]
    + ["gather_proj", "gather_ffn", "gather_in", "forward_proj", "forward_ffn"])}


def _sibling_peers(x, y, c):
    return [(x, y, 1 - c)]


def _chip_peers(x, y, c):
    return [(cx, cy, c) for cx, cy in _other_chips(x, y)]


def _handshake(peers):
    barrier = pltpu.get_barrier_semaphore()
    for peer in peers:
        pl.semaphore_signal(barrier, inc=1, device_id=peer, device_id_type=MESH)
    pl.semaphore_wait(barrier, len(peers))


def _split_start(name, arrays, n_sems, issue, after=None, handshake=None):
    n = len(arrays)
    order = [] if after is None else [after]

    def body(*refs):
        base = n + len(order)
        if handshake is not None:
            _handshake(handshake[1](*_mesh_pos()))
        issue(refs[:n], refs[base], refs[base + 1])
        refs[-1][...] = jnp.zeros(_TOKEN, F32)

    params = dict(has_side_effects=_DATAFLOW)
    if handshake is not None:
        params["collective_id"] = handshake[0]
    outs = pl.pallas_call(
        body, name=name,
        in_specs=[HBM] * n + [ANY] * len(order), out_specs=[SEM, SEM] + [HBM] * n + [pl.BlockSpec(memory_space=pltpu.VMEM)],
        out_shape=[pltpu.SemaphoreType.DMA((n_sems,)), pltpu.SemaphoreType.DMA((n_sems,))]
        + [pltpu.HBM(a.shape, a.dtype) for a in arrays] + [_sds(_TOKEN, F32)],
        input_output_aliases={k: 2 + k for k in range(n)},
        compiler_params=pltpu.CompilerParams(**params),
    )(*[pltpu.with_memory_space_constraint(a, pltpu.HBM) for a in arrays], *order)
    return outs[0], outs[1], list(outs[2:2 + n]), outs[-1]


def _split_wait(name, started, waits, after):
    send_sems, recv_sems, arrays, _ = started
    n = len(arrays)

    def body(*refs):
        waits(refs[:n], refs[n], refs[n + 1])

    return pl.pallas_call(
        body, name=name,
        in_specs=[HBM] * n + [SEM, SEM, ANY], out_specs=[HBM] * n,
        out_shape=[pltpu.HBM(a.shape, a.dtype) for a in arrays],
        input_output_aliases={k: k for k in range(n)},
        compiler_params=pltpu.CompilerParams(has_side_effects=_DATAFLOW),
    )(*arrays, send_sems, recv_sems, after)


def _wait_both(src, dst, send_sem, recv_sem):
    x, y, c = _mesh_pos()
    cp = _remote(src, dst, send_sem, recv_sem, (x, y, c))
    cp.wait_send()
    cp.wait_recv()


def _pair_exchange_start(parts, tag, after):
    names = list(parts)
    n = len(names)
    lands = [lax.empty(_half_shape(parts[k].shape, _KIND[k]), parts[k].dtype) for k in names]

    def issue(refs, send_sems, recv_sems):
        x, y, c = _mesh_pos()
        for hc in range(2):
            @pl.when(c == hc)
            def _():
                for k in range(n):
                    _remote(_half_view(refs[k], _KIND[names[k]], 1 - hc), refs[n + k], send_sems.at[k], recv_sems.at[k],
                            (x, y, 1 - c)).start()

    return names, _split_start("grad_pair_exchange_start_" + tag, [parts[k] for k in names] + lands, n, issue, after,
                               (_COLLECTIVE["pair_" + tag], _sibling_peers))


def _pair_exchange_wait(pending, tag, after):
    names, started = pending
    n = len(names)

    def waits(refs, send_sems, recv_sems):
        for k in range(n):
            _wait_both(_half_view(refs[k], _KIND[names[k]], 0), refs[n + k], send_sems.at[k], recv_sems.at[k])

    outs = _split_wait("grad_pair_exchange_wait_" + tag, started, waits, after)
    return dict(zip(names, outs[:n])), dict(zip(names, outs[n:]))


def _half_blocks(shape, kind):
    if kind == "stack":
        _, k, w = shape
        tr = k // 2
        nb = 1
        return (N_CHIPS, nb), (1, tr, w), (lambda i, r, s: (i, r, 0)), (lambda i, r, s: (i, s[1] * nb + r, 0))
    k, w = shape
    if kind == "col":
        tr = 256
        nb = k // 2 // tr
        return (nb,), (tr, w), (lambda r, s: (r, 0)), (lambda r, s: (s[1] * nb + r, 0))
    tr = k // N_CHIPS
    return (N_CHIPS,), (tr, w // 2), (lambda r, s: (r, 0)), (lambda r, s: (r, s[1]))


def _pair_add(part, from_sibling, name, pos):
    kind = _KIND[name]
    grid, block, half_map, full_map = _half_blocks(part.shape, kind)

    def body(s_ref, p_ref, q_ref, o_ref):
        o_ref[...] = (p_ref[...].astype(F32) + q_ref[...].astype(F32)).astype(BF16)

    return pl.pallas_call(
        body, name="grad_pair_add_" + name,
        grid_spec=pltpu.PrefetchScalarGridSpec(
            num_scalar_prefetch=1, grid=grid,
            in_specs=[pl.BlockSpec(block, full_map), pl.BlockSpec(block, half_map)],
            out_specs=pl.BlockSpec(block, half_map)),
        out_shape=_sds(from_sibling.shape, BF16),
        compiler_params=_cp(("arbitrary",) * len(grid), 40),
    )(pos, *_hbm(part, from_sibling))


def _chip_exchange_start(sums, tag, after):
    names = list(sums)
    n = len(names)
    lands = [lax.empty((3,) + _part_shape(sums[k].shape, _KIND[k]), sums[k].dtype) for k in names]

    def issue(refs, send_sems, recv_sems):
        x, y, c = _mesh_pos()
        me = 2 * x + y
        for i in range(N_CHIPS):
            xi, yi = i // 2, i % 2
            j = jnp.where(xi != x, jnp.where(yi != y, 2, 0), 1)

            @pl.when(i != me)
            def _():
                for k in range(n):
                    _remote(_shard_view(refs[k], _KIND[names[k]], i), refs[n + k].at[j], send_sems.at[3 * k + j],
                            recv_sems.at[3 * k + j], (xi, yi, c)).start()

    return names, _split_start("grad_chip_exchange_start_" + tag, [sums[k] for k in names] + lands, 3 * n, issue, after,
                               (_COLLECTIVE["chip_" + tag], _chip_peers))


def _chip_exchange_wait(pending, tag, after):
    names, started = pending
    n = len(names)

    def waits(refs, send_sems, recv_sems):
        for k in range(n):
            for j in range(3):
                _wait_both(_shard_view(refs[k], _KIND[names[k]], 0), refs[n + k].at[j], send_sems.at[3 * k + j], recv_sems.at[3 * k + j])

    return dict(zip(names, _split_wait("grad_chip_exchange_wait_" + tag, started, waits, after)[n:]))


def _allgather_start(stacks, tag, after):
    names = list(stacks)

    def issue(refs, send_sems, recv_sems):
        x, y, c = _mesh_pos()
        me = 2 * x + y
        for k, st in enumerate(refs):
            hr = st.shape[1] // 2
            mine = st.at[me, pl.ds(c * hr, hr), :]
            for j, (cx, cy) in enumerate(_other_chips(x, y)):
                _remote(mine, mine, send_sems.at[3 * k + j], recv_sems.at[3 * k + j], (cx, cy, c)).start()

    return names, _split_start("allgather_start_" + tag, [stacks[k] for k in names], 3 * len(names), issue, after,
                               (_COLLECTIVE["gather_" + tag], _chip_peers))


def _allgather_wait(pending, tag, after):
    names, started = pending

    def waits(refs, send_sems, recv_sems):
        for k, st in enumerate(refs):
            slot = st.at[0, pl.ds(0, st.shape[1] // 2), :]
            for j in range(3):
                _wait_both(slot, slot, send_sems.at[3 * k + j], recv_sems.at[3 * k + j])

    return dict(zip(names, _split_wait("allgather_wait_" + tag, started, waits, after)))


def _allgather_forward(stacks, tag):
    names = list(stacks)
    n = len(names)

    def body(*refs):
        ins, outs = refs[:n], refs[n:2 * n]
        send_sems, recv_sems = refs[2 * n:]
        x, y, c = _mesh_pos()
        _handshake(_sibling_peers(x, y, c))
        copies = []
        for k in range(n):
            hr = ins[k].shape[1] // 2
            for j, (cx, cy) in enumerate(_other_chips(x, y)):
                chip = 2 * cx + cy
                copies.append(_remote(ins[k].at[chip, pl.ds(c * hr, hr), :], outs[k].at[chip, pl.ds(c * hr, hr), :],
                                      send_sems.at[3 * k + j], recv_sems.at[3 * k + j], (x, y, 1 - c)))
        for cp in copies:
            cp.start()
        for cp in copies:
            cp.wait()

    arrays = [stacks[k] for k in names]
    outs = pl.pallas_call(
        body, name="allgather_forward_" + tag, in_specs=[HBM] * n, out_specs=[HBM] * n,
        input_output_aliases={k: k for k in range(n)},
        out_shape=[_sds(a.shape, a.dtype) for a in arrays],
        scratch_shapes=[pltpu.SemaphoreType.DMA((3 * n,)), pltpu.SemaphoreType.DMA((3 * n,))],
        compiler_params=pltpu.CompilerParams(collective_id=_COLLECTIVE["forward_" + tag]),
    )(*arrays)
    return dict(zip(names, outs))


def _owner_sum(part, from_sibling, from_chips, name, pos, shard_shape):
    kind = _KIND[name]
    _, pk, pw = from_chips.shape
    if kind == "row":
        tr, nb = pk, 1
        p_spec = pl.BlockSpec((tr, pw), lambda r, s: (s[0], s[1]))
        q_spec = pl.BlockSpec((tr, pw), lambda r, s: (s[0], 0))
        o_spec = pl.BlockSpec((tr, pw), lambda r, s: (0, s[1]))
    else:
        tr = 256
        nb = pk // tr
        if kind == "stack":
            p_spec = pl.BlockSpec((None, tr, pw), lambda r, s: (s[0], s[1] * nb + r, 0))
            q_spec = pl.BlockSpec((None, tr, pw), lambda r, s: (s[0], r, 0))
        else:
            p_spec = pl.BlockSpec((tr, pw), lambda r, s: (s[1] * nb + r, s[0]))
            q_spec = pl.BlockSpec((tr, pw), lambda r, s: (r, s[0]))
        o_spec = pl.BlockSpec((tr, pw), lambda r, s: (s[1] * nb + r, 0))

    def body(s_ref, p_ref, q_ref, r_ref, o_ref):
        acc = p_ref[...].astype(F32) + q_ref[...].astype(F32)
        for j in range(3):
            acc = acc + r_ref[j].astype(F32)
        o_ref[...] = acc

    return pl.pallas_call(
        body, name="grad_owner_sum_" + name,
        grid_spec=pltpu.PrefetchScalarGridSpec(
            num_scalar_prefetch=1, grid=(nb,),
            in_specs=[p_spec, q_spec, pl.BlockSpec((3, tr, pw), lambda r, s: (0, r, 0))],
            out_specs=o_spec),
        out_shape=_sds(shard_shape, F32),
        compiler_params=_cp(("arbitrary",), 32),
    )(pos, *_hbm(part, from_sibling, from_chips))


def _pair_share_start(shards, tag, after):
    names = list(shards)

    def issue(refs, send_sems, recv_sems):
        x, y, c = _mesh_pos()
        for hc in range(2):
            @pl.when(c == hc)
            def _():
                for k, g in enumerate(refs):
                    mine = _region_view(g, _KIND[names[k]], hc)
                    _remote(mine, mine, send_sems.at[k], recv_sems.at[k], (x, y, 1 - c)).start()

    return names, _split_start("grad_pair_share_start_" + tag, [shards[k] for k in names], len(names), issue, after,
                               (_COLLECTIVE["share_" + tag], _sibling_peers))


def _pair_share_wait(pending, tag, after):
    names, started = pending

    def waits(refs, send_sems, recv_sems):
        for k, g in enumerate(refs):
            region = _region_view(g, _KIND[names[k]], 0)
            _wait_both(region, region, send_sems.at[k], recv_sems.at[k])

    return dict(zip(names, _split_wait("grad_pair_share_wait_" + tag, started, waits, after)))


def _small_exchange_start(slots, after):
    def issue(refs, send_sems, recv_sems):
        x, y, c = _mesh_pos()
        mine = refs[0].at[4 * x + 2 * y + c]
        k = 0
        for px in range(2):
            for py in range(2):
                for pc in range(2):
                    if px + py + pc:
                        peer = (1 - x if px else x, 1 - y if py else y, 1 - c if pc else c)
                        _remote(mine, mine, send_sems.at[k], recv_sems.at[k], peer).start()
                        k += 1

    return _split_start("small_exchange_start", [slots], N_DEV - 1, issue, after)


def _small_exchange_wait(started, after):
    def waits(refs, send_sems, recv_sems):
        slot = refs[0].at[0]
        for k in range(N_DEV - 1):
            _wait_both(slot, slot, send_sems.at[k], recv_sems.at[k])

    return _split_wait("small_exchange_wait", started, waits, after)[0]


def _adam_math(w, g, m, v):
    m = ADAM_B1 * m + (1.0 - ADAM_B1) * g
    v = ADAM_B2 * v + (1.0 - ADAM_B2) * (g * g)
    m_hat = m / (1.0 - ADAM_B1 ** ADAM_STEP)
    v_hat = v / (1.0 - ADAM_B2 ** ADAM_STEP)
    delta = -ADAM_LR * (m_hat / (jnp.sqrt(v_hat) + ADAM_EPS) + ADAM_WD * w)
    return delta, m, v


def _adamw(w, g, m, v, name):
    rows, cols = w.shape
    fits = [t for t in range(SUBLANES, rows, SUBLANES) if rows % t == 0 and t * cols * 4 <= (3 << 19)]
    tr = max(fits) if fits else rows

    def body(w_ref, g_ref, m_ref, v_ref, d_ref, nm_ref, nv_ref, go_ref):
        g = g_ref[...]
        d, nm, nv = _adam_math(w_ref[...], g, m_ref[...], v_ref[...])
        d_ref[...] = d
        nm_ref[...] = nm
        nv_ref[...] = nv
        go_ref[...] = g

    spec = pl.BlockSpec((tr, cols), lambda i: (i, 0))
    return pl.pallas_call(
        body, name=name, grid=(rows // tr,), in_specs=[spec] * 4, out_specs=[spec] * 4,
        out_shape=[_sds(w.shape, F32)] * 4, compiler_params=_cp(("arbitrary",)),
    )(*_hbm(w, g, m, v))


def _small_sum_adamw(gathered, w, m, v):
    names = _SMALL_IN_CALL
    n = len(names)

    def body(*refs):
        a_ref = refs[0]
        w_refs, m_refs, v_refs = refs[1:1 + n], refs[1 + n:1 + 2 * n], refs[1 + 2 * n:1 + 3 * n]
        sum_ref = refs[1 + 3 * n]
        outs = refs[2 + 3 * n:]
        g = a_ref[0]
        for k in range(1, N_DEV):
            g = g + a_ref[k]
        sum_ref[...] = g
        for i, name in enumerate(names):
            r0, nr, nc = _SMALL_AT[name]
            gp = g[r0:r0 + nr, 0:nc]
            d, nm, nv = _adam_math(w_refs[i][...], gp, m_refs[i][...], v_refs[i][...])
            for k, val in enumerate((gp, d, nm, nv)):
                outs[4 * i + k][...] = val

    shapes = [w[k].shape for k in names]
    res = pl.pallas_call(
        body, name="small_sum_adamw",
        out_shape=[_sds((SMALL_ROWS, D_MODEL), F32)] + [_sds(s, F32) for s in shapes for _ in range(4)],
    )(gathered, *[w[k] for k in names], *[m[k] for k in names], *[v[k] for k in names])
    return res[0], {k: tuple(res[1 + 4 * i:5 + 4 * i]) for i, k in enumerate(names)}


_NAMES = ("g_mix", "w_in", "g_sgu", "w_s", "b_s", "sinks", "rel_bias", "w_pa", "w_pb", "w_out",
          "g_ffn", "w_up", "w_conv", "b_conv", "w_down", "g_final")

def kernel(x, g_mix, w_in, g_sgu, w_s, b_s, sinks, rel_bias, w_pa, w_pb, w_out, g_ffn, w_up, w_conv, b_conv, w_down, g_final, loss_target, m_g_mix, m_w_in, m_g_sgu, m_w_s, m_b_s, m_sinks, m_rel_bias, m_w_pa, m_w_pb, m_w_out, m_g_ffn, m_w_up, m_w_conv, m_b_conv, m_w_down, m_g_final, v_g_mix, v_w_in, v_g_sgu, v_w_s, v_b_s, v_sinks, v_rel_bias, v_w_pa, v_w_pb, v_w_out, v_g_ffn, v_w_up, v_w_conv, v_b_conv, v_w_down, v_g_final):
    w = dict(g_mix=g_mix, w_in=w_in, g_sgu=g_sgu, w_s=w_s, b_s=b_s, sinks=sinks, rel_bias=rel_bias, w_pa=w_pa, w_pb=w_pb,
             w_out=w_out, g_ffn=g_ffn, w_up=w_up, w_conv=w_conv, b_conv=b_conv, w_down=w_down, g_final=g_final)
    m = dict(g_mix=m_g_mix, w_in=m_w_in, g_sgu=m_g_sgu, w_s=m_w_s, b_s=m_b_s, sinks=m_sinks, rel_bias=m_rel_bias, w_pa=m_w_pa,
             w_pb=m_w_pb, w_out=m_w_out, g_ffn=m_g_ffn, w_up=m_w_up, w_conv=m_w_conv, b_conv=m_b_conv, w_down=m_w_down,
             g_final=m_g_final)
    v = dict(g_mix=v_g_mix, w_in=v_w_in, g_sgu=v_g_sgu, w_s=v_w_s, b_s=v_b_s, sinks=v_sinks, rel_bias=v_rel_bias, w_pa=v_w_pa,
             w_pb=v_w_pb, w_out=v_w_out, g_ffn=v_g_ffn, w_up=v_w_up, w_conv=v_w_conv, b_conv=v_b_conv, w_down=v_w_down,
             g_final=v_g_final)
    xi, yi, ci = _mesh_pos()
    me = 2 * xi + yi

    shard = {n: w[n][0] for n in _BIG}
    shard_shapes = {n: shard[n].shape for n in _BIG}
    wc_shard = w["w_conv"][0]
    wc_pad = jnp.pad(wc_shard, ((0, 5), (0, 0)))
    own = {n: _own_slot(shard[n].astype(BF16), N_CHIPS, me) for n in _BIG}
    stacks, wc_all = _allgather_weights({"w_in": own["w_in"]}, _own_slot(wc_pad, N_CHIPS, me))
    proj_gather = _allgather_start({n: own[n] for n in _MIXER[1:]}, "proj", stacks["w_in"])
    ffn_gather = _allgather_start({n: own[n] for n in _FFN}, "ffn", proj_gather[1][-1])
    w_conv_full = jnp.concatenate([wc_all[i, :3] for i in range(N_CHIPS)], axis=1)
    w_in_full = _unstack_w_in(stacks["w_in"])
    pos = jnp.stack([me, ci])

    def proj_weights(done):
        st = _allgather_forward(_allgather_wait(proj_gather, "proj", done), "proj")
        return st["w_pa"], st["w_pb"], st["w_out"].reshape(D_MODEL, D_MODEL)

    def ffn_weights(done):
        st = _allgather_forward(_allgather_wait(ffn_gather, "ffn", done), "ffn")
        return st["w_up"], st["w_down"].reshape(D_FF, D_MODEL)

    groups = {}

    def stage1(group, parts):
        groups[group] = dict(parts=parts, pair=_pair_exchange_start(parts, group, None))
        return groups[group]["pair"][1][-1]

    def stage2(group, after, order_after):
        g = groups[group]
        g["parts"], g["sib"] = _pair_exchange_wait(g["pair"], group, after)
        g["chip"] = _chip_exchange_start({n: _pair_add(g["parts"][n], g["sib"][n], n, pos) for n in g["parts"]}, group, order_after)
        return g["chip"][1][-1]

    def stage3(group, after, order_after):
        g = groups[group]
        got = _chip_exchange_wait(g["chip"], group, after)
        g["share"] = _pair_share_start(
            {n: _owner_sum(g["parts"][n], g["sib"][n], got[n], n, pos, shard_shapes[n]) for n in g["parts"]}, group, order_after)
        return g["share"][1][-1]

    grads, deltas, new_m, new_v = {}, {}, {}, {}

    def stage4(group, after):
        g_shard = _pair_share_wait(groups[group]["share"], group, after)
        last = None
        for n in g_shard:
            g = _tie(g_shard[n], last)
            if n == "w_in":
                d, nm, nv, gt = _adamw(shard[n].T, g.T, m[n][0].T, v[n][0].T, "adamw_" + n)
                grads[n], deltas[n], new_m[n], new_v[n] = gt.T[None], d.T[None], nm.T[None], nv.T[None]
            else:
                d, nm, nv, go = _adamw(shard[n], g, m[n][0], v[n][0], "adamw_" + n)
                grads[n], deltas[n], new_m[n], new_v[n] = go[None], d[None], nm[None], nv[None]
            last = nv
        return last

    def on_grads(group, parts):
        token = stage1(group, parts)
        some = next(iter(parts.values()))
        if group == "proj":
            token = stage2("ffn", some, token)
        if group == "in":
            token = stage2("proj", some, token)
            token = stage3("ffn", some, token)
            token = stage2("in", token, token)
        return token

    loss, grad_x, small, big = _local_step(
        x, loss_target, w["g_mix"], w["g_sgu"], w["w_s"][0], w["b_s"][0], w["sinks"], w["rel_bias"], w["g_ffn"],
        w["b_conv"], w["g_final"], w_in_full, w_conv_full, proj_weights, ffn_weights, on_grads, ffn_gather[1][-1])

    small["loss"] = loss
    small_gather = _small_exchange_start(_own_slot(_pack_small(small), N_DEV, 2 * me + ci), grad_x)
    token = stage3("proj", grad_x, small_gather[-1])
    done = stage4("ffn", token)
    done = stage4("proj", done)
    token = stage3("in", done, None)
    all_small = _small_exchange_wait(small_gather, token)
    two_d = {n: (lambda a, n=n: a.reshape(_SMALL_AT[n][1:])) for n in _SMALL_IN_CALL}
    s_sum, s_out = _small_sum_adamw(all_small, *[{n: two_d[n](p[n]) for n in _SMALL_IN_CALL} for p in (w, m, v)])
    stage4("in", all_small)
    for n in _SMALL_IN_CALL:
        grads[n], deltas[n], new_m[n], new_v[n] = [a.reshape(w[n].shape) for a in s_out[n]]

    def rows(n):
        r0, nr, _ = _SMALL_AT[n]
        return s_sum[r0:r0 + nr]

    wcols = wc_shard.shape[1]
    g_wc = lax.dynamic_slice(_unwide(rows("w_conv"), 3), (0, me * wcols), (3, wcols))
    d, nm, nv, _ = _adamw(wc_shard, g_wc, m["w_conv"][0], v["w_conv"][0], "adamw_w_conv")
    grads["w_conv"], deltas["w_conv"], new_m["w_conv"], new_v["w_conv"] = g_wc[None], d[None], nm[None], nv[None]
    d, nm, nv, go = _adamw(w["b_conv"], _unwide(rows("b_conv"), 1), m["b_conv"], v["b_conv"], "adamw_b_conv")
    grads["b_conv"], deltas["b_conv"], new_m["b_conv"], new_v["b_conv"] = go, d, nm, nv
    g_rb = rows("rel_bias")[:, :N_BUCKETS * N_HEADS].reshape(N_BUCKETS, N_HEADS)
    d, nm, nv, go = _adamw(w["rel_bias"], g_rb, m["rel_bias"], v["rel_bias"], "adamw_rel_bias")
    grads["rel_bias"], deltas["rel_bias"], new_m["rel_bias"], new_v["rel_bias"] = go, d, nm, nv
    flat_s = (A_GROUPS * CHUNK, CHUNK)
    d, nm, nv, go = _adamw(w["w_s"].reshape(flat_s), rows("w_s").reshape(flat_s), m["w_s"].reshape(flat_s),
                           v["w_s"].reshape(flat_s), "adamw_w_s")
    grads["w_s"], deltas["w_s"], new_m["w_s"], new_v["w_s"] = [a.reshape(w["w_s"].shape) for a in (go, d, nm, nv)]

    return (s_sum[0, 0], grad_x, *[grads[n] for n in _NAMES], *[deltas[n] for n in _NAMES],
            *[new_m[n] for n in _NAMES], *[new_v[n] for n in _NAMES])
```

```python
import functools

import numpy as np
import jax
import jax.numpy as jnp
from jax import lax
from jax.experimental import pallas as pl
from jax.experimental.pallas import tpu as pltpu

F32 = jnp.float32
BF16 = jnp.bfloat16

D_MODEL = 1024
CHUNK = 128
A_GROUPS = 4
A_WIDTH = 512
N_HEADS = 8
HEAD_DIM = 64
Q_DIM = 512
KV_DIM = 128
N_BUCKETS = 32
MAX_DISTANCE = 128
D_FF = 2816
EPS = 1e-6
NEG_INF = -1e30
G_DIM = 2 * D_MODEL
A_DIM = 2 * A_WIDTH
B_DIM = Q_DIM + 2 * KV_DIM
LANES = 128
SUBLANES = 8
ROW_TILE = 512
WIDE_ROW_TILE = 256
COL_CHUNK = 512
GRAD_ROW_TILE = 512
BF16_ROWS = 16
N_CHIPS = 4
N_DEV = 8

ADAM_LR = 0.001
ADAM_B1 = 0.9
ADAM_B2 = 0.999
ADAM_EPS = 1e-08
ADAM_WD = 0.01
ADAM_STEP = 10

MESH = pl.DeviceIdType.MESH
_GELU_C = 0.7978845608028654
_GELU_A = 0.044715


def _cp(sem=None, vmem_mb=None):
    kw = {}
    if sem is not None:
        kw["dimension_semantics"] = sem
    if vmem_mb is not None:
        kw["vmem_limit_bytes"] = vmem_mb << 20
    return pltpu.CompilerParams(**kw)


def _dot(a, b):
    return jnp.dot(a, b, preferred_element_type=F32)


def _dot_nt(a, b):
    return lax.dot_general(a, b, (((1,), (1,)), ((), ())), preferred_element_type=F32)


def _dot_tn(a, b):
    return lax.dot_general(a, b, (((0,), (0,)), ((), ())), preferred_element_type=F32)


def _rms_r(x):
    return lax.rsqrt(jnp.mean(x * x, axis=-1, keepdims=True) + EPS)


def _rms_bwd(dh, n, r, g):
    dn = dh * g
    return r * (dn - n * jnp.mean(dn * n, axis=-1, keepdims=True))


def _gelu(x):
    t = jnp.tanh(_GELU_C * (x + _GELU_A * (x * x * x)))
    return 0.5 * x * (1.0 + t), t


def _gelu_grad(x, t):
    return 0.5 * (1.0 + t) + 0.5 * x * (1.0 - t * t) * (_GELU_C * (1.0 + 3.0 * _GELU_A * x * x))


def _sigmoid(x):
    return 1.0 / (1.0 + jnp.exp(-x))


def _tie(x, dep):
    return x if dep is None else lax.optimization_barrier((x, dep))[0]


def _row(tm, w):
    return pl.BlockSpec((tm, w), lambda i: (i, 0))


def _full(shape):
    nd = len(shape)
    return pl.BlockSpec(tuple(shape), lambda *_: (0,) * nd)


def _resident(shape):
    nd = len(shape)
    return pl.BlockSpec(tuple(shape), lambda *_: (0,) * nd, pipeline_mode=pl.Buffered(1))


def _sds(shape, dtype):
    return jax.ShapeDtypeStruct(tuple(shape), dtype)


def _hbm(*arrays):
    return [pltpu.with_memory_space_constraint(a, pltpu.HBM) for a in arrays]


HBM = pl.BlockSpec(memory_space=pltpu.HBM)
ANY = pl.BlockSpec(memory_space=pl.ANY)
SEM = pl.BlockSpec(memory_space=pltpu.SEMAPHORE)


def _band_buckets():
    i = np.arange(CHUNK)[:, None]
    j = np.arange(2 * CHUNK)[None, :]
    dist = i + CHUNK - j
    valid = (dist >= 0) & (dist < CHUNK)
    d = np.clip(dist, 0, None)
    max_exact = N_BUCKETS // 2
    large = max_exact + (np.log(np.maximum(d, 1) / max_exact) / np.log(MAX_DISTANCE / max_exact)
                         * (N_BUCKETS - max_exact)).astype(np.int32)
    large = np.minimum(large, N_BUCKETS - 1)
    buckets = np.where(d < max_exact, d, large).astype(np.int32)
    return np.where(valid, buckets, -1).astype(np.int32)


_A_COLS = slice(0, A_DIM)
_B_COLS = slice(A_DIM, A_DIM + B_DIM)
_G_COLS = slice(A_DIM + B_DIM, A_DIM + B_DIM + G_DIM)


def _inproj(x2, g_mix, w_in, g_sgu, w_s, b_st, tm, after=None):
    T = x2.shape[0]
    order = [] if after is None else [after]

    def body(*refs):
        x_ref, g_ref, w_ref, gs_ref, ws_ref, bs_ref = refs[:6]
        pg_ref, pa_ref, pb_ref, h_ref, ya_ref = refs[6 + len(order):]
        x = x_ref[...]
        h = (x * _rms_r(x) * g_ref[...]).astype(BF16)
        h_ref[...] = h
        pa = _dot(h, w_ref[:, _A_COLS]).astype(BF16)
        pa_ref[...] = pa
        pb_ref[...] = _dot(h, w_ref[:, _B_COLS]).astype(BF16)
        pg_ref[...] = _dot(h, w_ref[:, _G_COLS]).astype(BF16)
        _sgu_apply(pa.astype(F32), gs_ref[...], ws_ref, bs_ref, ya_ref)

    return pl.pallas_call(
        body, name="inproj", grid=(T // tm,),
        in_specs=[_row(tm, D_MODEL), _full(g_mix.shape), _resident(w_in.shape), _full(g_sgu.shape), _full(w_s.shape),
                  _full(b_st.shape)] + [ANY] * len(order),
        out_specs=[_row(tm, G_DIM), _row(tm, A_DIM), _row(tm, B_DIM), _row(tm, D_MODEL), _row(tm, A_WIDTH)],
        out_shape=[_sds((T, G_DIM), BF16), _sds((T, A_DIM), BF16), _sds((T, B_DIM), BF16), _sds((T, D_MODEL), BF16),
                   _sds((T, A_WIDTH), BF16)],
        compiler_params=_cp(("arbitrary",), 48),
    )(*_hbm(x2, g_mix, w_in, g_sgu, w_s, b_st), *order)


def _sgu_parts(p, g):
    pu = p[:, :A_WIDTH]
    pv = p[:, A_WIDTH:]
    u, tu = _gelu(pu)
    vv, tv = _gelu(pv)
    rv = _rms_r(vv)
    vn = (vv * rv * g).astype(BF16)
    return pu, pv, u, tu, vv, tv, rv, vn


def _tril():
    r = lax.broadcasted_iota(jnp.int32, (CHUNK, CHUNK), 0)
    c = lax.broadcasted_iota(jnp.int32, (CHUNK, CHUNK), 1)
    return r >= c


def _sgu_apply(p, g, ws_ref, bs_ref, y_ref):
    tril = _tril()
    _, _, u, _, _, _, _, vn = _sgu_parts(p, g)
    for gi in range(A_GROUPS):
        wm = jnp.where(tril, ws_ref[gi], 0.0).astype(BF16)
        bcol = bs_ref[:, gi:gi + 1]
        cs = slice(gi * CHUNK, (gi + 1) * CHUNK)
        for c in range(p.shape[0] // CHUNK):
            rs = slice(c * CHUNK, (c + 1) * CHUNK)
            s = _dot(wm, vn[rs, cs]) + bcol
            y_ref[rs, cs] = (u[rs, cs] * s).astype(BF16)


HEAD_ROWS = N_HEADS * CHUNK


def _head_rows(h):
    return slice(h * CHUNK, (h + 1) * CHUNK)


def _attn_setup(bias_scr, sink_scr, kvar_scr, qkv_ref, bk_ref, rel_ref, sink_ref):
    @pl.when(pl.program_id(0) == 0)
    def _():
        bk = bk_ref[...]
        for h in range(N_HEADS):
            acc = jnp.full((CHUNK, 2 * CHUNK), NEG_INF, F32)
            for b in range(N_BUCKETS):
                acc = jnp.where(bk == b, rel_ref[b, h], acc)
            bias_scr[_head_rows(h), :] = acc
            sink_scr[_head_rows(h), :] = jnp.full((CHUNK, LANES), sink_ref[0, h], F32)

    seq = qkv_ref.shape[0]
    rows_per = 2 * CHUNK
    for is_v in range(2):
        c0 = Q_DIM + is_v * KV_DIM
        for r in range(seq // rows_per):
            rs = slice(r * rows_per, (r + 1) * rows_per)
            a = qkv_ref[rs, c0:c0 + KV_DIM].astype(F32)
            lane = lax.broadcasted_iota(jnp.int32, a.shape, 1)
            lo = jnp.where(lane < HEAD_DIM, a, 0.0)
            hi = jnp.where(lane >= HEAD_DIM, a, 0.0)
            kvar_scr[4 * is_v + 0, rs, :] = lo.astype(BF16)
            kvar_scr[4 * is_v + 1, rs, :] = pltpu.roll(lo, HEAD_DIM, 1).astype(BF16)
            kvar_scr[4 * is_v + 2, rs, :] = pltpu.roll(hi, HEAD_DIM, 1).astype(BF16)
            kvar_scr[4 * is_v + 3, rs, :] = hi.astype(BF16)


def _rowsum(a, ones):
    hi = a.astype(BF16)
    lo = (a - hi.astype(F32)).astype(BF16)
    return _dot(hi, ones) + _dot(lo, ones)


def _both(a):
    return jnp.concatenate([a, a], axis=1)


def _attn_probs(qkv_ref, r0, n, kv, bias_scr, sink_scr, ones):
    s = jnp.concatenate([_dot_nt(qkv_ref[pl.ds(r0, CHUNK), (h // 2) * LANES:(h // 2 + 1) * LANES], kv[h // 4][h % 2])
                         for h in range(N_HEADS)], axis=0)
    s = s * (HEAD_DIM ** -0.5) + bias_scr[...]
    col = lax.broadcasted_iota(jnp.int32, s.shape, 1)
    s = jnp.where((col < CHUNK) & (n == 0), NEG_INF, s)
    sink = sink_scr[...]
    m = jnp.maximum(jnp.max(s, axis=-1, keepdims=True), sink)
    p = jnp.exp(s - _both(m))
    es = jnp.exp(sink - m)
    inv = 1.0 / (_dot(p.astype(BF16), ones) + es)
    return p * _both(inv), es * inv


def _attn_block_inputs(kvar_scr, n):
    r0 = pl.multiple_of(n * CHUNK, CHUNK)
    rp = pl.multiple_of(jnp.maximum(n - 1, 0) * CHUNK, CHUNK)

    def both(idx):
        return jnp.concatenate([kvar_scr[idx, pl.ds(rp, CHUNK), :], kvar_scr[idx, pl.ds(r0, CHUNK), :]], axis=0)

    kv = ((both(0), both(1)), (both(2), both(3)))
    vv = ((both(4), both(5)), (both(6), both(7)))
    return r0, kv, vv


def _attn_fwd(proj_b, sinks, rel_bias, n_seq, seq):
    nb = seq // CHUNK
    bk = jnp.asarray(_band_buckets())

    def body(qkv_ref, bk_ref, rel_ref, sink_ref, o_ref, bias_scr, sink_scr, kvar_scr):
        _attn_setup(bias_scr, sink_scr, kvar_scr, qkv_ref, bk_ref, rel_ref, sink_ref)
        ones = jnp.ones((2 * CHUNK, LANES), BF16)

        def blk(n, carry):
            r0, kv, vv = _attn_block_inputs(kvar_scr, n)
            prob, _ = _attn_probs(qkv_ref, r0, n, kv, bias_scr, sink_scr, ones)
            pb = prob.astype(BF16)
            for pr in range(N_HEADS // 2):
                acc = _dot(pb[_head_rows(2 * pr)], vv[pr // 2][0]) + _dot(pb[_head_rows(2 * pr + 1)], vv[pr // 2][1])
                o_ref[pl.ds(r0, CHUNK), pr * LANES:(pr + 1) * LANES] = acc.astype(BF16)
            return carry

        lax.fori_loop(0, nb, blk, 0)

    smem = pl.BlockSpec(memory_space=pltpu.SMEM)
    return pl.pallas_call(
        body, name="attn_fwd", grid=(n_seq,),
        in_specs=[_row(seq, B_DIM), _full(bk.shape), smem, smem],
        out_specs=_row(seq, Q_DIM), out_shape=_sds((n_seq * seq, Q_DIM), BF16),
        scratch_shapes=[pltpu.VMEM((HEAD_ROWS, 2 * CHUNK), F32), pltpu.VMEM((HEAD_ROWS, LANES), F32),
                        pltpu.VMEM((8, seq, KV_DIM), BF16)],
        compiler_params=_cp(("arbitrary",), 40),
    )(*_hbm(proj_b, bk), rel_bias, sinks)


def _dot_stacked(a, w_ref):
    return jnp.concatenate([_dot(a, w_ref[i]) for i in range(N_CHIPS)], axis=1)


def _dot_nt_stacked(a, w_ref):
    w = w_ref.shape[2]
    acc = _dot_nt(a[:, :w], w_ref[0])
    for i in range(1, N_CHIPS):
        acc = acc + _dot_nt(a[:, i * w:(i + 1) * w], w_ref[i])
    return acc


def _merge_fwd(x2, y_a, y_b, proj_g, w_pa, w_pb, w_out, tm):
    T = x2.shape[0]

    def body(x_ref, ya_ref, yb_ref, g_ref, wpa_ref, wpb_ref, wo_ref, x1_ref, mg_ref):
        g = g_ref[...].astype(F32)
        pa = _dot_stacked(ya_ref[...], wpa_ref)
        pb = _dot_stacked(yb_ref[...], wpb_ref)
        merged = (_sigmoid(g[:, :D_MODEL]) * pa + _sigmoid(g[:, D_MODEL:]) * pb).astype(BF16)
        mg_ref[...] = merged
        x1_ref[...] = x_ref[...] + _dot(merged, wo_ref[...])

    return pl.pallas_call(
        body, name="merge_fwd", grid=(T // tm,),
        in_specs=[_row(tm, D_MODEL), _row(tm, A_WIDTH), _row(tm, Q_DIM), _row(tm, G_DIM),
                  _resident(w_pa.shape), _resident(w_pb.shape), _resident(w_out.shape)],
        out_specs=[_row(tm, D_MODEL), _row(tm, D_MODEL)],
        out_shape=[_sds((T, D_MODEL), F32), _sds((T, D_MODEL), BF16)],
        compiler_params=_cp(("arbitrary",), 40),
    )(*_hbm(x2, y_a, y_b, proj_g, w_pa, w_pb, w_out))


def _upproj(x1, g_ffn, w_up, w_conv, b_conv, tm, seq):
    T = x1.shape[0]
    cw = w_up.shape[2]
    tiles_per_seq = seq // tm

    def body(x_ref, g_ref, w_ref, wc_ref, bc_ref, u_ref, h_ref, gate_ref, val_ref, tail_scr):
        at_start = (pl.program_id(0) % tiles_per_seq) == 0
        x = x_ref[...]
        h = (x * _rms_r(x) * g_ref[...]).astype(BF16)
        h_ref[...] = h
        for i in range(N_CHIPS):
            cs = slice(i * cw, (i + 1) * cw)
            u = _dot(h, w_ref[i])
            u_ref[:, cs] = u.astype(BF16)
            hl = jnp.where(at_start, 0.0, tail_scr[SUBLANES - 2:SUBLANES, cs])
            tail_scr[:, cs] = u[tm - SUBLANES:]
            up = _conv_out((u, _shift_down(u, hl, 1), _shift_down(u, hl, 2)), wc_ref[:, cs], bc_ref[:, cs])
            out_ref = gate_ref if i < N_CHIPS // 2 else val_ref
            out_ref[:, (i % 2) * cw:(i % 2 + 1) * cw] = up.astype(BF16)

    return pl.pallas_call(
        body, name="upproj", grid=(T // tm,),
        in_specs=[_row(tm, D_MODEL), _full(g_ffn.shape), _resident(w_up.shape), _full(w_conv.shape), _full(b_conv.shape)],
        out_specs=[_row(tm, 2 * D_FF), _row(tm, D_MODEL), _row(tm, D_FF), _row(tm, D_FF)],
        out_shape=[_sds((T, 2 * D_FF), BF16), _sds((T, D_MODEL), BF16), _sds((T, D_FF), BF16), _sds((T, D_FF), BF16)],
        scratch_shapes=[pltpu.VMEM((SUBLANES, 2 * D_FF), F32)],
        compiler_params=_cp(("arbitrary",), 56),
    )(*_hbm(x1, g_ffn, w_up, w_conv, b_conv))


def _shift_down(u, halo, k):
    rolled = pltpu.roll(u, k, 0)
    head = rolled[:SUBLANES]
    row = lax.broadcasted_iota(jnp.int32, head.shape, 0)
    if k == 1:
        head = jnp.where(row == 0, halo[1:2], head)
    else:
        head = jnp.where(row == 0, halo[0:1], jnp.where(row == 1, halo[1:2], head))
    return jnp.concatenate([head, rolled[SUBLANES:]], axis=0)


def _shift_up(d, halo, k):
    tm = d.shape[0]
    rolled = pltpu.roll(d, tm - k, 0)
    tail = rolled[tm - SUBLANES:]
    row = lax.broadcasted_iota(jnp.int32, tail.shape, 0)
    if k == 1:
        tail = jnp.where(row == SUBLANES - 1, halo[0:1], tail)
    else:
        tail = jnp.where(row == SUBLANES - 2, halo[0:1], jnp.where(row == SUBLANES - 1, halo[1:2], tail))
    return jnp.concatenate([rolled[:tm - SUBLANES], tail], axis=0)


def _conv_out(taps, wc, bc):
    u, u1, u2 = taps
    return wc[0:1] * u2 + wc[1:2] * u1 + wc[2:3] * u + bc


def _ffn_down_loss(gate, val, x1, target, w_down, g_final, tm):
    T = x1.shape[0]
    half = D_FF // 2

    def body(gt_ref, vl_ref, x1_ref, t_ref, wd_ref, g_ref, dx2_ref, loss_ref, gg_ref):
        i = pl.program_id(0)
        acc = jnp.zeros((tm, D_MODEL), F32)
        for j in range(2):
            gc = slice(j * half, (j + 1) * half)
            gate = gt_ref[:, gc].astype(F32)
            act = (gate * _sigmoid(gate) * vl_ref[:, gc].astype(F32)).astype(BF16)
            acc = acc + _dot(act, wd_ref[gc, :])
        x2 = x1_ref[...] + acc
        r = _rms_r(x2)
        n = x2 * r
        g = g_ref[...]
        diff = n * g - t_ref[...]
        dy = diff * (1.0 / D_MODEL)
        dx2_ref[...] = _rms_bwd(dy, n, r, g)

        @pl.when(i == 0)
        def _():
            loss_ref[...] = jnp.zeros_like(loss_ref)
            gg_ref[...] = jnp.zeros_like(gg_ref)

        loss_ref[...] += 0.5 * jnp.sum(jnp.mean(diff * diff, axis=-1, keepdims=True), axis=0, keepdims=True)
        gg_ref[...] += jnp.sum(dy * n, axis=0, keepdims=True)

    return pl.pallas_call(
        body, name="ffn_down_loss", grid=(T // tm,),
        in_specs=[_row(tm, D_FF), _row(tm, D_FF), _row(tm, D_MODEL), _row(tm, D_MODEL),
                  _resident(w_down.shape), _full(g_final.shape)],
        out_specs=[_row(tm, D_MODEL), _full((1, 1)), _full((1, D_MODEL))],
        out_shape=[_sds((T, D_MODEL), F32), _sds((1, 1), F32), _sds((1, D_MODEL), F32)],
        compiler_params=_cp(("arbitrary",), 48),
    )(*_hbm(gate, val, x1, target, w_down, g_final))


def _ffn_bwd_act(gate, val, dx2, w_down, tm):
    T = dx2.shape[0]
    half = D_FF // 2
    nt = T // tm

    def body(g_ref, v_ref, dx_ref, wd_ref, dg_ref, dv_ref, gwd_out, gbg_ref, gbv_ref, gwd_ref):
        i = pl.program_id(1)

        @pl.when(i == 0)
        def _():
            for r in (gwd_ref, gbg_ref, gbv_ref):
                r[...] = jnp.zeros_like(r)

        dx = dx_ref[...].astype(BF16)
        for c0 in range(0, half, COL_CHUNK):
            cs = slice(c0, min(c0 + COL_CHUNK, half))
            gate = g_ref[:, cs].astype(F32)
            val = v_ref[:, cs].astype(F32)
            sg = _sigmoid(gate)
            silu = gate * sg
            d_act = _dot_nt(dx, wd_ref[cs, :])
            d_val = d_act * silu
            d_gate = d_act * val * (sg * (1.0 + gate * (1.0 - sg)))
            dg_ref[:, cs] = d_gate.astype(BF16)
            dv_ref[:, cs] = d_val.astype(BF16)
            gwd_ref[cs, :] += _dot_tn((silu * val).astype(BF16), dx)
            gbg_ref[:, cs] += jnp.sum(d_gate, axis=0, keepdims=True)
            gbv_ref[:, cs] += jnp.sum(d_val, axis=0, keepdims=True)

        @pl.when(i == nt - 1)
        def _():
            gwd_out[...] = gwd_ref[...].astype(BF16)

    tile = pl.BlockSpec((tm, half), lambda j, i: (i, j))
    vec = pl.BlockSpec((1, half), lambda j, i: (0, j))
    wrows = pl.BlockSpec((half, D_MODEL), lambda j, i: (j, 0))
    return pl.pallas_call(
        body, name="ffn_bwd_act", grid=(2, nt),
        in_specs=[tile, tile, pl.BlockSpec((tm, D_MODEL), lambda j, i: (i, 0)), wrows],
        out_specs=[tile, tile, wrows, vec, vec],
        out_shape=[_sds((T, D_FF), BF16), _sds((T, D_FF), BF16), _sds((D_FF, D_MODEL), BF16),
                   _sds((1, D_FF), F32), _sds((1, D_FF), F32)],
        scratch_shapes=[pltpu.VMEM((half, D_MODEL), F32)],
        compiler_params=_cp(("arbitrary", "arbitrary"), 56),
    )(*_hbm(gate, val, dx2, w_down))


def _ffn_bwd_up(d_gate, d_val, upre, dx2, x1, g_ffn, w_conv, w_up, tm, seq):
    T = dx2.shape[0]
    tiles_per_seq = seq // tm
    k16 = tm // BF16_ROWS
    n16 = T // BF16_ROWS
    cw = D_FF // 2

    def body(dg_ref, dv_ref, hg_ref, hv_ref, u_ref, dx2_ref, x1_ref, g_ref, wc_ref, wu_ref, du_ref, dx1_ref, gg_ref, gwc_ref):
        i = pl.program_id(0)
        at_end = (i % tiles_per_seq) == tiles_per_seq - 1

        @pl.when(i == 0)
        def _():
            gg_ref[...] = jnp.zeros_like(gg_ref)
            gwc_ref[...] = jnp.zeros_like(gwc_ref)

        dh = jnp.zeros((tm, D_MODEL), F32)
        for j in range(4):
            src, hsrc = (dg_ref, hg_ref) if j < 2 else (dv_ref, hv_ref)
            ls = slice((j % 2) * cw, (j % 2 + 1) * cw)
            cs = slice(j * cw, (j + 1) * cw)
            d = src[:, ls].astype(F32)
            hl = hsrc[:, ls].astype(F32)[0:2]
            hl = jnp.where(at_end, 0.0, hl)
            wc = wc_ref[:, cs]
            d1 = _shift_up(d, hl, 1)
            d2 = _shift_up(d, hl, 2)
            du = (wc[2:3] * d + wc[1:2] * d1 + wc[0:1] * d2).astype(BF16)
            du_ref[:, cs] = du
            dh = dh + _dot_nt(du, wu_ref[j])
            u = u_ref[:, cs].astype(F32)
            gwc_ref[0:1, cs] += jnp.sum(d2 * u, axis=0, keepdims=True)
            gwc_ref[1:2, cs] += jnp.sum(d1 * u, axis=0, keepdims=True)
            gwc_ref[2:3, cs] += jnp.sum(d * u, axis=0, keepdims=True)
        x = x1_ref[...]
        r = _rms_r(x)
        n = x * r
        dx1_ref[...] = dx2_ref[...] + _rms_bwd(dh, n, r, g_ref[...])
        gg_ref[...] += jnp.sum(dh * n, axis=0, keepdims=True)

    nxt = pl.BlockSpec((BF16_ROWS, D_FF), lambda i: (jnp.minimum((i + 1) * k16, n16 - 1), 0))
    return pl.pallas_call(
        body, name="ffn_bwd_up", grid=(T // tm,),
        in_specs=[_row(tm, D_FF), _row(tm, D_FF), nxt, nxt, _row(tm, 2 * D_FF), _row(tm, D_MODEL), _row(tm, D_MODEL),
                  _full(g_ffn.shape), _full(w_conv.shape), _resident(w_up.shape)],
        out_specs=[_row(tm, 2 * D_FF), _row(tm, D_MODEL), _full((1, D_MODEL)), _full((3, 2 * D_FF))],
        out_shape=[_sds((T, 2 * D_FF), BF16), _sds((T, D_MODEL), F32), _sds((1, D_MODEL), F32), _sds((3, 2 * D_FF), F32)],
        compiler_params=_cp(("arbitrary",), 56),
    )(*_hbm(d_gate, d_val, d_gate, d_val, upre, dx2, x1, g_ffn, w_conv, w_up))


def _matmul_tn(a, b, tn, tk, name):
    T, M = a.shape
    N = b.shape[1]
    nk = T // tk

    def body(a_ref, b_ref, o_ref, acc_ref):
        k = pl.program_id(1)

        @pl.when(k == 0)
        def _():
            acc_ref[...] = jnp.zeros_like(acc_ref)

        acc_ref[...] += _dot_tn(a_ref[...], b_ref[...])

        @pl.when(k == nk - 1)
        def _():
            o_ref[...] = acc_ref[...].astype(BF16)

    return pl.pallas_call(
        body, name=name, grid=(N // tn, nk),
        in_specs=[pl.BlockSpec((tk, M), lambda j, k: (k, 0)), pl.BlockSpec((tk, tn), lambda j, k: (k, j))],
        out_specs=pl.BlockSpec((M, tn), lambda j, k: (0, j)), out_shape=_sds((M, N), BF16),
        scratch_shapes=[pltpu.VMEM((M, tn), F32)],
        compiler_params=_cp(("arbitrary", "arbitrary"), 48),
    )(*_hbm(a, b))


def _merge_bwd(dx1, merged, y_a, y_b, proj_g, proj_a, w_pa, w_pb, w_out, g_sgu, w_s, b_st, tm, after=None):
    T = dx1.shape[0]

    nt = T // tm
    pshape = (A_WIDTH, D_MODEL)
    order = [] if after is None else [after]

    def body(*refs):
        dx_ref, mg_ref, ya_ref, yb_ref, g_ref, p_ref, wpa_ref, wpb_ref, wo_ref, gs_ref, ws_ref, bs_ref = refs[:12]
        (dg_ref, da_ref, dyb_ref, gwo_out, gwpa_out, gwpb_out, gws_ref, gbs_ref, gg_ref,
         gwo_ref, gwpa_ref, gwpb_ref) = refs[12 + len(order):]
        i = pl.program_id(0)

        @pl.when(i == 0)
        def _():
            for r in (gwo_ref, gwpa_ref, gwpb_ref, gws_ref, gbs_ref, gg_ref):
                r[...] = jnp.zeros_like(r)

        dx = dx_ref[...].astype(BF16)
        dm = _dot_nt(dx, wo_ref[...])
        g = g_ref[...].astype(F32)
        ya = ya_ref[...]
        yb = yb_ref[...]
        pa = _dot_stacked(ya, wpa_ref)
        pb = _dot_stacked(yb, wpb_ref)
        sa = _sigmoid(g[:, :D_MODEL])
        sb = _sigmoid(g[:, D_MODEL:])
        dpa = (dm * sa).astype(BF16)
        dpb = (dm * sb).astype(BF16)
        dg_ref[:, :D_MODEL] = (dm * pa * (sa * (1.0 - sa))).astype(BF16)
        dg_ref[:, D_MODEL:] = (dm * pb * (sb * (1.0 - sb))).astype(BF16)
        d_ya = _dot_nt_stacked(dpa, wpa_ref).astype(BF16)
        dyb_ref[...] = _dot_nt_stacked(dpb, wpb_ref).astype(BF16)
        _sgu_bwd_apply(p_ref[...].astype(F32), d_ya.astype(F32), gs_ref[...], ws_ref, bs_ref, da_ref, gws_ref, gbs_ref, gg_ref)
        gwo_ref[...] += _dot_tn(mg_ref[...], dx)
        gwpa_ref[...] += _dot_tn(ya, dpa)
        gwpb_ref[...] += _dot_tn(yb, dpb)

        @pl.when(i == nt - 1)
        def _():
            gwo_out[...] = gwo_ref[...].astype(BF16)
            gwpa_out[...] = gwpa_ref[...].astype(BF16)
            gwpb_out[...] = gwpb_ref[...].astype(BF16)

    return pl.pallas_call(
        body, name="merge_bwd", grid=(nt,),
        in_specs=[_row(tm, D_MODEL), _row(tm, D_MODEL), _row(tm, A_WIDTH), _row(tm, Q_DIM), _row(tm, G_DIM), _row(tm, A_DIM),
                  _resident(w_pa.shape), _resident(w_pb.shape), _resident(w_out.shape),
                  _full(g_sgu.shape), _full(w_s.shape), _full(b_st.shape)] + [ANY] * len(order),
        out_specs=[_row(tm, G_DIM), _row(tm, A_DIM), _row(tm, Q_DIM),
                   _full(w_out.shape), _full(pshape), _full(pshape), _full(w_s.shape), _full(b_st.shape), _full(g_sgu.shape)],
        out_shape=[_sds((T, G_DIM), BF16), _sds((T, A_DIM), BF16), _sds((T, Q_DIM), BF16),
                   _sds(w_out.shape, BF16), _sds(pshape, BF16), _sds(pshape, BF16),
                   _sds(w_s.shape, F32), _sds(b_st.shape, F32), _sds(g_sgu.shape, F32)],
        scratch_shapes=[pltpu.VMEM(w_out.shape, F32), pltpu.VMEM(pshape, F32), pltpu.VMEM(pshape, F32)],
        compiler_params=_cp(("arbitrary",), 56),
    )(*_hbm(dx1, merged, y_a, y_b, proj_g, proj_a, w_pa, w_pb, w_out, g_sgu, w_s, b_st), *order)


def _sgu_bwd_apply(p, dy, g, ws_ref, bs_ref, dp_ref, gws_ref, gbs_ref, gg_ref):
    tril = _tril()
    pu, pv, u, tu, vv, tv, rv, vn = _sgu_parts(p, g)
    du_cols = []
    dvn_cols = []
    for gi in range(A_GROUPS):
        wm = jnp.where(tril, ws_ref[gi], 0.0).astype(BF16)
        wmt = wm.astype(F32).T.astype(BF16)
        bcol = bs_ref[:, gi:gi + 1]
        cs = slice(gi * CHUNK, (gi + 1) * CHUNK)
        du_rows = []
        dvn_rows = []
        gw = jnp.zeros((CHUNK, CHUNK), F32)
        gb = jnp.zeros((CHUNK, 1), F32)
        for c in range(p.shape[0] // CHUNK):
            rs = slice(c * CHUNK, (c + 1) * CHUNK)
            vn_c = vn[rs, cs]
            s = _dot(wm, vn_c) + bcol
            dy_c = dy[rs, cs]
            ds = dy_c * u[rs, cs]
            du_rows.append(dy_c * s)
            dsb = ds.astype(BF16)
            gw = gw + _dot_nt(dsb, vn_c)
            gb = gb + jnp.sum(ds, axis=-1, keepdims=True)
            dvn_rows.append(_dot(wmt, dsb))
        gws_ref[gi] += jnp.where(tril, gw, 0.0)
        gbs_ref[:, gi:gi + 1] += gb
        du_cols.append(jnp.concatenate(du_rows, axis=0))
        dvn_cols.append(jnp.concatenate(dvn_rows, axis=0))
    du = jnp.concatenate(du_cols, axis=1)
    dvn = jnp.concatenate(dvn_cols, axis=1)
    vhat = vv * rv
    gg_ref[...] += jnp.sum(dvn * vhat, axis=0, keepdims=True)
    dvv = _rms_bwd(dvn, vhat, rv, g)
    dp_ref[:, :A_WIDTH] = (du * _gelu_grad(pu, tu)).astype(BF16)
    dp_ref[:, A_WIDTH:] = (dvv * _gelu_grad(pv, tv)).astype(BF16)


def _attn_bwd(proj_b, d_yb, sinks, rel_bias, n_seq, seq, after=None):
    nb = seq // CHUNK
    bk = jnp.asarray(_band_buckets())
    order = [] if after is None else [after]

    def body(*refs):
        qkv_ref, do_ref, bk_ref, rel_ref, sink_ref = refs[:5]
        (d_ref, gs_ref, gr_ref, bias_scr, sink_scr, kvar_scr, dbias_scr, dk_scr, dv_scr, ds_scr) = refs[5 + len(order):]
        b = pl.program_id(0)
        _attn_setup(bias_scr, sink_scr, kvar_scr, qkv_ref, bk_ref, rel_ref, sink_ref)
        ones = jnp.ones((2 * CHUNK, LANES), BF16)

        @pl.when(b == 0)
        def _():
            dbias_scr[...] = jnp.zeros_like(dbias_scr)
            ds_scr[...] = jnp.zeros_like(ds_scr)

        dk_scr[...] = jnp.zeros_like(dk_scr)
        dv_scr[...] = jnp.zeros_like(dv_scr)

        def transposed(a):
            return a.astype(F32).T.astype(BF16)

        def blk(n, carry):
            r0, kv, vv = _attn_block_inputs(kvar_scr, n)
            prob, psink = _attn_probs(qkv_ref, r0, n, kv, bias_scr, sink_scr, ones)
            dp = jnp.concatenate([_dot_nt(do_ref[pl.ds(r0, CHUNK), (h // 2) * LANES:(h // 2 + 1) * LANES], vv[h // 4][h % 2])
                                  for h in range(N_HEADS)], axis=0)
            delta = _rowsum(prob * dp, ones)
            dsc = prob * (dp - _both(delta))
            ds_scr[...] += psink * delta
            dbias_scr[...] += dsc
            dsb = (dsc * (HEAD_DIM ** -0.5)).astype(BF16)
            pb = prob.astype(BF16)
            dkt = [jnp.zeros((HEAD_DIM, 2 * CHUNK), F32) for _ in range(2)]
            dvt = [jnp.zeros((HEAD_DIM, 2 * CHUNK), F32) for _ in range(2)]
            for pr in range(N_HEADS // 2):
                ps = slice(pr * LANES, (pr + 1) * LANES)
                qpt = transposed(qkv_ref[pl.ds(r0, CHUNK), ps])
                dopt = transposed(do_ref[pl.ds(r0, CHUNK), ps])
                kvh = pr // 2
                dq = jnp.zeros((CHUNK, LANES), F32)
                for hh in range(2):
                    hr = _head_rows(2 * pr + hh)
                    rows = slice(hh * HEAD_DIM, (hh + 1) * HEAD_DIM)
                    dq = dq + _dot(dsb[hr], kv[kvh][hh])
                    dkt[kvh] = dkt[kvh] + _dot(qpt, dsb[hr])[rows]
                    dvt[kvh] = dvt[kvh] + _dot(dopt, pb[hr])[rows]
                d_ref[pl.ds(r0, CHUNK), ps] = dq.astype(BF16)
            dk_scr[:, pl.ds(r0, 2 * CHUNK)] += jnp.concatenate(dkt, axis=0)
            dv_scr[:, pl.ds(r0, 2 * CHUNK)] += jnp.concatenate(dvt, axis=0)
            return carry

        lax.fori_loop(0, nb, blk, 0)
        for n in range(nb):
            rows = slice(n * CHUNK, (n + 1) * CHUNK)
            cols = slice((n + 1) * CHUNK, (n + 2) * CHUNK)
            d_ref[rows, Q_DIM:Q_DIM + KV_DIM] = dk_scr[:, cols].T.astype(BF16)
            d_ref[rows, Q_DIM + KV_DIM:] = dv_scr[:, cols].T.astype(BF16)

        @pl.when(b == n_seq - 1)
        def _():
            bkv = bk_ref[...]
            for h in range(N_HEADS):
                gs_ref[0:1, h:h + 1] = -jnp.sum(ds_scr[_head_rows(h), 0:1], axis=0, keepdims=True)
                db = dbias_scr[_head_rows(h), :]
                for bb in range(N_BUCKETS):
                    part = jnp.sum(jnp.where(bkv == bb, db, 0.0), axis=-1, keepdims=True)
                    gr_ref[bb:bb + 1, h:h + 1] = jnp.sum(part, axis=0, keepdims=True)

    smem = pl.BlockSpec(memory_space=pltpu.SMEM)
    return pl.pallas_call(
        body, name="attn_bwd", grid=(n_seq,),
        in_specs=[_row(seq, B_DIM), _row(seq, Q_DIM), _full(bk.shape), smem, smem] + [ANY] * len(order),
        out_specs=[_row(seq, B_DIM), _full((1, N_HEADS)), _full((N_BUCKETS, N_HEADS))],
        out_shape=[_sds((n_seq * seq, B_DIM), BF16), _sds((1, N_HEADS), F32), _sds((N_BUCKETS, N_HEADS), F32)],
        scratch_shapes=[pltpu.VMEM((HEAD_ROWS, 2 * CHUNK), F32), pltpu.VMEM((HEAD_ROWS, LANES), F32),
                        pltpu.VMEM((8, seq, KV_DIM), BF16), pltpu.VMEM((HEAD_ROWS, 2 * CHUNK), F32),
                        pltpu.VMEM((KV_DIM, seq + CHUNK), F32), pltpu.VMEM((KV_DIM, seq + CHUNK), F32),
                        pltpu.VMEM((HEAD_ROWS, LANES), F32)],
        compiler_params=_cp(("arbitrary",), 40),
    )(*_hbm(proj_b, d_yb, bk), rel_bias, sinks, *order)


def _inproj_bwd(d_g, d_a, d_b, x2, dx1, g_mix, w_in, tm, after=None):
    T = x2.shape[0]
    order = [] if after is None else [after]

    def body(*refs):
        dg_ref, da_ref, db_ref, x_ref, dx1_ref, g_ref, w_ref = refs[:7]
        gx_ref, gg_ref = refs[7 + len(order):]
        dh = (_dot_nt(dg_ref[...], w_ref[:, _G_COLS]) + _dot_nt(da_ref[...], w_ref[:, _A_COLS])
              + _dot_nt(db_ref[...], w_ref[:, _B_COLS]))
        x = x_ref[...]
        r = _rms_r(x)
        n = x * r
        gx_ref[...] = dx1_ref[...] + _rms_bwd(dh, n, r, g_ref[...])

        @pl.when(pl.program_id(0) == 0)
        def _():
            gg_ref[...] = jnp.zeros_like(gg_ref)

        gg_ref[...] += jnp.sum(dh * n, axis=0, keepdims=True)

    return pl.pallas_call(
        body, name="inproj_bwd", grid=(T // tm,),
        in_specs=[_row(tm, G_DIM), _row(tm, A_DIM), _row(tm, B_DIM), _row(tm, D_MODEL), _row(tm, D_MODEL),
                  _full(g_mix.shape), _resident(w_in.shape)] + [ANY] * len(order),
        out_specs=[_row(tm, D_MODEL), _full((1, D_MODEL))],
        out_shape=[_sds((T, D_MODEL), F32), _sds((1, D_MODEL), F32)],
        compiler_params=_cp(("arbitrary",), 48),
    )(*_hbm(d_g, d_a, d_b, x2, dx1, g_mix, w_in), *order)


IN_SHARD = (A_DIM + B_DIM + G_DIM) // N_CHIPS


def _unstack_w_in(stack):
    tr = 256

    def body(s_ref, o_ref):
        for i in range(N_CHIPS):
            o_ref[:, i * IN_SHARD:(i + 1) * IN_SHARD] = s_ref[i]

    return pl.pallas_call(
        body, name="unstack_w_in", grid=(D_MODEL // tr,),
        in_specs=[pl.BlockSpec((N_CHIPS, tr, IN_SHARD), lambda r: (0, r, 0))],
        out_specs=pl.BlockSpec((tr, N_CHIPS * IN_SHARD), lambda r: (r, 0)),
        out_shape=_sds((D_MODEL, N_CHIPS * IN_SHARD), stack.dtype),
        compiler_params=_cp(("arbitrary",)),
    )(*_hbm(stack))


def _stack_grad_w_in(gw_a, gw_b, gw_g):
    tr = 256

    def body(a_ref, b_ref, g_ref, o_ref):
        full = jnp.concatenate([a_ref[...], b_ref[...], g_ref[...]], axis=1)
        for i in range(N_CHIPS):
            o_ref[i] = full[:, i * IN_SHARD:(i + 1) * IN_SHARD]

    return pl.pallas_call(
        body, name="stack_grad_w_in", grid=(D_MODEL // tr,),
        in_specs=[_row(tr, A_DIM), _row(tr, B_DIM), _row(tr, G_DIM)],
        out_specs=pl.BlockSpec((N_CHIPS, tr, IN_SHARD), lambda r: (0, r, 0)),
        out_shape=_sds((N_CHIPS, D_MODEL, IN_SHARD), gw_a.dtype),
        compiler_params=_cp(("arbitrary",)),
    )(*_hbm(gw_a, gw_b, gw_g))


def _local_step(x, target, g_mix, g_sgu, w_s, b_s, sinks, rel_bias, g_ffn, b_conv, g_final,
                w_in, w_conv, proj_weights, ffn_weights, on_grads, after=None):
    n_seq, seq, _ = x.shape
    T = n_seq * seq
    tm = min(ROW_TILE, seq)
    tw = min(GRAD_ROW_TILE, T)
    tf = min(WIDE_ROW_TILE, seq)
    x2 = x.reshape(T, D_MODEL)
    tgt = target.reshape(T, D_MODEL)
    b_st = b_s.T
    g_fin = g_final.reshape(1, D_MODEL)

    proj_g, proj_a, proj_b, h, y_a = _inproj(x2, g_mix, w_in, g_sgu, w_s, b_st, tm, after)
    y_b = _attn_fwd(proj_b, sinks, rel_bias, n_seq, seq)
    w_pa, w_pb, w_out = proj_weights(y_b)
    x1, merged = _merge_fwd(x2, y_a, y_b, proj_g, w_pa, w_pb, w_out, tm)
    w_up, w_down = ffn_weights(x1)
    upre, h2, gate, val = _upproj(x1, g_ffn, w_up, w_conv, b_conv, tf, seq)
    dx2, loss, gg_final = _ffn_down_loss(gate, val, x1, tgt, w_down, g_fin, tm)

    d_gate, d_val, gw_down, gb_g, gb_v = _ffn_bwd_act(gate, val, dx2, w_down, tw)
    gb_conv = jnp.concatenate([gb_g, gb_v], axis=1)
    d_upre, dx1, gg_ffn, gw_conv = _ffn_bwd_up(d_gate, d_val, upre, dx2, x1, g_ffn, w_conv, w_up, tf, seq)
    gw_up = _matmul_tn(h2, d_upre, 2 * D_FF // 4, min(2 * GRAD_ROW_TILE, T), "grad_w_up")
    sent = on_grads("ffn", dict(w_up=gw_up, w_down=gw_down))
    d_g, d_a, d_yb, gw_out, gw_pa, gw_pb, gw_s, gb_st, gg_sgu = _merge_bwd(
        dx1, merged, y_a, y_b, proj_g, proj_a, w_pa, w_pb, w_out, g_sgu, w_s, b_st, tw, sent)
    sent = on_grads("proj", dict(w_pa=gw_pa, w_pb=gw_pb, w_out=gw_out))
    d_b, g_sinks, g_rel = _attn_bwd(proj_b, d_yb, sinks, rel_bias, n_seq, seq, sent)
    gw_g = _matmul_tn(h, _tie(d_g, d_b), D_MODEL, min(2 * GRAD_ROW_TILE, T), "grad_w_in_gate")
    gw_a = _matmul_tn(h, _tie(d_a, gw_g), A_DIM, min(2 * GRAD_ROW_TILE, T), "grad_w_in_a")
    gw_b = _matmul_tn(h, _tie(d_b, gw_a), B_DIM, min(2 * GRAD_ROW_TILE, T), "grad_w_in_b")
    gw_in = _stack_grad_w_in(gw_a, gw_b, gw_g)
    sent = on_grads("in", dict(w_in=gw_in))
    grad_x, gg_mix = _inproj_bwd(d_g, d_a, d_b, x2, dx1, g_mix, w_in, tm, sent)

    small = dict(g_mix=gg_mix, g_sgu=gg_sgu, w_s=gw_s, b_s=gb_st.T, sinks=g_sinks, rel_bias=g_rel,
                 g_ffn=gg_ffn, b_conv=gb_conv, g_final=gg_final, w_conv=gw_conv)
    big = dict(w_in=gw_in, w_pa=gw_pa, w_pb=gw_pb, w_out=gw_out, w_up=gw_up, w_down=gw_down)
    return loss, grad_x.reshape(x.shape), small, big


_MIXER = ("w_in", "w_pa", "w_pb", "w_out")
_FFN = ("w_up", "w_down")
_BIG = _MIXER + _FFN

CONV_ROWS = 6
_SMALL_AT = dict(loss=(0, 1, 1), g_final=(1, 1, D_MODEL), g_mix=(2, 1, D_MODEL), g_ffn=(3, 1, D_MODEL), g_sgu=(4, 1, A_WIDTH),
                 sinks=(5, 1, N_HEADS), rel_bias=(6, 1, N_BUCKETS * N_HEADS), b_s=(8, A_GROUPS, CHUNK),
                 b_conv=(12, CONV_ROWS, D_MODEL), w_conv=(18, 3 * CONV_ROWS, D_MODEL), w_s=(40, A_GROUPS * CHUNK * CHUNK // D_MODEL, D_MODEL))
_SMALL_IN_CALL = ("g_final", "g_mix", "g_ffn", "g_sgu", "sinks", "b_s")
SMALL_ROWS = 104


def _pack_small(vals):
    def wide(a):
        return jnp.pad(a, ((0, 0), (0, CONV_ROWS * D_MODEL - a.shape[1]))).reshape(-1, D_MODEL)

    laid = dict(vals, b_conv=wide(vals["b_conv"]), w_conv=wide(vals["w_conv"]), w_s=vals["w_s"].reshape(-1, D_MODEL))
    rows, at = [], 0
    for n, (r0, nr, nc) in _SMALL_AT.items():
        if r0 > at:
            rows.append(jnp.zeros((r0 - at, D_MODEL), F32))
        rows.append(jnp.pad(laid[n].astype(F32).reshape(nr, nc), ((0, 0), (0, D_MODEL - nc))))
        at = r0 + nr
    return jnp.concatenate(rows, axis=0)


def _unwide(a, r):
    return a.reshape(r, CONV_ROWS * D_MODEL)[:, :2 * D_FF]


def _mesh_pos():
    return lax.axis_index("x"), lax.axis_index("y"), lax.axis_index("c")


def _other_chips(x, y):
    return [(1 - x, y), (x, 1 - y), (1 - x, 1 - y)]


def _remote(src, dst, send_sem, recv_sem, to):
    return pltpu.make_async_remote_copy(src_ref=src, dst_ref=dst, send_sem=send_sem, recv_sem=recv_sem,
                                        device_id=to, device_id_type=MESH)


def _own_slot(own, n, at):
    return lax.dynamic_update_slice(lax.empty((n,) + own.shape, own.dtype), own[None], (at,) + (0,) * own.ndim)


def _allgather_weights(stacks, wc_stack):
    names = list(stacks)
    n = len(names)

    def body(*refs):
        ins, outs = refs[:n + 1], refs[n + 1:2 * n + 2]
        send_sems, recv_sems = refs[2 * n + 2:]
        x, y, c = _mesh_pos()
        _handshake(_chip_peers(x, y, c) + _sibling_peers(x, y, c))
        me = 2 * x + y
        sibling = (x, y, 1 - c)
        chips = _other_chips(x, y)

        def half(ref, chip, hc):
            hr = ref.shape[1] // 2
            return ref.at[chip, pl.ds(hc * hr, hr), :]

        first = []
        for k in range(n):
            first += [_remote(half(ins[k], me, c), half(outs[k], me, c), send_sems.at[6 * k + j], recv_sems.at[6 * k + j], (cx, cy, c))
                      for j, (cx, cy) in enumerate(chips)]
        first += [_remote(ins[n].at[me], outs[n].at[me], send_sems.at[6 * n + j], recv_sems.at[6 * n + j], (cx, cy, c))
                  for j, (cx, cy) in enumerate(chips)]
        for cp in first:
            cp.start()
        passed = []
        for k in range(n):
            for j, (cx, cy) in enumerate(chips):
                landed = half(outs[k], 2 * cx + cy, c)
                _remote(landed, landed, send_sems.at[6 * k + j], recv_sems.at[6 * k + j], (x, y, c)).wait_recv()
                passed.append(_remote(landed, landed, send_sems.at[6 * k + 3 + j], recv_sems.at[6 * k + 3 + j], sibling))
                passed[-1].start()
        for k in range(n):
            for j, (cx, cy) in enumerate(chips):
                theirs = half(outs[k], 2 * cx + cy, 1 - c)
                _remote(theirs, theirs, send_sems.at[6 * k + 3 + j], recv_sems.at[6 * k + 3 + j], (x, y, c)).wait_recv()
        for j, (cx, cy) in enumerate(chips):
            slot = outs[n].at[2 * cx + cy]
            _remote(slot, slot, send_sems.at[6 * n + j], recv_sems.at[6 * n + j], (x, y, c)).wait_recv()
        for cp in first + passed:
            cp.wait_send()

    arrays = [stacks[k] for k in names] + [wc_stack]
    outs = pl.pallas_call(
        body, name="allgather_weights",
        in_specs=[HBM] * (n + 1), out_specs=[HBM] * (n + 1), input_output_aliases={k: k for k in range(n + 1)},
        out_shape=[_sds(a.shape, a.dtype) for a in arrays],
        scratch_shapes=[pltpu.SemaphoreType.DMA((6 * n + 3,)), pltpu.SemaphoreType.DMA((6 * n + 3,))],
        compiler_params=pltpu.CompilerParams(collective_id=_COLLECTIVE["gather_in"]),
    )(*arrays)
    return dict(zip(names, outs[:n])), outs[n]


_KIND = {"w_in": "stack", "w_pa": "col", "w_pb": "col", "w_up": "col", "w_out": "row", "w_down": "row"}


def _half_view(ref, kind, h):
    if kind == "stack":
        k = ref.shape[1] // 2
        return ref.at[:, pl.ds(h * k, k), :]
    if kind == "col":
        k = ref.shape[0] // 2
        return ref.at[pl.ds(h * k, k), :]
    k = ref.shape[1] // 2
    return ref.at[:, pl.ds(h * k, k)]


def _shard_view(ref, kind, i):
    if kind == "stack":
        return ref.at[i]
    if kind == "col":
        k = ref.shape[1] // N_CHIPS
        return ref.at[:, pl.ds(i * k, k)]
    k = ref.shape[0] // N_CHIPS
    return ref.at[pl.ds(i * k, k), :]


def _region_view(ref, kind, h):
    if kind == "row":
        k = ref.shape[1] // 2
        return ref.at[:, pl.ds(h * k, k)]
    k = ref.shape[0] // 2
    return ref.at[pl.ds(h * k, k), :]


def _half_shape(shape, kind):
    if kind == "stack":
        return (shape[0], shape[1] // 2, shape[2])
    return (shape[0] // 2, shape[1]) if kind == "col" else (shape[0], shape[1] // 2)


def _part_shape(half_shape, kind):
    if kind == "stack":
        return tuple(half_shape[1:])
    k, w = half_shape
    return (k, w // N_CHIPS) if kind == "col" else (k // N_CHIPS, w)


_DATAFLOW = pltpu.SideEffectType.DATAFLOW_SIDE_EFFECTING
_TOKEN = (SUBLANES, LANES)


_COLLECTIVE = {k: i for i, k in enumerate(
    [kind + "_" + g for kind in ("pair", "chip", "share") for g in ("ffn", "proj", "in")]
    + ["gather_proj", "gather_ffn", "gather_in", "forward_proj", "forward_ffn"])}


def _sibling_peers(x, y, c):
    return [(x, y, 1 - c)]


def _chip_peers(x, y, c):
    return [(cx, cy, c) for cx, cy in _other_chips(x, y)]


def _handshake(peers):
    barrier = pltpu.get_barrier_semaphore()
    for peer in peers:
        pl.semaphore_signal(barrier, inc=1, device_id=peer, device_id_type=MESH)
    pl.semaphore_wait(barrier, len(peers))


def _split_start(name, arrays, n_sems, issue, after=None, handshake=None):
    n = len(arrays)
    order = [] if after is None else [after]

    def body(*refs):
        base = n + len(order)
        if handshake is not None:
            _handshake(handshake[1](*_mesh_pos()))
        issue(refs[:n], refs[base], refs[base + 1])
        refs[-1][...] = jnp.zeros(_TOKEN, F32)

    params = dict(has_side_effects=_DATAFLOW)
    if handshake is not None:
        params["collective_id"] = handshake[0]
    outs = pl.pallas_call(
        body, name=name,
        in_specs=[HBM] * n + [ANY] * len(order), out_specs=[SEM, SEM] + [HBM] * n + [pl.BlockSpec(memory_space=pltpu.VMEM)],
        out_shape=[pltpu.SemaphoreType.DMA((n_sems,)), pltpu.SemaphoreType.DMA((n_sems,))]
        + [pltpu.HBM(a.shape, a.dtype) for a in arrays] + [_sds(_TOKEN, F32)],
        input_output_aliases={k: 2 + k for k in range(n)},
        compiler_params=pltpu.CompilerParams(**params),
    )(*[pltpu.with_memory_space_constraint(a, pltpu.HBM) for a in arrays], *order)
    return outs[0], outs[1], list(outs[2:2 + n]), outs[-1]


def _split_wait(name, started, waits, after):
    send_sems, recv_sems, arrays, _ = started
    n = len(arrays)

    def body(*refs):
        waits(refs[:n], refs[n], refs[n + 1])

    return pl.pallas_call(
        body, name=name,
        in_specs=[HBM] * n + [SEM, SEM, ANY], out_specs=[HBM] * n,
        out_shape=[pltpu.HBM(a.shape, a.dtype) for a in arrays],
        input_output_aliases={k: k for k in range(n)},
        compiler_params=pltpu.CompilerParams(has_side_effects=_DATAFLOW),
    )(*arrays, send_sems, recv_sems, after)


def _wait_both(src, dst, send_sem, recv_sem):
    x, y, c = _mesh_pos()
    cp = _remote(src, dst, send_sem, recv_sem, (x, y, c))
    cp.wait_send()
    cp.wait_recv()


def _pair_exchange_start(parts, tag, after):
    names = list(parts)
    n = len(names)
    lands = [lax.empty(_half_shape(parts[k].shape, _KIND[k]), parts[k].dtype) for k in names]

    def issue(refs, send_sems, recv_sems):
        x, y, c = _mesh_pos()
        for hc in range(2):
            @pl.when(c == hc)
            def _():
                for k in range(n):
                    _remote(_half_view(refs[k], _KIND[names[k]], 1 - hc), refs[n + k], send_sems.at[k], recv_sems.at[k],
                            (x, y, 1 - c)).start()

    return names, _split_start("grad_pair_exchange_start_" + tag, [parts[k] for k in names] + lands, n, issue, after,
                               (_COLLECTIVE["pair_" + tag], _sibling_peers))


def _pair_exchange_wait(pending, tag, after):
    names, started = pending
    n = len(names)

    def waits(refs, send_sems, recv_sems):
        for k in range(n):
            _wait_both(_half_view(refs[k], _KIND[names[k]], 0), refs[n + k], send_sems.at[k], recv_sems.at[k])

    outs = _split_wait("grad_pair_exchange_wait_" + tag, started, waits, after)
    return dict(zip(names, outs[:n])), dict(zip(names, outs[n:]))


def _half_blocks(shape, kind):
    if kind == "stack":
        _, k, w = shape
        tr = k // 2
        nb = 1
        return (N_CHIPS, nb), (1, tr, w), (lambda i, r, s: (i, r, 0)), (lambda i, r, s: (i, s[1] * nb + r, 0))
    k, w = shape
    if kind == "col":
        tr = 256
        nb = k // 2 // tr
        return (nb,), (tr, w), (lambda r, s: (r, 0)), (lambda r, s: (s[1] * nb + r, 0))
    tr = k // N_CHIPS
    return (N_CHIPS,), (tr, w // 2), (lambda r, s: (r, 0)), (lambda r, s: (r, s[1]))


def _pair_add(part, from_sibling, name, pos):
    kind = _KIND[name]
    grid, block, half_map, full_map = _half_blocks(part.shape, kind)

    def body(s_ref, p_ref, q_ref, o_ref):
        o_ref[...] = (p_ref[...].astype(F32) + q_ref[...].astype(F32)).astype(BF16)

    return pl.pallas_call(
        body, name="grad_pair_add_" + name,
        grid_spec=pltpu.PrefetchScalarGridSpec(
            num_scalar_prefetch=1, grid=grid,
            in_specs=[pl.BlockSpec(block, full_map), pl.BlockSpec(block, half_map)],
            out_specs=pl.BlockSpec(block, half_map)),
        out_shape=_sds(from_sibling.shape, BF16),
        compiler_params=_cp(("arbitrary",) * len(grid), 40),
    )(pos, *_hbm(part, from_sibling))


def _chip_exchange_start(sums, tag, after):
    names = list(sums)
    n = len(names)
    lands = [lax.empty((3,) + _part_shape(sums[k].shape, _KIND[k]), sums[k].dtype) for k in names]

    def issue(refs, send_sems, recv_sems):
        x, y, c = _mesh_pos()
        me = 2 * x + y
        for i in range(N_CHIPS):
            xi, yi = i // 2, i % 2
            j = jnp.where(xi != x, jnp.where(yi != y, 2, 0), 1)

            @pl.when(i != me)
            def _():
                for k in range(n):
                    _remote(_shard_view(refs[k], _KIND[names[k]], i), refs[n + k].at[j], send_sems.at[3 * k + j],
                            recv_sems.at[3 * k + j], (xi, yi, c)).start()

    return names, _split_start("grad_chip_exchange_start_" + tag, [sums[k] for k in names] + lands, 3 * n, issue, after,
                               (_COLLECTIVE["chip_" + tag], _chip_peers))


def _chip_exchange_wait(pending, tag, after):
    names, started = pending
    n = len(names)

    def waits(refs, send_sems, recv_sems):
        for k in range(n):
            for j in range(3):
                _wait_both(_shard_view(refs[k], _KIND[names[k]], 0), refs[n + k].at[j], send_sems.at[3 * k + j], recv_sems.at[3 * k + j])

    return dict(zip(names, _split_wait("grad_chip_exchange_wait_" + tag, started, waits, after)[n:]))


def _allgather_start(stacks, tag, after):
    names = list(stacks)

    def issue(refs, send_sems, recv_sems):
        x, y, c = _mesh_pos()
        me = 2 * x + y
        for k, st in enumerate(refs):
            hr = st.shape[1] // 2
            mine = st.at[me, pl.ds(c * hr, hr), :]
            for j, (cx, cy) in enumerate(_other_chips(x, y)):
                _remote(mine, mine, send_sems.at[3 * k + j], recv_sems.at[3 * k + j], (cx, cy, c)).start()

    return names, _split_start("allgather_start_" + tag, [stacks[k] for k in names], 3 * len(names), issue, after,
                               (_COLLECTIVE["gather_" + tag], _chip_peers))


def _allgather_wait(pending, tag, after):
    names, started = pending

    def waits(refs, send_sems, recv_sems):
        for k, st in enumerate(refs):
            slot = st.at[0, pl.ds(0, st.shape[1] // 2), :]
            for j in range(3):
                _wait_both(slot, slot, send_sems.at[3 * k + j], recv_sems.at[3 * k + j])

    return dict(zip(names, _split_wait("allgather_wait_" + tag, started, waits, after)))


def _allgather_forward(stacks, tag):
    names = list(stacks)
    n = len(names)

    def body(*refs):
        ins, outs = refs[:n], refs[n:2 * n]
        send_sems, recv_sems = refs[2 * n:]
        x, y, c = _mesh_pos()
        _handshake(_sibling_peers(x, y, c))
        copies = []
        for k in range(n):
            hr = ins[k].shape[1] // 2
            for j, (cx, cy) in enumerate(_other_chips(x, y)):
                chip = 2 * cx + cy
                copies.append(_remote(ins[k].at[chip, pl.ds(c * hr, hr), :], outs[k].at[chip, pl.ds(c * hr, hr), :],
                                      send_sems.at[3 * k + j], recv_sems.at[3 * k + j], (x, y, 1 - c)))
        for cp in copies:
            cp.start()
        for cp in copies:
            cp.wait()

    arrays = [stacks[k] for k in names]
    outs = pl.pallas_call(
        body, name="allgather_forward_" + tag, in_specs=[HBM] * n, out_specs=[HBM] * n,
        input_output_aliases={k: k for k in range(n)},
        out_shape=[_sds(a.shape, a.dtype) for a in arrays],
        scratch_shapes=[pltpu.SemaphoreType.DMA((3 * n,)), pltpu.SemaphoreType.DMA((3 * n,))],
        compiler_params=pltpu.CompilerParams(collective_id=_COLLECTIVE["forward_" + tag]),
    )(*arrays)
    return dict(zip(names, outs))


def _owner_sum(part, from_sibling, from_chips, name, pos, shard_shape):
    kind = _KIND[name]
    _, pk, pw = from_chips.shape
    if kind == "row":
        tr, nb = pk, 1
        p_spec = pl.BlockSpec((tr, pw), lambda r, s: (s[0], s[1]))
        q_spec = pl.BlockSpec((tr, pw), lambda r, s: (s[0], 0))
        o_spec = pl.BlockSpec((tr, pw), lambda r, s: (0, s[1]))
    else:
        tr = 256
        nb = pk // tr
        if kind == "stack":
            p_spec = pl.BlockSpec((None, tr, pw), lambda r, s: (s[0], s[1] * nb + r, 0))
            q_spec = pl.BlockSpec((None, tr, pw), lambda r, s: (s[0], r, 0))
        else:
            p_spec = pl.BlockSpec((tr, pw), lambda r, s: (s[1] * nb + r, s[0]))
            q_spec = pl.BlockSpec((tr, pw), lambda r, s: (r, s[0]))
        o_spec = pl.BlockSpec((tr, pw), lambda r, s: (s[1] * nb + r, 0))

    def body(s_ref, p_ref, q_ref, r_ref, o_ref):
        acc = p_ref[...].astype(F32) + q_ref[...].astype(F32)
        for j in range(3):
            acc = acc + r_ref[j].astype(F32)
        o_ref[...] = acc

    return pl.pallas_call(
        body, name="grad_owner_sum_" + name,
        grid_spec=pltpu.PrefetchScalarGridSpec(
            num_scalar_prefetch=1, grid=(nb,),
            in_specs=[p_spec, q_spec, pl.BlockSpec((3, tr, pw), lambda r, s: (0, r, 0))],
            out_specs=o_spec),
        out_shape=_sds(shard_shape, F32),
        compiler_params=_cp(("arbitrary",), 32),
    )(pos, *_hbm(part, from_sibling, from_chips))


def _pair_share_start(shards, tag, after):
    names = list(shards)

    def issue(refs, send_sems, recv_sems):
        x, y, c = _mesh_pos()
        for hc in range(2):
            @pl.when(c == hc)
            def _():
                for k, g in enumerate(refs):
                    mine = _region_view(g, _KIND[names[k]], hc)
                    _remote(mine, mine, send_sems.at[k], recv_sems.at[k], (x, y, 1 - c)).start()

    return names, _split_start("grad_pair_share_start_" + tag, [shards[k] for k in names], len(names), issue, after,
                               (_COLLECTIVE["share_" + tag], _sibling_peers))


def _pair_share_wait(pending, tag, after):
    names, started = pending

    def waits(refs, send_sems, recv_sems):
        for k, g in enumerate(refs):
            region = _region_view(g, _KIND[names[k]], 0)
            _wait_both(region, region, send_sems.at[k], recv_sems.at[k])

    return dict(zip(names, _split_wait("grad_pair_share_wait_" + tag, started, waits, after)))


def _small_exchange_start(slots, after):
    def issue(refs, send_sems, recv_sems):
        x, y, c = _mesh_pos()
        mine = refs[0].at[4 * x + 2 * y + c]
        k = 0
        for px in range(2):
            for py in range(2):
                for pc in range(2):
                    if px + py + pc:
                        peer = (1 - x if px else x, 1 - y if py else y, 1 - c if pc else c)
                        _remote(mine, mine, send_sems.at[k], recv_sems.at[k], peer).start()
                        k += 1

    return _split_start("small_exchange_start", [slots], N_DEV - 1, issue, after)


def _small_exchange_wait(started, after):
    def waits(refs, send_sems, recv_sems):
        slot = refs[0].at[0]
        for k in range(N_DEV - 1):
            _wait_both(slot, slot, send_sems.at[k], recv_sems.at[k])

    return _split_wait("small_exchange_wait", started, waits, after)[0]


def _adam_math(w, g, m, v):
    m = ADAM_B1 * m + (1.0 - ADAM_B1) * g
    v = ADAM_B2 * v + (1.0 - ADAM_B2) * (g * g)
    m_hat = m / (1.0 - ADAM_B1 ** ADAM_STEP)
    v_hat = v / (1.0 - ADAM_B2 ** ADAM_STEP)
    delta = -ADAM_LR * (m_hat / (jnp.sqrt(v_hat) + ADAM_EPS) + ADAM_WD * w)
    return delta, m, v


def _adamw(w, g, m, v, name):
    rows, cols = w.shape
    fits = [t for t in range(SUBLANES, rows, SUBLANES) if rows % t == 0 and t * cols * 4 <= (3 << 19)]
    tr = max(fits) if fits else rows

    def body(w_ref, g_ref, m_ref, v_ref, d_ref, nm_ref, nv_ref, go_ref):
        g = g_ref[...]
        d, nm, nv = _adam_math(w_ref[...], g, m_ref[...], v_ref[...])
        d_ref[...] = d
        nm_ref[...] = nm
        nv_ref[...] = nv
        go_ref[...] = g

    spec = pl.BlockSpec((tr, cols), lambda i: (i, 0))
    return pl.pallas_call(
        body, name=name, grid=(rows // tr,), in_specs=[spec] * 4, out_specs=[spec] * 4,
        out_shape=[_sds(w.shape, F32)] * 4, compiler_params=_cp(("arbitrary",)),
    )(*_hbm(w, g, m, v))


def _small_sum_adamw(gathered, w, m, v):
    names = _SMALL_IN_CALL
    n = len(names)

    def body(*refs):
        a_ref = refs[0]
        w_refs, m_refs, v_refs = refs[1:1 + n], refs[1 + n:1 + 2 * n], refs[1 + 2 * n:1 + 3 * n]
        sum_ref = refs[1 + 3 * n]
        outs = refs[2 + 3 * n:]
        g = a_ref[0]
        for k in range(1, N_DEV):
            g = g + a_ref[k]
        sum_ref[...] = g
        for i, name in enumerate(names):
            r0, nr, nc = _SMALL_AT[name]
            gp = g[r0:r0 + nr, 0:nc]
            d, nm, nv = _adam_math(w_refs[i][...], gp, m_refs[i][...], v_refs[i][...])
            for k, val in enumerate((gp, d, nm, nv)):
                outs[4 * i + k][...] = val

    shapes = [w[k].shape for k in names]
    res = pl.pallas_call(
        body, name="small_sum_adamw",
        out_shape=[_sds((SMALL_ROWS, D_MODEL), F32)] + [_sds(s, F32) for s in shapes for _ in range(4)],
    )(gathered, *[w[k] for k in names], *[m[k] for k in names], *[v[k] for k in names])
    return res[0], {k: tuple(res[1 + 4 * i:5 + 4 * i]) for i, k in enumerate(names)}


_NAMES = ("g_mix", "w_in", "g_sgu", "w_s", "b_s", "sinks", "rel_bias", "w_pa", "w_pb", "w_out",
          "g_ffn", "w_up", "w_conv", "b_conv", "w_down", "g_final")

def kernel(x, g_mix, w_in, g_sgu, w_s, b_s, sinks, rel_bias, w_pa, w_pb, w_out, g_ffn, w_up, w_conv, b_conv, w_down, g_final, loss_target, m_g_mix, m_w_in, m_g_sgu, m_w_s, m_b_s, m_sinks, m_rel_bias, m_w_pa, m_w_pb, m_w_out, m_g_ffn, m_w_up, m_w_conv, m_b_conv, m_w_down, m_g_final, v_g_mix, v_w_in, v_g_sgu, v_w_s, v_b_s, v_sinks, v_rel_bias, v_w_pa, v_w_pb, v_w_out, v_g_ffn, v_w_up, v_w_conv, v_b_conv, v_w_down, v_g_final):
    w = dict(g_mix=g_mix, w_in=w_in, g_sgu=g_sgu, w_s=w_s, b_s=b_s, sinks=sinks, rel_bias=rel_bias, w_pa=w_pa, w_pb=w_pb,
             w_out=w_out, g_ffn=g_ffn, w_up=w_up, w_conv=w_conv, b_conv=b_conv, w_down=w_down, g_final=g_final)
    m = dict(g_mix=m_g_mix, w_in=m_w_in, g_sgu=m_g_sgu, w_s=m_w_s, b_s=m_b_s, sinks=m_sinks, rel_bias=m_rel_bias, w_pa=m_w_pa,
             w_pb=m_w_pb, w_out=m_w_out, g_ffn=m_g_ffn, w_up=m_w_up, w_conv=m_w_conv, b_conv=m_b_conv, w_down=m_w_down,
             g_final=m_g_final)
    v = dict(g_mix=v_g_mix, w_in=v_w_in, g_sgu=v_g_sgu, w_s=v_w_s, b_s=v_b_s, sinks=v_sinks, rel_bias=v_rel_bias, w_pa=v_w_pa,
             w_pb=v_w_pb, w_out=v_w_out, g_ffn=v_g_ffn, w_up=v_w_up, w_conv=v_w_conv, b_conv=v_b_conv, w_down=v_w_down,
             g_final=v_g_final)
    xi, yi, ci = _mesh_pos()
    me = 2 * xi + yi

    shard = {n: w[n][0] for n in _BIG}
    shard_shapes = {n: shard[n].shape for n in _BIG}
    wc_shard = w["w_conv"][0]
    wc_pad = jnp.pad(wc_shard, ((0, 5), (0, 0)))
    own = {n: _own_slot(shard[n].astype(BF16), N_CHIPS, me) for n in _BIG}
    stacks, wc_all = _allgather_weights({"w_in": own["w_in"]}, _own_slot(wc_pad, N_CHIPS, me))
    proj_gather = _allgather_start({n: own[n] for n in _MIXER[1:]}, "proj", stacks["w_in"])
    ffn_gather = _allgather_start({n: own[n] for n in _FFN}, "ffn", proj_gather[1][-1])
    w_conv_full = jnp.concatenate([wc_all[i, :3] for i in range(N_CHIPS)], axis=1)
    w_in_full = _unstack_w_in(stacks["w_in"])
    pos = jnp.stack([me, ci])

    def proj_weights(done):
        st = _allgather_forward(_allgather_wait(proj_gather, "proj", done), "proj")
        return st["w_pa"], st["w_pb"], st["w_out"].reshape(D_MODEL, D_MODEL)

    def ffn_weights(done):
        st = _allgather_forward(_allgather_wait(ffn_gather, "ffn", done), "ffn")
        return st["w_up"], st["w_down"].reshape(D_FF, D_MODEL)

    groups = {}

    def stage1(group, parts):
        groups[group] = dict(parts=parts, pair=_pair_exchange_start(parts, group, None))
        return groups[group]["pair"][1][-1]

    def stage2(group, after, order_after):
        g = groups[group]
        g["parts"], g["sib"] = _pair_exchange_wait(g["pair"], group, after)
        g["chip"] = _chip_exchange_start({n: _pair_add(g["parts"][n], g["sib"][n], n, pos) for n in g["parts"]}, group, order_after)
        return g["chip"][1][-1]

    def stage3(group, after, order_after):
        g = groups[group]
        got = _chip_exchange_wait(g["chip"], group, after)
        g["share"] = _pair_share_start(
            {n: _owner_sum(g["parts"][n], g["sib"][n], got[n], n, pos, shard_shapes[n]) for n in g["parts"]}, group, order_after)
        return g["share"][1][-1]

    grads, deltas, new_m, new_v = {}, {}, {}, {}

    def stage4(group, after):
        g_shard = _pair_share_wait(groups[group]["share"], group, after)
        last = None
        for n in g_shard:
            g = _tie(g_shard[n], last)
            if n == "w_in":
                d, nm, nv, gt = _adamw(shard[n].T, g.T, m[n][0].T, v[n][0].T, "adamw_" + n)
                grads[n], deltas[n], new_m[n], new_v[n] = gt.T[None], d.T[None], nm.T[None], nv.T[None]
            else:
                d, nm, nv, go = _adamw(shard[n], g, m[n][0], v[n][0], "adamw_" + n)
                grads[n], deltas[n], new_m[n], new_v[n] = go[None], d[None], nm[None], nv[None]
            last = nv
        return last

    def on_grads(group, parts):
        token = stage1(group, parts)
        some = next(iter(parts.values()))
        if group == "proj":
            token = stage2("ffn", some, token)
        if group == "in":
            token = stage2("proj", some, token)
            token = stage3("ffn", some, token)
            token = stage2("in", token, token)
        return token

    loss, grad_x, small, big = _local_step(
        x, loss_target, w["g_mix"], w["g_sgu"], w["w_s"][0], w["b_s"][0], w["sinks"], w["rel_bias"], w["g_ffn"],
        w["b_conv"], w["g_final"], w_in_full, w_conv_full, proj_weights, ffn_weights, on_grads, ffn_gather[1][-1])

    small["loss"] = loss
    small_gather = _small_exchange_start(_own_slot(_pack_small(small), N_DEV, 2 * me + ci), grad_x)
    token = stage3("proj", grad_x, small_gather[-1])
    done = stage4("ffn", token)
    done = stage4("proj", done)
    token = stage3("in", done, None)
    all_small = _small_exchange_wait(small_gather, token)
    two_d = {n: (lambda a, n=n: a.reshape(_SMALL_AT[n][1:])) for n in _SMALL_IN_CALL}
    s_sum, s_out = _small_sum_adamw(all_small, *[{n: two_d[n](p[n]) for n in _SMALL_IN_CALL} for p in (w, m, v)])
    stage4("in", all_small)
    for n in _SMALL_IN_CALL:
        grads[n], deltas[n], new_m[n], new_v[n] = [a.reshape(w[n].shape) for a in s_out[n]]

    def rows(n):
        r0, nr, _ = _SMALL_AT[n]
        return s_sum[r0:r0 + nr]

    wcols = wc_shard.shape[1]
    g_wc = lax.dynamic_slice(_unwide(rows("w_conv"), 3), (0, me * wcols), (3, wcols))
    d, nm, nv, _ = _adamw(wc_shard, g_wc, m["w_conv"][0], v["w_conv"][0], "adamw_w_conv")
    grads["w_conv"], deltas["w_conv"], new_m["w_conv"], new_v["w_conv"] = g_wc[None], d[None], nm[None], nv[None]
    d, nm, nv, go = _adamw(w["b_conv"], _unwide(rows("b_conv"), 1), m["b_conv"], v["b_conv"], "adamw_b_conv")
    grads["b_conv"], deltas["b_conv"], new_m["b_conv"], new_v["b_conv"] = go, d, nm, nv
    g_rb = rows("rel_bias")[:, :N_BUCKETS * N_HEADS].reshape(N_BUCKETS, N_HEADS)
    d, nm, nv, go = _adamw(w["rel_bias"], g_rb, m["rel_bias"], v["rel_bias"], "adamw_rel_bias")
    grads["rel_bias"], deltas["rel_bias"], new_m["rel_bias"], new_v["rel_bias"] = go, d, nm, nv
    flat_s = (A_GROUPS * CHUNK, CHUNK)
    d, nm, nv, go = _adamw(w["w_s"].reshape(flat_s), rows("w_s").reshape(flat_s), m["w_s"].reshape(flat_s),
                           v["w_s"].reshape(flat_s), "adamw_w_s")
    grads["w_s"], deltas["w_s"], new_m["w_s"], new_v["w_s"] = [a.reshape(w["w_s"].shape) for a in (go, d, nm, nv)]

    return (s_sum[0, 0], grad_x, *[grads[n] for n in _NAMES], *[deltas[n] for n in _NAMES],
            *[new_m[n] for n in _NAMES], *[new_v[n] for n in _NAMES])
```

```python
import functools

import numpy as np
import jax
import jax.numpy as jnp
from jax import lax
from jax.experimental import pallas as pl
from jax.experimental.pallas import tpu as pltpu

F32 = jnp.float32
BF16 = jnp.bfloat16

D_MODEL = 1024
CHUNK = 128
A_GROUPS = 4
A_WIDTH = 512
N_HEADS = 8
HEAD_DIM = 64
Q_DIM = 512
KV_DIM = 128
N_BUCKETS = 32
MAX_DISTANCE = 128
D_FF = 2816
EPS = 1e-6
NEG_INF = -1e30
G_DIM = 2 * D_MODEL
A_DIM = 2 * A_WIDTH
B_DIM = Q_DIM + 2 * KV_DIM
LANES = 128
SUBLANES = 8
ROW_TILE = 512
WIDE_ROW_TILE = 256
COL_CHUNK = 512
GRAD_ROW_TILE = 512
BF16_ROWS = 16
N_CHIPS = 4
N_DEV = 8

ADAM_LR = 0.001
ADAM_B1 = 0.9
ADAM_B2 = 0.999
ADAM_EPS = 1e-08
ADAM_WD = 0.01
ADAM_STEP = 10

MESH = pl.DeviceIdType.MESH
_GELU_C = 0.7978845608028654
_GELU_A = 0.044715


def _cp(sem=None, vmem_mb=None):
    kw = {}
    if sem is not None:
        kw["dimension_semantics"] = sem
    if vmem_mb is not None:
        kw["vmem_limit_bytes"] = vmem_mb << 20
    return pltpu.CompilerParams(**kw)


def _dot(a, b):
    return jnp.dot(a, b, preferred_element_type=F32)


def _dot_nt(a, b):
    return lax.dot_general(a, b, (((1,), (1,)), ((), ())), preferred_element_type=F32)


def _dot_tn(a, b):
    return lax.dot_general(a, b, (((0,), (0,)), ((), ())), preferred_element_type=F32)


def _rms_r(x):
    return lax.rsqrt(jnp.mean(x * x, axis=-1, keepdims=True) + EPS)


def _rms_bwd(dh, n, r, g):
    dn = dh * g
    return r * (dn - n * jnp.mean(dn * n, axis=-1, keepdims=True))


def _gelu(x):
    t = jnp.tanh(_GELU_C * (x + _GELU_A * (x * x * x)))
    return 0.5 * x * (1.0 + t), t


def _gelu_grad(x, t):
    return 0.5 * (1.0 + t) + 0.5 * x * (1.0 - t * t) * (_GELU_C * (1.0 + 3.0 * _GELU_A * x * x))


def _sigmoid(x):
    return 1.0 / (1.0 + jnp.exp(-x))


def _tie(x, dep):
    return x if dep is None else lax.optimization_barrier((x, dep))[0]


def _row(tm, w):
    return pl.BlockSpec((tm, w), lambda i: (i, 0))


def _full(shape):
    nd = len(shape)
    return pl.BlockSpec(tuple(shape), lambda *_: (0,) * nd)


def _resident(shape):
    nd = len(shape)
    return pl.BlockSpec(tuple(shape), lambda *_: (0,) * nd, pipeline_mode=pl.Buffered(1))


def _sds(shape, dtype):
    return jax.ShapeDtypeStruct(tuple(shape), dtype)


def _hbm(*arrays):
    return [pltpu.with_memory_space_constraint(a, pltpu.HBM) for a in arrays]


HBM = pl.BlockSpec(memory_space=pltpu.HBM)
ANY = pl.BlockSpec(memory_space=pl.ANY)
SEM = pl.BlockSpec(memory_space=pltpu.SEMAPHORE)


def _band_buckets():
    i = np.arange(CHUNK)[:, None]
    j = np.arange(2 * CHUNK)[None, :]
    dist = i + CHUNK - j
    valid = (dist >= 0) & (dist < CHUNK)
    d = np.clip(dist, 0, None)
    max_exact = N_BUCKETS // 2
    large = max_exact + (np.log(np.maximum(d, 1) / max_exact) / np.log(MAX_DISTANCE / max_exact)
                         * (N_BUCKETS - max_exact)).astype(np.int32)
    large = np.minimum(large, N_BUCKETS - 1)
    buckets = np.where(d < max_exact, d, large).astype(np.int32)
    return np.where(valid, buckets, -1).astype(np.int32)


_A_COLS = slice(0, A_DIM)
_B_COLS = slice(A_DIM, A_DIM + B_DIM)
_G_COLS = slice(A_DIM + B_DIM, A_DIM + B_DIM + G_DIM)


def _inproj(x2, g_mix, w_in, g_sgu, w_s, b_st, tm, after=None):
    T = x2.shape[0]
    order = [] if after is None else [after]

    def body(*refs):
        x_ref, g_ref, w_ref, gs_ref, ws_ref, bs_ref = refs[:6]
        pg_ref, pa_ref, pb_ref, h_ref, ya_ref = refs[6 + len(order):]
        x = x_ref[...]
        h = (x * _rms_r(x) * g_ref[...]).astype(BF16)
        h_ref[...] = h
        pa = _dot(h, w_ref[:, _A_COLS]).astype(BF16)
        pa_ref[...] = pa
        pb_ref[...] = _dot(h, w_ref[:, _B_COLS]).astype(BF16)
        pg_ref[...] = _dot(h, w_ref[:, _G_COLS]).astype(BF16)
        _sgu_apply(pa.astype(F32), gs_ref[...], ws_ref, bs_ref, ya_ref)

    return pl.pallas_call(
        body, name="inproj", grid=(T // tm,),
        in_specs=[_row(tm, D_MODEL), _full(g_mix.shape), _resident(w_in.shape), _full(g_sgu.shape), _full(w_s.shape),
                  _full(b_st.shape)] + [ANY] * len(order),
        out_specs=[_row(tm, G_DIM), _row(tm, A_DIM), _row(tm, B_DIM), _row(tm, D_MODEL), _row(tm, A_WIDTH)],
        out_shape=[_sds((T, G_DIM), BF16), _sds((T, A_DIM), BF16), _sds((T, B_DIM), BF16), _sds((T, D_MODEL), BF16),
                   _sds((T, A_WIDTH), BF16)],
        compiler_params=_cp(("arbitrary",), 48),
    )(*_hbm(x2, g_mix, w_in, g_sgu, w_s, b_st), *order)


def _sgu_parts(p, g):
    pu = p[:, :A_WIDTH]
    pv = p[:, A_WIDTH:]
    u, tu = _gelu(pu)
    vv, tv = _gelu(pv)
    rv = _rms_r(vv)
    vn = (vv * rv * g).astype(BF16)
    return pu, pv, u, tu, vv, tv, rv, vn


def _tril():
    r = lax.broadcasted_iota(jnp.int32, (CHUNK, CHUNK), 0)
    c = lax.broadcasted_iota(jnp.int32, (CHUNK, CHUNK), 1)
    return r >= c


def _sgu_apply(p, g, ws_ref, bs_ref, y_ref):
    tril = _tril()
    _, _, u, _, _, _, _, vn = _sgu_parts(p, g)
    for gi in range(A_GROUPS):
        wm = jnp.where(tril, ws_ref[gi], 0.0).astype(BF16)
        bcol = bs_ref[:, gi:gi + 1]
        cs = slice(gi * CHUNK, (gi + 1) * CHUNK)
        for c in range(p.shape[0] // CHUNK):
            rs = slice(c * CHUNK, (c + 1) * CHUNK)
            s = _dot(wm, vn[rs, cs]) + bcol
            y_ref[rs, cs] = (u[rs, cs] * s).astype(BF16)


HEAD_ROWS = N_HEADS * CHUNK


def _head_rows(h):
    return slice(h * CHUNK, (h + 1) * CHUNK)


def _attn_setup(bias_scr, sink_scr, kvar_scr, qkv_ref, bk_ref, rel_ref, sink_ref):
    @pl.when(pl.program_id(0) == 0)
    def _():
        bk = bk_ref[...]
        for h in range(N_HEADS):
            acc = jnp.full((CHUNK, 2 * CHUNK), NEG_INF, F32)
            for b in range(N_BUCKETS):
                acc = jnp.where(bk == b, rel_ref[b, h], acc)
            bias_scr[_head_rows(h), :] = acc
            sink_scr[_head_rows(h), :] = jnp.full((CHUNK, LANES), sink_ref[0, h], F32)

    seq = qkv_ref.shape[0]
    rows_per = 2 * CHUNK
    for is_v in range(2):
        c0 = Q_DIM + is_v * KV_DIM
        for r in range(seq // rows_per):
            rs = slice(r * rows_per, (r + 1) * rows_per)
            a = qkv_ref[rs, c0:c0 + KV_DIM].astype(F32)
            lane = lax.broadcasted_iota(jnp.int32, a.shape, 1)
            lo = jnp.where(lane < HEAD_DIM, a, 0.0)
            hi = jnp.where(lane >= HEAD_DIM, a, 0.0)
            kvar_scr[4 * is_v + 0, rs, :] = lo.astype(BF16)
            kvar_scr[4 * is_v + 1, rs, :] = pltpu.roll(lo, HEAD_DIM, 1).astype(BF16)
            kvar_scr[4 * is_v + 2, rs, :] = pltpu.roll(hi, HEAD_DIM, 1).astype(BF16)
            kvar_scr[4 * is_v + 3, rs, :] = hi.astype(BF16)


def _rowsum(a, ones):
    hi = a.astype(BF16)
    lo = (a - hi.astype(F32)).astype(BF16)
    return _dot(hi, ones) + _dot(lo, ones)


def _both(a):
    return jnp.concatenate([a, a], axis=1)


def _attn_probs(qkv_ref, r0, n, kv, bias_scr, sink_scr, ones):
    s = jnp.concatenate([_dot_nt(qkv_ref[pl.ds(r0, CHUNK), (h // 2) * LANES:(h // 2 + 1) * LANES], kv[h // 4][h % 2])
                         for h in range(N_HEADS)], axis=0)
    s = s * (HEAD_DIM ** -0.5) + bias_scr[...]
    col = lax.broadcasted_iota(jnp.int32, s.shape, 1)
    s = jnp.where((col < CHUNK) & (n == 0), NEG_INF, s)
    sink = sink_scr[...]
    m = jnp.maximum(jnp.max(s, axis=-1, keepdims=True), sink)
    p = jnp.exp(s - _both(m))
    es = jnp.exp(sink - m)
    inv = 1.0 / (_dot(p.astype(BF16), ones) + es)
    return p * _both(inv), es * inv


def _attn_block_inputs(kvar_scr, n):
    r0 = pl.multiple_of(n * CHUNK, CHUNK)
    rp = pl.multiple_of(jnp.maximum(n - 1, 0) * CHUNK, CHUNK)

    def both(idx):
        return jnp.concatenate([kvar_scr[idx, pl.ds(rp, CHUNK), :], kvar_scr[idx, pl.ds(r0, CHUNK), :]], axis=0)

    kv = ((both(0), both(1)), (both(2), both(3)))
    vv = ((both(4), both(5)), (both(6), both(7)))
    return r0, kv, vv


def _attn_fwd(proj_b, sinks, rel_bias, n_seq, seq):
    nb = seq // CHUNK
    bk = jnp.asarray(_band_buckets())

    def body(qkv_ref, bk_ref, rel_ref, sink_ref, o_ref, bias_scr, sink_scr, kvar_scr):
        _attn_setup(bias_scr, sink_scr, kvar_scr, qkv_ref, bk_ref, rel_ref, sink_ref)
        ones = jnp.ones((2 * CHUNK, LANES), BF16)

        def blk(n, carry):
            r0, kv, vv = _attn_block_inputs(kvar_scr, n)
            prob, _ = _attn_probs(qkv_ref, r0, n, kv, bias_scr, sink_scr, ones)
            pb = prob.astype(BF16)
            for pr in range(N_HEADS // 2):
                acc = _dot(pb[_head_rows(2 * pr)], vv[pr // 2][0]) + _dot(pb[_head_rows(2 * pr + 1)], vv[pr // 2][1])
                o_ref[pl.ds(r0, CHUNK), pr * LANES:(pr + 1) * LANES] = acc.astype(BF16)
            return carry

        lax.fori_loop(0, nb, blk, 0)

    smem = pl.BlockSpec(memory_space=pltpu.SMEM)
    return pl.pallas_call(
        body, name="attn_fwd", grid=(n_seq,),
        in_specs=[_row(seq, B_DIM), _full(bk.shape), smem, smem],
        out_specs=_row(seq, Q_DIM), out_shape=_sds((n_seq * seq, Q_DIM), BF16),
        scratch_shapes=[pltpu.VMEM((HEAD_ROWS, 2 * CHUNK), F32), pltpu.VMEM((HEAD_ROWS, LANES), F32),
                        pltpu.VMEM((8, seq, KV_DIM), BF16)],
        compiler_params=_cp(("arbitrary",), 40),
    )(*_hbm(proj_b, bk), rel_bias, sinks)


def _dot_stacked(a, w_ref):
    return jnp.concatenate([_dot(a, w_ref[i]) for i in range(N_CHIPS)], axis=1)


def _dot_nt_stacked(a, w_ref):
    w = w_ref.shape[2]
    acc = _dot_nt(a[:, :w], w_ref[0])
    for i in range(1, N_CHIPS):
        acc = acc + _dot_nt(a[:, i * w:(i + 1) * w], w_ref[i])
    return acc


def _merge_fwd(x2, y_a, y_b, proj_g, w_pa, w_pb, w_out, tm):
    T = x2.shape[0]

    def body(x_ref, ya_ref, yb_ref, g_ref, wpa_ref, wpb_ref, wo_ref, x1_ref, mg_ref):
        g = g_ref[...].astype(F32)
        pa = _dot_stacked(ya_ref[...], wpa_ref)
        pb = _dot_stacked(yb_ref[...], wpb_ref)
        merged = (_sigmoid(g[:, :D_MODEL]) * pa + _sigmoid(g[:, D_MODEL:]) * pb).astype(BF16)
        mg_ref[...] = merged
        x1_ref[...] = x_ref[...] + _dot(merged, wo_ref[...])

    return pl.pallas_call(
        body, name="merge_fwd", grid=(T // tm,),
        in_specs=[_row(tm, D_MODEL), _row(tm, A_WIDTH), _row(tm, Q_DIM), _row(tm, G_DIM),
                  _resident(w_pa.shape), _resident(w_pb.shape), _resident(w_out.shape)],
        out_specs=[_row(tm, D_MODEL), _row(tm, D_MODEL)],
        out_shape=[_sds((T, D_MODEL), F32), _sds((T, D_MODEL), BF16)],
        compiler_params=_cp(("arbitrary",), 40),
    )(*_hbm(x2, y_a, y_b, proj_g, w_pa, w_pb, w_out))


def _upproj(x1, g_ffn, w_up, w_conv, b_conv, tm, seq):
    T = x1.shape[0]
    cw = w_up.shape[2]
    tiles_per_seq = seq // tm

    def body(x_ref, g_ref, w_ref, wc_ref, bc_ref, u_ref, h_ref, gate_ref, val_ref, tail_scr):
        at_start = (pl.program_id(0) % tiles_per_seq) == 0
        x = x_ref[...]
        h = (x * _rms_r(x) * g_ref[...]).astype(BF16)
        h_ref[...] = h
        for i in range(N_CHIPS):
            cs = slice(i * cw, (i + 1) * cw)
            u = _dot(h, w_ref[i])
            u_ref[:, cs] = u.astype(BF16)
            hl = jnp.where(at_start, 0.0, tail_scr[SUBLANES - 2:SUBLANES, cs])
            tail_scr[:, cs] = u[tm - SUBLANES:]
            up = _conv_out((u, _shift_down(u, hl, 1), _shift_down(u, hl, 2)), wc_ref[:, cs], bc_ref[:, cs])
            out_ref = gate_ref if i < N_CHIPS // 2 else val_ref
            out_ref[:, (i % 2) * cw:(i % 2 + 1) * cw] = up.astype(BF16)

    return pl.pallas_call(
        body, name="upproj", grid=(T // tm,),
        in_specs=[_row(tm, D_MODEL), _full(g_ffn.shape), _resident(w_up.shape), _full(w_conv.shape), _full(b_conv.shape)],
        out_specs=[_row(tm, 2 * D_FF), _row(tm, D_MODEL), _row(tm, D_FF), _row(tm, D_FF)],
        out_shape=[_sds((T, 2 * D_FF), BF16), _sds((T, D_MODEL), BF16), _sds((T, D_FF), BF16), _sds((T, D_FF), BF16)],
        scratch_shapes=[pltpu.VMEM((SUBLANES, 2 * D_FF), F32)],
        compiler_params=_cp(("arbitrary",), 56),
    )(*_hbm(x1, g_ffn, w_up, w_conv, b_conv))


def _shift_down(u, halo, k):
    rolled = pltpu.roll(u, k, 0)
    head = rolled[:SUBLANES]
    row = lax.broadcasted_iota(jnp.int32, head.shape, 0)
    if k == 1:
        head = jnp.where(row == 0, halo[1:2], head)
    else:
        head = jnp.where(row == 0, halo[0:1], jnp.where(row == 1, halo[1:2], head))
    return jnp.concatenate([head, rolled[SUBLANES:]], axis=0)


def _shift_up(d, halo, k):
    tm = d.shape[0]
    rolled = pltpu.roll(d, tm - k, 0)
    tail = rolled[tm - SUBLANES:]
    row = lax.broadcasted_iota(jnp.int32, tail.shape, 0)
    if k == 1:
        tail = jnp.where(row == SUBLANES - 1, halo[0:1], tail)
    else:
        tail = jnp.where(row == SUBLANES - 2, halo[0:1], jnp.where(row == SUBLANES - 1, halo[1:2], tail))
    return jnp.concatenate([rolled[:tm - SUBLANES], tail], axis=0)


def _conv_out(taps, wc, bc):
    u, u1, u2 = taps
    return wc[0:1] * u2 + wc[1:2] * u1 + wc[2:3] * u + bc


def _ffn_down_loss(gate, val, x1, target, w_down, g_final, tm):
    T = x1.shape[0]
    half = D_FF // 2

    def body(gt_ref, vl_ref, x1_ref, t_ref, wd_ref, g_ref, dx2_ref, loss_ref, gg_ref):
        i = pl.program_id(0)
        acc = jnp.zeros((tm, D_MODEL), F32)
        for j in range(2):
            gc = slice(j * half, (j + 1) * half)
            gate = gt_ref[:, gc].astype(F32)
            act = (gate * _sigmoid(gate) * vl_ref[:, gc].astype(F32)).astype(BF16)
            acc = acc + _dot(act, wd_ref[gc, :])
        x2 = x1_ref[...] + acc
        r = _rms_r(x2)
        n = x2 * r
        g = g_ref[...]
        diff = n * g - t_ref[...]
        dy = diff * (1.0 / D_MODEL)
        dx2_ref[...] = _rms_bwd(dy, n, r, g)

        @pl.when(i == 0)
        def _():
            loss_ref[...] = jnp.zeros_like(loss_ref)
            gg_ref[...] = jnp.zeros_like(gg_ref)

        loss_ref[...] += 0.5 * jnp.sum(jnp.mean(diff * diff, axis=-1, keepdims=True), axis=0, keepdims=True)
        gg_ref[...] += jnp.sum(dy * n, axis=0, keepdims=True)

    return pl.pallas_call(
        body, name="ffn_down_loss", grid=(T // tm,),
        in_specs=[_row(tm, D_FF), _row(tm, D_FF), _row(tm, D_MODEL), _row(tm, D_MODEL),
                  _resident(w_down.shape), _full(g_final.shape)],
        out_specs=[_row(tm, D_MODEL), _full((1, 1)), _full((1, D_MODEL))],
        out_shape=[_sds((T, D_MODEL), F32), _sds((1, 1), F32), _sds((1, D_MODEL), F32)],
        compiler_params=_cp(("arbitrary",), 48),
    )(*_hbm(gate, val, x1, target, w_down, g_final))


def _ffn_bwd_act(gate, val, dx2, w_down, tm):
    T = dx2.shape[0]
    half = D_FF // 2
    nt = T // tm

    def body(g_ref, v_ref, dx_ref, wd_ref, dg_ref, dv_ref, gwd_out, gbg_ref, gbv_ref, gwd_ref):
        i = pl.program_id(1)

        @pl.when(i == 0)
        def _():
            for r in (gwd_ref, gbg_ref, gbv_ref):
                r[...] = jnp.zeros_like(r)

        dx = dx_ref[...].astype(BF16)
        for c0 in range(0, half, COL_CHUNK):
            cs = slice(c0, min(c0 + COL_CHUNK, half))
            gate = g_ref[:, cs].astype(F32)
            val = v_ref[:, cs].astype(F32)
            sg = _sigmoid(gate)
            silu = gate * sg
            d_act = _dot_nt(dx, wd_ref[cs, :])
            d_val = d_act * silu
            d_gate = d_act * val * (sg * (1.0 + gate * (1.0 - sg)))
            dg_ref[:, cs] = d_gate.astype(BF16)
            dv_ref[:, cs] = d_val.astype(BF16)
            gwd_ref[cs, :] += _dot_tn((silu * val).astype(BF16), dx)
            gbg_ref[:, cs] += jnp.sum(d_gate, axis=0, keepdims=True)
            gbv_ref[:, cs] += jnp.sum(d_val, axis=0, keepdims=True)

        @pl.when(i == nt - 1)
        def _():
            gwd_out[...] = gwd_ref[...].astype(BF16)

    tile = pl.BlockSpec((tm, half), lambda j, i: (i, j))
    vec = pl.BlockSpec((1, half), lambda j, i: (0, j))
    wrows = pl.BlockSpec((half, D_MODEL), lambda j, i: (j, 0))
    return pl.pallas_call(
        body, name="ffn_bwd_act", grid=(2, nt),
        in_specs=[tile, tile, pl.BlockSpec((tm, D_MODEL), lambda j, i: (i, 0)), wrows],
        out_specs=[tile, tile, wrows, vec, vec],
        out_shape=[_sds((T, D_FF), BF16), _sds((T, D_FF), BF16), _sds((D_FF, D_MODEL), BF16),
                   _sds((1, D_FF), F32), _sds((1, D_FF), F32)],
        scratch_shapes=[pltpu.VMEM((half, D_MODEL), F32)],
        compiler_params=_cp(("arbitrary", "arbitrary"), 56),
    )(*_hbm(gate, val, dx2, w_down))


def _ffn_bwd_up(d_gate, d_val, upre, dx2, x1, g_ffn, w_conv, w_up, tm, seq):
    T = dx2.shape[0]
    tiles_per_seq = seq // tm
    k16 = tm // BF16_ROWS
    n16 = T // BF16_ROWS
    cw = D_FF // 2

    def body(dg_ref, dv_ref, hg_ref, hv_ref, u_ref, dx2_ref, x1_ref, g_ref, wc_ref, wu_ref, du_ref, dx1_ref, gg_ref, gwc_ref):
        i = pl.program_id(0)
        at_end = (i % tiles_per_seq) == tiles_per_seq - 1

        @pl.when(i == 0)
        def _():
            gg_ref[...] = jnp.zeros_like(gg_ref)
            gwc_ref[...] = jnp.zeros_like(gwc_ref)

        dh = jnp.zeros((tm, D_MODEL), F32)
        for j in range(4):
            src, hsrc = (dg_ref, hg_ref) if j < 2 else (dv_ref, hv_ref)
            ls = slice((j % 2) * cw, (j % 2 + 1) * cw)
            cs = slice(j * cw, (j + 1) * cw)
            d = src[:, ls].astype(F32)
            hl = hsrc[:, ls].astype(F32)[0:2]
            hl = jnp.where(at_end, 0.0, hl)
            wc = wc_ref[:, cs]
            d1 = _shift_up(d, hl, 1)
            d2 = _shift_up(d, hl, 2)
            du = (wc[2:3] * d + wc[1:2] * d1 + wc[0:1] * d2).astype(BF16)
            du_ref[:, cs] = du
            dh = dh + _dot_nt(du, wu_ref[j])
            u = u_ref[:, cs].astype(F32)
            gwc_ref[0:1, cs] += jnp.sum(d2 * u, axis=0, keepdims=True)
            gwc_ref[1:2, cs] += jnp.sum(d1 * u, axis=0, keepdims=True)
            gwc_ref[2:3, cs] += jnp.sum(d * u, axis=0, keepdims=True)
        x = x1_ref[...]
        r = _rms_r(x)
        n = x * r
        dx1_ref[...] = dx2_ref[...] + _rms_bwd(dh, n, r, g_ref[...])
        gg_ref[...] += jnp.sum(dh * n, axis=0, keepdims=True)

    nxt = pl.BlockSpec((BF16_ROWS, D_FF), lambda i: (jnp.minimum((i + 1) * k16, n16 - 1), 0))
    return pl.pallas_call(
        body, name="ffn_bwd_up", grid=(T // tm,),
        in_specs=[_row(tm, D_FF), _row(tm, D_FF), nxt, nxt, _row(tm, 2 * D_FF), _row(tm, D_MODEL), _row(tm, D_MODEL),
                  _full(g_ffn.shape), _full(w_conv.shape), _resident(w_up.shape)],
        out_specs=[_row(tm, 2 * D_FF), _row(tm, D_MODEL), _full((1, D_MODEL)), _full((3, 2 * D_FF))],
        out_shape=[_sds((T, 2 * D_FF), BF16), _sds((T, D_MODEL), F32), _sds((1, D_MODEL), F32), _sds((3, 2 * D_FF), F32)],
        compiler_params=_cp(("arbitrary",), 56),
    )(*_hbm(d_gate, d_val, d_gate, d_val, upre, dx2, x1, g_ffn, w_conv, w_up))


def _matmul_tn(a, b, tn, tk, name):
    T, M = a.shape
    N = b.shape[1]
    nk = T // tk

    def body(a_ref, b_ref, o_ref, acc_ref):
        k = pl.program_id(1)

        @pl.when(k == 0)
        def _():
            acc_ref[...] = jnp.zeros_like(acc_ref)

        acc_ref[...] += _dot_tn(a_ref[...], b_ref[...])

        @pl.when(k == nk - 1)
        def _():
            o_ref[...] = acc_ref[...].astype(BF16)

    return pl.pallas_call(
        body, name=name, grid=(N // tn, nk),
        in_specs=[pl.BlockSpec((tk, M), lambda j, k: (k, 0)), pl.BlockSpec((tk, tn), lambda j, k: (k, j))],
        out_specs=pl.BlockSpec((M, tn), lambda j, k: (0, j)), out_shape=_sds((M, N), BF16),
        scratch_shapes=[pltpu.VMEM((M, tn), F32)],
        compiler_params=_cp(("arbitrary", "arbitrary"), 48),
    )(*_hbm(a, b))


def _merge_bwd(dx1, merged, y_a, y_b, proj_g, proj_a, w_pa, w_pb, w_out, g_sgu, w_s, b_st, tm, after=None):
    T = dx1.shape[0]

    nt = T // tm
    pshape = (A_WIDTH, D_MODEL)
    order = [] if after is None else [after]

    def body(*refs):
        dx_ref, mg_ref, ya_ref, yb_ref, g_ref, p_ref, wpa_ref, wpb_ref, wo_ref, gs_ref, ws_ref, bs_ref = refs[:12]
        (dg_ref, da_ref, dyb_ref, gwo_out, gwpa_out, gwpb_out, gws_ref, gbs_ref, gg_ref,
         gwo_ref, gwpa_ref, gwpb_ref) = refs[12 + len(order):]
        i = pl.program_id(0)

        @pl.when(i == 0)
        def _():
            for r in (gwo_ref, gwpa_ref, gwpb_ref, gws_ref, gbs_ref, gg_ref):
                r[...] = jnp.zeros_like(r)

        dx = dx_ref[...].astype(BF16)
        dm = _dot_nt(dx, wo_ref[...])
        g = g_ref[...].astype(F32)
        ya = ya_ref[...]
        yb = yb_ref[...]
        pa = _dot_stacked(ya, wpa_ref)
        pb = _dot_stacked(yb, wpb_ref)
        sa = _sigmoid(g[:, :D_MODEL])
        sb = _sigmoid(g[:, D_MODEL:])
        dpa = (dm * sa).astype(BF16)
        dpb = (dm * sb).astype(BF16)
        dg_ref[:, :D_MODEL] = (dm * pa * (sa * (1.0 - sa))).astype(BF16)
        dg_ref[:, D_MODEL:] = (dm * pb * (sb * (1.0 - sb))).astype(BF16)
        d_ya = _dot_nt_stacked(dpa, wpa_ref).astype(BF16)
        dyb_ref[...] = _dot_nt_stacked(dpb, wpb_ref).astype(BF16)
        _sgu_bwd_apply(p_ref[...].astype(F32), d_ya.astype(F32), gs_ref[...], ws_ref, bs_ref, da_ref, gws_ref, gbs_ref, gg_ref)
        gwo_ref[...] += _dot_tn(mg_ref[...], dx)
        gwpa_ref[...] += _dot_tn(ya, dpa)
        gwpb_ref[...] += _dot_tn(yb, dpb)

        @pl.when(i == nt - 1)
        def _():
            gwo_out[...] = gwo_ref[...].astype(BF16)
            gwpa_out[...] = gwpa_ref[...].astype(BF16)
            gwpb_out[...] = gwpb_ref[...].astype(BF16)

    return pl.pallas_call(
        body, name="merge_bwd", grid=(nt,),
        in_specs=[_row(tm, D_MODEL), _row(tm, D_MODEL), _row(tm, A_WIDTH), _row(tm, Q_DIM), _row(tm, G_DIM), _row(tm, A_DIM),
                  _resident(w_pa.shape), _resident(w_pb.shape), _resident(w_out.shape),
                  _full(g_sgu.shape), _full(w_s.shape), _full(b_st.shape)] + [ANY] * len(order),
        out_specs=[_row(tm, G_DIM), _row(tm, A_DIM), _row(tm, Q_DIM),
                   _full(w_out.shape), _full(pshape), _full(pshape), _full(w_s.shape), _full(b_st.shape), _full(g_sgu.shape)],
        out_shape=[_sds((T, G_DIM), BF16), _sds((T, A_DIM), BF16), _sds((T, Q_DIM), BF16),
                   _sds(w_out.shape, BF16), _sds(pshape, BF16), _sds(pshape, BF16),
                   _sds(w_s.shape, F32), _sds(b_st.shape, F32), _sds(g_sgu.shape, F32)],
        scratch_shapes=[pltpu.VMEM(w_out.shape, F32), pltpu.VMEM(pshape, F32), pltpu.VMEM(pshape, F32)],
        compiler_params=_cp(("arbitrary",), 56),
    )(*_hbm(dx1, merged, y_a, y_b, proj_g, proj_a, w_pa, w_pb, w_out, g_sgu, w_s, b_st), *order)


def _sgu_bwd_apply(p, dy, g, ws_ref, bs_ref, dp_ref, gws_ref, gbs_ref, gg_ref):
    tril = _tril()
    pu, pv, u, tu, vv, tv, rv, vn = _sgu_parts(p, g)
    du_cols = []
    dvn_cols = []
    for gi in range(A_GROUPS):
        wm = jnp.where(tril, ws_ref[gi], 0.0).astype(BF16)
        wmt = wm.astype(F32).T.astype(BF16)
        bcol = bs_ref[:, gi:gi + 1]
        cs = slice(gi * CHUNK, (gi + 1) * CHUNK)
        du_rows = []
        dvn_rows = []
        gw = jnp.zeros((CHUNK, CHUNK), F32)
        gb = jnp.zeros((CHUNK, 1), F32)
        for c in range(p.shape[0] // CHUNK):
            rs = slice(c * CHUNK, (c + 1) * CHUNK)
            vn_c = vn[rs, cs]
            s = _dot(wm, vn_c) + bcol
            dy_c = dy[rs, cs]
            ds = dy_c * u[rs, cs]
            du_rows.append(dy_c * s)
            dsb = ds.astype(BF16)
            gw = gw + _dot_nt(dsb, vn_c)
            gb = gb + jnp.sum(ds, axis=-1, keepdims=True)
            dvn_rows.append(_dot(wmt, dsb))
        gws_ref[gi] += jnp.where(tril, gw, 0.0)
        gbs_ref[:, gi:gi + 1] += gb
        du_cols.append(jnp.concatenate(du_rows, axis=0))
        dvn_cols.append(jnp.concatenate(dvn_rows, axis=0))
    du = jnp.concatenate(du_cols, axis=1)
    dvn = jnp.concatenate(dvn_cols, axis=1)
    vhat = vv * rv
    gg_ref[...] += jnp.sum(dvn * vhat, axis=0, keepdims=True)
    dvv = _rms_bwd(dvn, vhat, rv, g)
    dp_ref[:, :A_WIDTH] = (du * _gelu_grad(pu, tu)).astype(BF16)
    dp_ref[:, A_WIDTH:] = (dvv * _gelu_grad(pv, tv)).astype(BF16)


def _attn_bwd(proj_b, d_yb, sinks, rel_bias, n_seq, seq, after=None):
    nb = seq // CHUNK
    bk = jnp.asarray(_band_buckets())
    order = [] if after is None else [after]

    def body(*refs):
        qkv_ref, do_ref, bk_ref, rel_ref, sink_ref = refs[:5]
        (d_ref, gs_ref, gr_ref, bias_scr, sink_scr, kvar_scr, dbias_scr, dk_scr, dv_scr, ds_scr) = refs[5 + len(order):]
        b = pl.program_id(0)
        _attn_setup(bias_scr, sink_scr, kvar_scr, qkv_ref, bk_ref, rel_ref, sink_ref)
        ones = jnp.ones((2 * CHUNK, LANES), BF16)

        @pl.when(b == 0)
        def _():
            dbias_scr[...] = jnp.zeros_like(dbias_scr)
            ds_scr[...] = jnp.zeros_like(ds_scr)

        dk_scr[...] = jnp.zeros_like(dk_scr)
        dv_scr[...] = jnp.zeros_like(dv_scr)

        def transposed(a):
            return a.astype(F32).T.astype(BF16)

        def blk(n, carry):
            r0, kv, vv = _attn_block_inputs(kvar_scr, n)
            prob, psink = _attn_probs(qkv_ref, r0, n, kv, bias_scr, sink_scr, ones)
            dp = jnp.concatenate([_dot_nt(do_ref[pl.ds(r0, CHUNK), (h // 2) * LANES:(h // 2 + 1) * LANES], vv[h // 4][h % 2])
                                  for h in range(N_HEADS)], axis=0)
            delta = _rowsum(prob * dp, ones)
            dsc = prob * (dp - _both(delta))
            ds_scr[...] += psink * delta
            dbias_scr[...] += dsc
            dsb = (dsc * (HEAD_DIM ** -0.5)).astype(BF16)
            pb = prob.astype(BF16)
            dkt = [jnp.zeros((HEAD_DIM, 2 * CHUNK), F32) for _ in range(2)]
            dvt = [jnp.zeros((HEAD_DIM, 2 * CHUNK), F32) for _ in range(2)]
            for pr in range(N_HEADS // 2):
                ps = slice(pr * LANES, (pr + 1) * LANES)
                qpt = transposed(qkv_ref[pl.ds(r0, CHUNK), ps])
                dopt = transposed(do_ref[pl.ds(r0, CHUNK), ps])
                kvh = pr // 2
                dq = jnp.zeros((CHUNK, LANES), F32)
                for hh in range(2):
                    hr = _head_rows(2 * pr + hh)
                    rows = slice(hh * HEAD_DIM, (hh + 1) * HEAD_DIM)
                    dq = dq + _dot(dsb[hr], kv[kvh][hh])
                    dkt[kvh] = dkt[kvh] + _dot(qpt, dsb[hr])[rows]
                    dvt[kvh] = dvt[kvh] + _dot(dopt, pb[hr])[rows]
                d_ref[pl.ds(r0, CHUNK), ps] = dq.astype(BF16)
            dk_scr[:, pl.ds(r0, 2 * CHUNK)] += jnp.concatenate(dkt, axis=0)
            dv_scr[:, pl.ds(r0, 2 * CHUNK)] += jnp.concatenate(dvt, axis=0)
            return carry

        lax.fori_loop(0, nb, blk, 0)
        for n in range(nb):
            rows = slice(n * CHUNK, (n + 1) * CHUNK)
            cols = slice((n + 1) * CHUNK, (n + 2) * CHUNK)
            d_ref[rows, Q_DIM:Q_DIM + KV_DIM] = dk_scr[:, cols].T.astype(BF16)
            d_ref[rows, Q_DIM + KV_DIM:] = dv_scr[:, cols].T.astype(BF16)

        @pl.when(b == n_seq - 1)
        def _():
            bkv = bk_ref[...]
            for h in range(N_HEADS):
                gs_ref[0:1, h:h + 1] = -jnp.sum(ds_scr[_head_rows(h), 0:1], axis=0, keepdims=True)
                db = dbias_scr[_head_rows(h), :]
                for bb in range(N_BUCKETS):
                    part = jnp.sum(jnp.where(bkv == bb, db, 0.0), axis=-1, keepdims=True)
                    gr_ref[bb:bb + 1, h:h + 1] = jnp.sum(part, axis=0, keepdims=True)

    smem = pl.BlockSpec(memory_space=pltpu.SMEM)
    return pl.pallas_call(
        body, name="attn_bwd", grid=(n_seq,),
        in_specs=[_row(seq, B_DIM), _row(seq, Q_DIM), _full(bk.shape), smem, smem] + [ANY] * len(order),
        out_specs=[_row(seq, B_DIM), _full((1, N_HEADS)), _full((N_BUCKETS, N_HEADS))],
        out_shape=[_sds((n_seq * seq, B_DIM), BF16), _sds((1, N_HEADS), F32), _sds((N_BUCKETS, N_HEADS), F32)],
        scratch_shapes=[pltpu.VMEM((HEAD_ROWS, 2 * CHUNK), F32), pltpu.VMEM((HEAD_ROWS, LANES), F32),
                        pltpu.VMEM((8, seq, KV_DIM), BF16), pltpu.VMEM((HEAD_ROWS, 2 * CHUNK), F32),
                        pltpu.VMEM((KV_DIM, seq + CHUNK), F32), pltpu.VMEM((KV_DIM, seq + CHUNK), F32),
                        pltpu.VMEM((HEAD_ROWS, LANES), F32)],
        compiler_params=_cp(("arbitrary",), 40),
    )(*_hbm(proj_b, d_yb, bk), rel_bias, sinks, *order)


def _inproj_bwd(d_g, d_a, d_b, x2, dx1, g_mix, w_in, tm, after=None):
    T = x2.shape[0]
    order = [] if after is None else [after]

    def body(*refs):
        dg_ref, da_ref, db_ref, x_ref, dx1_ref, g_ref, w_ref = refs[:7]
        gx_ref, gg_ref = refs[7 + len(order):]
        dh = (_dot_nt(dg_ref[...], w_ref[:, _G_COLS]) + _dot_nt(da_ref[...], w_ref[:, _A_COLS])
              + _dot_nt(db_ref[...], w_ref[:, _B_COLS]))
        x = x_ref[...]
        r = _rms_r(x)
        n = x * r
        gx_ref[...] = dx1_ref[...] + _rms_bwd(dh, n, r, g_ref[...])

        @pl.when(pl.program_id(0) == 0)
        def _():
            gg_ref[...] = jnp.zeros_like(gg_ref)

        gg_ref[...] += jnp.sum(dh * n, axis=0, keepdims=True)

    return pl.pallas_call(
        body, name="inproj_bwd", grid=(T // tm,),
        in_specs=[_row(tm, G_DIM), _row(tm, A_DIM), _row(tm, B_DIM), _row(tm, D_MODEL), _row(tm, D_MODEL),
                  _full(g_mix.shape), _resident(w_in.shape)] + [ANY] * len(order),
        out_specs=[_row(tm, D_MODEL), _full((1, D_MODEL))],
        out_shape=[_sds((T, D_MODEL), F32), _sds((1, D_MODEL), F32)],
        compiler_params=_cp(("arbitrary",), 48),
    )(*_hbm(d_g, d_a, d_b, x2, dx1, g_mix, w_in), *order)


IN_SHARD = (A_DIM + B_DIM + G_DIM) // N_CHIPS


def _unstack_w_in(stack):
    tr = 256

    def body(s_ref, o_ref):
        for i in range(N_CHIPS):
            o_ref[:, i * IN_SHARD:(i + 1) * IN_SHARD] = s_ref[i]

    return pl.pallas_call(
        body, name="unstack_w_in", grid=(D_MODEL // tr,),
        in_specs=[pl.BlockSpec((N_CHIPS, tr, IN_SHARD), lambda r: (0, r, 0))],
        out_specs=pl.BlockSpec((tr, N_CHIPS * IN_SHARD), lambda r: (r, 0)),
        out_shape=_sds((D_MODEL, N_CHIPS * IN_SHARD), stack.dtype),
        compiler_params=_cp(("arbitrary",)),
    )(*_hbm(stack))


def _grad_w_in(h, d_a, d_b, d_g, tk):
    T = h.shape[0]
    nk = T // tk
    in_dim = N_CHIPS * IN_SHARD

    def body(h_ref, da_ref, db_ref, dg_ref, o_ref, acc_ref):
        k = pl.program_id(0)

        @pl.when(k == 0)
        def _():
            acc_ref[...] = jnp.zeros_like(acc_ref)

        hb = h_ref[...]
        acc_ref[:, _A_COLS] += _dot_tn(hb, da_ref[...])
        acc_ref[:, _B_COLS] += _dot_tn(hb, db_ref[...])
        acc_ref[:, _G_COLS] += _dot_tn(hb, dg_ref[...])

        @pl.when(k == nk - 1)
        def _():
            for i in range(N_CHIPS):
                o_ref[i] = acc_ref[:, i * IN_SHARD:(i + 1) * IN_SHARD].astype(BF16)

    return pl.pallas_call(
        body, name="grad_w_in", grid=(nk,),
        in_specs=[_row(tk, D_MODEL), _row(tk, A_DIM), _row(tk, B_DIM), _row(tk, G_DIM)],
        out_specs=_full((N_CHIPS, D_MODEL, IN_SHARD)), out_shape=_sds((N_CHIPS, D_MODEL, IN_SHARD), BF16),
        scratch_shapes=[pltpu.VMEM((D_MODEL, in_dim), F32)],
        compiler_params=_cp(("arbitrary",), 56),
    )(*_hbm(h, d_a, d_b, d_g))


def _local_step(x, target, g_mix, g_sgu, w_s, b_s, sinks, rel_bias, g_ffn, b_conv, g_final,
                w_in, w_conv, proj_weights, ffn_weights, on_grads, after=None):
    n_seq, seq, _ = x.shape
    T = n_seq * seq
    tm = min(ROW_TILE, seq)
    tw = min(GRAD_ROW_TILE, T)
    tf = min(WIDE_ROW_TILE, seq)
    x2 = x.reshape(T, D_MODEL)
    tgt = target.reshape(T, D_MODEL)
    b_st = b_s.T
    g_fin = g_final.reshape(1, D_MODEL)

    proj_g, proj_a, proj_b, h, y_a = _inproj(x2, g_mix, w_in, g_sgu, w_s, b_st, tm, after)
    y_b = _attn_fwd(proj_b, sinks, rel_bias, n_seq, seq)
    w_pa, w_pb, w_out = proj_weights(y_b)
    x1, merged = _merge_fwd(x2, y_a, y_b, proj_g, w_pa, w_pb, w_out, tm)
    w_up, w_down = ffn_weights(x1)
    upre, h2, gate, val = _upproj(x1, g_ffn, w_up, w_conv, b_conv, tf, seq)
    dx2, loss, gg_final = _ffn_down_loss(gate, val, x1, tgt, w_down, g_fin, tm)

    d_gate, d_val, gw_down, gb_g, gb_v = _ffn_bwd_act(gate, val, dx2, w_down, tw)
    gb_conv = jnp.concatenate([gb_g, gb_v], axis=1)
    d_upre, dx1, gg_ffn, gw_conv = _ffn_bwd_up(d_gate, d_val, upre, dx2, x1, g_ffn, w_conv, w_up, tf, seq)
    gw_up = _matmul_tn(h2, d_upre, 2 * D_FF // 4, min(2 * GRAD_ROW_TILE, T), "grad_w_up")
    sent = on_grads("ffn", dict(w_up=gw_up, w_down=gw_down))
    d_g, d_a, d_yb, gw_out, gw_pa, gw_pb, gw_s, gb_st, gg_sgu = _merge_bwd(
        dx1, merged, y_a, y_b, proj_g, proj_a, w_pa, w_pb, w_out, g_sgu, w_s, b_st, tw, sent)
    sent = on_grads("proj", dict(w_pa=gw_pa, w_pb=gw_pb, w_out=gw_out))
    d_b, g_sinks, g_rel = _attn_bwd(proj_b, d_yb, sinks, rel_bias, n_seq, seq, sent)
    gw_in = _grad_w_in(h, d_a, d_b, d_g, min(2 * GRAD_ROW_TILE, T))
    sent = on_grads("in", dict(w_in=gw_in))
    grad_x, gg_mix = _inproj_bwd(d_g, d_a, d_b, x2, dx1, g_mix, w_in, tm, sent)

    small = dict(g_mix=gg_mix, g_sgu=gg_sgu, w_s=gw_s, b_s=gb_st.T, sinks=g_sinks, rel_bias=g_rel,
                 g_ffn=gg_ffn, b_conv=gb_conv, g_final=gg_final, w_conv=gw_conv)
    big = dict(w_in=gw_in, w_pa=gw_pa, w_pb=gw_pb, w_out=gw_out, w_up=gw_up, w_down=gw_down)
    return loss, grad_x.reshape(x.shape), small, big


_MIXER = ("w_in", "w_pa", "w_pb", "w_out")
_FFN = ("w_up", "w_down")
_BIG = _MIXER + _FFN

CONV_ROWS = 6
_SMALL_AT = dict(loss=(0, 1, 1), g_final=(1, 1, D_MODEL), g_mix=(2, 1, D_MODEL), g_ffn=(3, 1, D_MODEL), g_sgu=(4, 1, A_WIDTH),
                 sinks=(5, 1, N_HEADS), rel_bias=(6, 1, N_BUCKETS * N_HEADS), b_s=(8, A_GROUPS, CHUNK),
                 b_conv=(12, CONV_ROWS, D_MODEL), w_conv=(18, 3 * CONV_ROWS, D_MODEL), w_s=(40, A_GROUPS * CHUNK * CHUNK // D_MODEL, D_MODEL))
_SMALL_IN_CALL = ("g_final", "g_mix", "g_ffn", "g_sgu", "sinks", "b_s")
SMALL_ROWS = 104


def _pack_small(vals):
    def wide(a):
        return jnp.pad(a, ((0, 0), (0, CONV_ROWS * D_MODEL - a.shape[1]))).reshape(-1, D_MODEL)

    laid = dict(vals, b_conv=wide(vals["b_conv"]), w_conv=wide(vals["w_conv"]), w_s=vals["w_s"].reshape(-1, D_MODEL))
    rows, at = [], 0
    for n, (r0, nr, nc) in _SMALL_AT.items():
        if r0 > at:
            rows.append(jnp.zeros((r0 - at, D_MODEL), F32))
        rows.append(jnp.pad(laid[n].astype(F32).reshape(nr, nc), ((0, 0), (0, D_MODEL - nc))))
        at = r0 + nr
    return jnp.concatenate(rows, axis=0)


def _unwide(a, r):
    return a.reshape(r, CONV_ROWS * D_MODEL)[:, :2 * D_FF]


def _mesh_pos():
    return lax.axis_index("x"), lax.axis_index("y"), lax.axis_index("c")


def _other_chips(x, y):
    return [(1 - x, y), (x, 1 - y), (1 - x, 1 - y)]


def _remote(src, dst, send_sem, recv_sem, to):
    return pltpu.make_async_remote_copy(src_ref=src, dst_ref=dst, send_sem=send_sem, recv_sem=recv_sem,
                                        device_id=to, device_id_type=MESH)


def _own_slot(own, n, at):
    return lax.dynamic_update_slice(lax.empty((n,) + own.shape, own.dtype), own[None], (at,) + (0,) * own.ndim)


def _allgather_weights(stacks, wc_stack):
    names = list(stacks)
    n = len(names)

    def body(*refs):
        ins, outs = refs[:n + 1], refs[n + 1:2 * n + 2]
        send_sems, recv_sems = refs[2 * n + 2:]
        x, y, c = _mesh_pos()
        _handshake(_chip_peers(x, y, c) + _sibling_peers(x, y, c))
        me = 2 * x + y
        sibling = (x, y, 1 - c)
        chips = _other_chips(x, y)

        def half(ref, chip, hc):
            hr = ref.shape[1] // 2
            return ref.at[chip, pl.ds(hc * hr, hr), :]

        first = []
        for k in range(n):
            first += [_remote(half(ins[k], me, c), half(outs[k], me, c), send_sems.at[6 * k + j], recv_sems.at[6 * k + j], (cx, cy, c))
                      for j, (cx, cy) in enumerate(chips)]
        first += [_remote(ins[n].at[me], outs[n].at[me], send_sems.at[6 * n + j], recv_sems.at[6 * n + j], (cx, cy, c))
                  for j, (cx, cy) in enumerate(chips)]
        for cp in first:
            cp.start()
        passed = []
        for k in range(n):
            for j, (cx, cy) in enumerate(chips):
                landed = half(outs[k], 2 * cx + cy, c)
                _remote(landed, landed, send_sems.at[6 * k + j], recv_sems.at[6 * k + j], (x, y, c)).wait_recv()
                passed.append(_remote(landed, landed, send_sems.at[6 * k + 3 + j], recv_sems.at[6 * k + 3 + j], sibling))
                passed[-1].start()
        for k in range(n):
            for j, (cx, cy) in enumerate(chips):
                theirs = half(outs[k], 2 * cx + cy, 1 - c)
                _remote(theirs, theirs, send_sems.at[6 * k + 3 + j], recv_sems.at[6 * k + 3 + j], (x, y, c)).wait_recv()
        for j, (cx, cy) in enumerate(chips):
            slot = outs[n].at[2 * cx + cy]
            _remote(slot, slot, send_sems.at[6 * n + j], recv_sems.at[6 * n + j], (x, y, c)).wait_recv()
        for cp in first + passed:
            cp.wait_send()

    arrays = [stacks[k] for k in names] + [wc_stack]
    outs = pl.pallas_call(
        body, name="allgather_weights",
        in_specs=[HBM] * (n + 1), out_specs=[HBM] * (n + 1), input_output_aliases={k: k for k in range(n + 1)},
        out_shape=[_sds(a.shape, a.dtype) for a in arrays],
        scratch_shapes=[pltpu.SemaphoreType.DMA((6 * n + 3,)), pltpu.SemaphoreType.DMA((6 * n + 3,))],
        compiler_params=pltpu.CompilerParams(collective_id=_COLLECTIVE["gather_in"]),
    )(*arrays)
    return dict(zip(names, outs[:n])), outs[n]


_KIND = {"w_in": "stack", "w_pa": "col", "w_pb": "col", "w_up": "col", "w_out": "row", "w_down": "row"}


def _half_view(ref, kind, h):
    if kind == "stack":
        k = ref.shape[1] // 2
        return ref.at[:, pl.ds(h * k, k), :]
    if kind == "col":
        k = ref.shape[0] // 2
        return ref.at[pl.ds(h * k, k), :]
    k = ref.shape[1] // 2
    return ref.at[:, pl.ds(h * k, k)]


def _shard_view(ref, kind, i):
    if kind == "stack":
        return ref.at[i]
    if kind == "col":
        k = ref.shape[1] // N_CHIPS
        return ref.at[:, pl.ds(i * k, k)]
    k = ref.shape[0] // N_CHIPS
    return ref.at[pl.ds(i * k, k), :]


def _region_view(ref, kind, h):
    if kind == "row":
        k = ref.shape[1] // 2
        return ref.at[:, pl.ds(h * k, k)]
    k = ref.shape[0] // 2
    return ref.at[pl.ds(h * k, k), :]


def _half_shape(shape, kind):
    if kind == "stack":
        return (shape[0], shape[1] // 2, shape[2])
    return (shape[0] // 2, shape[1]) if kind == "col" else (shape[0], shape[1] // 2)


def _part_shape(half_shape, kind):
    if kind == "stack":
        return tuple(half_shape[1:])
    k, w = half_shape
    return (k, w // N_CHIPS) if kind == "col" else (k // N_CHIPS, w)


_DATAFLOW = pltpu.SideEffectType.DATAFLOW_SIDE_EFFECTING
_TOKEN = (SUBLANES, LANES)


_COLLECTIVE = {k: i for i, k in enumerate(
    [kind + "_" + g for kind in ("pair", "chip", "share") for g in ("ffn", "proj", "in")]
    + ["gather_proj", "gather_ffn", "gather_in", "forward_proj", "forward_ffn"])}


def _sibling_peers(x, y, c):
    return [(x, y, 1 - c)]


def _chip_peers(x, y, c):
    return [(cx, cy, c) for cx, cy in _other_chips(x, y)]


def _handshake(peers):
    barrier = pltpu.get_barrier_semaphore()
    for peer in peers:
        pl.semaphore_signal(barrier, inc=1, device_id=peer, device_id_type=MESH)
    pl.semaphore_wait(barrier, len(peers))


def _split_start(name, arrays, n_sems, issue, after=None, handshake=None):
    n = len(arrays)
    order = [] if after is None else [after]

    def body(*refs):
        base = n + len(order)
        if handshake is not None:
            _handshake(handshake[1](*_mesh_pos()))
        issue(refs[:n], refs[base], refs[base + 1])
        refs[-1][...] = jnp.zeros(_TOKEN, F32)

    params = dict(has_side_effects=_DATAFLOW)
    if handshake is not None:
        params["collective_id"] = handshake[0]
    outs = pl.pallas_call(
        body, name=name,
        in_specs=[HBM] * n + [ANY] * len(order), out_specs=[SEM, SEM] + [HBM] * n + [pl.BlockSpec(memory_space=pltpu.VMEM)],
        out_shape=[pltpu.SemaphoreType.DMA((n_sems,)), pltpu.SemaphoreType.DMA((n_sems,))]
        + [pltpu.HBM(a.shape, a.dtype) for a in arrays] + [_sds(_TOKEN, F32)],
        input_output_aliases={k: 2 + k for k in range(n)},
        compiler_params=pltpu.CompilerParams(**params),
    )(*[pltpu.with_memory_space_constraint(a, pltpu.HBM) for a in arrays], *order)
    return outs[0], outs[1], list(outs[2:2 + n]), outs[-1]


def _split_wait(name, started, waits, after):
    send_sems, recv_sems, arrays, _ = started
    n = len(arrays)

    def body(*refs):
        waits(refs[:n], refs[n], refs[n + 1])

    return pl.pallas_call(
        body, name=name,
        in_specs=[HBM] * n + [SEM, SEM, ANY], out_specs=[HBM] * n,
        out_shape=[pltpu.HBM(a.shape, a.dtype) for a in arrays],
        input_output_aliases={k: k for k in range(n)},
        compiler_params=pltpu.CompilerParams(has_side_effects=_DATAFLOW),
    )(*arrays, send_sems, recv_sems, after)


def _wait_both(src, dst, send_sem, recv_sem):
    x, y, c = _mesh_pos()
    cp = _remote(src, dst, send_sem, recv_sem, (x, y, c))
    cp.wait_send()
    cp.wait_recv()


def _pair_exchange_start(parts, tag, after):
    names = list(parts)
    n = len(names)
    lands = [lax.empty(_half_shape(parts[k].shape, _KIND[k]), parts[k].dtype) for k in names]

    def issue(refs, send_sems, recv_sems):
        x, y, c = _mesh_pos()
        for hc in range(2):
            @pl.when(c == hc)
            def _():
                for k in range(n):
                    _remote(_half_view(refs[k], _KIND[names[k]], 1 - hc), refs[n + k], send_sems.at[k], recv_sems.at[k],
                            (x, y, 1 - c)).start()

    return names, _split_start("grad_pair_exchange_start_" + tag, [parts[k] for k in names] + lands, n, issue, after,
                               (_COLLECTIVE["pair_" + tag], _sibling_peers))


def _pair_exchange_wait(pending, tag, after):
    names, started = pending
    n = len(names)

    def waits(refs, send_sems, recv_sems):
        for k in range(n):
            _wait_both(_half_view(refs[k], _KIND[names[k]], 0), refs[n + k], send_sems.at[k], recv_sems.at[k])

    outs = _split_wait("grad_pair_exchange_wait_" + tag, started, waits, after)
    return dict(zip(names, outs[:n])), dict(zip(names, outs[n:]))


def _half_blocks(shape, kind):
    if kind == "stack":
        _, k, w = shape
        tr = k // 2
        nb = 1
        return (N_CHIPS, nb), (1, tr, w), (lambda i, r, s: (i, r, 0)), (lambda i, r, s: (i, s[1] * nb + r, 0))
    k, w = shape
    if kind == "col":
        tr = 256
        nb = k // 2 // tr
        return (nb,), (tr, w), (lambda r, s: (r, 0)), (lambda r, s: (s[1] * nb + r, 0))
    tr = k // N_CHIPS
    return (N_CHIPS,), (tr, w // 2), (lambda r, s: (r, 0)), (lambda r, s: (r, s[1]))


def _pair_add(part, from_sibling, name, pos):
    kind = _KIND[name]
    grid, block, half_map, full_map = _half_blocks(part.shape, kind)

    def body(s_ref, p_ref, q_ref, o_ref):
        o_ref[...] = (p_ref[...].astype(F32) + q_ref[...].astype(F32)).astype(BF16)

    return pl.pallas_call(
        body, name="grad_pair_add_" + name,
        grid_spec=pltpu.PrefetchScalarGridSpec(
            num_scalar_prefetch=1, grid=grid,
            in_specs=[pl.BlockSpec(block, full_map), pl.BlockSpec(block, half_map)],
            out_specs=pl.BlockSpec(block, half_map)),
        out_shape=_sds(from_sibling.shape, BF16),
        compiler_params=_cp(("arbitrary",) * len(grid), 40),
    )(pos, *_hbm(part, from_sibling))


def _chip_exchange_start(sums, tag, after):
    names = list(sums)
    n = len(names)
    lands = [lax.empty((3,) + _part_shape(sums[k].shape, _KIND[k]), sums[k].dtype) for k in names]

    def issue(refs, send_sems, recv_sems):
        x, y, c = _mesh_pos()
        me = 2 * x + y
        for i in range(N_CHIPS):
            xi, yi = i // 2, i % 2
            j = jnp.where(xi != x, jnp.where(yi != y, 2, 0), 1)

            @pl.when(i != me)
            def _():
                for k in range(n):
                    _remote(_shard_view(refs[k], _KIND[names[k]], i), refs[n + k].at[j], send_sems.at[3 * k + j],
                            recv_sems.at[3 * k + j], (xi, yi, c)).start()

    return names, _split_start("grad_chip_exchange_start_" + tag, [sums[k] for k in names] + lands, 3 * n, issue, after,
                               (_COLLECTIVE["chip_" + tag], _chip_peers))


def _chip_exchange_wait(pending, tag, after):
    names, started = pending
    n = len(names)

    def waits(refs, send_sems, recv_sems):
        for k in range(n):
            for j in range(3):
                _wait_both(_shard_view(refs[k], _KIND[names[k]], 0), refs[n + k].at[j], send_sems.at[3 * k + j], recv_sems.at[3 * k + j])

    return dict(zip(names, _split_wait("grad_chip_exchange_wait_" + tag, started, waits, after)[n:]))


def _allgather_start(stacks, tag, after):
    names = list(stacks)

    def issue(refs, send_sems, recv_sems):
        x, y, c = _mesh_pos()
        me = 2 * x + y
        for k, st in enumerate(refs):
            hr = st.shape[1] // 2
            mine = st.at[me, pl.ds(c * hr, hr), :]
            for j, (cx, cy) in enumerate(_other_chips(x, y)):
                _remote(mine, mine, send_sems.at[3 * k + j], recv_sems.at[3 * k + j], (cx, cy, c)).start()

    return names, _split_start("allgather_start_" + tag, [stacks[k] for k in names], 3 * len(names), issue, after,
                               (_COLLECTIVE["gather_" + tag], _chip_peers))


def _allgather_wait(pending, tag, after):
    names, started = pending

    def waits(refs, send_sems, recv_sems):
        for k, st in enumerate(refs):
            slot = st.at[0, pl.ds(0, st.shape[1] // 2), :]
            for j in range(3):
                _wait_both(slot, slot, send_sems.at[3 * k + j], recv_sems.at[3 * k + j])

    return dict(zip(names, _split_wait("allgather_wait_" + tag, started, waits, after)))


def _allgather_forward(stacks, tag):
    names = list(stacks)
    n = len(names)

    def body(*refs):
        ins, outs = refs[:n], refs[n:2 * n]
        send_sems, recv_sems = refs[2 * n:]
        x, y, c = _mesh_pos()
        _handshake(_sibling_peers(x, y, c))
        copies = []
        for k in range(n):
            hr = ins[k].shape[1] // 2
            for j, (cx, cy) in enumerate(_other_chips(x, y)):
                chip = 2 * cx + cy
                copies.append(_remote(ins[k].at[chip, pl.ds(c * hr, hr), :], outs[k].at[chip, pl.ds(c * hr, hr), :],
                                      send_sems.at[3 * k + j], recv_sems.at[3 * k + j], (x, y, 1 - c)))
        for cp in copies:
            cp.start()
        for cp in copies:
            cp.wait()

    arrays = [stacks[k] for k in names]
    outs = pl.pallas_call(
        body, name="allgather_forward_" + tag, in_specs=[HBM] * n, out_specs=[HBM] * n,
        input_output_aliases={k: k for k in range(n)},
        out_shape=[_sds(a.shape, a.dtype) for a in arrays],
        scratch_shapes=[pltpu.SemaphoreType.DMA((3 * n,)), pltpu.SemaphoreType.DMA((3 * n,))],
        compiler_params=pltpu.CompilerParams(collective_id=_COLLECTIVE["forward_" + tag]),
    )(*arrays)
    return dict(zip(names, outs))


def _owner_sum(part, from_sibling, from_chips, name, pos, shard_shape):
    kind = _KIND[name]
    _, pk, pw = from_chips.shape
    if kind == "row":
        tr, nb = pk, 1
        p_spec = pl.BlockSpec((tr, pw), lambda r, s: (s[0], s[1]))
        q_spec = pl.BlockSpec((tr, pw), lambda r, s: (s[0], 0))
        o_spec = pl.BlockSpec((tr, pw), lambda r, s: (0, s[1]))
    else:
        tr = 256
        nb = pk // tr
        if kind == "stack":
            p_spec = pl.BlockSpec((None, tr, pw), lambda r, s: (s[0], s[1] * nb + r, 0))
            q_spec = pl.BlockSpec((None, tr, pw), lambda r, s: (s[0], r, 0))
        else:
            p_spec = pl.BlockSpec((tr, pw), lambda r, s: (s[1] * nb + r, s[0]))
            q_spec = pl.BlockSpec((tr, pw), lambda r, s: (r, s[0]))
        o_spec = pl.BlockSpec((tr, pw), lambda r, s: (s[1] * nb + r, 0))

    def body(s_ref, p_ref, q_ref, r_ref, o_ref):
        acc = p_ref[...].astype(F32) + q_ref[...].astype(F32)
        for j in range(3):
            acc = acc + r_ref[j].astype(F32)
        o_ref[...] = acc

    return pl.pallas_call(
        body, name="grad_owner_sum_" + name,
        grid_spec=pltpu.PrefetchScalarGridSpec(
            num_scalar_prefetch=1, grid=(nb,),
            in_specs=[p_spec, q_spec, pl.BlockSpec((3, tr, pw), lambda r, s: (0, r, 0))],
            out_specs=o_spec),
        out_shape=_sds(shard_shape, F32),
        compiler_params=_cp(("arbitrary",), 32),
    )(pos, *_hbm(part, from_sibling, from_chips))


def _pair_share_start(shards, tag, after):
    names = list(shards)

    def issue(refs, send_sems, recv_sems):
        x, y, c = _mesh_pos()
        for hc in range(2):
            @pl.when(c == hc)
            def _():
                for k, g in enumerate(refs):
                    mine = _region_view(g, _KIND[names[k]], hc)
                    _remote(mine, mine, send_sems.at[k], recv_sems.at[k], (x, y, 1 - c)).start()

    return names, _split_start("grad_pair_share_start_" + tag, [shards[k] for k in names], len(names), issue, after,
                               (_COLLECTIVE["share_" + tag], _sibling_peers))


def _pair_share_wait(pending, tag, after):
    names, started = pending

    def waits(refs, send_sems, recv_sems):
        for k, g in enumerate(refs):
            region = _region_view(g, _KIND[names[k]], 0)
            _wait_both(region, region, send_sems.at[k], recv_sems.at[k])

    return dict(zip(names, _split_wait("grad_pair_share_wait_" + tag, started, waits, after)))


def _small_exchange_start(slots, after):
    def issue(refs, send_sems, recv_sems):
        x, y, c = _mesh_pos()
        mine = refs[0].at[4 * x + 2 * y + c]
        k = 0
        for px in range(2):
            for py in range(2):
                for pc in range(2):
                    if px + py + pc:
                        peer = (1 - x if px else x, 1 - y if py else y, 1 - c if pc else c)
                        _remote(mine, mine, send_sems.at[k], recv_sems.at[k], peer).start()
                        k += 1

    return _split_start("small_exchange_start", [slots], N_DEV - 1, issue, after)


def _small_exchange_wait(started, after):
    def waits(refs, send_sems, recv_sems):
        slot = refs[0].at[0]
        for k in range(N_DEV - 1):
            _wait_both(slot, slot, send_sems.at[k], recv_sems.at[k])

    return _split_wait("small_exchange_wait", started, waits, after)[0]


def _adam_math(w, g, m, v):
    m = ADAM_B1 * m + (1.0 - ADAM_B1) * g
    v = ADAM_B2 * v + (1.0 - ADAM_B2) * (g * g)
    m_hat = m / (1.0 - ADAM_B1 ** ADAM_STEP)
    v_hat = v / (1.0 - ADAM_B2 ** ADAM_STEP)
    delta = -ADAM_LR * (m_hat / (jnp.sqrt(v_hat) + ADAM_EPS) + ADAM_WD * w)
    return delta, m, v


def _adamw(w, g, m, v, name):
    rows, cols = w.shape
    fits = [t for t in range(SUBLANES, rows, SUBLANES) if rows % t == 0 and t * cols * 4 <= (3 << 19)]
    tr = max(fits) if fits else rows

    def body(w_ref, g_ref, m_ref, v_ref, d_ref, nm_ref, nv_ref, go_ref):
        g = g_ref[...]
        d, nm, nv = _adam_math(w_ref[...], g, m_ref[...], v_ref[...])
        d_ref[...] = d
        nm_ref[...] = nm
        nv_ref[...] = nv
        go_ref[...] = g

    spec = pl.BlockSpec((tr, cols), lambda i: (i, 0))
    return pl.pallas_call(
        body, name=name, grid=(rows // tr,), in_specs=[spec] * 4, out_specs=[spec] * 4,
        out_shape=[_sds(w.shape, F32)] * 4, compiler_params=_cp(("arbitrary",)),
    )(*_hbm(w, g, m, v))


def _small_sum_adamw(gathered, w, m, v):
    names = _SMALL_IN_CALL
    n = len(names)

    def body(*refs):
        a_ref = refs[0]
        w_refs, m_refs, v_refs = refs[1:1 + n], refs[1 + n:1 + 2 * n], refs[1 + 2 * n:1 + 3 * n]
        sum_ref = refs[1 + 3 * n]
        outs = refs[2 + 3 * n:]
        g = a_ref[0]
        for k in range(1, N_DEV):
            g = g + a_ref[k]
        sum_ref[...] = g
        for i, name in enumerate(names):
            r0, nr, nc = _SMALL_AT[name]
            gp = g[r0:r0 + nr, 0:nc]
            d, nm, nv = _adam_math(w_refs[i][...], gp, m_refs[i][...], v_refs[i][...])
            for k, val in enumerate((gp, d, nm, nv)):
                outs[4 * i + k][...] = val

    shapes = [w[k].shape for k in names]
    res = pl.pallas_call(
        body, name="small_sum_adamw",
        out_shape=[_sds((SMALL_ROWS, D_MODEL), F32)] + [_sds(s, F32) for s in shapes for _ in range(4)],
    )(gathered, *[w[k] for k in names], *[m[k] for k in names], *[v[k] for k in names])
    return res[0], {k: tuple(res[1 + 4 * i:5 + 4 * i]) for i, k in enumerate(names)}


_NAMES = ("g_mix", "w_in", "g_sgu", "w_s", "b_s", "sinks", "rel_bias", "w_pa", "w_pb", "w_out",
          "g_ffn", "w_up", "w_conv", "b_conv", "w_down", "g_final")

def kernel(x, g_mix, w_in, g_sgu, w_s, b_s, sinks, rel_bias, w_pa, w_pb, w_out, g_ffn, w_up, w_conv, b_conv, w_down, g_final, loss_target, m_g_mix, m_w_in, m_g_sgu, m_w_s, m_b_s, m_sinks, m_rel_bias, m_w_pa, m_w_pb, m_w_out, m_g_ffn, m_w_up, m_w_conv, m_b_conv, m_w_down, m_g_final, v_g_mix, v_w_in, v_g_sgu, v_w_s, v_b_s, v_sinks, v_rel_bias, v_w_pa, v_w_pb, v_w_out, v_g_ffn, v_w_up, v_w_conv, v_b_conv, v_w_down, v_g_final):
    w = dict(g_mix=g_mix, w_in=w_in, g_sgu=g_sgu, w_s=w_s, b_s=b_s, sinks=sinks, rel_bias=rel_bias, w_pa=w_pa, w_pb=w_pb,
             w_out=w_out, g_ffn=g_ffn, w_up=w_up, w_conv=w_conv, b_conv=b_conv, w_down=w_down, g_final=g_final)
    m = dict(g_mix=m_g_mix, w_in=m_w_in, g_sgu=m_g_sgu, w_s=m_w_s, b_s=m_b_s, sinks=m_sinks, rel_bias=m_rel_bias, w_pa=m_w_pa,
             w_pb=m_w_pb, w_out=m_w_out, g_ffn=m_g_ffn, w_up=m_w_up, w_conv=m_w_conv, b_conv=m_b_conv, w_down=m_w_down,
             g_final=m_g_final)
    v = dict(g_mix=v_g_mix, w_in=v_w_in, g_sgu=v_g_sgu, w_s=v_w_s, b_s=v_b_s, sinks=v_sinks, rel_bias=v_rel_bias, w_pa=v_w_pa,
             w_pb=v_w_pb, w_out=v_w_out, g_ffn=v_g_ffn, w_up=v_w_up, w_conv=v_w_conv, b_conv=v_b_conv, w_down=v_w_down,
             g_final=v_g_final)
    xi, yi, ci = _mesh_pos()
    me = 2 * xi + yi

    shard = {n: w[n][0] for n in _BIG}
    shard_shapes = {n: shard[n].shape for n in _BIG}
    wc_shard = w["w_conv"][0]
    wc_pad = jnp.pad(wc_shard, ((0, 5), (0, 0)))
    own = {n: _own_slot(shard[n].astype(BF16), N_CHIPS, me) for n in _BIG}
    stacks, wc_all = _allgather_weights({"w_in": own["w_in"]}, _own_slot(wc_pad, N_CHIPS, me))
    proj_gather = _allgather_start({n: own[n] for n in _MIXER[1:]}, "proj", stacks["w_in"])
    ffn_gather = _allgather_start({n: own[n] for n in _FFN}, "ffn", proj_gather[1][-1])
    w_conv_full = jnp.concatenate([wc_all[i, :3] for i in range(N_CHIPS)], axis=1)
    w_in_full = _unstack_w_in(stacks["w_in"])
    pos = jnp.stack([me, ci])

    def proj_weights(done):
        st = _allgather_forward(_allgather_wait(proj_gather, "proj", done), "proj")
        return st["w_pa"], st["w_pb"], st["w_out"].reshape(D_MODEL, D_MODEL)

    def ffn_weights(done):
        st = _allgather_forward(_allgather_wait(ffn_gather, "ffn", done), "ffn")
        return st["w_up"], st["w_down"].reshape(D_FF, D_MODEL)

    groups = {}

    def stage1(group, parts):
        groups[group] = dict(parts=parts, pair=_pair_exchange_start(parts, group, None))
        return groups[group]["pair"][1][-1]

    def stage2(group, after, order_after):
        g = groups[group]
        g["parts"], g["sib"] = _pair_exchange_wait(g["pair"], group, after)
        g["chip"] = _chip_exchange_start({n: _pair_add(g["parts"][n], g["sib"][n], n, pos) for n in g["parts"]}, group, order_after)
        return g["chip"][1][-1]

    def stage3(group, after, order_after):
        g = groups[group]
        got = _chip_exchange_wait(g["chip"], group, after)
        g["share"] = _pair_share_start(
            {n: _owner_sum(g["parts"][n], g["sib"][n], got[n], n, pos, shard_shapes[n]) for n in g["parts"]}, group, order_after)
        return g["share"][1][-1]

    grads, deltas, new_m, new_v = {}, {}, {}, {}

    def stage4(group, after):
        g_shard = _pair_share_wait(groups[group]["share"], group, after)
        last = None
        for n in g_shard:
            g = _tie(g_shard[n], last)
            if n == "w_in":
                d, nm, nv, gt = _adamw(shard[n].T, g.T, m[n][0].T, v[n][0].T, "adamw_" + n)
                grads[n], deltas[n], new_m[n], new_v[n] = gt.T[None], d.T[None], nm.T[None], nv.T[None]
            else:
                d, nm, nv, go = _adamw(shard[n], g, m[n][0], v[n][0], "adamw_" + n)
                grads[n], deltas[n], new_m[n], new_v[n] = go[None], d[None], nm[None], nv[None]
            last = nv
        return last

    def on_grads(group, parts):
        token = stage1(group, parts)
        some = next(iter(parts.values()))
        if group == "proj":
            token = stage2("ffn", some, token)
        if group == "in":
            token = stage2("proj", some, token)
            token = stage3("ffn", some, token)
            token = stage2("in", token, token)
        return token

    loss, grad_x, small, big = _local_step(
        x, loss_target, w["g_mix"], w["g_sgu"], w["w_s"][0], w["b_s"][0], w["sinks"], w["rel_bias"], w["g_ffn"],
        w["b_conv"], w["g_final"], w_in_full, w_conv_full, proj_weights, ffn_weights, on_grads, ffn_gather[1][-1])

    small["loss"] = loss
    small_gather = _small_exchange_start(_own_slot(_pack_small(small), N_DEV, 2 * me + ci), grad_x)
    token = stage3("proj", grad_x, small_gather[-1])
    done = stage4("ffn", token)
    done = stage4("proj", done)
    token = stage3("in", done, None)
    all_small = _small_exchange_wait(small_gather, token)
    two_d = {n: (lambda a, n=n: a.reshape(_SMALL_AT[n][1:])) for n in _SMALL_IN_CALL}
    s_sum, s_out = _small_sum_adamw(all_small, *[{n: two_d[n](p[n]) for n in _SMALL_IN_CALL} for p in (w, m, v)])
    stage4("in", all_small)
    for n in _SMALL_IN_CALL:
        grads[n], deltas[n], new_m[n], new_v[n] = [a.reshape(w[n].shape) for a in s_out[n]]

    def rows(n):
        r0, nr, _ = _SMALL_AT[n]
        return s_sum[r0:r0 + nr]

    wcols = wc_shard.shape[1]
    g_wc = lax.dynamic_slice(_unwide(rows("w_conv"), 3), (0, me * wcols), (3, wcols))
    d, nm, nv, _ = _adamw(wc_shard, g_wc, m["w_conv"][0], v["w_conv"][0], "adamw_w_conv")
    grads["w_conv"], deltas["w_conv"], new_m["w_conv"], new_v["w_conv"] = g_wc[None], d[None], nm[None], nv[None]
    d, nm, nv, go = _adamw(w["b_conv"], _unwide(rows("b_conv"), 1), m["b_conv"], v["b_conv"], "adamw_b_conv")
    grads["b_conv"], deltas["b_conv"], new_m["b_conv"], new_v["b_conv"] = go, d, nm, nv
    g_rb = rows("rel_bias")[:, :N_BUCKETS * N_HEADS].reshape(N_BUCKETS, N_HEADS)
    d, nm, nv, go = _adamw(w["rel_bias"], g_rb, m["rel_bias"], v["rel_bias"], "adamw_rel_bias")
    grads["rel_bias"], deltas["rel_bias"], new_m["rel_bias"], new_v["rel_bias"] = go, d, nm, nv
    flat_s = (A_GROUPS * CHUNK, CHUNK)
    d, nm, nv, go = _adamw(w["w_s"].reshape(flat_s), rows("w_s").reshape(flat_s), m["w_s"].reshape(flat_s),
                           v["w_s"].reshape(flat_s), "adamw_w_s")
    grads["w_s"], deltas["w_s"], new_m["w_s"], new_v["w_s"] = [a.reshape(w["w_s"].shape) for a in (go, d, nm, nv)]

    return (s_sum[0, 0], grad_x, *[grads[n] for n in _NAMES], *[deltas[n] for n in _NAMES],
            *[new_m[n] for n in _NAMES], *[new_v[n] for n in _NAMES])
```

```python
import functools

import numpy as np
import jax
import jax.numpy as jnp
from jax import lax
from jax.experimental import pallas as pl
from jax.experimental.pallas import tpu as pltpu

F32 = jnp.float32
BF16 = jnp.bfloat16

D_MODEL = 1024
CHUNK = 128
A_GROUPS = 4
A_WIDTH = 512
N_HEADS = 8
HEAD_DIM = 64
Q_DIM = 512
KV_DIM = 128
N_BUCKETS = 32
MAX_DISTANCE = 128
D_FF = 2816
EPS = 1e-6
NEG_INF = -1e30
G_DIM = 2 * D_MODEL
A_DIM = 2 * A_WIDTH
B_DIM = Q_DIM + 2 * KV_DIM
LANES = 128
SUBLANES = 8
ROW_TILE = 512
WIDE_ROW_TILE = 256
COL_CHUNK = 512
GRAD_ROW_TILE = 512
BF16_ROWS = 16
N_CHIPS = 4
N_DEV = 8

ADAM_LR = 0.001
ADAM_B1 = 0.9
ADAM_B2 = 0.999
ADAM_EPS = 1e-08
ADAM_WD = 0.01
ADAM_STEP = 10

MESH = pl.DeviceIdType.MESH
_GELU_C = 0.7978845608028654
_GELU_A = 0.044715


def _cp(sem=None, vmem_mb=None):
    kw = {}
    if sem is not None:
        kw["dimension_semantics"] = sem
    if vmem_mb is not None:
        kw["vmem_limit_bytes"] = vmem_mb << 20
    return pltpu.CompilerParams(**kw)


def _dot(a, b):
    return jnp.dot(a, b, preferred_element_type=F32)


def _dot_nt(a, b):
    return lax.dot_general(a, b, (((1,), (1,)), ((), ())), preferred_element_type=F32)


def _dot_tn(a, b):
    return lax.dot_general(a, b, (((0,), (0,)), ((), ())), preferred_element_type=F32)


def _rms_r(x):
    return lax.rsqrt(jnp.mean(x * x, axis=-1, keepdims=True) + EPS)


def _rms_bwd(dh, n, r, g):
    dn = dh * g
    return r * (dn - n * jnp.mean(dn * n, axis=-1, keepdims=True))


def _gelu(x):
    t = jnp.tanh(_GELU_C * (x + _GELU_A * (x * x * x)))
    return 0.5 * x * (1.0 + t), t


def _gelu_grad(x, t):
    return 0.5 * (1.0 + t) + 0.5 * x * (1.0 - t * t) * (_GELU_C * (1.0 + 3.0 * _GELU_A * x * x))


def _sigmoid(x):
    return 1.0 / (1.0 + jnp.exp(-x))


def _tie(x, dep):
    return x if dep is None else lax.optimization_barrier((x, dep))[0]


def _row(tm, w):
    return pl.BlockSpec((tm, w), lambda i: (i, 0))


def _full(shape):
    nd = len(shape)
    return pl.BlockSpec(tuple(shape), lambda *_: (0,) * nd)


def _resident(shape):
    nd = len(shape)
    return pl.BlockSpec(tuple(shape), lambda *_: (0,) * nd, pipeline_mode=pl.Buffered(1))


def _sds(shape, dtype):
    return pltpu.HBM(tuple(shape), dtype)


def _hbm(*arrays):
    return [pltpu.with_memory_space_constraint(a, pltpu.HBM) for a in arrays]


HBM = pl.BlockSpec(memory_space=pltpu.HBM)
ANY = pl.BlockSpec(memory_space=pl.ANY)
SEM = pl.BlockSpec(memory_space=pltpu.SEMAPHORE)


def _band_buckets():
    i = np.arange(CHUNK)[:, None]
    j = np.arange(2 * CHUNK)[None, :]
    dist = i + CHUNK - j
    valid = (dist >= 0) & (dist < CHUNK)
    d = np.clip(dist, 0, None)
    max_exact = N_BUCKETS // 2
    large = max_exact + (np.log(np.maximum(d, 1) / max_exact) / np.log(MAX_DISTANCE / max_exact)
                         * (N_BUCKETS - max_exact)).astype(np.int32)
    large = np.minimum(large, N_BUCKETS - 1)
    buckets = np.where(d < max_exact, d, large).astype(np.int32)
    return np.where(valid, buckets, -1).astype(np.int32)


_A_COLS = slice(0, A_DIM)
_B_COLS = slice(A_DIM, A_DIM + B_DIM)
_G_COLS = slice(A_DIM + B_DIM, A_DIM + B_DIM + G_DIM)


def _inproj(x2, g_mix, w_in, g_sgu, w_s, b_st, tm, after=None):
    T = x2.shape[0]
    order = [] if after is None else [after]

    def body(*refs):
        x_ref, g_ref, w_ref, gs_ref, ws_ref, bs_ref = refs[:6]
        pg_ref, pa_ref, pb_ref, h_ref, ya_ref = refs[6 + len(order):]
        x = x_ref[...]
        h = (x * _rms_r(x) * g_ref[...]).astype(BF16)
        h_ref[...] = h
        pa = _dot(h, w_ref[:, _A_COLS]).astype(BF16)
        pa_ref[...] = pa
        pb_ref[...] = _dot(h, w_ref[:, _B_COLS]).astype(BF16)
        pg_ref[...] = _dot(h, w_ref[:, _G_COLS]).astype(BF16)
        _sgu_apply(pa.astype(F32), gs_ref[...], ws_ref, bs_ref, ya_ref)

    return pl.pallas_call(
        body, name="inproj", grid=(T // tm,),
        in_specs=[_row(tm, D_MODEL), _full(g_mix.shape), _resident(w_in.shape), _full(g_sgu.shape), _full(w_s.shape),
                  _full(b_st.shape)] + [ANY] * len(order),
        out_specs=[_row(tm, G_DIM), _row(tm, A_DIM), _row(tm, B_DIM), _row(tm, D_MODEL), _row(tm, A_WIDTH)],
        out_shape=[_sds((T, G_DIM), BF16), _sds((T, A_DIM), BF16), _sds((T, B_DIM), BF16), _sds((T, D_MODEL), BF16),
                   _sds((T, A_WIDTH), BF16)],
        compiler_params=_cp(("arbitrary",), 48),
    )(*_hbm(x2, g_mix, w_in, g_sgu, w_s, b_st), *order)


def _sgu_parts(p, g):
    pu = p[:, :A_WIDTH]
    pv = p[:, A_WIDTH:]
    u, tu = _gelu(pu)
    vv, tv = _gelu(pv)
    rv = _rms_r(vv)
    vn = (vv * rv * g).astype(BF16)
    return pu, pv, u, tu, vv, tv, rv, vn


def _tril():
    r = lax.broadcasted_iota(jnp.int32, (CHUNK, CHUNK), 0)
    c = lax.broadcasted_iota(jnp.int32, (CHUNK, CHUNK), 1)
    return r >= c


def _sgu_apply(p, g, ws_ref, bs_ref, y_ref):
    tril = _tril()
    _, _, u, _, _, _, _, vn = _sgu_parts(p, g)
    for gi in range(A_GROUPS):
        wm = jnp.where(tril, ws_ref[gi], 0.0).astype(BF16)
        bcol = bs_ref[:, gi:gi + 1]
        cs = slice(gi * CHUNK, (gi + 1) * CHUNK)
        for c in range(p.shape[0] // CHUNK):
            rs = slice(c * CHUNK, (c + 1) * CHUNK)
            s = _dot(wm, vn[rs, cs]) + bcol
            y_ref[rs, cs] = (u[rs, cs] * s).astype(BF16)


HEAD_ROWS = N_HEADS * CHUNK


def _head_rows(h):
    return slice(h * CHUNK, (h + 1) * CHUNK)


def _attn_setup(bias_scr, sink_scr, kvar_scr, qkv_ref, bk_ref, rel_ref, sink_ref):
    @pl.when(pl.program_id(0) == 0)
    def _():
        bk = bk_ref[...]
        for h in range(N_HEADS):
            acc = jnp.full((CHUNK, 2 * CHUNK), NEG_INF, F32)
            for b in range(N_BUCKETS):
                acc = jnp.where(bk == b, rel_ref[b, h], acc)
            bias_scr[_head_rows(h), :] = acc
            sink_scr[_head_rows(h), :] = jnp.full((CHUNK, LANES), sink_ref[0, h], F32)

    seq = qkv_ref.shape[0]
    rows_per = 2 * CHUNK
    for is_v in range(2):
        c0 = Q_DIM + is_v * KV_DIM
        for r in range(seq // rows_per):
            rs = slice(r * rows_per, (r + 1) * rows_per)
            a = qkv_ref[rs, c0:c0 + KV_DIM].astype(F32)
            lane = lax.broadcasted_iota(jnp.int32, a.shape, 1)
            lo = jnp.where(lane < HEAD_DIM, a, 0.0)
            hi = jnp.where(lane >= HEAD_DIM, a, 0.0)
            kvar_scr[4 * is_v + 0, rs, :] = lo.astype(BF16)
            kvar_scr[4 * is_v + 1, rs, :] = pltpu.roll(lo, HEAD_DIM, 1).astype(BF16)
            kvar_scr[4 * is_v + 2, rs, :] = pltpu.roll(hi, HEAD_DIM, 1).astype(BF16)
            kvar_scr[4 * is_v + 3, rs, :] = hi.astype(BF16)


def _rowsum(a, ones):
    hi = a.astype(BF16)
    lo = (a - hi.astype(F32)).astype(BF16)
    return _dot(hi, ones) + _dot(lo, ones)


def _both(a):
    return jnp.concatenate([a, a], axis=1)


def _attn_probs(qkv_ref, r0, n, kv, bias_scr, sink_scr, ones):
    s = jnp.concatenate([_dot_nt(qkv_ref[pl.ds(r0, CHUNK), (h // 2) * LANES:(h // 2 + 1) * LANES], kv[h // 4][h % 2])
                         for h in range(N_HEADS)], axis=0)
    s = s * (HEAD_DIM ** -0.5) + bias_scr[...]
    col = lax.broadcasted_iota(jnp.int32, s.shape, 1)
    s = jnp.where((col < CHUNK) & (n == 0), NEG_INF, s)
    sink = sink_scr[...]
    m = jnp.maximum(jnp.max(s, axis=-1, keepdims=True), sink)
    p = jnp.exp(s - _both(m))
    es = jnp.exp(sink - m)
    inv = 1.0 / (_dot(p.astype(BF16), ones) + es)
    return p * _both(inv), es * inv


def _attn_block_inputs(kvar_scr, n):
    r0 = pl.multiple_of(n * CHUNK, CHUNK)
    rp = pl.multiple_of(jnp.maximum(n - 1, 0) * CHUNK, CHUNK)

    def both(idx):
        return jnp.concatenate([kvar_scr[idx, pl.ds(rp, CHUNK), :], kvar_scr[idx, pl.ds(r0, CHUNK), :]], axis=0)

    kv = ((both(0), both(1)), (both(2), both(3)))
    vv = ((both(4), both(5)), (both(6), both(7)))
    return r0, kv, vv


def _attn_fwd(proj_b, sinks, rel_bias, n_seq, seq):
    nb = seq // CHUNK
    bk = jnp.asarray(_band_buckets())

    def body(qkv_ref, bk_ref, rel_ref, sink_ref, o_ref, bias_scr, sink_scr, kvar_scr):
        _attn_setup(bias_scr, sink_scr, kvar_scr, qkv_ref, bk_ref, rel_ref, sink_ref)
        ones = jnp.ones((2 * CHUNK, LANES), BF16)

        def blk(n, carry):
            r0, kv, vv = _attn_block_inputs(kvar_scr, n)
            prob, _ = _attn_probs(qkv_ref, r0, n, kv, bias_scr, sink_scr, ones)
            pb = prob.astype(BF16)
            for pr in range(N_HEADS // 2):
                acc = _dot(pb[_head_rows(2 * pr)], vv[pr // 2][0]) + _dot(pb[_head_rows(2 * pr + 1)], vv[pr // 2][1])
                o_ref[pl.ds(r0, CHUNK), pr * LANES:(pr + 1) * LANES] = acc.astype(BF16)
            return carry

        lax.fori_loop(0, nb, blk, 0)

    smem = pl.BlockSpec(memory_space=pltpu.SMEM)
    return pl.pallas_call(
        body, name="attn_fwd", grid=(n_seq,),
        in_specs=[_row(seq, B_DIM), _full(bk.shape), smem, smem],
        out_specs=_row(seq, Q_DIM), out_shape=_sds((n_seq * seq, Q_DIM), BF16),
        scratch_shapes=[pltpu.VMEM((HEAD_ROWS, 2 * CHUNK), F32), pltpu.VMEM((HEAD_ROWS, LANES), F32),
                        pltpu.VMEM((8, seq, KV_DIM), BF16)],
        compiler_params=_cp(("arbitrary",), 40),
    )(*_hbm(proj_b, bk), rel_bias, sinks)


def _dot_stacked(a, w_ref):
    return jnp.concatenate([_dot(a, w_ref[i]) for i in range(N_CHIPS)], axis=1)


def _dot_nt_stacked(a, w_ref):
    w = w_ref.shape[2]
    acc = _dot_nt(a[:, :w], w_ref[0])
    for i in range(1, N_CHIPS):
        acc = acc + _dot_nt(a[:, i * w:(i + 1) * w], w_ref[i])
    return acc


def _merge_fwd(x2, y_a, y_b, proj_g, w_pa, w_pb, w_out, tm):
    T = x2.shape[0]

    def body(x_ref, ya_ref, yb_ref, g_ref, wpa_ref, wpb_ref, wo_ref, x1_ref, mg_ref):
        g = g_ref[...].astype(F32)
        pa = _dot_stacked(ya_ref[...], wpa_ref)
        pb = _dot_stacked(yb_ref[...], wpb_ref)
        merged = (_sigmoid(g[:, :D_MODEL]) * pa + _sigmoid(g[:, D_MODEL:]) * pb).astype(BF16)
        mg_ref[...] = merged
        x1_ref[...] = x_ref[...] + _dot(merged, wo_ref[...])

    return pl.pallas_call(
        body, name="merge_fwd", grid=(T // tm,),
        in_specs=[_row(tm, D_MODEL), _row(tm, A_WIDTH), _row(tm, Q_DIM), _row(tm, G_DIM),
                  _resident(w_pa.shape), _resident(w_pb.shape), _resident(w_out.shape)],
        out_specs=[_row(tm, D_MODEL), _row(tm, D_MODEL)],
        out_shape=[_sds((T, D_MODEL), F32), _sds((T, D_MODEL), BF16)],
        compiler_params=_cp(("arbitrary",), 40),
    )(*_hbm(x2, y_a, y_b, proj_g, w_pa, w_pb, w_out))


def _upproj(x1, g_ffn, w_up, w_conv, b_conv, tm, seq):
    T = x1.shape[0]
    cw = w_up.shape[2]
    tiles_per_seq = seq // tm

    def body(x_ref, g_ref, w_ref, wc_ref, bc_ref, u_ref, h_ref, gate_ref, val_ref, tail_scr):
        at_start = (pl.program_id(0) % tiles_per_seq) == 0
        x = x_ref[...]
        h = (x * _rms_r(x) * g_ref[...]).astype(BF16)
        h_ref[...] = h
        for i in range(N_CHIPS):
            cs = slice(i * cw, (i + 1) * cw)
            u = _dot(h, w_ref[i])
            u_ref[:, cs] = u.astype(BF16)
            hl = jnp.where(at_start, 0.0, tail_scr[SUBLANES - 2:SUBLANES, cs])
            tail_scr[:, cs] = u[tm - SUBLANES:]
            up = _conv_out((u, _shift_down(u, hl, 1), _shift_down(u, hl, 2)), wc_ref[:, cs], bc_ref[:, cs])
            out_ref = gate_ref if i < N_CHIPS // 2 else val_ref
            out_ref[:, (i % 2) * cw:(i % 2 + 1) * cw] = up.astype(BF16)

    return pl.pallas_call(
        body, name="upproj", grid=(T // tm,),
        in_specs=[_row(tm, D_MODEL), _full(g_ffn.shape), _resident(w_up.shape), _full(w_conv.shape), _full(b_conv.shape)],
        out_specs=[_row(tm, 2 * D_FF), _row(tm, D_MODEL), _row(tm, D_FF), _row(tm, D_FF)],
        out_shape=[_sds((T, 2 * D_FF), BF16), _sds((T, D_MODEL), BF16), _sds((T, D_FF), BF16), _sds((T, D_FF), BF16)],
        scratch_shapes=[pltpu.VMEM((SUBLANES, 2 * D_FF), F32)],
        compiler_params=_cp(("arbitrary",), 56),
    )(*_hbm(x1, g_ffn, w_up, w_conv, b_conv))


def _shift_down(u, halo, k):
    rolled = pltpu.roll(u, k, 0)
    head = rolled[:SUBLANES]
    row = lax.broadcasted_iota(jnp.int32, head.shape, 0)
    if k == 1:
        head = jnp.where(row == 0, halo[1:2], head)
    else:
        head = jnp.where(row == 0, halo[0:1], jnp.where(row == 1, halo[1:2], head))
    return jnp.concatenate([head, rolled[SUBLANES:]], axis=0)


def _shift_up(d, halo, k):
    tm = d.shape[0]
    rolled = pltpu.roll(d, tm - k, 0)
    tail = rolled[tm - SUBLANES:]
    row = lax.broadcasted_iota(jnp.int32, tail.shape, 0)
    if k == 1:
        tail = jnp.where(row == SUBLANES - 1, halo[0:1], tail)
    else:
        tail = jnp.where(row == SUBLANES - 2, halo[0:1], jnp.where(row == SUBLANES - 1, halo[1:2], tail))
    return jnp.concatenate([rolled[:tm - SUBLANES], tail], axis=0)


def _conv_out(taps, wc, bc):
    u, u1, u2 = taps
    return wc[0:1] * u2 + wc[1:2] * u1 + wc[2:3] * u + bc


def _ffn_down_loss(gate, val, x1, target, w_down, g_final, tm):
    T = x1.shape[0]
    half = D_FF // 2

    def body(gt_ref, vl_ref, x1_ref, t_ref, wd_ref, g_ref, dx2_ref, loss_ref, gg_ref):
        i = pl.program_id(0)
        acc = jnp.zeros((tm, D_MODEL), F32)
        for j in range(2):
            gc = slice(j * half, (j + 1) * half)
            gate = gt_ref[:, gc].astype(F32)
            act = (gate * _sigmoid(gate) * vl_ref[:, gc].astype(F32)).astype(BF16)
            acc = acc + _dot(act, wd_ref[gc, :])
        x2 = x1_ref[...] + acc
        r = _rms_r(x2)
        n = x2 * r
        g = g_ref[...]
        diff = n * g - t_ref[...]
        dy = diff * (1.0 / D_MODEL)
        dx2_ref[...] = _rms_bwd(dy, n, r, g)

        @pl.when(i == 0)
        def _():
            loss_ref[...] = jnp.zeros_like(loss_ref)
            gg_ref[...] = jnp.zeros_like(gg_ref)

        loss_ref[...] += 0.5 * jnp.sum(jnp.mean(diff * diff, axis=-1, keepdims=True), axis=0, keepdims=True)
        gg_ref[...] += jnp.sum(dy * n, axis=0, keepdims=True)

    return pl.pallas_call(
        body, name="ffn_down_loss", grid=(T // tm,),
        in_specs=[_row(tm, D_FF), _row(tm, D_FF), _row(tm, D_MODEL), _row(tm, D_MODEL),
                  _resident(w_down.shape), _full(g_final.shape)],
        out_specs=[_row(tm, D_MODEL), _full((1, 1)), _full((1, D_MODEL))],
        out_shape=[_sds((T, D_MODEL), F32), _sds((1, 1), F32), _sds((1, D_MODEL), F32)],
        compiler_params=_cp(("arbitrary",), 48),
    )(*_hbm(gate, val, x1, target, w_down, g_final))


def _ffn_bwd_act(gate, val, dx2, w_down, tm):
    T = dx2.shape[0]
    half = D_FF // 2
    nt = T // tm

    def body(g_ref, v_ref, dx_ref, wd_ref, dg_ref, dv_ref, gwd_out, gbg_ref, gbv_ref, gwd_ref):
        i = pl.program_id(1)

        @pl.when(i == 0)
        def _():
            for r in (gwd_ref, gbg_ref, gbv_ref):
                r[...] = jnp.zeros_like(r)

        dx = dx_ref[...].astype(BF16)
        for c0 in range(0, half, COL_CHUNK):
            cs = slice(c0, min(c0 + COL_CHUNK, half))
            gate = g_ref[:, cs].astype(F32)
            val = v_ref[:, cs].astype(F32)
            sg = _sigmoid(gate)
            silu = gate * sg
            d_act = _dot_nt(dx, wd_ref[cs, :])
            d_val = d_act * silu
            d_gate = d_act * val * (sg * (1.0 + gate * (1.0 - sg)))
            dg_ref[:, cs] = d_gate.astype(BF16)
            dv_ref[:, cs] = d_val.astype(BF16)
            gwd_ref[cs, :] += _dot_tn((silu * val).astype(BF16), dx)
            gbg_ref[:, cs] += jnp.sum(d_gate, axis=0, keepdims=True)
            gbv_ref[:, cs] += jnp.sum(d_val, axis=0, keepdims=True)

        @pl.when(i == nt - 1)
        def _():
            gwd_out[...] = gwd_ref[...].astype(BF16)

    tile = pl.BlockSpec((tm, half), lambda j, i: (i, j))
    vec = pl.BlockSpec((1, half), lambda j, i: (0, j))
    wrows = pl.BlockSpec((half, D_MODEL), lambda j, i: (j, 0))
    return pl.pallas_call(
        body, name="ffn_bwd_act", grid=(2, nt),
        in_specs=[tile, tile, pl.BlockSpec((tm, D_MODEL), lambda j, i: (i, 0)), wrows],
        out_specs=[tile, tile, wrows, vec, vec],
        out_shape=[_sds((T, D_FF), BF16), _sds((T, D_FF), BF16), _sds((D_FF, D_MODEL), BF16),
                   _sds((1, D_FF), F32), _sds((1, D_FF), F32)],
        scratch_shapes=[pltpu.VMEM((half, D_MODEL), F32)],
        compiler_params=_cp(("arbitrary", "arbitrary"), 56),
    )(*_hbm(gate, val, dx2, w_down))


def _ffn_bwd_up(d_gate, d_val, upre, dx2, x1, g_ffn, w_conv, w_up, tm, seq):
    T = dx2.shape[0]
    tiles_per_seq = seq // tm
    k16 = tm // BF16_ROWS
    n16 = T // BF16_ROWS
    cw = D_FF // 2

    def body(dg_ref, dv_ref, hg_ref, hv_ref, u_ref, dx2_ref, x1_ref, g_ref, wc_ref, wu_ref, du_ref, dx1_ref, gg_ref, gwc_ref):
        i = pl.program_id(0)
        at_end = (i % tiles_per_seq) == tiles_per_seq - 1

        @pl.when(i == 0)
        def _():
            gg_ref[...] = jnp.zeros_like(gg_ref)
            gwc_ref[...] = jnp.zeros_like(gwc_ref)

        dh = jnp.zeros((tm, D_MODEL), F32)
        for j in range(4):
            src, hsrc = (dg_ref, hg_ref) if j < 2 else (dv_ref, hv_ref)
            ls = slice((j % 2) * cw, (j % 2 + 1) * cw)
            cs = slice(j * cw, (j + 1) * cw)
            d = src[:, ls].astype(F32)
            hl = hsrc[:, ls].astype(F32)[0:2]
            hl = jnp.where(at_end, 0.0, hl)
            wc = wc_ref[:, cs]
            d1 = _shift_up(d, hl, 1)
            d2 = _shift_up(d, hl, 2)
            du = (wc[2:3] * d + wc[1:2] * d1 + wc[0:1] * d2).astype(BF16)
            du_ref[:, cs] = du
            dh = dh + _dot_nt(du, wu_ref[j])
            u = u_ref[:, cs].astype(F32)
            gwc_ref[0:1, cs] += jnp.sum(d2 * u, axis=0, keepdims=True)
            gwc_ref[1:2, cs] += jnp.sum(d1 * u, axis=0, keepdims=True)
            gwc_ref[2:3, cs] += jnp.sum(d * u, axis=0, keepdims=True)
        x = x1_ref[...]
        r = _rms_r(x)
        n = x * r
        dx1_ref[...] = dx2_ref[...] + _rms_bwd(dh, n, r, g_ref[...])
        gg_ref[...] += jnp.sum(dh * n, axis=0, keepdims=True)

    nxt = pl.BlockSpec((BF16_ROWS, D_FF), lambda i: (jnp.minimum((i + 1) * k16, n16 - 1), 0))
    return pl.pallas_call(
        body, name="ffn_bwd_up", grid=(T // tm,),
        in_specs=[_row(tm, D_FF), _row(tm, D_FF), nxt, nxt, _row(tm, 2 * D_FF), _row(tm, D_MODEL), _row(tm, D_MODEL),
                  _full(g_ffn.shape), _full(w_conv.shape), _resident(w_up.shape)],
        out_specs=[_row(tm, 2 * D_FF), _row(tm, D_MODEL), _full((1, D_MODEL)), _full((3, 2 * D_FF))],
        out_shape=[_sds((T, 2 * D_FF), BF16), _sds((T, D_MODEL), F32), _sds((1, D_MODEL), F32), _sds((3, 2 * D_FF), F32)],
        compiler_params=_cp(("arbitrary",), 56),
    )(*_hbm(d_gate, d_val, d_gate, d_val, upre, dx2, x1, g_ffn, w_conv, w_up))


def _matmul_tn(a, b, tn, tk, name):
    T, M = a.shape
    N = b.shape[1]
    nk = T // tk

    def body(a_ref, b_ref, o_ref, acc_ref):
        k = pl.program_id(1)

        @pl.when(k == 0)
        def _():
            acc_ref[...] = jnp.zeros_like(acc_ref)

        acc_ref[...] += _dot_tn(a_ref[...], b_ref[...])

        @pl.when(k == nk - 1)
        def _():
            o_ref[...] = acc_ref[...].astype(BF16)

    return pl.pallas_call(
        body, name=name, grid=(N // tn, nk),
        in_specs=[pl.BlockSpec((tk, M), lambda j, k: (k, 0)), pl.BlockSpec((tk, tn), lambda j, k: (k, j))],
        out_specs=pl.BlockSpec((M, tn), lambda j, k: (0, j)), out_shape=_sds((M, N), BF16),
        scratch_shapes=[pltpu.VMEM((M, tn), F32)],
        compiler_params=_cp(("arbitrary", "arbitrary"), 48),
    )(*_hbm(a, b))


def _merge_bwd(dx1, merged, y_a, y_b, proj_g, proj_a, w_pa, w_pb, w_out, g_sgu, w_s, b_st, tm, after=None):
    T = dx1.shape[0]

    nt = T // tm
    pshape = (A_WIDTH, D_MODEL)
    order = [] if after is None else [after]

    def body(*refs):
        dx_ref, mg_ref, ya_ref, yb_ref, g_ref, p_ref, wpa_ref, wpb_ref, wo_ref, gs_ref, ws_ref, bs_ref = refs[:12]
        (dg_ref, da_ref, dyb_ref, gwo_out, gwpa_out, gwpb_out, gws_ref, gbs_ref, gg_ref,
         gwo_ref, gwpa_ref, gwpb_ref) = refs[12 + len(order):]
        i = pl.program_id(0)

        @pl.when(i == 0)
        def _():
            for r in (gwo_ref, gwpa_ref, gwpb_ref, gws_ref, gbs_ref, gg_ref):
                r[...] = jnp.zeros_like(r)

        dx = dx_ref[...].astype(BF16)
        dm = _dot_nt(dx, wo_ref[...])
        g = g_ref[...].astype(F32)
        ya = ya_ref[...]
        yb = yb_ref[...]
        pa = _dot_stacked(ya, wpa_ref)
        pb = _dot_stacked(yb, wpb_ref)
        sa = _sigmoid(g[:, :D_MODEL])
        sb = _sigmoid(g[:, D_MODEL:])
        dpa = (dm * sa).astype(BF16)
        dpb = (dm * sb).astype(BF16)
        dg_ref[:, :D_MODEL] = (dm * pa * (sa * (1.0 - sa))).astype(BF16)
        dg_ref[:, D_MODEL:] = (dm * pb * (sb * (1.0 - sb))).astype(BF16)
        d_ya = _dot_nt_stacked(dpa, wpa_ref).astype(BF16)
        dyb_ref[...] = _dot_nt_stacked(dpb, wpb_ref).astype(BF16)
        _sgu_bwd_apply(p_ref[...].astype(F32), d_ya.astype(F32), gs_ref[...], ws_ref, bs_ref, da_ref, gws_ref, gbs_ref, gg_ref)
        gwo_ref[...] += _dot_tn(mg_ref[...], dx)
        gwpa_ref[...] += _dot_tn(ya, dpa)
        gwpb_ref[...] += _dot_tn(yb, dpb)

        @pl.when(i == nt - 1)
        def _():
            gwo_out[...] = gwo_ref[...].astype(BF16)
            gwpa_out[...] = gwpa_ref[...].astype(BF16)
            gwpb_out[...] = gwpb_ref[...].astype(BF16)

    return pl.pallas_call(
        body, name="merge_bwd", grid=(nt,),
        in_specs=[_row(tm, D_MODEL), _row(tm, D_MODEL), _row(tm, A_WIDTH), _row(tm, Q_DIM), _row(tm, G_DIM), _row(tm, A_DIM),
                  _resident(w_pa.shape), _resident(w_pb.shape), _resident(w_out.shape),
                  _full(g_sgu.shape), _full(w_s.shape), _full(b_st.shape)] + [ANY] * len(order),
        out_specs=[_row(tm, G_DIM), _row(tm, A_DIM), _row(tm, Q_DIM),
                   _full(w_out.shape), _full(pshape), _full(pshape), _full(w_s.shape), _full(b_st.shape), _full(g_sgu.shape)],
        out_shape=[_sds((T, G_DIM), BF16), _sds((T, A_DIM), BF16), _sds((T, Q_DIM), BF16),
                   _sds(w_out.shape, BF16), _sds(pshape, BF16), _sds(pshape, BF16),
                   _sds(w_s.shape, F32), _sds(b_st.shape, F32), _sds(g_sgu.shape, F32)],
        scratch_shapes=[pltpu.VMEM(w_out.shape, F32), pltpu.VMEM(pshape, F32), pltpu.VMEM(pshape, F32)],
        compiler_params=_cp(("arbitrary",), 56),
    )(*_hbm(dx1, merged, y_a, y_b, proj_g, proj_a, w_pa, w_pb, w_out, g_sgu, w_s, b_st), *order)


def _sgu_bwd_apply(p, dy, g, ws_ref, bs_ref, dp_ref, gws_ref, gbs_ref, gg_ref):
    tril = _tril()
    pu, pv, u, tu, vv, tv, rv, vn = _sgu_parts(p, g)
    du_cols = []
    dvn_cols = []
    for gi in range(A_GROUPS):
        wm = jnp.where(tril, ws_ref[gi], 0.0).astype(BF16)
        wmt = wm.astype(F32).T.astype(BF16)
        bcol = bs_ref[:, gi:gi + 1]
        cs = slice(gi * CHUNK, (gi + 1) * CHUNK)
        du_rows = []
        dvn_rows = []
        gw = jnp.zeros((CHUNK, CHUNK), F32)
        gb = jnp.zeros((CHUNK, 1), F32)
        for c in range(p.shape[0] // CHUNK):
            rs = slice(c * CHUNK, (c + 1) * CHUNK)
            vn_c = vn[rs, cs]
            s = _dot(wm, vn_c) + bcol
            dy_c = dy[rs, cs]
            ds = dy_c * u[rs, cs]
            du_rows.append(dy_c * s)
            dsb = ds.astype(BF16)
            gw = gw + _dot_nt(dsb, vn_c)
            gb = gb + jnp.sum(ds, axis=-1, keepdims=True)
            dvn_rows.append(_dot(wmt, dsb))
        gws_ref[gi] += jnp.where(tril, gw, 0.0)
        gbs_ref[:, gi:gi + 1] += gb
        du_cols.append(jnp.concatenate(du_rows, axis=0))
        dvn_cols.append(jnp.concatenate(dvn_rows, axis=0))
    du = jnp.concatenate(du_cols, axis=1)
    dvn = jnp.concatenate(dvn_cols, axis=1)
    vhat = vv * rv
    gg_ref[...] += jnp.sum(dvn * vhat, axis=0, keepdims=True)
    dvv = _rms_bwd(dvn, vhat, rv, g)
    dp_ref[:, :A_WIDTH] = (du * _gelu_grad(pu, tu)).astype(BF16)
    dp_ref[:, A_WIDTH:] = (dvv * _gelu_grad(pv, tv)).astype(BF16)


def _attn_bwd(proj_b, d_yb, sinks, rel_bias, n_seq, seq, after=None):
    nb = seq // CHUNK
    bk = jnp.asarray(_band_buckets())
    order = [] if after is None else [after]

    def body(*refs):
        qkv_ref, do_ref, bk_ref, rel_ref, sink_ref = refs[:5]
        (d_ref, gs_ref, gr_ref, bias_scr, sink_scr, kvar_scr, dbias_scr, dk_scr, dv_scr, ds_scr) = refs[5 + len(order):]
        b = pl.program_id(0)
        _attn_setup(bias_scr, sink_scr, kvar_scr, qkv_ref, bk_ref, rel_ref, sink_ref)
        ones = jnp.ones((2 * CHUNK, LANES), BF16)

        @pl.when(b == 0)
        def _():
            dbias_scr[...] = jnp.zeros_like(dbias_scr)
            ds_scr[...] = jnp.zeros_like(ds_scr)

        dk_scr[...] = jnp.zeros_like(dk_scr)
        dv_scr[...] = jnp.zeros_like(dv_scr)

        def transposed(a):
            return a.astype(F32).T.astype(BF16)

        def blk(n, carry):
            r0, kv, vv = _attn_block_inputs(kvar_scr, n)
            prob, psink = _attn_probs(qkv_ref, r0, n, kv, bias_scr, sink_scr, ones)
            dp = jnp.concatenate([_dot_nt(do_ref[pl.ds(r0, CHUNK), (h // 2) * LANES:(h // 2 + 1) * LANES], vv[h // 4][h % 2])
                                  for h in range(N_HEADS)], axis=0)
            delta = _rowsum(prob * dp, ones)
            dsc = prob * (dp - _both(delta))
            ds_scr[...] += psink * delta
            dbias_scr[...] += dsc
            dsb = (dsc * (HEAD_DIM ** -0.5)).astype(BF16)
            pb = prob.astype(BF16)
            dkt = [jnp.zeros((HEAD_DIM, 2 * CHUNK), F32) for _ in range(2)]
            dvt = [jnp.zeros((HEAD_DIM, 2 * CHUNK), F32) for _ in range(2)]
            for pr in range(N_HEADS // 2):
                ps = slice(pr * LANES, (pr + 1) * LANES)
                qpt = transposed(qkv_ref[pl.ds(r0, CHUNK), ps])
                dopt = transposed(do_ref[pl.ds(r0, CHUNK), ps])
                kvh = pr // 2
                dq = jnp.zeros((CHUNK, LANES), F32)
                for hh in range(2):
                    hr = _head_rows(2 * pr + hh)
                    rows = slice(hh * HEAD_DIM, (hh + 1) * HEAD_DIM)
                    dq = dq + _dot(dsb[hr], kv[kvh][hh])
                    dkt[kvh] = dkt[kvh] + _dot(qpt, dsb[hr])[rows]
                    dvt[kvh] = dvt[kvh] + _dot(dopt, pb[hr])[rows]
                d_ref[pl.ds(r0, CHUNK), ps] = dq.astype(BF16)
            dk_scr[:, pl.ds(r0, 2 * CHUNK)] += jnp.concatenate(dkt, axis=0)
            dv_scr[:, pl.ds(r0, 2 * CHUNK)] += jnp.concatenate(dvt, axis=0)
            return carry

        lax.fori_loop(0, nb, blk, 0)
        for n in range(nb):
            rows = slice(n * CHUNK, (n + 1) * CHUNK)
            cols = slice((n + 1) * CHUNK, (n + 2) * CHUNK)
            d_ref[rows, Q_DIM:Q_DIM + KV_DIM] = dk_scr[:, cols].T.astype(BF16)
            d_ref[rows, Q_DIM + KV_DIM:] = dv_scr[:, cols].T.astype(BF16)

        @pl.when(b == n_seq - 1)
        def _():
            bkv = bk_ref[...]
            for h in range(N_HEADS):
                gs_ref[0:1, h:h + 1] = -jnp.sum(ds_scr[_head_rows(h), 0:1], axis=0, keepdims=True)
                db = dbias_scr[_head_rows(h), :]
                for bb in range(N_BUCKETS):
                    part = jnp.sum(jnp.where(bkv == bb, db, 0.0), axis=-1, keepdims=True)
                    gr_ref[bb:bb + 1, h:h + 1] = jnp.sum(part, axis=0, keepdims=True)

    smem = pl.BlockSpec(memory_space=pltpu.SMEM)
    return pl.pallas_call(
        body, name="attn_bwd", grid=(n_seq,),
        in_specs=[_row(seq, B_DIM), _row(seq, Q_DIM), _full(bk.shape), smem, smem] + [ANY] * len(order),
        out_specs=[_row(seq, B_DIM), _full((1, N_HEADS)), _full((N_BUCKETS, N_HEADS))],
        out_shape=[_sds((n_seq * seq, B_DIM), BF16), _sds((1, N_HEADS), F32), _sds((N_BUCKETS, N_HEADS), F32)],
        scratch_shapes=[pltpu.VMEM((HEAD_ROWS, 2 * CHUNK), F32), pltpu.VMEM((HEAD_ROWS, LANES), F32),
                        pltpu.VMEM((8, seq, KV_DIM), BF16), pltpu.VMEM((HEAD_ROWS, 2 * CHUNK), F32),
                        pltpu.VMEM((KV_DIM, seq + CHUNK), F32), pltpu.VMEM((KV_DIM, seq + CHUNK), F32),
                        pltpu.VMEM((HEAD_ROWS, LANES), F32)],
        compiler_params=_cp(("arbitrary",), 40),
    )(*_hbm(proj_b, d_yb, bk), rel_bias, sinks, *order)


def _inproj_bwd(d_g, d_a, d_b, x2, dx1, g_mix, w_in, tm, after=None):
    T = x2.shape[0]
    order = [] if after is None else [after]

    def body(*refs):
        dg_ref, da_ref, db_ref, x_ref, dx1_ref, g_ref, w_ref = refs[:7]
        gx_ref, gg_ref = refs[7 + len(order):]
        dh = (_dot_nt(dg_ref[...], w_ref[:, _G_COLS]) + _dot_nt(da_ref[...], w_ref[:, _A_COLS])
              + _dot_nt(db_ref[...], w_ref[:, _B_COLS]))
        x = x_ref[...]
        r = _rms_r(x)
        n = x * r
        gx_ref[...] = dx1_ref[...] + _rms_bwd(dh, n, r, g_ref[...])

        @pl.when(pl.program_id(0) == 0)
        def _():
            gg_ref[...] = jnp.zeros_like(gg_ref)

        gg_ref[...] += jnp.sum(dh * n, axis=0, keepdims=True)

    return pl.pallas_call(
        body, name="inproj_bwd", grid=(T // tm,),
        in_specs=[_row(tm, G_DIM), _row(tm, A_DIM), _row(tm, B_DIM), _row(tm, D_MODEL), _row(tm, D_MODEL),
                  _full(g_mix.shape), _resident(w_in.shape)] + [ANY] * len(order),
        out_specs=[_row(tm, D_MODEL), _full((1, D_MODEL))],
        out_shape=[_sds((T, D_MODEL), F32), _sds((1, D_MODEL), F32)],
        compiler_params=_cp(("arbitrary",), 48),
    )(*_hbm(d_g, d_a, d_b, x2, dx1, g_mix, w_in), *order)


IN_SHARD = (A_DIM + B_DIM + G_DIM) // N_CHIPS


def _unstack_w_in(stack):
    tr = 256

    def body(s_ref, o_ref):
        for i in range(N_CHIPS):
            o_ref[:, i * IN_SHARD:(i + 1) * IN_SHARD] = s_ref[i]

    return pl.pallas_call(
        body, name="unstack_w_in", grid=(D_MODEL // tr,),
        in_specs=[pl.BlockSpec((N_CHIPS, tr, IN_SHARD), lambda r: (0, r, 0))],
        out_specs=pl.BlockSpec((tr, N_CHIPS * IN_SHARD), lambda r: (r, 0)),
        out_shape=_sds((D_MODEL, N_CHIPS * IN_SHARD), stack.dtype),
        compiler_params=_cp(("arbitrary",)),
    )(*_hbm(stack))


def _grad_w_in(h, d_a, d_b, d_g, tk):
    T = h.shape[0]
    nk = T // tk
    in_dim = N_CHIPS * IN_SHARD

    def body(h_ref, da_ref, db_ref, dg_ref, o_ref, acc_ref):
        k = pl.program_id(0)

        @pl.when(k == 0)
        def _():
            acc_ref[...] = jnp.zeros_like(acc_ref)

        hb = h_ref[...]
        acc_ref[:, _A_COLS] += _dot_tn(hb, da_ref[...])
        acc_ref[:, _B_COLS] += _dot_tn(hb, db_ref[...])
        acc_ref[:, _G_COLS] += _dot_tn(hb, dg_ref[...])

        @pl.when(k == nk - 1)
        def _():
            for i in range(N_CHIPS):
                o_ref[i] = acc_ref[:, i * IN_SHARD:(i + 1) * IN_SHARD].astype(BF16)

    return pl.pallas_call(
        body, name="grad_w_in", grid=(nk,),
        in_specs=[_row(tk, D_MODEL), _row(tk, A_DIM), _row(tk, B_DIM), _row(tk, G_DIM)],
        out_specs=_full((N_CHIPS, D_MODEL, IN_SHARD)), out_shape=_sds((N_CHIPS, D_MODEL, IN_SHARD), BF16),
        scratch_shapes=[pltpu.VMEM((D_MODEL, in_dim), F32)],
        compiler_params=_cp(("arbitrary",), 56),
    )(*_hbm(h, d_a, d_b, d_g))


def _local_step(x, target, g_mix, g_sgu, w_s, b_s, sinks, rel_bias, g_ffn, b_conv, g_final,
                w_in, w_conv, proj_weights, ffn_weights, on_grads, after=None):
    n_seq, seq, _ = x.shape
    T = n_seq * seq
    tm = min(ROW_TILE, seq)
    tw = min(GRAD_ROW_TILE, T)
    tf = min(WIDE_ROW_TILE, seq)
    x2 = x.reshape(T, D_MODEL)
    tgt = target.reshape(T, D_MODEL)
    b_st = b_s.T
    g_fin = g_final.reshape(1, D_MODEL)

    proj_g, proj_a, proj_b, h, y_a = _inproj(x2, g_mix, w_in, g_sgu, w_s, b_st, tm, after)
    y_b = _attn_fwd(proj_b, sinks, rel_bias, n_seq, seq)
    w_pa, w_pb, w_out = proj_weights(y_b)
    x1, merged = _merge_fwd(x2, y_a, y_b, proj_g, w_pa, w_pb, w_out, tm)
    w_up, w_down = ffn_weights(x1)
    upre, h2, gate, val = _upproj(x1, g_ffn, w_up, w_conv, b_conv, tf, seq)
    dx2, loss, gg_final = _ffn_down_loss(gate, val, x1, tgt, w_down, g_fin, tm)

    d_gate, d_val, gw_down, gb_g, gb_v = _ffn_bwd_act(gate, val, dx2, w_down, tw)
    gb_conv = jnp.concatenate([gb_g, gb_v], axis=1)
    d_upre, dx1, gg_ffn, gw_conv = _ffn_bwd_up(d_gate, d_val, upre, dx2, x1, g_ffn, w_conv, w_up, tf, seq)
    gw_up = _matmul_tn(h2, d_upre, 2 * D_FF // 4, min(2 * GRAD_ROW_TILE, T), "grad_w_up")
    sent = on_grads("ffn", dict(w_up=gw_up, w_down=gw_down))
    d_g, d_a, d_yb, gw_out, gw_pa, gw_pb, gw_s, gb_st, gg_sgu = _merge_bwd(
        dx1, merged, y_a, y_b, proj_g, proj_a, w_pa, w_pb, w_out, g_sgu, w_s, b_st, tw, sent)
    sent = on_grads("proj", dict(w_pa=gw_pa, w_pb=gw_pb, w_out=gw_out))
    d_b, g_sinks, g_rel = _attn_bwd(proj_b, d_yb, sinks, rel_bias, n_seq, seq, sent)
    gw_in = _grad_w_in(h, d_a, d_b, d_g, min(2 * GRAD_ROW_TILE, T))
    sent = on_grads("in", dict(w_in=gw_in))
    grad_x, gg_mix = _inproj_bwd(d_g, d_a, d_b, x2, dx1, g_mix, w_in, tm, sent)

    small = dict(g_mix=gg_mix, g_sgu=gg_sgu, w_s=gw_s, b_s=gb_st.T, sinks=g_sinks, rel_bias=g_rel,
                 g_ffn=gg_ffn, b_conv=gb_conv, g_final=gg_final, w_conv=gw_conv)
    big = dict(w_in=gw_in, w_pa=gw_pa, w_pb=gw_pb, w_out=gw_out, w_up=gw_up, w_down=gw_down)
    return loss, grad_x.reshape(x.shape), small, big


_MIXER = ("w_in", "w_pa", "w_pb", "w_out")
_FFN = ("w_up", "w_down")
_BIG = _MIXER + _FFN

CONV_ROWS = 6
_SMALL_AT = dict(loss=(0, 1, 1), g_final=(1, 1, D_MODEL), g_mix=(2, 1, D_MODEL), g_ffn=(3, 1, D_MODEL), g_sgu=(4, 1, A_WIDTH),
                 sinks=(5, 1, N_HEADS), rel_bias=(6, 1, N_BUCKETS * N_HEADS), b_s=(8, A_GROUPS, CHUNK),
                 b_conv=(12, CONV_ROWS, D_MODEL), w_conv=(18, 3 * CONV_ROWS, D_MODEL), w_s=(40, A_GROUPS * CHUNK * CHUNK // D_MODEL, D_MODEL))
_SMALL_IN_CALL = ("g_final", "g_mix", "g_ffn", "g_sgu", "sinks", "b_s")
SMALL_ROWS = 104


def _pack_small(vals):
    def wide(a):
        return jnp.pad(a, ((0, 0), (0, CONV_ROWS * D_MODEL - a.shape[1]))).reshape(-1, D_MODEL)

    laid = dict(vals, b_conv=wide(vals["b_conv"]), w_conv=wide(vals["w_conv"]), w_s=vals["w_s"].reshape(-1, D_MODEL))
    rows, at = [], 0
    for n, (r0, nr, nc) in _SMALL_AT.items():
        if r0 > at:
            rows.append(jnp.zeros((r0 - at, D_MODEL), F32))
        rows.append(jnp.pad(laid[n].astype(F32).reshape(nr, nc), ((0, 0), (0, D_MODEL - nc))))
        at = r0 + nr
    return jnp.concatenate(rows, axis=0)


def _unwide(a, r):
    return a.reshape(r, CONV_ROWS * D_MODEL)[:, :2 * D_FF]


def _mesh_pos():
    return lax.axis_index("x"), lax.axis_index("y"), lax.axis_index("c")


def _other_chips(x, y):
    return [(1 - x, y), (x, 1 - y), (1 - x, 1 - y)]


def _remote(src, dst, send_sem, recv_sem, to):
    return pltpu.make_async_remote_copy(src_ref=src, dst_ref=dst, send_sem=send_sem, recv_sem=recv_sem,
                                        device_id=to, device_id_type=MESH)


def _own_slot(own, n, at):
    return lax.dynamic_update_slice(lax.empty((n,) + own.shape, own.dtype), own[None], (at,) + (0,) * own.ndim)


def _allgather_weights(stacks, wc_stack):
    names = list(stacks)
    n = len(names)

    def body(*refs):
        ins, outs = refs[:n + 1], refs[n + 1:2 * n + 2]
        send_sems, recv_sems = refs[2 * n + 2:]
        x, y, c = _mesh_pos()
        _handshake(_chip_peers(x, y, c) + _sibling_peers(x, y, c))
        me = 2 * x + y
        sibling = (x, y, 1 - c)
        chips = _other_chips(x, y)

        def half(ref, chip, hc):
            hr = ref.shape[1] // 2
            return ref.at[chip, pl.ds(hc * hr, hr), :]

        first = []
        for k in range(n):
            first += [_remote(half(ins[k], me, c), half(outs[k], me, c), send_sems.at[6 * k + j], recv_sems.at[6 * k + j], (cx, cy, c))
                      for j, (cx, cy) in enumerate(chips)]
        first += [_remote(ins[n].at[me], outs[n].at[me], send_sems.at[6 * n + j], recv_sems.at[6 * n + j], (cx, cy, c))
                  for j, (cx, cy) in enumerate(chips)]
        for cp in first:
            cp.start()
        passed = []
        for k in range(n):
            for j, (cx, cy) in enumerate(chips):
                landed = half(outs[k], 2 * cx + cy, c)
                _remote(landed, landed, send_sems.at[6 * k + j], recv_sems.at[6 * k + j], (x, y, c)).wait_recv()
                passed.append(_remote(landed, landed, send_sems.at[6 * k + 3 + j], recv_sems.at[6 * k + 3 + j], sibling))
                passed[-1].start()
        for k in range(n):
            for j, (cx, cy) in enumerate(chips):
                theirs = half(outs[k], 2 * cx + cy, 1 - c)
                _remote(theirs, theirs, send_sems.at[6 * k + 3 + j], recv_sems.at[6 * k + 3 + j], (x, y, c)).wait_recv()
        for j, (cx, cy) in enumerate(chips):
            slot = outs[n].at[2 * cx + cy]
            _remote(slot, slot, send_sems.at[6 * n + j], recv_sems.at[6 * n + j], (x, y, c)).wait_recv()
        for cp in first + passed:
            cp.wait_send()

    arrays = [stacks[k] for k in names] + [wc_stack]
    outs = pl.pallas_call(
        body, name="allgather_weights",
        in_specs=[HBM] * (n + 1), out_specs=[HBM] * (n + 1), input_output_aliases={k: k for k in range(n + 1)},
        out_shape=[_sds(a.shape, a.dtype) for a in arrays],
        scratch_shapes=[pltpu.SemaphoreType.DMA((6 * n + 3,)), pltpu.SemaphoreType.DMA((6 * n + 3,))],
        compiler_params=pltpu.CompilerParams(collective_id=_COLLECTIVE["gather_in"]),
    )(*arrays)
    return dict(zip(names, outs[:n])), outs[n]


_KIND = {"w_in": "stack", "w_pa": "col", "w_pb": "col", "w_up": "col", "w_out": "row", "w_down": "row"}


def _half_view(ref, kind, h):
    if kind == "stack":
        k = ref.shape[1] // 2
        return ref.at[:, pl.ds(h * k, k), :]
    if kind == "col":
        k = ref.shape[0] // 2
        return ref.at[pl.ds(h * k, k), :]
    k = ref.shape[1] // 2
    return ref.at[:, pl.ds(h * k, k)]


def _shard_view(ref, kind, i):
    if kind == "stack":
        return ref.at[i]
    if kind == "col":
        k = ref.shape[1] // N_CHIPS
        return ref.at[:, pl.ds(i * k, k)]
    k = ref.shape[0] // N_CHIPS
    return ref.at[pl.ds(i * k, k), :]


def _region_view(ref, kind, h):
    if kind == "row":
        k = ref.shape[1] // 2
        return ref.at[:, pl.ds(h * k, k)]
    k = ref.shape[0] // 2
    return ref.at[pl.ds(h * k, k), :]


def _half_shape(shape, kind):
    if kind == "stack":
        return (shape[0], shape[1] // 2, shape[2])
    return (shape[0] // 2, shape[1]) if kind == "col" else (shape[0], shape[1] // 2)


def _part_shape(half_shape, kind):
    if kind == "stack":
        return tuple(half_shape[1:])
    k, w = half_shape
    return (k, w // N_CHIPS) if kind == "col" else (k // N_CHIPS, w)


_DATAFLOW = pltpu.SideEffectType.DATAFLOW_SIDE_EFFECTING
_TOKEN = (SUBLANES, LANES)


_COLLECTIVE = {k: i for i, k in enumerate(
    [kind + "_" + g for kind in ("pair", "chip", "share") for g in ("ffn", "proj", "in")]
    + ["gather_proj", "gather_ffn", "gather_in", "forward_proj", "forward_ffn"])}


def _sibling_peers(x, y, c):
    return [(x, y, 1 - c)]


def _chip_peers(x, y, c):
    return [(cx, cy, c) for cx, cy in _other_chips(x, y)]


def _handshake(peers):
    barrier = pltpu.get_barrier_semaphore()
    for peer in peers:
        pl.semaphore_signal(barrier, inc=1, device_id=peer, device_id_type=MESH)
    pl.semaphore_wait(barrier, len(peers))


def _split_start(name, arrays, n_sems, issue, after=None, handshake=None):
    n = len(arrays)
    order = [] if after is None else [after]

    def body(*refs):
        base = n + len(order)
        if handshake is not None:
            _handshake(handshake[1](*_mesh_pos()))
        issue(refs[:n], refs[base], refs[base + 1])
        refs[-1][...] = jnp.zeros(_TOKEN, F32)

    params = dict(has_side_effects=_DATAFLOW)
    if handshake is not None:
        params["collective_id"] = handshake[0]
    outs = pl.pallas_call(
        body, name=name,
        in_specs=[HBM] * n + [ANY] * len(order), out_specs=[SEM, SEM] + [HBM] * n + [pl.BlockSpec(memory_space=pltpu.VMEM)],
        out_shape=[pltpu.SemaphoreType.DMA((n_sems,)), pltpu.SemaphoreType.DMA((n_sems,))]
        + [pltpu.HBM(a.shape, a.dtype) for a in arrays] + [_sds(_TOKEN, F32)],
        input_output_aliases={k: 2 + k for k in range(n)},
        compiler_params=pltpu.CompilerParams(**params),
    )(*[pltpu.with_memory_space_constraint(a, pltpu.HBM) for a in arrays], *order)
    return outs[0], outs[1], list(outs[2:2 + n]), outs[-1]


def _split_wait(name, started, waits, after):
    send_sems, recv_sems, arrays, _ = started
    n = len(arrays)

    def body(*refs):
        waits(refs[:n], refs[n], refs[n + 1])

    return pl.pallas_call(
        body, name=name,
        in_specs=[HBM] * n + [SEM, SEM, ANY], out_specs=[HBM] * n,
        out_shape=[pltpu.HBM(a.shape, a.dtype) for a in arrays],
        input_output_aliases={k: k for k in range(n)},
        compiler_params=pltpu.CompilerParams(has_side_effects=_DATAFLOW),
    )(*arrays, send_sems, recv_sems, after)


def _wait_both(src, dst, send_sem, recv_sem):
    x, y, c = _mesh_pos()
    cp = _remote(src, dst, send_sem, recv_sem, (x, y, c))
    cp.wait_send()
    cp.wait_recv()


def _pair_exchange_start(parts, tag, after):
    names = list(parts)
    n = len(names)
    lands = [lax.empty(_half_shape(parts[k].shape, _KIND[k]), parts[k].dtype) for k in names]

    def issue(refs, send_sems, recv_sems):
        x, y, c = _mesh_pos()
        for hc in range(2):
            @pl.when(c == hc)
            def _():
                for k in range(n):
                    _remote(_half_view(refs[k], _KIND[names[k]], 1 - hc), refs[n + k], send_sems.at[k], recv_sems.at[k],
                            (x, y, 1 - c)).start()

    return names, _split_start("grad_pair_exchange_start_" + tag, [parts[k] for k in names] + lands, n, issue, after,
                               (_COLLECTIVE["pair_" + tag], _sibling_peers))


def _pair_exchange_wait(pending, tag, after):
    names, started = pending
    n = len(names)

    def waits(refs, send_sems, recv_sems):
        for k in range(n):
            _wait_both(_half_view(refs[k], _KIND[names[k]], 0), refs[n + k], send_sems.at[k], recv_sems.at[k])

    outs = _split_wait("grad_pair_exchange_wait_" + tag, started, waits, after)
    return dict(zip(names, outs[:n])), dict(zip(names, outs[n:]))


def _half_blocks(shape, kind):
    if kind == "stack":
        _, k, w = shape
        tr = k // 2
        nb = 1
        return (N_CHIPS, nb), (1, tr, w), (lambda i, r, s: (i, r, 0)), (lambda i, r, s: (i, s[1] * nb + r, 0))
    k, w = shape
    if kind == "col":
        tr = 256
        nb = k // 2 // tr
        return (nb,), (tr, w), (lambda r, s: (r, 0)), (lambda r, s: (s[1] * nb + r, 0))
    tr = k // N_CHIPS
    return (N_CHIPS,), (tr, w // 2), (lambda r, s: (r, 0)), (lambda r, s: (r, s[1]))


def _pair_add(part, from_sibling, name, pos):
    kind = _KIND[name]
    grid, block, half_map, full_map = _half_blocks(part.shape, kind)

    def body(s_ref, p_ref, q_ref, o_ref):
        o_ref[...] = (p_ref[...].astype(F32) + q_ref[...].astype(F32)).astype(BF16)

    return pl.pallas_call(
        body, name="grad_pair_add_" + name,
        grid_spec=pltpu.PrefetchScalarGridSpec(
            num_scalar_prefetch=1, grid=grid,
            in_specs=[pl.BlockSpec(block, full_map), pl.BlockSpec(block, half_map)],
            out_specs=pl.BlockSpec(block, half_map)),
        out_shape=_sds(from_sibling.shape, BF16),
        compiler_params=_cp(("arbitrary",) * len(grid), 40),
    )(pos, *_hbm(part, from_sibling))


def _chip_exchange_start(sums, tag, after):
    names = list(sums)
    n = len(names)
    lands = [lax.empty((3,) + _part_shape(sums[k].shape, _KIND[k]), sums[k].dtype) for k in names]

    def issue(refs, send_sems, recv_sems):
        x, y, c = _mesh_pos()
        me = 2 * x + y
        for i in range(N_CHIPS):
            xi, yi = i // 2, i % 2
            j = jnp.where(xi != x, jnp.where(yi != y, 2, 0), 1)

            @pl.when(i != me)
            def _():
                for k in range(n):
                    _remote(_shard_view(refs[k], _KIND[names[k]], i), refs[n + k].at[j], send_sems.at[3 * k + j],
                            recv_sems.at[3 * k + j], (xi, yi, c)).start()

    return names, _split_start("grad_chip_exchange_start_" + tag, [sums[k] for k in names] + lands, 3 * n, issue, after,
                               (_COLLECTIVE["chip_" + tag], _chip_peers))


def _chip_exchange_wait(pending, tag, after):
    names, started = pending
    n = len(names)

    def waits(refs, send_sems, recv_sems):
        for k in range(n):
            for j in range(3):
                _wait_both(_shard_view(refs[k], _KIND[names[k]], 0), refs[n + k].at[j], send_sems.at[3 * k + j], recv_sems.at[3 * k + j])

    return dict(zip(names, _split_wait("grad_chip_exchange_wait_" + tag, started, waits, after)[n:]))


def _allgather_start(stacks, tag, after):
    names = list(stacks)

    def issue(refs, send_sems, recv_sems):
        x, y, c = _mesh_pos()
        me = 2 * x + y
        for k, st in enumerate(refs):
            hr = st.shape[1] // 2
            mine = st.at[me, pl.ds(c * hr, hr), :]
            for j, (cx, cy) in enumerate(_other_chips(x, y)):
                _remote(mine, mine, send_sems.at[3 * k + j], recv_sems.at[3 * k + j], (cx, cy, c)).start()

    return names, _split_start("allgather_start_" + tag, [stacks[k] for k in names], 3 * len(names), issue, after,
                               (_COLLECTIVE["gather_" + tag], _chip_peers))


def _allgather_wait(pending, tag, after):
    names, started = pending

    def waits(refs, send_sems, recv_sems):
        for k, st in enumerate(refs):
            slot = st.at[0, pl.ds(0, st.shape[1] // 2), :]
            for j in range(3):
                _wait_both(slot, slot, send_sems.at[3 * k + j], recv_sems.at[3 * k + j])

    return dict(zip(names, _split_wait("allgather_wait_" + tag, started, waits, after)))


def _allgather_forward(stacks, tag):
    names = list(stacks)
    n = len(names)

    def body(*refs):
        ins, outs = refs[:n], refs[n:2 * n]
        send_sems, recv_sems = refs[2 * n:]
        x, y, c = _mesh_pos()
        _handshake(_sibling_peers(x, y, c))
        copies = []
        for k in range(n):
            hr = ins[k].shape[1] // 2
            for j, (cx, cy) in enumerate(_other_chips(x, y)):
                chip = 2 * cx + cy
                copies.append(_remote(ins[k].at[chip, pl.ds(c * hr, hr), :], outs[k].at[chip, pl.ds(c * hr, hr), :],
                                      send_sems.at[3 * k + j], recv_sems.at[3 * k + j], (x, y, 1 - c)))
        for cp in copies:
            cp.start()
        for cp in copies:
            cp.wait()

    arrays = [stacks[k] for k in names]
    outs = pl.pallas_call(
        body, name="allgather_forward_" + tag, in_specs=[HBM] * n, out_specs=[HBM] * n,
        input_output_aliases={k: k for k in range(n)},
        out_shape=[_sds(a.shape, a.dtype) for a in arrays],
        scratch_shapes=[pltpu.SemaphoreType.DMA((3 * n,)), pltpu.SemaphoreType.DMA((3 * n,))],
        compiler_params=pltpu.CompilerParams(collective_id=_COLLECTIVE["forward_" + tag]),
    )(*arrays)
    return dict(zip(names, outs))


def _owner_sum(part, from_sibling, from_chips, name, pos, shard_shape):
    kind = _KIND[name]
    _, pk, pw = from_chips.shape
    if kind == "row":
        tr, nb = pk, 1
        p_spec = pl.BlockSpec((tr, pw), lambda r, s: (s[0], s[1]))
        q_spec = pl.BlockSpec((tr, pw), lambda r, s: (s[0], 0))
        o_spec = pl.BlockSpec((tr, pw), lambda r, s: (0, s[1]))
    else:
        tr = 256
        nb = pk // tr
        if kind == "stack":
            p_spec = pl.BlockSpec((None, tr, pw), lambda r, s: (s[0], s[1] * nb + r, 0))
            q_spec = pl.BlockSpec((None, tr, pw), lambda r, s: (s[0], r, 0))
        else:
            p_spec = pl.BlockSpec((tr, pw), lambda r, s: (s[1] * nb + r, s[0]))
            q_spec = pl.BlockSpec((tr, pw), lambda r, s: (r, s[0]))
        o_spec = pl.BlockSpec((tr, pw), lambda r, s: (s[1] * nb + r, 0))

    def body(s_ref, p_ref, q_ref, r_ref, o_ref):
        acc = p_ref[...].astype(F32) + q_ref[...].astype(F32)
        for j in range(3):
            acc = acc + r_ref[j].astype(F32)
        o_ref[...] = acc

    return pl.pallas_call(
        body, name="grad_owner_sum_" + name,
        grid_spec=pltpu.PrefetchScalarGridSpec(
            num_scalar_prefetch=1, grid=(nb,),
            in_specs=[p_spec, q_spec, pl.BlockSpec((3, tr, pw), lambda r, s: (0, r, 0))],
            out_specs=o_spec),
        out_shape=_sds(shard_shape, F32),
        compiler_params=_cp(("arbitrary",), 32),
    )(pos, *_hbm(part, from_sibling, from_chips))


def _pair_share_start(shards, tag, after):
    names = list(shards)

    def issue(refs, send_sems, recv_sems):
        x, y, c = _mesh_pos()
        for hc in range(2):
            @pl.when(c == hc)
            def _():
                for k, g in enumerate(refs):
                    mine = _region_view(g, _KIND[names[k]], hc)
                    _remote(mine, mine, send_sems.at[k], recv_sems.at[k], (x, y, 1 - c)).start()

    return names, _split_start("grad_pair_share_start_" + tag, [shards[k] for k in names], len(names), issue, after,
                               (_COLLECTIVE["share_" + tag], _sibling_peers))


def _pair_share_wait(pending, tag, after):
    names, started = pending

    def waits(refs, send_sems, recv_sems):
        for k, g in enumerate(refs):
            region = _region_view(g, _KIND[names[k]], 0)
            _wait_both(region, region, send_sems.at[k], recv_sems.at[k])

    return dict(zip(names, _split_wait("grad_pair_share_wait_" + tag, started, waits, after)))


def _small_exchange_start(slots, after):
    def issue(refs, send_sems, recv_sems):
        x, y, c = _mesh_pos()
        mine = refs[0].at[4 * x + 2 * y + c]
        k = 0
        for px in range(2):
            for py in range(2):
                for pc in range(2):
                    if px + py + pc:
                        peer = (1 - x if px else x, 1 - y if py else y, 1 - c if pc else c)
                        _remote(mine, mine, send_sems.at[k], recv_sems.at[k], peer).start()
                        k += 1

    return _split_start("small_exchange_start", [slots], N_DEV - 1, issue, after)


def _small_exchange_wait(started, after):
    def waits(refs, send_sems, recv_sems):
        slot = refs[0].at[0]
        for k in range(N_DEV - 1):
            _wait_both(slot, slot, send_sems.at[k], recv_sems.at[k])

    return _split_wait("small_exchange_wait", started, waits, after)[0]


def _adam_math(w, g, m, v):
    m = ADAM_B1 * m + (1.0 - ADAM_B1) * g
    v = ADAM_B2 * v + (1.0 - ADAM_B2) * (g * g)
    m_hat = m / (1.0 - ADAM_B1 ** ADAM_STEP)
    v_hat = v / (1.0 - ADAM_B2 ** ADAM_STEP)
    delta = -ADAM_LR * (m_hat / (jnp.sqrt(v_hat) + ADAM_EPS) + ADAM_WD * w)
    return delta, m, v


def _adamw(w, g, m, v, name):
    rows, cols = w.shape
    fits = [t for t in range(SUBLANES, rows, SUBLANES) if rows % t == 0 and t * cols * 4 <= (3 << 19)]
    tr = max(fits) if fits else rows

    def body(w_ref, g_ref, m_ref, v_ref, d_ref, nm_ref, nv_ref, go_ref):
        g = g_ref[...]
        d, nm, nv = _adam_math(w_ref[...], g, m_ref[...], v_ref[...])
        d_ref[...] = d
        nm_ref[...] = nm
        nv_ref[...] = nv
        go_ref[...] = g

    spec = pl.BlockSpec((tr, cols), lambda i: (i, 0))
    return pl.pallas_call(
        body, name=name, grid=(rows // tr,), in_specs=[spec] * 4, out_specs=[spec] * 4,
        out_shape=[_sds(w.shape, F32)] * 4, compiler_params=_cp(("arbitrary",)),
    )(*_hbm(w, g, m, v))


def _small_sum_adamw(gathered, w, m, v):
    names = _SMALL_IN_CALL
    n = len(names)

    def body(*refs):
        a_ref = refs[0]
        w_refs, m_refs, v_refs = refs[1:1 + n], refs[1 + n:1 + 2 * n], refs[1 + 2 * n:1 + 3 * n]
        sum_ref = refs[1 + 3 * n]
        outs = refs[2 + 3 * n:]
        g = a_ref[0]
        for k in range(1, N_DEV):
            g = g + a_ref[k]
        sum_ref[...] = g
        for i, name in enumerate(names):
            r0, nr, nc = _SMALL_AT[name]
            gp = g[r0:r0 + nr, 0:nc]
            d, nm, nv = _adam_math(w_refs[i][...], gp, m_refs[i][...], v_refs[i][...])
            for k, val in enumerate((gp, d, nm, nv)):
                outs[4 * i + k][...] = val

    shapes = [w[k].shape for k in names]
    res = pl.pallas_call(
        body, name="small_sum_adamw",
        out_shape=[_sds((SMALL_ROWS, D_MODEL), F32)] + [_sds(s, F32) for s in shapes for _ in range(4)],
    )(gathered, *[w[k] for k in names], *[m[k] for k in names], *[v[k] for k in names])
    return res[0], {k: tuple(res[1 + 4 * i:5 + 4 * i]) for i, k in enumerate(names)}


_NAMES = ("g_mix", "w_in", "g_sgu", "w_s", "b_s", "sinks", "rel_bias", "w_pa", "w_pb", "w_out",
          "g_ffn", "w_up", "w_conv", "b_conv", "w_down", "g_final")

def kernel(x, g_mix, w_in, g_sgu, w_s, b_s, sinks, rel_bias, w_pa, w_pb, w_out, g_ffn, w_up, w_conv, b_conv, w_down, g_final, loss_target, m_g_mix, m_w_in, m_g_sgu, m_w_s, m_b_s, m_sinks, m_rel_bias, m_w_pa, m_w_pb, m_w_out, m_g_ffn, m_w_up, m_w_conv, m_b_conv, m_w_down, m_g_final, v_g_mix, v_w_in, v_g_sgu, v_w_s, v_b_s, v_sinks, v_rel_bias, v_w_pa, v_w_pb, v_w_out, v_g_ffn, v_w_up, v_w_conv, v_b_conv, v_w_down, v_g_final):
    w = dict(g_mix=g_mix, w_in=w_in, g_sgu=g_sgu, w_s=w_s, b_s=b_s, sinks=sinks, rel_bias=rel_bias, w_pa=w_pa, w_pb=w_pb,
             w_out=w_out, g_ffn=g_ffn, w_up=w_up, w_conv=w_conv, b_conv=b_conv, w_down=w_down, g_final=g_final)
    m = dict(g_mix=m_g_mix, w_in=m_w_in, g_sgu=m_g_sgu, w_s=m_w_s, b_s=m_b_s, sinks=m_sinks, rel_bias=m_rel_bias, w_pa=m_w_pa,
             w_pb=m_w_pb, w_out=m_w_out, g_ffn=m_g_ffn, w_up=m_w_up, w_conv=m_w_conv, b_conv=m_b_conv, w_down=m_w_down,
             g_final=m_g_final)
    v = dict(g_mix=v_g_mix, w_in=v_w_in, g_sgu=v_g_sgu, w_s=v_w_s, b_s=v_b_s, sinks=v_sinks, rel_bias=v_rel_bias, w_pa=v_w_pa,
             w_pb=v_w_pb, w_out=v_w_out, g_ffn=v_g_ffn, w_up=v_w_up, w_conv=v_w_conv, b_conv=v_b_conv, w_down=v_w_down,
             g_final=v_g_final)
    xi, yi, ci = _mesh_pos()
    me = 2 * xi + yi

    shard = {n: w[n][0] for n in _BIG}
    shard_shapes = {n: shard[n].shape for n in _BIG}
    wc_shard = w["w_conv"][0]
    wc_pad = jnp.pad(wc_shard, ((0, 5), (0, 0)))
    own = {n: _own_slot(shard[n].astype(BF16), N_CHIPS, me) for n in _BIG}
    stacks, wc_all = _allgather_weights({"w_in": own["w_in"]}, _own_slot(wc_pad, N_CHIPS, me))
    proj_gather = _allgather_start({n: own[n] for n in _MIXER[1:]}, "proj", stacks["w_in"])
    ffn_gather = _allgather_start({n: own[n] for n in _FFN}, "ffn", proj_gather[1][-1])
    w_conv_full = jnp.concatenate([wc_all[i, :3] for i in range(N_CHIPS)], axis=1)
    w_in_full = _unstack_w_in(stacks["w_in"])
    pos = jnp.stack([me, ci])

    def proj_weights(done):
        st = _allgather_forward(_allgather_wait(proj_gather, "proj", done), "proj")
        return st["w_pa"], st["w_pb"], st["w_out"].reshape(D_MODEL, D_MODEL)

    def ffn_weights(done):
        st = _allgather_forward(_allgather_wait(ffn_gather, "ffn", done), "ffn")
        return st["w_up"], st["w_down"].reshape(D_FF, D_MODEL)

    groups = {}

    def stage1(group, parts):
        groups[group] = dict(parts=parts, pair=_pair_exchange_start(parts, group, None))
        return groups[group]["pair"][1][-1]

    def stage2(group, after, order_after):
        g = groups[group]
        g["parts"], g["sib"] = _pair_exchange_wait(g["pair"], group, after)
        g["chip"] = _chip_exchange_start({n: _pair_add(g["parts"][n], g["sib"][n], n, pos) for n in g["parts"]}, group, order_after)
        return g["chip"][1][-1]

    def stage3(group, after, order_after):
        g = groups[group]
        got = _chip_exchange_wait(g["chip"], group, after)
        g["share"] = _pair_share_start(
            {n: _owner_sum(g["parts"][n], g["sib"][n], got[n], n, pos, shard_shapes[n]) for n in g["parts"]}, group, order_after)
        return g["share"][1][-1]

    grads, deltas, new_m, new_v = {}, {}, {}, {}

    def stage4(group, after):
        g_shard = _pair_share_wait(groups[group]["share"], group, after)
        last = None
        for n in g_shard:
            g = _tie(g_shard[n], last)
            if n == "w_in":
                d, nm, nv, gt = _adamw(shard[n].T, g.T, m[n][0].T, v[n][0].T, "adamw_" + n)
                grads[n], deltas[n], new_m[n], new_v[n] = gt.T[None], d.T[None], nm.T[None], nv.T[None]
            else:
                d, nm, nv, go = _adamw(shard[n], g, m[n][0], v[n][0], "adamw_" + n)
                grads[n], deltas[n], new_m[n], new_v[n] = go[None], d[None], nm[None], nv[None]
            last = nv
        return last

    def on_grads(group, parts):
        token = stage1(group, parts)
        some = next(iter(parts.values()))
        if group == "proj":
            token = stage2("ffn", some, token)
        if group == "in":
            token = stage2("proj", some, token)
            token = stage3("ffn", some, token)
            token = stage2("in", token, token)
        return token

    loss, grad_x, small, big = _local_step(
        x, loss_target, w["g_mix"], w["g_sgu"], w["w_s"][0], w["b_s"][0], w["sinks"], w["rel_bias"], w["g_ffn"],
        w["b_conv"], w["g_final"], w_in_full, w_conv_full, proj_weights, ffn_weights, on_grads, ffn_gather[1][-1])

    small["loss"] = loss
    small_gather = _small_exchange_start(_own_slot(_pack_small(small), N_DEV, 2 * me + ci), grad_x)
    token = stage3("proj", grad_x, small_gather[-1])
    done = stage4("ffn", token)
    done = stage4("proj", done)
    token = stage3("in", done, None)
    all_small = _small_exchange_wait(small_gather, token)
    two_d = {n: (lambda a, n=n: a.reshape(_SMALL_AT[n][1:])) for n in _SMALL_IN_CALL}
    s_sum, s_out = _small_sum_adamw(all_small, *[{n: two_d[n](p[n]) for n in _SMALL_IN_CALL} for p in (w, m, v)])
    stage4("in", all_small)
    for n in _SMALL_IN_CALL:
        grads[n], deltas[n], new_m[n], new_v[n] = [a.reshape(w[n].shape) for a in s_out[n]]

    def rows(n):
        r0, nr, _ = _SMALL_AT[n]
        return s_sum[r0:r0 + nr]

    wcols = wc_shard.shape[1]
    g_wc = lax.dynamic_slice(_unwide(rows("w_conv"), 3), (0, me * wcols), (3, wcols))
    d, nm, nv, _ = _adamw(wc_shard, g_wc, m["w_conv"][0], v["w_conv"][0], "adamw_w_conv")
    grads["w_conv"], deltas["w_conv"], new_m["w_conv"], new_v["w_conv"] = g_wc[None], d[None], nm[None], nv[None]
    d, nm, nv, go = _adamw(w["b_conv"], _unwide(rows("b_conv"), 1), m["b_conv"], v["b_conv"], "adamw_b_conv")
    grads["b_conv"], deltas["b_conv"], new_m["b_conv"], new_v["b_conv"] = go, d, nm, nv
    g_rb = rows("rel_bias")[:, :N_BUCKETS * N_HEADS].reshape(N_BUCKETS, N_HEADS)
    d, nm, nv, go = _adamw(w["rel_bias"], g_rb, m["rel_bias"], v["rel_bias"], "adamw_rel_bias")
    grads["rel_bias"], deltas["rel_bias"], new_m["rel_bias"], new_v["rel_bias"] = go, d, nm, nv
    flat_s = (A_GROUPS * CHUNK, CHUNK)
    d, nm, nv, go = _adamw(w["w_s"].reshape(flat_s), rows("w_s").reshape(flat_s), m["w_s"].reshape(flat_s),
                           v["w_s"].reshape(flat_s), "adamw_w_s")
    grads["w_s"], deltas["w_s"], new_m["w_s"], new_v["w_s"] = [a.reshape(w["w_s"].shape) for a in (go, d, nm, nv)]

    return (s_sum[0, 0], grad_x, *[grads[n] for n in _NAMES], *[deltas[n] for n in _NAMES],
            *[new_m[n] for n in _NAMES], *[new_v[n] for n in _NAMES])
```

```python
import functools

import numpy as np
import jax
import jax.numpy as jnp
from jax import lax
from jax.experimental import pallas as pl
from jax.experimental.pallas import tpu as pltpu

F32 = jnp.float32
BF16 = jnp.bfloat16

D_MODEL = 1024
CHUNK = 128
A_GROUPS = 4
A_WIDTH = 512
N_HEADS = 8
HEAD_DIM = 64
Q_DIM = 512
KV_DIM = 128
N_BUCKETS = 32
MAX_DISTANCE = 128
D_FF = 2816
EPS = 1e-6
NEG_INF = -1e30
G_DIM = 2 * D_MODEL
A_DIM = 2 * A_WIDTH
B_DIM = Q_DIM + 2 * KV_DIM
LANES = 128
SUBLANES = 8
ROW_TILE = 512
WIDE_ROW_TILE = 256
COL_CHUNK = 512
GRAD_ROW_TILE = 512
STREAM_ROWS = 128
BF16_ROWS = 16
N_CHIPS = 4
N_DEV = 8

ADAM_LR = 0.001
ADAM_B1 = 0.9
ADAM_B2 = 0.999
ADAM_EPS = 1e-08
ADAM_WD = 0.01
ADAM_STEP = 10

MESH = pl.DeviceIdType.MESH
_GELU_C = 0.7978845608028654
_GELU_A = 0.044715


def _cp(sem=None, vmem_mb=None):
    kw = {}
    if sem is not None:
        kw["dimension_semantics"] = sem
    if vmem_mb is not None:
        kw["vmem_limit_bytes"] = vmem_mb << 20
    return pltpu.CompilerParams(**kw)


def _dot(a, b):
    return jnp.dot(a, b, preferred_element_type=F32)


def _dot_nt(a, b):
    return lax.dot_general(a, b, (((1,), (1,)), ((), ())), preferred_element_type=F32)


def _dot_tn(a, b):
    return lax.dot_general(a, b, (((0,), (0,)), ((), ())), preferred_element_type=F32)


def _rms_r(x):
    return lax.rsqrt(jnp.mean(x * x, axis=-1, keepdims=True) + EPS)


def _rms_bwd(dh, n, r, g):
    dn = dh * g
    return r * (dn - n * jnp.mean(dn * n, axis=-1, keepdims=True))


def _gelu(x):
    t = jnp.tanh(_GELU_C * (x + _GELU_A * (x * x * x)))
    return 0.5 * x * (1.0 + t), t


def _gelu_grad(x, t):
    return 0.5 * (1.0 + t) + 0.5 * x * (1.0 - t * t) * (_GELU_C * (1.0 + 3.0 * _GELU_A * x * x))


def _sigmoid(x):
    return 1.0 / (1.0 + jnp.exp(-x))


def _tie(x, dep):
    return x if dep is None else lax.optimization_barrier((x, dep))[0]


def _row(tm, w):
    return pl.BlockSpec((tm, w), lambda i: (i, 0))


def _full(shape):
    nd = len(shape)
    return pl.BlockSpec(tuple(shape), lambda *_: (0,) * nd)


def _resident(shape):
    nd = len(shape)
    return pl.BlockSpec(tuple(shape), lambda *_: (0,) * nd, pipeline_mode=pl.Buffered(1))


def _sds(shape, dtype):
    return pltpu.HBM(tuple(shape), dtype)


def _hbm(*arrays):
    return [pltpu.with_memory_space_constraint(a, pltpu.HBM) for a in arrays]


HBM = pl.BlockSpec(memory_space=pltpu.HBM)
ANY = pl.BlockSpec(memory_space=pl.ANY)
SEM = pl.BlockSpec(memory_space=pltpu.SEMAPHORE)


def _band_buckets():
    i = np.arange(CHUNK)[:, None]
    j = np.arange(2 * CHUNK)[None, :]
    dist = i + CHUNK - j
    valid = (dist >= 0) & (dist < CHUNK)
    d = np.clip(dist, 0, None)
    max_exact = N_BUCKETS // 2
    large = max_exact + (np.log(np.maximum(d, 1) / max_exact) / np.log(MAX_DISTANCE / max_exact)
                         * (N_BUCKETS - max_exact)).astype(np.int32)
    large = np.minimum(large, N_BUCKETS - 1)
    buckets = np.where(d < max_exact, d, large).astype(np.int32)
    return np.where(valid, buckets, -1).astype(np.int32)


_A_COLS = slice(0, A_DIM)
_B_COLS = slice(A_DIM, A_DIM + B_DIM)
_G_COLS = slice(A_DIM + B_DIM, A_DIM + B_DIM + G_DIM)


def _inproj(x2, g_mix, w_in, g_sgu, w_s, b_st, tm, after=None):
    T = x2.shape[0]
    order = [] if after is None else [after]

    def body(*refs):
        x_ref, g_ref, w_ref, gs_ref, ws_ref, bs_ref = refs[:6]
        pg_ref, pa_ref, pb_ref, h_ref, ya_ref = refs[6 + len(order):]
        x = x_ref[...]
        h = (x * _rms_r(x) * g_ref[...]).astype(BF16)
        h_ref[...] = h
        pa = _dot(h, w_ref[:, _A_COLS]).astype(BF16)
        pa_ref[...] = pa
        pb_ref[...] = _dot(h, w_ref[:, _B_COLS]).astype(BF16)
        pg_ref[...] = _dot(h, w_ref[:, _G_COLS]).astype(BF16)
        _sgu_apply(pa.astype(F32), gs_ref[...], ws_ref, bs_ref, ya_ref)

    return pl.pallas_call(
        body, name="inproj", grid=(T // tm,),
        in_specs=[_row(tm, D_MODEL), _full(g_mix.shape), _resident(w_in.shape), _full(g_sgu.shape), _full(w_s.shape),
                  _full(b_st.shape)] + [ANY] * len(order),
        out_specs=[_row(tm, G_DIM), _row(tm, A_DIM), _row(tm, B_DIM), _row(tm, D_MODEL), _row(tm, A_WIDTH)],
        out_shape=[_sds((T, G_DIM), BF16), _sds((T, A_DIM), BF16), _sds((T, B_DIM), BF16), _sds((T, D_MODEL), BF16),
                   _sds((T, A_WIDTH), BF16)],
        compiler_params=_cp(("arbitrary",), 48),
    )(*_hbm(x2, g_mix, w_in, g_sgu, w_s, b_st), *order)


def _sgu_parts(p, g):
    pu = p[:, :A_WIDTH]
    pv = p[:, A_WIDTH:]
    u, tu = _gelu(pu)
    vv, tv = _gelu(pv)
    rv = _rms_r(vv)
    vn = (vv * rv * g).astype(BF16)
    return pu, pv, u, tu, vv, tv, rv, vn


def _tril():
    r = lax.broadcasted_iota(jnp.int32, (CHUNK, CHUNK), 0)
    c = lax.broadcasted_iota(jnp.int32, (CHUNK, CHUNK), 1)
    return r >= c


def _sgu_apply(p, g, ws_ref, bs_ref, y_ref):
    tril = _tril()
    _, _, u, _, _, _, _, vn = _sgu_parts(p, g)
    for gi in range(A_GROUPS):
        wm = jnp.where(tril, ws_ref[gi], 0.0).astype(BF16)
        bcol = bs_ref[:, gi:gi + 1]
        cs = slice(gi * CHUNK, (gi + 1) * CHUNK)
        for c in range(p.shape[0] // CHUNK):
            rs = slice(c * CHUNK, (c + 1) * CHUNK)
            s = _dot(wm, vn[rs, cs]) + bcol
            y_ref[rs, cs] = (u[rs, cs] * s).astype(BF16)


HEAD_ROWS = N_HEADS * CHUNK


def _head_rows(h):
    return slice(h * CHUNK, (h + 1) * CHUNK)


def _attn_setup(bias_scr, sink_scr, kvar_scr, qkv_ref, bk_ref, rel_ref, sink_ref):
    @pl.when(pl.program_id(0) == 0)
    def _():
        bk = bk_ref[...]
        for h in range(N_HEADS):
            acc = jnp.full((CHUNK, 2 * CHUNK), NEG_INF, F32)
            for b in range(N_BUCKETS):
                acc = jnp.where(bk == b, rel_ref[b, h], acc)
            bias_scr[_head_rows(h), :] = acc
            sink_scr[_head_rows(h), :] = jnp.full((CHUNK, LANES), sink_ref[0, h], F32)

    seq = qkv_ref.shape[0]
    rows_per = 2 * CHUNK
    for is_v in range(2):
        c0 = Q_DIM + is_v * KV_DIM
        for r in range(seq // rows_per):
            rs = slice(r * rows_per, (r + 1) * rows_per)
            a = qkv_ref[rs, c0:c0 + KV_DIM].astype(F32)
            lane = lax.broadcasted_iota(jnp.int32, a.shape, 1)
            lo = jnp.where(lane < HEAD_DIM, a, 0.0)
            hi = jnp.where(lane >= HEAD_DIM, a, 0.0)
            kvar_scr[4 * is_v + 0, rs, :] = lo.astype(BF16)
            kvar_scr[4 * is_v + 1, rs, :] = pltpu.roll(lo, HEAD_DIM, 1).astype(BF16)
            kvar_scr[4 * is_v + 2, rs, :] = pltpu.roll(hi, HEAD_DIM, 1).astype(BF16)
            kvar_scr[4 * is_v + 3, rs, :] = hi.astype(BF16)


def _rowsum(a, ones):
    hi = a.astype(BF16)
    lo = (a - hi.astype(F32)).astype(BF16)
    return _dot(hi, ones) + _dot(lo, ones)


def _both(a):
    return jnp.concatenate([a, a], axis=1)


def _attn_probs(qkv_ref, r0, n, kv, bias_scr, sink_scr, ones):
    s = jnp.concatenate([_dot_nt(qkv_ref[pl.ds(r0, CHUNK), (h // 2) * LANES:(h // 2 + 1) * LANES], kv[h // 4][h % 2])
                         for h in range(N_HEADS)], axis=0)
    s = s * (HEAD_DIM ** -0.5) + bias_scr[...]
    col = lax.broadcasted_iota(jnp.int32, s.shape, 1)
    s = jnp.where((col < CHUNK) & (n == 0), NEG_INF, s)
    sink = sink_scr[...]
    m = jnp.maximum(jnp.max(s, axis=-1, keepdims=True), sink)
    p = jnp.exp(s - _both(m))
    es = jnp.exp(sink - m)
    inv = 1.0 / (_dot(p.astype(BF16), ones) + es)
    return p * _both(inv), es * inv


def _attn_block_inputs(kvar_scr, n):
    r0 = pl.multiple_of(n * CHUNK, CHUNK)
    rp = pl.multiple_of(jnp.maximum(n - 1, 0) * CHUNK, CHUNK)

    def both(idx):
        return jnp.concatenate([kvar_scr[idx, pl.ds(rp, CHUNK), :], kvar_scr[idx, pl.ds(r0, CHUNK), :]], axis=0)

    kv = ((both(0), both(1)), (both(2), both(3)))
    vv = ((both(4), both(5)), (both(6), both(7)))
    return r0, kv, vv


def _attn_fwd(proj_b, sinks, rel_bias, n_seq, seq):
    nb = seq // CHUNK
    bk = jnp.asarray(_band_buckets())

    def body(qkv_ref, bk_ref, rel_ref, sink_ref, o_ref, bias_scr, sink_scr, kvar_scr):
        _attn_setup(bias_scr, sink_scr, kvar_scr, qkv_ref, bk_ref, rel_ref, sink_ref)
        ones = jnp.ones((2 * CHUNK, LANES), BF16)

        def blk(n, carry):
            r0, kv, vv = _attn_block_inputs(kvar_scr, n)
            prob, _ = _attn_probs(qkv_ref, r0, n, kv, bias_scr, sink_scr, ones)
            pb = prob.astype(BF16)
            for pr in range(N_HEADS // 2):
                acc = _dot(pb[_head_rows(2 * pr)], vv[pr // 2][0]) + _dot(pb[_head_rows(2 * pr + 1)], vv[pr // 2][1])
                o_ref[pl.ds(r0, CHUNK), pr * LANES:(pr + 1) * LANES] = acc.astype(BF16)
            return carry

        lax.fori_loop(0, nb, blk, 0)

    smem = pl.BlockSpec(memory_space=pltpu.SMEM)
    return pl.pallas_call(
        body, name="attn_fwd", grid=(n_seq,),
        in_specs=[_row(seq, B_DIM), _full(bk.shape), smem, smem],
        out_specs=_row(seq, Q_DIM), out_shape=_sds((n_seq * seq, Q_DIM), BF16),
        scratch_shapes=[pltpu.VMEM((HEAD_ROWS, 2 * CHUNK), F32), pltpu.VMEM((HEAD_ROWS, LANES), F32),
                        pltpu.VMEM((8, seq, KV_DIM), BF16)],
        compiler_params=_cp(("arbitrary",), 40),
    )(*_hbm(proj_b, bk), rel_bias, sinks)


def _dot_stacked(a, w_ref):
    return jnp.concatenate([_dot(a, w_ref[i]) for i in range(N_CHIPS)], axis=1)


def _dot_nt_stacked(a, w_ref):
    w = w_ref.shape[2]
    acc = _dot_nt(a[:, :w], w_ref[0])
    for i in range(1, N_CHIPS):
        acc = acc + _dot_nt(a[:, i * w:(i + 1) * w], w_ref[i])
    return acc


def _merge_fwd(x2, y_a, y_b, proj_g, w_pa, w_pb, w_out, tm):
    T = x2.shape[0]

    def body(x_ref, ya_ref, yb_ref, g_ref, wpa_ref, wpb_ref, wo_ref, x1_ref, mg_ref):
        g = g_ref[...].astype(F32)
        pa = _dot_stacked(ya_ref[...], wpa_ref)
        pb = _dot_stacked(yb_ref[...], wpb_ref)
        merged = (_sigmoid(g[:, :D_MODEL]) * pa + _sigmoid(g[:, D_MODEL:]) * pb).astype(BF16)
        mg_ref[...] = merged
        x1_ref[...] = x_ref[...] + _dot(merged, wo_ref[...])

    return pl.pallas_call(
        body, name="merge_fwd", grid=(T // tm,),
        in_specs=[_row(tm, D_MODEL), _row(tm, A_WIDTH), _row(tm, Q_DIM), _row(tm, G_DIM),
                  _resident(w_pa.shape), _resident(w_pb.shape), _resident(w_out.shape)],
        out_specs=[_row(tm, D_MODEL), _row(tm, D_MODEL)],
        out_shape=[_sds((T, D_MODEL), F32), _sds((T, D_MODEL), BF16)],
        compiler_params=_cp(("arbitrary",), 40),
    )(*_hbm(x2, y_a, y_b, proj_g, w_pa, w_pb, w_out))


def _upproj(x1, g_ffn, w_up, w_conv, b_conv, tm, seq):
    T = x1.shape[0]
    cw = w_up.shape[2]
    tiles_per_seq = seq // tm

    def body(x_ref, g_ref, w_ref, wc_ref, bc_ref, u_ref, h_ref, gate_ref, val_ref, tail_scr):
        at_start = (pl.program_id(0) % tiles_per_seq) == 0
        x = x_ref[...]
        h = (x * _rms_r(x) * g_ref[...]).astype(BF16)
        h_ref[...] = h
        for i in range(N_CHIPS):
            cs = slice(i * cw, (i + 1) * cw)
            u = _dot(h, w_ref[i])
            u_ref[:, cs] = u.astype(BF16)
            hl = jnp.where(at_start, 0.0, tail_scr[SUBLANES - 2:SUBLANES, cs])
            tail_scr[:, cs] = u[tm - SUBLANES:]
            up = _conv_out((u, _shift_down(u, hl, 1), _shift_down(u, hl, 2)), wc_ref[:, cs], bc_ref[:, cs])
            out_ref = gate_ref if i < N_CHIPS // 2 else val_ref
            out_ref[:, (i % 2) * cw:(i % 2 + 1) * cw] = up.astype(BF16)

    return pl.pallas_call(
        body, name="upproj", grid=(T // tm,),
        in_specs=[_row(tm, D_MODEL), _full(g_ffn.shape), _resident(w_up.shape), _full(w_conv.shape), _full(b_conv.shape)],
        out_specs=[_row(tm, 2 * D_FF), _row(tm, D_MODEL), _row(tm, D_FF), _row(tm, D_FF)],
        out_shape=[_sds((T, 2 * D_FF), BF16), _sds((T, D_MODEL), BF16), _sds((T, D_FF), BF16), _sds((T, D_FF), BF16)],
        scratch_shapes=[pltpu.VMEM((SUBLANES, 2 * D_FF), F32)],
        compiler_params=_cp(("arbitrary",), 56),
    )(*_hbm(x1, g_ffn, w_up, w_conv, b_conv))


def _shift_down(u, halo, k):
    rolled = pltpu.roll(u, k, 0)
    head = rolled[:SUBLANES]
    row = lax.broadcasted_iota(jnp.int32, head.shape, 0)
    if k == 1:
        head = jnp.where(row == 0, halo[1:2], head)
    else:
        head = jnp.where(row == 0, halo[0:1], jnp.where(row == 1, halo[1:2], head))
    return jnp.concatenate([head, rolled[SUBLANES:]], axis=0)


def _shift_up(d, halo, k):
    tm = d.shape[0]
    rolled = pltpu.roll(d, tm - k, 0)
    tail = rolled[tm - SUBLANES:]
    row = lax.broadcasted_iota(jnp.int32, tail.shape, 0)
    if k == 1:
        tail = jnp.where(row == SUBLANES - 1, halo[0:1], tail)
    else:
        tail = jnp.where(row == SUBLANES - 2, halo[0:1], jnp.where(row == SUBLANES - 1, halo[1:2], tail))
    return jnp.concatenate([rolled[:tm - SUBLANES], tail], axis=0)


def _conv_out(taps, wc, bc):
    u, u1, u2 = taps
    return wc[0:1] * u2 + wc[1:2] * u1 + wc[2:3] * u + bc


def _ffn_down_loss(gate, val, x1, target, w_down, g_final, tm):
    T = x1.shape[0]
    half = D_FF // 2

    def body(gt_ref, vl_ref, x1_ref, t_ref, wd_ref, g_ref, dx2_ref, loss_ref, gg_ref):
        i = pl.program_id(0)
        acc = jnp.zeros((tm, D_MODEL), F32)
        for j in range(2):
            gc = slice(j * half, (j + 1) * half)
            gate = gt_ref[:, gc].astype(F32)
            act = (gate * _sigmoid(gate) * vl_ref[:, gc].astype(F32)).astype(BF16)
            acc = acc + _dot(act, wd_ref[gc, :])
        x2 = x1_ref[...] + acc
        r = _rms_r(x2)
        n = x2 * r
        g = g_ref[...]
        diff = n * g - t_ref[...]
        dy = diff * (1.0 / D_MODEL)
        dx2_ref[...] = _rms_bwd(dy, n, r, g)

        @pl.when(i == 0)
        def _():
            loss_ref[...] = jnp.zeros_like(loss_ref)
            gg_ref[...] = jnp.zeros_like(gg_ref)

        loss_ref[...] += 0.5 * jnp.sum(jnp.mean(diff * diff, axis=-1, keepdims=True), axis=0, keepdims=True)
        gg_ref[...] += jnp.sum(dy * n, axis=0, keepdims=True)

    return pl.pallas_call(
        body, name="ffn_down_loss", grid=(T // tm,),
        in_specs=[_row(tm, D_FF), _row(tm, D_FF), _row(tm, D_MODEL), _row(tm, D_MODEL),
                  _resident(w_down.shape), _full(g_final.shape)],
        out_specs=[_row(tm, D_MODEL), _full((1, 1)), _full((1, D_MODEL))],
        out_shape=[_sds((T, D_MODEL), F32), _sds((1, 1), F32), _sds((1, D_MODEL), F32)],
        compiler_params=_cp(("arbitrary",), 48),
    )(*_hbm(gate, val, x1, target, w_down, g_final))


def _ffn_bwd_act(gate, val, dx2, w_down, tm):
    T = dx2.shape[0]
    half = D_FF // 2
    nt = T // tm

    def body(g_ref, v_ref, dx_ref, wd_ref, dg_ref, dv_ref, gwd_out, gbg_ref, gbv_ref, gwd_ref):
        i = pl.program_id(1)

        @pl.when(i == 0)
        def _():
            for r in (gwd_ref, gbg_ref, gbv_ref):
                r[...] = jnp.zeros_like(r)

        dx = dx_ref[...].astype(BF16)
        for c0 in range(0, half, COL_CHUNK):
            cs = slice(c0, min(c0 + COL_CHUNK, half))
            gate = g_ref[:, cs].astype(F32)
            val = v_ref[:, cs].astype(F32)
            sg = _sigmoid(gate)
            silu = gate * sg
            d_act = _dot_nt(dx, wd_ref[cs, :])
            d_val = d_act * silu
            d_gate = d_act * val * (sg * (1.0 + gate * (1.0 - sg)))
            dg_ref[:, cs] = d_gate.astype(BF16)
            dv_ref[:, cs] = d_val.astype(BF16)
            gwd_ref[cs, :] += _dot_tn((silu * val).astype(BF16), dx)
            gbg_ref[:, cs] += jnp.sum(d_gate, axis=0, keepdims=True)
            gbv_ref[:, cs] += jnp.sum(d_val, axis=0, keepdims=True)

        @pl.when(i == nt - 1)
        def _():
            gwd_out[...] = gwd_ref[...].astype(BF16)

    tile = pl.BlockSpec((tm, half), lambda j, i: (i, j))
    vec = pl.BlockSpec((1, half), lambda j, i: (0, j))
    wrows = pl.BlockSpec((half, D_MODEL), lambda j, i: (j, 0))
    return pl.pallas_call(
        body, name="ffn_bwd_act", grid=(2, nt),
        in_specs=[tile, tile, pl.BlockSpec((tm, D_MODEL), lambda j, i: (i, 0)), wrows],
        out_specs=[tile, tile, wrows, vec, vec],
        out_shape=[_sds((T, D_FF), BF16), _sds((T, D_FF), BF16), _sds((D_FF, D_MODEL), BF16),
                   _sds((1, D_FF), F32), _sds((1, D_FF), F32)],
        scratch_shapes=[pltpu.VMEM((half, D_MODEL), F32)],
        compiler_params=_cp(("arbitrary", "arbitrary"), 56),
    )(*_hbm(gate, val, dx2, w_down))


def _ffn_bwd_up(d_gate, d_val, upre, dx2, x1, g_ffn, w_conv, w_up, tm, seq):
    T = dx2.shape[0]
    tiles_per_seq = seq // tm
    k16 = tm // BF16_ROWS
    n16 = T // BF16_ROWS
    cw = D_FF // 2

    def body(dg_ref, dv_ref, hg_ref, hv_ref, u_ref, dx2_ref, x1_ref, g_ref, wc_ref, wu_ref, du_ref, dx1_ref, gg_ref, gwc_ref):
        i = pl.program_id(0)
        at_end = (i % tiles_per_seq) == tiles_per_seq - 1

        @pl.when(i == 0)
        def _():
            gg_ref[...] = jnp.zeros_like(gg_ref)
            gwc_ref[...] = jnp.zeros_like(gwc_ref)

        dh = jnp.zeros((tm, D_MODEL), F32)
        for j in range(4):
            src, hsrc = (dg_ref, hg_ref) if j < 2 else (dv_ref, hv_ref)
            ls = slice((j % 2) * cw, (j % 2 + 1) * cw)
            cs = slice(j * cw, (j + 1) * cw)
            d = src[:, ls].astype(F32)
            hl = hsrc[:, ls].astype(F32)[0:2]
            hl = jnp.where(at_end, 0.0, hl)
            wc = wc_ref[:, cs]
            d1 = _shift_up(d, hl, 1)
            d2 = _shift_up(d, hl, 2)
            du = (wc[2:3] * d + wc[1:2] * d1 + wc[0:1] * d2).astype(BF16)
            du_ref[:, cs] = du
            dh = dh + _dot_nt(du, wu_ref[j])
            u = u_ref[:, cs].astype(F32)
            gwc_ref[0:1, cs] += jnp.sum(d2 * u, axis=0, keepdims=True)
            gwc_ref[1:2, cs] += jnp.sum(d1 * u, axis=0, keepdims=True)
            gwc_ref[2:3, cs] += jnp.sum(d * u, axis=0, keepdims=True)
        x = x1_ref[...]
        r = _rms_r(x)
        n = x * r
        dx1_ref[...] = dx2_ref[...] + _rms_bwd(dh, n, r, g_ref[...])
        gg_ref[...] += jnp.sum(dh * n, axis=0, keepdims=True)

    nxt = pl.BlockSpec((BF16_ROWS, D_FF), lambda i: (jnp.minimum((i + 1) * k16, n16 - 1), 0))
    return pl.pallas_call(
        body, name="ffn_bwd_up", grid=(T // tm,),
        in_specs=[_row(tm, D_FF), _row(tm, D_FF), nxt, nxt, _row(tm, 2 * D_FF), _row(tm, D_MODEL), _row(tm, D_MODEL),
                  _full(g_ffn.shape), _full(w_conv.shape), _resident(w_up.shape)],
        out_specs=[_row(tm, 2 * D_FF), _row(tm, D_MODEL), _full((1, D_MODEL)), _full((3, 2 * D_FF))],
        out_shape=[_sds((T, 2 * D_FF), BF16), _sds((T, D_MODEL), F32), _sds((1, D_MODEL), F32), _sds((3, 2 * D_FF), F32)],
        compiler_params=_cp(("arbitrary",), 56),
    )(*_hbm(d_gate, d_val, d_gate, d_val, upre, dx2, x1, g_ffn, w_conv, w_up))


def _matmul_tn(a, b, tn, tk, name):
    T, M = a.shape
    N = b.shape[1]
    nk = T // tk

    def body(a_ref, b_ref, o_ref, acc_ref):
        k = pl.program_id(1)

        @pl.when(k == 0)
        def _():
            acc_ref[...] = jnp.zeros_like(acc_ref)

        acc_ref[...] += _dot_tn(a_ref[...], b_ref[...])

        @pl.when(k == nk - 1)
        def _():
            o_ref[...] = acc_ref[...].astype(BF16)

    return pl.pallas_call(
        body, name=name, grid=(N // tn, nk),
        in_specs=[pl.BlockSpec((tk, M), lambda j, k: (k, 0)), pl.BlockSpec((tk, tn), lambda j, k: (k, j))],
        out_specs=pl.BlockSpec((M, tn), lambda j, k: (0, j)), out_shape=_sds((M, N), BF16),
        scratch_shapes=[pltpu.VMEM((M, tn), F32)],
        compiler_params=_cp(("arbitrary", "arbitrary"), 48),
    )(*_hbm(a, b))


def _merge_bwd(dx1, merged, y_a, y_b, proj_g, proj_a, w_pa, w_pb, w_out, g_sgu, w_s, b_st, tm, after=None):
    T = dx1.shape[0]

    nt = T // tm
    pshape = (A_WIDTH, D_MODEL)
    order = [] if after is None else [after]

    def body(*refs):
        dx_ref, mg_ref, ya_ref, yb_ref, g_ref, p_ref, wpa_ref, wpb_ref, wo_ref, gs_ref, ws_ref, bs_ref = refs[:12]
        (dg_ref, da_ref, dyb_ref, gwo_out, gwpa_out, gwpb_out, gws_ref, gbs_ref, gg_ref,
         gwo_ref, gwpa_ref, gwpb_ref) = refs[12 + len(order):]
        i = pl.program_id(0)

        @pl.when(i == 0)
        def _():
            for r in (gwo_ref, gwpa_ref, gwpb_ref, gws_ref, gbs_ref, gg_ref):
                r[...] = jnp.zeros_like(r)

        dx = dx_ref[...].astype(BF16)
        dm = _dot_nt(dx, wo_ref[...])
        g = g_ref[...].astype(F32)
        ya = ya_ref[...]
        yb = yb_ref[...]
        pa = _dot_stacked(ya, wpa_ref)
        pb = _dot_stacked(yb, wpb_ref)
        sa = _sigmoid(g[:, :D_MODEL])
        sb = _sigmoid(g[:, D_MODEL:])
        dpa = (dm * sa).astype(BF16)
        dpb = (dm * sb).astype(BF16)
        dg_ref[:, :D_MODEL] = (dm * pa * (sa * (1.0 - sa))).astype(BF16)
        dg_ref[:, D_MODEL:] = (dm * pb * (sb * (1.0 - sb))).astype(BF16)
        d_ya = _dot_nt_stacked(dpa, wpa_ref).astype(BF16)
        dyb_ref[...] = _dot_nt_stacked(dpb, wpb_ref).astype(BF16)
        _sgu_bwd_apply(p_ref[...].astype(F32), d_ya.astype(F32), gs_ref[...], ws_ref, bs_ref, da_ref, gws_ref, gbs_ref, gg_ref)
        gwo_ref[...] += _dot_tn(mg_ref[...], dx)
        gwpa_ref[...] += _dot_tn(ya, dpa)
        gwpb_ref[...] += _dot_tn(yb, dpb)

        @pl.when(i == nt - 1)
        def _():
            gwo_out[...] = gwo_ref[...].astype(BF16)
            gwpa_out[...] = gwpa_ref[...].astype(BF16)
            gwpb_out[...] = gwpb_ref[...].astype(BF16)

    return pl.pallas_call(
        body, name="merge_bwd", grid=(nt,),
        in_specs=[_row(tm, D_MODEL), _row(tm, D_MODEL), _row(tm, A_WIDTH), _row(tm, Q_DIM), _row(tm, G_DIM), _row(tm, A_DIM),
                  _resident(w_pa.shape), _resident(w_pb.shape), _resident(w_out.shape),
                  _full(g_sgu.shape), _full(w_s.shape), _full(b_st.shape)] + [ANY] * len(order),
        out_specs=[_row(tm, G_DIM), _row(tm, A_DIM), _row(tm, Q_DIM),
                   _full(w_out.shape), _full(pshape), _full(pshape), _full(w_s.shape), _full(b_st.shape), _full(g_sgu.shape)],
        out_shape=[_sds((T, G_DIM), BF16), _sds((T, A_DIM), BF16), _sds((T, Q_DIM), BF16),
                   _sds(w_out.shape, BF16), _sds(pshape, BF16), _sds(pshape, BF16),
                   _sds(w_s.shape, F32), _sds(b_st.shape, F32), _sds(g_sgu.shape, F32)],
        scratch_shapes=[pltpu.VMEM(w_out.shape, F32), pltpu.VMEM(pshape, F32), pltpu.VMEM(pshape, F32)],
        compiler_params=_cp(("arbitrary",), 56),
    )(*_hbm(dx1, merged, y_a, y_b, proj_g, proj_a, w_pa, w_pb, w_out, g_sgu, w_s, b_st), *order)


def _sgu_bwd_apply(p, dy, g, ws_ref, bs_ref, dp_ref, gws_ref, gbs_ref, gg_ref):
    tril = _tril()
    pu, pv, u, tu, vv, tv, rv, vn = _sgu_parts(p, g)
    du_cols = []
    dvn_cols = []
    for gi in range(A_GROUPS):
        wm = jnp.where(tril, ws_ref[gi], 0.0).astype(BF16)
        wmt = wm.astype(F32).T.astype(BF16)
        bcol = bs_ref[:, gi:gi + 1]
        cs = slice(gi * CHUNK, (gi + 1) * CHUNK)
        du_rows = []
        dvn_rows = []
        gw = jnp.zeros((CHUNK, CHUNK), F32)
        gb = jnp.zeros((CHUNK, 1), F32)
        for c in range(p.shape[0] // CHUNK):
            rs = slice(c * CHUNK, (c + 1) * CHUNK)
            vn_c = vn[rs, cs]
            s = _dot(wm, vn_c) + bcol
            dy_c = dy[rs, cs]
            ds = dy_c * u[rs, cs]
            du_rows.append(dy_c * s)
            dsb = ds.astype(BF16)
            gw = gw + _dot_nt(dsb, vn_c)
            gb = gb + jnp.sum(ds, axis=-1, keepdims=True)
            dvn_rows.append(_dot(wmt, dsb))
        gws_ref[gi] += jnp.where(tril, gw, 0.0)
        gbs_ref[:, gi:gi + 1] += gb
        du_cols.append(jnp.concatenate(du_rows, axis=0))
        dvn_cols.append(jnp.concatenate(dvn_rows, axis=0))
    du = jnp.concatenate(du_cols, axis=1)
    dvn = jnp.concatenate(dvn_cols, axis=1)
    vhat = vv * rv
    gg_ref[...] += jnp.sum(dvn * vhat, axis=0, keepdims=True)
    dvv = _rms_bwd(dvn, vhat, rv, g)
    dp_ref[:, :A_WIDTH] = (du * _gelu_grad(pu, tu)).astype(BF16)
    dp_ref[:, A_WIDTH:] = (dvv * _gelu_grad(pv, tv)).astype(BF16)


def _attn_bwd(proj_b, d_yb, sinks, rel_bias, n_seq, seq, after=None):
    nb = seq // CHUNK
    bk = jnp.asarray(_band_buckets())
    order = [] if after is None else [after]

    def body(*refs):
        qkv_ref, do_ref, bk_ref, rel_ref, sink_ref = refs[:5]
        (d_ref, gs_ref, gr_ref, bias_scr, sink_scr, kvar_scr, dbias_scr, dk_scr, dv_scr, ds_scr) = refs[5 + len(order):]
        b = pl.program_id(0)
        _attn_setup(bias_scr, sink_scr, kvar_scr, qkv_ref, bk_ref, rel_ref, sink_ref)
        ones = jnp.ones((2 * CHUNK, LANES), BF16)

        @pl.when(b == 0)
        def _():
            dbias_scr[...] = jnp.zeros_like(dbias_scr)
            ds_scr[...] = jnp.zeros_like(ds_scr)

        dk_scr[...] = jnp.zeros_like(dk_scr)
        dv_scr[...] = jnp.zeros_like(dv_scr)

        def transposed(a):
            return a.astype(F32).T.astype(BF16)

        def blk(n, carry):
            r0, kv, vv = _attn_block_inputs(kvar_scr, n)
            prob, psink = _attn_probs(qkv_ref, r0, n, kv, bias_scr, sink_scr, ones)
            dp = jnp.concatenate([_dot_nt(do_ref[pl.ds(r0, CHUNK), (h // 2) * LANES:(h // 2 + 1) * LANES], vv[h // 4][h % 2])
                                  for h in range(N_HEADS)], axis=0)
            delta = _rowsum(prob * dp, ones)
            dsc = prob * (dp - _both(delta))
            ds_scr[...] += psink * delta
            dbias_scr[...] += dsc
            dsb = (dsc * (HEAD_DIM ** -0.5)).astype(BF16)
            pb = prob.astype(BF16)
            dkt = [jnp.zeros((HEAD_DIM, 2 * CHUNK), F32) for _ in range(2)]
            dvt = [jnp.zeros((HEAD_DIM, 2 * CHUNK), F32) for _ in range(2)]
            for pr in range(N_HEADS // 2):
                ps = slice(pr * LANES, (pr + 1) * LANES)
                qpt = transposed(qkv_ref[pl.ds(r0, CHUNK), ps])
                dopt = transposed(do_ref[pl.ds(r0, CHUNK), ps])
                kvh = pr // 2
                dq = jnp.zeros((CHUNK, LANES), F32)
                for hh in range(2):
                    hr = _head_rows(2 * pr + hh)
                    rows = slice(hh * HEAD_DIM, (hh + 1) * HEAD_DIM)
                    dq = dq + _dot(dsb[hr], kv[kvh][hh])
                    dkt[kvh] = dkt[kvh] + _dot(qpt, dsb[hr])[rows]
                    dvt[kvh] = dvt[kvh] + _dot(dopt, pb[hr])[rows]
                d_ref[pl.ds(r0, CHUNK), ps] = dq.astype(BF16)
            dk_scr[:, pl.ds(r0, 2 * CHUNK)] += jnp.concatenate(dkt, axis=0)
            dv_scr[:, pl.ds(r0, 2 * CHUNK)] += jnp.concatenate(dvt, axis=0)
            return carry

        lax.fori_loop(0, nb, blk, 0)
        for n in range(nb):
            rows = slice(n * CHUNK, (n + 1) * CHUNK)
            cols = slice((n + 1) * CHUNK, (n + 2) * CHUNK)
            d_ref[rows, Q_DIM:Q_DIM + KV_DIM] = dk_scr[:, cols].T.astype(BF16)
            d_ref[rows, Q_DIM + KV_DIM:] = dv_scr[:, cols].T.astype(BF16)

        @pl.when(b == n_seq - 1)
        def _():
            bkv = bk_ref[...]
            for h in range(N_HEADS):
                gs_ref[0:1, h:h + 1] = -jnp.sum(ds_scr[_head_rows(h), 0:1], axis=0, keepdims=True)
                db = dbias_scr[_head_rows(h), :]
                for bb in range(N_BUCKETS):
                    part = jnp.sum(jnp.where(bkv == bb, db, 0.0), axis=-1, keepdims=True)
                    gr_ref[bb:bb + 1, h:h + 1] = jnp.sum(part, axis=0, keepdims=True)

    smem = pl.BlockSpec(memory_space=pltpu.SMEM)
    return pl.pallas_call(
        body, name="attn_bwd", grid=(n_seq,),
        in_specs=[_row(seq, B_DIM), _row(seq, Q_DIM), _full(bk.shape), smem, smem] + [ANY] * len(order),
        out_specs=[_row(seq, B_DIM), _full((1, N_HEADS)), _full((N_BUCKETS, N_HEADS))],
        out_shape=[_sds((n_seq * seq, B_DIM), BF16), _sds((1, N_HEADS), F32), _sds((N_BUCKETS, N_HEADS), F32)],
        scratch_shapes=[pltpu.VMEM((HEAD_ROWS, 2 * CHUNK), F32), pltpu.VMEM((HEAD_ROWS, LANES), F32),
                        pltpu.VMEM((8, seq, KV_DIM), BF16), pltpu.VMEM((HEAD_ROWS, 2 * CHUNK), F32),
                        pltpu.VMEM((KV_DIM, seq + CHUNK), F32), pltpu.VMEM((KV_DIM, seq + CHUNK), F32),
                        pltpu.VMEM((HEAD_ROWS, LANES), F32)],
        compiler_params=_cp(("arbitrary",), 40),
    )(*_hbm(proj_b, d_yb, bk), rel_bias, sinks, *order)


def _inproj_bwd(d_g, d_a, d_b, x2, dx1, g_mix, w_in, tm, after=None):
    T = x2.shape[0]
    order = [] if after is None else [after]

    def body(*refs):
        dg_ref, da_ref, db_ref, x_ref, dx1_ref, g_ref, w_ref = refs[:7]
        gx_ref, gg_ref = refs[7 + len(order):]
        dh = (_dot_nt(dg_ref[...], w_ref[:, _G_COLS]) + _dot_nt(da_ref[...], w_ref[:, _A_COLS])
              + _dot_nt(db_ref[...], w_ref[:, _B_COLS]))
        x = x_ref[...]
        r = _rms_r(x)
        n = x * r
        gx_ref[...] = dx1_ref[...] + _rms_bwd(dh, n, r, g_ref[...])

        @pl.when(pl.program_id(0) == 0)
        def _():
            gg_ref[...] = jnp.zeros_like(gg_ref)

        gg_ref[...] += jnp.sum(dh * n, axis=0, keepdims=True)

    return pl.pallas_call(
        body, name="inproj_bwd", grid=(T // tm,),
        in_specs=[_row(tm, G_DIM), _row(tm, A_DIM), _row(tm, B_DIM), _row(tm, D_MODEL), _row(tm, D_MODEL),
                  _full(g_mix.shape), _resident(w_in.shape)] + [ANY] * len(order),
        out_specs=[_row(tm, D_MODEL), _full((1, D_MODEL))],
        out_shape=[_sds((T, D_MODEL), F32), _sds((1, D_MODEL), F32)],
        compiler_params=_cp(("arbitrary",), 48),
    )(*_hbm(d_g, d_a, d_b, x2, dx1, g_mix, w_in), *order)


IN_SHARD = (A_DIM + B_DIM + G_DIM) // N_CHIPS


def _unstack_w_in(stack):
    tr = 256

    def body(s_ref, o_ref):
        for i in range(N_CHIPS):
            o_ref[:, i * IN_SHARD:(i + 1) * IN_SHARD] = s_ref[i]

    return pl.pallas_call(
        body, name="unstack_w_in", grid=(D_MODEL // tr,),
        in_specs=[pl.BlockSpec((N_CHIPS, tr, IN_SHARD), lambda r: (0, r, 0))],
        out_specs=pl.BlockSpec((tr, N_CHIPS * IN_SHARD), lambda r: (r, 0)),
        out_shape=_sds((D_MODEL, N_CHIPS * IN_SHARD), stack.dtype),
        compiler_params=_cp(("arbitrary",)),
    )(*_hbm(stack))


def _grad_w_in(h, d_a, d_b, d_g, tk):
    T = h.shape[0]
    nk = T // tk
    in_dim = N_CHIPS * IN_SHARD

    def body(h_ref, da_ref, db_ref, dg_ref, o_ref, acc_ref):
        k = pl.program_id(0)

        @pl.when(k == 0)
        def _():
            acc_ref[...] = jnp.zeros_like(acc_ref)

        hb = h_ref[...]
        acc_ref[:, _A_COLS] += _dot_tn(hb, da_ref[...])
        acc_ref[:, _B_COLS] += _dot_tn(hb, db_ref[...])
        acc_ref[:, _G_COLS] += _dot_tn(hb, dg_ref[...])

        @pl.when(k == nk - 1)
        def _():
            for i in range(N_CHIPS):
                o_ref[i] = acc_ref[:, i * IN_SHARD:(i + 1) * IN_SHARD].astype(BF16)

    return pl.pallas_call(
        body, name="grad_w_in", grid=(nk,),
        in_specs=[_row(tk, D_MODEL), _row(tk, A_DIM), _row(tk, B_DIM), _row(tk, G_DIM)],
        out_specs=_full((N_CHIPS, D_MODEL, IN_SHARD)), out_shape=_sds((N_CHIPS, D_MODEL, IN_SHARD), BF16),
        scratch_shapes=[pltpu.VMEM((D_MODEL, in_dim), F32)],
        compiler_params=_cp(("arbitrary",), 56),
    )(*_hbm(h, d_a, d_b, d_g))


def _local_step(x, target, g_mix, g_sgu, w_s, b_s, sinks, rel_bias, g_ffn, b_conv, g_final,
                w_in, w_conv, proj_weights, ffn_weights, on_grads, after=None):
    n_seq, seq, _ = x.shape
    T = n_seq * seq
    tm = min(ROW_TILE, seq)
    tw = min(GRAD_ROW_TILE, T)
    tf = min(WIDE_ROW_TILE, seq)
    x2 = x.reshape(T, D_MODEL)
    tgt = target.reshape(T, D_MODEL)
    b_st = b_s.T
    g_fin = g_final.reshape(1, D_MODEL)

    proj_g, proj_a, proj_b, h, y_a = _inproj(x2, g_mix, w_in, g_sgu, w_s, b_st, tm, after)
    y_b = _attn_fwd(proj_b, sinks, rel_bias, n_seq, seq)
    w_pa, w_pb, w_out = proj_weights(y_b)
    x1, merged = _merge_fwd(x2, y_a, y_b, proj_g, w_pa, w_pb, w_out, tm)
    w_up, w_down = ffn_weights(x1)
    upre, h2, gate, val = _upproj(x1, g_ffn, w_up, w_conv, b_conv, tf, seq)
    dx2, loss, gg_final = _ffn_down_loss(gate, val, x1, tgt, w_down, g_fin, tm)

    d_gate, d_val, gw_down, gb_g, gb_v = _ffn_bwd_act(gate, val, dx2, w_down, tw)
    gb_conv = jnp.concatenate([gb_g, gb_v], axis=1)
    d_upre, dx1, gg_ffn, gw_conv = _ffn_bwd_up(d_gate, d_val, upre, dx2, x1, g_ffn, w_conv, w_up, tf, seq)
    gw_up = _matmul_tn(h2, d_upre, 2 * D_FF // 4, min(2 * GRAD_ROW_TILE, T), "grad_w_up")
    sent = on_grads("ffn", dict(w_up=gw_up, w_down=gw_down))
    d_g, d_a, d_yb, gw_out, gw_pa, gw_pb, gw_s, gb_st, gg_sgu = _merge_bwd(
        dx1, merged, y_a, y_b, proj_g, proj_a, w_pa, w_pb, w_out, g_sgu, w_s, b_st, tw, sent)
    sent = on_grads("proj", dict(w_pa=gw_pa, w_pb=gw_pb, w_out=gw_out))
    d_b, g_sinks, g_rel = _attn_bwd(proj_b, d_yb, sinks, rel_bias, n_seq, seq, sent)
    gw_in = _grad_w_in(h, d_a, d_b, d_g, min(2 * GRAD_ROW_TILE, T))
    sent = on_grads("in", dict(w_in=gw_in))
    grad_x, gg_mix = _inproj_bwd(d_g, d_a, d_b, x2, dx1, g_mix, w_in, tm, sent)

    small = dict(g_mix=gg_mix, g_sgu=gg_sgu, w_s=gw_s, b_s=gb_st.T, sinks=g_sinks, rel_bias=g_rel,
                 g_ffn=gg_ffn, b_conv=gb_conv, g_final=gg_final, w_conv=gw_conv)
    big = dict(w_in=gw_in, w_pa=gw_pa, w_pb=gw_pb, w_out=gw_out, w_up=gw_up, w_down=gw_down)
    return loss, grad_x.reshape(x.shape), small, big


_MIXER = ("w_in", "w_pa", "w_pb", "w_out")
_FFN = ("w_up", "w_down")
_BIG = _MIXER + _FFN

CONV_ROWS = 6
_SMALL_AT = dict(loss=(0, 1, 1), g_sgu=(4, 1, A_WIDTH), sinks=(5, 1, N_HEADS), b_s=(8, A_GROUPS, CHUNK),
                 b_conv=(12, CONV_ROWS, D_MODEL), w_conv=(18, 3 * CONV_ROWS, D_MODEL),
                 g_final=(36, 1, D_MODEL), g_mix=(37, 1, D_MODEL), g_ffn=(38, 1, D_MODEL),
                 w_s=(40, A_GROUPS * CHUNK * CHUNK // D_MODEL, D_MODEL))
_REL_BIAS_AT = (0, A_WIDTH)
_SMALL_IN_CALL = ("g_final", "g_mix", "g_ffn", "g_sgu", "sinks", "b_s", "b_conv", "rel_bias")
SMALL_ROWS = 104


def _pack_small(vals):
    def wide(a):
        return jnp.pad(a, ((0, 0), (0, CONV_ROWS * D_MODEL - a.shape[1]))).reshape(-1, D_MODEL)

    laid = dict(vals, b_conv=wide(vals["b_conv"]), w_conv=wide(vals["w_conv"]), w_s=vals["w_s"].reshape(-1, D_MODEL))
    rows, at = [], 0
    for n, (r0, nr, nc) in _SMALL_AT.items():
        if r0 > at:
            rows.append(jnp.zeros((r0 - at, D_MODEL), F32))
        rows.append(jnp.pad(laid[n].astype(F32).reshape(nr, nc), ((0, 0), (0, D_MODEL - nc))))
        at = r0 + nr
    return lax.dynamic_update_slice(jnp.concatenate(rows, axis=0), vals["rel_bias"].T, _REL_BIAS_AT)


def _unwide(a, r):
    return a.reshape(r, CONV_ROWS * D_MODEL)[:, :2 * D_FF]


def _mesh_pos():
    return lax.axis_index("x"), lax.axis_index("y"), lax.axis_index("c")


def _other_chips(x, y):
    return [(1 - x, y), (x, 1 - y), (1 - x, 1 - y)]


def _remote(src, dst, send_sem, recv_sem, to):
    return pltpu.make_async_remote_copy(src_ref=src, dst_ref=dst, send_sem=send_sem, recv_sem=recv_sem,
                                        device_id=to, device_id_type=MESH)


def _own_slot(own, n, at):
    return lax.dynamic_update_slice(lax.empty((n,) + own.shape, own.dtype), own[None], (at,) + (0,) * own.ndim)


def _allgather_weights(stacks, wc_stack):
    names = list(stacks)
    n = len(names)

    def body(*refs):
        ins, outs = refs[:n + 1], refs[n + 1:2 * n + 2]
        send_sems, recv_sems = refs[2 * n + 2:]
        x, y, c = _mesh_pos()
        _handshake(_chip_peers(x, y, c) + _sibling_peers(x, y, c))
        me = 2 * x + y
        sibling = (x, y, 1 - c)
        chips = _other_chips(x, y)

        def half(ref, chip, hc):
            hr = ref.shape[1] // 2
            return ref.at[chip, pl.ds(hc * hr, hr), :]

        first = []
        for k in range(n):
            first += [_remote(half(ins[k], me, c), half(outs[k], me, c), send_sems.at[6 * k + j], recv_sems.at[6 * k + j], (cx, cy, c))
                      for j, (cx, cy) in enumerate(chips)]
        first += [_remote(ins[n].at[me], outs[n].at[me], send_sems.at[6 * n + j], recv_sems.at[6 * n + j], (cx, cy, c))
                  for j, (cx, cy) in enumerate(chips)]
        for cp in first:
            cp.start()
        passed = []
        for k in range(n):
            for j, (cx, cy) in enumerate(chips):
                landed = half(outs[k], 2 * cx + cy, c)
                _remote(landed, landed, send_sems.at[6 * k + j], recv_sems.at[6 * k + j], (x, y, c)).wait_recv()
                passed.append(_remote(landed, landed, send_sems.at[6 * k + 3 + j], recv_sems.at[6 * k + 3 + j], sibling))
                passed[-1].start()
        for k in range(n):
            for j, (cx, cy) in enumerate(chips):
                theirs = half(outs[k], 2 * cx + cy, 1 - c)
                _remote(theirs, theirs, send_sems.at[6 * k + 3 + j], recv_sems.at[6 * k + 3 + j], (x, y, c)).wait_recv()
        for j, (cx, cy) in enumerate(chips):
            slot = outs[n].at[2 * cx + cy]
            _remote(slot, slot, send_sems.at[6 * n + j], recv_sems.at[6 * n + j], (x, y, c)).wait_recv()
        for cp in first + passed:
            cp.wait_send()

    arrays = [stacks[k] for k in names] + [wc_stack]
    outs = pl.pallas_call(
        body, name="allgather_weights",
        in_specs=[HBM] * (n + 1), out_specs=[HBM] * (n + 1), input_output_aliases={k: k for k in range(n + 1)},
        out_shape=[_sds(a.shape, a.dtype) for a in arrays],
        scratch_shapes=[pltpu.SemaphoreType.DMA((6 * n + 3,)), pltpu.SemaphoreType.DMA((6 * n + 3,))],
        compiler_params=pltpu.CompilerParams(collective_id=_COLLECTIVE["gather_in"]),
    )(*arrays)
    return dict(zip(names, outs[:n])), outs[n]


_KIND = {"w_in": "stack", "w_pa": "col", "w_pb": "col", "w_up": "col", "w_out": "row", "w_down": "row"}


def _half_view(ref, kind, h):
    if kind == "stack":
        k = ref.shape[1] // 2
        return ref.at[:, pl.ds(h * k, k), :]
    if kind == "col":
        k = ref.shape[0] // 2
        return ref.at[pl.ds(h * k, k), :]
    k = ref.shape[1] // 2
    return ref.at[:, pl.ds(h * k, k)]


def _shard_view(ref, kind, i):
    if kind == "stack":
        return ref.at[i]
    if kind == "col":
        k = ref.shape[1] // N_CHIPS
        return ref.at[:, pl.ds(i * k, k)]
    k = ref.shape[0] // N_CHIPS
    return ref.at[pl.ds(i * k, k), :]


def _region_view(ref, kind, h):
    if kind == "row":
        k = ref.shape[1] // 2
        return ref.at[:, pl.ds(h * k, k)]
    k = ref.shape[0] // 2
    return ref.at[pl.ds(h * k, k), :]


def _half_shape(shape, kind):
    if kind == "stack":
        return (shape[0], shape[1] // 2, shape[2])
    return (shape[0] // 2, shape[1]) if kind == "col" else (shape[0], shape[1] // 2)


def _part_shape(half_shape, kind):
    if kind == "stack":
        return tuple(half_shape[1:])
    k, w = half_shape
    return (k, w // N_CHIPS) if kind == "col" else (k // N_CHIPS, w)


_DATAFLOW = pltpu.SideEffectType.DATAFLOW_SIDE_EFFECTING
_TOKEN = (SUBLANES, LANES)


_COLLECTIVE = {k: i for i, k in enumerate(
    [kind + "_" + g for kind in ("pair", "chip", "share") for g in ("ffn", "proj", "in")]
    + ["gather_proj", "gather_ffn", "gather_in", "forward_proj", "forward_ffn"])}


def _sibling_peers(x, y, c):
    return [(x, y, 1 - c)]


def _chip_peers(x, y, c):
    return [(cx, cy, c) for cx, cy in _other_chips(x, y)]


def _handshake(peers):
    barrier = pltpu.get_barrier_semaphore()
    for peer in peers:
        pl.semaphore_signal(barrier, inc=1, device_id=peer, device_id_type=MESH)
    pl.semaphore_wait(barrier, len(peers))


def _split_start(name, arrays, n_sems, issue, after=None, handshake=None):
    n = len(arrays)
    order = [] if after is None else [after]

    def body(*refs):
        base = n + len(order)
        if handshake is not None:
            _handshake(handshake[1](*_mesh_pos()))
        issue(refs[:n], refs[base], refs[base + 1])
        refs[-1][...] = jnp.zeros(_TOKEN, F32)

    params = dict(has_side_effects=_DATAFLOW)
    if handshake is not None:
        params["collective_id"] = handshake[0]
    outs = pl.pallas_call(
        body, name=name,
        in_specs=[HBM] * n + [ANY] * len(order), out_specs=[SEM, SEM] + [HBM] * n + [pl.BlockSpec(memory_space=pltpu.VMEM)],
        out_shape=[pltpu.SemaphoreType.DMA((n_sems,)), pltpu.SemaphoreType.DMA((n_sems,))]
        + [pltpu.HBM(a.shape, a.dtype) for a in arrays] + [_sds(_TOKEN, F32)],
        input_output_aliases={k: 2 + k for k in range(n)},
        compiler_params=pltpu.CompilerParams(**params),
    )(*[pltpu.with_memory_space_constraint(a, pltpu.HBM) for a in arrays], *order)
    return outs[0], outs[1], list(outs[2:2 + n]), outs[-1]


def _split_wait(name, started, waits, after):
    send_sems, recv_sems, arrays, _ = started
    n = len(arrays)

    def body(*refs):
        waits(refs[:n], refs[n], refs[n + 1])

    return pl.pallas_call(
        body, name=name,
        in_specs=[HBM] * n + [SEM, SEM, ANY], out_specs=[HBM] * n,
        out_shape=[pltpu.HBM(a.shape, a.dtype) for a in arrays],
        input_output_aliases={k: k for k in range(n)},
        compiler_params=pltpu.CompilerParams(has_side_effects=_DATAFLOW),
    )(*arrays, send_sems, recv_sems, after)


def _wait_both(src, dst, send_sem, recv_sem):
    x, y, c = _mesh_pos()
    cp = _remote(src, dst, send_sem, recv_sem, (x, y, c))
    cp.wait_send()
    cp.wait_recv()


def _pair_exchange_start(parts, tag, after):
    names = list(parts)
    n = len(names)
    lands = [lax.empty(_half_shape(parts[k].shape, _KIND[k]), parts[k].dtype) for k in names]

    def issue(refs, send_sems, recv_sems):
        x, y, c = _mesh_pos()
        for hc in range(2):
            @pl.when(c == hc)
            def _():
                for k in range(n):
                    _remote(_half_view(refs[k], _KIND[names[k]], 1 - hc), refs[n + k], send_sems.at[k], recv_sems.at[k],
                            (x, y, 1 - c)).start()

    return names, _split_start("grad_pair_exchange_start_" + tag, [parts[k] for k in names] + lands, n, issue, after,
                               (_COLLECTIVE["pair_" + tag], _sibling_peers))


def _pair_exchange_wait(pending, tag, after):
    names, started = pending
    n = len(names)

    def waits(refs, send_sems, recv_sems):
        for k in range(n):
            _wait_both(_half_view(refs[k], _KIND[names[k]], 0), refs[n + k], send_sems.at[k], recv_sems.at[k])

    outs = _split_wait("grad_pair_exchange_wait_" + tag, started, waits, after)
    return dict(zip(names, outs[:n])), dict(zip(names, outs[n:]))


def _half_blocks(shape, kind):
    if kind == "stack":
        _, k, w = shape
        nb = 2
        tr = k // 2 // nb
        return (N_CHIPS, nb), (1, tr, w), (lambda i, r, s: (i, r, 0)), (lambda i, r, s: (i, s[1] * nb + r, 0))
    k, w = shape
    if kind == "col":
        tr = STREAM_ROWS
        nb = k // 2 // tr
        return (nb,), (tr, w), (lambda r, s: (r, 0)), (lambda r, s: (s[1] * nb + r, 0))
    nb = 2 * N_CHIPS
    return (nb,), (k // nb, w // 2), (lambda r, s: (r, 0)), (lambda r, s: (r, s[1]))


def _pair_add(part, from_sibling, name, pos):
    kind = _KIND[name]
    grid, block, half_map, full_map = _half_blocks(part.shape, kind)

    def body(s_ref, p_ref, q_ref, o_ref):
        o_ref[...] = (p_ref[...].astype(F32) + q_ref[...].astype(F32)).astype(BF16)

    return pl.pallas_call(
        body, name="grad_pair_add_" + name,
        grid_spec=pltpu.PrefetchScalarGridSpec(
            num_scalar_prefetch=1, grid=grid,
            in_specs=[pl.BlockSpec(block, full_map), pl.BlockSpec(block, half_map)],
            out_specs=pl.BlockSpec(block, half_map)),
        out_shape=_sds(from_sibling.shape, BF16),
        compiler_params=_cp(("arbitrary",) * len(grid), 40),
    )(pos, *_hbm(part, from_sibling))


def _chip_exchange_start(sums, tag, after):
    names = list(sums)
    n = len(names)
    lands = [lax.empty((3,) + _part_shape(sums[k].shape, _KIND[k]), sums[k].dtype) for k in names]

    def issue(refs, send_sems, recv_sems):
        x, y, c = _mesh_pos()
        me = 2 * x + y
        for i in range(N_CHIPS):
            xi, yi = i // 2, i % 2
            j = jnp.where(xi != x, jnp.where(yi != y, 2, 0), 1)

            @pl.when(i != me)
            def _():
                for k in range(n):
                    _remote(_shard_view(refs[k], _KIND[names[k]], i), refs[n + k].at[j], send_sems.at[3 * k + j],
                            recv_sems.at[3 * k + j], (xi, yi, c)).start()

    return names, _split_start("grad_chip_exchange_start_" + tag, [sums[k] for k in names] + lands, 3 * n, issue, after,
                               (_COLLECTIVE["chip_" + tag], _chip_peers))


def _chip_exchange_wait(pending, tag, after):
    names, started = pending
    n = len(names)

    def waits(refs, send_sems, recv_sems):
        for k in range(n):
            for j in range(3):
                _wait_both(_shard_view(refs[k], _KIND[names[k]], 0), refs[n + k].at[j], send_sems.at[3 * k + j], recv_sems.at[3 * k + j])

    return dict(zip(names, _split_wait("grad_chip_exchange_wait_" + tag, started, waits, after)[n:]))


def _allgather_start(stacks, tag, after):
    names = list(stacks)

    def issue(refs, send_sems, recv_sems):
        x, y, c = _mesh_pos()
        me = 2 * x + y
        for k, st in enumerate(refs):
            hr = st.shape[1] // 2
            mine = st.at[me, pl.ds(c * hr, hr), :]
            for j, (cx, cy) in enumerate(_other_chips(x, y)):
                _remote(mine, mine, send_sems.at[3 * k + j], recv_sems.at[3 * k + j], (cx, cy, c)).start()

    return names, _split_start("allgather_start_" + tag, [stacks[k] for k in names], 3 * len(names), issue, after,
                               (_COLLECTIVE["gather_" + tag], _chip_peers))


def _allgather_wait(pending, tag, after):
    names, started = pending

    def waits(refs, send_sems, recv_sems):
        for k, st in enumerate(refs):
            slot = st.at[0, pl.ds(0, st.shape[1] // 2), :]
            for j in range(3):
                _wait_both(slot, slot, send_sems.at[3 * k + j], recv_sems.at[3 * k + j])

    return dict(zip(names, _split_wait("allgather_wait_" + tag, started, waits, after)))


def _allgather_forward(stacks, tag):
    names = list(stacks)
    n = len(names)

    def body(*refs):
        ins, outs = refs[:n], refs[n:2 * n]
        send_sems, recv_sems = refs[2 * n:]
        x, y, c = _mesh_pos()
        _handshake(_sibling_peers(x, y, c))
        copies = []
        for k in range(n):
            hr = ins[k].shape[1] // 2
            for j, (cx, cy) in enumerate(_other_chips(x, y)):
                chip = 2 * cx + cy
                copies.append(_remote(ins[k].at[chip, pl.ds(c * hr, hr), :], outs[k].at[chip, pl.ds(c * hr, hr), :],
                                      send_sems.at[3 * k + j], recv_sems.at[3 * k + j], (x, y, 1 - c)))
        for cp in copies:
            cp.start()
        for cp in copies:
            cp.wait()

    arrays = [stacks[k] for k in names]
    outs = pl.pallas_call(
        body, name="allgather_forward_" + tag, in_specs=[HBM] * n, out_specs=[HBM] * n,
        input_output_aliases={k: k for k in range(n)},
        out_shape=[_sds(a.shape, a.dtype) for a in arrays],
        scratch_shapes=[pltpu.SemaphoreType.DMA((3 * n,)), pltpu.SemaphoreType.DMA((3 * n,))],
        compiler_params=pltpu.CompilerParams(collective_id=_COLLECTIVE["forward_" + tag]),
    )(*arrays)
    return dict(zip(names, outs))


def _owner_sum(part, from_sibling, from_chips, name, pos, shard_shape):
    kind = _KIND[name]
    _, pk, pw = from_chips.shape
    if kind == "row":
        nb = 4
        tr = pk // nb
        p_spec = pl.BlockSpec((tr, pw), lambda r, s: (s[0] * nb + r, s[1]))
        q_spec = pl.BlockSpec((tr, pw), lambda r, s: (s[0] * nb + r, 0))
        o_spec = pl.BlockSpec((tr, pw), lambda r, s: (r, s[1]))
    else:
        tr = STREAM_ROWS
        nb = pk // tr
        if kind == "stack":
            p_spec = pl.BlockSpec((None, tr, pw), lambda r, s: (s[0], s[1] * nb + r, 0))
            q_spec = pl.BlockSpec((None, tr, pw), lambda r, s: (s[0], r, 0))
        else:
            p_spec = pl.BlockSpec((tr, pw), lambda r, s: (s[1] * nb + r, s[0]))
            q_spec = pl.BlockSpec((tr, pw), lambda r, s: (r, s[0]))
        o_spec = pl.BlockSpec((tr, pw), lambda r, s: (s[1] * nb + r, 0))

    def body(s_ref, p_ref, q_ref, r_ref, o_ref):
        acc = p_ref[...].astype(F32) + q_ref[...].astype(F32)
        for j in range(3):
            acc = acc + r_ref[j].astype(F32)
        o_ref[...] = acc

    return pl.pallas_call(
        body, name="grad_owner_sum_" + name,
        grid_spec=pltpu.PrefetchScalarGridSpec(
            num_scalar_prefetch=1, grid=(nb,),
            in_specs=[p_spec, q_spec, pl.BlockSpec((3, tr, pw), lambda r, s: (0, r, 0))],
            out_specs=o_spec),
        out_shape=_sds(shard_shape, F32),
        compiler_params=_cp(("arbitrary",), 32),
    )(pos, *_hbm(part, from_sibling, from_chips))


def _pair_share_start(shards, tag, after):
    names = list(shards)

    def issue(refs, send_sems, recv_sems):
        x, y, c = _mesh_pos()
        for hc in range(2):
            @pl.when(c == hc)
            def _():
                for k, g in enumerate(refs):
                    mine = _region_view(g, _KIND[names[k]], hc)
                    _remote(mine, mine, send_sems.at[k], recv_sems.at[k], (x, y, 1 - c)).start()

    return names, _split_start("grad_pair_share_start_" + tag, [shards[k] for k in names], len(names), issue, after,
                               (_COLLECTIVE["share_" + tag], _sibling_peers))


def _pair_share_wait(pending, tag, after):
    names, started = pending

    def waits(refs, send_sems, recv_sems):
        for k, g in enumerate(refs):
            region = _region_view(g, _KIND[names[k]], 0)
            _wait_both(region, region, send_sems.at[k], recv_sems.at[k])

    return dict(zip(names, _split_wait("grad_pair_share_wait_" + tag, started, waits, after)))


def _small_exchange_start(slots, after):
    def issue(refs, send_sems, recv_sems):
        x, y, c = _mesh_pos()
        mine = refs[0].at[4 * x + 2 * y + c]
        k = 0
        for px in range(2):
            for py in range(2):
                for pc in range(2):
                    if px + py + pc:
                        peer = (1 - x if px else x, 1 - y if py else y, 1 - c if pc else c)
                        _remote(mine, mine, send_sems.at[k], recv_sems.at[k], peer).start()
                        k += 1

    return _split_start("small_exchange_start", [slots], N_DEV - 1, issue, after)


def _small_exchange_wait(started, after):
    def waits(refs, send_sems, recv_sems):
        slot = refs[0].at[0]
        for k in range(N_DEV - 1):
            _wait_both(slot, slot, send_sems.at[k], recv_sems.at[k])

    return _split_wait("small_exchange_wait", started, waits, after)[0]


def _adam_math(w, g, m, v):
    m = ADAM_B1 * m + (1.0 - ADAM_B1) * g
    v = ADAM_B2 * v + (1.0 - ADAM_B2) * (g * g)
    m_hat = m / (1.0 - ADAM_B1 ** ADAM_STEP)
    v_hat = v / (1.0 - ADAM_B2 ** ADAM_STEP)
    delta = -ADAM_LR * (m_hat / (jnp.sqrt(v_hat) + ADAM_EPS) + ADAM_WD * w)
    return delta, m, v


def _adamw(w, g, m, v, name):
    rows, cols = w.shape[0], w.shape[-1]
    fits = [t for t in range(SUBLANES, rows, SUBLANES) if rows % t == 0 and t * cols * 4 <= (3 << 18)]
    tr = max(fits) if fits and w.ndim == 2 else rows

    def body(w_ref, g_ref, m_ref, v_ref, d_ref, nm_ref, nv_ref, go_ref):
        g = g_ref[...]
        d, nm, nv = _adam_math(w_ref[...], g, m_ref[...], v_ref[...])
        d_ref[...] = d
        nm_ref[...] = nm
        nv_ref[...] = nv
        go_ref[...] = g

    spec = pl.BlockSpec((tr,) + w.shape[1:], lambda i: (i,) + (0,) * (w.ndim - 1))
    return pl.pallas_call(
        body, name=name, grid=(rows // tr,), in_specs=[spec] * 4, out_specs=[spec] * 4,
        out_shape=[_sds(w.shape, F32)] * 4, compiler_params=_cp(("arbitrary",)),
    )(*_hbm(w, g, m, v))


def _small_sum_adamw(gathered, w, m, v):
    names = _SMALL_IN_CALL
    n = len(names)

    def body(*refs):
        a_ref = refs[0]
        w_refs, m_refs, v_refs = refs[1:1 + n], refs[1 + n:1 + 2 * n], refs[1 + 2 * n:1 + 3 * n]
        sum_ref, loss_ref = refs[1 + 3 * n], refs[2 + 3 * n]
        outs = refs[3 + 3 * n:]
        g = a_ref[0]
        for k in range(1, N_DEV):
            g = g + a_ref[k]
        sum_ref[...] = g
        loss_ref[...] = g[0:1, 0:1]
        for i, name in enumerate(names):
            if name == "rel_bias":
                r0, c0 = _REL_BIAS_AT
                gp = g[r0:r0 + N_HEADS, c0:c0 + N_BUCKETS]
            elif name == "b_conv":
                r0 = _SMALL_AT[name][0]
                gp = jnp.concatenate([g[r0 + k:r0 + k + 1, :] for k in range(CONV_ROWS)], axis=1)[:, :2 * D_FF]
            else:
                r0, nr, nc = _SMALL_AT[name]
                gp = g[r0:r0 + nr, 0:nc]
            d, nm, nv = _adam_math(w_refs[i][...], gp, m_refs[i][...], v_refs[i][...])
            for k, val in enumerate((gp, d, nm, nv)):
                outs[4 * i + k][...] = val

    shapes = [w[k].shape for k in names]
    res = pl.pallas_call(
        body, name="small_sum_adamw",
        out_shape=[_sds((SMALL_ROWS, D_MODEL), F32), _sds((1, 1), F32)] + [_sds(s, F32) for s in shapes for _ in range(4)],
    )(gathered, *[w[k] for k in names], *[m[k] for k in names], *[v[k] for k in names])
    return res[0], res[1], {k: tuple(res[2 + 4 * i:6 + 4 * i]) for i, k in enumerate(names)}


_NAMES = ("g_mix", "w_in", "g_sgu", "w_s", "b_s", "sinks", "rel_bias", "w_pa", "w_pb", "w_out",
          "g_ffn", "w_up", "w_conv", "b_conv", "w_down", "g_final")

def kernel(x, g_mix, w_in, g_sgu, w_s, b_s, sinks, rel_bias, w_pa, w_pb, w_out, g_ffn, w_up, w_conv, b_conv, w_down, g_final, loss_target, m_g_mix, m_w_in, m_g_sgu, m_w_s, m_b_s, m_sinks, m_rel_bias, m_w_pa, m_w_pb, m_w_out, m_g_ffn, m_w_up, m_w_conv, m_b_conv, m_w_down, m_g_final, v_g_mix, v_w_in, v_g_sgu, v_w_s, v_b_s, v_sinks, v_rel_bias, v_w_pa, v_w_pb, v_w_out, v_g_ffn, v_w_up, v_w_conv, v_b_conv, v_w_down, v_g_final):
    w = dict(g_mix=g_mix, w_in=w_in, g_sgu=g_sgu, w_s=w_s, b_s=b_s, sinks=sinks, rel_bias=rel_bias, w_pa=w_pa, w_pb=w_pb,
             w_out=w_out, g_ffn=g_ffn, w_up=w_up, w_conv=w_conv, b_conv=b_conv, w_down=w_down, g_final=g_final)
    m = dict(g_mix=m_g_mix, w_in=m_w_in, g_sgu=m_g_sgu, w_s=m_w_s, b_s=m_b_s, sinks=m_sinks, rel_bias=m_rel_bias, w_pa=m_w_pa,
             w_pb=m_w_pb, w_out=m_w_out, g_ffn=m_g_ffn, w_up=m_w_up, w_conv=m_w_conv, b_conv=m_b_conv, w_down=m_w_down,
             g_final=m_g_final)
    v = dict(g_mix=v_g_mix, w_in=v_w_in, g_sgu=v_g_sgu, w_s=v_w_s, b_s=v_b_s, sinks=v_sinks, rel_bias=v_rel_bias, w_pa=v_w_pa,
             w_pb=v_w_pb, w_out=v_w_out, g_ffn=v_g_ffn, w_up=v_w_up, w_conv=v_w_conv, b_conv=v_b_conv, w_down=v_w_down,
             g_final=v_g_final)
    xi, yi, ci = _mesh_pos()
    me = 2 * xi + yi

    shard = {n: w[n][0] for n in _BIG}
    shard_shapes = {n: shard[n].shape for n in _BIG}
    wc_shard = w["w_conv"][0]
    wc_pad = jnp.pad(wc_shard, ((0, 5), (0, 0)))
    own = {n: _own_slot(shard[n].astype(BF16), N_CHIPS, me) for n in _BIG}
    stacks, wc_all = _allgather_weights({"w_in": own["w_in"]}, _own_slot(wc_pad, N_CHIPS, me))
    proj_gather = _allgather_start({n: own[n] for n in _MIXER[1:]}, "proj", stacks["w_in"])
    ffn_gather = _allgather_start({n: own[n] for n in _FFN}, "ffn", proj_gather[1][-1])
    w_conv_full = jnp.concatenate([wc_all[i, :3] for i in range(N_CHIPS)], axis=1)
    w_in_full = _unstack_w_in(stacks["w_in"])
    pos = jnp.stack([me, ci])

    def proj_weights(done):
        st = _allgather_forward(_allgather_wait(proj_gather, "proj", done), "proj")
        return st["w_pa"], st["w_pb"], st["w_out"].reshape(D_MODEL, D_MODEL)

    def ffn_weights(done):
        st = _allgather_forward(_allgather_wait(ffn_gather, "ffn", done), "ffn")
        return st["w_up"], st["w_down"].reshape(D_FF, D_MODEL)

    groups = {}

    def stage1(group, parts):
        groups[group] = dict(parts=parts, pair=_pair_exchange_start(parts, group, None))
        return groups[group]["pair"][1][-1]

    def stage2(group, after, order_after):
        g = groups[group]
        g["parts"], g["sib"] = _pair_exchange_wait(g["pair"], group, after)
        g["chip"] = _chip_exchange_start({n: _pair_add(g["parts"][n], g["sib"][n], n, pos) for n in g["parts"]}, group, order_after)
        return g["chip"][1][-1]

    def stage3(group, after, order_after):
        g = groups[group]
        got = _chip_exchange_wait(g["chip"], group, after)
        g["share"] = _pair_share_start(
            {n: _owner_sum(g["parts"][n], g["sib"][n], got[n], n, pos, shard_shapes[n]) for n in g["parts"]}, group, order_after)
        return g["share"][1][-1]

    grads, deltas, new_m, new_v = {}, {}, {}, {}

    def stage4(group, after):
        g_shard = _pair_share_wait(groups[group]["share"], group, after)
        last = None
        for n in g_shard:
            g = _tie(g_shard[n], last)
            if n == "w_in":
                d, nm, nv, gt = _adamw(shard[n].T, g.T, m[n][0].T, v[n][0].T, "adamw_" + n)
                grads[n], deltas[n], new_m[n], new_v[n] = gt.T[None], d.T[None], nm.T[None], nv.T[None]
            else:
                d, nm, nv, go = _adamw(shard[n], g, m[n][0], v[n][0], "adamw_" + n)
                grads[n], deltas[n], new_m[n], new_v[n] = go[None], d[None], nm[None], nv[None]
            last = nv
        return last

    def on_grads(group, parts):
        token = stage1(group, parts)
        some = next(iter(parts.values()))
        if group == "proj":
            token = stage2("ffn", some, token)
        if group == "in":
            token = stage2("proj", some, token)
            token = stage3("ffn", some, token)
            token = stage2("in", token, token)
        return token

    loss, grad_x, small, big = _local_step(
        x, loss_target, w["g_mix"], w["g_sgu"], w["w_s"][0], w["b_s"][0], w["sinks"], w["rel_bias"], w["g_ffn"],
        w["b_conv"], w["g_final"], w_in_full, w_conv_full, proj_weights, ffn_weights, on_grads, ffn_gather[1][-1])

    small["loss"] = loss
    small_gather = _small_exchange_start(_own_slot(_pack_small(small), N_DEV, 2 * me + ci), grad_x)
    token = stage3("proj", grad_x, small_gather[-1])
    done = stage4("ffn", token)
    done = stage4("proj", done)
    token = stage3("in", done, None)
    all_small = _small_exchange_wait(small_gather, token)
    two_d = {n: (lambda a, n=n: a.reshape(_SMALL_AT[n][1:])) for n in _SMALL_IN_CALL}
    two_d["rel_bias"] = lambda a: a.T
    two_d["b_conv"] = lambda a: a
    s_sum, s_loss, s_out = _small_sum_adamw(all_small, *[{n: two_d[n](p[n]) for n in _SMALL_IN_CALL} for p in (w, m, v)])
    stage4("in", all_small)
    for n in _SMALL_IN_CALL:
        back = (lambda a: a.T) if n == "rel_bias" else (lambda a, n=n: a.reshape(w[n].shape))
        grads[n], deltas[n], new_m[n], new_v[n] = [back(a) for a in s_out[n]]

    def rows(n):
        r0, nr, _ = _SMALL_AT[n]
        return s_sum[r0:r0 + nr]

    wcols = wc_shard.shape[1]
    g_wc = lax.dynamic_slice(_unwide(rows("w_conv"), 3), (0, me * wcols), (3, wcols))
    taps = lambda a: a.transpose(1, 0, 2)
    res = _adamw(taps(w["w_conv"]), g_wc[:, None, :], taps(m["w_conv"]), taps(v["w_conv"]), "adamw_w_conv")
    deltas["w_conv"], new_m["w_conv"], new_v["w_conv"], grads["w_conv"] = [taps(a) for a in res]
    flat_s = (A_GROUPS * CHUNK, CHUNK)
    d, nm, nv, go = _adamw(w["w_s"].reshape(flat_s), rows("w_s").reshape(flat_s), m["w_s"].reshape(flat_s),
                           v["w_s"].reshape(flat_s), "adamw_w_s")
    grads["w_s"], deltas["w_s"], new_m["w_s"], new_v["w_s"] = [a.reshape(w["w_s"].shape) for a in (go, d, nm, nv)]

    return (s_loss.reshape(()), grad_x, *[grads[n] for n in _NAMES], *[deltas[n] for n in _NAMES],
            *[new_m[n] for n in _NAMES], *[new_v[n] for n in _NAMES])
```

```python
import functools

import numpy as np
import jax
import jax.numpy as jnp
from jax import lax
from jax.experimental import pallas as pl
from jax.experimental.pallas import tpu as pltpu

F32 = jnp.float32
BF16 = jnp.bfloat16

D_MODEL = 1024
CHUNK = 128
A_GROUPS = 4
A_WIDTH = 512
N_HEADS = 8
HEAD_DIM = 64
Q_DIM = 512
KV_DIM = 128
N_BUCKETS = 32
MAX_DISTANCE = 128
D_FF = 2816
EPS = 1e-6
NEG_INF = -1e30
G_DIM = 2 * D_MODEL
A_DIM = 2 * A_WIDTH
B_DIM = Q_DIM + 2 * KV_DIM
LANES = 128
SUBLANES = 8
ROW_TILE = 512
WIDE_ROW_TILE = 256
COL_CHUNK = 512
GRAD_ROW_TILE = 512
STREAM_ROWS = 256
BF16_ROWS = 16
N_CHIPS = 4
N_DEV = 8

ADAM_LR = 0.001
ADAM_B1 = 0.9
ADAM_B2 = 0.999
ADAM_EPS = 1e-08
ADAM_WD = 0.01
ADAM_STEP = 10

MESH = pl.DeviceIdType.MESH
_GELU_C = 0.7978845608028654
_GELU_A = 0.044715


def _cp(sem=None, vmem_mb=None):
    kw = {}
    if sem is not None:
        kw["dimension_semantics"] = sem
    if vmem_mb is not None:
        kw["vmem_limit_bytes"] = vmem_mb << 20
    return pltpu.CompilerParams(**kw)


def _dot(a, b):
    return jnp.dot(a, b, preferred_element_type=F32)


def _dot_nt(a, b):
    return lax.dot_general(a, b, (((1,), (1,)), ((), ())), preferred_element_type=F32)


def _dot_tn(a, b):
    return lax.dot_general(a, b, (((0,), (0,)), ((), ())), preferred_element_type=F32)


def _rms_r(x):
    return lax.rsqrt(jnp.mean(x * x, axis=-1, keepdims=True) + EPS)


def _rms_bwd(dh, n, r, g):
    dn = dh * g
    return r * (dn - n * jnp.mean(dn * n, axis=-1, keepdims=True))


def _gelu(x):
    t = jnp.tanh(_GELU_C * (x + _GELU_A * (x * x * x)))
    return 0.5 * x * (1.0 + t), t


def _gelu_grad(x, t):
    return 0.5 * (1.0 + t) + 0.5 * x * (1.0 - t * t) * (_GELU_C * (1.0 + 3.0 * _GELU_A * x * x))


def _sigmoid(x):
    return 1.0 / (1.0 + jnp.exp(-x))


def _tie(x, dep):
    return x if dep is None else lax.optimization_barrier((x, dep))[0]


def _row(tm, w):
    return pl.BlockSpec((tm, w), lambda i: (i, 0))


def _full(shape):
    nd = len(shape)
    return pl.BlockSpec(tuple(shape), lambda *_: (0,) * nd)


def _resident(shape):
    nd = len(shape)
    return pl.BlockSpec(tuple(shape), lambda *_: (0,) * nd, pipeline_mode=pl.Buffered(1))


def _sds(shape, dtype):
    return pltpu.HBM(tuple(shape), dtype)


def _hbm(*arrays):
    return [pltpu.with_memory_space_constraint(a, pltpu.HBM) for a in arrays]


HBM = pl.BlockSpec(memory_space=pltpu.HBM)
ANY = pl.BlockSpec(memory_space=pl.ANY)
SEM = pl.BlockSpec(memory_space=pltpu.SEMAPHORE)


def _band_buckets():
    i = np.arange(CHUNK)[:, None]
    j = np.arange(2 * CHUNK)[None, :]
    dist = i + CHUNK - j
    valid = (dist >= 0) & (dist < CHUNK)
    d = np.clip(dist, 0, None)
    max_exact = N_BUCKETS // 2
    large = max_exact + (np.log(np.maximum(d, 1) / max_exact) / np.log(MAX_DISTANCE / max_exact)
                         * (N_BUCKETS - max_exact)).astype(np.int32)
    large = np.minimum(large, N_BUCKETS - 1)
    buckets = np.where(d < max_exact, d, large).astype(np.int32)
    return np.where(valid, buckets, -1).astype(np.int32)


_A_COLS = slice(0, A_DIM)
_B_COLS = slice(A_DIM, A_DIM + B_DIM)
_G_COLS = slice(A_DIM + B_DIM, A_DIM + B_DIM + G_DIM)


def _inproj(x2, g_mix, w_in, g_sgu, w_s, b_st, tm, after=None):
    T = x2.shape[0]
    order = [] if after is None else [after]

    def body(*refs):
        x_ref, g_ref, w_ref, gs_ref, ws_ref, bs_ref = refs[:6]
        pg_ref, pa_ref, pb_ref, h_ref, ya_ref = refs[6 + len(order):]
        x = x_ref[...]
        h = (x * _rms_r(x) * g_ref[...]).astype(BF16)
        h_ref[...] = h
        pa = _dot(h, w_ref[:, _A_COLS]).astype(BF16)
        pa_ref[...] = pa
        pb_ref[...] = _dot(h, w_ref[:, _B_COLS]).astype(BF16)
        pg_ref[...] = _dot(h, w_ref[:, _G_COLS]).astype(BF16)
        _sgu_apply(pa.astype(F32), gs_ref[...], ws_ref, bs_ref, ya_ref)

    return pl.pallas_call(
        body, name="inproj", grid=(T // tm,),
        in_specs=[_row(tm, D_MODEL), _full(g_mix.shape), _resident(w_in.shape), _full(g_sgu.shape), _full(w_s.shape),
                  _full(b_st.shape)] + [ANY] * len(order),
        out_specs=[_row(tm, G_DIM), _row(tm, A_DIM), _row(tm, B_DIM), _row(tm, D_MODEL), _row(tm, A_WIDTH)],
        out_shape=[_sds((T, G_DIM), BF16), _sds((T, A_DIM), BF16), _sds((T, B_DIM), BF16), _sds((T, D_MODEL), BF16),
                   _sds((T, A_WIDTH), BF16)],
        compiler_params=_cp(("arbitrary",), 48),
    )(*_hbm(x2, g_mix, w_in, g_sgu, w_s, b_st), *order)


def _sgu_parts(p, g):
    pu = p[:, :A_WIDTH]
    pv = p[:, A_WIDTH:]
    u, tu = _gelu(pu)
    vv, tv = _gelu(pv)
    rv = _rms_r(vv)
    vn = (vv * rv * g).astype(BF16)
    return pu, pv, u, tu, vv, tv, rv, vn


def _tril():
    r = lax.broadcasted_iota(jnp.int32, (CHUNK, CHUNK), 0)
    c = lax.broadcasted_iota(jnp.int32, (CHUNK, CHUNK), 1)
    return r >= c


def _sgu_apply(p, g, ws_ref, bs_ref, y_ref):
    tril = _tril()
    _, _, u, _, _, _, _, vn = _sgu_parts(p, g)
    for gi in range(A_GROUPS):
        wm = jnp.where(tril, ws_ref[gi], 0.0).astype(BF16)
        bcol = bs_ref[:, gi:gi + 1]
        cs = slice(gi * CHUNK, (gi + 1) * CHUNK)
        for c in range(p.shape[0] // CHUNK):
            rs = slice(c * CHUNK, (c + 1) * CHUNK)
            s = _dot(wm, vn[rs, cs]) + bcol
            y_ref[rs, cs] = (u[rs, cs] * s).astype(BF16)


HEAD_ROWS = N_HEADS * CHUNK


def _head_rows(h):
    return slice(h * CHUNK, (h + 1) * CHUNK)


def _attn_setup(bias_scr, sink_scr, kvar_scr, qkv_ref, bk_ref, rel_ref, sink_ref):
    @pl.when(pl.program_id(0) == 0)
    def _():
        bk = bk_ref[...]
        for h in range(N_HEADS):
            acc = jnp.full((CHUNK, 2 * CHUNK), NEG_INF, F32)
            for b in range(N_BUCKETS):
                acc = jnp.where(bk == b, rel_ref[b, h], acc)
            bias_scr[_head_rows(h), :] = acc
            sink_scr[_head_rows(h), :] = jnp.full((CHUNK, LANES), sink_ref[0, h], F32)

    seq = qkv_ref.shape[0]
    rows_per = 2 * CHUNK
    for is_v in range(2):
        c0 = Q_DIM + is_v * KV_DIM
        for r in range(seq // rows_per):
            rs = slice(r * rows_per, (r + 1) * rows_per)
            a = qkv_ref[rs, c0:c0 + KV_DIM].astype(F32)
            lane = lax.broadcasted_iota(jnp.int32, a.shape, 1)
            lo = jnp.where(lane < HEAD_DIM, a, 0.0)
            hi = jnp.where(lane >= HEAD_DIM, a, 0.0)
            kvar_scr[4 * is_v + 0, rs, :] = lo.astype(BF16)
            kvar_scr[4 * is_v + 1, rs, :] = pltpu.roll(lo, HEAD_DIM, 1).astype(BF16)
            kvar_scr[4 * is_v + 2, rs, :] = pltpu.roll(hi, HEAD_DIM, 1).astype(BF16)
            kvar_scr[4 * is_v + 3, rs, :] = hi.astype(BF16)


def _rowsum(a, ones):
    hi = a.astype(BF16)
    lo = (a - hi.astype(F32)).astype(BF16)
    return _dot(hi, ones) + _dot(lo, ones)


def _both(a):
    return jnp.concatenate([a, a], axis=1)


def _attn_probs(qkv_ref, r0, n, kv, bias_scr, sink_scr, ones):
    s = jnp.concatenate([_dot_nt(qkv_ref[pl.ds(r0, CHUNK), (h // 2) * LANES:(h // 2 + 1) * LANES], kv[h // 4][h % 2])
                         for h in range(N_HEADS)], axis=0)
    s = s * (HEAD_DIM ** -0.5) + bias_scr[...]
    col = lax.broadcasted_iota(jnp.int32, s.shape, 1)
    s = jnp.where((col < CHUNK) & (n == 0), NEG_INF, s)
    sink = sink_scr[...]
    m = jnp.maximum(jnp.max(s, axis=-1, keepdims=True), sink)
    p = jnp.exp(s - _both(m))
    es = jnp.exp(sink - m)
    inv = 1.0 / (_dot(p.astype(BF16), ones) + es)
    return p * _both(inv), es * inv


def _attn_block_inputs(kvar_scr, n):
    r0 = pl.multiple_of(n * CHUNK, CHUNK)
    rp = pl.multiple_of(jnp.maximum(n - 1, 0) * CHUNK, CHUNK)

    def both(idx):
        return jnp.concatenate([kvar_scr[idx, pl.ds(rp, CHUNK), :], kvar_scr[idx, pl.ds(r0, CHUNK), :]], axis=0)

    kv = ((both(0), both(1)), (both(2), both(3)))
    vv = ((both(4), both(5)), (both(6), both(7)))
    return r0, kv, vv


def _attn_fwd(proj_b, sinks, rel_bias, n_seq, seq):
    nb = seq // CHUNK
    bk = jnp.asarray(_band_buckets())

    def body(qkv_ref, bk_ref, rel_ref, sink_ref, o_ref, bias_scr, sink_scr, kvar_scr):
        _attn_setup(bias_scr, sink_scr, kvar_scr, qkv_ref, bk_ref, rel_ref, sink_ref)
        ones = jnp.ones((2 * CHUNK, LANES), BF16)

        def blk(n, carry):
            r0, kv, vv = _attn_block_inputs(kvar_scr, n)
            prob, _ = _attn_probs(qkv_ref, r0, n, kv, bias_scr, sink_scr, ones)
            pb = prob.astype(BF16)
            for pr in range(N_HEADS // 2):
                acc = _dot(pb[_head_rows(2 * pr)], vv[pr // 2][0]) + _dot(pb[_head_rows(2 * pr + 1)], vv[pr // 2][1])
                o_ref[pl.ds(r0, CHUNK), pr * LANES:(pr + 1) * LANES] = acc.astype(BF16)
            return carry

        lax.fori_loop(0, nb, blk, 0)

    smem = pl.BlockSpec(memory_space=pltpu.SMEM)
    return pl.pallas_call(
        body, name="attn_fwd", grid=(n_seq,),
        in_specs=[_row(seq, B_DIM), _full(bk.shape), smem, smem],
        out_specs=_row(seq, Q_DIM), out_shape=_sds((n_seq * seq, Q_DIM), BF16),
        scratch_shapes=[pltpu.VMEM((HEAD_ROWS, 2 * CHUNK), F32), pltpu.VMEM((HEAD_ROWS, LANES), F32),
                        pltpu.VMEM((8, seq, KV_DIM), BF16)],
        compiler_params=_cp(("arbitrary",), 40),
    )(*_hbm(proj_b, bk), rel_bias, sinks)


def _dot_stacked(a, w_ref):
    return jnp.concatenate([_dot(a, w_ref[i]) for i in range(N_CHIPS)], axis=1)


def _dot_nt_stacked(a, w_ref):
    w = w_ref.shape[2]
    acc = _dot_nt(a[:, :w], w_ref[0])
    for i in range(1, N_CHIPS):
        acc = acc + _dot_nt(a[:, i * w:(i + 1) * w], w_ref[i])
    return acc


def _merge_fwd(x2, y_a, y_b, proj_g, w_pa, w_pb, w_out, tm):
    T = x2.shape[0]

    def body(x_ref, ya_ref, yb_ref, g_ref, wpa_ref, wpb_ref, wo_ref, x1_ref, mg_ref):
        g = g_ref[...].astype(F32)
        pa = _dot_stacked(ya_ref[...], wpa_ref)
        pb = _dot_stacked(yb_ref[...], wpb_ref)
        merged = (_sigmoid(g[:, :D_MODEL]) * pa + _sigmoid(g[:, D_MODEL:]) * pb).astype(BF16)
        mg_ref[...] = merged
        x1_ref[...] = x_ref[...] + _dot(merged, wo_ref[...])

    return pl.pallas_call(
        body, name="merge_fwd", grid=(T // tm,),
        in_specs=[_row(tm, D_MODEL), _row(tm, A_WIDTH), _row(tm, Q_DIM), _row(tm, G_DIM),
                  _resident(w_pa.shape), _resident(w_pb.shape), _resident(w_out.shape)],
        out_specs=[_row(tm, D_MODEL), _row(tm, D_MODEL)],
        out_shape=[_sds((T, D_MODEL), F32), _sds((T, D_MODEL), BF16)],
        compiler_params=_cp(("arbitrary",), 40),
    )(*_hbm(x2, y_a, y_b, proj_g, w_pa, w_pb, w_out))


def _upproj(x1, g_ffn, w_up, w_conv, b_conv, tm, seq):
    T = x1.shape[0]
    cw = w_up.shape[2]
    tiles_per_seq = seq // tm

    def body(x_ref, g_ref, w_ref, wc_ref, bc_ref, u_ref, h_ref, gate_ref, val_ref, tail_scr):
        at_start = (pl.program_id(0) % tiles_per_seq) == 0
        x = x_ref[...]
        h = (x * _rms_r(x) * g_ref[...]).astype(BF16)
        h_ref[...] = h
        for i in range(N_CHIPS):
            cs = slice(i * cw, (i + 1) * cw)
            u = _dot(h, w_ref[i])
            u_ref[:, cs] = u.astype(BF16)
            hl = jnp.where(at_start, 0.0, tail_scr[SUBLANES - 2:SUBLANES, cs])
            tail_scr[:, cs] = u[tm - SUBLANES:]
            up = _conv_out((u, _shift_down(u, hl, 1), _shift_down(u, hl, 2)), wc_ref[:, cs], bc_ref[:, cs])
            out_ref = gate_ref if i < N_CHIPS // 2 else val_ref
            out_ref[:, (i % 2) * cw:(i % 2 + 1) * cw] = up.astype(BF16)

    return pl.pallas_call(
        body, name="upproj", grid=(T // tm,),
        in_specs=[_row(tm, D_MODEL), _full(g_ffn.shape), _resident(w_up.shape), _full(w_conv.shape), _full(b_conv.shape)],
        out_specs=[_row(tm, 2 * D_FF), _row(tm, D_MODEL), _row(tm, D_FF), _row(tm, D_FF)],
        out_shape=[_sds((T, 2 * D_FF), BF16), _sds((T, D_MODEL), BF16), _sds((T, D_FF), BF16), _sds((T, D_FF), BF16)],
        scratch_shapes=[pltpu.VMEM((SUBLANES, 2 * D_FF), F32)],
        compiler_params=_cp(("arbitrary",), 56),
    )(*_hbm(x1, g_ffn, w_up, w_conv, b_conv))


def _shift_down(u, halo, k):
    rolled = pltpu.roll(u, k, 0)
    head = rolled[:SUBLANES]
    row = lax.broadcasted_iota(jnp.int32, head.shape, 0)
    if k == 1:
        head = jnp.where(row == 0, halo[1:2], head)
    else:
        head = jnp.where(row == 0, halo[0:1], jnp.where(row == 1, halo[1:2], head))
    return jnp.concatenate([head, rolled[SUBLANES:]], axis=0)


def _shift_up(d, halo, k):
    tm = d.shape[0]
    rolled = pltpu.roll(d, tm - k, 0)
    tail = rolled[tm - SUBLANES:]
    row = lax.broadcasted_iota(jnp.int32, tail.shape, 0)
    if k == 1:
        tail = jnp.where(row == SUBLANES - 1, halo[0:1], tail)
    else:
        tail = jnp.where(row == SUBLANES - 2, halo[0:1], jnp.where(row == SUBLANES - 1, halo[1:2], tail))
    return jnp.concatenate([rolled[:tm - SUBLANES], tail], axis=0)


def _conv_out(taps, wc, bc):
    u, u1, u2 = taps
    return wc[0:1] * u2 + wc[1:2] * u1 + wc[2:3] * u + bc


def _ffn_down_loss(gate, val, x1, target, w_down, g_final, tm):
    T = x1.shape[0]
    half = D_FF // 2

    def body(gt_ref, vl_ref, x1_ref, t_ref, wd_ref, g_ref, dx2_ref, loss_ref, gg_ref):
        i = pl.program_id(0)
        acc = jnp.zeros((tm, D_MODEL), F32)
        for j in range(2):
            gc = slice(j * half, (j + 1) * half)
            gate = gt_ref[:, gc].astype(F32)
            act = (gate * _sigmoid(gate) * vl_ref[:, gc].astype(F32)).astype(BF16)
            acc = acc + _dot(act, wd_ref[gc, :])
        x2 = x1_ref[...] + acc
        r = _rms_r(x2)
        n = x2 * r
        g = g_ref[...]
        diff = n * g - t_ref[...]
        dy = diff * (1.0 / D_MODEL)
        dx2_ref[...] = _rms_bwd(dy, n, r, g)

        @pl.when(i == 0)
        def _():
            loss_ref[...] = jnp.zeros_like(loss_ref)
            gg_ref[...] = jnp.zeros_like(gg_ref)

        loss_ref[...] += 0.5 * jnp.sum(jnp.mean(diff * diff, axis=-1, keepdims=True), axis=0, keepdims=True)
        gg_ref[...] += jnp.sum(dy * n, axis=0, keepdims=True)

    return pl.pallas_call(
        body, name="ffn_down_loss", grid=(T // tm,),
        in_specs=[_row(tm, D_FF), _row(tm, D_FF), _row(tm, D_MODEL), _row(tm, D_MODEL),
                  _resident(w_down.shape), _full(g_final.shape)],
        out_specs=[_row(tm, D_MODEL), _full((1, 1)), _full((1, D_MODEL))],
        out_shape=[_sds((T, D_MODEL), F32), _sds((1, 1), F32), _sds((1, D_MODEL), F32)],
        compiler_params=_cp(("arbitrary",), 48),
    )(*_hbm(gate, val, x1, target, w_down, g_final))


def _ffn_bwd_act(gate, val, dx2, w_down, tm):
    T = dx2.shape[0]
    half = D_FF // 2
    nt = T // tm

    def body(g_ref, v_ref, dx_ref, wd_ref, dg_ref, dv_ref, gwd_out, gbg_ref, gbv_ref, gwd_ref):
        i = pl.program_id(1)

        @pl.when(i == 0)
        def _():
            for r in (gwd_ref, gbg_ref, gbv_ref):
                r[...] = jnp.zeros_like(r)

        dx = dx_ref[...].astype(BF16)
        for c0 in range(0, half, COL_CHUNK):
            cs = slice(c0, min(c0 + COL_CHUNK, half))
            gate = g_ref[:, cs].astype(F32)
            val = v_ref[:, cs].astype(F32)
            sg = _sigmoid(gate)
            silu = gate * sg
            d_act = _dot_nt(dx, wd_ref[cs, :])
            d_val = d_act * silu
            d_gate = d_act * val * (sg * (1.0 + gate * (1.0 - sg)))
            dg_ref[:, cs] = d_gate.astype(BF16)
            dv_ref[:, cs] = d_val.astype(BF16)
            gwd_ref[cs, :] += _dot_tn((silu * val).astype(BF16), dx)
            gbg_ref[:, cs] += jnp.sum(d_gate, axis=0, keepdims=True)
            gbv_ref[:, cs] += jnp.sum(d_val, axis=0, keepdims=True)

        @pl.when(i == nt - 1)
        def _():
            gwd_out[...] = gwd_ref[...].astype(BF16)

    tile = pl.BlockSpec((tm, half), lambda j, i: (i, j))
    vec = pl.BlockSpec((1, half), lambda j, i: (0, j))
    wrows = pl.BlockSpec((half, D_MODEL), lambda j, i: (j, 0))
    return pl.pallas_call(
        body, name="ffn_bwd_act", grid=(2, nt),
        in_specs=[tile, tile, pl.BlockSpec((tm, D_MODEL), lambda j, i: (i, 0)), wrows],
        out_specs=[tile, tile, wrows, vec, vec],
        out_shape=[_sds((T, D_FF), BF16), _sds((T, D_FF), BF16), _sds((D_FF, D_MODEL), BF16),
                   _sds((1, D_FF), F32), _sds((1, D_FF), F32)],
        scratch_shapes=[pltpu.VMEM((half, D_MODEL), F32)],
        compiler_params=_cp(("arbitrary", "arbitrary"), 56),
    )(*_hbm(gate, val, dx2, w_down))


def _ffn_bwd_up(d_gate, d_val, upre, dx2, x1, g_ffn, w_conv, w_up, tm, seq):
    T = dx2.shape[0]
    tiles_per_seq = seq // tm
    k16 = tm // BF16_ROWS
    n16 = T // BF16_ROWS
    cw = D_FF // 2

    def body(dg_ref, dv_ref, hg_ref, hv_ref, u_ref, dx2_ref, x1_ref, g_ref, wc_ref, wu_ref, du_ref, dx1_ref, gg_ref, gwc_ref):
        i = pl.program_id(0)
        at_end = (i % tiles_per_seq) == tiles_per_seq - 1

        @pl.when(i == 0)
        def _():
            gg_ref[...] = jnp.zeros_like(gg_ref)
            gwc_ref[...] = jnp.zeros_like(gwc_ref)

        dh = jnp.zeros((tm, D_MODEL), F32)
        for j in range(4):
            src, hsrc = (dg_ref, hg_ref) if j < 2 else (dv_ref, hv_ref)
            ls = slice((j % 2) * cw, (j % 2 + 1) * cw)
            cs = slice(j * cw, (j + 1) * cw)
            d = src[:, ls].astype(F32)
            hl = hsrc[:, ls].astype(F32)[0:2]
            hl = jnp.where(at_end, 0.0, hl)
            wc = wc_ref[:, cs]
            d1 = _shift_up(d, hl, 1)
            d2 = _shift_up(d, hl, 2)
            du = (wc[2:3] * d + wc[1:2] * d1 + wc[0:1] * d2).astype(BF16)
            du_ref[:, cs] = du
            dh = dh + _dot_nt(du, wu_ref[j])
            u = u_ref[:, cs].astype(F32)
            gwc_ref[0:1, cs] += jnp.sum(d2 * u, axis=0, keepdims=True)
            gwc_ref[1:2, cs] += jnp.sum(d1 * u, axis=0, keepdims=True)
            gwc_ref[2:3, cs] += jnp.sum(d * u, axis=0, keepdims=True)
        x = x1_ref[...]
        r = _rms_r(x)
        n = x * r
        dx1_ref[...] = dx2_ref[...] + _rms_bwd(dh, n, r, g_ref[...])
        gg_ref[...] += jnp.sum(dh * n, axis=0, keepdims=True)

    nxt = pl.BlockSpec((BF16_ROWS, D_FF), lambda i: (jnp.minimum((i + 1) * k16, n16 - 1), 0))
    return pl.pallas_call(
        body, name="ffn_bwd_up", grid=(T // tm,),
        in_specs=[_row(tm, D_FF), _row(tm, D_FF), nxt, nxt, _row(tm, 2 * D_FF), _row(tm, D_MODEL), _row(tm, D_MODEL),
                  _full(g_ffn.shape), _full(w_conv.shape), _resident(w_up.shape)],
        out_specs=[_row(tm, 2 * D_FF), _row(tm, D_MODEL), _full((1, D_MODEL)), _full((3, 2 * D_FF))],
        out_shape=[_sds((T, 2 * D_FF), BF16), _sds((T, D_MODEL), F32), _sds((1, D_MODEL), F32), _sds((3, 2 * D_FF), F32)],
        compiler_params=_cp(("arbitrary",), 56),
    )(*_hbm(d_gate, d_val, d_gate, d_val, upre, dx2, x1, g_ffn, w_conv, w_up))


def _matmul_tn(a, b, tn, tk, name):
    T, M = a.shape
    N = b.shape[1]
    nk = T // tk

    def body(a_ref, b_ref, o_ref, acc_ref):
        k = pl.program_id(1)

        @pl.when(k == 0)
        def _():
            acc_ref[...] = jnp.zeros_like(acc_ref)

        acc_ref[...] += _dot_tn(a_ref[...], b_ref[...])

        @pl.when(k == nk - 1)
        def _():
            o_ref[...] = acc_ref[...].astype(BF16)

    return pl.pallas_call(
        body, name=name, grid=(N // tn, nk),
        in_specs=[pl.BlockSpec((tk, M), lambda j, k: (k, 0)), pl.BlockSpec((tk, tn), lambda j, k: (k, j))],
        out_specs=pl.BlockSpec((M, tn), lambda j, k: (0, j)), out_shape=_sds((M, N), BF16),
        scratch_shapes=[pltpu.VMEM((M, tn), F32)],
        compiler_params=_cp(("arbitrary", "arbitrary"), 48),
    )(*_hbm(a, b))


def _merge_bwd(dx1, merged, y_a, y_b, proj_g, proj_a, w_pa, w_pb, w_out, g_sgu, w_s, b_st, tm, after=None):
    T = dx1.shape[0]

    nt = T // tm
    pshape = (A_WIDTH, D_MODEL)
    order = [] if after is None else [after]

    def body(*refs):
        dx_ref, mg_ref, ya_ref, yb_ref, g_ref, p_ref, wpa_ref, wpb_ref, wo_ref, gs_ref, ws_ref, bs_ref = refs[:12]
        (dg_ref, da_ref, dyb_ref, gwo_out, gwpa_out, gwpb_out, gws_ref, gbs_ref, gg_ref,
         gwo_ref, gwpa_ref, gwpb_ref) = refs[12 + len(order):]
        i = pl.program_id(0)

        @pl.when(i == 0)
        def _():
            for r in (gwo_ref, gwpa_ref, gwpb_ref, gws_ref, gbs_ref, gg_ref):
                r[...] = jnp.zeros_like(r)

        dx = dx_ref[...].astype(BF16)
        dm = _dot_nt(dx, wo_ref[...])
        g = g_ref[...].astype(F32)
        ya = ya_ref[...]
        yb = yb_ref[...]
        pa = _dot_stacked(ya, wpa_ref)
        pb = _dot_stacked(yb, wpb_ref)
        sa = _sigmoid(g[:, :D_MODEL])
        sb = _sigmoid(g[:, D_MODEL:])
        dpa = (dm * sa).astype(BF16)
        dpb = (dm * sb).astype(BF16)
        dg_ref[:, :D_MODEL] = (dm * pa * (sa * (1.0 - sa))).astype(BF16)
        dg_ref[:, D_MODEL:] = (dm * pb * (sb * (1.0 - sb))).astype(BF16)
        d_ya = _dot_nt_stacked(dpa, wpa_ref).astype(BF16)
        dyb_ref[...] = _dot_nt_stacked(dpb, wpb_ref).astype(BF16)
        _sgu_bwd_apply(p_ref[...].astype(F32), d_ya.astype(F32), gs_ref[...], ws_ref, bs_ref, da_ref, gws_ref, gbs_ref, gg_ref)
        gwo_ref[...] += _dot_tn(mg_ref[...], dx)
        gwpa_ref[...] += _dot_tn(ya, dpa)
        gwpb_ref[...] += _dot_tn(yb, dpb)

        @pl.when(i == nt - 1)
        def _():
            gwo_out[...] = gwo_ref[...].astype(BF16)
            gwpa_out[...] = gwpa_ref[...].astype(BF16)
            gwpb_out[...] = gwpb_ref[...].astype(BF16)

    return pl.pallas_call(
        body, name="merge_bwd", grid=(nt,),
        in_specs=[_row(tm, D_MODEL), _row(tm, D_MODEL), _row(tm, A_WIDTH), _row(tm, Q_DIM), _row(tm, G_DIM), _row(tm, A_DIM),
                  _resident(w_pa.shape), _resident(w_pb.shape), _resident(w_out.shape),
                  _full(g_sgu.shape), _full(w_s.shape), _full(b_st.shape)] + [ANY] * len(order),
        out_specs=[_row(tm, G_DIM), _row(tm, A_DIM), _row(tm, Q_DIM),
                   _full(w_out.shape), _full(pshape), _full(pshape), _full(w_s.shape), _full(b_st.shape), _full(g_sgu.shape)],
        out_shape=[_sds((T, G_DIM), BF16), _sds((T, A_DIM), BF16), _sds((T, Q_DIM), BF16),
                   _sds(w_out.shape, BF16), _sds(pshape, BF16), _sds(pshape, BF16),
                   _sds(w_s.shape, F32), _sds(b_st.shape, F32), _sds(g_sgu.shape, F32)],
        scratch_shapes=[pltpu.VMEM(w_out.shape, F32), pltpu.VMEM(pshape, F32), pltpu.VMEM(pshape, F32)],
        compiler_params=_cp(("arbitrary",), 56),
    )(*_hbm(dx1, merged, y_a, y_b, proj_g, proj_a, w_pa, w_pb, w_out, g_sgu, w_s, b_st), *order)


def _sgu_bwd_apply(p, dy, g, ws_ref, bs_ref, dp_ref, gws_ref, gbs_ref, gg_ref):
    tril = _tril()
    pu, pv, u, tu, vv, tv, rv, vn = _sgu_parts(p, g)
    du_cols = []
    dvn_cols = []
    for gi in range(A_GROUPS):
        wm = jnp.where(tril, ws_ref[gi], 0.0).astype(BF16)
        wmt = wm.astype(F32).T.astype(BF16)
        bcol = bs_ref[:, gi:gi + 1]
        cs = slice(gi * CHUNK, (gi + 1) * CHUNK)
        du_rows = []
        dvn_rows = []
        gw = jnp.zeros((CHUNK, CHUNK), F32)
        gb = jnp.zeros((CHUNK, 1), F32)
        for c in range(p.shape[0] // CHUNK):
            rs = slice(c * CHUNK, (c + 1) * CHUNK)
            vn_c = vn[rs, cs]
            s = _dot(wm, vn_c) + bcol
            dy_c = dy[rs, cs]
            ds = dy_c * u[rs, cs]
            du_rows.append(dy_c * s)
            dsb = ds.astype(BF16)
            gw = gw + _dot_nt(dsb, vn_c)
            gb = gb + jnp.sum(ds, axis=-1, keepdims=True)
            dvn_rows.append(_dot(wmt, dsb))
        gws_ref[gi] += jnp.where(tril, gw, 0.0)
        gbs_ref[:, gi:gi + 1] += gb
        du_cols.append(jnp.concatenate(du_rows, axis=0))
        dvn_cols.append(jnp.concatenate(dvn_rows, axis=0))
    du = jnp.concatenate(du_cols, axis=1)
    dvn = jnp.concatenate(dvn_cols, axis=1)
    vhat = vv * rv
    gg_ref[...] += jnp.sum(dvn * vhat, axis=0, keepdims=True)
    dvv = _rms_bwd(dvn, vhat, rv, g)
    dp_ref[:, :A_WIDTH] = (du * _gelu_grad(pu, tu)).astype(BF16)
    dp_ref[:, A_WIDTH:] = (dvv * _gelu_grad(pv, tv)).astype(BF16)


def _attn_bwd(proj_b, d_yb, sinks, rel_bias, n_seq, seq, after=None):
    nb = seq // CHUNK
    bk = jnp.asarray(_band_buckets())
    order = [] if after is None else [after]

    def body(*refs):
        qkv_ref, do_ref, bk_ref, rel_ref, sink_ref = refs[:5]
        (d_ref, gs_ref, gr_ref, bias_scr, sink_scr, kvar_scr, dbias_scr, dk_scr, dv_scr, ds_scr) = refs[5 + len(order):]
        b = pl.program_id(0)
        _attn_setup(bias_scr, sink_scr, kvar_scr, qkv_ref, bk_ref, rel_ref, sink_ref)
        ones = jnp.ones((2 * CHUNK, LANES), BF16)

        @pl.when(b == 0)
        def _():
            dbias_scr[...] = jnp.zeros_like(dbias_scr)
            ds_scr[...] = jnp.zeros_like(ds_scr)

        dk_scr[...] = jnp.zeros_like(dk_scr)
        dv_scr[...] = jnp.zeros_like(dv_scr)

        def transposed(a):
            return a.astype(F32).T.astype(BF16)

        def blk(n, carry):
            r0, kv, vv = _attn_block_inputs(kvar_scr, n)
            prob, psink = _attn_probs(qkv_ref, r0, n, kv, bias_scr, sink_scr, ones)
            dp = jnp.concatenate([_dot_nt(do_ref[pl.ds(r0, CHUNK), (h // 2) * LANES:(h // 2 + 1) * LANES], vv[h // 4][h % 2])
                                  for h in range(N_HEADS)], axis=0)
            delta = _rowsum(prob * dp, ones)
            dsc = prob * (dp - _both(delta))
            ds_scr[...] += psink * delta
            dbias_scr[...] += dsc
            dsb = (dsc * (HEAD_DIM ** -0.5)).astype(BF16)
            pb = prob.astype(BF16)
            dkt = [jnp.zeros((HEAD_DIM, 2 * CHUNK), F32) for _ in range(2)]
            dvt = [jnp.zeros((HEAD_DIM, 2 * CHUNK), F32) for _ in range(2)]
            for pr in range(N_HEADS // 2):
                ps = slice(pr * LANES, (pr + 1) * LANES)
                qpt = transposed(qkv_ref[pl.ds(r0, CHUNK), ps])
                dopt = transposed(do_ref[pl.ds(r0, CHUNK), ps])
                kvh = pr // 2
                dq = jnp.zeros((CHUNK, LANES), F32)
                for hh in range(2):
                    hr = _head_rows(2 * pr + hh)
                    rows = slice(hh * HEAD_DIM, (hh + 1) * HEAD_DIM)
                    dq = dq + _dot(dsb[hr], kv[kvh][hh])
                    dkt[kvh] = dkt[kvh] + _dot(qpt, dsb[hr])[rows]
                    dvt[kvh] = dvt[kvh] + _dot(dopt, pb[hr])[rows]
                d_ref[pl.ds(r0, CHUNK), ps] = dq.astype(BF16)
            dk_scr[:, pl.ds(r0, 2 * CHUNK)] += jnp.concatenate(dkt, axis=0)
            dv_scr[:, pl.ds(r0, 2 * CHUNK)] += jnp.concatenate(dvt, axis=0)
            return carry

        lax.fori_loop(0, nb, blk, 0)
        for n in range(nb):
            rows = slice(n * CHUNK, (n + 1) * CHUNK)
            cols = slice((n + 1) * CHUNK, (n + 2) * CHUNK)
            d_ref[rows, Q_DIM:Q_DIM + KV_DIM] = dk_scr[:, cols].T.astype(BF16)
            d_ref[rows, Q_DIM + KV_DIM:] = dv_scr[:, cols].T.astype(BF16)

        @pl.when(b == n_seq - 1)
        def _():
            bkv = bk_ref[...]
            for h in range(N_HEADS):
                gs_ref[0:1, h:h + 1] = -jnp.sum(ds_scr[_head_rows(h), 0:1], axis=0, keepdims=True)
                db = dbias_scr[_head_rows(h), :]
                for bb in range(N_BUCKETS):
                    part = jnp.sum(jnp.where(bkv == bb, db, 0.0), axis=-1, keepdims=True)
                    gr_ref[bb:bb + 1, h:h + 1] = jnp.sum(part, axis=0, keepdims=True)

    smem = pl.BlockSpec(memory_space=pltpu.SMEM)
    return pl.pallas_call(
        body, name="attn_bwd", grid=(n_seq,),
        in_specs=[_row(seq, B_DIM), _row(seq, Q_DIM), _full(bk.shape), smem, smem] + [ANY] * len(order),
        out_specs=[_row(seq, B_DIM), _full((1, N_HEADS)), _full((N_BUCKETS, N_HEADS))],
        out_shape=[_sds((n_seq * seq, B_DIM), BF16), _sds((1, N_HEADS), F32), _sds((N_BUCKETS, N_HEADS), F32)],
        scratch_shapes=[pltpu.VMEM((HEAD_ROWS, 2 * CHUNK), F32), pltpu.VMEM((HEAD_ROWS, LANES), F32),
                        pltpu.VMEM((8, seq, KV_DIM), BF16), pltpu.VMEM((HEAD_ROWS, 2 * CHUNK), F32),
                        pltpu.VMEM((KV_DIM, seq + CHUNK), F32), pltpu.VMEM((KV_DIM, seq + CHUNK), F32),
                        pltpu.VMEM((HEAD_ROWS, LANES), F32)],
        compiler_params=_cp(("arbitrary",), 40),
    )(*_hbm(proj_b, d_yb, bk), rel_bias, sinks, *order)


def _inproj_bwd(d_g, d_a, d_b, x2, dx1, g_mix, w_in, tm, after=None):
    T = x2.shape[0]
    order = [] if after is None else [after]

    def body(*refs):
        dg_ref, da_ref, db_ref, x_ref, dx1_ref, g_ref, w_ref = refs[:7]
        gx_ref, gg_ref = refs[7 + len(order):]
        dh = (_dot_nt(dg_ref[...], w_ref[:, _G_COLS]) + _dot_nt(da_ref[...], w_ref[:, _A_COLS])
              + _dot_nt(db_ref[...], w_ref[:, _B_COLS]))
        x = x_ref[...]
        r = _rms_r(x)
        n = x * r
        gx_ref[...] = dx1_ref[...] + _rms_bwd(dh, n, r, g_ref[...])

        @pl.when(pl.program_id(0) == 0)
        def _():
            gg_ref[...] = jnp.zeros_like(gg_ref)

        gg_ref[...] += jnp.sum(dh * n, axis=0, keepdims=True)

    return pl.pallas_call(
        body, name="inproj_bwd", grid=(T // tm,),
        in_specs=[_row(tm, G_DIM), _row(tm, A_DIM), _row(tm, B_DIM), _row(tm, D_MODEL), _row(tm, D_MODEL),
                  _full(g_mix.shape), _resident(w_in.shape)] + [ANY] * len(order),
        out_specs=[_row(tm, D_MODEL), _full((1, D_MODEL))],
        out_shape=[_sds((T, D_MODEL), F32), _sds((1, D_MODEL), F32)],
        compiler_params=_cp(("arbitrary",), 48),
    )(*_hbm(d_g, d_a, d_b, x2, dx1, g_mix, w_in), *order)


IN_SHARD = (A_DIM + B_DIM + G_DIM) // N_CHIPS


def _unstack_w_in(stack):
    tr = 256

    def body(s_ref, o_ref):
        for i in range(N_CHIPS):
            o_ref[:, i * IN_SHARD:(i + 1) * IN_SHARD] = s_ref[i]

    return pl.pallas_call(
        body, name="unstack_w_in", grid=(D_MODEL // tr,),
        in_specs=[pl.BlockSpec((N_CHIPS, tr, IN_SHARD), lambda r: (0, r, 0))],
        out_specs=pl.BlockSpec((tr, N_CHIPS * IN_SHARD), lambda r: (r, 0)),
        out_shape=_sds((D_MODEL, N_CHIPS * IN_SHARD), stack.dtype),
        compiler_params=_cp(("arbitrary",)),
    )(*_hbm(stack))


def _grad_w_in(h, d_a, d_b, d_g, tk):
    T = h.shape[0]
    nk = T // tk
    in_dim = N_CHIPS * IN_SHARD

    def body(h_ref, da_ref, db_ref, dg_ref, o_ref, acc_ref):
        k = pl.program_id(0)

        @pl.when(k == 0)
        def _():
            acc_ref[...] = jnp.zeros_like(acc_ref)

        hb = h_ref[...]
        acc_ref[:, _A_COLS] += _dot_tn(hb, da_ref[...])
        acc_ref[:, _B_COLS] += _dot_tn(hb, db_ref[...])
        acc_ref[:, _G_COLS] += _dot_tn(hb, dg_ref[...])

        @pl.when(k == nk - 1)
        def _():
            for i in range(N_CHIPS):
                o_ref[i] = acc_ref[:, i * IN_SHARD:(i + 1) * IN_SHARD].astype(BF16)

    return pl.pallas_call(
        body, name="grad_w_in", grid=(nk,),
        in_specs=[_row(tk, D_MODEL), _row(tk, A_DIM), _row(tk, B_DIM), _row(tk, G_DIM)],
        out_specs=_full((N_CHIPS, D_MODEL, IN_SHARD)), out_shape=_sds((N_CHIPS, D_MODEL, IN_SHARD), BF16),
        scratch_shapes=[pltpu.VMEM((D_MODEL, in_dim), F32)],
        compiler_params=_cp(("arbitrary",), 56),
    )(*_hbm(h, d_a, d_b, d_g))


def _local_step(x, target, g_mix, g_sgu, w_s, b_s, sinks, rel_bias, g_ffn, b_conv, g_final,
                w_in, w_conv, proj_weights, ffn_weights, on_grads, after=None):
    n_seq, seq, _ = x.shape
    T = n_seq * seq
    tm = min(ROW_TILE, seq)
    tw = min(GRAD_ROW_TILE, T)
    tf = min(WIDE_ROW_TILE, seq)
    x2 = x.reshape(T, D_MODEL)
    tgt = target.reshape(T, D_MODEL)
    b_st = b_s.T
    g_fin = g_final.reshape(1, D_MODEL)

    proj_g, proj_a, proj_b, h, y_a = _inproj(x2, g_mix, w_in, g_sgu, w_s, b_st, tm, after)
    y_b = _attn_fwd(proj_b, sinks, rel_bias, n_seq, seq)
    w_pa, w_pb, w_out = proj_weights(y_b)
    x1, merged = _merge_fwd(x2, y_a, y_b, proj_g, w_pa, w_pb, w_out, tm)
    w_up, w_down = ffn_weights(x1)
    upre, h2, gate, val = _upproj(x1, g_ffn, w_up, w_conv, b_conv, tf, seq)
    dx2, loss, gg_final = _ffn_down_loss(gate, val, x1, tgt, w_down, g_fin, tm)

    d_gate, d_val, gw_down, gb_g, gb_v = _ffn_bwd_act(gate, val, dx2, w_down, tw)
    gb_conv = jnp.concatenate([gb_g, gb_v], axis=1)
    d_upre, dx1, gg_ffn, gw_conv = _ffn_bwd_up(d_gate, d_val, upre, dx2, x1, g_ffn, w_conv, w_up, tf, seq)
    gw_up = _matmul_tn(h2, d_upre, 2 * D_FF // 4, min(2 * GRAD_ROW_TILE, T), "grad_w_up")
    sent = on_grads("ffn", dict(w_up=gw_up, w_down=gw_down))
    d_g, d_a, d_yb, gw_out, gw_pa, gw_pb, gw_s, gb_st, gg_sgu = _merge_bwd(
        dx1, merged, y_a, y_b, proj_g, proj_a, w_pa, w_pb, w_out, g_sgu, w_s, b_st, tw, sent)
    sent = on_grads("proj", dict(w_pa=gw_pa, w_pb=gw_pb, w_out=gw_out))
    d_b, g_sinks, g_rel = _attn_bwd(proj_b, d_yb, sinks, rel_bias, n_seq, seq, sent)
    gw_in = _grad_w_in(h, d_a, d_b, d_g, min(2 * GRAD_ROW_TILE, T))
    sent = on_grads("in", dict(w_in=gw_in))
    grad_x, gg_mix = _inproj_bwd(d_g, d_a, d_b, x2, dx1, g_mix, w_in, tm, sent)

    small = dict(g_mix=gg_mix, g_sgu=gg_sgu, w_s=gw_s, b_s=gb_st.T, sinks=g_sinks, rel_bias=g_rel,
                 g_ffn=gg_ffn, b_conv=gb_conv, g_final=gg_final, w_conv=gw_conv)
    big = dict(w_in=gw_in, w_pa=gw_pa, w_pb=gw_pb, w_out=gw_out, w_up=gw_up, w_down=gw_down)
    return loss, grad_x.reshape(x.shape), small, big


_MIXER = ("w_in", "w_pa", "w_pb", "w_out")
_FFN = ("w_up", "w_down")
_BIG = _MIXER + _FFN

CONV_ROWS = 6
_SMALL_AT = dict(loss=(0, 1, 1), g_sgu=(4, 1, A_WIDTH), sinks=(5, 1, N_HEADS), b_s=(8, A_GROUPS, CHUNK),
                 b_conv=(12, CONV_ROWS, D_MODEL), w_conv=(18, 3 * CONV_ROWS, D_MODEL),
                 g_final=(36, 1, D_MODEL), g_mix=(37, 1, D_MODEL), g_ffn=(38, 1, D_MODEL),
                 w_s=(40, A_GROUPS * CHUNK * CHUNK // D_MODEL, D_MODEL))
_REL_BIAS_AT = (0, A_WIDTH)
_SMALL_IN_CALL = ("g_final", "g_mix", "g_ffn", "g_sgu", "sinks", "b_s", "b_conv", "rel_bias", "w_s")
SMALL_ROWS = 104


def _pack_small(vals):
    def wide(a):
        return jnp.pad(a, ((0, 0), (0, CONV_ROWS * D_MODEL - a.shape[1]))).reshape(-1, D_MODEL)

    nr = _SMALL_AT["w_s"][1]
    w_s = vals["w_s"].reshape(D_MODEL // CHUNK, nr, CHUNK).transpose(1, 0, 2).reshape(nr, D_MODEL)
    laid = dict(vals, b_conv=wide(vals["b_conv"]), w_conv=wide(vals["w_conv"]), w_s=w_s)
    rows, at = [], 0
    for n, (r0, nr, nc) in _SMALL_AT.items():
        if r0 > at:
            rows.append(jnp.zeros((r0 - at, D_MODEL), F32))
        rows.append(jnp.pad(laid[n].astype(F32).reshape(nr, nc), ((0, 0), (0, D_MODEL - nc))))
        at = r0 + nr
    return lax.dynamic_update_slice(jnp.concatenate(rows, axis=0), vals["rel_bias"].T, _REL_BIAS_AT)


def _unwide(a, r):
    return a.reshape(r, CONV_ROWS * D_MODEL)[:, :2 * D_FF]


def _mesh_pos():
    return lax.axis_index("x"), lax.axis_index("y"), lax.axis_index("c")


def _other_chips(x, y):
    return [(1 - x, y), (x, 1 - y), (1 - x, 1 - y)]


def _remote(src, dst, send_sem, recv_sem, to):
    return pltpu.make_async_remote_copy(src_ref=src, dst_ref=dst, send_sem=send_sem, recv_sem=recv_sem,
                                        device_id=to, device_id_type=MESH)


def _own_slot(own, n, at):
    return lax.dynamic_update_slice(lax.empty((n,) + own.shape, own.dtype), own[None], (at,) + (0,) * own.ndim)


def _allgather_weights(stacks, wc_stack):
    names = list(stacks)
    n = len(names)

    def body(*refs):
        ins, outs = refs[:n + 1], refs[n + 1:2 * n + 2]
        send_sems, recv_sems = refs[2 * n + 2:]
        x, y, c = _mesh_pos()
        _handshake(_chip_peers(x, y, c) + _sibling_peers(x, y, c))
        me = 2 * x + y
        sibling = (x, y, 1 - c)
        chips = _other_chips(x, y)

        def half(ref, chip, hc):
            hr = ref.shape[1] // 2
            return ref.at[chip, pl.ds(hc * hr, hr), :]

        first = []
        for k in range(n):
            first += [_remote(half(ins[k], me, c), half(outs[k], me, c), send_sems.at[6 * k + j], recv_sems.at[6 * k + j], (cx, cy, c))
                      for j, (cx, cy) in enumerate(chips)]
        first += [_remote(ins[n].at[me], outs[n].at[me], send_sems.at[6 * n + j], recv_sems.at[6 * n + j], (cx, cy, c))
                  for j, (cx, cy) in enumerate(chips)]
        for cp in first:
            cp.start()
        passed = []
        for k in range(n):
            for j, (cx, cy) in enumerate(chips):
                landed = half(outs[k], 2 * cx + cy, c)
                _remote(landed, landed, send_sems.at[6 * k + j], recv_sems.at[6 * k + j], (x, y, c)).wait_recv()
                passed.append(_remote(landed, landed, send_sems.at[6 * k + 3 + j], recv_sems.at[6 * k + 3 + j], sibling))
                passed[-1].start()
        for k in range(n):
            for j, (cx, cy) in enumerate(chips):
                theirs = half(outs[k], 2 * cx + cy, 1 - c)
                _remote(theirs, theirs, send_sems.at[6 * k + 3 + j], recv_sems.at[6 * k + 3 + j], (x, y, c)).wait_recv()
        for j, (cx, cy) in enumerate(chips):
            slot = outs[n].at[2 * cx + cy]
            _remote(slot, slot, send_sems.at[6 * n + j], recv_sems.at[6 * n + j], (x, y, c)).wait_recv()
        for cp in first + passed:
            cp.wait_send()

    arrays = [stacks[k] for k in names] + [wc_stack]
    outs = pl.pallas_call(
        body, name="allgather_weights",
        in_specs=[HBM] * (n + 1), out_specs=[HBM] * (n + 1), input_output_aliases={k: k for k in range(n + 1)},
        out_shape=[_sds(a.shape, a.dtype) for a in arrays],
        scratch_shapes=[pltpu.SemaphoreType.DMA((6 * n + 3,)), pltpu.SemaphoreType.DMA((6 * n + 3,))],
        compiler_params=pltpu.CompilerParams(collective_id=_COLLECTIVE["gather_in"]),
    )(*arrays)
    return dict(zip(names, outs[:n])), outs[n]


_KIND = {"w_in": "stack", "w_pa": "col", "w_pb": "col", "w_up": "col", "w_out": "row", "w_down": "row"}


def _half_view(ref, kind, h):
    if kind == "stack":
        k = ref.shape[1] // 2
        return ref.at[:, pl.ds(h * k, k), :]
    if kind == "col":
        k = ref.shape[0] // 2
        return ref.at[pl.ds(h * k, k), :]
    k = ref.shape[1] // 2
    return ref.at[:, pl.ds(h * k, k)]


def _shard_view(ref, kind, i):
    if kind == "stack":
        return ref.at[i]
    if kind == "col":
        k = ref.shape[1] // N_CHIPS
        return ref.at[:, pl.ds(i * k, k)]
    k = ref.shape[0] // N_CHIPS
    return ref.at[pl.ds(i * k, k), :]


def _region_view(ref, kind, h):
    if kind == "row":
        k = ref.shape[1] // 2
        return ref.at[:, pl.ds(h * k, k)]
    k = ref.shape[0] // 2
    return ref.at[pl.ds(h * k, k), :]


def _half_shape(shape, kind):
    if kind == "stack":
        return (shape[0], shape[1] // 2, shape[2])
    return (shape[0] // 2, shape[1]) if kind == "col" else (shape[0], shape[1] // 2)


def _part_shape(half_shape, kind):
    if kind == "stack":
        return tuple(half_shape[1:])
    k, w = half_shape
    return (k, w // N_CHIPS) if kind == "col" else (k // N_CHIPS, w)


_DATAFLOW = pltpu.SideEffectType.DATAFLOW_SIDE_EFFECTING
_TOKEN = (SUBLANES, LANES)


_COLLECTIVE = {k: i for i, k in enumerate(
    [kind + "_" + g for kind in ("pair", "chip", "share") for g in ("ffn", "proj", "in")]
    + ["gather_proj", "gather_ffn", "gather_in", "forward_proj", "forward_ffn"])}


def _sibling_peers(x, y, c):
    return [(x, y, 1 - c)]


def _chip_peers(x, y, c):
    return [(cx, cy, c) for cx, cy in _other_chips(x, y)]


def _handshake(peers):
    barrier = pltpu.get_barrier_semaphore()
    for peer in peers:
        pl.semaphore_signal(barrier, inc=1, device_id=peer, device_id_type=MESH)
    pl.semaphore_wait(barrier, len(peers))


def _split_start(name, arrays, n_sems, issue, after=None, handshake=None):
    n = len(arrays)
    order = [] if after is None else [after]

    def body(*refs):
        base = n + len(order)
        if handshake is not None:
            _handshake(handshake[1](*_mesh_pos()))
        issue(refs[:n], refs[base], refs[base + 1])
        refs[-1][...] = jnp.zeros(_TOKEN, F32)

    params = dict(has_side_effects=_DATAFLOW)
    if handshake is not None:
        params["collective_id"] = handshake[0]
    outs = pl.pallas_call(
        body, name=name,
        in_specs=[HBM] * n + [ANY] * len(order), out_specs=[SEM, SEM] + [HBM] * n + [pl.BlockSpec(memory_space=pltpu.VMEM)],
        out_shape=[pltpu.SemaphoreType.DMA((n_sems,)), pltpu.SemaphoreType.DMA((n_sems,))]
        + [pltpu.HBM(a.shape, a.dtype) for a in arrays] + [_sds(_TOKEN, F32)],
        input_output_aliases={k: 2 + k for k in range(n)},
        compiler_params=pltpu.CompilerParams(**params),
    )(*[pltpu.with_memory_space_constraint(a, pltpu.HBM) for a in arrays], *order)
    return outs[0], outs[1], list(outs[2:2 + n]), outs[-1]


def _split_wait(name, started, waits, after):
    send_sems, recv_sems, arrays, _ = started
    n = len(arrays)

    def body(*refs):
        waits(refs[:n], refs[n], refs[n + 1])

    return pl.pallas_call(
        body, name=name,
        in_specs=[HBM] * n + [SEM, SEM, ANY], out_specs=[HBM] * n,
        out_shape=[pltpu.HBM(a.shape, a.dtype) for a in arrays],
        input_output_aliases={k: k for k in range(n)},
        compiler_params=pltpu.CompilerParams(has_side_effects=_DATAFLOW),
    )(*arrays, send_sems, recv_sems, after)


def _wait_both(src, dst, send_sem, recv_sem):
    x, y, c = _mesh_pos()
    cp = _remote(src, dst, send_sem, recv_sem, (x, y, c))
    cp.wait_send()
    cp.wait_recv()


def _pair_exchange_start(parts, tag, after):
    names = list(parts)
    n = len(names)
    lands = [lax.empty(_half_shape(parts[k].shape, _KIND[k]), parts[k].dtype) for k in names]

    def issue(refs, send_sems, recv_sems):
        x, y, c = _mesh_pos()
        for hc in range(2):
            @pl.when(c == hc)
            def _():
                for k in range(n):
                    _remote(_half_view(refs[k], _KIND[names[k]], 1 - hc), refs[n + k], send_sems.at[k], recv_sems.at[k],
                            (x, y, 1 - c)).start()

    return names, _split_start("grad_pair_exchange_start_" + tag, [parts[k] for k in names] + lands, n, issue, after,
                               (_COLLECTIVE["pair_" + tag], _sibling_peers))


def _pair_exchange_wait(pending, tag, after):
    names, started = pending
    n = len(names)

    def waits(refs, send_sems, recv_sems):
        for k in range(n):
            _wait_both(_half_view(refs[k], _KIND[names[k]], 0), refs[n + k], send_sems.at[k], recv_sems.at[k])

    outs = _split_wait("grad_pair_exchange_wait_" + tag, started, waits, after)
    return dict(zip(names, outs[:n])), dict(zip(names, outs[n:]))


def _half_blocks(shape, kind):
    if kind == "stack":
        _, k, w = shape
        return (N_CHIPS // 2, 1), (2, k // 2, w), (lambda i, r, s: (i, r, 0)), (lambda i, r, s: (i, s[1] + r, 0))
    k, w = shape
    if kind == "col":
        tr = STREAM_ROWS
        nb = k // 2 // tr
        return (nb,), (tr, w), (lambda r, s: (r, 0)), (lambda r, s: (s[1] * nb + r, 0))
    nb = 2
    return (nb,), (k // nb, w // 2), (lambda r, s: (r, 0)), (lambda r, s: (r, s[1]))


def _pair_add(part, from_sibling, name, pos):
    kind = _KIND[name]
    grid, block, half_map, full_map = _half_blocks(part.shape, kind)

    def body(s_ref, p_ref, q_ref, o_ref):
        o_ref[...] = (p_ref[...].astype(F32) + q_ref[...].astype(F32)).astype(BF16)

    return pl.pallas_call(
        body, name="grad_pair_add_" + name,
        grid_spec=pltpu.PrefetchScalarGridSpec(
            num_scalar_prefetch=1, grid=grid,
            in_specs=[pl.BlockSpec(block, full_map), pl.BlockSpec(block, half_map)],
            out_specs=pl.BlockSpec(block, half_map)),
        out_shape=_sds(from_sibling.shape, BF16),
        compiler_params=_cp(("arbitrary",) * len(grid), 40),
    )(pos, *_hbm(part, from_sibling))


def _chip_exchange_start(sums, tag, after):
    names = list(sums)
    n = len(names)
    lands = [lax.empty((3,) + _part_shape(sums[k].shape, _KIND[k]), sums[k].dtype) for k in names]

    def issue(refs, send_sems, recv_sems):
        x, y, c = _mesh_pos()
        me = 2 * x + y
        for i in range(N_CHIPS):
            xi, yi = i // 2, i % 2
            j = jnp.where(xi != x, jnp.where(yi != y, 2, 0), 1)

            @pl.when(i != me)
            def _():
                for k in range(n):
                    _remote(_shard_view(refs[k], _KIND[names[k]], i), refs[n + k].at[j], send_sems.at[3 * k + j],
                            recv_sems.at[3 * k + j], (xi, yi, c)).start()

    return names, _split_start("grad_chip_exchange_start_" + tag, [sums[k] for k in names] + lands, 3 * n, issue, after,
                               (_COLLECTIVE["chip_" + tag], _chip_peers))


def _chip_exchange_wait(pending, tag, after):
    names, started = pending
    n = len(names)

    def waits(refs, send_sems, recv_sems):
        for k in range(n):
            for j in range(3):
                _wait_both(_shard_view(refs[k], _KIND[names[k]], 0), refs[n + k].at[j], send_sems.at[3 * k + j], recv_sems.at[3 * k + j])

    return dict(zip(names, _split_wait("grad_chip_exchange_wait_" + tag, started, waits, after)[n:]))


def _allgather_start(stacks, tag, after):
    names = list(stacks)

    def issue(refs, send_sems, recv_sems):
        x, y, c = _mesh_pos()
        me = 2 * x + y
        for k, st in enumerate(refs):
            hr = st.shape[1] // 2
            mine = st.at[me, pl.ds(c * hr, hr), :]
            for j, (cx, cy) in enumerate(_other_chips(x, y)):
                _remote(mine, mine, send_sems.at[3 * k + j], recv_sems.at[3 * k + j], (cx, cy, c)).start()

    return names, _split_start("allgather_start_" + tag, [stacks[k] for k in names], 3 * len(names), issue, after,
                               (_COLLECTIVE["gather_" + tag], _chip_peers))


def _allgather_wait(pending, tag, after):
    names, started = pending

    def waits(refs, send_sems, recv_sems):
        for k, st in enumerate(refs):
            slot = st.at[0, pl.ds(0, st.shape[1] // 2), :]
            for j in range(3):
                _wait_both(slot, slot, send_sems.at[3 * k + j], recv_sems.at[3 * k + j])

    return dict(zip(names, _split_wait("allgather_wait_" + tag, started, waits, after)))


def _allgather_forward(stacks, tag):
    names = list(stacks)
    n = len(names)

    def body(*refs):
        ins, outs = refs[:n], refs[n:2 * n]
        send_sems, recv_sems = refs[2 * n:]
        x, y, c = _mesh_pos()
        _handshake(_sibling_peers(x, y, c))
        copies = []
        for k in range(n):
            hr = ins[k].shape[1] // 2
            for j, (cx, cy) in enumerate(_other_chips(x, y)):
                chip = 2 * cx + cy
                copies.append(_remote(ins[k].at[chip, pl.ds(c * hr, hr), :], outs[k].at[chip, pl.ds(c * hr, hr), :],
                                      send_sems.at[3 * k + j], recv_sems.at[3 * k + j], (x, y, 1 - c)))
        for cp in copies:
            cp.start()
        for cp in copies:
            cp.wait()

    arrays = [stacks[k] for k in names]
    outs = pl.pallas_call(
        body, name="allgather_forward_" + tag, in_specs=[HBM] * n, out_specs=[HBM] * n,
        input_output_aliases={k: k for k in range(n)},
        out_shape=[_sds(a.shape, a.dtype) for a in arrays],
        scratch_shapes=[pltpu.SemaphoreType.DMA((3 * n,)), pltpu.SemaphoreType.DMA((3 * n,))],
        compiler_params=pltpu.CompilerParams(collective_id=_COLLECTIVE["forward_" + tag]),
    )(*arrays)
    return dict(zip(names, outs))


def _owner_sum(part, from_sibling, from_chips, name, pos, shard_shape):
    kind = _KIND[name]
    _, pk, pw = from_chips.shape
    if kind == "row":
        nb = 1
        tr = pk // nb
        p_spec = pl.BlockSpec((tr, pw), lambda r, s: (s[0] * nb + r, s[1]))
        q_spec = pl.BlockSpec((tr, pw), lambda r, s: (s[0] * nb + r, 0))
        o_spec = pl.BlockSpec((tr, pw), lambda r, s: (r, s[1]))
    else:
        tr = STREAM_ROWS
        nb = pk // tr
        if kind == "stack":
            p_spec = pl.BlockSpec((None, tr, pw), lambda r, s: (s[0], s[1] * nb + r, 0))
            q_spec = pl.BlockSpec((None, tr, pw), lambda r, s: (s[0], r, 0))
        else:
            p_spec = pl.BlockSpec((tr, pw), lambda r, s: (s[1] * nb + r, s[0]))
            q_spec = pl.BlockSpec((tr, pw), lambda r, s: (r, s[0]))
        o_spec = pl.BlockSpec((tr, pw), lambda r, s: (s[1] * nb + r, 0))

    def body(s_ref, p_ref, q_ref, r_ref, o_ref):
        acc = p_ref[...].astype(F32) + q_ref[...].astype(F32)
        for j in range(3):
            acc = acc + r_ref[j].astype(F32)
        o_ref[...] = acc

    return pl.pallas_call(
        body, name="grad_owner_sum_" + name,
        grid_spec=pltpu.PrefetchScalarGridSpec(
            num_scalar_prefetch=1, grid=(nb,),
            in_specs=[p_spec, q_spec, pl.BlockSpec((3, tr, pw), lambda r, s: (0, r, 0))],
            out_specs=o_spec),
        out_shape=_sds(shard_shape, F32),
        compiler_params=_cp(("arbitrary",), 32),
    )(pos, *_hbm(part, from_sibling, from_chips))


def _pair_share_start(shards, tag, after):
    names = list(shards)

    def issue(refs, send_sems, recv_sems):
        x, y, c = _mesh_pos()
        for hc in range(2):
            @pl.when(c == hc)
            def _():
                for k, g in enumerate(refs):
                    mine = _region_view(g, _KIND[names[k]], hc)
                    _remote(mine, mine, send_sems.at[k], recv_sems.at[k], (x, y, 1 - c)).start()

    return names, _split_start("grad_pair_share_start_" + tag, [shards[k] for k in names], len(names), issue, after,
                               (_COLLECTIVE["share_" + tag], _sibling_peers))


def _pair_share_wait(pending, tag, after):
    names, started = pending

    def waits(refs, send_sems, recv_sems):
        for k, g in enumerate(refs):
            region = _region_view(g, _KIND[names[k]], 0)
            _wait_both(region, region, send_sems.at[k], recv_sems.at[k])

    return dict(zip(names, _split_wait("grad_pair_share_wait_" + tag, started, waits, after)))


def _small_exchange_start(slots, after):
    def issue(refs, send_sems, recv_sems):
        x, y, c = _mesh_pos()
        mine = refs[0].at[4 * x + 2 * y + c]
        k = 0
        for px in range(2):
            for py in range(2):
                for pc in range(2):
                    if px + py + pc:
                        peer = (1 - x if px else x, 1 - y if py else y, 1 - c if pc else c)
                        _remote(mine, mine, send_sems.at[k], recv_sems.at[k], peer).start()
                        k += 1

    return _split_start("small_exchange_start", [slots], N_DEV - 1, issue, after)


def _small_exchange_wait(started, after):
    def waits(refs, send_sems, recv_sems):
        slot = refs[0].at[0]
        for k in range(N_DEV - 1):
            _wait_both(slot, slot, send_sems.at[k], recv_sems.at[k])

    return _split_wait("small_exchange_wait", started, waits, after)[0]


def _adam_math(w, g, m, v):
    m = ADAM_B1 * m + (1.0 - ADAM_B1) * g
    v = ADAM_B2 * v + (1.0 - ADAM_B2) * (g * g)
    m_hat = m / (1.0 - ADAM_B1 ** ADAM_STEP)
    v_hat = v / (1.0 - ADAM_B2 ** ADAM_STEP)
    delta = -ADAM_LR * (m_hat / (jnp.sqrt(v_hat) + ADAM_EPS) + ADAM_WD * w)
    return delta, m, v


def _adamw(w, g, m, v, name):
    rows, cols = w.shape[0], w.shape[-1]
    fits = [t for t in range(SUBLANES, rows, SUBLANES) if rows % t == 0 and t * cols * 4 <= (3 << 19)]
    tr = max(fits) if fits and w.ndim == 2 else rows

    def body(w_ref, g_ref, m_ref, v_ref, d_ref, nm_ref, nv_ref, go_ref):
        g = g_ref[...]
        d, nm, nv = _adam_math(w_ref[...], g, m_ref[...], v_ref[...])
        d_ref[...] = d
        nm_ref[...] = nm
        nv_ref[...] = nv
        go_ref[...] = g

    spec = pl.BlockSpec((tr,) + w.shape[1:], lambda i: (i,) + (0,) * (w.ndim - 1))
    return pl.pallas_call(
        body, name=name, grid=(rows // tr,), in_specs=[spec] * 4, out_specs=[spec] * 4,
        out_shape=[_sds(w.shape, F32)] * 4, compiler_params=_cp(("arbitrary",)),
    )(*_hbm(w, g, m, v))


def _small_sum_adamw(gathered, w, m, v):
    names = _SMALL_IN_CALL
    n = len(names)

    def body(*refs):
        a_ref = refs[0]
        w_refs, m_refs, v_refs = refs[1:1 + n], refs[1 + n:1 + 2 * n], refs[1 + 2 * n:1 + 3 * n]
        sum_ref, loss_ref = refs[1 + 3 * n], refs[2 + 3 * n]
        outs = refs[3 + 3 * n:]
        g = a_ref[0]
        for k in range(1, N_DEV):
            g = g + a_ref[k]
        sum_ref[...] = g
        loss_ref[...] = g[0:1, 0:1]
        for i, name in enumerate(names):
            if name == "rel_bias":
                r0, c0 = _REL_BIAS_AT
                pieces = [(slice(None), g[r0:r0 + N_HEADS, c0:c0 + N_BUCKETS])]
            elif name == "b_conv":
                r0 = _SMALL_AT[name][0]
                pieces = [(slice(None), jnp.concatenate([g[r0 + k:r0 + k + 1, :] for k in range(CONV_ROWS)], axis=1)[:, :2 * D_FF])]
            elif name == "w_s":
                r0, nr, _ = _SMALL_AT[name]
                pieces = [(slice(nr * j, nr * (j + 1)), g[r0:r0 + nr, CHUNK * j:CHUNK * (j + 1)]) for j in range(D_MODEL // CHUNK)]
            else:
                r0, nr, nc = _SMALL_AT[name]
                pieces = [(slice(None), g[r0:r0 + nr, 0:nc])]
            for at, gp in pieces:
                d, nm, nv = _adam_math(w_refs[i][at], gp, m_refs[i][at], v_refs[i][at])
                for k, val in enumerate((gp, d, nm, nv)):
                    outs[4 * i + k][at] = val

    shapes = [w[k].shape for k in names]
    res = pl.pallas_call(
        body, name="small_sum_adamw",
        out_shape=[_sds((SMALL_ROWS, D_MODEL), F32), _sds((1, 1), F32)] + [_sds(s, F32) for s in shapes for _ in range(4)],
    )(gathered, *[w[k] for k in names], *[m[k] for k in names], *[v[k] for k in names])
    return res[0], res[1], {k: tuple(res[2 + 4 * i:6 + 4 * i]) for i, k in enumerate(names)}


_NAMES = ("g_mix", "w_in", "g_sgu", "w_s", "b_s", "sinks", "rel_bias", "w_pa", "w_pb", "w_out",
          "g_ffn", "w_up", "w_conv", "b_conv", "w_down", "g_final")

def kernel(x, g_mix, w_in, g_sgu, w_s, b_s, sinks, rel_bias, w_pa, w_pb, w_out, g_ffn, w_up, w_conv, b_conv, w_down, g_final, loss_target, m_g_mix, m_w_in, m_g_sgu, m_w_s, m_b_s, m_sinks, m_rel_bias, m_w_pa, m_w_pb, m_w_out, m_g_ffn, m_w_up, m_w_conv, m_b_conv, m_w_down, m_g_final, v_g_mix, v_w_in, v_g_sgu, v_w_s, v_b_s, v_sinks, v_rel_bias, v_w_pa, v_w_pb, v_w_out, v_g_ffn, v_w_up, v_w_conv, v_b_conv, v_w_down, v_g_final):
    w = dict(g_mix=g_mix, w_in=w_in, g_sgu=g_sgu, w_s=w_s, b_s=b_s, sinks=sinks, rel_bias=rel_bias, w_pa=w_pa, w_pb=w_pb,
             w_out=w_out, g_ffn=g_ffn, w_up=w_up, w_conv=w_conv, b_conv=b_conv, w_down=w_down, g_final=g_final)
    m = dict(g_mix=m_g_mix, w_in=m_w_in, g_sgu=m_g_sgu, w_s=m_w_s, b_s=m_b_s, sinks=m_sinks, rel_bias=m_rel_bias, w_pa=m_w_pa,
             w_pb=m_w_pb, w_out=m_w_out, g_ffn=m_g_ffn, w_up=m_w_up, w_conv=m_w_conv, b_conv=m_b_conv, w_down=m_w_down,
             g_final=m_g_final)
    v = dict(g_mix=v_g_mix, w_in=v_w_in, g_sgu=v_g_sgu, w_s=v_w_s, b_s=v_b_s, sinks=v_sinks, rel_bias=v_rel_bias, w_pa=v_w_pa,
             w_pb=v_w_pb, w_out=v_w_out, g_ffn=v_g_ffn, w_up=v_w_up, w_conv=v_w_conv, b_conv=v_b_conv, w_down=v_w_down,
             g_final=v_g_final)
    xi, yi, ci = _mesh_pos()
    me = 2 * xi + yi

    shard = {n: w[n][0] for n in _BIG}
    shard_shapes = {n: shard[n].shape for n in _BIG}
    wc_shard = w["w_conv"][0]
    wc_pad = jnp.pad(wc_shard, ((0, 5), (0, 0)))
    own = {n: _own_slot(shard[n].astype(BF16), N_CHIPS, me) for n in _BIG}
    stacks, wc_all = _allgather_weights({"w_in": own["w_in"]}, _own_slot(wc_pad, N_CHIPS, me))
    proj_gather = _allgather_start({n: own[n] for n in _MIXER[1:]}, "proj", stacks["w_in"])
    ffn_gather = _allgather_start({n: own[n] for n in _FFN}, "ffn", proj_gather[1][-1])
    w_conv_full = jnp.concatenate([wc_all[i, :3] for i in range(N_CHIPS)], axis=1)
    w_in_full = _unstack_w_in(stacks["w_in"])
    pos = jnp.stack([me, ci])

    def proj_weights(done):
        st = _allgather_forward(_allgather_wait(proj_gather, "proj", done), "proj")
        return st["w_pa"], st["w_pb"], st["w_out"].reshape(D_MODEL, D_MODEL)

    def ffn_weights(done):
        st = _allgather_forward(_allgather_wait(ffn_gather, "ffn", done), "ffn")
        return st["w_up"], st["w_down"].reshape(D_FF, D_MODEL)

    groups = {}

    def stage1(group, parts):
        groups[group] = dict(parts=parts, pair=_pair_exchange_start(parts, group, None))
        return groups[group]["pair"][1][-1]

    def stage2(group, after, order_after):
        g = groups[group]
        g["parts"], g["sib"] = _pair_exchange_wait(g["pair"], group, after)
        g["chip"] = _chip_exchange_start({n: _pair_add(g["parts"][n], g["sib"][n], n, pos) for n in g["parts"]}, group, order_after)
        return g["chip"][1][-1]

    def stage3(group, after, order_after):
        g = groups[group]
        got = _chip_exchange_wait(g["chip"], group, after)
        g["share"] = _pair_share_start(
            {n: _owner_sum(g["parts"][n], g["sib"][n], got[n], n, pos, shard_shapes[n]) for n in g["parts"]}, group, order_after)
        return g["share"][1][-1]

    grads, deltas, new_m, new_v = {}, {}, {}, {}

    def stage4(group, after):
        g_shard = _pair_share_wait(groups[group]["share"], group, after)
        last = None
        for n in g_shard:
            g = _tie(g_shard[n], last)
            if n == "w_in":
                d, nm, nv, gt = _adamw(shard[n].T, g.T, m[n][0].T, v[n][0].T, "adamw_" + n)
                grads[n], deltas[n], new_m[n], new_v[n] = gt.T[None], d.T[None], nm.T[None], nv.T[None]
            else:
                d, nm, nv, go = _adamw(shard[n], g, m[n][0], v[n][0], "adamw_" + n)
                grads[n], deltas[n], new_m[n], new_v[n] = go[None], d[None], nm[None], nv[None]
            last = nv
        return last

    def on_grads(group, parts):
        token = stage1(group, parts)
        some = next(iter(parts.values()))
        if group == "proj":
            token = stage2("ffn", some, token)
        if group == "in":
            token = stage2("proj", some, token)
            token = stage3("ffn", some, token)
            token = stage2("in", token, token)
        return token

    loss, grad_x, small, big = _local_step(
        x, loss_target, w["g_mix"], w["g_sgu"], w["w_s"][0], w["b_s"][0], w["sinks"], w["rel_bias"], w["g_ffn"],
        w["b_conv"], w["g_final"], w_in_full, w_conv_full, proj_weights, ffn_weights, on_grads, ffn_gather[1][-1])

    small["loss"] = loss
    small_gather = _small_exchange_start(_own_slot(_pack_small(small), N_DEV, 2 * me + ci), grad_x)
    token = stage3("proj", grad_x, small_gather[-1])
    done = stage4("ffn", token)
    done = stage4("proj", done)
    token = stage3("in", done, None)
    all_small = _small_exchange_wait(small_gather, token)
    two_d = {n: (lambda a, n=n: a.reshape(_SMALL_AT[n][1:])) for n in _SMALL_IN_CALL}
    two_d["rel_bias"] = lambda a: a.T
    two_d["b_conv"] = lambda a: a
    two_d["w_s"] = lambda a: a.reshape(A_GROUPS * CHUNK, CHUNK)
    s_sum, s_loss, s_out = _small_sum_adamw(all_small, *[{n: two_d[n](p[n]) for n in _SMALL_IN_CALL} for p in (w, m, v)])
    stage4("in", all_small)
    for n in _SMALL_IN_CALL:
        back = (lambda a: a.T) if n == "rel_bias" else (lambda a, n=n: a.reshape(w[n].shape))
        grads[n], deltas[n], new_m[n], new_v[n] = [back(a) for a in s_out[n]]

    def rows(n):
        r0, nr, _ = _SMALL_AT[n]
        return s_sum[r0:r0 + nr]

    wcols = wc_shard.shape[1]
    g_wc = lax.dynamic_slice(_unwide(rows("w_conv"), 3), (0, me * wcols), (3, wcols))
    taps = lambda a: a.transpose(1, 0, 2)
    res = _adamw(taps(w["w_conv"]), g_wc[:, None, :], taps(m["w_conv"]), taps(v["w_conv"]), "adamw_w_conv")
    deltas["w_conv"], new_m["w_conv"], new_v["w_conv"], grads["w_conv"] = [taps(a) for a in res]

    return (s_loss.reshape(()), grad_x, *[grads[n] for n in _NAMES], *[deltas[n] for n in _NAMES],
            *[new_m[n] for n in _NAMES], *[new_v[n] for n in _NAMES])
```

```python
import functools

import numpy as np
import jax
import jax.numpy as jnp
from jax import lax
from jax.experimental import pallas as pl
from jax.experimental.pallas import tpu as pltpu

F32 = jnp.float32
BF16 = jnp.bfloat16

D_MODEL = 1024
CHUNK = 128
A_GROUPS = 4
A_WIDTH = 512
N_HEADS = 8
HEAD_DIM = 64
Q_DIM = 512
KV_DIM = 128
N_BUCKETS = 32
MAX_DISTANCE = 128
D_FF = 2816
EPS = 1e-6
NEG_INF = -1e30
G_DIM = 2 * D_MODEL
A_DIM = 2 * A_WIDTH
B_DIM = Q_DIM + 2 * KV_DIM
LANES = 128
SUBLANES = 8
ROW_TILE = 512
WIDE_ROW_TILE = 256
COL_CHUNK = 512
GRAD_ROW_TILE = 512
STREAM_ROWS = 256
BF16_ROWS = 16
N_CHIPS = 4
N_DEV = 8

ADAM_LR = 0.001
ADAM_B1 = 0.9
ADAM_B2 = 0.999
ADAM_EPS = 1e-08
ADAM_WD = 0.01
ADAM_STEP = 10

MESH = pl.DeviceIdType.MESH
_GELU_C = 0.7978845608028654
_GELU_A = 0.044715


def _cp(sem=None, vmem_mb=None):
    kw = {}
    if sem is not None:
        kw["dimension_semantics"] = sem
    if vmem_mb is not None:
        kw["vmem_limit_bytes"] = vmem_mb << 20
    return pltpu.CompilerParams(**kw)


def _dot(a, b):
    return jnp.dot(a, b, preferred_element_type=F32)


def _dot_nt(a, b):
    return lax.dot_general(a, b, (((1,), (1,)), ((), ())), preferred_element_type=F32)


def _dot_tn(a, b):
    return lax.dot_general(a, b, (((0,), (0,)), ((), ())), preferred_element_type=F32)


def _rms_r(x):
    return lax.rsqrt(jnp.mean(x * x, axis=-1, keepdims=True) + EPS)


def _rms_bwd(dh, n, r, g):
    dn = dh * g
    return r * (dn - n * jnp.mean(dn * n, axis=-1, keepdims=True))


def _gelu(x):
    t = jnp.tanh(_GELU_C * (x + _GELU_A * (x * x * x)))
    return 0.5 * x * (1.0 + t), t


def _gelu_grad(x, t):
    return 0.5 * (1.0 + t) + 0.5 * x * (1.0 - t * t) * (_GELU_C * (1.0 + 3.0 * _GELU_A * x * x))


def _sigmoid(x):
    return 1.0 / (1.0 + jnp.exp(-x))


def _tie(x, dep):
    return x if dep is None else lax.optimization_barrier((x, dep))[0]


def _row(tm, w):
    return pl.BlockSpec((tm, w), lambda i: (i, 0))


def _full(shape):
    nd = len(shape)
    return pl.BlockSpec(tuple(shape), lambda *_: (0,) * nd)


def _resident(shape):
    nd = len(shape)
    return pl.BlockSpec(tuple(shape), lambda *_: (0,) * nd, pipeline_mode=pl.Buffered(1))


def _sds(shape, dtype):
    return pltpu.HBM(tuple(shape), dtype)


def _hbm(*arrays):
    return [pltpu.with_memory_space_constraint(a, pltpu.HBM) for a in arrays]


HBM = pl.BlockSpec(memory_space=pltpu.HBM)
ANY = pl.BlockSpec(memory_space=pl.ANY)
SEM = pl.BlockSpec(memory_space=pltpu.SEMAPHORE)


def _band_buckets():
    i = np.arange(CHUNK)[:, None]
    j = np.arange(2 * CHUNK)[None, :]
    dist = i + CHUNK - j
    valid = (dist >= 0) & (dist < CHUNK)
    d = np.clip(dist, 0, None)
    max_exact = N_BUCKETS // 2
    large = max_exact + (np.log(np.maximum(d, 1) / max_exact) / np.log(MAX_DISTANCE / max_exact)
                         * (N_BUCKETS - max_exact)).astype(np.int32)
    large = np.minimum(large, N_BUCKETS - 1)
    buckets = np.where(d < max_exact, d, large).astype(np.int32)
    return np.where(valid, buckets, -1).astype(np.int32)


_A_COLS = slice(0, A_DIM)
_B_COLS = slice(A_DIM, A_DIM + B_DIM)
_G_COLS = slice(A_DIM + B_DIM, A_DIM + B_DIM + G_DIM)


def _inproj(x2, g_mix, w_in, g_sgu, w_s, b_st, tm, after=None):
    T = x2.shape[0]
    order = [] if after is None else [after]

    def body(*refs):
        x_ref, g_ref, w_ref, gs_ref, ws_ref, bs_ref = refs[:6]
        pg_ref, pa_ref, pb_ref, h_ref, ya_ref = refs[6 + len(order):]
        x = x_ref[...]
        h = (x * _rms_r(x) * g_ref[...]).astype(BF16)
        h_ref[...] = h
        pa = _dot_nt(h, w_ref[_A_COLS, :]).astype(BF16)
        pa_ref[...] = pa
        pb_ref[...] = _dot_nt(h, w_ref[_B_COLS, :]).astype(BF16)
        pg_ref[...] = _dot_nt(h, w_ref[_G_COLS, :]).astype(BF16)
        _sgu_apply(pa.astype(F32), gs_ref[...], ws_ref, bs_ref, ya_ref)

    return pl.pallas_call(
        body, name="inproj", grid=(T // tm,),
        in_specs=[_row(tm, D_MODEL), _full(g_mix.shape), _resident(w_in.shape), _full(g_sgu.shape), _full(w_s.shape),
                  _full(b_st.shape)] + [ANY] * len(order),
        out_specs=[_row(tm, G_DIM), _row(tm, A_DIM), _row(tm, B_DIM), _row(tm, D_MODEL), _row(tm, A_WIDTH)],
        out_shape=[_sds((T, G_DIM), BF16), _sds((T, A_DIM), BF16), _sds((T, B_DIM), BF16), _sds((T, D_MODEL), BF16),
                   _sds((T, A_WIDTH), BF16)],
        compiler_params=_cp(("arbitrary",), 48),
    )(*_hbm(x2, g_mix, w_in, g_sgu, w_s, b_st), *order)


def _sgu_parts(p, g):
    pu = p[:, :A_WIDTH]
    pv = p[:, A_WIDTH:]
    u, tu = _gelu(pu)
    vv, tv = _gelu(pv)
    rv = _rms_r(vv)
    vn = (vv * rv * g).astype(BF16)
    return pu, pv, u, tu, vv, tv, rv, vn


def _tril():
    r = lax.broadcasted_iota(jnp.int32, (CHUNK, CHUNK), 0)
    c = lax.broadcasted_iota(jnp.int32, (CHUNK, CHUNK), 1)
    return r >= c


def _sgu_apply(p, g, ws_ref, bs_ref, y_ref):
    tril = _tril()
    _, _, u, _, _, _, _, vn = _sgu_parts(p, g)
    for gi in range(A_GROUPS):
        wm = jnp.where(tril, ws_ref[gi], 0.0).astype(BF16)
        bcol = bs_ref[:, gi:gi + 1]
        cs = slice(gi * CHUNK, (gi + 1) * CHUNK)
        for c in range(p.shape[0] // CHUNK):
            rs = slice(c * CHUNK, (c + 1) * CHUNK)
            s = _dot(wm, vn[rs, cs]) + bcol
            y_ref[rs, cs] = (u[rs, cs] * s).astype(BF16)


HEAD_ROWS = N_HEADS * CHUNK


def _head_rows(h):
    return slice(h * CHUNK, (h + 1) * CHUNK)


def _attn_setup(bias_scr, sink_scr, kvar_scr, qkv_ref, bk_ref, rel_ref, sink_ref):
    @pl.when(pl.program_id(0) == 0)
    def _():
        bk = bk_ref[...]
        for h in range(N_HEADS):
            acc = jnp.full((CHUNK, 2 * CHUNK), NEG_INF, F32)
            for b in range(N_BUCKETS):
                acc = jnp.where(bk == b, rel_ref[b, h], acc)
            bias_scr[_head_rows(h), :] = acc
            sink_scr[_head_rows(h), :] = jnp.full((CHUNK, LANES), sink_ref[0, h], F32)

    seq = qkv_ref.shape[0]
    rows_per = 2 * CHUNK
    for is_v in range(2):
        c0 = Q_DIM + is_v * KV_DIM
        for r in range(seq // rows_per):
            rs = slice(r * rows_per, (r + 1) * rows_per)
            a = qkv_ref[rs, c0:c0 + KV_DIM].astype(F32)
            lane = lax.broadcasted_iota(jnp.int32, a.shape, 1)
            lo = jnp.where(lane < HEAD_DIM, a, 0.0)
            hi = jnp.where(lane >= HEAD_DIM, a, 0.0)
            kvar_scr[4 * is_v + 0, rs, :] = lo.astype(BF16)
            kvar_scr[4 * is_v + 1, rs, :] = pltpu.roll(lo, HEAD_DIM, 1).astype(BF16)
            kvar_scr[4 * is_v + 2, rs, :] = pltpu.roll(hi, HEAD_DIM, 1).astype(BF16)
            kvar_scr[4 * is_v + 3, rs, :] = hi.astype(BF16)


def _rowsum(a, ones):
    hi = a.astype(BF16)
    lo = (a - hi.astype(F32)).astype(BF16)
    return _dot(hi, ones) + _dot(lo, ones)


def _both(a):
    return jnp.concatenate([a, a], axis=1)


def _attn_probs(qkv_ref, r0, n, kv, bias_scr, sink_scr, ones):
    s = jnp.concatenate([_dot_nt(qkv_ref[pl.ds(r0, CHUNK), (h // 2) * LANES:(h // 2 + 1) * LANES], kv[h // 4][h % 2])
                         for h in range(N_HEADS)], axis=0)
    s = s * (HEAD_DIM ** -0.5) + bias_scr[...]
    col = lax.broadcasted_iota(jnp.int32, s.shape, 1)
    s = jnp.where((col < CHUNK) & (n == 0), NEG_INF, s)
    sink = sink_scr[...]
    m = jnp.maximum(jnp.max(s, axis=-1, keepdims=True), sink)
    p = jnp.exp(s - _both(m))
    es = jnp.exp(sink - m)
    inv = 1.0 / (_dot(p.astype(BF16), ones) + es)
    return p * _both(inv), es * inv


def _attn_block_inputs(kvar_scr, n):
    r0 = pl.multiple_of(n * CHUNK, CHUNK)
    rp = pl.multiple_of(jnp.maximum(n - 1, 0) * CHUNK, CHUNK)

    def both(idx):
        return jnp.concatenate([kvar_scr[idx, pl.ds(rp, CHUNK), :], kvar_scr[idx, pl.ds(r0, CHUNK), :]], axis=0)

    kv = ((both(0), both(1)), (both(2), both(3)))
    vv = ((both(4), both(5)), (both(6), both(7)))
    return r0, kv, vv


def _attn_fwd(proj_b, sinks, rel_bias, n_seq, seq):
    nb = seq // CHUNK
    bk = jnp.asarray(_band_buckets())

    def body(qkv_ref, bk_ref, rel_ref, sink_ref, o_ref, bias_scr, sink_scr, kvar_scr):
        _attn_setup(bias_scr, sink_scr, kvar_scr, qkv_ref, bk_ref, rel_ref, sink_ref)
        ones = jnp.ones((2 * CHUNK, LANES), BF16)

        def blk(n, carry):
            r0, kv, vv = _attn_block_inputs(kvar_scr, n)
            prob, _ = _attn_probs(qkv_ref, r0, n, kv, bias_scr, sink_scr, ones)
            pb = prob.astype(BF16)
            for pr in range(N_HEADS // 2):
                acc = _dot(pb[_head_rows(2 * pr)], vv[pr // 2][0]) + _dot(pb[_head_rows(2 * pr + 1)], vv[pr // 2][1])
                o_ref[pl.ds(r0, CHUNK), pr * LANES:(pr + 1) * LANES] = acc.astype(BF16)
            return carry

        lax.fori_loop(0, nb, blk, 0)

    smem = pl.BlockSpec(memory_space=pltpu.SMEM)
    return pl.pallas_call(
        body, name="attn_fwd", grid=(n_seq,),
        in_specs=[_row(seq, B_DIM), _full(bk.shape), smem, smem],
        out_specs=_row(seq, Q_DIM), out_shape=_sds((n_seq * seq, Q_DIM), BF16),
        scratch_shapes=[pltpu.VMEM((HEAD_ROWS, 2 * CHUNK), F32), pltpu.VMEM((HEAD_ROWS, LANES), F32),
                        pltpu.VMEM((8, seq, KV_DIM), BF16)],
        compiler_params=_cp(("arbitrary",), 40),
    )(*_hbm(proj_b, bk), rel_bias, sinks)


def _dot_stacked(a, w_ref):
    return jnp.concatenate([_dot(a, w_ref[i]) for i in range(N_CHIPS)], axis=1)


def _dot_nt_stacked(a, w_ref):
    w = w_ref.shape[2]
    acc = _dot_nt(a[:, :w], w_ref[0])
    for i in range(1, N_CHIPS):
        acc = acc + _dot_nt(a[:, i * w:(i + 1) * w], w_ref[i])
    return acc


def _merge_fwd(x2, y_a, y_b, proj_g, w_pa, w_pb, w_out, tm):
    T = x2.shape[0]

    def body(x_ref, ya_ref, yb_ref, g_ref, wpa_ref, wpb_ref, wo_ref, x1_ref, mg_ref):
        g = g_ref[...].astype(F32)
        pa = _dot_stacked(ya_ref[...], wpa_ref)
        pb = _dot_stacked(yb_ref[...], wpb_ref)
        merged = (_sigmoid(g[:, :D_MODEL]) * pa + _sigmoid(g[:, D_MODEL:]) * pb).astype(BF16)
        mg_ref[...] = merged
        x1_ref[...] = x_ref[...] + _dot(merged, wo_ref[...])

    return pl.pallas_call(
        body, name="merge_fwd", grid=(T // tm,),
        in_specs=[_row(tm, D_MODEL), _row(tm, A_WIDTH), _row(tm, Q_DIM), _row(tm, G_DIM),
                  _resident(w_pa.shape), _resident(w_pb.shape), _resident(w_out.shape)],
        out_specs=[_row(tm, D_MODEL), _row(tm, D_MODEL)],
        out_shape=[_sds((T, D_MODEL), F32), _sds((T, D_MODEL), BF16)],
        compiler_params=_cp(("arbitrary",), 40),
    )(*_hbm(x2, y_a, y_b, proj_g, w_pa, w_pb, w_out))


def _upproj(x1, g_ffn, w_up, w_conv, b_conv, tm, seq):
    T = x1.shape[0]
    cw = w_up.shape[2]
    tiles_per_seq = seq // tm

    def body(x_ref, g_ref, w_ref, wc_ref, bc_ref, u_ref, h_ref, gate_ref, val_ref, tail_scr):
        at_start = (pl.program_id(0) % tiles_per_seq) == 0
        x = x_ref[...]
        h = (x * _rms_r(x) * g_ref[...]).astype(BF16)
        h_ref[...] = h
        for i in range(N_CHIPS):
            cs = slice(i * cw, (i + 1) * cw)
            u = _dot(h, w_ref[i])
            u_ref[:, cs] = u.astype(BF16)
            hl = jnp.where(at_start, 0.0, tail_scr[SUBLANES - 2:SUBLANES, cs])
            tail_scr[:, cs] = u[tm - SUBLANES:]
            up = _conv_out((u, _shift_down(u, hl, 1), _shift_down(u, hl, 2)), wc_ref[:, cs], bc_ref[:, cs])
            out_ref = gate_ref if i < N_CHIPS // 2 else val_ref
            out_ref[:, (i % 2) * cw:(i % 2 + 1) * cw] = up.astype(BF16)

    return pl.pallas_call(
        body, name="upproj", grid=(T // tm,),
        in_specs=[_row(tm, D_MODEL), _full(g_ffn.shape), _resident(w_up.shape), _full(w_conv.shape), _full(b_conv.shape)],
        out_specs=[_row(tm, 2 * D_FF), _row(tm, D_MODEL), _row(tm, D_FF), _row(tm, D_FF)],
        out_shape=[_sds((T, 2 * D_FF), BF16), _sds((T, D_MODEL), BF16), _sds((T, D_FF), BF16), _sds((T, D_FF), BF16)],
        scratch_shapes=[pltpu.VMEM((SUBLANES, 2 * D_FF), F32)],
        compiler_params=_cp(("arbitrary",), 56),
    )(*_hbm(x1, g_ffn, w_up, w_conv, b_conv))


def _shift_down(u, halo, k):
    rolled = pltpu.roll(u, k, 0)
    head = rolled[:SUBLANES]
    row = lax.broadcasted_iota(jnp.int32, head.shape, 0)
    if k == 1:
        head = jnp.where(row == 0, halo[1:2], head)
    else:
        head = jnp.where(row == 0, halo[0:1], jnp.where(row == 1, halo[1:2], head))
    return jnp.concatenate([head, rolled[SUBLANES:]], axis=0)


def _shift_up(d, halo, k):
    tm = d.shape[0]
    rolled = pltpu.roll(d, tm - k, 0)
    tail = rolled[tm - SUBLANES:]
    row = lax.broadcasted_iota(jnp.int32, tail.shape, 0)
    if k == 1:
        tail = jnp.where(row == SUBLANES - 1, halo[0:1], tail)
    else:
        tail = jnp.where(row == SUBLANES - 2, halo[0:1], jnp.where(row == SUBLANES - 1, halo[1:2], tail))
    return jnp.concatenate([rolled[:tm - SUBLANES], tail], axis=0)


def _conv_out(taps, wc, bc):
    u, u1, u2 = taps
    return wc[0:1] * u2 + wc[1:2] * u1 + wc[2:3] * u + bc


def _ffn_down_loss(gate, val, x1, target, w_down, g_final, tm):
    T = x1.shape[0]
    half = D_FF // 2

    def body(gt_ref, vl_ref, x1_ref, t_ref, wd_ref, g_ref, dx2_ref, loss_ref, gg_ref):
        i = pl.program_id(0)
        acc = jnp.zeros((tm, D_MODEL), F32)
        for j in range(2):
            gc = slice(j * half, (j + 1) * half)
            gate = gt_ref[:, gc].astype(F32)
            act = (gate * _sigmoid(gate) * vl_ref[:, gc].astype(F32)).astype(BF16)
            acc = acc + _dot(act, wd_ref[gc, :])
        x2 = x1_ref[...] + acc
        r = _rms_r(x2)
        n = x2 * r
        g = g_ref[...]
        diff = n * g - t_ref[...]
        dy = diff * (1.0 / D_MODEL)
        dx2_ref[...] = _rms_bwd(dy, n, r, g)

        @pl.when(i == 0)
        def _():
            loss_ref[...] = jnp.zeros_like(loss_ref)
            gg_ref[...] = jnp.zeros_like(gg_ref)

        loss_ref[...] += 0.5 * jnp.sum(jnp.mean(diff * diff, axis=-1, keepdims=True), axis=0, keepdims=True)
        gg_ref[...] += jnp.sum(dy * n, axis=0, keepdims=True)

    return pl.pallas_call(
        body, name="ffn_down_loss", grid=(T // tm,),
        in_specs=[_row(tm, D_FF), _row(tm, D_FF), _row(tm, D_MODEL), _row(tm, D_MODEL),
                  _resident(w_down.shape), _full(g_final.shape)],
        out_specs=[_row(tm, D_MODEL), _full((1, 1)), _full((1, D_MODEL))],
        out_shape=[_sds((T, D_MODEL), F32), _sds((1, 1), F32), _sds((1, D_MODEL), F32)],
        compiler_params=_cp(("arbitrary",), 48),
    )(*_hbm(gate, val, x1, target, w_down, g_final))


def _ffn_bwd_act(gate, val, dx2, w_down, tm):
    T = dx2.shape[0]
    half = D_FF // 2
    nt = T // tm

    def body(g_ref, v_ref, dx_ref, wd_ref, dg_ref, dv_ref, gwd_out, gbg_ref, gbv_ref, gwd_ref):
        i = pl.program_id(1)

        @pl.when(i == 0)
        def _():
            for r in (gwd_ref, gbg_ref, gbv_ref):
                r[...] = jnp.zeros_like(r)

        dx = dx_ref[...].astype(BF16)
        for c0 in range(0, half, COL_CHUNK):
            cs = slice(c0, min(c0 + COL_CHUNK, half))
            gate = g_ref[:, cs].astype(F32)
            val = v_ref[:, cs].astype(F32)
            sg = _sigmoid(gate)
            silu = gate * sg
            d_act = _dot_nt(dx, wd_ref[cs, :])
            d_val = d_act * silu
            d_gate = d_act * val * (sg * (1.0 + gate * (1.0 - sg)))
            dg_ref[:, cs] = d_gate.astype(BF16)
            dv_ref[:, cs] = d_val.astype(BF16)
            gwd_ref[cs, :] += _dot_tn((silu * val).astype(BF16), dx)
            gbg_ref[:, cs] += jnp.sum(d_gate, axis=0, keepdims=True)
            gbv_ref[:, cs] += jnp.sum(d_val, axis=0, keepdims=True)

        @pl.when(i == nt - 1)
        def _():
            gwd_out[...] = gwd_ref[...].astype(BF16)

    tile = pl.BlockSpec((tm, half), lambda j, i: (i, j))
    vec = pl.BlockSpec((1, half), lambda j, i: (0, j))
    wrows = pl.BlockSpec((half, D_MODEL), lambda j, i: (j, 0))
    return pl.pallas_call(
        body, name="ffn_bwd_act", grid=(2, nt),
        in_specs=[tile, tile, pl.BlockSpec((tm, D_MODEL), lambda j, i: (i, 0)), wrows],
        out_specs=[tile, tile, wrows, vec, vec],
        out_shape=[_sds((T, D_FF), BF16), _sds((T, D_FF), BF16), _sds((D_FF, D_MODEL), BF16),
                   _sds((1, D_FF), F32), _sds((1, D_FF), F32)],
        scratch_shapes=[pltpu.VMEM((half, D_MODEL), F32)],
        compiler_params=_cp(("arbitrary", "arbitrary"), 56),
    )(*_hbm(gate, val, dx2, w_down))


def _ffn_bwd_up(d_gate, d_val, upre, dx2, x1, g_ffn, w_conv, w_up, tm, seq):
    T = dx2.shape[0]
    tiles_per_seq = seq // tm
    k16 = tm // BF16_ROWS
    n16 = T // BF16_ROWS
    cw = D_FF // 2

    def body(dg_ref, dv_ref, hg_ref, hv_ref, u_ref, dx2_ref, x1_ref, g_ref, wc_ref, wu_ref, du_ref, dx1_ref, gg_ref, gwc_ref):
        i = pl.program_id(0)
        at_end = (i % tiles_per_seq) == tiles_per_seq - 1

        @pl.when(i == 0)
        def _():
            gg_ref[...] = jnp.zeros_like(gg_ref)
            gwc_ref[...] = jnp.zeros_like(gwc_ref)

        dh = jnp.zeros((tm, D_MODEL), F32)
        for j in range(4):
            src, hsrc = (dg_ref, hg_ref) if j < 2 else (dv_ref, hv_ref)
            ls = slice((j % 2) * cw, (j % 2 + 1) * cw)
            cs = slice(j * cw, (j + 1) * cw)
            d = src[:, ls].astype(F32)
            hl = hsrc[:, ls].astype(F32)[0:2]
            hl = jnp.where(at_end, 0.0, hl)
            wc = wc_ref[:, cs]
            d1 = _shift_up(d, hl, 1)
            d2 = _shift_up(d, hl, 2)
            du = (wc[2:3] * d + wc[1:2] * d1 + wc[0:1] * d2).astype(BF16)
            du_ref[:, cs] = du
            dh = dh + _dot_nt(du, wu_ref[j])
            u = u_ref[:, cs].astype(F32)
            gwc_ref[0:1, cs] += jnp.sum(d2 * u, axis=0, keepdims=True)
            gwc_ref[1:2, cs] += jnp.sum(d1 * u, axis=0, keepdims=True)
            gwc_ref[2:3, cs] += jnp.sum(d * u, axis=0, keepdims=True)
        x = x1_ref[...]
        r = _rms_r(x)
        n = x * r
        dx1_ref[...] = dx2_ref[...] + _rms_bwd(dh, n, r, g_ref[...])
        gg_ref[...] += jnp.sum(dh * n, axis=0, keepdims=True)

    nxt = pl.BlockSpec((BF16_ROWS, D_FF), lambda i: (jnp.minimum((i + 1) * k16, n16 - 1), 0))
    return pl.pallas_call(
        body, name="ffn_bwd_up", grid=(T // tm,),
        in_specs=[_row(tm, D_FF), _row(tm, D_FF), nxt, nxt, _row(tm, 2 * D_FF), _row(tm, D_MODEL), _row(tm, D_MODEL),
                  _full(g_ffn.shape), _full(w_conv.shape), _resident(w_up.shape)],
        out_specs=[_row(tm, 2 * D_FF), _row(tm, D_MODEL), _full((1, D_MODEL)), _full((3, 2 * D_FF))],
        out_shape=[_sds((T, 2 * D_FF), BF16), _sds((T, D_MODEL), F32), _sds((1, D_MODEL), F32), _sds((3, 2 * D_FF), F32)],
        compiler_params=_cp(("arbitrary",), 56),
    )(*_hbm(d_gate, d_val, d_gate, d_val, upre, dx2, x1, g_ffn, w_conv, w_up))


def _matmul_tn(a, b, tn, tk, name):
    T, M = a.shape
    N = b.shape[1]
    nk = T // tk

    def body(a_ref, b_ref, o_ref, acc_ref):
        k = pl.program_id(1)

        @pl.when(k == 0)
        def _():
            acc_ref[...] = jnp.zeros_like(acc_ref)

        acc_ref[...] += _dot_tn(a_ref[...], b_ref[...])

        @pl.when(k == nk - 1)
        def _():
            o_ref[...] = acc_ref[...].astype(BF16)

    return pl.pallas_call(
        body, name=name, grid=(N // tn, nk),
        in_specs=[pl.BlockSpec((tk, M), lambda j, k: (k, 0)), pl.BlockSpec((tk, tn), lambda j, k: (k, j))],
        out_specs=pl.BlockSpec((M, tn), lambda j, k: (0, j)), out_shape=_sds((M, N), BF16),
        scratch_shapes=[pltpu.VMEM((M, tn), F32)],
        compiler_params=_cp(("arbitrary", "arbitrary"), 48),
    )(*_hbm(a, b))


def _merge_bwd(dx1, merged, y_a, y_b, proj_g, proj_a, w_pa, w_pb, w_out, g_sgu, w_s, b_st, tm, after=None):
    T = dx1.shape[0]

    nt = T // tm
    pshape = (A_WIDTH, D_MODEL)
    order = [] if after is None else [after]

    def body(*refs):
        dx_ref, mg_ref, ya_ref, yb_ref, g_ref, p_ref, wpa_ref, wpb_ref, wo_ref, gs_ref, ws_ref, bs_ref = refs[:12]
        (dg_ref, da_ref, dyb_ref, gwo_out, gwpa_out, gwpb_out, gws_ref, gbs_ref, gg_ref,
         gwo_ref, gwpa_ref, gwpb_ref) = refs[12 + len(order):]
        i = pl.program_id(0)

        @pl.when(i == 0)
        def _():
            for r in (gwo_ref, gwpa_ref, gwpb_ref, gws_ref, gbs_ref, gg_ref):
                r[...] = jnp.zeros_like(r)

        dx = dx_ref[...].astype(BF16)
        dm = _dot_nt(dx, wo_ref[...])
        g = g_ref[...].astype(F32)
        ya = ya_ref[...]
        yb = yb_ref[...]
        pa = _dot_stacked(ya, wpa_ref)
        pb = _dot_stacked(yb, wpb_ref)
        sa = _sigmoid(g[:, :D_MODEL])
        sb = _sigmoid(g[:, D_MODEL:])
        dpa = (dm * sa).astype(BF16)
        dpb = (dm * sb).astype(BF16)
        dg_ref[:, :D_MODEL] = (dm * pa * (sa * (1.0 - sa))).astype(BF16)
        dg_ref[:, D_MODEL:] = (dm * pb * (sb * (1.0 - sb))).astype(BF16)
        d_ya = _dot_nt_stacked(dpa, wpa_ref).astype(BF16)
        dyb_ref[...] = _dot_nt_stacked(dpb, wpb_ref).astype(BF16)
        _sgu_bwd_apply(p_ref[...].astype(F32), d_ya.astype(F32), gs_ref[...], ws_ref, bs_ref, da_ref, gws_ref, gbs_ref, gg_ref)
        gwo_ref[...] += _dot_tn(mg_ref[...], dx)
        gwpa_ref[...] += _dot_tn(ya, dpa)
        gwpb_ref[...] += _dot_tn(yb, dpb)

        @pl.when(i == nt - 1)
        def _():
            gwo_out[...] = gwo_ref[...].astype(BF16)
            gwpa_out[...] = gwpa_ref[...].astype(BF16)
            gwpb_out[...] = gwpb_ref[...].astype(BF16)

    return pl.pallas_call(
        body, name="merge_bwd", grid=(nt,),
        in_specs=[_row(tm, D_MODEL), _row(tm, D_MODEL), _row(tm, A_WIDTH), _row(tm, Q_DIM), _row(tm, G_DIM), _row(tm, A_DIM),
                  _resident(w_pa.shape), _resident(w_pb.shape), _resident(w_out.shape),
                  _full(g_sgu.shape), _full(w_s.shape), _full(b_st.shape)] + [ANY] * len(order),
        out_specs=[_row(tm, G_DIM), _row(tm, A_DIM), _row(tm, Q_DIM),
                   _full(w_out.shape), _full(pshape), _full(pshape), _full(w_s.shape), _full(b_st.shape), _full(g_sgu.shape)],
        out_shape=[_sds((T, G_DIM), BF16), _sds((T, A_DIM), BF16), _sds((T, Q_DIM), BF16),
                   _sds(w_out.shape, BF16), _sds(pshape, BF16), _sds(pshape, BF16),
                   _sds(w_s.shape, F32), _sds(b_st.shape, F32), _sds(g_sgu.shape, F32)],
        scratch_shapes=[pltpu.VMEM(w_out.shape, F32), pltpu.VMEM(pshape, F32), pltpu.VMEM(pshape, F32)],
        compiler_params=_cp(("arbitrary",), 56),
    )(*_hbm(dx1, merged, y_a, y_b, proj_g, proj_a, w_pa, w_pb, w_out, g_sgu, w_s, b_st), *order)


def _sgu_bwd_apply(p, dy, g, ws_ref, bs_ref, dp_ref, gws_ref, gbs_ref, gg_ref):
    tril = _tril()
    pu, pv, u, tu, vv, tv, rv, vn = _sgu_parts(p, g)
    du_cols = []
    dvn_cols = []
    for gi in range(A_GROUPS):
        wm = jnp.where(tril, ws_ref[gi], 0.0).astype(BF16)
        wmt = wm.astype(F32).T.astype(BF16)
        bcol = bs_ref[:, gi:gi + 1]
        cs = slice(gi * CHUNK, (gi + 1) * CHUNK)
        du_rows = []
        dvn_rows = []
        gw = jnp.zeros((CHUNK, CHUNK), F32)
        gb = jnp.zeros((CHUNK, 1), F32)
        for c in range(p.shape[0] // CHUNK):
            rs = slice(c * CHUNK, (c + 1) * CHUNK)
            vn_c = vn[rs, cs]
            s = _dot(wm, vn_c) + bcol
            dy_c = dy[rs, cs]
            ds = dy_c * u[rs, cs]
            du_rows.append(dy_c * s)
            dsb = ds.astype(BF16)
            gw = gw + _dot_nt(dsb, vn_c)
            gb = gb + jnp.sum(ds, axis=-1, keepdims=True)
            dvn_rows.append(_dot(wmt, dsb))
        gws_ref[gi] += jnp.where(tril, gw, 0.0)
        gbs_ref[:, gi:gi + 1] += gb
        du_cols.append(jnp.concatenate(du_rows, axis=0))
        dvn_cols.append(jnp.concatenate(dvn_rows, axis=0))
    du = jnp.concatenate(du_cols, axis=1)
    dvn = jnp.concatenate(dvn_cols, axis=1)
    vhat = vv * rv
    gg_ref[...] += jnp.sum(dvn * vhat, axis=0, keepdims=True)
    dvv = _rms_bwd(dvn, vhat, rv, g)
    dp_ref[:, :A_WIDTH] = (du * _gelu_grad(pu, tu)).astype(BF16)
    dp_ref[:, A_WIDTH:] = (dvv * _gelu_grad(pv, tv)).astype(BF16)


def _attn_bwd(proj_b, d_yb, sinks, rel_bias, n_seq, seq, after=None):
    nb = seq // CHUNK
    bk = jnp.asarray(_band_buckets())
    order = [] if after is None else [after]

    def body(*refs):
        qkv_ref, do_ref, bk_ref, rel_ref, sink_ref = refs[:5]
        (d_ref, gs_ref, gr_ref, bias_scr, sink_scr, kvar_scr, dbias_scr, dk_scr, dv_scr, ds_scr) = refs[5 + len(order):]
        b = pl.program_id(0)
        _attn_setup(bias_scr, sink_scr, kvar_scr, qkv_ref, bk_ref, rel_ref, sink_ref)
        ones = jnp.ones((2 * CHUNK, LANES), BF16)

        @pl.when(b == 0)
        def _():
            dbias_scr[...] = jnp.zeros_like(dbias_scr)
            ds_scr[...] = jnp.zeros_like(ds_scr)

        dk_scr[...] = jnp.zeros_like(dk_scr)
        dv_scr[...] = jnp.zeros_like(dv_scr)

        def transposed(a):
            return a.astype(F32).T.astype(BF16)

        def blk(n, carry):
            r0, kv, vv = _attn_block_inputs(kvar_scr, n)
            prob, psink = _attn_probs(qkv_ref, r0, n, kv, bias_scr, sink_scr, ones)
            dp = jnp.concatenate([_dot_nt(do_ref[pl.ds(r0, CHUNK), (h // 2) * LANES:(h // 2 + 1) * LANES], vv[h // 4][h % 2])
                                  for h in range(N_HEADS)], axis=0)
            delta = _rowsum(prob * dp, ones)
            dsc = prob * (dp - _both(delta))
            ds_scr[...] += psink * delta
            dbias_scr[...] += dsc
            dsb = (dsc * (HEAD_DIM ** -0.5)).astype(BF16)
            pb = prob.astype(BF16)
            dkt = [jnp.zeros((HEAD_DIM, 2 * CHUNK), F32) for _ in range(2)]
            dvt = [jnp.zeros((HEAD_DIM, 2 * CHUNK), F32) for _ in range(2)]
            for pr in range(N_HEADS // 2):
                ps = slice(pr * LANES, (pr + 1) * LANES)
                qpt = transposed(qkv_ref[pl.ds(r0, CHUNK), ps])
                dopt = transposed(do_ref[pl.ds(r0, CHUNK), ps])
                kvh = pr // 2
                dq = jnp.zeros((CHUNK, LANES), F32)
                for hh in range(2):
                    hr = _head_rows(2 * pr + hh)
                    rows = slice(hh * HEAD_DIM, (hh + 1) * HEAD_DIM)
                    dq = dq + _dot(dsb[hr], kv[kvh][hh])
                    dkt[kvh] = dkt[kvh] + _dot(qpt, dsb[hr])[rows]
                    dvt[kvh] = dvt[kvh] + _dot(dopt, pb[hr])[rows]
                d_ref[pl.ds(r0, CHUNK), ps] = dq.astype(BF16)
            dk_scr[:, pl.ds(r0, 2 * CHUNK)] += jnp.concatenate(dkt, axis=0)
            dv_scr[:, pl.ds(r0, 2 * CHUNK)] += jnp.concatenate(dvt, axis=0)
            return carry

        lax.fori_loop(0, nb, blk, 0)
        for n in range(nb):
            rows = slice(n * CHUNK, (n + 1) * CHUNK)
            cols = slice((n + 1) * CHUNK, (n + 2) * CHUNK)
            d_ref[rows, Q_DIM:Q_DIM + KV_DIM] = dk_scr[:, cols].T.astype(BF16)
            d_ref[rows, Q_DIM + KV_DIM:] = dv_scr[:, cols].T.astype(BF16)

        @pl.when(b == n_seq - 1)
        def _():
            bkv = bk_ref[...]
            for h in range(N_HEADS):
                gs_ref[0:1, h:h + 1] = -jnp.sum(ds_scr[_head_rows(h), 0:1], axis=0, keepdims=True)
                db = dbias_scr[_head_rows(h), :]
                for bb in range(N_BUCKETS):
                    part = jnp.sum(jnp.where(bkv == bb, db, 0.0), axis=-1, keepdims=True)
                    gr_ref[bb:bb + 1, h:h + 1] = jnp.sum(part, axis=0, keepdims=True)

    smem = pl.BlockSpec(memory_space=pltpu.SMEM)
    return pl.pallas_call(
        body, name="attn_bwd", grid=(n_seq,),
        in_specs=[_row(seq, B_DIM), _row(seq, Q_DIM), _full(bk.shape), smem, smem] + [ANY] * len(order),
        out_specs=[_row(seq, B_DIM), _full((1, N_HEADS)), _full((N_BUCKETS, N_HEADS))],
        out_shape=[_sds((n_seq * seq, B_DIM), BF16), _sds((1, N_HEADS), F32), _sds((N_BUCKETS, N_HEADS), F32)],
        scratch_shapes=[pltpu.VMEM((HEAD_ROWS, 2 * CHUNK), F32), pltpu.VMEM((HEAD_ROWS, LANES), F32),
                        pltpu.VMEM((8, seq, KV_DIM), BF16), pltpu.VMEM((HEAD_ROWS, 2 * CHUNK), F32),
                        pltpu.VMEM((KV_DIM, seq + CHUNK), F32), pltpu.VMEM((KV_DIM, seq + CHUNK), F32),
                        pltpu.VMEM((HEAD_ROWS, LANES), F32)],
        compiler_params=_cp(("arbitrary",), 40),
    )(*_hbm(proj_b, d_yb, bk), rel_bias, sinks, *order)


def _inproj_bwd(d_g, d_a, d_b, x2, dx1, g_mix, w_in, tm, after=None):
    T = x2.shape[0]
    order = [] if after is None else [after]

    def body(*refs):
        dg_ref, da_ref, db_ref, x_ref, dx1_ref, g_ref, w_ref = refs[:7]
        gx_ref, gg_ref = refs[7 + len(order):]
        dh = (_dot(dg_ref[...], w_ref[_G_COLS, :]) + _dot(da_ref[...], w_ref[_A_COLS, :])
              + _dot(db_ref[...], w_ref[_B_COLS, :]))
        x = x_ref[...]
        r = _rms_r(x)
        n = x * r
        gx_ref[...] = dx1_ref[...] + _rms_bwd(dh, n, r, g_ref[...])

        @pl.when(pl.program_id(0) == 0)
        def _():
            gg_ref[...] = jnp.zeros_like(gg_ref)

        gg_ref[...] += jnp.sum(dh * n, axis=0, keepdims=True)

    return pl.pallas_call(
        body, name="inproj_bwd", grid=(T // tm,),
        in_specs=[_row(tm, G_DIM), _row(tm, A_DIM), _row(tm, B_DIM), _row(tm, D_MODEL), _row(tm, D_MODEL),
                  _full(g_mix.shape), _resident(w_in.shape)] + [ANY] * len(order),
        out_specs=[_row(tm, D_MODEL), _full((1, D_MODEL))],
        out_shape=[_sds((T, D_MODEL), F32), _sds((1, D_MODEL), F32)],
        compiler_params=_cp(("arbitrary",), 48),
    )(*_hbm(d_g, d_a, d_b, x2, dx1, g_mix, w_in), *order)


IN_SHARD = (A_DIM + B_DIM + G_DIM) // N_CHIPS


def _grad_w_in(h, d_a, d_b, d_g, tk):
    T = h.shape[0]
    nk = T // tk
    in_dim = N_CHIPS * IN_SHARD

    def body(h_ref, da_ref, db_ref, dg_ref, o_ref, acc_ref):
        k = pl.program_id(0)

        @pl.when(k == 0)
        def _():
            acc_ref[...] = jnp.zeros_like(acc_ref)

        hb = h_ref[...]
        acc_ref[:, _A_COLS] += _dot_tn(hb, da_ref[...])
        acc_ref[:, _B_COLS] += _dot_tn(hb, db_ref[...])
        acc_ref[:, _G_COLS] += _dot_tn(hb, dg_ref[...])

        @pl.when(k == nk - 1)
        def _():
            for i in range(N_CHIPS):
                o_ref[i] = acc_ref[:, i * IN_SHARD:(i + 1) * IN_SHARD].astype(BF16)

    return pl.pallas_call(
        body, name="grad_w_in", grid=(nk,),
        in_specs=[_row(tk, D_MODEL), _row(tk, A_DIM), _row(tk, B_DIM), _row(tk, G_DIM)],
        out_specs=_full((N_CHIPS, D_MODEL, IN_SHARD)), out_shape=_sds((N_CHIPS, D_MODEL, IN_SHARD), BF16),
        scratch_shapes=[pltpu.VMEM((D_MODEL, in_dim), F32)],
        compiler_params=_cp(("arbitrary",), 56),
    )(*_hbm(h, d_a, d_b, d_g))


def _local_step(x, target, g_mix, g_sgu, w_s, b_s, sinks, rel_bias, g_ffn, b_conv, g_final,
                w_in, w_conv, proj_weights, ffn_weights, on_grads, after=None):
    n_seq, seq, _ = x.shape
    T = n_seq * seq
    tm = min(ROW_TILE, seq)
    tw = min(GRAD_ROW_TILE, T)
    tf = min(WIDE_ROW_TILE, seq)
    x2 = x.reshape(T, D_MODEL)
    tgt = target.reshape(T, D_MODEL)
    b_st = b_s.T
    g_fin = g_final.reshape(1, D_MODEL)

    proj_g, proj_a, proj_b, h, y_a = _inproj(x2, g_mix, w_in, g_sgu, w_s, b_st, tm, after)
    y_b = _attn_fwd(proj_b, sinks, rel_bias, n_seq, seq)
    w_pa, w_pb, w_out = proj_weights(y_b)
    x1, merged = _merge_fwd(x2, y_a, y_b, proj_g, w_pa, w_pb, w_out, tm)
    w_up, w_down = ffn_weights(x1)
    upre, h2, gate, val = _upproj(x1, g_ffn, w_up, w_conv, b_conv, tf, seq)
    dx2, loss, gg_final = _ffn_down_loss(gate, val, x1, tgt, w_down, g_fin, tm)

    d_gate, d_val, gw_down, gb_g, gb_v = _ffn_bwd_act(gate, val, dx2, w_down, tw)
    gb_conv = jnp.concatenate([gb_g, gb_v], axis=1)
    d_upre, dx1, gg_ffn, gw_conv = _ffn_bwd_up(d_gate, d_val, upre, dx2, x1, g_ffn, w_conv, w_up, tf, seq)
    gw_up = _matmul_tn(h2, d_upre, 2 * D_FF // 4, min(2 * GRAD_ROW_TILE, T), "grad_w_up")
    sent = on_grads("ffn", dict(w_up=gw_up, w_down=gw_down))
    d_g, d_a, d_yb, gw_out, gw_pa, gw_pb, gw_s, gb_st, gg_sgu = _merge_bwd(
        dx1, merged, y_a, y_b, proj_g, proj_a, w_pa, w_pb, w_out, g_sgu, w_s, b_st, tw, sent)
    sent = on_grads("proj", dict(w_pa=gw_pa, w_pb=gw_pb, w_out=gw_out))
    d_b, g_sinks, g_rel = _attn_bwd(proj_b, d_yb, sinks, rel_bias, n_seq, seq, sent)
    gw_in = _grad_w_in(h, d_a, d_b, d_g, min(2 * GRAD_ROW_TILE, T))
    sent = on_grads("in", dict(w_in=gw_in))
    grad_x, gg_mix = _inproj_bwd(d_g, d_a, d_b, x2, dx1, g_mix, w_in, tm, sent)

    small = dict(g_mix=gg_mix, g_sgu=gg_sgu, w_s=gw_s, b_s=gb_st.T, sinks=g_sinks, rel_bias=g_rel,
                 g_ffn=gg_ffn, b_conv=gb_conv, g_final=gg_final, w_conv=gw_conv)
    big = dict(w_in=gw_in, w_pa=gw_pa, w_pb=gw_pb, w_out=gw_out, w_up=gw_up, w_down=gw_down)
    return loss, grad_x.reshape(x.shape), small, big


_MIXER = ("w_in", "w_pa", "w_pb", "w_out")
_FFN = ("w_up", "w_down")
_BIG = _MIXER + _FFN

CONV_ROWS = 6
_SMALL_AT = dict(loss=(0, 1, 1), g_sgu=(4, 1, A_WIDTH), sinks=(5, 1, N_HEADS), b_s=(8, A_GROUPS, CHUNK),
                 b_conv=(12, CONV_ROWS, D_MODEL), w_conv=(18, 3 * CONV_ROWS, D_MODEL),
                 g_final=(36, 1, D_MODEL), g_mix=(37, 1, D_MODEL), g_ffn=(38, 1, D_MODEL),
                 w_s=(40, A_GROUPS * CHUNK * CHUNK // D_MODEL, D_MODEL))
_REL_BIAS_AT = (0, A_WIDTH)
_SMALL_IN_CALL = ("g_final", "g_mix", "g_ffn", "g_sgu", "sinks", "b_s", "b_conv", "rel_bias", "w_s")
SMALL_ROWS = 104


def _pack_small(vals):
    def wide(a):
        return jnp.pad(a, ((0, 0), (0, CONV_ROWS * D_MODEL - a.shape[1]))).reshape(-1, D_MODEL)

    nr = _SMALL_AT["w_s"][1]
    w_s = vals["w_s"].reshape(D_MODEL // CHUNK, nr, CHUNK).transpose(1, 0, 2).reshape(nr, D_MODEL)
    laid = dict(vals, b_conv=wide(vals["b_conv"]), w_conv=wide(vals["w_conv"]), w_s=w_s)
    rows, at = [], 0
    for n, (r0, nr, nc) in _SMALL_AT.items():
        if r0 > at:
            rows.append(jnp.zeros((r0 - at, D_MODEL), F32))
        rows.append(jnp.pad(laid[n].astype(F32).reshape(nr, nc), ((0, 0), (0, D_MODEL - nc))))
        at = r0 + nr
    return lax.dynamic_update_slice(jnp.concatenate(rows, axis=0), vals["rel_bias"].T, _REL_BIAS_AT)


def _unwide(a, r):
    return a.reshape(r, CONV_ROWS * D_MODEL)[:, :2 * D_FF]


def _mesh_pos():
    return lax.axis_index("x"), lax.axis_index("y"), lax.axis_index("c")


def _other_chips(x, y):
    return [(1 - x, y), (x, 1 - y), (1 - x, 1 - y)]


def _remote(src, dst, send_sem, recv_sem, to):
    return pltpu.make_async_remote_copy(src_ref=src, dst_ref=dst, send_sem=send_sem, recv_sem=recv_sem,
                                        device_id=to, device_id_type=MESH)


def _own_slot(own, n, at):
    return lax.dynamic_update_slice(lax.empty((n,) + own.shape, own.dtype), own[None], (at,) + (0,) * own.ndim)


def _allgather_weights(stacks, wc_stack):
    names = list(stacks)
    n = len(names)

    def body(*refs):
        ins, outs = refs[:n + 1], refs[n + 1:2 * n + 2]
        send_sems, recv_sems = refs[2 * n + 2:]
        x, y, c = _mesh_pos()
        _handshake(_chip_peers(x, y, c) + _sibling_peers(x, y, c))
        me = 2 * x + y
        sibling = (x, y, 1 - c)
        chips = _other_chips(x, y)

        def half(ref, chip, hc):
            hr = ref.shape[1] // 2
            return ref.at[chip, pl.ds(hc * hr, hr), :]

        first = []
        for k in range(n):
            first += [_remote(half(ins[k], me, c), half(outs[k], me, c), send_sems.at[6 * k + j], recv_sems.at[6 * k + j], (cx, cy, c))
                      for j, (cx, cy) in enumerate(chips)]
        first += [_remote(ins[n].at[me], outs[n].at[me], send_sems.at[6 * n + j], recv_sems.at[6 * n + j], (cx, cy, c))
                  for j, (cx, cy) in enumerate(chips)]
        for cp in first:
            cp.start()
        passed = []
        for k in range(n):
            for j, (cx, cy) in enumerate(chips):
                landed = half(outs[k], 2 * cx + cy, c)
                _remote(landed, landed, send_sems.at[6 * k + j], recv_sems.at[6 * k + j], (x, y, c)).wait_recv()
                passed.append(_remote(landed, landed, send_sems.at[6 * k + 3 + j], recv_sems.at[6 * k + 3 + j], sibling))
                passed[-1].start()
        for k in range(n):
            for j, (cx, cy) in enumerate(chips):
                theirs = half(outs[k], 2 * cx + cy, 1 - c)
                _remote(theirs, theirs, send_sems.at[6 * k + 3 + j], recv_sems.at[6 * k + 3 + j], (x, y, c)).wait_recv()
        for j, (cx, cy) in enumerate(chips):
            slot = outs[n].at[2 * cx + cy]
            _remote(slot, slot, send_sems.at[6 * n + j], recv_sems.at[6 * n + j], (x, y, c)).wait_recv()
        for cp in first + passed:
            cp.wait_send()

    arrays = [stacks[k] for k in names] + [wc_stack]
    outs = pl.pallas_call(
        body, name="allgather_weights",
        in_specs=[HBM] * (n + 1), out_specs=[HBM] * (n + 1), input_output_aliases={k: k for k in range(n + 1)},
        out_shape=[_sds(a.shape, a.dtype) for a in arrays],
        scratch_shapes=[pltpu.SemaphoreType.DMA((6 * n + 3,)), pltpu.SemaphoreType.DMA((6 * n + 3,))],
        compiler_params=pltpu.CompilerParams(collective_id=_COLLECTIVE["gather_in"]),
    )(*arrays)
    return dict(zip(names, outs[:n])), outs[n]


_KIND = {"w_in": "stack", "w_pa": "col", "w_pb": "col", "w_up": "col", "w_out": "row", "w_down": "row"}


def _half_view(ref, kind, h):
    if kind == "stack":
        k = ref.shape[1] // 2
        return ref.at[:, pl.ds(h * k, k), :]
    if kind == "col":
        k = ref.shape[0] // 2
        return ref.at[pl.ds(h * k, k), :]
    k = ref.shape[1] // 2
    return ref.at[:, pl.ds(h * k, k)]


def _shard_view(ref, kind, i):
    if kind == "stack":
        return ref.at[i]
    if kind == "col":
        k = ref.shape[1] // N_CHIPS
        return ref.at[:, pl.ds(i * k, k)]
    k = ref.shape[0] // N_CHIPS
    return ref.at[pl.ds(i * k, k), :]


def _region_view(ref, kind, h):
    if kind == "row":
        k = ref.shape[1] // 2
        return ref.at[:, pl.ds(h * k, k)]
    k = ref.shape[0] // 2
    return ref.at[pl.ds(h * k, k), :]


def _half_shape(shape, kind):
    if kind == "stack":
        return (shape[0], shape[1] // 2, shape[2])
    return (shape[0] // 2, shape[1]) if kind == "col" else (shape[0], shape[1] // 2)


def _part_shape(half_shape, kind):
    if kind == "stack":
        return tuple(half_shape[1:])
    k, w = half_shape
    return (k, w // N_CHIPS) if kind == "col" else (k // N_CHIPS, w)


_DATAFLOW = pltpu.SideEffectType.DATAFLOW_SIDE_EFFECTING
_TOKEN = (SUBLANES, LANES)


_COLLECTIVE = {k: i for i, k in enumerate(
    [kind + "_" + g for kind in ("pair", "chip", "share") for g in ("ffn", "proj", "in")]
    + ["gather_proj", "gather_ffn", "gather_in", "forward_proj", "forward_ffn"])}


def _sibling_peers(x, y, c):
    return [(x, y, 1 - c)]


def _chip_peers(x, y, c):
    return [(cx, cy, c) for cx, cy in _other_chips(x, y)]


def _handshake(peers):
    barrier = pltpu.get_barrier_semaphore()
    for peer in peers:
        pl.semaphore_signal(barrier, inc=1, device_id=peer, device_id_type=MESH)
    pl.semaphore_wait(barrier, len(peers))


def _split_start(name, arrays, n_sems, issue, after=None, handshake=None):
    n = len(arrays)
    order = [] if after is None else [after]

    def body(*refs):
        base = n + len(order)
        if handshake is not None:
            _handshake(handshake[1](*_mesh_pos()))
        issue(refs[:n], refs[base], refs[base + 1])
        refs[-1][...] = jnp.zeros(_TOKEN, F32)

    params = dict(has_side_effects=_DATAFLOW)
    if handshake is not None:
        params["collective_id"] = handshake[0]
    outs = pl.pallas_call(
        body, name=name,
        in_specs=[HBM] * n + [ANY] * len(order), out_specs=[SEM, SEM] + [HBM] * n + [pl.BlockSpec(memory_space=pltpu.VMEM)],
        out_shape=[pltpu.SemaphoreType.DMA((n_sems,)), pltpu.SemaphoreType.DMA((n_sems,))]
        + [pltpu.HBM(a.shape, a.dtype) for a in arrays] + [_sds(_TOKEN, F32)],
        input_output_aliases={k: 2 + k for k in range(n)},
        compiler_params=pltpu.CompilerParams(**params),
    )(*[pltpu.with_memory_space_constraint(a, pltpu.HBM) for a in arrays], *order)
    return outs[0], outs[1], list(outs[2:2 + n]), outs[-1]


def _split_wait(name, started, waits, after):
    send_sems, recv_sems, arrays, _ = started
    n = len(arrays)

    def body(*refs):
        waits(refs[:n], refs[n], refs[n + 1])

    return pl.pallas_call(
        body, name=name,
        in_specs=[HBM] * n + [SEM, SEM, ANY], out_specs=[HBM] * n,
        out_shape=[pltpu.HBM(a.shape, a.dtype) for a in arrays],
        input_output_aliases={k: k for k in range(n)},
        compiler_params=pltpu.CompilerParams(has_side_effects=_DATAFLOW),
    )(*arrays, send_sems, recv_sems, after)


def _wait_both(src, dst, send_sem, recv_sem):
    x, y, c = _mesh_pos()
    cp = _remote(src, dst, send_sem, recv_sem, (x, y, c))
    cp.wait_send()
    cp.wait_recv()


def _pair_exchange_start(parts, tag, after):
    names = list(parts)
    n = len(names)
    lands = [lax.empty(_half_shape(parts[k].shape, _KIND[k]), parts[k].dtype) for k in names]

    def issue(refs, send_sems, recv_sems):
        x, y, c = _mesh_pos()
        for hc in range(2):
            @pl.when(c == hc)
            def _():
                for k in range(n):
                    _remote(_half_view(refs[k], _KIND[names[k]], 1 - hc), refs[n + k], send_sems.at[k], recv_sems.at[k],
                            (x, y, 1 - c)).start()

    return names, _split_start("grad_pair_exchange_start_" + tag, [parts[k] for k in names] + lands, n, issue, after,
                               (_COLLECTIVE["pair_" + tag], _sibling_peers))


def _pair_exchange_wait(pending, tag, after):
    names, started = pending
    n = len(names)

    def waits(refs, send_sems, recv_sems):
        for k in range(n):
            _wait_both(_half_view(refs[k], _KIND[names[k]], 0), refs[n + k], send_sems.at[k], recv_sems.at[k])

    outs = _split_wait("grad_pair_exchange_wait_" + tag, started, waits, after)
    return dict(zip(names, outs[:n])), dict(zip(names, outs[n:]))


def _half_blocks(shape, kind):
    if kind == "stack":
        _, k, w = shape
        return (N_CHIPS // 2, 1), (2, k // 2, w), (lambda i, r, s: (i, r, 0)), (lambda i, r, s: (i, s[1] + r, 0))
    k, w = shape
    if kind == "col":
        tr = STREAM_ROWS
        nb = k // 2 // tr
        return (nb,), (tr, w), (lambda r, s: (r, 0)), (lambda r, s: (s[1] * nb + r, 0))
    nb = 2
    return (nb,), (k // nb, w // 2), (lambda r, s: (r, 0)), (lambda r, s: (r, s[1]))


def _pair_add(part, from_sibling, name, pos):
    kind = _KIND[name]
    grid, block, half_map, full_map = _half_blocks(part.shape, kind)

    def body(s_ref, p_ref, q_ref, o_ref):
        o_ref[...] = (p_ref[...].astype(F32) + q_ref[...].astype(F32)).astype(BF16)

    return pl.pallas_call(
        body, name="grad_pair_add_" + name,
        grid_spec=pltpu.PrefetchScalarGridSpec(
            num_scalar_prefetch=1, grid=grid,
            in_specs=[pl.BlockSpec(block, full_map), pl.BlockSpec(block, half_map)],
            out_specs=pl.BlockSpec(block, half_map)),
        out_shape=_sds(from_sibling.shape, BF16),
        compiler_params=_cp(("arbitrary",) * len(grid), 40),
    )(pos, *_hbm(part, from_sibling))


def _chip_exchange_start(sums, tag, after):
    names = list(sums)
    n = len(names)
    lands = [lax.empty((3,) + _part_shape(sums[k].shape, _KIND[k]), sums[k].dtype) for k in names]

    def issue(refs, send_sems, recv_sems):
        x, y, c = _mesh_pos()
        me = 2 * x + y
        for i in range(N_CHIPS):
            xi, yi = i // 2, i % 2
            j = jnp.where(xi != x, jnp.where(yi != y, 2, 0), 1)

            @pl.when(i != me)
            def _():
                for k in range(n):
                    _remote(_shard_view(refs[k], _KIND[names[k]], i), refs[n + k].at[j], send_sems.at[3 * k + j],
                            recv_sems.at[3 * k + j], (xi, yi, c)).start()

    return names, _split_start("grad_chip_exchange_start_" + tag, [sums[k] for k in names] + lands, 3 * n, issue, after,
                               (_COLLECTIVE["chip_" + tag], _chip_peers))


def _chip_exchange_wait(pending, tag, after):
    names, started = pending
    n = len(names)

    def waits(refs, send_sems, recv_sems):
        for k in range(n):
            for j in range(3):
                _wait_both(_shard_view(refs[k], _KIND[names[k]], 0), refs[n + k].at[j], send_sems.at[3 * k + j], recv_sems.at[3 * k + j])

    return dict(zip(names, _split_wait("grad_chip_exchange_wait_" + tag, started, waits, after)[n:]))


def _allgather_start(stacks, tag, after):
    names = list(stacks)

    def issue(refs, send_sems, recv_sems):
        x, y, c = _mesh_pos()
        me = 2 * x + y
        for k, st in enumerate(refs):
            hr = st.shape[1] // 2
            mine = st.at[me, pl.ds(c * hr, hr), :]
            for j, (cx, cy) in enumerate(_other_chips(x, y)):
                _remote(mine, mine, send_sems.at[3 * k + j], recv_sems.at[3 * k + j], (cx, cy, c)).start()

    return names, _split_start("allgather_start_" + tag, [stacks[k] for k in names], 3 * len(names), issue, after,
                               (_COLLECTIVE["gather_" + tag], _chip_peers))


def _allgather_wait(pending, tag, after):
    names, started = pending

    def waits(refs, send_sems, recv_sems):
        for k, st in enumerate(refs):
            slot = st.at[0, pl.ds(0, st.shape[1] // 2), :]
            for j in range(3):
                _wait_both(slot, slot, send_sems.at[3 * k + j], recv_sems.at[3 * k + j])

    return dict(zip(names, _split_wait("allgather_wait_" + tag, started, waits, after)))


def _allgather_forward(stacks, tag):
    names = list(stacks)
    n = len(names)

    def body(*refs):
        ins, outs = refs[:n], refs[n:2 * n]
        send_sems, recv_sems = refs[2 * n:]
        x, y, c = _mesh_pos()
        _handshake(_sibling_peers(x, y, c))
        copies = []
        for k in range(n):
            hr = ins[k].shape[1] // 2
            for j, (cx, cy) in enumerate(_other_chips(x, y)):
                chip = 2 * cx + cy
                copies.append(_remote(ins[k].at[chip, pl.ds(c * hr, hr), :], outs[k].at[chip, pl.ds(c * hr, hr), :],
                                      send_sems.at[3 * k + j], recv_sems.at[3 * k + j], (x, y, 1 - c)))
        for cp in copies:
            cp.start()
        for cp in copies:
            cp.wait()

    arrays = [stacks[k] for k in names]
    outs = pl.pallas_call(
        body, name="allgather_forward_" + tag, in_specs=[HBM] * n, out_specs=[HBM] * n,
        input_output_aliases={k: k for k in range(n)},
        out_shape=[_sds(a.shape, a.dtype) for a in arrays],
        scratch_shapes=[pltpu.SemaphoreType.DMA((3 * n,)), pltpu.SemaphoreType.DMA((3 * n,))],
        compiler_params=pltpu.CompilerParams(collective_id=_COLLECTIVE["forward_" + tag]),
    )(*arrays)
    return dict(zip(names, outs))


def _owner_sum(part, from_sibling, from_chips, name, pos, shard_shape):
    kind = _KIND[name]
    _, pk, pw = from_chips.shape
    if kind == "row":
        nb = 1
        tr = pk // nb
        p_spec = pl.BlockSpec((tr, pw), lambda r, s: (s[0] * nb + r, s[1]))
        q_spec = pl.BlockSpec((tr, pw), lambda r, s: (s[0] * nb + r, 0))
        o_spec = pl.BlockSpec((tr, pw), lambda r, s: (r, s[1]))
    else:
        tr = STREAM_ROWS
        nb = pk // tr
        if kind == "stack":
            p_spec = pl.BlockSpec((None, tr, pw), lambda r, s: (s[0], s[1] * nb + r, 0))
            q_spec = pl.BlockSpec((None, tr, pw), lambda r, s: (s[0], r, 0))
        else:
            p_spec = pl.BlockSpec((tr, pw), lambda r, s: (s[1] * nb + r, s[0]))
            q_spec = pl.BlockSpec((tr, pw), lambda r, s: (r, s[0]))
        o_spec = pl.BlockSpec((tr, pw), lambda r, s: (s[1] * nb + r, 0))

    def body(s_ref, p_ref, q_ref, r_ref, o_ref):
        acc = p_ref[...].astype(F32) + q_ref[...].astype(F32)
        for j in range(3):
            acc = acc + r_ref[j].astype(F32)
        o_ref[...] = acc

    return pl.pallas_call(
        body, name="grad_owner_sum_" + name,
        grid_spec=pltpu.PrefetchScalarGridSpec(
            num_scalar_prefetch=1, grid=(nb,),
            in_specs=[p_spec, q_spec, pl.BlockSpec((3, tr, pw), lambda r, s: (0, r, 0))],
            out_specs=o_spec),
        out_shape=_sds(shard_shape, F32),
        compiler_params=_cp(("arbitrary",), 32),
    )(pos, *_hbm(part, from_sibling, from_chips))


def _pair_share_start(shards, tag, after):
    names = list(shards)

    def issue(refs, send_sems, recv_sems):
        x, y, c = _mesh_pos()
        for hc in range(2):
            @pl.when(c == hc)
            def _():
                for k, g in enumerate(refs):
                    mine = _region_view(g, _KIND[names[k]], hc)
                    _remote(mine, mine, send_sems.at[k], recv_sems.at[k], (x, y, 1 - c)).start()

    return names, _split_start("grad_pair_share_start_" + tag, [shards[k] for k in names], len(names), issue, after,
                               (_COLLECTIVE["share_" + tag], _sibling_peers))


def _pair_share_wait(pending, tag, after):
    names, started = pending

    def waits(refs, send_sems, recv_sems):
        for k, g in enumerate(refs):
            region = _region_view(g, _KIND[names[k]], 0)
            _wait_both(region, region, send_sems.at[k], recv_sems.at[k])

    return dict(zip(names, _split_wait("grad_pair_share_wait_" + tag, started, waits, after)))


def _small_exchange_start(slots, after):
    def issue(refs, send_sems, recv_sems):
        x, y, c = _mesh_pos()
        mine = refs[0].at[4 * x + 2 * y + c]
        k = 0
        for px in range(2):
            for py in range(2):
                for pc in range(2):
                    if px + py + pc:
                        peer = (1 - x if px else x, 1 - y if py else y, 1 - c if pc else c)
                        _remote(mine, mine, send_sems.at[k], recv_sems.at[k], peer).start()
                        k += 1

    return _split_start("small_exchange_start", [slots], N_DEV - 1, issue, after)


def _small_exchange_wait(started, after):
    def waits(refs, send_sems, recv_sems):
        slot = refs[0].at[0]
        for k in range(N_DEV - 1):
            _wait_both(slot, slot, send_sems.at[k], recv_sems.at[k])

    return _split_wait("small_exchange_wait", started, waits, after)[0]


def _adam_math(w, g, m, v):
    m = ADAM_B1 * m + (1.0 - ADAM_B1) * g
    v = ADAM_B2 * v + (1.0 - ADAM_B2) * (g * g)
    m_hat = m / (1.0 - ADAM_B1 ** ADAM_STEP)
    v_hat = v / (1.0 - ADAM_B2 ** ADAM_STEP)
    delta = -ADAM_LR * (m_hat / (jnp.sqrt(v_hat) + ADAM_EPS) + ADAM_WD * w)
    return delta, m, v


def _adamw(w, g, m, v, name):
    rows, cols = w.shape[0], w.shape[-1]
    fits = [t for t in range(SUBLANES, rows, SUBLANES) if rows % t == 0 and t * cols * 4 <= (3 << 19)]
    tr = max(fits) if fits and w.ndim == 2 else rows

    def body(w_ref, g_ref, m_ref, v_ref, d_ref, nm_ref, nv_ref, go_ref):
        g = g_ref[...]
        d, nm, nv = _adam_math(w_ref[...], g, m_ref[...], v_ref[...])
        d_ref[...] = d
        nm_ref[...] = nm
        nv_ref[...] = nv
        go_ref[...] = g

    spec = pl.BlockSpec((tr,) + w.shape[1:], lambda i: (i,) + (0,) * (w.ndim - 1))
    return pl.pallas_call(
        body, name=name, grid=(rows // tr,), in_specs=[spec] * 4, out_specs=[spec] * 4,
        out_shape=[_sds(w.shape, F32)] * 4, compiler_params=_cp(("arbitrary",)),
    )(*_hbm(w, g, m, v))


def _small_sum_adamw(gathered, w, m, v):
    names = _SMALL_IN_CALL
    n = len(names)

    def body(*refs):
        a_ref = refs[0]
        w_refs, m_refs, v_refs = refs[1:1 + n], refs[1 + n:1 + 2 * n], refs[1 + 2 * n:1 + 3 * n]
        sum_ref, loss_ref = refs[1 + 3 * n], refs[2 + 3 * n]
        outs = refs[3 + 3 * n:]
        g = a_ref[0]
        for k in range(1, N_DEV):
            g = g + a_ref[k]
        sum_ref[...] = g
        loss_ref[...] = g[0:1, 0:1]
        for i, name in enumerate(names):
            if name == "rel_bias":
                r0, c0 = _REL_BIAS_AT
                pieces = [(slice(None), g[r0:r0 + N_HEADS, c0:c0 + N_BUCKETS])]
            elif name == "b_conv":
                r0 = _SMALL_AT[name][0]
                pieces = [(slice(None), jnp.concatenate([g[r0 + k:r0 + k + 1, :] for k in range(CONV_ROWS)], axis=1)[:, :2 * D_FF])]
            elif name == "w_s":
                r0, nr, _ = _SMALL_AT[name]
                pieces = [(slice(nr * j, nr * (j + 1)), g[r0:r0 + nr, CHUNK * j:CHUNK * (j + 1)]) for j in range(D_MODEL // CHUNK)]
            else:
                r0, nr, nc = _SMALL_AT[name]
                pieces = [(slice(None), g[r0:r0 + nr, 0:nc])]
            for at, gp in pieces:
                d, nm, nv = _adam_math(w_refs[i][at], gp, m_refs[i][at], v_refs[i][at])
                for k, val in enumerate((gp, d, nm, nv)):
                    outs[4 * i + k][at] = val

    shapes = [w[k].shape for k in names]
    res = pl.pallas_call(
        body, name="small_sum_adamw",
        out_shape=[_sds((SMALL_ROWS, D_MODEL), F32), _sds((1, 1), F32)] + [_sds(s, F32) for s in shapes for _ in range(4)],
    )(gathered, *[w[k] for k in names], *[m[k] for k in names], *[v[k] for k in names])
    return res[0], res[1], {k: tuple(res[2 + 4 * i:6 + 4 * i]) for i, k in enumerate(names)}


_NAMES = ("g_mix", "w_in", "g_sgu", "w_s", "b_s", "sinks", "rel_bias", "w_pa", "w_pb", "w_out",
          "g_ffn", "w_up", "w_conv", "b_conv", "w_down", "g_final")

def kernel(x, g_mix, w_in, g_sgu, w_s, b_s, sinks, rel_bias, w_pa, w_pb, w_out, g_ffn, w_up, w_conv, b_conv, w_down, g_final, loss_target, m_g_mix, m_w_in, m_g_sgu, m_w_s, m_b_s, m_sinks, m_rel_bias, m_w_pa, m_w_pb, m_w_out, m_g_ffn, m_w_up, m_w_conv, m_b_conv, m_w_down, m_g_final, v_g_mix, v_w_in, v_g_sgu, v_w_s, v_b_s, v_sinks, v_rel_bias, v_w_pa, v_w_pb, v_w_out, v_g_ffn, v_w_up, v_w_conv, v_b_conv, v_w_down, v_g_final):
    w = dict(g_mix=g_mix, w_in=w_in, g_sgu=g_sgu, w_s=w_s, b_s=b_s, sinks=sinks, rel_bias=rel_bias, w_pa=w_pa, w_pb=w_pb,
             w_out=w_out, g_ffn=g_ffn, w_up=w_up, w_conv=w_conv, b_conv=b_conv, w_down=w_down, g_final=g_final)
    m = dict(g_mix=m_g_mix, w_in=m_w_in, g_sgu=m_g_sgu, w_s=m_w_s, b_s=m_b_s, sinks=m_sinks, rel_bias=m_rel_bias, w_pa=m_w_pa,
             w_pb=m_w_pb, w_out=m_w_out, g_ffn=m_g_ffn, w_up=m_w_up, w_conv=m_w_conv, b_conv=m_b_conv, w_down=m_w_down,
             g_final=m_g_final)
    v = dict(g_mix=v_g_mix, w_in=v_w_in, g_sgu=v_g_sgu, w_s=v_w_s, b_s=v_b_s, sinks=v_sinks, rel_bias=v_rel_bias, w_pa=v_w_pa,
             w_pb=v_w_pb, w_out=v_w_out, g_ffn=v_g_ffn, w_up=v_w_up, w_conv=v_w_conv, b_conv=v_b_conv, w_down=v_w_down,
             g_final=v_g_final)
    xi, yi, ci = _mesh_pos()
    me = 2 * xi + yi

    shard = {n: w[n][0] for n in _BIG}
    shard_shapes = {n: shard[n].shape for n in _BIG}
    wc_shard = w["w_conv"][0]
    wc_pad = jnp.pad(wc_shard, ((0, 5), (0, 0)))
    own = {n: _own_slot(shard[n].astype(BF16), N_CHIPS, me) for n in _BIG if n != "w_in"}
    own["w_in"] = _own_slot(shard["w_in"].T.astype(BF16), N_CHIPS, me)
    stacks, wc_all = _allgather_weights({"w_in": own["w_in"]}, _own_slot(wc_pad, N_CHIPS, me))
    proj_gather = _allgather_start({n: own[n] for n in _MIXER[1:]}, "proj", stacks["w_in"])
    ffn_gather = _allgather_start({n: own[n] for n in _FFN}, "ffn", proj_gather[1][-1])
    w_conv_full = jnp.concatenate([wc_all[i, :3] for i in range(N_CHIPS)], axis=1)
    w_in_full = stacks["w_in"].reshape(N_CHIPS * IN_SHARD, D_MODEL)
    pos = jnp.stack([me, ci])

    def proj_weights(done):
        st = _allgather_forward(_allgather_wait(proj_gather, "proj", done), "proj")
        return st["w_pa"], st["w_pb"], st["w_out"].reshape(D_MODEL, D_MODEL)

    def ffn_weights(done):
        st = _allgather_forward(_allgather_wait(ffn_gather, "ffn", done), "ffn")
        return st["w_up"], st["w_down"].reshape(D_FF, D_MODEL)

    groups = {}

    def stage1(group, parts):
        groups[group] = dict(parts=parts, pair=_pair_exchange_start(parts, group, None))
        return groups[group]["pair"][1][-1]

    def stage2(group, after, order_after):
        g = groups[group]
        g["parts"], g["sib"] = _pair_exchange_wait(g["pair"], group, after)
        g["chip"] = _chip_exchange_start({n: _pair_add(g["parts"][n], g["sib"][n], n, pos) for n in g["parts"]}, group, order_after)
        return g["chip"][1][-1]

    def stage3(group, after, order_after):
        g = groups[group]
        got = _chip_exchange_wait(g["chip"], group, after)
        g["share"] = _pair_share_start(
            {n: _owner_sum(g["parts"][n], g["sib"][n], got[n], n, pos, shard_shapes[n]) for n in g["parts"]}, group, order_after)
        return g["share"][1][-1]

    grads, deltas, new_m, new_v = {}, {}, {}, {}

    def stage4(group, after):
        g_shard = _pair_share_wait(groups[group]["share"], group, after)
        last = None
        for n in g_shard:
            g = _tie(g_shard[n], last)
            if n == "w_in":
                d, nm, nv, gt = _adamw(shard[n].T, g.T, m[n][0].T, v[n][0].T, "adamw_" + n)
                grads[n], deltas[n], new_m[n], new_v[n] = gt.T[None], d.T[None], nm.T[None], nv.T[None]
            else:
                d, nm, nv, go = _adamw(shard[n], g, m[n][0], v[n][0], "adamw_" + n)
                grads[n], deltas[n], new_m[n], new_v[n] = go[None], d[None], nm[None], nv[None]
            last = nv
        return last

    def on_grads(group, parts):
        token = stage1(group, parts)
        some = next(iter(parts.values()))
        if group == "proj":
            token = stage2("ffn", some, token)
        if group == "in":
            token = stage2("proj", some, token)
            token = stage3("ffn", some, token)
            token = stage2("in", token, token)
        return token

    loss, grad_x, small, big = _local_step(
        x, loss_target, w["g_mix"], w["g_sgu"], w["w_s"][0], w["b_s"][0], w["sinks"], w["rel_bias"], w["g_ffn"],
        w["b_conv"], w["g_final"], w_in_full, w_conv_full, proj_weights, ffn_weights, on_grads, ffn_gather[1][-1])

    small["loss"] = loss
    small_gather = _small_exchange_start(_own_slot(_pack_small(small), N_DEV, 2 * me + ci), grad_x)
    token = stage3("proj", grad_x, small_gather[-1])
    done = stage4("ffn", token)
    done = stage4("proj", done)
    token = stage3("in", done, None)
    all_small = _small_exchange_wait(small_gather, token)
    two_d = {n: (lambda a, n=n: a.reshape(_SMALL_AT[n][1:])) for n in _SMALL_IN_CALL}
    two_d["rel_bias"] = lambda a: a.T
    two_d["b_conv"] = lambda a: a
    two_d["w_s"] = lambda a: a.reshape(A_GROUPS * CHUNK, CHUNK)
    s_sum, s_loss, s_out = _small_sum_adamw(all_small, *[{n: two_d[n](p[n]) for n in _SMALL_IN_CALL} for p in (w, m, v)])
    stage4("in", s_sum)
    for n in _SMALL_IN_CALL:
        back = (lambda a: a.T) if n == "rel_bias" else (lambda a, n=n: a.reshape(w[n].shape))
        grads[n], deltas[n], new_m[n], new_v[n] = [back(a) for a in s_out[n]]

    def rows(n):
        r0, nr, _ = _SMALL_AT[n]
        return s_sum[r0:r0 + nr]

    wcols = wc_shard.shape[1]
    g_wc = lax.dynamic_slice(_unwide(rows("w_conv"), 3), (0, me * wcols), (3, wcols))
    taps = lambda a: a.transpose(1, 0, 2)
    res = _adamw(taps(w["w_conv"]), g_wc[:, None, :], taps(m["w_conv"]), taps(v["w_conv"]), "adamw_w_conv")
    deltas["w_conv"], new_m["w_conv"], new_v["w_conv"], grads["w_conv"] = [taps(a) for a in res]

    return (s_loss.reshape(()), grad_x, *[grads[n] for n in _NAMES], *[deltas[n] for n in _NAMES],
            *[new_m[n] for n in _NAMES], *[new_v[n] for n in _NAMES])
```

```python
import functools

import numpy as np
import jax
import jax.numpy as jnp
from jax import lax
from jax.experimental import pallas as pl
from jax.experimental.pallas import tpu as pltpu

F32 = jnp.float32
BF16 = jnp.bfloat16

D_MODEL = 1024
CHUNK = 128
A_GROUPS = 4
A_WIDTH = 512
N_HEADS = 8
HEAD_DIM = 64
Q_DIM = 512
KV_DIM = 128
N_BUCKETS = 32
MAX_DISTANCE = 128
D_FF = 2816
EPS = 1e-6
NEG_INF = -1e30
G_DIM = 2 * D_MODEL
A_DIM = 2 * A_WIDTH
B_DIM = Q_DIM + 2 * KV_DIM
LANES = 128
SUBLANES = 8
ROW_TILE = 512
WIDE_ROW_TILE = 256
COL_CHUNK = 512
GRAD_ROW_TILE = 512
STREAM_ROWS = 256
BF16_ROWS = 16
N_CHIPS = 4
N_DEV = 8

ADAM_LR = 0.001
ADAM_B1 = 0.9
ADAM_B2 = 0.999
ADAM_EPS = 1e-08
ADAM_WD = 0.01
ADAM_STEP = 10

MESH = pl.DeviceIdType.MESH
_GELU_C = 0.7978845608028654
_GELU_A = 0.044715


def _cp(sem=None, vmem_mb=None):
    kw = {}
    if sem is not None:
        kw["dimension_semantics"] = sem
    if vmem_mb is not None:
        kw["vmem_limit_bytes"] = vmem_mb << 20
    return pltpu.CompilerParams(**kw)


def _dot(a, b):
    return jnp.dot(a, b, preferred_element_type=F32)


def _dot_nt(a, b):
    return lax.dot_general(a, b, (((1,), (1,)), ((), ())), preferred_element_type=F32)


def _dot_tn(a, b):
    return lax.dot_general(a, b, (((0,), (0,)), ((), ())), preferred_element_type=F32)


def _rms_r(x):
    return lax.rsqrt(jnp.mean(x * x, axis=-1, keepdims=True) + EPS)


def _rms_bwd(dh, n, r, g):
    dn = dh * g
    return r * (dn - n * jnp.mean(dn * n, axis=-1, keepdims=True))


def _gelu(x):
    t = jnp.tanh(_GELU_C * (x + _GELU_A * (x * x * x)))
    return 0.5 * x * (1.0 + t), t


def _gelu_grad(x, t):
    return 0.5 * (1.0 + t) + 0.5 * x * (1.0 - t * t) * (_GELU_C * (1.0 + 3.0 * _GELU_A * x * x))


def _sigmoid(x):
    return 1.0 / (1.0 + jnp.exp(-x))


def _tie(x, dep):
    return x if dep is None else lax.optimization_barrier((x, dep))[0]


def _row(tm, w):
    return pl.BlockSpec((tm, w), lambda i: (i, 0))


def _full(shape):
    nd = len(shape)
    return pl.BlockSpec(tuple(shape), lambda *_: (0,) * nd)


def _resident(shape):
    nd = len(shape)
    return pl.BlockSpec(tuple(shape), lambda *_: (0,) * nd, pipeline_mode=pl.Buffered(1))


def _sds(shape, dtype):
    return pltpu.HBM(tuple(shape), dtype)


def _hbm(*arrays):
    return [pltpu.with_memory_space_constraint(a, pltpu.HBM) for a in arrays]


HBM = pl.BlockSpec(memory_space=pltpu.HBM)
ANY = pl.BlockSpec(memory_space=pl.ANY)
SEM = pl.BlockSpec(memory_space=pltpu.SEMAPHORE)


def _band_buckets():
    i = np.arange(CHUNK)[:, None]
    j = np.arange(2 * CHUNK)[None, :]
    dist = i + CHUNK - j
    valid = (dist >= 0) & (dist < CHUNK)
    d = np.clip(dist, 0, None)
    max_exact = N_BUCKETS // 2
    large = max_exact + (np.log(np.maximum(d, 1) / max_exact) / np.log(MAX_DISTANCE / max_exact)
                         * (N_BUCKETS - max_exact)).astype(np.int32)
    large = np.minimum(large, N_BUCKETS - 1)
    buckets = np.where(d < max_exact, d, large).astype(np.int32)
    return np.where(valid, buckets, -1).astype(np.int32)


_A_COLS = slice(0, A_DIM)
_B_COLS = slice(A_DIM, A_DIM + B_DIM)
_G_COLS = slice(A_DIM + B_DIM, A_DIM + B_DIM + G_DIM)


def _inproj(x2, g_mix, w_in, g_sgu, w_s, b_st, tm, after=None):
    T = x2.shape[0]
    order = [] if after is None else [after]

    def body(*refs):
        x_ref, g_ref, w_ref, gs_ref, ws_ref, bs_ref = refs[:6]
        pg_ref, pa_ref, pb_ref, h_ref, ya_ref = refs[6 + len(order):]
        x = x_ref[...]
        h = (x * _rms_r(x) * g_ref[...]).astype(BF16)
        h_ref[...] = h
        pa = _dot_nt(h, w_ref[_A_COLS, :]).astype(BF16)
        pa_ref[...] = pa
        pb_ref[...] = _dot_nt(h, w_ref[_B_COLS, :]).astype(BF16)
        pg_ref[...] = _dot_nt(h, w_ref[_G_COLS, :]).astype(BF16)
        _sgu_apply(pa.astype(F32), gs_ref[...], ws_ref, bs_ref, ya_ref)

    return pl.pallas_call(
        body, name="inproj", grid=(T // tm,),
        in_specs=[_row(tm, D_MODEL), _full(g_mix.shape), _resident(w_in.shape), _full(g_sgu.shape), _full(w_s.shape),
                  _full(b_st.shape)] + [ANY] * len(order),
        out_specs=[_row(tm, G_DIM), _row(tm, A_DIM), _row(tm, B_DIM), _row(tm, D_MODEL), _row(tm, A_WIDTH)],
        out_shape=[_sds((T, G_DIM), BF16), _sds((T, A_DIM), BF16), _sds((T, B_DIM), BF16), _sds((T, D_MODEL), BF16),
                   _sds((T, A_WIDTH), BF16)],
        compiler_params=_cp(("arbitrary",), 48),
    )(*_hbm(x2, g_mix, w_in, g_sgu, w_s, b_st), *order)


def _sgu_parts(p, g):
    pu = p[:, :A_WIDTH]
    pv = p[:, A_WIDTH:]
    u, tu = _gelu(pu)
    vv, tv = _gelu(pv)
    rv = _rms_r(vv)
    vn = (vv * rv * g).astype(BF16)
    return pu, pv, u, tu, vv, tv, rv, vn


def _tril():
    r = lax.broadcasted_iota(jnp.int32, (CHUNK, CHUNK), 0)
    c = lax.broadcasted_iota(jnp.int32, (CHUNK, CHUNK), 1)
    return r >= c


def _sgu_apply(p, g, ws_ref, bs_ref, y_ref):
    tril = _tril()
    _, _, u, _, _, _, _, vn = _sgu_parts(p, g)
    for gi in range(A_GROUPS):
        wm = jnp.where(tril, ws_ref[gi], 0.0).astype(BF16)
        bcol = bs_ref[:, gi:gi + 1]
        cs = slice(gi * CHUNK, (gi + 1) * CHUNK)
        for c in range(p.shape[0] // CHUNK):
            rs = slice(c * CHUNK, (c + 1) * CHUNK)
            s = _dot(wm, vn[rs, cs]) + bcol
            y_ref[rs, cs] = (u[rs, cs] * s).astype(BF16)


HEAD_ROWS = N_HEADS * CHUNK


def _head_rows(h):
    return slice(h * CHUNK, (h + 1) * CHUNK)


def _attn_setup(bias_scr, sink_scr, kvar_scr, qkv_ref, bk_ref, rel_ref, sink_ref):
    @pl.when(pl.program_id(0) == 0)
    def _():
        bk = bk_ref[...]
        for h in range(N_HEADS):
            acc = jnp.full((CHUNK, 2 * CHUNK), NEG_INF, F32)
            for b in range(N_BUCKETS):
                acc = jnp.where(bk == b, rel_ref[b, h], acc)
            bias_scr[_head_rows(h), :] = acc
            sink_scr[_head_rows(h), :] = jnp.full((CHUNK, LANES), sink_ref[0, h], F32)

    seq = qkv_ref.shape[0]
    rows_per = 2 * CHUNK
    for is_v in range(2):
        c0 = Q_DIM + is_v * KV_DIM
        for r in range(seq // rows_per):
            rs = slice(r * rows_per, (r + 1) * rows_per)
            a = qkv_ref[rs, c0:c0 + KV_DIM].astype(F32)
            lane = lax.broadcasted_iota(jnp.int32, a.shape, 1)
            lo = jnp.where(lane < HEAD_DIM, a, 0.0)
            hi = jnp.where(lane >= HEAD_DIM, a, 0.0)
            kvar_scr[4 * is_v + 0, rs, :] = lo.astype(BF16)
            kvar_scr[4 * is_v + 1, rs, :] = pltpu.roll(lo, HEAD_DIM, 1).astype(BF16)
            kvar_scr[4 * is_v + 2, rs, :] = pltpu.roll(hi, HEAD_DIM, 1).astype(BF16)
            kvar_scr[4 * is_v + 3, rs, :] = hi.astype(BF16)


def _rowsum(a, ones):
    hi = a.astype(BF16)
    lo = (a - hi.astype(F32)).astype(BF16)
    return _dot(hi, ones) + _dot(lo, ones)


def _both(a):
    return jnp.concatenate([a, a], axis=1)


def _attn_probs(qkv_ref, r0, n, kv, bias_scr, sink_scr, ones):
    s = jnp.concatenate([_dot_nt(qkv_ref[pl.ds(r0, CHUNK), (h // 2) * LANES:(h // 2 + 1) * LANES], kv[h // 4][h % 2])
                         for h in range(N_HEADS)], axis=0)
    s = s * (HEAD_DIM ** -0.5) + bias_scr[...]
    col = lax.broadcasted_iota(jnp.int32, s.shape, 1)
    s = jnp.where((col < CHUNK) & (n == 0), NEG_INF, s)
    sink = sink_scr[...]
    m = jnp.maximum(jnp.max(s, axis=-1, keepdims=True), sink)
    p = jnp.exp(s - _both(m))
    es = jnp.exp(sink - m)
    inv = 1.0 / (_dot(p.astype(BF16), ones) + es)
    return p * _both(inv), es * inv


def _attn_block_inputs(kvar_scr, n):
    r0 = pl.multiple_of(n * CHUNK, CHUNK)
    rp = pl.multiple_of(jnp.maximum(n - 1, 0) * CHUNK, CHUNK)

    def both(idx):
        return jnp.concatenate([kvar_scr[idx, pl.ds(rp, CHUNK), :], kvar_scr[idx, pl.ds(r0, CHUNK), :]], axis=0)

    kv = ((both(0), both(1)), (both(2), both(3)))
    vv = ((both(4), both(5)), (both(6), both(7)))
    return r0, kv, vv


def _attn_fwd(proj_b, sinks, rel_bias, n_seq, seq):
    nb = seq // CHUNK
    bk = jnp.asarray(_band_buckets())

    def body(qkv_ref, bk_ref, rel_ref, sink_ref, o_ref, bias_scr, sink_scr, kvar_scr):
        _attn_setup(bias_scr, sink_scr, kvar_scr, qkv_ref, bk_ref, rel_ref, sink_ref)
        ones = jnp.ones((2 * CHUNK, LANES), BF16)

        def blk(n, carry):
            r0, kv, vv = _attn_block_inputs(kvar_scr, n)
            prob, _ = _attn_probs(qkv_ref, r0, n, kv, bias_scr, sink_scr, ones)
            pb = prob.astype(BF16)
            for pr in range(N_HEADS // 2):
                acc = _dot(pb[_head_rows(2 * pr)], vv[pr // 2][0]) + _dot(pb[_head_rows(2 * pr + 1)], vv[pr // 2][1])
                o_ref[pl.ds(r0, CHUNK), pr * LANES:(pr + 1) * LANES] = acc.astype(BF16)
            return carry

        lax.fori_loop(0, nb, blk, 0)

    smem = pl.BlockSpec(memory_space=pltpu.SMEM)
    return pl.pallas_call(
        body, name="attn_fwd", grid=(n_seq,),
        in_specs=[_row(seq, B_DIM), _full(bk.shape), smem, smem],
        out_specs=_row(seq, Q_DIM), out_shape=_sds((n_seq * seq, Q_DIM), BF16),
        scratch_shapes=[pltpu.VMEM((HEAD_ROWS, 2 * CHUNK), F32), pltpu.VMEM((HEAD_ROWS, LANES), F32),
                        pltpu.VMEM((8, seq, KV_DIM), BF16)],
        compiler_params=_cp(("arbitrary",), 40),
    )(*_hbm(proj_b, bk), rel_bias, sinks)


def _dot_stacked(a, w_ref):
    return jnp.concatenate([_dot(a, w_ref[i]) for i in range(N_CHIPS)], axis=1)


def _dot_nt_stacked(a, w_ref):
    w = w_ref.shape[2]
    acc = _dot_nt(a[:, :w], w_ref[0])
    for i in range(1, N_CHIPS):
        acc = acc + _dot_nt(a[:, i * w:(i + 1) * w], w_ref[i])
    return acc


def _merge_fwd(x2, y_a, y_b, proj_g, w_pa, w_pb, w_out, tm):
    T = x2.shape[0]

    def body(x_ref, ya_ref, yb_ref, g_ref, wpa_ref, wpb_ref, wo_ref, x1_ref, mg_ref):
        g = g_ref[...].astype(F32)
        pa = _dot_stacked(ya_ref[...], wpa_ref)
        pb = _dot_stacked(yb_ref[...], wpb_ref)
        merged = (_sigmoid(g[:, :D_MODEL]) * pa + _sigmoid(g[:, D_MODEL:]) * pb).astype(BF16)
        mg_ref[...] = merged
        x1_ref[...] = x_ref[...] + _dot(merged, wo_ref[...])

    return pl.pallas_call(
        body, name="merge_fwd", grid=(T // tm,),
        in_specs=[_row(tm, D_MODEL), _row(tm, A_WIDTH), _row(tm, Q_DIM), _row(tm, G_DIM),
                  _resident(w_pa.shape), _resident(w_pb.shape), _resident(w_out.shape)],
        out_specs=[_row(tm, D_MODEL), _row(tm, D_MODEL)],
        out_shape=[_sds((T, D_MODEL), F32), _sds((T, D_MODEL), BF16)],
        compiler_params=_cp(("arbitrary",), 40),
    )(*_hbm(x2, y_a, y_b, proj_g, w_pa, w_pb, w_out))


def _upproj(x1, g_ffn, w_up, w_conv, b_conv, tm, seq):
    T = x1.shape[0]
    cw = w_up.shape[2]
    tiles_per_seq = seq // tm

    def body(x_ref, g_ref, w_ref, wc_ref, bc_ref, u_ref, h_ref, gate_ref, val_ref, tail_scr):
        at_start = (pl.program_id(0) % tiles_per_seq) == 0
        x = x_ref[...]
        h = (x * _rms_r(x) * g_ref[...]).astype(BF16)
        h_ref[...] = h
        for i in range(N_CHIPS):
            cs = slice(i * cw, (i + 1) * cw)
            u = _dot(h, w_ref[i])
            u_ref[:, cs] = u.astype(BF16)
            hl = jnp.where(at_start, 0.0, tail_scr[SUBLANES - 2:SUBLANES, cs])
            tail_scr[:, cs] = u[tm - SUBLANES:]
            up = _conv_out((u, _shift_down(u, hl, 1), _shift_down(u, hl, 2)), wc_ref[:, cs], bc_ref[:, cs])
            out_ref = gate_ref if i < N_CHIPS // 2 else val_ref
            out_ref[:, (i % 2) * cw:(i % 2 + 1) * cw] = up.astype(BF16)

    return pl.pallas_call(
        body, name="upproj", grid=(T // tm,),
        in_specs=[_row(tm, D_MODEL), _full(g_ffn.shape), _resident(w_up.shape), _full(w_conv.shape), _full(b_conv.shape)],
        out_specs=[_row(tm, 2 * D_FF), _row(tm, D_MODEL), _row(tm, D_FF), _row(tm, D_FF)],
        out_shape=[_sds((T, 2 * D_FF), BF16), _sds((T, D_MODEL), BF16), _sds((T, D_FF), BF16), _sds((T, D_FF), BF16)],
        scratch_shapes=[pltpu.VMEM((SUBLANES, 2 * D_FF), F32)],
        compiler_params=_cp(("arbitrary",), 56),
    )(*_hbm(x1, g_ffn, w_up, w_conv, b_conv))


def _shift_down(u, halo, k):
    rolled = pltpu.roll(u, k, 0)
    head = rolled[:SUBLANES]
    row = lax.broadcasted_iota(jnp.int32, head.shape, 0)
    if k == 1:
        head = jnp.where(row == 0, halo[1:2], head)
    else:
        head = jnp.where(row == 0, halo[0:1], jnp.where(row == 1, halo[1:2], head))
    return jnp.concatenate([head, rolled[SUBLANES:]], axis=0)


def _shift_up(d, halo, k):
    tm = d.shape[0]
    rolled = pltpu.roll(d, tm - k, 0)
    tail = rolled[tm - SUBLANES:]
    row = lax.broadcasted_iota(jnp.int32, tail.shape, 0)
    if k == 1:
        tail = jnp.where(row == SUBLANES - 1, halo[0:1], tail)
    else:
        tail = jnp.where(row == SUBLANES - 2, halo[0:1], jnp.where(row == SUBLANES - 1, halo[1:2], tail))
    return jnp.concatenate([rolled[:tm - SUBLANES], tail], axis=0)


def _conv_out(taps, wc, bc):
    u, u1, u2 = taps
    return wc[0:1] * u2 + wc[1:2] * u1 + wc[2:3] * u + bc


def _ffn_down_loss(gate, val, x1, target, w_down, g_final, tm):
    T = x1.shape[0]
    half = D_FF // 2

    sub = min(tm, 128)

    def body(gt_ref, vl_ref, x1_ref, t_ref, wd_ref, g_ref, dx2_ref, loss_ref, gg_ref):
        i = pl.program_id(0)
        g = g_ref[...]

        def down(rs):
            acc = jnp.zeros((sub, D_MODEL), F32)
            for j in range(2):
                gc = slice(j * half, (j + 1) * half)
                gate = gt_ref[rs, gc].astype(F32)
                act = (gate * _sigmoid(gate) * vl_ref[rs, gc].astype(F32)).astype(BF16)
                acc = acc + _dot(act, wd_ref[gc, :])
            return acc

        def norm_loss(rs, acc):
            x2 = x1_ref[rs, :] + acc
            r = _rms_r(x2)
            n = x2 * r
            diff = n * g - t_ref[rs, :]
            dy = diff * (1.0 / D_MODEL)
            dx2_ref[rs, :] = _rms_bwd(dy, n, r, g)
            return (jnp.sum(jnp.mean(diff * diff, axis=-1, keepdims=True), axis=0, keepdims=True),
                    jnp.sum(dy * n, axis=0, keepdims=True))

        subs = [slice(s0, s0 + sub) for s0 in range(0, tm, sub)]
        accs = [down(rs) for rs in subs]
        parts = [norm_loss(rs, acc) for rs, acc in zip(subs, accs)]

        @pl.when(i == 0)
        def _():
            loss_ref[...] = jnp.zeros_like(loss_ref)
            gg_ref[...] = jnp.zeros_like(gg_ref)

        loss_ref[...] += 0.5 * sum(p[0] for p in parts)
        gg_ref[...] += sum(p[1] for p in parts)

    return pl.pallas_call(
        body, name="ffn_down_loss", grid=(T // tm,),
        in_specs=[_row(tm, D_FF), _row(tm, D_FF), _row(tm, D_MODEL), _row(tm, D_MODEL),
                  _resident(w_down.shape), _full(g_final.shape)],
        out_specs=[_row(tm, D_MODEL), _full((1, 1)), _full((1, D_MODEL))],
        out_shape=[_sds((T, D_MODEL), F32), _sds((1, 1), F32), _sds((1, D_MODEL), F32)],
        compiler_params=_cp(("arbitrary",), 48),
    )(*_hbm(gate, val, x1, target, w_down, g_final))


def _ffn_bwd_act(gate, val, dx2, w_down, tm):
    T = dx2.shape[0]
    half = D_FF // 2
    nt = T // tm

    def body(g_ref, v_ref, dx_ref, wd_ref, dg_ref, dv_ref, gwd_out, gbg_ref, gbv_ref, gwd_ref):
        i = pl.program_id(1)

        @pl.when(i == 0)
        def _():
            for r in (gwd_ref, gbg_ref, gbv_ref):
                r[...] = jnp.zeros_like(r)

        dx = dx_ref[...].astype(BF16)
        for c0 in range(0, half, COL_CHUNK):
            cs = slice(c0, min(c0 + COL_CHUNK, half))
            gate = g_ref[:, cs].astype(F32)
            val = v_ref[:, cs].astype(F32)
            sg = _sigmoid(gate)
            silu = gate * sg
            d_act = _dot_nt(dx, wd_ref[cs, :])
            d_val = d_act * silu
            d_gate = d_act * val * (sg * (1.0 + gate * (1.0 - sg)))
            dg_ref[:, cs] = d_gate.astype(BF16)
            dv_ref[:, cs] = d_val.astype(BF16)
            gwd_ref[cs, :] += _dot_tn((silu * val).astype(BF16), dx)
            gbg_ref[:, cs] += jnp.sum(d_gate, axis=0, keepdims=True)
            gbv_ref[:, cs] += jnp.sum(d_val, axis=0, keepdims=True)

        @pl.when(i == nt - 1)
        def _():
            gwd_out[...] = gwd_ref[...].astype(BF16)

    tile = pl.BlockSpec((tm, half), lambda j, i: (i, j))
    vec = pl.BlockSpec((1, half), lambda j, i: (0, j))
    wrows = pl.BlockSpec((half, D_MODEL), lambda j, i: (j, 0))
    return pl.pallas_call(
        body, name="ffn_bwd_act", grid=(2, nt),
        in_specs=[tile, tile, pl.BlockSpec((tm, D_MODEL), lambda j, i: (i, 0)), wrows],
        out_specs=[tile, tile, wrows, vec, vec],
        out_shape=[_sds((T, D_FF), BF16), _sds((T, D_FF), BF16), _sds((D_FF, D_MODEL), BF16),
                   _sds((1, D_FF), F32), _sds((1, D_FF), F32)],
        scratch_shapes=[pltpu.VMEM((half, D_MODEL), F32)],
        compiler_params=_cp(("arbitrary", "arbitrary"), 56),
    )(*_hbm(gate, val, dx2, w_down))


def _ffn_bwd_up(d_gate, d_val, upre, dx2, x1, g_ffn, w_conv, w_up, tm, seq):
    T = dx2.shape[0]
    tiles_per_seq = seq // tm
    k16 = tm // BF16_ROWS
    n16 = T // BF16_ROWS
    cw = D_FF // 2

    def body(dg_ref, dv_ref, hg_ref, hv_ref, u_ref, dx2_ref, x1_ref, g_ref, wc_ref, wu_ref, du_ref, dx1_ref, gg_ref, gwc_ref):
        i = pl.program_id(0)
        at_end = (i % tiles_per_seq) == tiles_per_seq - 1

        @pl.when(i == 0)
        def _():
            gg_ref[...] = jnp.zeros_like(gg_ref)
            gwc_ref[...] = jnp.zeros_like(gwc_ref)

        dh = jnp.zeros((tm, D_MODEL), F32)
        for j in range(4):
            src, hsrc = (dg_ref, hg_ref) if j < 2 else (dv_ref, hv_ref)
            ls = slice((j % 2) * cw, (j % 2 + 1) * cw)
            cs = slice(j * cw, (j + 1) * cw)
            d = src[:, ls].astype(F32)
            hl = hsrc[:, ls].astype(F32)[0:2]
            hl = jnp.where(at_end, 0.0, hl)
            wc = wc_ref[:, cs]
            d1 = _shift_up(d, hl, 1)
            d2 = _shift_up(d, hl, 2)
            du = (wc[2:3] * d + wc[1:2] * d1 + wc[0:1] * d2).astype(BF16)
            du_ref[:, cs] = du
            dh = dh + _dot_nt(du, wu_ref[j])
            u = u_ref[:, cs].astype(F32)
            gwc_ref[0:1, cs] += jnp.sum(d2 * u, axis=0, keepdims=True)
            gwc_ref[1:2, cs] += jnp.sum(d1 * u, axis=0, keepdims=True)
            gwc_ref[2:3, cs] += jnp.sum(d * u, axis=0, keepdims=True)
        x = x1_ref[...]
        r = _rms_r(x)
        n = x * r
        dx1_ref[...] = dx2_ref[...] + _rms_bwd(dh, n, r, g_ref[...])
        gg_ref[...] += jnp.sum(dh * n, axis=0, keepdims=True)

    nxt = pl.BlockSpec((BF16_ROWS, D_FF), lambda i: (jnp.minimum((i + 1) * k16, n16 - 1), 0))
    return pl.pallas_call(
        body, name="ffn_bwd_up", grid=(T // tm,),
        in_specs=[_row(tm, D_FF), _row(tm, D_FF), nxt, nxt, _row(tm, 2 * D_FF), _row(tm, D_MODEL), _row(tm, D_MODEL),
                  _full(g_ffn.shape), _full(w_conv.shape), _resident(w_up.shape)],
        out_specs=[_row(tm, 2 * D_FF), _row(tm, D_MODEL), _full((1, D_MODEL)), _full((3, 2 * D_FF))],
        out_shape=[_sds((T, 2 * D_FF), BF16), _sds((T, D_MODEL), F32), _sds((1, D_MODEL), F32), _sds((3, 2 * D_FF), F32)],
        compiler_params=_cp(("arbitrary",), 56),
    )(*_hbm(d_gate, d_val, d_gate, d_val, upre, dx2, x1, g_ffn, w_conv, w_up))


def _matmul_tn(a, b, tn, tk, name):
    T, M = a.shape
    N = b.shape[1]
    nk = T // tk

    def body(a_ref, b_ref, o_ref, acc_ref):
        k = pl.program_id(1)

        @pl.when(k == 0)
        def _():
            acc_ref[...] = jnp.zeros_like(acc_ref)

        acc_ref[...] += _dot_tn(a_ref[...], b_ref[...])

        @pl.when(k == nk - 1)
        def _():
            o_ref[...] = acc_ref[...].astype(BF16)

    return pl.pallas_call(
        body, name=name, grid=(N // tn, nk),
        in_specs=[pl.BlockSpec((tk, M), lambda j, k: (k, 0)), pl.BlockSpec((tk, tn), lambda j, k: (k, j))],
        out_specs=pl.BlockSpec((M, tn), lambda j, k: (0, j)), out_shape=_sds((M, N), BF16),
        scratch_shapes=[pltpu.VMEM((M, tn), F32)],
        compiler_params=_cp(("arbitrary", "arbitrary"), 48),
    )(*_hbm(a, b))


def _merge_bwd(dx1, merged, y_a, y_b, proj_g, proj_a, w_pa, w_pb, w_out, g_sgu, w_s, b_st, tm, after=None):
    T = dx1.shape[0]

    nt = T // tm
    pshape = (A_WIDTH, D_MODEL)
    order = [] if after is None else [after]

    def body(*refs):
        dx_ref, mg_ref, ya_ref, yb_ref, g_ref, p_ref, wpa_ref, wpb_ref, wo_ref, gs_ref, ws_ref, bs_ref = refs[:12]
        (dg_ref, da_ref, dyb_ref, gwo_out, gwpa_out, gwpb_out, gws_ref, gbs_ref, gg_ref,
         gwo_ref, gwpa_ref, gwpb_ref) = refs[12 + len(order):]
        i = pl.program_id(0)

        @pl.when(i == 0)
        def _():
            for r in (gwo_ref, gwpa_ref, gwpb_ref, gws_ref, gbs_ref, gg_ref):
                r[...] = jnp.zeros_like(r)

        dx = dx_ref[...].astype(BF16)
        dm = _dot_nt(dx, wo_ref[...])
        g = g_ref[...].astype(F32)
        ya = ya_ref[...]
        yb = yb_ref[...]
        pa = _dot_stacked(ya, wpa_ref)
        pb = _dot_stacked(yb, wpb_ref)
        sa = _sigmoid(g[:, :D_MODEL])
        sb = _sigmoid(g[:, D_MODEL:])
        dpa = (dm * sa).astype(BF16)
        dpb = (dm * sb).astype(BF16)
        dg_ref[:, :D_MODEL] = (dm * pa * (sa * (1.0 - sa))).astype(BF16)
        dg_ref[:, D_MODEL:] = (dm * pb * (sb * (1.0 - sb))).astype(BF16)
        d_ya = _dot_nt_stacked(dpa, wpa_ref).astype(BF16)
        dyb_ref[...] = _dot_nt_stacked(dpb, wpb_ref).astype(BF16)
        _sgu_bwd_apply(p_ref[...].astype(F32), d_ya.astype(F32), gs_ref[...], ws_ref, bs_ref, da_ref, gws_ref, gbs_ref, gg_ref)
        gwo_ref[...] += _dot_tn(mg_ref[...], dx)
        gwpa_ref[...] += _dot_tn(ya, dpa)
        gwpb_ref[...] += _dot_tn(yb, dpb)

        @pl.when(i == nt - 1)
        def _():
            gwo_out[...] = gwo_ref[...].astype(BF16)
            gwpa_out[...] = gwpa_ref[...].astype(BF16)
            gwpb_out[...] = gwpb_ref[...].astype(BF16)

    return pl.pallas_call(
        body, name="merge_bwd", grid=(nt,),
        in_specs=[_row(tm, D_MODEL), _row(tm, D_MODEL), _row(tm, A_WIDTH), _row(tm, Q_DIM), _row(tm, G_DIM), _row(tm, A_DIM),
                  _resident(w_pa.shape), _resident(w_pb.shape), _resident(w_out.shape),
                  _full(g_sgu.shape), _full(w_s.shape), _full(b_st.shape)] + [ANY] * len(order),
        out_specs=[_row(tm, G_DIM), _row(tm, A_DIM), _row(tm, Q_DIM),
                   _full(w_out.shape), _full(pshape), _full(pshape), _full(w_s.shape), _full(b_st.shape), _full(g_sgu.shape)],
        out_shape=[_sds((T, G_DIM), BF16), _sds((T, A_DIM), BF16), _sds((T, Q_DIM), BF16),
                   _sds(w_out.shape, BF16), _sds(pshape, BF16), _sds(pshape, BF16),
                   _sds(w_s.shape, F32), _sds(b_st.shape, F32), _sds(g_sgu.shape, F32)],
        scratch_shapes=[pltpu.VMEM(w_out.shape, F32), pltpu.VMEM(pshape, F32), pltpu.VMEM(pshape, F32)],
        compiler_params=_cp(("arbitrary",), 56),
    )(*_hbm(dx1, merged, y_a, y_b, proj_g, proj_a, w_pa, w_pb, w_out, g_sgu, w_s, b_st), *order)


def _sgu_bwd_apply(p, dy, g, ws_ref, bs_ref, dp_ref, gws_ref, gbs_ref, gg_ref):
    tril = _tril()
    pu, pv, u, tu, vv, tv, rv, vn = _sgu_parts(p, g)
    du_cols = []
    dvn_cols = []
    for gi in range(A_GROUPS):
        wm = jnp.where(tril, ws_ref[gi], 0.0).astype(BF16)
        wmt = wm.astype(F32).T.astype(BF16)
        bcol = bs_ref[:, gi:gi + 1]
        cs = slice(gi * CHUNK, (gi + 1) * CHUNK)
        du_rows = []
        dvn_rows = []
        gw = jnp.zeros((CHUNK, CHUNK), F32)
        gb = jnp.zeros((CHUNK, 1), F32)
        for c in range(p.shape[0] // CHUNK):
            rs = slice(c * CHUNK, (c + 1) * CHUNK)
            vn_c = vn[rs, cs]
            s = _dot(wm, vn_c) + bcol
            dy_c = dy[rs, cs]
            ds = dy_c * u[rs, cs]
            du_rows.append(dy_c * s)
            dsb = ds.astype(BF16)
            gw = gw + _dot_nt(dsb, vn_c)
            gb = gb + jnp.sum(ds, axis=-1, keepdims=True)
            dvn_rows.append(_dot(wmt, dsb))
        gws_ref[gi] += jnp.where(tril, gw, 0.0)
        gbs_ref[:, gi:gi + 1] += gb
        du_cols.append(jnp.concatenate(du_rows, axis=0))
        dvn_cols.append(jnp.concatenate(dvn_rows, axis=0))
    du = jnp.concatenate(du_cols, axis=1)
    dvn = jnp.concatenate(dvn_cols, axis=1)
    vhat = vv * rv
    gg_ref[...] += jnp.sum(dvn * vhat, axis=0, keepdims=True)
    dvv = _rms_bwd(dvn, vhat, rv, g)
    dp_ref[:, :A_WIDTH] = (du * _gelu_grad(pu, tu)).astype(BF16)
    dp_ref[:, A_WIDTH:] = (dvv * _gelu_grad(pv, tv)).astype(BF16)


def _attn_bwd(proj_b, d_yb, sinks, rel_bias, n_seq, seq, after=None):
    nb = seq // CHUNK
    bk = jnp.asarray(_band_buckets())
    order = [] if after is None else [after]

    def body(*refs):
        qkv_ref, do_ref, bk_ref, rel_ref, sink_ref = refs[:5]
        (d_ref, gs_ref, gr_ref, bias_scr, sink_scr, kvar_scr, dbias_scr, dk_scr, dv_scr, ds_scr) = refs[5 + len(order):]
        b = pl.program_id(0)
        _attn_setup(bias_scr, sink_scr, kvar_scr, qkv_ref, bk_ref, rel_ref, sink_ref)
        ones = jnp.ones((2 * CHUNK, LANES), BF16)

        @pl.when(b == 0)
        def _():
            dbias_scr[...] = jnp.zeros_like(dbias_scr)
            ds_scr[...] = jnp.zeros_like(ds_scr)

        dk_scr[...] = jnp.zeros_like(dk_scr)
        dv_scr[...] = jnp.zeros_like(dv_scr)

        def transposed(a):
            return a.astype(F32).T.astype(BF16)

        def blk(n, carry):
            r0, kv, vv = _attn_block_inputs(kvar_scr, n)
            prob, psink = _attn_probs(qkv_ref, r0, n, kv, bias_scr, sink_scr, ones)
            dp = jnp.concatenate([_dot_nt(do_ref[pl.ds(r0, CHUNK), (h // 2) * LANES:(h // 2 + 1) * LANES], vv[h // 4][h % 2])
                                  for h in range(N_HEADS)], axis=0)
            delta = _rowsum(prob * dp, ones)
            dsc = prob * (dp - _both(delta))
            ds_scr[...] += psink * delta
            dbias_scr[...] += dsc
            dsb = (dsc * (HEAD_DIM ** -0.5)).astype(BF16)
            pb = prob.astype(BF16)
            dkt = [jnp.zeros((HEAD_DIM, 2 * CHUNK), F32) for _ in range(2)]
            dvt = [jnp.zeros((HEAD_DIM, 2 * CHUNK), F32) for _ in range(2)]
            for pr in range(N_HEADS // 2):
                ps = slice(pr * LANES, (pr + 1) * LANES)
                qpt = transposed(qkv_ref[pl.ds(r0, CHUNK), ps])
                dopt = transposed(do_ref[pl.ds(r0, CHUNK), ps])
                kvh = pr // 2
                dq = jnp.zeros((CHUNK, LANES), F32)
                for hh in range(2):
                    hr = _head_rows(2 * pr + hh)
                    rows = slice(hh * HEAD_DIM, (hh + 1) * HEAD_DIM)
                    dq = dq + _dot(dsb[hr], kv[kvh][hh])
                    dkt[kvh] = dkt[kvh] + _dot(qpt, dsb[hr])[rows]
                    dvt[kvh] = dvt[kvh] + _dot(dopt, pb[hr])[rows]
                d_ref[pl.ds(r0, CHUNK), ps] = dq.astype(BF16)
            dk_scr[:, pl.ds(r0, 2 * CHUNK)] += jnp.concatenate(dkt, axis=0)
            dv_scr[:, pl.ds(r0, 2 * CHUNK)] += jnp.concatenate(dvt, axis=0)
            return carry

        lax.fori_loop(0, nb, blk, 0)
        for n in range(nb):
            rows = slice(n * CHUNK, (n + 1) * CHUNK)
            cols = slice((n + 1) * CHUNK, (n + 2) * CHUNK)
            d_ref[rows, Q_DIM:Q_DIM + KV_DIM] = dk_scr[:, cols].T.astype(BF16)
            d_ref[rows, Q_DIM + KV_DIM:] = dv_scr[:, cols].T.astype(BF16)

        @pl.when(b == n_seq - 1)
        def _():
            bkv = bk_ref[...]
            for h in range(N_HEADS):
                gs_ref[0:1, h:h + 1] = -jnp.sum(ds_scr[_head_rows(h), 0:1], axis=0, keepdims=True)
                db = dbias_scr[_head_rows(h), :]
                for bb in range(N_BUCKETS):
                    part = jnp.sum(jnp.where(bkv == bb, db, 0.0), axis=-1, keepdims=True)
                    gr_ref[bb:bb + 1, h:h + 1] = jnp.sum(part, axis=0, keepdims=True)

    smem = pl.BlockSpec(memory_space=pltpu.SMEM)
    return pl.pallas_call(
        body, name="attn_bwd", grid=(n_seq,),
        in_specs=[_row(seq, B_DIM), _row(seq, Q_DIM), _full(bk.shape), smem, smem] + [ANY] * len(order),
        out_specs=[_row(seq, B_DIM), _full((1, N_HEADS)), _full((N_BUCKETS, N_HEADS))],
        out_shape=[_sds((n_seq * seq, B_DIM), BF16), _sds((1, N_HEADS), F32), _sds((N_BUCKETS, N_HEADS), F32)],
        scratch_shapes=[pltpu.VMEM((HEAD_ROWS, 2 * CHUNK), F32), pltpu.VMEM((HEAD_ROWS, LANES), F32),
                        pltpu.VMEM((8, seq, KV_DIM), BF16), pltpu.VMEM((HEAD_ROWS, 2 * CHUNK), F32),
                        pltpu.VMEM((KV_DIM, seq + CHUNK), F32), pltpu.VMEM((KV_DIM, seq + CHUNK), F32),
                        pltpu.VMEM((HEAD_ROWS, LANES), F32)],
        compiler_params=_cp(("arbitrary",), 40),
    )(*_hbm(proj_b, d_yb, bk), rel_bias, sinks, *order)


def _inproj_bwd(d_g, d_a, d_b, x2, dx1, g_mix, w_in, tm, after=None):
    T = x2.shape[0]
    sub = min(tm, 128)
    order = [] if after is None else [after]

    def body(*refs):
        dg_ref, da_ref, db_ref, x_ref, dx1_ref, g_ref, w_ref = refs[:7]
        gx_ref, gg_ref = refs[7 + len(order):]
        subs = [slice(s0, s0 + sub) for s0 in range(0, tm, sub)]
        dhs = [_dot(dg_ref[rs, :], w_ref[_G_COLS, :]) + _dot(da_ref[rs, :], w_ref[_A_COLS, :])
               + _dot(db_ref[rs, :], w_ref[_B_COLS, :]) for rs in subs]
        gg = jnp.zeros((1, D_MODEL), F32)
        for rs, dh in zip(subs, dhs):
            x = x_ref[rs, :]
            r = _rms_r(x)
            n = x * r
            gx_ref[rs, :] = dx1_ref[rs, :] + _rms_bwd(dh, n, r, g_ref[...])
            gg = gg + jnp.sum(dh * n, axis=0, keepdims=True)

        @pl.when(pl.program_id(0) == 0)
        def _():
            gg_ref[...] = jnp.zeros_like(gg_ref)

        gg_ref[...] += gg

    return pl.pallas_call(
        body, name="inproj_bwd", grid=(T // tm,),
        in_specs=[_row(tm, G_DIM), _row(tm, A_DIM), _row(tm, B_DIM), _row(tm, D_MODEL), _row(tm, D_MODEL),
                  _full(g_mix.shape), _resident(w_in.shape)] + [ANY] * len(order),
        out_specs=[_row(tm, D_MODEL), _full((1, D_MODEL))],
        out_shape=[_sds((T, D_MODEL), F32), _sds((1, D_MODEL), F32)],
        compiler_params=_cp(("arbitrary",), 48),
    )(*_hbm(d_g, d_a, d_b, x2, dx1, g_mix, w_in), *order)


IN_SHARD = (A_DIM + B_DIM + G_DIM) // N_CHIPS


def _grad_w_in(h, d_a, d_b, d_g, tk):
    T = h.shape[0]
    nk = T // tk
    in_dim = N_CHIPS * IN_SHARD

    def body(h_ref, da_ref, db_ref, dg_ref, o_ref, acc_ref):
        k = pl.program_id(0)

        @pl.when(k == 0)
        def _():
            acc_ref[...] = jnp.zeros_like(acc_ref)

        hb = h_ref[...]
        acc_ref[:, _A_COLS] += _dot_tn(hb, da_ref[...])
        acc_ref[:, _B_COLS] += _dot_tn(hb, db_ref[...])
        acc_ref[:, _G_COLS] += _dot_tn(hb, dg_ref[...])

        @pl.when(k == nk - 1)
        def _():
            for i in range(N_CHIPS):
                o_ref[i] = acc_ref[:, i * IN_SHARD:(i + 1) * IN_SHARD].astype(BF16)

    return pl.pallas_call(
        body, name="grad_w_in", grid=(nk,),
        in_specs=[_row(tk, D_MODEL), _row(tk, A_DIM), _row(tk, B_DIM), _row(tk, G_DIM)],
        out_specs=_full((N_CHIPS, D_MODEL, IN_SHARD)), out_shape=_sds((N_CHIPS, D_MODEL, IN_SHARD), BF16),
        scratch_shapes=[pltpu.VMEM((D_MODEL, in_dim), F32)],
        compiler_params=_cp(("arbitrary",), 56),
    )(*_hbm(h, d_a, d_b, d_g))


def _local_step(x, target, g_mix, g_sgu, w_s, b_s, sinks, rel_bias, g_ffn, b_conv, g_final,
                w_in, w_conv, proj_weights, ffn_weights, on_grads, after=None):
    n_seq, seq, _ = x.shape
    T = n_seq * seq
    tm = min(ROW_TILE, seq)
    tw = min(GRAD_ROW_TILE, T)
    tf = min(WIDE_ROW_TILE, seq)
    x2 = x.reshape(T, D_MODEL)
    tgt = target.reshape(T, D_MODEL)
    b_st = b_s.T
    g_fin = g_final.reshape(1, D_MODEL)

    proj_g, proj_a, proj_b, h, y_a = _inproj(x2, g_mix, w_in, g_sgu, w_s, b_st, tm, after)
    y_b = _attn_fwd(proj_b, sinks, rel_bias, n_seq, seq)
    w_pa, w_pb, w_out = proj_weights(y_b)
    x1, merged = _merge_fwd(x2, y_a, y_b, proj_g, w_pa, w_pb, w_out, tm)
    w_up, w_down = ffn_weights(x1)
    upre, h2, gate, val = _upproj(x1, g_ffn, w_up, w_conv, b_conv, tf, seq)
    dx2, loss, gg_final = _ffn_down_loss(gate, val, x1, tgt, w_down, g_fin, tm)

    d_gate, d_val, gw_down, gb_g, gb_v = _ffn_bwd_act(gate, val, dx2, w_down, tw)
    gb_conv = jnp.concatenate([gb_g, gb_v], axis=1)
    d_upre, dx1, gg_ffn, gw_conv = _ffn_bwd_up(d_gate, d_val, upre, dx2, x1, g_ffn, w_conv, w_up, tf, seq)
    gw_up = _matmul_tn(h2, d_upre, 2 * D_FF // 4, min(4 * GRAD_ROW_TILE, T), "grad_w_up")
    sent = on_grads("ffn", dict(w_up=gw_up, w_down=gw_down))
    d_g, d_a, d_yb, gw_out, gw_pa, gw_pb, gw_s, gb_st, gg_sgu = _merge_bwd(
        dx1, merged, y_a, y_b, proj_g, proj_a, w_pa, w_pb, w_out, g_sgu, w_s, b_st, tw, sent)
    sent = on_grads("proj", dict(w_pa=gw_pa, w_pb=gw_pb, w_out=gw_out))
    d_b, g_sinks, g_rel = _attn_bwd(proj_b, d_yb, sinks, rel_bias, n_seq, seq, sent)
    gw_in = _grad_w_in(h, d_a, d_b, d_g, min(2 * GRAD_ROW_TILE, T))
    sent = on_grads("in", dict(w_in=gw_in))
    grad_x, gg_mix = _inproj_bwd(d_g, d_a, d_b, x2, dx1, g_mix, w_in, tm, sent)

    small = dict(g_mix=gg_mix, g_sgu=gg_sgu, w_s=gw_s, b_s=gb_st.T, sinks=g_sinks, rel_bias=g_rel,
                 g_ffn=gg_ffn, b_conv=gb_conv, g_final=gg_final, w_conv=gw_conv)
    big = dict(w_in=gw_in, w_pa=gw_pa, w_pb=gw_pb, w_out=gw_out, w_up=gw_up, w_down=gw_down)
    return loss, grad_x.reshape(x.shape), small, big


_MIXER = ("w_in", "w_pa", "w_pb", "w_out")
_FFN = ("w_up", "w_down")
_BIG = _MIXER + _FFN

CONV_ROWS = 6
_SMALL_AT = dict(loss=(0, 1, 1), g_sgu=(4, 1, A_WIDTH), sinks=(5, 1, N_HEADS), b_s=(8, A_GROUPS, CHUNK),
                 b_conv=(12, CONV_ROWS, D_MODEL), w_conv=(18, 3 * CONV_ROWS, D_MODEL),
                 g_final=(36, 1, D_MODEL), g_mix=(37, 1, D_MODEL), g_ffn=(38, 1, D_MODEL),
                 w_s=(40, A_GROUPS * CHUNK * CHUNK // D_MODEL, D_MODEL))
_REL_BIAS_AT = (0, A_WIDTH)
_SMALL_IN_CALL = ("g_final", "g_mix", "g_ffn", "g_sgu", "sinks", "b_s", "b_conv", "rel_bias", "w_s")
SMALL_ROWS = 104


def _pack_small(vals):
    def wide(a):
        return jnp.pad(a, ((0, 0), (0, CONV_ROWS * D_MODEL - a.shape[1]))).reshape(-1, D_MODEL)

    nr = _SMALL_AT["w_s"][1]
    w_s = vals["w_s"].reshape(D_MODEL // CHUNK, nr, CHUNK).transpose(1, 0, 2).reshape(nr, D_MODEL)
    laid = dict(vals, b_conv=wide(vals["b_conv"]), w_conv=wide(vals["w_conv"]), w_s=w_s)
    rows, at = [], 0
    for n, (r0, nr, nc) in _SMALL_AT.items():
        if r0 > at:
            rows.append(jnp.zeros((r0 - at, D_MODEL), F32))
        rows.append(jnp.pad(laid[n].astype(F32).reshape(nr, nc), ((0, 0), (0, D_MODEL - nc))))
        at = r0 + nr
    return lax.dynamic_update_slice(jnp.concatenate(rows, axis=0), vals["rel_bias"].T, _REL_BIAS_AT)


def _unwide(a, r):
    return a.reshape(r, CONV_ROWS * D_MODEL)[:, :2 * D_FF]


def _mesh_pos():
    return lax.axis_index("x"), lax.axis_index("y"), lax.axis_index("c")


def _other_chips(x, y):
    return [(1 - x, y), (x, 1 - y), (1 - x, 1 - y)]


def _remote(src, dst, send_sem, recv_sem, to):
    return pltpu.make_async_remote_copy(src_ref=src, dst_ref=dst, send_sem=send_sem, recv_sem=recv_sem,
                                        device_id=to, device_id_type=MESH)


def _own_slot(own, n, at):
    return lax.dynamic_update_slice(lax.empty((n,) + own.shape, own.dtype), own[None], (at,) + (0,) * own.ndim)


def _allgather_weights(stacks, wc_stack):
    names = list(stacks)
    n = len(names)

    def body(*refs):
        ins, outs = refs[:n + 1], refs[n + 1:2 * n + 2]
        send_sems, recv_sems = refs[2 * n + 2:]
        x, y, c = _mesh_pos()
        _handshake(_chip_peers(x, y, c) + _sibling_peers(x, y, c))
        me = 2 * x + y
        sibling = (x, y, 1 - c)
        chips = _other_chips(x, y)

        def half(ref, chip, hc):
            hr = ref.shape[1] // 2
            return ref.at[chip, pl.ds(hc * hr, hr), :]

        first = []
        for k in range(n):
            first += [_remote(half(ins[k], me, c), half(outs[k], me, c), send_sems.at[6 * k + j], recv_sems.at[6 * k + j], (cx, cy, c))
                      for j, (cx, cy) in enumerate(chips)]
        first += [_remote(ins[n].at[me], outs[n].at[me], send_sems.at[6 * n + j], recv_sems.at[6 * n + j], (cx, cy, c))
                  for j, (cx, cy) in enumerate(chips)]
        for cp in first:
            cp.start()
        passed = []
        for k in range(n):
            for j, (cx, cy) in enumerate(chips):
                landed = half(outs[k], 2 * cx + cy, c)
                _remote(landed, landed, send_sems.at[6 * k + j], recv_sems.at[6 * k + j], (x, y, c)).wait_recv()
                passed.append(_remote(landed, landed, send_sems.at[6 * k + 3 + j], recv_sems.at[6 * k + 3 + j], sibling))
                passed[-1].start()
        for k in range(n):
            for j, (cx, cy) in enumerate(chips):
                theirs = half(outs[k], 2 * cx + cy, 1 - c)
                _remote(theirs, theirs, send_sems.at[6 * k + 3 + j], recv_sems.at[6 * k + 3 + j], (x, y, c)).wait_recv()
        for j, (cx, cy) in enumerate(chips):
            slot = outs[n].at[2 * cx + cy]
            _remote(slot, slot, send_sems.at[6 * n + j], recv_sems.at[6 * n + j], (x, y, c)).wait_recv()
        for cp in first + passed:
            cp.wait_send()

    arrays = [stacks[k] for k in names] + [wc_stack]
    outs = pl.pallas_call(
        body, name="allgather_weights",
        in_specs=[HBM] * (n + 1), out_specs=[HBM] * (n + 1), input_output_aliases={k: k for k in range(n + 1)},
        out_shape=[_sds(a.shape, a.dtype) for a in arrays],
        scratch_shapes=[pltpu.SemaphoreType.DMA((6 * n + 3,)), pltpu.SemaphoreType.DMA((6 * n + 3,))],
        compiler_params=pltpu.CompilerParams(collective_id=_COLLECTIVE["gather_in"]),
    )(*arrays)
    return dict(zip(names, outs[:n])), outs[n]


_KIND = {"w_in": "stack", "w_pa": "col", "w_pb": "col", "w_up": "col", "w_out": "row", "w_down": "row"}


def _half_view(ref, kind, h):
    if kind == "stack":
        k = ref.shape[1] // 2
        return ref.at[:, pl.ds(h * k, k), :]
    if kind == "col":
        k = ref.shape[0] // 2
        return ref.at[pl.ds(h * k, k), :]
    k = ref.shape[1] // 2
    return ref.at[:, pl.ds(h * k, k)]


def _shard_view(ref, kind, i):
    if kind == "stack":
        return ref.at[i]
    if kind == "col":
        k = ref.shape[1] // N_CHIPS
        return ref.at[:, pl.ds(i * k, k)]
    k = ref.shape[0] // N_CHIPS
    return ref.at[pl.ds(i * k, k), :]


def _region_view(ref, kind, h):
    if kind == "row":
        k = ref.shape[1] // 2
        return ref.at[:, pl.ds(h * k, k)]
    k = ref.shape[0] // 2
    return ref.at[pl.ds(h * k, k), :]


def _half_shape(shape, kind):
    if kind == "stack":
        return (shape[0], shape[1] // 2, shape[2])
    return (shape[0] // 2, shape[1]) if kind == "col" else (shape[0], shape[1] // 2)


def _part_shape(half_shape, kind):
    if kind == "stack":
        return tuple(half_shape[1:])
    k, w = half_shape
    return (k, w // N_CHIPS) if kind == "col" else (k // N_CHIPS, w)


_DATAFLOW = pltpu.SideEffectType.DATAFLOW_SIDE_EFFECTING
_TOKEN = (SUBLANES, LANES)


_COLLECTIVE = {k: i for i, k in enumerate(
    [kind + "_" + g for kind in ("pair", "chip", "share") for g in ("ffn", "proj", "in")]
    + ["gather_proj", "gather_ffn", "gather_in", "forward_proj", "forward_ffn"])}


def _sibling_peers(x, y, c):
    return [(x, y, 1 - c)]


def _chip_peers(x, y, c):
    return [(cx, cy, c) for cx, cy in _other_chips(x, y)]


def _handshake(peers):
    barrier = pltpu.get_barrier_semaphore()
    for peer in peers:
        pl.semaphore_signal(barrier, inc=1, device_id=peer, device_id_type=MESH)
    pl.semaphore_wait(barrier, len(peers))


def _split_start(name, arrays, n_sems, issue, after=None, handshake=None):
    n = len(arrays)
    order = [] if after is None else [after]

    def body(*refs):
        base = n + len(order)
        if handshake is not None:
            _handshake(handshake[1](*_mesh_pos()))
        issue(refs[:n], refs[base], refs[base + 1])
        refs[-1][...] = jnp.zeros(_TOKEN, F32)

    params = dict(has_side_effects=_DATAFLOW)
    if handshake is not None:
        params["collective_id"] = handshake[0]
    outs = pl.pallas_call(
        body, name=name,
        in_specs=[HBM] * n + [ANY] * len(order), out_specs=[SEM, SEM] + [HBM] * n + [pl.BlockSpec(memory_space=pltpu.VMEM)],
        out_shape=[pltpu.SemaphoreType.DMA((n_sems,)), pltpu.SemaphoreType.DMA((n_sems,))]
        + [pltpu.HBM(a.shape, a.dtype) for a in arrays] + [_sds(_TOKEN, F32)],
        input_output_aliases={k: 2 + k for k in range(n)},
        compiler_params=pltpu.CompilerParams(**params),
    )(*[pltpu.with_memory_space_constraint(a, pltpu.HBM) for a in arrays], *order)
    return outs[0], outs[1], list(outs[2:2 + n]), outs[-1]


def _split_wait(name, started, waits, after):
    send_sems, recv_sems, arrays, _ = started
    n = len(arrays)

    def body(*refs):
        waits(refs[:n], refs[n], refs[n + 1])

    return pl.pallas_call(
        body, name=name,
        in_specs=[HBM] * n + [SEM, SEM, ANY], out_specs=[HBM] * n,
        out_shape=[pltpu.HBM(a.shape, a.dtype) for a in arrays],
        input_output_aliases={k: k for k in range(n)},
        compiler_params=pltpu.CompilerParams(has_side_effects=_DATAFLOW),
    )(*arrays, send_sems, recv_sems, after)


def _wait_both(src, dst, send_sem, recv_sem):
    x, y, c = _mesh_pos()
    cp = _remote(src, dst, send_sem, recv_sem, (x, y, c))
    cp.wait_send()
    cp.wait_recv()


def _pair_exchange_start(parts, tag, after):
    names = list(parts)
    n = len(names)
    lands = [lax.empty(_half_shape(parts[k].shape, _KIND[k]), parts[k].dtype) for k in names]

    def issue(refs, send_sems, recv_sems):
        x, y, c = _mesh_pos()
        for hc in range(2):
            @pl.when(c == hc)
            def _():
                for k in range(n):
                    _remote(_half_view(refs[k], _KIND[names[k]], 1 - hc), refs[n + k], send_sems.at[k], recv_sems.at[k],
                            (x, y, 1 - c)).start()

    return names, _split_start("grad_pair_exchange_start_" + tag, [parts[k] for k in names] + lands, n, issue, after,
                               (_COLLECTIVE["pair_" + tag], _sibling_peers))


def _pair_exchange_wait(pending, tag, after):
    names, started = pending
    n = len(names)

    def waits(refs, send_sems, recv_sems):
        for k in range(n):
            _wait_both(_half_view(refs[k], _KIND[names[k]], 0), refs[n + k], send_sems.at[k], recv_sems.at[k])

    outs = _split_wait("grad_pair_exchange_wait_" + tag, started, waits, after)
    return dict(zip(names, outs[:n])), dict(zip(names, outs[n:]))


def _half_blocks(shape, kind):
    if kind == "stack":
        _, k, w = shape
        return (N_CHIPS // 2, 1), (2, k // 2, w), (lambda i, r, s: (i, r, 0)), (lambda i, r, s: (i, s[1] + r, 0))
    k, w = shape
    if kind == "col":
        tr = STREAM_ROWS
        nb = k // 2 // tr
        return (nb,), (tr, w), (lambda r, s: (r, 0)), (lambda r, s: (s[1] * nb + r, 0))
    nb = 2
    return (nb,), (k // nb, w // 2), (lambda r, s: (r, 0)), (lambda r, s: (r, s[1]))


def _pair_add(part, from_sibling, name, pos):
    kind = _KIND[name]
    grid, block, half_map, full_map = _half_blocks(part.shape, kind)

    def body(s_ref, p_ref, q_ref, o_ref):
        o_ref[...] = (p_ref[...].astype(F32) + q_ref[...].astype(F32)).astype(BF16)

    return pl.pallas_call(
        body, name="grad_pair_add_" + name,
        grid_spec=pltpu.PrefetchScalarGridSpec(
            num_scalar_prefetch=1, grid=grid,
            in_specs=[pl.BlockSpec(block, full_map), pl.BlockSpec(block, half_map)],
            out_specs=pl.BlockSpec(block, half_map)),
        out_shape=_sds(from_sibling.shape, BF16),
        compiler_params=_cp(("arbitrary",) * len(grid), 40),
    )(pos, *_hbm(part, from_sibling))


def _chip_exchange_start(sums, tag, after):
    names = list(sums)
    n = len(names)
    lands = [lax.empty((3,) + _part_shape(sums[k].shape, _KIND[k]), sums[k].dtype) for k in names]

    def issue(refs, send_sems, recv_sems):
        x, y, c = _mesh_pos()
        me = 2 * x + y
        for i in range(N_CHIPS):
            xi, yi = i // 2, i % 2
            j = jnp.where(xi != x, jnp.where(yi != y, 2, 0), 1)

            @pl.when(i != me)
            def _():
                for k in range(n):
                    _remote(_shard_view(refs[k], _KIND[names[k]], i), refs[n + k].at[j], send_sems.at[3 * k + j],
                            recv_sems.at[3 * k + j], (xi, yi, c)).start()

    return names, _split_start("grad_chip_exchange_start_" + tag, [sums[k] for k in names] + lands, 3 * n, issue, after,
                               (_COLLECTIVE["chip_" + tag], _chip_peers))


def _chip_exchange_wait(pending, tag, after):
    names, started = pending
    n = len(names)

    def waits(refs, send_sems, recv_sems):
        for k in range(n):
            for j in range(3):
                _wait_both(_shard_view(refs[k], _KIND[names[k]], 0), refs[n + k].at[j], send_sems.at[3 * k + j], recv_sems.at[3 * k + j])

    return dict(zip(names, _split_wait("grad_chip_exchange_wait_" + tag, started, waits, after)[n:]))


def _allgather_start(stacks, tag, after):
    names = list(stacks)

    def issue(refs, send_sems, recv_sems):
        x, y, c = _mesh_pos()
        me = 2 * x + y
        for k, st in enumerate(refs):
            hr = st.shape[1] // 2
            mine = st.at[me, pl.ds(c * hr, hr), :]
            for j, (cx, cy) in enumerate(_other_chips(x, y)):
                _remote(mine, mine, send_sems.at[3 * k + j], recv_sems.at[3 * k + j], (cx, cy, c)).start()

    return names, _split_start("allgather_start_" + tag, [stacks[k] for k in names], 3 * len(names), issue, after,
                               (_COLLECTIVE["gather_" + tag], _chip_peers))


def _allgather_wait(pending, tag, after):
    names, started = pending

    def waits(refs, send_sems, recv_sems):
        for k, st in enumerate(refs):
            slot = st.at[0, pl.ds(0, st.shape[1] // 2), :]
            for j in range(3):
                _wait_both(slot, slot, send_sems.at[3 * k + j], recv_sems.at[3 * k + j])

    return dict(zip(names, _split_wait("allgather_wait_" + tag, started, waits, after)))


def _allgather_forward(stacks, tag):
    names = list(stacks)
    n = len(names)

    def body(*refs):
        ins, outs = refs[:n], refs[n:2 * n]
        send_sems, recv_sems = refs[2 * n:]
        x, y, c = _mesh_pos()
        _handshake(_sibling_peers(x, y, c))
        copies = []
        for k in range(n):
            hr = ins[k].shape[1] // 2
            for j, (cx, cy) in enumerate(_other_chips(x, y)):
                chip = 2 * cx + cy
                copies.append(_remote(ins[k].at[chip, pl.ds(c * hr, hr), :], outs[k].at[chip, pl.ds(c * hr, hr), :],
                                      send_sems.at[3 * k + j], recv_sems.at[3 * k + j], (x, y, 1 - c)))
        for cp in copies:
            cp.start()
        for cp in copies:
            cp.wait()

    arrays = [stacks[k] for k in names]
    outs = pl.pallas_call(
        body, name="allgather_forward_" + tag, in_specs=[HBM] * n, out_specs=[HBM] * n,
        input_output_aliases={k: k for k in range(n)},
        out_shape=[_sds(a.shape, a.dtype) for a in arrays],
        scratch_shapes=[pltpu.SemaphoreType.DMA((3 * n,)), pltpu.SemaphoreType.DMA((3 * n,))],
        compiler_params=pltpu.CompilerParams(collective_id=_COLLECTIVE["forward_" + tag]),
    )(*arrays)
    return dict(zip(names, outs))


def _owner_sum(part, from_sibling, from_chips, name, pos, shard_shape):
    kind = _KIND[name]
    _, pk, pw = from_chips.shape
    if kind == "row":
        nb = 1
        tr = pk // nb
        p_spec = pl.BlockSpec((tr, pw), lambda r, s: (s[0] * nb + r, s[1]))
        q_spec = pl.BlockSpec((tr, pw), lambda r, s: (s[0] * nb + r, 0))
        o_spec = pl.BlockSpec((tr, pw), lambda r, s: (r, s[1]))
    else:
        tr = STREAM_ROWS
        nb = pk // tr
        if kind == "stack":
            p_spec = pl.BlockSpec((None, tr, pw), lambda r, s: (s[0], s[1] * nb + r, 0))
            q_spec = pl.BlockSpec((None, tr, pw), lambda r, s: (s[0], r, 0))
        else:
            p_spec = pl.BlockSpec((tr, pw), lambda r, s: (s[1] * nb + r, s[0]))
            q_spec = pl.BlockSpec((tr, pw), lambda r, s: (r, s[0]))
        o_spec = pl.BlockSpec((tr, pw), lambda r, s: (s[1] * nb + r, 0))

    def body(s_ref, p_ref, q_ref, r_ref, o_ref):
        acc = p_ref[...].astype(F32) + q_ref[...].astype(F32)
        for j in range(3):
            acc = acc + r_ref[j].astype(F32)
        o_ref[...] = acc

    return pl.pallas_call(
        body, name="grad_owner_sum_" + name,
        grid_spec=pltpu.PrefetchScalarGridSpec(
            num_scalar_prefetch=1, grid=(nb,),
            in_specs=[p_spec, q_spec, pl.BlockSpec((3, tr, pw), lambda r, s: (0, r, 0))],
            out_specs=o_spec),
        out_shape=_sds(shard_shape, F32),
        compiler_params=_cp(("arbitrary",), 32),
    )(pos, *_hbm(part, from_sibling, from_chips))


def _pair_share_start(shards, tag, after):
    names = list(shards)

    def issue(refs, send_sems, recv_sems):
        x, y, c = _mesh_pos()
        for hc in range(2):
            @pl.when(c == hc)
            def _():
                for k, g in enumerate(refs):
                    mine = _region_view(g, _KIND[names[k]], hc)
                    _remote(mine, mine, send_sems.at[k], recv_sems.at[k], (x, y, 1 - c)).start()

    return names, _split_start("grad_pair_share_start_" + tag, [shards[k] for k in names], len(names), issue, after,
                               (_COLLECTIVE["share_" + tag], _sibling_peers))


def _pair_share_wait(pending, tag, after):
    names, started = pending

    def waits(refs, send_sems, recv_sems):
        for k, g in enumerate(refs):
            region = _region_view(g, _KIND[names[k]], 0)
            _wait_both(region, region, send_sems.at[k], recv_sems.at[k])

    return dict(zip(names, _split_wait("grad_pair_share_wait_" + tag, started, waits, after)))


def _small_exchange_start(slots, after):
    def issue(refs, send_sems, recv_sems):
        x, y, c = _mesh_pos()
        mine = refs[0].at[4 * x + 2 * y + c]
        k = 0
        for px in range(2):
            for py in range(2):
                for pc in range(2):
                    if px + py + pc:
                        peer = (1 - x if px else x, 1 - y if py else y, 1 - c if pc else c)
                        _remote(mine, mine, send_sems.at[k], recv_sems.at[k], peer).start()
                        k += 1

    return _split_start("small_exchange_start", [slots], N_DEV - 1, issue, after)


def _small_exchange_wait(started, after):
    def waits(refs, send_sems, recv_sems):
        slot = refs[0].at[0]
        for k in range(N_DEV - 1):
            _wait_both(slot, slot, send_sems.at[k], recv_sems.at[k])

    return _split_wait("small_exchange_wait", started, waits, after)[0]


def _adam_math(w, g, m, v):
    m = ADAM_B1 * m + (1.0 - ADAM_B1) * g
    v = ADAM_B2 * v + (1.0 - ADAM_B2) * (g * g)
    m_hat = m / (1.0 - ADAM_B1 ** ADAM_STEP)
    v_hat = v / (1.0 - ADAM_B2 ** ADAM_STEP)
    delta = -ADAM_LR * (m_hat / (jnp.sqrt(v_hat) + ADAM_EPS) + ADAM_WD * w)
    return delta, m, v


def _adamw(w, g, m, v, name):
    rows, cols = w.shape[0], w.shape[-1]
    fits = [t for t in range(SUBLANES, rows, SUBLANES) if rows % t == 0 and t * cols * 4 <= (3 << 19)]
    tr = max(fits) if fits and w.ndim == 2 else rows

    def body(w_ref, g_ref, m_ref, v_ref, d_ref, nm_ref, nv_ref, go_ref):
        g = g_ref[...]
        d, nm, nv = _adam_math(w_ref[...], g, m_ref[...], v_ref[...])
        d_ref[...] = d
        nm_ref[...] = nm
        nv_ref[...] = nv
        go_ref[...] = g

    spec = pl.BlockSpec((tr,) + w.shape[1:], lambda i: (i,) + (0,) * (w.ndim - 1))
    return pl.pallas_call(
        body, name=name, grid=(rows // tr,), in_specs=[spec] * 4, out_specs=[spec] * 4,
        out_shape=[_sds(w.shape, F32)] * 4, compiler_params=_cp(("arbitrary",)),
    )(*_hbm(w, g, m, v))


def _small_sum_adamw(gathered, w, m, v):
    names = _SMALL_IN_CALL
    n = len(names)

    def body(*refs):
        a_ref = refs[0]
        w_refs, m_refs, v_refs = refs[1:1 + n], refs[1 + n:1 + 2 * n], refs[1 + 2 * n:1 + 3 * n]
        sum_ref, loss_ref = refs[1 + 3 * n], refs[2 + 3 * n]
        outs = refs[3 + 3 * n:]
        g = a_ref[0]
        for k in range(1, N_DEV):
            g = g + a_ref[k]
        sum_ref[...] = g
        loss_ref[...] = g[0:1, 0:1]
        for i, name in enumerate(names):
            if name == "rel_bias":
                r0, c0 = _REL_BIAS_AT
                pieces = [(slice(None), g[r0:r0 + N_HEADS, c0:c0 + N_BUCKETS])]
            elif name == "b_conv":
                r0 = _SMALL_AT[name][0]
                pieces = [(slice(None), jnp.concatenate([g[r0 + k:r0 + k + 1, :] for k in range(CONV_ROWS)], axis=1)[:, :2 * D_FF])]
            elif name == "w_s":
                r0, nr, _ = _SMALL_AT[name]
                pieces = [(slice(nr * j, nr * (j + 1)), g[r0:r0 + nr, CHUNK * j:CHUNK * (j + 1)]) for j in range(D_MODEL // CHUNK)]
            else:
                r0, nr, nc = _SMALL_AT[name]
                pieces = [(slice(None), g[r0:r0 + nr, 0:nc])]
            for at, gp in pieces:
                d, nm, nv = _adam_math(w_refs[i][at], gp, m_refs[i][at], v_refs[i][at])
                for k, val in enumerate((gp, d, nm, nv)):
                    outs[4 * i + k][at] = val

    shapes = [w[k].shape for k in names]
    res = pl.pallas_call(
        body, name="small_sum_adamw",
        out_shape=[_sds((SMALL_ROWS, D_MODEL), F32), _sds((1, 1), F32)] + [_sds(s, F32) for s in shapes for _ in range(4)],
    )(gathered, *[w[k] for k in names], *[m[k] for k in names], *[v[k] for k in names])
    return res[0], res[1], {k: tuple(res[2 + 4 * i:6 + 4 * i]) for i, k in enumerate(names)}


_NAMES = ("g_mix", "w_in", "g_sgu", "w_s", "b_s", "sinks", "rel_bias", "w_pa", "w_pb", "w_out",
          "g_ffn", "w_up", "w_conv", "b_conv", "w_down", "g_final")

def kernel(x, g_mix, w_in, g_sgu, w_s, b_s, sinks, rel_bias, w_pa, w_pb, w_out, g_ffn, w_up, w_conv, b_conv, w_down, g_final, loss_target, m_g_mix, m_w_in, m_g_sgu, m_w_s, m_b_s, m_sinks, m_rel_bias, m_w_pa, m_w_pb, m_w_out, m_g_ffn, m_w_up, m_w_conv, m_b_conv, m_w_down, m_g_final, v_g_mix, v_w_in, v_g_sgu, v_w_s, v_b_s, v_sinks, v_rel_bias, v_w_pa, v_w_pb, v_w_out, v_g_ffn, v_w_up, v_w_conv, v_b_conv, v_w_down, v_g_final):
    w = dict(g_mix=g_mix, w_in=w_in, g_sgu=g_sgu, w_s=w_s, b_s=b_s, sinks=sinks, rel_bias=rel_bias, w_pa=w_pa, w_pb=w_pb,
             w_out=w_out, g_ffn=g_ffn, w_up=w_up, w_conv=w_conv, b_conv=b_conv, w_down=w_down, g_final=g_final)
    m = dict(g_mix=m_g_mix, w_in=m_w_in, g_sgu=m_g_sgu, w_s=m_w_s, b_s=m_b_s, sinks=m_sinks, rel_bias=m_rel_bias, w_pa=m_w_pa,
             w_pb=m_w_pb, w_out=m_w_out, g_ffn=m_g_ffn, w_up=m_w_up, w_conv=m_w_conv, b_conv=m_b_conv, w_down=m_w_down,
             g_final=m_g_final)
    v = dict(g_mix=v_g_mix, w_in=v_w_in, g_sgu=v_g_sgu, w_s=v_w_s, b_s=v_b_s, sinks=v_sinks, rel_bias=v_rel_bias, w_pa=v_w_pa,
             w_pb=v_w_pb, w_out=v_w_out, g_ffn=v_g_ffn, w_up=v_w_up, w_conv=v_w_conv, b_conv=v_b_conv, w_down=v_w_down,
             g_final=v_g_final)
    xi, yi, ci = _mesh_pos()
    me = 2 * xi + yi

    shard = {n: w[n][0] for n in _BIG}
    shard_shapes = {n: shard[n].shape for n in _BIG}
    wc_shard = w["w_conv"][0]
    wc_pad = jnp.pad(wc_shard, ((0, 5), (0, 0)))
    own = {n: _own_slot(shard[n].astype(BF16), N_CHIPS, me) for n in _BIG if n != "w_in"}
    own["w_in"] = _own_slot(shard["w_in"].T.astype(BF16), N_CHIPS, me)
    stacks, wc_all = _allgather_weights({"w_in": own["w_in"]}, _own_slot(wc_pad, N_CHIPS, me))
    proj_gather = _allgather_start({n: own[n] for n in _MIXER[1:]}, "proj", stacks["w_in"])
    ffn_gather = _allgather_start({n: own[n] for n in _FFN}, "ffn", proj_gather[1][-1])
    w_conv_full = jnp.concatenate([wc_all[i, :3] for i in range(N_CHIPS)], axis=1)
    w_in_full = stacks["w_in"].reshape(N_CHIPS * IN_SHARD, D_MODEL)
    pos = jnp.stack([me, ci])

    def proj_weights(done):
        st = _allgather_forward(_allgather_wait(proj_gather, "proj", done), "proj")
        return st["w_pa"], st["w_pb"], st["w_out"].reshape(D_MODEL, D_MODEL)

    def ffn_weights(done):
        st = _allgather_forward(_allgather_wait(ffn_gather, "ffn", done), "ffn")
        return st["w_up"], st["w_down"].reshape(D_FF, D_MODEL)

    groups = {}

    def stage1(group, parts):
        groups[group] = dict(parts=parts, pair=_pair_exchange_start(parts, group, None))
        return groups[group]["pair"][1][-1]

    def stage2(group, after, order_after):
        g = groups[group]
        g["parts"], g["sib"] = _pair_exchange_wait(g["pair"], group, after)
        g["chip"] = _chip_exchange_start({n: _pair_add(g["parts"][n], g["sib"][n], n, pos) for n in g["parts"]}, group, order_after)
        return g["chip"][1][-1]

    def stage3(group, after, order_after):
        g = groups[group]
        got = _chip_exchange_wait(g["chip"], group, after)
        g["share"] = _pair_share_start(
            {n: _owner_sum(g["parts"][n], g["sib"][n], got[n], n, pos, shard_shapes[n]) for n in g["parts"]}, group, order_after)
        return g["share"][1][-1]

    grads, deltas, new_m, new_v = {}, {}, {}, {}

    def stage4(group, after):
        g_shard = _pair_share_wait(groups[group]["share"], group, after)
        last = None
        for n in g_shard:
            g = _tie(g_shard[n], last)
            if n == "w_in":
                d, nm, nv, gt = _adamw(shard[n].T, g.T, m[n][0].T, v[n][0].T, "adamw_" + n)
                grads[n], deltas[n], new_m[n], new_v[n] = gt.T[None], d.T[None], nm.T[None], nv.T[None]
            else:
                d, nm, nv, go = _adamw(shard[n], g, m[n][0], v[n][0], "adamw_" + n)
                grads[n], deltas[n], new_m[n], new_v[n] = go[None], d[None], nm[None], nv[None]
            last = nv
        return last

    def on_grads(group, parts):
        token = stage1(group, parts)
        some = next(iter(parts.values()))
        if group == "proj":
            token = stage2("ffn", some, token)
        if group == "in":
            token = stage2("proj", some, token)
            token = stage3("ffn", some, token)
            token = stage2("in", token, token)
        return token

    loss, grad_x, small, big = _local_step(
        x, loss_target, w["g_mix"], w["g_sgu"], w["w_s"][0], w["b_s"][0], w["sinks"], w["rel_bias"], w["g_ffn"],
        w["b_conv"], w["g_final"], w_in_full, w_conv_full, proj_weights, ffn_weights, on_grads, ffn_gather[1][-1])

    small["loss"] = loss
    small_gather = _small_exchange_start(_own_slot(_pack_small(small), N_DEV, 2 * me + ci), grad_x)
    token = stage3("proj", grad_x, small_gather[-1])
    done = stage4("ffn", token)
    done = stage4("proj", done)
    token = stage3("in", done, None)
    all_small = _small_exchange_wait(small_gather, token)
    two_d = {n: (lambda a, n=n: a.reshape(_SMALL_AT[n][1:])) for n in _SMALL_IN_CALL}
    two_d["rel_bias"] = lambda a: a.T
    two_d["b_conv"] = lambda a: a
    two_d["w_s"] = lambda a: a.reshape(A_GROUPS * CHUNK, CHUNK)
    s_sum, s_loss, s_out = _small_sum_adamw(all_small, *[{n: two_d[n](p[n]) for n in _SMALL_IN_CALL} for p in (w, m, v)])
    stage4("in", s_sum)
    for n in _SMALL_IN_CALL:
        back = (lambda a: a.T) if n == "rel_bias" else (lambda a, n=n: a.reshape(w[n].shape))
        grads[n], deltas[n], new_m[n], new_v[n] = [back(a) for a in s_out[n]]

    def rows(n):
        r0, nr, _ = _SMALL_AT[n]
        return s_sum[r0:r0 + nr]

    wcols = wc_shard.shape[1]
    g_wc = lax.dynamic_slice(_unwide(rows("w_conv"), 3), (0, me * wcols), (3, wcols))
    taps = lambda a: a.transpose(1, 0, 2)
    res = _adamw(taps(w["w_conv"]), g_wc[:, None, :], taps(m["w_conv"]), taps(v["w_conv"]), "adamw_w_conv")
    deltas["w_conv"], new_m["w_conv"], new_v["w_conv"], grads["w_conv"] = [taps(a) for a in res]

    return (s_loss.reshape(()), grad_x, *[grads[n] for n in _NAMES], *[deltas[n] for n in _NAMES],
            *[new_m[n] for n in _NAMES], *[new_v[n] for n in _NAMES])
```

```python
import functools

import numpy as np
import jax
import jax.numpy as jnp
from jax import lax
from jax.experimental import pallas as pl
from jax.experimental.pallas import tpu as pltpu

F32 = jnp.float32
BF16 = jnp.bfloat16

D_MODEL = 1024
CHUNK = 128
A_GROUPS = 4
A_WIDTH = 512
N_HEADS = 8
HEAD_DIM = 64
Q_DIM = 512
KV_DIM = 128
N_BUCKETS = 32
MAX_DISTANCE = 128
D_FF = 2816
EPS = 1e-6
NEG_INF = -1e30
G_DIM = 2 * D_MODEL
A_DIM = 2 * A_WIDTH
B_DIM = Q_DIM + 2 * KV_DIM
LANES = 128
SUBLANES = 8
ROW_TILE = 512
WIDE_ROW_TILE = 256
COL_CHUNK = 512
GRAD_ROW_TILE = 512
STREAM_ROWS = 256
BF16_ROWS = 16
N_CHIPS = 4
N_DEV = 8

ADAM_LR = 0.001
ADAM_B1 = 0.9
ADAM_B2 = 0.999
ADAM_EPS = 1e-08
ADAM_WD = 0.01
ADAM_STEP = 10

MESH = pl.DeviceIdType.MESH
_GELU_C = 0.7978845608028654
_GELU_A = 0.044715


def _cp(sem=None, vmem_mb=None):
    kw = {}
    if sem is not None:
        kw["dimension_semantics"] = sem
    if vmem_mb is not None:
        kw["vmem_limit_bytes"] = vmem_mb << 20
    return pltpu.CompilerParams(**kw)


def _dot(a, b):
    return jnp.dot(a, b, preferred_element_type=F32)


def _dot_nt(a, b):
    return lax.dot_general(a, b, (((1,), (1,)), ((), ())), preferred_element_type=F32)


def _dot_tn(a, b):
    return lax.dot_general(a, b, (((0,), (0,)), ((), ())), preferred_element_type=F32)


def _rms_r(x):
    return lax.rsqrt(jnp.mean(x * x, axis=-1, keepdims=True) + EPS)


def _rms_bwd(dh, n, r, g):
    dn = dh * g
    return r * (dn - n * jnp.mean(dn * n, axis=-1, keepdims=True))


def _gelu(x):
    t = jnp.tanh(_GELU_C * (x + _GELU_A * (x * x * x)))
    return 0.5 * x * (1.0 + t), t


def _gelu_grad(x, t):
    return 0.5 * (1.0 + t) + 0.5 * x * (1.0 - t * t) * (_GELU_C * (1.0 + 3.0 * _GELU_A * x * x))


def _sigmoid(x):
    return 1.0 / (1.0 + jnp.exp(-x))


def _tie(x, dep):
    return x if dep is None else lax.optimization_barrier((x, dep))[0]


def _row(tm, w):
    return pl.BlockSpec((tm, w), lambda i: (i, 0))


def _full(shape):
    nd = len(shape)
    return pl.BlockSpec(tuple(shape), lambda *_: (0,) * nd)


def _resident(shape):
    nd = len(shape)
    return pl.BlockSpec(tuple(shape), lambda *_: (0,) * nd, pipeline_mode=pl.Buffered(1))


def _sds(shape, dtype):
    return pltpu.HBM(tuple(shape), dtype)


def _hbm(*arrays):
    return [pltpu.with_memory_space_constraint(a, pltpu.HBM) for a in arrays]


HBM = pl.BlockSpec(memory_space=pltpu.HBM)
ANY = pl.BlockSpec(memory_space=pl.ANY)
SEM = pl.BlockSpec(memory_space=pltpu.SEMAPHORE)


def _band_buckets():
    i = np.arange(CHUNK)[:, None]
    j = np.arange(2 * CHUNK)[None, :]
    dist = i + CHUNK - j
    valid = (dist >= 0) & (dist < CHUNK)
    d = np.clip(dist, 0, None)
    max_exact = N_BUCKETS // 2
    large = max_exact + (np.log(np.maximum(d, 1) / max_exact) / np.log(MAX_DISTANCE / max_exact)
                         * (N_BUCKETS - max_exact)).astype(np.int32)
    large = np.minimum(large, N_BUCKETS - 1)
    buckets = np.where(d < max_exact, d, large).astype(np.int32)
    return np.where(valid, buckets, -1).astype(np.int32)


_A_COLS = slice(0, A_DIM)
_B_COLS = slice(A_DIM, A_DIM + B_DIM)
_G_COLS = slice(A_DIM + B_DIM, A_DIM + B_DIM + G_DIM)


def _inproj(x2, g_mix, w_in, g_sgu, w_s, b_st, tm, after=None):
    T = x2.shape[0]
    order = [] if after is None else [after]

    def body(*refs):
        x_ref, g_ref, w_ref, gs_ref, ws_ref, bs_ref = refs[:6]
        pg_ref, pa_ref, pb_ref, h_ref, ya_ref = refs[6 + len(order):]
        x = x_ref[...]
        h = (x * _rms_r(x) * g_ref[...]).astype(BF16)
        h_ref[...] = h
        pa = _dot_nt(h, w_ref[_A_COLS, :]).astype(BF16)
        pa_ref[...] = pa
        pb_ref[...] = _dot_nt(h, w_ref[_B_COLS, :]).astype(BF16)
        pg_ref[...] = _dot_nt(h, w_ref[_G_COLS, :]).astype(BF16)
        _sgu_apply(pa.astype(F32), gs_ref[...], ws_ref, bs_ref, ya_ref)

    return pl.pallas_call(
        body, name="inproj", grid=(T // tm,),
        in_specs=[_row(tm, D_MODEL), _full(g_mix.shape), _resident(w_in.shape), _full(g_sgu.shape), _full(w_s.shape),
                  _full(b_st.shape)] + [ANY] * len(order),
        out_specs=[_row(tm, G_DIM), _row(tm, A_DIM), _row(tm, B_DIM), _row(tm, D_MODEL), _row(tm, A_WIDTH)],
        out_shape=[_sds((T, G_DIM), BF16), _sds((T, A_DIM), BF16), _sds((T, B_DIM), BF16), _sds((T, D_MODEL), BF16),
                   _sds((T, A_WIDTH), BF16)],
        compiler_params=_cp(("arbitrary",), 48),
    )(*_hbm(x2, g_mix, w_in, g_sgu, w_s, b_st), *order)


def _sgu_parts(p, g):
    pu = p[:, :A_WIDTH]
    pv = p[:, A_WIDTH:]
    u, tu = _gelu(pu)
    vv, tv = _gelu(pv)
    rv = _rms_r(vv)
    vn = (vv * rv * g).astype(BF16)
    return pu, pv, u, tu, vv, tv, rv, vn


def _tril():
    r = lax.broadcasted_iota(jnp.int32, (CHUNK, CHUNK), 0)
    c = lax.broadcasted_iota(jnp.int32, (CHUNK, CHUNK), 1)
    return r >= c


def _sgu_apply(p, g, ws_ref, bs_ref, y_ref):
    tril = _tril()
    _, _, u, _, _, _, _, vn = _sgu_parts(p, g)
    for gi in range(A_GROUPS):
        wm = jnp.where(tril, ws_ref[gi], 0.0).astype(BF16)
        bcol = bs_ref[:, gi:gi + 1]
        cs = slice(gi * CHUNK, (gi + 1) * CHUNK)
        for c in range(p.shape[0] // CHUNK):
            rs = slice(c * CHUNK, (c + 1) * CHUNK)
            s = _dot(wm, vn[rs, cs]) + bcol
            y_ref[rs, cs] = (u[rs, cs] * s).astype(BF16)


HEAD_ROWS = N_HEADS * CHUNK


def _head_rows(h):
    return slice(h * CHUNK, (h + 1) * CHUNK)


def _attn_setup(bias_scr, sink_scr, kvar_scr, qkv_ref, bk_ref, rel_ref, sink_ref):
    @pl.when(pl.program_id(0) == 0)
    def _():
        bk = bk_ref[...]
        for h in range(N_HEADS):
            acc = jnp.full((CHUNK, 2 * CHUNK), NEG_INF, F32)
            for b in range(N_BUCKETS):
                acc = jnp.where(bk == b, rel_ref[b, h], acc)
            bias_scr[_head_rows(h), :] = acc
            sink_scr[_head_rows(h), :] = jnp.full((CHUNK, LANES), sink_ref[0, h], F32)

    seq = qkv_ref.shape[0]
    rows_per = 2 * CHUNK
    for is_v in range(2):
        c0 = Q_DIM + is_v * KV_DIM
        for r in range(seq // rows_per):
            rs = slice(r * rows_per, (r + 1) * rows_per)
            a = qkv_ref[rs, c0:c0 + KV_DIM].astype(F32)
            lane = lax.broadcasted_iota(jnp.int32, a.shape, 1)
            lo = jnp.where(lane < HEAD_DIM, a, 0.0)
            hi = jnp.where(lane >= HEAD_DIM, a, 0.0)
            kvar_scr[4 * is_v + 0, rs, :] = lo.astype(BF16)
            kvar_scr[4 * is_v + 1, rs, :] = pltpu.roll(lo, HEAD_DIM, 1).astype(BF16)
            kvar_scr[4 * is_v + 2, rs, :] = pltpu.roll(hi, HEAD_DIM, 1).astype(BF16)
            kvar_scr[4 * is_v + 3, rs, :] = hi.astype(BF16)


def _rowsum(a, ones):
    hi = a.astype(BF16)
    lo = (a - hi.astype(F32)).astype(BF16)
    return _dot(hi, ones) + _dot(lo, ones)


def _both(a):
    return jnp.concatenate([a, a], axis=1)


def _attn_probs(qkv_ref, r0, n, kv, bias_scr, sink_scr, ones):
    s = jnp.concatenate([_dot_nt(qkv_ref[pl.ds(r0, CHUNK), (h // 2) * LANES:(h // 2 + 1) * LANES], kv[h // 4][h % 2])
                         for h in range(N_HEADS)], axis=0)
    s = s * (HEAD_DIM ** -0.5) + bias_scr[...]
    col = lax.broadcasted_iota(jnp.int32, s.shape, 1)
    s = jnp.where((col < CHUNK) & (n == 0), NEG_INF, s)
    sink = sink_scr[...]
    m = jnp.maximum(jnp.max(s, axis=-1, keepdims=True), sink)
    p = jnp.exp(s - _both(m))
    es = jnp.exp(sink - m)
    inv = 1.0 / (_dot(p.astype(BF16), ones) + es)
    return p * _both(inv), es * inv


def _attn_block_inputs(kvar_scr, n):
    r0 = pl.multiple_of(n * CHUNK, CHUNK)
    rp = pl.multiple_of(jnp.maximum(n - 1, 0) * CHUNK, CHUNK)

    def both(idx):
        return jnp.concatenate([kvar_scr[idx, pl.ds(rp, CHUNK), :], kvar_scr[idx, pl.ds(r0, CHUNK), :]], axis=0)

    kv = ((both(0), both(1)), (both(2), both(3)))
    vv = ((both(4), both(5)), (both(6), both(7)))
    return r0, kv, vv


def _attn_fwd(proj_b, sinks, rel_bias, n_seq, seq):
    nb = seq // CHUNK
    bk = jnp.asarray(_band_buckets())

    def body(qkv_ref, bk_ref, rel_ref, sink_ref, o_ref, bias_scr, sink_scr, kvar_scr):
        _attn_setup(bias_scr, sink_scr, kvar_scr, qkv_ref, bk_ref, rel_ref, sink_ref)
        ones = jnp.ones((2 * CHUNK, LANES), BF16)

        def blk(n, carry):
            r0, kv, vv = _attn_block_inputs(kvar_scr, n)
            prob, _ = _attn_probs(qkv_ref, r0, n, kv, bias_scr, sink_scr, ones)
            pb = prob.astype(BF16)
            for pr in range(N_HEADS // 2):
                acc = _dot(pb[_head_rows(2 * pr)], vv[pr // 2][0]) + _dot(pb[_head_rows(2 * pr + 1)], vv[pr // 2][1])
                o_ref[pl.ds(r0, CHUNK), pr * LANES:(pr + 1) * LANES] = acc.astype(BF16)
            return carry

        lax.fori_loop(0, nb, blk, 0)

    smem = pl.BlockSpec(memory_space=pltpu.SMEM)
    return pl.pallas_call(
        body, name="attn_fwd", grid=(n_seq,),
        in_specs=[_row(seq, B_DIM), _full(bk.shape), smem, smem],
        out_specs=_row(seq, Q_DIM), out_shape=_sds((n_seq * seq, Q_DIM), BF16),
        scratch_shapes=[pltpu.VMEM((HEAD_ROWS, 2 * CHUNK), F32), pltpu.VMEM((HEAD_ROWS, LANES), F32),
                        pltpu.VMEM((8, seq, KV_DIM), BF16)],
        compiler_params=_cp(("arbitrary",), 40),
    )(*_hbm(proj_b, bk), rel_bias, sinks)


def _dot_stacked(a, w_ref):
    return jnp.concatenate([_dot(a, w_ref[i]) for i in range(N_CHIPS)], axis=1)


def _dot_nt_stacked(a, w_ref):
    w = w_ref.shape[2]
    acc = _dot_nt(a[:, :w], w_ref[0])
    for i in range(1, N_CHIPS):
        acc = acc + _dot_nt(a[:, i * w:(i + 1) * w], w_ref[i])
    return acc


def _merge_fwd(x2, y_a, y_b, proj_g, w_pa, w_pb, w_out, tm):
    T = x2.shape[0]

    def body(x_ref, ya_ref, yb_ref, g_ref, wpa_ref, wpb_ref, wo_ref, x1_ref, mg_ref):
        g = g_ref[...].astype(F32)
        pa = _dot_stacked(ya_ref[...], wpa_ref)
        pb = _dot_stacked(yb_ref[...], wpb_ref)
        merged = (_sigmoid(g[:, :D_MODEL]) * pa + _sigmoid(g[:, D_MODEL:]) * pb).astype(BF16)
        mg_ref[...] = merged
        x1_ref[...] = x_ref[...] + _dot(merged, wo_ref[...])

    return pl.pallas_call(
        body, name="merge_fwd", grid=(T // tm,),
        in_specs=[_row(tm, D_MODEL), _row(tm, A_WIDTH), _row(tm, Q_DIM), _row(tm, G_DIM),
                  _resident(w_pa.shape), _resident(w_pb.shape), _resident(w_out.shape)],
        out_specs=[_row(tm, D_MODEL), _row(tm, D_MODEL)],
        out_shape=[_sds((T, D_MODEL), F32), _sds((T, D_MODEL), BF16)],
        compiler_params=_cp(("arbitrary",), 40),
    )(*_hbm(x2, y_a, y_b, proj_g, w_pa, w_pb, w_out))


def _upproj(x1, g_ffn, w_up, w_conv, b_conv, tm, seq):
    T = x1.shape[0]
    cw = w_up.shape[2]
    tiles_per_seq = seq // tm

    def body(x_ref, g_ref, w_hbm, wc_ref, bc_ref, u_ref, h_ref, gate_ref, val_ref, tail_scr, w_ref, w_sems):
        first = pl.program_id(0) == 0
        _stage_blocks(first, w_hbm, w_ref, w_sems, range(N_CHIPS), start=True)
        at_start = (pl.program_id(0) % tiles_per_seq) == 0
        x = x_ref[...]
        h = (x * _rms_r(x) * g_ref[...]).astype(BF16)
        h_ref[...] = h
        for i in range(N_CHIPS):
            cs = slice(i * cw, (i + 1) * cw)
            _stage_blocks(first, w_hbm, w_ref, w_sems, [i], start=False)
            u = _dot(h, w_ref[i])
            u_ref[:, cs] = u.astype(BF16)
            hl = jnp.where(at_start, 0.0, tail_scr[SUBLANES - 2:SUBLANES, cs])
            tail_scr[:, cs] = u[tm - SUBLANES:]
            up = _conv_out((u, _shift_down(u, hl, 1), _shift_down(u, hl, 2)), wc_ref[:, cs], bc_ref[:, cs])
            out_ref = gate_ref if i < N_CHIPS // 2 else val_ref
            out_ref[:, (i % 2) * cw:(i % 2 + 1) * cw] = up.astype(BF16)

    return pl.pallas_call(
        body, name="upproj", grid=(T // tm,),
        in_specs=[_row(tm, D_MODEL), _full(g_ffn.shape), HBM, _full(w_conv.shape), _full(b_conv.shape)],
        out_specs=[_row(tm, 2 * D_FF), _row(tm, D_MODEL), _row(tm, D_FF), _row(tm, D_FF)],
        out_shape=[_sds((T, 2 * D_FF), BF16), _sds((T, D_MODEL), BF16), _sds((T, D_FF), BF16), _sds((T, D_FF), BF16)],
        scratch_shapes=[pltpu.VMEM((SUBLANES, 2 * D_FF), F32), pltpu.VMEM(w_up.shape, w_up.dtype), pltpu.SemaphoreType.DMA((N_CHIPS,))],
        compiler_params=_cp(("arbitrary",), 56),
    )(*_hbm(x1, g_ffn, w_up, w_conv, b_conv))


def _stage_blocks(first, w_hbm, w_ref, sems, blocks, start):
    @pl.when(first)
    def _():
        for i in blocks:
            cp = pltpu.make_async_copy(w_hbm.at[i], w_ref.at[i], sems.at[i])
            cp.start() if start else cp.wait()


def _shift_down(u, halo, k):
    rolled = pltpu.roll(u, k, 0)
    head = rolled[:SUBLANES]
    row = lax.broadcasted_iota(jnp.int32, head.shape, 0)
    if k == 1:
        head = jnp.where(row == 0, halo[1:2], head)
    else:
        head = jnp.where(row == 0, halo[0:1], jnp.where(row == 1, halo[1:2], head))
    return jnp.concatenate([head, rolled[SUBLANES:]], axis=0)


def _shift_up(d, halo, k):
    tm = d.shape[0]
    rolled = pltpu.roll(d, tm - k, 0)
    tail = rolled[tm - SUBLANES:]
    row = lax.broadcasted_iota(jnp.int32, tail.shape, 0)
    if k == 1:
        tail = jnp.where(row == SUBLANES - 1, halo[0:1], tail)
    else:
        tail = jnp.where(row == SUBLANES - 2, halo[0:1], jnp.where(row == SUBLANES - 1, halo[1:2], tail))
    return jnp.concatenate([rolled[:tm - SUBLANES], tail], axis=0)


def _conv_out(taps, wc, bc):
    u, u1, u2 = taps
    return wc[0:1] * u2 + wc[1:2] * u1 + wc[2:3] * u + bc


def _ffn_down_loss(gate, val, x1, target, w_down, g_final, tm):
    T = x1.shape[0]
    half = D_FF // 2

    sub = min(tm, 128)

    def body(gt_ref, vl_ref, x1_ref, t_ref, wd_hbm, g_ref, dx2_ref, loss_ref, gg_ref, wd_ref, wd_sems):
        i = pl.program_id(0)
        _stage_blocks(i == 0, wd_hbm, wd_ref, wd_sems, range(2), start=True)
        g = g_ref[...]

        def down(rs):
            acc = jnp.zeros((sub, D_MODEL), F32)
            for j in range(2):
                gc = slice(j * half, (j + 1) * half)
                gate = gt_ref[rs, gc].astype(F32)
                act = (gate * _sigmoid(gate) * vl_ref[rs, gc].astype(F32)).astype(BF16)
                if rs.start == 0:
                    _stage_blocks(i == 0, wd_hbm, wd_ref, wd_sems, [j], start=False)
                acc = acc + _dot(act, wd_ref[j])
            return acc

        def norm_loss(rs, acc):
            x2 = x1_ref[rs, :] + acc
            r = _rms_r(x2)
            n = x2 * r
            diff = n * g - t_ref[rs, :]
            dy = diff * (1.0 / D_MODEL)
            dx2_ref[rs, :] = _rms_bwd(dy, n, r, g)
            return (jnp.sum(jnp.mean(diff * diff, axis=-1, keepdims=True), axis=0, keepdims=True),
                    jnp.sum(dy * n, axis=0, keepdims=True))

        subs = [slice(s0, s0 + sub) for s0 in range(0, tm, sub)]
        accs = [down(rs) for rs in subs]
        parts = [norm_loss(rs, acc) for rs, acc in zip(subs, accs)]

        @pl.when(i == 0)
        def _():
            loss_ref[...] = jnp.zeros_like(loss_ref)
            gg_ref[...] = jnp.zeros_like(gg_ref)

        loss_ref[...] += 0.5 * sum(p[0] for p in parts)
        gg_ref[...] += sum(p[1] for p in parts)

    return pl.pallas_call(
        body, name="ffn_down_loss", grid=(T // tm,),
        in_specs=[_row(tm, D_FF), _row(tm, D_FF), _row(tm, D_MODEL), _row(tm, D_MODEL), HBM, _full(g_final.shape)],
        out_specs=[_row(tm, D_MODEL), _full((1, 1)), _full((1, D_MODEL))],
        out_shape=[_sds((T, D_MODEL), F32), _sds((1, 1), F32), _sds((1, D_MODEL), F32)],
        scratch_shapes=[pltpu.VMEM((2, half, D_MODEL), w_down.dtype), pltpu.SemaphoreType.DMA((2,))],
        compiler_params=_cp(("arbitrary",), 48),
    )(*_hbm(gate, val, x1, target, w_down.reshape(2, half, D_MODEL), g_final))


def _ffn_bwd_act(gate, val, dx2, w_down, tm):
    T = dx2.shape[0]
    half = D_FF // 2
    nt = T // tm

    def body(g_ref, v_ref, dx_ref, wd_ref, dg_ref, dv_ref, gwd_out, gbg_ref, gbv_ref, gwd_ref):
        i = pl.program_id(1)

        @pl.when(i == 0)
        def _():
            for r in (gwd_ref, gbg_ref, gbv_ref):
                r[...] = jnp.zeros_like(r)

        dx = dx_ref[...].astype(BF16)
        for c0 in range(0, half, COL_CHUNK):
            cs = slice(c0, min(c0 + COL_CHUNK, half))
            gate = g_ref[:, cs].astype(F32)
            val = v_ref[:, cs].astype(F32)
            sg = _sigmoid(gate)
            silu = gate * sg
            d_act = _dot_nt(dx, wd_ref[cs, :])
            d_val = d_act * silu
            d_gate = d_act * val * (sg * (1.0 + gate * (1.0 - sg)))
            dg_ref[:, cs] = d_gate.astype(BF16)
            dv_ref[:, cs] = d_val.astype(BF16)
            gwd_ref[cs, :] += _dot_tn((silu * val).astype(BF16), dx)
            gbg_ref[:, cs] += jnp.sum(d_gate, axis=0, keepdims=True)
            gbv_ref[:, cs] += jnp.sum(d_val, axis=0, keepdims=True)

        @pl.when(i == nt - 1)
        def _():
            gwd_out[...] = gwd_ref[...].astype(BF16)

    tile = pl.BlockSpec((tm, half), lambda j, i: (i, j))
    vec = pl.BlockSpec((1, half), lambda j, i: (0, j))
    wrows = pl.BlockSpec((half, D_MODEL), lambda j, i: (j, 0))
    return pl.pallas_call(
        body, name="ffn_bwd_act", grid=(2, nt),
        in_specs=[tile, tile, pl.BlockSpec((tm, D_MODEL), lambda j, i: (i, 0)), wrows],
        out_specs=[tile, tile, wrows, vec, vec],
        out_shape=[_sds((T, D_FF), BF16), _sds((T, D_FF), BF16), _sds((D_FF, D_MODEL), BF16),
                   _sds((1, D_FF), F32), _sds((1, D_FF), F32)],
        scratch_shapes=[pltpu.VMEM((half, D_MODEL), F32)],
        compiler_params=_cp(("arbitrary", "arbitrary"), 56),
    )(*_hbm(gate, val, dx2, w_down))


def _ffn_bwd_up(d_gate, d_val, upre, dx2, x1, g_ffn, w_conv, w_up, tm, seq):
    T = dx2.shape[0]
    tiles_per_seq = seq // tm
    k16 = tm // BF16_ROWS
    n16 = T // BF16_ROWS
    cw = D_FF // 2

    def body(dg_ref, dv_ref, hg_ref, hv_ref, u_ref, dx2_ref, x1_ref, g_ref, wc_ref, wu_hbm, du_ref, dx1_ref, gg_ref, gwc_ref,
             wu_ref, wu_sems):
        i = pl.program_id(0)
        _stage_blocks(i == 0, wu_hbm, wu_ref, wu_sems, range(N_CHIPS), start=True)
        at_end = (i % tiles_per_seq) == tiles_per_seq - 1

        @pl.when(i == 0)
        def _():
            gg_ref[...] = jnp.zeros_like(gg_ref)
            gwc_ref[...] = jnp.zeros_like(gwc_ref)

        dh = jnp.zeros((tm, D_MODEL), F32)
        for j in range(4):
            src, hsrc = (dg_ref, hg_ref) if j < 2 else (dv_ref, hv_ref)
            ls = slice((j % 2) * cw, (j % 2 + 1) * cw)
            cs = slice(j * cw, (j + 1) * cw)
            d = src[:, ls].astype(F32)
            hl = hsrc[:, ls].astype(F32)[0:2]
            hl = jnp.where(at_end, 0.0, hl)
            wc = wc_ref[:, cs]
            d1 = _shift_up(d, hl, 1)
            d2 = _shift_up(d, hl, 2)
            du = (wc[2:3] * d + wc[1:2] * d1 + wc[0:1] * d2).astype(BF16)
            du_ref[:, cs] = du
            _stage_blocks(i == 0, wu_hbm, wu_ref, wu_sems, [j], start=False)
            dh = dh + _dot_nt(du, wu_ref[j])
            u = u_ref[:, cs].astype(F32)
            gwc_ref[0:1, cs] += jnp.sum(d2 * u, axis=0, keepdims=True)
            gwc_ref[1:2, cs] += jnp.sum(d1 * u, axis=0, keepdims=True)
            gwc_ref[2:3, cs] += jnp.sum(d * u, axis=0, keepdims=True)
        x = x1_ref[...]
        r = _rms_r(x)
        n = x * r
        dx1_ref[...] = dx2_ref[...] + _rms_bwd(dh, n, r, g_ref[...])
        gg_ref[...] += jnp.sum(dh * n, axis=0, keepdims=True)

    nxt = pl.BlockSpec((BF16_ROWS, D_FF), lambda i: (jnp.minimum((i + 1) * k16, n16 - 1), 0))
    return pl.pallas_call(
        body, name="ffn_bwd_up", grid=(T // tm,),
        in_specs=[_row(tm, D_FF), _row(tm, D_FF), nxt, nxt, _row(tm, 2 * D_FF), _row(tm, D_MODEL), _row(tm, D_MODEL),
                  _full(g_ffn.shape), _full(w_conv.shape), HBM],
        out_specs=[_row(tm, 2 * D_FF), _row(tm, D_MODEL), _full((1, D_MODEL)), _full((3, 2 * D_FF))],
        out_shape=[_sds((T, 2 * D_FF), BF16), _sds((T, D_MODEL), F32), _sds((1, D_MODEL), F32), _sds((3, 2 * D_FF), F32)],
        scratch_shapes=[pltpu.VMEM(w_up.shape, w_up.dtype), pltpu.SemaphoreType.DMA((N_CHIPS,))],
        compiler_params=_cp(("arbitrary",), 56),
    )(*_hbm(d_gate, d_val, d_gate, d_val, upre, dx2, x1, g_ffn, w_conv, w_up))


def _matmul_tn(a, b, tn, tk, name):
    T, M = a.shape
    N = b.shape[1]
    nk = T // tk

    def body(a_ref, b_ref, o_ref, acc_ref):
        k = pl.program_id(1)

        @pl.when(k == 0)
        def _():
            acc_ref[...] = jnp.zeros_like(acc_ref)

        acc_ref[...] += _dot_tn(a_ref[...], b_ref[...])

        @pl.when(k == nk - 1)
        def _():
            o_ref[...] = acc_ref[...].astype(BF16)

    return pl.pallas_call(
        body, name=name, grid=(N // tn, nk),
        in_specs=[pl.BlockSpec((tk, M), lambda j, k: (k, 0)), pl.BlockSpec((tk, tn), lambda j, k: (k, j))],
        out_specs=pl.BlockSpec((M, tn), lambda j, k: (0, j)), out_shape=_sds((M, N), BF16),
        scratch_shapes=[pltpu.VMEM((M, tn), F32)],
        compiler_params=_cp(("arbitrary", "arbitrary"), 48),
    )(*_hbm(a, b))


def _merge_bwd(dx1, merged, y_a, y_b, proj_g, proj_a, w_pa, w_pb, w_out, g_sgu, w_s, b_st, tm, after=None):
    T = dx1.shape[0]

    nt = T // tm
    pshape = (A_WIDTH, D_MODEL)
    order = [] if after is None else [after]

    def body(*refs):
        dx_ref, mg_ref, ya_ref, yb_ref, g_ref, p_ref, wpa_ref, wpb_ref, wo_ref, gs_ref, ws_ref, bs_ref = refs[:12]
        (dg_ref, da_ref, dyb_ref, gwo_out, gwpa_out, gwpb_out, gws_ref, gbs_ref, gg_ref,
         gwo_ref, gwpa_ref, gwpb_ref) = refs[12 + len(order):]
        i = pl.program_id(0)

        @pl.when(i == 0)
        def _():
            for r in (gwo_ref, gwpa_ref, gwpb_ref, gws_ref, gbs_ref, gg_ref):
                r[...] = jnp.zeros_like(r)

        dx = dx_ref[...].astype(BF16)
        dm = _dot_nt(dx, wo_ref[...])
        g = g_ref[...].astype(F32)
        ya = ya_ref[...]
        yb = yb_ref[...]
        pa = _dot_stacked(ya, wpa_ref)
        pb = _dot_stacked(yb, wpb_ref)
        sa = _sigmoid(g[:, :D_MODEL])
        sb = _sigmoid(g[:, D_MODEL:])
        dpa = (dm * sa).astype(BF16)
        dpb = (dm * sb).astype(BF16)
        dg_ref[:, :D_MODEL] = (dm * pa * (sa * (1.0 - sa))).astype(BF16)
        dg_ref[:, D_MODEL:] = (dm * pb * (sb * (1.0 - sb))).astype(BF16)
        d_ya = _dot_nt_stacked(dpa, wpa_ref).astype(BF16)
        dyb_ref[...] = _dot_nt_stacked(dpb, wpb_ref).astype(BF16)
        _sgu_bwd_apply(p_ref[...].astype(F32), d_ya.astype(F32), gs_ref[...], ws_ref, bs_ref, da_ref, gws_ref, gbs_ref, gg_ref)
        gwo_ref[...] += _dot_tn(mg_ref[...], dx)
        gwpa_ref[...] += _dot_tn(ya, dpa)
        gwpb_ref[...] += _dot_tn(yb, dpb)

        @pl.when(i == nt - 1)
        def _():
            gwo_out[...] = gwo_ref[...].astype(BF16)
            gwpa_out[...] = gwpa_ref[...].astype(BF16)
            gwpb_out[...] = gwpb_ref[...].astype(BF16)

    return pl.pallas_call(
        body, name="merge_bwd", grid=(nt,),
        in_specs=[_row(tm, D_MODEL), _row(tm, D_MODEL), _row(tm, A_WIDTH), _row(tm, Q_DIM), _row(tm, G_DIM), _row(tm, A_DIM),
                  _resident(w_pa.shape), _resident(w_pb.shape), _resident(w_out.shape),
                  _full(g_sgu.shape), _full(w_s.shape), _full(b_st.shape)] + [ANY] * len(order),
        out_specs=[_row(tm, G_DIM), _row(tm, A_DIM), _row(tm, Q_DIM),
                   _full(w_out.shape), _full(pshape), _full(pshape), _full(w_s.shape), _full(b_st.shape), _full(g_sgu.shape)],
        out_shape=[_sds((T, G_DIM), BF16), _sds((T, A_DIM), BF16), _sds((T, Q_DIM), BF16),
                   _sds(w_out.shape, BF16), _sds(pshape, BF16), _sds(pshape, BF16),
                   _sds(w_s.shape, F32), _sds(b_st.shape, F32), _sds(g_sgu.shape, F32)],
        scratch_shapes=[pltpu.VMEM(w_out.shape, F32), pltpu.VMEM(pshape, F32), pltpu.VMEM(pshape, F32)],
        compiler_params=_cp(("arbitrary",), 56),
    )(*_hbm(dx1, merged, y_a, y_b, proj_g, proj_a, w_pa, w_pb, w_out, g_sgu, w_s, b_st), *order)


def _sgu_bwd_apply(p, dy, g, ws_ref, bs_ref, dp_ref, gws_ref, gbs_ref, gg_ref):
    tril = _tril()
    pu, pv, u, tu, vv, tv, rv, vn = _sgu_parts(p, g)
    du_cols = []
    dvn_cols = []
    for gi in range(A_GROUPS):
        wm = jnp.where(tril, ws_ref[gi], 0.0).astype(BF16)
        wmt = wm.astype(F32).T.astype(BF16)
        bcol = bs_ref[:, gi:gi + 1]
        cs = slice(gi * CHUNK, (gi + 1) * CHUNK)
        du_rows = []
        dvn_rows = []
        gw = jnp.zeros((CHUNK, CHUNK), F32)
        gb = jnp.zeros((CHUNK, 1), F32)
        for c in range(p.shape[0] // CHUNK):
            rs = slice(c * CHUNK, (c + 1) * CHUNK)
            vn_c = vn[rs, cs]
            s = _dot(wm, vn_c) + bcol
            dy_c = dy[rs, cs]
            ds = dy_c * u[rs, cs]
            du_rows.append(dy_c * s)
            dsb = ds.astype(BF16)
            gw = gw + _dot_nt(dsb, vn_c)
            gb = gb + jnp.sum(ds, axis=-1, keepdims=True)
            dvn_rows.append(_dot(wmt, dsb))
        gws_ref[gi] += jnp.where(tril, gw, 0.0)
        gbs_ref[:, gi:gi + 1] += gb
        du_cols.append(jnp.concatenate(du_rows, axis=0))
        dvn_cols.append(jnp.concatenate(dvn_rows, axis=0))
    du = jnp.concatenate(du_cols, axis=1)
    dvn = jnp.concatenate(dvn_cols, axis=1)
    vhat = vv * rv
    gg_ref[...] += jnp.sum(dvn * vhat, axis=0, keepdims=True)
    dvv = _rms_bwd(dvn, vhat, rv, g)
    dp_ref[:, :A_WIDTH] = (du * _gelu_grad(pu, tu)).astype(BF16)
    dp_ref[:, A_WIDTH:] = (dvv * _gelu_grad(pv, tv)).astype(BF16)


def _attn_bwd(proj_b, d_yb, sinks, rel_bias, n_seq, seq, after=None):
    nb = seq // CHUNK
    bk = jnp.asarray(_band_buckets())
    order = [] if after is None else [after]

    def body(*refs):
        qkv_ref, do_ref, bk_ref, rel_ref, sink_ref = refs[:5]
        (d_ref, gs_ref, gr_ref, bias_scr, sink_scr, kvar_scr, dbias_scr, dk_scr, dv_scr, ds_scr) = refs[5 + len(order):]
        b = pl.program_id(0)
        _attn_setup(bias_scr, sink_scr, kvar_scr, qkv_ref, bk_ref, rel_ref, sink_ref)
        ones = jnp.ones((2 * CHUNK, LANES), BF16)

        @pl.when(b == 0)
        def _():
            dbias_scr[...] = jnp.zeros_like(dbias_scr)
            ds_scr[...] = jnp.zeros_like(ds_scr)

        dk_scr[...] = jnp.zeros_like(dk_scr)
        dv_scr[...] = jnp.zeros_like(dv_scr)

        def transposed(a):
            return a.astype(F32).T.astype(BF16)

        def blk(n, carry):
            r0, kv, vv = _attn_block_inputs(kvar_scr, n)
            prob, psink = _attn_probs(qkv_ref, r0, n, kv, bias_scr, sink_scr, ones)
            dp = jnp.concatenate([_dot_nt(do_ref[pl.ds(r0, CHUNK), (h // 2) * LANES:(h // 2 + 1) * LANES], vv[h // 4][h % 2])
                                  for h in range(N_HEADS)], axis=0)
            delta = _rowsum(prob * dp, ones)
            dsc = prob * (dp - _both(delta))
            ds_scr[...] += psink * delta
            dbias_scr[...] += dsc
            dsb = (dsc * (HEAD_DIM ** -0.5)).astype(BF16)
            pb = prob.astype(BF16)
            dkt = [jnp.zeros((HEAD_DIM, 2 * CHUNK), F32) for _ in range(2)]
            dvt = [jnp.zeros((HEAD_DIM, 2 * CHUNK), F32) for _ in range(2)]
            for pr in range(N_HEADS // 2):
                ps = slice(pr * LANES, (pr + 1) * LANES)
                qpt = transposed(qkv_ref[pl.ds(r0, CHUNK), ps])
                dopt = transposed(do_ref[pl.ds(r0, CHUNK), ps])
                kvh = pr // 2
                dq = jnp.zeros((CHUNK, LANES), F32)
                for hh in range(2):
                    hr = _head_rows(2 * pr + hh)
                    rows = slice(hh * HEAD_DIM, (hh + 1) * HEAD_DIM)
                    dq = dq + _dot(dsb[hr], kv[kvh][hh])
                    dkt[kvh] = dkt[kvh] + _dot(qpt, dsb[hr])[rows]
                    dvt[kvh] = dvt[kvh] + _dot(dopt, pb[hr])[rows]
                d_ref[pl.ds(r0, CHUNK), ps] = dq.astype(BF16)
            dk_scr[:, pl.ds(r0, 2 * CHUNK)] += jnp.concatenate(dkt, axis=0)
            dv_scr[:, pl.ds(r0, 2 * CHUNK)] += jnp.concatenate(dvt, axis=0)
            return carry

        lax.fori_loop(0, nb, blk, 0)
        for n in range(nb):
            rows = slice(n * CHUNK, (n + 1) * CHUNK)
            cols = slice((n + 1) * CHUNK, (n + 2) * CHUNK)
            d_ref[rows, Q_DIM:Q_DIM + KV_DIM] = dk_scr[:, cols].T.astype(BF16)
            d_ref[rows, Q_DIM + KV_DIM:] = dv_scr[:, cols].T.astype(BF16)

        @pl.when(b == n_seq - 1)
        def _():
            bkv = bk_ref[...]
            for h in range(N_HEADS):
                gs_ref[0:1, h:h + 1] = -jnp.sum(ds_scr[_head_rows(h), 0:1], axis=0, keepdims=True)
                db = dbias_scr[_head_rows(h), :]
                for bb in range(N_BUCKETS):
                    part = jnp.sum(jnp.where(bkv == bb, db, 0.0), axis=-1, keepdims=True)
                    gr_ref[bb:bb + 1, h:h + 1] = jnp.sum(part, axis=0, keepdims=True)

    smem = pl.BlockSpec(memory_space=pltpu.SMEM)
    return pl.pallas_call(
        body, name="attn_bwd", grid=(n_seq,),
        in_specs=[_row(seq, B_DIM), _row(seq, Q_DIM), _full(bk.shape), smem, smem] + [ANY] * len(order),
        out_specs=[_row(seq, B_DIM), _full((1, N_HEADS)), _full((N_BUCKETS, N_HEADS))],
        out_shape=[_sds((n_seq * seq, B_DIM), BF16), _sds((1, N_HEADS), F32), _sds((N_BUCKETS, N_HEADS), F32)],
        scratch_shapes=[pltpu.VMEM((HEAD_ROWS, 2 * CHUNK), F32), pltpu.VMEM((HEAD_ROWS, LANES), F32),
                        pltpu.VMEM((8, seq, KV_DIM), BF16), pltpu.VMEM((HEAD_ROWS, 2 * CHUNK), F32),
                        pltpu.VMEM((KV_DIM, seq + CHUNK), F32), pltpu.VMEM((KV_DIM, seq + CHUNK), F32),
                        pltpu.VMEM((HEAD_ROWS, LANES), F32)],
        compiler_params=_cp(("arbitrary",), 40),
    )(*_hbm(proj_b, d_yb, bk), rel_bias, sinks, *order)


def _inproj_bwd(d_g, d_a, d_b, x2, dx1, g_mix, w_in, tm, after=None):
    T = x2.shape[0]
    sub = min(tm, 128)
    order = [] if after is None else [after]

    def body(*refs):
        dg_ref, da_ref, db_ref, x_ref, dx1_ref, g_ref, w_ref = refs[:7]
        gx_ref, gg_ref = refs[7 + len(order):]
        subs = [slice(s0, s0 + sub) for s0 in range(0, tm, sub)]
        dhs = [_dot(dg_ref[rs, :], w_ref[_G_COLS, :]) + _dot(da_ref[rs, :], w_ref[_A_COLS, :])
               + _dot(db_ref[rs, :], w_ref[_B_COLS, :]) for rs in subs]
        gg = jnp.zeros((1, D_MODEL), F32)
        for rs, dh in zip(subs, dhs):
            x = x_ref[rs, :]
            r = _rms_r(x)
            n = x * r
            gx_ref[rs, :] = dx1_ref[rs, :] + _rms_bwd(dh, n, r, g_ref[...])
            gg = gg + jnp.sum(dh * n, axis=0, keepdims=True)

        @pl.when(pl.program_id(0) == 0)
        def _():
            gg_ref[...] = jnp.zeros_like(gg_ref)

        gg_ref[...] += gg

    return pl.pallas_call(
        body, name="inproj_bwd", grid=(T // tm,),
        in_specs=[_row(tm, G_DIM), _row(tm, A_DIM), _row(tm, B_DIM), _row(tm, D_MODEL), _row(tm, D_MODEL),
                  _full(g_mix.shape), _resident(w_in.shape)] + [ANY] * len(order),
        out_specs=[_row(tm, D_MODEL), _full((1, D_MODEL))],
        out_shape=[_sds((T, D_MODEL), F32), _sds((1, D_MODEL), F32)],
        compiler_params=_cp(("arbitrary",), 48),
    )(*_hbm(d_g, d_a, d_b, x2, dx1, g_mix, w_in), *order)


IN_SHARD = (A_DIM + B_DIM + G_DIM) // N_CHIPS


def _grad_w_in(h, d_a, d_b, d_g, tk):
    T = h.shape[0]
    nk = T // tk
    in_dim = N_CHIPS * IN_SHARD

    def body(h_ref, da_ref, db_ref, dg_ref, o_ref, acc_ref):
        k = pl.program_id(0)

        @pl.when(k == 0)
        def _():
            acc_ref[...] = jnp.zeros_like(acc_ref)

        hb = h_ref[...]
        acc_ref[:, _A_COLS] += _dot_tn(hb, da_ref[...])
        acc_ref[:, _B_COLS] += _dot_tn(hb, db_ref[...])
        acc_ref[:, _G_COLS] += _dot_tn(hb, dg_ref[...])

        @pl.when(k == nk - 1)
        def _():
            for i in range(N_CHIPS):
                o_ref[i] = acc_ref[:, i * IN_SHARD:(i + 1) * IN_SHARD].astype(BF16)

    return pl.pallas_call(
        body, name="grad_w_in", grid=(nk,),
        in_specs=[_row(tk, D_MODEL), _row(tk, A_DIM), _row(tk, B_DIM), _row(tk, G_DIM)],
        out_specs=_full((N_CHIPS, D_MODEL, IN_SHARD)), out_shape=_sds((N_CHIPS, D_MODEL, IN_SHARD), BF16),
        scratch_shapes=[pltpu.VMEM((D_MODEL, in_dim), F32)],
        compiler_params=_cp(("arbitrary",), 56),
    )(*_hbm(h, d_a, d_b, d_g))


def _local_step(x, target, g_mix, g_sgu, w_s, b_s, sinks, rel_bias, g_ffn, b_conv, g_final,
                w_in, w_conv, proj_weights, ffn_weights, on_grads, after=None):
    n_seq, seq, _ = x.shape
    T = n_seq * seq
    tm = min(ROW_TILE, seq)
    tw = min(GRAD_ROW_TILE, T)
    tf = min(WIDE_ROW_TILE, seq)
    x2 = x.reshape(T, D_MODEL)
    tgt = target.reshape(T, D_MODEL)
    b_st = b_s.T
    g_fin = g_final.reshape(1, D_MODEL)

    proj_g, proj_a, proj_b, h, y_a = _inproj(x2, g_mix, w_in, g_sgu, w_s, b_st, tm, after)
    y_b = _attn_fwd(proj_b, sinks, rel_bias, n_seq, seq)
    w_pa, w_pb, w_out = proj_weights(y_b)
    x1, merged = _merge_fwd(x2, y_a, y_b, proj_g, w_pa, w_pb, w_out, tm)
    w_up, w_down = ffn_weights(x1)
    upre, h2, gate, val = _upproj(x1, g_ffn, w_up, w_conv, b_conv, tf, seq)
    dx2, loss, gg_final = _ffn_down_loss(gate, val, x1, tgt, w_down, g_fin, tm)

    d_gate, d_val, gw_down, gb_g, gb_v = _ffn_bwd_act(gate, val, dx2, w_down, tw)
    gb_conv = jnp.concatenate([gb_g, gb_v], axis=1)
    d_upre, dx1, gg_ffn, gw_conv = _ffn_bwd_up(d_gate, d_val, upre, dx2, x1, g_ffn, w_conv, w_up, tf, seq)
    gw_up = _matmul_tn(h2, d_upre, 2 * D_FF // 4, min(4 * GRAD_ROW_TILE, T), "grad_w_up")
    sent = on_grads("ffn", dict(w_up=gw_up, w_down=gw_down))
    d_g, d_a, d_yb, gw_out, gw_pa, gw_pb, gw_s, gb_st, gg_sgu = _merge_bwd(
        dx1, merged, y_a, y_b, proj_g, proj_a, w_pa, w_pb, w_out, g_sgu, w_s, b_st, tw, sent)
    sent = on_grads("proj", dict(w_pa=gw_pa, w_pb=gw_pb, w_out=gw_out))
    d_b, g_sinks, g_rel = _attn_bwd(proj_b, d_yb, sinks, rel_bias, n_seq, seq, sent)
    gw_in = _grad_w_in(h, d_a, d_b, d_g, min(2 * GRAD_ROW_TILE, T))
    sent = on_grads("in", dict(w_in=gw_in))
    grad_x, gg_mix = _inproj_bwd(d_g, d_a, d_b, x2, dx1, g_mix, w_in, tm, sent)

    small = dict(g_mix=gg_mix, g_sgu=gg_sgu, w_s=gw_s, b_s=gb_st.T, sinks=g_sinks, rel_bias=g_rel,
                 g_ffn=gg_ffn, b_conv=gb_conv, g_final=gg_final, w_conv=gw_conv)
    big = dict(w_in=gw_in, w_pa=gw_pa, w_pb=gw_pb, w_out=gw_out, w_up=gw_up, w_down=gw_down)
    return loss, grad_x.reshape(x.shape), small, big


_MIXER = ("w_in", "w_pa", "w_pb", "w_out")
_FFN = ("w_up", "w_down")
_BIG = _MIXER + _FFN

CONV_ROWS = 6
_SMALL_AT = dict(loss=(0, 1, 1), g_sgu=(4, 1, A_WIDTH), sinks=(5, 1, N_HEADS), b_s=(8, A_GROUPS, CHUNK),
                 b_conv=(12, CONV_ROWS, D_MODEL), w_conv=(18, 3 * CONV_ROWS, D_MODEL),
                 g_final=(36, 1, D_MODEL), g_mix=(37, 1, D_MODEL), g_ffn=(38, 1, D_MODEL),
                 w_s=(40, A_GROUPS * CHUNK * CHUNK // D_MODEL, D_MODEL))
_REL_BIAS_AT = (0, A_WIDTH)
_SMALL_IN_CALL = ("g_final", "g_mix", "g_ffn", "g_sgu", "sinks", "b_s", "b_conv", "rel_bias", "w_s")
SMALL_ROWS = 104


def _pack_small(vals):
    def wide(a):
        return jnp.pad(a, ((0, 0), (0, CONV_ROWS * D_MODEL - a.shape[1]))).reshape(-1, D_MODEL)

    nr = _SMALL_AT["w_s"][1]
    w_s = vals["w_s"].reshape(D_MODEL // CHUNK, nr, CHUNK).transpose(1, 0, 2).reshape(nr, D_MODEL)
    laid = dict(vals, b_conv=wide(vals["b_conv"]), w_conv=wide(vals["w_conv"]), w_s=w_s)
    rows, at = [], 0
    for n, (r0, nr, nc) in _SMALL_AT.items():
        if r0 > at:
            rows.append(jnp.zeros((r0 - at, D_MODEL), F32))
        rows.append(jnp.pad(laid[n].astype(F32).reshape(nr, nc), ((0, 0), (0, D_MODEL - nc))))
        at = r0 + nr
    return lax.dynamic_update_slice(jnp.concatenate(rows, axis=0), vals["rel_bias"].T, _REL_BIAS_AT)


def _unwide(a, r):
    return a.reshape(r, CONV_ROWS * D_MODEL)[:, :2 * D_FF]


def _mesh_pos():
    return lax.axis_index("x"), lax.axis_index("y"), lax.axis_index("c")


def _other_chips(x, y):
    return [(1 - x, y), (x, 1 - y), (1 - x, 1 - y)]


def _remote(src, dst, send_sem, recv_sem, to):
    return pltpu.make_async_remote_copy(src_ref=src, dst_ref=dst, send_sem=send_sem, recv_sem=recv_sem,
                                        device_id=to, device_id_type=MESH)


def _own_slot(own, n, at):
    return lax.dynamic_update_slice(lax.empty((n,) + own.shape, own.dtype), own[None], (at,) + (0,) * own.ndim)


def _allgather_weights(stacks, wc_stack):
    names = list(stacks)
    n = len(names)

    def body(*refs):
        ins, outs = refs[:n + 1], refs[n + 1:2 * n + 2]
        send_sems, recv_sems = refs[2 * n + 2:]
        x, y, c = _mesh_pos()
        _handshake(_chip_peers(x, y, c) + _sibling_peers(x, y, c))
        me = 2 * x + y
        sibling = (x, y, 1 - c)
        chips = _other_chips(x, y)

        def half(ref, chip, hc):
            hr = ref.shape[1] // 2
            return ref.at[chip, pl.ds(hc * hr, hr), :]

        first = []
        for k in range(n):
            first += [_remote(half(ins[k], me, c), half(outs[k], me, c), send_sems.at[6 * k + j], recv_sems.at[6 * k + j], (cx, cy, c))
                      for j, (cx, cy) in enumerate(chips)]
        first += [_remote(ins[n].at[me], outs[n].at[me], send_sems.at[6 * n + j], recv_sems.at[6 * n + j], (cx, cy, c))
                  for j, (cx, cy) in enumerate(chips)]
        for cp in first:
            cp.start()
        passed = []
        for k in range(n):
            for j, (cx, cy) in enumerate(chips):
                landed = half(outs[k], 2 * cx + cy, c)
                _remote(landed, landed, send_sems.at[6 * k + j], recv_sems.at[6 * k + j], (x, y, c)).wait_recv()
                passed.append(_remote(landed, landed, send_sems.at[6 * k + 3 + j], recv_sems.at[6 * k + 3 + j], sibling))
                passed[-1].start()
        for k in range(n):
            for j, (cx, cy) in enumerate(chips):
                theirs = half(outs[k], 2 * cx + cy, 1 - c)
                _remote(theirs, theirs, send_sems.at[6 * k + 3 + j], recv_sems.at[6 * k + 3 + j], (x, y, c)).wait_recv()
        for j, (cx, cy) in enumerate(chips):
            slot = outs[n].at[2 * cx + cy]
            _remote(slot, slot, send_sems.at[6 * n + j], recv_sems.at[6 * n + j], (x, y, c)).wait_recv()
        for cp in first + passed:
            cp.wait_send()

    arrays = [stacks[k] for k in names] + [wc_stack]
    outs = pl.pallas_call(
        body, name="allgather_weights",
        in_specs=[HBM] * (n + 1), out_specs=[HBM] * (n + 1), input_output_aliases={k: k for k in range(n + 1)},
        out_shape=[_sds(a.shape, a.dtype) for a in arrays],
        scratch_shapes=[pltpu.SemaphoreType.DMA((6 * n + 3,)), pltpu.SemaphoreType.DMA((6 * n + 3,))],
        compiler_params=pltpu.CompilerParams(collective_id=_COLLECTIVE["gather_in"]),
    )(*arrays)
    return dict(zip(names, outs[:n])), outs[n]


_KIND = {"w_in": "stack", "w_pa": "col", "w_pb": "col", "w_up": "col", "w_out": "row", "w_down": "row"}


def _half_view(ref, kind, h):
    if kind == "stack":
        k = ref.shape[1] // 2
        return ref.at[:, pl.ds(h * k, k), :]
    if kind == "col":
        k = ref.shape[0] // 2
        return ref.at[pl.ds(h * k, k), :]
    k = ref.shape[1] // 2
    return ref.at[:, pl.ds(h * k, k)]


def _shard_view(ref, kind, i):
    if kind == "stack":
        return ref.at[i]
    if kind == "col":
        k = ref.shape[1] // N_CHIPS
        return ref.at[:, pl.ds(i * k, k)]
    k = ref.shape[0] // N_CHIPS
    return ref.at[pl.ds(i * k, k), :]


def _region_view(ref, kind, h):
    if kind == "row":
        k = ref.shape[1] // 2
        return ref.at[:, pl.ds(h * k, k)]
    k = ref.shape[0] // 2
    return ref.at[pl.ds(h * k, k), :]


def _half_shape(shape, kind):
    if kind == "stack":
        return (shape[0], shape[1] // 2, shape[2])
    return (shape[0] // 2, shape[1]) if kind == "col" else (shape[0], shape[1] // 2)


def _part_shape(half_shape, kind):
    if kind == "stack":
        return tuple(half_shape[1:])
    k, w = half_shape
    return (k, w // N_CHIPS) if kind == "col" else (k // N_CHIPS, w)


_DATAFLOW = pltpu.SideEffectType.DATAFLOW_SIDE_EFFECTING
_TOKEN = (SUBLANES, LANES)


_COLLECTIVE = {k: i for i, k in enumerate(
    [kind + "_" + g for kind in ("pair", "chip", "share") for g in ("ffn", "proj", "in")]
    + ["gather_proj", "gather_ffn", "gather_in", "forward_proj", "forward_ffn"])}


def _sibling_peers(x, y, c):
    return [(x, y, 1 - c)]


def _chip_peers(x, y, c):
    return [(cx, cy, c) for cx, cy in _other_chips(x, y)]


def _handshake(peers):
    barrier = pltpu.get_barrier_semaphore()
    for peer in peers:
        pl.semaphore_signal(barrier, inc=1, device_id=peer, device_id_type=MESH)
    pl.semaphore_wait(barrier, len(peers))


def _split_start(name, arrays, n_sems, issue, after=None, handshake=None):
    n = len(arrays)
    order = [] if after is None else [after]

    def body(*refs):
        base = n + len(order)
        if handshake is not None:
            _handshake(handshake[1](*_mesh_pos()))
        issue(refs[:n], refs[base], refs[base + 1])
        refs[-1][...] = jnp.zeros(_TOKEN, F32)

    params = dict(has_side_effects=_DATAFLOW)
    if handshake is not None:
        params["collective_id"] = handshake[0]
    outs = pl.pallas_call(
        body, name=name,
        in_specs=[HBM] * n + [ANY] * len(order), out_specs=[SEM, SEM] + [HBM] * n + [pl.BlockSpec(memory_space=pltpu.VMEM)],
        out_shape=[pltpu.SemaphoreType.DMA((n_sems,)), pltpu.SemaphoreType.DMA((n_sems,))]
        + [pltpu.HBM(a.shape, a.dtype) for a in arrays] + [_sds(_TOKEN, F32)],
        input_output_aliases={k: 2 + k for k in range(n)},
        compiler_params=pltpu.CompilerParams(**params),
    )(*[pltpu.with_memory_space_constraint(a, pltpu.HBM) for a in arrays], *order)
    return outs[0], outs[1], list(outs[2:2 + n]), outs[-1]


def _split_wait(name, started, waits, after):
    send_sems, recv_sems, arrays, _ = started
    n = len(arrays)

    def body(*refs):
        waits(refs[:n], refs[n], refs[n + 1])

    return pl.pallas_call(
        body, name=name,
        in_specs=[HBM] * n + [SEM, SEM, ANY], out_specs=[HBM] * n,
        out_shape=[pltpu.HBM(a.shape, a.dtype) for a in arrays],
        input_output_aliases={k: k for k in range(n)},
        compiler_params=pltpu.CompilerParams(has_side_effects=_DATAFLOW),
    )(*arrays, send_sems, recv_sems, after)


def _wait_both(src, dst, send_sem, recv_sem):
    x, y, c = _mesh_pos()
    cp = _remote(src, dst, send_sem, recv_sem, (x, y, c))
    cp.wait_send()
    cp.wait_recv()


def _pair_exchange_start(parts, tag, after):
    names = list(parts)
    n = len(names)
    lands = [lax.empty(_half_shape(parts[k].shape, _KIND[k]), parts[k].dtype) for k in names]

    def issue(refs, send_sems, recv_sems):
        x, y, c = _mesh_pos()
        for hc in range(2):
            @pl.when(c == hc)
            def _():
                for k in range(n):
                    _remote(_half_view(refs[k], _KIND[names[k]], 1 - hc), refs[n + k], send_sems.at[k], recv_sems.at[k],
                            (x, y, 1 - c)).start()

    return names, _split_start("grad_pair_exchange_start_" + tag, [parts[k] for k in names] + lands, n, issue, after,
                               (_COLLECTIVE["pair_" + tag], _sibling_peers))


def _pair_exchange_wait(pending, tag, after):
    names, started = pending
    n = len(names)

    def waits(refs, send_sems, recv_sems):
        for k in range(n):
            _wait_both(_half_view(refs[k], _KIND[names[k]], 0), refs[n + k], send_sems.at[k], recv_sems.at[k])

    outs = _split_wait("grad_pair_exchange_wait_" + tag, started, waits, after)
    return dict(zip(names, outs[:n])), dict(zip(names, outs[n:]))


def _half_blocks(shape, kind):
    if kind == "stack":
        _, k, w = shape
        return (N_CHIPS // 2, 1), (2, k // 2, w), (lambda i, r, s: (i, r, 0)), (lambda i, r, s: (i, s[1] + r, 0))
    k, w = shape
    if kind == "col":
        tr = STREAM_ROWS
        nb = k // 2 // tr
        return (nb,), (tr, w), (lambda r, s: (r, 0)), (lambda r, s: (s[1] * nb + r, 0))
    nb = 2
    return (nb,), (k // nb, w // 2), (lambda r, s: (r, 0)), (lambda r, s: (r, s[1]))


def _pair_add(part, from_sibling, name, pos):
    kind = _KIND[name]
    grid, block, half_map, full_map = _half_blocks(part.shape, kind)

    def body(s_ref, p_ref, q_ref, o_ref):
        o_ref[...] = (p_ref[...].astype(F32) + q_ref[...].astype(F32)).astype(BF16)

    return pl.pallas_call(
        body, name="grad_pair_add_" + name,
        grid_spec=pltpu.PrefetchScalarGridSpec(
            num_scalar_prefetch=1, grid=grid,
            in_specs=[pl.BlockSpec(block, full_map), pl.BlockSpec(block, half_map)],
            out_specs=pl.BlockSpec(block, half_map)),
        out_shape=_sds(from_sibling.shape, BF16),
        compiler_params=_cp(("arbitrary",) * len(grid), 40),
    )(pos, *_hbm(part, from_sibling))


def _chip_exchange_start(sums, tag, after):
    names = list(sums)
    n = len(names)
    lands = [lax.empty((3,) + _part_shape(sums[k].shape, _KIND[k]), sums[k].dtype) for k in names]

    def issue(refs, send_sems, recv_sems):
        x, y, c = _mesh_pos()
        me = 2 * x + y
        for i in range(N_CHIPS):
            xi, yi = i // 2, i % 2
            j = jnp.where(xi != x, jnp.where(yi != y, 2, 0), 1)

            @pl.when(i != me)
            def _():
                for k in range(n):
                    _remote(_shard_view(refs[k], _KIND[names[k]], i), refs[n + k].at[j], send_sems.at[3 * k + j],
                            recv_sems.at[3 * k + j], (xi, yi, c)).start()

    return names, _split_start("grad_chip_exchange_start_" + tag, [sums[k] for k in names] + lands, 3 * n, issue, after,
                               (_COLLECTIVE["chip_" + tag], _chip_peers))


def _chip_exchange_wait(pending, tag, after):
    names, started = pending
    n = len(names)

    def waits(refs, send_sems, recv_sems):
        for k in range(n):
            for j in range(3):
                _wait_both(_shard_view(refs[k], _KIND[names[k]], 0), refs[n + k].at[j], send_sems.at[3 * k + j], recv_sems.at[3 * k + j])

    return dict(zip(names, _split_wait("grad_chip_exchange_wait_" + tag, started, waits, after)[n:]))


def _allgather_start(stacks, tag, after):
    names = list(stacks)

    def issue(refs, send_sems, recv_sems):
        x, y, c = _mesh_pos()
        me = 2 * x + y
        for k, st in enumerate(refs):
            hr = st.shape[1] // 2
            mine = st.at[me, pl.ds(c * hr, hr), :]
            for j, (cx, cy) in enumerate(_other_chips(x, y)):
                _remote(mine, mine, send_sems.at[3 * k + j], recv_sems.at[3 * k + j], (cx, cy, c)).start()

    return names, _split_start("allgather_start_" + tag, [stacks[k] for k in names], 3 * len(names), issue, after,
                               (_COLLECTIVE["gather_" + tag], _chip_peers))


def _allgather_wait(pending, tag, after):
    names, started = pending

    def waits(refs, send_sems, recv_sems):
        for k, st in enumerate(refs):
            slot = st.at[0, pl.ds(0, st.shape[1] // 2), :]
            for j in range(3):
                _wait_both(slot, slot, send_sems.at[3 * k + j], recv_sems.at[3 * k + j])

    return dict(zip(names, _split_wait("allgather_wait_" + tag, started, waits, after)))


def _allgather_forward(stacks, tag):
    names = list(stacks)
    n = len(names)

    def body(*refs):
        ins, outs = refs[:n], refs[n:2 * n]
        send_sems, recv_sems = refs[2 * n:]
        x, y, c = _mesh_pos()
        _handshake(_sibling_peers(x, y, c))
        copies = []
        for k in range(n):
            hr = ins[k].shape[1] // 2
            for j, (cx, cy) in enumerate(_other_chips(x, y)):
                chip = 2 * cx + cy
                copies.append(_remote(ins[k].at[chip, pl.ds(c * hr, hr), :], outs[k].at[chip, pl.ds(c * hr, hr), :],
                                      send_sems.at[3 * k + j], recv_sems.at[3 * k + j], (x, y, 1 - c)))
        for cp in copies:
            cp.start()
        for cp in copies:
            cp.wait()

    arrays = [stacks[k] for k in names]
    outs = pl.pallas_call(
        body, name="allgather_forward_" + tag, in_specs=[HBM] * n, out_specs=[HBM] * n,
        input_output_aliases={k: k for k in range(n)},
        out_shape=[_sds(a.shape, a.dtype) for a in arrays],
        scratch_shapes=[pltpu.SemaphoreType.DMA((3 * n,)), pltpu.SemaphoreType.DMA((3 * n,))],
        compiler_params=pltpu.CompilerParams(collective_id=_COLLECTIVE["forward_" + tag]),
    )(*arrays)
    return dict(zip(names, outs))


def _owner_sum(part, from_sibling, from_chips, name, pos, shard_shape):
    kind = _KIND[name]
    _, pk, pw = from_chips.shape
    if kind == "row":
        nb = 1
        tr = pk // nb
        p_spec = pl.BlockSpec((tr, pw), lambda r, s: (s[0] * nb + r, s[1]))
        q_spec = pl.BlockSpec((tr, pw), lambda r, s: (s[0] * nb + r, 0))
        o_spec = pl.BlockSpec((tr, pw), lambda r, s: (r, s[1]))
    else:
        tr = STREAM_ROWS
        nb = pk // tr
        if kind == "stack":
            p_spec = pl.BlockSpec((None, tr, pw), lambda r, s: (s[0], s[1] * nb + r, 0))
            q_spec = pl.BlockSpec((None, tr, pw), lambda r, s: (s[0], r, 0))
        else:
            p_spec = pl.BlockSpec((tr, pw), lambda r, s: (s[1] * nb + r, s[0]))
            q_spec = pl.BlockSpec((tr, pw), lambda r, s: (r, s[0]))
        o_spec = pl.BlockSpec((tr, pw), lambda r, s: (s[1] * nb + r, 0))

    def body(s_ref, p_ref, q_ref, r_ref, o_ref):
        acc = p_ref[...].astype(F32) + q_ref[...].astype(F32)
        for j in range(3):
            acc = acc + r_ref[j].astype(F32)
        o_ref[...] = acc

    return pl.pallas_call(
        body, name="grad_owner_sum_" + name,
        grid_spec=pltpu.PrefetchScalarGridSpec(
            num_scalar_prefetch=1, grid=(nb,),
            in_specs=[p_spec, q_spec, pl.BlockSpec((3, tr, pw), lambda r, s: (0, r, 0))],
            out_specs=o_spec),
        out_shape=_sds(shard_shape, F32),
        compiler_params=_cp(("arbitrary",), 32),
    )(pos, *_hbm(part, from_sibling, from_chips))


def _pair_share_start(shards, tag, after):
    names = list(shards)

    def issue(refs, send_sems, recv_sems):
        x, y, c = _mesh_pos()
        for hc in range(2):
            @pl.when(c == hc)
            def _():
                for k, g in enumerate(refs):
                    mine = _region_view(g, _KIND[names[k]], hc)
                    _remote(mine, mine, send_sems.at[k], recv_sems.at[k], (x, y, 1 - c)).start()

    return names, _split_start("grad_pair_share_start_" + tag, [shards[k] for k in names], len(names), issue, after,
                               (_COLLECTIVE["share_" + tag], _sibling_peers))


def _pair_share_wait(pending, tag, after):
    names, started = pending

    def waits(refs, send_sems, recv_sems):
        for k, g in enumerate(refs):
            region = _region_view(g, _KIND[names[k]], 0)
            _wait_both(region, region, send_sems.at[k], recv_sems.at[k])

    return dict(zip(names, _split_wait("grad_pair_share_wait_" + tag, started, waits, after)))


def _small_exchange_start(slots, after):
    def issue(refs, send_sems, recv_sems):
        x, y, c = _mesh_pos()
        mine = refs[0].at[4 * x + 2 * y + c]
        k = 0
        for px in range(2):
            for py in range(2):
                for pc in range(2):
                    if px + py + pc:
                        peer = (1 - x if px else x, 1 - y if py else y, 1 - c if pc else c)
                        _remote(mine, mine, send_sems.at[k], recv_sems.at[k], peer).start()
                        k += 1

    return _split_start("small_exchange_start", [slots], N_DEV - 1, issue, after)


def _small_exchange_wait(started, after):
    def waits(refs, send_sems, recv_sems):
        slot = refs[0].at[0]
        for k in range(N_DEV - 1):
            _wait_both(slot, slot, send_sems.at[k], recv_sems.at[k])

    return _split_wait("small_exchange_wait", started, waits, after)[0]


def _adam_math(w, g, m, v):
    m = ADAM_B1 * m + (1.0 - ADAM_B1) * g
    v = ADAM_B2 * v + (1.0 - ADAM_B2) * (g * g)
    m_hat = m / (1.0 - ADAM_B1 ** ADAM_STEP)
    v_hat = v / (1.0 - ADAM_B2 ** ADAM_STEP)
    delta = -ADAM_LR * (m_hat / (jnp.sqrt(v_hat) + ADAM_EPS) + ADAM_WD * w)
    return delta, m, v


def _adamw(w, g, m, v, name):
    rows, cols = w.shape[0], w.shape[-1]
    fits = [t for t in range(SUBLANES, rows, SUBLANES) if rows % t == 0 and t * cols * 4 <= (3 << 19)]
    tr = max(fits) if fits and w.ndim == 2 else rows

    def body(w_ref, g_ref, m_ref, v_ref, d_ref, nm_ref, nv_ref, go_ref):
        g = g_ref[...]
        d, nm, nv = _adam_math(w_ref[...], g, m_ref[...], v_ref[...])
        d_ref[...] = d
        nm_ref[...] = nm
        nv_ref[...] = nv
        go_ref[...] = g

    spec = pl.BlockSpec((tr,) + w.shape[1:], lambda i: (i,) + (0,) * (w.ndim - 1))
    return pl.pallas_call(
        body, name=name, grid=(rows // tr,), in_specs=[spec] * 4, out_specs=[spec] * 4,
        out_shape=[_sds(w.shape, F32)] * 4, compiler_params=_cp(("arbitrary",)),
    )(*_hbm(w, g, m, v))


def _small_sum_adamw(gathered, w, m, v):
    names = _SMALL_IN_CALL
    n = len(names)

    def body(*refs):
        a_ref = refs[0]
        w_refs, m_refs, v_refs = refs[1:1 + n], refs[1 + n:1 + 2 * n], refs[1 + 2 * n:1 + 3 * n]
        sum_ref, loss_ref = refs[1 + 3 * n], refs[2 + 3 * n]
        outs = refs[3 + 3 * n:]
        g = a_ref[0]
        for k in range(1, N_DEV):
            g = g + a_ref[k]
        sum_ref[...] = g
        loss_ref[...] = g[0:1, 0:1]
        for i, name in enumerate(names):
            if name == "rel_bias":
                r0, c0 = _REL_BIAS_AT
                pieces = [(slice(None), g[r0:r0 + N_HEADS, c0:c0 + N_BUCKETS])]
            elif name == "b_conv":
                r0 = _SMALL_AT[name][0]
                pieces = [(slice(None), jnp.concatenate([g[r0 + k:r0 + k + 1, :] for k in range(CONV_ROWS)], axis=1)[:, :2 * D_FF])]
            elif name == "w_s":
                r0, nr, _ = _SMALL_AT[name]
                pieces = [(slice(nr * j, nr * (j + 1)), g[r0:r0 + nr, CHUNK * j:CHUNK * (j + 1)]) for j in range(D_MODEL // CHUNK)]
            else:
                r0, nr, nc = _SMALL_AT[name]
                pieces = [(slice(None), g[r0:r0 + nr, 0:nc])]
            for at, gp in pieces:
                d, nm, nv = _adam_math(w_refs[i][at], gp, m_refs[i][at], v_refs[i][at])
                for k, val in enumerate((gp, d, nm, nv)):
                    outs[4 * i + k][at] = val

    shapes = [w[k].shape for k in names]
    res = pl.pallas_call(
        body, name="small_sum_adamw",
        out_shape=[_sds((SMALL_ROWS, D_MODEL), F32), _sds((1, 1), F32)] + [_sds(s, F32) for s in shapes for _ in range(4)],
    )(gathered, *[w[k] for k in names], *[m[k] for k in names], *[v[k] for k in names])
    return res[0], res[1], {k: tuple(res[2 + 4 * i:6 + 4 * i]) for i, k in enumerate(names)}


_NAMES = ("g_mix", "w_in", "g_sgu", "w_s", "b_s", "sinks", "rel_bias", "w_pa", "w_pb", "w_out",
          "g_ffn", "w_up", "w_conv", "b_conv", "w_down", "g_final")

def kernel(x, g_mix, w_in, g_sgu, w_s, b_s, sinks, rel_bias, w_pa, w_pb, w_out, g_ffn, w_up, w_conv, b_conv, w_down, g_final, loss_target, m_g_mix, m_w_in, m_g_sgu, m_w_s, m_b_s, m_sinks, m_rel_bias, m_w_pa, m_w_pb, m_w_out, m_g_ffn, m_w_up, m_w_conv, m_b_conv, m_w_down, m_g_final, v_g_mix, v_w_in, v_g_sgu, v_w_s, v_b_s, v_sinks, v_rel_bias, v_w_pa, v_w_pb, v_w_out, v_g_ffn, v_w_up, v_w_conv, v_b_conv, v_w_down, v_g_final):
    w = dict(g_mix=g_mix, w_in=w_in, g_sgu=g_sgu, w_s=w_s, b_s=b_s, sinks=sinks, rel_bias=rel_bias, w_pa=w_pa, w_pb=w_pb,
             w_out=w_out, g_ffn=g_ffn, w_up=w_up, w_conv=w_conv, b_conv=b_conv, w_down=w_down, g_final=g_final)
    m = dict(g_mix=m_g_mix, w_in=m_w_in, g_sgu=m_g_sgu, w_s=m_w_s, b_s=m_b_s, sinks=m_sinks, rel_bias=m_rel_bias, w_pa=m_w_pa,
             w_pb=m_w_pb, w_out=m_w_out, g_ffn=m_g_ffn, w_up=m_w_up, w_conv=m_w_conv, b_conv=m_b_conv, w_down=m_w_down,
             g_final=m_g_final)
    v = dict(g_mix=v_g_mix, w_in=v_w_in, g_sgu=v_g_sgu, w_s=v_w_s, b_s=v_b_s, sinks=v_sinks, rel_bias=v_rel_bias, w_pa=v_w_pa,
             w_pb=v_w_pb, w_out=v_w_out, g_ffn=v_g_ffn, w_up=v_w_up, w_conv=v_w_conv, b_conv=v_b_conv, w_down=v_w_down,
             g_final=v_g_final)
    xi, yi, ci = _mesh_pos()
    me = 2 * xi + yi

    shard = {n: w[n][0] for n in _BIG}
    shard_shapes = {n: shard[n].shape for n in _BIG}
    wc_shard = w["w_conv"][0]
    wc_pad = jnp.pad(wc_shard, ((0, 5), (0, 0)))
    own = {n: _own_slot(shard[n].astype(BF16), N_CHIPS, me) for n in _BIG if n != "w_in"}
    own["w_in"] = _own_slot(shard["w_in"].T.astype(BF16), N_CHIPS, me)
    stacks, wc_all = _allgather_weights({"w_in": own["w_in"]}, _own_slot(wc_pad, N_CHIPS, me))
    proj_gather = _allgather_start({n: own[n] for n in _MIXER[1:]}, "proj", stacks["w_in"])
    ffn_gather = _allgather_start({n: own[n] for n in _FFN}, "ffn", proj_gather[1][-1])
    w_conv_full = jnp.concatenate([wc_all[i, :3] for i in range(N_CHIPS)], axis=1)
    w_in_full = stacks["w_in"].reshape(N_CHIPS * IN_SHARD, D_MODEL)
    pos = jnp.stack([me, ci])

    def proj_weights(done):
        st = _allgather_forward(_allgather_wait(proj_gather, "proj", done), "proj")
        return st["w_pa"], st["w_pb"], st["w_out"].reshape(D_MODEL, D_MODEL)

    def ffn_weights(done):
        st = _allgather_forward(_allgather_wait(ffn_gather, "ffn", done), "ffn")
        return st["w_up"], st["w_down"].reshape(D_FF, D_MODEL)

    groups = {}

    def stage1(group, parts):
        groups[group] = dict(parts=parts, pair=_pair_exchange_start(parts, group, None))
        return groups[group]["pair"][1][-1]

    def stage2(group, after, order_after):
        g = groups[group]
        g["parts"], g["sib"] = _pair_exchange_wait(g["pair"], group, after)
        g["chip"] = _chip_exchange_start({n: _pair_add(g["parts"][n], g["sib"][n], n, pos) for n in g["parts"]}, group, order_after)
        return g["chip"][1][-1]

    def stage3(group, after, order_after):
        g = groups[group]
        got = _chip_exchange_wait(g["chip"], group, after)
        g["share"] = _pair_share_start(
            {n: _owner_sum(g["parts"][n], g["sib"][n], got[n], n, pos, shard_shapes[n]) for n in g["parts"]}, group, order_after)
        return g["share"][1][-1]

    grads, deltas, new_m, new_v = {}, {}, {}, {}

    def stage4(group, after):
        g_shard = _pair_share_wait(groups[group]["share"], group, after)
        last = None
        for n in g_shard:
            g = _tie(g_shard[n], last)
            if n == "w_in":
                d, nm, nv, gt = _adamw(shard[n].T, g.T, m[n][0].T, v[n][0].T, "adamw_" + n)
                grads[n], deltas[n], new_m[n], new_v[n] = gt.T[None], d.T[None], nm.T[None], nv.T[None]
            else:
                d, nm, nv, go = _adamw(shard[n], g, m[n][0], v[n][0], "adamw_" + n)
                grads[n], deltas[n], new_m[n], new_v[n] = go[None], d[None], nm[None], nv[None]
            last = nv
        return last

    def on_grads(group, parts):
        token = stage1(group, parts)
        some = next(iter(parts.values()))
        if group == "proj":
            token = stage2("ffn", some, token)
        if group == "in":
            token = stage2("proj", some, token)
            token = stage3("ffn", some, token)
            token = stage2("in", token, token)
        return token

    loss, grad_x, small, big = _local_step(
        x, loss_target, w["g_mix"], w["g_sgu"], w["w_s"][0], w["b_s"][0], w["sinks"], w["rel_bias"], w["g_ffn"],
        w["b_conv"], w["g_final"], w_in_full, w_conv_full, proj_weights, ffn_weights, on_grads, ffn_gather[1][-1])

    small["loss"] = loss
    small_gather = _small_exchange_start(_own_slot(_pack_small(small), N_DEV, 2 * me + ci), grad_x)
    token = stage3("proj", grad_x, small_gather[-1])
    done = stage4("ffn", token)
    done = stage4("proj", done)
    token = stage3("in", done, None)
    all_small = _small_exchange_wait(small_gather, token)
    two_d = {n: (lambda a, n=n: a.reshape(_SMALL_AT[n][1:])) for n in _SMALL_IN_CALL}
    two_d["rel_bias"] = lambda a: a.T
    two_d["b_conv"] = lambda a: a
    two_d["w_s"] = lambda a: a.reshape(A_GROUPS * CHUNK, CHUNK)
    s_sum, s_loss, s_out = _small_sum_adamw(all_small, *[{n: two_d[n](p[n]) for n in _SMALL_IN_CALL} for p in (w, m, v)])
    stage4("in", s_sum)
    for n in _SMALL_IN_CALL:
        back = (lambda a: a.T) if n == "rel_bias" else (lambda a, n=n: a.reshape(w[n].shape))
        grads[n], deltas[n], new_m[n], new_v[n] = [back(a) for a in s_out[n]]

    def rows(n):
        r0, nr, _ = _SMALL_AT[n]
        return s_sum[r0:r0 + nr]

    wcols = wc_shard.shape[1]
    g_wc = lax.dynamic_slice(_unwide(rows("w_conv"), 3), (0, me * wcols), (3, wcols))
    taps = lambda a: a.transpose(1, 0, 2)
    res = _adamw(taps(w["w_conv"]), g_wc[:, None, :], taps(m["w_conv"]), taps(v["w_conv"]), "adamw_w_conv")
    deltas["w_conv"], new_m["w_conv"], new_v["w_conv"], grads["w_conv"] = [taps(a) for a in res]

    return (s_loss.reshape(()), grad_x, *[grads[n] for n in _NAMES], *[deltas[n] for n in _NAMES],
            *[new_m[n] for n in _NAMES], *[new_v[n] for n in _NAMES])
```

```python
import functools

import numpy as np
import jax
import jax.numpy as jnp
from jax import lax
from jax.experimental import pallas as pl
from jax.experimental.pallas import tpu as pltpu

F32 = jnp.float32
BF16 = jnp.bfloat16

D_MODEL = 1024
CHUNK = 128
A_GROUPS = 4
A_WIDTH = 512
N_HEADS = 8
HEAD_DIM = 64
Q_DIM = 512
KV_DIM = 128
N_BUCKETS = 32
MAX_DISTANCE = 128
D_FF = 2816
EPS = 1e-6
NEG_INF = -1e30
G_DIM = 2 * D_MODEL
A_DIM = 2 * A_WIDTH
B_DIM = Q_DIM + 2 * KV_DIM
LANES = 128
SUBLANES = 8
ROW_TILE = 512
WIDE_ROW_TILE = 256
COL_CHUNK = 512
GRAD_ROW_TILE = 512
STREAM_ROWS = 256
BF16_ROWS = 16
N_CHIPS = 4
N_DEV = 8

ADAM_LR = 0.001
ADAM_B1 = 0.9
ADAM_B2 = 0.999
ADAM_EPS = 1e-08
ADAM_WD = 0.01
ADAM_STEP = 10

MESH = pl.DeviceIdType.MESH
_GELU_C = 0.7978845608028654
_GELU_A = 0.044715


def _cp(sem=None, vmem_mb=None):
    kw = {}
    if sem is not None:
        kw["dimension_semantics"] = sem
    if vmem_mb is not None:
        kw["vmem_limit_bytes"] = vmem_mb << 20
    return pltpu.CompilerParams(**kw)


def _dot(a, b):
    return jnp.dot(a, b, preferred_element_type=F32)


def _dot_nt(a, b):
    return lax.dot_general(a, b, (((1,), (1,)), ((), ())), preferred_element_type=F32)


def _dot_tn(a, b):
    return lax.dot_general(a, b, (((0,), (0,)), ((), ())), preferred_element_type=F32)


def _rms_r(x):
    return lax.rsqrt(jnp.mean(x * x, axis=-1, keepdims=True) + EPS)


def _rms_bwd(dh, n, r, g):
    dn = dh * g
    return r * (dn - n * jnp.mean(dn * n, axis=-1, keepdims=True))


def _gelu(x):
    t = jnp.tanh(_GELU_C * (x + _GELU_A * (x * x * x)))
    return 0.5 * x * (1.0 + t), t


def _gelu_grad(x, t):
    return 0.5 * (1.0 + t) + 0.5 * x * (1.0 - t * t) * (_GELU_C * (1.0 + 3.0 * _GELU_A * x * x))


def _sigmoid(x):
    return 1.0 / (1.0 + jnp.exp(-x))


def _tie(x, dep):
    return x if dep is None else lax.optimization_barrier((x, dep))[0]


def _row(tm, w):
    return pl.BlockSpec((tm, w), lambda i: (i, 0))


def _full(shape):
    nd = len(shape)
    return pl.BlockSpec(tuple(shape), lambda *_: (0,) * nd)


def _resident(shape):
    nd = len(shape)
    return pl.BlockSpec(tuple(shape), lambda *_: (0,) * nd, pipeline_mode=pl.Buffered(1))


def _sds(shape, dtype):
    return pltpu.HBM(tuple(shape), dtype)


def _hbm(*arrays):
    return [pltpu.with_memory_space_constraint(a, pltpu.HBM) for a in arrays]


HBM = pl.BlockSpec(memory_space=pltpu.HBM)
ANY = pl.BlockSpec(memory_space=pl.ANY)
SEM = pl.BlockSpec(memory_space=pltpu.SEMAPHORE)


def _band_buckets():
    i = np.arange(CHUNK)[:, None]
    j = np.arange(2 * CHUNK)[None, :]
    dist = i + CHUNK - j
    valid = (dist >= 0) & (dist < CHUNK)
    d = np.clip(dist, 0, None)
    max_exact = N_BUCKETS // 2
    large = max_exact + (np.log(np.maximum(d, 1) / max_exact) / np.log(MAX_DISTANCE / max_exact)
                         * (N_BUCKETS - max_exact)).astype(np.int32)
    large = np.minimum(large, N_BUCKETS - 1)
    buckets = np.where(d < max_exact, d, large).astype(np.int32)
    return np.where(valid, buckets, -1).astype(np.int32)


_A_COLS = slice(0, A_DIM)
_B_COLS = slice(A_DIM, A_DIM + B_DIM)
_G_COLS = slice(A_DIM + B_DIM, A_DIM + B_DIM + G_DIM)


def _inproj(x2, g_mix, w_in, g_sgu, w_s, b_st, tm, after=None):
    T = x2.shape[0]
    order = [] if after is None else [after]

    def body(*refs):
        x_ref, g_ref, w_ref, gs_ref, ws_ref, bs_ref = refs[:6]
        pg_ref, pa_ref, pb_ref, h_ref, ya_ref = refs[6 + len(order):]
        x = x_ref[...]
        h = (x * _rms_r(x) * g_ref[...]).astype(BF16)
        h_ref[...] = h
        pa = _dot_nt(h, w_ref[_A_COLS, :]).astype(BF16)
        pa_ref[...] = pa
        pb_ref[...] = _dot_nt(h, w_ref[_B_COLS, :]).astype(BF16)
        pg_ref[...] = _dot_nt(h, w_ref[_G_COLS, :]).astype(BF16)
        _sgu_apply(pa.astype(F32), gs_ref[...], ws_ref, bs_ref, ya_ref)

    return pl.pallas_call(
        body, name="inproj", grid=(T // tm,),
        in_specs=[_row(tm, D_MODEL), _full(g_mix.shape), _resident(w_in.shape), _full(g_sgu.shape), _full(w_s.shape),
                  _full(b_st.shape)] + [ANY] * len(order),
        out_specs=[_row(tm, G_DIM), _row(tm, A_DIM), _row(tm, B_DIM), _row(tm, D_MODEL), _row(tm, A_WIDTH)],
        out_shape=[_sds((T, G_DIM), BF16), _sds((T, A_DIM), BF16), _sds((T, B_DIM), BF16), _sds((T, D_MODEL), BF16),
                   _sds((T, A_WIDTH), BF16)],
        compiler_params=_cp(("arbitrary",), 48),
    )(*_hbm(x2, g_mix, w_in, g_sgu, w_s, b_st), *order)


def _sgu_parts(p, g):
    pu = p[:, :A_WIDTH]
    pv = p[:, A_WIDTH:]
    u, tu = _gelu(pu)
    vv, tv = _gelu(pv)
    rv = _rms_r(vv)
    vn = (vv * rv * g).astype(BF16)
    return pu, pv, u, tu, vv, tv, rv, vn


def _tril():
    r = lax.broadcasted_iota(jnp.int32, (CHUNK, CHUNK), 0)
    c = lax.broadcasted_iota(jnp.int32, (CHUNK, CHUNK), 1)
    return r >= c


def _sgu_apply(p, g, ws_ref, bs_ref, y_ref):
    tril = _tril()
    _, _, u, _, _, _, _, vn = _sgu_parts(p, g)
    for gi in range(A_GROUPS):
        wm = jnp.where(tril, ws_ref[gi], 0.0).astype(BF16)
        bcol = bs_ref[:, gi:gi + 1]
        cs = slice(gi * CHUNK, (gi + 1) * CHUNK)
        for c in range(p.shape[0] // CHUNK):
            rs = slice(c * CHUNK, (c + 1) * CHUNK)
            s = _dot(wm, vn[rs, cs]) + bcol
            y_ref[rs, cs] = (u[rs, cs] * s).astype(BF16)


HEAD_ROWS = N_HEADS * CHUNK


def _head_rows(h):
    return slice(h * CHUNK, (h + 1) * CHUNK)


def _attn_setup(bias_scr, sink_scr, kvar_scr, qkv_ref, bk_ref, rel_ref, sink_ref):
    @pl.when(pl.program_id(0) == 0)
    def _():
        bk = bk_ref[...]
        for h in range(N_HEADS):
            acc = jnp.full((CHUNK, 2 * CHUNK), NEG_INF, F32)
            for b in range(N_BUCKETS):
                acc = jnp.where(bk == b, rel_ref[b, h], acc)
            bias_scr[_head_rows(h), :] = acc
            sink_scr[_head_rows(h), :] = jnp.full((CHUNK, LANES), sink_ref[0, h], F32)

    seq = qkv_ref.shape[0]
    rows_per = 2 * CHUNK
    for is_v in range(2):
        c0 = Q_DIM + is_v * KV_DIM
        for r in range(seq // rows_per):
            rs = slice(r * rows_per, (r + 1) * rows_per)
            a = qkv_ref[rs, c0:c0 + KV_DIM].astype(F32)
            lane = lax.broadcasted_iota(jnp.int32, a.shape, 1)
            lo = jnp.where(lane < HEAD_DIM, a, 0.0)
            hi = jnp.where(lane >= HEAD_DIM, a, 0.0)
            kvar_scr[4 * is_v + 0, rs, :] = lo.astype(BF16)
            kvar_scr[4 * is_v + 1, rs, :] = pltpu.roll(lo, HEAD_DIM, 1).astype(BF16)
            kvar_scr[4 * is_v + 2, rs, :] = pltpu.roll(hi, HEAD_DIM, 1).astype(BF16)
            kvar_scr[4 * is_v + 3, rs, :] = hi.astype(BF16)


def _rowsum(a, ones):
    hi = a.astype(BF16)
    lo = (a - hi.astype(F32)).astype(BF16)
    return _dot(hi, ones) + _dot(lo, ones)


def _both(a):
    return jnp.concatenate([a, a], axis=1)


def _attn_probs(qkv_ref, r0, n, kv, bias_scr, sink_scr, ones):
    s = jnp.concatenate([_dot_nt(qkv_ref[pl.ds(r0, CHUNK), (h // 2) * LANES:(h // 2 + 1) * LANES], kv[h // 4][h % 2])
                         for h in range(N_HEADS)], axis=0)
    s = s * (HEAD_DIM ** -0.5) + bias_scr[...]
    col = lax.broadcasted_iota(jnp.int32, s.shape, 1)
    s = jnp.where((col < CHUNK) & (n == 0), NEG_INF, s)
    sink = sink_scr[...]
    m = jnp.maximum(jnp.max(s, axis=-1, keepdims=True), sink)
    p = jnp.exp(s - _both(m))
    es = jnp.exp(sink - m)
    inv = 1.0 / (_dot(p.astype(BF16), ones) + es)
    return p * _both(inv), es * inv


def _attn_block_inputs(kvar_scr, n):
    r0 = pl.multiple_of(n * CHUNK, CHUNK)
    rp = pl.multiple_of(jnp.maximum(n - 1, 0) * CHUNK, CHUNK)

    def both(idx):
        return jnp.concatenate([kvar_scr[idx, pl.ds(rp, CHUNK), :], kvar_scr[idx, pl.ds(r0, CHUNK), :]], axis=0)

    kv = ((both(0), both(1)), (both(2), both(3)))
    vv = ((both(4), both(5)), (both(6), both(7)))
    return r0, kv, vv


def _attn_fwd(proj_b, sinks, rel_bias, n_seq, seq):
    nb = seq // CHUNK
    bk = jnp.asarray(_band_buckets())

    def body(qkv_ref, bk_ref, rel_ref, sink_ref, o_ref, bias_scr, sink_scr, kvar_scr):
        _attn_setup(bias_scr, sink_scr, kvar_scr, qkv_ref, bk_ref, rel_ref, sink_ref)
        ones = jnp.ones((2 * CHUNK, LANES), BF16)

        def blk(n, carry):
            r0, kv, vv = _attn_block_inputs(kvar_scr, n)
            prob, _ = _attn_probs(qkv_ref, r0, n, kv, bias_scr, sink_scr, ones)
            pb = prob.astype(BF16)
            for pr in range(N_HEADS // 2):
                acc = _dot(pb[_head_rows(2 * pr)], vv[pr // 2][0]) + _dot(pb[_head_rows(2 * pr + 1)], vv[pr // 2][1])
                o_ref[pl.ds(r0, CHUNK), pr * LANES:(pr + 1) * LANES] = acc.astype(BF16)
            return carry

        lax.fori_loop(0, nb, blk, 0)

    smem = pl.BlockSpec(memory_space=pltpu.SMEM)
    return pl.pallas_call(
        body, name="attn_fwd", grid=(n_seq,),
        in_specs=[_row(seq, B_DIM), _full(bk.shape), smem, smem],
        out_specs=_row(seq, Q_DIM), out_shape=_sds((n_seq * seq, Q_DIM), BF16),
        scratch_shapes=[pltpu.VMEM((HEAD_ROWS, 2 * CHUNK), F32), pltpu.VMEM((HEAD_ROWS, LANES), F32),
                        pltpu.VMEM((8, seq, KV_DIM), BF16)],
        compiler_params=_cp(("arbitrary",), 40),
    )(*_hbm(proj_b, bk), rel_bias, sinks)


def _dot_stacked(a, w_ref):
    return jnp.concatenate([_dot(a, w_ref[i]) for i in range(N_CHIPS)], axis=1)


def _dot_nt_stacked(a, w_ref):
    w = w_ref.shape[2]
    acc = _dot_nt(a[:, :w], w_ref[0])
    for i in range(1, N_CHIPS):
        acc = acc + _dot_nt(a[:, i * w:(i + 1) * w], w_ref[i])
    return acc


def _merge_fwd(x2, y_a, y_b, proj_g, w_pa, w_pb, w_out, tm):
    T = x2.shape[0]

    def body(x_ref, ya_ref, yb_ref, g_ref, wpa_ref, wpb_ref, wo_ref, x1_ref, mg_ref):
        g = g_ref[...].astype(F32)
        pa = _dot_stacked(ya_ref[...], wpa_ref)
        pb = _dot_stacked(yb_ref[...], wpb_ref)
        merged = (_sigmoid(g[:, :D_MODEL]) * pa + _sigmoid(g[:, D_MODEL:]) * pb).astype(BF16)
        mg_ref[...] = merged
        x1_ref[...] = x_ref[...] + _dot(merged, wo_ref[...])

    return pl.pallas_call(
        body, name="merge_fwd", grid=(T // tm,),
        in_specs=[_row(tm, D_MODEL), _row(tm, A_WIDTH), _row(tm, Q_DIM), _row(tm, G_DIM),
                  _resident(w_pa.shape), _resident(w_pb.shape), _resident(w_out.shape)],
        out_specs=[_row(tm, D_MODEL), _row(tm, D_MODEL)],
        out_shape=[_sds((T, D_MODEL), F32), _sds((T, D_MODEL), BF16)],
        compiler_params=_cp(("arbitrary",), 40),
    )(*_hbm(x2, y_a, y_b, proj_g, w_pa, w_pb, w_out))


def _upproj(x1, g_ffn, w_up, w_conv, b_conv, tm, seq):
    T = x1.shape[0]
    cw = w_up.shape[2]
    tiles_per_seq = seq // tm

    def body(x_ref, g_ref, w_ref, wc_ref, bc_ref, u_ref, h_ref, gate_ref, val_ref, tail_scr):
        at_start = (pl.program_id(0) % tiles_per_seq) == 0
        x = x_ref[...]
        h = (x * _rms_r(x) * g_ref[...]).astype(BF16)
        h_ref[...] = h
        for i in range(N_CHIPS):
            cs = slice(i * cw, (i + 1) * cw)
            u = _dot(h, w_ref[i])
            u_ref[:, cs] = u.astype(BF16)
            hl = jnp.where(at_start, 0.0, tail_scr[SUBLANES - 2:SUBLANES, cs])
            tail_scr[:, cs] = u[tm - SUBLANES:]
            up = _conv_out((u, _shift_down(u, hl, 1), _shift_down(u, hl, 2)), wc_ref[:, cs], bc_ref[:, cs])
            out_ref = gate_ref if i < N_CHIPS // 2 else val_ref
            out_ref[:, (i % 2) * cw:(i % 2 + 1) * cw] = up.astype(BF16)

    return pl.pallas_call(
        body, name="upproj", grid=(T // tm,),
        in_specs=[_row(tm, D_MODEL), _full(g_ffn.shape), _resident(w_up.shape), _full(w_conv.shape), _full(b_conv.shape)],
        out_specs=[_row(tm, 2 * D_FF), _row(tm, D_MODEL), _row(tm, D_FF), _row(tm, D_FF)],
        out_shape=[_sds((T, 2 * D_FF), BF16), _sds((T, D_MODEL), BF16), _sds((T, D_FF), BF16), _sds((T, D_FF), BF16)],
        scratch_shapes=[pltpu.VMEM((SUBLANES, 2 * D_FF), F32)],
        compiler_params=_cp(("arbitrary",), 56),
    )(*_hbm(x1, g_ffn, w_up, w_conv, b_conv))


def _shift_down(u, halo, k):
    rolled = pltpu.roll(u, k, 0)
    head = rolled[:SUBLANES]
    row = lax.broadcasted_iota(jnp.int32, head.shape, 0)
    if k == 1:
        head = jnp.where(row == 0, halo[1:2], head)
    else:
        head = jnp.where(row == 0, halo[0:1], jnp.where(row == 1, halo[1:2], head))
    return jnp.concatenate([head, rolled[SUBLANES:]], axis=0)


def _shift_up(d, halo, k):
    tm = d.shape[0]
    rolled = pltpu.roll(d, tm - k, 0)
    tail = rolled[tm - SUBLANES:]
    row = lax.broadcasted_iota(jnp.int32, tail.shape, 0)
    if k == 1:
        tail = jnp.where(row == SUBLANES - 1, halo[0:1], tail)
    else:
        tail = jnp.where(row == SUBLANES - 2, halo[0:1], jnp.where(row == SUBLANES - 1, halo[1:2], tail))
    return jnp.concatenate([rolled[:tm - SUBLANES], tail], axis=0)


def _conv_out(taps, wc, bc):
    u, u1, u2 = taps
    return wc[0:1] * u2 + wc[1:2] * u1 + wc[2:3] * u + bc


def _ffn_down_loss(gate, val, x1, target, w_down, g_final, tm):
    T = x1.shape[0]
    half = D_FF // 2

    sub = min(tm, 128)

    def body(gt_ref, vl_ref, x1_ref, t_ref, wd_ref, g_ref, dx2_ref, loss_ref, gg_ref):
        i = pl.program_id(0)
        g = g_ref[...]

        def down(rs):
            acc = jnp.zeros((sub, D_MODEL), F32)
            for j in range(2):
                gc = slice(j * half, (j + 1) * half)
                gate = gt_ref[rs, gc].astype(F32)
                act = (gate * _sigmoid(gate) * vl_ref[rs, gc].astype(F32)).astype(BF16)
                acc = acc + _dot(act, wd_ref[gc, :])
            return acc

        def norm_loss(rs, acc):
            x2 = x1_ref[rs, :] + acc
            r = _rms_r(x2)
            n = x2 * r
            diff = n * g - t_ref[rs, :]
            dy = diff * (1.0 / D_MODEL)
            dx2_ref[rs, :] = _rms_bwd(dy, n, r, g)
            return (jnp.sum(jnp.mean(diff * diff, axis=-1, keepdims=True), axis=0, keepdims=True),
                    jnp.sum(dy * n, axis=0, keepdims=True))

        subs = [slice(s0, s0 + sub) for s0 in range(0, tm, sub)]
        accs = [down(rs) for rs in subs]
        parts = [norm_loss(rs, acc) for rs, acc in zip(subs, accs)]

        @pl.when(i == 0)
        def _():
            loss_ref[...] = jnp.zeros_like(loss_ref)
            gg_ref[...] = jnp.zeros_like(gg_ref)

        loss_ref[...] += 0.5 * sum(p[0] for p in parts)
        gg_ref[...] += sum(p[1] for p in parts)

    return pl.pallas_call(
        body, name="ffn_down_loss", grid=(T // tm,),
        in_specs=[_row(tm, D_FF), _row(tm, D_FF), _row(tm, D_MODEL), _row(tm, D_MODEL),
                  _resident(w_down.shape), _full(g_final.shape)],
        out_specs=[_row(tm, D_MODEL), _full((1, 1)), _full((1, D_MODEL))],
        out_shape=[_sds((T, D_MODEL), F32), _sds((1, 1), F32), _sds((1, D_MODEL), F32)],
        compiler_params=_cp(("arbitrary",), 48),
    )(*_hbm(gate, val, x1, target, w_down, g_final))


def _ffn_bwd_act(gate, val, dx2, w_down, tm):
    T = dx2.shape[0]
    half = D_FF // 2
    nt = T // tm

    def body(g_ref, v_ref, dx_ref, wd_ref, dg_ref, dv_ref, gwd_out, gbg_ref, gbv_ref, gwd_ref):
        i = pl.program_id(1)

        @pl.when(i == 0)
        def _():
            for r in (gwd_ref, gbg_ref, gbv_ref):
                r[...] = jnp.zeros_like(r)

        dx = dx_ref[...].astype(BF16)
        for c0 in range(0, half, COL_CHUNK):
            cs = slice(c0, min(c0 + COL_CHUNK, half))
            gate = g_ref[:, cs].astype(F32)
            val = v_ref[:, cs].astype(F32)
            sg = _sigmoid(gate)
            silu = gate * sg
            d_act = _dot_nt(dx, wd_ref[cs, :])
            d_val = d_act * silu
            d_gate = d_act * val * (sg * (1.0 + gate * (1.0 - sg)))
            dg_ref[:, cs] = d_gate.astype(BF16)
            dv_ref[:, cs] = d_val.astype(BF16)
            gwd_ref[cs, :] += _dot_tn((silu * val).astype(BF16), dx)
            gbg_ref[:, cs] += jnp.sum(d_gate, axis=0, keepdims=True)
            gbv_ref[:, cs] += jnp.sum(d_val, axis=0, keepdims=True)

        @pl.when(i == nt - 1)
        def _():
            gwd_out[...] = gwd_ref[...].astype(BF16)

    tile = pl.BlockSpec((tm, half), lambda j, i: (i, j))
    vec = pl.BlockSpec((1, half), lambda j, i: (0, j))
    wrows = pl.BlockSpec((half, D_MODEL), lambda j, i: (j, 0))
    return pl.pallas_call(
        body, name="ffn_bwd_act", grid=(2, nt),
        in_specs=[tile, tile, pl.BlockSpec((tm, D_MODEL), lambda j, i: (i, 0)), wrows],
        out_specs=[tile, tile, wrows, vec, vec],
        out_shape=[_sds((T, D_FF), BF16), _sds((T, D_FF), BF16), _sds((D_FF, D_MODEL), BF16),
                   _sds((1, D_FF), F32), _sds((1, D_FF), F32)],
        scratch_shapes=[pltpu.VMEM((half, D_MODEL), F32)],
        compiler_params=_cp(("arbitrary", "arbitrary"), 56),
    )(*_hbm(gate, val, dx2, w_down))


def _ffn_bwd_up(d_gate, d_val, upre, dx2, x1, g_ffn, w_conv, w_up, tm, seq):
    T = dx2.shape[0]
    tiles_per_seq = seq // tm
    k16 = tm // BF16_ROWS
    n16 = T // BF16_ROWS
    cw = D_FF // 2

    def body(dg_ref, dv_ref, hg_ref, hv_ref, u_ref, dx2_ref, x1_ref, g_ref, wc_ref, wu_ref, du_ref, dx1_ref, gg_ref, gwc_ref):
        i = pl.program_id(0)
        at_end = (i % tiles_per_seq) == tiles_per_seq - 1

        @pl.when(i == 0)
        def _():
            gg_ref[...] = jnp.zeros_like(gg_ref)
            gwc_ref[...] = jnp.zeros_like(gwc_ref)

        dh = jnp.zeros((tm, D_MODEL), F32)
        for j in range(4):
            src, hsrc = (dg_ref, hg_ref) if j < 2 else (dv_ref, hv_ref)
            ls = slice((j % 2) * cw, (j % 2 + 1) * cw)
            cs = slice(j * cw, (j + 1) * cw)
            d = src[:, ls].astype(F32)
            hl = hsrc[:, ls].astype(F32)[0:2]
            hl = jnp.where(at_end, 0.0, hl)
            wc = wc_ref[:, cs]
            d1 = _shift_up(d, hl, 1)
            d2 = _shift_up(d, hl, 2)
            du = (wc[2:3] * d + wc[1:2] * d1 + wc[0:1] * d2).astype(BF16)
            du_ref[:, cs] = du
            dh = dh + _dot_nt(du, wu_ref[j])
            u = u_ref[:, cs].astype(F32)
            gwc_ref[0:1, cs] += jnp.sum(d2 * u, axis=0, keepdims=True)
            gwc_ref[1:2, cs] += jnp.sum(d1 * u, axis=0, keepdims=True)
            gwc_ref[2:3, cs] += jnp.sum(d * u, axis=0, keepdims=True)
        x = x1_ref[...]
        r = _rms_r(x)
        n = x * r
        dx1_ref[...] = dx2_ref[...] + _rms_bwd(dh, n, r, g_ref[...])
        gg_ref[...] += jnp.sum(dh * n, axis=0, keepdims=True)

    nxt = pl.BlockSpec((BF16_ROWS, D_FF), lambda i: (jnp.minimum((i + 1) * k16, n16 - 1), 0))
    return pl.pallas_call(
        body, name="ffn_bwd_up", grid=(T // tm,),
        in_specs=[_row(tm, D_FF), _row(tm, D_FF), nxt, nxt, _row(tm, 2 * D_FF), _row(tm, D_MODEL), _row(tm, D_MODEL),
                  _full(g_ffn.shape), _full(w_conv.shape), _resident(w_up.shape)],
        out_specs=[_row(tm, 2 * D_FF), _row(tm, D_MODEL), _full((1, D_MODEL)), _full((3, 2 * D_FF))],
        out_shape=[_sds((T, 2 * D_FF), BF16), _sds((T, D_MODEL), F32), _sds((1, D_MODEL), F32), _sds((3, 2 * D_FF), F32)],
        compiler_params=_cp(("arbitrary",), 56),
    )(*_hbm(d_gate, d_val, d_gate, d_val, upre, dx2, x1, g_ffn, w_conv, w_up))


def _matmul_tn(a, b, tn, tk, name):
    T, M = a.shape
    N = b.shape[1]
    nk = T // tk

    def body(a_ref, b_ref, o_ref, acc_ref):
        k = pl.program_id(1)

        @pl.when(k == 0)
        def _():
            acc_ref[...] = jnp.zeros_like(acc_ref)

        acc_ref[...] += _dot_tn(a_ref[...], b_ref[...])

        @pl.when(k == nk - 1)
        def _():
            o_ref[...] = acc_ref[...].astype(BF16)

    return pl.pallas_call(
        body, name=name, grid=(N // tn, nk),
        in_specs=[pl.BlockSpec((tk, M), lambda j, k: (k, 0)), pl.BlockSpec((tk, tn), lambda j, k: (k, j))],
        out_specs=pl.BlockSpec((M, tn), lambda j, k: (0, j)), out_shape=_sds((M, N), BF16),
        scratch_shapes=[pltpu.VMEM((M, tn), F32)],
        compiler_params=_cp(("arbitrary", "arbitrary"), 48),
    )(*_hbm(a, b))


def _merge_bwd(dx1, merged, y_a, y_b, proj_g, proj_a, w_pa, w_pb, w_out, g_sgu, w_s, b_st, tm, after=None):
    T = dx1.shape[0]

    nt = T // tm
    pshape = (A_WIDTH, D_MODEL)
    order = [] if after is None else [after]

    def body(*refs):
        dx_ref, mg_ref, ya_ref, yb_ref, g_ref, p_ref, wpa_ref, wpb_ref, wo_ref, gs_ref, ws_ref, bs_ref = refs[:12]
        (dg_ref, da_ref, dyb_ref, gwo_out, gwpa_out, gwpb_out, gws_ref, gbs_ref, gg_ref,
         gwo_ref, gwpa_ref, gwpb_ref) = refs[12 + len(order):]
        i = pl.program_id(0)

        @pl.when(i == 0)
        def _():
            for r in (gwo_ref, gwpa_ref, gwpb_ref, gws_ref, gbs_ref, gg_ref):
                r[...] = jnp.zeros_like(r)

        dx = dx_ref[...].astype(BF16)
        dm = _dot_nt(dx, wo_ref[...])
        g = g_ref[...].astype(F32)
        ya = ya_ref[...]
        yb = yb_ref[...]
        pa = _dot_stacked(ya, wpa_ref)
        pb = _dot_stacked(yb, wpb_ref)
        sa = _sigmoid(g[:, :D_MODEL])
        sb = _sigmoid(g[:, D_MODEL:])
        dpa = (dm * sa).astype(BF16)
        dpb = (dm * sb).astype(BF16)
        dg_ref[:, :D_MODEL] = (dm * pa * (sa * (1.0 - sa))).astype(BF16)
        dg_ref[:, D_MODEL:] = (dm * pb * (sb * (1.0 - sb))).astype(BF16)
        d_ya = _dot_nt_stacked(dpa, wpa_ref).astype(BF16)
        dyb_ref[...] = _dot_nt_stacked(dpb, wpb_ref).astype(BF16)
        _sgu_bwd_apply(p_ref[...].astype(F32), d_ya.astype(F32), gs_ref[...], ws_ref, bs_ref, da_ref, gws_ref, gbs_ref, gg_ref)
        gwo_ref[...] += _dot_tn(mg_ref[...], dx)
        gwpa_ref[...] += _dot_tn(ya, dpa)
        gwpb_ref[...] += _dot_tn(yb, dpb)

        @pl.when(i == nt - 1)
        def _():
            gwo_out[...] = gwo_ref[...].astype(BF16)
            gwpa_out[...] = gwpa_ref[...].astype(BF16)
            gwpb_out[...] = gwpb_ref[...].astype(BF16)

    return pl.pallas_call(
        body, name="merge_bwd", grid=(nt,),
        in_specs=[_row(tm, D_MODEL), _row(tm, D_MODEL), _row(tm, A_WIDTH), _row(tm, Q_DIM), _row(tm, G_DIM), _row(tm, A_DIM),
                  _resident(w_pa.shape), _resident(w_pb.shape), _resident(w_out.shape),
                  _full(g_sgu.shape), _full(w_s.shape), _full(b_st.shape)] + [ANY] * len(order),
        out_specs=[_row(tm, G_DIM), _row(tm, A_DIM), _row(tm, Q_DIM),
                   _full(w_out.shape), _full(pshape), _full(pshape), _full(w_s.shape), _full(b_st.shape), _full(g_sgu.shape)],
        out_shape=[_sds((T, G_DIM), BF16), _sds((T, A_DIM), BF16), _sds((T, Q_DIM), BF16),
                   _sds(w_out.shape, BF16), _sds(pshape, BF16), _sds(pshape, BF16),
                   _sds(w_s.shape, F32), _sds(b_st.shape, F32), _sds(g_sgu.shape, F32)],
        scratch_shapes=[pltpu.VMEM(w_out.shape, F32), pltpu.VMEM(pshape, F32), pltpu.VMEM(pshape, F32)],
        compiler_params=_cp(("arbitrary",), 56),
    )(*_hbm(dx1, merged, y_a, y_b, proj_g, proj_a, w_pa, w_pb, w_out, g_sgu, w_s, b_st), *order)


def _sgu_bwd_apply(p, dy, g, ws_ref, bs_ref, dp_ref, gws_ref, gbs_ref, gg_ref):
    tril = _tril()
    pu, pv, u, tu, vv, tv, rv, vn = _sgu_parts(p, g)
    du_cols = []
    dvn_cols = []
    for gi in range(A_GROUPS):
        wm = jnp.where(tril, ws_ref[gi], 0.0).astype(BF16)
        wmt = wm.astype(F32).T.astype(BF16)
        bcol = bs_ref[:, gi:gi + 1]
        cs = slice(gi * CHUNK, (gi + 1) * CHUNK)
        du_rows = []
        dvn_rows = []
        gw = jnp.zeros((CHUNK, CHUNK), F32)
        gb = jnp.zeros((CHUNK, 1), F32)
        for c in range(p.shape[0] // CHUNK):
            rs = slice(c * CHUNK, (c + 1) * CHUNK)
            vn_c = vn[rs, cs]
            s = _dot(wm, vn_c) + bcol
            dy_c = dy[rs, cs]
            ds = dy_c * u[rs, cs]
            du_rows.append(dy_c * s)
            dsb = ds.astype(BF16)
            gw = gw + _dot_nt(dsb, vn_c)
            gb = gb + jnp.sum(ds, axis=-1, keepdims=True)
            dvn_rows.append(_dot(wmt, dsb))
        gws_ref[gi] += jnp.where(tril, gw, 0.0)
        gbs_ref[:, gi:gi + 1] += gb
        du_cols.append(jnp.concatenate(du_rows, axis=0))
        dvn_cols.append(jnp.concatenate(dvn_rows, axis=0))
    du = jnp.concatenate(du_cols, axis=1)
    dvn = jnp.concatenate(dvn_cols, axis=1)
    vhat = vv * rv
    gg_ref[...] += jnp.sum(dvn * vhat, axis=0, keepdims=True)
    dvv = _rms_bwd(dvn, vhat, rv, g)
    dp_ref[:, :A_WIDTH] = (du * _gelu_grad(pu, tu)).astype(BF16)
    dp_ref[:, A_WIDTH:] = (dvv * _gelu_grad(pv, tv)).astype(BF16)


def _attn_bwd(proj_b, d_yb, sinks, rel_bias, n_seq, seq, after=None):
    nb = seq // CHUNK
    bk = jnp.asarray(_band_buckets())
    order = [] if after is None else [after]

    def body(*refs):
        qkv_ref, do_ref, bk_ref, rel_ref, sink_ref = refs[:5]
        (d_ref, gs_ref, gr_ref, bias_scr, sink_scr, kvar_scr, dbias_scr, dk_scr, dv_scr, ds_scr) = refs[5 + len(order):]
        b = pl.program_id(0)
        _attn_setup(bias_scr, sink_scr, kvar_scr, qkv_ref, bk_ref, rel_ref, sink_ref)
        ones = jnp.ones((2 * CHUNK, LANES), BF16)

        @pl.when(b == 0)
        def _():
            dbias_scr[...] = jnp.zeros_like(dbias_scr)
            ds_scr[...] = jnp.zeros_like(ds_scr)

        dk_scr[...] = jnp.zeros_like(dk_scr)
        dv_scr[...] = jnp.zeros_like(dv_scr)

        def transposed(a):
            return a.astype(F32).T.astype(BF16)

        def blk(n, carry):
            r0, kv, vv = _attn_block_inputs(kvar_scr, n)
            prob, psink = _attn_probs(qkv_ref, r0, n, kv, bias_scr, sink_scr, ones)
            dp = jnp.concatenate([_dot_nt(do_ref[pl.ds(r0, CHUNK), (h // 2) * LANES:(h // 2 + 1) * LANES], vv[h // 4][h % 2])
                                  for h in range(N_HEADS)], axis=0)
            delta = _rowsum(prob * dp, ones)
            dsc = prob * (dp - _both(delta))
            ds_scr[...] += psink * delta
            dbias_scr[...] += dsc
            dsb = (dsc * (HEAD_DIM ** -0.5)).astype(BF16)
            pb = prob.astype(BF16)
            dkt = [jnp.zeros((HEAD_DIM, 2 * CHUNK), F32) for _ in range(2)]
            dvt = [jnp.zeros((HEAD_DIM, 2 * CHUNK), F32) for _ in range(2)]
            for pr in range(N_HEADS // 2):
                ps = slice(pr * LANES, (pr + 1) * LANES)
                qpt = transposed(qkv_ref[pl.ds(r0, CHUNK), ps])
                dopt = transposed(do_ref[pl.ds(r0, CHUNK), ps])
                kvh = pr // 2
                dq = jnp.zeros((CHUNK, LANES), F32)
                for hh in range(2):
                    hr = _head_rows(2 * pr + hh)
                    rows = slice(hh * HEAD_DIM, (hh + 1) * HEAD_DIM)
                    dq = dq + _dot(dsb[hr], kv[kvh][hh])
                    dkt[kvh] = dkt[kvh] + _dot(qpt, dsb[hr])[rows]
                    dvt[kvh] = dvt[kvh] + _dot(dopt, pb[hr])[rows]
                d_ref[pl.ds(r0, CHUNK), ps] = dq.astype(BF16)
            dk_scr[:, pl.ds(r0, 2 * CHUNK)] += jnp.concatenate(dkt, axis=0)
            dv_scr[:, pl.ds(r0, 2 * CHUNK)] += jnp.concatenate(dvt, axis=0)
            return carry

        lax.fori_loop(0, nb, blk, 0)
        for n in range(nb):
            rows = slice(n * CHUNK, (n + 1) * CHUNK)
            cols = slice((n + 1) * CHUNK, (n + 2) * CHUNK)
            d_ref[rows, Q_DIM:Q_DIM + KV_DIM] = dk_scr[:, cols].T.astype(BF16)
            d_ref[rows, Q_DIM + KV_DIM:] = dv_scr[:, cols].T.astype(BF16)

        @pl.when(b == n_seq - 1)
        def _():
            bkv = bk_ref[...]
            for h in range(N_HEADS):
                gs_ref[0:1, h:h + 1] = -jnp.sum(ds_scr[_head_rows(h), 0:1], axis=0, keepdims=True)
                db = dbias_scr[_head_rows(h), :]
                for bb in range(N_BUCKETS):
                    part = jnp.sum(jnp.where(bkv == bb, db, 0.0), axis=-1, keepdims=True)
                    gr_ref[bb:bb + 1, h:h + 1] = jnp.sum(part, axis=0, keepdims=True)

    smem = pl.BlockSpec(memory_space=pltpu.SMEM)
    return pl.pallas_call(
        body, name="attn_bwd", grid=(n_seq,),
        in_specs=[_row(seq, B_DIM), _row(seq, Q_DIM), _full(bk.shape), smem, smem] + [ANY] * len(order),
        out_specs=[_row(seq, B_DIM), _full((1, N_HEADS)), _full((N_BUCKETS, N_HEADS))],
        out_shape=[_sds((n_seq * seq, B_DIM), BF16), _sds((1, N_HEADS), F32), _sds((N_BUCKETS, N_HEADS), F32)],
        scratch_shapes=[pltpu.VMEM((HEAD_ROWS, 2 * CHUNK), F32), pltpu.VMEM((HEAD_ROWS, LANES), F32),
                        pltpu.VMEM((8, seq, KV_DIM), BF16), pltpu.VMEM((HEAD_ROWS, 2 * CHUNK), F32),
                        pltpu.VMEM((KV_DIM, seq + CHUNK), F32), pltpu.VMEM((KV_DIM, seq + CHUNK), F32),
                        pltpu.VMEM((HEAD_ROWS, LANES), F32)],
        compiler_params=_cp(("arbitrary",), 40),
    )(*_hbm(proj_b, d_yb, bk), rel_bias, sinks, *order)


def _inproj_bwd(d_g, d_a, d_b, x2, dx1, g_mix, w_in, tm, after=None):
    T = x2.shape[0]
    sub = min(tm, 128)
    order = [] if after is None else [after]

    def body(*refs):
        dg_ref, da_ref, db_ref, x_ref, dx1_ref, g_ref, w_ref = refs[:7]
        gx_ref, gg_ref = refs[7 + len(order):]
        subs = [slice(s0, s0 + sub) for s0 in range(0, tm, sub)]
        dhs = [_dot(dg_ref[rs, :], w_ref[_G_COLS, :]) + _dot(da_ref[rs, :], w_ref[_A_COLS, :])
               + _dot(db_ref[rs, :], w_ref[_B_COLS, :]) for rs in subs]
        gg = jnp.zeros((1, D_MODEL), F32)
        for rs, dh in zip(subs, dhs):
            x = x_ref[rs, :]
            r = _rms_r(x)
            n = x * r
            gx_ref[rs, :] = dx1_ref[rs, :] + _rms_bwd(dh, n, r, g_ref[...])
            gg = gg + jnp.sum(dh * n, axis=0, keepdims=True)

        @pl.when(pl.program_id(0) == 0)
        def _():
            gg_ref[...] = jnp.zeros_like(gg_ref)

        gg_ref[...] += gg

    return pl.pallas_call(
        body, name="inproj_bwd", grid=(T // tm,),
        in_specs=[_row(tm, G_DIM), _row(tm, A_DIM), _row(tm, B_DIM), _row(tm, D_MODEL), _row(tm, D_MODEL),
                  _full(g_mix.shape), _resident(w_in.shape)] + [ANY] * len(order),
        out_specs=[_row(tm, D_MODEL), _full((1, D_MODEL))],
        out_shape=[_sds((T, D_MODEL), F32), _sds((1, D_MODEL), F32)],
        compiler_params=_cp(("arbitrary",), 48),
    )(*_hbm(d_g, d_a, d_b, x2, dx1, g_mix, w_in), *order)


IN_SHARD = (A_DIM + B_DIM + G_DIM) // N_CHIPS


def _grad_w_in(h, d_a, d_b, d_g, tk):
    T = h.shape[0]
    nk = T // tk
    in_dim = N_CHIPS * IN_SHARD

    def body(h_ref, da_ref, db_ref, dg_ref, o_ref, acc_ref):
        k = pl.program_id(0)

        @pl.when(k == 0)
        def _():
            acc_ref[...] = jnp.zeros_like(acc_ref)

        hb = h_ref[...]
        acc_ref[:, _A_COLS] += _dot_tn(hb, da_ref[...])
        acc_ref[:, _B_COLS] += _dot_tn(hb, db_ref[...])
        acc_ref[:, _G_COLS] += _dot_tn(hb, dg_ref[...])

        @pl.when(k == nk - 1)
        def _():
            for i in range(N_CHIPS):
                o_ref[i] = acc_ref[:, i * IN_SHARD:(i + 1) * IN_SHARD].astype(BF16)

    return pl.pallas_call(
        body, name="grad_w_in", grid=(nk,),
        in_specs=[_row(tk, D_MODEL), _row(tk, A_DIM), _row(tk, B_DIM), _row(tk, G_DIM)],
        out_specs=_full((N_CHIPS, D_MODEL, IN_SHARD)), out_shape=_sds((N_CHIPS, D_MODEL, IN_SHARD), BF16),
        scratch_shapes=[pltpu.VMEM((D_MODEL, in_dim), F32)],
        compiler_params=_cp(("arbitrary",), 56),
    )(*_hbm(h, d_a, d_b, d_g))


def _local_step(x, target, g_mix, g_sgu, w_s, b_s, sinks, rel_bias, g_ffn, b_conv, g_final,
                w_in, w_conv, proj_weights, ffn_weights, on_grads, after=None):
    n_seq, seq, _ = x.shape
    T = n_seq * seq
    tm = min(ROW_TILE, seq)
    tw = min(GRAD_ROW_TILE, T)
    tf = min(WIDE_ROW_TILE, seq)
    x2 = x.reshape(T, D_MODEL)
    tgt = target.reshape(T, D_MODEL)
    b_st = b_s.T
    g_fin = g_final.reshape(1, D_MODEL)

    proj_g, proj_a, proj_b, h, y_a = _inproj(x2, g_mix, w_in, g_sgu, w_s, b_st, tm, after)
    y_b = _attn_fwd(proj_b, sinks, rel_bias, n_seq, seq)
    w_pa, w_pb, w_out = proj_weights(y_b)
    x1, merged = _merge_fwd(x2, y_a, y_b, proj_g, w_pa, w_pb, w_out, tm)
    w_up, w_down = ffn_weights(x1)
    upre, h2, gate, val = _upproj(x1, g_ffn, w_up, w_conv, b_conv, tf, seq)
    dx2, loss, gg_final = _ffn_down_loss(gate, val, x1, tgt, w_down, g_fin, tm)

    d_gate, d_val, gw_down, gb_g, gb_v = _ffn_bwd_act(gate, val, dx2, w_down, tw)
    gb_conv = jnp.concatenate([gb_g, gb_v], axis=1)
    d_upre, dx1, gg_ffn, gw_conv = _ffn_bwd_up(d_gate, d_val, upre, dx2, x1, g_ffn, w_conv, w_up, tf, seq)
    gw_up = _matmul_tn(h2, d_upre, 2 * D_FF // 4, min(4 * GRAD_ROW_TILE, T), "grad_w_up")
    sent = on_grads("ffn", dict(w_up=gw_up, w_down=gw_down))
    d_g, d_a, d_yb, gw_out, gw_pa, gw_pb, gw_s, gb_st, gg_sgu = _merge_bwd(
        dx1, merged, y_a, y_b, proj_g, proj_a, w_pa, w_pb, w_out, g_sgu, w_s, b_st, tw, sent)
    sent = on_grads("proj", dict(w_pa=gw_pa, w_pb=gw_pb, w_out=gw_out))
    d_b, g_sinks, g_rel = _attn_bwd(proj_b, d_yb, sinks, rel_bias, n_seq, seq, sent)
    gw_in = _grad_w_in(h, d_a, d_b, d_g, min(2 * GRAD_ROW_TILE, T))
    sent = on_grads("in", dict(w_in=gw_in))
    grad_x, gg_mix = _inproj_bwd(d_g, d_a, d_b, x2, dx1, g_mix, w_in, tm, sent)

    small = dict(g_mix=gg_mix, g_sgu=gg_sgu, w_s=gw_s, b_s=gb_st.T, sinks=g_sinks, rel_bias=g_rel,
                 g_ffn=gg_ffn, b_conv=gb_conv, g_final=gg_final, w_conv=gw_conv)
    big = dict(w_in=gw_in, w_pa=gw_pa, w_pb=gw_pb, w_out=gw_out, w_up=gw_up, w_down=gw_down)
    return loss, grad_x.reshape(x.shape), small, big


_MIXER = ("w_in", "w_pa", "w_pb", "w_out")
_FFN = ("w_up", "w_down")
_BIG = _MIXER + _FFN

CONV_ROWS = 6
_SMALL_AT = dict(loss=(0, 1, 1), g_sgu=(4, 1, A_WIDTH), sinks=(5, 1, N_HEADS), b_s=(8, A_GROUPS, CHUNK),
                 b_conv=(12, CONV_ROWS, D_MODEL), w_conv=(18, 3 * CONV_ROWS, D_MODEL),
                 g_final=(36, 1, D_MODEL), g_mix=(37, 1, D_MODEL), g_ffn=(38, 1, D_MODEL),
                 w_s=(40, A_GROUPS * CHUNK * CHUNK // D_MODEL, D_MODEL))
_REL_BIAS_AT = (0, A_WIDTH)
_SMALL_IN_CALL = ("g_final", "g_mix", "g_ffn", "g_sgu", "sinks", "b_s", "b_conv", "rel_bias", "w_s")
SMALL_ROWS = 104


def _pack_small(vals):
    def wide(a):
        return jnp.pad(a, ((0, 0), (0, CONV_ROWS * D_MODEL - a.shape[1]))).reshape(-1, D_MODEL)

    nr = _SMALL_AT["w_s"][1]
    w_s = vals["w_s"].reshape(D_MODEL // CHUNK, nr, CHUNK).transpose(1, 0, 2).reshape(nr, D_MODEL)
    laid = dict(vals, b_conv=wide(vals["b_conv"]), w_conv=wide(vals["w_conv"]), w_s=w_s)
    rows, at = [], 0
    for n, (r0, nr, nc) in _SMALL_AT.items():
        if r0 > at:
            rows.append(jnp.zeros((r0 - at, D_MODEL), F32))
        rows.append(jnp.pad(laid[n].astype(F32).reshape(nr, nc), ((0, 0), (0, D_MODEL - nc))))
        at = r0 + nr
    return lax.dynamic_update_slice(jnp.concatenate(rows, axis=0), vals["rel_bias"].T, _REL_BIAS_AT)


def _unwide(a, r):
    return a.reshape(r, CONV_ROWS * D_MODEL)[:, :2 * D_FF]


def _mesh_pos():
    return lax.axis_index("x"), lax.axis_index("y"), lax.axis_index("c")


def _other_chips(x, y):
    return [(1 - x, y), (x, 1 - y), (1 - x, 1 - y)]


def _remote(src, dst, send_sem, recv_sem, to):
    return pltpu.make_async_remote_copy(src_ref=src, dst_ref=dst, send_sem=send_sem, recv_sem=recv_sem,
                                        device_id=to, device_id_type=MESH)


def _own_slot(own, n, at):
    return lax.dynamic_update_slice(lax.empty((n,) + own.shape, own.dtype), own[None], (at,) + (0,) * own.ndim)


def _allgather_weights(stacks, wc_stack):
    names = list(stacks)
    n = len(names)

    def body(*refs):
        ins, outs = refs[:n + 1], refs[n + 1:2 * n + 2]
        send_sems, recv_sems = refs[2 * n + 2:]
        x, y, c = _mesh_pos()
        _handshake(_chip_peers(x, y, c) + _sibling_peers(x, y, c))
        me = 2 * x + y
        sibling = (x, y, 1 - c)
        chips = _other_chips(x, y)

        def half(ref, chip, hc):
            hr = ref.shape[1] // 2
            return ref.at[chip, pl.ds(hc * hr, hr), :]

        first = []
        for k in range(n):
            first += [_remote(half(ins[k], me, c), half(outs[k], me, c), send_sems.at[6 * k + j], recv_sems.at[6 * k + j], (cx, cy, c))
                      for j, (cx, cy) in enumerate(chips)]
        first += [_remote(ins[n].at[me], outs[n].at[me], send_sems.at[6 * n + j], recv_sems.at[6 * n + j], (cx, cy, c))
                  for j, (cx, cy) in enumerate(chips)]
        for cp in first:
            cp.start()
        passed = []
        for k in range(n):
            for j, (cx, cy) in enumerate(chips):
                landed = half(outs[k], 2 * cx + cy, c)
                _remote(landed, landed, send_sems.at[6 * k + j], recv_sems.at[6 * k + j], (x, y, c)).wait_recv()
                passed.append(_remote(landed, landed, send_sems.at[6 * k + 3 + j], recv_sems.at[6 * k + 3 + j], sibling))
                passed[-1].start()
        for k in range(n):
            for j, (cx, cy) in enumerate(chips):
                theirs = half(outs[k], 2 * cx + cy, 1 - c)
                _remote(theirs, theirs, send_sems.at[6 * k + 3 + j], recv_sems.at[6 * k + 3 + j], (x, y, c)).wait_recv()
        for j, (cx, cy) in enumerate(chips):
            slot = outs[n].at[2 * cx + cy]
            _remote(slot, slot, send_sems.at[6 * n + j], recv_sems.at[6 * n + j], (x, y, c)).wait_recv()
        for cp in first + passed:
            cp.wait_send()

    arrays = [stacks[k] for k in names] + [wc_stack]
    outs = pl.pallas_call(
        body, name="allgather_weights",
        in_specs=[HBM] * (n + 1), out_specs=[HBM] * (n + 1), input_output_aliases={k: k for k in range(n + 1)},
        out_shape=[_sds(a.shape, a.dtype) for a in arrays],
        scratch_shapes=[pltpu.SemaphoreType.DMA((6 * n + 3,)), pltpu.SemaphoreType.DMA((6 * n + 3,))],
        compiler_params=pltpu.CompilerParams(collective_id=_COLLECTIVE["gather_in"]),
    )(*arrays)
    return dict(zip(names, outs[:n])), outs[n]


_KIND = {"w_in": "stack", "w_pa": "col", "w_pb": "col", "w_up": "col", "w_out": "row", "w_down": "row"}


def _half_view(ref, kind, h):
    if kind == "stack":
        k = ref.shape[1] // 2
        return ref.at[:, pl.ds(h * k, k), :]
    if kind == "col":
        k = ref.shape[0] // 2
        return ref.at[pl.ds(h * k, k), :]
    k = ref.shape[1] // 2
    return ref.at[:, pl.ds(h * k, k)]


def _shard_view(ref, kind, i):
    if kind == "stack":
        return ref.at[i]
    if kind == "col":
        k = ref.shape[1] // N_CHIPS
        return ref.at[:, pl.ds(i * k, k)]
    k = ref.shape[0] // N_CHIPS
    return ref.at[pl.ds(i * k, k), :]


def _region_view(ref, kind, h):
    if kind == "row":
        k = ref.shape[1] // 2
        return ref.at[:, pl.ds(h * k, k)]
    k = ref.shape[0] // 2
    return ref.at[pl.ds(h * k, k), :]


def _half_shape(shape, kind):
    if kind == "stack":
        return (shape[0], shape[1] // 2, shape[2])
    return (shape[0] // 2, shape[1]) if kind == "col" else (shape[0], shape[1] // 2)


def _part_shape(half_shape, kind):
    if kind == "stack":
        return tuple(half_shape[1:])
    k, w = half_shape
    return (k, w // N_CHIPS) if kind == "col" else (k // N_CHIPS, w)


_DATAFLOW = pltpu.SideEffectType.DATAFLOW_SIDE_EFFECTING
_TOKEN = (SUBLANES, LANES)


_COLLECTIVE = {k: i for i, k in enumerate(
    [kind + "_" + g for kind in ("pair", "chip", "share") for g in ("ffn", "proj", "in")]
    + ["gather_proj", "gather_ffn", "gather_in", "forward_proj", "forward_ffn"])}


def _sibling_peers(x, y, c):
    return [(x, y, 1 - c)]


def _chip_peers(x, y, c):
    return [(cx, cy, c) for cx, cy in _other_chips(x, y)]


def _handshake(peers):
    barrier = pltpu.get_barrier_semaphore()
    for peer in peers:
        pl.semaphore_signal(barrier, inc=1, device_id=peer, device_id_type=MESH)
    pl.semaphore_wait(barrier, len(peers))


def _split_start(name, arrays, n_sems, issue, after=None, handshake=None):
    n = len(arrays)
    order = [] if after is None else [after]

    def body(*refs):
        base = n + len(order)
        if handshake is not None:
            _handshake(handshake[1](*_mesh_pos()))
        issue(refs[:n], refs[base], refs[base + 1])
        refs[-1][...] = jnp.zeros(_TOKEN, F32)

    params = dict(has_side_effects=_DATAFLOW)
    if handshake is not None:
        params["collective_id"] = handshake[0]
    outs = pl.pallas_call(
        body, name=name,
        in_specs=[HBM] * n + [ANY] * len(order), out_specs=[SEM, SEM] + [HBM] * n + [pl.BlockSpec(memory_space=pltpu.VMEM)],
        out_shape=[pltpu.SemaphoreType.DMA((n_sems,)), pltpu.SemaphoreType.DMA((n_sems,))]
        + [pltpu.HBM(a.shape, a.dtype) for a in arrays] + [_sds(_TOKEN, F32)],
        input_output_aliases={k: 2 + k for k in range(n)},
        compiler_params=pltpu.CompilerParams(**params),
    )(*[pltpu.with_memory_space_constraint(a, pltpu.HBM) for a in arrays], *order)
    return outs[0], outs[1], list(outs[2:2 + n]), outs[-1]


def _split_wait(name, started, waits, after):
    send_sems, recv_sems, arrays, _ = started
    n = len(arrays)

    def body(*refs):
        waits(refs[:n], refs[n], refs[n + 1])

    return pl.pallas_call(
        body, name=name,
        in_specs=[HBM] * n + [SEM, SEM, ANY], out_specs=[HBM] * n,
        out_shape=[pltpu.HBM(a.shape, a.dtype) for a in arrays],
        input_output_aliases={k: k for k in range(n)},
        compiler_params=pltpu.CompilerParams(has_side_effects=_DATAFLOW),
    )(*arrays, send_sems, recv_sems, after)


def _wait_both(src, dst, send_sem, recv_sem):
    x, y, c = _mesh_pos()
    cp = _remote(src, dst, send_sem, recv_sem, (x, y, c))
    cp.wait_send()
    cp.wait_recv()


def _pair_exchange_start(parts, tag, after):
    names = list(parts)
    n = len(names)
    lands = [lax.empty(_half_shape(parts[k].shape, _KIND[k]), parts[k].dtype) for k in names]

    def issue(refs, send_sems, recv_sems):
        x, y, c = _mesh_pos()
        for hc in range(2):
            @pl.when(c == hc)
            def _():
                for k in range(n):
                    _remote(_half_view(refs[k], _KIND[names[k]], 1 - hc), refs[n + k], send_sems.at[k], recv_sems.at[k],
                            (x, y, 1 - c)).start()

    return names, _split_start("grad_pair_exchange_start_" + tag, [parts[k] for k in names] + lands, n, issue, after,
                               (_COLLECTIVE["pair_" + tag], _sibling_peers))


def _pair_exchange_wait(pending, tag, after):
    names, started = pending
    n = len(names)

    def waits(refs, send_sems, recv_sems):
        for k in range(n):
            _wait_both(_half_view(refs[k], _KIND[names[k]], 0), refs[n + k], send_sems.at[k], recv_sems.at[k])

    outs = _split_wait("grad_pair_exchange_wait_" + tag, started, waits, after)
    return dict(zip(names, outs[:n])), dict(zip(names, outs[n:]))


def _half_blocks(shape, kind):
    if kind == "stack":
        _, k, w = shape
        return (N_CHIPS // 2, 1), (2, k // 2, w), (lambda i, r, s: (i, r, 0)), (lambda i, r, s: (i, s[1] + r, 0))
    k, w = shape
    if kind == "col":
        tr = STREAM_ROWS
        nb = k // 2 // tr
        return (nb,), (tr, w), (lambda r, s: (r, 0)), (lambda r, s: (s[1] * nb + r, 0))
    nb = 2
    return (nb,), (k // nb, w // 2), (lambda r, s: (r, 0)), (lambda r, s: (r, s[1]))


def _pair_add(part, from_sibling, name, pos):
    kind = _KIND[name]
    grid, block, half_map, full_map = _half_blocks(part.shape, kind)

    def body(s_ref, p_ref, q_ref, o_ref):
        o_ref[...] = (p_ref[...].astype(F32) + q_ref[...].astype(F32)).astype(BF16)

    return pl.pallas_call(
        body, name="grad_pair_add_" + name,
        grid_spec=pltpu.PrefetchScalarGridSpec(
            num_scalar_prefetch=1, grid=grid,
            in_specs=[pl.BlockSpec(block, full_map), pl.BlockSpec(block, half_map)],
            out_specs=pl.BlockSpec(block, half_map)),
        out_shape=_sds(from_sibling.shape, BF16),
        compiler_params=_cp(("arbitrary",) * len(grid), 40),
    )(pos, *_hbm(part, from_sibling))


def _pair_add_group(parts, from_sibling, tag, pos):
    names = list(parts)
    n = len(names)
    full_specs, half_specs = [], []
    for name in names:
        k, w = parts[name].shape
        if _KIND[name] == "col":
            block, full_map = (k // 4, w), (lambda r, s: (2 * s[1] + r, 0))
        else:
            block, full_map = (k // 2, w // 2), (lambda r, s: (r, s[1]))
        full_specs.append(pl.BlockSpec(block, full_map))
        half_specs.append(pl.BlockSpec(block, lambda r, s: (r, 0)))

    def body(s_ref, *refs):
        for k in range(n):
            refs[2 * n + k][...] = (refs[k][...].astype(F32) + refs[n + k][...].astype(F32)).astype(BF16)

    outs = pl.pallas_call(
        body, name="grad_pair_add_" + tag,
        grid_spec=pltpu.PrefetchScalarGridSpec(
            num_scalar_prefetch=1, grid=(2,), in_specs=full_specs + half_specs, out_specs=half_specs),
        out_shape=[_sds(from_sibling[k].shape, BF16) for k in names],
        compiler_params=_cp(("arbitrary",), 40),
    )(pos, *_hbm(*[parts[k] for k in names], *[from_sibling[k] for k in names]))
    return dict(zip(names, outs))


def _owner_sum_group(parts, from_sibling, from_chips, tag, pos, shard_shapes):
    names = list(parts)
    n = len(names)
    p_specs, q_specs, r_specs, o_specs = [], [], [], []
    for name in names:
        _, pk, pw = from_chips[name].shape
        block = (pk // 2, pw)
        if _KIND[name] == "row":
            maps = (lambda r, s: (2 * s[0] + r, s[1])), (lambda r, s: (2 * s[0] + r, 0)), (lambda r, s: (r, s[1]))
        else:
            maps = (lambda r, s: (2 * s[1] + r, s[0])), (lambda r, s: (r, s[0])), (lambda r, s: (2 * s[1] + r, 0))
        p_specs.append(pl.BlockSpec(block, maps[0]))
        q_specs.append(pl.BlockSpec(block, maps[1]))
        o_specs.append(pl.BlockSpec(block, maps[2]))
        r_specs.append(pl.BlockSpec((3,) + block, lambda r, s: (0, r, 0)))

    def body(s_ref, *refs):
        for k in range(n):
            acc = refs[k][...].astype(F32) + refs[n + k][...].astype(F32)
            for j in range(3):
                acc = acc + refs[2 * n + k][j].astype(F32)
            refs[3 * n + k][...] = acc

    outs = pl.pallas_call(
        body, name="grad_owner_sum_" + tag,
        grid_spec=pltpu.PrefetchScalarGridSpec(
            num_scalar_prefetch=1, grid=(2,), in_specs=p_specs + q_specs + r_specs, out_specs=o_specs),
        out_shape=[_sds(shard_shapes[k], F32) for k in names],
        compiler_params=_cp(("arbitrary",), 32),
    )(pos, *_hbm(*[parts[k] for k in names], *[from_sibling[k] for k in names], *[from_chips[k] for k in names]))
    return dict(zip(names, outs))


def _chip_exchange_start(sums, tag, after):
    names = list(sums)
    n = len(names)
    lands = [lax.empty((3,) + _part_shape(sums[k].shape, _KIND[k]), sums[k].dtype) for k in names]

    def issue(refs, send_sems, recv_sems):
        x, y, c = _mesh_pos()
        me = 2 * x + y
        for i in range(N_CHIPS):
            xi, yi = i // 2, i % 2
            j = jnp.where(xi != x, jnp.where(yi != y, 2, 0), 1)

            @pl.when(i != me)
            def _():
                for k in range(n):
                    _remote(_shard_view(refs[k], _KIND[names[k]], i), refs[n + k].at[j], send_sems.at[3 * k + j],
                            recv_sems.at[3 * k + j], (xi, yi, c)).start()

    return names, _split_start("grad_chip_exchange_start_" + tag, [sums[k] for k in names] + lands, 3 * n, issue, after,
                               (_COLLECTIVE["chip_" + tag], _chip_peers))


def _chip_exchange_wait(pending, tag, after):
    names, started = pending
    n = len(names)

    def waits(refs, send_sems, recv_sems):
        for k in range(n):
            for j in range(3):
                _wait_both(_shard_view(refs[k], _KIND[names[k]], 0), refs[n + k].at[j], send_sems.at[3 * k + j], recv_sems.at[3 * k + j])

    return dict(zip(names, _split_wait("grad_chip_exchange_wait_" + tag, started, waits, after)[n:]))


def _allgather_start(stacks, tag, after):
    names = list(stacks)

    def issue(refs, send_sems, recv_sems):
        x, y, c = _mesh_pos()
        me = 2 * x + y
        for k, st in enumerate(refs):
            hr = st.shape[1] // 2
            mine = st.at[me, pl.ds(c * hr, hr), :]
            for j, (cx, cy) in enumerate(_other_chips(x, y)):
                _remote(mine, mine, send_sems.at[3 * k + j], recv_sems.at[3 * k + j], (cx, cy, c)).start()

    return names, _split_start("allgather_start_" + tag, [stacks[k] for k in names], 3 * len(names), issue, after,
                               (_COLLECTIVE["gather_" + tag], _chip_peers))


def _allgather_wait(pending, tag, after):
    names, started = pending

    def waits(refs, send_sems, recv_sems):
        for k, st in enumerate(refs):
            slot = st.at[0, pl.ds(0, st.shape[1] // 2), :]
            for j in range(3):
                _wait_both(slot, slot, send_sems.at[3 * k + j], recv_sems.at[3 * k + j])

    return dict(zip(names, _split_wait("allgather_wait_" + tag, started, waits, after)))


def _allgather_forward(stacks, tag):
    names = list(stacks)
    n = len(names)

    def body(*refs):
        ins, outs = refs[:n], refs[n:2 * n]
        send_sems, recv_sems = refs[2 * n:]
        x, y, c = _mesh_pos()
        _handshake(_sibling_peers(x, y, c))
        copies = []
        for k in range(n):
            hr = ins[k].shape[1] // 2
            for j, (cx, cy) in enumerate(_other_chips(x, y)):
                chip = 2 * cx + cy
                copies.append(_remote(ins[k].at[chip, pl.ds(c * hr, hr), :], outs[k].at[chip, pl.ds(c * hr, hr), :],
                                      send_sems.at[3 * k + j], recv_sems.at[3 * k + j], (x, y, 1 - c)))
        for cp in copies:
            cp.start()
        for cp in copies:
            cp.wait()

    arrays = [stacks[k] for k in names]
    outs = pl.pallas_call(
        body, name="allgather_forward_" + tag, in_specs=[HBM] * n, out_specs=[HBM] * n,
        input_output_aliases={k: k for k in range(n)},
        out_shape=[_sds(a.shape, a.dtype) for a in arrays],
        scratch_shapes=[pltpu.SemaphoreType.DMA((3 * n,)), pltpu.SemaphoreType.DMA((3 * n,))],
        compiler_params=pltpu.CompilerParams(collective_id=_COLLECTIVE["forward_" + tag]),
    )(*arrays)
    return dict(zip(names, outs))


def _owner_sum(part, from_sibling, from_chips, name, pos, shard_shape):
    kind = _KIND[name]
    _, pk, pw = from_chips.shape
    if kind == "row":
        nb = 1
        tr = pk // nb
        p_spec = pl.BlockSpec((tr, pw), lambda r, s: (s[0] * nb + r, s[1]))
        q_spec = pl.BlockSpec((tr, pw), lambda r, s: (s[0] * nb + r, 0))
        o_spec = pl.BlockSpec((tr, pw), lambda r, s: (r, s[1]))
    else:
        tr = STREAM_ROWS
        nb = pk // tr
        if kind == "stack":
            p_spec = pl.BlockSpec((None, tr, pw), lambda r, s: (s[0], s[1] * nb + r, 0))
            q_spec = pl.BlockSpec((None, tr, pw), lambda r, s: (s[0], r, 0))
        else:
            p_spec = pl.BlockSpec((tr, pw), lambda r, s: (s[1] * nb + r, s[0]))
            q_spec = pl.BlockSpec((tr, pw), lambda r, s: (r, s[0]))
        o_spec = pl.BlockSpec((tr, pw), lambda r, s: (s[1] * nb + r, 0))

    def body(s_ref, p_ref, q_ref, r_ref, o_ref):
        acc = p_ref[...].astype(F32) + q_ref[...].astype(F32)
        for j in range(3):
            acc = acc + r_ref[j].astype(F32)
        o_ref[...] = acc

    return pl.pallas_call(
        body, name="grad_owner_sum_" + name,
        grid_spec=pltpu.PrefetchScalarGridSpec(
            num_scalar_prefetch=1, grid=(nb,),
            in_specs=[p_spec, q_spec, pl.BlockSpec((3, tr, pw), lambda r, s: (0, r, 0))],
            out_specs=o_spec),
        out_shape=_sds(shard_shape, F32),
        compiler_params=_cp(("arbitrary",), 32),
    )(pos, *_hbm(part, from_sibling, from_chips))


def _pair_share_start(shards, tag, after):
    names = list(shards)

    def issue(refs, send_sems, recv_sems):
        x, y, c = _mesh_pos()
        for hc in range(2):
            @pl.when(c == hc)
            def _():
                for k, g in enumerate(refs):
                    mine = _region_view(g, _KIND[names[k]], hc)
                    _remote(mine, mine, send_sems.at[k], recv_sems.at[k], (x, y, 1 - c)).start()

    return names, _split_start("grad_pair_share_start_" + tag, [shards[k] for k in names], len(names), issue, after,
                               (_COLLECTIVE["share_" + tag], _sibling_peers))


def _pair_share_wait(pending, tag, after):
    names, started = pending

    def waits(refs, send_sems, recv_sems):
        for k, g in enumerate(refs):
            region = _region_view(g, _KIND[names[k]], 0)
            _wait_both(region, region, send_sems.at[k], recv_sems.at[k])

    return dict(zip(names, _split_wait("grad_pair_share_wait_" + tag, started, waits, after)))


def _small_exchange_start(slots, after):
    def issue(refs, send_sems, recv_sems):
        x, y, c = _mesh_pos()
        mine = refs[0].at[4 * x + 2 * y + c]
        k = 0
        for px in range(2):
            for py in range(2):
                for pc in range(2):
                    if px + py + pc:
                        peer = (1 - x if px else x, 1 - y if py else y, 1 - c if pc else c)
                        _remote(mine, mine, send_sems.at[k], recv_sems.at[k], peer).start()
                        k += 1

    return _split_start("small_exchange_start", [slots], N_DEV - 1, issue, after)


def _small_exchange_wait(started, after):
    def waits(refs, send_sems, recv_sems):
        slot = refs[0].at[0]
        for k in range(N_DEV - 1):
            _wait_both(slot, slot, send_sems.at[k], recv_sems.at[k])

    return _split_wait("small_exchange_wait", started, waits, after)[0]


def _adam_math(w, g, m, v):
    m = ADAM_B1 * m + (1.0 - ADAM_B1) * g
    v = ADAM_B2 * v + (1.0 - ADAM_B2) * (g * g)
    m_hat = m / (1.0 - ADAM_B1 ** ADAM_STEP)
    v_hat = v / (1.0 - ADAM_B2 ** ADAM_STEP)
    delta = -ADAM_LR * (m_hat / (jnp.sqrt(v_hat) + ADAM_EPS) + ADAM_WD * w)
    return delta, m, v


def _adamw(w, g, m, v, name):
    rows, cols = w.shape[0], w.shape[-1]
    fits = [t for t in range(SUBLANES, rows, SUBLANES) if rows % t == 0 and t * cols * 4 <= (3 << 19)]
    tr = max(fits) if fits and w.ndim == 2 else rows

    def body(w_ref, g_ref, m_ref, v_ref, d_ref, nm_ref, nv_ref, go_ref):
        g = g_ref[...]
        d, nm, nv = _adam_math(w_ref[...], g, m_ref[...], v_ref[...])
        d_ref[...] = d
        nm_ref[...] = nm
        nv_ref[...] = nv
        go_ref[...] = g

    spec = pl.BlockSpec((tr,) + w.shape[1:], lambda i: (i,) + (0,) * (w.ndim - 1))
    return pl.pallas_call(
        body, name=name, grid=(rows // tr,), in_specs=[spec] * 4, out_specs=[spec] * 4,
        out_shape=[_sds(w.shape, F32)] * 4, compiler_params=_cp(("arbitrary",)),
    )(*_hbm(w, g, m, v))


def _small_sum_adamw(gathered, w, m, v):
    names = _SMALL_IN_CALL
    n = len(names)

    def body(*refs):
        a_ref = refs[0]
        w_refs, m_refs, v_refs = refs[1:1 + n], refs[1 + n:1 + 2 * n], refs[1 + 2 * n:1 + 3 * n]
        sum_ref, loss_ref = refs[1 + 3 * n], refs[2 + 3 * n]
        outs = refs[3 + 3 * n:]
        g = a_ref[0]
        for k in range(1, N_DEV):
            g = g + a_ref[k]
        sum_ref[...] = g
        loss_ref[...] = g[0:1, 0:1]
        for i, name in enumerate(names):
            if name == "rel_bias":
                r0, c0 = _REL_BIAS_AT
                pieces = [(slice(None), g[r0:r0 + N_HEADS, c0:c0 + N_BUCKETS])]
            elif name == "b_conv":
                r0 = _SMALL_AT[name][0]
                pieces = [(slice(None), jnp.concatenate([g[r0 + k:r0 + k + 1, :] for k in range(CONV_ROWS)], axis=1)[:, :2 * D_FF])]
            elif name == "w_s":
                r0, nr, _ = _SMALL_AT[name]
                pieces = [(slice(nr * j, nr * (j + 1)), g[r0:r0 + nr, CHUNK * j:CHUNK * (j + 1)]) for j in range(D_MODEL // CHUNK)]
            else:
                r0, nr, nc = _SMALL_AT[name]
                pieces = [(slice(None), g[r0:r0 + nr, 0:nc])]
            for at, gp in pieces:
                d, nm, nv = _adam_math(w_refs[i][at], gp, m_refs[i][at], v_refs[i][at])
                for k, val in enumerate((gp, d, nm, nv)):
                    outs[4 * i + k][at] = val

    shapes = [w[k].shape for k in names]
    res = pl.pallas_call(
        body, name="small_sum_adamw",
        out_shape=[_sds((SMALL_ROWS, D_MODEL), F32), _sds((1, 1), F32)] + [_sds(s, F32) for s in shapes for _ in range(4)],
    )(gathered, *[w[k] for k in names], *[m[k] for k in names], *[v[k] for k in names])
    return res[0], res[1], {k: tuple(res[2 + 4 * i:6 + 4 * i]) for i, k in enumerate(names)}


_NAMES = ("g_mix", "w_in", "g_sgu", "w_s", "b_s", "sinks", "rel_bias", "w_pa", "w_pb", "w_out",
          "g_ffn", "w_up", "w_conv", "b_conv", "w_down", "g_final")

def kernel(x, g_mix, w_in, g_sgu, w_s, b_s, sinks, rel_bias, w_pa, w_pb, w_out, g_ffn, w_up, w_conv, b_conv, w_down, g_final, loss_target, m_g_mix, m_w_in, m_g_sgu, m_w_s, m_b_s, m_sinks, m_rel_bias, m_w_pa, m_w_pb, m_w_out, m_g_ffn, m_w_up, m_w_conv, m_b_conv, m_w_down, m_g_final, v_g_mix, v_w_in, v_g_sgu, v_w_s, v_b_s, v_sinks, v_rel_bias, v_w_pa, v_w_pb, v_w_out, v_g_ffn, v_w_up, v_w_conv, v_b_conv, v_w_down, v_g_final):
    w = dict(g_mix=g_mix, w_in=w_in, g_sgu=g_sgu, w_s=w_s, b_s=b_s, sinks=sinks, rel_bias=rel_bias, w_pa=w_pa, w_pb=w_pb,
             w_out=w_out, g_ffn=g_ffn, w_up=w_up, w_conv=w_conv, b_conv=b_conv, w_down=w_down, g_final=g_final)
    m = dict(g_mix=m_g_mix, w_in=m_w_in, g_sgu=m_g_sgu, w_s=m_w_s, b_s=m_b_s, sinks=m_sinks, rel_bias=m_rel_bias, w_pa=m_w_pa,
             w_pb=m_w_pb, w_out=m_w_out, g_ffn=m_g_ffn, w_up=m_w_up, w_conv=m_w_conv, b_conv=m_b_conv, w_down=m_w_down,
             g_final=m_g_final)
    v = dict(g_mix=v_g_mix, w_in=v_w_in, g_sgu=v_g_sgu, w_s=v_w_s, b_s=v_b_s, sinks=v_sinks, rel_bias=v_rel_bias, w_pa=v_w_pa,
             w_pb=v_w_pb, w_out=v_w_out, g_ffn=v_g_ffn, w_up=v_w_up, w_conv=v_w_conv, b_conv=v_b_conv, w_down=v_w_down,
             g_final=v_g_final)
    xi, yi, ci = _mesh_pos()
    me = 2 * xi + yi

    shard = {n: w[n][0] for n in _BIG}
    shard_shapes = {n: shard[n].shape for n in _BIG}
    wc_shard = w["w_conv"][0]
    wc_pad = jnp.pad(wc_shard, ((0, 5), (0, 0)))
    own = {n: _own_slot(shard[n].astype(BF16), N_CHIPS, me) for n in _BIG if n != "w_in"}
    own["w_in"] = _own_slot(shard["w_in"].T.astype(BF16), N_CHIPS, me)
    stacks, wc_all = _allgather_weights({"w_in": own["w_in"]}, _own_slot(wc_pad, N_CHIPS, me))
    proj_gather = _allgather_start({n: own[n] for n in _MIXER[1:]}, "proj", stacks["w_in"])
    ffn_gather = _allgather_start({n: own[n] for n in _FFN}, "ffn", proj_gather[1][-1])
    w_conv_full = jnp.concatenate([wc_all[i, :3] for i in range(N_CHIPS)], axis=1)
    w_in_full = stacks["w_in"].reshape(N_CHIPS * IN_SHARD, D_MODEL)
    pos = jnp.stack([me, ci])

    def proj_weights(done):
        st = _allgather_forward(_allgather_wait(proj_gather, "proj", done), "proj")
        return st["w_pa"], st["w_pb"], st["w_out"].reshape(D_MODEL, D_MODEL)

    def ffn_weights(done):
        st = _allgather_forward(_allgather_wait(ffn_gather, "ffn", done), "ffn")
        return st["w_up"], st["w_down"].reshape(D_FF, D_MODEL)

    groups = {}

    def stage1(group, parts):
        groups[group] = dict(parts=parts, pair=_pair_exchange_start(parts, group, None))
        return groups[group]["pair"][1][-1]

    def stage2(group, after, order_after):
        g = groups[group]
        g["parts"], g["sib"] = _pair_exchange_wait(g["pair"], group, after)
        if group == "in":
            sums = {n: _pair_add(g["parts"][n], g["sib"][n], n, pos) for n in g["parts"]}
        else:
            sums = _pair_add_group(g["parts"], g["sib"], group, pos)
        g["chip"] = _chip_exchange_start(sums, group, order_after)
        return g["chip"][1][-1]

    def stage3(group, after, order_after):
        g = groups[group]
        got = _chip_exchange_wait(g["chip"], group, after)
        if group == "in":
            owned = {n: _owner_sum(g["parts"][n], g["sib"][n], got[n], n, pos, shard_shapes[n]) for n in g["parts"]}
        else:
            owned = _owner_sum_group(g["parts"], g["sib"], got, group, pos, shard_shapes)
        g["share"] = _pair_share_start(owned, group, order_after)
        return g["share"][1][-1]

    grads, deltas, new_m, new_v = {}, {}, {}, {}

    def stage4(group, after):
        g_shard = _pair_share_wait(groups[group]["share"], group, after)
        last = None
        for n in g_shard:
            g = _tie(g_shard[n], last)
            if n == "w_in":
                d, nm, nv, gt = _adamw(shard[n].T, g.T, m[n][0].T, v[n][0].T, "adamw_" + n)
                grads[n], deltas[n], new_m[n], new_v[n] = gt.T[None], d.T[None], nm.T[None], nv.T[None]
            else:
                d, nm, nv, go = _adamw(shard[n], g, m[n][0], v[n][0], "adamw_" + n)
                grads[n], deltas[n], new_m[n], new_v[n] = go[None], d[None], nm[None], nv[None]
            last = nv
        return last

    def on_grads(group, parts):
        token = stage1(group, parts)
        some = next(iter(parts.values()))
        if group == "proj":
            token = stage2("ffn", some, token)
        if group == "in":
            token = stage2("proj", some, token)
            token = stage3("ffn", some, token)
            token = stage2("in", token, token)
        return token

    loss, grad_x, small, big = _local_step(
        x, loss_target, w["g_mix"], w["g_sgu"], w["w_s"][0], w["b_s"][0], w["sinks"], w["rel_bias"], w["g_ffn"],
        w["b_conv"], w["g_final"], w_in_full, w_conv_full, proj_weights, ffn_weights, on_grads, ffn_gather[1][-1])

    small["loss"] = loss
    small_gather = _small_exchange_start(_own_slot(_pack_small(small), N_DEV, 2 * me + ci), grad_x)
    token = stage3("proj", grad_x, small_gather[-1])
    done = stage4("ffn", token)
    done = stage4("proj", done)
    token = stage3("in", done, None)
    all_small = _small_exchange_wait(small_gather, token)
    two_d = {n: (lambda a, n=n: a.reshape(_SMALL_AT[n][1:])) for n in _SMALL_IN_CALL}
    two_d["rel_bias"] = lambda a: a.T
    two_d["b_conv"] = lambda a: a
    two_d["w_s"] = lambda a: a.reshape(A_GROUPS * CHUNK, CHUNK)
    s_sum, s_loss, s_out = _small_sum_adamw(all_small, *[{n: two_d[n](p[n]) for n in _SMALL_IN_CALL} for p in (w, m, v)])
    stage4("in", s_sum)
    for n in _SMALL_IN_CALL:
        back = (lambda a: a.T) if n == "rel_bias" else (lambda a, n=n: a.reshape(w[n].shape))
        grads[n], deltas[n], new_m[n], new_v[n] = [back(a) for a in s_out[n]]

    def rows(n):
        r0, nr, _ = _SMALL_AT[n]
        return s_sum[r0:r0 + nr]

    wcols = wc_shard.shape[1]
    g_wc = lax.dynamic_slice(_unwide(rows("w_conv"), 3), (0, me * wcols), (3, wcols))
    taps = lambda a: a.transpose(1, 0, 2)
    res = _adamw(taps(w["w_conv"]), g_wc[:, None, :], taps(m["w_conv"]), taps(v["w_conv"]), "adamw_w_conv")
    deltas["w_conv"], new_m["w_conv"], new_v["w_conv"], grads["w_conv"] = [taps(a) for a in res]

    return (s_loss.reshape(()), grad_x, *[grads[n] for n in _NAMES], *[deltas[n] for n in _NAMES],
            *[new_m[n] for n in _NAMES], *[new_v[n] for n in _NAMES])
```

```python
import functools

import numpy as np
import jax
import jax.numpy as jnp
from jax import lax
from jax.experimental import pallas as pl
from jax.experimental.pallas import tpu as pltpu

F32 = jnp.float32
BF16 = jnp.bfloat16

D_MODEL = 1024
CHUNK = 128
A_GROUPS = 4
A_WIDTH = 512
N_HEADS = 8
HEAD_DIM = 64
Q_DIM = 512
KV_DIM = 128
N_BUCKETS = 32
MAX_DISTANCE = 128
D_FF = 2816
EPS = 1e-6
NEG_INF = -1e30
G_DIM = 2 * D_MODEL
A_DIM = 2 * A_WIDTH
B_DIM = Q_DIM + 2 * KV_DIM
LANES = 128
SUBLANES = 8
ROW_TILE = 512
WIDE_ROW_TILE = 256
COL_CHUNK = 512
GRAD_ROW_TILE = 512
STREAM_ROWS = 256
BF16_ROWS = 16
N_CHIPS = 4
N_DEV = 8

ADAM_LR = 0.001
ADAM_B1 = 0.9
ADAM_B2 = 0.999
ADAM_EPS = 1e-08
ADAM_WD = 0.01
ADAM_STEP = 10

MESH = pl.DeviceIdType.MESH
_GELU_C = 0.7978845608028654
_GELU_A = 0.044715


def _cp(sem=None, vmem_mb=None):
    kw = {}
    if sem is not None:
        kw["dimension_semantics"] = sem
    if vmem_mb is not None:
        kw["vmem_limit_bytes"] = vmem_mb << 20
    return pltpu.CompilerParams(**kw)


def _dot(a, b):
    return jnp.dot(a, b, preferred_element_type=F32)


def _dot_nt(a, b):
    return lax.dot_general(a, b, (((1,), (1,)), ((), ())), preferred_element_type=F32)


def _dot_tn(a, b):
    return lax.dot_general(a, b, (((0,), (0,)), ((), ())), preferred_element_type=F32)


def _rms_r(x):
    return lax.rsqrt(jnp.mean(x * x, axis=-1, keepdims=True) + EPS)


def _rms_bwd(dh, n, r, g):
    dn = dh * g
    return r * (dn - n * jnp.mean(dn * n, axis=-1, keepdims=True))


def _gelu(x):
    t = jnp.tanh(_GELU_C * (x + _GELU_A * (x * x * x)))
    return 0.5 * x * (1.0 + t), t


def _gelu_grad(x, t):
    return 0.5 * (1.0 + t) + 0.5 * x * (1.0 - t * t) * (_GELU_C * (1.0 + 3.0 * _GELU_A * x * x))


def _sigmoid(x):
    return 1.0 / (1.0 + jnp.exp(-x))


def _tie(x, dep):
    return x if dep is None else lax.optimization_barrier((x, dep))[0]


def _row(tm, w):
    return pl.BlockSpec((tm, w), lambda i: (i, 0))


def _full(shape):
    nd = len(shape)
    return pl.BlockSpec(tuple(shape), lambda *_: (0,) * nd)


def _resident(shape):
    nd = len(shape)
    return pl.BlockSpec(tuple(shape), lambda *_: (0,) * nd, pipeline_mode=pl.Buffered(1))


def _sds(shape, dtype):
    return pltpu.HBM(tuple(shape), dtype)


def _hbm(*arrays):
    return [pltpu.with_memory_space_constraint(a, pltpu.HBM) for a in arrays]


HBM = pl.BlockSpec(memory_space=pltpu.HBM)
ANY = pl.BlockSpec(memory_space=pl.ANY)
SEM = pl.BlockSpec(memory_space=pltpu.SEMAPHORE)


def _band_buckets():
    i = np.arange(CHUNK)[:, None]
    j = np.arange(2 * CHUNK)[None, :]
    dist = i + CHUNK - j
    valid = (dist >= 0) & (dist < CHUNK)
    d = np.clip(dist, 0, None)
    max_exact = N_BUCKETS // 2
    large = max_exact + (np.log(np.maximum(d, 1) / max_exact) / np.log(MAX_DISTANCE / max_exact)
                         * (N_BUCKETS - max_exact)).astype(np.int32)
    large = np.minimum(large, N_BUCKETS - 1)
    buckets = np.where(d < max_exact, d, large).astype(np.int32)
    return np.where(valid, buckets, -1).astype(np.int32)


_A_COLS = slice(0, A_DIM)
_B_COLS = slice(A_DIM, A_DIM + B_DIM)
_G_COLS = slice(A_DIM + B_DIM, A_DIM + B_DIM + G_DIM)


def _inproj(x2, g_mix, w_in, g_sgu, w_s, b_st, tm, after=None):
    T = x2.shape[0]
    order = [] if after is None else [after]

    def body(*refs):
        x_ref, g_ref, w_ref, gs_ref, ws_ref, bs_ref = refs[:6]
        pg_ref, pa_ref, pb_ref, h_ref, ya_ref = refs[6 + len(order):]
        x = x_ref[...]
        h = (x * _rms_r(x) * g_ref[...]).astype(BF16)
        h_ref[...] = h
        pa = _dot_nt(h, w_ref[_A_COLS, :]).astype(BF16)
        pa_ref[...] = pa
        pb_ref[...] = _dot_nt(h, w_ref[_B_COLS, :]).astype(BF16)
        pg_ref[...] = _dot_nt(h, w_ref[_G_COLS, :]).astype(BF16)
        _sgu_apply(pa.astype(F32), gs_ref[...], ws_ref, bs_ref, ya_ref)

    return pl.pallas_call(
        body, name="inproj", grid=(T // tm,),
        in_specs=[_row(tm, D_MODEL), _full(g_mix.shape), _resident(w_in.shape), _full(g_sgu.shape), _full(w_s.shape),
                  _full(b_st.shape)] + [ANY] * len(order),
        out_specs=[_row(tm, G_DIM), _row(tm, A_DIM), _row(tm, B_DIM), _row(tm, D_MODEL), _row(tm, A_WIDTH)],
        out_shape=[_sds((T, G_DIM), BF16), _sds((T, A_DIM), BF16), _sds((T, B_DIM), BF16), _sds((T, D_MODEL), BF16),
                   _sds((T, A_WIDTH), BF16)],
        compiler_params=_cp(("arbitrary",), 48),
    )(*_hbm(x2, g_mix, w_in, g_sgu, w_s, b_st), *order)


def _sgu_parts(p, g):
    pu = p[:, :A_WIDTH]
    pv = p[:, A_WIDTH:]
    u, tu = _gelu(pu)
    vv, tv = _gelu(pv)
    rv = _rms_r(vv)
    vn = (vv * rv * g).astype(BF16)
    return pu, pv, u, tu, vv, tv, rv, vn


def _tril():
    r = lax.broadcasted_iota(jnp.int32, (CHUNK, CHUNK), 0)
    c = lax.broadcasted_iota(jnp.int32, (CHUNK, CHUNK), 1)
    return r >= c


def _sgu_apply(p, g, ws_ref, bs_ref, y_ref):
    tril = _tril()
    _, _, u, _, _, _, _, vn = _sgu_parts(p, g)
    for gi in range(A_GROUPS):
        wm = jnp.where(tril, ws_ref[gi], 0.0).astype(BF16)
        bcol = bs_ref[:, gi:gi + 1]
        cs = slice(gi * CHUNK, (gi + 1) * CHUNK)
        for c in range(p.shape[0] // CHUNK):
            rs = slice(c * CHUNK, (c + 1) * CHUNK)
            s = _dot(wm, vn[rs, cs]) + bcol
            y_ref[rs, cs] = (u[rs, cs] * s).astype(BF16)


HEAD_ROWS = N_HEADS * CHUNK


def _head_rows(h):
    return slice(h * CHUNK, (h + 1) * CHUNK)


def _attn_setup(bias_scr, sink_scr, kvar_scr, qkv_ref, bk_ref, rel_ref, sink_ref):
    @pl.when(pl.program_id(0) == 0)
    def _():
        bk = bk_ref[...]
        for h in range(N_HEADS):
            acc = jnp.full((CHUNK, 2 * CHUNK), NEG_INF, F32)
            for b in range(N_BUCKETS):
                acc = jnp.where(bk == b, rel_ref[b, h], acc)
            bias_scr[_head_rows(h), :] = acc
            sink_scr[_head_rows(h), :] = jnp.full((CHUNK, LANES), sink_ref[0, h], F32)

    seq = qkv_ref.shape[0]
    rows_per = 2 * CHUNK
    for is_v in range(2):
        c0 = Q_DIM + is_v * KV_DIM
        for r in range(seq // rows_per):
            rs = slice(r * rows_per, (r + 1) * rows_per)
            a = qkv_ref[rs, c0:c0 + KV_DIM].astype(F32)
            lane = lax.broadcasted_iota(jnp.int32, a.shape, 1)
            lo = jnp.where(lane < HEAD_DIM, a, 0.0)
            hi = jnp.where(lane >= HEAD_DIM, a, 0.0)
            kvar_scr[4 * is_v + 0, rs, :] = lo.astype(BF16)
            kvar_scr[4 * is_v + 1, rs, :] = pltpu.roll(lo, HEAD_DIM, 1).astype(BF16)
            kvar_scr[4 * is_v + 2, rs, :] = pltpu.roll(hi, HEAD_DIM, 1).astype(BF16)
            kvar_scr[4 * is_v + 3, rs, :] = hi.astype(BF16)


def _rowsum(a, ones):
    hi = a.astype(BF16)
    lo = (a - hi.astype(F32)).astype(BF16)
    return _dot(hi, ones) + _dot(lo, ones)


def _both(a):
    return jnp.concatenate([a, a], axis=1)


def _attn_probs(qkv_ref, r0, n, kv, bias_scr, sink_scr, ones):
    s = jnp.concatenate([_dot_nt(qkv_ref[pl.ds(r0, CHUNK), (h // 2) * LANES:(h // 2 + 1) * LANES], kv[h // 4][h % 2])
                         for h in range(N_HEADS)], axis=0)
    s = s * (HEAD_DIM ** -0.5) + bias_scr[...]
    col = lax.broadcasted_iota(jnp.int32, s.shape, 1)
    s = jnp.where((col < CHUNK) & (n == 0), NEG_INF, s)
    sink = sink_scr[...]
    m = jnp.maximum(jnp.max(s, axis=-1, keepdims=True), sink)
    p = jnp.exp(s - _both(m))
    es = jnp.exp(sink - m)
    inv = 1.0 / (_dot(p.astype(BF16), ones) + es)
    return p * _both(inv), es * inv


def _attn_block_inputs(kvar_scr, n):
    r0 = pl.multiple_of(n * CHUNK, CHUNK)
    rp = pl.multiple_of(jnp.maximum(n - 1, 0) * CHUNK, CHUNK)

    def both(idx):
        return jnp.concatenate([kvar_scr[idx, pl.ds(rp, CHUNK), :], kvar_scr[idx, pl.ds(r0, CHUNK), :]], axis=0)

    kv = ((both(0), both(1)), (both(2), both(3)))
    vv = ((both(4), both(5)), (both(6), both(7)))
    return r0, kv, vv


def _attn_fwd(proj_b, sinks, rel_bias, n_seq, seq):
    nb = seq // CHUNK
    bk = jnp.asarray(_band_buckets())

    def body(qkv_ref, bk_ref, rel_ref, sink_ref, o_ref, bias_scr, sink_scr, kvar_scr):
        _attn_setup(bias_scr, sink_scr, kvar_scr, qkv_ref, bk_ref, rel_ref, sink_ref)
        ones = jnp.ones((2 * CHUNK, LANES), BF16)

        def blk(n, carry):
            r0, kv, vv = _attn_block_inputs(kvar_scr, n)
            prob, _ = _attn_probs(qkv_ref, r0, n, kv, bias_scr, sink_scr, ones)
            pb = prob.astype(BF16)
            for pr in range(N_HEADS // 2):
                acc = _dot(pb[_head_rows(2 * pr)], vv[pr // 2][0]) + _dot(pb[_head_rows(2 * pr + 1)], vv[pr // 2][1])
                o_ref[pl.ds(r0, CHUNK), pr * LANES:(pr + 1) * LANES] = acc.astype(BF16)
            return carry

        lax.fori_loop(0, nb, blk, 0)

    smem = pl.BlockSpec(memory_space=pltpu.SMEM)
    return pl.pallas_call(
        body, name="attn_fwd", grid=(n_seq,),
        in_specs=[_row(seq, B_DIM), _full(bk.shape), smem, smem],
        out_specs=_row(seq, Q_DIM), out_shape=_sds((n_seq * seq, Q_DIM), BF16),
        scratch_shapes=[pltpu.VMEM((HEAD_ROWS, 2 * CHUNK), F32), pltpu.VMEM((HEAD_ROWS, LANES), F32),
                        pltpu.VMEM((8, seq, KV_DIM), BF16)],
        compiler_params=_cp(("arbitrary",), 40),
    )(*_hbm(proj_b, bk), rel_bias, sinks)


def _dot_stacked(a, w_ref):
    return jnp.concatenate([_dot(a, w_ref[i]) for i in range(N_CHIPS)], axis=1)


def _dot_nt_stacked(a, w_ref):
    w = w_ref.shape[2]
    acc = _dot_nt(a[:, :w], w_ref[0])
    for i in range(1, N_CHIPS):
        acc = acc + _dot_nt(a[:, i * w:(i + 1) * w], w_ref[i])
    return acc


def _merge_fwd(x2, y_a, y_b, proj_g, w_pa, w_pb, w_out, tm):
    T = x2.shape[0]

    def body(x_ref, ya_ref, yb_ref, g_ref, wpa_ref, wpb_ref, wo_ref, x1_ref, mg_ref):
        g = g_ref[...].astype(F32)
        pa = _dot_stacked(ya_ref[...], wpa_ref)
        pb = _dot_stacked(yb_ref[...], wpb_ref)
        merged = (_sigmoid(g[:, :D_MODEL]) * pa + _sigmoid(g[:, D_MODEL:]) * pb).astype(BF16)
        mg_ref[...] = merged
        x1_ref[...] = x_ref[...] + _dot(merged, wo_ref[...])

    return pl.pallas_call(
        body, name="merge_fwd", grid=(T // tm,),
        in_specs=[_row(tm, D_MODEL), _row(tm, A_WIDTH), _row(tm, Q_DIM), _row(tm, G_DIM),
                  _resident(w_pa.shape), _resident(w_pb.shape), _resident(w_out.shape)],
        out_specs=[_row(tm, D_MODEL), _row(tm, D_MODEL)],
        out_shape=[_sds((T, D_MODEL), F32), _sds((T, D_MODEL), BF16)],
        compiler_params=_cp(("arbitrary",), 40),
    )(*_hbm(x2, y_a, y_b, proj_g, w_pa, w_pb, w_out))


def _upproj(x1, g_ffn, w_up, w_conv, b_conv, tm, seq):
    T = x1.shape[0]
    cw = w_up.shape[2]
    tiles_per_seq = seq // tm

    def body(x_ref, g_ref, w_ref, wc_ref, bc_ref, u_ref, h_ref, gate_ref, val_ref, tail_scr):
        at_start = (pl.program_id(0) % tiles_per_seq) == 0
        x = x_ref[...]
        h = (x * _rms_r(x) * g_ref[...]).astype(BF16)
        h_ref[...] = h
        for i in range(N_CHIPS):
            cs = slice(i * cw, (i + 1) * cw)
            u = _dot(h, w_ref[i])
            u_ref[:, cs] = u.astype(BF16)
            hl = jnp.where(at_start, 0.0, tail_scr[SUBLANES - 2:SUBLANES, cs])
            tail_scr[:, cs] = u[tm - SUBLANES:]
            up = _conv_out((u, _shift_down(u, hl, 1), _shift_down(u, hl, 2)), wc_ref[:, cs], bc_ref[:, cs])
            out_ref = gate_ref if i < N_CHIPS // 2 else val_ref
            out_ref[:, (i % 2) * cw:(i % 2 + 1) * cw] = up.astype(BF16)

    return pl.pallas_call(
        body, name="upproj", grid=(T // tm,),
        in_specs=[_row(tm, D_MODEL), _full(g_ffn.shape), _resident(w_up.shape), _full(w_conv.shape), _full(b_conv.shape)],
        out_specs=[_row(tm, 2 * D_FF), _row(tm, D_MODEL), _row(tm, D_FF), _row(tm, D_FF)],
        out_shape=[_sds((T, 2 * D_FF), BF16), _sds((T, D_MODEL), BF16), _sds((T, D_FF), BF16), _sds((T, D_FF), BF16)],
        scratch_shapes=[pltpu.VMEM((SUBLANES, 2 * D_FF), F32)],
        compiler_params=_cp(("arbitrary",), 56),
    )(*_hbm(x1, g_ffn, w_up, w_conv, b_conv))


def _shift_down(u, halo, k):
    rolled = pltpu.roll(u, k, 0)
    head = rolled[:SUBLANES]
    row = lax.broadcasted_iota(jnp.int32, head.shape, 0)
    if k == 1:
        head = jnp.where(row == 0, halo[1:2], head)
    else:
        head = jnp.where(row == 0, halo[0:1], jnp.where(row == 1, halo[1:2], head))
    return jnp.concatenate([head, rolled[SUBLANES:]], axis=0)


def _shift_up(d, halo, k):
    tm = d.shape[0]
    rolled = pltpu.roll(d, tm - k, 0)
    tail = rolled[tm - SUBLANES:]
    row = lax.broadcasted_iota(jnp.int32, tail.shape, 0)
    if k == 1:
        tail = jnp.where(row == SUBLANES - 1, halo[0:1], tail)
    else:
        tail = jnp.where(row == SUBLANES - 2, halo[0:1], jnp.where(row == SUBLANES - 1, halo[1:2], tail))
    return jnp.concatenate([rolled[:tm - SUBLANES], tail], axis=0)


def _conv_out(taps, wc, bc):
    u, u1, u2 = taps
    return wc[0:1] * u2 + wc[1:2] * u1 + wc[2:3] * u + bc


def _ffn_down_loss(gate, val, x1, target, w_down, g_final, tm):
    T = x1.shape[0]
    half = D_FF // 2

    sub = min(tm, 128)

    def body(gt_ref, vl_ref, x1_ref, t_ref, wd_ref, g_ref, dx2_ref, loss_ref, gg_ref):
        i = pl.program_id(0)
        g = g_ref[...]

        def down(rs):
            acc = jnp.zeros((sub, D_MODEL), F32)
            for j in range(2):
                gc = slice(j * half, (j + 1) * half)
                gate = gt_ref[rs, gc].astype(F32)
                act = (gate * _sigmoid(gate) * vl_ref[rs, gc].astype(F32)).astype(BF16)
                acc = acc + _dot(act, wd_ref[gc, :])
            return acc

        def norm_loss(rs, acc):
            x2 = x1_ref[rs, :] + acc
            r = _rms_r(x2)
            n = x2 * r
            diff = n * g - t_ref[rs, :]
            dy = diff * (1.0 / D_MODEL)
            dx2_ref[rs, :] = _rms_bwd(dy, n, r, g)
            return (jnp.sum(jnp.mean(diff * diff, axis=-1, keepdims=True), axis=0, keepdims=True),
                    jnp.sum(dy * n, axis=0, keepdims=True))

        subs = [slice(s0, s0 + sub) for s0 in range(0, tm, sub)]
        accs = [down(rs) for rs in subs]
        parts = [norm_loss(rs, acc) for rs, acc in zip(subs, accs)]

        @pl.when(i == 0)
        def _():
            loss_ref[...] = jnp.zeros_like(loss_ref)
            gg_ref[...] = jnp.zeros_like(gg_ref)

        loss_ref[...] += 0.5 * sum(p[0] for p in parts)
        gg_ref[...] += sum(p[1] for p in parts)

    return pl.pallas_call(
        body, name="ffn_down_loss", grid=(T // tm,),
        in_specs=[_row(tm, D_FF), _row(tm, D_FF), _row(tm, D_MODEL), _row(tm, D_MODEL),
                  _resident(w_down.shape), _full(g_final.shape)],
        out_specs=[_row(tm, D_MODEL), _full((1, 1)), _full((1, D_MODEL))],
        out_shape=[_sds((T, D_MODEL), F32), _sds((1, 1), F32), _sds((1, D_MODEL), F32)],
        compiler_params=_cp(("arbitrary",), 48),
    )(*_hbm(gate, val, x1, target, w_down, g_final))


def _ffn_bwd_act(gate, val, dx2, w_down, tm):
    T = dx2.shape[0]
    half = D_FF // 2
    nt = T // tm

    def body(g_ref, v_ref, dx_ref, wd_ref, dg_ref, dv_ref, gwd_out, gbg_ref, gbv_ref, gwd_ref):
        i = pl.program_id(1)

        @pl.when(i == 0)
        def _():
            for r in (gwd_ref, gbg_ref, gbv_ref):
                r[...] = jnp.zeros_like(r)

        dx = dx_ref[...].astype(BF16)
        for c0 in range(0, half, COL_CHUNK):
            cs = slice(c0, min(c0 + COL_CHUNK, half))
            gate = g_ref[:, cs].astype(F32)
            val = v_ref[:, cs].astype(F32)
            sg = _sigmoid(gate)
            silu = gate * sg
            d_act = _dot_nt(dx, wd_ref[cs, :])
            d_val = d_act * silu
            d_gate = d_act * val * (sg * (1.0 + gate * (1.0 - sg)))
            dg_ref[:, cs] = d_gate.astype(BF16)
            dv_ref[:, cs] = d_val.astype(BF16)
            gwd_ref[cs, :] += _dot_tn((silu * val).astype(BF16), dx)
            gbg_ref[:, cs] += jnp.sum(d_gate, axis=0, keepdims=True)
            gbv_ref[:, cs] += jnp.sum(d_val, axis=0, keepdims=True)

        @pl.when(i == nt - 1)
        def _():
            gwd_out[...] = gwd_ref[...].astype(BF16)

    tile = pl.BlockSpec((tm, half), lambda j, i: (i, j))
    vec = pl.BlockSpec((1, half), lambda j, i: (0, j))
    wrows = pl.BlockSpec((half, D_MODEL), lambda j, i: (j, 0))
    return pl.pallas_call(
        body, name="ffn_bwd_act", grid=(2, nt),
        in_specs=[tile, tile, pl.BlockSpec((tm, D_MODEL), lambda j, i: (i, 0)), wrows],
        out_specs=[tile, tile, wrows, vec, vec],
        out_shape=[_sds((T, D_FF), BF16), _sds((T, D_FF), BF16), _sds((D_FF, D_MODEL), BF16),
                   _sds((1, D_FF), F32), _sds((1, D_FF), F32)],
        scratch_shapes=[pltpu.VMEM((half, D_MODEL), F32)],
        compiler_params=_cp(("arbitrary", "arbitrary"), 56),
    )(*_hbm(gate, val, dx2, w_down))


def _ffn_bwd_up(d_gate, d_val, upre, dx2, x1, g_ffn, w_conv, w_up, tm, seq):
    T = dx2.shape[0]
    tiles_per_seq = seq // tm
    k16 = tm // BF16_ROWS
    n16 = T // BF16_ROWS
    cw = D_FF // 2

    def body(dg_ref, dv_ref, hg_ref, hv_ref, u_ref, dx2_ref, x1_ref, g_ref, wc_ref, wu_ref, du_ref, dx1_ref, gg_ref, gwc_ref):
        i = pl.program_id(0)
        at_end = (i % tiles_per_seq) == tiles_per_seq - 1

        @pl.when(i == 0)
        def _():
            gg_ref[...] = jnp.zeros_like(gg_ref)
            gwc_ref[...] = jnp.zeros_like(gwc_ref)

        dh = jnp.zeros((tm, D_MODEL), F32)
        for j in range(4):
            src, hsrc = (dg_ref, hg_ref) if j < 2 else (dv_ref, hv_ref)
            ls = slice((j % 2) * cw, (j % 2 + 1) * cw)
            cs = slice(j * cw, (j + 1) * cw)
            d = src[:, ls].astype(F32)
            hl = hsrc[:, ls].astype(F32)[0:2]
            hl = jnp.where(at_end, 0.0, hl)
            wc = wc_ref[:, cs]
            d1 = _shift_up(d, hl, 1)
            d2 = _shift_up(d, hl, 2)
            du = (wc[2:3] * d + wc[1:2] * d1 + wc[0:1] * d2).astype(BF16)
            du_ref[:, cs] = du
            dh = dh + _dot_nt(du, wu_ref[j])
            u = u_ref[:, cs].astype(F32)
            gwc_ref[0:1, cs] += jnp.sum(d2 * u, axis=0, keepdims=True)
            gwc_ref[1:2, cs] += jnp.sum(d1 * u, axis=0, keepdims=True)
            gwc_ref[2:3, cs] += jnp.sum(d * u, axis=0, keepdims=True)
        x = x1_ref[...]
        r = _rms_r(x)
        n = x * r
        dx1_ref[...] = dx2_ref[...] + _rms_bwd(dh, n, r, g_ref[...])
        gg_ref[...] += jnp.sum(dh * n, axis=0, keepdims=True)

    nxt = pl.BlockSpec((BF16_ROWS, D_FF), lambda i: (jnp.minimum((i + 1) * k16, n16 - 1), 0))
    return pl.pallas_call(
        body, name="ffn_bwd_up", grid=(T // tm,),
        in_specs=[_row(tm, D_FF), _row(tm, D_FF), nxt, nxt, _row(tm, 2 * D_FF), _row(tm, D_MODEL), _row(tm, D_MODEL),
                  _full(g_ffn.shape), _full(w_conv.shape), _resident(w_up.shape)],
        out_specs=[_row(tm, 2 * D_FF), _row(tm, D_MODEL), _full((1, D_MODEL)), _full((3, 2 * D_FF))],
        out_shape=[_sds((T, 2 * D_FF), BF16), _sds((T, D_MODEL), F32), _sds((1, D_MODEL), F32), _sds((3, 2 * D_FF), F32)],
        compiler_params=_cp(("arbitrary",), 56),
    )(*_hbm(d_gate, d_val, d_gate, d_val, upre, dx2, x1, g_ffn, w_conv, w_up))


def _matmul_tn(a, b, tn, tk, name):
    T, M = a.shape
    N = b.shape[1]
    nk = T // tk

    def body(a_ref, b_ref, o_ref, acc_ref):
        k = pl.program_id(1)

        @pl.when(k == 0)
        def _():
            acc_ref[...] = jnp.zeros_like(acc_ref)

        acc_ref[...] += _dot_tn(a_ref[...], b_ref[...])

        @pl.when(k == nk - 1)
        def _():
            o_ref[...] = acc_ref[...].astype(BF16)

    return pl.pallas_call(
        body, name=name, grid=(N // tn, nk),
        in_specs=[pl.BlockSpec((tk, M), lambda j, k: (k, 0)), pl.BlockSpec((tk, tn), lambda j, k: (k, j))],
        out_specs=pl.BlockSpec((M, tn), lambda j, k: (0, j)), out_shape=_sds((M, N), BF16),
        scratch_shapes=[pltpu.VMEM((M, tn), F32)],
        compiler_params=_cp(("arbitrary", "arbitrary"), 48),
    )(*_hbm(a, b))


def _merge_bwd(dx1, merged, y_a, y_b, proj_g, proj_a, w_pa, w_pb, w_out, g_sgu, w_s, b_st, tm, after=None):
    T = dx1.shape[0]

    nt = T // tm
    pshape = (A_WIDTH, D_MODEL)
    order = [] if after is None else [after]

    def body(*refs):
        dx_ref, mg_ref, ya_ref, yb_ref, g_ref, p_ref, wpa_ref, wpb_ref, wo_ref, gs_ref, ws_ref, bs_ref = refs[:12]
        (dg_ref, da_ref, dyb_ref, gwo_out, gwpa_out, gwpb_out, gws_ref, gbs_ref, gg_ref,
         gwo_ref, gwpa_ref, gwpb_ref) = refs[12 + len(order):]
        i = pl.program_id(0)

        @pl.when(i == 0)
        def _():
            for r in (gwo_ref, gwpa_ref, gwpb_ref, gws_ref, gbs_ref, gg_ref):
                r[...] = jnp.zeros_like(r)

        dx = dx_ref[...].astype(BF16)
        dm = _dot_nt(dx, wo_ref[...])
        g = g_ref[...].astype(F32)
        ya = ya_ref[...]
        yb = yb_ref[...]
        pa = _dot_stacked(ya, wpa_ref)
        pb = _dot_stacked(yb, wpb_ref)
        sa = _sigmoid(g[:, :D_MODEL])
        sb = _sigmoid(g[:, D_MODEL:])
        dpa = (dm * sa).astype(BF16)
        dpb = (dm * sb).astype(BF16)
        dg_ref[:, :D_MODEL] = (dm * pa * (sa * (1.0 - sa))).astype(BF16)
        dg_ref[:, D_MODEL:] = (dm * pb * (sb * (1.0 - sb))).astype(BF16)
        d_ya = _dot_nt_stacked(dpa, wpa_ref).astype(BF16)
        dyb_ref[...] = _dot_nt_stacked(dpb, wpb_ref).astype(BF16)
        _sgu_bwd_apply(p_ref[...].astype(F32), d_ya.astype(F32), gs_ref[...], ws_ref, bs_ref, da_ref, gws_ref, gbs_ref, gg_ref)
        gwo_ref[...] += _dot_tn(mg_ref[...], dx)
        gwpa_ref[...] += _dot_tn(ya, dpa)
        gwpb_ref[...] += _dot_tn(yb, dpb)

        @pl.when(i == nt - 1)
        def _():
            gwo_out[...] = gwo_ref[...].astype(BF16)
            gwpa_out[...] = gwpa_ref[...].astype(BF16)
            gwpb_out[...] = gwpb_ref[...].astype(BF16)

    return pl.pallas_call(
        body, name="merge_bwd", grid=(nt,),
        in_specs=[_row(tm, D_MODEL), _row(tm, D_MODEL), _row(tm, A_WIDTH), _row(tm, Q_DIM), _row(tm, G_DIM), _row(tm, A_DIM),
                  _resident(w_pa.shape), _resident(w_pb.shape), _resident(w_out.shape),
                  _full(g_sgu.shape), _full(w_s.shape), _full(b_st.shape)] + [ANY] * len(order),
        out_specs=[_row(tm, G_DIM), _row(tm, A_DIM), _row(tm, Q_DIM),
                   _full(w_out.shape), _full(pshape), _full(pshape), _full(w_s.shape), _full(b_st.shape), _full(g_sgu.shape)],
        out_shape=[_sds((T, G_DIM), BF16), _sds((T, A_DIM), BF16), _sds((T, Q_DIM), BF16),
                   _sds(w_out.shape, BF16), _sds(pshape, BF16), _sds(pshape, BF16),
                   _sds(w_s.shape, F32), _sds(b_st.shape, F32), _sds(g_sgu.shape, F32)],
        scratch_shapes=[pltpu.VMEM(w_out.shape, F32), pltpu.VMEM(pshape, F32), pltpu.VMEM(pshape, F32)],
        compiler_params=_cp(("arbitrary",), 56),
    )(*_hbm(dx1, merged, y_a, y_b, proj_g, proj_a, w_pa, w_pb, w_out, g_sgu, w_s, b_st), *order)


def _sgu_bwd_apply(p, dy, g, ws_ref, bs_ref, dp_ref, gws_ref, gbs_ref, gg_ref):
    tril = _tril()
    pu, pv, u, tu, vv, tv, rv, vn = _sgu_parts(p, g)
    du_cols = []
    dvn_cols = []
    for gi in range(A_GROUPS):
        wm = jnp.where(tril, ws_ref[gi], 0.0).astype(BF16)
        wmt = wm.astype(F32).T.astype(BF16)
        bcol = bs_ref[:, gi:gi + 1]
        cs = slice(gi * CHUNK, (gi + 1) * CHUNK)
        du_rows = []
        dvn_rows = []
        gw = jnp.zeros((CHUNK, CHUNK), F32)
        gb = jnp.zeros((CHUNK, 1), F32)
        for c in range(p.shape[0] // CHUNK):
            rs = slice(c * CHUNK, (c + 1) * CHUNK)
            vn_c = vn[rs, cs]
            s = _dot(wm, vn_c) + bcol
            dy_c = dy[rs, cs]
            ds = dy_c * u[rs, cs]
            du_rows.append(dy_c * s)
            dsb = ds.astype(BF16)
            gw = gw + _dot_nt(dsb, vn_c)
            gb = gb + jnp.sum(ds, axis=-1, keepdims=True)
            dvn_rows.append(_dot(wmt, dsb))
        gws_ref[gi] += jnp.where(tril, gw, 0.0)
        gbs_ref[:, gi:gi + 1] += gb
        du_cols.append(jnp.concatenate(du_rows, axis=0))
        dvn_cols.append(jnp.concatenate(dvn_rows, axis=0))
    du = jnp.concatenate(du_cols, axis=1)
    dvn = jnp.concatenate(dvn_cols, axis=1)
    vhat = vv * rv
    gg_ref[...] += jnp.sum(dvn * vhat, axis=0, keepdims=True)
    dvv = _rms_bwd(dvn, vhat, rv, g)
    dp_ref[:, :A_WIDTH] = (du * _gelu_grad(pu, tu)).astype(BF16)
    dp_ref[:, A_WIDTH:] = (dvv * _gelu_grad(pv, tv)).astype(BF16)


def _attn_bwd(proj_b, d_yb, sinks, rel_bias, n_seq, seq, after=None):
    nb = seq // CHUNK
    bk = jnp.asarray(_band_buckets())
    order = [] if after is None else [after]

    def body(*refs):
        qkv_ref, do_ref, bk_ref, rel_ref, sink_ref = refs[:5]
        (d_ref, gs_ref, gr_ref, bias_scr, sink_scr, kvar_scr, dbias_scr, dk_scr, dv_scr, ds_scr) = refs[5 + len(order):]
        b = pl.program_id(0)
        _attn_setup(bias_scr, sink_scr, kvar_scr, qkv_ref, bk_ref, rel_ref, sink_ref)
        ones = jnp.ones((2 * CHUNK, LANES), BF16)

        @pl.when(b == 0)
        def _():
            dbias_scr[...] = jnp.zeros_like(dbias_scr)
            ds_scr[...] = jnp.zeros_like(ds_scr)

        dk_scr[...] = jnp.zeros_like(dk_scr)
        dv_scr[...] = jnp.zeros_like(dv_scr)

        def transposed(a):
            return a.astype(F32).T.astype(BF16)

        def blk(n, carry):
            r0, kv, vv = _attn_block_inputs(kvar_scr, n)
            prob, psink = _attn_probs(qkv_ref, r0, n, kv, bias_scr, sink_scr, ones)
            dp = jnp.concatenate([_dot_nt(do_ref[pl.ds(r0, CHUNK), (h // 2) * LANES:(h // 2 + 1) * LANES], vv[h // 4][h % 2])
                                  for h in range(N_HEADS)], axis=0)
            delta = _rowsum(prob * dp, ones)
            dsc = prob * (dp - _both(delta))
            ds_scr[...] += psink * delta
            dbias_scr[...] += dsc
            dsb = (dsc * (HEAD_DIM ** -0.5)).astype(BF16)
            pb = prob.astype(BF16)
            dkt = [jnp.zeros((HEAD_DIM, 2 * CHUNK), F32) for _ in range(2)]
            dvt = [jnp.zeros((HEAD_DIM, 2 * CHUNK), F32) for _ in range(2)]
            for pr in range(N_HEADS // 2):
                ps = slice(pr * LANES, (pr + 1) * LANES)
                qpt = transposed(qkv_ref[pl.ds(r0, CHUNK), ps])
                dopt = transposed(do_ref[pl.ds(r0, CHUNK), ps])
                kvh = pr // 2
                dq = jnp.zeros((CHUNK, LANES), F32)
                for hh in range(2):
                    hr = _head_rows(2 * pr + hh)
                    rows = slice(hh * HEAD_DIM, (hh + 1) * HEAD_DIM)
                    dq = dq + _dot(dsb[hr], kv[kvh][hh])
                    dkt[kvh] = dkt[kvh] + _dot(qpt, dsb[hr])[rows]
                    dvt[kvh] = dvt[kvh] + _dot(dopt, pb[hr])[rows]
                d_ref[pl.ds(r0, CHUNK), ps] = dq.astype(BF16)
            dk_scr[:, pl.ds(r0, 2 * CHUNK)] += jnp.concatenate(dkt, axis=0)
            dv_scr[:, pl.ds(r0, 2 * CHUNK)] += jnp.concatenate(dvt, axis=0)
            return carry

        lax.fori_loop(0, nb, blk, 0)
        for n in range(nb):
            rows = slice(n * CHUNK, (n + 1) * CHUNK)
            cols = slice((n + 1) * CHUNK, (n + 2) * CHUNK)
            d_ref[rows, Q_DIM:Q_DIM + KV_DIM] = dk_scr[:, cols].T.astype(BF16)
            d_ref[rows, Q_DIM + KV_DIM:] = dv_scr[:, cols].T.astype(BF16)

        @pl.when(b == n_seq - 1)
        def _():
            bkv = bk_ref[...]
            for h in range(N_HEADS):
                gs_ref[0:1, h:h + 1] = -jnp.sum(ds_scr[_head_rows(h), 0:1], axis=0, keepdims=True)
                db = dbias_scr[_head_rows(h), :]
                for bb in range(N_BUCKETS):
                    part = jnp.sum(jnp.where(bkv == bb, db, 0.0), axis=-1, keepdims=True)
                    gr_ref[bb:bb + 1, h:h + 1] = jnp.sum(part, axis=0, keepdims=True)

    smem = pl.BlockSpec(memory_space=pltpu.SMEM)
    return pl.pallas_call(
        body, name="attn_bwd", grid=(n_seq,),
        in_specs=[_row(seq, B_DIM), _row(seq, Q_DIM), _full(bk.shape), smem, smem] + [ANY] * len(order),
        out_specs=[_row(seq, B_DIM), _full((1, N_HEADS)), _full((N_BUCKETS, N_HEADS))],
        out_shape=[_sds((n_seq * seq, B_DIM), BF16), _sds((1, N_HEADS), F32), _sds((N_BUCKETS, N_HEADS), F32)],
        scratch_shapes=[pltpu.VMEM((HEAD_ROWS, 2 * CHUNK), F32), pltpu.VMEM((HEAD_ROWS, LANES), F32),
                        pltpu.VMEM((8, seq, KV_DIM), BF16), pltpu.VMEM((HEAD_ROWS, 2 * CHUNK), F32),
                        pltpu.VMEM((KV_DIM, seq + CHUNK), F32), pltpu.VMEM((KV_DIM, seq + CHUNK), F32),
                        pltpu.VMEM((HEAD_ROWS, LANES), F32)],
        compiler_params=_cp(("arbitrary",), 40),
    )(*_hbm(proj_b, d_yb, bk), rel_bias, sinks, *order)


def _inproj_bwd(d_g, d_a, d_b, x2, dx1, g_mix, w_in, tm, after=None):
    T = x2.shape[0]
    sub = min(tm, 128)
    order = [] if after is None else [after]

    def body(*refs):
        dg_ref, da_ref, db_ref, x_ref, dx1_ref, g_ref, w_ref = refs[:7]
        gx_ref, gg_ref = refs[7 + len(order):]
        subs = [slice(s0, s0 + sub) for s0 in range(0, tm, sub)]
        dhs = [_dot(dg_ref[rs, :], w_ref[_G_COLS, :]) + _dot(da_ref[rs, :], w_ref[_A_COLS, :])
               + _dot(db_ref[rs, :], w_ref[_B_COLS, :]) for rs in subs]
        gg = jnp.zeros((1, D_MODEL), F32)
        for rs, dh in zip(subs, dhs):
            x = x_ref[rs, :]
            r = _rms_r(x)
            n = x * r
            gx_ref[rs, :] = dx1_ref[rs, :] + _rms_bwd(dh, n, r, g_ref[...])
            gg = gg + jnp.sum(dh * n, axis=0, keepdims=True)

        @pl.when(pl.program_id(0) == 0)
        def _():
            gg_ref[...] = jnp.zeros_like(gg_ref)

        gg_ref[...] += gg

    return pl.pallas_call(
        body, name="inproj_bwd", grid=(T // tm,),
        in_specs=[_row(tm, G_DIM), _row(tm, A_DIM), _row(tm, B_DIM), _row(tm, D_MODEL), _row(tm, D_MODEL),
                  _full(g_mix.shape), _resident(w_in.shape)] + [ANY] * len(order),
        out_specs=[_row(tm, D_MODEL), _full((1, D_MODEL))],
        out_shape=[_sds((T, D_MODEL), F32), _sds((1, D_MODEL), F32)],
        compiler_params=_cp(("arbitrary",), 48),
    )(*_hbm(d_g, d_a, d_b, x2, dx1, g_mix, w_in), *order)


IN_SHARD = (A_DIM + B_DIM + G_DIM) // N_CHIPS


def _grad_w_in(h, d_a, d_b, d_g, tk):
    T = h.shape[0]
    nk = T // tk
    in_dim = N_CHIPS * IN_SHARD

    def body(h_ref, da_ref, db_ref, dg_ref, o_ref, acc_ref):
        k = pl.program_id(0)

        @pl.when(k == 0)
        def _():
            acc_ref[...] = jnp.zeros_like(acc_ref)

        hb = h_ref[...]
        acc_ref[:, _A_COLS] += _dot_tn(hb, da_ref[...])
        acc_ref[:, _B_COLS] += _dot_tn(hb, db_ref[...])
        acc_ref[:, _G_COLS] += _dot_tn(hb, dg_ref[...])

        @pl.when(k == nk - 1)
        def _():
            for i in range(N_CHIPS):
                o_ref[i] = acc_ref[:, i * IN_SHARD:(i + 1) * IN_SHARD].astype(BF16)

    return pl.pallas_call(
        body, name="grad_w_in", grid=(nk,),
        in_specs=[_row(tk, D_MODEL), _row(tk, A_DIM), _row(tk, B_DIM), _row(tk, G_DIM)],
        out_specs=_full((N_CHIPS, D_MODEL, IN_SHARD)), out_shape=_sds((N_CHIPS, D_MODEL, IN_SHARD), BF16),
        scratch_shapes=[pltpu.VMEM((D_MODEL, in_dim), F32)],
        compiler_params=_cp(("arbitrary",), 56),
    )(*_hbm(h, d_a, d_b, d_g))


def _local_step(x, target, g_mix, g_sgu, w_s, b_s, sinks, rel_bias, g_ffn, b_conv, g_final,
                w_in, w_conv, proj_weights, ffn_weights, on_grads, after=None):
    n_seq, seq, _ = x.shape
    T = n_seq * seq
    tm = min(ROW_TILE, seq)
    tw = min(GRAD_ROW_TILE, T)
    tf = min(WIDE_ROW_TILE, seq)
    x2 = x.reshape(T, D_MODEL)
    tgt = target.reshape(T, D_MODEL)
    b_st = b_s.T
    g_fin = g_final.reshape(1, D_MODEL)

    proj_g, proj_a, proj_b, h, y_a = _inproj(x2, g_mix, w_in, g_sgu, w_s, b_st, tm, after)
    y_b = _attn_fwd(proj_b, sinks, rel_bias, n_seq, seq)
    w_pa, w_pb, w_out = proj_weights(y_b)
    x1, merged = _merge_fwd(x2, y_a, y_b, proj_g, w_pa, w_pb, w_out, tm)
    w_up, w_down = ffn_weights(x1)
    upre, h2, gate, val = _upproj(x1, g_ffn, w_up, w_conv, b_conv, tf, seq)
    dx2, loss, gg_final = _ffn_down_loss(gate, val, x1, tgt, w_down, g_fin, tm)

    d_gate, d_val, gw_down, gb_g, gb_v = _ffn_bwd_act(gate, val, dx2, w_down, tw)
    gb_conv = jnp.concatenate([gb_g, gb_v], axis=1)
    d_upre, dx1, gg_ffn, gw_conv = _ffn_bwd_up(d_gate, d_val, upre, dx2, x1, g_ffn, w_conv, w_up, tf, seq)
    gw_up = _matmul_tn(h2, d_upre, 2 * D_FF // 4, min(4 * GRAD_ROW_TILE, T), "grad_w_up")
    sent = on_grads("ffn", dict(w_up=gw_up, w_down=gw_down))
    d_g, d_a, d_yb, gw_out, gw_pa, gw_pb, gw_s, gb_st, gg_sgu = _merge_bwd(
        dx1, merged, y_a, y_b, proj_g, proj_a, w_pa, w_pb, w_out, g_sgu, w_s, b_st, tw, sent)
    sent = on_grads("proj", dict(w_pa=gw_pa, w_pb=gw_pb, w_out=gw_out))
    d_b, g_sinks, g_rel = _attn_bwd(proj_b, d_yb, sinks, rel_bias, n_seq, seq, sent)
    gw_in = _grad_w_in(h, d_a, d_b, d_g, min(2 * GRAD_ROW_TILE, T))
    sent = on_grads("in", dict(w_in=gw_in))
    grad_x, gg_mix = _inproj_bwd(d_g, d_a, d_b, x2, dx1, g_mix, w_in, tm, sent)

    small = dict(g_mix=gg_mix, g_sgu=gg_sgu, w_s=gw_s, b_s=gb_st.T, sinks=g_sinks, rel_bias=g_rel,
                 g_ffn=gg_ffn, b_conv=gb_conv, g_final=gg_final, w_conv=gw_conv)
    big = dict(w_in=gw_in, w_pa=gw_pa, w_pb=gw_pb, w_out=gw_out, w_up=gw_up, w_down=gw_down)
    return loss, grad_x.reshape(x.shape), small, big


_MIXER = ("w_in", "w_pa", "w_pb", "w_out")
_FFN = ("w_up", "w_down")
_BIG = _MIXER + _FFN

CONV_ROWS = 6
_SMALL_AT = dict(loss=(0, 1, 1), g_sgu=(4, 1, A_WIDTH), sinks=(5, 1, N_HEADS), b_s=(8, A_GROUPS, CHUNK),
                 b_conv=(12, CONV_ROWS, D_MODEL), w_conv=(18, 3 * CONV_ROWS, D_MODEL),
                 g_final=(36, 1, D_MODEL), g_mix=(37, 1, D_MODEL), g_ffn=(38, 1, D_MODEL),
                 w_s=(40, A_GROUPS * CHUNK * CHUNK // D_MODEL, D_MODEL))
_REL_BIAS_AT = (0, A_WIDTH)
_SMALL_IN_CALL = ("g_final", "g_mix", "g_ffn", "g_sgu", "sinks", "b_s", "b_conv", "rel_bias", "w_s")
SMALL_ROWS = 104


def _pack_small(vals):
    def wide(a):
        return jnp.pad(a, ((0, 0), (0, CONV_ROWS * D_MODEL - a.shape[1]))).reshape(-1, D_MODEL)

    nr = _SMALL_AT["w_s"][1]
    w_s = vals["w_s"].reshape(D_MODEL // CHUNK, nr, CHUNK).transpose(1, 0, 2).reshape(nr, D_MODEL)
    laid = dict(vals, b_conv=wide(vals["b_conv"]), w_conv=wide(vals["w_conv"]), w_s=w_s)
    rows, at = [], 0
    for n, (r0, nr, nc) in _SMALL_AT.items():
        if r0 > at:
            rows.append(jnp.zeros((r0 - at, D_MODEL), F32))
        rows.append(jnp.pad(laid[n].astype(F32).reshape(nr, nc), ((0, 0), (0, D_MODEL - nc))))
        at = r0 + nr
    return lax.dynamic_update_slice(jnp.concatenate(rows, axis=0), vals["rel_bias"].T, _REL_BIAS_AT)


def _unwide(a, r):
    return a.reshape(r, CONV_ROWS * D_MODEL)[:, :2 * D_FF]


def _mesh_pos():
    return lax.axis_index("x"), lax.axis_index("y"), lax.axis_index("c")


def _other_chips(x, y):
    return [(1 - x, y), (x, 1 - y), (1 - x, 1 - y)]


def _remote(src, dst, send_sem, recv_sem, to):
    return pltpu.make_async_remote_copy(src_ref=src, dst_ref=dst, send_sem=send_sem, recv_sem=recv_sem,
                                        device_id=to, device_id_type=MESH)


def _own_slot(own, n, at):
    return lax.dynamic_update_slice(lax.empty((n,) + own.shape, own.dtype), own[None], (at,) + (0,) * own.ndim)


def _allgather_weights(stacks, wc_stack):
    names = list(stacks)
    n = len(names)

    def body(*refs):
        ins, outs = refs[:n + 1], refs[n + 1:2 * n + 2]
        send_sems, recv_sems = refs[2 * n + 2:]
        x, y, c = _mesh_pos()
        _handshake(_chip_peers(x, y, c) + _sibling_peers(x, y, c))
        me = 2 * x + y
        sibling = (x, y, 1 - c)
        chips = _other_chips(x, y)

        def half(ref, chip, hc):
            hr = ref.shape[1] // 2
            return ref.at[chip, pl.ds(hc * hr, hr), :]

        first = []
        for k in range(n):
            first += [_remote(half(ins[k], me, c), half(outs[k], me, c), send_sems.at[6 * k + j], recv_sems.at[6 * k + j], (cx, cy, c))
                      for j, (cx, cy) in enumerate(chips)]
        first += [_remote(ins[n].at[me], outs[n].at[me], send_sems.at[6 * n + j], recv_sems.at[6 * n + j], (cx, cy, c))
                  for j, (cx, cy) in enumerate(chips)]
        for cp in first:
            cp.start()
        passed = []
        for k in range(n):
            for j, (cx, cy) in enumerate(chips):
                landed = half(outs[k], 2 * cx + cy, c)
                _remote(landed, landed, send_sems.at[6 * k + j], recv_sems.at[6 * k + j], (x, y, c)).wait_recv()
                passed.append(_remote(landed, landed, send_sems.at[6 * k + 3 + j], recv_sems.at[6 * k + 3 + j], sibling))
                passed[-1].start()
        for k in range(n):
            for j, (cx, cy) in enumerate(chips):
                theirs = half(outs[k], 2 * cx + cy, 1 - c)
                _remote(theirs, theirs, send_sems.at[6 * k + 3 + j], recv_sems.at[6 * k + 3 + j], (x, y, c)).wait_recv()
        for j, (cx, cy) in enumerate(chips):
            slot = outs[n].at[2 * cx + cy]
            _remote(slot, slot, send_sems.at[6 * n + j], recv_sems.at[6 * n + j], (x, y, c)).wait_recv()
        for cp in first + passed:
            cp.wait_send()

    arrays = [stacks[k] for k in names] + [wc_stack]
    outs = pl.pallas_call(
        body, name="allgather_weights",
        in_specs=[HBM] * (n + 1), out_specs=[HBM] * (n + 1), input_output_aliases={k: k for k in range(n + 1)},
        out_shape=[_sds(a.shape, a.dtype) for a in arrays],
        scratch_shapes=[pltpu.SemaphoreType.DMA((6 * n + 3,)), pltpu.SemaphoreType.DMA((6 * n + 3,))],
        compiler_params=pltpu.CompilerParams(collective_id=_COLLECTIVE["gather_in"]),
    )(*arrays)
    return dict(zip(names, outs[:n])), outs[n]


_KIND = {"w_in": "stack", "w_pa": "col", "w_pb": "col", "w_up": "col", "w_out": "row", "w_down": "row"}


def _half_view(ref, kind, h):
    if kind == "stack":
        k = ref.shape[1] // 2
        return ref.at[:, pl.ds(h * k, k), :]
    if kind == "col":
        k = ref.shape[0] // 2
        return ref.at[pl.ds(h * k, k), :]
    k = ref.shape[1] // 2
    return ref.at[:, pl.ds(h * k, k)]


def _shard_view(ref, kind, i):
    if kind == "stack":
        return ref.at[i]
    if kind == "col":
        k = ref.shape[1] // N_CHIPS
        return ref.at[:, pl.ds(i * k, k)]
    k = ref.shape[0] // N_CHIPS
    return ref.at[pl.ds(i * k, k), :]


def _region_view(ref, kind, h):
    if kind == "row":
        k = ref.shape[1] // 2
        return ref.at[:, pl.ds(h * k, k)]
    k = ref.shape[0] // 2
    return ref.at[pl.ds(h * k, k), :]


def _half_shape(shape, kind):
    if kind == "stack":
        return (shape[0], shape[1] // 2, shape[2])
    return (shape[0] // 2, shape[1]) if kind == "col" else (shape[0], shape[1] // 2)


def _part_shape(half_shape, kind):
    if kind == "stack":
        return tuple(half_shape[1:])
    k, w = half_shape
    return (k, w // N_CHIPS) if kind == "col" else (k // N_CHIPS, w)


_DATAFLOW = pltpu.SideEffectType.DATAFLOW_SIDE_EFFECTING
_TOKEN = (SUBLANES, LANES)


_COLLECTIVE = {k: i for i, k in enumerate(
    [kind + "_" + g for kind in ("pair", "chip", "share") for g in ("ffn", "proj", "in")]
    + ["gather_proj", "gather_ffn", "gather_in", "forward_proj", "forward_ffn"])}


def _sibling_peers(x, y, c):
    return [(x, y, 1 - c)]


def _chip_peers(x, y, c):
    return [(cx, cy, c) for cx, cy in _other_chips(x, y)]


def _handshake(peers):
    barrier = pltpu.get_barrier_semaphore()
    for peer in peers:
        pl.semaphore_signal(barrier, inc=1, device_id=peer, device_id_type=MESH)
    pl.semaphore_wait(barrier, len(peers))


def _split_start(name, arrays, n_sems, issue, after=None, handshake=None):
    n = len(arrays)
    order = [] if after is None else [after]

    def body(*refs):
        base = n + len(order)
        if handshake is not None:
            _handshake(handshake[1](*_mesh_pos()))
        issue(refs[:n], refs[base], refs[base + 1])
        refs[-1][...] = jnp.zeros(_TOKEN, F32)

    params = dict(has_side_effects=_DATAFLOW)
    if handshake is not None:
        params["collective_id"] = handshake[0]
    outs = pl.pallas_call(
        body, name=name,
        in_specs=[HBM] * n + [ANY] * len(order), out_specs=[SEM, SEM] + [HBM] * n + [pl.BlockSpec(memory_space=pltpu.VMEM)],
        out_shape=[pltpu.SemaphoreType.DMA((n_sems,)), pltpu.SemaphoreType.DMA((n_sems,))]
        + [pltpu.HBM(a.shape, a.dtype) for a in arrays] + [_sds(_TOKEN, F32)],
        input_output_aliases={k: 2 + k for k in range(n)},
        compiler_params=pltpu.CompilerParams(**params),
    )(*[pltpu.with_memory_space_constraint(a, pltpu.HBM) for a in arrays], *order)
    return outs[0], outs[1], list(outs[2:2 + n]), outs[-1]


def _split_wait(name, started, waits, after):
    send_sems, recv_sems, arrays, _ = started
    n = len(arrays)

    def body(*refs):
        waits(refs[:n], refs[n], refs[n + 1])

    return pl.pallas_call(
        body, name=name,
        in_specs=[HBM] * n + [SEM, SEM, ANY], out_specs=[HBM] * n,
        out_shape=[pltpu.HBM(a.shape, a.dtype) for a in arrays],
        input_output_aliases={k: k for k in range(n)},
        compiler_params=pltpu.CompilerParams(has_side_effects=_DATAFLOW),
    )(*arrays, send_sems, recv_sems, after)


def _wait_both(src, dst, send_sem, recv_sem):
    x, y, c = _mesh_pos()
    cp = _remote(src, dst, send_sem, recv_sem, (x, y, c))
    cp.wait_send()
    cp.wait_recv()


def _pair_exchange_start(parts, tag, after):
    names = list(parts)
    n = len(names)
    lands = [lax.empty(_half_shape(parts[k].shape, _KIND[k]), parts[k].dtype) for k in names]

    def issue(refs, send_sems, recv_sems):
        x, y, c = _mesh_pos()
        for hc in range(2):
            @pl.when(c == hc)
            def _():
                for k in range(n):
                    _remote(_half_view(refs[k], _KIND[names[k]], 1 - hc), refs[n + k], send_sems.at[k], recv_sems.at[k],
                            (x, y, 1 - c)).start()

    return names, _split_start("grad_pair_exchange_start_" + tag, [parts[k] for k in names] + lands, n, issue, after,
                               (_COLLECTIVE["pair_" + tag], _sibling_peers))


def _pair_exchange_wait(pending, tag, after):
    names, started = pending
    n = len(names)

    def waits(refs, send_sems, recv_sems):
        for k in range(n):
            _wait_both(_half_view(refs[k], _KIND[names[k]], 0), refs[n + k], send_sems.at[k], recv_sems.at[k])

    outs = _split_wait("grad_pair_exchange_wait_" + tag, started, waits, after)
    return dict(zip(names, outs[:n])), dict(zip(names, outs[n:]))


def _half_blocks(shape, kind):
    if kind == "stack":
        _, k, w = shape
        return (N_CHIPS // 2, 1), (2, k // 2, w), (lambda i, r, s: (i, r, 0)), (lambda i, r, s: (i, s[1] + r, 0))
    k, w = shape
    if kind == "col":
        tr = STREAM_ROWS
        nb = k // 2 // tr
        return (nb,), (tr, w), (lambda r, s: (r, 0)), (lambda r, s: (s[1] * nb + r, 0))
    nb = 2
    return (nb,), (k // nb, w // 2), (lambda r, s: (r, 0)), (lambda r, s: (r, s[1]))


def _pair_add(part, from_sibling, name, pos):
    kind = _KIND[name]
    grid, block, half_map, full_map = _half_blocks(part.shape, kind)

    def body(s_ref, p_ref, q_ref, o_ref):
        o_ref[...] = (p_ref[...].astype(F32) + q_ref[...].astype(F32)).astype(BF16)

    return pl.pallas_call(
        body, name="grad_pair_add_" + name,
        grid_spec=pltpu.PrefetchScalarGridSpec(
            num_scalar_prefetch=1, grid=grid,
            in_specs=[pl.BlockSpec(block, full_map), pl.BlockSpec(block, half_map)],
            out_specs=pl.BlockSpec(block, half_map)),
        out_shape=_sds(from_sibling.shape, BF16),
        compiler_params=_cp(("arbitrary",) * len(grid), 40),
    )(pos, *_hbm(part, from_sibling))


def _pair_add_group(parts, from_sibling, tag, pos):
    names = list(parts)
    n = len(names)
    full_specs, half_specs = [], []
    for name in names:
        k, w = parts[name].shape
        if _KIND[name] == "col":
            block, full_map = (k // 4, w), (lambda r, s: (2 * s[1] + r, 0))
        else:
            block, full_map = (k // 2, w // 2), (lambda r, s: (r, s[1]))
        full_specs.append(pl.BlockSpec(block, full_map))
        half_specs.append(pl.BlockSpec(block, lambda r, s: (r, 0)))

    def body(s_ref, *refs):
        for k in range(n):
            refs[2 * n + k][...] = (refs[k][...].astype(F32) + refs[n + k][...].astype(F32)).astype(BF16)

    outs = pl.pallas_call(
        body, name="grad_pair_add_" + tag,
        grid_spec=pltpu.PrefetchScalarGridSpec(
            num_scalar_prefetch=1, grid=(2,), in_specs=full_specs + half_specs, out_specs=half_specs),
        out_shape=[_sds(from_sibling[k].shape, BF16) for k in names],
        compiler_params=_cp(("arbitrary",), 40),
    )(pos, *_hbm(*[parts[k] for k in names], *[from_sibling[k] for k in names]))
    return dict(zip(names, outs))


def _owner_sum_group(parts, from_sibling, from_chips, tag, pos, shard_shapes):
    names = list(parts)
    n = len(names)
    p_specs, q_specs, r_specs, o_specs = [], [], [], []
    for name in names:
        _, pk, pw = from_chips[name].shape
        block = (pk // 2, pw)
        if _KIND[name] == "row":
            maps = (lambda r, s: (2 * s[0] + r, s[1])), (lambda r, s: (2 * s[0] + r, 0)), (lambda r, s: (r, s[1]))
        else:
            maps = (lambda r, s: (2 * s[1] + r, s[0])), (lambda r, s: (r, s[0])), (lambda r, s: (2 * s[1] + r, 0))
        p_specs.append(pl.BlockSpec(block, maps[0]))
        q_specs.append(pl.BlockSpec(block, maps[1]))
        o_specs.append(pl.BlockSpec(block, maps[2]))
        r_specs.append(pl.BlockSpec((3,) + block, lambda r, s: (0, r, 0)))

    def body(s_ref, *refs):
        for k in range(n):
            acc = refs[k][...].astype(F32) + refs[n + k][...].astype(F32)
            for j in range(3):
                acc = acc + refs[2 * n + k][j].astype(F32)
            refs[3 * n + k][...] = acc

    outs = pl.pallas_call(
        body, name="grad_owner_sum_" + tag,
        grid_spec=pltpu.PrefetchScalarGridSpec(
            num_scalar_prefetch=1, grid=(2,), in_specs=p_specs + q_specs + r_specs, out_specs=o_specs),
        out_shape=[_sds(shard_shapes[k], F32) for k in names],
        compiler_params=_cp(("arbitrary",), 32),
    )(pos, *_hbm(*[parts[k] for k in names], *[from_sibling[k] for k in names], *[from_chips[k] for k in names]))
    return dict(zip(names, outs))


def _chip_exchange_start(sums, tag, after):
    names = list(sums)
    n = len(names)
    lands = [lax.empty((3,) + _part_shape(sums[k].shape, _KIND[k]), sums[k].dtype) for k in names]

    def issue(refs, send_sems, recv_sems):
        x, y, c = _mesh_pos()
        me = 2 * x + y
        for i in range(N_CHIPS):
            xi, yi = i // 2, i % 2
            j = jnp.where(xi != x, jnp.where(yi != y, 2, 0), 1)

            @pl.when(i != me)
            def _():
                for k in range(n):
                    _remote(_shard_view(refs[k], _KIND[names[k]], i), refs[n + k].at[j], send_sems.at[3 * k + j],
                            recv_sems.at[3 * k + j], (xi, yi, c)).start()

    return names, _split_start("grad_chip_exchange_start_" + tag, [sums[k] for k in names] + lands, 3 * n, issue, after,
                               (_COLLECTIVE["chip_" + tag], _chip_peers))


def _chip_exchange_wait(pending, tag, after):
    names, started = pending
    n = len(names)

    def waits(refs, send_sems, recv_sems):
        for k in range(n):
            for j in range(3):
                _wait_both(_shard_view(refs[k], _KIND[names[k]], 0), refs[n + k].at[j], send_sems.at[3 * k + j], recv_sems.at[3 * k + j])

    return dict(zip(names, _split_wait("grad_chip_exchange_wait_" + tag, started, waits, after)[n:]))


def _allgather_start(stacks, tag, after):
    names = list(stacks)

    def issue(refs, send_sems, recv_sems):
        x, y, c = _mesh_pos()
        me = 2 * x + y
        for k, st in enumerate(refs):
            hr = st.shape[1] // 2
            mine = st.at[me, pl.ds(c * hr, hr), :]
            for j, (cx, cy) in enumerate(_other_chips(x, y)):
                _remote(mine, mine, send_sems.at[3 * k + j], recv_sems.at[3 * k + j], (cx, cy, c)).start()

    return names, _split_start("allgather_start_" + tag, [stacks[k] for k in names], 3 * len(names), issue, after,
                               (_COLLECTIVE["gather_" + tag], _chip_peers))


def _allgather_wait(pending, tag, after):
    names, started = pending

    def waits(refs, send_sems, recv_sems):
        for k, st in enumerate(refs):
            slot = st.at[0, pl.ds(0, st.shape[1] // 2), :]
            for j in range(3):
                _wait_both(slot, slot, send_sems.at[3 * k + j], recv_sems.at[3 * k + j])

    return dict(zip(names, _split_wait("allgather_wait_" + tag, started, waits, after)))


def _allgather_forward(stacks, tag):
    names = list(stacks)
    n = len(names)

    def body(*refs):
        ins, outs = refs[:n], refs[n:2 * n]
        send_sems, recv_sems = refs[2 * n:]
        x, y, c = _mesh_pos()
        _handshake(_sibling_peers(x, y, c))
        copies = []
        for k in range(n):
            hr = ins[k].shape[1] // 2
            for j, (cx, cy) in enumerate(_other_chips(x, y)):
                chip = 2 * cx + cy
                copies.append(_remote(ins[k].at[chip, pl.ds(c * hr, hr), :], outs[k].at[chip, pl.ds(c * hr, hr), :],
                                      send_sems.at[3 * k + j], recv_sems.at[3 * k + j], (x, y, 1 - c)))
        for cp in copies:
            cp.start()
        for cp in copies:
            cp.wait()

    arrays = [stacks[k] for k in names]
    outs = pl.pallas_call(
        body, name="allgather_forward_" + tag, in_specs=[HBM] * n, out_specs=[HBM] * n,
        input_output_aliases={k: k for k in range(n)},
        out_shape=[_sds(a.shape, a.dtype) for a in arrays],
        scratch_shapes=[pltpu.SemaphoreType.DMA((3 * n,)), pltpu.SemaphoreType.DMA((3 * n,))],
        compiler_params=pltpu.CompilerParams(collective_id=_COLLECTIVE["forward_" + tag]),
    )(*arrays)
    return dict(zip(names, outs))


def _owner_sum(part, from_sibling, from_chips, name, pos, shard_shape):
    kind = _KIND[name]
    _, pk, pw = from_chips.shape
    if kind == "row":
        nb = 1
        tr = pk // nb
        p_spec = pl.BlockSpec((tr, pw), lambda r, s: (s[0] * nb + r, s[1]))
        q_spec = pl.BlockSpec((tr, pw), lambda r, s: (s[0] * nb + r, 0))
        o_spec = pl.BlockSpec((tr, pw), lambda r, s: (r, s[1]))
    else:
        tr = STREAM_ROWS
        nb = pk // tr
        if kind == "stack":
            p_spec = pl.BlockSpec((None, tr, pw), lambda r, s: (s[0], s[1] * nb + r, 0))
            q_spec = pl.BlockSpec((None, tr, pw), lambda r, s: (s[0], r, 0))
        else:
            p_spec = pl.BlockSpec((tr, pw), lambda r, s: (s[1] * nb + r, s[0]))
            q_spec = pl.BlockSpec((tr, pw), lambda r, s: (r, s[0]))
        o_spec = pl.BlockSpec((tr, pw), lambda r, s: (s[1] * nb + r, 0))

    def body(s_ref, p_ref, q_ref, r_ref, o_ref):
        acc = p_ref[...].astype(F32) + q_ref[...].astype(F32)
        for j in range(3):
            acc = acc + r_ref[j].astype(F32)
        o_ref[...] = acc

    return pl.pallas_call(
        body, name="grad_owner_sum_" + name,
        grid_spec=pltpu.PrefetchScalarGridSpec(
            num_scalar_prefetch=1, grid=(nb,),
            in_specs=[p_spec, q_spec, pl.BlockSpec((3, tr, pw), lambda r, s: (0, r, 0))],
            out_specs=o_spec),
        out_shape=_sds(shard_shape, F32),
        compiler_params=_cp(("arbitrary",), 32),
    )(pos, *_hbm(part, from_sibling, from_chips))


def _pair_share_start(shards, tag, after):
    names = list(shards)

    def issue(refs, send_sems, recv_sems):
        x, y, c = _mesh_pos()
        for hc in range(2):
            @pl.when(c == hc)
            def _():
                for k, g in enumerate(refs):
                    mine = _region_view(g, _KIND[names[k]], hc)
                    _remote(mine, mine, send_sems.at[k], recv_sems.at[k], (x, y, 1 - c)).start()

    return names, _split_start("grad_pair_share_start_" + tag, [shards[k] for k in names], len(names), issue, after,
                               (_COLLECTIVE["share_" + tag], _sibling_peers))


def _pair_share_wait(pending, tag, after):
    names, started = pending

    def waits(refs, send_sems, recv_sems):
        for k, g in enumerate(refs):
            region = _region_view(g, _KIND[names[k]], 0)
            _wait_both(region, region, send_sems.at[k], recv_sems.at[k])

    return dict(zip(names, _split_wait("grad_pair_share_wait_" + tag, started, waits, after)))


def _small_exchange_start(slots, after):
    def issue(refs, send_sems, recv_sems):
        x, y, c = _mesh_pos()
        mine = refs[0].at[4 * x + 2 * y + c]
        k = 0
        for px in range(2):
            for py in range(2):
                for pc in range(2):
                    if px + py + pc:
                        peer = (1 - x if px else x, 1 - y if py else y, 1 - c if pc else c)
                        _remote(mine, mine, send_sems.at[k], recv_sems.at[k], peer).start()
                        k += 1

    return _split_start("small_exchange_start", [slots], N_DEV - 1, issue, after)


def _small_exchange_wait(started, after):
    def waits(refs, send_sems, recv_sems):
        slot = refs[0].at[0]
        for k in range(N_DEV - 1):
            _wait_both(slot, slot, send_sems.at[k], recv_sems.at[k])

    return _split_wait("small_exchange_wait", started, waits, after)[0]


def _adam_math(w, g, m, v):
    m = ADAM_B1 * m + (1.0 - ADAM_B1) * g
    v = ADAM_B2 * v + (1.0 - ADAM_B2) * (g * g)
    m_hat = m / (1.0 - ADAM_B1 ** ADAM_STEP)
    v_hat = v / (1.0 - ADAM_B2 ** ADAM_STEP)
    delta = -ADAM_LR * (m_hat / (jnp.sqrt(v_hat) + ADAM_EPS) + ADAM_WD * w)
    return delta, m, v


def _adamw(w, g, m, v, name):
    rows, cols = w.shape[0], w.shape[-1]
    fits = [t for t in range(SUBLANES, rows, SUBLANES) if rows % t == 0 and t * cols * 4 <= (3 << 19)]
    tr = max(fits) if fits and w.ndim == 2 else rows

    def body(w_ref, g_ref, m_ref, v_ref, d_ref, nm_ref, nv_ref, go_ref):
        g = g_ref[...]
        d, nm, nv = _adam_math(w_ref[...], g, m_ref[...], v_ref[...])
        d_ref[...] = d
        nm_ref[...] = nm
        nv_ref[...] = nv
        go_ref[...] = g

    spec = pl.BlockSpec((tr,) + w.shape[1:], lambda i: (i,) + (0,) * (w.ndim - 1))
    return pl.pallas_call(
        body, name=name, grid=(rows // tr,), in_specs=[spec] * 4, out_specs=[spec] * 4,
        out_shape=[_sds(w.shape, F32)] * 4, compiler_params=_cp(("arbitrary",)),
    )(*_hbm(w, g, m, v))


def _adamw_group(w, g, m, v, tag):
    names = list(w)
    n = len(names)
    steps = 4
    specs = [pl.BlockSpec((w[k].shape[0] // steps, w[k].shape[1]), lambda i: (i, 0)) for k in names]

    def body(*refs):
        for k in range(n):
            w_ref, g_ref, m_ref, v_ref = [refs[j * n + k] for j in range(4)]
            d_ref, nm_ref, nv_ref, go_ref = refs[4 * n + 4 * k:4 * n + 4 * k + 4]
            grad = g_ref[...]
            d, nm, nv = _adam_math(w_ref[...], grad, m_ref[...], v_ref[...])
            d_ref[...] = d
            nm_ref[...] = nm
            nv_ref[...] = nv
            go_ref[...] = grad

    res = pl.pallas_call(
        body, name="adamw_" + tag, grid=(steps,), in_specs=specs * 4, out_specs=[s for s in specs for _ in range(4)],
        out_shape=[_sds(w[k].shape, F32) for k in names for _ in range(4)], compiler_params=_cp(("arbitrary",), 48),
    )(*_hbm(*[a[k] for a in (w, g, m, v) for k in names]))
    return {k: tuple(res[4 * i:4 * i + 4]) for i, k in enumerate(names)}


def _small_sum_adamw(gathered, w, m, v):
    names = _SMALL_IN_CALL
    n = len(names)

    def body(*refs):
        a_ref = refs[0]
        w_refs, m_refs, v_refs = refs[1:1 + n], refs[1 + n:1 + 2 * n], refs[1 + 2 * n:1 + 3 * n]
        sum_ref, loss_ref = refs[1 + 3 * n], refs[2 + 3 * n]
        outs = refs[3 + 3 * n:]
        g = a_ref[0]
        for k in range(1, N_DEV):
            g = g + a_ref[k]
        sum_ref[...] = g
        loss_ref[...] = g[0:1, 0:1]
        for i, name in enumerate(names):
            if name == "rel_bias":
                r0, c0 = _REL_BIAS_AT
                pieces = [(slice(None), g[r0:r0 + N_HEADS, c0:c0 + N_BUCKETS])]
            elif name == "b_conv":
                r0 = _SMALL_AT[name][0]
                pieces = [(slice(None), jnp.concatenate([g[r0 + k:r0 + k + 1, :] for k in range(CONV_ROWS)], axis=1)[:, :2 * D_FF])]
            elif name == "w_s":
                r0, nr, _ = _SMALL_AT[name]
                pieces = [(slice(nr * j, nr * (j + 1)), g[r0:r0 + nr, CHUNK * j:CHUNK * (j + 1)]) for j in range(D_MODEL // CHUNK)]
            else:
                r0, nr, nc = _SMALL_AT[name]
                pieces = [(slice(None), g[r0:r0 + nr, 0:nc])]
            for at, gp in pieces:
                d, nm, nv = _adam_math(w_refs[i][at], gp, m_refs[i][at], v_refs[i][at])
                for k, val in enumerate((gp, d, nm, nv)):
                    outs[4 * i + k][at] = val

    shapes = [w[k].shape for k in names]
    res = pl.pallas_call(
        body, name="small_sum_adamw",
        out_shape=[_sds((SMALL_ROWS, D_MODEL), F32), _sds((1, 1), F32)] + [_sds(s, F32) for s in shapes for _ in range(4)],
    )(gathered, *[w[k] for k in names], *[m[k] for k in names], *[v[k] for k in names])
    return res[0], res[1], {k: tuple(res[2 + 4 * i:6 + 4 * i]) for i, k in enumerate(names)}


_NAMES = ("g_mix", "w_in", "g_sgu", "w_s", "b_s", "sinks", "rel_bias", "w_pa", "w_pb", "w_out",
          "g_ffn", "w_up", "w_conv", "b_conv", "w_down", "g_final")

def kernel(x, g_mix, w_in, g_sgu, w_s, b_s, sinks, rel_bias, w_pa, w_pb, w_out, g_ffn, w_up, w_conv, b_conv, w_down, g_final, loss_target, m_g_mix, m_w_in, m_g_sgu, m_w_s, m_b_s, m_sinks, m_rel_bias, m_w_pa, m_w_pb, m_w_out, m_g_ffn, m_w_up, m_w_conv, m_b_conv, m_w_down, m_g_final, v_g_mix, v_w_in, v_g_sgu, v_w_s, v_b_s, v_sinks, v_rel_bias, v_w_pa, v_w_pb, v_w_out, v_g_ffn, v_w_up, v_w_conv, v_b_conv, v_w_down, v_g_final):
    w = dict(g_mix=g_mix, w_in=w_in, g_sgu=g_sgu, w_s=w_s, b_s=b_s, sinks=sinks, rel_bias=rel_bias, w_pa=w_pa, w_pb=w_pb,
             w_out=w_out, g_ffn=g_ffn, w_up=w_up, w_conv=w_conv, b_conv=b_conv, w_down=w_down, g_final=g_final)
    m = dict(g_mix=m_g_mix, w_in=m_w_in, g_sgu=m_g_sgu, w_s=m_w_s, b_s=m_b_s, sinks=m_sinks, rel_bias=m_rel_bias, w_pa=m_w_pa,
             w_pb=m_w_pb, w_out=m_w_out, g_ffn=m_g_ffn, w_up=m_w_up, w_conv=m_w_conv, b_conv=m_b_conv, w_down=m_w_down,
             g_final=m_g_final)
    v = dict(g_mix=v_g_mix, w_in=v_w_in, g_sgu=v_g_sgu, w_s=v_w_s, b_s=v_b_s, sinks=v_sinks, rel_bias=v_rel_bias, w_pa=v_w_pa,
             w_pb=v_w_pb, w_out=v_w_out, g_ffn=v_g_ffn, w_up=v_w_up, w_conv=v_w_conv, b_conv=v_b_conv, w_down=v_w_down,
             g_final=v_g_final)
    xi, yi, ci = _mesh_pos()
    me = 2 * xi + yi

    shard = {n: w[n][0] for n in _BIG}
    shard_shapes = {n: shard[n].shape for n in _BIG}
    wc_shard = w["w_conv"][0]
    wc_pad = jnp.pad(wc_shard, ((0, 5), (0, 0)))
    own = {n: _own_slot(shard[n].astype(BF16), N_CHIPS, me) for n in _BIG if n != "w_in"}
    own["w_in"] = _own_slot(shard["w_in"].T.astype(BF16), N_CHIPS, me)
    stacks, wc_all = _allgather_weights({"w_in": own["w_in"]}, _own_slot(wc_pad, N_CHIPS, me))
    proj_gather = _allgather_start({n: own[n] for n in _MIXER[1:]}, "proj", stacks["w_in"])
    ffn_gather = _allgather_start({n: own[n] for n in _FFN}, "ffn", proj_gather[1][-1])
    w_conv_full = jnp.concatenate([wc_all[i, :3] for i in range(N_CHIPS)], axis=1)
    w_in_full = stacks["w_in"].reshape(N_CHIPS * IN_SHARD, D_MODEL)
    pos = jnp.stack([me, ci])

    def proj_weights(done):
        st = _allgather_forward(_allgather_wait(proj_gather, "proj", done), "proj")
        return st["w_pa"], st["w_pb"], st["w_out"].reshape(D_MODEL, D_MODEL)

    def ffn_weights(done):
        st = _allgather_forward(_allgather_wait(ffn_gather, "ffn", done), "ffn")
        return st["w_up"], st["w_down"].reshape(D_FF, D_MODEL)

    groups = {}

    def stage1(group, parts):
        groups[group] = dict(parts=parts, pair=_pair_exchange_start(parts, group, None))
        return groups[group]["pair"][1][-1]

    def stage2(group, after, order_after):
        g = groups[group]
        g["parts"], g["sib"] = _pair_exchange_wait(g["pair"], group, after)
        if group == "in":
            sums = {n: _pair_add(g["parts"][n], g["sib"][n], n, pos) for n in g["parts"]}
        else:
            sums = _pair_add_group(g["parts"], g["sib"], group, pos)
        g["chip"] = _chip_exchange_start(sums, group, order_after)
        return g["chip"][1][-1]

    def stage3(group, after, order_after):
        g = groups[group]
        got = _chip_exchange_wait(g["chip"], group, after)
        if group == "in":
            owned = {n: _owner_sum(g["parts"][n], g["sib"][n], got[n], n, pos, shard_shapes[n]) for n in g["parts"]}
        else:
            owned = _owner_sum_group(g["parts"], g["sib"], got, group, pos, shard_shapes)
        g["share"] = _pair_share_start(owned, group, order_after)
        return g["share"][1][-1]

    grads, deltas, new_m, new_v = {}, {}, {}, {}

    def stage4(group, after):
        g_shard = _pair_share_wait(groups[group]["share"], group, after)
        if group != "in":
            res = _adamw_group({n: shard[n] for n in g_shard}, g_shard, {n: m[n][0] for n in g_shard},
                               {n: v[n][0] for n in g_shard}, group)
            for n, (d, nm, nv, go) in res.items():
                grads[n], deltas[n], new_m[n], new_v[n] = go[None], d[None], nm[None], nv[None]
            return nv
        last = None
        for n in g_shard:
            g = _tie(g_shard[n], last)
            if n == "w_in":
                d, nm, nv, gt = _adamw(shard[n].T, g.T, m[n][0].T, v[n][0].T, "adamw_" + n)
                grads[n], deltas[n], new_m[n], new_v[n] = gt.T[None], d.T[None], nm.T[None], nv.T[None]
            else:
                d, nm, nv, go = _adamw(shard[n], g, m[n][0], v[n][0], "adamw_" + n)
                grads[n], deltas[n], new_m[n], new_v[n] = go[None], d[None], nm[None], nv[None]
            last = nv
        return last

    def on_grads(group, parts):
        token = stage1(group, parts)
        some = next(iter(parts.values()))
        if group == "proj":
            token = stage2("ffn", some, token)
        if group == "in":
            token = stage2("proj", some, token)
            token = stage3("ffn", some, token)
            token = stage2("in", token, token)
        return token

    loss, grad_x, small, big = _local_step(
        x, loss_target, w["g_mix"], w["g_sgu"], w["w_s"][0], w["b_s"][0], w["sinks"], w["rel_bias"], w["g_ffn"],
        w["b_conv"], w["g_final"], w_in_full, w_conv_full, proj_weights, ffn_weights, on_grads, ffn_gather[1][-1])

    small["loss"] = loss
    small_gather = _small_exchange_start(_own_slot(_pack_small(small), N_DEV, 2 * me + ci), grad_x)
    token = stage3("proj", grad_x, small_gather[-1])
    done = stage4("ffn", token)
    done = stage4("proj", done)
    token = stage3("in", done, None)
    all_small = _small_exchange_wait(small_gather, token)
    two_d = {n: (lambda a, n=n: a.reshape(_SMALL_AT[n][1:])) for n in _SMALL_IN_CALL}
    two_d["rel_bias"] = lambda a: a.T
    two_d["b_conv"] = lambda a: a
    two_d["w_s"] = lambda a: a.reshape(A_GROUPS * CHUNK, CHUNK)
    s_sum, s_loss, s_out = _small_sum_adamw(all_small, *[{n: two_d[n](p[n]) for n in _SMALL_IN_CALL} for p in (w, m, v)])
    stage4("in", s_sum)
    for n in _SMALL_IN_CALL:
        back = (lambda a: a.T) if n == "rel_bias" else (lambda a, n=n: a.reshape(w[n].shape))
        grads[n], deltas[n], new_m[n], new_v[n] = [back(a) for a in s_out[n]]

    def rows(n):
        r0, nr, _ = _SMALL_AT[n]
        return s_sum[r0:r0 + nr]

    wcols = wc_shard.shape[1]
    g_wc = lax.dynamic_slice(_unwide(rows("w_conv"), 3), (0, me * wcols), (3, wcols))
    taps = lambda a: a.transpose(1, 0, 2)
    res = _adamw(taps(w["w_conv"]), g_wc[:, None, :], taps(m["w_conv"]), taps(v["w_conv"]), "adamw_w_conv")
    deltas["w_conv"], new_m["w_conv"], new_v["w_conv"], grads["w_conv"] = [taps(a) for a in res]

    return (s_loss.reshape(()), grad_x, *[grads[n] for n in _NAMES], *[deltas[n] for n in _NAMES],
            *[new_m[n] for n in _NAMES], *[new_v[n] for n in _NAMES])
```

```python
import functools

import numpy as np
import jax
import jax.numpy as jnp
from jax import lax
from jax.experimental import pallas as pl
from jax.experimental.pallas import tpu as pltpu

F32 = jnp.float32
BF16 = jnp.bfloat16

D_MODEL = 1024
CHUNK = 128
A_GROUPS = 4
A_WIDTH = 512
N_HEADS = 8
HEAD_DIM = 64
Q_DIM = 512
KV_DIM = 128
N_BUCKETS = 32
MAX_DISTANCE = 128
D_FF = 2816
EPS = 1e-6
NEG_INF = -1e30
G_DIM = 2 * D_MODEL
A_DIM = 2 * A_WIDTH
B_DIM = Q_DIM + 2 * KV_DIM
LANES = 128
SUBLANES = 8
ROW_TILE = 512
WIDE_ROW_TILE = 256
COL_CHUNK = 512
GRAD_ROW_TILE = 512
STREAM_ROWS = 256
BF16_ROWS = 16
N_CHIPS = 4
N_DEV = 8

ADAM_LR = 0.001
ADAM_B1 = 0.9
ADAM_B2 = 0.999
ADAM_EPS = 1e-08
ADAM_WD = 0.01
ADAM_STEP = 10

MESH = pl.DeviceIdType.MESH
_GELU_C = 0.7978845608028654
_GELU_A = 0.044715


def _cp(sem=None, vmem_mb=None):
    kw = {}
    if sem is not None:
        kw["dimension_semantics"] = sem
    if vmem_mb is not None:
        kw["vmem_limit_bytes"] = vmem_mb << 20
    return pltpu.CompilerParams(**kw)


def _dot(a, b):
    return jnp.dot(a, b, preferred_element_type=F32)


def _dot_nt(a, b):
    return lax.dot_general(a, b, (((1,), (1,)), ((), ())), preferred_element_type=F32)


def _dot_tn(a, b):
    return lax.dot_general(a, b, (((0,), (0,)), ((), ())), preferred_element_type=F32)


def _rms_r(x):
    return lax.rsqrt(jnp.mean(x * x, axis=-1, keepdims=True) + EPS)


def _rms_bwd(dh, n, r, g):
    dn = dh * g
    return r * (dn - n * jnp.mean(dn * n, axis=-1, keepdims=True))


def _gelu(x):
    t = jnp.tanh(_GELU_C * (x + _GELU_A * (x * x * x)))
    return 0.5 * x * (1.0 + t), t


def _gelu_grad(x, t):
    return 0.5 * (1.0 + t) + 0.5 * x * (1.0 - t * t) * (_GELU_C * (1.0 + 3.0 * _GELU_A * x * x))


def _sigmoid(x):
    return 1.0 / (1.0 + jnp.exp(-x))


def _tie(x, dep):
    return x if dep is None else lax.optimization_barrier((x, dep))[0]


def _row(tm, w):
    return pl.BlockSpec((tm, w), lambda i: (i, 0))


def _full(shape):
    nd = len(shape)
    return pl.BlockSpec(tuple(shape), lambda *_: (0,) * nd)


def _resident(shape):
    nd = len(shape)
    return pl.BlockSpec(tuple(shape), lambda *_: (0,) * nd, pipeline_mode=pl.Buffered(1))


def _sds(shape, dtype):
    return pltpu.HBM(tuple(shape), dtype)


def _hbm(*arrays):
    return [pltpu.with_memory_space_constraint(a, pltpu.HBM) for a in arrays]


HBM = pl.BlockSpec(memory_space=pltpu.HBM)
ANY = pl.BlockSpec(memory_space=pl.ANY)
SEM = pl.BlockSpec(memory_space=pltpu.SEMAPHORE)


def _band_buckets():
    i = np.arange(CHUNK)[:, None]
    j = np.arange(2 * CHUNK)[None, :]
    dist = i + CHUNK - j
    valid = (dist >= 0) & (dist < CHUNK)
    d = np.clip(dist, 0, None)
    max_exact = N_BUCKETS // 2
    large = max_exact + (np.log(np.maximum(d, 1) / max_exact) / np.log(MAX_DISTANCE / max_exact)
                         * (N_BUCKETS - max_exact)).astype(np.int32)
    large = np.minimum(large, N_BUCKETS - 1)
    buckets = np.where(d < max_exact, d, large).astype(np.int32)
    return np.where(valid, buckets, -1).astype(np.int32)


_A_COLS = slice(0, A_DIM)
_B_COLS = slice(A_DIM, A_DIM + B_DIM)
_G_COLS = slice(A_DIM + B_DIM, A_DIM + B_DIM + G_DIM)


def _inproj(x2, g_mix, w_in, g_sgu, w_s, b_st, tm, after=None):
    T = x2.shape[0]
    order = [] if after is None else [after]

    def body(*refs):
        x_ref, g_ref, w_ref, gs_ref, ws_ref, bs_ref = refs[:6]
        pg_ref, pa_ref, pb_ref, h_ref, ya_ref = refs[6 + len(order):]
        x = x_ref[...]
        h = (x * _rms_r(x) * g_ref[...]).astype(BF16)
        h_ref[...] = h
        pa = _dot_nt(h, w_ref[_A_COLS, :]).astype(BF16)
        pa_ref[...] = pa
        pb_ref[...] = _dot_nt(h, w_ref[_B_COLS, :]).astype(BF16)
        pg_ref[...] = _dot_nt(h, w_ref[_G_COLS, :]).astype(BF16)
        _sgu_apply(pa.astype(F32), gs_ref[...], ws_ref, bs_ref, ya_ref)

    return pl.pallas_call(
        body, name="inproj", grid=(T // tm,),
        in_specs=[_row(tm, D_MODEL), _full(g_mix.shape), _resident(w_in.shape), _full(g_sgu.shape), _full(w_s.shape),
                  _full(b_st.shape)] + [ANY] * len(order),
        out_specs=[_row(tm, G_DIM), _row(tm, A_DIM), _row(tm, B_DIM), _row(tm, D_MODEL), _row(tm, A_WIDTH)],
        out_shape=[_sds((T, G_DIM), BF16), _sds((T, A_DIM), BF16), _sds((T, B_DIM), BF16), _sds((T, D_MODEL), BF16),
                   _sds((T, A_WIDTH), BF16)],
        compiler_params=_cp(("arbitrary",), 48),
    )(*_hbm(x2, g_mix, w_in, g_sgu, w_s, b_st), *order)


def _sgu_parts(p, g):
    pu = p[:, :A_WIDTH]
    pv = p[:, A_WIDTH:]
    u, tu = _gelu(pu)
    vv, tv = _gelu(pv)
    rv = _rms_r(vv)
    vn = (vv * rv * g).astype(BF16)
    return pu, pv, u, tu, vv, tv, rv, vn


def _tril():
    r = lax.broadcasted_iota(jnp.int32, (CHUNK, CHUNK), 0)
    c = lax.broadcasted_iota(jnp.int32, (CHUNK, CHUNK), 1)
    return r >= c


def _sgu_apply(p, g, ws_ref, bs_ref, y_ref):
    tril = _tril()
    _, _, u, _, _, _, _, vn = _sgu_parts(p, g)
    for gi in range(A_GROUPS):
        wm = jnp.where(tril, ws_ref[gi], 0.0).astype(BF16)
        bcol = bs_ref[:, gi:gi + 1]
        cs = slice(gi * CHUNK, (gi + 1) * CHUNK)
        for c in range(p.shape[0] // CHUNK):
            rs = slice(c * CHUNK, (c + 1) * CHUNK)
            s = _dot(wm, vn[rs, cs]) + bcol
            y_ref[rs, cs] = (u[rs, cs] * s).astype(BF16)


HEAD_ROWS = N_HEADS * CHUNK


def _head_rows(h):
    return slice(h * CHUNK, (h + 1) * CHUNK)


def _attn_setup(bias_scr, sink_scr, kvar_scr, qkv_ref, bk_ref, rel_ref, sink_ref):
    @pl.when(pl.program_id(0) == 0)
    def _():
        bk = bk_ref[...]
        for h in range(N_HEADS):
            acc = jnp.full((CHUNK, 2 * CHUNK), NEG_INF, F32)
            for b in range(N_BUCKETS):
                acc = jnp.where(bk == b, rel_ref[b, h], acc)
            bias_scr[_head_rows(h), :] = acc
            sink_scr[_head_rows(h), :] = jnp.full((CHUNK, LANES), sink_ref[0, h], F32)

    seq = qkv_ref.shape[0]
    rows_per = 2 * CHUNK
    for is_v in range(2):
        c0 = Q_DIM + is_v * KV_DIM
        for r in range(seq // rows_per):
            rs = slice(r * rows_per, (r + 1) * rows_per)
            a = qkv_ref[rs, c0:c0 + KV_DIM].astype(F32)
            lane = lax.broadcasted_iota(jnp.int32, a.shape, 1)
            lo = jnp.where(lane < HEAD_DIM, a, 0.0)
            hi = jnp.where(lane >= HEAD_DIM, a, 0.0)
            kvar_scr[4 * is_v + 0, rs, :] = lo.astype(BF16)
            kvar_scr[4 * is_v + 1, rs, :] = pltpu.roll(lo, HEAD_DIM, 1).astype(BF16)
            kvar_scr[4 * is_v + 2, rs, :] = pltpu.roll(hi, HEAD_DIM, 1).astype(BF16)
            kvar_scr[4 * is_v + 3, rs, :] = hi.astype(BF16)


def _rowsum(a, ones):
    hi = a.astype(BF16)
    lo = (a - hi.astype(F32)).astype(BF16)
    return _dot(hi, ones) + _dot(lo, ones)


def _both(a):
    return jnp.concatenate([a, a], axis=1)


def _attn_probs(qkv_ref, r0, n, kv, bias_scr, sink_scr, ones):
    s = jnp.concatenate([_dot_nt(qkv_ref[pl.ds(r0, CHUNK), (h // 2) * LANES:(h // 2 + 1) * LANES], kv[h // 4][h % 2])
                         for h in range(N_HEADS)], axis=0)
    s = s * (HEAD_DIM ** -0.5) + bias_scr[...]
    col = lax.broadcasted_iota(jnp.int32, s.shape, 1)
    s = jnp.where((col < CHUNK) & (n == 0), NEG_INF, s)
    sink = sink_scr[...]
    m = jnp.maximum(jnp.max(s, axis=-1, keepdims=True), sink)
    p = jnp.exp(s - _both(m))
    es = jnp.exp(sink - m)
    inv = 1.0 / (_dot(p.astype(BF16), ones) + es)
    return p * _both(inv), es * inv


def _attn_block_inputs(kvar_scr, n):
    r0 = pl.multiple_of(n * CHUNK, CHUNK)
    rp = pl.multiple_of(jnp.maximum(n - 1, 0) * CHUNK, CHUNK)

    def both(idx):
        return jnp.concatenate([kvar_scr[idx, pl.ds(rp, CHUNK), :], kvar_scr[idx, pl.ds(r0, CHUNK), :]], axis=0)

    kv = ((both(0), both(1)), (both(2), both(3)))
    vv = ((both(4), both(5)), (both(6), both(7)))
    return r0, kv, vv


def _attn_fwd(proj_b, sinks, rel_bias, n_seq, seq):
    nb = seq // CHUNK
    bk = jnp.asarray(_band_buckets())

    def body(qkv_ref, bk_ref, rel_ref, sink_ref, o_ref, bias_scr, sink_scr, kvar_scr):
        _attn_setup(bias_scr, sink_scr, kvar_scr, qkv_ref, bk_ref, rel_ref, sink_ref)
        ones = jnp.ones((2 * CHUNK, LANES), BF16)

        def blk(n, carry):
            r0, kv, vv = _attn_block_inputs(kvar_scr, n)
            prob, _ = _attn_probs(qkv_ref, r0, n, kv, bias_scr, sink_scr, ones)
            pb = prob.astype(BF16)
            for pr in range(N_HEADS // 2):
                acc = _dot(pb[_head_rows(2 * pr)], vv[pr // 2][0]) + _dot(pb[_head_rows(2 * pr + 1)], vv[pr // 2][1])
                o_ref[pl.ds(r0, CHUNK), pr * LANES:(pr + 1) * LANES] = acc.astype(BF16)
            return carry

        lax.fori_loop(0, nb, blk, 0)

    smem = pl.BlockSpec(memory_space=pltpu.SMEM)
    return pl.pallas_call(
        body, name="attn_fwd", grid=(n_seq,),
        in_specs=[_row(seq, B_DIM), _full(bk.shape), smem, smem],
        out_specs=_row(seq, Q_DIM), out_shape=_sds((n_seq * seq, Q_DIM), BF16),
        scratch_shapes=[pltpu.VMEM((HEAD_ROWS, 2 * CHUNK), F32), pltpu.VMEM((HEAD_ROWS, LANES), F32),
                        pltpu.VMEM((8, seq, KV_DIM), BF16)],
        compiler_params=_cp(("arbitrary",), 40),
    )(*_hbm(proj_b, bk), rel_bias, sinks)


def _dot_stacked(a, w_ref):
    return jnp.concatenate([_dot(a, w_ref[i]) for i in range(N_CHIPS)], axis=1)


def _dot_nt_stacked(a, w_ref):
    w = w_ref.shape[2]
    acc = _dot_nt(a[:, :w], w_ref[0])
    for i in range(1, N_CHIPS):
        acc = acc + _dot_nt(a[:, i * w:(i + 1) * w], w_ref[i])
    return acc


def _merge_fwd(x2, y_a, y_b, proj_g, w_pa, w_pb, w_out, tm):
    T = x2.shape[0]

    def body(x_ref, ya_ref, yb_ref, g_ref, wpa_ref, wpb_ref, wo_ref, x1_ref, mg_ref):
        g = g_ref[...].astype(F32)
        pa = _dot_stacked(ya_ref[...], wpa_ref)
        pb = _dot_stacked(yb_ref[...], wpb_ref)
        merged = (_sigmoid(g[:, :D_MODEL]) * pa + _sigmoid(g[:, D_MODEL:]) * pb).astype(BF16)
        mg_ref[...] = merged
        x1_ref[...] = x_ref[...] + _dot(merged, wo_ref[...])

    return pl.pallas_call(
        body, name="merge_fwd", grid=(T // tm,),
        in_specs=[_row(tm, D_MODEL), _row(tm, A_WIDTH), _row(tm, Q_DIM), _row(tm, G_DIM),
                  _resident(w_pa.shape), _resident(w_pb.shape), _resident(w_out.shape)],
        out_specs=[_row(tm, D_MODEL), _row(tm, D_MODEL)],
        out_shape=[_sds((T, D_MODEL), F32), _sds((T, D_MODEL), BF16)],
        compiler_params=_cp(("arbitrary",), 40),
    )(*_hbm(x2, y_a, y_b, proj_g, w_pa, w_pb, w_out))


def _upproj(x1, g_ffn, w_up, w_conv, b_conv, tm, seq):
    T = x1.shape[0]
    cw = w_up.shape[2]
    tiles_per_seq = seq // tm

    def body(x_ref, g_ref, w_ref, wc_ref, bc_ref, u_ref, h_ref, gate_ref, val_ref, tail_scr):
        at_start = (pl.program_id(0) % tiles_per_seq) == 0
        x = x_ref[...]
        h = (x * _rms_r(x) * g_ref[...]).astype(BF16)
        h_ref[...] = h
        for i in range(N_CHIPS):
            cs = slice(i * cw, (i + 1) * cw)
            u = _dot(h, w_ref[i])
            u_ref[:, cs] = u.astype(BF16)
            hl = jnp.where(at_start, 0.0, tail_scr[SUBLANES - 2:SUBLANES, cs])
            tail_scr[:, cs] = u[tm - SUBLANES:]
            up = _conv_out((u, _shift_down(u, hl, 1), _shift_down(u, hl, 2)), wc_ref[:, cs], bc_ref[:, cs])
            out_ref = gate_ref if i < N_CHIPS // 2 else val_ref
            out_ref[:, (i % 2) * cw:(i % 2 + 1) * cw] = up.astype(BF16)

    return pl.pallas_call(
        body, name="upproj", grid=(T // tm,),
        in_specs=[_row(tm, D_MODEL), _full(g_ffn.shape), _resident(w_up.shape), _full(w_conv.shape), _full(b_conv.shape)],
        out_specs=[_row(tm, 2 * D_FF), _row(tm, D_MODEL), _row(tm, D_FF), _row(tm, D_FF)],
        out_shape=[_sds((T, 2 * D_FF), BF16), _sds((T, D_MODEL), BF16), _sds((T, D_FF), BF16), _sds((T, D_FF), BF16)],
        scratch_shapes=[pltpu.VMEM((SUBLANES, 2 * D_FF), F32)],
        compiler_params=_cp(("arbitrary",), 56),
    )(*_hbm(x1, g_ffn, w_up, w_conv, b_conv))


def _shift_down(u, halo, k):
    rolled = pltpu.roll(u, k, 0)
    head = rolled[:SUBLANES]
    row = lax.broadcasted_iota(jnp.int32, head.shape, 0)
    if k == 1:
        head = jnp.where(row == 0, halo[1:2], head)
    else:
        head = jnp.where(row == 0, halo[0:1], jnp.where(row == 1, halo[1:2], head))
    return jnp.concatenate([head, rolled[SUBLANES:]], axis=0)


def _shift_up(d, halo, k):
    tm = d.shape[0]
    rolled = pltpu.roll(d, tm - k, 0)
    tail = rolled[tm - SUBLANES:]
    row = lax.broadcasted_iota(jnp.int32, tail.shape, 0)
    if k == 1:
        tail = jnp.where(row == SUBLANES - 1, halo[0:1], tail)
    else:
        tail = jnp.where(row == SUBLANES - 2, halo[0:1], jnp.where(row == SUBLANES - 1, halo[1:2], tail))
    return jnp.concatenate([rolled[:tm - SUBLANES], tail], axis=0)


def _conv_out(taps, wc, bc):
    u, u1, u2 = taps
    return wc[0:1] * u2 + wc[1:2] * u1 + wc[2:3] * u + bc


def _ffn_down_loss(gate, val, x1, target, w_down, g_final, tm):
    T = x1.shape[0]
    half = D_FF // 2

    sub = min(tm, 128)

    def body(gt_ref, vl_ref, x1_ref, t_ref, wd_ref, g_ref, dx2_ref, loss_ref, gg_ref):
        i = pl.program_id(0)
        g = g_ref[...]

        def down(rs):
            acc = jnp.zeros((sub, D_MODEL), F32)
            for j in range(2):
                gc = slice(j * half, (j + 1) * half)
                gate = gt_ref[rs, gc].astype(F32)
                act = (gate * _sigmoid(gate) * vl_ref[rs, gc].astype(F32)).astype(BF16)
                acc = acc + _dot(act, wd_ref[gc, :])
            return acc

        def norm_loss(rs, acc):
            x2 = x1_ref[rs, :] + acc
            r = _rms_r(x2)
            n = x2 * r
            diff = n * g - t_ref[rs, :]
            dy = diff * (1.0 / D_MODEL)
            dx2_ref[rs, :] = _rms_bwd(dy, n, r, g)
            return (jnp.sum(jnp.mean(diff * diff, axis=-1, keepdims=True), axis=0, keepdims=True),
                    jnp.sum(dy * n, axis=0, keepdims=True))

        subs = [slice(s0, s0 + sub) for s0 in range(0, tm, sub)]
        accs = [down(rs) for rs in subs]
        parts = [norm_loss(rs, acc) for rs, acc in zip(subs, accs)]

        @pl.when(i == 0)
        def _():
            loss_ref[...] = jnp.zeros_like(loss_ref)
            gg_ref[...] = jnp.zeros_like(gg_ref)

        loss_ref[...] += 0.5 * sum(p[0] for p in parts)
        gg_ref[...] += sum(p[1] for p in parts)

    return pl.pallas_call(
        body, name="ffn_down_loss", grid=(T // tm,),
        in_specs=[_row(tm, D_FF), _row(tm, D_FF), _row(tm, D_MODEL), _row(tm, D_MODEL),
                  _resident(w_down.shape), _full(g_final.shape)],
        out_specs=[_row(tm, D_MODEL), _full((1, 1)), _full((1, D_MODEL))],
        out_shape=[_sds((T, D_MODEL), F32), _sds((1, 1), F32), _sds((1, D_MODEL), F32)],
        compiler_params=_cp(("arbitrary",), 48),
    )(*_hbm(gate, val, x1, target, w_down, g_final))


def _ffn_bwd_act(gate, val, dx2, w_down, tm):
    T = dx2.shape[0]
    half = D_FF // 2
    nt = T // tm

    def body(g_ref, v_ref, dx_ref, wd_ref, dg_ref, dv_ref, gwd_out, gbg_ref, gbv_ref, gwd_ref):
        i = pl.program_id(1)

        @pl.when(i == 0)
        def _():
            for r in (gwd_ref, gbg_ref, gbv_ref):
                r[...] = jnp.zeros_like(r)

        dx = dx_ref[...].astype(BF16)
        for c0 in range(0, half, COL_CHUNK):
            cs = slice(c0, min(c0 + COL_CHUNK, half))
            gate = g_ref[:, cs].astype(F32)
            val = v_ref[:, cs].astype(F32)
            sg = _sigmoid(gate)
            silu = gate * sg
            d_act = _dot_nt(dx, wd_ref[cs, :])
            d_val = d_act * silu
            d_gate = d_act * val * (sg * (1.0 + gate * (1.0 - sg)))
            dg_ref[:, cs] = d_gate.astype(BF16)
            dv_ref[:, cs] = d_val.astype(BF16)
            gwd_ref[cs, :] += _dot_tn((silu * val).astype(BF16), dx)
            gbg_ref[:, cs] += jnp.sum(d_gate, axis=0, keepdims=True)
            gbv_ref[:, cs] += jnp.sum(d_val, axis=0, keepdims=True)

        @pl.when(i == nt - 1)
        def _():
            gwd_out[...] = gwd_ref[...].astype(BF16)

    tile = pl.BlockSpec((tm, half), lambda j, i: (i, j))
    vec = pl.BlockSpec((1, half), lambda j, i: (0, j))
    wrows = pl.BlockSpec((half, D_MODEL), lambda j, i: (j, 0))
    return pl.pallas_call(
        body, name="ffn_bwd_act", grid=(2, nt),
        in_specs=[tile, tile, pl.BlockSpec((tm, D_MODEL), lambda j, i: (i, 0)), wrows],
        out_specs=[tile, tile, wrows, vec, vec],
        out_shape=[_sds((T, D_FF), BF16), _sds((T, D_FF), BF16), _sds((D_FF, D_MODEL), BF16),
                   _sds((1, D_FF), F32), _sds((1, D_FF), F32)],
        scratch_shapes=[pltpu.VMEM((half, D_MODEL), F32)],
        compiler_params=_cp(("arbitrary", "arbitrary"), 56),
    )(*_hbm(gate, val, dx2, w_down))


def _ffn_bwd_up(d_gate, d_val, upre, dx2, x1, g_ffn, w_conv, w_up, tm, seq):
    T = dx2.shape[0]
    tiles_per_seq = seq // tm
    k16 = tm // BF16_ROWS
    n16 = T // BF16_ROWS
    cw = D_FF // 2

    def body(dg_ref, dv_ref, hg_ref, hv_ref, u_ref, dx2_ref, x1_ref, g_ref, wc_ref, wu_ref, du_ref, dx1_ref, gg_ref, gwc_ref):
        i = pl.program_id(0)
        at_end = (i % tiles_per_seq) == tiles_per_seq - 1

        @pl.when(i == 0)
        def _():
            gg_ref[...] = jnp.zeros_like(gg_ref)
            gwc_ref[...] = jnp.zeros_like(gwc_ref)

        dh = jnp.zeros((tm, D_MODEL), F32)
        for j in range(4):
            src, hsrc = (dg_ref, hg_ref) if j < 2 else (dv_ref, hv_ref)
            ls = slice((j % 2) * cw, (j % 2 + 1) * cw)
            cs = slice(j * cw, (j + 1) * cw)
            d = src[:, ls].astype(F32)
            hl = hsrc[:, ls].astype(F32)[0:2]
            hl = jnp.where(at_end, 0.0, hl)
            wc = wc_ref[:, cs]
            d1 = _shift_up(d, hl, 1)
            d2 = _shift_up(d, hl, 2)
            du = (wc[2:3] * d + wc[1:2] * d1 + wc[0:1] * d2).astype(BF16)
            du_ref[:, cs] = du
            dh = dh + _dot_nt(du, wu_ref[j])
            u = u_ref[:, cs].astype(F32)
            gwc_ref[0:1, cs] += jnp.sum(d2 * u, axis=0, keepdims=True)
            gwc_ref[1:2, cs] += jnp.sum(d1 * u, axis=0, keepdims=True)
            gwc_ref[2:3, cs] += jnp.sum(d * u, axis=0, keepdims=True)
        x = x1_ref[...]
        r = _rms_r(x)
        n = x * r
        dx1_ref[...] = dx2_ref[...] + _rms_bwd(dh, n, r, g_ref[...])
        gg_ref[...] += jnp.sum(dh * n, axis=0, keepdims=True)

    nxt = pl.BlockSpec((BF16_ROWS, D_FF), lambda i: (jnp.minimum((i + 1) * k16, n16 - 1), 0))
    return pl.pallas_call(
        body, name="ffn_bwd_up", grid=(T // tm,),
        in_specs=[_row(tm, D_FF), _row(tm, D_FF), nxt, nxt, _row(tm, 2 * D_FF), _row(tm, D_MODEL), _row(tm, D_MODEL),
                  _full(g_ffn.shape), _full(w_conv.shape), _resident(w_up.shape)],
        out_specs=[_row(tm, 2 * D_FF), _row(tm, D_MODEL), _full((1, D_MODEL)), _full((3, 2 * D_FF))],
        out_shape=[_sds((T, 2 * D_FF), BF16), _sds((T, D_MODEL), F32), _sds((1, D_MODEL), F32), _sds((3, 2 * D_FF), F32)],
        compiler_params=_cp(("arbitrary",), 56),
    )(*_hbm(d_gate, d_val, d_gate, d_val, upre, dx2, x1, g_ffn, w_conv, w_up))


def _matmul_tn(a, b, tn, tk, name):
    T, M = a.shape
    N = b.shape[1]
    nk = T // tk

    def body(a_ref, b_ref, o_ref, acc_ref):
        k = pl.program_id(1)

        @pl.when(k == 0)
        def _():
            acc_ref[...] = jnp.zeros_like(acc_ref)

        acc_ref[...] += _dot_tn(a_ref[...], b_ref[...])

        @pl.when(k == nk - 1)
        def _():
            o_ref[...] = acc_ref[...].astype(BF16)

    return pl.pallas_call(
        body, name=name, grid=(N // tn, nk),
        in_specs=[pl.BlockSpec((tk, M), lambda j, k: (k, 0)), pl.BlockSpec((tk, tn), lambda j, k: (k, j))],
        out_specs=pl.BlockSpec((M, tn), lambda j, k: (0, j)), out_shape=_sds((M, N), BF16),
        scratch_shapes=[pltpu.VMEM((M, tn), F32)],
        compiler_params=_cp(("arbitrary", "arbitrary"), 48),
    )(*_hbm(a, b))


def _merge_bwd(dx1, merged, y_a, y_b, proj_g, proj_a, w_pa, w_pb, w_out, g_sgu, w_s, b_st, tm, after=None):
    T = dx1.shape[0]

    nt = T // tm
    pshape = (A_WIDTH, D_MODEL)
    order = [] if after is None else [after]

    def body(*refs):
        dx_ref, mg_ref, ya_ref, yb_ref, g_ref, p_ref, wpa_ref, wpb_ref, wo_ref, gs_ref, ws_ref, bs_ref = refs[:12]
        (dg_ref, da_ref, dyb_ref, gwo_out, gwpa_out, gwpb_out, gws_ref, gbs_ref, gg_ref,
         gwo_ref, gwpa_ref, gwpb_ref) = refs[12 + len(order):]
        i = pl.program_id(0)

        @pl.when(i == 0)
        def _():
            for r in (gwo_ref, gwpa_ref, gwpb_ref, gws_ref, gbs_ref, gg_ref):
                r[...] = jnp.zeros_like(r)

        dx = dx_ref[...].astype(BF16)
        dm = _dot_nt(dx, wo_ref[...])
        g = g_ref[...].astype(F32)
        ya = ya_ref[...]
        yb = yb_ref[...]
        pa = _dot_stacked(ya, wpa_ref)
        pb = _dot_stacked(yb, wpb_ref)
        sa = _sigmoid(g[:, :D_MODEL])
        sb = _sigmoid(g[:, D_MODEL:])
        dpa = (dm * sa).astype(BF16)
        dpb = (dm * sb).astype(BF16)
        dg_ref[:, :D_MODEL] = (dm * pa * (sa * (1.0 - sa))).astype(BF16)
        dg_ref[:, D_MODEL:] = (dm * pb * (sb * (1.0 - sb))).astype(BF16)
        d_ya = _dot_nt_stacked(dpa, wpa_ref).astype(BF16)
        dyb_ref[...] = _dot_nt_stacked(dpb, wpb_ref).astype(BF16)
        _sgu_bwd_apply(p_ref[...].astype(F32), d_ya.astype(F32), gs_ref[...], ws_ref, bs_ref, da_ref, gws_ref, gbs_ref, gg_ref)
        gwo_ref[...] += _dot_tn(mg_ref[...], dx)
        gwpa_ref[...] += _dot_tn(ya, dpa)
        gwpb_ref[...] += _dot_tn(yb, dpb)

        @pl.when(i == nt - 1)
        def _():
            gwo_out[...] = gwo_ref[...].astype(BF16)
            gwpa_out[...] = gwpa_ref[...].astype(BF16)
            gwpb_out[...] = gwpb_ref[...].astype(BF16)

    return pl.pallas_call(
        body, name="merge_bwd", grid=(nt,),
        in_specs=[_row(tm, D_MODEL), _row(tm, D_MODEL), _row(tm, A_WIDTH), _row(tm, Q_DIM), _row(tm, G_DIM), _row(tm, A_DIM),
                  _resident(w_pa.shape), _resident(w_pb.shape), _resident(w_out.shape),
                  _full(g_sgu.shape), _full(w_s.shape), _full(b_st.shape)] + [ANY] * len(order),
        out_specs=[_row(tm, G_DIM), _row(tm, A_DIM), _row(tm, Q_DIM),
                   _full(w_out.shape), _full(pshape), _full(pshape), _full(w_s.shape), _full(b_st.shape), _full(g_sgu.shape)],
        out_shape=[_sds((T, G_DIM), BF16), _sds((T, A_DIM), BF16), _sds((T, Q_DIM), BF16),
                   _sds(w_out.shape, BF16), _sds(pshape, BF16), _sds(pshape, BF16),
                   _sds(w_s.shape, F32), _sds(b_st.shape, F32), _sds(g_sgu.shape, F32)],
        scratch_shapes=[pltpu.VMEM(w_out.shape, F32), pltpu.VMEM(pshape, F32), pltpu.VMEM(pshape, F32)],
        compiler_params=_cp(("arbitrary",), 56),
    )(*_hbm(dx1, merged, y_a, y_b, proj_g, proj_a, w_pa, w_pb, w_out, g_sgu, w_s, b_st), *order)


def _sgu_bwd_apply(p, dy, g, ws_ref, bs_ref, dp_ref, gws_ref, gbs_ref, gg_ref):
    tril = _tril()
    pu, pv, u, tu, vv, tv, rv, vn = _sgu_parts(p, g)
    du_cols = []
    dvn_cols = []
    for gi in range(A_GROUPS):
        wm = jnp.where(tril, ws_ref[gi], 0.0).astype(BF16)
        wmt = wm.astype(F32).T.astype(BF16)
        bcol = bs_ref[:, gi:gi + 1]
        cs = slice(gi * CHUNK, (gi + 1) * CHUNK)
        du_rows = []
        dvn_rows = []
        gw = jnp.zeros((CHUNK, CHUNK), F32)
        gb = jnp.zeros((CHUNK, 1), F32)
        for c in range(p.shape[0] // CHUNK):
            rs = slice(c * CHUNK, (c + 1) * CHUNK)
            vn_c = vn[rs, cs]
            s = _dot(wm, vn_c) + bcol
            dy_c = dy[rs, cs]
            ds = dy_c * u[rs, cs]
            du_rows.append(dy_c * s)
            dsb = ds.astype(BF16)
            gw = gw + _dot_nt(dsb, vn_c)
            gb = gb + jnp.sum(ds, axis=-1, keepdims=True)
            dvn_rows.append(_dot(wmt, dsb))
        gws_ref[gi] += jnp.where(tril, gw, 0.0)
        gbs_ref[:, gi:gi + 1] += gb
        du_cols.append(jnp.concatenate(du_rows, axis=0))
        dvn_cols.append(jnp.concatenate(dvn_rows, axis=0))
    du = jnp.concatenate(du_cols, axis=1)
    dvn = jnp.concatenate(dvn_cols, axis=1)
    vhat = vv * rv
    gg_ref[...] += jnp.sum(dvn * vhat, axis=0, keepdims=True)
    dvv = _rms_bwd(dvn, vhat, rv, g)
    dp_ref[:, :A_WIDTH] = (du * _gelu_grad(pu, tu)).astype(BF16)
    dp_ref[:, A_WIDTH:] = (dvv * _gelu_grad(pv, tv)).astype(BF16)


def _attn_bwd(proj_b, d_yb, sinks, rel_bias, n_seq, seq, after=None):
    nb = seq // CHUNK
    bk = jnp.asarray(_band_buckets())
    order = [] if after is None else [after]

    def body(*refs):
        qkv_ref, do_ref, bk_ref, rel_ref, sink_ref = refs[:5]
        (d_ref, gs_ref, gr_ref, bias_scr, sink_scr, kvar_scr, dbias_scr, dk_scr, dv_scr, ds_scr) = refs[5 + len(order):]
        b = pl.program_id(0)
        _attn_setup(bias_scr, sink_scr, kvar_scr, qkv_ref, bk_ref, rel_ref, sink_ref)
        ones = jnp.ones((2 * CHUNK, LANES), BF16)

        @pl.when(b == 0)
        def _():
            dbias_scr[...] = jnp.zeros_like(dbias_scr)
            ds_scr[...] = jnp.zeros_like(ds_scr)

        dk_scr[...] = jnp.zeros_like(dk_scr)
        dv_scr[...] = jnp.zeros_like(dv_scr)

        def transposed(a):
            return a.astype(F32).T.astype(BF16)

        def blk(n, carry):
            r0, kv, vv = _attn_block_inputs(kvar_scr, n)
            prob, psink = _attn_probs(qkv_ref, r0, n, kv, bias_scr, sink_scr, ones)
            dp = jnp.concatenate([_dot_nt(do_ref[pl.ds(r0, CHUNK), (h // 2) * LANES:(h // 2 + 1) * LANES], vv[h // 4][h % 2])
                                  for h in range(N_HEADS)], axis=0)
            delta = _rowsum(prob * dp, ones)
            dsc = prob * (dp - _both(delta))
            ds_scr[...] += psink * delta
            dbias_scr[...] += dsc
            dsb = (dsc * (HEAD_DIM ** -0.5)).astype(BF16)
            pb = prob.astype(BF16)
            dkt = [jnp.zeros((HEAD_DIM, 2 * CHUNK), F32) for _ in range(2)]
            dvt = [jnp.zeros((HEAD_DIM, 2 * CHUNK), F32) for _ in range(2)]
            for pr in range(N_HEADS // 2):
                ps = slice(pr * LANES, (pr + 1) * LANES)
                qpt = transposed(qkv_ref[pl.ds(r0, CHUNK), ps])
                dopt = transposed(do_ref[pl.ds(r0, CHUNK), ps])
                kvh = pr // 2
                dq = jnp.zeros((CHUNK, LANES), F32)
                for hh in range(2):
                    hr = _head_rows(2 * pr + hh)
                    rows = slice(hh * HEAD_DIM, (hh + 1) * HEAD_DIM)
                    dq = dq + _dot(dsb[hr], kv[kvh][hh])
                    dkt[kvh] = dkt[kvh] + _dot(qpt, dsb[hr])[rows]
                    dvt[kvh] = dvt[kvh] + _dot(dopt, pb[hr])[rows]
                d_ref[pl.ds(r0, CHUNK), ps] = dq.astype(BF16)
            dk_scr[:, pl.ds(r0, 2 * CHUNK)] += jnp.concatenate(dkt, axis=0)
            dv_scr[:, pl.ds(r0, 2 * CHUNK)] += jnp.concatenate(dvt, axis=0)
            return carry

        lax.fori_loop(0, nb, blk, 0)
        for n in range(nb):
            rows = slice(n * CHUNK, (n + 1) * CHUNK)
            cols = slice((n + 1) * CHUNK, (n + 2) * CHUNK)
            d_ref[rows, Q_DIM:Q_DIM + KV_DIM] = dk_scr[:, cols].T.astype(BF16)
            d_ref[rows, Q_DIM + KV_DIM:] = dv_scr[:, cols].T.astype(BF16)

        @pl.when(b == n_seq - 1)
        def _():
            bkv = bk_ref[...]
            for h in range(N_HEADS):
                gs_ref[0:1, h:h + 1] = -jnp.sum(ds_scr[_head_rows(h), 0:1], axis=0, keepdims=True)
                db = dbias_scr[_head_rows(h), :]
                for bb in range(N_BUCKETS):
                    part = jnp.sum(jnp.where(bkv == bb, db, 0.0), axis=-1, keepdims=True)
                    gr_ref[bb:bb + 1, h:h + 1] = jnp.sum(part, axis=0, keepdims=True)

    smem = pl.BlockSpec(memory_space=pltpu.SMEM)
    return pl.pallas_call(
        body, name="attn_bwd", grid=(n_seq,),
        in_specs=[_row(seq, B_DIM), _row(seq, Q_DIM), _full(bk.shape), smem, smem] + [ANY] * len(order),
        out_specs=[_row(seq, B_DIM), _full((1, N_HEADS)), _full((N_BUCKETS, N_HEADS))],
        out_shape=[_sds((n_seq * seq, B_DIM), BF16), _sds((1, N_HEADS), F32), _sds((N_BUCKETS, N_HEADS), F32)],
        scratch_shapes=[pltpu.VMEM((HEAD_ROWS, 2 * CHUNK), F32), pltpu.VMEM((HEAD_ROWS, LANES), F32),
                        pltpu.VMEM((8, seq, KV_DIM), BF16), pltpu.VMEM((HEAD_ROWS, 2 * CHUNK), F32),
                        pltpu.VMEM((KV_DIM, seq + CHUNK), F32), pltpu.VMEM((KV_DIM, seq + CHUNK), F32),
                        pltpu.VMEM((HEAD_ROWS, LANES), F32)],
        compiler_params=_cp(("arbitrary",), 40),
    )(*_hbm(proj_b, d_yb, bk), rel_bias, sinks, *order)


def _inproj_bwd(d_g, d_a, d_b, x2, dx1, g_mix, w_in, tm, after=None):
    T = x2.shape[0]
    sub = min(tm, 128)
    order = [] if after is None else [after]

    def body(*refs):
        dg_ref, da_ref, db_ref, x_ref, dx1_ref, g_ref, w_ref = refs[:7]
        gx_ref, gg_ref = refs[7 + len(order):]
        subs = [slice(s0, s0 + sub) for s0 in range(0, tm, sub)]
        dhs = [_dot(dg_ref[rs, :], w_ref[_G_COLS, :]) + _dot(da_ref[rs, :], w_ref[_A_COLS, :])
               + _dot(db_ref[rs, :], w_ref[_B_COLS, :]) for rs in subs]
        gg = jnp.zeros((1, D_MODEL), F32)
        for rs, dh in zip(subs, dhs):
            x = x_ref[rs, :]
            r = _rms_r(x)
            n = x * r
            gx_ref[rs, :] = dx1_ref[rs, :] + _rms_bwd(dh, n, r, g_ref[...])
            gg = gg + jnp.sum(dh * n, axis=0, keepdims=True)

        @pl.when(pl.program_id(0) == 0)
        def _():
            gg_ref[...] = jnp.zeros_like(gg_ref)

        gg_ref[...] += gg

    return pl.pallas_call(
        body, name="inproj_bwd", grid=(T // tm,),
        in_specs=[_row(tm, G_DIM), _row(tm, A_DIM), _row(tm, B_DIM), _row(tm, D_MODEL), _row(tm, D_MODEL),
                  _full(g_mix.shape), _resident(w_in.shape)] + [ANY] * len(order),
        out_specs=[_row(tm, D_MODEL), _full((1, D_MODEL))],
        out_shape=[_sds((T, D_MODEL), F32), _sds((1, D_MODEL), F32)],
        compiler_params=_cp(("arbitrary",), 48),
    )(*_hbm(d_g, d_a, d_b, x2, dx1, g_mix, w_in), *order)


IN_SHARD = (A_DIM + B_DIM + G_DIM) // N_CHIPS


def _grad_w_in(h, d_a, d_b, d_g, tk):
    T = h.shape[0]
    nk = T // tk
    in_dim = N_CHIPS * IN_SHARD

    def body(h_ref, da_ref, db_ref, dg_ref, o_ref, acc_ref):
        k = pl.program_id(0)

        @pl.when(k == 0)
        def _():
            acc_ref[...] = jnp.zeros_like(acc_ref)

        hb = h_ref[...]
        acc_ref[:, _A_COLS] += _dot_tn(hb, da_ref[...])
        acc_ref[:, _B_COLS] += _dot_tn(hb, db_ref[...])
        acc_ref[:, _G_COLS] += _dot_tn(hb, dg_ref[...])

        @pl.when(k == nk - 1)
        def _():
            for i in range(N_CHIPS):
                o_ref[i] = acc_ref[:, i * IN_SHARD:(i + 1) * IN_SHARD].astype(BF16)

    return pl.pallas_call(
        body, name="grad_w_in", grid=(nk,),
        in_specs=[_row(tk, D_MODEL), _row(tk, A_DIM), _row(tk, B_DIM), _row(tk, G_DIM)],
        out_specs=_full((N_CHIPS, D_MODEL, IN_SHARD)), out_shape=_sds((N_CHIPS, D_MODEL, IN_SHARD), BF16),
        scratch_shapes=[pltpu.VMEM((D_MODEL, in_dim), F32)],
        compiler_params=_cp(("arbitrary",), 56),
    )(*_hbm(h, d_a, d_b, d_g))


def _local_step(x, target, g_mix, g_sgu, w_s, b_s, sinks, rel_bias, g_ffn, b_conv, g_final,
                w_in, w_conv, proj_weights, ffn_weights, on_grads, after=None):
    n_seq, seq, _ = x.shape
    T = n_seq * seq
    tm = min(ROW_TILE, seq)
    tw = min(GRAD_ROW_TILE, T)
    tf = min(WIDE_ROW_TILE, seq)
    x2 = x.reshape(T, D_MODEL)
    tgt = target.reshape(T, D_MODEL)
    b_st = b_s.T
    g_fin = g_final.reshape(1, D_MODEL)

    proj_g, proj_a, proj_b, h, y_a = _inproj(x2, g_mix, w_in, g_sgu, w_s, b_st, tm, after)
    y_b = _attn_fwd(proj_b, sinks, rel_bias, n_seq, seq)
    w_pa, w_pb, w_out = proj_weights(y_b)
    x1, merged = _merge_fwd(x2, y_a, y_b, proj_g, w_pa, w_pb, w_out, tm)
    w_up, w_down = ffn_weights(x1)
    upre, h2, gate, val = _upproj(x1, g_ffn, w_up, w_conv, b_conv, tf, seq)
    dx2, loss, gg_final = _ffn_down_loss(gate, val, x1, tgt, w_down, g_fin, tm)

    d_gate, d_val, gw_down, gb_g, gb_v = _ffn_bwd_act(gate, val, dx2, w_down, tw)
    gb_conv = jnp.concatenate([gb_g, gb_v], axis=1)
    d_upre, dx1, gg_ffn, gw_conv = _ffn_bwd_up(d_gate, d_val, upre, dx2, x1, g_ffn, w_conv, w_up, tf, seq)
    gw_up = _matmul_tn(h2, d_upre, 2 * D_FF // 4, min(4 * GRAD_ROW_TILE, T), "grad_w_up")
    sent = on_grads("ffn", dict(w_up=gw_up, w_down=gw_down))
    d_g, d_a, d_yb, gw_out, gw_pa, gw_pb, gw_s, gb_st, gg_sgu = _merge_bwd(
        dx1, merged, y_a, y_b, proj_g, proj_a, w_pa, w_pb, w_out, g_sgu, w_s, b_st, tw, sent)
    sent = on_grads("proj", dict(w_pa=gw_pa, w_pb=gw_pb, w_out=gw_out))
    d_b, g_sinks, g_rel = _attn_bwd(proj_b, d_yb, sinks, rel_bias, n_seq, seq, sent)
    gw_in = _grad_w_in(h, d_a, d_b, d_g, min(2 * GRAD_ROW_TILE, T))
    sent = on_grads("in", dict(w_in=gw_in))
    grad_x, gg_mix = _inproj_bwd(d_g, d_a, d_b, x2, dx1, g_mix, w_in, tm, sent)

    small = dict(g_mix=gg_mix, g_sgu=gg_sgu, w_s=gw_s, b_s=gb_st.T, sinks=g_sinks, rel_bias=g_rel,
                 g_ffn=gg_ffn, b_conv=gb_conv, g_final=gg_final, w_conv=gw_conv)
    big = dict(w_in=gw_in, w_pa=gw_pa, w_pb=gw_pb, w_out=gw_out, w_up=gw_up, w_down=gw_down)
    return loss, grad_x.reshape(x.shape), small, big


_MIXER = ("w_in", "w_pa", "w_pb", "w_out")
_FFN = ("w_up", "w_down")
_BIG = _MIXER + _FFN

CONV_ROWS = 6
_SMALL_AT = dict(loss=(0, 1, 1), g_sgu=(4, 1, A_WIDTH), sinks=(5, 1, N_HEADS), b_s=(8, A_GROUPS, CHUNK),
                 b_conv=(12, CONV_ROWS, D_MODEL), w_conv=(18, 3 * CONV_ROWS, D_MODEL),
                 g_final=(36, 1, D_MODEL), g_mix=(37, 1, D_MODEL), g_ffn=(38, 1, D_MODEL),
                 w_s=(40, A_GROUPS * CHUNK * CHUNK // D_MODEL, D_MODEL))
_REL_BIAS_AT = (0, A_WIDTH)
_SMALL_IN_CALL = ("g_final", "g_mix", "g_ffn", "g_sgu", "sinks", "b_s", "b_conv", "rel_bias", "w_s")
SMALL_ROWS = 104


def _pack_small(vals):
    def wide(a):
        return jnp.pad(a, ((0, 0), (0, CONV_ROWS * D_MODEL - a.shape[1]))).reshape(-1, D_MODEL)

    nr = _SMALL_AT["w_s"][1]
    w_s = vals["w_s"].reshape(D_MODEL // CHUNK, nr, CHUNK).transpose(1, 0, 2).reshape(nr, D_MODEL)
    laid = dict(vals, b_conv=wide(vals["b_conv"]), w_conv=wide(vals["w_conv"]), w_s=w_s)
    rows, at = [], 0
    for n, (r0, nr, nc) in _SMALL_AT.items():
        if r0 > at:
            rows.append(jnp.zeros((r0 - at, D_MODEL), F32))
        rows.append(jnp.pad(laid[n].astype(F32).reshape(nr, nc), ((0, 0), (0, D_MODEL - nc))))
        at = r0 + nr
    return lax.dynamic_update_slice(jnp.concatenate(rows, axis=0), vals["rel_bias"].T, _REL_BIAS_AT)


def _unwide(a, r):
    return a.reshape(r, CONV_ROWS * D_MODEL)[:, :2 * D_FF]


def _mesh_pos():
    return lax.axis_index("x"), lax.axis_index("y"), lax.axis_index("c")


def _other_chips(x, y):
    return [(1 - x, y), (x, 1 - y), (1 - x, 1 - y)]


def _remote(src, dst, send_sem, recv_sem, to):
    return pltpu.make_async_remote_copy(src_ref=src, dst_ref=dst, send_sem=send_sem, recv_sem=recv_sem,
                                        device_id=to, device_id_type=MESH)


def _own_slot(own, n, at):
    return lax.dynamic_update_slice(lax.empty((n,) + own.shape, own.dtype), own[None], (at,) + (0,) * own.ndim)


def _allgather_weights(stacks, wc_stack):
    names = list(stacks)
    n = len(names)

    def body(*refs):
        ins, outs = refs[:n + 1], refs[n + 1:2 * n + 2]
        send_sems, recv_sems = refs[2 * n + 2:]
        x, y, c = _mesh_pos()
        _handshake(_chip_peers(x, y, c) + _sibling_peers(x, y, c))
        me = 2 * x + y
        sibling = (x, y, 1 - c)
        chips = _other_chips(x, y)

        def half(ref, chip, hc):
            hr = ref.shape[1] // 2
            return ref.at[chip, pl.ds(hc * hr, hr), :]

        first = []
        for k in range(n):
            first += [_remote(half(ins[k], me, c), half(outs[k], me, c), send_sems.at[6 * k + j], recv_sems.at[6 * k + j], (cx, cy, c))
                      for j, (cx, cy) in enumerate(chips)]
        first += [_remote(ins[n].at[me], outs[n].at[me], send_sems.at[6 * n + j], recv_sems.at[6 * n + j], (cx, cy, c))
                  for j, (cx, cy) in enumerate(chips)]
        for cp in first:
            cp.start()
        passed = []
        for k in range(n):
            for j, (cx, cy) in enumerate(chips):
                landed = half(outs[k], 2 * cx + cy, c)
                _remote(landed, landed, send_sems.at[6 * k + j], recv_sems.at[6 * k + j], (x, y, c)).wait_recv()
                passed.append(_remote(landed, landed, send_sems.at[6 * k + 3 + j], recv_sems.at[6 * k + 3 + j], sibling))
                passed[-1].start()
        for k in range(n):
            for j, (cx, cy) in enumerate(chips):
                theirs = half(outs[k], 2 * cx + cy, 1 - c)
                _remote(theirs, theirs, send_sems.at[6 * k + 3 + j], recv_sems.at[6 * k + 3 + j], (x, y, c)).wait_recv()
        for j, (cx, cy) in enumerate(chips):
            slot = outs[n].at[2 * cx + cy]
            _remote(slot, slot, send_sems.at[6 * n + j], recv_sems.at[6 * n + j], (x, y, c)).wait_recv()
        for cp in first + passed:
            cp.wait_send()

    arrays = [stacks[k] for k in names] + [wc_stack]
    outs = pl.pallas_call(
        body, name="allgather_weights",
        in_specs=[HBM] * (n + 1), out_specs=[HBM] * (n + 1), input_output_aliases={k: k for k in range(n + 1)},
        out_shape=[_sds(a.shape, a.dtype) for a in arrays],
        scratch_shapes=[pltpu.SemaphoreType.DMA((6 * n + 3,)), pltpu.SemaphoreType.DMA((6 * n + 3,))],
        compiler_params=pltpu.CompilerParams(collective_id=_COLLECTIVE["gather_in"]),
    )(*arrays)
    return dict(zip(names, outs[:n])), outs[n]


_KIND = {"w_in": "stack", "w_pa": "col", "w_pb": "col", "w_up": "col", "w_out": "row", "w_down": "row"}


def _half_view(ref, kind, h):
    if kind == "stack":
        k = ref.shape[1] // 2
        return ref.at[:, pl.ds(h * k, k), :]
    if kind == "col":
        k = ref.shape[0] // 2
        return ref.at[pl.ds(h * k, k), :]
    k = ref.shape[1] // 2
    return ref.at[:, pl.ds(h * k, k)]


def _shard_view(ref, kind, i):
    if kind == "stack":
        return ref.at[i]
    if kind == "col":
        k = ref.shape[1] // N_CHIPS
        return ref.at[:, pl.ds(i * k, k)]
    k = ref.shape[0] // N_CHIPS
    return ref.at[pl.ds(i * k, k), :]


def _region_view(ref, kind, h):
    if kind == "row":
        k = ref.shape[1] // 2
        return ref.at[:, pl.ds(h * k, k)]
    k = ref.shape[0] // 2
    return ref.at[pl.ds(h * k, k), :]


def _half_shape(shape, kind):
    if kind == "stack":
        return (shape[0], shape[1] // 2, shape[2])
    return (shape[0] // 2, shape[1]) if kind == "col" else (shape[0], shape[1] // 2)


def _part_shape(half_shape, kind):
    if kind == "stack":
        return tuple(half_shape[1:])
    k, w = half_shape
    return (k, w // N_CHIPS) if kind == "col" else (k // N_CHIPS, w)


_DATAFLOW = pltpu.SideEffectType.DATAFLOW_SIDE_EFFECTING
_TOKEN = (SUBLANES, LANES)


_COLLECTIVE = {k: i for i, k in enumerate(
    [kind + "_" + g for kind in ("pair", "chip", "share") for g in ("ffn", "proj", "in")]
    + ["gather_proj", "gather_ffn", "gather_in", "forward_proj", "forward_ffn"])}


def _sibling_peers(x, y, c):
    return [(x, y, 1 - c)]


def _chip_peers(x, y, c):
    return [(cx, cy, c) for cx, cy in _other_chips(x, y)]


def _handshake(peers):
    barrier = pltpu.get_barrier_semaphore()
    for peer in peers:
        pl.semaphore_signal(barrier, inc=1, device_id=peer, device_id_type=MESH)
    pl.semaphore_wait(barrier, len(peers))


def _split_start(name, arrays, n_sems, issue, after=None, handshake=None):
    n = len(arrays)
    order = [] if after is None else [after]

    def body(*refs):
        base = n + len(order)
        if handshake is not None:
            _handshake(handshake[1](*_mesh_pos()))
        issue(refs[:n], refs[base], refs[base + 1])
        refs[-1][...] = jnp.zeros(_TOKEN, F32)

    params = dict(has_side_effects=_DATAFLOW)
    if handshake is not None:
        params["collective_id"] = handshake[0]
    outs = pl.pallas_call(
        body, name=name,
        in_specs=[HBM] * n + [ANY] * len(order), out_specs=[SEM, SEM] + [HBM] * n + [pl.BlockSpec(memory_space=pltpu.VMEM)],
        out_shape=[pltpu.SemaphoreType.DMA((n_sems,)), pltpu.SemaphoreType.DMA((n_sems,))]
        + [pltpu.HBM(a.shape, a.dtype) for a in arrays] + [_sds(_TOKEN, F32)],
        input_output_aliases={k: 2 + k for k in range(n)},
        compiler_params=pltpu.CompilerParams(**params),
    )(*[pltpu.with_memory_space_constraint(a, pltpu.HBM) for a in arrays], *order)
    return outs[0], outs[1], list(outs[2:2 + n]), outs[-1]


def _split_wait(name, started, waits, after):
    send_sems, recv_sems, arrays, _ = started
    n = len(arrays)

    def body(*refs):
        waits(refs[:n], refs[n], refs[n + 1])

    return pl.pallas_call(
        body, name=name,
        in_specs=[HBM] * n + [SEM, SEM, ANY], out_specs=[HBM] * n,
        out_shape=[pltpu.HBM(a.shape, a.dtype) for a in arrays],
        input_output_aliases={k: k for k in range(n)},
        compiler_params=pltpu.CompilerParams(has_side_effects=_DATAFLOW),
    )(*arrays, send_sems, recv_sems, after)


def _wait_both(src, dst, send_sem, recv_sem):
    x, y, c = _mesh_pos()
    cp = _remote(src, dst, send_sem, recv_sem, (x, y, c))
    cp.wait_send()
    cp.wait_recv()


def _pair_exchange_start(parts, tag, after):
    names = list(parts)
    n = len(names)
    lands = [lax.empty(_half_shape(parts[k].shape, _KIND[k]), parts[k].dtype) for k in names]

    def issue(refs, send_sems, recv_sems):
        x, y, c = _mesh_pos()
        for hc in range(2):
            @pl.when(c == hc)
            def _():
                for k in range(n):
                    _remote(_half_view(refs[k], _KIND[names[k]], 1 - hc), refs[n + k], send_sems.at[k], recv_sems.at[k],
                            (x, y, 1 - c)).start()

    return names, _split_start("grad_pair_exchange_start_" + tag, [parts[k] for k in names] + lands, n, issue, after,
                               (_COLLECTIVE["pair_" + tag], _sibling_peers))


def _pair_exchange_wait(pending, tag, after):
    names, started = pending
    n = len(names)

    def waits(refs, send_sems, recv_sems):
        for k in range(n):
            _wait_both(_half_view(refs[k], _KIND[names[k]], 0), refs[n + k], send_sems.at[k], recv_sems.at[k])

    outs = _split_wait("grad_pair_exchange_wait_" + tag, started, waits, after)
    return dict(zip(names, outs[:n])), dict(zip(names, outs[n:]))


def _pair_add(part, from_sibling, name, pos):
    assert _KIND[name] == "stack"
    _, k, w = part.shape
    block = (2, k // 2, w)

    def body(s_ref, p_ref, q_ref, o_ref):
        o_ref[...] = (p_ref[...].astype(F32) + q_ref[...].astype(F32)).astype(BF16)

    return pl.pallas_call(
        body, name="grad_pair_add_" + name,
        grid_spec=pltpu.PrefetchScalarGridSpec(
            num_scalar_prefetch=1, grid=(N_CHIPS // 2,),
            in_specs=[pl.BlockSpec(block, lambda i, s: (i, s[1], 0)), pl.BlockSpec(block, lambda i, s: (i, 0, 0))],
            out_specs=pl.BlockSpec(block, lambda i, s: (i, 0, 0))),
        out_shape=_sds(from_sibling.shape, BF16),
        compiler_params=_cp(("arbitrary",), 40),
    )(pos, *_hbm(part, from_sibling))


def _pair_add_group(parts, from_sibling, tag, pos):
    names = list(parts)
    n = len(names)
    full_specs, half_specs = [], []
    for name in names:
        k, w = parts[name].shape
        if _KIND[name] == "col":
            block, full_map = (k // 4, w), (lambda r, s: (2 * s[1] + r, 0))
        else:
            block, full_map = (k // 2, w // 2), (lambda r, s: (r, s[1]))
        full_specs.append(pl.BlockSpec(block, full_map))
        half_specs.append(pl.BlockSpec(block, lambda r, s: (r, 0)))

    def body(s_ref, *refs):
        for k in range(n):
            refs[2 * n + k][...] = (refs[k][...].astype(F32) + refs[n + k][...].astype(F32)).astype(BF16)

    outs = pl.pallas_call(
        body, name="grad_pair_add_" + tag,
        grid_spec=pltpu.PrefetchScalarGridSpec(
            num_scalar_prefetch=1, grid=(2,), in_specs=full_specs + half_specs, out_specs=half_specs),
        out_shape=[_sds(from_sibling[k].shape, BF16) for k in names],
        compiler_params=_cp(("arbitrary",), 40),
    )(pos, *_hbm(*[parts[k] for k in names], *[from_sibling[k] for k in names]))
    return dict(zip(names, outs))


def _owner_sum_group(parts, from_sibling, from_chips, tag, pos, shard_shapes):
    names = list(parts)
    n = len(names)
    p_specs, q_specs, r_specs, o_specs = [], [], [], []
    for name in names:
        _, pk, pw = from_chips[name].shape
        block = (pk // 2, pw)
        if _KIND[name] == "row":
            maps = (lambda r, s: (2 * s[0] + r, s[1])), (lambda r, s: (2 * s[0] + r, 0)), (lambda r, s: (r, s[1]))
        else:
            maps = (lambda r, s: (2 * s[1] + r, s[0])), (lambda r, s: (r, s[0])), (lambda r, s: (2 * s[1] + r, 0))
        p_specs.append(pl.BlockSpec(block, maps[0]))
        q_specs.append(pl.BlockSpec(block, maps[1]))
        o_specs.append(pl.BlockSpec(block, maps[2]))
        r_specs.append(pl.BlockSpec((3,) + block, lambda r, s: (0, r, 0)))

    def body(s_ref, *refs):
        for k in range(n):
            acc = refs[k][...].astype(F32) + refs[n + k][...].astype(F32)
            for j in range(3):
                acc = acc + refs[2 * n + k][j].astype(F32)
            refs[3 * n + k][...] = acc

    outs = pl.pallas_call(
        body, name="grad_owner_sum_" + tag,
        grid_spec=pltpu.PrefetchScalarGridSpec(
            num_scalar_prefetch=1, grid=(2,), in_specs=p_specs + q_specs + r_specs, out_specs=o_specs),
        out_shape=[_sds(shard_shapes[k], F32) for k in names],
        compiler_params=_cp(("arbitrary",), 32),
    )(pos, *_hbm(*[parts[k] for k in names], *[from_sibling[k] for k in names], *[from_chips[k] for k in names]))
    return dict(zip(names, outs))


def _chip_exchange_start(sums, tag, after):
    names = list(sums)
    n = len(names)
    lands = [lax.empty((3,) + _part_shape(sums[k].shape, _KIND[k]), sums[k].dtype) for k in names]

    def issue(refs, send_sems, recv_sems):
        x, y, c = _mesh_pos()
        me = 2 * x + y
        for i in range(N_CHIPS):
            xi, yi = i // 2, i % 2
            j = jnp.where(xi != x, jnp.where(yi != y, 2, 0), 1)

            @pl.when(i != me)
            def _():
                for k in range(n):
                    _remote(_shard_view(refs[k], _KIND[names[k]], i), refs[n + k].at[j], send_sems.at[3 * k + j],
                            recv_sems.at[3 * k + j], (xi, yi, c)).start()

    return names, _split_start("grad_chip_exchange_start_" + tag, [sums[k] for k in names] + lands, 3 * n, issue, after,
                               (_COLLECTIVE["chip_" + tag], _chip_peers))


def _chip_exchange_wait(pending, tag, after):
    names, started = pending
    n = len(names)

    def waits(refs, send_sems, recv_sems):
        for k in range(n):
            for j in range(3):
                _wait_both(_shard_view(refs[k], _KIND[names[k]], 0), refs[n + k].at[j], send_sems.at[3 * k + j], recv_sems.at[3 * k + j])

    return dict(zip(names, _split_wait("grad_chip_exchange_wait_" + tag, started, waits, after)[n:]))


def _allgather_start(stacks, tag, after):
    names = list(stacks)

    def issue(refs, send_sems, recv_sems):
        x, y, c = _mesh_pos()
        me = 2 * x + y
        for k, st in enumerate(refs):
            hr = st.shape[1] // 2
            mine = st.at[me, pl.ds(c * hr, hr), :]
            for j, (cx, cy) in enumerate(_other_chips(x, y)):
                _remote(mine, mine, send_sems.at[3 * k + j], recv_sems.at[3 * k + j], (cx, cy, c)).start()

    return names, _split_start("allgather_start_" + tag, [stacks[k] for k in names], 3 * len(names), issue, after,
                               (_COLLECTIVE["gather_" + tag], _chip_peers))


def _allgather_wait(pending, tag, after):
    names, started = pending

    def waits(refs, send_sems, recv_sems):
        for k, st in enumerate(refs):
            slot = st.at[0, pl.ds(0, st.shape[1] // 2), :]
            for j in range(3):
                _wait_both(slot, slot, send_sems.at[3 * k + j], recv_sems.at[3 * k + j])

    return dict(zip(names, _split_wait("allgather_wait_" + tag, started, waits, after)))


def _allgather_forward(stacks, tag):
    names = list(stacks)
    n = len(names)

    def body(*refs):
        ins, outs = refs[:n], refs[n:2 * n]
        send_sems, recv_sems = refs[2 * n:]
        x, y, c = _mesh_pos()
        _handshake(_sibling_peers(x, y, c))
        copies = []
        for k in range(n):
            hr = ins[k].shape[1] // 2
            for j, (cx, cy) in enumerate(_other_chips(x, y)):
                chip = 2 * cx + cy
                copies.append(_remote(ins[k].at[chip, pl.ds(c * hr, hr), :], outs[k].at[chip, pl.ds(c * hr, hr), :],
                                      send_sems.at[3 * k + j], recv_sems.at[3 * k + j], (x, y, 1 - c)))
        for cp in copies:
            cp.start()
        for cp in copies:
            cp.wait()

    arrays = [stacks[k] for k in names]
    outs = pl.pallas_call(
        body, name="allgather_forward_" + tag, in_specs=[HBM] * n, out_specs=[HBM] * n,
        input_output_aliases={k: k for k in range(n)},
        out_shape=[_sds(a.shape, a.dtype) for a in arrays],
        scratch_shapes=[pltpu.SemaphoreType.DMA((3 * n,)), pltpu.SemaphoreType.DMA((3 * n,))],
        compiler_params=pltpu.CompilerParams(collective_id=_COLLECTIVE["forward_" + tag]),
    )(*arrays)
    return dict(zip(names, outs))


def _owner_sum(part, from_sibling, from_chips, name, pos, shard_shape):
    assert _KIND[name] == "stack"
    _, pk, pw = from_chips.shape
    tr = STREAM_ROWS
    nb = pk // tr
    p_spec = pl.BlockSpec((None, tr, pw), lambda r, s: (s[0], s[1] * nb + r, 0))
    q_spec = pl.BlockSpec((None, tr, pw), lambda r, s: (s[0], r, 0))
    o_spec = pl.BlockSpec((tr, pw), lambda r, s: (s[1] * nb + r, 0))

    def body(s_ref, p_ref, q_ref, r_ref, o_ref):
        acc = p_ref[...].astype(F32) + q_ref[...].astype(F32)
        for j in range(3):
            acc = acc + r_ref[j].astype(F32)
        o_ref[...] = acc

    return pl.pallas_call(
        body, name="grad_owner_sum_" + name,
        grid_spec=pltpu.PrefetchScalarGridSpec(
            num_scalar_prefetch=1, grid=(nb,),
            in_specs=[p_spec, q_spec, pl.BlockSpec((3, tr, pw), lambda r, s: (0, r, 0))],
            out_specs=o_spec),
        out_shape=_sds(shard_shape, F32),
        compiler_params=_cp(("arbitrary",), 32),
    )(pos, *_hbm(part, from_sibling, from_chips))


def _pair_share_start(shards, tag, after):
    names = list(shards)

    def issue(refs, send_sems, recv_sems):
        x, y, c = _mesh_pos()
        for hc in range(2):
            @pl.when(c == hc)
            def _():
                for k, g in enumerate(refs):
                    mine = _region_view(g, _KIND[names[k]], hc)
                    _remote(mine, mine, send_sems.at[k], recv_sems.at[k], (x, y, 1 - c)).start()

    return names, _split_start("grad_pair_share_start_" + tag, [shards[k] for k in names], len(names), issue, after,
                               (_COLLECTIVE["share_" + tag], _sibling_peers))


def _pair_share_wait(pending, tag, after):
    names, started = pending

    def waits(refs, send_sems, recv_sems):
        for k, g in enumerate(refs):
            region = _region_view(g, _KIND[names[k]], 0)
            _wait_both(region, region, send_sems.at[k], recv_sems.at[k])

    return dict(zip(names, _split_wait("grad_pair_share_wait_" + tag, started, waits, after)))


def _small_exchange_start(slots, after):
    def issue(refs, send_sems, recv_sems):
        x, y, c = _mesh_pos()
        mine = refs[0].at[4 * x + 2 * y + c]
        k = 0
        for px in range(2):
            for py in range(2):
                for pc in range(2):
                    if px + py + pc:
                        peer = (1 - x if px else x, 1 - y if py else y, 1 - c if pc else c)
                        _remote(mine, mine, send_sems.at[k], recv_sems.at[k], peer).start()
                        k += 1

    return _split_start("small_exchange_start", [slots], N_DEV - 1, issue, after)


def _small_exchange_wait(started, after):
    def waits(refs, send_sems, recv_sems):
        slot = refs[0].at[0]
        for k in range(N_DEV - 1):
            _wait_both(slot, slot, send_sems.at[k], recv_sems.at[k])

    return _split_wait("small_exchange_wait", started, waits, after)[0]


def _adam_math(w, g, m, v):
    m = ADAM_B1 * m + (1.0 - ADAM_B1) * g
    v = ADAM_B2 * v + (1.0 - ADAM_B2) * (g * g)
    m_hat = m / (1.0 - ADAM_B1 ** ADAM_STEP)
    v_hat = v / (1.0 - ADAM_B2 ** ADAM_STEP)
    delta = -ADAM_LR * (m_hat / (jnp.sqrt(v_hat) + ADAM_EPS) + ADAM_WD * w)
    return delta, m, v


def _adamw(w, g, m, v, name):
    rows, cols = w.shape[0], w.shape[-1]
    fits = [t for t in range(SUBLANES, rows, SUBLANES) if rows % t == 0 and t * cols * 4 <= (3 << 19)]
    tr = max(fits) if fits and w.ndim == 2 else rows

    def body(w_ref, g_ref, m_ref, v_ref, d_ref, nm_ref, nv_ref, go_ref):
        g = g_ref[...]
        d, nm, nv = _adam_math(w_ref[...], g, m_ref[...], v_ref[...])
        d_ref[...] = d
        nm_ref[...] = nm
        nv_ref[...] = nv
        go_ref[...] = g

    spec = pl.BlockSpec((tr,) + w.shape[1:], lambda i: (i,) + (0,) * (w.ndim - 1))
    return pl.pallas_call(
        body, name=name, grid=(rows // tr,), in_specs=[spec] * 4, out_specs=[spec] * 4,
        out_shape=[_sds(w.shape, F32)] * 4, compiler_params=_cp(("arbitrary",)),
    )(*_hbm(w, g, m, v))


def _adamw_group(w, g, m, v, tag):
    names = list(w)
    n = len(names)
    steps = 4
    specs = [pl.BlockSpec((w[k].shape[0] // steps, w[k].shape[1]), lambda i: (i, 0)) for k in names]

    def body(*refs):
        for k in range(n):
            w_ref, g_ref, m_ref, v_ref = [refs[j * n + k] for j in range(4)]
            d_ref, nm_ref, nv_ref, go_ref = refs[4 * n + 4 * k:4 * n + 4 * k + 4]
            grad = g_ref[...]
            d, nm, nv = _adam_math(w_ref[...], grad, m_ref[...], v_ref[...])
            d_ref[...] = d
            nm_ref[...] = nm
            nv_ref[...] = nv
            go_ref[...] = grad

    res = pl.pallas_call(
        body, name="adamw_" + tag, grid=(steps,), in_specs=specs * 4, out_specs=[s for s in specs for _ in range(4)],
        out_shape=[_sds(w[k].shape, F32) for k in names for _ in range(4)], compiler_params=_cp(("arbitrary",), 48),
    )(*_hbm(*[a[k] for a in (w, g, m, v) for k in names]))
    return {k: tuple(res[4 * i:4 * i + 4]) for i, k in enumerate(names)}


def _small_sum_adamw(gathered, w, m, v):
    names = _SMALL_IN_CALL
    n = len(names)

    def body(*refs):
        a_ref = refs[0]
        w_refs, m_refs, v_refs = refs[1:1 + n], refs[1 + n:1 + 2 * n], refs[1 + 2 * n:1 + 3 * n]
        sum_ref, loss_ref = refs[1 + 3 * n], refs[2 + 3 * n]
        outs = refs[3 + 3 * n:]
        g = a_ref[0]
        for k in range(1, N_DEV):
            g = g + a_ref[k]
        sum_ref[...] = g
        loss_ref[...] = g[0:1, 0:1]
        for i, name in enumerate(names):
            if name == "rel_bias":
                r0, c0 = _REL_BIAS_AT
                pieces = [(slice(None), g[r0:r0 + N_HEADS, c0:c0 + N_BUCKETS])]
            elif name == "b_conv":
                r0 = _SMALL_AT[name][0]
                pieces = [(slice(None), jnp.concatenate([g[r0 + k:r0 + k + 1, :] for k in range(CONV_ROWS)], axis=1)[:, :2 * D_FF])]
            elif name == "w_s":
                r0, nr, _ = _SMALL_AT[name]
                pieces = [(slice(nr * j, nr * (j + 1)), g[r0:r0 + nr, CHUNK * j:CHUNK * (j + 1)]) for j in range(D_MODEL // CHUNK)]
            else:
                r0, nr, nc = _SMALL_AT[name]
                pieces = [(slice(None), g[r0:r0 + nr, 0:nc])]
            for at, gp in pieces:
                d, nm, nv = _adam_math(w_refs[i][at], gp, m_refs[i][at], v_refs[i][at])
                for k, val in enumerate((gp, d, nm, nv)):
                    outs[4 * i + k][at] = val

    shapes = [w[k].shape for k in names]
    res = pl.pallas_call(
        body, name="small_sum_adamw",
        out_shape=[_sds((SMALL_ROWS, D_MODEL), F32), _sds((1, 1), F32)] + [_sds(s, F32) for s in shapes for _ in range(4)],
    )(gathered, *[w[k] for k in names], *[m[k] for k in names], *[v[k] for k in names])
    return res[0], res[1], {k: tuple(res[2 + 4 * i:6 + 4 * i]) for i, k in enumerate(names)}


_NAMES = ("g_mix", "w_in", "g_sgu", "w_s", "b_s", "sinks", "rel_bias", "w_pa", "w_pb", "w_out",
          "g_ffn", "w_up", "w_conv", "b_conv", "w_down", "g_final")

def kernel(x, g_mix, w_in, g_sgu, w_s, b_s, sinks, rel_bias, w_pa, w_pb, w_out, g_ffn, w_up, w_conv, b_conv, w_down, g_final, loss_target, m_g_mix, m_w_in, m_g_sgu, m_w_s, m_b_s, m_sinks, m_rel_bias, m_w_pa, m_w_pb, m_w_out, m_g_ffn, m_w_up, m_w_conv, m_b_conv, m_w_down, m_g_final, v_g_mix, v_w_in, v_g_sgu, v_w_s, v_b_s, v_sinks, v_rel_bias, v_w_pa, v_w_pb, v_w_out, v_g_ffn, v_w_up, v_w_conv, v_b_conv, v_w_down, v_g_final):
    w = dict(g_mix=g_mix, w_in=w_in, g_sgu=g_sgu, w_s=w_s, b_s=b_s, sinks=sinks, rel_bias=rel_bias, w_pa=w_pa, w_pb=w_pb,
             w_out=w_out, g_ffn=g_ffn, w_up=w_up, w_conv=w_conv, b_conv=b_conv, w_down=w_down, g_final=g_final)
    m = dict(g_mix=m_g_mix, w_in=m_w_in, g_sgu=m_g_sgu, w_s=m_w_s, b_s=m_b_s, sinks=m_sinks, rel_bias=m_rel_bias, w_pa=m_w_pa,
             w_pb=m_w_pb, w_out=m_w_out, g_ffn=m_g_ffn, w_up=m_w_up, w_conv=m_w_conv, b_conv=m_b_conv, w_down=m_w_down,
             g_final=m_g_final)
    v = dict(g_mix=v_g_mix, w_in=v_w_in, g_sgu=v_g_sgu, w_s=v_w_s, b_s=v_b_s, sinks=v_sinks, rel_bias=v_rel_bias, w_pa=v_w_pa,
             w_pb=v_w_pb, w_out=v_w_out, g_ffn=v_g_ffn, w_up=v_w_up, w_conv=v_w_conv, b_conv=v_b_conv, w_down=v_w_down,
             g_final=v_g_final)
    xi, yi, ci = _mesh_pos()
    me = 2 * xi + yi

    shard = {n: w[n][0] for n in _BIG}
    shard_shapes = {n: shard[n].shape for n in _BIG}
    wc_shard = w["w_conv"][0]
    wc_pad = jnp.pad(wc_shard, ((0, 5), (0, 0)))
    own = {n: _own_slot(shard[n].astype(BF16), N_CHIPS, me) for n in _BIG if n != "w_in"}
    own["w_in"] = _own_slot(shard["w_in"].T.astype(BF16), N_CHIPS, me)
    stacks, wc_all = _allgather_weights({"w_in": own["w_in"]}, _own_slot(wc_pad, N_CHIPS, me))
    proj_gather = _allgather_start({n: own[n] for n in _MIXER[1:]}, "proj", stacks["w_in"])
    ffn_gather = _allgather_start({n: own[n] for n in _FFN}, "ffn", proj_gather[1][-1])
    w_conv_full = jnp.concatenate([wc_all[i, :3] for i in range(N_CHIPS)], axis=1)
    w_in_full = stacks["w_in"].reshape(N_CHIPS * IN_SHARD, D_MODEL)
    pos = jnp.stack([me, ci])

    def proj_weights(done):
        st = _allgather_forward(_allgather_wait(proj_gather, "proj", done), "proj")
        return st["w_pa"], st["w_pb"], st["w_out"].reshape(D_MODEL, D_MODEL)

    def ffn_weights(done):
        st = _allgather_forward(_allgather_wait(ffn_gather, "ffn", done), "ffn")
        return st["w_up"], st["w_down"].reshape(D_FF, D_MODEL)

    groups = {}

    def stage1(group, parts):
        groups[group] = dict(parts=parts, pair=_pair_exchange_start(parts, group, None))
        return groups[group]["pair"][1][-1]

    def stage2(group, after, order_after):
        g = groups[group]
        g["parts"], g["sib"] = _pair_exchange_wait(g["pair"], group, after)
        if group == "in":
            sums = {n: _pair_add(g["parts"][n], g["sib"][n], n, pos) for n in g["parts"]}
        else:
            sums = _pair_add_group(g["parts"], g["sib"], group, pos)
        g["chip"] = _chip_exchange_start(sums, group, order_after)
        return g["chip"][1][-1]

    def stage3(group, after, order_after):
        g = groups[group]
        got = _chip_exchange_wait(g["chip"], group, after)
        if group == "in":
            owned = {n: _owner_sum(g["parts"][n], g["sib"][n], got[n], n, pos, shard_shapes[n]) for n in g["parts"]}
        else:
            owned = _owner_sum_group(g["parts"], g["sib"], got, group, pos, shard_shapes)
        g["share"] = _pair_share_start(owned, group, order_after)
        return g["share"][1][-1]

    grads, deltas, new_m, new_v = {}, {}, {}, {}

    def stage4(group, after):
        g_shard = _pair_share_wait(groups[group]["share"], group, after)
        if group != "in":
            res = _adamw_group({n: shard[n] for n in g_shard}, g_shard, {n: m[n][0] for n in g_shard},
                               {n: v[n][0] for n in g_shard}, group)
            for n, (d, nm, nv, go) in res.items():
                grads[n], deltas[n], new_m[n], new_v[n] = go[None], d[None], nm[None], nv[None]
            return nv
        last = None
        for n in g_shard:
            g = _tie(g_shard[n], last)
            if n == "w_in":
                d, nm, nv, gt = _adamw(shard[n].T, g.T, m[n][0].T, v[n][0].T, "adamw_" + n)
                grads[n], deltas[n], new_m[n], new_v[n] = gt.T[None], d.T[None], nm.T[None], nv.T[None]
            else:
                d, nm, nv, go = _adamw(shard[n], g, m[n][0], v[n][0], "adamw_" + n)
                grads[n], deltas[n], new_m[n], new_v[n] = go[None], d[None], nm[None], nv[None]
            last = nv
        return last

    def on_grads(group, parts):
        token = stage1(group, parts)
        some = next(iter(parts.values()))
        if group == "proj":
            token = stage2("ffn", some, token)
        if group == "in":
            token = stage2("proj", some, token)
            token = stage3("ffn", some, token)
            token = stage2("in", token, token)
        return token

    loss, grad_x, small, big = _local_step(
        x, loss_target, w["g_mix"], w["g_sgu"], w["w_s"][0], w["b_s"][0], w["sinks"], w["rel_bias"], w["g_ffn"],
        w["b_conv"], w["g_final"], w_in_full, w_conv_full, proj_weights, ffn_weights, on_grads, ffn_gather[1][-1])

    small["loss"] = loss
    small_gather = _small_exchange_start(_own_slot(_pack_small(small), N_DEV, 2 * me + ci), grad_x)
    token = stage3("proj", grad_x, small_gather[-1])
    done = stage4("ffn", token)
    done = stage4("proj", done)
    token = stage3("in", done, None)
    all_small = _small_exchange_wait(small_gather, token)
    two_d = {n: (lambda a, n=n: a.reshape(_SMALL_AT[n][1:])) for n in _SMALL_IN_CALL}
    two_d["rel_bias"] = lambda a: a.T
    two_d["b_conv"] = lambda a: a
    two_d["w_s"] = lambda a: a.reshape(A_GROUPS * CHUNK, CHUNK)
    s_sum, s_loss, s_out = _small_sum_adamw(all_small, *[{n: two_d[n](p[n]) for n in _SMALL_IN_CALL} for p in (w, m, v)])
    stage4("in", s_sum)
    for n in _SMALL_IN_CALL:
        back = (lambda a: a.T) if n == "rel_bias" else (lambda a, n=n: a.reshape(w[n].shape))
        grads[n], deltas[n], new_m[n], new_v[n] = [back(a) for a in s_out[n]]

    def rows(n):
        r0, nr, _ = _SMALL_AT[n]
        return s_sum[r0:r0 + nr]

    wcols = wc_shard.shape[1]
    g_wc = lax.dynamic_slice(_unwide(rows("w_conv"), 3), (0, me * wcols), (3, wcols))
    taps = lambda a: a.transpose(1, 0, 2)
    res = _adamw(taps(w["w_conv"]), g_wc[:, None, :], taps(m["w_conv"]), taps(v["w_conv"]), "adamw_w_conv")
    deltas["w_conv"], new_m["w_conv"], new_v["w_conv"], grads["w_conv"] = [taps(a) for a in res]

    return (s_loss.reshape(()), grad_x, *[grads[n] for n in _NAMES], *[deltas[n] for n in _NAMES],
            *[new_m[n] for n in _NAMES], *[new_v[n] for n in _NAMES])
```

```python
import functools

import numpy as np
import jax
import jax.numpy as jnp
from jax import lax
from jax.experimental import pallas as pl
from jax.experimental.pallas import tpu as pltpu

F32 = jnp.float32
BF16 = jnp.bfloat16

D_MODEL = 1024
CHUNK = 128
A_GROUPS = 4
A_WIDTH = 512
N_HEADS = 8
HEAD_DIM = 64
Q_DIM = 512
KV_DIM = 128
N_BUCKETS = 32
MAX_DISTANCE = 128
D_FF = 2816
EPS = 1e-6
NEG_INF = -1e30
G_DIM = 2 * D_MODEL
A_DIM = 2 * A_WIDTH
B_DIM = Q_DIM + 2 * KV_DIM
LANES = 128
SUBLANES = 8
ROW_TILE = 512
WIDE_ROW_TILE = 256
COL_CHUNK = 1408
GRAD_ROW_TILE = 512
STREAM_ROWS = 256
BF16_ROWS = 16
N_CHIPS = 4
N_DEV = 8

ADAM_LR = 0.001
ADAM_B1 = 0.9
ADAM_B2 = 0.999
ADAM_EPS = 1e-08
ADAM_WD = 0.01
ADAM_STEP = 10

MESH = pl.DeviceIdType.MESH
_GELU_C = 0.7978845608028654
_GELU_A = 0.044715


def _cp(sem=None, vmem_mb=None):
    kw = {}
    if sem is not None:
        kw["dimension_semantics"] = sem
    if vmem_mb is not None:
        kw["vmem_limit_bytes"] = vmem_mb << 20
    return pltpu.CompilerParams(**kw)


def _dot(a, b):
    return jnp.dot(a, b, preferred_element_type=F32)


def _dot_nt(a, b):
    return lax.dot_general(a, b, (((1,), (1,)), ((), ())), preferred_element_type=F32)


def _dot_tn(a, b):
    return lax.dot_general(a, b, (((0,), (0,)), ((), ())), preferred_element_type=F32)


def _rms_r(x):
    return lax.rsqrt(jnp.mean(x * x, axis=-1, keepdims=True) + EPS)


def _rms_bwd(dh, n, r, g):
    dn = dh * g
    return r * (dn - n * jnp.mean(dn * n, axis=-1, keepdims=True))


def _gelu(x):
    t = jnp.tanh(_GELU_C * (x + _GELU_A * (x * x * x)))
    return 0.5 * x * (1.0 + t), t


def _gelu_grad(x, t):
    return 0.5 * (1.0 + t) + 0.5 * x * (1.0 - t * t) * (_GELU_C * (1.0 + 3.0 * _GELU_A * x * x))


def _sigmoid(x):
    return 1.0 / (1.0 + jnp.exp(-x))


def _tie(x, dep):
    return x if dep is None else lax.optimization_barrier((x, dep))[0]


def _row(tm, w):
    return pl.BlockSpec((tm, w), lambda i: (i, 0))


def _full(shape):
    nd = len(shape)
    return pl.BlockSpec(tuple(shape), lambda *_: (0,) * nd)


def _resident(shape):
    nd = len(shape)
    return pl.BlockSpec(tuple(shape), lambda *_: (0,) * nd, pipeline_mode=pl.Buffered(1))


def _sds(shape, dtype):
    return pltpu.HBM(tuple(shape), dtype)


def _hbm(*arrays):
    return [pltpu.with_memory_space_constraint(a, pltpu.HBM) for a in arrays]


HBM = pl.BlockSpec(memory_space=pltpu.HBM)
ANY = pl.BlockSpec(memory_space=pl.ANY)
SEM = pl.BlockSpec(memory_space=pltpu.SEMAPHORE)


def _band_buckets():
    i = np.arange(CHUNK)[:, None]
    j = np.arange(2 * CHUNK)[None, :]
    dist = i + CHUNK - j
    valid = (dist >= 0) & (dist < CHUNK)
    d = np.clip(dist, 0, None)
    max_exact = N_BUCKETS // 2
    large = max_exact + (np.log(np.maximum(d, 1) / max_exact) / np.log(MAX_DISTANCE / max_exact)
                         * (N_BUCKETS - max_exact)).astype(np.int32)
    large = np.minimum(large, N_BUCKETS - 1)
    buckets = np.where(d < max_exact, d, large).astype(np.int32)
    return np.where(valid, buckets, -1).astype(np.int32)


_A_COLS = slice(0, A_DIM)
_B_COLS = slice(A_DIM, A_DIM + B_DIM)
_G_COLS = slice(A_DIM + B_DIM, A_DIM + B_DIM + G_DIM)


def _inproj(x2, g_mix, w_in, g_sgu, w_s, b_st, tm, after=None):
    T = x2.shape[0]
    order = [] if after is None else [after]

    def body(*refs):
        x_ref, g_ref, w_ref, gs_ref, ws_ref, bs_ref = refs[:6]
        pg_ref, pa_ref, pb_ref, h_ref, ya_ref = refs[6 + len(order):]
        x = x_ref[...]
        h = (x * _rms_r(x) * g_ref[...]).astype(BF16)
        h_ref[...] = h
        pa = _dot_nt(h, w_ref[_A_COLS, :]).astype(BF16)
        pa_ref[...] = pa
        pb_ref[...] = _dot_nt(h, w_ref[_B_COLS, :]).astype(BF16)
        pg_ref[...] = _dot_nt(h, w_ref[_G_COLS, :]).astype(BF16)
        _sgu_apply(pa.astype(F32), gs_ref[...], ws_ref, bs_ref, ya_ref)

    return pl.pallas_call(
        body, name="inproj", grid=(T // tm,),
        in_specs=[_row(tm, D_MODEL), _full(g_mix.shape), _resident(w_in.shape), _full(g_sgu.shape), _full(w_s.shape),
                  _full(b_st.shape)] + [ANY] * len(order),
        out_specs=[_row(tm, G_DIM), _row(tm, A_DIM), _row(tm, B_DIM), _row(tm, D_MODEL), _row(tm, A_WIDTH)],
        out_shape=[_sds((T, G_DIM), BF16), _sds((T, A_DIM), BF16), _sds((T, B_DIM), BF16), _sds((T, D_MODEL), BF16),
                   _sds((T, A_WIDTH), BF16)],
        compiler_params=_cp(("arbitrary",), 48),
    )(*_hbm(x2, g_mix, w_in, g_sgu, w_s, b_st), *order)


def _sgu_parts(p, g):
    pu = p[:, :A_WIDTH]
    pv = p[:, A_WIDTH:]
    u, tu = _gelu(pu)
    vv, tv = _gelu(pv)
    rv = _rms_r(vv)
    vn = (vv * rv * g).astype(BF16)
    return pu, pv, u, tu, vv, tv, rv, vn


def _tril():
    r = lax.broadcasted_iota(jnp.int32, (CHUNK, CHUNK), 0)
    c = lax.broadcasted_iota(jnp.int32, (CHUNK, CHUNK), 1)
    return r >= c


def _sgu_apply(p, g, ws_ref, bs_ref, y_ref):
    tril = _tril()
    _, _, u, _, _, _, _, vn = _sgu_parts(p, g)
    for gi in range(A_GROUPS):
        wm = jnp.where(tril, ws_ref[gi], 0.0).astype(BF16)
        bcol = bs_ref[:, gi:gi + 1]
        cs = slice(gi * CHUNK, (gi + 1) * CHUNK)
        for c in range(p.shape[0] // CHUNK):
            rs = slice(c * CHUNK, (c + 1) * CHUNK)
            s = _dot(wm, vn[rs, cs]) + bcol
            y_ref[rs, cs] = (u[rs, cs] * s).astype(BF16)


HEAD_ROWS = N_HEADS * CHUNK


def _head_rows(h):
    return slice(h * CHUNK, (h + 1) * CHUNK)


def _attn_setup(bias_scr, sink_scr, kvar_scr, qkv_ref, bk_ref, rel_ref, sink_ref):
    @pl.when(pl.program_id(0) == 0)
    def _():
        bk = bk_ref[...]
        for h in range(N_HEADS):
            acc = jnp.full((CHUNK, 2 * CHUNK), NEG_INF, F32)
            for b in range(N_BUCKETS):
                acc = jnp.where(bk == b, rel_ref[b, h], acc)
            bias_scr[_head_rows(h), :] = acc
            sink_scr[_head_rows(h), :] = jnp.full((CHUNK, LANES), sink_ref[0, h], F32)

    seq = qkv_ref.shape[0]
    rows_per = 2 * CHUNK
    for is_v in range(2):
        c0 = Q_DIM + is_v * KV_DIM
        for r in range(seq // rows_per):
            rs = slice(r * rows_per, (r + 1) * rows_per)
            a = qkv_ref[rs, c0:c0 + KV_DIM].astype(F32)
            lane = lax.broadcasted_iota(jnp.int32, a.shape, 1)
            lo = jnp.where(lane < HEAD_DIM, a, 0.0)
            hi = jnp.where(lane >= HEAD_DIM, a, 0.0)
            kvar_scr[4 * is_v + 0, rs, :] = lo.astype(BF16)
            kvar_scr[4 * is_v + 1, rs, :] = pltpu.roll(lo, HEAD_DIM, 1).astype(BF16)
            kvar_scr[4 * is_v + 2, rs, :] = pltpu.roll(hi, HEAD_DIM, 1).astype(BF16)
            kvar_scr[4 * is_v + 3, rs, :] = hi.astype(BF16)


def _rowsum(a, ones):
    hi = a.astype(BF16)
    lo = (a - hi.astype(F32)).astype(BF16)
    return _dot(hi, ones) + _dot(lo, ones)


def _both(a):
    return jnp.concatenate([a, a], axis=1)


def _attn_probs(qkv_ref, r0, n, kv, bias_scr, sink_scr, ones):
    s = jnp.concatenate([_dot_nt(qkv_ref[pl.ds(r0, CHUNK), (h // 2) * LANES:(h // 2 + 1) * LANES], kv[h // 4][h % 2])
                         for h in range(N_HEADS)], axis=0)
    s = s * (HEAD_DIM ** -0.5) + bias_scr[...]
    col = lax.broadcasted_iota(jnp.int32, s.shape, 1)
    s = jnp.where((col < CHUNK) & (n == 0), NEG_INF, s)
    sink = sink_scr[...]
    m = jnp.maximum(jnp.max(s, axis=-1, keepdims=True), sink)
    p = jnp.exp(s - _both(m))
    es = jnp.exp(sink - m)
    inv = 1.0 / (_dot(p.astype(BF16), ones) + es)
    return p * _both(inv), es * inv


def _attn_block_inputs(kvar_scr, n):
    r0 = pl.multiple_of(n * CHUNK, CHUNK)
    rp = pl.multiple_of(jnp.maximum(n - 1, 0) * CHUNK, CHUNK)

    def both(idx):
        return jnp.concatenate([kvar_scr[idx, pl.ds(rp, CHUNK), :], kvar_scr[idx, pl.ds(r0, CHUNK), :]], axis=0)

    kv = ((both(0), both(1)), (both(2), both(3)))
    vv = ((both(4), both(5)), (both(6), both(7)))
    return r0, kv, vv


def _attn_fwd(proj_b, sinks, rel_bias, n_seq, seq):
    nb = seq // CHUNK
    bk = jnp.asarray(_band_buckets())

    def body(qkv_ref, bk_ref, rel_ref, sink_ref, o_ref, bias_scr, sink_scr, kvar_scr):
        _attn_setup(bias_scr, sink_scr, kvar_scr, qkv_ref, bk_ref, rel_ref, sink_ref)
        ones = jnp.ones((2 * CHUNK, LANES), BF16)

        def blk(n, carry):
            r0, kv, vv = _attn_block_inputs(kvar_scr, n)
            prob, _ = _attn_probs(qkv_ref, r0, n, kv, bias_scr, sink_scr, ones)
            pb = prob.astype(BF16)
            for pr in range(N_HEADS // 2):
                acc = _dot(pb[_head_rows(2 * pr)], vv[pr // 2][0]) + _dot(pb[_head_rows(2 * pr + 1)], vv[pr // 2][1])
                o_ref[pl.ds(r0, CHUNK), pr * LANES:(pr + 1) * LANES] = acc.astype(BF16)
            return carry

        lax.fori_loop(0, nb, blk, 0)

    smem = pl.BlockSpec(memory_space=pltpu.SMEM)
    return pl.pallas_call(
        body, name="attn_fwd", grid=(n_seq,),
        in_specs=[_row(seq, B_DIM), _full(bk.shape), smem, smem],
        out_specs=_row(seq, Q_DIM), out_shape=_sds((n_seq * seq, Q_DIM), BF16),
        scratch_shapes=[pltpu.VMEM((HEAD_ROWS, 2 * CHUNK), F32), pltpu.VMEM((HEAD_ROWS, LANES), F32),
                        pltpu.VMEM((8, seq, KV_DIM), BF16)],
        compiler_params=_cp(("arbitrary",), 40),
    )(*_hbm(proj_b, bk), rel_bias, sinks)


def _dot_stacked(a, w_ref):
    return jnp.concatenate([_dot(a, w_ref[i]) for i in range(N_CHIPS)], axis=1)


def _dot_nt_stacked(a, w_ref):
    w = w_ref.shape[2]
    acc = _dot_nt(a[:, :w], w_ref[0])
    for i in range(1, N_CHIPS):
        acc = acc + _dot_nt(a[:, i * w:(i + 1) * w], w_ref[i])
    return acc


def _merge_fwd(x2, y_a, y_b, proj_g, w_pa, w_pb, w_out, tm):
    T = x2.shape[0]

    def body(x_ref, ya_ref, yb_ref, g_ref, wpa_ref, wpb_ref, wo_ref, x1_ref, mg_ref):
        g = g_ref[...].astype(F32)
        pa = _dot_stacked(ya_ref[...], wpa_ref)
        pb = _dot_stacked(yb_ref[...], wpb_ref)
        merged = (_sigmoid(g[:, :D_MODEL]) * pa + _sigmoid(g[:, D_MODEL:]) * pb).astype(BF16)
        mg_ref[...] = merged
        x1_ref[...] = x_ref[...] + _dot(merged, wo_ref[...])

    return pl.pallas_call(
        body, name="merge_fwd", grid=(T // tm,),
        in_specs=[_row(tm, D_MODEL), _row(tm, A_WIDTH), _row(tm, Q_DIM), _row(tm, G_DIM),
                  _resident(w_pa.shape), _resident(w_pb.shape), _resident(w_out.shape)],
        out_specs=[_row(tm, D_MODEL), _row(tm, D_MODEL)],
        out_shape=[_sds((T, D_MODEL), F32), _sds((T, D_MODEL), BF16)],
        compiler_params=_cp(("arbitrary",), 40),
    )(*_hbm(x2, y_a, y_b, proj_g, w_pa, w_pb, w_out))


def _upproj(x1, g_ffn, w_up, w_conv, b_conv, tm, seq):
    T = x1.shape[0]
    cw = w_up.shape[2]
    tiles_per_seq = seq // tm

    def body(x_ref, g_ref, w_ref, wc_ref, bc_ref, u_ref, h_ref, gate_ref, val_ref, tail_scr):
        at_start = (pl.program_id(0) % tiles_per_seq) == 0
        x = x_ref[...]
        h = (x * _rms_r(x) * g_ref[...]).astype(BF16)
        h_ref[...] = h
        for i in range(N_CHIPS):
            cs = slice(i * cw, (i + 1) * cw)
            u = _dot(h, w_ref[i])
            u_ref[:, cs] = u.astype(BF16)
            hl = jnp.where(at_start, 0.0, tail_scr[SUBLANES - 2:SUBLANES, cs])
            tail_scr[:, cs] = u[tm - SUBLANES:]
            up = _conv_out((u, _shift_down(u, hl, 1), _shift_down(u, hl, 2)), wc_ref[:, cs], bc_ref[:, cs])
            out_ref = gate_ref if i < N_CHIPS // 2 else val_ref
            out_ref[:, (i % 2) * cw:(i % 2 + 1) * cw] = up.astype(BF16)

    return pl.pallas_call(
        body, name="upproj", grid=(T // tm,),
        in_specs=[_row(tm, D_MODEL), _full(g_ffn.shape), _resident(w_up.shape), _full(w_conv.shape), _full(b_conv.shape)],
        out_specs=[_row(tm, 2 * D_FF), _row(tm, D_MODEL), _row(tm, D_FF), _row(tm, D_FF)],
        out_shape=[_sds((T, 2 * D_FF), BF16), _sds((T, D_MODEL), BF16), _sds((T, D_FF), BF16), _sds((T, D_FF), BF16)],
        scratch_shapes=[pltpu.VMEM((SUBLANES, 2 * D_FF), F32)],
        compiler_params=_cp(("arbitrary",), 56),
    )(*_hbm(x1, g_ffn, w_up, w_conv, b_conv))


def _shift_down(u, halo, k):
    rolled = pltpu.roll(u, k, 0)
    head = rolled[:SUBLANES]
    row = lax.broadcasted_iota(jnp.int32, head.shape, 0)
    if k == 1:
        head = jnp.where(row == 0, halo[1:2], head)
    else:
        head = jnp.where(row == 0, halo[0:1], jnp.where(row == 1, halo[1:2], head))
    return jnp.concatenate([head, rolled[SUBLANES:]], axis=0)


def _shift_up(d, halo, k):
    tm = d.shape[0]
    rolled = pltpu.roll(d, tm - k, 0)
    tail = rolled[tm - SUBLANES:]
    row = lax.broadcasted_iota(jnp.int32, tail.shape, 0)
    if k == 1:
        tail = jnp.where(row == SUBLANES - 1, halo[0:1], tail)
    else:
        tail = jnp.where(row == SUBLANES - 2, halo[0:1], jnp.where(row == SUBLANES - 1, halo[1:2], tail))
    return jnp.concatenate([rolled[:tm - SUBLANES], tail], axis=0)


def _conv_out(taps, wc, bc):
    u, u1, u2 = taps
    return wc[0:1] * u2 + wc[1:2] * u1 + wc[2:3] * u + bc


def _ffn_down_loss(gate, val, x1, target, w_down, g_final, tm):
    T = x1.shape[0]
    half = D_FF // 2

    sub = min(tm, 128)

    def body(gt_ref, vl_ref, x1_ref, t_ref, wd_ref, g_ref, dx2_ref, loss_ref, gg_ref):
        i = pl.program_id(0)
        g = g_ref[...]

        def down(rs):
            acc = jnp.zeros((sub, D_MODEL), F32)
            for j in range(2):
                gc = slice(j * half, (j + 1) * half)
                gate = gt_ref[rs, gc].astype(F32)
                act = (gate * _sigmoid(gate) * vl_ref[rs, gc].astype(F32)).astype(BF16)
                acc = acc + _dot(act, wd_ref[gc, :])
            return acc

        def norm_loss(rs, acc):
            x2 = x1_ref[rs, :] + acc
            r = _rms_r(x2)
            n = x2 * r
            diff = n * g - t_ref[rs, :]
            dy = diff * (1.0 / D_MODEL)
            dx2_ref[rs, :] = _rms_bwd(dy, n, r, g)
            return (jnp.sum(jnp.mean(diff * diff, axis=-1, keepdims=True), axis=0, keepdims=True),
                    jnp.sum(dy * n, axis=0, keepdims=True))

        subs = [slice(s0, s0 + sub) for s0 in range(0, tm, sub)]
        accs = [down(rs) for rs in subs]
        parts = [norm_loss(rs, acc) for rs, acc in zip(subs, accs)]

        @pl.when(i == 0)
        def _():
            loss_ref[...] = jnp.zeros_like(loss_ref)
            gg_ref[...] = jnp.zeros_like(gg_ref)

        loss_ref[...] += 0.5 * sum(p[0] for p in parts)
        gg_ref[...] += sum(p[1] for p in parts)

    return pl.pallas_call(
        body, name="ffn_down_loss", grid=(T // tm,),
        in_specs=[_row(tm, D_FF), _row(tm, D_FF), _row(tm, D_MODEL), _row(tm, D_MODEL),
                  _resident(w_down.shape), _full(g_final.shape)],
        out_specs=[_row(tm, D_MODEL), _full((1, 1)), _full((1, D_MODEL))],
        out_shape=[_sds((T, D_MODEL), F32), _sds((1, 1), F32), _sds((1, D_MODEL), F32)],
        compiler_params=_cp(("arbitrary",), 48),
    )(*_hbm(gate, val, x1, target, w_down, g_final))


def _ffn_bwd_act(gate, val, dx2, w_down, tm):
    T = dx2.shape[0]
    half = D_FF // 2
    nt = T // tm

    def body(g_ref, v_ref, dx_ref, wd_ref, dg_ref, dv_ref, gwd_out, gbg_ref, gbv_ref, gwd_ref):
        i = pl.program_id(1)

        @pl.when(i == 0)
        def _():
            for r in (gwd_ref, gbg_ref, gbv_ref):
                r[...] = jnp.zeros_like(r)

        dx = dx_ref[...].astype(BF16)
        for c0 in range(0, half, COL_CHUNK):
            cs = slice(c0, min(c0 + COL_CHUNK, half))
            gate = g_ref[:, cs].astype(F32)
            val = v_ref[:, cs].astype(F32)
            sg = _sigmoid(gate)
            silu = gate * sg
            d_act = _dot_nt(dx, wd_ref[cs, :])
            d_val = d_act * silu
            d_gate = d_act * val * (sg * (1.0 + gate * (1.0 - sg)))
            dg_ref[:, cs] = d_gate.astype(BF16)
            dv_ref[:, cs] = d_val.astype(BF16)
            gwd_ref[cs, :] += _dot_tn((silu * val).astype(BF16), dx)
            gbg_ref[:, cs] += jnp.sum(d_gate, axis=0, keepdims=True)
            gbv_ref[:, cs] += jnp.sum(d_val, axis=0, keepdims=True)

        @pl.when(i == nt - 1)
        def _():
            gwd_out[...] = gwd_ref[...].astype(BF16)

    tile = pl.BlockSpec((tm, half), lambda j, i: (i, j))
    vec = pl.BlockSpec((1, half), lambda j, i: (0, j))
    wrows = pl.BlockSpec((half, D_MODEL), lambda j, i: (j, 0))
    return pl.pallas_call(
        body, name="ffn_bwd_act", grid=(2, nt),
        in_specs=[tile, tile, pl.BlockSpec((tm, D_MODEL), lambda j, i: (i, 0)), wrows],
        out_specs=[tile, tile, wrows, vec, vec],
        out_shape=[_sds((T, D_FF), BF16), _sds((T, D_FF), BF16), _sds((D_FF, D_MODEL), BF16),
                   _sds((1, D_FF), F32), _sds((1, D_FF), F32)],
        scratch_shapes=[pltpu.VMEM((half, D_MODEL), F32)],
        compiler_params=_cp(("arbitrary", "arbitrary"), 56),
    )(*_hbm(gate, val, dx2, w_down))


def _ffn_bwd_up(d_gate, d_val, upre, dx2, x1, g_ffn, w_conv, w_up, tm, seq):
    T = dx2.shape[0]
    tiles_per_seq = seq // tm
    k16 = tm // BF16_ROWS
    n16 = T // BF16_ROWS
    cw = D_FF // 2

    def body(dg_ref, dv_ref, hg_ref, hv_ref, u_ref, dx2_ref, x1_ref, g_ref, wc_ref, wu_ref, du_ref, dx1_ref, gg_ref, gwc_ref):
        i = pl.program_id(0)
        at_end = (i % tiles_per_seq) == tiles_per_seq - 1

        @pl.when(i == 0)
        def _():
            gg_ref[...] = jnp.zeros_like(gg_ref)
            gwc_ref[...] = jnp.zeros_like(gwc_ref)

        dh = jnp.zeros((tm, D_MODEL), F32)
        for j in range(4):
            src, hsrc = (dg_ref, hg_ref) if j < 2 else (dv_ref, hv_ref)
            ls = slice((j % 2) * cw, (j % 2 + 1) * cw)
            cs = slice(j * cw, (j + 1) * cw)
            d = src[:, ls].astype(F32)
            hl = hsrc[:, ls].astype(F32)[0:2]
            hl = jnp.where(at_end, 0.0, hl)
            wc = wc_ref[:, cs]
            d1 = _shift_up(d, hl, 1)
            d2 = _shift_up(d, hl, 2)
            du = (wc[2:3] * d + wc[1:2] * d1 + wc[0:1] * d2).astype(BF16)
            du_ref[:, cs] = du
            dh = dh + _dot_nt(du, wu_ref[j])
            u = u_ref[:, cs].astype(F32)
            gwc_ref[0:1, cs] += jnp.sum(d2 * u, axis=0, keepdims=True)
            gwc_ref[1:2, cs] += jnp.sum(d1 * u, axis=0, keepdims=True)
            gwc_ref[2:3, cs] += jnp.sum(d * u, axis=0, keepdims=True)
        x = x1_ref[...]
        r = _rms_r(x)
        n = x * r
        dx1_ref[...] = dx2_ref[...] + _rms_bwd(dh, n, r, g_ref[...])
        gg_ref[...] += jnp.sum(dh * n, axis=0, keepdims=True)

    nxt = pl.BlockSpec((BF16_ROWS, D_FF), lambda i: (jnp.minimum((i + 1) * k16, n16 - 1), 0))
    return pl.pallas_call(
        body, name="ffn_bwd_up", grid=(T // tm,),
        in_specs=[_row(tm, D_FF), _row(tm, D_FF), nxt, nxt, _row(tm, 2 * D_FF), _row(tm, D_MODEL), _row(tm, D_MODEL),
                  _full(g_ffn.shape), _full(w_conv.shape), _resident(w_up.shape)],
        out_specs=[_row(tm, 2 * D_FF), _row(tm, D_MODEL), _full((1, D_MODEL)), _full((3, 2 * D_FF))],
        out_shape=[_sds((T, 2 * D_FF), BF16), _sds((T, D_MODEL), F32), _sds((1, D_MODEL), F32), _sds((3, 2 * D_FF), F32)],
        compiler_params=_cp(("arbitrary",), 56),
    )(*_hbm(d_gate, d_val, d_gate, d_val, upre, dx2, x1, g_ffn, w_conv, w_up))


def _matmul_tn(a, b, tn, tk, name):
    T, M = a.shape
    N = b.shape[1]
    nk = T // tk

    def body(a_ref, b_ref, o_ref, acc_ref):
        k = pl.program_id(1)

        @pl.when(k == 0)
        def _():
            acc_ref[...] = jnp.zeros_like(acc_ref)

        acc_ref[...] += _dot_tn(a_ref[...], b_ref[...])

        @pl.when(k == nk - 1)
        def _():
            o_ref[...] = acc_ref[...].astype(BF16)

    return pl.pallas_call(
        body, name=name, grid=(N // tn, nk),
        in_specs=[pl.BlockSpec((tk, M), lambda j, k: (k, 0)), pl.BlockSpec((tk, tn), lambda j, k: (k, j))],
        out_specs=pl.BlockSpec((M, tn), lambda j, k: (0, j)), out_shape=_sds((M, N), BF16),
        scratch_shapes=[pltpu.VMEM((M, tn), F32)],
        compiler_params=_cp(("arbitrary", "arbitrary"), 48),
    )(*_hbm(a, b))


def _merge_bwd(dx1, merged, y_a, y_b, proj_g, proj_a, w_pa, w_pb, w_out, g_sgu, w_s, b_st, tm, after=None):
    T = dx1.shape[0]

    nt = T // tm
    pshape = (A_WIDTH, D_MODEL)
    order = [] if after is None else [after]

    def body(*refs):
        dx_ref, mg_ref, ya_ref, yb_ref, g_ref, p_ref, wpa_ref, wpb_ref, wo_ref, gs_ref, ws_ref, bs_ref = refs[:12]
        (dg_ref, da_ref, dyb_ref, gwo_out, gwpa_out, gwpb_out, gws_ref, gbs_ref, gg_ref,
         gwo_ref, gwpa_ref, gwpb_ref) = refs[12 + len(order):]
        i = pl.program_id(0)

        @pl.when(i == 0)
        def _():
            for r in (gwo_ref, gwpa_ref, gwpb_ref, gws_ref, gbs_ref, gg_ref):
                r[...] = jnp.zeros_like(r)

        dx = dx_ref[...].astype(BF16)
        dm = _dot_nt(dx, wo_ref[...])
        g = g_ref[...].astype(F32)
        ya = ya_ref[...]
        yb = yb_ref[...]
        pa = _dot_stacked(ya, wpa_ref)
        pb = _dot_stacked(yb, wpb_ref)
        sa = _sigmoid(g[:, :D_MODEL])
        sb = _sigmoid(g[:, D_MODEL:])
        dpa = (dm * sa).astype(BF16)
        dpb = (dm * sb).astype(BF16)
        dg_ref[:, :D_MODEL] = (dm * pa * (sa * (1.0 - sa))).astype(BF16)
        dg_ref[:, D_MODEL:] = (dm * pb * (sb * (1.0 - sb))).astype(BF16)
        d_ya = _dot_nt_stacked(dpa, wpa_ref).astype(BF16)
        dyb_ref[...] = _dot_nt_stacked(dpb, wpb_ref).astype(BF16)
        _sgu_bwd_apply(p_ref[...].astype(F32), d_ya.astype(F32), gs_ref[...], ws_ref, bs_ref, da_ref, gws_ref, gbs_ref, gg_ref)
        gwo_ref[...] += _dot_tn(mg_ref[...], dx)
        gwpa_ref[...] += _dot_tn(ya, dpa)
        gwpb_ref[...] += _dot_tn(yb, dpb)

        @pl.when(i == nt - 1)
        def _():
            gwo_out[...] = gwo_ref[...].astype(BF16)
            gwpa_out[...] = gwpa_ref[...].astype(BF16)
            gwpb_out[...] = gwpb_ref[...].astype(BF16)

    return pl.pallas_call(
        body, name="merge_bwd", grid=(nt,),
        in_specs=[_row(tm, D_MODEL), _row(tm, D_MODEL), _row(tm, A_WIDTH), _row(tm, Q_DIM), _row(tm, G_DIM), _row(tm, A_DIM),
                  _resident(w_pa.shape), _resident(w_pb.shape), _resident(w_out.shape),
                  _full(g_sgu.shape), _full(w_s.shape), _full(b_st.shape)] + [ANY] * len(order),
        out_specs=[_row(tm, G_DIM), _row(tm, A_DIM), _row(tm, Q_DIM),
                   _full(w_out.shape), _full(pshape), _full(pshape), _full(w_s.shape), _full(b_st.shape), _full(g_sgu.shape)],
        out_shape=[_sds((T, G_DIM), BF16), _sds((T, A_DIM), BF16), _sds((T, Q_DIM), BF16),
                   _sds(w_out.shape, BF16), _sds(pshape, BF16), _sds(pshape, BF16),
                   _sds(w_s.shape, F32), _sds(b_st.shape, F32), _sds(g_sgu.shape, F32)],
        scratch_shapes=[pltpu.VMEM(w_out.shape, F32), pltpu.VMEM(pshape, F32), pltpu.VMEM(pshape, F32)],
        compiler_params=_cp(("arbitrary",), 56),
    )(*_hbm(dx1, merged, y_a, y_b, proj_g, proj_a, w_pa, w_pb, w_out, g_sgu, w_s, b_st), *order)


def _sgu_bwd_apply(p, dy, g, ws_ref, bs_ref, dp_ref, gws_ref, gbs_ref, gg_ref):
    tril = _tril()
    pu, pv, u, tu, vv, tv, rv, vn = _sgu_parts(p, g)
    du_cols = []
    dvn_cols = []
    for gi in range(A_GROUPS):
        wm = jnp.where(tril, ws_ref[gi], 0.0).astype(BF16)
        wmt = wm.astype(F32).T.astype(BF16)
        bcol = bs_ref[:, gi:gi + 1]
        cs = slice(gi * CHUNK, (gi + 1) * CHUNK)
        du_rows = []
        dvn_rows = []
        gw = jnp.zeros((CHUNK, CHUNK), F32)
        gb = jnp.zeros((CHUNK, 1), F32)
        for c in range(p.shape[0] // CHUNK):
            rs = slice(c * CHUNK, (c + 1) * CHUNK)
            vn_c = vn[rs, cs]
            s = _dot(wm, vn_c) + bcol
            dy_c = dy[rs, cs]
            ds = dy_c * u[rs, cs]
            du_rows.append(dy_c * s)
            dsb = ds.astype(BF16)
            gw = gw + _dot_nt(dsb, vn_c)
            gb = gb + jnp.sum(ds, axis=-1, keepdims=True)
            dvn_rows.append(_dot(wmt, dsb))
        gws_ref[gi] += jnp.where(tril, gw, 0.0)
        gbs_ref[:, gi:gi + 1] += gb
        du_cols.append(jnp.concatenate(du_rows, axis=0))
        dvn_cols.append(jnp.concatenate(dvn_rows, axis=0))
    du = jnp.concatenate(du_cols, axis=1)
    dvn = jnp.concatenate(dvn_cols, axis=1)
    vhat = vv * rv
    gg_ref[...] += jnp.sum(dvn * vhat, axis=0, keepdims=True)
    dvv = _rms_bwd(dvn, vhat, rv, g)
    dp_ref[:, :A_WIDTH] = (du * _gelu_grad(pu, tu)).astype(BF16)
    dp_ref[:, A_WIDTH:] = (dvv * _gelu_grad(pv, tv)).astype(BF16)


def _attn_bwd(proj_b, d_yb, sinks, rel_bias, n_seq, seq, after=None):
    nb = seq // CHUNK
    bk = jnp.asarray(_band_buckets())
    order = [] if after is None else [after]

    def body(*refs):
        qkv_ref, do_ref, bk_ref, rel_ref, sink_ref = refs[:5]
        (d_ref, gs_ref, gr_ref, bias_scr, sink_scr, kvar_scr, dbias_scr, dk_scr, dv_scr, ds_scr) = refs[5 + len(order):]
        b = pl.program_id(0)
        _attn_setup(bias_scr, sink_scr, kvar_scr, qkv_ref, bk_ref, rel_ref, sink_ref)
        ones = jnp.ones((2 * CHUNK, LANES), BF16)

        @pl.when(b == 0)
        def _():
            dbias_scr[...] = jnp.zeros_like(dbias_scr)
            ds_scr[...] = jnp.zeros_like(ds_scr)

        dk_scr[...] = jnp.zeros_like(dk_scr)
        dv_scr[...] = jnp.zeros_like(dv_scr)

        def transposed(a):
            return a.astype(F32).T.astype(BF16)

        def blk(n, carry):
            r0, kv, vv = _attn_block_inputs(kvar_scr, n)
            prob, psink = _attn_probs(qkv_ref, r0, n, kv, bias_scr, sink_scr, ones)
            dp = jnp.concatenate([_dot_nt(do_ref[pl.ds(r0, CHUNK), (h // 2) * LANES:(h // 2 + 1) * LANES], vv[h // 4][h % 2])
                                  for h in range(N_HEADS)], axis=0)
            delta = _rowsum(prob * dp, ones)
            dsc = prob * (dp - _both(delta))
            ds_scr[...] += psink * delta
            dbias_scr[...] += dsc
            dsb = (dsc * (HEAD_DIM ** -0.5)).astype(BF16)
            pb = prob.astype(BF16)
            dkt = [jnp.zeros((HEAD_DIM, 2 * CHUNK), F32) for _ in range(2)]
            dvt = [jnp.zeros((HEAD_DIM, 2 * CHUNK), F32) for _ in range(2)]
            for pr in range(N_HEADS // 2):
                ps = slice(pr * LANES, (pr + 1) * LANES)
                qpt = transposed(qkv_ref[pl.ds(r0, CHUNK), ps])
                dopt = transposed(do_ref[pl.ds(r0, CHUNK), ps])
                kvh = pr // 2
                dq = jnp.zeros((CHUNK, LANES), F32)
                for hh in range(2):
                    hr = _head_rows(2 * pr + hh)
                    rows = slice(hh * HEAD_DIM, (hh + 1) * HEAD_DIM)
                    dq = dq + _dot(dsb[hr], kv[kvh][hh])
                    dkt[kvh] = dkt[kvh] + _dot(qpt, dsb[hr])[rows]
                    dvt[kvh] = dvt[kvh] + _dot(dopt, pb[hr])[rows]
                d_ref[pl.ds(r0, CHUNK), ps] = dq.astype(BF16)
            dk_scr[:, pl.ds(r0, 2 * CHUNK)] += jnp.concatenate(dkt, axis=0)
            dv_scr[:, pl.ds(r0, 2 * CHUNK)] += jnp.concatenate(dvt, axis=0)
            return carry

        lax.fori_loop(0, nb, blk, 0)
        for n in range(nb):
            rows = slice(n * CHUNK, (n + 1) * CHUNK)
            cols = slice((n + 1) * CHUNK, (n + 2) * CHUNK)
            d_ref[rows, Q_DIM:Q_DIM + KV_DIM] = dk_scr[:, cols].T.astype(BF16)
            d_ref[rows, Q_DIM + KV_DIM:] = dv_scr[:, cols].T.astype(BF16)

        @pl.when(b == n_seq - 1)
        def _():
            bkv = bk_ref[...]
            for h in range(N_HEADS):
                gs_ref[0:1, h:h + 1] = -jnp.sum(ds_scr[_head_rows(h), 0:1], axis=0, keepdims=True)
                db = dbias_scr[_head_rows(h), :]
                for bb in range(N_BUCKETS):
                    part = jnp.sum(jnp.where(bkv == bb, db, 0.0), axis=-1, keepdims=True)
                    gr_ref[bb:bb + 1, h:h + 1] = jnp.sum(part, axis=0, keepdims=True)

    smem = pl.BlockSpec(memory_space=pltpu.SMEM)
    return pl.pallas_call(
        body, name="attn_bwd", grid=(n_seq,),
        in_specs=[_row(seq, B_DIM), _row(seq, Q_DIM), _full(bk.shape), smem, smem] + [ANY] * len(order),
        out_specs=[_row(seq, B_DIM), _full((1, N_HEADS)), _full((N_BUCKETS, N_HEADS))],
        out_shape=[_sds((n_seq * seq, B_DIM), BF16), _sds((1, N_HEADS), F32), _sds((N_BUCKETS, N_HEADS), F32)],
        scratch_shapes=[pltpu.VMEM((HEAD_ROWS, 2 * CHUNK), F32), pltpu.VMEM((HEAD_ROWS, LANES), F32),
                        pltpu.VMEM((8, seq, KV_DIM), BF16), pltpu.VMEM((HEAD_ROWS, 2 * CHUNK), F32),
                        pltpu.VMEM((KV_DIM, seq + CHUNK), F32), pltpu.VMEM((KV_DIM, seq + CHUNK), F32),
                        pltpu.VMEM((HEAD_ROWS, LANES), F32)],
        compiler_params=_cp(("arbitrary",), 40),
    )(*_hbm(proj_b, d_yb, bk), rel_bias, sinks, *order)


def _inproj_bwd(d_g, d_a, d_b, x2, dx1, g_mix, w_in, tm, after=None):
    T = x2.shape[0]
    sub = min(tm, 128)
    order = [] if after is None else [after]

    def body(*refs):
        dg_ref, da_ref, db_ref, x_ref, dx1_ref, g_ref, w_ref = refs[:7]
        gx_ref, gg_ref = refs[7 + len(order):]
        subs = [slice(s0, s0 + sub) for s0 in range(0, tm, sub)]
        dhs = [_dot(dg_ref[rs, :], w_ref[_G_COLS, :]) + _dot(da_ref[rs, :], w_ref[_A_COLS, :])
               + _dot(db_ref[rs, :], w_ref[_B_COLS, :]) for rs in subs]
        gg = jnp.zeros((1, D_MODEL), F32)
        for rs, dh in zip(subs, dhs):
            x = x_ref[rs, :]
            r = _rms_r(x)
            n = x * r
            gx_ref[rs, :] = dx1_ref[rs, :] + _rms_bwd(dh, n, r, g_ref[...])
            gg = gg + jnp.sum(dh * n, axis=0, keepdims=True)

        @pl.when(pl.program_id(0) == 0)
        def _():
            gg_ref[...] = jnp.zeros_like(gg_ref)

        gg_ref[...] += gg

    return pl.pallas_call(
        body, name="inproj_bwd", grid=(T // tm,),
        in_specs=[_row(tm, G_DIM), _row(tm, A_DIM), _row(tm, B_DIM), _row(tm, D_MODEL), _row(tm, D_MODEL),
                  _full(g_mix.shape), _resident(w_in.shape)] + [ANY] * len(order),
        out_specs=[_row(tm, D_MODEL), _full((1, D_MODEL))],
        out_shape=[_sds((T, D_MODEL), F32), _sds((1, D_MODEL), F32)],
        compiler_params=_cp(("arbitrary",), 48),
    )(*_hbm(d_g, d_a, d_b, x2, dx1, g_mix, w_in), *order)


IN_SHARD = (A_DIM + B_DIM + G_DIM) // N_CHIPS


def _grad_w_in(h, d_a, d_b, d_g, tk):
    T = h.shape[0]
    nk = T // tk
    in_dim = N_CHIPS * IN_SHARD

    def body(h_ref, da_ref, db_ref, dg_ref, o_ref, acc_ref):
        k = pl.program_id(0)

        @pl.when(k == 0)
        def _():
            acc_ref[...] = jnp.zeros_like(acc_ref)

        hb = h_ref[...]
        acc_ref[:, _A_COLS] += _dot_tn(hb, da_ref[...])
        acc_ref[:, _B_COLS] += _dot_tn(hb, db_ref[...])
        acc_ref[:, _G_COLS] += _dot_tn(hb, dg_ref[...])

        @pl.when(k == nk - 1)
        def _():
            for i in range(N_CHIPS):
                o_ref[i] = acc_ref[:, i * IN_SHARD:(i + 1) * IN_SHARD].astype(BF16)

    return pl.pallas_call(
        body, name="grad_w_in", grid=(nk,),
        in_specs=[_row(tk, D_MODEL), _row(tk, A_DIM), _row(tk, B_DIM), _row(tk, G_DIM)],
        out_specs=_full((N_CHIPS, D_MODEL, IN_SHARD)), out_shape=_sds((N_CHIPS, D_MODEL, IN_SHARD), BF16),
        scratch_shapes=[pltpu.VMEM((D_MODEL, in_dim), F32)],
        compiler_params=_cp(("arbitrary",), 56),
    )(*_hbm(h, d_a, d_b, d_g))


def _local_step(x, target, g_mix, g_sgu, w_s, b_s, sinks, rel_bias, g_ffn, b_conv, g_final,
                w_in, w_conv, proj_weights, ffn_weights, on_grads, after=None):
    n_seq, seq, _ = x.shape
    T = n_seq * seq
    tm = min(ROW_TILE, seq)
    tw = min(GRAD_ROW_TILE, T)
    tf = min(WIDE_ROW_TILE, seq)
    x2 = x.reshape(T, D_MODEL)
    tgt = target.reshape(T, D_MODEL)
    b_st = b_s.T
    g_fin = g_final.reshape(1, D_MODEL)

    proj_g, proj_a, proj_b, h, y_a = _inproj(x2, g_mix, w_in, g_sgu, w_s, b_st, tm, after)
    y_b = _attn_fwd(proj_b, sinks, rel_bias, n_seq, seq)
    w_pa, w_pb, w_out = proj_weights(y_b)
    x1, merged = _merge_fwd(x2, y_a, y_b, proj_g, w_pa, w_pb, w_out, tm)
    w_up, w_down = ffn_weights(x1)
    upre, h2, gate, val = _upproj(x1, g_ffn, w_up, w_conv, b_conv, tf, seq)
    dx2, loss, gg_final = _ffn_down_loss(gate, val, x1, tgt, w_down, g_fin, tm)

    d_gate, d_val, gw_down, gb_g, gb_v = _ffn_bwd_act(gate, val, dx2, w_down, tw)
    gb_conv = jnp.concatenate([gb_g, gb_v], axis=1)
    d_upre, dx1, gg_ffn, gw_conv = _ffn_bwd_up(d_gate, d_val, upre, dx2, x1, g_ffn, w_conv, w_up, tf, seq)
    gw_up = _matmul_tn(h2, d_upre, 2 * D_FF // 4, min(4 * GRAD_ROW_TILE, T), "grad_w_up")
    sent = on_grads("ffn", dict(w_up=gw_up, w_down=gw_down))
    d_g, d_a, d_yb, gw_out, gw_pa, gw_pb, gw_s, gb_st, gg_sgu = _merge_bwd(
        dx1, merged, y_a, y_b, proj_g, proj_a, w_pa, w_pb, w_out, g_sgu, w_s, b_st, tw, sent)
    sent = on_grads("proj", dict(w_pa=gw_pa, w_pb=gw_pb, w_out=gw_out))
    d_b, g_sinks, g_rel = _attn_bwd(proj_b, d_yb, sinks, rel_bias, n_seq, seq, sent)
    gw_in = _grad_w_in(h, d_a, d_b, d_g, min(2 * GRAD_ROW_TILE, T))
    sent = on_grads("in", dict(w_in=gw_in))
    grad_x, gg_mix = _inproj_bwd(d_g, d_a, d_b, x2, dx1, g_mix, w_in, tm, sent)

    small = dict(g_mix=gg_mix, g_sgu=gg_sgu, w_s=gw_s, b_s=gb_st.T, sinks=g_sinks, rel_bias=g_rel,
                 g_ffn=gg_ffn, b_conv=gb_conv, g_final=gg_final, w_conv=gw_conv)
    big = dict(w_in=gw_in, w_pa=gw_pa, w_pb=gw_pb, w_out=gw_out, w_up=gw_up, w_down=gw_down)
    return loss, grad_x.reshape(x.shape), small, big


_MIXER = ("w_in", "w_pa", "w_pb", "w_out")
_FFN = ("w_up", "w_down")
_BIG = _MIXER + _FFN

CONV_ROWS = 6
_SMALL_AT = dict(loss=(0, 1, 1), g_sgu=(4, 1, A_WIDTH), sinks=(5, 1, N_HEADS), b_s=(8, A_GROUPS, CHUNK),
                 b_conv=(12, CONV_ROWS, D_MODEL), w_conv=(18, 3 * CONV_ROWS, D_MODEL),
                 g_final=(36, 1, D_MODEL), g_mix=(37, 1, D_MODEL), g_ffn=(38, 1, D_MODEL),
                 w_s=(40, A_GROUPS * CHUNK * CHUNK // D_MODEL, D_MODEL))
_REL_BIAS_AT = (0, A_WIDTH)
_SMALL_IN_CALL = ("g_final", "g_mix", "g_ffn", "g_sgu", "sinks", "b_s", "b_conv", "rel_bias", "w_s")
SMALL_ROWS = 104


def _pack_small(vals):
    def wide(a):
        return jnp.pad(a, ((0, 0), (0, CONV_ROWS * D_MODEL - a.shape[1]))).reshape(-1, D_MODEL)

    nr = _SMALL_AT["w_s"][1]
    w_s = vals["w_s"].reshape(D_MODEL // CHUNK, nr, CHUNK).transpose(1, 0, 2).reshape(nr, D_MODEL)
    laid = dict(vals, b_conv=wide(vals["b_conv"]), w_conv=wide(vals["w_conv"]), w_s=w_s)
    rows, at = [], 0
    for n, (r0, nr, nc) in _SMALL_AT.items():
        if r0 > at:
            rows.append(jnp.zeros((r0 - at, D_MODEL), F32))
        rows.append(jnp.pad(laid[n].astype(F32).reshape(nr, nc), ((0, 0), (0, D_MODEL - nc))))
        at = r0 + nr
    return lax.dynamic_update_slice(jnp.concatenate(rows, axis=0), vals["rel_bias"].T, _REL_BIAS_AT)


def _unwide(a, r):
    return a.reshape(r, CONV_ROWS * D_MODEL)[:, :2 * D_FF]


def _mesh_pos():
    return lax.axis_index("x"), lax.axis_index("y"), lax.axis_index("c")


def _other_chips(x, y):
    return [(1 - x, y), (x, 1 - y), (1 - x, 1 - y)]


def _remote(src, dst, send_sem, recv_sem, to):
    return pltpu.make_async_remote_copy(src_ref=src, dst_ref=dst, send_sem=send_sem, recv_sem=recv_sem,
                                        device_id=to, device_id_type=MESH)


def _own_slot(own, n, at):
    return lax.dynamic_update_slice(lax.empty((n,) + own.shape, own.dtype), own[None], (at,) + (0,) * own.ndim)


def _allgather_weights(stacks, wc_stack):
    names = list(stacks)
    n = len(names)

    def body(*refs):
        ins, outs = refs[:n + 1], refs[n + 1:2 * n + 2]
        send_sems, recv_sems = refs[2 * n + 2:]
        x, y, c = _mesh_pos()
        _handshake(_chip_peers(x, y, c) + _sibling_peers(x, y, c))
        me = 2 * x + y
        sibling = (x, y, 1 - c)
        chips = _other_chips(x, y)

        def half(ref, chip, hc):
            hr = ref.shape[1] // 2
            return ref.at[chip, pl.ds(hc * hr, hr), :]

        first = []
        for k in range(n):
            first += [_remote(half(ins[k], me, c), half(outs[k], me, c), send_sems.at[6 * k + j], recv_sems.at[6 * k + j], (cx, cy, c))
                      for j, (cx, cy) in enumerate(chips)]
        first += [_remote(ins[n].at[me], outs[n].at[me], send_sems.at[6 * n + j], recv_sems.at[6 * n + j], (cx, cy, c))
                  for j, (cx, cy) in enumerate(chips)]
        for cp in first:
            cp.start()
        passed = []
        for k in range(n):
            for j, (cx, cy) in enumerate(chips):
                landed = half(outs[k], 2 * cx + cy, c)
                _remote(landed, landed, send_sems.at[6 * k + j], recv_sems.at[6 * k + j], (x, y, c)).wait_recv()
                passed.append(_remote(landed, landed, send_sems.at[6 * k + 3 + j], recv_sems.at[6 * k + 3 + j], sibling))
                passed[-1].start()
        for k in range(n):
            for j, (cx, cy) in enumerate(chips):
                theirs = half(outs[k], 2 * cx + cy, 1 - c)
                _remote(theirs, theirs, send_sems.at[6 * k + 3 + j], recv_sems.at[6 * k + 3 + j], (x, y, c)).wait_recv()
        for j, (cx, cy) in enumerate(chips):
            slot = outs[n].at[2 * cx + cy]
            _remote(slot, slot, send_sems.at[6 * n + j], recv_sems.at[6 * n + j], (x, y, c)).wait_recv()
        for cp in first + passed:
            cp.wait_send()

    arrays = [stacks[k] for k in names] + [wc_stack]
    outs = pl.pallas_call(
        body, name="allgather_weights",
        in_specs=[HBM] * (n + 1), out_specs=[HBM] * (n + 1), input_output_aliases={k: k for k in range(n + 1)},
        out_shape=[_sds(a.shape, a.dtype) for a in arrays],
        scratch_shapes=[pltpu.SemaphoreType.DMA((6 * n + 3,)), pltpu.SemaphoreType.DMA((6 * n + 3,))],
        compiler_params=pltpu.CompilerParams(collective_id=_COLLECTIVE["gather_in"]),
    )(*arrays)
    return dict(zip(names, outs[:n])), outs[n]


_KIND = {"w_in": "stack", "w_pa": "col", "w_pb": "col", "w_up": "col", "w_out": "row", "w_down": "row"}


def _half_view(ref, kind, h):
    if kind == "stack":
        k = ref.shape[1] // 2
        return ref.at[:, pl.ds(h * k, k), :]
    if kind == "col":
        k = ref.shape[0] // 2
        return ref.at[pl.ds(h * k, k), :]
    k = ref.shape[1] // 2
    return ref.at[:, pl.ds(h * k, k)]


def _shard_view(ref, kind, i):
    if kind == "stack":
        return ref.at[i]
    if kind == "col":
        k = ref.shape[1] // N_CHIPS
        return ref.at[:, pl.ds(i * k, k)]
    k = ref.shape[0] // N_CHIPS
    return ref.at[pl.ds(i * k, k), :]


def _region_view(ref, kind, h):
    if kind == "row":
        k = ref.shape[1] // 2
        return ref.at[:, pl.ds(h * k, k)]
    k = ref.shape[0] // 2
    return ref.at[pl.ds(h * k, k), :]


def _half_shape(shape, kind):
    if kind == "stack":
        return (shape[0], shape[1] // 2, shape[2])
    return (shape[0] // 2, shape[1]) if kind == "col" else (shape[0], shape[1] // 2)


def _part_shape(half_shape, kind):
    if kind == "stack":
        return tuple(half_shape[1:])
    k, w = half_shape
    return (k, w // N_CHIPS) if kind == "col" else (k // N_CHIPS, w)


_DATAFLOW = pltpu.SideEffectType.DATAFLOW_SIDE_EFFECTING
_TOKEN = (SUBLANES, LANES)


_COLLECTIVE = {k: i for i, k in enumerate(
    [kind + "_" + g for kind in ("pair", "chip", "share") for g in ("ffn", "proj", "in")]
    + ["gather_proj", "gather_ffn", "gather_in", "forward_proj", "forward_ffn"])}


def _sibling_peers(x, y, c):
    return [(x, y, 1 - c)]


def _chip_peers(x, y, c):
    return [(cx, cy, c) for cx, cy in _other_chips(x, y)]


def _handshake(peers):
    barrier = pltpu.get_barrier_semaphore()
    for peer in peers:
        pl.semaphore_signal(barrier, inc=1, device_id=peer, device_id_type=MESH)
    pl.semaphore_wait(barrier, len(peers))


def _split_start(name, arrays, n_sems, issue, after=None, handshake=None):
    n = len(arrays)
    order = [] if after is None else [after]

    def body(*refs):
        base = n + len(order)
        if handshake is not None:
            _handshake(handshake[1](*_mesh_pos()))
        issue(refs[:n], refs[base], refs[base + 1])
        refs[-1][...] = jnp.zeros(_TOKEN, F32)

    params = dict(has_side_effects=_DATAFLOW)
    if handshake is not None:
        params["collective_id"] = handshake[0]
    outs = pl.pallas_call(
        body, name=name,
        in_specs=[HBM] * n + [ANY] * len(order), out_specs=[SEM, SEM] + [HBM] * n + [pl.BlockSpec(memory_space=pltpu.VMEM)],
        out_shape=[pltpu.SemaphoreType.DMA((n_sems,)), pltpu.SemaphoreType.DMA((n_sems,))]
        + [pltpu.HBM(a.shape, a.dtype) for a in arrays] + [_sds(_TOKEN, F32)],
        input_output_aliases={k: 2 + k for k in range(n)},
        compiler_params=pltpu.CompilerParams(**params),
    )(*[pltpu.with_memory_space_constraint(a, pltpu.HBM) for a in arrays], *order)
    return outs[0], outs[1], list(outs[2:2 + n]), outs[-1]


def _split_wait(name, started, waits, after):
    send_sems, recv_sems, arrays, _ = started
    n = len(arrays)

    def body(*refs):
        waits(refs[:n], refs[n], refs[n + 1])

    return pl.pallas_call(
        body, name=name,
        in_specs=[HBM] * n + [SEM, SEM, ANY], out_specs=[HBM] * n,
        out_shape=[pltpu.HBM(a.shape, a.dtype) for a in arrays],
        input_output_aliases={k: k for k in range(n)},
        compiler_params=pltpu.CompilerParams(has_side_effects=_DATAFLOW),
    )(*arrays, send_sems, recv_sems, after)


def _wait_both(src, dst, send_sem, recv_sem):
    x, y, c = _mesh_pos()
    cp = _remote(src, dst, send_sem, recv_sem, (x, y, c))
    cp.wait_send()
    cp.wait_recv()


def _pair_exchange_start(parts, tag, after):
    names = list(parts)
    n = len(names)
    lands = [lax.empty(_half_shape(parts[k].shape, _KIND[k]), parts[k].dtype) for k in names]

    def issue(refs, send_sems, recv_sems):
        x, y, c = _mesh_pos()
        for hc in range(2):
            @pl.when(c == hc)
            def _():
                for k in range(n):
                    _remote(_half_view(refs[k], _KIND[names[k]], 1 - hc), refs[n + k], send_sems.at[k], recv_sems.at[k],
                            (x, y, 1 - c)).start()

    return names, _split_start("grad_pair_exchange_start_" + tag, [parts[k] for k in names] + lands, n, issue, after,
                               (_COLLECTIVE["pair_" + tag], _sibling_peers))


def _pair_exchange_wait(pending, tag, after):
    names, started = pending
    n = len(names)

    def waits(refs, send_sems, recv_sems):
        for k in range(n):
            _wait_both(_half_view(refs[k], _KIND[names[k]], 0), refs[n + k], send_sems.at[k], recv_sems.at[k])

    outs = _split_wait("grad_pair_exchange_wait_" + tag, started, waits, after)
    return dict(zip(names, outs[:n])), dict(zip(names, outs[n:]))


def _pair_add(part, from_sibling, name, pos):
    assert _KIND[name] == "stack"
    _, k, w = part.shape
    block = (2, k // 2, w)

    def body(s_ref, p_ref, q_ref, o_ref):
        o_ref[...] = (p_ref[...].astype(F32) + q_ref[...].astype(F32)).astype(BF16)

    return pl.pallas_call(
        body, name="grad_pair_add_" + name,
        grid_spec=pltpu.PrefetchScalarGridSpec(
            num_scalar_prefetch=1, grid=(N_CHIPS // 2,),
            in_specs=[pl.BlockSpec(block, lambda i, s: (i, s[1], 0)), pl.BlockSpec(block, lambda i, s: (i, 0, 0))],
            out_specs=pl.BlockSpec(block, lambda i, s: (i, 0, 0))),
        out_shape=_sds(from_sibling.shape, BF16),
        compiler_params=_cp(("arbitrary",), 40),
    )(pos, *_hbm(part, from_sibling))


def _pair_add_group(parts, from_sibling, tag, pos):
    names = list(parts)
    n = len(names)
    full_specs, half_specs = [], []
    for name in names:
        k, w = parts[name].shape
        if _KIND[name] == "col":
            block, full_map = (k // 4, w), (lambda r, s: (2 * s[1] + r, 0))
        else:
            block, full_map = (k // 2, w // 2), (lambda r, s: (r, s[1]))
        full_specs.append(pl.BlockSpec(block, full_map))
        half_specs.append(pl.BlockSpec(block, lambda r, s: (r, 0)))

    def body(s_ref, *refs):
        for k in range(n):
            refs[2 * n + k][...] = (refs[k][...].astype(F32) + refs[n + k][...].astype(F32)).astype(BF16)

    outs = pl.pallas_call(
        body, name="grad_pair_add_" + tag,
        grid_spec=pltpu.PrefetchScalarGridSpec(
            num_scalar_prefetch=1, grid=(2,), in_specs=full_specs + half_specs, out_specs=half_specs),
        out_shape=[_sds(from_sibling[k].shape, BF16) for k in names],
        compiler_params=_cp(("arbitrary",), 40),
    )(pos, *_hbm(*[parts[k] for k in names], *[from_sibling[k] for k in names]))
    return dict(zip(names, outs))


def _owner_sum_group(parts, from_sibling, from_chips, tag, pos, shard_shapes):
    names = list(parts)
    n = len(names)
    p_specs, q_specs, r_specs, o_specs = [], [], [], []
    for name in names:
        _, pk, pw = from_chips[name].shape
        block = (pk // 2, pw)
        if _KIND[name] == "row":
            maps = (lambda r, s: (2 * s[0] + r, s[1])), (lambda r, s: (2 * s[0] + r, 0)), (lambda r, s: (r, s[1]))
        else:
            maps = (lambda r, s: (2 * s[1] + r, s[0])), (lambda r, s: (r, s[0])), (lambda r, s: (2 * s[1] + r, 0))
        p_specs.append(pl.BlockSpec(block, maps[0]))
        q_specs.append(pl.BlockSpec(block, maps[1]))
        o_specs.append(pl.BlockSpec(block, maps[2]))
        r_specs.append(pl.BlockSpec((3,) + block, lambda r, s: (0, r, 0)))

    def body(s_ref, *refs):
        for k in range(n):
            acc = refs[k][...].astype(F32) + refs[n + k][...].astype(F32)
            for j in range(3):
                acc = acc + refs[2 * n + k][j].astype(F32)
            refs[3 * n + k][...] = acc

    outs = pl.pallas_call(
        body, name="grad_owner_sum_" + tag,
        grid_spec=pltpu.PrefetchScalarGridSpec(
            num_scalar_prefetch=1, grid=(2,), in_specs=p_specs + q_specs + r_specs, out_specs=o_specs),
        out_shape=[_sds(shard_shapes[k], F32) for k in names],
        compiler_params=_cp(("arbitrary",), 32),
    )(pos, *_hbm(*[parts[k] for k in names], *[from_sibling[k] for k in names], *[from_chips[k] for k in names]))
    return dict(zip(names, outs))


def _chip_exchange_start(sums, tag, after):
    names = list(sums)
    n = len(names)
    lands = [lax.empty((3,) + _part_shape(sums[k].shape, _KIND[k]), sums[k].dtype) for k in names]

    def issue(refs, send_sems, recv_sems):
        x, y, c = _mesh_pos()
        me = 2 * x + y
        for i in range(N_CHIPS):
            xi, yi = i // 2, i % 2
            j = jnp.where(xi != x, jnp.where(yi != y, 2, 0), 1)

            @pl.when(i != me)
            def _():
                for k in range(n):
                    _remote(_shard_view(refs[k], _KIND[names[k]], i), refs[n + k].at[j], send_sems.at[3 * k + j],
                            recv_sems.at[3 * k + j], (xi, yi, c)).start()

    return names, _split_start("grad_chip_exchange_start_" + tag, [sums[k] for k in names] + lands, 3 * n, issue, after,
                               (_COLLECTIVE["chip_" + tag], _chip_peers))


def _chip_exchange_wait(pending, tag, after):
    names, started = pending
    n = len(names)

    def waits(refs, send_sems, recv_sems):
        for k in range(n):
            for j in range(3):
                _wait_both(_shard_view(refs[k], _KIND[names[k]], 0), refs[n + k].at[j], send_sems.at[3 * k + j], recv_sems.at[3 * k + j])

    return dict(zip(names, _split_wait("grad_chip_exchange_wait_" + tag, started, waits, after)[n:]))


def _allgather_start(stacks, tag, after):
    names = list(stacks)

    def issue(refs, send_sems, recv_sems):
        x, y, c = _mesh_pos()
        me = 2 * x + y
        for k, st in enumerate(refs):
            hr = st.shape[1] // 2
            mine = st.at[me, pl.ds(c * hr, hr), :]
            for j, (cx, cy) in enumerate(_other_chips(x, y)):
                _remote(mine, mine, send_sems.at[3 * k + j], recv_sems.at[3 * k + j], (cx, cy, c)).start()

    return names, _split_start("allgather_start_" + tag, [stacks[k] for k in names], 3 * len(names), issue, after,
                               (_COLLECTIVE["gather_" + tag], _chip_peers))


def _allgather_wait(pending, tag, after):
    names, started = pending

    def waits(refs, send_sems, recv_sems):
        for k, st in enumerate(refs):
            slot = st.at[0, pl.ds(0, st.shape[1] // 2), :]
            for j in range(3):
                _wait_both(slot, slot, send_sems.at[3 * k + j], recv_sems.at[3 * k + j])

    return dict(zip(names, _split_wait("allgather_wait_" + tag, started, waits, after)))


def _allgather_forward(stacks, tag):
    names = list(stacks)
    n = len(names)

    def body(*refs):
        ins, outs = refs[:n], refs[n:2 * n]
        send_sems, recv_sems = refs[2 * n:]
        x, y, c = _mesh_pos()
        _handshake(_sibling_peers(x, y, c))
        copies = []
        for k in range(n):
            hr = ins[k].shape[1] // 2
            for j, (cx, cy) in enumerate(_other_chips(x, y)):
                chip = 2 * cx + cy
                copies.append(_remote(ins[k].at[chip, pl.ds(c * hr, hr), :], outs[k].at[chip, pl.ds(c * hr, hr), :],
                                      send_sems.at[3 * k + j], recv_sems.at[3 * k + j], (x, y, 1 - c)))
        for cp in copies:
            cp.start()
        for cp in copies:
            cp.wait()

    arrays = [stacks[k] for k in names]
    outs = pl.pallas_call(
        body, name="allgather_forward_" + tag, in_specs=[HBM] * n, out_specs=[HBM] * n,
        input_output_aliases={k: k for k in range(n)},
        out_shape=[_sds(a.shape, a.dtype) for a in arrays],
        scratch_shapes=[pltpu.SemaphoreType.DMA((3 * n,)), pltpu.SemaphoreType.DMA((3 * n,))],
        compiler_params=pltpu.CompilerParams(collective_id=_COLLECTIVE["forward_" + tag]),
    )(*arrays)
    return dict(zip(names, outs))


def _owner_sum(part, from_sibling, from_chips, name, pos, shard_shape):
    assert _KIND[name] == "stack"
    _, pk, pw = from_chips.shape
    tr = STREAM_ROWS
    nb = pk // tr
    p_spec = pl.BlockSpec((None, tr, pw), lambda r, s: (s[0], s[1] * nb + r, 0))
    q_spec = pl.BlockSpec((None, tr, pw), lambda r, s: (s[0], r, 0))
    o_spec = pl.BlockSpec((tr, pw), lambda r, s: (s[1] * nb + r, 0))

    def body(s_ref, p_ref, q_ref, r_ref, o_ref):
        acc = p_ref[...].astype(F32) + q_ref[...].astype(F32)
        for j in range(3):
            acc = acc + r_ref[j].astype(F32)
        o_ref[...] = acc

    return pl.pallas_call(
        body, name="grad_owner_sum_" + name,
        grid_spec=pltpu.PrefetchScalarGridSpec(
            num_scalar_prefetch=1, grid=(nb,),
            in_specs=[p_spec, q_spec, pl.BlockSpec((3, tr, pw), lambda r, s: (0, r, 0))],
            out_specs=o_spec),
        out_shape=_sds(shard_shape, F32),
        compiler_params=_cp(("arbitrary",), 32),
    )(pos, *_hbm(part, from_sibling, from_chips))


def _pair_share_start(shards, tag, after):
    names = list(shards)

    def issue(refs, send_sems, recv_sems):
        x, y, c = _mesh_pos()
        for hc in range(2):
            @pl.when(c == hc)
            def _():
                for k, g in enumerate(refs):
                    mine = _region_view(g, _KIND[names[k]], hc)
                    _remote(mine, mine, send_sems.at[k], recv_sems.at[k], (x, y, 1 - c)).start()

    return names, _split_start("grad_pair_share_start_" + tag, [shards[k] for k in names], len(names), issue, after,
                               (_COLLECTIVE["share_" + tag], _sibling_peers))


def _pair_share_wait(pending, tag, after):
    names, started = pending

    def waits(refs, send_sems, recv_sems):
        for k, g in enumerate(refs):
            region = _region_view(g, _KIND[names[k]], 0)
            _wait_both(region, region, send_sems.at[k], recv_sems.at[k])

    return dict(zip(names, _split_wait("grad_pair_share_wait_" + tag, started, waits, after)))


def _small_exchange_start(slots, after):
    def issue(refs, send_sems, recv_sems):
        x, y, c = _mesh_pos()
        mine = refs[0].at[4 * x + 2 * y + c]
        k = 0
        for px in range(2):
            for py in range(2):
                for pc in range(2):
                    if px + py + pc:
                        peer = (1 - x if px else x, 1 - y if py else y, 1 - c if pc else c)
                        _remote(mine, mine, send_sems.at[k], recv_sems.at[k], peer).start()
                        k += 1

    return _split_start("small_exchange_start", [slots], N_DEV - 1, issue, after)


def _small_exchange_wait(started, after):
    def waits(refs, send_sems, recv_sems):
        slot = refs[0].at[0]
        for k in range(N_DEV - 1):
            _wait_both(slot, slot, send_sems.at[k], recv_sems.at[k])

    return _split_wait("small_exchange_wait", started, waits, after)[0]


def _adam_math(w, g, m, v):
    m = ADAM_B1 * m + (1.0 - ADAM_B1) * g
    v = ADAM_B2 * v + (1.0 - ADAM_B2) * (g * g)
    m_hat = m / (1.0 - ADAM_B1 ** ADAM_STEP)
    v_hat = v / (1.0 - ADAM_B2 ** ADAM_STEP)
    delta = -ADAM_LR * (m_hat / (jnp.sqrt(v_hat) + ADAM_EPS) + ADAM_WD * w)
    return delta, m, v


def _adamw(w, g, m, v, name):
    rows, cols = w.shape[0], w.shape[-1]
    fits = [t for t in range(SUBLANES, rows, SUBLANES) if rows % t == 0 and t * cols * 4 <= (3 << 19)]
    tr = max(fits) if fits and w.ndim == 2 else rows

    def body(w_ref, g_ref, m_ref, v_ref, d_ref, nm_ref, nv_ref, go_ref):
        g = g_ref[...]
        d, nm, nv = _adam_math(w_ref[...], g, m_ref[...], v_ref[...])
        d_ref[...] = d
        nm_ref[...] = nm
        nv_ref[...] = nv
        go_ref[...] = g

    spec = pl.BlockSpec((tr,) + w.shape[1:], lambda i: (i,) + (0,) * (w.ndim - 1))
    return pl.pallas_call(
        body, name=name, grid=(rows // tr,), in_specs=[spec] * 4, out_specs=[spec] * 4,
        out_shape=[_sds(w.shape, F32)] * 4, compiler_params=_cp(("arbitrary",)),
    )(*_hbm(w, g, m, v))


def _adamw_group(w, g, m, v, tag):
    names = list(w)
    n = len(names)
    steps = 4
    specs = [pl.BlockSpec((w[k].shape[0] // steps, w[k].shape[1]), lambda i: (i, 0)) for k in names]

    def body(*refs):
        for k in range(n):
            w_ref, g_ref, m_ref, v_ref = [refs[j * n + k] for j in range(4)]
            d_ref, nm_ref, nv_ref, go_ref = refs[4 * n + 4 * k:4 * n + 4 * k + 4]
            grad = g_ref[...]
            d, nm, nv = _adam_math(w_ref[...], grad, m_ref[...], v_ref[...])
            d_ref[...] = d
            nm_ref[...] = nm
            nv_ref[...] = nv
            go_ref[...] = grad

    res = pl.pallas_call(
        body, name="adamw_" + tag, grid=(steps,), in_specs=specs * 4, out_specs=[s for s in specs for _ in range(4)],
        out_shape=[_sds(w[k].shape, F32) for k in names for _ in range(4)], compiler_params=_cp(("arbitrary",), 48),
    )(*_hbm(*[a[k] for a in (w, g, m, v) for k in names]))
    return {k: tuple(res[4 * i:4 * i + 4]) for i, k in enumerate(names)}


def _small_sum_adamw(gathered, w, m, v):
    names = _SMALL_IN_CALL
    n = len(names)

    def body(*refs):
        a_ref = refs[0]
        w_refs, m_refs, v_refs = refs[1:1 + n], refs[1 + n:1 + 2 * n], refs[1 + 2 * n:1 + 3 * n]
        sum_ref, loss_ref = refs[1 + 3 * n], refs[2 + 3 * n]
        outs = refs[3 + 3 * n:]
        g = a_ref[0]
        for k in range(1, N_DEV):
            g = g + a_ref[k]
        sum_ref[...] = g
        loss_ref[...] = g[0:1, 0:1]
        for i, name in enumerate(names):
            if name == "rel_bias":
                r0, c0 = _REL_BIAS_AT
                pieces = [(slice(None), g[r0:r0 + N_HEADS, c0:c0 + N_BUCKETS])]
            elif name == "b_conv":
                r0 = _SMALL_AT[name][0]
                pieces = [(slice(None), jnp.concatenate([g[r0 + k:r0 + k + 1, :] for k in range(CONV_ROWS)], axis=1)[:, :2 * D_FF])]
            elif name == "w_s":
                r0, nr, _ = _SMALL_AT[name]
                pieces = [(slice(nr * j, nr * (j + 1)), g[r0:r0 + nr, CHUNK * j:CHUNK * (j + 1)]) for j in range(D_MODEL // CHUNK)]
            else:
                r0, nr, nc = _SMALL_AT[name]
                pieces = [(slice(None), g[r0:r0 + nr, 0:nc])]
            for at, gp in pieces:
                d, nm, nv = _adam_math(w_refs[i][at], gp, m_refs[i][at], v_refs[i][at])
                for k, val in enumerate((gp, d, nm, nv)):
                    outs[4 * i + k][at] = val

    shapes = [w[k].shape for k in names]
    res = pl.pallas_call(
        body, name="small_sum_adamw",
        out_shape=[_sds((SMALL_ROWS, D_MODEL), F32), _sds((1, 1), F32)] + [_sds(s, F32) for s in shapes for _ in range(4)],
    )(gathered, *[w[k] for k in names], *[m[k] for k in names], *[v[k] for k in names])
    return res[0], res[1], {k: tuple(res[2 + 4 * i:6 + 4 * i]) for i, k in enumerate(names)}


_NAMES = ("g_mix", "w_in", "g_sgu", "w_s", "b_s", "sinks", "rel_bias", "w_pa", "w_pb", "w_out",
          "g_ffn", "w_up", "w_conv", "b_conv", "w_down", "g_final")

def kernel(x, g_mix, w_in, g_sgu, w_s, b_s, sinks, rel_bias, w_pa, w_pb, w_out, g_ffn, w_up, w_conv, b_conv, w_down, g_final, loss_target, m_g_mix, m_w_in, m_g_sgu, m_w_s, m_b_s, m_sinks, m_rel_bias, m_w_pa, m_w_pb, m_w_out, m_g_ffn, m_w_up, m_w_conv, m_b_conv, m_w_down, m_g_final, v_g_mix, v_w_in, v_g_sgu, v_w_s, v_b_s, v_sinks, v_rel_bias, v_w_pa, v_w_pb, v_w_out, v_g_ffn, v_w_up, v_w_conv, v_b_conv, v_w_down, v_g_final):
    w = dict(g_mix=g_mix, w_in=w_in, g_sgu=g_sgu, w_s=w_s, b_s=b_s, sinks=sinks, rel_bias=rel_bias, w_pa=w_pa, w_pb=w_pb,
             w_out=w_out, g_ffn=g_ffn, w_up=w_up, w_conv=w_conv, b_conv=b_conv, w_down=w_down, g_final=g_final)
    m = dict(g_mix=m_g_mix, w_in=m_w_in, g_sgu=m_g_sgu, w_s=m_w_s, b_s=m_b_s, sinks=m_sinks, rel_bias=m_rel_bias, w_pa=m_w_pa,
             w_pb=m_w_pb, w_out=m_w_out, g_ffn=m_g_ffn, w_up=m_w_up, w_conv=m_w_conv, b_conv=m_b_conv, w_down=m_w_down,
             g_final=m_g_final)
    v = dict(g_mix=v_g_mix, w_in=v_w_in, g_sgu=v_g_sgu, w_s=v_w_s, b_s=v_b_s, sinks=v_sinks, rel_bias=v_rel_bias, w_pa=v_w_pa,
             w_pb=v_w_pb, w_out=v_w_out, g_ffn=v_g_ffn, w_up=v_w_up, w_conv=v_w_conv, b_conv=v_b_conv, w_down=v_w_down,
             g_final=v_g_final)
    xi, yi, ci = _mesh_pos()
    me = 2 * xi + yi

    shard = {n: w[n][0] for n in _BIG}
    shard_shapes = {n: shard[n].shape for n in _BIG}
    wc_shard = w["w_conv"][0]
    wc_pad = jnp.pad(wc_shard, ((0, 5), (0, 0)))
    own = {n: _own_slot(shard[n].astype(BF16), N_CHIPS, me) for n in _BIG if n != "w_in"}
    own["w_in"] = _own_slot(shard["w_in"].T.astype(BF16), N_CHIPS, me)
    stacks, wc_all = _allgather_weights({"w_in": own["w_in"]}, _own_slot(wc_pad, N_CHIPS, me))
    proj_gather = _allgather_start({n: own[n] for n in _MIXER[1:]}, "proj", stacks["w_in"])
    ffn_gather = _allgather_start({n: own[n] for n in _FFN}, "ffn", proj_gather[1][-1])
    w_conv_full = jnp.concatenate([wc_all[i, :3] for i in range(N_CHIPS)], axis=1)
    w_in_full = stacks["w_in"].reshape(N_CHIPS * IN_SHARD, D_MODEL)
    pos = jnp.stack([me, ci])

    def proj_weights(done):
        st = _allgather_forward(_allgather_wait(proj_gather, "proj", done), "proj")
        return st["w_pa"], st["w_pb"], st["w_out"].reshape(D_MODEL, D_MODEL)

    def ffn_weights(done):
        st = _allgather_forward(_allgather_wait(ffn_gather, "ffn", done), "ffn")
        return st["w_up"], st["w_down"].reshape(D_FF, D_MODEL)

    groups = {}

    def stage1(group, parts):
        groups[group] = dict(parts=parts, pair=_pair_exchange_start(parts, group, None))
        return groups[group]["pair"][1][-1]

    def stage2(group, after, order_after):
        g = groups[group]
        g["parts"], g["sib"] = _pair_exchange_wait(g["pair"], group, after)
        if group == "in":
            sums = {n: _pair_add(g["parts"][n], g["sib"][n], n, pos) for n in g["parts"]}
        else:
            sums = _pair_add_group(g["parts"], g["sib"], group, pos)
        g["chip"] = _chip_exchange_start(sums, group, order_after)
        return g["chip"][1][-1]

    def stage3(group, after, order_after):
        g = groups[group]
        got = _chip_exchange_wait(g["chip"], group, after)
        if group == "in":
            owned = {n: _owner_sum(g["parts"][n], g["sib"][n], got[n], n, pos, shard_shapes[n]) for n in g["parts"]}
        else:
            owned = _owner_sum_group(g["parts"], g["sib"], got, group, pos, shard_shapes)
        g["share"] = _pair_share_start(owned, group, order_after)
        return g["share"][1][-1]

    grads, deltas, new_m, new_v = {}, {}, {}, {}

    def stage4(group, after):
        g_shard = _pair_share_wait(groups[group]["share"], group, after)
        if group != "in":
            res = _adamw_group({n: shard[n] for n in g_shard}, g_shard, {n: m[n][0] for n in g_shard},
                               {n: v[n][0] for n in g_shard}, group)
            for n, (d, nm, nv, go) in res.items():
                grads[n], deltas[n], new_m[n], new_v[n] = go[None], d[None], nm[None], nv[None]
            return nv
        last = None
        for n in g_shard:
            g = _tie(g_shard[n], last)
            if n == "w_in":
                d, nm, nv, gt = _adamw(shard[n].T, g.T, m[n][0].T, v[n][0].T, "adamw_" + n)
                grads[n], deltas[n], new_m[n], new_v[n] = gt.T[None], d.T[None], nm.T[None], nv.T[None]
            else:
                d, nm, nv, go = _adamw(shard[n], g, m[n][0], v[n][0], "adamw_" + n)
                grads[n], deltas[n], new_m[n], new_v[n] = go[None], d[None], nm[None], nv[None]
            last = nv
        return last

    def on_grads(group, parts):
        token = stage1(group, parts)
        some = next(iter(parts.values()))
        if group == "proj":
            token = stage2("ffn", some, token)
        if group == "in":
            token = stage2("proj", some, token)
            token = stage3("ffn", some, token)
            token = stage2("in", token, token)
        return token

    loss, grad_x, small, big = _local_step(
        x, loss_target, w["g_mix"], w["g_sgu"], w["w_s"][0], w["b_s"][0], w["sinks"], w["rel_bias"], w["g_ffn"],
        w["b_conv"], w["g_final"], w_in_full, w_conv_full, proj_weights, ffn_weights, on_grads, ffn_gather[1][-1])

    small["loss"] = loss
    small_gather = _small_exchange_start(_own_slot(_pack_small(small), N_DEV, 2 * me + ci), grad_x)
    token = stage3("proj", grad_x, small_gather[-1])
    done = stage4("ffn", token)
    done = stage4("proj", done)
    token = stage3("in", done, None)
    all_small = _small_exchange_wait(small_gather, token)
    two_d = {n: (lambda a, n=n: a.reshape(_SMALL_AT[n][1:])) for n in _SMALL_IN_CALL}
    two_d["rel_bias"] = lambda a: a.T
    two_d["b_conv"] = lambda a: a
    two_d["w_s"] = lambda a: a.reshape(A_GROUPS * CHUNK, CHUNK)
    s_sum, s_loss, s_out = _small_sum_adamw(all_small, *[{n: two_d[n](p[n]) for n in _SMALL_IN_CALL} for p in (w, m, v)])
    stage4("in", s_sum)
    for n in _SMALL_IN_CALL:
        back = (lambda a: a.T) if n == "rel_bias" else (lambda a, n=n: a.reshape(w[n].shape))
        grads[n], deltas[n], new_m[n], new_v[n] = [back(a) for a in s_out[n]]

    def rows(n):
        r0, nr, _ = _SMALL_AT[n]
        return s_sum[r0:r0 + nr]

    wcols = wc_shard.shape[1]
    g_wc = lax.dynamic_slice(_unwide(rows("w_conv"), 3), (0, me * wcols), (3, wcols))
    taps = lambda a: a.transpose(1, 0, 2)
    res = _adamw(taps(w["w_conv"]), g_wc[:, None, :], taps(m["w_conv"]), taps(v["w_conv"]), "adamw_w_conv")
    deltas["w_conv"], new_m["w_conv"], new_v["w_conv"], grads["w_conv"] = [taps(a) for a in res]

    return (s_loss.reshape(()), grad_x, *[grads[n] for n in _NAMES], *[deltas[n] for n in _NAMES],
            *[new_m[n] for n in _NAMES], *[new_v[n] for n in _NAMES])
```
